```python
import jax
import jax.numpy as jnp
from jax import lax
import numpy as np

D_MODEL = 2048
BATCH = 8
SEQ = 4096
DEPTH = 1

D_MIX = D_MODEL
V_HEAD_DIM = 128
MLA_WIDTH = D_MIX // 2
MLA_HEADS = MLA_WIDTH // V_HEAD_DIM
QK_NOPE_DIM = 128
QK_ROPE_DIM = 64
QK_HEAD_DIM = QK_NOPE_DIM + QK_ROPE_DIM
Q_LORA_RANK = 512
KV_LORA_RANK = 512
ROPE_THETA = 10000.0
Q_BLOCK = 128
SSD_WIDTH = D_MIX - MLA_WIDTH
SSD_HEAD_DIM = 64
SSD_HEADS = SSD_WIDTH // SSD_HEAD_DIM
SSD_GROUPS = 2
SSD_HEADS_PER_GROUP = SSD_HEADS // SSD_GROUPS
SSD_STATE = 128
SSD_CONV = 4
SSD_CHUNK = 128
SSD_CONV_DIM = SSD_WIDTH + 2 * SSD_GROUPS * SSD_STATE
D_FF = -(-8 * D_MODEL // (3 * 256)) * 256
IN_SIZES = (Q_LORA_RANK, KV_LORA_RANK, QK_ROPE_DIM, SSD_WIDTH, SSD_CONV_DIM, SSD_HEADS)
D_IN = Q_LORA_RANK + KV_LORA_RANK + QK_ROPE_DIM + SSD_WIDTH + SSD_CONV_DIM + SSD_HEADS
EPS = 1e-6

kernel_name = "hymba_mla_ssd_sandwich_layer"


def rms_norm(t, w):
    tf = t.astype(jnp.float32)
    y = tf * lax.rsqrt(jnp.mean(tf * tf, axis=-1, keepdims=True) + EPS)
    return (y * w.astype(jnp.float32)).astype(t.dtype)


def split_cols(t, sizes):
    offsets = np.cumsum(np.array(sizes))[:-1].tolist()
    return jnp.split(t, offsets, axis=-1)


def rope_tables(positions):
    inv_freq = ROPE_THETA ** (-jnp.arange(0, QK_ROPE_DIM, 2, dtype=jnp.float32) / QK_ROPE_DIM)
    ang = positions.astype(jnp.float32)[..., None] * inv_freq
    return jnp.cos(ang), jnp.sin(ang)


def apply_rope(t, cos, sin):
    t1, t2 = jnp.split(t.astype(jnp.float32), 2, axis=-1)
    return jnp.concatenate([t1 * cos - t2 * sin, t2 * cos + t1 * sin], axis=-1).astype(t.dtype)


def mla_group(c_q, c_kv, k_rope, cos, sin, q_norm_w, w_uq, kv_norm_w, w_ukv):
    b, s, _ = c_q.shape
    q = (rms_norm(c_q, q_norm_w) @ w_uq).reshape(b, s, MLA_HEADS, QK_HEAD_DIM)
    q_nope, q_rope = q[..., :QK_NOPE_DIM], q[..., QK_NOPE_DIM:]
    kv = (rms_norm(c_kv, kv_norm_w) @ w_ukv).reshape(b, s, MLA_HEADS, QK_NOPE_DIM + V_HEAD_DIM)
    k_nope, v = kv[..., :QK_NOPE_DIM], kv[..., QK_NOPE_DIM:]
    q_rope = apply_rope(q_rope, cos[:, :, None, :], sin[:, :, None, :])
    k_rope = apply_rope(k_rope, cos, sin)
    scale = QK_HEAD_DIM ** -0.5
    n_blk = s // Q_BLOCK
    qn_blocks = jnp.moveaxis(q_nope.reshape(b, n_blk, Q_BLOCK, MLA_HEADS, QK_NOPE_DIM), 1, 0)
    qr_blocks = jnp.moveaxis(q_rope.reshape(b, n_blk, Q_BLOCK, MLA_HEADS, QK_ROPE_DIM), 1, 0)
    key_idx = jnp.arange(s)

    def attend(args):
        blk, qn, qr = args
        sc = (jnp.einsum('bqhd,bkhd->bhqk', qn, k_nope, preferred_element_type=jnp.float32)
              + jnp.einsum('bqhr,bkr->bhqk', qr, k_rope, preferred_element_type=jnp.float32)) * scale
        q_idx = blk * Q_BLOCK + jnp.arange(Q_BLOCK)
        causal = key_idx[None, :] <= q_idx[:, None]
        p = jax.nn.softmax(jnp.where(causal, sc, -jnp.inf), axis=-1).astype(v.dtype)
        return jnp.einsum('bhqk,bkhd->bqhd', p, v)

    o = lax.map(attend, (jnp.arange(n_blk), qn_blocks, qr_blocks))
    return jnp.moveaxis(o, 0, 1).reshape(b, s, MLA_HEADS * V_HEAD_DIM)


def causal_depthwise_conv(t, w, bias):
    y = lax.conv_general_dilated(t, w[:, None, :], window_strides=(1,), padding=[(SSD_CONV - 1, 0)],
                                 dimension_numbers=('NWC', 'WIO', 'NWC'), feature_group_count=t.shape[-1])
    return y + bias


def ssd_group(z, xbc, dt_raw, conv_w, conv_b, dt_bias, a_log, d_skip, norm_w):
    b, s, _ = z.shape
    G, E, P, N, T = SSD_GROUPS, SSD_HEADS_PER_GROUP, SSD_HEAD_DIM, SSD_STATE, SSD_CHUNK
    c = s // T
    xbc = jax.nn.silu(causal_depthwise_conv(xbc, conv_w, conv_b))
    xs, bm, cm = split_cols(xbc, (SSD_WIDTH, G * N, G * N))
    dt = jax.nn.softplus(dt_raw.astype(jnp.float32) + dt_bias.astype(jnp.float32))
    a_neg = -jnp.exp(a_log.astype(jnp.float32)).reshape(G, E)
    x = xs.astype(jnp.float32).reshape(b, c, T, G, E, P)
    dt_c = dt.reshape(b, c, T, G, E)
    bc = bm.astype(jnp.float32).reshape(b, c, T, G, N)
    cc = cm.astype(jnp.float32).reshape(b, c, T, G, N)
    xdt = x * dt_c[..., None]
    a_cum = jnp.cumsum(jnp.transpose(dt_c * a_neg, (0, 1, 3, 4, 2)), axis=-1)
    seg = a_cum[..., :, None] - a_cum[..., None, :]
    tri = jnp.tril(jnp.ones((T, T), dtype=bool))
    decay = jnp.exp(jnp.where(tri, seg, -jnp.inf))
    cb = jnp.einsum('bclgn,bcsgn->bcgls', cc, bc)
    y_diag = jnp.einsum('bcgels,bcsgep->bclgep', cb[:, :, :, None] * decay, xdt)
    decay_states = jnp.exp(a_cum[..., -1:] - a_cum)
    states = jnp.einsum('bcsgn,bcsgep->bcgepn', bc, xdt * jnp.transpose(decay_states, (0, 1, 4, 2, 3))[..., None])
    chunk_decay = jnp.exp(a_cum[..., -1])

    def step(h, inp):
        st, dec = inp
        return h * dec[..., None, None] + st, h

    h0 = jnp.zeros((b, G, E, P, N), jnp.float32)
    _, prev = lax.scan(step, h0, (jnp.moveaxis(states, 1, 0), jnp.moveaxis(chunk_decay, 1, 0)))
    prev = jnp.moveaxis(prev, 0, 1)
    y_off = jnp.einsum('bclgn,bcgepn->bclgep', cc, prev) * jnp.transpose(jnp.exp(a_cum), (0, 1, 4, 2, 3))[..., None]
    y = (y_diag + y_off + x * d_skip.astype(jnp.float32).reshape(G, E, 1)).reshape(b, s, SSD_WIDTH)
    g = (y * jax.nn.silu(z.astype(jnp.float32))).reshape(b, s, G, SSD_WIDTH // G)
    g = g * lax.rsqrt(jnp.mean(g * g, axis=-1, keepdims=True) + EPS)
    return (g.reshape(b, s, SSD_WIDTH) * norm_w.astype(jnp.float32)).astype(z.dtype)


def _fwd_setup_inputs(seed: int = 0) -> dict:
    key = jax.random.key(seed)
    ks = jax.random.split(key, 24)
    f32 = jnp.float32
    L = DEPTH

    def dense(k, fan_in, fan_out):
        return jax.random.normal(k, (L, fan_in, fan_out), f32) * fan_in ** -0.5

    def gain(k, n):
        return 1.0 + 0.02 * jax.random.normal(k, (L, n), f32)

    x = jax.random.normal(ks[0], (BATCH, SEQ, D_MODEL), f32)
    positions = jnp.arange(SEQ, dtype=jnp.int32)[None, :] + jax.random.randint(ks[1], (BATCH, 1), 0, SEQ, dtype=jnp.int32)
    dt0 = jnp.exp(jax.random.uniform(ks[2], (L, SSD_HEADS), f32, float(np.log(1e-3)), float(np.log(1e-1))))
    dt_bias = dt0 + jnp.log(-jnp.expm1(-dt0))
    a_log = jnp.log(jax.random.uniform(ks[3], (L, SSD_HEADS), f32, 1.0, 16.0))
    return {
        "x": x,
        "positions": positions,
        "w_in": dense(ks[4], D_MODEL, D_IN),
        "q_norm_w": gain(ks[5], Q_LORA_RANK),
        "w_uq": dense(ks[6], Q_LORA_RANK, MLA_HEADS * QK_HEAD_DIM),
        "kv_norm_w": gain(ks[7], KV_LORA_RANK),
        "w_ukv": dense(ks[8], KV_LORA_RANK, MLA_HEADS * (QK_NOPE_DIM + V_HEAD_DIM)),
        "conv_w": jax.random.normal(ks[9], (L, SSD_CONV, SSD_CONV_DIM), f32) * SSD_CONV ** -0.5,
        "conv_b": 0.02 * jax.random.normal(ks[10], (L, SSD_CONV_DIM), f32),
        "dt_bias": dt_bias,
        "a_log": a_log,
        "d_skip": gain(ks[11], SSD_HEADS),
        "ssd_norm_w": gain(ks[12], SSD_WIDTH),
        "attn_out_norm_w": gain(ks[13], MLA_WIDTH),
        "w_out": dense(ks[14], D_MIX, D_MODEL),
        "pre_mix_norm_w": gain(ks[15], D_MODEL),
        "post_mix_norm_w": gain(ks[16], D_MODEL),
        "pre_ffn_norm_w": gain(ks[17], D_MODEL),
        "post_ffn_norm_w": gain(ks[18], D_MODEL),
        "w_gate": dense(ks[19], D_MODEL, D_FF),
        "w_up": dense(ks[20], D_MODEL, D_FF),
        "w_down": dense(ks[21], D_FF, D_MODEL),
    }


def _fwd_reference(x, positions, w_in, q_norm_w, w_uq, kv_norm_w, w_ukv, conv_w, conv_b, dt_bias, a_log,
              d_skip, ssd_norm_w, attn_out_norm_w, w_out, pre_mix_norm_w, post_mix_norm_w,
              pre_ffn_norm_w, post_ffn_norm_w, w_gate, w_up, w_down):
    cos, sin = rope_tables(positions)
    h = x
    for l in range(DEPTH):
        u = rms_norm(h, pre_mix_norm_w[l])
        c_q, c_kv, k_rope, z, xbc, dt_raw = split_cols(u @ w_in[l], IN_SIZES)
        attn = rms_norm(mla_group(c_q, c_kv, k_rope, cos, sin, q_norm_w[l], w_uq[l], kv_norm_w[l], w_ukv[l]),
                        attn_out_norm_w[l])
        ssm = ssd_group(z, xbc, dt_raw, conv_w[l], conv_b[l], dt_bias[l], a_log[l], d_skip[l], ssd_norm_w[l])
        mix = jnp.concatenate([attn, ssm], axis=-1) @ w_out[l]
        h = h + rms_norm(mix, post_mix_norm_w[l])
        v = rms_norm(h, pre_ffn_norm_w[l])
        ffn = (jax.nn.silu(v @ w_gate[l]) * (v @ w_up[l])) @ w_down[l]
        h = h + rms_norm(ffn, post_ffn_norm_w[l])
    return h


import jax as _jax
import jax.numpy as _jnp

TWIN_FORMAT = 'train_step'
FWD_PARAMS = ['x', 'positions', 'w_in', 'q_norm_w', 'w_uq', 'kv_norm_w', 'w_ukv', 'conv_w', 'conv_b', 'dt_bias', 'a_log', 'd_skip', 'ssd_norm_w', 'attn_out_norm_w', 'w_out', 'pre_mix_norm_w', 'post_mix_norm_w', 'pre_ffn_norm_w', 'post_ffn_norm_w', 'w_gate', 'w_up', 'w_down']
TWIN_WEIGHTS = ['w_in', 'q_norm_w', 'w_uq', 'kv_norm_w', 'w_ukv', 'conv_w', 'conv_b', 'dt_bias', 'a_log', 'd_skip', 'ssd_norm_w', 'attn_out_norm_w', 'w_out', 'pre_mix_norm_w', 'post_mix_norm_w', 'pre_ffn_norm_w', 'post_ffn_norm_w', 'w_gate', 'w_up', 'w_down']
TWIN_DIFF_INPUT = 'x'
TWIN_INPUTS = ['x', 'positions', 'w_in', 'q_norm_w', 'w_uq', 'kv_norm_w', 'w_ukv', 'conv_w', 'conv_b', 'dt_bias', 'a_log', 'd_skip', 'ssd_norm_w', 'attn_out_norm_w', 'w_out', 'pre_mix_norm_w', 'post_mix_norm_w', 'pre_ffn_norm_w', 'post_ffn_norm_w', 'w_gate', 'w_up', 'w_down', 'loss_target', 'm_w_in', 'm_q_norm_w', 'm_w_uq', 'm_kv_norm_w', 'm_w_ukv', 'm_conv_w', 'm_conv_b', 'm_dt_bias', 'm_a_log', 'm_d_skip', 'm_ssd_norm_w', 'm_attn_out_norm_w', 'm_w_out', 'm_pre_mix_norm_w', 'm_post_mix_norm_w', 'm_pre_ffn_norm_w', 'm_post_ffn_norm_w', 'm_w_gate', 'm_w_up', 'm_w_down', 'v_w_in', 'v_q_norm_w', 'v_w_uq', 'v_kv_norm_w', 'v_w_ukv', 'v_conv_w', 'v_conv_b', 'v_dt_bias', 'v_a_log', 'v_d_skip', 'v_ssd_norm_w', 'v_attn_out_norm_w', 'v_w_out', 'v_pre_mix_norm_w', 'v_post_mix_norm_w', 'v_pre_ffn_norm_w', 'v_post_ffn_norm_w', 'v_w_gate', 'v_w_up', 'v_w_down']
TWIN_OUTPUTS = ['loss', 'grad_x', 'grad_w_in', 'grad_q_norm_w', 'grad_w_uq', 'grad_kv_norm_w', 'grad_w_ukv', 'grad_conv_w', 'grad_conv_b', 'grad_dt_bias', 'grad_a_log', 'grad_d_skip', 'grad_ssd_norm_w', 'grad_attn_out_norm_w', 'grad_w_out', 'grad_pre_mix_norm_w', 'grad_post_mix_norm_w', 'grad_pre_ffn_norm_w', 'grad_post_ffn_norm_w', 'grad_w_gate', 'grad_w_up', 'grad_w_down', 'delta_w_in', 'delta_q_norm_w', 'delta_w_uq', 'delta_kv_norm_w', 'delta_w_ukv', 'delta_conv_w', 'delta_conv_b', 'delta_dt_bias', 'delta_a_log', 'delta_d_skip', 'delta_ssd_norm_w', 'delta_attn_out_norm_w', 'delta_w_out', 'delta_pre_mix_norm_w', 'delta_post_mix_norm_w', 'delta_pre_ffn_norm_w', 'delta_post_ffn_norm_w', 'delta_w_gate', 'delta_w_up', 'delta_w_down', 'new_m_w_in', 'new_m_q_norm_w', 'new_m_w_uq', 'new_m_kv_norm_w', 'new_m_w_ukv', 'new_m_conv_w', 'new_m_conv_b', 'new_m_dt_bias', 'new_m_a_log', 'new_m_d_skip', 'new_m_ssd_norm_w', 'new_m_attn_out_norm_w', 'new_m_w_out', 'new_m_pre_mix_norm_w', 'new_m_post_mix_norm_w', 'new_m_pre_ffn_norm_w', 'new_m_post_ffn_norm_w', 'new_m_w_gate', 'new_m_w_up', 'new_m_w_down', 'new_v_w_in', 'new_v_q_norm_w', 'new_v_w_uq', 'new_v_kv_norm_w', 'new_v_w_ukv', 'new_v_conv_w', 'new_v_conv_b', 'new_v_dt_bias', 'new_v_a_log', 'new_v_d_skip', 'new_v_ssd_norm_w', 'new_v_attn_out_norm_w', 'new_v_w_out', 'new_v_pre_mix_norm_w', 'new_v_post_mix_norm_w', 'new_v_pre_ffn_norm_w', 'new_v_post_ffn_norm_w', 'new_v_w_gate', 'new_v_w_up', 'new_v_w_down']
TWIN_LEAF_KINDS = {'loss': 'loss', 'grad_x': 'grad_x', 'grad_w_in': 'grad_w', 'grad_q_norm_w': 'grad_w', 'grad_w_uq': 'grad_w', 'grad_kv_norm_w': 'grad_w', 'grad_w_ukv': 'grad_w', 'grad_conv_w': 'grad_w', 'grad_conv_b': 'grad_w', 'grad_dt_bias': 'grad_w', 'grad_a_log': 'grad_w', 'grad_d_skip': 'grad_w', 'grad_ssd_norm_w': 'grad_w', 'grad_attn_out_norm_w': 'grad_w', 'grad_w_out': 'grad_w', 'grad_pre_mix_norm_w': 'grad_w', 'grad_post_mix_norm_w': 'grad_w', 'grad_pre_ffn_norm_w': 'grad_w', 'grad_post_ffn_norm_w': 'grad_w', 'grad_w_gate': 'grad_w', 'grad_w_up': 'grad_w', 'grad_w_down': 'grad_w', 'delta_w_in': 'delta_w', 'delta_q_norm_w': 'delta_w', 'delta_w_uq': 'delta_w', 'delta_kv_norm_w': 'delta_w', 'delta_w_ukv': 'delta_w', 'delta_conv_w': 'delta_w', 'delta_conv_b': 'delta_w', 'delta_dt_bias': 'delta_w', 'delta_a_log': 'delta_w', 'delta_d_skip': 'delta_w', 'delta_ssd_norm_w': 'delta_w', 'delta_attn_out_norm_w': 'delta_w', 'delta_w_out': 'delta_w', 'delta_pre_mix_norm_w': 'delta_w', 'delta_post_mix_norm_w': 'delta_w', 'delta_pre_ffn_norm_w': 'delta_w', 'delta_post_ffn_norm_w': 'delta_w', 'delta_w_gate': 'delta_w', 'delta_w_up': 'delta_w', 'delta_w_down': 'delta_w', 'new_m_w_in': 'new_m', 'new_m_q_norm_w': 'new_m', 'new_m_w_uq': 'new_m', 'new_m_kv_norm_w': 'new_m', 'new_m_w_ukv': 'new_m', 'new_m_conv_w': 'new_m', 'new_m_conv_b': 'new_m', 'new_m_dt_bias': 'new_m', 'new_m_a_log': 'new_m', 'new_m_d_skip': 'new_m', 'new_m_ssd_norm_w': 'new_m', 'new_m_attn_out_norm_w': 'new_m', 'new_m_w_out': 'new_m', 'new_m_pre_mix_norm_w': 'new_m', 'new_m_post_mix_norm_w': 'new_m', 'new_m_pre_ffn_norm_w': 'new_m', 'new_m_post_ffn_norm_w': 'new_m', 'new_m_w_gate': 'new_m', 'new_m_w_up': 'new_m', 'new_m_w_down': 'new_m', 'new_v_w_in': 'new_v', 'new_v_q_norm_w': 'new_v', 'new_v_w_uq': 'new_v', 'new_v_kv_norm_w': 'new_v', 'new_v_w_ukv': 'new_v', 'new_v_conv_w': 'new_v', 'new_v_conv_b': 'new_v', 'new_v_dt_bias': 'new_v', 'new_v_a_log': 'new_v', 'new_v_d_skip': 'new_v', 'new_v_ssd_norm_w': 'new_v', 'new_v_attn_out_norm_w': 'new_v', 'new_v_w_out': 'new_v', 'new_v_pre_mix_norm_w': 'new_v', 'new_v_post_mix_norm_w': 'new_v', 'new_v_pre_ffn_norm_w': 'new_v', 'new_v_post_ffn_norm_w': 'new_v', 'new_v_w_gate': 'new_v', 'new_v_w_up': 'new_v', 'new_v_w_down': 'new_v'}


def _forward(args):
    return _fwd_reference(*[args[k] for k in FWD_PARAMS])


def _output_shape():
    out = _jax.eval_shape(lambda: _forward(_fwd_setup_inputs(0)))
    return out.shape, out.dtype

N_MICROBATCH = 1
ADAM_LR = 0.001
ADAM_B1 = 0.9
ADAM_B2 = 0.999
ADAM_EPS = 1e-08
ADAM_WD = 0.01
ADAM_STEP = 10
PER_EXAMPLE_BATCH_AXIS = {'x': 0, 'positions': 0, 'loss_target': 0}
SHARED_INPUTS = []
_WEIGHT_DTYPES = {'w_in': _jnp.float32, 'q_norm_w': _jnp.float32, 'w_uq': _jnp.float32, 'kv_norm_w': _jnp.float32, 'w_ukv': _jnp.float32, 'conv_w': _jnp.float32, 'conv_b': _jnp.float32, 'dt_bias': _jnp.float32, 'a_log': _jnp.float32, 'd_skip': _jnp.float32, 'ssd_norm_w': _jnp.float32, 'attn_out_norm_w': _jnp.float32, 'w_out': _jnp.float32, 'pre_mix_norm_w': _jnp.float32, 'post_mix_norm_w': _jnp.float32, 'pre_ffn_norm_w': _jnp.float32, 'post_ffn_norm_w': _jnp.float32, 'w_gate': _jnp.float32, 'w_up': _jnp.float32, 'w_down': _jnp.float32}
MOMENT_SCALE = {'w_in': 2.941031e-01, 'q_norm_w': 3.650768e-01, 'w_uq': 2.067330e-01, 'kv_norm_w': 6.900135e-01, 'w_ukv': 3.197927e-01, 'conv_w': 2.355752e-01, 'conv_b': 5.783368e-01, 'dt_bias': 5.616195e-01, 'a_log': 1.922125e+00, 'd_skip': 1.531581e+00, 'ssd_norm_w': 3.486621e-01, 'attn_out_norm_w': 4.014557e-01, 'w_out': 3.702638e-01, 'pre_mix_norm_w': 3.930488e-01, 'post_mix_norm_w': 1.599574e+01, 'pre_ffn_norm_w': 3.429417e-01, 'post_ffn_norm_w': 1.597011e+01, 'w_gate': 1.148831e-01, 'w_up': 1.628090e-01, 'w_down': 2.687829e-01}


def _to_microbatches(a, axis):
    t = _jnp.moveaxis(a, axis, 0)
    t = t.reshape((N_MICROBATCH, t.shape[0] // N_MICROBATCH) + t.shape[1:])
    return _jnp.moveaxis(t, 1, axis + 1)


def setup_inputs(seed: int = 0) -> dict:
    inp = _fwd_setup_inputs(seed)
    key = _jax.random.fold_in(_jax.random.key(seed), 7919)
    shape, _ = _output_shape()
    out = dict(inp)
    out["loss_target"] = _jax.random.normal(_jax.random.fold_in(key, 0), shape, _jnp.float32)
    for i, name in enumerate(TWIN_WEIGHTS):
        w = inp[name].astype(_jnp.float32)
        if MOMENT_SCALE is None:
            s = _jnp.sqrt(_jnp.mean(_jnp.square(w)) + 1e-30)
        else:
            s = MOMENT_SCALE[name]
        km, kv = _jax.random.split(_jax.random.fold_in(key, i + 1))
        out[name] = w
        out["m_" + name] = s * _jax.random.normal(km, w.shape, _jnp.float32)
        out["v_" + name] = (s * s) * _jax.random.uniform(kv, w.shape, _jnp.float32, 0.5, 1.5)
    if N_MICROBATCH > 1:
        for name, axis in PER_EXAMPLE_BATCH_AXIS.items():
            out[name] = _to_microbatches(out[name], axis)
    return {'x': out['x'], 'positions': out['positions'], 'w_in': out['w_in'], 'q_norm_w': out['q_norm_w'], 'w_uq': out['w_uq'], 'kv_norm_w': out['kv_norm_w'], 'w_ukv': out['w_ukv'], 'conv_w': out['conv_w'], 'conv_b': out['conv_b'], 'dt_bias': out['dt_bias'], 'a_log': out['a_log'], 'd_skip': out['d_skip'], 'ssd_norm_w': out['ssd_norm_w'], 'attn_out_norm_w': out['attn_out_norm_w'], 'w_out': out['w_out'], 'pre_mix_norm_w': out['pre_mix_norm_w'], 'post_mix_norm_w': out['post_mix_norm_w'], 'pre_ffn_norm_w': out['pre_ffn_norm_w'], 'post_ffn_norm_w': out['post_ffn_norm_w'], 'w_gate': out['w_gate'], 'w_up': out['w_up'], 'w_down': out['w_down'], 'loss_target': out['loss_target'], 'm_w_in': out['m_w_in'], 'm_q_norm_w': out['m_q_norm_w'], 'm_w_uq': out['m_w_uq'], 'm_kv_norm_w': out['m_kv_norm_w'], 'm_w_ukv': out['m_w_ukv'], 'm_conv_w': out['m_conv_w'], 'm_conv_b': out['m_conv_b'], 'm_dt_bias': out['m_dt_bias'], 'm_a_log': out['m_a_log'], 'm_d_skip': out['m_d_skip'], 'm_ssd_norm_w': out['m_ssd_norm_w'], 'm_attn_out_norm_w': out['m_attn_out_norm_w'], 'm_w_out': out['m_w_out'], 'm_pre_mix_norm_w': out['m_pre_mix_norm_w'], 'm_post_mix_norm_w': out['m_post_mix_norm_w'], 'm_pre_ffn_norm_w': out['m_pre_ffn_norm_w'], 'm_post_ffn_norm_w': out['m_post_ffn_norm_w'], 'm_w_gate': out['m_w_gate'], 'm_w_up': out['m_w_up'], 'm_w_down': out['m_w_down'], 'v_w_in': out['v_w_in'], 'v_q_norm_w': out['v_q_norm_w'], 'v_w_uq': out['v_w_uq'], 'v_kv_norm_w': out['v_kv_norm_w'], 'v_w_ukv': out['v_w_ukv'], 'v_conv_w': out['v_conv_w'], 'v_conv_b': out['v_conv_b'], 'v_dt_bias': out['v_dt_bias'], 'v_a_log': out['v_a_log'], 'v_d_skip': out['v_d_skip'], 'v_ssd_norm_w': out['v_ssd_norm_w'], 'v_attn_out_norm_w': out['v_attn_out_norm_w'], 'v_w_out': out['v_w_out'], 'v_pre_mix_norm_w': out['v_pre_mix_norm_w'], 'v_post_mix_norm_w': out['v_post_mix_norm_w'], 'v_pre_ffn_norm_w': out['v_pre_ffn_norm_w'], 'v_post_ffn_norm_w': out['v_post_ffn_norm_w'], 'v_w_gate': out['v_w_gate'], 'v_w_up': out['v_w_up'], 'v_w_down': out['v_w_down']}


def _loss(weights, diff, rest, loss_target):
    with _jax.named_scope("forward"):
        args = {**rest, TWIN_DIFF_INPUT: diff, **{k: w.astype(_WEIGHT_DTYPES[k]) for k, w in weights.items()}}
        y = _forward(args)
    with _jax.named_scope("loss_head"):
        err = _jnp.square(y.astype(_jnp.float32) - loss_target)
        return 0.5 * _jnp.sum(_jnp.mean(err, axis=-1)) if err.ndim else 0.5 * err


def _adamw(w, g, m, v):
    m = ADAM_B1 * m + (1.0 - ADAM_B1) * g
    v = ADAM_B2 * v + (1.0 - ADAM_B2) * _jnp.square(g)
    m_hat = m / (1.0 - ADAM_B1 ** ADAM_STEP)
    v_hat = v / (1.0 - ADAM_B2 ** ADAM_STEP)
    delta = -ADAM_LR * (m_hat / (_jnp.sqrt(v_hat) + ADAM_EPS) + ADAM_WD * w)
    return delta, m, v


def reference(x, positions, w_in, q_norm_w, w_uq, kv_norm_w, w_ukv, conv_w, conv_b, dt_bias, a_log, d_skip, ssd_norm_w, attn_out_norm_w, w_out, pre_mix_norm_w, post_mix_norm_w, pre_ffn_norm_w, post_ffn_norm_w, w_gate, w_up, w_down, loss_target, m_w_in, m_q_norm_w, m_w_uq, m_kv_norm_w, m_w_ukv, m_conv_w, m_conv_b, m_dt_bias, m_a_log, m_d_skip, m_ssd_norm_w, m_attn_out_norm_w, m_w_out, m_pre_mix_norm_w, m_post_mix_norm_w, m_pre_ffn_norm_w, m_post_ffn_norm_w, m_w_gate, m_w_up, m_w_down, v_w_in, v_q_norm_w, v_w_uq, v_kv_norm_w, v_w_ukv, v_conv_w, v_conv_b, v_dt_bias, v_a_log, v_d_skip, v_ssd_norm_w, v_attn_out_norm_w, v_w_out, v_pre_mix_norm_w, v_post_mix_norm_w, v_pre_ffn_norm_w, v_post_ffn_norm_w, v_w_gate, v_w_up, v_w_down):
    given = dict(x=x, positions=positions, w_in=w_in, q_norm_w=q_norm_w, w_uq=w_uq, kv_norm_w=kv_norm_w, w_ukv=w_ukv, conv_w=conv_w, conv_b=conv_b, dt_bias=dt_bias, a_log=a_log, d_skip=d_skip, ssd_norm_w=ssd_norm_w, attn_out_norm_w=attn_out_norm_w, w_out=w_out, pre_mix_norm_w=pre_mix_norm_w, post_mix_norm_w=post_mix_norm_w, pre_ffn_norm_w=pre_ffn_norm_w, post_ffn_norm_w=post_ffn_norm_w, w_gate=w_gate, w_up=w_up, w_down=w_down, loss_target=loss_target, m_w_in=m_w_in, m_q_norm_w=m_q_norm_w, m_w_uq=m_w_uq, m_kv_norm_w=m_kv_norm_w, m_w_ukv=m_w_ukv, m_conv_w=m_conv_w, m_conv_b=m_conv_b, m_dt_bias=m_dt_bias, m_a_log=m_a_log, m_d_skip=m_d_skip, m_ssd_norm_w=m_ssd_norm_w, m_attn_out_norm_w=m_attn_out_norm_w, m_w_out=m_w_out, m_pre_mix_norm_w=m_pre_mix_norm_w, m_post_mix_norm_w=m_post_mix_norm_w, m_pre_ffn_norm_w=m_pre_ffn_norm_w, m_post_ffn_norm_w=m_post_ffn_norm_w, m_w_gate=m_w_gate, m_w_up=m_w_up, m_w_down=m_w_down, v_w_in=v_w_in, v_q_norm_w=v_q_norm_w, v_w_uq=v_w_uq, v_kv_norm_w=v_kv_norm_w, v_w_ukv=v_w_ukv, v_conv_w=v_conv_w, v_conv_b=v_conv_b, v_dt_bias=v_dt_bias, v_a_log=v_a_log, v_d_skip=v_d_skip, v_ssd_norm_w=v_ssd_norm_w, v_attn_out_norm_w=v_attn_out_norm_w, v_w_out=v_w_out, v_pre_mix_norm_w=v_pre_mix_norm_w, v_post_mix_norm_w=v_post_mix_norm_w, v_pre_ffn_norm_w=v_pre_ffn_norm_w, v_post_ffn_norm_w=v_post_ffn_norm_w, v_w_gate=v_w_gate, v_w_up=v_w_up, v_w_down=v_w_down)
    weights = {n: given[n] for n in TWIN_WEIGHTS}
    shared = {n: given[n] for n in SHARED_INPUTS}
    per_example = {n: given[n] for n in ['x', 'positions']}
    grad_fn = _jax.value_and_grad(_loss, argnums=(0, 1))

    def one_microbatch(ex, loss_target):
        ex = dict(ex)
        diff = ex.pop(TWIN_DIFF_INPUT)
        return grad_fn(weights, diff, {**shared, **ex}, loss_target)

    if N_MICROBATCH == 1:
        loss, (grad_w, grad_x) = one_microbatch(per_example, given["loss_target"])
    else:
        def body(carry, xs):
            loss_sum, grad_sum = carry
            l_k, (gw_k, gx_k) = one_microbatch(xs[0], xs[1])
            with _jax.named_scope("update"):
                return (loss_sum + l_k, _jax.tree.map(_jnp.add, grad_sum, gw_k)), gx_k

        init = (_jnp.zeros((), _jnp.float32), _jax.tree.map(_jnp.zeros_like, weights))
        (loss, grad_w), grad_x = _jax.lax.scan(body, init, (per_example, given["loss_target"]))
    with _jax.named_scope("update"):
        delta_w, new_m, new_v = {}, {}, {}
        for n in TWIN_WEIGHTS:
            delta_w[n], new_m[n], new_v[n] = _adamw(weights[n], grad_w[n], given["m_" + n], given["v_" + n])
    return (loss, grad_x, *[grad_w[n] for n in TWIN_WEIGHTS], *[delta_w[n] for n in TWIN_WEIGHTS],
            *[new_m[n] for n in TWIN_WEIGHTS], *[new_v[n] for n in TWIN_WEIGHTS])
```

```python
import numpy as np

import jax
import jax.numpy as jnp
from jax import lax
from jax.experimental import pallas as pl
from jax.experimental.pallas import tpu as pltpu

F32 = jnp.float32
BF16 = jnp.bfloat16
MXU_DTYPE = jnp.bfloat16
EPS = 1e-6
VMEM_LIMIT_BYTES = 48 * 1024 * 1024
K_TILE_MAX = 2048

N_DEV = 8
D_MODEL = 2048
Q_RANK = 512
KV_RANK = 512
ROPE = 64
HALF = ROPE // 2
HEADS = 8
NOPE = 128
VDIM = 128
QK = NOPE + ROPE
SSD_W = 1024
SSD_H = 16
SSD_P = 64
SSD_G = 2
SSD_E = SSD_H // SSD_G
SSD_N = 128
CHUNK = 128
CONV_K = 4
CONV_DIM = SSD_W + 2 * SSD_G * SSD_N
B_OFF = SSD_W
C_OFF = SSD_W + SSD_G * SSD_N
D_FF = 5632
D_IN = Q_RANK + KV_RANK + ROPE + SSD_W + CONV_DIM + SSD_H
ROPE_THETA = 10000.0
LANE = 128
HEAD_LANE = ROPE

ADAM_LR = 0.001
ADAM_B1 = 0.9
ADAM_B2 = 0.999
ADAM_EPS = 1e-08
ADAM_WD = 0.01
ADAM_STEP = 10


def _pick(n, cands):
    for c in cands:
        if n % c == 0:
            return c
    return n


def _params(*sem):
    return pltpu.CompilerParams(dimension_semantics=sem, vmem_limit_bytes=VMEM_LIMIT_BYTES)


def _sigmoid(x):
    return 1.0 / (1.0 + jnp.exp(-x))


def _silu(x):
    return x * _sigmoid(x)


def _dsilu(x):
    s = _sigmoid(x)
    return s * (1.0 + x * (1.0 - s))


def _softplus(x):
    e = jnp.exp(-jnp.abs(x))
    small = e * (1.0 - e * (0.5 - e * (1.0 / 3.0)))
    return jnp.maximum(x, 0.0) + jnp.where(e < 0.01, small, jnp.log(1.0 + e))


def _dot(a, b, ca, cb):
    return lax.dot_general(a, b, (((ca,), (cb,)), ((), ())), preferred_element_type=F32)


def _mx(v):
    return v.astype(MXU_DTYPE)


def _split3(a):
    hi = a.astype(BF16)
    r1 = a - hi.astype(F32)
    mid = r1.astype(BF16)
    lo = (r1 - mid.astype(F32)).astype(BF16)
    return hi, mid, lo


def _exact_dot(a, b, ca, cb, split_a):
    if split_a:
        return sum(_dot(p, b, ca, cb) for p in _split3(a))
    return sum(_dot(a, p, ca, cb) for p in _split3(b))


def _mm(a, b, mode, *, a_blk=False, b_blk=False, out_blk=False, a_cols=None, add=None, out_dtype=F32, name="mm"):
    a2, b2 = a.shape[-2:], b.shape[-2:]
    a_last = a2[1] if a_cols is None else a_cols[1]
    a_start = 0 if a_cols is None else a_cols[0]
    if mode == "nn":
        m, k, (k2, n) = a2[0], a_last, b2
    elif mode == "nt":
        m, k, (n, k2) = a2[0], a_last, b2
    else:
        k, m, (k2, n) = a2[0], a_last, b2
    assert k == k2, (a.shape, b.shape, mode)
    tm = _pick(m, (1024, 704, 512, 256, 128))
    tn = _pick(n, (1024, 768, 704, 512, 256, 192, 128))
    tk = k if k <= K_TILE_MAX else _pick(k, (K_TILE_MAX, 1024, 512))
    nk = k // tk
    jo = N_DEV if out_blk else 1
    jr = N_DEV if (a_blk and b_blk and not out_blk) else 1
    ca, cb = {"nn": (1, 0), "nt": (1, 1), "tn": (0, 0)}[mode]
    has_add = add is not None
    single = jr * nk == 1
    if mode == "tn":
        assert a_start % tm == 0
        a_block, a_idx = (tk, tm), (lambda i, kk: (kk, i + a_start // tm))
    else:
        assert a_start % tk == 0
        a_block, a_idx = (tm, tk), (lambda i, kk: (i, kk + a_start // tk))
    b_block, b_idx = ((tn, tk), (lambda nn_, kk: (nn_, kk))) if mode == "nt" else ((tk, tn), (lambda nn_, kk: (kk, nn_)))

    def sel(o, r):
        return o if out_blk else r

    a_spec = (pl.BlockSpec((None,) + a_block, lambda o, i, nn_, r, kk: (sel(o, r),) + a_idx(i, kk)) if a_blk
              else pl.BlockSpec(a_block, lambda o, i, nn_, r, kk: a_idx(i, kk)))
    b_spec = (pl.BlockSpec((None,) + b_block, lambda o, i, nn_, r, kk: (sel(o, r),) + b_idx(nn_, kk)) if b_blk
              else pl.BlockSpec(b_block, lambda o, i, nn_, r, kk: b_idx(nn_, kk)))
    o_spec = (pl.BlockSpec((None, tm, tn), lambda o, i, nn_, r, kk: (o, i, nn_)) if out_blk
              else pl.BlockSpec((tm, tn), lambda o, i, nn_, r, kk: (i, nn_)))

    def body(*refs):
        a_ref, b_ref = refs[0], refs[1]
        add_ref = refs[2] if has_add else None
        o_ref = refs[3] if has_add else refs[2]
        part = _dot(_mx(a_ref[...]), _mx(b_ref[...]), ca, cb)
        if single:
            if has_add:
                part = part + add_ref[...]
            o_ref[...] = part.astype(o_ref.dtype)
            return
        acc = refs[-1]
        r, kk = pl.program_id(3), pl.program_id(4)
        first = jnp.logical_and(r == 0, kk == 0)
        last = jnp.logical_and(r == jr - 1, kk == nk - 1)

        @pl.when(first)
        def _():
            acc[...] = part

        @pl.when(jnp.logical_not(first))
        def _():
            acc[...] += part

        @pl.when(last)
        def _():
            res = acc[...]
            if has_add:
                res = res + add_ref[...]
            o_ref[...] = res.astype(o_ref.dtype)

    out_shape = ((N_DEV, m, n) if out_blk else (m, n))
    return pl.pallas_call(
        body, name=name, grid=(jo, m // tm, n // tn, jr, nk),
        in_specs=[a_spec, b_spec] + ([o_spec] if has_add else []), out_specs=o_spec,
        out_shape=jax.ShapeDtypeStruct(out_shape, out_dtype),
        scratch_shapes=[] if single else [pltpu.VMEM((tm, tn), F32)],
        compiler_params=_params("parallel", "parallel", "parallel", "arbitrary", "arbitrary"),
    )(*((a, b) + ((add,) if has_add else ())))


def _row_tile(r_):
    return _pick(r_, (256, 128, 64, 32, 16, 8))


def _rms_fwd(t, w, groups=1, res=None, out_dtype=F32, name="rms_fwd"):
    r_, f = t.shape
    fg = f // groups
    tr = _row_tile(r_)
    has_res = res is not None

    def body(*refs):
        t_ref, w_ref = refs[0], refs[1]
        res_ref = refs[2] if has_res else None
        o_ref = refs[-1]
        for g in range(groups):
            sl = slice(g * fg, (g + 1) * fg)
            tv = t_ref[:, sl].astype(F32)
            r = lax.rsqrt(jnp.mean(tv * tv, axis=-1, keepdims=True) + EPS)
            y = tv * r * w_ref[:, sl]
            if has_res:
                y = y + res_ref[:, sl]
            o_ref[:, sl] = y.astype(o_ref.dtype)

    row = pl.BlockSpec((tr, f), lambda i: (i, 0))
    wsp = pl.BlockSpec((1, f), lambda i: (0, 0))
    return pl.pallas_call(
        body, name=name, grid=(r_ // tr,),
        in_specs=[row, wsp] + ([row] if has_res else []), out_specs=row,
        out_shape=jax.ShapeDtypeStruct((r_, f), out_dtype),
        compiler_params=_params("parallel"),
    )(*((t, w.reshape(1, f)) + ((res,) if has_res else ())))


def _rms_bwd(t, w, dys, res=None, out_dtype=F32, name="rms_bwd"):
    r_, f = t.shape
    groups = len(dys)
    fg = f // groups
    tr = _row_tile(r_)
    has_res = res is not None

    def body(*refs):
        t_ref, w_ref = refs[0], refs[1]
        dy_refs = refs[2:2 + groups]
        res_ref = refs[2 + groups] if has_res else None
        dt_ref, dw_ref = refs[-2], refs[-1]

        @pl.when(pl.program_id(0) == 0)
        def _():
            dw_ref[...] = jnp.zeros_like(dw_ref)

        for g in range(groups):
            sl = slice(g * fg, (g + 1) * fg)
            tv = t_ref[:, sl].astype(F32)
            dyv = dy_refs[g][...].astype(F32)
            r = lax.rsqrt(jnp.mean(tv * tv, axis=-1, keepdims=True) + EPS)
            gw = dyv * w_ref[:, sl]
            c = jnp.mean(gw * tv, axis=-1, keepdims=True)
            dt = r * gw - tv * (r * r * r * c)
            if has_res:
                dt = dt + res_ref[:, sl]
            dt_ref[:, sl] = dt.astype(dt_ref.dtype)
            dw_ref[:, sl] += jnp.sum(dyv * tv * r, axis=0, keepdims=True)

    row = pl.BlockSpec((tr, f), lambda i: (i, 0))
    grow = pl.BlockSpec((tr, fg), lambda i: (i, 0))
    wsp = pl.BlockSpec((1, f), lambda i: (0, 0))
    return pl.pallas_call(
        body, name=name, grid=(r_ // tr,),
        in_specs=[row, wsp] + [grow] * groups + ([row] if has_res else []), out_specs=[row, wsp],
        out_shape=[jax.ShapeDtypeStruct((r_, f), out_dtype), jax.ShapeDtypeStruct((1, f), F32)],
        compiler_params=_params("arbitrary"),
    )(*((t, w.reshape(1, f)) + tuple(dys) + ((res,) if has_res else ())))


def _hnorm_fwd(o, w, name="attn_out_norm"):
    h, s_, v = o.shape
    tr = _row_tile(s_)

    def body(o_ref, w_ref, y_ref):
        ss = jnp.sum(o_ref[0] * o_ref[0], axis=-1, keepdims=True)
        for i in range(1, h):
            ss = ss + jnp.sum(o_ref[i] * o_ref[i], axis=-1, keepdims=True)
        r = lax.rsqrt(ss * (1.0 / (h * v)) + EPS)
        for i in range(h):
            y_ref[i] = (o_ref[i] * r * w_ref[i]).astype(y_ref.dtype)

    blk = pl.BlockSpec((h, tr, v), lambda i: (0, i, 0))
    wsp = pl.BlockSpec((h, 1, v), lambda i: (0, 0, 0))
    return pl.pallas_call(
        body, name=name, grid=(s_ // tr,), in_specs=[blk, wsp], out_specs=blk,
        out_shape=jax.ShapeDtypeStruct(o.shape, MXU_DTYPE), compiler_params=_params("parallel"),
    )(o, w)


def _hnorm_bwd(o, w, dy, name="attn_out_norm_bwd"):
    h, s_, v = o.shape
    tr = _row_tile(s_)

    def body(o_ref, w_ref, dy_ref, do_ref, dw_ref):
        @pl.when(pl.program_id(0) == 0)
        def _():
            dw_ref[...] = jnp.zeros_like(dw_ref)

        ss = jnp.zeros((tr, 1), F32)
        cc = jnp.zeros((tr, 1), F32)
        for i in range(h):
            ov = o_ref[i]
            ss = ss + jnp.sum(ov * ov, axis=-1, keepdims=True)
            cc = cc + jnp.sum(dy_ref[i] * w_ref[i] * ov, axis=-1, keepdims=True)
        r = lax.rsqrt(ss * (1.0 / (h * v)) + EPS)
        c = cc * (1.0 / (h * v))
        for i in range(h):
            ov = o_ref[i]
            dyv = dy_ref[i]
            do_ref[i] = r * dyv * w_ref[i] - ov * (r * r * r * c)
            dw_ref[i] += jnp.sum(dyv * ov * r, axis=0, keepdims=True)

    blk = pl.BlockSpec((h, tr, v), lambda i: (0, i, 0))
    wsp = pl.BlockSpec((h, 1, v), lambda i: (0, 0, 0))
    return pl.pallas_call(
        body, name=name, grid=(s_ // tr,), in_specs=[blk, wsp, blk], out_specs=[blk, wsp],
        out_shape=[jax.ShapeDtypeStruct(o.shape, F32), jax.ShapeDtypeStruct((h, 1, v), F32)],
        compiler_params=_params("arbitrary"),
    )(o, w, dy)


def _loss_head(ffn, h1, target, w, name="loss_head"):
    r_, f = ffn.shape
    tr = _row_tile(r_)

    def body(ffn_ref, h1_ref, tg_ref, w_ref, loss_ref, dy_ref, dffn_ref, dw_ref):
        @pl.when(pl.program_id(0) == 0)
        def _():
            dw_ref[...] = jnp.zeros_like(dw_ref)
            loss_ref[...] = jnp.zeros_like(loss_ref)

        tv = ffn_ref[...]
        wv = w_ref[...]
        r = lax.rsqrt(jnp.mean(tv * tv, axis=-1, keepdims=True) + EPS)
        tn = tv * r
        e = h1_ref[...] + tn * wv - tg_ref[...]
        tot = jnp.sum(jnp.sum(e * e, axis=1, keepdims=True), axis=0, keepdims=True) * (0.5 / f)
        loss_ref[...] += tot + jnp.zeros_like(loss_ref)
        dyv = e * (1.0 / f)
        dy_ref[...] = dyv
        gw = dyv * wv
        c = jnp.mean(gw * tv, axis=-1, keepdims=True)
        dffn_ref[...] = (r * gw - tv * (r * r * r * c)).astype(dffn_ref.dtype)
        dw_ref[...] += jnp.sum(dyv * tn, axis=0, keepdims=True)

    row = pl.BlockSpec((tr, f), lambda i: (i, 0))
    wsp = pl.BlockSpec((1, f), lambda i: (0, 0))
    lsp = pl.BlockSpec((1, LANE), lambda i: (0, 0))
    return pl.pallas_call(
        body, name=name, grid=(r_ // tr,),
        in_specs=[row, row, row, wsp], out_specs=[lsp, row, row, wsp],
        out_shape=[jax.ShapeDtypeStruct((1, LANE), F32), jax.ShapeDtypeStruct((r_, f), F32),
                   jax.ShapeDtypeStruct((r_, f), MXU_DTYPE), jax.ShapeDtypeStruct((1, f), F32)],
        compiler_params=_params("arbitrary"),
    )(ffn, h1, target, w.reshape(1, f))


def _rot_matrix():
    p = np.zeros((ROPE, ROPE), np.float32)
    for i in range(HALF):
        p[i + HALF, i] = -1.0
        p[i, i + HALF] = 1.0
    return jnp.asarray(p, BF16)


def _rope_val(r, c2, s2, rot):
    return r * c2 + _exact_dot(r, rot, 1, 0, True) * s2


def _q_prep(q, cos2, sin2, scale, name):
    h, s_, _ = q.shape
    tr = _row_tile(s_)

    def body(q_ref, c_ref, s_ref, rot_ref, o_ref):
        x = q_ref[...]
        o_ref[:, :NOPE] = (x[:, :NOPE] * scale).astype(o_ref.dtype)
        o_ref[:, NOPE:] = (_rope_val(x[:, NOPE:], c_ref[...], s_ref[...], rot_ref[...]) * scale).astype(o_ref.dtype)

    blk = pl.BlockSpec((None, tr, QK), lambda hh, i: (hh, i, 0))
    csp = pl.BlockSpec((tr, ROPE), lambda hh, i: (i, 0))
    return pl.pallas_call(
        body, name=name, grid=(h, s_ // tr),
        in_specs=[blk, csp, csp, pl.BlockSpec((ROPE, ROPE), lambda hh, i: (0, 0))], out_specs=blk,
        out_shape=jax.ShapeDtypeStruct(q.shape, MXU_DTYPE), compiler_params=_params("parallel", "parallel"),
    )(q, cos2, sin2, _rot_matrix())


def _kv_prep(kv, small, cos2, sin2, name="kv_prep"):
    h, s_, _ = kv.shape
    tr = _row_tile(s_)

    def body(kv_ref, sm_ref, c_ref, s_ref, rot_ref, k_ref, v_ref):
        kr = _rope_val(sm_ref[:, :ROPE], c_ref[...], s_ref[...], rot_ref[...]).astype(k_ref.dtype)
        for i in range(h):
            k_ref[i, :, :NOPE] = kv_ref[i, :, :NOPE].astype(k_ref.dtype)
            k_ref[i, :, NOPE:] = kr
            v_ref[i] = kv_ref[i, :, NOPE:].astype(v_ref.dtype)

    csp = pl.BlockSpec((tr, ROPE), lambda i: (i, 0))
    return pl.pallas_call(
        body, name=name, grid=(s_ // tr,),
        in_specs=[pl.BlockSpec((h, tr, NOPE + VDIM), lambda i: (0, i, 0)), pl.BlockSpec((tr, LANE), lambda i: (i, 0)),
                  csp, csp, pl.BlockSpec((ROPE, ROPE), lambda i: (0, 0))],
        out_specs=[pl.BlockSpec((h, tr, QK), lambda i: (0, i, 0)), pl.BlockSpec((h, tr, VDIM), lambda i: (0, i, 0))],
        out_shape=[jax.ShapeDtypeStruct((h, s_, QK), MXU_DTYPE), jax.ShapeDtypeStruct((h, s_, VDIM), MXU_DTYPE)],
        compiler_params=_params("parallel"),
    )(kv, small, cos2, sin2, _rot_matrix())


def _dkv_post(dk, dv, ddt, cos2, nsin2, name="dkv_post"):
    h, s_, _ = dk.shape
    tr = _row_tile(s_)

    def body(dk_ref, dv_ref, ddt_ref, c_ref, s_ref, rot_ref, dkv_ref, dsm_ref):
        acc = dk_ref[0, :, NOPE:]
        for i in range(1, h):
            acc = acc + dk_ref[i, :, NOPE:]
        dsm_ref[:, :ROPE] = _rope_val(acc, c_ref[...], s_ref[...], rot_ref[...]).astype(dsm_ref.dtype)
        dsm_ref[:, ROPE:] = ddt_ref[:, ROPE:].astype(dsm_ref.dtype)
        for i in range(h):
            dkv_ref[i, :, :NOPE] = dk_ref[i, :, :NOPE].astype(dkv_ref.dtype)
            dkv_ref[i, :, NOPE:] = dv_ref[i].astype(dkv_ref.dtype)

    csp = pl.BlockSpec((tr, ROPE), lambda i: (i, 0))
    return pl.pallas_call(
        body, name=name, grid=(s_ // tr,),
        in_specs=[pl.BlockSpec((h, tr, QK), lambda i: (0, i, 0)), pl.BlockSpec((h, tr, VDIM), lambda i: (0, i, 0)),
                  pl.BlockSpec((tr, LANE), lambda i: (i, 0)), csp, csp, pl.BlockSpec((ROPE, ROPE), lambda i: (0, 0))],
        out_specs=[pl.BlockSpec((h, tr, NOPE + VDIM), lambda i: (0, i, 0)), pl.BlockSpec((tr, LANE), lambda i: (i, 0))],
        out_shape=[jax.ShapeDtypeStruct((h, s_, NOPE + VDIM), MXU_DTYPE), jax.ShapeDtypeStruct((s_, LANE), MXU_DTYPE)],
        compiler_params=_params("parallel"),
    )(dk, dv, ddt, cos2, nsin2, _rot_matrix())


def _attn_tile(s):
    return 512 if s % 1024 == 0 else s // 2


def _pairs(n, by_key):
    if by_key:
        pr = [(i, j) for j in range(n) for i in range(j, n)]
    else:
        pr = [(i, j) for i in range(n) for j in range(i + 1)]
    return (jnp.asarray([p[0] for p in pr], jnp.int32), jnp.asarray([p[1] for p in pr], jnp.int32))


def _diag_mask(t):
    return lax.broadcasted_iota(jnp.int32, (t, t), 1) <= lax.broadcasted_iota(jnp.int32, (t, t), 0)


def _flash_specs(t, dk, dv):
    qsp = pl.BlockSpec((None, t, dk), lambda hh, p, qi, kj: (hh, qi[p], 0))
    ksp = pl.BlockSpec((None, t, dk), lambda hh, p, qi, kj: (hh, kj[p], 0))
    vsp = pl.BlockSpec((None, t, dv), lambda hh, p, qi, kj: (hh, kj[p], 0))
    osp = pl.BlockSpec((None, t, dv), lambda hh, p, qi, kj: (hh, qi[p], 0))
    lsp = pl.BlockSpec((None, t, 1), lambda hh, p, qi, kj: (hh, qi[p], 0))
    return qsp, ksp, vsp, osp, lsp


def _flash_fwd(q, k, v, name="flash_fwd"):
    h, s_, dk = q.shape
    dv = v.shape[-1]
    t = _attn_tile(s_)
    n = s_ // t
    qi, kj = _pairs(n, False)

    def body(qi_ref, kj_ref, q_ref, k_ref, v_ref, o_ref, lse_ref, m_s, l_s, acc):
        p_ = pl.program_id(1)
        i, j = qi_ref[p_], kj_ref[p_]

        @pl.when(j == 0)
        def _():
            m_s[...] = jnp.full_like(m_s, -jnp.inf)
            l_s[...] = jnp.zeros_like(l_s)
            acc[...] = jnp.zeros_like(acc)

        def update(sc):
            m_new = jnp.maximum(m_s[...], jnp.max(sc, axis=1, keepdims=True))
            alpha = jnp.exp(m_s[...] - m_new)
            p = jnp.exp(sc - m_new)
            l_s[...] = alpha * l_s[...] + jnp.sum(p, axis=1, keepdims=True)
            acc[...] = alpha * acc[...] + _dot(_mx(p), v_ref[...], 1, 0)
            m_s[...] = m_new

        @pl.when(j < i)
        def _():
            update(_dot(q_ref[...], k_ref[...], 1, 1))

        @pl.when(j == i)
        def _():
            update(jnp.where(_diag_mask(t), _dot(q_ref[...], k_ref[...], 1, 1), -jnp.inf))
            o_ref[...] = acc[...] / l_s[...]
            lse_ref[...] = m_s[...] + jnp.log(l_s[...])

    qsp, ksp, vsp, osp, lsp = _flash_specs(t, dk, dv)
    gs = pltpu.PrefetchScalarGridSpec(
        num_scalar_prefetch=2, grid=(h, qi.shape[0]), in_specs=[qsp, ksp, vsp], out_specs=[osp, lsp],
        scratch_shapes=[pltpu.VMEM((t, 1), F32), pltpu.VMEM((t, 1), F32), pltpu.VMEM((t, dv), F32)])
    return pl.pallas_call(
        body, name=name, grid_spec=gs,
        out_shape=[jax.ShapeDtypeStruct((h, s_, dv), F32), jax.ShapeDtypeStruct((h, s_, 1), F32)],
        compiler_params=_params("parallel", "arbitrary"),
    )(qi, kj, q, k, v)


def _flash_bwd_dq(q, k, v, o, do, lse, name="flash_bwd_dq"):
    h, s_, dk = q.shape
    dv = v.shape[-1]
    t = _attn_tile(s_)
    n = s_ // t
    qi, kj = _pairs(n, False)

    def body(qi_ref, kj_ref, q_ref, k_ref, v_ref, o_ref, do_ref, lse_ref, dq_ref, delta_ref, acc, delta_s):
        p_ = pl.program_id(1)
        i, j = qi_ref[p_], kj_ref[p_]

        @pl.when(j == 0)
        def _():
            delta_s[...] = jnp.sum(do_ref[...] * o_ref[...], axis=1, keepdims=True)
            acc[...] = jnp.zeros_like(acc)

        def update(sc):
            p = jnp.exp(sc - lse_ref[...])
            dp = _dot(_mx(do_ref[...]), v_ref[...], 1, 1)
            ds = p * (dp - delta_s[...])
            acc[...] += _dot(_mx(ds), k_ref[...], 1, 0)

        @pl.when(j < i)
        def _():
            update(_dot(q_ref[...], k_ref[...], 1, 1))

        @pl.when(j == i)
        def _():
            update(jnp.where(_diag_mask(t), _dot(q_ref[...], k_ref[...], 1, 1), -jnp.inf))
            dq_ref[...] = acc[...]
            delta_ref[...] = delta_s[...]

    qsp, ksp, vsp, osp, lsp = _flash_specs(t, dk, dv)
    gs = pltpu.PrefetchScalarGridSpec(
        num_scalar_prefetch=2, grid=(h, qi.shape[0]), in_specs=[qsp, ksp, vsp, osp, osp, lsp], out_specs=[qsp, lsp],
        scratch_shapes=[pltpu.VMEM((t, dk), F32), pltpu.VMEM((t, 1), F32)])
    return pl.pallas_call(
        body, name=name, grid_spec=gs,
        out_shape=[jax.ShapeDtypeStruct((h, s_, dk), F32), jax.ShapeDtypeStruct((h, s_, 1), F32)],
        compiler_params=_params("parallel", "arbitrary"),
    )(qi, kj, q, k, v, o, do, lse)


def _flash_bwd_dkv(q, k, v, do, lse, delta, name="flash_bwd_dkv"):
    h, s_, dk = q.shape
    dv = v.shape[-1]
    t = _attn_tile(s_)
    n = s_ // t
    qi, kj = _pairs(n, True)

    def body(qi_ref, kj_ref, q_ref, k_ref, v_ref, do_ref, lse_ref, delta_ref, dk_ref, dv_ref, dk_acc, dv_acc):
        p_ = pl.program_id(1)
        i, j = qi_ref[p_], kj_ref[p_]

        def update(sc):
            p = jnp.exp(sc - lse_ref[...])
            dob = _mx(do_ref[...])
            dv_acc[...] += _dot(_mx(p), dob, 0, 0)
            dp = _dot(dob, v_ref[...], 1, 1)
            ds = p * (dp - delta_ref[...])
            dk_acc[...] += _dot(_mx(ds), q_ref[...], 0, 0)

        @pl.when(i == j)
        def _():
            dk_acc[...] = jnp.zeros_like(dk_acc)
            dv_acc[...] = jnp.zeros_like(dv_acc)
            update(jnp.where(_diag_mask(t), _dot(q_ref[...], k_ref[...], 1, 1), -jnp.inf))

        @pl.when(i > j)
        def _():
            update(_dot(q_ref[...], k_ref[...], 1, 1))

        @pl.when(i == n - 1)
        def _():
            dk_ref[...] = dk_acc[...]
            dv_ref[...] = dv_acc[...]

    qsp, ksp, vsp, osp, lsp = _flash_specs(t, dk, dv)
    gs = pltpu.PrefetchScalarGridSpec(
        num_scalar_prefetch=2, grid=(h, qi.shape[0]), in_specs=[qsp, ksp, vsp, osp, lsp, lsp], out_specs=[ksp, vsp],
        scratch_shapes=[pltpu.VMEM((t, dk), F32), pltpu.VMEM((t, dv), F32)])
    return pl.pallas_call(
        body, name=name, grid_spec=gs,
        out_shape=[jax.ShapeDtypeStruct((h, s_, dk), F32), jax.ShapeDtypeStruct((h, s_, dv), F32)],
        compiler_params=_params("parallel", "arbitrary"),
    )(qi, kj, q, k, v, do, lse, delta)


HALO = 8


def _conv_specs(s_, c, tr, after):
    main = pl.BlockSpec((tr, c), lambda i: (i, 0))
    per = tr // HALO
    if after:
        halo = pl.BlockSpec((HALO, c), lambda i: (jnp.minimum((i + 1) * per, s_ // HALO - 1), 0))
    else:
        halo = pl.BlockSpec((HALO, c), lambda i: (jnp.maximum(i * per - 1, 0), 0))
    return main, halo


def _fill_before(ext, t_ref, h_ref, tr):
    ext[0:HALO, :] = jnp.where(pl.program_id(0) > 0, h_ref[...], 0.0)
    ext[HALO:HALO + tr, :] = t_ref[...]


def _taps(ext, w_ref, tr):
    base = HALO - (CONV_K - 1)
    acc = ext[base:base + tr, :] * w_ref[0:1, :]
    for k in range(1, CONV_K):
        acc = acc + ext[base + k:base + k + tr, :] * w_ref[k:k + 1, :]
    return acc


def _conv_fwd(t, w, b, name="conv_fwd"):
    s_, c = t.shape
    tr = _row_tile(s_)

    def body(t_ref, h_ref, w_ref, b_ref, o_ref, ext):
        _fill_before(ext, t_ref, h_ref, tr)
        o_ref[...] = _silu(_taps(ext, w_ref, tr) + b_ref[...])

    main, halo = _conv_specs(s_, c, tr, False)
    return pl.pallas_call(
        body, name=name, grid=(s_ // tr,),
        in_specs=[main, halo, pl.BlockSpec((CONV_K, c), lambda i: (0, 0)), pl.BlockSpec((1, c), lambda i: (0, 0))],
        out_specs=main, out_shape=jax.ShapeDtypeStruct((s_, c), F32),
        scratch_shapes=[pltpu.VMEM((tr + HALO, c), F32)], compiler_params=_params("parallel"),
    )(t, t, w, b)


def _conv_bwd_pre(t, w, b, dact, name="conv_bwd_pre"):
    s_, c = t.shape
    tr = _row_tile(s_)

    def body(t_ref, h_ref, w_ref, b_ref, da_ref, dpre_ref, dwb_ref, ext):
        @pl.when(pl.program_id(0) == 0)
        def _():
            dwb_ref[...] = jnp.zeros_like(dwb_ref)

        _fill_before(ext, t_ref, h_ref, tr)
        dpre = da_ref[...] * _dsilu(_taps(ext, w_ref, tr) + b_ref[...])
        dpre_ref[...] = dpre
        base = HALO - (CONV_K - 1)
        for k in range(CONV_K):
            dwb_ref[k:k + 1, :] += jnp.sum(dpre * ext[base + k:base + k + tr, :], axis=0, keepdims=True)
        dwb_ref[CONV_K:CONV_K + 1, :] += jnp.sum(dpre, axis=0, keepdims=True)

    main, halo = _conv_specs(s_, c, tr, False)
    return pl.pallas_call(
        body, name=name, grid=(s_ // tr,),
        in_specs=[main, halo, pl.BlockSpec((CONV_K, c), lambda i: (0, 0)), pl.BlockSpec((1, c), lambda i: (0, 0)), main],
        out_specs=[main, pl.BlockSpec((8, c), lambda i: (0, 0))],
        out_shape=[jax.ShapeDtypeStruct((s_, c), F32), jax.ShapeDtypeStruct((8, c), F32)],
        scratch_shapes=[pltpu.VMEM((tr + HALO, c), F32)], compiler_params=_params("arbitrary"),
    )(t, t, w, b, dact)


def _conv_bwd_in(dpre, w, name="conv_bwd_in"):
    s_, c = dpre.shape
    tr = _row_tile(s_)
    nt = s_ // tr

    def body(d_ref, h_ref, w_ref, o_ref, ext):
        ext[0:tr, :] = d_ref[...]
        ext[tr:tr + HALO, :] = jnp.where(pl.program_id(0) < nt - 1, h_ref[...], 0.0)
        acc = ext[CONV_K - 1:CONV_K - 1 + tr, :] * w_ref[0:1, :]
        for k in range(1, CONV_K):
            acc = acc + ext[CONV_K - 1 - k:CONV_K - 1 - k + tr, :] * w_ref[k:k + 1, :]
        o_ref[...] = acc.astype(o_ref.dtype)

    main, halo = _conv_specs(s_, c, tr, True)
    return pl.pallas_call(
        body, name=name, grid=(nt,),
        in_specs=[main, halo, pl.BlockSpec((CONV_K, c), lambda i: (0, 0))],
        out_specs=main, out_shape=jax.ShapeDtypeStruct((s_, c), MXU_DTYPE),
        scratch_shapes=[pltpu.VMEM((tr + HALO, c), F32)], compiler_params=_params("parallel"),
    )(dpre, dpre, w)


def _ssd_chunk_common(dt_ref, dtt_ref, br_ref, bc_ref, ar_ref, ac_ref):
    li = lax.broadcasted_iota(jnp.int32, (CHUNK, CHUNK), 0)
    si = lax.broadcasted_iota(jnp.int32, (CHUNK, CHUNK), 1)
    lower = li >= si
    lower_b = lower.astype(BF16)
    upper_b = (li <= si).astype(BF16)
    zr = dt_ref[...] + br_ref[...]
    dtc = _softplus(zr)
    a_row = -jnp.exp(ar_ref[...])
    acum = _exact_dot(lower_b, dtc * a_row, 1, 0, False)
    dtt = _softplus(dtt_ref[...] + bc_ref[...])
    acum_t = _exact_dot(dtt * (-jnp.exp(ac_ref[...])), upper_b, 1, 0, True)
    return lower, upper_b, zr, dtc, a_row, acum, acum_t


def _head_terms(h, lower, dtc, acum, acum_t):
    lane = lax.broadcasted_iota(jnp.int32, (1, LANE), 1)
    sub = lax.broadcasted_iota(jnp.int32, (SSD_H, 1), 0)
    rowid = lax.broadcasted_iota(jnp.int32, (CHUNK, 1), 0)
    oh = (lane == HEAD_LANE + h).astype(F32)
    acol = jnp.sum(acum * oh, axis=1, keepdims=True)
    dcol = jnp.sum(dtc * oh, axis=1, keepdims=True)
    arow = jnp.sum(acum_t * (sub == h).astype(F32), axis=0, keepdims=True)
    alast = jnp.sum(jnp.where(rowid == CHUNK - 1, acol, 0.0), axis=0, keepdims=True)
    decay = jnp.exp(jnp.where(lower, acol - arow, -jnp.inf))
    return oh, acol, dcol, alast, decay


def _hs(h):
    return slice(h * SSD_P, (h + 1) * SSD_P)


def _gs(off, g):
    return slice(off + g * SSD_N, off + (g + 1) * SSD_N)


def _ssd_in_specs(rev):
    def ci(c):
        return c if rev is None else rev - c
    return [pl.BlockSpec((CHUNK, CONV_DIM), lambda c: (ci(c), 0)),
            pl.BlockSpec((CHUNK, LANE), lambda c: (ci(c), 0)),
            pl.BlockSpec((SSD_H, CHUNK), lambda c: (0, ci(c))),
            pl.BlockSpec((1, LANE), lambda c: (0, 0)), pl.BlockSpec((SSD_H, 1), lambda c: (0, 0)),
            pl.BlockSpec((1, LANE), lambda c: (0, 0)), pl.BlockSpec((SSD_H, 1), lambda c: (0, 0)),
            pl.BlockSpec((SSD_H, 1, SSD_P), lambda c: (0, 0, 0))]


def _ssd_fwd(xbc, small, dtt, bias_r, bias_c, alog_r, alog_c, dsk, name="ssd_fwd"):
    s_ = xbc.shape[0]
    nc = s_ // CHUNK

    def body(x_ref, dt_ref, dtt_ref, br_ref, bc_ref, ar_ref, ac_ref, dsk_ref, y_ref, prev_ref, state):
        @pl.when(pl.program_id(0) == 0)
        def _():
            state[...] = jnp.zeros_like(state)

        lower, _, _, dtc, _, acum, acum_t = _ssd_chunk_common(dt_ref, dtt_ref, br_ref, bc_ref, ar_ref, ac_ref)
        for g in range(SSD_G):
            bb = _mx(x_ref[:, _gs(B_OFF, g)])
            cb_ = _mx(x_ref[:, _gs(C_OFF, g)])
            cbm = _dot(cb_, bb, 1, 1)
            for e in range(SSD_E):
                h = g * SSD_E + e
                _, acol, dcol, alast, decay = _head_terms(h, lower, dtc, acum, acum_t)
                x = x_ref[:, _hs(h)]
                xdt = x * dcol
                yd = _dot(_mx(cbm * decay), _mx(xdt), 1, 0)
                prev = state[h]
                prev_ref[0, h] = prev
                yo = _dot(cb_, _mx(prev), 1, 1) * jnp.exp(acol)
                ds = jnp.exp(alast - acol)
                st = _dot(_mx(xdt * ds), bb, 0, 0)
                state[h] = prev * jnp.exp(alast) + st
                y_ref[:, _hs(h)] = yd + yo + x * dsk_ref[h]

    psp = pl.BlockSpec((1, SSD_H, SSD_P, SSD_N), lambda c: (c, 0, 0, 0))
    return pl.pallas_call(
        body, name=name, grid=(nc,),
        in_specs=_ssd_in_specs(None), out_specs=[pl.BlockSpec((CHUNK, SSD_W), lambda c: (c, 0)), psp],
        out_shape=[jax.ShapeDtypeStruct((s_, SSD_W), F32),
                   jax.ShapeDtypeStruct((nc, SSD_H, SSD_P, SSD_N), F32)],
        scratch_shapes=[pltpu.VMEM((SSD_H, SSD_P, SSD_N), F32)],
        compiler_params=_params("arbitrary"),
    )(xbc, small, dtt, bias_r, bias_c, alog_r, alog_c, dsk)


def _ssd_bwd(xbc, small, dtt, bias_r, bias_c, alog_r, alog_c, dsk, prev, dy, name="ssd_bwd"):
    s_ = xbc.shape[0]
    nc = s_ // CHUNK

    def body(x_ref, dt_ref, dtt_ref, br_ref, bc_ref, ar_ref, ac_ref, dsk_ref, prev_ref, dy_ref,
             dx_ref, ddt_ref, dpar_ref, dstate):
        @pl.when(pl.program_id(0) == 0)
        def _():
            dstate[...] = jnp.zeros_like(dstate)
            dpar_ref[...] = jnp.zeros_like(dpar_ref)

        lower, upper_b, zr, dtc, a_row, acum, acum_t = _ssd_chunk_common(
            dt_ref, dtt_ref, br_ref, bc_ref, ar_ref, ac_ref)
        strict = (lax.broadcasted_iota(jnp.int32, (CHUNK, CHUNK), 1)
                  < lax.broadcasted_iota(jnp.int32, (CHUNK, CHUNK), 0))
        strict_b = strict.astype(BF16)
        da_in = jnp.zeros((CHUNK, LANE), F32)
        r_off = jnp.zeros((CHUNK, LANE), F32)
        c_int = jnp.zeros((CHUNK, LANE), F32)
        c_row = jnp.zeros((1, LANE), F32)
        ddt = jnp.zeros((CHUNK, LANE), F32)
        dskip = jnp.zeros((1, LANE), F32)
        for g in range(SSD_G):
            bb = _mx(x_ref[:, _gs(B_OFF, g)])
            cb_ = _mx(x_ref[:, _gs(C_OFF, g)])
            cbm = _dot(cb_, bb, 1, 1)
            dcb = jnp.zeros((CHUNK, CHUNK), F32)
            dc_acc = jnp.zeros((CHUNK, SSD_N), F32)
            db_acc = jnp.zeros((CHUNK, SSD_N), F32)
            for e in range(SSD_E):
                h = g * SSD_E + e
                oh, acol, dcol, alast, decay = _head_terms(h, lower, dtc, acum, acum_t)
                x = x_ref[:, _hs(h)]
                dy = dy_ref[:, _hs(h)]
                xdt = x * dcol
                eacol = jnp.exp(acol)
                ds = jnp.exp(alast - acol)
                dyb = _mx(dy)
                dsh = dstate[h]
                dshb = _mx(dsh)
                prev = prev_ref[0, h]
                prevb = _mx(prev)
                dxdt_inter = ds * _dot(bb, dshb, 1, 1)
                dxdt = _dot(_mx(cbm * decay), dyb, 0, 0) + dxdt_inter
                dwl = _dot(dyb, _mx(xdt), 1, 1) * decay
                dcb = dcb + dwl
                dc_acc = dc_acc + eacol * _dot(dyb, prevb, 1, 0)
                db_acc = db_acc + _dot(_mx(xdt * ds), dshb, 1, 0)
                dstate[h] = _dot(_mx(dy * eacol), cb_, 0, 0) + jnp.exp(alast) * dsh
                above = _exact_dot(upper_b, dwl * cbm, 1, 0, False)
                da_in = da_in + jnp.sum(jnp.where(strict, above, 0.0), axis=1, keepdims=True) * oh
                y_off = _dot(cb_, prevb, 1, 1) * eacol
                r_off = r_off + jnp.sum(dy * y_off, axis=1, keepdims=True) * oh
                c_int = c_int + jnp.sum(xdt * dxdt_inter, axis=1, keepdims=True) * oh
                both = jnp.sum(jnp.sum(dsh * prev, axis=1, keepdims=True), axis=0, keepdims=True)
                c_row = c_row + jnp.exp(alast) * both * oh
                dk = dsk_ref[h]
                ddt = ddt + jnp.sum(dxdt * x, axis=1, keepdims=True) * oh
                dx_ref[:, _hs(h)] = dxdt * dcol + dy * dk
                dskip = dskip + jnp.sum(jnp.sum(dy * x, axis=1, keepdims=True), axis=0, keepdims=True) * oh
            dcbb = _mx(dcb)
            dx_ref[:, _gs(C_OFF, g)] = dc_acc + _dot(dcbb, bb, 1, 0)
            dx_ref[:, _gs(B_OFF, g)] = db_acc + _dot(dcbb, cb_, 0, 0)
        da = (da_in + _exact_dot(upper_b, r_off, 1, 0, False) + _exact_dot(strict_b, c_int, 1, 0, False) + c_row)
        draw = (ddt + da * a_row) * _sigmoid(zr)
        ddt_ref[...] = draw
        dpar_ref[0:1, :] += jnp.sum(draw, axis=0, keepdims=True)
        dpar_ref[1:2, :] += jnp.sum(da * dtc, axis=0, keepdims=True) * a_row
        dpar_ref[2:3, :] += dskip

    rev = nc - 1
    psp = pl.BlockSpec((1, SSD_H, SSD_P, SSD_N), lambda c: (rev - c, 0, 0, 0))
    return pl.pallas_call(
        body, name=name, grid=(nc,),
        in_specs=_ssd_in_specs(rev) + [psp, pl.BlockSpec((CHUNK, SSD_W), lambda c: (rev - c, 0))],
        out_specs=[pl.BlockSpec((CHUNK, CONV_DIM), lambda c: (rev - c, 0)),
                   pl.BlockSpec((CHUNK, LANE), lambda c: (rev - c, 0)), pl.BlockSpec((8, LANE), lambda c: (0, 0))],
        out_shape=[jax.ShapeDtypeStruct((s_, CONV_DIM), F32), jax.ShapeDtypeStruct((s_, LANE), F32),
                   jax.ShapeDtypeStruct((8, LANE), F32)],
        scratch_shapes=[pltpu.VMEM((SSD_H, SSD_P, SSD_N), F32)],
        compiler_params=_params("arbitrary"),
    )(xbc, small, dtt, bias_r, bias_c, alog_r, alog_c, dsk, prev, dy)


GN = SSD_W // SSD_G


def _gated_norm_fwd(y, z, w, name="gated_norm_fwd"):
    s_, f = y.shape
    tr = _row_tile(s_)

    def body(y_ref, z_ref, w_ref, o_ref):
        for g in range(SSD_G):
            sl = slice(g * GN, (g + 1) * GN)
            gg = y_ref[:, sl] * _silu(z_ref[:, sl])
            r = lax.rsqrt(jnp.mean(gg * gg, axis=-1, keepdims=True) + EPS)
            o_ref[:, sl] = (gg * r * w_ref[:, sl]).astype(o_ref.dtype)

    row = pl.BlockSpec((tr, f), lambda i: (i, 0))
    wsp = pl.BlockSpec((1, f), lambda i: (0, 0))
    return pl.pallas_call(
        body, name=name, grid=(s_ // tr,), in_specs=[row, row, wsp], out_specs=row,
        out_shape=jax.ShapeDtypeStruct((s_, f), MXU_DTYPE), compiler_params=_params("parallel"),
    )(y, z, w.reshape(1, f))


def _gated_norm_bwd(y, z, w, dout, name="gated_norm_bwd"):
    s_, f = y.shape
    tr = _row_tile(s_)

    def body(y_ref, z_ref, w_ref, do_ref, dy_ref, dz_ref, dw_ref):
        @pl.when(pl.program_id(0) == 0)
        def _():
            dw_ref[...] = jnp.zeros_like(dw_ref)

        for g in range(SSD_G):
            sl = slice(g * GN, (g + 1) * GN)
            yv = y_ref[:, sl]
            zv = z_ref[:, sl]
            dov = do_ref[:, sl].astype(F32)
            sz = _silu(zv)
            gg = yv * sz
            r = lax.rsqrt(jnp.mean(gg * gg, axis=-1, keepdims=True) + EPS)
            gw = dov * w_ref[:, sl]
            c = jnp.mean(gw * gg, axis=-1, keepdims=True)
            dgg = r * gw - gg * (r * r * r * c)
            dy_ref[:, sl] = dgg * sz
            dz_ref[:, sl] = (dgg * yv * _dsilu(zv)).astype(dz_ref.dtype)
            dw_ref[:, sl] += jnp.sum(dov * gg * r, axis=0, keepdims=True)

    row = pl.BlockSpec((tr, f), lambda i: (i, 0))
    wsp = pl.BlockSpec((1, f), lambda i: (0, 0))
    return pl.pallas_call(
        body, name=name, grid=(s_ // tr,), in_specs=[row, row, wsp, row], out_specs=[row, row, wsp],
        out_shape=[jax.ShapeDtypeStruct((s_, f), F32), jax.ShapeDtypeStruct((s_, f), MXU_DTYPE),
                   jax.ShapeDtypeStruct((1, f), F32)],
        compiler_params=_params("arbitrary"),
    )(y, z, w.reshape(1, f), dout)


def _swiglu_fwd(g, u, name="swiglu_fwd"):
    nb, s_, f = g.shape
    tr = _row_tile(s_)

    def body(g_ref, u_ref, o_ref):
        o_ref[...] = (_silu(g_ref[...]) * u_ref[...]).astype(o_ref.dtype)

    sp = pl.BlockSpec((None, tr, f), lambda j, i: (j, i, 0))
    return pl.pallas_call(
        body, name=name, grid=(nb, s_ // tr), in_specs=[sp, sp], out_specs=sp,
        out_shape=jax.ShapeDtypeStruct(g.shape, MXU_DTYPE), compiler_params=_params("parallel", "parallel"),
    )(g, u)


def _swiglu_bwd(g, u, da, name="swiglu_bwd"):
    nb, s_, f = g.shape
    tr = _row_tile(s_)

    def body(g_ref, u_ref, da_ref, dg_ref, du_ref):
        gv = g_ref[...]
        dav = da_ref[...]
        dg_ref[...] = (dav * u_ref[...] * _dsilu(gv)).astype(dg_ref.dtype)
        du_ref[...] = (dav * _silu(gv)).astype(du_ref.dtype)

    sp = pl.BlockSpec((None, tr, f), lambda j, i: (j, i, 0))
    return pl.pallas_call(
        body, name=name, grid=(nb, s_ // tr), in_specs=[sp, sp, sp], out_specs=[sp, sp],
        out_shape=[jax.ShapeDtypeStruct(g.shape, MXU_DTYPE)] * 2, compiler_params=_params("parallel", "parallel"),
    )(g, u, da)


def _adam_math(g, w, m, v):
    m2 = ADAM_B1 * m + (1.0 - ADAM_B1) * g
    v2 = ADAM_B2 * v + (1.0 - ADAM_B2) * (g * g)
    m_hat = m2 / (1.0 - ADAM_B1 ** ADAM_STEP)
    v_hat = v2 / (1.0 - ADAM_B2 ** ADAM_STEP)
    delta = -ADAM_LR * (m_hat / (jnp.sqrt(v_hat) + ADAM_EPS) + ADAM_WD * w)
    return delta, m2, v2


def _adamw(parts, w, m, v, name="adamw"):
    nd, r_, c = parts.shape
    tr = _pick(r_, (128, 64, 32, 16, 8))

    def body(p_ref, w_ref, m_ref, v_ref, g_ref, d_ref, m2_ref, v2_ref):
        g = p_ref[0].astype(F32)
        for i in range(1, nd):
            g = g + p_ref[i].astype(F32)
        delta, m2, v2 = _adam_math(g, w_ref[...], m_ref[...], v_ref[...])
        g_ref[...] = g
        d_ref[...] = delta
        m2_ref[...] = m2
        v2_ref[...] = v2

    row = pl.BlockSpec((tr, c), lambda i: (i, 0))
    psp = pl.BlockSpec((nd, tr, c), lambda i: (0, i, 0))
    return pl.pallas_call(
        body, name=name, grid=(r_ // tr,), in_specs=[psp, row, row, row], out_specs=[row] * 4,
        out_shape=[jax.ShapeDtypeStruct((r_, c), F32)] * 4, compiler_params=_params("parallel"),
    )(parts, w, m, v)


def _adamw_small(parts, w, m, v, name="adamw_small"):
    nd = parts.shape[0]

    def body(p_ref, w_ref, m_ref, v_ref, g_ref, d_ref, m2_ref, v2_ref):
        g = p_ref[0]
        for i in range(1, nd):
            g = g + p_ref[i]
        delta, m2, v2 = _adam_math(g, w_ref[...], m_ref[...], v_ref[...])
        g_ref[...] = g
        d_ref[...] = delta
        m2_ref[...] = m2
        v2_ref[...] = v2

    return pl.pallas_call(
        body, name=name, out_shape=[jax.ShapeDtypeStruct(w.shape, F32)] * 4,
        compiler_params=pltpu.CompilerParams(vmem_limit_bytes=VMEM_LIMIT_BYTES),
    )(parts, w, m, v)


_HBM = pl.BlockSpec(memory_space=pltpu.HBM)
_MESH = pl.DeviceIdType.MESH


def _all_gather(xs, name):
    na = len(xs)

    def body(*refs):
        x_refs, out_refs = refs[:na], refs[na:2 * na]
        send_sems, recv_sems, local_sems = refs[2 * na:]
        x, y, c = lax.axis_index("x"), lax.axis_index("y"), lax.axis_index("c")
        me, sibling = (x, y, c), (x, y, 1 - c)
        chips = [(1 - x, y), (x, 1 - y), (1 - x, 1 - y)]

        def slot(a, px, py, pc):
            return out_refs[a].at[4 * px + 2 * py + pc]

        def copy(a, k, block, to, src=None):
            return pltpu.make_async_remote_copy(
                src_ref=slot(a, *block) if src is None else src, dst_ref=slot(a, *block),
                send_sem=send_sems.at[a, k], recv_sem=recv_sems.at[a, k], device_id=to, device_id_type=_MESH)

        mine = [pltpu.make_async_copy(x_refs[a], slot(a, *me), local_sems.at[a]) for a in range(na)]
        started = []
        for a in range(na):
            mine[a].start()
            first = [copy(a, 0, me, sibling, src=x_refs[a])]
            first += [copy(a, 1 + j, me, (*chip, c), src=x_refs[a]) for j, chip in enumerate(chips)]
            for cp in first:
                cp.start()
            started += first
        for a in range(na):
            for j, chip in enumerate(chips):
                copy(a, 1 + j, (*chip, c), me).wait_recv()
                fwd = copy(a, 4 + j, (*chip, c), sibling)
                fwd.start()
                started.append(fwd)
        for a in range(na):
            copy(a, 0, sibling, me).wait_recv()
            for j, chip in enumerate(chips):
                copy(a, 4 + j, (*chip, 1 - c), me).wait_recv()
        for cp in started:
            cp.wait_send()
        for cp in mine:
            cp.wait()

    return pl.pallas_call(
        body, name=name, out_shape=[jax.ShapeDtypeStruct((N_DEV,) + t.shape, t.dtype) for t in xs],
        in_specs=[_HBM] * na, out_specs=[_HBM] * na,
        scratch_shapes=[pltpu.SemaphoreType.DMA((na, 7)), pltpu.SemaphoreType.DMA((na, 7)),
                        pltpu.SemaphoreType.DMA((na,))],
    )(*xs)


def _all_to_all(srcs, name):
    na = len(srcs)

    def body(*refs):
        src_refs, out_refs = refs[:na], refs[na:2 * na]
        send_sems, recv_sems, local_sems = refs[2 * na:]
        x, y, c = lax.axis_index("x"), lax.axis_index("y"), lax.axis_index("c")
        me = 4 * x + 2 * y + c
        mine = [pltpu.make_async_copy(src_refs[a].at[me], out_refs[a].at[me], local_sems.at[a]) for a in range(na)]
        copies = []
        for a in range(na):
            mine[a].start()
            for k in range(1, N_DEV):
                px = 1 - x if k & 4 else x
                py = 1 - y if k & 2 else y
                pc = 1 - c if k & 1 else c
                peer = 4 * px + 2 * py + pc
                send = pltpu.make_async_remote_copy(
                    src_ref=src_refs[a].at[peer], dst_ref=out_refs[a].at[me], send_sem=send_sems.at[a, k - 1],
                    recv_sem=recv_sems.at[a, k - 1], device_id=(px, py, pc), device_id_type=_MESH)
                recv = pltpu.make_async_remote_copy(
                    src_ref=src_refs[a].at[peer], dst_ref=out_refs[a].at[peer], send_sem=send_sems.at[a, k - 1],
                    recv_sem=recv_sems.at[a, k - 1], device_id=(px, py, pc), device_id_type=_MESH)
                send.start()
                copies.append((send, recv))
        for send, recv in copies:
            recv.wait_recv()
        for send, recv in copies:
            send.wait_send()
        for cp in mine:
            cp.wait()

    return pl.pallas_call(
        body, name=name, out_shape=[jax.ShapeDtypeStruct(t.shape, t.dtype) for t in srcs],
        in_specs=[_HBM] * na, out_specs=[_HBM] * na,
        scratch_shapes=[pltpu.SemaphoreType.DMA((na, 7)), pltpu.SemaphoreType.DMA((na, 7)),
                        pltpu.SemaphoreType.DMA((na,))],
    )(*srcs)


_BIG = (("w_in", D_MODEL, D_IN, 1), ("w_uq", Q_RANK, HEADS * QK, 1), ("w_ukv", KV_RANK, HEADS * (NOPE + VDIM), 1),
        ("w_out", D_MODEL, D_MODEL, 0), ("w_gate", D_MODEL, D_FF, 1), ("w_up", D_MODEL, D_FF, 1),
        ("w_down", D_FF, D_MODEL, 0))
_CQKV = (0, Q_RANK + KV_RANK)
_KR = (_CQKV[1], _CQKV[1] + ROPE)
_Z = (_KR[1], _KR[1] + SSD_W)
_XBC = (_Z[1], _Z[1] + CONV_DIM)
_DT = (_XBC[1], _XBC[1] + SSD_H)


def _win_segments(w_in_g):
    w = jnp.transpose(w_in_g, (1, 0, 2)).reshape(D_MODEL, D_IN)
    small = jnp.concatenate([w[:, _KR[0]:_KR[1]], w[:, _DT[0]:_DT[1]],
                             jnp.zeros((D_MODEL, LANE - ROPE - SSD_H), w.dtype)], axis=1)
    return w[:, _CQKV[0]:_CQKV[1]], w[:, _Z[0]:_Z[1]], w[:, _XBC[0]:_XBC[1]], small


def _win_from_segments(g_cqkv, g_z, g_xbc, g_small):
    w = jnp.concatenate([g_cqkv, g_small[:, :ROPE], g_z, g_xbc, g_small[:, ROPE:ROPE + SSD_H]], axis=1)
    return jnp.transpose(w.reshape(D_MODEL, N_DEV, D_IN // N_DEV), (1, 0, 2))


_SMALL = (("q_norm_w", 512), ("kv_norm_w", 512), ("conv_b", CONV_DIM), ("dt_bias", SSD_H), ("a_log", SSD_H),
          ("d_skip", SSD_H), ("ssd_norm_w", SSD_W), ("attn_out_norm_w", 1024), ("pre_mix_norm_w", D_MODEL),
          ("post_mix_norm_w", D_MODEL), ("pre_ffn_norm_w", D_MODEL), ("post_ffn_norm_w", D_MODEL),
          ("conv_w", CONV_K * CONV_DIM))
_SMALL_ROWS = -(-sum(-(-n // LANE) for _, n in _SMALL) // 8) * 8


def _pack_small(vals):
    rows = []
    for name, n in _SMALL:
        v = vals[name].reshape(-1).astype(F32)
        pad = -(-n // LANE) * LANE
        rows.append(jnp.pad(v, (0, pad - n)).reshape(-1, LANE))
    m = jnp.concatenate(rows, axis=0)
    return jnp.pad(m, ((0, _SMALL_ROWS - m.shape[0]), (0, 0)))


def _unpack_small(m):
    out, r = {}, 0
    for name, n in _SMALL:
        nr = -(-n // LANE)
        out[name] = m[r:r + nr].reshape(-1)[:n]
        r += nr
    return out


def _head_row(v):
    return jnp.pad(v.reshape(1, -1).astype(F32), ((0, 0), (HEAD_LANE, LANE - HEAD_LANE - v.shape[-1])))


def _local_step(x, positions, target, wg, small):
    w_cqkv, w_z, w_xbc, w_small = _win_segments(wg["w_in"])
    w_uq, w_ukv, w_gate, w_up, w_down = wg["w_uq"], wg["w_ukv"], wg["w_gate"], wg["w_up"], wg["w_down"]
    w_out = wg["w_out"].reshape(D_MODEL, D_MODEL)
    w_out_a = w_out[:HEADS * VDIM].reshape(HEADS, VDIM, D_MODEL)
    w_out_s = w_out[HEADS * VDIM:]
    conv_w = wg["conv_w"]
    conv_b = small["conv_b"].reshape(1, CONV_DIM)
    qkv_norm_w = jnp.concatenate([small["q_norm_w"], small["kv_norm_w"]])
    attn_norm_w = small["attn_out_norm_w"].reshape(HEADS, 1, VDIM)
    scale = QK ** -0.5

    inv_freq = ROPE_THETA ** (-jnp.arange(0, ROPE, 2, dtype=F32) / ROPE)
    ang = positions.astype(F32)[:, None] * inv_freq
    cos2 = jnp.tile(jnp.cos(ang), (1, 2))
    sin2 = jnp.tile(jnp.sin(ang), (1, 2))

    u = _rms_fwd(x, small["pre_mix_norm_w"], out_dtype=MXU_DTYPE, name="pre_mix_norm")
    cqkv = _mm(u, w_cqkv, "nn", name="in_proj_qkv")
    z = _mm(u, w_z, "nn", name="in_proj_z")
    xbc = _mm(u, w_xbc, "nn", name="in_proj_xbc")
    sm = _mm(u, w_small, "nn", name="in_proj_small")

    qkvn = _rms_fwd(cqkv, qkv_norm_w, groups=2, out_dtype=MXU_DTYPE, name="qkv_norm")
    q = _mm(qkvn, w_uq, "nn", b_blk=True, out_blk=True, a_cols=(0, Q_RANK), name="q_up")
    kv = _mm(qkvn, w_ukv, "nn", b_blk=True, out_blk=True, a_cols=(Q_RANK, KV_RANK), name="kv_up")
    q_h = _q_prep(q, cos2, sin2, scale, name="q_prep")
    k_h, v_h = _kv_prep(kv, sm, cos2, sin2)
    o_h, lse = _flash_fwd(q_h, k_h, v_h)
    attn = _hnorm_fwd(o_h, attn_norm_w)

    xbc_act = _conv_fwd(xbc, conv_w, conv_b)
    dtt = jnp.transpose(sm[:, HEAD_LANE:HEAD_LANE + SSD_H])
    ssd_args = (xbc_act, sm, dtt, _head_row(small["dt_bias"]), small["dt_bias"].reshape(SSD_H, 1),
                _head_row(small["a_log"]), small["a_log"].reshape(SSD_H, 1),
                jnp.broadcast_to(small["d_skip"].reshape(SSD_H, 1, 1), (SSD_H, 1, SSD_P)))
    y_ssd, prev = _ssd_fwd(*ssd_args)
    ssm = _gated_norm_fwd(y_ssd, z, small["ssd_norm_w"])

    mix = _mm(attn, w_out_a, "nn", a_blk=True, b_blk=True, name="out_proj_attn")
    mix = _mm(ssm, w_out_s, "nn", add=mix, name="out_proj_ssm")
    h1 = _rms_fwd(mix, small["post_mix_norm_w"], res=x, name="post_mix_norm")

    vv = _rms_fwd(h1, small["pre_ffn_norm_w"], out_dtype=MXU_DTYPE, name="pre_ffn_norm")
    gate = _mm(vv, w_gate, "nn", b_blk=True, out_blk=True, name="ffn_gate")
    up = _mm(vv, w_up, "nn", b_blk=True, out_blk=True, name="ffn_up")
    act = _swiglu_fwd(gate, up)
    ffn = _mm(act, w_down, "nn", a_blk=True, b_blk=True, name="ffn_down")
    loss_blk, dy, dffn, g_post_ffn = _loss_head(ffn, h1, target, small["post_ffn_norm_w"])

    dact = _mm(dffn, w_down, "nt", b_blk=True, out_blk=True, name="d_act")
    g_down = _mm(act, dffn, "tn", a_blk=True, out_blk=True, out_dtype=MXU_DTYPE, name="g_down")
    dgate, dup = _swiglu_bwd(gate, up, dact)
    dvv = _mm(dgate, w_gate, "nt", a_blk=True, b_blk=True, name="d_v_gate")
    dvv = _mm(dup, w_up, "nt", a_blk=True, b_blk=True, add=dvv, name="d_v_up")
    g_gate = _mm(vv, dgate, "tn", b_blk=True, out_blk=True, out_dtype=MXU_DTYPE, name="g_gate")
    g_up = _mm(vv, dup, "tn", b_blk=True, out_blk=True, out_dtype=MXU_DTYPE, name="g_up")
    dh1, g_pre_ffn = _rms_bwd(h1, small["pre_ffn_norm_w"], [dvv], res=dy, name="pre_ffn_norm_bwd")

    dmix, g_post_mix = _rms_bwd(mix, small["post_mix_norm_w"], [dh1], out_dtype=MXU_DTYPE, name="post_mix_norm_bwd")
    dattn = _mm(dmix, w_out_a, "nt", b_blk=True, out_blk=True, name="d_attn")
    dssm = _mm(dmix, w_out_s, "nt", name="d_ssm")
    g_out_a = _mm(attn, dmix, "tn", a_blk=True, out_blk=True, out_dtype=MXU_DTYPE, name="g_out_attn")
    g_out_s = _mm(ssm, dmix, "tn", out_dtype=MXU_DTYPE, name="g_out_ssm")
    g_out = jnp.concatenate([g_out_a.reshape(HEADS * VDIM, D_MODEL), g_out_s], axis=0)

    do_h, g_attn_norm = _hnorm_bwd(o_h, attn_norm_w, dattn)
    dq_h, delta = _flash_bwd_dq(q_h, k_h, v_h, o_h, do_h, lse)
    dk_h, dv_h = _flash_bwd_dkv(q_h, k_h, v_h, do_h, lse, delta)
    dq = _q_prep(dq_h, cos2, -sin2, scale, name="dq_post")

    dy_ssd, dz, g_ssd_norm = _gated_norm_bwd(y_ssd, z, small["ssd_norm_w"], dssm)
    dxbc_act, ddt, dpar = _ssd_bwd(*ssd_args, prev, dy_ssd)
    dkv, dsm = _dkv_post(dk_h, dv_h, ddt, cos2, -sin2)
    dpre, dwb = _conv_bwd_pre(xbc, conv_w, conv_b, dxbc_act)
    dxbc = _conv_bwd_in(dpre, conv_w)

    dqn = _mm(dq, w_uq, "nt", a_blk=True, b_blk=True, name="d_qn")
    dkvn = _mm(dkv, w_ukv, "nt", a_blk=True, b_blk=True, name="d_kvn")
    g_uq = _mm(qkvn, dq, "tn", b_blk=True, out_blk=True, a_cols=(0, Q_RANK), out_dtype=MXU_DTYPE, name="g_uq")
    g_ukv = _mm(qkvn, dkv, "tn", b_blk=True, out_blk=True, a_cols=(Q_RANK, KV_RANK), out_dtype=MXU_DTYPE, name="g_ukv")
    dcqkv, g_qkv_norm = _rms_bwd(cqkv, qkv_norm_w, [dqn, dkvn], out_dtype=MXU_DTYPE, name="qkv_norm_bwd")

    du = _mm(dcqkv, w_cqkv, "nt", name="d_u_qkv")
    du = _mm(dz, w_z, "nt", add=du, name="d_u_z")
    du = _mm(dxbc, w_xbc, "nt", add=du, name="d_u_xbc")
    du = _mm(dsm, w_small, "nt", add=du, name="d_u_small")
    g_in = _win_from_segments(_mm(u, dcqkv, "tn", out_dtype=MXU_DTYPE, name="g_in_qkv"),
                              _mm(u, dz, "tn", out_dtype=MXU_DTYPE, name="g_in_z"),
                              _mm(u, dxbc, "tn", out_dtype=MXU_DTYPE, name="g_in_xbc"),
                              _mm(u, dsm, "tn", out_dtype=MXU_DTYPE, name="g_in_small"))
    dx, g_pre_mix = _rms_bwd(x, small["pre_mix_norm_w"], [du], res=dh1, name="pre_mix_norm_bwd")

    g_big = {"w_in": g_in, "w_uq": g_uq, "w_ukv": g_ukv, "w_out": g_out.reshape(N_DEV, D_MODEL // N_DEV, D_MODEL),
             "w_gate": g_gate, "w_up": g_up, "w_down": g_down}
    hl = slice(HEAD_LANE, HEAD_LANE + SSD_H)
    g_small = {"q_norm_w": g_qkv_norm[0, :Q_RANK], "kv_norm_w": g_qkv_norm[0, Q_RANK:], "conv_b": dwb[CONV_K],
               "dt_bias": dpar[0, hl], "a_log": dpar[1, hl], "d_skip": dpar[2, hl], "ssd_norm_w": g_ssd_norm,
               "attn_out_norm_w": g_attn_norm, "pre_mix_norm_w": g_pre_mix, "post_mix_norm_w": g_post_mix,
               "pre_ffn_norm_w": g_pre_ffn, "post_ffn_norm_w": g_post_ffn, "conv_w": dwb[:CONV_K]}
    return loss_blk[0, 0], dx, g_big, g_small


_WEIGHT_ORDER = ("w_in", "q_norm_w", "w_uq", "kv_norm_w", "w_ukv", "conv_w", "conv_b", "dt_bias", "a_log", "d_skip",
                 "ssd_norm_w", "attn_out_norm_w", "w_out", "pre_mix_norm_w", "post_mix_norm_w", "pre_ffn_norm_w",
                 "post_ffn_norm_w", "w_gate", "w_up", "w_down")


def kernel(x, positions, w_in, q_norm_w, w_uq, kv_norm_w, w_ukv, conv_w, conv_b, dt_bias, a_log, d_skip, ssd_norm_w, attn_out_norm_w, w_out, pre_mix_norm_w, post_mix_norm_w, pre_ffn_norm_w, post_ffn_norm_w, w_gate, w_up, w_down, loss_target, m_w_in, m_q_norm_w, m_w_uq, m_kv_norm_w, m_w_ukv, m_conv_w, m_conv_b, m_dt_bias, m_a_log, m_d_skip, m_ssd_norm_w, m_attn_out_norm_w, m_w_out, m_pre_mix_norm_w, m_post_mix_norm_w, m_pre_ffn_norm_w, m_post_ffn_norm_w, m_w_gate, m_w_up, m_w_down, v_w_in, v_q_norm_w, v_w_uq, v_kv_norm_w, v_w_ukv, v_conv_w, v_conv_b, v_dt_bias, v_a_log, v_d_skip, v_ssd_norm_w, v_attn_out_norm_w, v_w_out, v_pre_mix_norm_w, v_post_mix_norm_w, v_pre_ffn_norm_w, v_post_ffn_norm_w, v_w_gate, v_w_up, v_w_down):
    w = dict(w_in=w_in, q_norm_w=q_norm_w, w_uq=w_uq, kv_norm_w=kv_norm_w, w_ukv=w_ukv, conv_w=conv_w, conv_b=conv_b,
             dt_bias=dt_bias, a_log=a_log, d_skip=d_skip, ssd_norm_w=ssd_norm_w, attn_out_norm_w=attn_out_norm_w,
             w_out=w_out, pre_mix_norm_w=pre_mix_norm_w, post_mix_norm_w=post_mix_norm_w,
             pre_ffn_norm_w=pre_ffn_norm_w, post_ffn_norm_w=post_ffn_norm_w, w_gate=w_gate, w_up=w_up, w_down=w_down)
    m = dict(w_in=m_w_in, q_norm_w=m_q_norm_w, w_uq=m_w_uq, kv_norm_w=m_kv_norm_w, w_ukv=m_w_ukv, conv_w=m_conv_w,
             conv_b=m_conv_b, dt_bias=m_dt_bias, a_log=m_a_log, d_skip=m_d_skip, ssd_norm_w=m_ssd_norm_w,
             attn_out_norm_w=m_attn_out_norm_w, w_out=m_w_out, pre_mix_norm_w=m_pre_mix_norm_w,
             post_mix_norm_w=m_post_mix_norm_w, pre_ffn_norm_w=m_pre_ffn_norm_w, post_ffn_norm_w=m_post_ffn_norm_w,
             w_gate=m_w_gate, w_up=m_w_up, w_down=m_w_down)
    v = dict(w_in=v_w_in, q_norm_w=v_q_norm_w, w_uq=v_w_uq, kv_norm_w=v_kv_norm_w, w_ukv=v_w_ukv, conv_w=v_conv_w,
             conv_b=v_conv_b, dt_bias=v_dt_bias, a_log=v_a_log, d_skip=v_d_skip, ssd_norm_w=v_ssd_norm_w,
             attn_out_norm_w=v_attn_out_norm_w, w_out=v_w_out, pre_mix_norm_w=v_pre_mix_norm_w,
             post_mix_norm_w=v_post_mix_norm_w, pre_ffn_norm_w=v_pre_ffn_norm_w, post_ffn_norm_w=v_post_ffn_norm_w,
             w_gate=v_w_gate, w_up=v_w_up, w_down=v_w_down)
    w, m, v = ({k: t[0] for k, t in d.items()} for d in (w, m, v))
    me = 4 * lax.axis_index("x") + 2 * lax.axis_index("y") + lax.axis_index("c")
    big_names = [b[0] for b in _BIG]
    cshard = CONV_DIM // N_DEV

    shards = [w[name].astype(MXU_DTYPE) for name in big_names]
    shards.append(jnp.stack(_split3(w["conv_w"])).reshape(3 * CONV_K, cshard).astype(MXU_DTYPE))
    gathered = _all_gather(shards, name="gather_weights")
    wg = dict(zip(big_names, gathered[:-1]))
    cw = gathered[-1].astype(F32).reshape(N_DEV, 3, CONV_K, cshard)
    wg["conv_w"] = jnp.transpose(cw[:, 0] + cw[:, 1] + cw[:, 2], (1, 0, 2)).reshape(CONV_K, CONV_DIM)
    small = {name: w[name] for name, _ in _SMALL if name != "conv_w"}

    loss_local, dx, g_big, g_small = _local_step(x[0], positions[0], loss_target[0], wg, small)
    loss = lax.psum(loss_local, ("x", "y", "c"))

    recv = _all_to_all([g_big[name] for name in big_names], name="exchange_grads")
    grads, deltas, new_m, new_v = {}, {}, {}, {}
    for name, parts in zip(big_names, recv):
        grads[name], deltas[name], new_m[name], new_v[name] = _adamw(parts, w[name], m[name], v[name],
                                                                     name="adamw_" + name)

    def embed(t):
        return lax.dynamic_update_slice(jnp.zeros((CONV_K, CONV_DIM), F32), t, (0, me * cshard))

    parts_s = _all_gather([_pack_small(g_small)], name="gather_small_grads")[0]
    packs = [_pack_small({**{n_: d[n_] for n_, _ in _SMALL if n_ != "conv_w"}, "conv_w": embed(d["conv_w"])})
             for d in (w, m, v)]
    outs = [_unpack_small(t) for t in _adamw_small(parts_s, *packs)]
    for name, n in _SMALL:
        for dst, src in zip((grads, deltas, new_m, new_v), outs):
            if name == "conv_w":
                dst[name] = lax.dynamic_slice(src[name].reshape(CONV_K, CONV_DIM), (0, me * cshard), (CONV_K, cshard))
            else:
                dst[name] = src[name]

    def lead(d):
        return [d[name][None] for name in _WEIGHT_ORDER]

    return (loss, dx[None], *lead(grads), *lead(deltas), *lead(new_m), *lead(new_v))
```

```python
import numpy as np

import jax
import jax.numpy as jnp
from jax import lax
from jax.experimental import pallas as pl
from jax.experimental.pallas import tpu as pltpu

F32 = jnp.float32
BF16 = jnp.bfloat16
MXU_DTYPE = jnp.bfloat16
EPS = 1e-6
VMEM_LIMIT_BYTES = 48 * 1024 * 1024
K_TILE_MAX = 2048

N_DEV = 8
D_MODEL = 2048
Q_RANK = 512
KV_RANK = 512
ROPE = 64
HALF = ROPE // 2
HEADS = 8
NOPE = 128
VDIM = 128
QK = NOPE + ROPE
SSD_W = 1024
SSD_H = 16
SSD_P = 64
SSD_G = 2
SSD_E = SSD_H // SSD_G
SSD_N = 128
CHUNK = 128
CONV_K = 4
CONV_DIM = SSD_W + 2 * SSD_G * SSD_N
B_OFF = SSD_W
C_OFF = SSD_W + SSD_G * SSD_N
D_FF = 5632
D_IN = Q_RANK + KV_RANK + ROPE + SSD_W + CONV_DIM + SSD_H
ROPE_THETA = 10000.0
LANE = 128
HEAD_LANE = ROPE

ADAM_LR = 0.001
ADAM_B1 = 0.9
ADAM_B2 = 0.999
ADAM_EPS = 1e-08
ADAM_WD = 0.01
ADAM_STEP = 10


def _pick(n, cands):
    for c in cands:
        if n % c == 0:
            return c
    return n


def _params(*sem):
    return pltpu.CompilerParams(dimension_semantics=sem, vmem_limit_bytes=VMEM_LIMIT_BYTES)


def _sigmoid(x):
    return 1.0 / (1.0 + jnp.exp(-x))


def _silu(x):
    return x * _sigmoid(x)


def _dsilu(x):
    s = _sigmoid(x)
    return s * (1.0 + x * (1.0 - s))


def _softplus(x):
    e = jnp.exp(-jnp.abs(x))
    small = e * (1.0 - e * (0.5 - e * (1.0 / 3.0)))
    return jnp.maximum(x, 0.0) + jnp.where(e < 0.01, small, jnp.log(1.0 + e))


def _dot(a, b, ca, cb):
    return lax.dot_general(a, b, (((ca,), (cb,)), ((), ())), preferred_element_type=F32)


def _mx(v):
    return v.astype(MXU_DTYPE)


def _split3(a):
    hi = a.astype(BF16)
    r1 = a - hi.astype(F32)
    mid = r1.astype(BF16)
    lo = (r1 - mid.astype(F32)).astype(BF16)
    return hi, mid, lo


def _exact_dot(a, b, ca, cb, split_a):
    if split_a:
        return sum(_dot(p, b, ca, cb) for p in _split3(a))
    return sum(_dot(a, p, ca, cb) for p in _split3(b))


def _mm(a, b, mode, *, a_blk=False, b_blk=False, out_blk=False, a_cols=None, add=None, out_dtype=F32, name="mm"):
    a2, b2 = a.shape[-2:], b.shape[-2:]
    a_last = a2[1] if a_cols is None else a_cols[1]
    a_start = 0 if a_cols is None else a_cols[0]
    if mode == "nn":
        m, k, (k2, n) = a2[0], a_last, b2
    elif mode == "nt":
        m, k, (n, k2) = a2[0], a_last, b2
    else:
        k, m, (k2, n) = a2[0], a_last, b2
    assert k == k2, (a.shape, b.shape, mode)
    tm = _pick(m, (1024, 704, 512, 256, 128))
    tn = _pick(n, (1024, 768, 704, 512, 256, 192, 128))
    tk = k if k <= K_TILE_MAX else _pick(k, (K_TILE_MAX, 1024, 512))
    nk = k // tk
    jo = N_DEV if out_blk else 1
    jr = N_DEV if (a_blk and b_blk and not out_blk) else 1
    ca, cb = {"nn": (1, 0), "nt": (1, 1), "tn": (0, 0)}[mode]
    has_add = add is not None
    single = jr * nk == 1
    if mode == "tn":
        assert a_start % tm == 0
        a_block, a_idx = (tk, tm), (lambda i, kk: (kk, i + a_start // tm))
    else:
        assert a_start % tk == 0
        a_block, a_idx = (tm, tk), (lambda i, kk: (i, kk + a_start // tk))
    b_block, b_idx = ((tn, tk), (lambda nn_, kk: (nn_, kk))) if mode == "nt" else ((tk, tn), (lambda nn_, kk: (kk, nn_)))

    def sel(o, r):
        return o if out_blk else r

    a_spec = (pl.BlockSpec((None,) + a_block, lambda o, i, nn_, r, kk: (sel(o, r),) + a_idx(i, kk)) if a_blk
              else pl.BlockSpec(a_block, lambda o, i, nn_, r, kk: a_idx(i, kk)))
    b_spec = (pl.BlockSpec((None,) + b_block, lambda o, i, nn_, r, kk: (sel(o, r),) + b_idx(nn_, kk)) if b_blk
              else pl.BlockSpec(b_block, lambda o, i, nn_, r, kk: b_idx(nn_, kk)))
    o_spec = (pl.BlockSpec((None, tm, tn), lambda o, i, nn_, r, kk: (o, i, nn_)) if out_blk
              else pl.BlockSpec((tm, tn), lambda o, i, nn_, r, kk: (i, nn_)))

    def body(*refs):
        a_ref, b_ref = refs[0], refs[1]
        add_ref = refs[2] if has_add else None
        o_ref = refs[3] if has_add else refs[2]
        part = _dot(_mx(a_ref[...]), _mx(b_ref[...]), ca, cb)
        if single:
            if has_add:
                part = part + add_ref[...]
            o_ref[...] = part.astype(o_ref.dtype)
            return
        acc = refs[-1]
        r, kk = pl.program_id(3), pl.program_id(4)
        first = jnp.logical_and(r == 0, kk == 0)
        last = jnp.logical_and(r == jr - 1, kk == nk - 1)

        @pl.when(first)
        def _():
            acc[...] = part

        @pl.when(jnp.logical_not(first))
        def _():
            acc[...] += part

        @pl.when(last)
        def _():
            res = acc[...]
            if has_add:
                res = res + add_ref[...]
            o_ref[...] = res.astype(o_ref.dtype)

    out_shape = ((N_DEV, m, n) if out_blk else (m, n))
    return pl.pallas_call(
        body, name=name, grid=(jo, m // tm, n // tn, jr, nk),
        in_specs=[a_spec, b_spec] + ([o_spec] if has_add else []), out_specs=o_spec,
        out_shape=jax.ShapeDtypeStruct(out_shape, out_dtype),
        scratch_shapes=[] if single else [pltpu.VMEM((tm, tn), F32)],
        compiler_params=_params("parallel", "parallel", "parallel", "arbitrary", "arbitrary"),
    )(*((a, b) + ((add,) if has_add else ())))


def _row_tile(r_):
    return _pick(r_, (256, 128, 64, 32, 16, 8))


def _rms_fwd(t, w, groups=1, res=None, out_dtype=F32, name="rms_fwd"):
    r_, f = t.shape
    fg = f // groups
    tr = _row_tile(r_)
    has_res = res is not None

    def body(*refs):
        t_ref, w_ref = refs[0], refs[1]
        res_ref = refs[2] if has_res else None
        o_ref = refs[-1]
        for g in range(groups):
            sl = slice(g * fg, (g + 1) * fg)
            tv = t_ref[:, sl].astype(F32)
            r = lax.rsqrt(jnp.mean(tv * tv, axis=-1, keepdims=True) + EPS)
            y = tv * r * w_ref[:, sl]
            if has_res:
                y = y + res_ref[:, sl]
            o_ref[:, sl] = y.astype(o_ref.dtype)

    row = pl.BlockSpec((tr, f), lambda i: (i, 0))
    wsp = pl.BlockSpec((1, f), lambda i: (0, 0))
    return pl.pallas_call(
        body, name=name, grid=(r_ // tr,),
        in_specs=[row, wsp] + ([row] if has_res else []), out_specs=row,
        out_shape=jax.ShapeDtypeStruct((r_, f), out_dtype),
        compiler_params=_params("parallel"),
    )(*((t, w.reshape(1, f)) + ((res,) if has_res else ())))


def _rms_bwd(t, w, dys, res=None, out_dtype=F32, name="rms_bwd"):
    r_, f = t.shape
    groups = len(dys)
    fg = f // groups
    tr = _row_tile(r_)
    has_res = res is not None

    def body(*refs):
        t_ref, w_ref = refs[0], refs[1]
        dy_refs = refs[2:2 + groups]
        res_ref = refs[2 + groups] if has_res else None
        dt_ref, dw_ref = refs[-2], refs[-1]

        @pl.when(pl.program_id(0) == 0)
        def _():
            dw_ref[...] = jnp.zeros_like(dw_ref)

        for g in range(groups):
            sl = slice(g * fg, (g + 1) * fg)
            tv = t_ref[:, sl].astype(F32)
            dyv = dy_refs[g][...].astype(F32)
            r = lax.rsqrt(jnp.mean(tv * tv, axis=-1, keepdims=True) + EPS)
            gw = dyv * w_ref[:, sl]
            c = jnp.mean(gw * tv, axis=-1, keepdims=True)
            dt = r * gw - tv * (r * r * r * c)
            if has_res:
                dt = dt + res_ref[:, sl]
            dt_ref[:, sl] = dt.astype(dt_ref.dtype)
            dw_ref[:, sl] += jnp.sum(dyv * tv * r, axis=0, keepdims=True)

    row = pl.BlockSpec((tr, f), lambda i: (i, 0))
    grow = pl.BlockSpec((tr, fg), lambda i: (i, 0))
    wsp = pl.BlockSpec((1, f), lambda i: (0, 0))
    return pl.pallas_call(
        body, name=name, grid=(r_ // tr,),
        in_specs=[row, wsp] + [grow] * groups + ([row] if has_res else []), out_specs=[row, wsp],
        out_shape=[jax.ShapeDtypeStruct((r_, f), out_dtype), jax.ShapeDtypeStruct((1, f), F32)],
        compiler_params=_params("arbitrary"),
    )(*((t, w.reshape(1, f)) + tuple(dys) + ((res,) if has_res else ())))


def _hnorm_fwd(o, w, name="attn_out_norm"):
    h, s_, v = o.shape
    tr = _row_tile(s_)

    def body(o_ref, w_ref, y_ref):
        ss = jnp.sum(o_ref[0] * o_ref[0], axis=-1, keepdims=True)
        for i in range(1, h):
            ss = ss + jnp.sum(o_ref[i] * o_ref[i], axis=-1, keepdims=True)
        r = lax.rsqrt(ss * (1.0 / (h * v)) + EPS)
        for i in range(h):
            y_ref[i] = (o_ref[i] * r * w_ref[i]).astype(y_ref.dtype)

    blk = pl.BlockSpec((h, tr, v), lambda i: (0, i, 0))
    wsp = pl.BlockSpec((h, 1, v), lambda i: (0, 0, 0))
    return pl.pallas_call(
        body, name=name, grid=(s_ // tr,), in_specs=[blk, wsp], out_specs=blk,
        out_shape=jax.ShapeDtypeStruct(o.shape, MXU_DTYPE), compiler_params=_params("parallel"),
    )(o, w)


def _hnorm_bwd(o, w, dy, name="attn_out_norm_bwd"):
    h, s_, v = o.shape
    tr = _row_tile(s_)

    def body(o_ref, w_ref, dy_ref, do_ref, dw_ref):
        @pl.when(pl.program_id(0) == 0)
        def _():
            dw_ref[...] = jnp.zeros_like(dw_ref)

        ss = jnp.zeros((tr, 1), F32)
        cc = jnp.zeros((tr, 1), F32)
        for i in range(h):
            ov = o_ref[i]
            ss = ss + jnp.sum(ov * ov, axis=-1, keepdims=True)
            cc = cc + jnp.sum(dy_ref[i] * w_ref[i] * ov, axis=-1, keepdims=True)
        r = lax.rsqrt(ss * (1.0 / (h * v)) + EPS)
        c = cc * (1.0 / (h * v))
        for i in range(h):
            ov = o_ref[i]
            dyv = dy_ref[i]
            do_ref[i] = r * dyv * w_ref[i] - ov * (r * r * r * c)
            dw_ref[i] += jnp.sum(dyv * ov * r, axis=0, keepdims=True)

    blk = pl.BlockSpec((h, tr, v), lambda i: (0, i, 0))
    wsp = pl.BlockSpec((h, 1, v), lambda i: (0, 0, 0))
    return pl.pallas_call(
        body, name=name, grid=(s_ // tr,), in_specs=[blk, wsp, blk], out_specs=[blk, wsp],
        out_shape=[jax.ShapeDtypeStruct(o.shape, F32), jax.ShapeDtypeStruct((h, 1, v), F32)],
        compiler_params=_params("arbitrary"),
    )(o, w, dy)


def _loss_head(ffn, h1, target, w, name="loss_head"):
    r_, f = ffn.shape
    tr = _row_tile(r_)

    def body(ffn_ref, h1_ref, tg_ref, w_ref, loss_ref, dy_ref, dffn_ref, dw_ref):
        @pl.when(pl.program_id(0) == 0)
        def _():
            dw_ref[...] = jnp.zeros_like(dw_ref)
            loss_ref[...] = jnp.zeros_like(loss_ref)

        tv = ffn_ref[...]
        wv = w_ref[...]
        r = lax.rsqrt(jnp.mean(tv * tv, axis=-1, keepdims=True) + EPS)
        tn = tv * r
        e = h1_ref[...] + tn * wv - tg_ref[...]
        tot = jnp.sum(jnp.sum(e * e, axis=1, keepdims=True), axis=0, keepdims=True) * (0.5 / f)
        loss_ref[...] += tot + jnp.zeros_like(loss_ref)
        dyv = e * (1.0 / f)
        dy_ref[...] = dyv
        gw = dyv * wv
        c = jnp.mean(gw * tv, axis=-1, keepdims=True)
        dffn_ref[...] = (r * gw - tv * (r * r * r * c)).astype(dffn_ref.dtype)
        dw_ref[...] += jnp.sum(dyv * tn, axis=0, keepdims=True)

    row = pl.BlockSpec((tr, f), lambda i: (i, 0))
    wsp = pl.BlockSpec((1, f), lambda i: (0, 0))
    lsp = pl.BlockSpec((1, LANE), lambda i: (0, 0))
    return pl.pallas_call(
        body, name=name, grid=(r_ // tr,),
        in_specs=[row, row, row, wsp], out_specs=[lsp, row, row, wsp],
        out_shape=[jax.ShapeDtypeStruct((1, LANE), F32), jax.ShapeDtypeStruct((r_, f), F32),
                   jax.ShapeDtypeStruct((r_, f), MXU_DTYPE), jax.ShapeDtypeStruct((1, f), F32)],
        compiler_params=_params("arbitrary"),
    )(ffn, h1, target, w.reshape(1, f))


def _rot_matrix():
    p = np.zeros((ROPE, ROPE), np.float32)
    for i in range(HALF):
        p[i + HALF, i] = -1.0
        p[i, i + HALF] = 1.0
    return jnp.asarray(p, BF16)


def _rope_val(r, c2, s2, rot):
    return r * c2 + _exact_dot(r, rot, 1, 0, True) * s2


def _q_prep(q, cos2, sin2, scale, name):
    h, s_, _ = q.shape
    tr = _pick(s_, (1024, 512, 256, 128, 64, 32, 16, 8))

    def body(q_ref, c_ref, s_ref, rot_ref, o_ref):
        x = q_ref[...]
        o_ref[:, :NOPE] = (x[:, :NOPE] * scale).astype(o_ref.dtype)
        o_ref[:, NOPE:] = (_rope_val(x[:, NOPE:], c_ref[...], s_ref[...], rot_ref[...]) * scale).astype(o_ref.dtype)

    blk = pl.BlockSpec((None, tr, QK), lambda hh, i: (hh, i, 0))
    csp = pl.BlockSpec((tr, ROPE), lambda hh, i: (i, 0))
    return pl.pallas_call(
        body, name=name, grid=(h, s_ // tr),
        in_specs=[blk, csp, csp, pl.BlockSpec((ROPE, ROPE), lambda hh, i: (0, 0))], out_specs=blk,
        out_shape=jax.ShapeDtypeStruct(q.shape, MXU_DTYPE), compiler_params=_params("parallel", "parallel"),
    )(q, cos2, sin2, _rot_matrix())


def _kv_prep(kv, small, cos2, sin2, name="kv_prep"):
    h, s_, _ = kv.shape
    tr = _row_tile(s_)

    def body(kv_ref, sm_ref, c_ref, s_ref, rot_ref, k_ref, v_ref):
        kr = _rope_val(sm_ref[:, :ROPE], c_ref[...], s_ref[...], rot_ref[...]).astype(k_ref.dtype)
        for i in range(h):
            k_ref[i, :, :NOPE] = kv_ref[i, :, :NOPE].astype(k_ref.dtype)
            k_ref[i, :, NOPE:] = kr
            v_ref[i] = kv_ref[i, :, NOPE:].astype(v_ref.dtype)

    csp = pl.BlockSpec((tr, ROPE), lambda i: (i, 0))
    return pl.pallas_call(
        body, name=name, grid=(s_ // tr,),
        in_specs=[pl.BlockSpec((h, tr, NOPE + VDIM), lambda i: (0, i, 0)), pl.BlockSpec((tr, LANE), lambda i: (i, 0)),
                  csp, csp, pl.BlockSpec((ROPE, ROPE), lambda i: (0, 0))],
        out_specs=[pl.BlockSpec((h, tr, QK), lambda i: (0, i, 0)), pl.BlockSpec((h, tr, VDIM), lambda i: (0, i, 0))],
        out_shape=[jax.ShapeDtypeStruct((h, s_, QK), MXU_DTYPE), jax.ShapeDtypeStruct((h, s_, VDIM), MXU_DTYPE)],
        compiler_params=_params("parallel"),
    )(kv, small, cos2, sin2, _rot_matrix())


def _dkv_post(dk, dv, ddt, cos2, nsin2, name="dkv_post"):
    h, s_, _ = dk.shape
    tr = _row_tile(s_)

    def body(dk_ref, dv_ref, ddt_ref, c_ref, s_ref, rot_ref, dkv_ref, dsm_ref):
        acc = dk_ref[0, :, NOPE:]
        for i in range(1, h):
            acc = acc + dk_ref[i, :, NOPE:]
        dsm_ref[:, :ROPE] = _rope_val(acc, c_ref[...], s_ref[...], rot_ref[...]).astype(dsm_ref.dtype)
        dsm_ref[:, ROPE:] = ddt_ref[:, ROPE:].astype(dsm_ref.dtype)
        for i in range(h):
            dkv_ref[i, :, :NOPE] = dk_ref[i, :, :NOPE].astype(dkv_ref.dtype)
            dkv_ref[i, :, NOPE:] = dv_ref[i].astype(dkv_ref.dtype)

    csp = pl.BlockSpec((tr, ROPE), lambda i: (i, 0))
    return pl.pallas_call(
        body, name=name, grid=(s_ // tr,),
        in_specs=[pl.BlockSpec((h, tr, QK), lambda i: (0, i, 0)), pl.BlockSpec((h, tr, VDIM), lambda i: (0, i, 0)),
                  pl.BlockSpec((tr, LANE), lambda i: (i, 0)), csp, csp, pl.BlockSpec((ROPE, ROPE), lambda i: (0, 0))],
        out_specs=[pl.BlockSpec((h, tr, NOPE + VDIM), lambda i: (0, i, 0)), pl.BlockSpec((tr, LANE), lambda i: (i, 0))],
        out_shape=[jax.ShapeDtypeStruct((h, s_, NOPE + VDIM), MXU_DTYPE), jax.ShapeDtypeStruct((s_, LANE), MXU_DTYPE)],
        compiler_params=_params("parallel"),
    )(dk, dv, ddt, cos2, nsin2, _rot_matrix())


def _attn_tile(s):
    return 512 if s % 1024 == 0 else s // 2


def _pairs(n, by_key):
    if by_key:
        pr = [(i, j) for j in range(n) for i in range(j, n)]
    else:
        pr = [(i, j) for i in range(n) for j in range(i + 1)]
    return (jnp.asarray([p[0] for p in pr], jnp.int32), jnp.asarray([p[1] for p in pr], jnp.int32))


def _diag_mask(t):
    return lax.broadcasted_iota(jnp.int32, (t, t), 1) <= lax.broadcasted_iota(jnp.int32, (t, t), 0)


def _flash_specs(t, dk, dv):
    qsp = pl.BlockSpec((None, t, dk), lambda hh, p, qi, kj: (hh, qi[p], 0))
    ksp = pl.BlockSpec((None, t, dk), lambda hh, p, qi, kj: (hh, kj[p], 0))
    vsp = pl.BlockSpec((None, t, dv), lambda hh, p, qi, kj: (hh, kj[p], 0))
    osp = pl.BlockSpec((None, t, dv), lambda hh, p, qi, kj: (hh, qi[p], 0))
    lsp = pl.BlockSpec((None, t, 1), lambda hh, p, qi, kj: (hh, qi[p], 0))
    return qsp, ksp, vsp, osp, lsp


def _flash_fwd(q, k, v, name="flash_fwd"):
    h, s_, dk = q.shape
    dv = v.shape[-1]
    t = _attn_tile(s_)
    n = s_ // t
    qi, kj = _pairs(n, False)

    def body(qi_ref, kj_ref, q_ref, k_ref, v_ref, o_ref, lse_ref, m_s, l_s, acc):
        p_ = pl.program_id(1)
        i, j = qi_ref[p_], kj_ref[p_]

        @pl.when(j == 0)
        def _():
            m_s[...] = jnp.full_like(m_s, -jnp.inf)
            l_s[...] = jnp.zeros_like(l_s)
            acc[...] = jnp.zeros_like(acc)

        def update(sc):
            m_new = jnp.maximum(m_s[...], jnp.max(sc, axis=1, keepdims=True))
            alpha = jnp.exp(m_s[...] - m_new)
            p = jnp.exp(sc - m_new)
            l_s[...] = alpha * l_s[...] + jnp.sum(p, axis=1, keepdims=True)
            acc[...] = alpha * acc[...] + _dot(_mx(p), v_ref[...], 1, 0)
            m_s[...] = m_new

        @pl.when(j < i)
        def _():
            update(_dot(q_ref[...], k_ref[...], 1, 1))

        @pl.when(j == i)
        def _():
            update(jnp.where(_diag_mask(t), _dot(q_ref[...], k_ref[...], 1, 1), -jnp.inf))
            o_ref[...] = acc[...] / l_s[...]
            lse_ref[...] = m_s[...] + jnp.log(l_s[...])

    qsp, ksp, vsp, osp, lsp = _flash_specs(t, dk, dv)
    gs = pltpu.PrefetchScalarGridSpec(
        num_scalar_prefetch=2, grid=(h, qi.shape[0]), in_specs=[qsp, ksp, vsp], out_specs=[osp, lsp],
        scratch_shapes=[pltpu.VMEM((t, 1), F32), pltpu.VMEM((t, 1), F32), pltpu.VMEM((t, dv), F32)])
    return pl.pallas_call(
        body, name=name, grid_spec=gs,
        out_shape=[jax.ShapeDtypeStruct((h, s_, dv), F32), jax.ShapeDtypeStruct((h, s_, 1), F32)],
        compiler_params=_params("parallel", "arbitrary"),
    )(qi, kj, q, k, v)


def _flash_bwd_dq(q, k, v, o, do, lse, name="flash_bwd_dq"):
    h, s_, dk = q.shape
    dv = v.shape[-1]
    t = _attn_tile(s_)
    n = s_ // t
    qi, kj = _pairs(n, False)

    def body(qi_ref, kj_ref, q_ref, k_ref, v_ref, o_ref, do_ref, lse_ref, dq_ref, delta_ref, acc, delta_s):
        p_ = pl.program_id(1)
        i, j = qi_ref[p_], kj_ref[p_]

        @pl.when(j == 0)
        def _():
            delta_s[...] = jnp.sum(do_ref[...] * o_ref[...], axis=1, keepdims=True)
            acc[...] = jnp.zeros_like(acc)

        def update(sc):
            p = jnp.exp(sc - lse_ref[...])
            dp = _dot(_mx(do_ref[...]), v_ref[...], 1, 1)
            ds = p * (dp - delta_s[...])
            acc[...] += _dot(_mx(ds), k_ref[...], 1, 0)

        @pl.when(j < i)
        def _():
            update(_dot(q_ref[...], k_ref[...], 1, 1))

        @pl.when(j == i)
        def _():
            update(jnp.where(_diag_mask(t), _dot(q_ref[...], k_ref[...], 1, 1), -jnp.inf))
            dq_ref[...] = acc[...]
            delta_ref[...] = delta_s[...]

    qsp, ksp, vsp, osp, lsp = _flash_specs(t, dk, dv)
    gs = pltpu.PrefetchScalarGridSpec(
        num_scalar_prefetch=2, grid=(h, qi.shape[0]), in_specs=[qsp, ksp, vsp, osp, osp, lsp], out_specs=[qsp, lsp],
        scratch_shapes=[pltpu.VMEM((t, dk), F32), pltpu.VMEM((t, 1), F32)])
    return pl.pallas_call(
        body, name=name, grid_spec=gs,
        out_shape=[jax.ShapeDtypeStruct((h, s_, dk), F32), jax.ShapeDtypeStruct((h, s_, 1), F32)],
        compiler_params=_params("parallel", "arbitrary"),
    )(qi, kj, q, k, v, o, do, lse)


def _flash_bwd_dkv(q, k, v, do, lse, delta, name="flash_bwd_dkv"):
    h, s_, dk = q.shape
    dv = v.shape[-1]
    t = _attn_tile(s_)
    n = s_ // t
    qi, kj = _pairs(n, True)

    def body(qi_ref, kj_ref, q_ref, k_ref, v_ref, do_ref, lse_ref, delta_ref, dk_ref, dv_ref, dk_acc, dv_acc):
        p_ = pl.program_id(1)
        i, j = qi_ref[p_], kj_ref[p_]

        def update(sc):
            p = jnp.exp(sc - lse_ref[...])
            dob = _mx(do_ref[...])
            dv_acc[...] += _dot(_mx(p), dob, 0, 0)
            dp = _dot(dob, v_ref[...], 1, 1)
            ds = p * (dp - delta_ref[...])
            dk_acc[...] += _dot(_mx(ds), q_ref[...], 0, 0)

        @pl.when(i == j)
        def _():
            dk_acc[...] = jnp.zeros_like(dk_acc)
            dv_acc[...] = jnp.zeros_like(dv_acc)
            update(jnp.where(_diag_mask(t), _dot(q_ref[...], k_ref[...], 1, 1), -jnp.inf))

        @pl.when(i > j)
        def _():
            update(_dot(q_ref[...], k_ref[...], 1, 1))

        @pl.when(i == n - 1)
        def _():
            dk_ref[...] = dk_acc[...]
            dv_ref[...] = dv_acc[...]

    qsp, ksp, vsp, osp, lsp = _flash_specs(t, dk, dv)
    gs = pltpu.PrefetchScalarGridSpec(
        num_scalar_prefetch=2, grid=(h, qi.shape[0]), in_specs=[qsp, ksp, vsp, osp, lsp, lsp], out_specs=[ksp, vsp],
        scratch_shapes=[pltpu.VMEM((t, dk), F32), pltpu.VMEM((t, dv), F32)])
    return pl.pallas_call(
        body, name=name, grid_spec=gs,
        out_shape=[jax.ShapeDtypeStruct((h, s_, dk), F32), jax.ShapeDtypeStruct((h, s_, dv), F32)],
        compiler_params=_params("parallel", "arbitrary"),
    )(qi, kj, q, k, v, do, lse, delta)


HALO = 8


def _conv_specs(s_, c, tr, after):
    main = pl.BlockSpec((tr, c), lambda i: (i, 0))
    per = tr // HALO
    if after:
        halo = pl.BlockSpec((HALO, c), lambda i: (jnp.minimum((i + 1) * per, s_ // HALO - 1), 0))
    else:
        halo = pl.BlockSpec((HALO, c), lambda i: (jnp.maximum(i * per - 1, 0), 0))
    return main, halo


def _fill_before(ext, t_ref, h_ref, tr):
    ext[0:HALO, :] = jnp.where(pl.program_id(0) > 0, h_ref[...], 0.0)
    ext[HALO:HALO + tr, :] = t_ref[...]


def _taps(ext, w_ref, tr):
    base = HALO - (CONV_K - 1)
    acc = ext[base:base + tr, :] * w_ref[0:1, :]
    for k in range(1, CONV_K):
        acc = acc + ext[base + k:base + k + tr, :] * w_ref[k:k + 1, :]
    return acc


def _conv_fwd(t, w, b, name="conv_fwd"):
    s_, c = t.shape
    tr = _row_tile(s_)

    def body(t_ref, h_ref, w_ref, b_ref, o_ref, ext):
        _fill_before(ext, t_ref, h_ref, tr)
        o_ref[...] = _silu(_taps(ext, w_ref, tr) + b_ref[...])

    main, halo = _conv_specs(s_, c, tr, False)
    return pl.pallas_call(
        body, name=name, grid=(s_ // tr,),
        in_specs=[main, halo, pl.BlockSpec((CONV_K, c), lambda i: (0, 0)), pl.BlockSpec((1, c), lambda i: (0, 0))],
        out_specs=main, out_shape=jax.ShapeDtypeStruct((s_, c), F32),
        scratch_shapes=[pltpu.VMEM((tr + HALO, c), F32)], compiler_params=_params("parallel"),
    )(t, t, w, b)


def _conv_bwd_pre(t, w, b, dact, name="conv_bwd_pre"):
    s_, c = t.shape
    tr = _row_tile(s_)

    def body(t_ref, h_ref, w_ref, b_ref, da_ref, dpre_ref, dwb_ref, ext):
        @pl.when(pl.program_id(0) == 0)
        def _():
            dwb_ref[...] = jnp.zeros_like(dwb_ref)

        _fill_before(ext, t_ref, h_ref, tr)
        dpre = da_ref[...] * _dsilu(_taps(ext, w_ref, tr) + b_ref[...])
        dpre_ref[...] = dpre
        base = HALO - (CONV_K - 1)
        for k in range(CONV_K):
            dwb_ref[k:k + 1, :] += jnp.sum(dpre * ext[base + k:base + k + tr, :], axis=0, keepdims=True)
        dwb_ref[CONV_K:CONV_K + 1, :] += jnp.sum(dpre, axis=0, keepdims=True)

    main, halo = _conv_specs(s_, c, tr, False)
    return pl.pallas_call(
        body, name=name, grid=(s_ // tr,),
        in_specs=[main, halo, pl.BlockSpec((CONV_K, c), lambda i: (0, 0)), pl.BlockSpec((1, c), lambda i: (0, 0)), main],
        out_specs=[main, pl.BlockSpec((8, c), lambda i: (0, 0))],
        out_shape=[jax.ShapeDtypeStruct((s_, c), F32), jax.ShapeDtypeStruct((8, c), F32)],
        scratch_shapes=[pltpu.VMEM((tr + HALO, c), F32)], compiler_params=_params("arbitrary"),
    )(t, t, w, b, dact)


def _conv_bwd_in(dpre, w, name="conv_bwd_in"):
    s_, c = dpre.shape
    tr = _row_tile(s_)
    nt = s_ // tr

    def body(d_ref, h_ref, w_ref, o_ref, ext):
        ext[0:tr, :] = d_ref[...]
        ext[tr:tr + HALO, :] = jnp.where(pl.program_id(0) < nt - 1, h_ref[...], 0.0)
        acc = ext[CONV_K - 1:CONV_K - 1 + tr, :] * w_ref[0:1, :]
        for k in range(1, CONV_K):
            acc = acc + ext[CONV_K - 1 - k:CONV_K - 1 - k + tr, :] * w_ref[k:k + 1, :]
        o_ref[...] = acc.astype(o_ref.dtype)

    main, halo = _conv_specs(s_, c, tr, True)
    return pl.pallas_call(
        body, name=name, grid=(nt,),
        in_specs=[main, halo, pl.BlockSpec((CONV_K, c), lambda i: (0, 0))],
        out_specs=main, out_shape=jax.ShapeDtypeStruct((s_, c), MXU_DTYPE),
        scratch_shapes=[pltpu.VMEM((tr + HALO, c), F32)], compiler_params=_params("parallel"),
    )(dpre, dpre, w)


def _ssd_chunk_common(dt_ref, dtt_ref, br_ref, bc_ref, ar_ref, ac_ref):
    li = lax.broadcasted_iota(jnp.int32, (CHUNK, CHUNK), 0)
    si = lax.broadcasted_iota(jnp.int32, (CHUNK, CHUNK), 1)
    lower = li >= si
    lower_b = lower.astype(BF16)
    upper_b = (li <= si).astype(BF16)
    zr = dt_ref[...] + br_ref[...]
    dtc = _softplus(zr)
    a_row = -jnp.exp(ar_ref[...])
    acum = _exact_dot(lower_b, dtc * a_row, 1, 0, False)
    dtt = _softplus(dtt_ref[...] + bc_ref[...])
    acum_t = _exact_dot(dtt * (-jnp.exp(ac_ref[...])), upper_b, 1, 0, True)
    return lower, upper_b, zr, dtc, a_row, acum, acum_t


def _head_terms(h, lower, dtc, acum, acum_t):
    lane = lax.broadcasted_iota(jnp.int32, (1, LANE), 1)
    sub = lax.broadcasted_iota(jnp.int32, (SSD_H, 1), 0)
    rowid = lax.broadcasted_iota(jnp.int32, (CHUNK, 1), 0)
    oh = (lane == HEAD_LANE + h).astype(F32)
    acol = jnp.sum(acum * oh, axis=1, keepdims=True)
    dcol = jnp.sum(dtc * oh, axis=1, keepdims=True)
    arow = jnp.sum(acum_t * (sub == h).astype(F32), axis=0, keepdims=True)
    alast = jnp.sum(jnp.where(rowid == CHUNK - 1, acol, 0.0), axis=0, keepdims=True)
    decay = jnp.exp(jnp.where(lower, acol - arow, -jnp.inf))
    return oh, acol, dcol, alast, decay


def _hs(h):
    return slice(h * SSD_P, (h + 1) * SSD_P)


def _gs(off, g):
    return slice(off + g * SSD_N, off + (g + 1) * SSD_N)


def _ssd_in_specs(rev):
    def ci(c):
        return c if rev is None else rev - c
    return [pl.BlockSpec((CHUNK, CONV_DIM), lambda c: (ci(c), 0)),
            pl.BlockSpec((CHUNK, LANE), lambda c: (ci(c), 0)),
            pl.BlockSpec((SSD_H, CHUNK), lambda c: (0, ci(c))),
            pl.BlockSpec((1, LANE), lambda c: (0, 0)), pl.BlockSpec((SSD_H, 1), lambda c: (0, 0)),
            pl.BlockSpec((1, LANE), lambda c: (0, 0)), pl.BlockSpec((SSD_H, 1), lambda c: (0, 0)),
            pl.BlockSpec((SSD_H, 1, SSD_P), lambda c: (0, 0, 0))]


def _ssd_fwd(xbc, small, dtt, bias_r, bias_c, alog_r, alog_c, dsk, name="ssd_fwd"):
    s_ = xbc.shape[0]
    nc = s_ // CHUNK

    def body(x_ref, dt_ref, dtt_ref, br_ref, bc_ref, ar_ref, ac_ref, dsk_ref, y_ref, prev_ref, state):
        @pl.when(pl.program_id(0) == 0)
        def _():
            state[...] = jnp.zeros_like(state)

        lower, _, _, dtc, _, acum, acum_t = _ssd_chunk_common(dt_ref, dtt_ref, br_ref, bc_ref, ar_ref, ac_ref)
        for g in range(SSD_G):
            bb = _mx(x_ref[:, _gs(B_OFF, g)])
            cb_ = _mx(x_ref[:, _gs(C_OFF, g)])
            cbm = _dot(cb_, bb, 1, 1)
            for e in range(SSD_E):
                h = g * SSD_E + e
                _, acol, dcol, alast, decay = _head_terms(h, lower, dtc, acum, acum_t)
                x = x_ref[:, _hs(h)]
                xdt = x * dcol
                yd = _dot(_mx(cbm * decay), _mx(xdt), 1, 0)
                prev = state[h]
                prev_ref[0, h] = prev
                yo = _dot(cb_, _mx(prev), 1, 1) * jnp.exp(acol)
                ds = jnp.exp(alast - acol)
                st = _dot(_mx(xdt * ds), bb, 0, 0)
                state[h] = prev * jnp.exp(alast) + st
                y_ref[:, _hs(h)] = yd + yo + x * dsk_ref[h]

    psp = pl.BlockSpec((1, SSD_H, SSD_P, SSD_N), lambda c: (c, 0, 0, 0))
    return pl.pallas_call(
        body, name=name, grid=(nc,),
        in_specs=_ssd_in_specs(None), out_specs=[pl.BlockSpec((CHUNK, SSD_W), lambda c: (c, 0)), psp],
        out_shape=[jax.ShapeDtypeStruct((s_, SSD_W), F32),
                   jax.ShapeDtypeStruct((nc, SSD_H, SSD_P, SSD_N), F32)],
        scratch_shapes=[pltpu.VMEM((SSD_H, SSD_P, SSD_N), F32)],
        compiler_params=_params("arbitrary"),
    )(xbc, small, dtt, bias_r, bias_c, alog_r, alog_c, dsk)


def _ssd_bwd(xbc, small, dtt, bias_r, bias_c, alog_r, alog_c, dsk, prev, dy, name="ssd_bwd"):
    s_ = xbc.shape[0]
    nc = s_ // CHUNK

    def body(x_ref, dt_ref, dtt_ref, br_ref, bc_ref, ar_ref, ac_ref, dsk_ref, prev_ref, dy_ref,
             dx_ref, ddt_ref, dpar_ref, dstate):
        @pl.when(pl.program_id(0) == 0)
        def _():
            dstate[...] = jnp.zeros_like(dstate)
            dpar_ref[...] = jnp.zeros_like(dpar_ref)

        lower, upper_b, zr, dtc, a_row, acum, acum_t = _ssd_chunk_common(
            dt_ref, dtt_ref, br_ref, bc_ref, ar_ref, ac_ref)
        strict = (lax.broadcasted_iota(jnp.int32, (CHUNK, CHUNK), 1)
                  < lax.broadcasted_iota(jnp.int32, (CHUNK, CHUNK), 0))
        strict_b = strict.astype(BF16)
        da_in = jnp.zeros((CHUNK, LANE), F32)
        r_off = jnp.zeros((CHUNK, LANE), F32)
        c_int = jnp.zeros((CHUNK, LANE), F32)
        c_row = jnp.zeros((1, LANE), F32)
        ddt = jnp.zeros((CHUNK, LANE), F32)
        dskip = jnp.zeros((1, LANE), F32)
        for g in range(SSD_G):
            bb = _mx(x_ref[:, _gs(B_OFF, g)])
            cb_ = _mx(x_ref[:, _gs(C_OFF, g)])
            cbm = _dot(cb_, bb, 1, 1)
            dcb = jnp.zeros((CHUNK, CHUNK), F32)
            dc_acc = jnp.zeros((CHUNK, SSD_N), F32)
            db_acc = jnp.zeros((CHUNK, SSD_N), F32)
            for e in range(SSD_E):
                h = g * SSD_E + e
                oh, acol, dcol, alast, decay = _head_terms(h, lower, dtc, acum, acum_t)
                x = x_ref[:, _hs(h)]
                dy = dy_ref[:, _hs(h)]
                xdt = x * dcol
                eacol = jnp.exp(acol)
                ds = jnp.exp(alast - acol)
                dyb = _mx(dy)
                dsh = dstate[h]
                dshb = _mx(dsh)
                prev = prev_ref[0, h]
                prevb = _mx(prev)
                dxdt_inter = ds * _dot(bb, dshb, 1, 1)
                dxdt = _dot(_mx(cbm * decay), dyb, 0, 0) + dxdt_inter
                dwl = _dot(dyb, _mx(xdt), 1, 1) * decay
                dcb = dcb + dwl
                dc_acc = dc_acc + eacol * _dot(dyb, prevb, 1, 0)
                db_acc = db_acc + _dot(_mx(xdt * ds), dshb, 1, 0)
                dstate[h] = _dot(_mx(dy * eacol), cb_, 0, 0) + jnp.exp(alast) * dsh
                above = _exact_dot(upper_b, dwl * cbm, 1, 0, False)
                da_in = da_in + jnp.sum(jnp.where(strict, above, 0.0), axis=1, keepdims=True) * oh
                y_off = _dot(cb_, prevb, 1, 1) * eacol
                r_off = r_off + jnp.sum(dy * y_off, axis=1, keepdims=True) * oh
                c_int = c_int + jnp.sum(xdt * dxdt_inter, axis=1, keepdims=True) * oh
                both = jnp.sum(jnp.sum(dsh * prev, axis=1, keepdims=True), axis=0, keepdims=True)
                c_row = c_row + jnp.exp(alast) * both * oh
                dk = dsk_ref[h]
                ddt = ddt + jnp.sum(dxdt * x, axis=1, keepdims=True) * oh
                dx_ref[:, _hs(h)] = dxdt * dcol + dy * dk
                dskip = dskip + jnp.sum(jnp.sum(dy * x, axis=1, keepdims=True), axis=0, keepdims=True) * oh
            dcbb = _mx(dcb)
            dx_ref[:, _gs(C_OFF, g)] = dc_acc + _dot(dcbb, bb, 1, 0)
            dx_ref[:, _gs(B_OFF, g)] = db_acc + _dot(dcbb, cb_, 0, 0)
        da = (da_in + _exact_dot(upper_b, r_off, 1, 0, False) + _exact_dot(strict_b, c_int, 1, 0, False) + c_row)
        draw = (ddt + da * a_row) * _sigmoid(zr)
        ddt_ref[...] = draw
        dpar_ref[0:1, :] += jnp.sum(draw, axis=0, keepdims=True)
        dpar_ref[1:2, :] += jnp.sum(da * dtc, axis=0, keepdims=True) * a_row
        dpar_ref[2:3, :] += dskip

    rev = nc - 1
    psp = pl.BlockSpec((1, SSD_H, SSD_P, SSD_N), lambda c: (rev - c, 0, 0, 0))
    return pl.pallas_call(
        body, name=name, grid=(nc,),
        in_specs=_ssd_in_specs(rev) + [psp, pl.BlockSpec((CHUNK, SSD_W), lambda c: (rev - c, 0))],
        out_specs=[pl.BlockSpec((CHUNK, CONV_DIM), lambda c: (rev - c, 0)),
                   pl.BlockSpec((CHUNK, LANE), lambda c: (rev - c, 0)), pl.BlockSpec((8, LANE), lambda c: (0, 0))],
        out_shape=[jax.ShapeDtypeStruct((s_, CONV_DIM), F32), jax.ShapeDtypeStruct((s_, LANE), F32),
                   jax.ShapeDtypeStruct((8, LANE), F32)],
        scratch_shapes=[pltpu.VMEM((SSD_H, SSD_P, SSD_N), F32)],
        compiler_params=_params("arbitrary"),
    )(xbc, small, dtt, bias_r, bias_c, alog_r, alog_c, dsk, prev, dy)


GN = SSD_W // SSD_G


def _gated_norm_fwd(y, z, w, name="gated_norm_fwd"):
    s_, f = y.shape
    tr = _row_tile(s_)

    def body(y_ref, z_ref, w_ref, o_ref):
        for g in range(SSD_G):
            sl = slice(g * GN, (g + 1) * GN)
            gg = y_ref[:, sl] * _silu(z_ref[:, sl])
            r = lax.rsqrt(jnp.mean(gg * gg, axis=-1, keepdims=True) + EPS)
            o_ref[:, sl] = (gg * r * w_ref[:, sl]).astype(o_ref.dtype)

    row = pl.BlockSpec((tr, f), lambda i: (i, 0))
    wsp = pl.BlockSpec((1, f), lambda i: (0, 0))
    return pl.pallas_call(
        body, name=name, grid=(s_ // tr,), in_specs=[row, row, wsp], out_specs=row,
        out_shape=jax.ShapeDtypeStruct((s_, f), MXU_DTYPE), compiler_params=_params("parallel"),
    )(y, z, w.reshape(1, f))


def _gated_norm_bwd(y, z, w, dout, name="gated_norm_bwd"):
    s_, f = y.shape
    tr = _row_tile(s_)

    def body(y_ref, z_ref, w_ref, do_ref, dy_ref, dz_ref, dw_ref):
        @pl.when(pl.program_id(0) == 0)
        def _():
            dw_ref[...] = jnp.zeros_like(dw_ref)

        for g in range(SSD_G):
            sl = slice(g * GN, (g + 1) * GN)
            yv = y_ref[:, sl]
            zv = z_ref[:, sl]
            dov = do_ref[:, sl].astype(F32)
            sz = _silu(zv)
            gg = yv * sz
            r = lax.rsqrt(jnp.mean(gg * gg, axis=-1, keepdims=True) + EPS)
            gw = dov * w_ref[:, sl]
            c = jnp.mean(gw * gg, axis=-1, keepdims=True)
            dgg = r * gw - gg * (r * r * r * c)
            dy_ref[:, sl] = dgg * sz
            dz_ref[:, sl] = (dgg * yv * _dsilu(zv)).astype(dz_ref.dtype)
            dw_ref[:, sl] += jnp.sum(dov * gg * r, axis=0, keepdims=True)

    row = pl.BlockSpec((tr, f), lambda i: (i, 0))
    wsp = pl.BlockSpec((1, f), lambda i: (0, 0))
    return pl.pallas_call(
        body, name=name, grid=(s_ // tr,), in_specs=[row, row, wsp, row], out_specs=[row, row, wsp],
        out_shape=[jax.ShapeDtypeStruct((s_, f), F32), jax.ShapeDtypeStruct((s_, f), MXU_DTYPE),
                   jax.ShapeDtypeStruct((1, f), F32)],
        compiler_params=_params("arbitrary"),
    )(y, z, w.reshape(1, f), dout)


def _swiglu_fwd(g, u, name="swiglu_fwd"):
    nb, s_, f = g.shape
    tr = _row_tile(s_)

    def body(g_ref, u_ref, o_ref):
        o_ref[...] = (_silu(g_ref[...]) * u_ref[...]).astype(o_ref.dtype)

    sp = pl.BlockSpec((None, tr, f), lambda j, i: (j, i, 0))
    return pl.pallas_call(
        body, name=name, grid=(nb, s_ // tr), in_specs=[sp, sp], out_specs=sp,
        out_shape=jax.ShapeDtypeStruct(g.shape, MXU_DTYPE), compiler_params=_params("parallel", "parallel"),
    )(g, u)


def _swiglu_bwd(g, u, da, name="swiglu_bwd"):
    nb, s_, f = g.shape
    tr = _row_tile(s_)

    def body(g_ref, u_ref, da_ref, dg_ref, du_ref):
        gv = g_ref[...]
        dav = da_ref[...]
        dg_ref[...] = (dav * u_ref[...] * _dsilu(gv)).astype(dg_ref.dtype)
        du_ref[...] = (dav * _silu(gv)).astype(du_ref.dtype)

    sp = pl.BlockSpec((None, tr, f), lambda j, i: (j, i, 0))
    return pl.pallas_call(
        body, name=name, grid=(nb, s_ // tr), in_specs=[sp, sp, sp], out_specs=[sp, sp],
        out_shape=[jax.ShapeDtypeStruct(g.shape, MXU_DTYPE)] * 2, compiler_params=_params("parallel", "parallel"),
    )(g, u, da)


def _adam_math(g, w, m, v):
    m2 = ADAM_B1 * m + (1.0 - ADAM_B1) * g
    v2 = ADAM_B2 * v + (1.0 - ADAM_B2) * (g * g)
    m_hat = m2 / (1.0 - ADAM_B1 ** ADAM_STEP)
    v_hat = v2 / (1.0 - ADAM_B2 ** ADAM_STEP)
    delta = -ADAM_LR * (m_hat / (jnp.sqrt(v_hat) + ADAM_EPS) + ADAM_WD * w)
    return delta, m2, v2


def _adamw(parts, w, m, v, name="adamw"):
    nd, r_, c = parts.shape
    tr = _pick(r_, (128, 64, 32, 16, 8))

    def body(p_ref, w_ref, m_ref, v_ref, g_ref, d_ref, m2_ref, v2_ref):
        g = p_ref[0].astype(F32)
        for i in range(1, nd):
            g = g + p_ref[i].astype(F32)
        delta, m2, v2 = _adam_math(g, w_ref[...], m_ref[...], v_ref[...])
        g_ref[...] = g
        d_ref[...] = delta
        m2_ref[...] = m2
        v2_ref[...] = v2

    row = pl.BlockSpec((tr, c), lambda i: (i, 0))
    psp = pl.BlockSpec((nd, tr, c), lambda i: (0, i, 0))
    return pl.pallas_call(
        body, name=name, grid=(r_ // tr,), in_specs=[psp, row, row, row], out_specs=[row] * 4,
        out_shape=[jax.ShapeDtypeStruct((r_, c), F32)] * 4, compiler_params=_params("parallel"),
    )(parts, w, m, v)


def _adamw_small(parts, w, m, v, name="adamw_small"):
    nd = parts.shape[0]

    def body(p_ref, w_ref, m_ref, v_ref, g_ref, d_ref, m2_ref, v2_ref):
        g = p_ref[0]
        for i in range(1, nd):
            g = g + p_ref[i]
        delta, m2, v2 = _adam_math(g, w_ref[...], m_ref[...], v_ref[...])
        g_ref[...] = g
        d_ref[...] = delta
        m2_ref[...] = m2
        v2_ref[...] = v2

    return pl.pallas_call(
        body, name=name, out_shape=[jax.ShapeDtypeStruct(w.shape, F32)] * 4,
        compiler_params=pltpu.CompilerParams(vmem_limit_bytes=VMEM_LIMIT_BYTES),
    )(parts, w, m, v)


_HBM = pl.BlockSpec(memory_space=pltpu.HBM)
_MESH = pl.DeviceIdType.MESH


def _all_gather(xs, name):
    na = len(xs)

    def body(*refs):
        x_refs, out_refs = refs[:na], refs[na:2 * na]
        send_sems, recv_sems, local_sems = refs[2 * na:]
        x, y, c = lax.axis_index("x"), lax.axis_index("y"), lax.axis_index("c")
        me, sibling = (x, y, c), (x, y, 1 - c)
        chips = [(1 - x, y), (x, 1 - y), (1 - x, 1 - y)]

        def slot(a, px, py, pc):
            return out_refs[a].at[4 * px + 2 * py + pc]

        def copy(a, k, block, to, src=None):
            return pltpu.make_async_remote_copy(
                src_ref=slot(a, *block) if src is None else src, dst_ref=slot(a, *block),
                send_sem=send_sems.at[a, k], recv_sem=recv_sems.at[a, k], device_id=to, device_id_type=_MESH)

        mine = [pltpu.make_async_copy(x_refs[a], slot(a, *me), local_sems.at[a]) for a in range(na)]
        started = []
        for a in range(na):
            mine[a].start()
            first = [copy(a, 0, me, sibling, src=x_refs[a])]
            first += [copy(a, 1 + j, me, (*chip, c), src=x_refs[a]) for j, chip in enumerate(chips)]
            for cp in first:
                cp.start()
            started += first
        for a in range(na):
            for j, chip in enumerate(chips):
                copy(a, 1 + j, (*chip, c), me).wait_recv()
                fwd = copy(a, 4 + j, (*chip, c), sibling)
                fwd.start()
                started.append(fwd)
        for a in range(na):
            copy(a, 0, sibling, me).wait_recv()
            for j, chip in enumerate(chips):
                copy(a, 4 + j, (*chip, 1 - c), me).wait_recv()
        for cp in started:
            cp.wait_send()
        for cp in mine:
            cp.wait()

    return pl.pallas_call(
        body, name=name, out_shape=[jax.ShapeDtypeStruct((N_DEV,) + t.shape, t.dtype) for t in xs],
        in_specs=[_HBM] * na, out_specs=[_HBM] * na,
        scratch_shapes=[pltpu.SemaphoreType.DMA((na, 7)), pltpu.SemaphoreType.DMA((na, 7)),
                        pltpu.SemaphoreType.DMA((na,))],
    )(*xs)


def _all_to_all(srcs, name):
    na = len(srcs)

    def body(*refs):
        src_refs, out_refs = refs[:na], refs[na:2 * na]
        send_sems, recv_sems, local_sems = refs[2 * na:]
        x, y, c = lax.axis_index("x"), lax.axis_index("y"), lax.axis_index("c")
        me = 4 * x + 2 * y + c
        mine = [pltpu.make_async_copy(src_refs[a].at[me], out_refs[a].at[me], local_sems.at[a]) for a in range(na)]
        copies = []
        for a in range(na):
            mine[a].start()
            for k in range(1, N_DEV):
                px = 1 - x if k & 4 else x
                py = 1 - y if k & 2 else y
                pc = 1 - c if k & 1 else c
                peer = 4 * px + 2 * py + pc
                send = pltpu.make_async_remote_copy(
                    src_ref=src_refs[a].at[peer], dst_ref=out_refs[a].at[me], send_sem=send_sems.at[a, k - 1],
                    recv_sem=recv_sems.at[a, k - 1], device_id=(px, py, pc), device_id_type=_MESH)
                recv = pltpu.make_async_remote_copy(
                    src_ref=src_refs[a].at[peer], dst_ref=out_refs[a].at[peer], send_sem=send_sems.at[a, k - 1],
                    recv_sem=recv_sems.at[a, k - 1], device_id=(px, py, pc), device_id_type=_MESH)
                send.start()
                copies.append((send, recv))
        for send, recv in copies:
            recv.wait_recv()
        for send, recv in copies:
            send.wait_send()
        for cp in mine:
            cp.wait()

    return pl.pallas_call(
        body, name=name, out_shape=[jax.ShapeDtypeStruct(t.shape, t.dtype) for t in srcs],
        in_specs=[_HBM] * na, out_specs=[_HBM] * na,
        scratch_shapes=[pltpu.SemaphoreType.DMA((na, 7)), pltpu.SemaphoreType.DMA((na, 7)),
                        pltpu.SemaphoreType.DMA((na,))],
    )(*srcs)


_SEM = pl.BlockSpec(memory_space=pltpu.SEMAPHORE)
_EFFECT = pltpu.SideEffectType.DATAFLOW_SIDE_EFFECTING


def _peers(x, y, c):
    out = []
    for k in range(1, N_DEV):
        px = 1 - x if k & 4 else x
        py = 1 - y if k & 2 else y
        pc = 1 - c if k & 1 else c
        out.append(((px, py, pc), 4 * px + 2 * py + pc))
    return out


def _push_copies(scatter, src_refs, land_refs, send_sems, recv_sems):
    x, y, c = lax.axis_index("x"), lax.axis_index("y"), lax.axis_index("c")
    me = 4 * x + 2 * y + c
    pairs = []
    for a, (src, land) in enumerate(zip(src_refs, land_refs)):
        for k, (peer, slot) in enumerate(_peers(x, y, c)):
            out_src = src.at[slot] if scatter else src
            si = a * (N_DEV - 1) + k
            send = pltpu.make_async_remote_copy(src_ref=out_src, dst_ref=land.at[me], send_sem=send_sems.at[si],
                                                recv_sem=recv_sems.at[si], device_id=peer, device_id_type=_MESH)
            recv = pltpu.make_async_remote_copy(src_ref=out_src, dst_ref=land.at[slot], send_sem=send_sems.at[si],
                                                recv_sem=recv_sems.at[si], device_id=peer, device_id_type=_MESH)
            pairs.append((send, recv))
    return pairs


def _push_start(srcs, scatter, dep, name):
    na = len(srcs)
    shapes = [t.shape[1:] if scatter else t.shape for t in srcs]
    lands = [pltpu.with_memory_space_constraint(lax.empty((N_DEV,) + s, t.dtype), pltpu.HBM) for s, t in zip(shapes, srcs)]

    def body(*refs):
        src_refs, land_refs = refs[:na], refs[na:2 * na]
        send_sems, recv_sems = refs[2 * na + 1], refs[2 * na + 2]
        token = refs[-1]
        for send, _ in _push_copies(scatter, src_refs, land_refs, send_sems, recv_sems):
            send.start()
        token[...] = jnp.zeros_like(token)

    sem = pltpu.SemaphoreType.DMA((na * (N_DEV - 1),))
    outs = pl.pallas_call(
        body, name=name,
        out_shape=(sem, sem) + tuple(pltpu.HBM(t.shape, t.dtype) for t in srcs)
        + tuple(pltpu.HBM(t.shape, t.dtype) for t in lands) + (jax.ShapeDtypeStruct((8, LANE), F32),),
        in_specs=[_HBM] * (2 * na) + [pl.BlockSpec(memory_space=pl.ANY)],
        out_specs=(_SEM, _SEM) + (_HBM,) * (2 * na) + (pl.BlockSpec(memory_space=pltpu.VMEM),),
        input_output_aliases={i: 2 + i for i in range(2 * na)},
        compiler_params=pltpu.CompilerParams(has_side_effects=_EFFECT),
    )(*[pltpu.with_memory_space_constraint(t, pltpu.HBM) for t in srcs], *lands, dep)
    return outs[0], outs[1], outs[2:2 + na], outs[2 + na:2 + 2 * na], outs[-1]


def _push_wait(send_sems, recv_sems, src_thru, land_thru, scatter, after, name):
    na = len(src_thru)

    def body(*refs):
        src_refs, land_refs = refs[:na], refs[na:2 * na]
        ssem, rsem = refs[2 * na], refs[2 * na + 1]
        for send, recv in _push_copies(scatter, src_refs, land_refs, ssem, rsem):
            send.wait_send()
            recv.wait_recv()

    outs = pl.pallas_call(
        body, name=name,
        out_shape=tuple(pltpu.HBM(t.shape, t.dtype) for t in src_thru) + tuple(pltpu.HBM(t.shape, t.dtype) for t in land_thru),
        in_specs=[_HBM] * (2 * na) + [_SEM, _SEM, pl.BlockSpec(memory_space=pl.ANY)],
        out_specs=(_HBM,) * (2 * na),
        input_output_aliases={i: i for i in range(2 * na)},
        compiler_params=pltpu.CompilerParams(has_side_effects=_EFFECT),
    )(*src_thru, *land_thru, send_sems, recv_sems, after)
    return outs[:na], outs[na:]


def _exchange_behind(srcs, scatter, dep, name):
    send_sems, recv_sems, thru, lands, token = _push_start(srcs, scatter, dep, name + "_start")

    def finish(after):
        src_done, land_done = _push_wait(send_sems, recv_sems, thru, lands, scatter, after, name + "_wait")
        return _place_own(land_done, src_done, scatter, name + "_own")

    return token[0, 0], finish


def _place_own(lands, srcs, scatter, name):
    na = len(lands)

    def body(*refs):
        land_in, src_refs, land_out = refs[:na], refs[na:2 * na], refs[2 * na:3 * na]
        sems = refs[-1]
        me = 4 * lax.axis_index("x") + 2 * lax.axis_index("y") + lax.axis_index("c")
        cps = [pltpu.make_async_copy(src_refs[a].at[me] if scatter else src_refs[a], land_out[a].at[me], sems.at[a])
               for a in range(na)]
        for cp in cps:
            cp.start()
        for cp in cps:
            cp.wait()

    return pl.pallas_call(
        body, name=name, out_shape=[jax.ShapeDtypeStruct(t.shape, t.dtype) for t in lands],
        in_specs=[_HBM] * (2 * na), out_specs=[_HBM] * na,
        input_output_aliases={i: i for i in range(na)},
        scratch_shapes=[pltpu.SemaphoreType.DMA((na,))],
    )(*lands, *srcs)


_BIG = (("w_in", D_MODEL, D_IN, 1), ("w_uq", Q_RANK, HEADS * QK, 1), ("w_ukv", KV_RANK, HEADS * (NOPE + VDIM), 1),
        ("w_out", D_MODEL, D_MODEL, 0), ("w_gate", D_MODEL, D_FF, 1), ("w_up", D_MODEL, D_FF, 1),
        ("w_down", D_FF, D_MODEL, 0))
_CQKV = (0, Q_RANK + KV_RANK)
_KR = (_CQKV[1], _CQKV[1] + ROPE)
_Z = (_KR[1], _KR[1] + SSD_W)
_XBC = (_Z[1], _Z[1] + CONV_DIM)
_DT = (_XBC[1], _XBC[1] + SSD_H)


def _win_segments(w_in_g):
    w = jnp.transpose(w_in_g, (1, 0, 2)).reshape(D_MODEL, D_IN)
    small = jnp.concatenate([w[:, _KR[0]:_KR[1]], w[:, _DT[0]:_DT[1]],
                             jnp.zeros((D_MODEL, LANE - ROPE - SSD_H), w.dtype)], axis=1)
    return w[:, _CQKV[0]:_CQKV[1]], w[:, _Z[0]:_Z[1]], w[:, _XBC[0]:_XBC[1]], small


def _win_from_segments(g_cqkv, g_z, g_xbc, g_small):
    w = jnp.concatenate([g_cqkv, g_small[:, :ROPE], g_z, g_xbc, g_small[:, ROPE:ROPE + SSD_H]], axis=1)
    return jnp.transpose(w.reshape(D_MODEL, N_DEV, D_IN // N_DEV), (1, 0, 2))


_SMALL = (("q_norm_w", 512), ("kv_norm_w", 512), ("conv_b", CONV_DIM), ("dt_bias", SSD_H), ("a_log", SSD_H),
          ("d_skip", SSD_H), ("ssd_norm_w", SSD_W), ("attn_out_norm_w", 1024), ("pre_mix_norm_w", D_MODEL),
          ("post_mix_norm_w", D_MODEL), ("pre_ffn_norm_w", D_MODEL), ("post_ffn_norm_w", D_MODEL),
          ("conv_w", CONV_K * CONV_DIM))
_SMALL_ROWS = -(-sum(-(-n // LANE) for _, n in _SMALL) // 8) * 8


def _pack_small(vals):
    rows = []
    for name, n in _SMALL:
        v = vals[name].reshape(-1).astype(F32)
        pad = -(-n // LANE) * LANE
        rows.append(jnp.pad(v, (0, pad - n)).reshape(-1, LANE))
    m = jnp.concatenate(rows, axis=0)
    return jnp.pad(m, ((0, _SMALL_ROWS - m.shape[0]), (0, 0)))


def _unpack_small(m):
    out, r = {}, 0
    for name, n in _SMALL:
        nr = -(-n // LANE)
        out[name] = m[r:r + nr].reshape(-1)[:n]
        r += nr
    return out


def _head_row(v):
    return jnp.pad(v.reshape(1, -1).astype(F32), ((0, 0), (HEAD_LANE, LANE - HEAD_LANE - v.shape[-1])))


def _local_step(x, positions, target, wg, small, ffn_weights, on_ffn_grads):
    w_cqkv, w_z, w_xbc, w_small = _win_segments(wg["w_in"])
    w_uq, w_ukv = wg["w_uq"], wg["w_ukv"]
    w_out = wg["w_out"].reshape(D_MODEL, D_MODEL)
    w_out_a = w_out[:HEADS * VDIM].reshape(HEADS, VDIM, D_MODEL)
    w_out_s = w_out[HEADS * VDIM:]
    conv_w = wg["conv_w"]
    conv_b = small["conv_b"].reshape(1, CONV_DIM)
    qkv_norm_w = jnp.concatenate([small["q_norm_w"], small["kv_norm_w"]])
    attn_norm_w = small["attn_out_norm_w"].reshape(HEADS, 1, VDIM)
    scale = QK ** -0.5

    inv_freq = ROPE_THETA ** (-jnp.arange(0, ROPE, 2, dtype=F32) / ROPE)
    ang = positions.astype(F32)[:, None] * inv_freq
    cos2 = jnp.tile(jnp.cos(ang), (1, 2))
    sin2 = jnp.tile(jnp.sin(ang), (1, 2))

    u = _rms_fwd(x, small["pre_mix_norm_w"], out_dtype=MXU_DTYPE, name="pre_mix_norm")
    cqkv = _mm(u, w_cqkv, "nn", name="in_proj_qkv")
    z = _mm(u, w_z, "nn", name="in_proj_z")
    xbc = _mm(u, w_xbc, "nn", name="in_proj_xbc")
    sm = _mm(u, w_small, "nn", name="in_proj_small")

    qkvn = _rms_fwd(cqkv, qkv_norm_w, groups=2, out_dtype=MXU_DTYPE, name="qkv_norm")
    q = _mm(qkvn, w_uq, "nn", b_blk=True, out_blk=True, a_cols=(0, Q_RANK), name="q_up")
    kv = _mm(qkvn, w_ukv, "nn", b_blk=True, out_blk=True, a_cols=(Q_RANK, KV_RANK), name="kv_up")
    q_h = _q_prep(q, cos2, sin2, scale, name="q_prep")
    k_h, v_h = _kv_prep(kv, sm, cos2, sin2)
    o_h, lse = _flash_fwd(q_h, k_h, v_h)
    attn = _hnorm_fwd(o_h, attn_norm_w)

    xbc_act = _conv_fwd(xbc, conv_w, conv_b)
    dtt = jnp.transpose(sm[:, HEAD_LANE:HEAD_LANE + SSD_H])
    ssd_args = (xbc_act, sm, dtt, _head_row(small["dt_bias"]), small["dt_bias"].reshape(SSD_H, 1),
                _head_row(small["a_log"]), small["a_log"].reshape(SSD_H, 1),
                jnp.broadcast_to(small["d_skip"].reshape(SSD_H, 1, 1), (SSD_H, 1, SSD_P)))
    y_ssd, prev = _ssd_fwd(*ssd_args)
    ssm = _gated_norm_fwd(y_ssd, z, small["ssd_norm_w"])

    mix = _mm(attn, w_out_a, "nn", a_blk=True, b_blk=True, name="out_proj_attn")
    mix = _mm(ssm, w_out_s, "nn", add=mix, name="out_proj_ssm")
    h1 = _rms_fwd(mix, small["post_mix_norm_w"], res=x, name="post_mix_norm")

    w_gate, w_up, w_down = ffn_weights(mix)
    vv = _rms_fwd(h1, small["pre_ffn_norm_w"], out_dtype=MXU_DTYPE, name="pre_ffn_norm")
    gate = _mm(vv, w_gate, "nn", b_blk=True, out_blk=True, name="ffn_gate")
    up = _mm(vv, w_up, "nn", b_blk=True, out_blk=True, name="ffn_up")
    act = _swiglu_fwd(gate, up)
    ffn = _mm(act, w_down, "nn", a_blk=True, b_blk=True, name="ffn_down")
    loss_blk, dy, dffn, g_post_ffn = _loss_head(ffn, h1, target, small["post_ffn_norm_w"])

    dact = _mm(dffn, w_down, "nt", b_blk=True, out_blk=True, name="d_act")
    g_down = _mm(act, dffn, "tn", a_blk=True, out_blk=True, out_dtype=MXU_DTYPE, name="g_down")
    dgate, dup = _swiglu_bwd(gate, up, dact)
    dvv = _mm(dgate, w_gate, "nt", a_blk=True, b_blk=True, name="d_v_gate")
    dvv = _mm(dup, w_up, "nt", a_blk=True, b_blk=True, add=dvv, name="d_v_up")
    g_gate = _mm(vv, dgate, "tn", b_blk=True, out_blk=True, out_dtype=MXU_DTYPE, name="g_gate")
    g_up = _mm(vv, dup, "tn", b_blk=True, out_blk=True, out_dtype=MXU_DTYPE, name="g_up")
    pre_ffn_w = small["pre_ffn_norm_w"] + on_ffn_grads([g_gate, g_up, g_down])
    dh1, g_pre_ffn = _rms_bwd(h1, pre_ffn_w, [dvv], res=dy, name="pre_ffn_norm_bwd")

    dmix, g_post_mix = _rms_bwd(mix, small["post_mix_norm_w"], [dh1], out_dtype=MXU_DTYPE, name="post_mix_norm_bwd")
    dattn = _mm(dmix, w_out_a, "nt", b_blk=True, out_blk=True, name="d_attn")
    dssm = _mm(dmix, w_out_s, "nt", name="d_ssm")
    g_out_a = _mm(attn, dmix, "tn", a_blk=True, out_blk=True, out_dtype=MXU_DTYPE, name="g_out_attn")
    g_out_s = _mm(ssm, dmix, "tn", out_dtype=MXU_DTYPE, name="g_out_ssm")
    g_out = jnp.concatenate([g_out_a.reshape(HEADS * VDIM, D_MODEL), g_out_s], axis=0)

    do_h, g_attn_norm = _hnorm_bwd(o_h, attn_norm_w, dattn)
    dq_h, delta = _flash_bwd_dq(q_h, k_h, v_h, o_h, do_h, lse)
    dk_h, dv_h = _flash_bwd_dkv(q_h, k_h, v_h, do_h, lse, delta)
    dq = _q_prep(dq_h, cos2, -sin2, scale, name="dq_post")

    dy_ssd, dz, g_ssd_norm = _gated_norm_bwd(y_ssd, z, small["ssd_norm_w"], dssm)
    dxbc_act, ddt, dpar = _ssd_bwd(*ssd_args, prev, dy_ssd)
    dkv, dsm = _dkv_post(dk_h, dv_h, ddt, cos2, -sin2)
    dpre, dwb = _conv_bwd_pre(xbc, conv_w, conv_b, dxbc_act)
    dxbc = _conv_bwd_in(dpre, conv_w)

    dqn = _mm(dq, w_uq, "nt", a_blk=True, b_blk=True, name="d_qn")
    dkvn = _mm(dkv, w_ukv, "nt", a_blk=True, b_blk=True, name="d_kvn")
    g_uq = _mm(qkvn, dq, "tn", b_blk=True, out_blk=True, a_cols=(0, Q_RANK), out_dtype=MXU_DTYPE, name="g_uq")
    g_ukv = _mm(qkvn, dkv, "tn", b_blk=True, out_blk=True, a_cols=(Q_RANK, KV_RANK), out_dtype=MXU_DTYPE, name="g_ukv")
    dcqkv, g_qkv_norm = _rms_bwd(cqkv, qkv_norm_w, [dqn, dkvn], out_dtype=MXU_DTYPE, name="qkv_norm_bwd")

    du = _mm(dcqkv, w_cqkv, "nt", name="d_u_qkv")
    du = _mm(dz, w_z, "nt", add=du, name="d_u_z")
    du = _mm(dxbc, w_xbc, "nt", add=du, name="d_u_xbc")
    du = _mm(dsm, w_small, "nt", add=du, name="d_u_small")
    g_in = _win_from_segments(_mm(u, dcqkv, "tn", out_dtype=MXU_DTYPE, name="g_in_qkv"),
                              _mm(u, dz, "tn", out_dtype=MXU_DTYPE, name="g_in_z"),
                              _mm(u, dxbc, "tn", out_dtype=MXU_DTYPE, name="g_in_xbc"),
                              _mm(u, dsm, "tn", out_dtype=MXU_DTYPE, name="g_in_small"))
    dx, g_pre_mix = _rms_bwd(x, small["pre_mix_norm_w"], [du], res=dh1, name="pre_mix_norm_bwd")

    g_big = {"w_in": g_in, "w_uq": g_uq, "w_ukv": g_ukv, "w_out": g_out.reshape(N_DEV, D_MODEL // N_DEV, D_MODEL)}
    hl = slice(HEAD_LANE, HEAD_LANE + SSD_H)
    g_small = {"q_norm_w": g_qkv_norm[0, :Q_RANK], "kv_norm_w": g_qkv_norm[0, Q_RANK:], "conv_b": dwb[CONV_K],
               "dt_bias": dpar[0, hl], "a_log": dpar[1, hl], "d_skip": dpar[2, hl], "ssd_norm_w": g_ssd_norm,
               "attn_out_norm_w": g_attn_norm, "pre_mix_norm_w": g_pre_mix, "post_mix_norm_w": g_post_mix,
               "pre_ffn_norm_w": g_pre_ffn, "post_ffn_norm_w": g_post_ffn, "conv_w": dwb[:CONV_K]}
    return loss_blk[0, 0], dx, g_big, g_small


_WEIGHT_ORDER = ("w_in", "q_norm_w", "w_uq", "kv_norm_w", "w_ukv", "conv_w", "conv_b", "dt_bias", "a_log", "d_skip",
                 "ssd_norm_w", "attn_out_norm_w", "w_out", "pre_mix_norm_w", "post_mix_norm_w", "pre_ffn_norm_w",
                 "post_ffn_norm_w", "w_gate", "w_up", "w_down")


def kernel(x, positions, w_in, q_norm_w, w_uq, kv_norm_w, w_ukv, conv_w, conv_b, dt_bias, a_log, d_skip, ssd_norm_w, attn_out_norm_w, w_out, pre_mix_norm_w, post_mix_norm_w, pre_ffn_norm_w, post_ffn_norm_w, w_gate, w_up, w_down, loss_target, m_w_in, m_q_norm_w, m_w_uq, m_kv_norm_w, m_w_ukv, m_conv_w, m_conv_b, m_dt_bias, m_a_log, m_d_skip, m_ssd_norm_w, m_attn_out_norm_w, m_w_out, m_pre_mix_norm_w, m_post_mix_norm_w, m_pre_ffn_norm_w, m_post_ffn_norm_w, m_w_gate, m_w_up, m_w_down, v_w_in, v_q_norm_w, v_w_uq, v_kv_norm_w, v_w_ukv, v_conv_w, v_conv_b, v_dt_bias, v_a_log, v_d_skip, v_ssd_norm_w, v_attn_out_norm_w, v_w_out, v_pre_mix_norm_w, v_post_mix_norm_w, v_pre_ffn_norm_w, v_post_ffn_norm_w, v_w_gate, v_w_up, v_w_down):
    w = dict(w_in=w_in, q_norm_w=q_norm_w, w_uq=w_uq, kv_norm_w=kv_norm_w, w_ukv=w_ukv, conv_w=conv_w, conv_b=conv_b,
             dt_bias=dt_bias, a_log=a_log, d_skip=d_skip, ssd_norm_w=ssd_norm_w, attn_out_norm_w=attn_out_norm_w,
             w_out=w_out, pre_mix_norm_w=pre_mix_norm_w, post_mix_norm_w=post_mix_norm_w,
             pre_ffn_norm_w=pre_ffn_norm_w, post_ffn_norm_w=post_ffn_norm_w, w_gate=w_gate, w_up=w_up, w_down=w_down)
    m = dict(w_in=m_w_in, q_norm_w=m_q_norm_w, w_uq=m_w_uq, kv_norm_w=m_kv_norm_w, w_ukv=m_w_ukv, conv_w=m_conv_w,
             conv_b=m_conv_b, dt_bias=m_dt_bias, a_log=m_a_log, d_skip=m_d_skip, ssd_norm_w=m_ssd_norm_w,
             attn_out_norm_w=m_attn_out_norm_w, w_out=m_w_out, pre_mix_norm_w=m_pre_mix_norm_w,
             post_mix_norm_w=m_post_mix_norm_w, pre_ffn_norm_w=m_pre_ffn_norm_w, post_ffn_norm_w=m_post_ffn_norm_w,
             w_gate=m_w_gate, w_up=m_w_up, w_down=m_w_down)
    v = dict(w_in=v_w_in, q_norm_w=v_q_norm_w, w_uq=v_w_uq, kv_norm_w=v_kv_norm_w, w_ukv=v_w_ukv, conv_w=v_conv_w,
             conv_b=v_conv_b, dt_bias=v_dt_bias, a_log=v_a_log, d_skip=v_d_skip, ssd_norm_w=v_ssd_norm_w,
             attn_out_norm_w=v_attn_out_norm_w, w_out=v_w_out, pre_mix_norm_w=v_pre_mix_norm_w,
             post_mix_norm_w=v_post_mix_norm_w, pre_ffn_norm_w=v_pre_ffn_norm_w, post_ffn_norm_w=v_post_ffn_norm_w,
             w_gate=v_w_gate, w_up=v_w_up, w_down=v_w_down)
    w, m, v = ({k: t[0] for k, t in d.items()} for d in (w, m, v))
    me = 4 * lax.axis_index("x") + 2 * lax.axis_index("y") + lax.axis_index("c")
    mix_names = ("w_in", "w_uq", "w_ukv", "w_out")
    ffn_names = ("w_gate", "w_up", "w_down")
    cshard = CONV_DIM // N_DEV

    shards = [w[name].astype(MXU_DTYPE) for name in mix_names]
    shards.append(jnp.stack(_split3(w["conv_w"])).reshape(3 * CONV_K, cshard).astype(MXU_DTYPE))
    gathered = _all_gather(shards, name="gather_weights")
    wg = dict(zip(mix_names, gathered[:-1]))
    cw = gathered[-1].astype(F32).reshape(N_DEV, 3, CONV_K, cshard)
    wg["conv_w"] = jnp.transpose(cw[:, 0] + cw[:, 1] + cw[:, 2], (1, 0, 2)).reshape(CONV_K, CONV_DIM)
    ffn_token, ffn_weights = _exchange_behind([w[name].astype(MXU_DTYPE) for name in ffn_names], False,
                                              gathered[-1], "ffn_weights")
    small = {name: w[name] for name, _ in _SMALL if name != "conv_w"}
    small["pre_mix_norm_w"] = small["pre_mix_norm_w"] + ffn_token

    ffn_grads = {}

    def on_ffn_grads(gs):
        token, ffn_grads["finish"] = _exchange_behind(gs, True, jnp.zeros((8, LANE), F32), "ffn_grads")
        return token

    loss_local, dx, g_big, g_small = _local_step(x[0], positions[0], loss_target[0], wg, small, ffn_weights,
                                                 on_ffn_grads)
    loss = lax.psum(loss_local, ("x", "y", "c"))

    recv = dict(zip(mix_names, _all_to_all([g_big[name] for name in mix_names], name="exchange_grads")))
    recv.update(zip(ffn_names, ffn_grads["finish"](dx)))
    grads, deltas, new_m, new_v = {}, {}, {}, {}
    for name, parts in recv.items():
        grads[name], deltas[name], new_m[name], new_v[name] = _adamw(parts, w[name], m[name], v[name],
                                                                     name="adamw_" + name)

    def embed(t):
        return lax.dynamic_update_slice(jnp.zeros((CONV_K, CONV_DIM), F32), t, (0, me * cshard))

    parts_s = _all_gather([_pack_small(g_small)], name="gather_small_grads")[0]
    packs = [_pack_small({**{n_: d[n_] for n_, _ in _SMALL if n_ != "conv_w"}, "conv_w": embed(d["conv_w"])})
             for d in (w, m, v)]
    outs = [_unpack_small(t) for t in _adamw_small(parts_s, *packs)]
    for name, n in _SMALL:
        for dst, src in zip((grads, deltas, new_m, new_v), outs):
            if name == "conv_w":
                dst[name] = lax.dynamic_slice(src[name].reshape(CONV_K, CONV_DIM), (0, me * cshard), (CONV_K, cshard))
            else:
                dst[name] = src[name]

    def lead(d):
        return [d[name][None] for name in _WEIGHT_ORDER]

    return (loss, dx[None], *lead(grads), *lead(deltas), *lead(new_m), *lead(new_v))
```

```python
import numpy as np

import jax
import jax.numpy as jnp
from jax import lax
from jax.experimental import pallas as pl
from jax.experimental.pallas import tpu as pltpu

F32 = jnp.float32
BF16 = jnp.bfloat16
MXU_DTYPE = jnp.bfloat16
EPS = 1e-6
VMEM_LIMIT_BYTES = 48 * 1024 * 1024
K_TILE_MAX = 2048

N_DEV = 8
D_MODEL = 2048
Q_RANK = 512
KV_RANK = 512
ROPE = 64
HALF = ROPE // 2
HEADS = 8
NOPE = 128
VDIM = 128
QK = NOPE + ROPE
SSD_W = 1024
SSD_H = 16
SSD_P = 64
SSD_G = 2
SSD_E = SSD_H // SSD_G
SSD_N = 128
CHUNK = 128
CONV_K = 4
CONV_DIM = SSD_W + 2 * SSD_G * SSD_N
B_OFF = SSD_W
C_OFF = SSD_W + SSD_G * SSD_N
D_FF = 5632
D_IN = Q_RANK + KV_RANK + ROPE + SSD_W + CONV_DIM + SSD_H
ROPE_THETA = 10000.0
LANE = 128
HEAD_LANE = ROPE

ADAM_LR = 0.001
ADAM_B1 = 0.9
ADAM_B2 = 0.999
ADAM_EPS = 1e-08
ADAM_WD = 0.01
ADAM_STEP = 10


def _pick(n, cands):
    for c in cands:
        if n % c == 0:
            return c
    return n


def _params(*sem):
    return pltpu.CompilerParams(dimension_semantics=sem, vmem_limit_bytes=VMEM_LIMIT_BYTES)


def _sigmoid(x):
    return 1.0 / (1.0 + jnp.exp(-x))


def _silu(x):
    return x * _sigmoid(x)


def _dsilu(x):
    s = _sigmoid(x)
    return s * (1.0 + x * (1.0 - s))


def _softplus(x):
    e = jnp.exp(-jnp.abs(x))
    small = e * (1.0 - e * (0.5 - e * (1.0 / 3.0)))
    return jnp.maximum(x, 0.0) + jnp.where(e < 0.01, small, jnp.log(1.0 + e))


def _dot(a, b, ca, cb):
    return lax.dot_general(a, b, (((ca,), (cb,)), ((), ())), preferred_element_type=F32)


def _mx(v):
    return v.astype(MXU_DTYPE)


def _split3(a):
    hi = a.astype(BF16)
    r1 = a - hi.astype(F32)
    mid = r1.astype(BF16)
    lo = (r1 - mid.astype(F32)).astype(BF16)
    return hi, mid, lo


def _exact_dot(a, b, ca, cb, split_a):
    if split_a:
        return sum(_dot(p, b, ca, cb) for p in _split3(a))
    return sum(_dot(a, p, ca, cb) for p in _split3(b))


def _mm(a, b, mode, *, a_blk=False, b_blk=False, out_blk=False, a_cols=None, add=None, out_dtype=F32, name="mm"):
    a2, b2 = a.shape[-2:], b.shape[-2:]
    a_last = a2[1] if a_cols is None else a_cols[1]
    a_start = 0 if a_cols is None else a_cols[0]
    if mode == "nn":
        m, k, (k2, n) = a2[0], a_last, b2
    elif mode == "nt":
        m, k, (n, k2) = a2[0], a_last, b2
    else:
        k, m, (k2, n) = a2[0], a_last, b2
    assert k == k2, (a.shape, b.shape, mode)
    tm = _pick(m, (1024, 704, 512, 256, 128))
    tn = _pick(n, (1024, 768, 704, 512, 256, 192, 128))
    tk = k if k <= K_TILE_MAX else _pick(k, (K_TILE_MAX, 1024, 512))
    nk = k // tk
    jo = N_DEV if out_blk else 1
    jr = N_DEV if (a_blk and b_blk and not out_blk) else 1
    ca, cb = {"nn": (1, 0), "nt": (1, 1), "tn": (0, 0)}[mode]
    has_add = add is not None
    single = jr * nk == 1
    if mode == "tn":
        assert a_start % tm == 0
        a_block, a_idx = (tk, tm), (lambda i, kk: (kk, i + a_start // tm))
    else:
        assert a_start % tk == 0
        a_block, a_idx = (tm, tk), (lambda i, kk: (i, kk + a_start // tk))
    b_block, b_idx = ((tn, tk), (lambda nn_, kk: (nn_, kk))) if mode == "nt" else ((tk, tn), (lambda nn_, kk: (kk, nn_)))

    def sel(o, r):
        return o if out_blk else r

    a_spec = (pl.BlockSpec((None,) + a_block, lambda o, i, nn_, r, kk: (sel(o, r),) + a_idx(i, kk)) if a_blk
              else pl.BlockSpec(a_block, lambda o, i, nn_, r, kk: a_idx(i, kk)))
    b_spec = (pl.BlockSpec((None,) + b_block, lambda o, i, nn_, r, kk: (sel(o, r),) + b_idx(nn_, kk)) if b_blk
              else pl.BlockSpec(b_block, lambda o, i, nn_, r, kk: b_idx(nn_, kk)))
    o_spec = (pl.BlockSpec((None, tm, tn), lambda o, i, nn_, r, kk: (o, i, nn_)) if out_blk
              else pl.BlockSpec((tm, tn), lambda o, i, nn_, r, kk: (i, nn_)))

    def body(*refs):
        a_ref, b_ref = refs[0], refs[1]
        add_ref = refs[2] if has_add else None
        o_ref = refs[3] if has_add else refs[2]
        part = _dot(_mx(a_ref[...]), _mx(b_ref[...]), ca, cb)
        if single:
            if has_add:
                part = part + add_ref[...]
            o_ref[...] = part.astype(o_ref.dtype)
            return
        acc = refs[-1]
        r, kk = pl.program_id(3), pl.program_id(4)
        first = jnp.logical_and(r == 0, kk == 0)
        last = jnp.logical_and(r == jr - 1, kk == nk - 1)

        @pl.when(first)
        def _():
            acc[...] = part

        @pl.when(jnp.logical_not(first))
        def _():
            acc[...] += part

        @pl.when(last)
        def _():
            res = acc[...]
            if has_add:
                res = res + add_ref[...]
            o_ref[...] = res.astype(o_ref.dtype)

    out_shape = ((N_DEV, m, n) if out_blk else (m, n))
    return pl.pallas_call(
        body, name=name, grid=(jo, m // tm, n // tn, jr, nk),
        in_specs=[a_spec, b_spec] + ([o_spec] if has_add else []), out_specs=o_spec,
        out_shape=jax.ShapeDtypeStruct(out_shape, out_dtype),
        scratch_shapes=[] if single else [pltpu.VMEM((tm, tn), F32)],
        compiler_params=_params("parallel", "parallel", "parallel", "arbitrary", "arbitrary"),
    )(*((a, b) + ((add,) if has_add else ())))


def _row_tile(r_):
    return _pick(r_, (256, 128, 64, 32, 16, 8))


def _rms_fwd(t, w, groups=1, res=None, out_dtype=F32, name="rms_fwd"):
    r_, f = t.shape
    fg = f // groups
    tr = _row_tile(r_)
    has_res = res is not None

    def body(*refs):
        t_ref, w_ref = refs[0], refs[1]
        res_ref = refs[2] if has_res else None
        o_ref = refs[-1]
        for g in range(groups):
            sl = slice(g * fg, (g + 1) * fg)
            tv = t_ref[:, sl].astype(F32)
            r = lax.rsqrt(jnp.mean(tv * tv, axis=-1, keepdims=True) + EPS)
            y = tv * r * w_ref[:, sl]
            if has_res:
                y = y + res_ref[:, sl]
            o_ref[:, sl] = y.astype(o_ref.dtype)

    row = pl.BlockSpec((tr, f), lambda i: (i, 0))
    wsp = pl.BlockSpec((1, f), lambda i: (0, 0))
    return pl.pallas_call(
        body, name=name, grid=(r_ // tr,),
        in_specs=[row, wsp] + ([row] if has_res else []), out_specs=row,
        out_shape=jax.ShapeDtypeStruct((r_, f), out_dtype),
        compiler_params=_params("parallel"),
    )(*((t, w.reshape(1, f)) + ((res,) if has_res else ())))


def _rms_bwd(t, w, dys, res=None, out_dtype=F32, name="rms_bwd"):
    r_, f = t.shape
    groups = len(dys)
    fg = f // groups
    tr = _row_tile(r_)
    has_res = res is not None

    def body(*refs):
        t_ref, w_ref = refs[0], refs[1]
        dy_refs = refs[2:2 + groups]
        res_ref = refs[2 + groups] if has_res else None
        dt_ref, dw_ref = refs[-2], refs[-1]

        @pl.when(pl.program_id(0) == 0)
        def _():
            dw_ref[...] = jnp.zeros_like(dw_ref)

        for g in range(groups):
            sl = slice(g * fg, (g + 1) * fg)
            tv = t_ref[:, sl].astype(F32)
            dyv = dy_refs[g][...].astype(F32)
            r = lax.rsqrt(jnp.mean(tv * tv, axis=-1, keepdims=True) + EPS)
            gw = dyv * w_ref[:, sl]
            c = jnp.mean(gw * tv, axis=-1, keepdims=True)
            dt = r * gw - tv * (r * r * r * c)
            if has_res:
                dt = dt + res_ref[:, sl]
            dt_ref[:, sl] = dt.astype(dt_ref.dtype)
            dw_ref[:, sl] += jnp.sum(dyv * tv * r, axis=0, keepdims=True)

    row = pl.BlockSpec((tr, f), lambda i: (i, 0))
    grow = pl.BlockSpec((tr, fg), lambda i: (i, 0))
    wsp = pl.BlockSpec((1, f), lambda i: (0, 0))
    return pl.pallas_call(
        body, name=name, grid=(r_ // tr,),
        in_specs=[row, wsp] + [grow] * groups + ([row] if has_res else []), out_specs=[row, wsp],
        out_shape=[jax.ShapeDtypeStruct((r_, f), out_dtype), jax.ShapeDtypeStruct((1, f), F32)],
        compiler_params=_params("arbitrary"),
    )(*((t, w.reshape(1, f)) + tuple(dys) + ((res,) if has_res else ())))


def _hnorm_fwd(o, w, name="attn_out_norm"):
    h, s_, v = o.shape
    tr = _row_tile(s_)

    def body(o_ref, w_ref, y_ref):
        ss = jnp.sum(o_ref[0] * o_ref[0], axis=-1, keepdims=True)
        for i in range(1, h):
            ss = ss + jnp.sum(o_ref[i] * o_ref[i], axis=-1, keepdims=True)
        r = lax.rsqrt(ss * (1.0 / (h * v)) + EPS)
        for i in range(h):
            y_ref[i] = (o_ref[i] * r * w_ref[i]).astype(y_ref.dtype)

    blk = pl.BlockSpec((h, tr, v), lambda i: (0, i, 0))
    wsp = pl.BlockSpec((h, 1, v), lambda i: (0, 0, 0))
    return pl.pallas_call(
        body, name=name, grid=(s_ // tr,), in_specs=[blk, wsp], out_specs=blk,
        out_shape=jax.ShapeDtypeStruct(o.shape, MXU_DTYPE), compiler_params=_params("parallel"),
    )(o, w)


def _hnorm_bwd(o, w, dy, name="attn_out_norm_bwd"):
    h, s_, v = o.shape
    tr = _row_tile(s_)

    def body(o_ref, w_ref, dy_ref, do_ref, dw_ref):
        @pl.when(pl.program_id(0) == 0)
        def _():
            dw_ref[...] = jnp.zeros_like(dw_ref)

        ss = jnp.zeros((tr, 1), F32)
        cc = jnp.zeros((tr, 1), F32)
        for i in range(h):
            ov = o_ref[i]
            ss = ss + jnp.sum(ov * ov, axis=-1, keepdims=True)
            cc = cc + jnp.sum(dy_ref[i] * w_ref[i] * ov, axis=-1, keepdims=True)
        r = lax.rsqrt(ss * (1.0 / (h * v)) + EPS)
        c = cc * (1.0 / (h * v))
        for i in range(h):
            ov = o_ref[i]
            dyv = dy_ref[i]
            do_ref[i] = r * dyv * w_ref[i] - ov * (r * r * r * c)
            dw_ref[i] += jnp.sum(dyv * ov * r, axis=0, keepdims=True)

    blk = pl.BlockSpec((h, tr, v), lambda i: (0, i, 0))
    wsp = pl.BlockSpec((h, 1, v), lambda i: (0, 0, 0))
    return pl.pallas_call(
        body, name=name, grid=(s_ // tr,), in_specs=[blk, wsp, blk], out_specs=[blk, wsp],
        out_shape=[jax.ShapeDtypeStruct(o.shape, F32), jax.ShapeDtypeStruct((h, 1, v), F32)],
        compiler_params=_params("arbitrary"),
    )(o, w, dy)


def _loss_head(ffn, h1, target, w, name="loss_head"):
    r_, f = ffn.shape
    tr = _row_tile(r_)

    def body(ffn_ref, h1_ref, tg_ref, w_ref, loss_ref, dy_ref, dffn_ref, dw_ref):
        @pl.when(pl.program_id(0) == 0)
        def _():
            dw_ref[...] = jnp.zeros_like(dw_ref)
            loss_ref[...] = jnp.zeros_like(loss_ref)

        tv = ffn_ref[...]
        wv = w_ref[...]
        r = lax.rsqrt(jnp.mean(tv * tv, axis=-1, keepdims=True) + EPS)
        tn = tv * r
        e = h1_ref[...] + tn * wv - tg_ref[...]
        tot = jnp.sum(jnp.sum(e * e, axis=1, keepdims=True), axis=0, keepdims=True) * (0.5 / f)
        loss_ref[...] += tot + jnp.zeros_like(loss_ref)
        dyv = e * (1.0 / f)
        dy_ref[...] = dyv
        gw = dyv * wv
        c = jnp.mean(gw * tv, axis=-1, keepdims=True)
        dffn_ref[...] = (r * gw - tv * (r * r * r * c)).astype(dffn_ref.dtype)
        dw_ref[...] += jnp.sum(dyv * tn, axis=0, keepdims=True)

    row = pl.BlockSpec((tr, f), lambda i: (i, 0))
    wsp = pl.BlockSpec((1, f), lambda i: (0, 0))
    lsp = pl.BlockSpec((1, LANE), lambda i: (0, 0))
    return pl.pallas_call(
        body, name=name, grid=(r_ // tr,),
        in_specs=[row, row, row, wsp], out_specs=[lsp, row, row, wsp],
        out_shape=[jax.ShapeDtypeStruct((1, LANE), F32), jax.ShapeDtypeStruct((r_, f), F32),
                   jax.ShapeDtypeStruct((r_, f), MXU_DTYPE), jax.ShapeDtypeStruct((1, f), F32)],
        compiler_params=_params("arbitrary"),
    )(ffn, h1, target, w.reshape(1, f))


def _rot_matrix():
    p = np.zeros((ROPE, ROPE), np.float32)
    for i in range(HALF):
        p[i + HALF, i] = -1.0
        p[i, i + HALF] = 1.0
    return jnp.asarray(p, BF16)


def _rope_val(r, c2, s2, rot):
    return r * c2 + _exact_dot(r, rot, 1, 0, True) * s2


def _q_prep(q, cos2, sin2, scale, name):
    h, s_, _ = q.shape
    tr = _pick(s_, (1024, 512, 256, 128, 64, 32, 16, 8))

    def body(q_ref, c_ref, s_ref, rot_ref, o_ref):
        x = q_ref[...]
        o_ref[:, :NOPE] = (x[:, :NOPE] * scale).astype(o_ref.dtype)
        o_ref[:, NOPE:] = (_rope_val(x[:, NOPE:], c_ref[...], s_ref[...], rot_ref[...]) * scale).astype(o_ref.dtype)

    blk = pl.BlockSpec((None, tr, QK), lambda hh, i: (hh, i, 0))
    csp = pl.BlockSpec((tr, ROPE), lambda hh, i: (i, 0))
    return pl.pallas_call(
        body, name=name, grid=(h, s_ // tr),
        in_specs=[blk, csp, csp, pl.BlockSpec((ROPE, ROPE), lambda hh, i: (0, 0))], out_specs=blk,
        out_shape=jax.ShapeDtypeStruct(q.shape, MXU_DTYPE), compiler_params=_params("parallel", "parallel"),
    )(q, cos2, sin2, _rot_matrix())


def _kv_prep(kv, small, cos2, sin2, name="kv_prep"):
    h, s_, _ = kv.shape
    tr = _row_tile(s_)

    def body(kv_ref, sm_ref, c_ref, s_ref, rot_ref, k_ref, v_ref):
        kr = _rope_val(sm_ref[:, :ROPE], c_ref[...], s_ref[...], rot_ref[...]).astype(k_ref.dtype)
        for i in range(h):
            k_ref[i, :, :NOPE] = kv_ref[i, :, :NOPE].astype(k_ref.dtype)
            k_ref[i, :, NOPE:] = kr
            v_ref[i] = kv_ref[i, :, NOPE:].astype(v_ref.dtype)

    csp = pl.BlockSpec((tr, ROPE), lambda i: (i, 0))
    return pl.pallas_call(
        body, name=name, grid=(s_ // tr,),
        in_specs=[pl.BlockSpec((h, tr, NOPE + VDIM), lambda i: (0, i, 0)), pl.BlockSpec((tr, LANE), lambda i: (i, 0)),
                  csp, csp, pl.BlockSpec((ROPE, ROPE), lambda i: (0, 0))],
        out_specs=[pl.BlockSpec((h, tr, QK), lambda i: (0, i, 0)), pl.BlockSpec((h, tr, VDIM), lambda i: (0, i, 0))],
        out_shape=[jax.ShapeDtypeStruct((h, s_, QK), MXU_DTYPE), jax.ShapeDtypeStruct((h, s_, VDIM), MXU_DTYPE)],
        compiler_params=_params("parallel"),
    )(kv, small, cos2, sin2, _rot_matrix())


def _dkv_post(dk, dv, ddt, cos2, nsin2, name="dkv_post"):
    h, s_, _ = dk.shape
    tr = _row_tile(s_)

    def body(dk_ref, dv_ref, ddt_ref, c_ref, s_ref, rot_ref, dkv_ref, dsm_ref):
        acc = dk_ref[0, :, NOPE:]
        for i in range(1, h):
            acc = acc + dk_ref[i, :, NOPE:]
        dsm_ref[:, :ROPE] = _rope_val(acc, c_ref[...], s_ref[...], rot_ref[...]).astype(dsm_ref.dtype)
        dsm_ref[:, ROPE:] = ddt_ref[:, ROPE:].astype(dsm_ref.dtype)
        for i in range(h):
            dkv_ref[i, :, :NOPE] = dk_ref[i, :, :NOPE].astype(dkv_ref.dtype)
            dkv_ref[i, :, NOPE:] = dv_ref[i].astype(dkv_ref.dtype)

    csp = pl.BlockSpec((tr, ROPE), lambda i: (i, 0))
    return pl.pallas_call(
        body, name=name, grid=(s_ // tr,),
        in_specs=[pl.BlockSpec((h, tr, QK), lambda i: (0, i, 0)), pl.BlockSpec((h, tr, VDIM), lambda i: (0, i, 0)),
                  pl.BlockSpec((tr, LANE), lambda i: (i, 0)), csp, csp, pl.BlockSpec((ROPE, ROPE), lambda i: (0, 0))],
        out_specs=[pl.BlockSpec((h, tr, NOPE + VDIM), lambda i: (0, i, 0)), pl.BlockSpec((tr, LANE), lambda i: (i, 0))],
        out_shape=[jax.ShapeDtypeStruct((h, s_, NOPE + VDIM), MXU_DTYPE), jax.ShapeDtypeStruct((s_, LANE), MXU_DTYPE)],
        compiler_params=_params("parallel"),
    )(dk, dv, ddt, cos2, nsin2, _rot_matrix())


def _attn_tile(s):
    return 512 if s % 1024 == 0 else s // 2


def _pairs(n, by_key):
    if by_key:
        pr = [(i, j) for j in range(n) for i in range(j, n)]
    else:
        pr = [(i, j) for i in range(n) for j in range(i + 1)]
    return (jnp.asarray([p[0] for p in pr], jnp.int32), jnp.asarray([p[1] for p in pr], jnp.int32))


def _diag_mask(t):
    return lax.broadcasted_iota(jnp.int32, (t, t), 1) <= lax.broadcasted_iota(jnp.int32, (t, t), 0)


def _flash_specs(t, dk, dv):
    qsp = pl.BlockSpec((None, t, dk), lambda hh, p, qi, kj: (hh, qi[p], 0))
    ksp = pl.BlockSpec((None, t, dk), lambda hh, p, qi, kj: (hh, kj[p], 0))
    vsp = pl.BlockSpec((None, t, dv), lambda hh, p, qi, kj: (hh, kj[p], 0))
    osp = pl.BlockSpec((None, t, dv), lambda hh, p, qi, kj: (hh, qi[p], 0))
    lsp = pl.BlockSpec((None, t, 1), lambda hh, p, qi, kj: (hh, qi[p], 0))
    return qsp, ksp, vsp, osp, lsp


def _flash_fwd(q, k, v, name="flash_fwd"):
    h, s_, dk = q.shape
    dv = v.shape[-1]
    t = _attn_tile(s_)
    n = s_ // t
    qi, kj = _pairs(n, False)

    def body(qi_ref, kj_ref, q_ref, k_ref, v_ref, o_ref, lse_ref, m_s, l_s, acc):
        p_ = pl.program_id(1)
        i, j = qi_ref[p_], kj_ref[p_]

        @pl.when(j == 0)
        def _():
            m_s[...] = jnp.full_like(m_s, -jnp.inf)
            l_s[...] = jnp.zeros_like(l_s)
            acc[...] = jnp.zeros_like(acc)

        def update(sc):
            m_new = jnp.maximum(m_s[...], jnp.max(sc, axis=1, keepdims=True))
            alpha = jnp.exp(m_s[...] - m_new)
            p = jnp.exp(sc - m_new)
            l_s[...] = alpha * l_s[...] + jnp.sum(p, axis=1, keepdims=True)
            acc[...] = alpha * acc[...] + _dot(_mx(p), v_ref[...], 1, 0)
            m_s[...] = m_new

        @pl.when(j < i)
        def _():
            update(_dot(q_ref[...], k_ref[...], 1, 1))

        @pl.when(j == i)
        def _():
            update(jnp.where(_diag_mask(t), _dot(q_ref[...], k_ref[...], 1, 1), -jnp.inf))
            o_ref[...] = acc[...] / l_s[...]
            lse_ref[...] = m_s[...] + jnp.log(l_s[...])

    qsp, ksp, vsp, osp, lsp = _flash_specs(t, dk, dv)
    gs = pltpu.PrefetchScalarGridSpec(
        num_scalar_prefetch=2, grid=(h, qi.shape[0]), in_specs=[qsp, ksp, vsp], out_specs=[osp, lsp],
        scratch_shapes=[pltpu.VMEM((t, 1), F32), pltpu.VMEM((t, 1), F32), pltpu.VMEM((t, dv), F32)])
    return pl.pallas_call(
        body, name=name, grid_spec=gs,
        out_shape=[jax.ShapeDtypeStruct((h, s_, dv), F32), jax.ShapeDtypeStruct((h, s_, 1), F32)],
        compiler_params=_params("parallel", "arbitrary"),
    )(qi, kj, q, k, v)


def _flash_bwd_dq(q, k, v, o, do, lse, name="flash_bwd_dq"):
    h, s_, dk = q.shape
    dv = v.shape[-1]
    t = _attn_tile(s_)
    n = s_ // t
    qi, kj = _pairs(n, False)

    def body(qi_ref, kj_ref, q_ref, k_ref, v_ref, o_ref, do_ref, lse_ref, dq_ref, delta_ref, acc, delta_s):
        p_ = pl.program_id(1)
        i, j = qi_ref[p_], kj_ref[p_]

        @pl.when(j == 0)
        def _():
            delta_s[...] = jnp.sum(do_ref[...] * o_ref[...], axis=1, keepdims=True)
            acc[...] = jnp.zeros_like(acc)

        def update(sc):
            p = jnp.exp(sc - lse_ref[...])
            dp = _dot(_mx(do_ref[...]), v_ref[...], 1, 1)
            ds = p * (dp - delta_s[...])
            acc[...] += _dot(_mx(ds), k_ref[...], 1, 0)

        @pl.when(j < i)
        def _():
            update(_dot(q_ref[...], k_ref[...], 1, 1))

        @pl.when(j == i)
        def _():
            update(jnp.where(_diag_mask(t), _dot(q_ref[...], k_ref[...], 1, 1), -jnp.inf))
            dq_ref[...] = acc[...]
            delta_ref[...] = delta_s[...]

    qsp, ksp, vsp, osp, lsp = _flash_specs(t, dk, dv)
    gs = pltpu.PrefetchScalarGridSpec(
        num_scalar_prefetch=2, grid=(h, qi.shape[0]), in_specs=[qsp, ksp, vsp, osp, osp, lsp], out_specs=[qsp, lsp],
        scratch_shapes=[pltpu.VMEM((t, dk), F32), pltpu.VMEM((t, 1), F32)])
    return pl.pallas_call(
        body, name=name, grid_spec=gs,
        out_shape=[jax.ShapeDtypeStruct((h, s_, dk), F32), jax.ShapeDtypeStruct((h, s_, 1), F32)],
        compiler_params=_params("parallel", "arbitrary"),
    )(qi, kj, q, k, v, o, do, lse)


def _flash_bwd_dkv(q, k, v, do, lse, delta, name="flash_bwd_dkv"):
    h, s_, dk = q.shape
    dv = v.shape[-1]
    t = _attn_tile(s_)
    n = s_ // t
    qi, kj = _pairs(n, True)

    def body(qi_ref, kj_ref, q_ref, k_ref, v_ref, do_ref, lse_ref, delta_ref, dk_ref, dv_ref, dk_acc, dv_acc):
        p_ = pl.program_id(1)
        i, j = qi_ref[p_], kj_ref[p_]

        def update(sc):
            p = jnp.exp(sc - lse_ref[...])
            dob = _mx(do_ref[...])
            dv_acc[...] += _dot(_mx(p), dob, 0, 0)
            dp = _dot(dob, v_ref[...], 1, 1)
            ds = p * (dp - delta_ref[...])
            dk_acc[...] += _dot(_mx(ds), q_ref[...], 0, 0)

        @pl.when(i == j)
        def _():
            dk_acc[...] = jnp.zeros_like(dk_acc)
            dv_acc[...] = jnp.zeros_like(dv_acc)
            update(jnp.where(_diag_mask(t), _dot(q_ref[...], k_ref[...], 1, 1), -jnp.inf))

        @pl.when(i > j)
        def _():
            update(_dot(q_ref[...], k_ref[...], 1, 1))

        @pl.when(i == n - 1)
        def _():
            dk_ref[...] = dk_acc[...]
            dv_ref[...] = dv_acc[...]

    qsp, ksp, vsp, osp, lsp = _flash_specs(t, dk, dv)
    gs = pltpu.PrefetchScalarGridSpec(
        num_scalar_prefetch=2, grid=(h, qi.shape[0]), in_specs=[qsp, ksp, vsp, osp, lsp, lsp], out_specs=[ksp, vsp],
        scratch_shapes=[pltpu.VMEM((t, dk), F32), pltpu.VMEM((t, dv), F32)])
    return pl.pallas_call(
        body, name=name, grid_spec=gs,
        out_shape=[jax.ShapeDtypeStruct((h, s_, dk), F32), jax.ShapeDtypeStruct((h, s_, dv), F32)],
        compiler_params=_params("parallel", "arbitrary"),
    )(qi, kj, q, k, v, do, lse, delta)


HALO = 8


def _conv_specs(s_, c, tr, after):
    main = pl.BlockSpec((tr, c), lambda i: (i, 0))
    per = tr // HALO
    if after:
        halo = pl.BlockSpec((HALO, c), lambda i: (jnp.minimum((i + 1) * per, s_ // HALO - 1), 0))
    else:
        halo = pl.BlockSpec((HALO, c), lambda i: (jnp.maximum(i * per - 1, 0), 0))
    return main, halo


def _fill_before(ext, t_ref, h_ref, tr):
    ext[0:HALO, :] = jnp.where(pl.program_id(0) > 0, h_ref[...], 0.0)
    ext[HALO:HALO + tr, :] = t_ref[...]


def _taps(ext, w_ref, tr):
    base = HALO - (CONV_K - 1)
    acc = ext[base:base + tr, :] * w_ref[0:1, :]
    for k in range(1, CONV_K):
        acc = acc + ext[base + k:base + k + tr, :] * w_ref[k:k + 1, :]
    return acc


def _conv_fwd(t, w, b, name="conv_fwd"):
    s_, c = t.shape
    tr = _row_tile(s_)

    def body(t_ref, h_ref, w_ref, b_ref, o_ref, ext):
        _fill_before(ext, t_ref, h_ref, tr)
        o_ref[...] = _silu(_taps(ext, w_ref, tr) + b_ref[...])

    main, halo = _conv_specs(s_, c, tr, False)
    return pl.pallas_call(
        body, name=name, grid=(s_ // tr,),
        in_specs=[main, halo, pl.BlockSpec((CONV_K, c), lambda i: (0, 0)), pl.BlockSpec((1, c), lambda i: (0, 0))],
        out_specs=main, out_shape=jax.ShapeDtypeStruct((s_, c), F32),
        scratch_shapes=[pltpu.VMEM((tr + HALO, c), F32)], compiler_params=_params("parallel"),
    )(t, t, w, b)


def _conv_bwd_pre(t, w, b, dact, name="conv_bwd_pre"):
    s_, c = t.shape
    tr = _row_tile(s_)

    def body(t_ref, h_ref, w_ref, b_ref, da_ref, dpre_ref, dwb_ref, ext):
        @pl.when(pl.program_id(0) == 0)
        def _():
            dwb_ref[...] = jnp.zeros_like(dwb_ref)

        _fill_before(ext, t_ref, h_ref, tr)
        dpre = da_ref[...] * _dsilu(_taps(ext, w_ref, tr) + b_ref[...])
        dpre_ref[...] = dpre
        base = HALO - (CONV_K - 1)
        for k in range(CONV_K):
            dwb_ref[k:k + 1, :] += jnp.sum(dpre * ext[base + k:base + k + tr, :], axis=0, keepdims=True)
        dwb_ref[CONV_K:CONV_K + 1, :] += jnp.sum(dpre, axis=0, keepdims=True)

    main, halo = _conv_specs(s_, c, tr, False)
    return pl.pallas_call(
        body, name=name, grid=(s_ // tr,),
        in_specs=[main, halo, pl.BlockSpec((CONV_K, c), lambda i: (0, 0)), pl.BlockSpec((1, c), lambda i: (0, 0)), main],
        out_specs=[main, pl.BlockSpec((8, c), lambda i: (0, 0))],
        out_shape=[jax.ShapeDtypeStruct((s_, c), F32), jax.ShapeDtypeStruct((8, c), F32)],
        scratch_shapes=[pltpu.VMEM((tr + HALO, c), F32)], compiler_params=_params("arbitrary"),
    )(t, t, w, b, dact)


def _conv_bwd_in(dpre, w, name="conv_bwd_in"):
    s_, c = dpre.shape
    tr = _row_tile(s_)
    nt = s_ // tr

    def body(d_ref, h_ref, w_ref, o_ref, ext):
        ext[0:tr, :] = d_ref[...]
        ext[tr:tr + HALO, :] = jnp.where(pl.program_id(0) < nt - 1, h_ref[...], 0.0)
        acc = ext[CONV_K - 1:CONV_K - 1 + tr, :] * w_ref[0:1, :]
        for k in range(1, CONV_K):
            acc = acc + ext[CONV_K - 1 - k:CONV_K - 1 - k + tr, :] * w_ref[k:k + 1, :]
        o_ref[...] = acc.astype(o_ref.dtype)

    main, halo = _conv_specs(s_, c, tr, True)
    return pl.pallas_call(
        body, name=name, grid=(nt,),
        in_specs=[main, halo, pl.BlockSpec((CONV_K, c), lambda i: (0, 0))],
        out_specs=main, out_shape=jax.ShapeDtypeStruct((s_, c), MXU_DTYPE),
        scratch_shapes=[pltpu.VMEM((tr + HALO, c), F32)], compiler_params=_params("parallel"),
    )(dpre, dpre, w)


def _ssd_chunk_common(dt_ref, dtt_ref, br_ref, bc_ref, ar_ref, ac_ref):
    li = lax.broadcasted_iota(jnp.int32, (CHUNK, CHUNK), 0)
    si = lax.broadcasted_iota(jnp.int32, (CHUNK, CHUNK), 1)
    lower = li >= si
    lower_b = lower.astype(BF16)
    upper_b = (li <= si).astype(BF16)
    zr = dt_ref[...] + br_ref[...]
    dtc = _softplus(zr)
    a_row = -jnp.exp(ar_ref[...])
    acum = _exact_dot(lower_b, dtc * a_row, 1, 0, False)
    dtt = _softplus(dtt_ref[...] + bc_ref[...])
    acum_t = _exact_dot(dtt * (-jnp.exp(ac_ref[...])), upper_b, 1, 0, True)
    return lower, upper_b, zr, dtc, a_row, acum, acum_t


def _head_terms(h, lower, dtc, acum, acum_t):
    lane = lax.broadcasted_iota(jnp.int32, (1, LANE), 1)
    sub = lax.broadcasted_iota(jnp.int32, (SSD_H, 1), 0)
    rowid = lax.broadcasted_iota(jnp.int32, (CHUNK, 1), 0)
    oh = (lane == HEAD_LANE + h).astype(F32)
    acol = jnp.sum(acum * oh, axis=1, keepdims=True)
    dcol = jnp.sum(dtc * oh, axis=1, keepdims=True)
    arow = jnp.sum(acum_t * (sub == h).astype(F32), axis=0, keepdims=True)
    alast = jnp.sum(jnp.where(rowid == CHUNK - 1, acol, 0.0), axis=0, keepdims=True)
    decay = jnp.exp(jnp.where(lower, acol - arow, -jnp.inf))
    return oh, acol, dcol, alast, decay


def _hs(h):
    return slice(h * SSD_P, (h + 1) * SSD_P)


def _gs(off, g):
    return slice(off + g * SSD_N, off + (g + 1) * SSD_N)


def _ssd_in_specs(rev):
    def ci(c):
        return c if rev is None else rev - c
    return [pl.BlockSpec((CHUNK, CONV_DIM), lambda c: (ci(c), 0)),
            pl.BlockSpec((CHUNK, LANE), lambda c: (ci(c), 0)),
            pl.BlockSpec((SSD_H, CHUNK), lambda c: (0, ci(c))),
            pl.BlockSpec((1, LANE), lambda c: (0, 0)), pl.BlockSpec((SSD_H, 1), lambda c: (0, 0)),
            pl.BlockSpec((1, LANE), lambda c: (0, 0)), pl.BlockSpec((SSD_H, 1), lambda c: (0, 0)),
            pl.BlockSpec((SSD_H, 1, SSD_P), lambda c: (0, 0, 0))]


def _ssd_fwd(xbc, small, dtt, bias_r, bias_c, alog_r, alog_c, dsk, name="ssd_fwd"):
    s_ = xbc.shape[0]
    nc = s_ // CHUNK

    def body(x_ref, dt_ref, dtt_ref, br_ref, bc_ref, ar_ref, ac_ref, dsk_ref, y_ref, prev_ref, state):
        @pl.when(pl.program_id(0) == 0)
        def _():
            state[...] = jnp.zeros_like(state)

        lower, _, _, dtc, _, acum, acum_t = _ssd_chunk_common(dt_ref, dtt_ref, br_ref, bc_ref, ar_ref, ac_ref)
        for g in range(SSD_G):
            bb = _mx(x_ref[:, _gs(B_OFF, g)])
            cb_ = _mx(x_ref[:, _gs(C_OFF, g)])
            cbm = _dot(cb_, bb, 1, 1)
            for e in range(SSD_E):
                h = g * SSD_E + e
                _, acol, dcol, alast, decay = _head_terms(h, lower, dtc, acum, acum_t)
                x = x_ref[:, _hs(h)]
                xdt = x * dcol
                yd = _dot(_mx(cbm * decay), _mx(xdt), 1, 0)
                prev = state[h]
                prev_ref[0, h] = prev
                yo = _dot(cb_, _mx(prev), 1, 1) * jnp.exp(acol)
                ds = jnp.exp(alast - acol)
                st = _dot(_mx(xdt * ds), bb, 0, 0)
                state[h] = prev * jnp.exp(alast) + st
                y_ref[:, _hs(h)] = yd + yo + x * dsk_ref[h]

    psp = pl.BlockSpec((1, SSD_H, SSD_P, SSD_N), lambda c: (c, 0, 0, 0))
    return pl.pallas_call(
        body, name=name, grid=(nc,),
        in_specs=_ssd_in_specs(None), out_specs=[pl.BlockSpec((CHUNK, SSD_W), lambda c: (c, 0)), psp],
        out_shape=[jax.ShapeDtypeStruct((s_, SSD_W), F32),
                   jax.ShapeDtypeStruct((nc, SSD_H, SSD_P, SSD_N), F32)],
        scratch_shapes=[pltpu.VMEM((SSD_H, SSD_P, SSD_N), F32)],
        compiler_params=_params("arbitrary"),
    )(xbc, small, dtt, bias_r, bias_c, alog_r, alog_c, dsk)


def _ssd_bwd(xbc, small, dtt, bias_r, bias_c, alog_r, alog_c, dsk, prev, dy, name="ssd_bwd"):
    s_ = xbc.shape[0]
    nc = s_ // CHUNK

    def body(x_ref, dt_ref, dtt_ref, br_ref, bc_ref, ar_ref, ac_ref, dsk_ref, prev_ref, dy_ref,
             dx_ref, ddt_ref, dpar_ref, dstate):
        @pl.when(pl.program_id(0) == 0)
        def _():
            dstate[...] = jnp.zeros_like(dstate)
            dpar_ref[...] = jnp.zeros_like(dpar_ref)

        lower, upper_b, zr, dtc, a_row, acum, acum_t = _ssd_chunk_common(
            dt_ref, dtt_ref, br_ref, bc_ref, ar_ref, ac_ref)
        strict = (lax.broadcasted_iota(jnp.int32, (CHUNK, CHUNK), 1)
                  < lax.broadcasted_iota(jnp.int32, (CHUNK, CHUNK), 0))
        strict_b = strict.astype(BF16)
        da_in = jnp.zeros((CHUNK, LANE), F32)
        r_off = jnp.zeros((CHUNK, LANE), F32)
        c_int = jnp.zeros((CHUNK, LANE), F32)
        c_row = jnp.zeros((1, LANE), F32)
        ddt = jnp.zeros((CHUNK, LANE), F32)
        dskip = jnp.zeros((1, LANE), F32)
        for g in range(SSD_G):
            bb = _mx(x_ref[:, _gs(B_OFF, g)])
            cb_ = _mx(x_ref[:, _gs(C_OFF, g)])
            cbm = _dot(cb_, bb, 1, 1)
            dcb = jnp.zeros((CHUNK, CHUNK), F32)
            dc_acc = jnp.zeros((CHUNK, SSD_N), F32)
            db_acc = jnp.zeros((CHUNK, SSD_N), F32)
            for e in range(SSD_E):
                h = g * SSD_E + e
                oh, acol, dcol, alast, decay = _head_terms(h, lower, dtc, acum, acum_t)
                x = x_ref[:, _hs(h)]
                dy = dy_ref[:, _hs(h)]
                xdt = x * dcol
                eacol = jnp.exp(acol)
                ds = jnp.exp(alast - acol)
                dyb = _mx(dy)
                dsh = dstate[h]
                dshb = _mx(dsh)
                prev = prev_ref[0, h]
                prevb = _mx(prev)
                dxdt_inter = ds * _dot(bb, dshb, 1, 1)
                dxdt = _dot(_mx(cbm * decay), dyb, 0, 0) + dxdt_inter
                dwl = _dot(dyb, _mx(xdt), 1, 1) * decay
                dcb = dcb + dwl
                dc_acc = dc_acc + eacol * _dot(dyb, prevb, 1, 0)
                db_acc = db_acc + _dot(_mx(xdt * ds), dshb, 1, 0)
                dstate[h] = _dot(_mx(dy * eacol), cb_, 0, 0) + jnp.exp(alast) * dsh
                above = _exact_dot(upper_b, dwl * cbm, 1, 0, False)
                da_in = da_in + jnp.sum(jnp.where(strict, above, 0.0), axis=1, keepdims=True) * oh
                y_off = _dot(cb_, prevb, 1, 1) * eacol
                r_off = r_off + jnp.sum(dy * y_off, axis=1, keepdims=True) * oh
                c_int = c_int + jnp.sum(xdt * dxdt_inter, axis=1, keepdims=True) * oh
                both = jnp.sum(jnp.sum(dsh * prev, axis=1, keepdims=True), axis=0, keepdims=True)
                c_row = c_row + jnp.exp(alast) * both * oh
                dk = dsk_ref[h]
                ddt = ddt + jnp.sum(dxdt * x, axis=1, keepdims=True) * oh
                dx_ref[:, _hs(h)] = dxdt * dcol + dy * dk
                dskip = dskip + jnp.sum(jnp.sum(dy * x, axis=1, keepdims=True), axis=0, keepdims=True) * oh
            dcbb = _mx(dcb)
            dx_ref[:, _gs(C_OFF, g)] = dc_acc + _dot(dcbb, bb, 1, 0)
            dx_ref[:, _gs(B_OFF, g)] = db_acc + _dot(dcbb, cb_, 0, 0)
        da = (da_in + _exact_dot(upper_b, r_off, 1, 0, False) + _exact_dot(strict_b, c_int, 1, 0, False) + c_row)
        draw = (ddt + da * a_row) * _sigmoid(zr)
        ddt_ref[...] = draw
        dpar_ref[0:1, :] += jnp.sum(draw, axis=0, keepdims=True)
        dpar_ref[1:2, :] += jnp.sum(da * dtc, axis=0, keepdims=True) * a_row
        dpar_ref[2:3, :] += dskip

    rev = nc - 1
    psp = pl.BlockSpec((1, SSD_H, SSD_P, SSD_N), lambda c: (rev - c, 0, 0, 0))
    return pl.pallas_call(
        body, name=name, grid=(nc,),
        in_specs=_ssd_in_specs(rev) + [psp, pl.BlockSpec((CHUNK, SSD_W), lambda c: (rev - c, 0))],
        out_specs=[pl.BlockSpec((CHUNK, CONV_DIM), lambda c: (rev - c, 0)),
                   pl.BlockSpec((CHUNK, LANE), lambda c: (rev - c, 0)), pl.BlockSpec((8, LANE), lambda c: (0, 0))],
        out_shape=[jax.ShapeDtypeStruct((s_, CONV_DIM), F32), jax.ShapeDtypeStruct((s_, LANE), F32),
                   jax.ShapeDtypeStruct((8, LANE), F32)],
        scratch_shapes=[pltpu.VMEM((SSD_H, SSD_P, SSD_N), F32)],
        compiler_params=_params("arbitrary"),
    )(xbc, small, dtt, bias_r, bias_c, alog_r, alog_c, dsk, prev, dy)


GN = SSD_W // SSD_G


def _gated_norm_fwd(y, z, w, name="gated_norm_fwd"):
    s_, f = y.shape
    tr = _row_tile(s_)

    def body(y_ref, z_ref, w_ref, o_ref):
        for g in range(SSD_G):
            sl = slice(g * GN, (g + 1) * GN)
            gg = y_ref[:, sl] * _silu(z_ref[:, sl])
            r = lax.rsqrt(jnp.mean(gg * gg, axis=-1, keepdims=True) + EPS)
            o_ref[:, sl] = (gg * r * w_ref[:, sl]).astype(o_ref.dtype)

    row = pl.BlockSpec((tr, f), lambda i: (i, 0))
    wsp = pl.BlockSpec((1, f), lambda i: (0, 0))
    return pl.pallas_call(
        body, name=name, grid=(s_ // tr,), in_specs=[row, row, wsp], out_specs=row,
        out_shape=jax.ShapeDtypeStruct((s_, f), MXU_DTYPE), compiler_params=_params("parallel"),
    )(y, z, w.reshape(1, f))


def _gated_norm_bwd(y, z, w, dout, name="gated_norm_bwd"):
    s_, f = y.shape
    tr = _row_tile(s_)

    def body(y_ref, z_ref, w_ref, do_ref, dy_ref, dz_ref, dw_ref):
        @pl.when(pl.program_id(0) == 0)
        def _():
            dw_ref[...] = jnp.zeros_like(dw_ref)

        for g in range(SSD_G):
            sl = slice(g * GN, (g + 1) * GN)
            yv = y_ref[:, sl]
            zv = z_ref[:, sl]
            dov = do_ref[:, sl].astype(F32)
            sz = _silu(zv)
            gg = yv * sz
            r = lax.rsqrt(jnp.mean(gg * gg, axis=-1, keepdims=True) + EPS)
            gw = dov * w_ref[:, sl]
            c = jnp.mean(gw * gg, axis=-1, keepdims=True)
            dgg = r * gw - gg * (r * r * r * c)
            dy_ref[:, sl] = dgg * sz
            dz_ref[:, sl] = (dgg * yv * _dsilu(zv)).astype(dz_ref.dtype)
            dw_ref[:, sl] += jnp.sum(dov * gg * r, axis=0, keepdims=True)

    row = pl.BlockSpec((tr, f), lambda i: (i, 0))
    wsp = pl.BlockSpec((1, f), lambda i: (0, 0))
    return pl.pallas_call(
        body, name=name, grid=(s_ // tr,), in_specs=[row, row, wsp, row], out_specs=[row, row, wsp],
        out_shape=[jax.ShapeDtypeStruct((s_, f), F32), jax.ShapeDtypeStruct((s_, f), MXU_DTYPE),
                   jax.ShapeDtypeStruct((1, f), F32)],
        compiler_params=_params("arbitrary"),
    )(y, z, w.reshape(1, f), dout)


def _swiglu_fwd(g, u, name="swiglu_fwd"):
    nb, s_, f = g.shape
    tr = _row_tile(s_)

    def body(g_ref, u_ref, o_ref):
        o_ref[...] = (_silu(g_ref[...]) * u_ref[...]).astype(o_ref.dtype)

    sp = pl.BlockSpec((None, tr, f), lambda j, i: (j, i, 0))
    return pl.pallas_call(
        body, name=name, grid=(nb, s_ // tr), in_specs=[sp, sp], out_specs=sp,
        out_shape=jax.ShapeDtypeStruct(g.shape, MXU_DTYPE), compiler_params=_params("parallel", "parallel"),
    )(g, u)


def _swiglu_bwd(g, u, da, name="swiglu_bwd"):
    nb, s_, f = g.shape
    tr = _row_tile(s_)

    def body(g_ref, u_ref, da_ref, dg_ref, du_ref):
        gv = g_ref[...]
        dav = da_ref[...]
        dg_ref[...] = (dav * u_ref[...] * _dsilu(gv)).astype(dg_ref.dtype)
        du_ref[...] = (dav * _silu(gv)).astype(du_ref.dtype)

    sp = pl.BlockSpec((None, tr, f), lambda j, i: (j, i, 0))
    return pl.pallas_call(
        body, name=name, grid=(nb, s_ // tr), in_specs=[sp, sp, sp], out_specs=[sp, sp],
        out_shape=[jax.ShapeDtypeStruct(g.shape, MXU_DTYPE)] * 2, compiler_params=_params("parallel", "parallel"),
    )(g, u, da)


def _adam_math(g, w, m, v):
    m2 = ADAM_B1 * m + (1.0 - ADAM_B1) * g
    v2 = ADAM_B2 * v + (1.0 - ADAM_B2) * (g * g)
    m_hat = m2 / (1.0 - ADAM_B1 ** ADAM_STEP)
    v_hat = v2 / (1.0 - ADAM_B2 ** ADAM_STEP)
    delta = -ADAM_LR * (m_hat / (jnp.sqrt(v_hat) + ADAM_EPS) + ADAM_WD * w)
    return delta, m2, v2


def _adamw(parts, w, m, v, name="adamw"):
    nd, r_, c = parts.shape
    tr = _pick(r_, (128, 64, 32, 16, 8))

    def body(p_ref, w_ref, m_ref, v_ref, g_ref, d_ref, m2_ref, v2_ref):
        g = p_ref[0].astype(F32)
        for i in range(1, nd):
            g = g + p_ref[i].astype(F32)
        delta, m2, v2 = _adam_math(g, w_ref[...], m_ref[...], v_ref[...])
        g_ref[...] = g
        d_ref[...] = delta
        m2_ref[...] = m2
        v2_ref[...] = v2

    row = pl.BlockSpec((tr, c), lambda i: (i, 0))
    psp = pl.BlockSpec((nd, tr, c), lambda i: (0, i, 0))
    return pl.pallas_call(
        body, name=name, grid=(r_ // tr,), in_specs=[psp, row, row, row], out_specs=[row] * 4,
        out_shape=[jax.ShapeDtypeStruct((r_, c), F32)] * 4, compiler_params=_params("parallel"),
    )(parts, w, m, v)


def _adamw_small(parts, w, m, v, name="adamw_small"):
    nd = parts.shape[0]

    def body(p_ref, w_ref, m_ref, v_ref, g_ref, d_ref, m2_ref, v2_ref):
        g = p_ref[0]
        for i in range(1, nd):
            g = g + p_ref[i]
        delta, m2, v2 = _adam_math(g, w_ref[...], m_ref[...], v_ref[...])
        g_ref[...] = g
        d_ref[...] = delta
        m2_ref[...] = m2
        v2_ref[...] = v2

    return pl.pallas_call(
        body, name=name, out_shape=[jax.ShapeDtypeStruct(w.shape, F32)] * 4,
        compiler_params=pltpu.CompilerParams(vmem_limit_bytes=VMEM_LIMIT_BYTES),
    )(parts, w, m, v)


_HBM = pl.BlockSpec(memory_space=pltpu.HBM)
_MESH = pl.DeviceIdType.MESH


def _all_gather(xs, name):
    na = len(xs)

    def body(*refs):
        x_refs, out_refs = refs[:na], refs[na:2 * na]
        send_sems, recv_sems, local_sems = refs[2 * na:]
        x, y, c = lax.axis_index("x"), lax.axis_index("y"), lax.axis_index("c")
        me, sibling = (x, y, c), (x, y, 1 - c)
        chips = [(1 - x, y), (x, 1 - y), (1 - x, 1 - y)]

        def slot(a, px, py, pc):
            return out_refs[a].at[4 * px + 2 * py + pc]

        def copy(a, k, block, to, src=None):
            return pltpu.make_async_remote_copy(
                src_ref=slot(a, *block) if src is None else src, dst_ref=slot(a, *block),
                send_sem=send_sems.at[a, k], recv_sem=recv_sems.at[a, k], device_id=to, device_id_type=_MESH)

        mine = [pltpu.make_async_copy(x_refs[a], slot(a, *me), local_sems.at[a]) for a in range(na)]
        started = []
        for a in range(na):
            mine[a].start()
            first = [copy(a, 0, me, sibling, src=x_refs[a])]
            first += [copy(a, 1 + j, me, (*chip, c), src=x_refs[a]) for j, chip in enumerate(chips)]
            for cp in first:
                cp.start()
            started += first
        for a in range(na):
            for j, chip in enumerate(chips):
                copy(a, 1 + j, (*chip, c), me).wait_recv()
                fwd = copy(a, 4 + j, (*chip, c), sibling)
                fwd.start()
                started.append(fwd)
        for a in range(na):
            copy(a, 0, sibling, me).wait_recv()
            for j, chip in enumerate(chips):
                copy(a, 4 + j, (*chip, 1 - c), me).wait_recv()
        for cp in started:
            cp.wait_send()
        for cp in mine:
            cp.wait()

    return pl.pallas_call(
        body, name=name, out_shape=[jax.ShapeDtypeStruct((N_DEV,) + t.shape, t.dtype) for t in xs],
        in_specs=[_HBM] * na, out_specs=[_HBM] * na,
        scratch_shapes=[pltpu.SemaphoreType.DMA((na, 7)), pltpu.SemaphoreType.DMA((na, 7)),
                        pltpu.SemaphoreType.DMA((na,))],
    )(*xs)


def _all_to_all(srcs, name):
    na = len(srcs)

    def body(*refs):
        src_refs, out_refs = refs[:na], refs[na:2 * na]
        send_sems, recv_sems, local_sems = refs[2 * na:]
        x, y, c = lax.axis_index("x"), lax.axis_index("y"), lax.axis_index("c")
        me = 4 * x + 2 * y + c
        mine = [pltpu.make_async_copy(src_refs[a].at[me], out_refs[a].at[me], local_sems.at[a]) for a in range(na)]
        copies = []
        for a in range(na):
            mine[a].start()
            for k in range(1, N_DEV):
                px = 1 - x if k & 4 else x
                py = 1 - y if k & 2 else y
                pc = 1 - c if k & 1 else c
                peer = 4 * px + 2 * py + pc
                send = pltpu.make_async_remote_copy(
                    src_ref=src_refs[a].at[peer], dst_ref=out_refs[a].at[me], send_sem=send_sems.at[a, k - 1],
                    recv_sem=recv_sems.at[a, k - 1], device_id=(px, py, pc), device_id_type=_MESH)
                recv = pltpu.make_async_remote_copy(
                    src_ref=src_refs[a].at[peer], dst_ref=out_refs[a].at[peer], send_sem=send_sems.at[a, k - 1],
                    recv_sem=recv_sems.at[a, k - 1], device_id=(px, py, pc), device_id_type=_MESH)
                send.start()
                copies.append((send, recv))
        for send, recv in copies:
            recv.wait_recv()
        for send, recv in copies:
            send.wait_send()
        for cp in mine:
            cp.wait()

    return pl.pallas_call(
        body, name=name, out_shape=[jax.ShapeDtypeStruct(t.shape, t.dtype) for t in srcs],
        in_specs=[_HBM] * na, out_specs=[_HBM] * na,
        scratch_shapes=[pltpu.SemaphoreType.DMA((na, 7)), pltpu.SemaphoreType.DMA((na, 7)),
                        pltpu.SemaphoreType.DMA((na,))],
    )(*srcs)


_SEM = pl.BlockSpec(memory_space=pltpu.SEMAPHORE)
_EFFECT = pltpu.SideEffectType.DATAFLOW_SIDE_EFFECTING


def _peers(x, y, c):
    out = []
    for k in range(1, N_DEV):
        px = 1 - x if k & 4 else x
        py = 1 - y if k & 2 else y
        pc = 1 - c if k & 1 else c
        out.append(((px, py, pc), 4 * px + 2 * py + pc))
    return out


def _push_copies(scatter, src_refs, land_refs, send_sems, recv_sems):
    x, y, c = lax.axis_index("x"), lax.axis_index("y"), lax.axis_index("c")
    me = 4 * x + 2 * y + c
    pairs = []
    for a, (src, land) in enumerate(zip(src_refs, land_refs)):
        for k, (peer, slot) in enumerate(_peers(x, y, c)):
            out_src = src.at[slot] if scatter else src
            si = a * (N_DEV - 1) + k
            send = pltpu.make_async_remote_copy(src_ref=out_src, dst_ref=land.at[me], send_sem=send_sems.at[si],
                                                recv_sem=recv_sems.at[si], device_id=peer, device_id_type=_MESH)
            recv = pltpu.make_async_remote_copy(src_ref=out_src, dst_ref=land.at[slot], send_sem=send_sems.at[si],
                                                recv_sem=recv_sems.at[si], device_id=peer, device_id_type=_MESH)
            pairs.append((send, recv))
    return pairs


def _push_start(srcs, scatter, dep, name):
    na = len(srcs)
    shapes = [t.shape[1:] if scatter else t.shape for t in srcs]
    lands = [pltpu.with_memory_space_constraint(lax.empty((N_DEV,) + s, t.dtype), pltpu.HBM) for s, t in zip(shapes, srcs)]

    def body(*refs):
        src_refs, land_refs = refs[:na], refs[na:2 * na]
        send_sems, recv_sems = refs[2 * na + 1], refs[2 * na + 2]
        token = refs[-1]
        for send, _ in _push_copies(scatter, src_refs, land_refs, send_sems, recv_sems):
            send.start()
        token[...] = jnp.zeros_like(token)

    sem = pltpu.SemaphoreType.DMA((na * (N_DEV - 1),))
    outs = pl.pallas_call(
        body, name=name,
        out_shape=(sem, sem) + tuple(pltpu.HBM(t.shape, t.dtype) for t in srcs)
        + tuple(pltpu.HBM(t.shape, t.dtype) for t in lands) + (jax.ShapeDtypeStruct((8, LANE), F32),),
        in_specs=[_HBM] * (2 * na) + [pl.BlockSpec(memory_space=pl.ANY)],
        out_specs=(_SEM, _SEM) + (_HBM,) * (2 * na) + (pl.BlockSpec(memory_space=pltpu.VMEM),),
        input_output_aliases={i: 2 + i for i in range(2 * na)},
        compiler_params=pltpu.CompilerParams(has_side_effects=_EFFECT),
    )(*[pltpu.with_memory_space_constraint(t, pltpu.HBM) for t in srcs], *lands, dep)
    return outs[0], outs[1], outs[2:2 + na], outs[2 + na:2 + 2 * na], outs[-1]


def _push_wait(send_sems, recv_sems, src_thru, land_thru, scatter, after, name):
    na = len(src_thru)

    def body(*refs):
        src_refs, land_refs = refs[:na], refs[na:2 * na]
        ssem, rsem = refs[2 * na], refs[2 * na + 1]
        for send, recv in _push_copies(scatter, src_refs, land_refs, ssem, rsem):
            send.wait_send()
            recv.wait_recv()

    outs = pl.pallas_call(
        body, name=name,
        out_shape=tuple(pltpu.HBM(t.shape, t.dtype) for t in src_thru) + tuple(pltpu.HBM(t.shape, t.dtype) for t in land_thru),
        in_specs=[_HBM] * (2 * na) + [_SEM, _SEM, pl.BlockSpec(memory_space=pl.ANY)],
        out_specs=(_HBM,) * (2 * na),
        input_output_aliases={i: i for i in range(2 * na)},
        compiler_params=pltpu.CompilerParams(has_side_effects=_EFFECT),
    )(*src_thru, *land_thru, send_sems, recv_sems, after)
    return outs[:na], outs[na:]


def _exchange_behind(srcs, scatter, dep, name):
    send_sems, recv_sems, thru, lands, token = _push_start(srcs, scatter, dep, name + "_start")

    def finish(after):
        src_done, land_done = _push_wait(send_sems, recv_sems, thru, lands, scatter, after, name + "_wait")
        return _place_own(land_done, src_done, scatter, name + "_own")

    return token[0, 0], finish


def _place_own(lands, srcs, scatter, name):
    me = (4 * lax.axis_index("x") + 2 * lax.axis_index("y") + lax.axis_index("c")).astype(jnp.int32).reshape(1)
    outs = []
    for a, (land, src) in enumerate(zip(lands, srcs)):
        r_, c_ = land.shape[1:]
        tr = _pick(r_, (512, 256, 128, 64, 32, 16))

        def body(me_ref, land_ref, src_ref, out_ref):
            out_ref[...] = src_ref[...]

        src_spec = (pl.BlockSpec((None, tr, c_), lambda i, me_: (me_[0], i, 0)) if scatter
                    else pl.BlockSpec((tr, c_), lambda i, me_: (i, 0)))
        gs = pltpu.PrefetchScalarGridSpec(
            num_scalar_prefetch=1, grid=(r_ // tr,),
            in_specs=[pl.BlockSpec(memory_space=pl.ANY), src_spec],
            out_specs=pl.BlockSpec((None, tr, c_), lambda i, me_: (me_[0], i, 0)))
        outs.append(pl.pallas_call(
            body, name=f"{name}_{a}", grid_spec=gs, out_shape=jax.ShapeDtypeStruct(land.shape, land.dtype),
            input_output_aliases={1: 0}, compiler_params=_params("arbitrary"),
        )(me, land, src))
    return outs


_BIG = (("w_in", D_MODEL, D_IN, 1), ("w_uq", Q_RANK, HEADS * QK, 1), ("w_ukv", KV_RANK, HEADS * (NOPE + VDIM), 1),
        ("w_out", D_MODEL, D_MODEL, 0), ("w_gate", D_MODEL, D_FF, 1), ("w_up", D_MODEL, D_FF, 1),
        ("w_down", D_FF, D_MODEL, 0))
_CQKV = (0, Q_RANK + KV_RANK)
_KR = (_CQKV[1], _CQKV[1] + ROPE)
_Z = (_KR[1], _KR[1] + SSD_W)
_XBC = (_Z[1], _Z[1] + CONV_DIM)
_DT = (_XBC[1], _XBC[1] + SSD_H)


def _win_segments(w_in_g):
    w = jnp.transpose(w_in_g, (1, 0, 2)).reshape(D_MODEL, D_IN)
    small = jnp.concatenate([w[:, _KR[0]:_KR[1]], w[:, _DT[0]:_DT[1]],
                             jnp.zeros((D_MODEL, LANE - ROPE - SSD_H), w.dtype)], axis=1)
    return w[:, _CQKV[0]:_CQKV[1]], w[:, _Z[0]:_Z[1]], w[:, _XBC[0]:_XBC[1]], small


def _win_from_segments(g_cqkv, g_z, g_xbc, g_small):
    w = jnp.concatenate([g_cqkv, g_small[:, :ROPE], g_z, g_xbc, g_small[:, ROPE:ROPE + SSD_H]], axis=1)
    return jnp.transpose(w.reshape(D_MODEL, N_DEV, D_IN // N_DEV), (1, 0, 2))


_SMALL = (("q_norm_w", 512), ("kv_norm_w", 512), ("conv_b", CONV_DIM), ("dt_bias", SSD_H), ("a_log", SSD_H),
          ("d_skip", SSD_H), ("ssd_norm_w", SSD_W), ("attn_out_norm_w", 1024), ("pre_mix_norm_w", D_MODEL),
          ("post_mix_norm_w", D_MODEL), ("pre_ffn_norm_w", D_MODEL), ("post_ffn_norm_w", D_MODEL),
          ("conv_w", CONV_K * CONV_DIM))
_SMALL_ROWS = -(-sum(-(-n // LANE) for _, n in _SMALL) // 8) * 8


def _pack_small(vals):
    rows = []
    for name, n in _SMALL:
        v = vals[name].reshape(-1).astype(F32)
        pad = -(-n // LANE) * LANE
        rows.append(jnp.pad(v, (0, pad - n)).reshape(-1, LANE))
    m = jnp.concatenate(rows, axis=0)
    return jnp.pad(m, ((0, _SMALL_ROWS - m.shape[0]), (0, 0)))


def _unpack_small(m):
    out, r = {}, 0
    for name, n in _SMALL:
        nr = -(-n // LANE)
        out[name] = m[r:r + nr].reshape(-1)[:n]
        r += nr
    return out


def _head_row(v):
    return jnp.pad(v.reshape(1, -1).astype(F32), ((0, 0), (HEAD_LANE, LANE - HEAD_LANE - v.shape[-1])))


def _local_step(x, positions, target, wg, small, weights, on_grads):
    w_cqkv, w_z, w_xbc, w_small = _win_segments(wg["w_in"])
    conv_w = wg["conv_w"]
    conv_b = small["conv_b"].reshape(1, CONV_DIM)
    qkv_norm_w = jnp.concatenate([small["q_norm_w"], small["kv_norm_w"]])
    attn_norm_w = small["attn_out_norm_w"].reshape(HEADS, 1, VDIM)
    scale = QK ** -0.5

    inv_freq = ROPE_THETA ** (-jnp.arange(0, ROPE, 2, dtype=F32) / ROPE)
    ang = positions.astype(F32)[:, None] * inv_freq
    cos2 = jnp.tile(jnp.cos(ang), (1, 2))
    sin2 = jnp.tile(jnp.sin(ang), (1, 2))

    u = _rms_fwd(x, small["pre_mix_norm_w"], out_dtype=MXU_DTYPE, name="pre_mix_norm")
    cqkv = _mm(u, w_cqkv, "nn", name="in_proj_qkv")
    z = _mm(u, w_z, "nn", name="in_proj_z")
    xbc = _mm(u, w_xbc, "nn", name="in_proj_xbc")
    sm = _mm(u, w_small, "nn", name="in_proj_small")

    w_uq, w_ukv, w_out = weights("heads", cqkv)
    w_out = w_out.reshape(D_MODEL, D_MODEL)
    w_out_a = w_out[:HEADS * VDIM].reshape(HEADS, VDIM, D_MODEL)
    w_out_s = w_out[HEADS * VDIM:]
    qkvn = _rms_fwd(cqkv, qkv_norm_w, groups=2, out_dtype=MXU_DTYPE, name="qkv_norm")
    q = _mm(qkvn, w_uq, "nn", b_blk=True, out_blk=True, a_cols=(0, Q_RANK), name="q_up")
    kv = _mm(qkvn, w_ukv, "nn", b_blk=True, out_blk=True, a_cols=(Q_RANK, KV_RANK), name="kv_up")
    q_h = _q_prep(q, cos2, sin2, scale, name="q_prep")
    k_h, v_h = _kv_prep(kv, sm, cos2, sin2)
    o_h, lse = _flash_fwd(q_h, k_h, v_h)
    attn = _hnorm_fwd(o_h, attn_norm_w)

    xbc_act = _conv_fwd(xbc, conv_w, conv_b)
    dtt = jnp.transpose(sm[:, HEAD_LANE:HEAD_LANE + SSD_H])
    ssd_args = (xbc_act, sm, dtt, _head_row(small["dt_bias"]), small["dt_bias"].reshape(SSD_H, 1),
                _head_row(small["a_log"]), small["a_log"].reshape(SSD_H, 1),
                jnp.broadcast_to(small["d_skip"].reshape(SSD_H, 1, 1), (SSD_H, 1, SSD_P)))
    y_ssd, prev = _ssd_fwd(*ssd_args)
    ssm = _gated_norm_fwd(y_ssd, z, small["ssd_norm_w"])

    mix = _mm(attn, w_out_a, "nn", a_blk=True, b_blk=True, name="out_proj_attn")
    mix = _mm(ssm, w_out_s, "nn", add=mix, name="out_proj_ssm")
    h1 = _rms_fwd(mix, small["post_mix_norm_w"], res=x, name="post_mix_norm")

    w_gate, w_up, w_down = weights("ffn", mix)
    vv = _rms_fwd(h1, small["pre_ffn_norm_w"], out_dtype=MXU_DTYPE, name="pre_ffn_norm")
    gate = _mm(vv, w_gate, "nn", b_blk=True, out_blk=True, name="ffn_gate")
    up = _mm(vv, w_up, "nn", b_blk=True, out_blk=True, name="ffn_up")
    act = _swiglu_fwd(gate, up)
    ffn = _mm(act, w_down, "nn", a_blk=True, b_blk=True, name="ffn_down")
    loss_blk, dy, dffn, g_post_ffn = _loss_head(ffn, h1, target, small["post_ffn_norm_w"])

    dact = _mm(dffn, w_down, "nt", b_blk=True, out_blk=True, name="d_act")
    g_down = _mm(act, dffn, "tn", a_blk=True, out_blk=True, out_dtype=MXU_DTYPE, name="g_down")
    dgate, dup = _swiglu_bwd(gate, up, dact)
    dvv = _mm(dgate, w_gate, "nt", a_blk=True, b_blk=True, name="d_v_gate")
    dvv = _mm(dup, w_up, "nt", a_blk=True, b_blk=True, add=dvv, name="d_v_up")
    g_gate = _mm(vv, dgate, "tn", b_blk=True, out_blk=True, out_dtype=MXU_DTYPE, name="g_gate")
    g_up = _mm(vv, dup, "tn", b_blk=True, out_blk=True, out_dtype=MXU_DTYPE, name="g_up")
    pre_ffn_w = small["pre_ffn_norm_w"] + on_grads("ffn", [g_gate, g_up, g_down])
    dh1, g_pre_ffn = _rms_bwd(h1, pre_ffn_w, [dvv], res=dy, name="pre_ffn_norm_bwd")

    dmix, g_post_mix = _rms_bwd(mix, small["post_mix_norm_w"], [dh1], out_dtype=MXU_DTYPE, name="post_mix_norm_bwd")
    dattn = _mm(dmix, w_out_a, "nt", b_blk=True, out_blk=True, name="d_attn")
    dssm = _mm(dmix, w_out_s, "nt", name="d_ssm")
    g_out_a = _mm(attn, dmix, "tn", a_blk=True, out_blk=True, out_dtype=MXU_DTYPE, name="g_out_attn")
    g_out_s = _mm(ssm, dmix, "tn", out_dtype=MXU_DTYPE, name="g_out_ssm")
    g_out = jnp.concatenate([g_out_a.reshape(HEADS * VDIM, D_MODEL), g_out_s], axis=0)

    do_h, g_attn_norm = _hnorm_bwd(o_h, attn_norm_w, dattn)
    dq_h, delta = _flash_bwd_dq(q_h, k_h, v_h, o_h, do_h, lse)
    dk_h, dv_h = _flash_bwd_dkv(q_h, k_h, v_h, do_h, lse, delta)
    dq = _q_prep(dq_h, cos2, -sin2, scale, name="dq_post")

    dy_ssd, dz, g_ssd_norm = _gated_norm_bwd(y_ssd, z, small["ssd_norm_w"], dssm)
    dxbc_act, ddt, dpar = _ssd_bwd(*ssd_args, prev, dy_ssd)
    dkv, dsm = _dkv_post(dk_h, dv_h, ddt, cos2, -sin2)
    dpre, dwb = _conv_bwd_pre(xbc, conv_w, conv_b, dxbc_act)
    dxbc = _conv_bwd_in(dpre, conv_w)

    dqn = _mm(dq, w_uq, "nt", a_blk=True, b_blk=True, name="d_qn")
    dkvn = _mm(dkv, w_ukv, "nt", a_blk=True, b_blk=True, name="d_kvn")
    g_uq = _mm(qkvn, dq, "tn", b_blk=True, out_blk=True, a_cols=(0, Q_RANK), out_dtype=MXU_DTYPE, name="g_uq")
    g_ukv = _mm(qkvn, dkv, "tn", b_blk=True, out_blk=True, a_cols=(Q_RANK, KV_RANK), out_dtype=MXU_DTYPE, name="g_ukv")
    heads_token = on_grads("heads", [g_uq, g_ukv, g_out.reshape(N_DEV, D_MODEL // N_DEV, D_MODEL)])
    dcqkv, g_qkv_norm = _rms_bwd(cqkv, qkv_norm_w + heads_token, [dqn, dkvn], out_dtype=MXU_DTYPE, name="qkv_norm_bwd")

    du = _mm(dcqkv, w_cqkv, "nt", name="d_u_qkv")
    du = _mm(dz, w_z, "nt", add=du, name="d_u_z")
    du = _mm(dxbc, w_xbc, "nt", add=du, name="d_u_xbc")
    du = _mm(dsm, w_small, "nt", add=du, name="d_u_small")
    g_in = _win_from_segments(_mm(u, dcqkv, "tn", out_dtype=MXU_DTYPE, name="g_in_qkv"),
                              _mm(u, dz, "tn", out_dtype=MXU_DTYPE, name="g_in_z"),
                              _mm(u, dxbc, "tn", out_dtype=MXU_DTYPE, name="g_in_xbc"),
                              _mm(u, dsm, "tn", out_dtype=MXU_DTYPE, name="g_in_small"))
    dx, g_pre_mix = _rms_bwd(x, small["pre_mix_norm_w"], [du], res=dh1, name="pre_mix_norm_bwd")

    g_big = {"w_in": g_in}
    hl = slice(HEAD_LANE, HEAD_LANE + SSD_H)
    g_small = {"q_norm_w": g_qkv_norm[0, :Q_RANK], "kv_norm_w": g_qkv_norm[0, Q_RANK:], "conv_b": dwb[CONV_K],
               "dt_bias": dpar[0, hl], "a_log": dpar[1, hl], "d_skip": dpar[2, hl], "ssd_norm_w": g_ssd_norm,
               "attn_out_norm_w": g_attn_norm, "pre_mix_norm_w": g_pre_mix, "post_mix_norm_w": g_post_mix,
               "pre_ffn_norm_w": g_pre_ffn, "post_ffn_norm_w": g_post_ffn, "conv_w": dwb[:CONV_K]}
    return loss_blk[0, 0], dx, g_big, g_small


_WEIGHT_ORDER = ("w_in", "q_norm_w", "w_uq", "kv_norm_w", "w_ukv", "conv_w", "conv_b", "dt_bias", "a_log", "d_skip",
                 "ssd_norm_w", "attn_out_norm_w", "w_out", "pre_mix_norm_w", "post_mix_norm_w", "pre_ffn_norm_w",
                 "post_ffn_norm_w", "w_gate", "w_up", "w_down")


def kernel(x, positions, w_in, q_norm_w, w_uq, kv_norm_w, w_ukv, conv_w, conv_b, dt_bias, a_log, d_skip, ssd_norm_w, attn_out_norm_w, w_out, pre_mix_norm_w, post_mix_norm_w, pre_ffn_norm_w, post_ffn_norm_w, w_gate, w_up, w_down, loss_target, m_w_in, m_q_norm_w, m_w_uq, m_kv_norm_w, m_w_ukv, m_conv_w, m_conv_b, m_dt_bias, m_a_log, m_d_skip, m_ssd_norm_w, m_attn_out_norm_w, m_w_out, m_pre_mix_norm_w, m_post_mix_norm_w, m_pre_ffn_norm_w, m_post_ffn_norm_w, m_w_gate, m_w_up, m_w_down, v_w_in, v_q_norm_w, v_w_uq, v_kv_norm_w, v_w_ukv, v_conv_w, v_conv_b, v_dt_bias, v_a_log, v_d_skip, v_ssd_norm_w, v_attn_out_norm_w, v_w_out, v_pre_mix_norm_w, v_post_mix_norm_w, v_pre_ffn_norm_w, v_post_ffn_norm_w, v_w_gate, v_w_up, v_w_down):
    w = dict(w_in=w_in, q_norm_w=q_norm_w, w_uq=w_uq, kv_norm_w=kv_norm_w, w_ukv=w_ukv, conv_w=conv_w, conv_b=conv_b,
             dt_bias=dt_bias, a_log=a_log, d_skip=d_skip, ssd_norm_w=ssd_norm_w, attn_out_norm_w=attn_out_norm_w,
             w_out=w_out, pre_mix_norm_w=pre_mix_norm_w, post_mix_norm_w=post_mix_norm_w,
             pre_ffn_norm_w=pre_ffn_norm_w, post_ffn_norm_w=post_ffn_norm_w, w_gate=w_gate, w_up=w_up, w_down=w_down)
    m = dict(w_in=m_w_in, q_norm_w=m_q_norm_w, w_uq=m_w_uq, kv_norm_w=m_kv_norm_w, w_ukv=m_w_ukv, conv_w=m_conv_w,
             conv_b=m_conv_b, dt_bias=m_dt_bias, a_log=m_a_log, d_skip=m_d_skip, ssd_norm_w=m_ssd_norm_w,
             attn_out_norm_w=m_attn_out_norm_w, w_out=m_w_out, pre_mix_norm_w=m_pre_mix_norm_w,
             post_mix_norm_w=m_post_mix_norm_w, pre_ffn_norm_w=m_pre_ffn_norm_w, post_ffn_norm_w=m_post_ffn_norm_w,
             w_gate=m_w_gate, w_up=m_w_up, w_down=m_w_down)
    v = dict(w_in=v_w_in, q_norm_w=v_q_norm_w, w_uq=v_w_uq, kv_norm_w=v_kv_norm_w, w_ukv=v_w_ukv, conv_w=v_conv_w,
             conv_b=v_conv_b, dt_bias=v_dt_bias, a_log=v_a_log, d_skip=v_d_skip, ssd_norm_w=v_ssd_norm_w,
             attn_out_norm_w=v_attn_out_norm_w, w_out=v_w_out, pre_mix_norm_w=v_pre_mix_norm_w,
             post_mix_norm_w=v_post_mix_norm_w, pre_ffn_norm_w=v_pre_ffn_norm_w, post_ffn_norm_w=v_post_ffn_norm_w,
             w_gate=v_w_gate, w_up=v_w_up, w_down=v_w_down)
    w, m, v = ({k: t[0] for k, t in d.items()} for d in (w, m, v))
    me = 4 * lax.axis_index("x") + 2 * lax.axis_index("y") + lax.axis_index("c")
    groups = {"heads": ("w_uq", "w_ukv", "w_out"), "ffn": ("w_gate", "w_up", "w_down")}
    cshard = CONV_DIM // N_DEV

    shards = [w["w_in"].astype(MXU_DTYPE),
              jnp.stack(_split3(w["conv_w"])).reshape(3 * CONV_K, cshard).astype(MXU_DTYPE)]
    w_in_g, cw = _all_gather(shards, name="gather_weights")
    cw = cw.astype(F32).reshape(N_DEV, 3, CONV_K, cshard)
    wg = {"w_in": w_in_g, "conv_w": jnp.transpose(cw[:, 0] + cw[:, 1] + cw[:, 2], (1, 0, 2)).reshape(CONV_K, CONV_DIM)}
    arriving, dep, started = {}, wg["conv_w"], jnp.zeros((), F32)
    small = {name: w[name] for name, _ in _SMALL if name != "conv_w"}
    for group in ("heads", "ffn"):
        token, arriving[group] = _exchange_behind([w[name].astype(MXU_DTYPE) for name in groups[group]], False,
                                                  dep, group + "_weights")
        started = started + token
        dep = jnp.zeros((8, LANE), F32) + started
    small["pre_mix_norm_w"] = small["pre_mix_norm_w"] + started

    leaving = {}

    def on_grads(group, gs):
        token, leaving[group] = _exchange_behind(gs, True, jnp.zeros((8, LANE), F32), group + "_grads")
        return token

    loss_local, dx, g_big, g_small = _local_step(x[0], positions[0], loss_target[0], wg, small,
                                                 lambda group, after: arriving[group](after), on_grads)
    loss = lax.psum(loss_local, ("x", "y", "c"))

    recv = {"w_in": _all_to_all([g_big["w_in"]], name="exchange_grads")[0]}
    for group in ("heads", "ffn"):
        recv.update(zip(groups[group], leaving[group](dx)))
    grads, deltas, new_m, new_v = {}, {}, {}, {}
    for name, parts in recv.items():
        grads[name], deltas[name], new_m[name], new_v[name] = _adamw(parts, w[name], m[name], v[name],
                                                                     name="adamw_" + name)

    def embed(t):
        return lax.dynamic_update_slice(jnp.zeros((CONV_K, CONV_DIM), F32), t, (0, me * cshard))

    parts_s = _all_gather([_pack_small(g_small)], name="gather_small_grads")[0]
    packs = [_pack_small({**{n_: d[n_] for n_, _ in _SMALL if n_ != "conv_w"}, "conv_w": embed(d["conv_w"])})
             for d in (w, m, v)]
    outs = [_unpack_small(t) for t in _adamw_small(parts_s, *packs)]
    for name, n in _SMALL:
        for dst, src in zip((grads, deltas, new_m, new_v), outs):
            if name == "conv_w":
                dst[name] = lax.dynamic_slice(src[name].reshape(CONV_K, CONV_DIM), (0, me * cshard), (CONV_K, cshard))
            else:
                dst[name] = src[name]

    def lead(d):
        return [d[name][None] for name in _WEIGHT_ORDER]

    return (loss, dx[None], *lead(grads), *lead(deltas), *lead(new_m), *lead(new_v))
```

```python
import numpy as np

import jax
import jax.numpy as jnp
from jax import lax
from jax.experimental import pallas as pl
from jax.experimental.pallas import tpu as pltpu

F32 = jnp.float32
BF16 = jnp.bfloat16
MXU_DTYPE = jnp.bfloat16
EPS = 1e-6
VMEM_LIMIT_BYTES = 48 * 1024 * 1024
K_TILE_MAX = 2048

N_DEV = 8
D_MODEL = 2048
Q_RANK = 512
KV_RANK = 512
ROPE = 64
HALF = ROPE // 2
HEADS = 8
NOPE = 128
VDIM = 128
QK = NOPE + ROPE
SSD_W = 1024
SSD_H = 16
SSD_P = 64
SSD_G = 2
SSD_E = SSD_H // SSD_G
SSD_N = 128
CHUNK = 128
CONV_K = 4
CONV_DIM = SSD_W + 2 * SSD_G * SSD_N
B_OFF = SSD_W
C_OFF = SSD_W + SSD_G * SSD_N
D_FF = 5632
D_IN = Q_RANK + KV_RANK + ROPE + SSD_W + CONV_DIM + SSD_H
ROPE_THETA = 10000.0
LANE = 128
HEAD_LANE = ROPE

ADAM_LR = 0.001
ADAM_B1 = 0.9
ADAM_B2 = 0.999
ADAM_EPS = 1e-08
ADAM_WD = 0.01
ADAM_STEP = 10


def _pick(n, cands):
    for c in cands:
        if n % c == 0:
            return c
    return n


def _params(*sem):
    return pltpu.CompilerParams(dimension_semantics=sem, vmem_limit_bytes=VMEM_LIMIT_BYTES)


def _sigmoid(x):
    return 1.0 / (1.0 + jnp.exp(-x))


def _silu(x):
    return x * _sigmoid(x)


def _dsilu(x):
    s = _sigmoid(x)
    return s * (1.0 + x * (1.0 - s))


def _softplus(x):
    e = jnp.exp(-jnp.abs(x))
    small = e * (1.0 - e * (0.5 - e * (1.0 / 3.0)))
    return jnp.maximum(x, 0.0) + jnp.where(e < 0.01, small, jnp.log(1.0 + e))


def _dot(a, b, ca, cb):
    return lax.dot_general(a, b, (((ca,), (cb,)), ((), ())), preferred_element_type=F32)


def _mx(v):
    return v.astype(MXU_DTYPE)


def _split3(a):
    hi = a.astype(BF16)
    r1 = a - hi.astype(F32)
    mid = r1.astype(BF16)
    lo = (r1 - mid.astype(F32)).astype(BF16)
    return hi, mid, lo


def _exact_dot(a, b, ca, cb, split_a):
    if split_a:
        return sum(_dot(p, b, ca, cb) for p in _split3(a))
    return sum(_dot(a, p, ca, cb) for p in _split3(b))


def _mm(a, b, mode, *, a_blk=False, b_blk=False, out_blk=False, a_cols=None, add=None, out_dtype=F32, fuse=1,
        name="mm"):
    a2, b2 = a.shape[-2:], b.shape[-2:]
    a_last = a2[1] if a_cols is None else a_cols[1]
    a_start = 0 if a_cols is None else a_cols[0]
    if mode == "nn":
        m, k, (k2, n) = a2[0], a_last, b2
    elif mode == "nt":
        m, k, (n, k2) = a2[0], a_last, b2
    else:
        k, m, (k2, n) = a2[0], a_last, b2
    assert k == k2, (a.shape, b.shape, mode)
    tm = _pick(m, (1024, 704, 512, 256, 128))
    tn = _pick(n, (1024, 768, 704, 512, 256, 192, 128))
    tk = k if k <= K_TILE_MAX else _pick(k, (K_TILE_MAX, 1024, 512))
    nk = k // tk
    jo = N_DEV if out_blk else 1
    reduce_blocks = a_blk and b_blk and not out_blk
    assert fuse == 1 or reduce_blocks
    jr = N_DEV // fuse if reduce_blocks else 1
    ca, cb = {"nn": (1, 0), "nt": (1, 1), "tn": (0, 0)}[mode]
    has_add = add is not None
    single = jr * nk == 1
    if mode == "tn":
        assert a_start % tm == 0
        a_block, a_idx = (tk, tm), (lambda i, kk: (kk, i + a_start // tm))
    else:
        assert a_start % tk == 0
        a_block, a_idx = (tm, tk), (lambda i, kk: (i, kk + a_start // tk))
    b_block, b_idx = ((tn, tk), (lambda nn_, kk: (nn_, kk))) if mode == "nt" else ((tk, tn), (lambda nn_, kk: (kk, nn_)))

    def blk_specs(blocked, block, idx, of_a, t):
        def pos(o, i, nn_, kk):
            return idx(i, kk) if of_a else idx(nn_, kk)
        if blocked:
            return pl.BlockSpec((None,) + block,
                                lambda o, i, nn_, r, kk: ((o if out_blk else r * fuse + t),) + pos(o, i, nn_, kk))
        return pl.BlockSpec(block, lambda o, i, nn_, r, kk: pos(o, i, nn_, kk))

    a_specs = [blk_specs(a_blk, a_block, a_idx, True, t) for t in range(fuse)]
    b_specs = [blk_specs(b_blk, b_block, b_idx, False, t) for t in range(fuse)]
    o_spec = (pl.BlockSpec((None, tm, tn), lambda o, i, nn_, r, kk: (o, i, nn_)) if out_blk
              else pl.BlockSpec((tm, tn), lambda o, i, nn_, r, kk: (i, nn_)))

    def body(*refs):
        a_refs, b_refs = refs[:fuse], refs[fuse:2 * fuse]
        add_ref = refs[2 * fuse] if has_add else None
        o_ref = refs[2 * fuse + 1] if has_add else refs[2 * fuse]
        part = _dot(_mx(a_refs[0][...]), _mx(b_refs[0][...]), ca, cb)
        for t in range(1, fuse):
            part = part + _dot(_mx(a_refs[t][...]), _mx(b_refs[t][...]), ca, cb)
        if single:
            if has_add:
                part = part + add_ref[...]
            o_ref[...] = part.astype(o_ref.dtype)
            return
        acc = refs[-1]
        r, kk = pl.program_id(3), pl.program_id(4)
        first = jnp.logical_and(r == 0, kk == 0)
        last = jnp.logical_and(r == jr - 1, kk == nk - 1)

        @pl.when(first)
        def _():
            acc[...] = part

        @pl.when(jnp.logical_not(first))
        def _():
            acc[...] += part

        @pl.when(last)
        def _():
            res = acc[...]
            if has_add:
                res = res + add_ref[...]
            o_ref[...] = res.astype(o_ref.dtype)

    out_shape = ((N_DEV, m, n) if out_blk else (m, n))
    return pl.pallas_call(
        body, name=name, grid=(jo, m // tm, n // tn, jr, nk),
        in_specs=a_specs + b_specs + ([o_spec] if has_add else []), out_specs=o_spec,
        out_shape=jax.ShapeDtypeStruct(out_shape, out_dtype),
        scratch_shapes=[] if single else [pltpu.VMEM((tm, tn), F32)],
        compiler_params=_params("parallel", "parallel", "parallel", "arbitrary", "arbitrary"),
    )(*((a,) * fuse + (b,) * fuse + ((add,) if has_add else ())))


def _row_tile(r_):
    return _pick(r_, (256, 128, 64, 32, 16, 8))


def _rms_fwd(t, w, groups=1, res=None, out_dtype=F32, name="rms_fwd"):
    r_, f = t.shape
    fg = f // groups
    tr = _row_tile(r_)
    has_res = res is not None

    def body(*refs):
        t_ref, w_ref = refs[0], refs[1]
        res_ref = refs[2] if has_res else None
        o_ref = refs[-1]
        for g in range(groups):
            sl = slice(g * fg, (g + 1) * fg)
            tv = t_ref[:, sl].astype(F32)
            r = lax.rsqrt(jnp.mean(tv * tv, axis=-1, keepdims=True) + EPS)
            y = tv * r * w_ref[:, sl]
            if has_res:
                y = y + res_ref[:, sl]
            o_ref[:, sl] = y.astype(o_ref.dtype)

    row = pl.BlockSpec((tr, f), lambda i: (i, 0))
    wsp = pl.BlockSpec((1, f), lambda i: (0, 0))
    return pl.pallas_call(
        body, name=name, grid=(r_ // tr,),
        in_specs=[row, wsp] + ([row] if has_res else []), out_specs=row,
        out_shape=jax.ShapeDtypeStruct((r_, f), out_dtype),
        compiler_params=_params("parallel"),
    )(*((t, w.reshape(1, f)) + ((res,) if has_res else ())))


def _rms_bwd(t, w, dys, res=None, out_dtype=F32, name="rms_bwd"):
    r_, f = t.shape
    groups = len(dys)
    fg = f // groups
    tr = _row_tile(r_)
    has_res = res is not None

    def body(*refs):
        t_ref, w_ref = refs[0], refs[1]
        dy_refs = refs[2:2 + groups]
        res_ref = refs[2 + groups] if has_res else None
        dt_ref, dw_ref = refs[-2], refs[-1]

        @pl.when(pl.program_id(0) == 0)
        def _():
            dw_ref[...] = jnp.zeros_like(dw_ref)

        for g in range(groups):
            sl = slice(g * fg, (g + 1) * fg)
            tv = t_ref[:, sl].astype(F32)
            dyv = dy_refs[g][...].astype(F32)
            r = lax.rsqrt(jnp.mean(tv * tv, axis=-1, keepdims=True) + EPS)
            gw = dyv * w_ref[:, sl]
            c = jnp.mean(gw * tv, axis=-1, keepdims=True)
            dt = r * gw - tv * (r * r * r * c)
            if has_res:
                dt = dt + res_ref[:, sl]
            dt_ref[:, sl] = dt.astype(dt_ref.dtype)
            dw_ref[:, sl] += jnp.sum(dyv * tv * r, axis=0, keepdims=True)

    row = pl.BlockSpec((tr, f), lambda i: (i, 0))
    grow = pl.BlockSpec((tr, fg), lambda i: (i, 0))
    wsp = pl.BlockSpec((1, f), lambda i: (0, 0))
    return pl.pallas_call(
        body, name=name, grid=(r_ // tr,),
        in_specs=[row, wsp] + [grow] * groups + ([row] if has_res else []), out_specs=[row, wsp],
        out_shape=[jax.ShapeDtypeStruct((r_, f), out_dtype), jax.ShapeDtypeStruct((1, f), F32)],
        compiler_params=_params("arbitrary"),
    )(*((t, w.reshape(1, f)) + tuple(dys) + ((res,) if has_res else ())))


def _hnorm_fwd(o, w, name="attn_out_norm"):
    h, s_, v = o.shape
    tr = _row_tile(s_)

    def body(o_ref, w_ref, y_ref):
        ss = jnp.sum(o_ref[0] * o_ref[0], axis=-1, keepdims=True)
        for i in range(1, h):
            ss = ss + jnp.sum(o_ref[i] * o_ref[i], axis=-1, keepdims=True)
        r = lax.rsqrt(ss * (1.0 / (h * v)) + EPS)
        for i in range(h):
            y_ref[i] = (o_ref[i] * r * w_ref[i]).astype(y_ref.dtype)

    blk = pl.BlockSpec((h, tr, v), lambda i: (0, i, 0))
    wsp = pl.BlockSpec((h, 1, v), lambda i: (0, 0, 0))
    return pl.pallas_call(
        body, name=name, grid=(s_ // tr,), in_specs=[blk, wsp], out_specs=blk,
        out_shape=jax.ShapeDtypeStruct(o.shape, MXU_DTYPE), compiler_params=_params("parallel"),
    )(o, w)


def _hnorm_bwd(o, w, dy, name="attn_out_norm_bwd"):
    h, s_, v = o.shape
    tr = _row_tile(s_)

    def body(o_ref, w_ref, dy_ref, do_ref, dw_ref):
        @pl.when(pl.program_id(0) == 0)
        def _():
            dw_ref[...] = jnp.zeros_like(dw_ref)

        ss = jnp.zeros((tr, 1), F32)
        cc = jnp.zeros((tr, 1), F32)
        for i in range(h):
            ov = o_ref[i]
            ss = ss + jnp.sum(ov * ov, axis=-1, keepdims=True)
            cc = cc + jnp.sum(dy_ref[i] * w_ref[i] * ov, axis=-1, keepdims=True)
        r = lax.rsqrt(ss * (1.0 / (h * v)) + EPS)
        c = cc * (1.0 / (h * v))
        for i in range(h):
            ov = o_ref[i]
            dyv = dy_ref[i]
            do_ref[i] = r * dyv * w_ref[i] - ov * (r * r * r * c)
            dw_ref[i] += jnp.sum(dyv * ov * r, axis=0, keepdims=True)

    blk = pl.BlockSpec((h, tr, v), lambda i: (0, i, 0))
    wsp = pl.BlockSpec((h, 1, v), lambda i: (0, 0, 0))
    return pl.pallas_call(
        body, name=name, grid=(s_ // tr,), in_specs=[blk, wsp, blk], out_specs=[blk, wsp],
        out_shape=[jax.ShapeDtypeStruct(o.shape, F32), jax.ShapeDtypeStruct((h, 1, v), F32)],
        compiler_params=_params("arbitrary"),
    )(o, w, dy)


def _loss_head(ffn, h1, target, w, name="loss_head"):
    r_, f = ffn.shape
    tr = _row_tile(r_)

    def body(ffn_ref, h1_ref, tg_ref, w_ref, loss_ref, dy_ref, dffn_ref, dw_ref):
        @pl.when(pl.program_id(0) == 0)
        def _():
            dw_ref[...] = jnp.zeros_like(dw_ref)
            loss_ref[...] = jnp.zeros_like(loss_ref)

        tv = ffn_ref[...]
        wv = w_ref[...]
        r = lax.rsqrt(jnp.mean(tv * tv, axis=-1, keepdims=True) + EPS)
        tn = tv * r
        e = h1_ref[...] + tn * wv - tg_ref[...]
        tot = jnp.sum(jnp.sum(e * e, axis=1, keepdims=True), axis=0, keepdims=True) * (0.5 / f)
        loss_ref[...] += tot + jnp.zeros_like(loss_ref)
        dyv = e * (1.0 / f)
        dy_ref[...] = dyv
        gw = dyv * wv
        c = jnp.mean(gw * tv, axis=-1, keepdims=True)
        dffn_ref[...] = (r * gw - tv * (r * r * r * c)).astype(dffn_ref.dtype)
        dw_ref[...] += jnp.sum(dyv * tn, axis=0, keepdims=True)

    row = pl.BlockSpec((tr, f), lambda i: (i, 0))
    wsp = pl.BlockSpec((1, f), lambda i: (0, 0))
    lsp = pl.BlockSpec((1, LANE), lambda i: (0, 0))
    return pl.pallas_call(
        body, name=name, grid=(r_ // tr,),
        in_specs=[row, row, row, wsp], out_specs=[lsp, row, row, wsp],
        out_shape=[jax.ShapeDtypeStruct((1, LANE), F32), jax.ShapeDtypeStruct((r_, f), F32),
                   jax.ShapeDtypeStruct((r_, f), MXU_DTYPE), jax.ShapeDtypeStruct((1, f), F32)],
        compiler_params=_params("arbitrary"),
    )(ffn, h1, target, w.reshape(1, f))


def _rot_matrix():
    p = np.zeros((ROPE, ROPE), np.float32)
    for i in range(HALF):
        p[i + HALF, i] = -1.0
        p[i, i + HALF] = 1.0
    return jnp.asarray(p, BF16)


def _rope_val(r, c2, s2, rot):
    return r * c2 + _exact_dot(r, rot, 1, 0, True) * s2


def _q_prep(q, cos2, sin2, scale, name):
    h, s_, _ = q.shape
    tr = _pick(s_, (1024, 512, 256, 128, 64, 32, 16, 8))

    def body(q_ref, c_ref, s_ref, rot_ref, o_ref):
        x = q_ref[...]
        o_ref[:, :NOPE] = (x[:, :NOPE] * scale).astype(o_ref.dtype)
        o_ref[:, NOPE:] = (_rope_val(x[:, NOPE:], c_ref[...], s_ref[...], rot_ref[...]) * scale).astype(o_ref.dtype)

    blk = pl.BlockSpec((None, tr, QK), lambda hh, i: (hh, i, 0))
    csp = pl.BlockSpec((tr, ROPE), lambda hh, i: (i, 0))
    return pl.pallas_call(
        body, name=name, grid=(h, s_ // tr),
        in_specs=[blk, csp, csp, pl.BlockSpec((ROPE, ROPE), lambda hh, i: (0, 0))], out_specs=blk,
        out_shape=jax.ShapeDtypeStruct(q.shape, MXU_DTYPE), compiler_params=_params("parallel", "parallel"),
    )(q, cos2, sin2, _rot_matrix())


def _kv_prep(kv, small, cos2, sin2, name="kv_prep"):
    h, s_, _ = kv.shape
    tr = _row_tile(s_)

    def body(kv_ref, sm_ref, c_ref, s_ref, rot_ref, k_ref, v_ref):
        kr = _rope_val(sm_ref[:, :ROPE], c_ref[...], s_ref[...], rot_ref[...]).astype(k_ref.dtype)
        for i in range(h):
            k_ref[i, :, :NOPE] = kv_ref[i, :, :NOPE].astype(k_ref.dtype)
            k_ref[i, :, NOPE:] = kr
            v_ref[i] = kv_ref[i, :, NOPE:].astype(v_ref.dtype)

    csp = pl.BlockSpec((tr, ROPE), lambda i: (i, 0))
    return pl.pallas_call(
        body, name=name, grid=(s_ // tr,),
        in_specs=[pl.BlockSpec((h, tr, NOPE + VDIM), lambda i: (0, i, 0)), pl.BlockSpec((tr, LANE), lambda i: (i, 0)),
                  csp, csp, pl.BlockSpec((ROPE, ROPE), lambda i: (0, 0))],
        out_specs=[pl.BlockSpec((h, tr, QK), lambda i: (0, i, 0)), pl.BlockSpec((h, tr, VDIM), lambda i: (0, i, 0))],
        out_shape=[jax.ShapeDtypeStruct((h, s_, QK), MXU_DTYPE), jax.ShapeDtypeStruct((h, s_, VDIM), MXU_DTYPE)],
        compiler_params=_params("parallel"),
    )(kv, small, cos2, sin2, _rot_matrix())


def _dkv_post(dk, dv, ddt, cos2, nsin2, name="dkv_post"):
    h, s_, _ = dk.shape
    tr = _row_tile(s_)

    def body(dk_ref, dv_ref, ddt_ref, c_ref, s_ref, rot_ref, dkv_ref, dsm_ref):
        acc = dk_ref[0, :, NOPE:]
        for i in range(1, h):
            acc = acc + dk_ref[i, :, NOPE:]
        dsm_ref[:, :ROPE] = _rope_val(acc, c_ref[...], s_ref[...], rot_ref[...]).astype(dsm_ref.dtype)
        dsm_ref[:, ROPE:] = ddt_ref[:, ROPE:].astype(dsm_ref.dtype)
        for i in range(h):
            dkv_ref[i, :, :NOPE] = dk_ref[i, :, :NOPE].astype(dkv_ref.dtype)
            dkv_ref[i, :, NOPE:] = dv_ref[i].astype(dkv_ref.dtype)

    csp = pl.BlockSpec((tr, ROPE), lambda i: (i, 0))
    return pl.pallas_call(
        body, name=name, grid=(s_ // tr,),
        in_specs=[pl.BlockSpec((h, tr, QK), lambda i: (0, i, 0)), pl.BlockSpec((h, tr, VDIM), lambda i: (0, i, 0)),
                  pl.BlockSpec((tr, LANE), lambda i: (i, 0)), csp, csp, pl.BlockSpec((ROPE, ROPE), lambda i: (0, 0))],
        out_specs=[pl.BlockSpec((h, tr, NOPE + VDIM), lambda i: (0, i, 0)), pl.BlockSpec((tr, LANE), lambda i: (i, 0))],
        out_shape=[jax.ShapeDtypeStruct((h, s_, NOPE + VDIM), MXU_DTYPE), jax.ShapeDtypeStruct((s_, LANE), MXU_DTYPE)],
        compiler_params=_params("parallel"),
    )(dk, dv, ddt, cos2, nsin2, _rot_matrix())


def _attn_tile(s):
    return 512 if s % 1024 == 0 else s // 2


def _pairs(n, by_key):
    if by_key:
        pr = [(i, j) for j in range(n) for i in range(j, n)]
    else:
        pr = [(i, j) for i in range(n) for j in range(i + 1)]
    return (jnp.asarray([p[0] for p in pr], jnp.int32), jnp.asarray([p[1] for p in pr], jnp.int32))


ATTN_ROW_GROUPS = 2


def _row_groups(t, diag):
    tg = t // ATTN_ROW_GROUPS
    out = []
    for r in range(ATTN_ROW_GROUPS):
        nc = (r + 1) * tg if diag else t
        mask = None
        if diag:
            mask = (lax.broadcasted_iota(jnp.int32, (tg, nc), 1)
                    <= lax.broadcasted_iota(jnp.int32, (tg, nc), 0) + r * tg)
        out.append((slice(r * tg, (r + 1) * tg), nc, mask))
    return out


def _flash_specs(t, dk, dv):
    qsp = pl.BlockSpec((None, t, dk), lambda hh, p, qi, kj: (hh, qi[p], 0))
    ksp = pl.BlockSpec((None, t, dk), lambda hh, p, qi, kj: (hh, kj[p], 0))
    vsp = pl.BlockSpec((None, t, dv), lambda hh, p, qi, kj: (hh, kj[p], 0))
    osp = pl.BlockSpec((None, t, dv), lambda hh, p, qi, kj: (hh, qi[p], 0))
    lsp = pl.BlockSpec((None, t, 1), lambda hh, p, qi, kj: (hh, qi[p], 0))
    return qsp, ksp, vsp, osp, lsp


def _flash_fwd(q, k, v, name="flash_fwd"):
    h, s_, dk = q.shape
    dv = v.shape[-1]
    t = _attn_tile(s_)
    n = s_ // t
    qi, kj = _pairs(n, False)

    def body(qi_ref, kj_ref, q_ref, k_ref, v_ref, o_ref, lse_ref, m_s, l_s, acc):
        p_ = pl.program_id(1)
        i, j = qi_ref[p_], kj_ref[p_]

        @pl.when(j == 0)
        def _():
            m_s[...] = jnp.full_like(m_s, -jnp.inf)
            l_s[...] = jnp.zeros_like(l_s)
            acc[...] = jnp.zeros_like(acc)

        def update(diag):
            for rs, nc, mask in _row_groups(t, diag):
                sc = _dot(q_ref[rs, :], k_ref[0:nc, :], 1, 1)
                if mask is not None:
                    sc = jnp.where(mask, sc, -jnp.inf)
                m_old = m_s[rs, :]
                m_new = jnp.maximum(m_old, jnp.max(sc, axis=1, keepdims=True))
                alpha = jnp.exp(m_old - m_new)
                p = jnp.exp(sc - m_new)
                l_s[rs, :] = alpha * l_s[rs, :] + jnp.sum(p, axis=1, keepdims=True)
                acc[rs, :] = alpha * acc[rs, :] + _dot(_mx(p), v_ref[0:nc, :], 1, 0)
                m_s[rs, :] = m_new

        @pl.when(j < i)
        def _():
            update(False)

        @pl.when(j == i)
        def _():
            update(True)
            o_ref[...] = acc[...] / l_s[...]
            lse_ref[...] = m_s[...] + jnp.log(l_s[...])

    qsp, ksp, vsp, osp, lsp = _flash_specs(t, dk, dv)
    gs = pltpu.PrefetchScalarGridSpec(
        num_scalar_prefetch=2, grid=(h, qi.shape[0]), in_specs=[qsp, ksp, vsp], out_specs=[osp, lsp],
        scratch_shapes=[pltpu.VMEM((t, 1), F32), pltpu.VMEM((t, 1), F32), pltpu.VMEM((t, dv), F32)])
    return pl.pallas_call(
        body, name=name, grid_spec=gs,
        out_shape=[jax.ShapeDtypeStruct((h, s_, dv), F32), jax.ShapeDtypeStruct((h, s_, 1), F32)],
        compiler_params=_params("parallel", "arbitrary"),
    )(qi, kj, q, k, v)


def _flash_bwd_dq(q, k, v, o, do, lse, name="flash_bwd_dq"):
    h, s_, dk = q.shape
    dv = v.shape[-1]
    t = _attn_tile(s_)
    n = s_ // t
    qi, kj = _pairs(n, False)

    def body(qi_ref, kj_ref, q_ref, k_ref, v_ref, o_ref, do_ref, lse_ref, dq_ref, delta_ref, acc, delta_s):
        p_ = pl.program_id(1)
        i, j = qi_ref[p_], kj_ref[p_]

        @pl.when(j == 0)
        def _():
            delta_s[...] = jnp.sum(do_ref[...] * o_ref[...], axis=1, keepdims=True)
            acc[...] = jnp.zeros_like(acc)

        def update(diag):
            for rs, nc, mask in _row_groups(t, diag):
                sc = _dot(q_ref[rs, :], k_ref[0:nc, :], 1, 1)
                if mask is not None:
                    sc = jnp.where(mask, sc, -jnp.inf)
                p = jnp.exp(sc - lse_ref[rs, :])
                dp = _dot(_mx(do_ref[rs, :]), v_ref[0:nc, :], 1, 1)
                ds = p * (dp - delta_s[rs, :])
                acc[rs, :] += _dot(_mx(ds), k_ref[0:nc, :], 1, 0)

        @pl.when(j < i)
        def _():
            update(False)

        @pl.when(j == i)
        def _():
            update(True)
            dq_ref[...] = acc[...]
            delta_ref[...] = delta_s[...]

    qsp, ksp, vsp, osp, lsp = _flash_specs(t, dk, dv)
    gs = pltpu.PrefetchScalarGridSpec(
        num_scalar_prefetch=2, grid=(h, qi.shape[0]), in_specs=[qsp, ksp, vsp, osp, osp, lsp], out_specs=[qsp, lsp],
        scratch_shapes=[pltpu.VMEM((t, dk), F32), pltpu.VMEM((t, 1), F32)])
    return pl.pallas_call(
        body, name=name, grid_spec=gs,
        out_shape=[jax.ShapeDtypeStruct((h, s_, dk), F32), jax.ShapeDtypeStruct((h, s_, 1), F32)],
        compiler_params=_params("parallel", "arbitrary"),
    )(qi, kj, q, k, v, o, do, lse)


def _flash_bwd_dkv(q, k, v, do, lse, delta, name="flash_bwd_dkv"):
    h, s_, dk = q.shape
    dv = v.shape[-1]
    t = _attn_tile(s_)
    n = s_ // t
    qi, kj = _pairs(n, True)

    def body(qi_ref, kj_ref, q_ref, k_ref, v_ref, do_ref, lse_ref, delta_ref, dk_ref, dv_ref, dk_acc, dv_acc):
        p_ = pl.program_id(1)
        i, j = qi_ref[p_], kj_ref[p_]

        def update(diag):
            for rs, nc, mask in _row_groups(t, diag):
                sc = _dot(q_ref[rs, :], k_ref[0:nc, :], 1, 1)
                if mask is not None:
                    sc = jnp.where(mask, sc, -jnp.inf)
                p = jnp.exp(sc - lse_ref[rs, :])
                dob = _mx(do_ref[rs, :])
                dv_acc[0:nc, :] += _dot(_mx(p), dob, 0, 0)
                dp = _dot(dob, v_ref[0:nc, :], 1, 1)
                ds = p * (dp - delta_ref[rs, :])
                dk_acc[0:nc, :] += _dot(_mx(ds), q_ref[rs, :], 0, 0)

        @pl.when(i == j)
        def _():
            dk_acc[...] = jnp.zeros_like(dk_acc)
            dv_acc[...] = jnp.zeros_like(dv_acc)
            update(True)

        @pl.when(i > j)
        def _():
            update(False)

        @pl.when(i == n - 1)
        def _():
            dk_ref[...] = dk_acc[...]
            dv_ref[...] = dv_acc[...]

    qsp, ksp, vsp, osp, lsp = _flash_specs(t, dk, dv)
    gs = pltpu.PrefetchScalarGridSpec(
        num_scalar_prefetch=2, grid=(h, qi.shape[0]), in_specs=[qsp, ksp, vsp, osp, lsp, lsp], out_specs=[ksp, vsp],
        scratch_shapes=[pltpu.VMEM((t, dk), F32), pltpu.VMEM((t, dv), F32)])
    return pl.pallas_call(
        body, name=name, grid_spec=gs,
        out_shape=[jax.ShapeDtypeStruct((h, s_, dk), F32), jax.ShapeDtypeStruct((h, s_, dv), F32)],
        compiler_params=_params("parallel", "arbitrary"),
    )(qi, kj, q, k, v, do, lse, delta)


HALO = 8


def _conv_specs(s_, c, tr, after):
    main = pl.BlockSpec((tr, c), lambda i: (i, 0))
    per = tr // HALO
    if after:
        halo = pl.BlockSpec((HALO, c), lambda i: (jnp.minimum((i + 1) * per, s_ // HALO - 1), 0))
    else:
        halo = pl.BlockSpec((HALO, c), lambda i: (jnp.maximum(i * per - 1, 0), 0))
    return main, halo


def _fill_before(ext, t_ref, h_ref, tr):
    ext[0:HALO, :] = jnp.where(pl.program_id(0) > 0, h_ref[...], 0.0)
    ext[HALO:HALO + tr, :] = t_ref[...]


def _taps(ext, w_ref, tr):
    base = HALO - (CONV_K - 1)
    acc = ext[base:base + tr, :] * w_ref[0:1, :]
    for k in range(1, CONV_K):
        acc = acc + ext[base + k:base + k + tr, :] * w_ref[k:k + 1, :]
    return acc


def _conv_fwd(t, w, b, name="conv_fwd"):
    s_, c = t.shape
    tr = _row_tile(s_)

    def body(t_ref, h_ref, w_ref, b_ref, o_ref, ext):
        _fill_before(ext, t_ref, h_ref, tr)
        o_ref[...] = _silu(_taps(ext, w_ref, tr) + b_ref[...])

    main, halo = _conv_specs(s_, c, tr, False)
    return pl.pallas_call(
        body, name=name, grid=(s_ // tr,),
        in_specs=[main, halo, pl.BlockSpec((CONV_K, c), lambda i: (0, 0)), pl.BlockSpec((1, c), lambda i: (0, 0))],
        out_specs=main, out_shape=jax.ShapeDtypeStruct((s_, c), F32),
        scratch_shapes=[pltpu.VMEM((tr + HALO, c), F32)], compiler_params=_params("parallel"),
    )(t, t, w, b)


def _conv_bwd_pre(t, w, b, dact, name="conv_bwd_pre"):
    s_, c = t.shape
    tr = _row_tile(s_)

    def body(t_ref, h_ref, w_ref, b_ref, da_ref, dpre_ref, dwb_ref, ext):
        @pl.when(pl.program_id(0) == 0)
        def _():
            dwb_ref[...] = jnp.zeros_like(dwb_ref)

        _fill_before(ext, t_ref, h_ref, tr)
        dpre = da_ref[...] * _dsilu(_taps(ext, w_ref, tr) + b_ref[...])
        dpre_ref[...] = dpre
        base = HALO - (CONV_K - 1)
        for k in range(CONV_K):
            dwb_ref[k:k + 1, :] += jnp.sum(dpre * ext[base + k:base + k + tr, :], axis=0, keepdims=True)
        dwb_ref[CONV_K:CONV_K + 1, :] += jnp.sum(dpre, axis=0, keepdims=True)

    main, halo = _conv_specs(s_, c, tr, False)
    return pl.pallas_call(
        body, name=name, grid=(s_ // tr,),
        in_specs=[main, halo, pl.BlockSpec((CONV_K, c), lambda i: (0, 0)), pl.BlockSpec((1, c), lambda i: (0, 0)), main],
        out_specs=[main, pl.BlockSpec((8, c), lambda i: (0, 0))],
        out_shape=[jax.ShapeDtypeStruct((s_, c), F32), jax.ShapeDtypeStruct((8, c), F32)],
        scratch_shapes=[pltpu.VMEM((tr + HALO, c), F32)], compiler_params=_params("arbitrary"),
    )(t, t, w, b, dact)


def _conv_bwd_in(dpre, w, name="conv_bwd_in"):
    s_, c = dpre.shape
    tr = _row_tile(s_)
    nt = s_ // tr

    def body(d_ref, h_ref, w_ref, o_ref, ext):
        ext[0:tr, :] = d_ref[...]
        ext[tr:tr + HALO, :] = jnp.where(pl.program_id(0) < nt - 1, h_ref[...], 0.0)
        acc = ext[CONV_K - 1:CONV_K - 1 + tr, :] * w_ref[0:1, :]
        for k in range(1, CONV_K):
            acc = acc + ext[CONV_K - 1 - k:CONV_K - 1 - k + tr, :] * w_ref[k:k + 1, :]
        o_ref[...] = acc.astype(o_ref.dtype)

    main, halo = _conv_specs(s_, c, tr, True)
    return pl.pallas_call(
        body, name=name, grid=(nt,),
        in_specs=[main, halo, pl.BlockSpec((CONV_K, c), lambda i: (0, 0))],
        out_specs=main, out_shape=jax.ShapeDtypeStruct((s_, c), MXU_DTYPE),
        scratch_shapes=[pltpu.VMEM((tr + HALO, c), F32)], compiler_params=_params("parallel"),
    )(dpre, dpre, w)


def _ssd_chunk_common(dt_ref, dtt_ref, br_ref, bc_ref, ar_ref, ac_ref):
    li = lax.broadcasted_iota(jnp.int32, (CHUNK, CHUNK), 0)
    si = lax.broadcasted_iota(jnp.int32, (CHUNK, CHUNK), 1)
    lower = li >= si
    lower_b = lower.astype(BF16)
    upper_b = (li <= si).astype(BF16)
    zr = dt_ref[...] + br_ref[...]
    dtc = _softplus(zr)
    a_row = -jnp.exp(ar_ref[...])
    acum = _exact_dot(lower_b, dtc * a_row, 1, 0, False)
    dtt = _softplus(dtt_ref[...] + bc_ref[...])
    acum_t = _exact_dot(dtt * (-jnp.exp(ac_ref[...])), upper_b, 1, 0, True)
    return lower, upper_b, zr, dtc, a_row, acum, acum_t


def _head_terms(h, lower, dtc, acum, acum_t):
    lane = lax.broadcasted_iota(jnp.int32, (1, LANE), 1)
    sub = lax.broadcasted_iota(jnp.int32, (SSD_H, 1), 0)
    rowid = lax.broadcasted_iota(jnp.int32, (CHUNK, 1), 0)
    oh = (lane == HEAD_LANE + h).astype(F32)
    acol = jnp.sum(acum * oh, axis=1, keepdims=True)
    dcol = jnp.sum(dtc * oh, axis=1, keepdims=True)
    arow = jnp.sum(acum_t * (sub == h).astype(F32), axis=0, keepdims=True)
    alast = jnp.sum(jnp.where(rowid == CHUNK - 1, acol, 0.0), axis=0, keepdims=True)
    decay = jnp.exp(jnp.where(lower, acol - arow, -jnp.inf))
    return oh, acol, dcol, alast, decay


def _hs(h):
    return slice(h * SSD_P, (h + 1) * SSD_P)


def _gs(off, g):
    return slice(off + g * SSD_N, off + (g + 1) * SSD_N)


def _ssd_in_specs(rev):
    def ci(c):
        return c if rev is None else rev - c
    return [pl.BlockSpec((CHUNK, CONV_DIM), lambda c: (ci(c), 0)),
            pl.BlockSpec((CHUNK, LANE), lambda c: (ci(c), 0)),
            pl.BlockSpec((SSD_H, CHUNK), lambda c: (0, ci(c))),
            pl.BlockSpec((1, LANE), lambda c: (0, 0)), pl.BlockSpec((SSD_H, 1), lambda c: (0, 0)),
            pl.BlockSpec((1, LANE), lambda c: (0, 0)), pl.BlockSpec((SSD_H, 1), lambda c: (0, 0)),
            pl.BlockSpec((SSD_H, 1, SSD_P), lambda c: (0, 0, 0))]


def _ssd_fwd(xbc, small, dtt, bias_r, bias_c, alog_r, alog_c, dsk, name="ssd_fwd"):
    s_ = xbc.shape[0]
    nc = s_ // CHUNK

    def body(x_ref, dt_ref, dtt_ref, br_ref, bc_ref, ar_ref, ac_ref, dsk_ref, y_ref, prev_ref, state):
        @pl.when(pl.program_id(0) == 0)
        def _():
            state[...] = jnp.zeros_like(state)

        lower, _, _, dtc, _, acum, acum_t = _ssd_chunk_common(dt_ref, dtt_ref, br_ref, bc_ref, ar_ref, ac_ref)
        for g in range(SSD_G):
            bb = _mx(x_ref[:, _gs(B_OFF, g)])
            cb_ = _mx(x_ref[:, _gs(C_OFF, g)])
            cbm = _dot(cb_, bb, 1, 1)
            for e in range(SSD_E):
                h = g * SSD_E + e
                _, acol, dcol, alast, decay = _head_terms(h, lower, dtc, acum, acum_t)
                x = x_ref[:, _hs(h)]
                xdt = x * dcol
                yd = _dot(_mx(cbm * decay), _mx(xdt), 1, 0)
                prev = state[h]
                prev_ref[0, h] = prev
                yo = _dot(cb_, _mx(prev), 1, 1) * jnp.exp(acol)
                ds = jnp.exp(alast - acol)
                st = _dot(_mx(xdt * ds), bb, 0, 0)
                state[h] = prev * jnp.exp(alast) + st
                y_ref[:, _hs(h)] = yd + yo + x * dsk_ref[h]

    psp = pl.BlockSpec((1, SSD_H, SSD_P, SSD_N), lambda c: (c, 0, 0, 0))
    return pl.pallas_call(
        body, name=name, grid=(nc,),
        in_specs=_ssd_in_specs(None), out_specs=[pl.BlockSpec((CHUNK, SSD_W), lambda c: (c, 0)), psp],
        out_shape=[jax.ShapeDtypeStruct((s_, SSD_W), F32),
                   jax.ShapeDtypeStruct((nc, SSD_H, SSD_P, SSD_N), F32)],
        scratch_shapes=[pltpu.VMEM((SSD_H, SSD_P, SSD_N), F32)],
        compiler_params=_params("arbitrary"),
    )(xbc, small, dtt, bias_r, bias_c, alog_r, alog_c, dsk)


def _ssd_bwd(xbc, small, dtt, bias_r, bias_c, alog_r, alog_c, dsk, prev, dy, name="ssd_bwd"):
    s_ = xbc.shape[0]
    nc = s_ // CHUNK

    def body(x_ref, dt_ref, dtt_ref, br_ref, bc_ref, ar_ref, ac_ref, dsk_ref, prev_ref, dy_ref,
             dx_ref, ddt_ref, dpar_ref, dstate):
        @pl.when(pl.program_id(0) == 0)
        def _():
            dstate[...] = jnp.zeros_like(dstate)
            dpar_ref[...] = jnp.zeros_like(dpar_ref)

        lower, upper_b, zr, dtc, a_row, acum, acum_t = _ssd_chunk_common(
            dt_ref, dtt_ref, br_ref, bc_ref, ar_ref, ac_ref)
        strict = (lax.broadcasted_iota(jnp.int32, (CHUNK, CHUNK), 1)
                  < lax.broadcasted_iota(jnp.int32, (CHUNK, CHUNK), 0))
        strict_b = strict.astype(BF16)
        da_in = jnp.zeros((CHUNK, LANE), F32)
        r_off = jnp.zeros((CHUNK, LANE), F32)
        c_int = jnp.zeros((CHUNK, LANE), F32)
        c_row = jnp.zeros((1, LANE), F32)
        ddt = jnp.zeros((CHUNK, LANE), F32)
        dskip = jnp.zeros((1, LANE), F32)
        for g in range(SSD_G):
            bb = _mx(x_ref[:, _gs(B_OFF, g)])
            cb_ = _mx(x_ref[:, _gs(C_OFF, g)])
            cbm = _dot(cb_, bb, 1, 1)
            dcb = jnp.zeros((CHUNK, CHUNK), F32)
            dc_acc = jnp.zeros((CHUNK, SSD_N), F32)
            db_acc = jnp.zeros((CHUNK, SSD_N), F32)
            for e in range(SSD_E):
                h = g * SSD_E + e
                oh, acol, dcol, alast, decay = _head_terms(h, lower, dtc, acum, acum_t)
                x = x_ref[:, _hs(h)]
                dy = dy_ref[:, _hs(h)]
                xdt = x * dcol
                eacol = jnp.exp(acol)
                ds = jnp.exp(alast - acol)
                dyb = _mx(dy)
                dsh = dstate[h]
                dshb = _mx(dsh)
                prev = prev_ref[0, h]
                prevb = _mx(prev)
                dxdt_inter = ds * _dot(bb, dshb, 1, 1)
                dxdt = _dot(_mx(cbm * decay), dyb, 0, 0) + dxdt_inter
                dwl = _dot(dyb, _mx(xdt), 1, 1) * decay
                dcb = dcb + dwl
                dc_acc = dc_acc + eacol * _dot(dyb, prevb, 1, 0)
                db_acc = db_acc + _dot(_mx(xdt * ds), dshb, 1, 0)
                dstate[h] = _dot(_mx(dy * eacol), cb_, 0, 0) + jnp.exp(alast) * dsh
                above = _exact_dot(upper_b, dwl * cbm, 1, 0, False)
                da_in = da_in + jnp.sum(jnp.where(strict, above, 0.0), axis=1, keepdims=True) * oh
                y_off = _dot(cb_, prevb, 1, 1) * eacol
                r_off = r_off + jnp.sum(dy * y_off, axis=1, keepdims=True) * oh
                c_int = c_int + jnp.sum(xdt * dxdt_inter, axis=1, keepdims=True) * oh
                both = jnp.sum(jnp.sum(dsh * prev, axis=1, keepdims=True), axis=0, keepdims=True)
                c_row = c_row + jnp.exp(alast) * both * oh
                dk = dsk_ref[h]
                ddt = ddt + jnp.sum(dxdt * x, axis=1, keepdims=True) * oh
                dx_ref[:, _hs(h)] = dxdt * dcol + dy * dk
                dskip = dskip + jnp.sum(jnp.sum(dy * x, axis=1, keepdims=True), axis=0, keepdims=True) * oh
            dcbb = _mx(dcb)
            dx_ref[:, _gs(C_OFF, g)] = dc_acc + _dot(dcbb, bb, 1, 0)
            dx_ref[:, _gs(B_OFF, g)] = db_acc + _dot(dcbb, cb_, 0, 0)
        da = (da_in + _exact_dot(upper_b, r_off, 1, 0, False) + _exact_dot(strict_b, c_int, 1, 0, False) + c_row)
        draw = (ddt + da * a_row) * _sigmoid(zr)
        ddt_ref[...] = draw
        dpar_ref[0:1, :] += jnp.sum(draw, axis=0, keepdims=True)
        dpar_ref[1:2, :] += jnp.sum(da * dtc, axis=0, keepdims=True) * a_row
        dpar_ref[2:3, :] += dskip

    rev = nc - 1
    psp = pl.BlockSpec((1, SSD_H, SSD_P, SSD_N), lambda c: (rev - c, 0, 0, 0))
    return pl.pallas_call(
        body, name=name, grid=(nc,),
        in_specs=_ssd_in_specs(rev) + [psp, pl.BlockSpec((CHUNK, SSD_W), lambda c: (rev - c, 0))],
        out_specs=[pl.BlockSpec((CHUNK, CONV_DIM), lambda c: (rev - c, 0)),
                   pl.BlockSpec((CHUNK, LANE), lambda c: (rev - c, 0)), pl.BlockSpec((8, LANE), lambda c: (0, 0))],
        out_shape=[jax.ShapeDtypeStruct((s_, CONV_DIM), F32), jax.ShapeDtypeStruct((s_, LANE), F32),
                   jax.ShapeDtypeStruct((8, LANE), F32)],
        scratch_shapes=[pltpu.VMEM((SSD_H, SSD_P, SSD_N), F32)],
        compiler_params=_params("arbitrary"),
    )(xbc, small, dtt, bias_r, bias_c, alog_r, alog_c, dsk, prev, dy)


GN = SSD_W // SSD_G


def _gated_norm_fwd(y, z, w, name="gated_norm_fwd"):
    s_, f = y.shape
    tr = _row_tile(s_)

    def body(y_ref, z_ref, w_ref, o_ref):
        for g in range(SSD_G):
            sl = slice(g * GN, (g + 1) * GN)
            gg = y_ref[:, sl] * _silu(z_ref[:, sl])
            r = lax.rsqrt(jnp.mean(gg * gg, axis=-1, keepdims=True) + EPS)
            o_ref[:, sl] = (gg * r * w_ref[:, sl]).astype(o_ref.dtype)

    row = pl.BlockSpec((tr, f), lambda i: (i, 0))
    wsp = pl.BlockSpec((1, f), lambda i: (0, 0))
    return pl.pallas_call(
        body, name=name, grid=(s_ // tr,), in_specs=[row, row, wsp], out_specs=row,
        out_shape=jax.ShapeDtypeStruct((s_, f), MXU_DTYPE), compiler_params=_params("parallel"),
    )(y, z, w.reshape(1, f))


def _gated_norm_bwd(y, z, w, dout, name="gated_norm_bwd"):
    s_, f = y.shape
    tr = _row_tile(s_)

    def body(y_ref, z_ref, w_ref, do_ref, dy_ref, dz_ref, dw_ref):
        @pl.when(pl.program_id(0) == 0)
        def _():
            dw_ref[...] = jnp.zeros_like(dw_ref)

        for g in range(SSD_G):
            sl = slice(g * GN, (g + 1) * GN)
            yv = y_ref[:, sl]
            zv = z_ref[:, sl]
            dov = do_ref[:, sl].astype(F32)
            sz = _silu(zv)
            gg = yv * sz
            r = lax.rsqrt(jnp.mean(gg * gg, axis=-1, keepdims=True) + EPS)
            gw = dov * w_ref[:, sl]
            c = jnp.mean(gw * gg, axis=-1, keepdims=True)
            dgg = r * gw - gg * (r * r * r * c)
            dy_ref[:, sl] = dgg * sz
            dz_ref[:, sl] = (dgg * yv * _dsilu(zv)).astype(dz_ref.dtype)
            dw_ref[:, sl] += jnp.sum(dov * gg * r, axis=0, keepdims=True)

    row = pl.BlockSpec((tr, f), lambda i: (i, 0))
    wsp = pl.BlockSpec((1, f), lambda i: (0, 0))
    return pl.pallas_call(
        body, name=name, grid=(s_ // tr,), in_specs=[row, row, wsp, row], out_specs=[row, row, wsp],
        out_shape=[jax.ShapeDtypeStruct((s_, f), F32), jax.ShapeDtypeStruct((s_, f), MXU_DTYPE),
                   jax.ShapeDtypeStruct((1, f), F32)],
        compiler_params=_params("arbitrary"),
    )(y, z, w.reshape(1, f), dout)


def _ffn_fwd(vv, w_gate, w_up, name="ffn_gate_up"):
    s_, d = vv.shape
    nb, _, f8 = w_gate.shape
    tm = _pick(s_, (1024, 512, 256, 128))

    def body(v_ref, wg_ref, wu_ref, g_ref, u_ref, a_ref):
        a = _mx(v_ref[...])
        g = _dot(a, _mx(wg_ref[...]), 1, 0)
        u = _dot(a, _mx(wu_ref[...]), 1, 0)
        g_ref[...] = g.astype(g_ref.dtype)
        u_ref[...] = u.astype(u_ref.dtype)
        a_ref[...] = (_silu(g) * u).astype(a_ref.dtype)

    wsp = pl.BlockSpec((None, d, f8), lambda j, i: (j, 0, 0))
    osp = pl.BlockSpec((None, tm, f8), lambda j, i: (j, i, 0))
    return pl.pallas_call(
        body, name=name, grid=(nb, s_ // tm),
        in_specs=[pl.BlockSpec((tm, d), lambda j, i: (i, 0)), wsp, wsp], out_specs=[osp] * 3,
        out_shape=[jax.ShapeDtypeStruct((nb, s_, f8), MXU_DTYPE)] * 3,
        compiler_params=_params("parallel", "parallel"),
    )(vv, w_gate, w_up)


def _ffn_bwd_act(dffn, w_down, gate, up, name="ffn_d_act"):
    s_, d = dffn.shape
    nb, f8, _ = w_down.shape
    tm = _pick(s_, (1024, 512, 256, 128))

    def body(d_ref, w_ref, g_ref, u_ref, dg_ref, du_ref):
        dact = _dot(_mx(d_ref[...]), _mx(w_ref[...]), 1, 1)
        g = g_ref[...].astype(F32)
        dg_ref[...] = (dact * u_ref[...].astype(F32) * _dsilu(g)).astype(dg_ref.dtype)
        du_ref[...] = (dact * _silu(g)).astype(du_ref.dtype)

    osp = pl.BlockSpec((None, tm, f8), lambda j, i: (j, i, 0))
    return pl.pallas_call(
        body, name=name, grid=(nb, s_ // tm),
        in_specs=[pl.BlockSpec((tm, d), lambda j, i: (i, 0)), pl.BlockSpec((None, f8, d), lambda j, i: (j, 0, 0)),
                  osp, osp],
        out_specs=[osp, osp], out_shape=[jax.ShapeDtypeStruct((nb, s_, f8), MXU_DTYPE)] * 2,
        compiler_params=_params("parallel", "parallel"),
    )(dffn, w_down, gate, up)


def _ffn_bwd_in(dgate, w_gate, dup, w_up, name="ffn_d_in"):
    nb, s_, f8 = dgate.shape
    d = w_gate.shape[1]
    tm = _pick(s_, (1024, 512, 256, 128))
    tn = _pick(d, (1024, 512, 256, 128))

    def body(dg_ref, wg_ref, du_ref, wu_ref, o_ref, acc):
        j = pl.program_id(2)
        part = _dot(_mx(dg_ref[...]), _mx(wg_ref[...]), 1, 1) + _dot(_mx(du_ref[...]), _mx(wu_ref[...]), 1, 1)

        @pl.when(j == 0)
        def _():
            acc[...] = part

        @pl.when(j > 0)
        def _():
            acc[...] += part

        @pl.when(j == nb - 1)
        def _():
            o_ref[...] = acc[...]

    asp = pl.BlockSpec((None, tm, f8), lambda i, n, j: (j, i, 0))
    wsp = pl.BlockSpec((None, tn, f8), lambda i, n, j: (j, n, 0))
    return pl.pallas_call(
        body, name=name, grid=(s_ // tm, d // tn, nb),
        in_specs=[asp, wsp, asp, wsp], out_specs=pl.BlockSpec((tm, tn), lambda i, n, j: (i, n)),
        out_shape=jax.ShapeDtypeStruct((s_, d), F32), scratch_shapes=[pltpu.VMEM((tm, tn), F32)],
        compiler_params=_params("parallel", "parallel", "arbitrary"),
    )(dgate, w_gate, dup, w_up)


def _adam_math(g, w, m, v):
    m2 = ADAM_B1 * m + (1.0 - ADAM_B1) * g
    v2 = ADAM_B2 * v + (1.0 - ADAM_B2) * (g * g)
    m_hat = m2 / (1.0 - ADAM_B1 ** ADAM_STEP)
    v_hat = v2 / (1.0 - ADAM_B2 ** ADAM_STEP)
    delta = -ADAM_LR * (m_hat / (jnp.sqrt(v_hat) + ADAM_EPS) + ADAM_WD * w)
    return delta, m2, v2


def _adamw(parts, w, m, v, name="adamw"):
    nd, r_, c = parts.shape
    tr = _pick(r_, (128, 64, 32, 16, 8))

    def body(p_ref, w_ref, m_ref, v_ref, g_ref, d_ref, m2_ref, v2_ref):
        g = p_ref[0].astype(F32)
        for i in range(1, nd):
            g = g + p_ref[i].astype(F32)
        delta, m2, v2 = _adam_math(g, w_ref[...], m_ref[...], v_ref[...])
        g_ref[...] = g
        d_ref[...] = delta
        m2_ref[...] = m2
        v2_ref[...] = v2

    row = pl.BlockSpec((tr, c), lambda i: (i, 0))
    psp = pl.BlockSpec((nd, tr, c), lambda i: (0, i, 0))
    return pl.pallas_call(
        body, name=name, grid=(r_ // tr,), in_specs=[psp, row, row, row], out_specs=[row] * 4,
        out_shape=[jax.ShapeDtypeStruct((r_, c), F32)] * 4, compiler_params=_params("parallel"),
    )(parts, w, m, v)


def _adamw_small(parts, w, m, v, name="adamw_small"):
    nd = parts.shape[0]

    def body(p_ref, w_ref, m_ref, v_ref, g_ref, d_ref, m2_ref, v2_ref):
        g = p_ref[0]
        for i in range(1, nd):
            g = g + p_ref[i]
        delta, m2, v2 = _adam_math(g, w_ref[...], m_ref[...], v_ref[...])
        g_ref[...] = g
        d_ref[...] = delta
        m2_ref[...] = m2
        v2_ref[...] = v2

    return pl.pallas_call(
        body, name=name, out_shape=[jax.ShapeDtypeStruct(w.shape, F32)] * 4,
        compiler_params=pltpu.CompilerParams(vmem_limit_bytes=VMEM_LIMIT_BYTES),
    )(parts, w, m, v)


_HBM = pl.BlockSpec(memory_space=pltpu.HBM)
_MESH = pl.DeviceIdType.MESH


def _all_gather(xs, name):
    na = len(xs)

    def body(*refs):
        x_refs, out_refs = refs[:na], refs[na:2 * na]
        send_sems, recv_sems, local_sems = refs[2 * na:]
        x, y, c = lax.axis_index("x"), lax.axis_index("y"), lax.axis_index("c")
        me, sibling = (x, y, c), (x, y, 1 - c)
        chips = [(1 - x, y), (x, 1 - y), (1 - x, 1 - y)]

        def slot(a, px, py, pc):
            return out_refs[a].at[4 * px + 2 * py + pc]

        def copy(a, k, block, to, src=None):
            return pltpu.make_async_remote_copy(
                src_ref=slot(a, *block) if src is None else src, dst_ref=slot(a, *block),
                send_sem=send_sems.at[a, k], recv_sem=recv_sems.at[a, k], device_id=to, device_id_type=_MESH)

        mine = [pltpu.make_async_copy(x_refs[a], slot(a, *me), local_sems.at[a]) for a in range(na)]
        started = []
        for a in range(na):
            mine[a].start()
            first = [copy(a, 0, me, sibling, src=x_refs[a])]
            first += [copy(a, 1 + j, me, (*chip, c), src=x_refs[a]) for j, chip in enumerate(chips)]
            for cp in first:
                cp.start()
            started += first
        for a in range(na):
            for j, chip in enumerate(chips):
                copy(a, 1 + j, (*chip, c), me).wait_recv()
                fwd = copy(a, 4 + j, (*chip, c), sibling)
                fwd.start()
                started.append(fwd)
        for a in range(na):
            copy(a, 0, sibling, me).wait_recv()
            for j, chip in enumerate(chips):
                copy(a, 4 + j, (*chip, 1 - c), me).wait_recv()
        for cp in started:
            cp.wait_send()
        for cp in mine:
            cp.wait()

    return pl.pallas_call(
        body, name=name, out_shape=[jax.ShapeDtypeStruct((N_DEV,) + t.shape, t.dtype) for t in xs],
        in_specs=[_HBM] * na, out_specs=[_HBM] * na,
        scratch_shapes=[pltpu.SemaphoreType.DMA((na, 7)), pltpu.SemaphoreType.DMA((na, 7)),
                        pltpu.SemaphoreType.DMA((na,))],
    )(*xs)


def _all_to_all(srcs, name):
    na = len(srcs)

    def body(*refs):
        src_refs, out_refs = refs[:na], refs[na:2 * na]
        send_sems, recv_sems, local_sems = refs[2 * na:]
        x, y, c = lax.axis_index("x"), lax.axis_index("y"), lax.axis_index("c")
        me = 4 * x + 2 * y + c
        mine = [pltpu.make_async_copy(src_refs[a].at[me], out_refs[a].at[me], local_sems.at[a]) for a in range(na)]
        copies = []
        for a in range(na):
            mine[a].start()
            for k in range(1, N_DEV):
                px = 1 - x if k & 4 else x
                py = 1 - y if k & 2 else y
                pc = 1 - c if k & 1 else c
                peer = 4 * px + 2 * py + pc
                send = pltpu.make_async_remote_copy(
                    src_ref=src_refs[a].at[peer], dst_ref=out_refs[a].at[me], send_sem=send_sems.at[a, k - 1],
                    recv_sem=recv_sems.at[a, k - 1], device_id=(px, py, pc), device_id_type=_MESH)
                recv = pltpu.make_async_remote_copy(
                    src_ref=src_refs[a].at[peer], dst_ref=out_refs[a].at[peer], send_sem=send_sems.at[a, k - 1],
                    recv_sem=recv_sems.at[a, k - 1], device_id=(px, py, pc), device_id_type=_MESH)
                send.start()
                copies.append((send, recv))
        for send, recv in copies:
            recv.wait_recv()
        for send, recv in copies:
            send.wait_send()
        for cp in mine:
            cp.wait()

    return pl.pallas_call(
        body, name=name, out_shape=[jax.ShapeDtypeStruct(t.shape, t.dtype) for t in srcs],
        in_specs=[_HBM] * na, out_specs=[_HBM] * na,
        scratch_shapes=[pltpu.SemaphoreType.DMA((na, 7)), pltpu.SemaphoreType.DMA((na, 7)),
                        pltpu.SemaphoreType.DMA((na,))],
    )(*srcs)


_SEM = pl.BlockSpec(memory_space=pltpu.SEMAPHORE)
_EFFECT = pltpu.SideEffectType.DATAFLOW_SIDE_EFFECTING


def _peers(x, y, c):
    out = []
    for k in range(1, N_DEV):
        px = 1 - x if k & 4 else x
        py = 1 - y if k & 2 else y
        pc = 1 - c if k & 1 else c
        out.append(((px, py, pc), 4 * px + 2 * py + pc))
    return out


def _push_copies(scatter, src_refs, land_refs, send_sems, recv_sems):
    x, y, c = lax.axis_index("x"), lax.axis_index("y"), lax.axis_index("c")
    me = 4 * x + 2 * y + c
    pairs = []
    for a, (src, land) in enumerate(zip(src_refs, land_refs)):
        for k, (peer, slot) in enumerate(_peers(x, y, c)):
            out_src = src.at[slot] if scatter else src
            si = a * (N_DEV - 1) + k
            send = pltpu.make_async_remote_copy(src_ref=out_src, dst_ref=land.at[me], send_sem=send_sems.at[si],
                                                recv_sem=recv_sems.at[si], device_id=peer, device_id_type=_MESH)
            recv = pltpu.make_async_remote_copy(src_ref=out_src, dst_ref=land.at[slot], send_sem=send_sems.at[si],
                                                recv_sem=recv_sems.at[si], device_id=peer, device_id_type=_MESH)
            pairs.append((send, recv))
    return pairs


def _push_start(srcs, scatter, dep, name):
    na = len(srcs)
    shapes = [t.shape[1:] if scatter else t.shape for t in srcs]
    lands = [pltpu.with_memory_space_constraint(lax.empty((N_DEV,) + s, t.dtype), pltpu.HBM) for s, t in zip(shapes, srcs)]

    def body(*refs):
        src_refs, land_refs = refs[:na], refs[na:2 * na]
        send_sems, recv_sems = refs[2 * na + 1], refs[2 * na + 2]
        token = refs[-1]
        for send, _ in _push_copies(scatter, src_refs, land_refs, send_sems, recv_sems):
            send.start()
        token[...] = jnp.zeros_like(token)

    sem = pltpu.SemaphoreType.DMA((na * (N_DEV - 1),))
    outs = pl.pallas_call(
        body, name=name,
        out_shape=(sem, sem) + tuple(pltpu.HBM(t.shape, t.dtype) for t in srcs)
        + tuple(pltpu.HBM(t.shape, t.dtype) for t in lands) + (jax.ShapeDtypeStruct((8, LANE), F32),),
        in_specs=[_HBM] * (2 * na) + [pl.BlockSpec(memory_space=pl.ANY)],
        out_specs=(_SEM, _SEM) + (_HBM,) * (2 * na) + (pl.BlockSpec(memory_space=pltpu.VMEM),),
        input_output_aliases={i: 2 + i for i in range(2 * na)},
        compiler_params=pltpu.CompilerParams(has_side_effects=_EFFECT),
    )(*[pltpu.with_memory_space_constraint(t, pltpu.HBM) for t in srcs], *lands, dep)
    return outs[0], outs[1], outs[2:2 + na], outs[2 + na:2 + 2 * na], outs[-1]


def _push_wait(send_sems, recv_sems, src_thru, land_thru, scatter, after, name):
    na = len(src_thru)

    def body(*refs):
        src_refs, land_refs = refs[:na], refs[na:2 * na]
        ssem, rsem = refs[2 * na], refs[2 * na + 1]
        for send, recv in _push_copies(scatter, src_refs, land_refs, ssem, rsem):
            send.wait_send()
            recv.wait_recv()

    outs = pl.pallas_call(
        body, name=name,
        out_shape=tuple(pltpu.HBM(t.shape, t.dtype) for t in src_thru) + tuple(pltpu.HBM(t.shape, t.dtype) for t in land_thru),
        in_specs=[_HBM] * (2 * na) + [_SEM, _SEM, pl.BlockSpec(memory_space=pl.ANY)],
        out_specs=(_HBM,) * (2 * na),
        input_output_aliases={i: i for i in range(2 * na)},
        compiler_params=pltpu.CompilerParams(has_side_effects=_EFFECT),
    )(*src_thru, *land_thru, send_sems, recv_sems, after)
    return outs[:na], outs[na:]


def _exchange_behind(srcs, scatter, dep, name):
    send_sems, recv_sems, thru, lands, token = _push_start(srcs, scatter, dep, name + "_start")

    def finish(after):
        src_done, land_done = _push_wait(send_sems, recv_sems, thru, lands, scatter, after, name + "_wait")
        return _place_own(land_done, src_done, scatter, name + "_own")

    return token[0, 0], finish


def _place_own(lands, srcs, scatter, name):
    me = (4 * lax.axis_index("x") + 2 * lax.axis_index("y") + lax.axis_index("c")).astype(jnp.int32).reshape(1)
    outs = []
    for a, (land, src) in enumerate(zip(lands, srcs)):
        r_, c_ = land.shape[1:]
        tr = _pick(r_, (512, 256, 128, 64, 32, 16))

        def body(me_ref, land_ref, src_ref, out_ref):
            out_ref[...] = src_ref[...]

        src_spec = (pl.BlockSpec((None, tr, c_), lambda i, me_: (me_[0], i, 0)) if scatter
                    else pl.BlockSpec((tr, c_), lambda i, me_: (i, 0)))
        gs = pltpu.PrefetchScalarGridSpec(
            num_scalar_prefetch=1, grid=(r_ // tr,),
            in_specs=[pl.BlockSpec(memory_space=pl.ANY), src_spec],
            out_specs=pl.BlockSpec((None, tr, c_), lambda i, me_: (me_[0], i, 0)))
        outs.append(pl.pallas_call(
            body, name=f"{name}_{a}", grid_spec=gs, out_shape=jax.ShapeDtypeStruct(land.shape, land.dtype),
            input_output_aliases={1: 0}, compiler_params=_params("arbitrary"),
        )(me, land, src))
    return outs


_BIG = (("w_in", D_MODEL, D_IN, 1), ("w_uq", Q_RANK, HEADS * QK, 1), ("w_ukv", KV_RANK, HEADS * (NOPE + VDIM), 1),
        ("w_out", D_MODEL, D_MODEL, 0), ("w_gate", D_MODEL, D_FF, 1), ("w_up", D_MODEL, D_FF, 1),
        ("w_down", D_FF, D_MODEL, 0))
_CQKV = (0, Q_RANK + KV_RANK)
_KR = (_CQKV[1], _CQKV[1] + ROPE)
_Z = (_KR[1], _KR[1] + SSD_W)
_XBC = (_Z[1], _Z[1] + CONV_DIM)
_DT = (_XBC[1], _XBC[1] + SSD_H)


def _win_segments(w_in_g):
    w = jnp.transpose(w_in_g, (1, 0, 2)).reshape(D_MODEL, D_IN)
    small = jnp.concatenate([w[:, _KR[0]:_KR[1]], w[:, _DT[0]:_DT[1]],
                             jnp.zeros((D_MODEL, LANE - ROPE - SSD_H), w.dtype)], axis=1)
    return w[:, _CQKV[0]:_CQKV[1]], w[:, _Z[0]:_Z[1]], w[:, _XBC[0]:_XBC[1]], small


def _win_from_segments(g_cqkv, g_z, g_xbc, g_small):
    w = jnp.concatenate([g_cqkv, g_small[:, :ROPE], g_z, g_xbc, g_small[:, ROPE:ROPE + SSD_H]], axis=1)
    return jnp.transpose(w.reshape(D_MODEL, N_DEV, D_IN // N_DEV), (1, 0, 2))


_SMALL = (("q_norm_w", 512), ("kv_norm_w", 512), ("conv_b", CONV_DIM), ("dt_bias", SSD_H), ("a_log", SSD_H),
          ("d_skip", SSD_H), ("ssd_norm_w", SSD_W), ("attn_out_norm_w", 1024), ("pre_mix_norm_w", D_MODEL),
          ("post_mix_norm_w", D_MODEL), ("pre_ffn_norm_w", D_MODEL), ("post_ffn_norm_w", D_MODEL),
          ("conv_w", CONV_K * CONV_DIM))
_SMALL_ROWS = -(-sum(-(-n // LANE) for _, n in _SMALL) // 8) * 8


def _pack_small(vals):
    rows = []
    for name, n in _SMALL:
        v = vals[name].reshape(-1).astype(F32)
        pad = -(-n // LANE) * LANE
        rows.append(jnp.pad(v, (0, pad - n)).reshape(-1, LANE))
    m = jnp.concatenate(rows, axis=0)
    return jnp.pad(m, ((0, _SMALL_ROWS - m.shape[0]), (0, 0)))


def _unpack_small(m):
    out, r = {}, 0
    for name, n in _SMALL:
        nr = -(-n // LANE)
        out[name] = m[r:r + nr].reshape(-1)[:n]
        r += nr
    return out


def _head_row(v):
    return jnp.pad(v.reshape(1, -1).astype(F32), ((0, 0), (HEAD_LANE, LANE - HEAD_LANE - v.shape[-1])))


def _local_step(x, positions, target, wg, small, weights, on_grads):
    w_cqkv, w_z, w_xbc, w_small = _win_segments(wg["w_in"])
    conv_w = wg["conv_w"]
    conv_b = small["conv_b"].reshape(1, CONV_DIM)
    qkv_norm_w = jnp.concatenate([small["q_norm_w"], small["kv_norm_w"]])
    attn_norm_w = small["attn_out_norm_w"].reshape(HEADS, 1, VDIM)
    scale = QK ** -0.5

    inv_freq = ROPE_THETA ** (-jnp.arange(0, ROPE, 2, dtype=F32) / ROPE)
    ang = positions.astype(F32)[:, None] * inv_freq
    cos2 = jnp.tile(jnp.cos(ang), (1, 2))
    sin2 = jnp.tile(jnp.sin(ang), (1, 2))

    u = _rms_fwd(x, small["pre_mix_norm_w"], out_dtype=MXU_DTYPE, name="pre_mix_norm")
    cqkv = _mm(u, w_cqkv, "nn", name="in_proj_qkv")
    z = _mm(u, w_z, "nn", name="in_proj_z")
    xbc = _mm(u, w_xbc, "nn", name="in_proj_xbc")
    sm = _mm(u, w_small, "nn", name="in_proj_small")

    w_uq, w_ukv, w_out = weights("heads", cqkv)
    w_out = w_out.reshape(D_MODEL, D_MODEL)
    w_out_a = w_out[:HEADS * VDIM].reshape(HEADS, VDIM, D_MODEL)
    w_out_s = w_out[HEADS * VDIM:]
    qkvn = _rms_fwd(cqkv, qkv_norm_w, groups=2, out_dtype=MXU_DTYPE, name="qkv_norm")
    q = _mm(qkvn, w_uq, "nn", b_blk=True, out_blk=True, a_cols=(0, Q_RANK), name="q_up")
    kv = _mm(qkvn, w_ukv, "nn", b_blk=True, out_blk=True, a_cols=(Q_RANK, KV_RANK), name="kv_up")
    q_h = _q_prep(q, cos2, sin2, scale, name="q_prep")
    k_h, v_h = _kv_prep(kv, sm, cos2, sin2)
    o_h, lse = _flash_fwd(q_h, k_h, v_h)
    attn = _hnorm_fwd(o_h, attn_norm_w)

    xbc_act = _conv_fwd(xbc, conv_w, conv_b)
    dtt = jnp.transpose(sm[:, HEAD_LANE:HEAD_LANE + SSD_H])
    ssd_args = (xbc_act, sm, dtt, _head_row(small["dt_bias"]), small["dt_bias"].reshape(SSD_H, 1),
                _head_row(small["a_log"]), small["a_log"].reshape(SSD_H, 1),
                jnp.broadcast_to(small["d_skip"].reshape(SSD_H, 1, 1), (SSD_H, 1, SSD_P)))
    y_ssd, prev = _ssd_fwd(*ssd_args)
    ssm = _gated_norm_fwd(y_ssd, z, small["ssd_norm_w"])

    mix = _mm(attn, w_out_a, "nn", a_blk=True, b_blk=True, fuse=HEADS, name="out_proj_attn")
    mix = _mm(ssm, w_out_s, "nn", add=mix, name="out_proj_ssm")
    h1 = _rms_fwd(mix, small["post_mix_norm_w"], res=x, name="post_mix_norm")

    w_gate, w_up, w_down = weights("ffn", mix)
    vv = _rms_fwd(h1, small["pre_ffn_norm_w"], out_dtype=MXU_DTYPE, name="pre_ffn_norm")
    gate, up, act = _ffn_fwd(vv, w_gate, w_up)
    ffn = _mm(act, w_down, "nn", a_blk=True, b_blk=True, fuse=2, name="ffn_down")
    loss_blk, dy, dffn, g_post_ffn = _loss_head(ffn, h1, target, small["post_ffn_norm_w"])

    g_down = _mm(act, dffn, "tn", a_blk=True, out_blk=True, out_dtype=MXU_DTYPE, name="g_down")
    dgate, dup = _ffn_bwd_act(dffn, w_down, gate, up)
    dvv = _ffn_bwd_in(dgate, w_gate, dup, w_up)
    g_gate = _mm(vv, dgate, "tn", b_blk=True, out_blk=True, out_dtype=MXU_DTYPE, name="g_gate")
    g_up = _mm(vv, dup, "tn", b_blk=True, out_blk=True, out_dtype=MXU_DTYPE, name="g_up")
    pre_ffn_w = small["pre_ffn_norm_w"] + on_grads("ffn", [g_gate, g_up, g_down])
    dh1, g_pre_ffn = _rms_bwd(h1, pre_ffn_w, [dvv], res=dy, name="pre_ffn_norm_bwd")

    dmix, g_post_mix = _rms_bwd(mix, small["post_mix_norm_w"], [dh1], out_dtype=MXU_DTYPE, name="post_mix_norm_bwd")
    dattn = _mm(dmix, w_out_a, "nt", b_blk=True, out_blk=True, name="d_attn")
    dssm = _mm(dmix, w_out_s, "nt", name="d_ssm")
    g_out_a = _mm(attn, dmix, "tn", a_blk=True, out_blk=True, out_dtype=MXU_DTYPE, name="g_out_attn")
    g_out_s = _mm(ssm, dmix, "tn", out_dtype=MXU_DTYPE, name="g_out_ssm")
    g_out = jnp.concatenate([g_out_a.reshape(HEADS * VDIM, D_MODEL), g_out_s], axis=0)

    do_h, g_attn_norm = _hnorm_bwd(o_h, attn_norm_w, dattn)
    dq_h, delta = _flash_bwd_dq(q_h, k_h, v_h, o_h, do_h, lse)
    dk_h, dv_h = _flash_bwd_dkv(q_h, k_h, v_h, do_h, lse, delta)
    dq = _q_prep(dq_h, cos2, -sin2, scale, name="dq_post")

    dy_ssd, dz, g_ssd_norm = _gated_norm_bwd(y_ssd, z, small["ssd_norm_w"], dssm)
    dxbc_act, ddt, dpar = _ssd_bwd(*ssd_args, prev, dy_ssd)
    dkv, dsm = _dkv_post(dk_h, dv_h, ddt, cos2, -sin2)
    dpre, dwb = _conv_bwd_pre(xbc, conv_w, conv_b, dxbc_act)
    dxbc = _conv_bwd_in(dpre, conv_w)

    dqn = _mm(dq, w_uq, "nt", a_blk=True, b_blk=True, fuse=HEADS, name="d_qn")
    dkvn = _mm(dkv, w_ukv, "nt", a_blk=True, b_blk=True, fuse=HEADS, name="d_kvn")
    g_uq = _mm(qkvn, dq, "tn", b_blk=True, out_blk=True, a_cols=(0, Q_RANK), out_dtype=MXU_DTYPE, name="g_uq")
    g_ukv = _mm(qkvn, dkv, "tn", b_blk=True, out_blk=True, a_cols=(Q_RANK, KV_RANK), out_dtype=MXU_DTYPE, name="g_ukv")
    heads_token = on_grads("heads", [g_uq, g_ukv, g_out.reshape(N_DEV, D_MODEL // N_DEV, D_MODEL)])
    dcqkv, g_qkv_norm = _rms_bwd(cqkv, qkv_norm_w + heads_token, [dqn, dkvn], out_dtype=MXU_DTYPE, name="qkv_norm_bwd")

    du = _mm(dcqkv, w_cqkv, "nt", name="d_u_qkv")
    du = _mm(dz, w_z, "nt", add=du, name="d_u_z")
    du = _mm(dxbc, w_xbc, "nt", add=du, name="d_u_xbc")
    du = _mm(dsm, w_small, "nt", add=du, name="d_u_small")
    g_in = _win_from_segments(_mm(u, dcqkv, "tn", out_dtype=MXU_DTYPE, name="g_in_qkv"),
                              _mm(u, dz, "tn", out_dtype=MXU_DTYPE, name="g_in_z"),
                              _mm(u, dxbc, "tn", out_dtype=MXU_DTYPE, name="g_in_xbc"),
                              _mm(u, dsm, "tn", out_dtype=MXU_DTYPE, name="g_in_small"))
    dx, g_pre_mix = _rms_bwd(x, small["pre_mix_norm_w"], [du], res=dh1, name="pre_mix_norm_bwd")

    g_big = {"w_in": g_in}
    hl = slice(HEAD_LANE, HEAD_LANE + SSD_H)
    g_small = {"q_norm_w": g_qkv_norm[0, :Q_RANK], "kv_norm_w": g_qkv_norm[0, Q_RANK:], "conv_b": dwb[CONV_K],
               "dt_bias": dpar[0, hl], "a_log": dpar[1, hl], "d_skip": dpar[2, hl], "ssd_norm_w": g_ssd_norm,
               "attn_out_norm_w": g_attn_norm, "pre_mix_norm_w": g_pre_mix, "post_mix_norm_w": g_post_mix,
               "pre_ffn_norm_w": g_pre_ffn, "post_ffn_norm_w": g_post_ffn, "conv_w": dwb[:CONV_K]}
    return loss_blk[0, 0], dx, g_big, g_small


_WEIGHT_ORDER = ("w_in", "q_norm_w", "w_uq", "kv_norm_w", "w_ukv", "conv_w", "conv_b", "dt_bias", "a_log", "d_skip",
                 "ssd_norm_w", "attn_out_norm_w", "w_out", "pre_mix_norm_w", "post_mix_norm_w", "pre_ffn_norm_w",
                 "post_ffn_norm_w", "w_gate", "w_up", "w_down")


def kernel(x, positions, w_in, q_norm_w, w_uq, kv_norm_w, w_ukv, conv_w, conv_b, dt_bias, a_log, d_skip, ssd_norm_w, attn_out_norm_w, w_out, pre_mix_norm_w, post_mix_norm_w, pre_ffn_norm_w, post_ffn_norm_w, w_gate, w_up, w_down, loss_target, m_w_in, m_q_norm_w, m_w_uq, m_kv_norm_w, m_w_ukv, m_conv_w, m_conv_b, m_dt_bias, m_a_log, m_d_skip, m_ssd_norm_w, m_attn_out_norm_w, m_w_out, m_pre_mix_norm_w, m_post_mix_norm_w, m_pre_ffn_norm_w, m_post_ffn_norm_w, m_w_gate, m_w_up, m_w_down, v_w_in, v_q_norm_w, v_w_uq, v_kv_norm_w, v_w_ukv, v_conv_w, v_conv_b, v_dt_bias, v_a_log, v_d_skip, v_ssd_norm_w, v_attn_out_norm_w, v_w_out, v_pre_mix_norm_w, v_post_mix_norm_w, v_pre_ffn_norm_w, v_post_ffn_norm_w, v_w_gate, v_w_up, v_w_down):
    w = dict(w_in=w_in, q_norm_w=q_norm_w, w_uq=w_uq, kv_norm_w=kv_norm_w, w_ukv=w_ukv, conv_w=conv_w, conv_b=conv_b,
             dt_bias=dt_bias, a_log=a_log, d_skip=d_skip, ssd_norm_w=ssd_norm_w, attn_out_norm_w=attn_out_norm_w,
             w_out=w_out, pre_mix_norm_w=pre_mix_norm_w, post_mix_norm_w=post_mix_norm_w,
             pre_ffn_norm_w=pre_ffn_norm_w, post_ffn_norm_w=post_ffn_norm_w, w_gate=w_gate, w_up=w_up, w_down=w_down)
    m = dict(w_in=m_w_in, q_norm_w=m_q_norm_w, w_uq=m_w_uq, kv_norm_w=m_kv_norm_w, w_ukv=m_w_ukv, conv_w=m_conv_w,
             conv_b=m_conv_b, dt_bias=m_dt_bias, a_log=m_a_log, d_skip=m_d_skip, ssd_norm_w=m_ssd_norm_w,
             attn_out_norm_w=m_attn_out_norm_w, w_out=m_w_out, pre_mix_norm_w=m_pre_mix_norm_w,
             post_mix_norm_w=m_post_mix_norm_w, pre_ffn_norm_w=m_pre_ffn_norm_w, post_ffn_norm_w=m_post_ffn_norm_w,
             w_gate=m_w_gate, w_up=m_w_up, w_down=m_w_down)
    v = dict(w_in=v_w_in, q_norm_w=v_q_norm_w, w_uq=v_w_uq, kv_norm_w=v_kv_norm_w, w_ukv=v_w_ukv, conv_w=v_conv_w,
             conv_b=v_conv_b, dt_bias=v_dt_bias, a_log=v_a_log, d_skip=v_d_skip, ssd_norm_w=v_ssd_norm_w,
             attn_out_norm_w=v_attn_out_norm_w, w_out=v_w_out, pre_mix_norm_w=v_pre_mix_norm_w,
             post_mix_norm_w=v_post_mix_norm_w, pre_ffn_norm_w=v_pre_ffn_norm_w, post_ffn_norm_w=v_post_ffn_norm_w,
             w_gate=v_w_gate, w_up=v_w_up, w_down=v_w_down)
    w, m, v = ({k: t[0] for k, t in d.items()} for d in (w, m, v))
    me = 4 * lax.axis_index("x") + 2 * lax.axis_index("y") + lax.axis_index("c")
    groups = {"heads": ("w_uq", "w_ukv", "w_out"), "ffn": ("w_gate", "w_up", "w_down")}
    cshard = CONV_DIM // N_DEV

    shards = [w["w_in"].astype(MXU_DTYPE),
              jnp.stack(_split3(w["conv_w"])).reshape(3 * CONV_K, cshard).astype(MXU_DTYPE)]
    w_in_g, cw = _all_gather(shards, name="gather_weights")
    cw = cw.astype(F32).reshape(N_DEV, 3, CONV_K, cshard)
    wg = {"w_in": w_in_g, "conv_w": jnp.transpose(cw[:, 0] + cw[:, 1] + cw[:, 2], (1, 0, 2)).reshape(CONV_K, CONV_DIM)}
    arriving, dep, started = {}, wg["conv_w"], jnp.zeros((), F32)
    small = {name: w[name] for name, _ in _SMALL if name != "conv_w"}
    for group in ("heads", "ffn"):
        token, arriving[group] = _exchange_behind([w[name].astype(MXU_DTYPE) for name in groups[group]], False,
                                                  dep, group + "_weights")
        started = started + token
        dep = jnp.zeros((8, LANE), F32) + started
    small["pre_mix_norm_w"] = small["pre_mix_norm_w"] + started

    leaving = {}

    def on_grads(group, gs):
        token, leaving[group] = _exchange_behind(gs, True, jnp.zeros((8, LANE), F32), group + "_grads")
        return token

    loss_local, dx, g_big, g_small = _local_step(x[0], positions[0], loss_target[0], wg, small,
                                                 lambda group, after: arriving[group](after), on_grads)
    loss = lax.psum(loss_local, ("x", "y", "c"))

    recv = {"w_in": _all_to_all([g_big["w_in"]], name="exchange_grads")[0]}
    for group in ("heads", "ffn"):
        recv.update(zip(groups[group], leaving[group](dx)))
    grads, deltas, new_m, new_v = {}, {}, {}, {}
    for name, parts in recv.items():
        grads[name], deltas[name], new_m[name], new_v[name] = _adamw(parts, w[name], m[name], v[name],
                                                                     name="adamw_" + name)

    def embed(t):
        return lax.dynamic_update_slice(jnp.zeros((CONV_K, CONV_DIM), F32), t, (0, me * cshard))

    parts_s = _all_gather([_pack_small(g_small)], name="gather_small_grads")[0]
    packs = [_pack_small({**{n_: d[n_] for n_, _ in _SMALL if n_ != "conv_w"}, "conv_w": embed(d["conv_w"])})
             for d in (w, m, v)]
    outs = [_unpack_small(t) for t in _adamw_small(parts_s, *packs)]
    for name, n in _SMALL:
        for dst, src in zip((grads, deltas, new_m, new_v), outs):
            if name == "conv_w":
                dst[name] = lax.dynamic_slice(src[name].reshape(CONV_K, CONV_DIM), (0, me * cshard), (CONV_K, cshard))
            else:
                dst[name] = src[name]

    def lead(d):
        return [d[name][None] for name in _WEIGHT_ORDER]

    return (loss, dx[None], *lead(grads), *lead(deltas), *lead(new_m), *lead(new_v))
```

```python
import numpy as np

import jax
import jax.numpy as jnp
from jax import lax
from jax.experimental import pallas as pl
from jax.experimental.pallas import tpu as pltpu

F32 = jnp.float32
BF16 = jnp.bfloat16
MXU_DTYPE = jnp.bfloat16
EPS = 1e-6
VMEM_LIMIT_BYTES = 48 * 1024 * 1024
K_TILE_MAX = 2048

N_DEV = 8
D_MODEL = 2048
Q_RANK = 512
KV_RANK = 512
ROPE = 64
HALF = ROPE // 2
HEADS = 8
NOPE = 128
VDIM = 128
QK = NOPE + ROPE
SSD_W = 1024
SSD_H = 16
SSD_P = 64
SSD_G = 2
SSD_E = SSD_H // SSD_G
SSD_N = 128
CHUNK = 128
CONV_K = 4
CONV_DIM = SSD_W + 2 * SSD_G * SSD_N
B_OFF = SSD_W
C_OFF = SSD_W + SSD_G * SSD_N
D_FF = 5632
D_IN = Q_RANK + KV_RANK + ROPE + SSD_W + CONV_DIM + SSD_H
ROPE_THETA = 10000.0
LANE = 128
HEAD_LANE = ROPE

ADAM_LR = 0.001
ADAM_B1 = 0.9
ADAM_B2 = 0.999
ADAM_EPS = 1e-08
ADAM_WD = 0.01
ADAM_STEP = 10


def _pick(n, cands):
    for c in cands:
        if n % c == 0:
            return c
    return n


def _params(*sem):
    return pltpu.CompilerParams(dimension_semantics=sem, vmem_limit_bytes=VMEM_LIMIT_BYTES)


def _sigmoid(x):
    return 1.0 / (1.0 + jnp.exp(-x))


def _silu(x):
    return x * _sigmoid(x)


def _dsilu(x):
    s = _sigmoid(x)
    return s * (1.0 + x * (1.0 - s))


def _softplus(x):
    e = jnp.exp(-jnp.abs(x))
    small = e * (1.0 - e * (0.5 - e * (1.0 / 3.0)))
    return jnp.maximum(x, 0.0) + jnp.where(e < 0.01, small, jnp.log(1.0 + e))


def _dot(a, b, ca, cb):
    return lax.dot_general(a, b, (((ca,), (cb,)), ((), ())), preferred_element_type=F32)


def _mx(v):
    return v.astype(MXU_DTYPE)


def _split3(a):
    hi = a.astype(BF16)
    r1 = a - hi.astype(F32)
    mid = r1.astype(BF16)
    lo = (r1 - mid.astype(F32)).astype(BF16)
    return hi, mid, lo


def _exact_dot(a, b, ca, cb, split_a):
    if split_a:
        return sum(_dot(p, b, ca, cb) for p in _split3(a))
    return sum(_dot(a, p, ca, cb) for p in _split3(b))


def _mm(a, b, mode, *, a_blk=False, b_blk=False, out_blk=False, a_cols=None, add=None, out_dtype=F32, fuse=1,
        name="mm"):
    a2, b2 = a.shape[-2:], b.shape[-2:]
    a_last = a2[1] if a_cols is None else a_cols[1]
    a_start = 0 if a_cols is None else a_cols[0]
    if mode == "nn":
        m, k, (k2, n) = a2[0], a_last, b2
    elif mode == "nt":
        m, k, (n, k2) = a2[0], a_last, b2
    else:
        k, m, (k2, n) = a2[0], a_last, b2
    assert k == k2, (a.shape, b.shape, mode)
    tm = _pick(m, (1024, 704, 512, 256, 128))
    tn = _pick(n, (1024, 768, 704, 512, 256, 192, 128))
    tk = k if k <= K_TILE_MAX else _pick(k, (K_TILE_MAX, 1024, 512))
    nk = k // tk
    jo = N_DEV if out_blk else 1
    reduce_blocks = a_blk and b_blk and not out_blk
    assert fuse == 1 or reduce_blocks
    jr = N_DEV // fuse if reduce_blocks else 1
    ca, cb = {"nn": (1, 0), "nt": (1, 1), "tn": (0, 0)}[mode]
    has_add = add is not None
    single = jr * nk == 1
    if mode == "tn":
        assert a_start % tm == 0
        a_block, a_idx = (tk, tm), (lambda i, kk: (kk, i + a_start // tm))
    else:
        assert a_start % tk == 0
        a_block, a_idx = (tm, tk), (lambda i, kk: (i, kk + a_start // tk))
    b_block, b_idx = ((tn, tk), (lambda nn_, kk: (nn_, kk))) if mode == "nt" else ((tk, tn), (lambda nn_, kk: (kk, nn_)))

    def blk_specs(blocked, block, idx, of_a, t):
        def pos(o, i, nn_, kk):
            return idx(i, kk) if of_a else idx(nn_, kk)
        if blocked:
            return pl.BlockSpec((None,) + block,
                                lambda o, i, nn_, r, kk: ((o if out_blk else r * fuse + t),) + pos(o, i, nn_, kk))
        return pl.BlockSpec(block, lambda o, i, nn_, r, kk: pos(o, i, nn_, kk))

    a_specs = [blk_specs(a_blk, a_block, a_idx, True, t) for t in range(fuse)]
    b_specs = [blk_specs(b_blk, b_block, b_idx, False, t) for t in range(fuse)]
    o_spec = (pl.BlockSpec((None, tm, tn), lambda o, i, nn_, r, kk: (o, i, nn_)) if out_blk
              else pl.BlockSpec((tm, tn), lambda o, i, nn_, r, kk: (i, nn_)))

    def body(*refs):
        a_refs, b_refs = refs[:fuse], refs[fuse:2 * fuse]
        add_ref = refs[2 * fuse] if has_add else None
        o_ref = refs[2 * fuse + 1] if has_add else refs[2 * fuse]
        part = _dot(_mx(a_refs[0][...]), _mx(b_refs[0][...]), ca, cb)
        for t in range(1, fuse):
            part = part + _dot(_mx(a_refs[t][...]), _mx(b_refs[t][...]), ca, cb)
        if single:
            if has_add:
                part = part + add_ref[...]
            o_ref[...] = part.astype(o_ref.dtype)
            return
        acc = refs[-1]
        r, kk = pl.program_id(3), pl.program_id(4)
        first = jnp.logical_and(r == 0, kk == 0)
        last = jnp.logical_and(r == jr - 1, kk == nk - 1)

        @pl.when(first)
        def _():
            acc[...] = part

        @pl.when(jnp.logical_not(first))
        def _():
            acc[...] += part

        @pl.when(last)
        def _():
            res = acc[...]
            if has_add:
                res = res + add_ref[...]
            o_ref[...] = res.astype(o_ref.dtype)

    out_shape = ((N_DEV, m, n) if out_blk else (m, n))
    return pl.pallas_call(
        body, name=name, grid=(jo, m // tm, n // tn, jr, nk),
        in_specs=a_specs + b_specs + ([o_spec] if has_add else []), out_specs=o_spec,
        out_shape=jax.ShapeDtypeStruct(out_shape, out_dtype),
        scratch_shapes=[] if single else [pltpu.VMEM((tm, tn), F32)],
        compiler_params=_params("parallel", "parallel", "parallel", "arbitrary", "arbitrary"),
    )(*((a,) * fuse + (b,) * fuse + ((add,) if has_add else ())))


def _row_tile(r_):
    return _pick(r_, (256, 128, 64, 32, 16, 8))


def _rms_fwd(t, w, groups=1, res=None, out_dtype=F32, name="rms_fwd"):
    r_, f = t.shape
    fg = f // groups
    tr = _row_tile(r_)
    has_res = res is not None

    def body(*refs):
        t_ref, w_ref = refs[0], refs[1]
        res_ref = refs[2] if has_res else None
        o_ref = refs[-1]
        for g in range(groups):
            sl = slice(g * fg, (g + 1) * fg)
            tv = t_ref[:, sl].astype(F32)
            r = lax.rsqrt(jnp.mean(tv * tv, axis=-1, keepdims=True) + EPS)
            y = tv * r * w_ref[:, sl]
            if has_res:
                y = y + res_ref[:, sl]
            o_ref[:, sl] = y.astype(o_ref.dtype)

    row = pl.BlockSpec((tr, f), lambda i: (i, 0))
    wsp = pl.BlockSpec((1, f), lambda i: (0, 0))
    return pl.pallas_call(
        body, name=name, grid=(r_ // tr,),
        in_specs=[row, wsp] + ([row] if has_res else []), out_specs=row,
        out_shape=jax.ShapeDtypeStruct((r_, f), out_dtype),
        compiler_params=_params("parallel"),
    )(*((t, w.reshape(1, f)) + ((res,) if has_res else ())))


def _rms_bwd(t, w, dys, res=None, out_dtype=F32, name="rms_bwd"):
    r_, f = t.shape
    groups = len(dys)
    fg = f // groups
    tr = _row_tile(r_)
    has_res = res is not None

    def body(*refs):
        t_ref, w_ref = refs[0], refs[1]
        dy_refs = refs[2:2 + groups]
        res_ref = refs[2 + groups] if has_res else None
        dt_ref, dw_ref = refs[-2], refs[-1]

        @pl.when(pl.program_id(0) == 0)
        def _():
            dw_ref[...] = jnp.zeros_like(dw_ref)

        for g in range(groups):
            sl = slice(g * fg, (g + 1) * fg)
            tv = t_ref[:, sl].astype(F32)
            dyv = dy_refs[g][...].astype(F32)
            r = lax.rsqrt(jnp.mean(tv * tv, axis=-1, keepdims=True) + EPS)
            gw = dyv * w_ref[:, sl]
            c = jnp.mean(gw * tv, axis=-1, keepdims=True)
            dt = r * gw - tv * (r * r * r * c)
            if has_res:
                dt = dt + res_ref[:, sl]
            dt_ref[:, sl] = dt.astype(dt_ref.dtype)
            dw_ref[:, sl] += jnp.sum(dyv * tv * r, axis=0, keepdims=True)

    row = pl.BlockSpec((tr, f), lambda i: (i, 0))
    grow = pl.BlockSpec((tr, fg), lambda i: (i, 0))
    wsp = pl.BlockSpec((1, f), lambda i: (0, 0))
    return pl.pallas_call(
        body, name=name, grid=(r_ // tr,),
        in_specs=[row, wsp] + [grow] * groups + ([row] if has_res else []), out_specs=[row, wsp],
        out_shape=[jax.ShapeDtypeStruct((r_, f), out_dtype), jax.ShapeDtypeStruct((1, f), F32)],
        compiler_params=_params("arbitrary"),
    )(*((t, w.reshape(1, f)) + tuple(dys) + ((res,) if has_res else ())))


def _hnorm_fwd(o, w, name="attn_out_norm"):
    h, s_, v = o.shape
    tr = _row_tile(s_)

    def body(o_ref, w_ref, y_ref):
        ss = jnp.sum(o_ref[0] * o_ref[0], axis=-1, keepdims=True)
        for i in range(1, h):
            ss = ss + jnp.sum(o_ref[i] * o_ref[i], axis=-1, keepdims=True)
        r = lax.rsqrt(ss * (1.0 / (h * v)) + EPS)
        for i in range(h):
            y_ref[i] = (o_ref[i] * r * w_ref[i]).astype(y_ref.dtype)

    blk = pl.BlockSpec((h, tr, v), lambda i: (0, i, 0))
    wsp = pl.BlockSpec((h, 1, v), lambda i: (0, 0, 0))
    return pl.pallas_call(
        body, name=name, grid=(s_ // tr,), in_specs=[blk, wsp], out_specs=blk,
        out_shape=jax.ShapeDtypeStruct(o.shape, MXU_DTYPE), compiler_params=_params("parallel"),
    )(o, w)


def _hnorm_bwd(o, w, dy, name="attn_out_norm_bwd"):
    h, s_, v = o.shape
    tr = _row_tile(s_)

    def body(o_ref, w_ref, dy_ref, do_ref, dw_ref):
        @pl.when(pl.program_id(0) == 0)
        def _():
            dw_ref[...] = jnp.zeros_like(dw_ref)

        ss = jnp.zeros((tr, 1), F32)
        cc = jnp.zeros((tr, 1), F32)
        for i in range(h):
            ov = o_ref[i]
            ss = ss + jnp.sum(ov * ov, axis=-1, keepdims=True)
            cc = cc + jnp.sum(dy_ref[i] * w_ref[i] * ov, axis=-1, keepdims=True)
        r = lax.rsqrt(ss * (1.0 / (h * v)) + EPS)
        c = cc * (1.0 / (h * v))
        for i in range(h):
            ov = o_ref[i]
            dyv = dy_ref[i]
            do_ref[i] = r * dyv * w_ref[i] - ov * (r * r * r * c)
            dw_ref[i] += jnp.sum(dyv * ov * r, axis=0, keepdims=True)

    blk = pl.BlockSpec((h, tr, v), lambda i: (0, i, 0))
    wsp = pl.BlockSpec((h, 1, v), lambda i: (0, 0, 0))
    return pl.pallas_call(
        body, name=name, grid=(s_ // tr,), in_specs=[blk, wsp, blk], out_specs=[blk, wsp],
        out_shape=[jax.ShapeDtypeStruct(o.shape, F32), jax.ShapeDtypeStruct((h, 1, v), F32)],
        compiler_params=_params("arbitrary"),
    )(o, w, dy)


def _loss_head(ffn, h1, target, w, name="loss_head"):
    r_, f = ffn.shape
    tr = _row_tile(r_)

    def body(ffn_ref, h1_ref, tg_ref, w_ref, loss_ref, dy_ref, dffn_ref, dw_ref):
        @pl.when(pl.program_id(0) == 0)
        def _():
            dw_ref[...] = jnp.zeros_like(dw_ref)
            loss_ref[...] = jnp.zeros_like(loss_ref)

        tv = ffn_ref[...]
        wv = w_ref[...]
        r = lax.rsqrt(jnp.mean(tv * tv, axis=-1, keepdims=True) + EPS)
        tn = tv * r
        e = h1_ref[...] + tn * wv - tg_ref[...]
        tot = jnp.sum(jnp.sum(e * e, axis=1, keepdims=True), axis=0, keepdims=True) * (0.5 / f)
        loss_ref[...] += tot + jnp.zeros_like(loss_ref)
        dyv = e * (1.0 / f)
        dy_ref[...] = dyv
        gw = dyv * wv
        c = jnp.mean(gw * tv, axis=-1, keepdims=True)
        dffn_ref[...] = (r * gw - tv * (r * r * r * c)).astype(dffn_ref.dtype)
        dw_ref[...] += jnp.sum(dyv * tn, axis=0, keepdims=True)

    row = pl.BlockSpec((tr, f), lambda i: (i, 0))
    wsp = pl.BlockSpec((1, f), lambda i: (0, 0))
    lsp = pl.BlockSpec((1, LANE), lambda i: (0, 0))
    return pl.pallas_call(
        body, name=name, grid=(r_ // tr,),
        in_specs=[row, row, row, wsp], out_specs=[lsp, row, row, wsp],
        out_shape=[jax.ShapeDtypeStruct((1, LANE), F32), jax.ShapeDtypeStruct((r_, f), F32),
                   jax.ShapeDtypeStruct((r_, f), MXU_DTYPE), jax.ShapeDtypeStruct((1, f), F32)],
        compiler_params=_params("arbitrary"),
    )(ffn, h1, target, w.reshape(1, f))


def _rot_matrix():
    p = np.zeros((ROPE, ROPE), np.float32)
    for i in range(HALF):
        p[i + HALF, i] = -1.0
        p[i, i + HALF] = 1.0
    return jnp.asarray(p, BF16)


def _rope_val(r, c2, s2, rot):
    return r * c2 + _exact_dot(r, rot, 1, 0, True) * s2


def _q_prep(q, cos2, sin2, scale, name):
    h, s_, _ = q.shape
    tr = _pick(s_, (1024, 512, 256, 128, 64, 32, 16, 8))

    def body(q_ref, c_ref, s_ref, rot_ref, o_ref):
        x = q_ref[...]
        o_ref[:, :NOPE] = (x[:, :NOPE] * scale).astype(o_ref.dtype)
        o_ref[:, NOPE:] = (_rope_val(x[:, NOPE:], c_ref[...], s_ref[...], rot_ref[...]) * scale).astype(o_ref.dtype)

    blk = pl.BlockSpec((None, tr, QK), lambda hh, i: (hh, i, 0))
    csp = pl.BlockSpec((tr, ROPE), lambda hh, i: (i, 0))
    return pl.pallas_call(
        body, name=name, grid=(h, s_ // tr),
        in_specs=[blk, csp, csp, pl.BlockSpec((ROPE, ROPE), lambda hh, i: (0, 0))], out_specs=blk,
        out_shape=jax.ShapeDtypeStruct(q.shape, MXU_DTYPE), compiler_params=_params("parallel", "parallel"),
    )(q, cos2, sin2, _rot_matrix())


def _kv_prep(kv, small, cos2, sin2, name="kv_prep"):
    h, s_, _ = kv.shape
    tr = _row_tile(s_)

    def body(kv_ref, sm_ref, c_ref, s_ref, rot_ref, k_ref, v_ref):
        kr = _rope_val(sm_ref[:, :ROPE], c_ref[...], s_ref[...], rot_ref[...]).astype(k_ref.dtype)
        for i in range(h):
            k_ref[i, :, :NOPE] = kv_ref[i, :, :NOPE].astype(k_ref.dtype)
            k_ref[i, :, NOPE:] = kr
            v_ref[i] = kv_ref[i, :, NOPE:].astype(v_ref.dtype)

    csp = pl.BlockSpec((tr, ROPE), lambda i: (i, 0))
    return pl.pallas_call(
        body, name=name, grid=(s_ // tr,),
        in_specs=[pl.BlockSpec((h, tr, NOPE + VDIM), lambda i: (0, i, 0)), pl.BlockSpec((tr, LANE), lambda i: (i, 0)),
                  csp, csp, pl.BlockSpec((ROPE, ROPE), lambda i: (0, 0))],
        out_specs=[pl.BlockSpec((h, tr, QK), lambda i: (0, i, 0)), pl.BlockSpec((h, tr, VDIM), lambda i: (0, i, 0))],
        out_shape=[jax.ShapeDtypeStruct((h, s_, QK), MXU_DTYPE), jax.ShapeDtypeStruct((h, s_, VDIM), MXU_DTYPE)],
        compiler_params=_params("parallel"),
    )(kv, small, cos2, sin2, _rot_matrix())


def _dkv_post(dk, dv, ddt, cos2, nsin2, name="dkv_post"):
    h, s_, _ = dk.shape
    tr = _row_tile(s_)

    def body(dk_ref, dv_ref, ddt_ref, c_ref, s_ref, rot_ref, dkv_ref, dsm_ref):
        acc = dk_ref[0, :, NOPE:]
        for i in range(1, h):
            acc = acc + dk_ref[i, :, NOPE:]
        dsm_ref[:, :ROPE] = _rope_val(acc, c_ref[...], s_ref[...], rot_ref[...]).astype(dsm_ref.dtype)
        dsm_ref[:, ROPE:] = ddt_ref[:, ROPE:].astype(dsm_ref.dtype)
        for i in range(h):
            dkv_ref[i, :, :NOPE] = dk_ref[i, :, :NOPE].astype(dkv_ref.dtype)
            dkv_ref[i, :, NOPE:] = dv_ref[i].astype(dkv_ref.dtype)

    csp = pl.BlockSpec((tr, ROPE), lambda i: (i, 0))
    return pl.pallas_call(
        body, name=name, grid=(s_ // tr,),
        in_specs=[pl.BlockSpec((h, tr, QK), lambda i: (0, i, 0)), pl.BlockSpec((h, tr, VDIM), lambda i: (0, i, 0)),
                  pl.BlockSpec((tr, LANE), lambda i: (i, 0)), csp, csp, pl.BlockSpec((ROPE, ROPE), lambda i: (0, 0))],
        out_specs=[pl.BlockSpec((h, tr, NOPE + VDIM), lambda i: (0, i, 0)), pl.BlockSpec((tr, LANE), lambda i: (i, 0))],
        out_shape=[jax.ShapeDtypeStruct((h, s_, NOPE + VDIM), MXU_DTYPE), jax.ShapeDtypeStruct((s_, LANE), MXU_DTYPE)],
        compiler_params=_params("parallel"),
    )(dk, dv, ddt, cos2, nsin2, _rot_matrix())


def _attn_tile(s):
    return 512 if s % 1024 == 0 else s // 2


def _pairs(n, by_key):
    if by_key:
        pr = [(i, j) for j in range(n) for i in range(j, n)]
    else:
        pr = [(i, j) for i in range(n) for j in range(i + 1)]
    return (jnp.asarray([p[0] for p in pr], jnp.int32), jnp.asarray([p[1] for p in pr], jnp.int32))


ATTN_ROW_GROUPS = 2


def _row_groups(t, diag):
    tg = t // ATTN_ROW_GROUPS
    out = []
    for r in range(ATTN_ROW_GROUPS):
        nc = (r + 1) * tg if diag else t
        mask = None
        if diag:
            mask = (lax.broadcasted_iota(jnp.int32, (tg, nc), 1)
                    <= lax.broadcasted_iota(jnp.int32, (tg, nc), 0) + r * tg)
        out.append((slice(r * tg, (r + 1) * tg), nc, mask))
    return out


def _flash_specs(t, dk, dv):
    qsp = pl.BlockSpec((None, t, dk), lambda hh, p, qi, kj: (hh, qi[p], 0))
    ksp = pl.BlockSpec((None, t, dk), lambda hh, p, qi, kj: (hh, kj[p], 0))
    vsp = pl.BlockSpec((None, t, dv), lambda hh, p, qi, kj: (hh, kj[p], 0))
    osp = pl.BlockSpec((None, t, dv), lambda hh, p, qi, kj: (hh, qi[p], 0))
    lsp = pl.BlockSpec((None, t, 1), lambda hh, p, qi, kj: (hh, qi[p], 0))
    return qsp, ksp, vsp, osp, lsp


def _flash_fwd(q, k, v, name="flash_fwd"):
    h, s_, dk = q.shape
    dv = v.shape[-1]
    t = _attn_tile(s_)
    n = s_ // t
    qi, kj = _pairs(n, False)

    def body(qi_ref, kj_ref, q_ref, k_ref, v_ref, o_ref, lse_ref, m_s, l_s, acc):
        p_ = pl.program_id(1)
        i, j = qi_ref[p_], kj_ref[p_]

        @pl.when(j == 0)
        def _():
            m_s[...] = jnp.full_like(m_s, -jnp.inf)
            l_s[...] = jnp.zeros_like(l_s)
            acc[...] = jnp.zeros_like(acc)

        def update(diag):
            for rs, nc, mask in _row_groups(t, diag):
                sc = _dot(q_ref[rs, :], k_ref[0:nc, :], 1, 1)
                if mask is not None:
                    sc = jnp.where(mask, sc, -jnp.inf)
                m_old = m_s[rs, :]
                m_new = jnp.maximum(m_old, jnp.max(sc, axis=1, keepdims=True))
                alpha = jnp.exp(m_old - m_new)
                p = jnp.exp(sc - m_new)
                l_s[rs, :] = alpha * l_s[rs, :] + jnp.sum(p, axis=1, keepdims=True)
                acc[rs, :] = alpha * acc[rs, :] + _dot(_mx(p), v_ref[0:nc, :], 1, 0)
                m_s[rs, :] = m_new

        @pl.when(j < i)
        def _():
            update(False)

        @pl.when(j == i)
        def _():
            update(True)
            o_ref[...] = acc[...] / l_s[...]
            lse_ref[...] = m_s[...] + jnp.log(l_s[...])

    qsp, ksp, vsp, osp, lsp = _flash_specs(t, dk, dv)
    gs = pltpu.PrefetchScalarGridSpec(
        num_scalar_prefetch=2, grid=(h, qi.shape[0]), in_specs=[qsp, ksp, vsp], out_specs=[osp, lsp],
        scratch_shapes=[pltpu.VMEM((t, 1), F32), pltpu.VMEM((t, 1), F32), pltpu.VMEM((t, dv), F32)])
    return pl.pallas_call(
        body, name=name, grid_spec=gs,
        out_shape=[jax.ShapeDtypeStruct((h, s_, dv), F32), jax.ShapeDtypeStruct((h, s_, 1), F32)],
        compiler_params=_params("parallel", "arbitrary"),
    )(qi, kj, q, k, v)


def _flash_bwd_dq(q, k, v, o, do, lse, name="flash_bwd_dq"):
    h, s_, dk = q.shape
    dv = v.shape[-1]
    t = _attn_tile(s_)
    n = s_ // t
    qi, kj = _pairs(n, False)

    def body(qi_ref, kj_ref, q_ref, k_ref, v_ref, o_ref, do_ref, lse_ref, dq_ref, delta_ref, acc, delta_s):
        p_ = pl.program_id(1)
        i, j = qi_ref[p_], kj_ref[p_]

        @pl.when(j == 0)
        def _():
            delta_s[...] = jnp.sum(do_ref[...] * o_ref[...], axis=1, keepdims=True)
            acc[...] = jnp.zeros_like(acc)

        def update(diag):
            for rs, nc, mask in _row_groups(t, diag):
                sc = _dot(q_ref[rs, :], k_ref[0:nc, :], 1, 1)
                if mask is not None:
                    sc = jnp.where(mask, sc, -jnp.inf)
                p = jnp.exp(sc - lse_ref[rs, :])
                dp = _dot(_mx(do_ref[rs, :]), v_ref[0:nc, :], 1, 1)
                ds = p * (dp - delta_s[rs, :])
                acc[rs, :] += _dot(_mx(ds), k_ref[0:nc, :], 1, 0)

        @pl.when(j < i)
        def _():
            update(False)

        @pl.when(j == i)
        def _():
            update(True)
            dq_ref[...] = acc[...]
            delta_ref[...] = delta_s[...]

    qsp, ksp, vsp, osp, lsp = _flash_specs(t, dk, dv)
    gs = pltpu.PrefetchScalarGridSpec(
        num_scalar_prefetch=2, grid=(h, qi.shape[0]), in_specs=[qsp, ksp, vsp, osp, osp, lsp], out_specs=[qsp, lsp],
        scratch_shapes=[pltpu.VMEM((t, dk), F32), pltpu.VMEM((t, 1), F32)])
    return pl.pallas_call(
        body, name=name, grid_spec=gs,
        out_shape=[jax.ShapeDtypeStruct((h, s_, dk), F32), jax.ShapeDtypeStruct((h, s_, 1), F32)],
        compiler_params=_params("parallel", "arbitrary"),
    )(qi, kj, q, k, v, o, do, lse)


def _flash_bwd_dkv(q, k, v, do, lse, delta, name="flash_bwd_dkv"):
    h, s_, dk = q.shape
    dv = v.shape[-1]
    t = _attn_tile(s_)
    n = s_ // t
    qi, kj = _pairs(n, True)

    def body(qi_ref, kj_ref, q_ref, k_ref, v_ref, do_ref, lse_ref, delta_ref, dk_ref, dv_ref, dk_acc, dv_acc):
        p_ = pl.program_id(1)
        i, j = qi_ref[p_], kj_ref[p_]

        def update(diag):
            for rs, nc, mask in _row_groups(t, diag):
                sc = _dot(q_ref[rs, :], k_ref[0:nc, :], 1, 1)
                if mask is not None:
                    sc = jnp.where(mask, sc, -jnp.inf)
                p = jnp.exp(sc - lse_ref[rs, :])
                dob = _mx(do_ref[rs, :])
                dv_acc[0:nc, :] += _dot(_mx(p), dob, 0, 0)
                dp = _dot(dob, v_ref[0:nc, :], 1, 1)
                ds = p * (dp - delta_ref[rs, :])
                dk_acc[0:nc, :] += _dot(_mx(ds), q_ref[rs, :], 0, 0)

        @pl.when(i == j)
        def _():
            dk_acc[...] = jnp.zeros_like(dk_acc)
            dv_acc[...] = jnp.zeros_like(dv_acc)
            update(True)

        @pl.when(i > j)
        def _():
            update(False)

        @pl.when(i == n - 1)
        def _():
            dk_ref[...] = dk_acc[...]
            dv_ref[...] = dv_acc[...]

    qsp, ksp, vsp, osp, lsp = _flash_specs(t, dk, dv)
    gs = pltpu.PrefetchScalarGridSpec(
        num_scalar_prefetch=2, grid=(h, qi.shape[0]), in_specs=[qsp, ksp, vsp, osp, lsp, lsp], out_specs=[ksp, vsp],
        scratch_shapes=[pltpu.VMEM((t, dk), F32), pltpu.VMEM((t, dv), F32)])
    return pl.pallas_call(
        body, name=name, grid_spec=gs,
        out_shape=[jax.ShapeDtypeStruct((h, s_, dk), F32), jax.ShapeDtypeStruct((h, s_, dv), F32)],
        compiler_params=_params("parallel", "arbitrary"),
    )(qi, kj, q, k, v, do, lse, delta)


HALO = 8


def _conv_specs(s_, c, tr, after):
    main = pl.BlockSpec((tr, c), lambda i: (i, 0))
    per = tr // HALO
    if after:
        halo = pl.BlockSpec((HALO, c), lambda i: (jnp.minimum((i + 1) * per, s_ // HALO - 1), 0))
    else:
        halo = pl.BlockSpec((HALO, c), lambda i: (jnp.maximum(i * per - 1, 0), 0))
    return main, halo


def _fill_before(ext, t_ref, h_ref, tr):
    ext[0:HALO, :] = jnp.where(pl.program_id(0) > 0, h_ref[...], 0.0)
    ext[HALO:HALO + tr, :] = t_ref[...]


def _taps(ext, w_ref, tr):
    base = HALO - (CONV_K - 1)
    acc = ext[base:base + tr, :] * w_ref[0:1, :]
    for k in range(1, CONV_K):
        acc = acc + ext[base + k:base + k + tr, :] * w_ref[k:k + 1, :]
    return acc


def _conv_fwd(t, w, b, name="conv_fwd"):
    s_, c = t.shape
    tr = _row_tile(s_)

    def body(t_ref, h_ref, w_ref, b_ref, o_ref, ext):
        _fill_before(ext, t_ref, h_ref, tr)
        o_ref[...] = _silu(_taps(ext, w_ref, tr) + b_ref[...])

    main, halo = _conv_specs(s_, c, tr, False)
    return pl.pallas_call(
        body, name=name, grid=(s_ // tr,),
        in_specs=[main, halo, pl.BlockSpec((CONV_K, c), lambda i: (0, 0)), pl.BlockSpec((1, c), lambda i: (0, 0))],
        out_specs=main, out_shape=jax.ShapeDtypeStruct((s_, c), F32),
        scratch_shapes=[pltpu.VMEM((tr + HALO, c), F32)], compiler_params=_params("parallel"),
    )(t, t, w, b)


def _conv_bwd_pre(t, w, b, dact, name="conv_bwd_pre"):
    s_, c = t.shape
    tr = _row_tile(s_)

    def body(t_ref, h_ref, w_ref, b_ref, da_ref, dpre_ref, dwb_ref, ext):
        @pl.when(pl.program_id(0) == 0)
        def _():
            dwb_ref[...] = jnp.zeros_like(dwb_ref)

        _fill_before(ext, t_ref, h_ref, tr)
        dpre = da_ref[...] * _dsilu(_taps(ext, w_ref, tr) + b_ref[...])
        dpre_ref[...] = dpre
        base = HALO - (CONV_K - 1)
        for k in range(CONV_K):
            dwb_ref[k:k + 1, :] += jnp.sum(dpre * ext[base + k:base + k + tr, :], axis=0, keepdims=True)
        dwb_ref[CONV_K:CONV_K + 1, :] += jnp.sum(dpre, axis=0, keepdims=True)

    main, halo = _conv_specs(s_, c, tr, False)
    return pl.pallas_call(
        body, name=name, grid=(s_ // tr,),
        in_specs=[main, halo, pl.BlockSpec((CONV_K, c), lambda i: (0, 0)), pl.BlockSpec((1, c), lambda i: (0, 0)), main],
        out_specs=[main, pl.BlockSpec((8, c), lambda i: (0, 0))],
        out_shape=[jax.ShapeDtypeStruct((s_, c), F32), jax.ShapeDtypeStruct((8, c), F32)],
        scratch_shapes=[pltpu.VMEM((tr + HALO, c), F32)], compiler_params=_params("arbitrary"),
    )(t, t, w, b, dact)


def _conv_bwd_in(dpre, w, name="conv_bwd_in"):
    s_, c = dpre.shape
    tr = _row_tile(s_)
    nt = s_ // tr

    def body(d_ref, h_ref, w_ref, o_ref, ext):
        ext[0:tr, :] = d_ref[...]
        ext[tr:tr + HALO, :] = jnp.where(pl.program_id(0) < nt - 1, h_ref[...], 0.0)
        acc = ext[CONV_K - 1:CONV_K - 1 + tr, :] * w_ref[0:1, :]
        for k in range(1, CONV_K):
            acc = acc + ext[CONV_K - 1 - k:CONV_K - 1 - k + tr, :] * w_ref[k:k + 1, :]
        o_ref[...] = acc.astype(o_ref.dtype)

    main, halo = _conv_specs(s_, c, tr, True)
    return pl.pallas_call(
        body, name=name, grid=(nt,),
        in_specs=[main, halo, pl.BlockSpec((CONV_K, c), lambda i: (0, 0))],
        out_specs=main, out_shape=jax.ShapeDtypeStruct((s_, c), MXU_DTYPE),
        scratch_shapes=[pltpu.VMEM((tr + HALO, c), F32)], compiler_params=_params("parallel"),
    )(dpre, dpre, w)


def _ssd_chunk_common(dt_ref, dtt_ref, br_ref, bc_ref, ar_ref, ac_ref):
    li = lax.broadcasted_iota(jnp.int32, (CHUNK, CHUNK), 0)
    si = lax.broadcasted_iota(jnp.int32, (CHUNK, CHUNK), 1)
    lower = li >= si
    lower_b = lower.astype(BF16)
    upper_b = (li <= si).astype(BF16)
    zr = dt_ref[...] + br_ref[...]
    dtc = _softplus(zr)
    a_row = -jnp.exp(ar_ref[...])
    acum = _exact_dot(lower_b, dtc * a_row, 1, 0, False)
    dtt = _softplus(dtt_ref[...] + bc_ref[...])
    acum_t = _exact_dot(dtt * (-jnp.exp(ac_ref[...])), upper_b, 1, 0, True)
    return lower, upper_b, zr, dtc, a_row, acum, acum_t


def _head_terms(h, lower, dtc, acum, acum_t):
    lane = lax.broadcasted_iota(jnp.int32, (1, LANE), 1)
    sub = lax.broadcasted_iota(jnp.int32, (SSD_H, 1), 0)
    rowid = lax.broadcasted_iota(jnp.int32, (CHUNK, 1), 0)
    oh = (lane == HEAD_LANE + h).astype(F32)
    acol = jnp.sum(acum * oh, axis=1, keepdims=True)
    dcol = jnp.sum(dtc * oh, axis=1, keepdims=True)
    arow = jnp.sum(acum_t * (sub == h).astype(F32), axis=0, keepdims=True)
    alast = jnp.sum(jnp.where(rowid == CHUNK - 1, acol, 0.0), axis=0, keepdims=True)
    decay = jnp.exp(jnp.where(lower, acol - arow, -jnp.inf))
    return oh, acol, dcol, alast, decay


SSD_PAIRS = SSD_H // 2
PAIRS_PER_GROUP = SSD_E // 2


def _ps(q):
    return slice(q * LANE, (q + 1) * LANE)


def _gs(off, g):
    return slice(off + g * SSD_N, off + (g + 1) * SSD_N)


def _lanes(c0, c1):
    return jnp.where(lax.broadcasted_iota(jnp.int32, (1, LANE), 1) < SSD_P, c0, c1)


def _rows(c0, c1):
    return jnp.where(lax.broadcasted_iota(jnp.int32, (LANE, 1), 0) < SSD_P, c0, c1)


def _lane_halves(t):
    first = lax.broadcasted_iota(jnp.int32, (1, LANE), 1) < SSD_P
    return (jnp.sum(jnp.where(first, t, 0.0), axis=1, keepdims=True),
            jnp.sum(jnp.where(first, 0.0, t), axis=1, keepdims=True))


def _ssd_in_specs(rev):
    def ci(c):
        return c if rev is None else rev - c
    return [pl.BlockSpec((CHUNK, CONV_DIM), lambda c: (ci(c), 0)),
            pl.BlockSpec((CHUNK, LANE), lambda c: (ci(c), 0)),
            pl.BlockSpec((SSD_H, CHUNK), lambda c: (0, ci(c))),
            pl.BlockSpec((1, LANE), lambda c: (0, 0)), pl.BlockSpec((SSD_H, 1), lambda c: (0, 0)),
            pl.BlockSpec((1, LANE), lambda c: (0, 0)), pl.BlockSpec((SSD_H, 1), lambda c: (0, 0)),
            pl.BlockSpec((SSD_PAIRS, 1, LANE), lambda c: (0, 0, 0))]


def _ssd_fwd(xbc, small, dtt, bias_r, bias_c, alog_r, alog_c, dsk, name="ssd_fwd"):
    s_ = xbc.shape[0]
    nc = s_ // CHUNK

    def body(x_ref, dt_ref, dtt_ref, br_ref, bc_ref, ar_ref, ac_ref, dsk_ref, y_ref, prev_ref, state):
        @pl.when(pl.program_id(0) == 0)
        def _():
            state[...] = jnp.zeros_like(state)

        lower, _, _, dtc, _, acum, acum_t = _ssd_chunk_common(dt_ref, dtt_ref, br_ref, bc_ref, ar_ref, ac_ref)
        for g in range(SSD_G):
            bb = _mx(x_ref[:, _gs(B_OFF, g)])
            cb_ = _mx(x_ref[:, _gs(C_OFF, g)])
            cbm = _dot(cb_, bb, 1, 1)
            for e in range(PAIRS_PER_GROUP):
                q = g * PAIRS_PER_GROUP + e
                _, acol0, dcol0, alast0, decay0 = _head_terms(2 * q, lower, dtc, acum, acum_t)
                _, acol1, dcol1, alast1, decay1 = _head_terms(2 * q + 1, lower, dtc, acum, acum_t)
                x = x_ref[:, _ps(q)]
                xdt = x * _lanes(dcol0, dcol1)
                xb = _mx(xdt)
                yd = _lanes(_dot(_mx(cbm * decay0), xb, 1, 0), _dot(_mx(cbm * decay1), xb, 1, 0))
                prev = state[q]
                prev_ref[0, q] = prev
                yo = _dot(cb_, _mx(prev), 1, 1) * _lanes(jnp.exp(acol0), jnp.exp(acol1))
                ds = _lanes(jnp.exp(alast0 - acol0), jnp.exp(alast1 - acol1))
                st = _dot(_mx(xdt * ds), bb, 0, 0)
                state[q] = prev * _rows(jnp.exp(alast0), jnp.exp(alast1)) + st
                y_ref[:, _ps(q)] = yd + yo + x * dsk_ref[q]

    psp = pl.BlockSpec((1, SSD_PAIRS, LANE, SSD_N), lambda c: (c, 0, 0, 0))
    return pl.pallas_call(
        body, name=name, grid=(nc,),
        in_specs=_ssd_in_specs(None), out_specs=[pl.BlockSpec((CHUNK, SSD_W), lambda c: (c, 0)), psp],
        out_shape=[jax.ShapeDtypeStruct((s_, SSD_W), F32),
                   jax.ShapeDtypeStruct((nc, SSD_PAIRS, LANE, SSD_N), F32)],
        scratch_shapes=[pltpu.VMEM((SSD_PAIRS, LANE, SSD_N), F32)],
        compiler_params=_params("arbitrary"),
    )(xbc, small, dtt, bias_r, bias_c, alog_r, alog_c, dsk)


def _ssd_bwd(xbc, small, dtt, bias_r, bias_c, alog_r, alog_c, dsk, prev, dy, name="ssd_bwd"):
    s_ = xbc.shape[0]
    nc = s_ // CHUNK

    def body(x_ref, dt_ref, dtt_ref, br_ref, bc_ref, ar_ref, ac_ref, dsk_ref, prev_ref, dy_ref,
             dx_ref, ddt_ref, dpar_ref, dstate):
        @pl.when(pl.program_id(0) == 0)
        def _():
            dstate[...] = jnp.zeros_like(dstate)
            dpar_ref[...] = jnp.zeros_like(dpar_ref)

        lower, upper_b, zr, dtc, a_row, acum, acum_t = _ssd_chunk_common(
            dt_ref, dtt_ref, br_ref, bc_ref, ar_ref, ac_ref)
        strict = (lax.broadcasted_iota(jnp.int32, (CHUNK, CHUNK), 1)
                  < lax.broadcasted_iota(jnp.int32, (CHUNK, CHUNK), 0))
        strict_b = strict.astype(BF16)
        col2 = lax.broadcasted_iota(jnp.int32, (CHUNK, 2 * CHUNK), 1)
        strict2 = (jnp.where(col2 >= CHUNK, col2 - CHUNK, col2)
                   < lax.broadcasted_iota(jnp.int32, (CHUNK, 2 * CHUNK), 0))
        da_in = jnp.zeros((CHUNK, LANE), F32)
        r_off = jnp.zeros((CHUNK, LANE), F32)
        c_int = jnp.zeros((CHUNK, LANE), F32)
        c_row = jnp.zeros((1, LANE), F32)
        ddt = jnp.zeros((CHUNK, LANE), F32)
        dskip = jnp.zeros((1, LANE), F32)
        for g in range(SSD_G):
            bb = _mx(x_ref[:, _gs(B_OFF, g)])
            cb_ = _mx(x_ref[:, _gs(C_OFF, g)])
            cbm = _dot(cb_, bb, 1, 1)
            dcb = jnp.zeros((CHUNK, CHUNK), F32)
            dc_acc = jnp.zeros((CHUNK, SSD_N), F32)
            db_acc = jnp.zeros((CHUNK, SSD_N), F32)
            for e in range(PAIRS_PER_GROUP):
                q = g * PAIRS_PER_GROUP + e
                oh0, acol0, dcol0, alast0, decay0 = _head_terms(2 * q, lower, dtc, acum, acum_t)
                oh1, acol1, dcol1, alast1, decay1 = _head_terms(2 * q + 1, lower, dtc, acum, acum_t)
                x = x_ref[:, _ps(q)]
                dy = dy_ref[:, _ps(q)]
                dcol = _lanes(dcol0, dcol1)
                xdt = x * dcol
                xb = _mx(xdt)
                eacol = _lanes(jnp.exp(acol0), jnp.exp(acol1))
                ds = _lanes(jnp.exp(alast0 - acol0), jnp.exp(alast1 - acol1))
                ealast = _rows(jnp.exp(alast0), jnp.exp(alast1))
                dyb = _mx(dy)
                dyb0, dyb1 = _mx(_lanes(dy, 0.0)), _mx(_lanes(0.0, dy))
                dsh = dstate[q]
                dshb = _mx(dsh)
                prev = prev_ref[0, q]
                prevb = _mx(prev)
                dxdt_inter = ds * _dot(bb, dshb, 1, 1)
                dxdt = _lanes(_dot(_mx(cbm * decay0), dyb, 0, 0), _dot(_mx(cbm * decay1), dyb, 0, 0)) + dxdt_inter
                dwl0 = _dot(dyb0, xb, 1, 1) * decay0
                dwl1 = _dot(dyb1, xb, 1, 1) * decay1
                dcb = dcb + dwl0 + dwl1
                dyeb = _mx(dy * eacol)
                dc_acc = dc_acc + _dot(dyeb, prevb, 1, 0)
                db_acc = db_acc + _dot(_mx(xdt * ds), dshb, 1, 0)
                dstate[q] = _dot(dyeb, cb_, 0, 0) + ealast * dsh
                above = _exact_dot(upper_b, jnp.concatenate([dwl0 * cbm, dwl1 * cbm], axis=1), 1, 0, False)
                above = jnp.where(strict2, above, 0.0)
                da_in = (da_in + jnp.sum(above[:, :CHUNK], axis=1, keepdims=True) * oh0
                         + jnp.sum(above[:, CHUNK:], axis=1, keepdims=True) * oh1)
                y_off = _dot(cb_, prevb, 1, 1) * eacol
                r0, r1 = _lane_halves(dy * y_off)
                r_off = r_off + r0 * oh0 + r1 * oh1
                c0, c1 = _lane_halves(xdt * dxdt_inter)
                c_int = c_int + c0 * oh0 + c1 * oh1
                both = jnp.sum(dsh * prev, axis=1, keepdims=True) * ealast
                c_row = (c_row + jnp.sum(_rows(both, 0.0), axis=0, keepdims=True) * oh0
                         + jnp.sum(_rows(0.0, both), axis=0, keepdims=True) * oh1)
                t0, t1 = _lane_halves(dxdt * x)
                ddt = ddt + t0 * oh0 + t1 * oh1
                dx_ref[:, _ps(q)] = dxdt * dcol + dy * dsk_ref[q]
                k0, k1 = _lane_halves(dy * x)
                dskip = (dskip + jnp.sum(k0, axis=0, keepdims=True) * oh0 + jnp.sum(k1, axis=0, keepdims=True) * oh1)
            dcbb = _mx(dcb)
            dx_ref[:, _gs(C_OFF, g)] = dc_acc + _dot(dcbb, bb, 1, 0)
            dx_ref[:, _gs(B_OFF, g)] = db_acc + _dot(dcbb, cb_, 0, 0)
        da = (da_in + _exact_dot(upper_b, r_off, 1, 0, False) + _exact_dot(strict_b, c_int, 1, 0, False) + c_row)
        draw = (ddt + da * a_row) * _sigmoid(zr)
        ddt_ref[...] = draw
        dpar_ref[0:1, :] += jnp.sum(draw, axis=0, keepdims=True)
        dpar_ref[1:2, :] += jnp.sum(da * dtc, axis=0, keepdims=True) * a_row
        dpar_ref[2:3, :] += dskip

    rev = nc - 1
    psp = pl.BlockSpec((1, SSD_PAIRS, LANE, SSD_N), lambda c: (rev - c, 0, 0, 0))
    return pl.pallas_call(
        body, name=name, grid=(nc,),
        in_specs=_ssd_in_specs(rev) + [psp, pl.BlockSpec((CHUNK, SSD_W), lambda c: (rev - c, 0))],
        out_specs=[pl.BlockSpec((CHUNK, CONV_DIM), lambda c: (rev - c, 0)),
                   pl.BlockSpec((CHUNK, LANE), lambda c: (rev - c, 0)), pl.BlockSpec((8, LANE), lambda c: (0, 0))],
        out_shape=[jax.ShapeDtypeStruct((s_, CONV_DIM), F32), jax.ShapeDtypeStruct((s_, LANE), F32),
                   jax.ShapeDtypeStruct((8, LANE), F32)],
        scratch_shapes=[pltpu.VMEM((SSD_PAIRS, LANE, SSD_N), F32)],
        compiler_params=_params("arbitrary"),
    )(xbc, small, dtt, bias_r, bias_c, alog_r, alog_c, dsk, prev, dy)


GN = SSD_W // SSD_G


def _gated_norm_fwd(y, z, w, name="gated_norm_fwd"):
    s_, f = y.shape
    tr = _row_tile(s_)

    def body(y_ref, z_ref, w_ref, o_ref):
        for g in range(SSD_G):
            sl = slice(g * GN, (g + 1) * GN)
            gg = y_ref[:, sl] * _silu(z_ref[:, sl])
            r = lax.rsqrt(jnp.mean(gg * gg, axis=-1, keepdims=True) + EPS)
            o_ref[:, sl] = (gg * r * w_ref[:, sl]).astype(o_ref.dtype)

    row = pl.BlockSpec((tr, f), lambda i: (i, 0))
    wsp = pl.BlockSpec((1, f), lambda i: (0, 0))
    return pl.pallas_call(
        body, name=name, grid=(s_ // tr,), in_specs=[row, row, wsp], out_specs=row,
        out_shape=jax.ShapeDtypeStruct((s_, f), MXU_DTYPE), compiler_params=_params("parallel"),
    )(y, z, w.reshape(1, f))


def _gated_norm_bwd(y, z, w, dout, name="gated_norm_bwd"):
    s_, f = y.shape
    tr = _row_tile(s_)

    def body(y_ref, z_ref, w_ref, do_ref, dy_ref, dz_ref, dw_ref):
        @pl.when(pl.program_id(0) == 0)
        def _():
            dw_ref[...] = jnp.zeros_like(dw_ref)

        for g in range(SSD_G):
            sl = slice(g * GN, (g + 1) * GN)
            yv = y_ref[:, sl]
            zv = z_ref[:, sl]
            dov = do_ref[:, sl].astype(F32)
            sz = _silu(zv)
            gg = yv * sz
            r = lax.rsqrt(jnp.mean(gg * gg, axis=-1, keepdims=True) + EPS)
            gw = dov * w_ref[:, sl]
            c = jnp.mean(gw * gg, axis=-1, keepdims=True)
            dgg = r * gw - gg * (r * r * r * c)
            dy_ref[:, sl] = dgg * sz
            dz_ref[:, sl] = (dgg * yv * _dsilu(zv)).astype(dz_ref.dtype)
            dw_ref[:, sl] += jnp.sum(dov * gg * r, axis=0, keepdims=True)

    row = pl.BlockSpec((tr, f), lambda i: (i, 0))
    wsp = pl.BlockSpec((1, f), lambda i: (0, 0))
    return pl.pallas_call(
        body, name=name, grid=(s_ // tr,), in_specs=[row, row, wsp, row], out_specs=[row, row, wsp],
        out_shape=[jax.ShapeDtypeStruct((s_, f), F32), jax.ShapeDtypeStruct((s_, f), MXU_DTYPE),
                   jax.ShapeDtypeStruct((1, f), F32)],
        compiler_params=_params("arbitrary"),
    )(y, z, w.reshape(1, f), dout)


def _ffn_fwd(vv, w_gate, w_up, name="ffn_gate_up"):
    s_, d = vv.shape
    nb, _, f8 = w_gate.shape
    tm = _pick(s_, (1024, 512, 256, 128))

    def body(v_ref, wg_ref, wu_ref, g_ref, u_ref, a_ref):
        a = _mx(v_ref[...])
        g = _dot(a, _mx(wg_ref[...]), 1, 0)
        u = _dot(a, _mx(wu_ref[...]), 1, 0)
        g_ref[...] = g.astype(g_ref.dtype)
        u_ref[...] = u.astype(u_ref.dtype)
        a_ref[...] = (_silu(g) * u).astype(a_ref.dtype)

    wsp = pl.BlockSpec((None, d, f8), lambda j, i: (j, 0, 0))
    osp = pl.BlockSpec((None, tm, f8), lambda j, i: (j, i, 0))
    return pl.pallas_call(
        body, name=name, grid=(nb, s_ // tm),
        in_specs=[pl.BlockSpec((tm, d), lambda j, i: (i, 0)), wsp, wsp], out_specs=[osp] * 3,
        out_shape=[jax.ShapeDtypeStruct((nb, s_, f8), MXU_DTYPE)] * 3,
        compiler_params=_params("parallel", "parallel"),
    )(vv, w_gate, w_up)


def _ffn_bwd_act(dffn, w_down, gate, up, name="ffn_d_act"):
    s_, d = dffn.shape
    nb, f8, _ = w_down.shape
    tm = _pick(s_, (1024, 512, 256, 128))

    def body(d_ref, w_ref, g_ref, u_ref, dg_ref, du_ref):
        dact = _dot(_mx(d_ref[...]), _mx(w_ref[...]), 1, 1)
        g = g_ref[...].astype(F32)
        dg_ref[...] = (dact * u_ref[...].astype(F32) * _dsilu(g)).astype(dg_ref.dtype)
        du_ref[...] = (dact * _silu(g)).astype(du_ref.dtype)

    osp = pl.BlockSpec((None, tm, f8), lambda j, i: (j, i, 0))
    return pl.pallas_call(
        body, name=name, grid=(nb, s_ // tm),
        in_specs=[pl.BlockSpec((tm, d), lambda j, i: (i, 0)), pl.BlockSpec((None, f8, d), lambda j, i: (j, 0, 0)),
                  osp, osp],
        out_specs=[osp, osp], out_shape=[jax.ShapeDtypeStruct((nb, s_, f8), MXU_DTYPE)] * 2,
        compiler_params=_params("parallel", "parallel"),
    )(dffn, w_down, gate, up)


def _ffn_bwd_in(dgate, w_gate, dup, w_up, name="ffn_d_in"):
    nb, s_, f8 = dgate.shape
    d = w_gate.shape[1]
    tm = _pick(s_, (1024, 512, 256, 128))
    tn = _pick(d, (1024, 512, 256, 128))

    def body(dg_ref, wg_ref, du_ref, wu_ref, o_ref, acc):
        j = pl.program_id(2)
        part = _dot(_mx(dg_ref[...]), _mx(wg_ref[...]), 1, 1) + _dot(_mx(du_ref[...]), _mx(wu_ref[...]), 1, 1)

        @pl.when(j == 0)
        def _():
            acc[...] = part

        @pl.when(j > 0)
        def _():
            acc[...] += part

        @pl.when(j == nb - 1)
        def _():
            o_ref[...] = acc[...]

    asp = pl.BlockSpec((None, tm, f8), lambda i, n, j: (j, i, 0))
    wsp = pl.BlockSpec((None, tn, f8), lambda i, n, j: (j, n, 0))
    return pl.pallas_call(
        body, name=name, grid=(s_ // tm, d // tn, nb),
        in_specs=[asp, wsp, asp, wsp], out_specs=pl.BlockSpec((tm, tn), lambda i, n, j: (i, n)),
        out_shape=jax.ShapeDtypeStruct((s_, d), F32), scratch_shapes=[pltpu.VMEM((tm, tn), F32)],
        compiler_params=_params("parallel", "parallel", "arbitrary"),
    )(dgate, w_gate, dup, w_up)


def _adam_math(g, w, m, v):
    m2 = ADAM_B1 * m + (1.0 - ADAM_B1) * g
    v2 = ADAM_B2 * v + (1.0 - ADAM_B2) * (g * g)
    m_hat = m2 / (1.0 - ADAM_B1 ** ADAM_STEP)
    v_hat = v2 / (1.0 - ADAM_B2 ** ADAM_STEP)
    delta = -ADAM_LR * (m_hat / (jnp.sqrt(v_hat) + ADAM_EPS) + ADAM_WD * w)
    return delta, m2, v2


def _adamw(parts, w, m, v, name="adamw"):
    nd, r_, c = parts.shape
    tr = _pick(r_, (128, 64, 32, 16, 8))

    def body(p_ref, w_ref, m_ref, v_ref, g_ref, d_ref, m2_ref, v2_ref):
        g = p_ref[0].astype(F32)
        for i in range(1, nd):
            g = g + p_ref[i].astype(F32)
        delta, m2, v2 = _adam_math(g, w_ref[...], m_ref[...], v_ref[...])
        g_ref[...] = g
        d_ref[...] = delta
        m2_ref[...] = m2
        v2_ref[...] = v2

    row = pl.BlockSpec((tr, c), lambda i: (i, 0))
    psp = pl.BlockSpec((nd, tr, c), lambda i: (0, i, 0))
    return pl.pallas_call(
        body, name=name, grid=(r_ // tr,), in_specs=[psp, row, row, row], out_specs=[row] * 4,
        out_shape=[jax.ShapeDtypeStruct((r_, c), F32)] * 4, compiler_params=_params("parallel"),
    )(parts, w, m, v)


def _adamw_small(parts, w, m, v, name="adamw_small"):
    nd = parts.shape[0]

    def body(p_ref, w_ref, m_ref, v_ref, g_ref, d_ref, m2_ref, v2_ref):
        g = p_ref[0]
        for i in range(1, nd):
            g = g + p_ref[i]
        delta, m2, v2 = _adam_math(g, w_ref[...], m_ref[...], v_ref[...])
        g_ref[...] = g
        d_ref[...] = delta
        m2_ref[...] = m2
        v2_ref[...] = v2

    return pl.pallas_call(
        body, name=name, out_shape=[jax.ShapeDtypeStruct(w.shape, F32)] * 4,
        compiler_params=pltpu.CompilerParams(vmem_limit_bytes=VMEM_LIMIT_BYTES),
    )(parts, w, m, v)


_HBM = pl.BlockSpec(memory_space=pltpu.HBM)
_MESH = pl.DeviceIdType.MESH


def _all_gather(xs, name):
    na = len(xs)

    def body(*refs):
        x_refs, out_refs = refs[:na], refs[na:2 * na]
        send_sems, recv_sems, local_sems = refs[2 * na:]
        x, y, c = lax.axis_index("x"), lax.axis_index("y"), lax.axis_index("c")
        me, sibling = (x, y, c), (x, y, 1 - c)
        chips = [(1 - x, y), (x, 1 - y), (1 - x, 1 - y)]

        def slot(a, px, py, pc):
            return out_refs[a].at[4 * px + 2 * py + pc]

        def copy(a, k, block, to, src=None):
            return pltpu.make_async_remote_copy(
                src_ref=slot(a, *block) if src is None else src, dst_ref=slot(a, *block),
                send_sem=send_sems.at[a, k], recv_sem=recv_sems.at[a, k], device_id=to, device_id_type=_MESH)

        mine = [pltpu.make_async_copy(x_refs[a], slot(a, *me), local_sems.at[a]) for a in range(na)]
        started = []
        for a in range(na):
            mine[a].start()
            first = [copy(a, 0, me, sibling, src=x_refs[a])]
            first += [copy(a, 1 + j, me, (*chip, c), src=x_refs[a]) for j, chip in enumerate(chips)]
            for cp in first:
                cp.start()
            started += first
        for a in range(na):
            for j, chip in enumerate(chips):
                copy(a, 1 + j, (*chip, c), me).wait_recv()
                fwd = copy(a, 4 + j, (*chip, c), sibling)
                fwd.start()
                started.append(fwd)
        for a in range(na):
            copy(a, 0, sibling, me).wait_recv()
            for j, chip in enumerate(chips):
                copy(a, 4 + j, (*chip, 1 - c), me).wait_recv()
        for cp in started:
            cp.wait_send()
        for cp in mine:
            cp.wait()

    return pl.pallas_call(
        body, name=name, out_shape=[jax.ShapeDtypeStruct((N_DEV,) + t.shape, t.dtype) for t in xs],
        in_specs=[_HBM] * na, out_specs=[_HBM] * na,
        scratch_shapes=[pltpu.SemaphoreType.DMA((na, 7)), pltpu.SemaphoreType.DMA((na, 7)),
                        pltpu.SemaphoreType.DMA((na,))],
    )(*xs)


_SEM = pl.BlockSpec(memory_space=pltpu.SEMAPHORE)
_EFFECT = pltpu.SideEffectType.DATAFLOW_SIDE_EFFECTING


def _peers(x, y, c):
    out = []
    for k in range(1, N_DEV):
        px = 1 - x if k & 4 else x
        py = 1 - y if k & 2 else y
        pc = 1 - c if k & 1 else c
        out.append(((px, py, pc), 4 * px + 2 * py + pc))
    return out


def _push_copies(scatter, src_refs, land_refs, send_sems, recv_sems):
    x, y, c = lax.axis_index("x"), lax.axis_index("y"), lax.axis_index("c")
    me = 4 * x + 2 * y + c
    pairs = []
    for a, (src, land) in enumerate(zip(src_refs, land_refs)):
        for k, (peer, slot) in enumerate(_peers(x, y, c)):
            out_src = src.at[slot] if scatter else src
            si = a * (N_DEV - 1) + k
            send = pltpu.make_async_remote_copy(src_ref=out_src, dst_ref=land.at[me], send_sem=send_sems.at[si],
                                                recv_sem=recv_sems.at[si], device_id=peer, device_id_type=_MESH)
            recv = pltpu.make_async_remote_copy(src_ref=out_src, dst_ref=land.at[slot], send_sem=send_sems.at[si],
                                                recv_sem=recv_sems.at[si], device_id=peer, device_id_type=_MESH)
            pairs.append((send, recv))
    return pairs


def _push_start(srcs, scatter, dep, name):
    na = len(srcs)
    shapes = [t.shape[1:] if scatter else t.shape for t in srcs]
    lands = [pltpu.with_memory_space_constraint(lax.empty((N_DEV,) + s, t.dtype), pltpu.HBM) for s, t in zip(shapes, srcs)]

    def body(*refs):
        src_refs, land_refs = refs[:na], refs[na:2 * na]
        send_sems, recv_sems = refs[2 * na + 1], refs[2 * na + 2]
        token = refs[-1]
        for send, _ in _push_copies(scatter, src_refs, land_refs, send_sems, recv_sems):
            send.start()
        token[...] = jnp.zeros_like(token)

    sem = pltpu.SemaphoreType.DMA((na * (N_DEV - 1),))
    outs = pl.pallas_call(
        body, name=name,
        out_shape=(sem, sem) + tuple(pltpu.HBM(t.shape, t.dtype) for t in srcs)
        + tuple(pltpu.HBM(t.shape, t.dtype) for t in lands) + (jax.ShapeDtypeStruct((8, LANE), F32),),
        in_specs=[_HBM] * (2 * na) + [pl.BlockSpec(memory_space=pl.ANY)],
        out_specs=(_SEM, _SEM) + (_HBM,) * (2 * na) + (pl.BlockSpec(memory_space=pltpu.VMEM),),
        input_output_aliases={i: 2 + i for i in range(2 * na)},
        compiler_params=pltpu.CompilerParams(has_side_effects=_EFFECT),
    )(*[pltpu.with_memory_space_constraint(t, pltpu.HBM) for t in srcs], *lands, dep)
    return outs[0], outs[1], outs[2:2 + na], outs[2 + na:2 + 2 * na], outs[-1]


def _push_wait(send_sems, recv_sems, src_thru, land_thru, scatter, after, name):
    na = len(src_thru)

    def body(*refs):
        src_refs, land_refs = refs[:na], refs[na:2 * na]
        ssem, rsem = refs[2 * na], refs[2 * na + 1]
        for send, recv in _push_copies(scatter, src_refs, land_refs, ssem, rsem):
            send.wait_send()
            recv.wait_recv()

    outs = pl.pallas_call(
        body, name=name,
        out_shape=tuple(pltpu.HBM(t.shape, t.dtype) for t in src_thru) + tuple(pltpu.HBM(t.shape, t.dtype) for t in land_thru),
        in_specs=[_HBM] * (2 * na) + [_SEM, _SEM, pl.BlockSpec(memory_space=pl.ANY)],
        out_specs=(_HBM,) * (2 * na),
        input_output_aliases={i: i for i in range(2 * na)},
        compiler_params=pltpu.CompilerParams(has_side_effects=_EFFECT),
    )(*src_thru, *land_thru, send_sems, recv_sems, after)
    return outs[:na], outs[na:]


def _exchange_behind(srcs, scatter, dep, name):
    send_sems, recv_sems, thru, lands, token = _push_start(srcs, scatter, dep, name + "_start")

    def finish(after):
        src_done, land_done = _push_wait(send_sems, recv_sems, thru, lands, scatter, after, name + "_wait")
        return _place_own(land_done, src_done, scatter, name + "_own")

    return token[0, 0], finish


def _place_own(lands, srcs, scatter, name):
    me = (4 * lax.axis_index("x") + 2 * lax.axis_index("y") + lax.axis_index("c")).astype(jnp.int32).reshape(1)
    outs = []
    for a, (land, src) in enumerate(zip(lands, srcs)):
        r_, c_ = land.shape[1:]
        tr = _pick(r_, (512, 256, 128, 64, 32, 16))

        def body(me_ref, land_ref, src_ref, out_ref):
            out_ref[...] = src_ref[...]

        src_spec = (pl.BlockSpec((None, tr, c_), lambda i, me_: (me_[0], i, 0)) if scatter
                    else pl.BlockSpec((tr, c_), lambda i, me_: (i, 0)))
        gs = pltpu.PrefetchScalarGridSpec(
            num_scalar_prefetch=1, grid=(r_ // tr,),
            in_specs=[pl.BlockSpec(memory_space=pl.ANY), src_spec],
            out_specs=pl.BlockSpec((None, tr, c_), lambda i, me_: (me_[0], i, 0)))
        outs.append(pl.pallas_call(
            body, name=f"{name}_{a}", grid_spec=gs, out_shape=jax.ShapeDtypeStruct(land.shape, land.dtype),
            input_output_aliases={1: 0}, compiler_params=_params("arbitrary"),
        )(me, land, src))
    return outs


_BIG = (("w_in", D_MODEL, D_IN, 1), ("w_uq", Q_RANK, HEADS * QK, 1), ("w_ukv", KV_RANK, HEADS * (NOPE + VDIM), 1),
        ("w_out", D_MODEL, D_MODEL, 0), ("w_gate", D_MODEL, D_FF, 1), ("w_up", D_MODEL, D_FF, 1),
        ("w_down", D_FF, D_MODEL, 0))
_CQKV = (0, Q_RANK + KV_RANK)
_KR = (_CQKV[1], _CQKV[1] + ROPE)
_Z = (_KR[1], _KR[1] + SSD_W)
_XBC = (_Z[1], _Z[1] + CONV_DIM)
_DT = (_XBC[1], _XBC[1] + SSD_H)


def _win_segments(w_in_g):
    w = jnp.transpose(w_in_g, (1, 0, 2)).reshape(D_MODEL, D_IN)
    small = jnp.concatenate([w[:, _KR[0]:_KR[1]], w[:, _DT[0]:_DT[1]],
                             jnp.zeros((D_MODEL, LANE - ROPE - SSD_H), w.dtype)], axis=1)
    return w[:, _CQKV[0]:_CQKV[1]], w[:, _Z[0]:_Z[1]], w[:, _XBC[0]:_XBC[1]], small


def _win_from_segments(g_cqkv, g_z, g_xbc, g_small):
    w = jnp.concatenate([g_cqkv, g_small[:, :ROPE], g_z, g_xbc, g_small[:, ROPE:ROPE + SSD_H]], axis=1)
    return jnp.transpose(w.reshape(D_MODEL, N_DEV, D_IN // N_DEV), (1, 0, 2))


_SMALL = (("q_norm_w", 512), ("kv_norm_w", 512), ("conv_b", CONV_DIM), ("dt_bias", SSD_H), ("a_log", SSD_H),
          ("d_skip", SSD_H), ("ssd_norm_w", SSD_W), ("attn_out_norm_w", 1024), ("pre_mix_norm_w", D_MODEL),
          ("post_mix_norm_w", D_MODEL), ("pre_ffn_norm_w", D_MODEL), ("post_ffn_norm_w", D_MODEL),
          ("conv_w", CONV_K * CONV_DIM))
_SMALL_ROWS = -(-sum(-(-n // LANE) for _, n in _SMALL) // 8) * 8


def _pack_small(vals):
    rows = []
    for name, n in _SMALL:
        v = vals[name].reshape(-1).astype(F32)
        pad = -(-n // LANE) * LANE
        rows.append(jnp.pad(v, (0, pad - n)).reshape(-1, LANE))
    m = jnp.concatenate(rows, axis=0)
    return jnp.pad(m, ((0, _SMALL_ROWS - m.shape[0]), (0, 0)))


def _unpack_small(m):
    out, r = {}, 0
    for name, n in _SMALL:
        nr = -(-n // LANE)
        out[name] = m[r:r + nr].reshape(-1)[:n]
        r += nr
    return out


def _head_row(v):
    return jnp.pad(v.reshape(1, -1).astype(F32), ((0, 0), (HEAD_LANE, LANE - HEAD_LANE - v.shape[-1])))


def _local_step(x, positions, target, wg, small, weights, on_grads):
    w_cqkv, w_z, w_xbc, w_small = _win_segments(wg["w_in"])
    conv_w = wg["conv_w"]
    conv_b = small["conv_b"].reshape(1, CONV_DIM)
    qkv_norm_w = jnp.concatenate([small["q_norm_w"], small["kv_norm_w"]])
    attn_norm_w = small["attn_out_norm_w"].reshape(HEADS, 1, VDIM)
    scale = QK ** -0.5

    inv_freq = ROPE_THETA ** (-jnp.arange(0, ROPE, 2, dtype=F32) / ROPE)
    ang = positions.astype(F32)[:, None] * inv_freq
    cos2 = jnp.tile(jnp.cos(ang), (1, 2))
    sin2 = jnp.tile(jnp.sin(ang), (1, 2))

    u = _rms_fwd(x, small["pre_mix_norm_w"], out_dtype=MXU_DTYPE, name="pre_mix_norm")
    cqkv = _mm(u, w_cqkv, "nn", name="in_proj_qkv")
    z = _mm(u, w_z, "nn", name="in_proj_z")
    xbc = _mm(u, w_xbc, "nn", name="in_proj_xbc")
    sm = _mm(u, w_small, "nn", name="in_proj_small")

    w_uq, w_ukv, w_out = weights("heads", cqkv)
    w_out = w_out.reshape(D_MODEL, D_MODEL)
    w_out_a = w_out[:HEADS * VDIM].reshape(HEADS, VDIM, D_MODEL)
    w_out_s = w_out[HEADS * VDIM:]
    qkvn = _rms_fwd(cqkv, qkv_norm_w, groups=2, out_dtype=MXU_DTYPE, name="qkv_norm")
    q = _mm(qkvn, w_uq, "nn", b_blk=True, out_blk=True, a_cols=(0, Q_RANK), name="q_up")
    kv = _mm(qkvn, w_ukv, "nn", b_blk=True, out_blk=True, a_cols=(Q_RANK, KV_RANK), name="kv_up")
    q_h = _q_prep(q, cos2, sin2, scale, name="q_prep")
    k_h, v_h = _kv_prep(kv, sm, cos2, sin2)
    o_h, lse = _flash_fwd(q_h, k_h, v_h)
    attn = _hnorm_fwd(o_h, attn_norm_w)

    xbc_act = _conv_fwd(xbc, conv_w, conv_b)
    dtt = jnp.transpose(sm[:, HEAD_LANE:HEAD_LANE + SSD_H])
    ssd_args = (xbc_act, sm, dtt, _head_row(small["dt_bias"]), small["dt_bias"].reshape(SSD_H, 1),
                _head_row(small["a_log"]), small["a_log"].reshape(SSD_H, 1),
                jnp.broadcast_to(small["d_skip"].reshape(SSD_H, 1), (SSD_H, SSD_P)).reshape(SSD_PAIRS, 1, LANE))
    y_ssd, prev = _ssd_fwd(*ssd_args)
    ssm = _gated_norm_fwd(y_ssd, z, small["ssd_norm_w"])

    mix = _mm(attn, w_out_a, "nn", a_blk=True, b_blk=True, fuse=HEADS, name="out_proj_attn")
    mix = _mm(ssm, w_out_s, "nn", add=mix, name="out_proj_ssm")
    h1 = _rms_fwd(mix, small["post_mix_norm_w"], res=x, name="post_mix_norm")

    w_gate, w_up, w_down = weights("ffn", mix)
    vv = _rms_fwd(h1, small["pre_ffn_norm_w"], out_dtype=MXU_DTYPE, name="pre_ffn_norm")
    gate, up, act = _ffn_fwd(vv, w_gate, w_up)
    ffn = _mm(act, w_down, "nn", a_blk=True, b_blk=True, fuse=2, name="ffn_down")
    loss_blk, dy, dffn, g_post_ffn = _loss_head(ffn, h1, target, small["post_ffn_norm_w"])

    g_down = _mm(act, dffn, "tn", a_blk=True, out_blk=True, out_dtype=MXU_DTYPE, name="g_down")
    dgate, dup = _ffn_bwd_act(dffn, w_down, gate, up)
    dvv = _ffn_bwd_in(dgate, w_gate, dup, w_up)
    g_gate = _mm(vv, dgate, "tn", b_blk=True, out_blk=True, out_dtype=MXU_DTYPE, name="g_gate")
    g_up = _mm(vv, dup, "tn", b_blk=True, out_blk=True, out_dtype=MXU_DTYPE, name="g_up")
    pre_ffn_w = small["pre_ffn_norm_w"] + on_grads("ffn", [g_gate, g_up, g_down])
    dh1, g_pre_ffn = _rms_bwd(h1, pre_ffn_w, [dvv], res=dy, name="pre_ffn_norm_bwd")

    dmix, g_post_mix = _rms_bwd(mix, small["post_mix_norm_w"], [dh1], out_dtype=MXU_DTYPE, name="post_mix_norm_bwd")
    dattn = _mm(dmix, w_out_a, "nt", b_blk=True, out_blk=True, name="d_attn")
    dssm = _mm(dmix, w_out_s, "nt", name="d_ssm")
    g_out_a = _mm(attn, dmix, "tn", a_blk=True, out_blk=True, out_dtype=MXU_DTYPE, name="g_out_attn")
    g_out_s = _mm(ssm, dmix, "tn", out_dtype=MXU_DTYPE, name="g_out_ssm")
    g_out = jnp.concatenate([g_out_a.reshape(HEADS * VDIM, D_MODEL), g_out_s], axis=0)

    do_h, g_attn_norm = _hnorm_bwd(o_h, attn_norm_w, dattn)
    dq_h, delta = _flash_bwd_dq(q_h, k_h, v_h, o_h, do_h, lse)
    dk_h, dv_h = _flash_bwd_dkv(q_h, k_h, v_h, do_h, lse, delta)
    dq = _q_prep(dq_h, cos2, -sin2, scale, name="dq_post")

    dy_ssd, dz, g_ssd_norm = _gated_norm_bwd(y_ssd, z, small["ssd_norm_w"], dssm)
    dxbc_act, ddt, dpar = _ssd_bwd(*ssd_args, prev, dy_ssd)
    dkv, dsm = _dkv_post(dk_h, dv_h, ddt, cos2, -sin2)
    dpre, dwb = _conv_bwd_pre(xbc, conv_w, conv_b, dxbc_act)
    dxbc = _conv_bwd_in(dpre, conv_w)

    dqn = _mm(dq, w_uq, "nt", a_blk=True, b_blk=True, fuse=HEADS, name="d_qn")
    dkvn = _mm(dkv, w_ukv, "nt", a_blk=True, b_blk=True, fuse=HEADS, name="d_kvn")
    g_uq = _mm(qkvn, dq, "tn", b_blk=True, out_blk=True, a_cols=(0, Q_RANK), out_dtype=MXU_DTYPE, name="g_uq")
    g_ukv = _mm(qkvn, dkv, "tn", b_blk=True, out_blk=True, a_cols=(Q_RANK, KV_RANK), out_dtype=MXU_DTYPE, name="g_ukv")
    heads_token = on_grads("heads", [g_uq, g_ukv, g_out.reshape(N_DEV, D_MODEL // N_DEV, D_MODEL)])
    dcqkv, g_qkv_norm = _rms_bwd(cqkv, qkv_norm_w + heads_token, [dqn, dkvn], out_dtype=MXU_DTYPE, name="qkv_norm_bwd")

    g_in = _win_from_segments(_mm(u, dcqkv, "tn", out_dtype=MXU_DTYPE, name="g_in_qkv"),
                              _mm(u, dz, "tn", out_dtype=MXU_DTYPE, name="g_in_z"),
                              _mm(u, dxbc, "tn", out_dtype=MXU_DTYPE, name="g_in_xbc"),
                              _mm(u, dsm, "tn", out_dtype=MXU_DTYPE, name="g_in_small"))
    in_token = on_grads("in", [g_in])
    du = _mm(dsm + in_token.astype(dsm.dtype), w_small, "nt", name="d_u_small")
    du = _mm(dcqkv, w_cqkv, "nt", add=du, name="d_u_qkv")
    du = _mm(dz, w_z, "nt", add=du, name="d_u_z")
    du = _mm(dxbc, w_xbc, "nt", add=du, name="d_u_xbc")
    dx, g_pre_mix = _rms_bwd(x, small["pre_mix_norm_w"], [du], res=dh1, name="pre_mix_norm_bwd")

    hl = slice(HEAD_LANE, HEAD_LANE + SSD_H)
    g_small = {"q_norm_w": g_qkv_norm[0, :Q_RANK], "kv_norm_w": g_qkv_norm[0, Q_RANK:], "conv_b": dwb[CONV_K],
               "dt_bias": dpar[0, hl], "a_log": dpar[1, hl], "d_skip": dpar[2, hl], "ssd_norm_w": g_ssd_norm,
               "attn_out_norm_w": g_attn_norm, "pre_mix_norm_w": g_pre_mix, "post_mix_norm_w": g_post_mix,
               "pre_ffn_norm_w": g_pre_ffn, "post_ffn_norm_w": g_post_ffn, "conv_w": dwb[:CONV_K]}
    return loss_blk[0, 0], dx, g_small


_WEIGHT_ORDER = ("w_in", "q_norm_w", "w_uq", "kv_norm_w", "w_ukv", "conv_w", "conv_b", "dt_bias", "a_log", "d_skip",
                 "ssd_norm_w", "attn_out_norm_w", "w_out", "pre_mix_norm_w", "post_mix_norm_w", "pre_ffn_norm_w",
                 "post_ffn_norm_w", "w_gate", "w_up", "w_down")


def kernel(x, positions, w_in, q_norm_w, w_uq, kv_norm_w, w_ukv, conv_w, conv_b, dt_bias, a_log, d_skip, ssd_norm_w, attn_out_norm_w, w_out, pre_mix_norm_w, post_mix_norm_w, pre_ffn_norm_w, post_ffn_norm_w, w_gate, w_up, w_down, loss_target, m_w_in, m_q_norm_w, m_w_uq, m_kv_norm_w, m_w_ukv, m_conv_w, m_conv_b, m_dt_bias, m_a_log, m_d_skip, m_ssd_norm_w, m_attn_out_norm_w, m_w_out, m_pre_mix_norm_w, m_post_mix_norm_w, m_pre_ffn_norm_w, m_post_ffn_norm_w, m_w_gate, m_w_up, m_w_down, v_w_in, v_q_norm_w, v_w_uq, v_kv_norm_w, v_w_ukv, v_conv_w, v_conv_b, v_dt_bias, v_a_log, v_d_skip, v_ssd_norm_w, v_attn_out_norm_w, v_w_out, v_pre_mix_norm_w, v_post_mix_norm_w, v_pre_ffn_norm_w, v_post_ffn_norm_w, v_w_gate, v_w_up, v_w_down):
    w = dict(w_in=w_in, q_norm_w=q_norm_w, w_uq=w_uq, kv_norm_w=kv_norm_w, w_ukv=w_ukv, conv_w=conv_w, conv_b=conv_b,
             dt_bias=dt_bias, a_log=a_log, d_skip=d_skip, ssd_norm_w=ssd_norm_w, attn_out_norm_w=attn_out_norm_w,
             w_out=w_out, pre_mix_norm_w=pre_mix_norm_w, post_mix_norm_w=post_mix_norm_w,
             pre_ffn_norm_w=pre_ffn_norm_w, post_ffn_norm_w=post_ffn_norm_w, w_gate=w_gate, w_up=w_up, w_down=w_down)
    m = dict(w_in=m_w_in, q_norm_w=m_q_norm_w, w_uq=m_w_uq, kv_norm_w=m_kv_norm_w, w_ukv=m_w_ukv, conv_w=m_conv_w,
             conv_b=m_conv_b, dt_bias=m_dt_bias, a_log=m_a_log, d_skip=m_d_skip, ssd_norm_w=m_ssd_norm_w,
             attn_out_norm_w=m_attn_out_norm_w, w_out=m_w_out, pre_mix_norm_w=m_pre_mix_norm_w,
             post_mix_norm_w=m_post_mix_norm_w, pre_ffn_norm_w=m_pre_ffn_norm_w, post_ffn_norm_w=m_post_ffn_norm_w,
             w_gate=m_w_gate, w_up=m_w_up, w_down=m_w_down)
    v = dict(w_in=v_w_in, q_norm_w=v_q_norm_w, w_uq=v_w_uq, kv_norm_w=v_kv_norm_w, w_ukv=v_w_ukv, conv_w=v_conv_w,
             conv_b=v_conv_b, dt_bias=v_dt_bias, a_log=v_a_log, d_skip=v_d_skip, ssd_norm_w=v_ssd_norm_w,
             attn_out_norm_w=v_attn_out_norm_w, w_out=v_w_out, pre_mix_norm_w=v_pre_mix_norm_w,
             post_mix_norm_w=v_post_mix_norm_w, pre_ffn_norm_w=v_pre_ffn_norm_w, post_ffn_norm_w=v_post_ffn_norm_w,
             w_gate=v_w_gate, w_up=v_w_up, w_down=v_w_down)
    w, m, v = ({k: t[0] for k, t in d.items()} for d in (w, m, v))
    me = 4 * lax.axis_index("x") + 2 * lax.axis_index("y") + lax.axis_index("c")
    groups = {"in": ("w_in",), "heads": ("w_uq", "w_ukv", "w_out"), "ffn": ("w_gate", "w_up", "w_down")}
    cshard = CONV_DIM // N_DEV

    shards = [w["w_in"].astype(MXU_DTYPE),
              jnp.stack(_split3(w["conv_w"])).reshape(3 * CONV_K, cshard).astype(MXU_DTYPE)]
    w_in_g, cw = _all_gather(shards, name="gather_weights")
    cw = cw.astype(F32).reshape(N_DEV, 3, CONV_K, cshard)
    wg = {"w_in": w_in_g, "conv_w": jnp.transpose(cw[:, 0] + cw[:, 1] + cw[:, 2], (1, 0, 2)).reshape(CONV_K, CONV_DIM)}
    arriving, dep, started = {}, wg["conv_w"], jnp.zeros((), F32)
    small = {name: w[name] for name, _ in _SMALL if name != "conv_w"}
    for group in ("heads", "ffn"):
        token, arriving[group] = _exchange_behind([w[name].astype(MXU_DTYPE) for name in groups[group]], False,
                                                  dep, group + "_weights")
        started = started + token
        dep = jnp.zeros((8, LANE), F32) + started
    small["pre_mix_norm_w"] = small["pre_mix_norm_w"] + started

    leaving = {}

    def on_grads(group, gs):
        token, leaving[group] = _exchange_behind(gs, True, jnp.zeros((8, LANE), F32), group + "_grads")
        return token

    loss_local, dx, g_small = _local_step(x[0], positions[0], loss_target[0], wg, small,
                                          lambda group, after: arriving[group](after), on_grads)
    loss = lax.psum(loss_local, ("x", "y", "c"))

    recv = {}
    for group in ("ffn", "heads", "in"):
        recv.update(zip(groups[group], leaving[group](dx)))
    grads, deltas, new_m, new_v = {}, {}, {}, {}
    for name, parts in recv.items():
        grads[name], deltas[name], new_m[name], new_v[name] = _adamw(parts, w[name], m[name], v[name],
                                                                     name="adamw_" + name)

    def embed(t):
        return lax.dynamic_update_slice(jnp.zeros((CONV_K, CONV_DIM), F32), t, (0, me * cshard))

    parts_s = _all_gather([_pack_small(g_small)], name="gather_small_grads")[0]
    packs = [_pack_small({**{n_: d[n_] for n_, _ in _SMALL if n_ != "conv_w"}, "conv_w": embed(d["conv_w"])})
             for d in (w, m, v)]
    outs = [_unpack_small(t) for t in _adamw_small(parts_s, *packs)]
    for name, n in _SMALL:
        for dst, src in zip((grads, deltas, new_m, new_v), outs):
            if name == "conv_w":
                dst[name] = lax.dynamic_slice(src[name].reshape(CONV_K, CONV_DIM), (0, me * cshard), (CONV_K, cshard))
            else:
                dst[name] = src[name]

    def lead(d):
        return [d[name][None] for name in _WEIGHT_ORDER]

    return (loss, dx[None], *lead(grads), *lead(deltas), *lead(new_m), *lead(new_v))
```

```python
import numpy as np

import jax
import jax.numpy as jnp
from jax import lax
from jax.experimental import pallas as pl
from jax.experimental.pallas import tpu as pltpu

F32 = jnp.float32
BF16 = jnp.bfloat16
MXU_DTYPE = jnp.bfloat16
EPS = 1e-6
VMEM_LIMIT_BYTES = 48 * 1024 * 1024
K_TILE_MAX = 2048

N_DEV = 8
D_MODEL = 2048
Q_RANK = 512
KV_RANK = 512
ROPE = 64
HALF = ROPE // 2
HEADS = 8
NOPE = 128
VDIM = 128
QK = NOPE + ROPE
SSD_W = 1024
SSD_H = 16
SSD_P = 64
SSD_G = 2
SSD_E = SSD_H // SSD_G
SSD_N = 128
CHUNK = 128
CONV_K = 4
CONV_DIM = SSD_W + 2 * SSD_G * SSD_N
B_OFF = SSD_W
C_OFF = SSD_W + SSD_G * SSD_N
D_FF = 5632
D_IN = Q_RANK + KV_RANK + ROPE + SSD_W + CONV_DIM + SSD_H
ROPE_THETA = 10000.0
LANE = 128
HEAD_LANE = ROPE

ADAM_LR = 0.001
ADAM_B1 = 0.9
ADAM_B2 = 0.999
ADAM_EPS = 1e-08
ADAM_WD = 0.01
ADAM_STEP = 10


def _pick(n, cands):
    for c in cands:
        if n % c == 0:
            return c
    return n


def _params(*sem):
    return pltpu.CompilerParams(dimension_semantics=sem, vmem_limit_bytes=VMEM_LIMIT_BYTES)


def _sigmoid(x):
    return 1.0 / (1.0 + jnp.exp(-x))


def _silu(x):
    return x * _sigmoid(x)


def _dsilu(x):
    s = _sigmoid(x)
    return s * (1.0 + x * (1.0 - s))


def _softplus(x):
    e = jnp.exp(-jnp.abs(x))
    small = e * (1.0 - e * (0.5 - e * (1.0 / 3.0)))
    return jnp.maximum(x, 0.0) + jnp.where(e < 0.01, small, jnp.log(1.0 + e))


def _dot(a, b, ca, cb):
    return lax.dot_general(a, b, (((ca,), (cb,)), ((), ())), preferred_element_type=F32)


def _mx(v):
    return v.astype(MXU_DTYPE)


def _split3(a):
    hi = a.astype(BF16)
    r1 = a - hi.astype(F32)
    mid = r1.astype(BF16)
    lo = (r1 - mid.astype(F32)).astype(BF16)
    return hi, mid, lo


def _exact_dot(a, b, ca, cb, split_a):
    if split_a:
        return sum(_dot(p, b, ca, cb) for p in _split3(a))
    return sum(_dot(a, p, ca, cb) for p in _split3(b))


def _mm(a, b, mode, *, a_blk=False, b_blk=False, out_blk=False, a_cols=None, add=None, out_dtype=F32, fuse=1,
        name="mm"):
    a2, b2 = a.shape[-2:], b.shape[-2:]
    a_last = a2[1] if a_cols is None else a_cols[1]
    a_start = 0 if a_cols is None else a_cols[0]
    if mode == "nn":
        m, k, (k2, n) = a2[0], a_last, b2
    elif mode == "nt":
        m, k, (n, k2) = a2[0], a_last, b2
    else:
        k, m, (k2, n) = a2[0], a_last, b2
    assert k == k2, (a.shape, b.shape, mode)
    tm = _pick(m, (1024, 704, 512, 256, 128))
    tn = _pick(n, (1024, 768, 704, 512, 256, 192, 128))
    tk = k if k <= K_TILE_MAX else _pick(k, (K_TILE_MAX, 1024, 512))
    nk = k // tk
    jo = N_DEV if out_blk else 1
    reduce_blocks = a_blk and b_blk and not out_blk
    assert fuse == 1 or reduce_blocks
    jr = N_DEV // fuse if reduce_blocks else 1
    ca, cb = {"nn": (1, 0), "nt": (1, 1), "tn": (0, 0)}[mode]
    has_add = add is not None
    single = jr * nk == 1
    if mode == "tn":
        assert a_start % tm == 0
        a_block, a_idx = (tk, tm), (lambda i, kk: (kk, i + a_start // tm))
    else:
        assert a_start % tk == 0
        a_block, a_idx = (tm, tk), (lambda i, kk: (i, kk + a_start // tk))
    b_block, b_idx = ((tn, tk), (lambda nn_, kk: (nn_, kk))) if mode == "nt" else ((tk, tn), (lambda nn_, kk: (kk, nn_)))

    def blk_specs(blocked, block, idx, of_a, t):
        def pos(o, i, nn_, kk):
            return idx(i, kk) if of_a else idx(nn_, kk)
        if blocked:
            return pl.BlockSpec((None,) + block,
                                lambda o, i, nn_, r, kk: ((o if out_blk else r * fuse + t),) + pos(o, i, nn_, kk))
        return pl.BlockSpec(block, lambda o, i, nn_, r, kk: pos(o, i, nn_, kk))

    a_specs = [blk_specs(a_blk, a_block, a_idx, True, t) for t in range(fuse)]
    b_specs = [blk_specs(b_blk, b_block, b_idx, False, t) for t in range(fuse)]
    o_spec = (pl.BlockSpec((None, tm, tn), lambda o, i, nn_, r, kk: (o, i, nn_)) if out_blk
              else pl.BlockSpec((tm, tn), lambda o, i, nn_, r, kk: (i, nn_)))

    def body(*refs):
        a_refs, b_refs = refs[:fuse], refs[fuse:2 * fuse]
        add_ref = refs[2 * fuse] if has_add else None
        o_ref = refs[2 * fuse + 1] if has_add else refs[2 * fuse]
        part = _dot(_mx(a_refs[0][...]), _mx(b_refs[0][...]), ca, cb)
        for t in range(1, fuse):
            part = part + _dot(_mx(a_refs[t][...]), _mx(b_refs[t][...]), ca, cb)
        if single:
            if has_add:
                part = part + add_ref[...]
            o_ref[...] = part.astype(o_ref.dtype)
            return
        acc = refs[-1]
        r, kk = pl.program_id(3), pl.program_id(4)
        first = jnp.logical_and(r == 0, kk == 0)
        last = jnp.logical_and(r == jr - 1, kk == nk - 1)

        @pl.when(first)
        def _():
            acc[...] = part

        @pl.when(jnp.logical_not(first))
        def _():
            acc[...] += part

        @pl.when(last)
        def _():
            res = acc[...]
            if has_add:
                res = res + add_ref[...]
            o_ref[...] = res.astype(o_ref.dtype)

    out_shape = ((N_DEV, m, n) if out_blk else (m, n))
    return pl.pallas_call(
        body, name=name, grid=(jo, m // tm, n // tn, jr, nk),
        in_specs=a_specs + b_specs + ([o_spec] if has_add else []), out_specs=o_spec,
        out_shape=jax.ShapeDtypeStruct(out_shape, out_dtype),
        scratch_shapes=[] if single else [pltpu.VMEM((tm, tn), F32)],
        compiler_params=_params("parallel", "parallel", "parallel", "arbitrary", "arbitrary"),
    )(*((a,) * fuse + (b,) * fuse + ((add,) if has_add else ())))


def _row_tile(r_):
    return _pick(r_, (256, 128, 64, 32, 16, 8))


def _rms_fwd(t, w, groups=1, res=None, out_dtype=F32, name="rms_fwd"):
    r_, f = t.shape
    fg = f // groups
    tr = _row_tile(r_)
    has_res = res is not None

    def body(*refs):
        t_ref, w_ref = refs[0], refs[1]
        res_ref = refs[2] if has_res else None
        o_ref = refs[-1]
        for g in range(groups):
            sl = slice(g * fg, (g + 1) * fg)
            tv = t_ref[:, sl].astype(F32)
            r = lax.rsqrt(jnp.mean(tv * tv, axis=-1, keepdims=True) + EPS)
            y = tv * r * w_ref[:, sl]
            if has_res:
                y = y + res_ref[:, sl]
            o_ref[:, sl] = y.astype(o_ref.dtype)

    row = pl.BlockSpec((tr, f), lambda i: (i, 0))
    wsp = pl.BlockSpec((1, f), lambda i: (0, 0))
    return pl.pallas_call(
        body, name=name, grid=(r_ // tr,),
        in_specs=[row, wsp] + ([row] if has_res else []), out_specs=row,
        out_shape=jax.ShapeDtypeStruct((r_, f), out_dtype),
        compiler_params=_params("parallel"),
    )(*((t, w.reshape(1, f)) + ((res,) if has_res else ())))


def _rms_bwd(t, w, dys, res=None, out_dtype=F32, name="rms_bwd"):
    r_, f = t.shape
    groups = len(dys)
    fg = f // groups
    tr = _row_tile(r_)
    has_res = res is not None

    def body(*refs):
        t_ref, w_ref = refs[0], refs[1]
        dy_refs = refs[2:2 + groups]
        res_ref = refs[2 + groups] if has_res else None
        dt_ref, dw_ref = refs[-2], refs[-1]

        @pl.when(pl.program_id(0) == 0)
        def _():
            dw_ref[...] = jnp.zeros_like(dw_ref)

        for g in range(groups):
            sl = slice(g * fg, (g + 1) * fg)
            tv = t_ref[:, sl].astype(F32)
            dyv = dy_refs[g][...].astype(F32)
            r = lax.rsqrt(jnp.mean(tv * tv, axis=-1, keepdims=True) + EPS)
            gw = dyv * w_ref[:, sl]
            c = jnp.mean(gw * tv, axis=-1, keepdims=True)
            dt = r * gw - tv * (r * r * r * c)
            if has_res:
                dt = dt + res_ref[:, sl]
            dt_ref[:, sl] = dt.astype(dt_ref.dtype)
            dw_ref[:, sl] += jnp.sum(dyv * tv * r, axis=0, keepdims=True)

    row = pl.BlockSpec((tr, f), lambda i: (i, 0))
    grow = pl.BlockSpec((tr, fg), lambda i: (i, 0))
    wsp = pl.BlockSpec((1, f), lambda i: (0, 0))
    return pl.pallas_call(
        body, name=name, grid=(r_ // tr,),
        in_specs=[row, wsp] + [grow] * groups + ([row] if has_res else []), out_specs=[row, wsp],
        out_shape=[jax.ShapeDtypeStruct((r_, f), out_dtype), jax.ShapeDtypeStruct((1, f), F32)],
        compiler_params=_params("arbitrary"),
    )(*((t, w.reshape(1, f)) + tuple(dys) + ((res,) if has_res else ())))


def _hnorm_fwd(o, w, name="attn_out_norm"):
    h, s_, v = o.shape
    tr = _row_tile(s_)

    def body(o_ref, w_ref, y_ref):
        ss = jnp.sum(o_ref[0] * o_ref[0], axis=-1, keepdims=True)
        for i in range(1, h):
            ss = ss + jnp.sum(o_ref[i] * o_ref[i], axis=-1, keepdims=True)
        r = lax.rsqrt(ss * (1.0 / (h * v)) + EPS)
        for i in range(h):
            y_ref[i] = (o_ref[i] * r * w_ref[i]).astype(y_ref.dtype)

    blk = pl.BlockSpec((h, tr, v), lambda i: (0, i, 0))
    wsp = pl.BlockSpec((h, 1, v), lambda i: (0, 0, 0))
    return pl.pallas_call(
        body, name=name, grid=(s_ // tr,), in_specs=[blk, wsp], out_specs=blk,
        out_shape=jax.ShapeDtypeStruct(o.shape, MXU_DTYPE), compiler_params=_params("parallel"),
    )(o, w)


def _hnorm_bwd(o, w, dy, name="attn_out_norm_bwd"):
    h, s_, v = o.shape
    tr = _row_tile(s_)

    def body(o_ref, w_ref, dy_ref, do_ref, delta_ref, dw_ref):
        @pl.when(pl.program_id(0) == 0)
        def _():
            dw_ref[...] = jnp.zeros_like(dw_ref)

        ss = jnp.zeros((tr, 1), F32)
        cc = jnp.zeros((tr, 1), F32)
        for i in range(h):
            ov = o_ref[i]
            ss = ss + jnp.sum(ov * ov, axis=-1, keepdims=True)
            cc = cc + jnp.sum(dy_ref[i] * w_ref[i] * ov, axis=-1, keepdims=True)
        r = lax.rsqrt(ss * (1.0 / (h * v)) + EPS)
        c = cc * (1.0 / (h * v))
        for i in range(h):
            ov = o_ref[i]
            dyv = dy_ref[i]
            dov = r * dyv * w_ref[i] - ov * (r * r * r * c)
            do_ref[i] = dov.astype(do_ref.dtype)
            delta_ref[i] = jnp.sum(dov * ov, axis=-1, keepdims=True)
            dw_ref[i] += jnp.sum(dyv * ov * r, axis=0, keepdims=True)

    blk = pl.BlockSpec((h, tr, v), lambda i: (0, i, 0))
    wsp = pl.BlockSpec((h, 1, v), lambda i: (0, 0, 0))
    return pl.pallas_call(
        body, name=name, grid=(s_ // tr,), in_specs=[blk, wsp, blk],
        out_specs=[blk, pl.BlockSpec((h, tr, 1), lambda i: (0, i, 0)), wsp],
        out_shape=[jax.ShapeDtypeStruct(o.shape, MXU_DTYPE), jax.ShapeDtypeStruct((h, s_, 1), F32),
                   jax.ShapeDtypeStruct((h, 1, v), F32)],
        compiler_params=_params("arbitrary"),
    )(o, w, dy)


def _loss_head(ffn, h1, target, w, name="loss_head"):
    r_, f = ffn.shape
    tr = _row_tile(r_)

    def body(ffn_ref, h1_ref, tg_ref, w_ref, loss_ref, dy_ref, dffn_ref, dw_ref):
        @pl.when(pl.program_id(0) == 0)
        def _():
            dw_ref[...] = jnp.zeros_like(dw_ref)
            loss_ref[...] = jnp.zeros_like(loss_ref)

        tv = ffn_ref[...]
        wv = w_ref[...]
        r = lax.rsqrt(jnp.mean(tv * tv, axis=-1, keepdims=True) + EPS)
        tn = tv * r
        e = h1_ref[...] + tn * wv - tg_ref[...]
        tot = jnp.sum(jnp.sum(e * e, axis=1, keepdims=True), axis=0, keepdims=True) * (0.5 / f)
        loss_ref[...] += tot + jnp.zeros_like(loss_ref)
        dyv = e * (1.0 / f)
        dy_ref[...] = dyv
        gw = dyv * wv
        c = jnp.mean(gw * tv, axis=-1, keepdims=True)
        dffn_ref[...] = (r * gw - tv * (r * r * r * c)).astype(dffn_ref.dtype)
        dw_ref[...] += jnp.sum(dyv * tn, axis=0, keepdims=True)

    row = pl.BlockSpec((tr, f), lambda i: (i, 0))
    wsp = pl.BlockSpec((1, f), lambda i: (0, 0))
    lsp = pl.BlockSpec((1, LANE), lambda i: (0, 0))
    return pl.pallas_call(
        body, name=name, grid=(r_ // tr,),
        in_specs=[row, row, row, wsp], out_specs=[lsp, row, row, wsp],
        out_shape=[jax.ShapeDtypeStruct((1, LANE), F32), jax.ShapeDtypeStruct((r_, f), F32),
                   jax.ShapeDtypeStruct((r_, f), MXU_DTYPE), jax.ShapeDtypeStruct((1, f), F32)],
        compiler_params=_params("arbitrary"),
    )(ffn, h1, target, w.reshape(1, f))


def _rot_matrix():
    p = np.zeros((ROPE, ROPE), np.float32)
    for i in range(HALF):
        p[i + HALF, i] = -1.0
        p[i, i + HALF] = 1.0
    return jnp.asarray(p, BF16)


def _rope_val(r, c2, s2, rot):
    return r * c2 + _exact_dot(r, rot, 1, 0, True) * s2


def _q_prep(q, cos2, sin2, scale, name):
    h, s_, _ = q.shape
    tr = _pick(s_, (1024, 512, 256, 128, 64, 32, 16, 8))

    def body(q_ref, c_ref, s_ref, rot_ref, o_ref):
        x = q_ref[...]
        o_ref[:, :NOPE] = (x[:, :NOPE] * scale).astype(o_ref.dtype)
        o_ref[:, NOPE:] = (_rope_val(x[:, NOPE:], c_ref[...], s_ref[...], rot_ref[...]) * scale).astype(o_ref.dtype)

    blk = pl.BlockSpec((None, tr, QK), lambda hh, i: (hh, i, 0))
    csp = pl.BlockSpec((tr, ROPE), lambda hh, i: (i, 0))
    return pl.pallas_call(
        body, name=name, grid=(h, s_ // tr),
        in_specs=[blk, csp, csp, pl.BlockSpec((ROPE, ROPE), lambda hh, i: (0, 0))], out_specs=blk,
        out_shape=jax.ShapeDtypeStruct(q.shape, MXU_DTYPE), compiler_params=_params("parallel", "parallel"),
    )(q, cos2, sin2, _rot_matrix())


def _kv_prep(kv, small, cos2, sin2, name="kv_prep"):
    h, s_, _ = kv.shape
    tr = _row_tile(s_)

    def body(kv_ref, sm_ref, c_ref, s_ref, rot_ref, k_ref, v_ref):
        kr = _rope_val(sm_ref[:, :ROPE], c_ref[...], s_ref[...], rot_ref[...]).astype(k_ref.dtype)
        for i in range(h):
            k_ref[i, :, :NOPE] = kv_ref[i, :, :NOPE].astype(k_ref.dtype)
            k_ref[i, :, NOPE:] = kr
            v_ref[i] = kv_ref[i, :, NOPE:].astype(v_ref.dtype)

    csp = pl.BlockSpec((tr, ROPE), lambda i: (i, 0))
    return pl.pallas_call(
        body, name=name, grid=(s_ // tr,),
        in_specs=[pl.BlockSpec((h, tr, NOPE + VDIM), lambda i: (0, i, 0)), pl.BlockSpec((tr, LANE), lambda i: (i, 0)),
                  csp, csp, pl.BlockSpec((ROPE, ROPE), lambda i: (0, 0))],
        out_specs=[pl.BlockSpec((h, tr, QK), lambda i: (0, i, 0)), pl.BlockSpec((h, tr, VDIM), lambda i: (0, i, 0))],
        out_shape=[jax.ShapeDtypeStruct((h, s_, QK), MXU_DTYPE), jax.ShapeDtypeStruct((h, s_, VDIM), MXU_DTYPE)],
        compiler_params=_params("parallel"),
    )(kv, small, cos2, sin2, _rot_matrix())


def _dkv_post(dk, dv, ddt, cos2, nsin2, name="dkv_post"):
    h, s_, _ = dk.shape
    tr = _row_tile(s_)

    def body(dk_ref, dv_ref, ddt_ref, c_ref, s_ref, rot_ref, dkv_ref, dsm_ref):
        acc = dk_ref[0, :, NOPE:]
        for i in range(1, h):
            acc = acc + dk_ref[i, :, NOPE:]
        dsm_ref[:, :ROPE] = _rope_val(acc, c_ref[...], s_ref[...], rot_ref[...]).astype(dsm_ref.dtype)
        dsm_ref[:, ROPE:] = ddt_ref[:, ROPE:].astype(dsm_ref.dtype)
        for i in range(h):
            dkv_ref[i, :, :NOPE] = dk_ref[i, :, :NOPE].astype(dkv_ref.dtype)
            dkv_ref[i, :, NOPE:] = dv_ref[i].astype(dkv_ref.dtype)

    csp = pl.BlockSpec((tr, ROPE), lambda i: (i, 0))
    return pl.pallas_call(
        body, name=name, grid=(s_ // tr,),
        in_specs=[pl.BlockSpec((h, tr, QK), lambda i: (0, i, 0)), pl.BlockSpec((h, tr, VDIM), lambda i: (0, i, 0)),
                  pl.BlockSpec((tr, LANE), lambda i: (i, 0)), csp, csp, pl.BlockSpec((ROPE, ROPE), lambda i: (0, 0))],
        out_specs=[pl.BlockSpec((h, tr, NOPE + VDIM), lambda i: (0, i, 0)), pl.BlockSpec((tr, LANE), lambda i: (i, 0))],
        out_shape=[jax.ShapeDtypeStruct((h, s_, NOPE + VDIM), MXU_DTYPE), jax.ShapeDtypeStruct((s_, LANE), MXU_DTYPE)],
        compiler_params=_params("parallel"),
    )(dk, dv, ddt, cos2, nsin2, _rot_matrix())


def _attn_tile(s):
    return 512 if s % 1024 == 0 else s // 2


def _pairs(n, by_key):
    if by_key:
        pr = [(i, j) for j in range(n) for i in range(j, n)]
    else:
        pr = [(i, j) for i in range(n) for j in range(i + 1)]
    return (jnp.asarray([p[0] for p in pr], jnp.int32), jnp.asarray([p[1] for p in pr], jnp.int32))


ATTN_ROW_GROUPS = 2


def _row_groups(t, diag):
    tg = t // ATTN_ROW_GROUPS
    out = []
    for r in range(ATTN_ROW_GROUPS):
        nc = (r + 1) * tg if diag else t
        mask = None
        if diag:
            mask = (lax.broadcasted_iota(jnp.int32, (tg, nc), 1)
                    <= lax.broadcasted_iota(jnp.int32, (tg, nc), 0) + r * tg)
        out.append((slice(r * tg, (r + 1) * tg), nc, mask))
    return out


def _flash_specs(t, dk, dv):
    qsp = pl.BlockSpec((None, t, dk), lambda hh, p, qi, kj: (hh, qi[p], 0))
    ksp = pl.BlockSpec((None, t, dk), lambda hh, p, qi, kj: (hh, kj[p], 0))
    vsp = pl.BlockSpec((None, t, dv), lambda hh, p, qi, kj: (hh, kj[p], 0))
    osp = pl.BlockSpec((None, t, dv), lambda hh, p, qi, kj: (hh, qi[p], 0))
    lsp = pl.BlockSpec((None, t, 1), lambda hh, p, qi, kj: (hh, qi[p], 0))
    return qsp, ksp, vsp, osp, lsp


def _flash_fwd(q, k, v, name="flash_fwd"):
    h, s_, dk = q.shape
    dv = v.shape[-1]
    t = _attn_tile(s_)
    n = s_ // t
    qi, kj = _pairs(n, False)

    def body(qi_ref, kj_ref, q_ref, k_ref, v_ref, o_ref, lse_ref, m_s, l_s, acc):
        p_ = pl.program_id(1)
        i, j = qi_ref[p_], kj_ref[p_]

        @pl.when(j == 0)
        def _():
            m_s[...] = jnp.full_like(m_s, -jnp.inf)
            l_s[...] = jnp.zeros_like(l_s)
            acc[...] = jnp.zeros_like(acc)

        def update(diag):
            for rs, nc, mask in _row_groups(t, diag):
                sc = _dot(q_ref[rs, :], k_ref[0:nc, :], 1, 1)
                if mask is not None:
                    sc = jnp.where(mask, sc, -jnp.inf)
                m_old = m_s[rs, :]
                m_new = jnp.maximum(m_old, jnp.max(sc, axis=1, keepdims=True))
                alpha = jnp.exp(m_old - m_new)
                p = jnp.exp(sc - m_new)
                l_s[rs, :] = alpha * l_s[rs, :] + jnp.sum(p, axis=1, keepdims=True)
                acc[rs, :] = alpha * acc[rs, :] + _dot(_mx(p), v_ref[0:nc, :], 1, 0)
                m_s[rs, :] = m_new

        @pl.when(j < i)
        def _():
            update(False)

        @pl.when(j == i)
        def _():
            update(True)
            o_ref[...] = acc[...] / l_s[...]
            lse_ref[...] = m_s[...] + jnp.log(l_s[...])

    qsp, ksp, vsp, osp, lsp = _flash_specs(t, dk, dv)
    gs = pltpu.PrefetchScalarGridSpec(
        num_scalar_prefetch=2, grid=(h, qi.shape[0]), in_specs=[qsp, ksp, vsp], out_specs=[osp, lsp],
        scratch_shapes=[pltpu.VMEM((t, 1), F32), pltpu.VMEM((t, 1), F32), pltpu.VMEM((t, dv), F32)])
    return pl.pallas_call(
        body, name=name, grid_spec=gs,
        out_shape=[jax.ShapeDtypeStruct((h, s_, dv), F32), jax.ShapeDtypeStruct((h, s_, 1), F32)],
        compiler_params=_params("parallel", "arbitrary"),
    )(qi, kj, q, k, v)


def _flash_bwd(q, k, v, do, lse, delta, name="flash_bwd"):
    h, s_, dk = q.shape
    dv = v.shape[-1]
    t = _attn_tile(s_)
    tg = t // ATTN_ROW_GROUPS
    n = s_ // t
    qi, kj = _pairs(n, True)

    def body(qi_ref, kj_ref, q_ref, k_ref, v_ref, do_ref, lse_ref, delta_ref, dq_ref, dk_ref, dv_ref, dk_acc, dv_acc):
        p_ = pl.program_id(1)
        i, j = qi_ref[p_], kj_ref[p_]

        @pl.when(p_ == 0)
        def _():
            dq_ref[...] = jnp.zeros_like(dq_ref)

        def update(diag):
            for g, (rs, nc, mask) in enumerate(_row_groups(t, diag)):
                sc = _dot(q_ref[rs, :], k_ref[0:nc, :], 1, 1)
                if mask is not None:
                    sc = jnp.where(mask, sc, -jnp.inf)
                p = jnp.exp(sc - lse_ref[rs, :])
                dob = _mx(do_ref[rs, :])
                dv_acc[0:nc, :] += _dot(_mx(p), dob, 0, 0)
                dp = _dot(dob, v_ref[0:nc, :], 1, 1)
                dsb = _mx(p * (dp - delta_ref[rs, :]))
                dk_acc[0:nc, :] += _dot(dsb, q_ref[rs, :], 0, 0)
                rows = pl.ds(pl.multiple_of(i * t + g * tg, tg), tg)
                dq_ref[rows, :] += _dot(dsb, k_ref[0:nc, :], 1, 0)

        @pl.when(i == j)
        def _():
            dk_acc[...] = jnp.zeros_like(dk_acc)
            dv_acc[...] = jnp.zeros_like(dv_acc)
            update(True)

        @pl.when(i > j)
        def _():
            update(False)

        @pl.when(i == n - 1)
        def _():
            dk_ref[...] = dk_acc[...]
            dv_ref[...] = dv_acc[...]

    qsp, ksp, vsp, osp, lsp = _flash_specs(t, dk, dv)
    dqsp = pl.BlockSpec((None, s_, dk), lambda hh, p, qi, kj: (hh, 0, 0))
    gs = pltpu.PrefetchScalarGridSpec(
        num_scalar_prefetch=2, grid=(h, qi.shape[0]), in_specs=[qsp, ksp, vsp, osp, lsp, lsp],
        out_specs=[dqsp, ksp, vsp],
        scratch_shapes=[pltpu.VMEM((t, dk), F32), pltpu.VMEM((t, dv), F32)])
    return pl.pallas_call(
        body, name=name, grid_spec=gs,
        out_shape=[jax.ShapeDtypeStruct((h, s_, dk), F32), jax.ShapeDtypeStruct((h, s_, dk), F32),
                   jax.ShapeDtypeStruct((h, s_, dv), F32)],
        compiler_params=_params("parallel", "arbitrary"),
    )(qi, kj, q, k, v, do, lse, delta)


HALO = 8


def _conv_specs(s_, c, tr, after):
    main = pl.BlockSpec((tr, c), lambda i: (i, 0))
    per = tr // HALO
    if after:
        halo = pl.BlockSpec((HALO, c), lambda i: (jnp.minimum((i + 1) * per, s_ // HALO - 1), 0))
    else:
        halo = pl.BlockSpec((HALO, c), lambda i: (jnp.maximum(i * per - 1, 0), 0))
    return main, halo


def _fill_before(ext, t_ref, h_ref, tr):
    ext[0:HALO, :] = jnp.where(pl.program_id(0) > 0, h_ref[...], 0.0)
    ext[HALO:HALO + tr, :] = t_ref[...]


def _taps(ext, w_ref, tr):
    base = HALO - (CONV_K - 1)
    acc = ext[base:base + tr, :] * w_ref[0:1, :]
    for k in range(1, CONV_K):
        acc = acc + ext[base + k:base + k + tr, :] * w_ref[k:k + 1, :]
    return acc


def _conv_fwd(t, w, b, name="conv_fwd"):
    s_, c = t.shape
    tr = _row_tile(s_)

    def body(t_ref, h_ref, w_ref, b_ref, o_ref, ext):
        _fill_before(ext, t_ref, h_ref, tr)
        o_ref[...] = _silu(_taps(ext, w_ref, tr) + b_ref[...])

    main, halo = _conv_specs(s_, c, tr, False)
    return pl.pallas_call(
        body, name=name, grid=(s_ // tr,),
        in_specs=[main, halo, pl.BlockSpec((CONV_K, c), lambda i: (0, 0)), pl.BlockSpec((1, c), lambda i: (0, 0))],
        out_specs=main, out_shape=jax.ShapeDtypeStruct((s_, c), F32),
        scratch_shapes=[pltpu.VMEM((tr + HALO, c), F32)], compiler_params=_params("parallel"),
    )(t, t, w, b)


def _conv_bwd_pre(t, w, b, dact, name="conv_bwd_pre"):
    s_, c = t.shape
    tr = _row_tile(s_)

    def body(t_ref, h_ref, w_ref, b_ref, da_ref, dpre_ref, dwb_ref, ext):
        @pl.when(pl.program_id(0) == 0)
        def _():
            dwb_ref[...] = jnp.zeros_like(dwb_ref)

        _fill_before(ext, t_ref, h_ref, tr)
        dpre = da_ref[...] * _dsilu(_taps(ext, w_ref, tr) + b_ref[...])
        dpre_ref[...] = dpre
        base = HALO - (CONV_K - 1)
        for k in range(CONV_K):
            dwb_ref[k:k + 1, :] += jnp.sum(dpre * ext[base + k:base + k + tr, :], axis=0, keepdims=True)
        dwb_ref[CONV_K:CONV_K + 1, :] += jnp.sum(dpre, axis=0, keepdims=True)

    main, halo = _conv_specs(s_, c, tr, False)
    return pl.pallas_call(
        body, name=name, grid=(s_ // tr,),
        in_specs=[main, halo, pl.BlockSpec((CONV_K, c), lambda i: (0, 0)), pl.BlockSpec((1, c), lambda i: (0, 0)), main],
        out_specs=[main, pl.BlockSpec((8, c), lambda i: (0, 0))],
        out_shape=[jax.ShapeDtypeStruct((s_, c), F32), jax.ShapeDtypeStruct((8, c), F32)],
        scratch_shapes=[pltpu.VMEM((tr + HALO, c), F32)], compiler_params=_params("arbitrary"),
    )(t, t, w, b, dact)


def _conv_bwd_in(dpre, w, name="conv_bwd_in"):
    s_, c = dpre.shape
    tr = _row_tile(s_)
    nt = s_ // tr

    def body(d_ref, h_ref, w_ref, o_ref, ext):
        ext[0:tr, :] = d_ref[...]
        ext[tr:tr + HALO, :] = jnp.where(pl.program_id(0) < nt - 1, h_ref[...], 0.0)
        acc = ext[CONV_K - 1:CONV_K - 1 + tr, :] * w_ref[0:1, :]
        for k in range(1, CONV_K):
            acc = acc + ext[CONV_K - 1 - k:CONV_K - 1 - k + tr, :] * w_ref[k:k + 1, :]
        o_ref[...] = acc.astype(o_ref.dtype)

    main, halo = _conv_specs(s_, c, tr, True)
    return pl.pallas_call(
        body, name=name, grid=(nt,),
        in_specs=[main, halo, pl.BlockSpec((CONV_K, c), lambda i: (0, 0))],
        out_specs=main, out_shape=jax.ShapeDtypeStruct((s_, c), MXU_DTYPE),
        scratch_shapes=[pltpu.VMEM((tr + HALO, c), F32)], compiler_params=_params("parallel"),
    )(dpre, dpre, w)


def _ssd_chunk_common(dt_ref, dtt_ref, br_ref, bc_ref, ar_ref, ac_ref):
    li = lax.broadcasted_iota(jnp.int32, (CHUNK, CHUNK), 0)
    si = lax.broadcasted_iota(jnp.int32, (CHUNK, CHUNK), 1)
    lower = li >= si
    lower_b = lower.astype(BF16)
    upper_b = (li <= si).astype(BF16)
    zr = dt_ref[...] + br_ref[...]
    dtc = _softplus(zr)
    a_row = -jnp.exp(ar_ref[...])
    acum = _exact_dot(lower_b, dtc * a_row, 1, 0, False)
    dtt = _softplus(dtt_ref[...] + bc_ref[...])
    acum_t = _exact_dot(dtt * (-jnp.exp(ac_ref[...])), upper_b, 1, 0, True)
    return lower, upper_b, zr, dtc, a_row, acum, acum_t


def _head_terms(h, lower, dtc, acum, acum_t):
    lane = lax.broadcasted_iota(jnp.int32, (1, LANE), 1)
    sub = lax.broadcasted_iota(jnp.int32, (SSD_H, 1), 0)
    rowid = lax.broadcasted_iota(jnp.int32, (CHUNK, 1), 0)
    oh = (lane == HEAD_LANE + h).astype(F32)
    acol = jnp.sum(acum * oh, axis=1, keepdims=True)
    dcol = jnp.sum(dtc * oh, axis=1, keepdims=True)
    arow = jnp.sum(acum_t * (sub == h).astype(F32), axis=0, keepdims=True)
    alast = jnp.sum(jnp.where(rowid == CHUNK - 1, acol, 0.0), axis=0, keepdims=True)
    decay = jnp.exp(jnp.where(lower, acol - arow, -jnp.inf))
    return oh, acol, dcol, alast, decay


SSD_PAIRS = SSD_H // 2
PAIRS_PER_GROUP = SSD_E // 2


def _ps(q):
    return slice(q * LANE, (q + 1) * LANE)


def _gs(off, g):
    return slice(off + g * SSD_N, off + (g + 1) * SSD_N)


def _lanes(c0, c1):
    return jnp.where(lax.broadcasted_iota(jnp.int32, (1, LANE), 1) < SSD_P, c0, c1)


def _rows(c0, c1):
    return jnp.where(lax.broadcasted_iota(jnp.int32, (LANE, 1), 0) < SSD_P, c0, c1)


def _lane_halves(t):
    first = lax.broadcasted_iota(jnp.int32, (1, LANE), 1) < SSD_P
    return (jnp.sum(jnp.where(first, t, 0.0), axis=1, keepdims=True),
            jnp.sum(jnp.where(first, 0.0, t), axis=1, keepdims=True))


def _ssd_in_specs(rev):
    def ci(c):
        return c if rev is None else rev - c
    return [pl.BlockSpec((CHUNK, CONV_DIM), lambda c: (ci(c), 0)),
            pl.BlockSpec((CHUNK, LANE), lambda c: (ci(c), 0)),
            pl.BlockSpec((SSD_H, CHUNK), lambda c: (0, ci(c))),
            pl.BlockSpec((1, LANE), lambda c: (0, 0)), pl.BlockSpec((SSD_H, 1), lambda c: (0, 0)),
            pl.BlockSpec((1, LANE), lambda c: (0, 0)), pl.BlockSpec((SSD_H, 1), lambda c: (0, 0)),
            pl.BlockSpec((SSD_PAIRS, 1, LANE), lambda c: (0, 0, 0))]


def _ssd_fwd(xbc, small, dtt, bias_r, bias_c, alog_r, alog_c, dsk, name="ssd_fwd"):
    s_ = xbc.shape[0]
    nc = s_ // CHUNK

    def body(x_ref, dt_ref, dtt_ref, br_ref, bc_ref, ar_ref, ac_ref, dsk_ref, y_ref, prev_ref, state):
        @pl.when(pl.program_id(0) == 0)
        def _():
            state[...] = jnp.zeros_like(state)

        lower, _, _, dtc, _, acum, acum_t = _ssd_chunk_common(dt_ref, dtt_ref, br_ref, bc_ref, ar_ref, ac_ref)
        for g in range(SSD_G):
            bb = _mx(x_ref[:, _gs(B_OFF, g)])
            cb_ = _mx(x_ref[:, _gs(C_OFF, g)])
            cbm = _dot(cb_, bb, 1, 1)
            for e in range(PAIRS_PER_GROUP):
                q = g * PAIRS_PER_GROUP + e
                _, acol0, dcol0, alast0, decay0 = _head_terms(2 * q, lower, dtc, acum, acum_t)
                _, acol1, dcol1, alast1, decay1 = _head_terms(2 * q + 1, lower, dtc, acum, acum_t)
                x = x_ref[:, _ps(q)]
                xdt = x * _lanes(dcol0, dcol1)
                xb = _mx(xdt)
                yd = _lanes(_dot(_mx(cbm * decay0), xb, 1, 0), _dot(_mx(cbm * decay1), xb, 1, 0))
                prev = state[q]
                prev_ref[0, q] = prev
                yo = _dot(cb_, _mx(prev), 1, 1) * _lanes(jnp.exp(acol0), jnp.exp(acol1))
                ds = _lanes(jnp.exp(alast0 - acol0), jnp.exp(alast1 - acol1))
                st = _dot(_mx(xdt * ds), bb, 0, 0)
                state[q] = prev * _rows(jnp.exp(alast0), jnp.exp(alast1)) + st
                y_ref[:, _ps(q)] = yd + yo + x * dsk_ref[q]

    psp = pl.BlockSpec((1, SSD_PAIRS, LANE, SSD_N), lambda c: (c, 0, 0, 0))
    return pl.pallas_call(
        body, name=name, grid=(nc,),
        in_specs=_ssd_in_specs(None), out_specs=[pl.BlockSpec((CHUNK, SSD_W), lambda c: (c, 0)), psp],
        out_shape=[jax.ShapeDtypeStruct((s_, SSD_W), F32),
                   jax.ShapeDtypeStruct((nc, SSD_PAIRS, LANE, SSD_N), F32)],
        scratch_shapes=[pltpu.VMEM((SSD_PAIRS, LANE, SSD_N), F32)],
        compiler_params=_params("arbitrary"),
    )(xbc, small, dtt, bias_r, bias_c, alog_r, alog_c, dsk)


def _ssd_bwd(xbc, small, dtt, bias_r, bias_c, alog_r, alog_c, dsk, prev, dy, name="ssd_bwd"):
    s_ = xbc.shape[0]
    nc = s_ // CHUNK

    def body(x_ref, dt_ref, dtt_ref, br_ref, bc_ref, ar_ref, ac_ref, dsk_ref, prev_ref, dy_ref,
             dx_ref, ddt_ref, dpar_ref, dstate):
        @pl.when(pl.program_id(0) == 0)
        def _():
            dstate[...] = jnp.zeros_like(dstate)
            dpar_ref[...] = jnp.zeros_like(dpar_ref)

        lower, upper_b, zr, dtc, a_row, acum, acum_t = _ssd_chunk_common(
            dt_ref, dtt_ref, br_ref, bc_ref, ar_ref, ac_ref)
        strict = (lax.broadcasted_iota(jnp.int32, (CHUNK, CHUNK), 1)
                  < lax.broadcasted_iota(jnp.int32, (CHUNK, CHUNK), 0))
        strict_b = strict.astype(BF16)
        col2 = lax.broadcasted_iota(jnp.int32, (CHUNK, 2 * CHUNK), 1)
        strict2 = (jnp.where(col2 >= CHUNK, col2 - CHUNK, col2)
                   < lax.broadcasted_iota(jnp.int32, (CHUNK, 2 * CHUNK), 0))
        da_in = jnp.zeros((CHUNK, LANE), F32)
        r_off = jnp.zeros((CHUNK, LANE), F32)
        c_int = jnp.zeros((CHUNK, LANE), F32)
        c_row = jnp.zeros((1, LANE), F32)
        ddt = jnp.zeros((CHUNK, LANE), F32)
        dskip = jnp.zeros((1, LANE), F32)
        for g in range(SSD_G):
            bb = _mx(x_ref[:, _gs(B_OFF, g)])
            cb_ = _mx(x_ref[:, _gs(C_OFF, g)])
            cbm = _dot(cb_, bb, 1, 1)
            dcb = jnp.zeros((CHUNK, CHUNK), F32)
            dc_acc = jnp.zeros((CHUNK, SSD_N), F32)
            db_acc = jnp.zeros((CHUNK, SSD_N), F32)
            for e in range(PAIRS_PER_GROUP):
                q = g * PAIRS_PER_GROUP + e
                oh0, acol0, dcol0, alast0, decay0 = _head_terms(2 * q, lower, dtc, acum, acum_t)
                oh1, acol1, dcol1, alast1, decay1 = _head_terms(2 * q + 1, lower, dtc, acum, acum_t)
                x = x_ref[:, _ps(q)]
                dy = dy_ref[:, _ps(q)]
                dcol = _lanes(dcol0, dcol1)
                xdt = x * dcol
                xb = _mx(xdt)
                eacol = _lanes(jnp.exp(acol0), jnp.exp(acol1))
                ds = _lanes(jnp.exp(alast0 - acol0), jnp.exp(alast1 - acol1))
                ealast = _rows(jnp.exp(alast0), jnp.exp(alast1))
                dyb = _mx(dy)
                dyb0, dyb1 = _mx(_lanes(dy, 0.0)), _mx(_lanes(0.0, dy))
                dsh = dstate[q]
                dshb = _mx(dsh)
                prev = prev_ref[0, q]
                prevb = _mx(prev)
                dxdt_inter = ds * _dot(bb, dshb, 1, 1)
                dxdt = _lanes(_dot(_mx(cbm * decay0), dyb, 0, 0), _dot(_mx(cbm * decay1), dyb, 0, 0)) + dxdt_inter
                dwl0 = _dot(dyb0, xb, 1, 1) * decay0
                dwl1 = _dot(dyb1, xb, 1, 1) * decay1
                dcb = dcb + dwl0 + dwl1
                dyeb = _mx(dy * eacol)
                dc_acc = dc_acc + _dot(dyeb, prevb, 1, 0)
                db_acc = db_acc + _dot(_mx(xdt * ds), dshb, 1, 0)
                dstate[q] = _dot(dyeb, cb_, 0, 0) + ealast * dsh
                above = _exact_dot(upper_b, jnp.concatenate([dwl0 * cbm, dwl1 * cbm], axis=1), 1, 0, False)
                above = jnp.where(strict2, above, 0.0)
                da_in = (da_in + jnp.sum(above[:, :CHUNK], axis=1, keepdims=True) * oh0
                         + jnp.sum(above[:, CHUNK:], axis=1, keepdims=True) * oh1)
                y_off = _dot(cb_, prevb, 1, 1) * eacol
                r0, r1 = _lane_halves(dy * y_off)
                r_off = r_off + r0 * oh0 + r1 * oh1
                c0, c1 = _lane_halves(xdt * dxdt_inter)
                c_int = c_int + c0 * oh0 + c1 * oh1
                both = jnp.sum(dsh * prev, axis=1, keepdims=True) * ealast
                c_row = (c_row + jnp.sum(_rows(both, 0.0), axis=0, keepdims=True) * oh0
                         + jnp.sum(_rows(0.0, both), axis=0, keepdims=True) * oh1)
                t0, t1 = _lane_halves(dxdt * x)
                ddt = ddt + t0 * oh0 + t1 * oh1
                dx_ref[:, _ps(q)] = dxdt * dcol + dy * dsk_ref[q]
                k0, k1 = _lane_halves(dy * x)
                dskip = (dskip + jnp.sum(k0, axis=0, keepdims=True) * oh0 + jnp.sum(k1, axis=0, keepdims=True) * oh1)
            dcbb = _mx(dcb)
            dx_ref[:, _gs(C_OFF, g)] = dc_acc + _dot(dcbb, bb, 1, 0)
            dx_ref[:, _gs(B_OFF, g)] = db_acc + _dot(dcbb, cb_, 0, 0)
        da = (da_in + _exact_dot(upper_b, r_off, 1, 0, False) + _exact_dot(strict_b, c_int, 1, 0, False) + c_row)
        draw = (ddt + da * a_row) * _sigmoid(zr)
        ddt_ref[...] = draw
        dpar_ref[0:1, :] += jnp.sum(draw, axis=0, keepdims=True)
        dpar_ref[1:2, :] += jnp.sum(da * dtc, axis=0, keepdims=True) * a_row
        dpar_ref[2:3, :] += dskip

    rev = nc - 1
    psp = pl.BlockSpec((1, SSD_PAIRS, LANE, SSD_N), lambda c: (rev - c, 0, 0, 0))
    return pl.pallas_call(
        body, name=name, grid=(nc,),
        in_specs=_ssd_in_specs(rev) + [psp, pl.BlockSpec((CHUNK, SSD_W), lambda c: (rev - c, 0))],
        out_specs=[pl.BlockSpec((CHUNK, CONV_DIM), lambda c: (rev - c, 0)),
                   pl.BlockSpec((CHUNK, LANE), lambda c: (rev - c, 0)), pl.BlockSpec((8, LANE), lambda c: (0, 0))],
        out_shape=[jax.ShapeDtypeStruct((s_, CONV_DIM), F32), jax.ShapeDtypeStruct((s_, LANE), F32),
                   jax.ShapeDtypeStruct((8, LANE), F32)],
        scratch_shapes=[pltpu.VMEM((SSD_PAIRS, LANE, SSD_N), F32)],
        compiler_params=_params("arbitrary"),
    )(xbc, small, dtt, bias_r, bias_c, alog_r, alog_c, dsk, prev, dy)


GN = SSD_W // SSD_G


def _gated_norm_fwd(y, z, w, name="gated_norm_fwd"):
    s_, f = y.shape
    tr = _row_tile(s_)

    def body(y_ref, z_ref, w_ref, o_ref):
        for g in range(SSD_G):
            sl = slice(g * GN, (g + 1) * GN)
            gg = y_ref[:, sl] * _silu(z_ref[:, sl])
            r = lax.rsqrt(jnp.mean(gg * gg, axis=-1, keepdims=True) + EPS)
            o_ref[:, sl] = (gg * r * w_ref[:, sl]).astype(o_ref.dtype)

    row = pl.BlockSpec((tr, f), lambda i: (i, 0))
    wsp = pl.BlockSpec((1, f), lambda i: (0, 0))
    return pl.pallas_call(
        body, name=name, grid=(s_ // tr,), in_specs=[row, row, wsp], out_specs=row,
        out_shape=jax.ShapeDtypeStruct((s_, f), MXU_DTYPE), compiler_params=_params("parallel"),
    )(y, z, w.reshape(1, f))


def _gated_norm_bwd(y, z, w, dout, name="gated_norm_bwd"):
    s_, f = y.shape
    tr = _row_tile(s_)

    def body(y_ref, z_ref, w_ref, do_ref, dy_ref, dz_ref, dw_ref):
        @pl.when(pl.program_id(0) == 0)
        def _():
            dw_ref[...] = jnp.zeros_like(dw_ref)

        for g in range(SSD_G):
            sl = slice(g * GN, (g + 1) * GN)
            yv = y_ref[:, sl]
            zv = z_ref[:, sl]
            dov = do_ref[:, sl].astype(F32)
            sz = _silu(zv)
            gg = yv * sz
            r = lax.rsqrt(jnp.mean(gg * gg, axis=-1, keepdims=True) + EPS)
            gw = dov * w_ref[:, sl]
            c = jnp.mean(gw * gg, axis=-1, keepdims=True)
            dgg = r * gw - gg * (r * r * r * c)
            dy_ref[:, sl] = dgg * sz
            dz_ref[:, sl] = (dgg * yv * _dsilu(zv)).astype(dz_ref.dtype)
            dw_ref[:, sl] += jnp.sum(dov * gg * r, axis=0, keepdims=True)

    row = pl.BlockSpec((tr, f), lambda i: (i, 0))
    wsp = pl.BlockSpec((1, f), lambda i: (0, 0))
    return pl.pallas_call(
        body, name=name, grid=(s_ // tr,), in_specs=[row, row, wsp, row], out_specs=[row, row, wsp],
        out_shape=[jax.ShapeDtypeStruct((s_, f), F32), jax.ShapeDtypeStruct((s_, f), MXU_DTYPE),
                   jax.ShapeDtypeStruct((1, f), F32)],
        compiler_params=_params("arbitrary"),
    )(y, z, w.reshape(1, f), dout)


def _ffn_fwd(vv, w_gate, w_up, name="ffn_gate_up"):
    s_, d = vv.shape
    nb, _, f8 = w_gate.shape
    tm = _pick(s_, (1024, 512, 256, 128))

    def body(v_ref, wg_ref, wu_ref, g_ref, u_ref, a_ref):
        a = _mx(v_ref[...])
        g = _dot(a, _mx(wg_ref[...]), 1, 0)
        u = _dot(a, _mx(wu_ref[...]), 1, 0)
        g_ref[...] = g.astype(g_ref.dtype)
        u_ref[...] = u.astype(u_ref.dtype)
        a_ref[...] = (_silu(g) * u).astype(a_ref.dtype)

    wsp = pl.BlockSpec((None, d, f8), lambda j, i: (j, 0, 0))
    osp = pl.BlockSpec((None, tm, f8), lambda j, i: (j, i, 0))
    return pl.pallas_call(
        body, name=name, grid=(nb, s_ // tm),
        in_specs=[pl.BlockSpec((tm, d), lambda j, i: (i, 0)), wsp, wsp], out_specs=[osp] * 3,
        out_shape=[jax.ShapeDtypeStruct((nb, s_, f8), MXU_DTYPE)] * 3,
        compiler_params=_params("parallel", "parallel"),
    )(vv, w_gate, w_up)


def _ffn_bwd_act(dffn, w_down, gate, up, name="ffn_d_act"):
    s_, d = dffn.shape
    nb, f8, _ = w_down.shape
    tm = _pick(s_, (1024, 512, 256, 128))

    def body(d_ref, w_ref, g_ref, u_ref, dg_ref, du_ref):
        dact = _dot(_mx(d_ref[...]), _mx(w_ref[...]), 1, 1)
        g = g_ref[...].astype(F32)
        dg_ref[...] = (dact * u_ref[...].astype(F32) * _dsilu(g)).astype(dg_ref.dtype)
        du_ref[...] = (dact * _silu(g)).astype(du_ref.dtype)

    osp = pl.BlockSpec((None, tm, f8), lambda j, i: (j, i, 0))
    return pl.pallas_call(
        body, name=name, grid=(nb, s_ // tm),
        in_specs=[pl.BlockSpec((tm, d), lambda j, i: (i, 0)), pl.BlockSpec((None, f8, d), lambda j, i: (j, 0, 0)),
                  osp, osp],
        out_specs=[osp, osp], out_shape=[jax.ShapeDtypeStruct((nb, s_, f8), MXU_DTYPE)] * 2,
        compiler_params=_params("parallel", "parallel"),
    )(dffn, w_down, gate, up)


def _ffn_bwd_in(dgate, w_gate, dup, w_up, name="ffn_d_in"):
    nb, s_, f8 = dgate.shape
    d = w_gate.shape[1]
    tm = _pick(s_, (1024, 512, 256, 128))
    tn = _pick(d, (1024, 512, 256, 128))

    def body(dg_ref, wg_ref, du_ref, wu_ref, o_ref, acc):
        j = pl.program_id(2)
        part = _dot(_mx(dg_ref[...]), _mx(wg_ref[...]), 1, 1) + _dot(_mx(du_ref[...]), _mx(wu_ref[...]), 1, 1)

        @pl.when(j == 0)
        def _():
            acc[...] = part

        @pl.when(j > 0)
        def _():
            acc[...] += part

        @pl.when(j == nb - 1)
        def _():
            o_ref[...] = acc[...]

    asp = pl.BlockSpec((None, tm, f8), lambda i, n, j: (j, i, 0))
    wsp = pl.BlockSpec((None, tn, f8), lambda i, n, j: (j, n, 0))
    return pl.pallas_call(
        body, name=name, grid=(s_ // tm, d // tn, nb),
        in_specs=[asp, wsp, asp, wsp], out_specs=pl.BlockSpec((tm, tn), lambda i, n, j: (i, n)),
        out_shape=jax.ShapeDtypeStruct((s_, d), F32), scratch_shapes=[pltpu.VMEM((tm, tn), F32)],
        compiler_params=_params("parallel", "parallel", "arbitrary"),
    )(dgate, w_gate, dup, w_up)


def _adam_math(g, w, m, v):
    m2 = ADAM_B1 * m + (1.0 - ADAM_B1) * g
    v2 = ADAM_B2 * v + (1.0 - ADAM_B2) * (g * g)
    m_hat = m2 / (1.0 - ADAM_B1 ** ADAM_STEP)
    v_hat = v2 / (1.0 - ADAM_B2 ** ADAM_STEP)
    delta = -ADAM_LR * (m_hat / (jnp.sqrt(v_hat) + ADAM_EPS) + ADAM_WD * w)
    return delta, m2, v2


def _adamw(parts, w, m, v, name="adamw"):
    nd, r_, c = parts.shape
    tr = _pick(r_, (128, 64, 32, 16, 8))

    def body(p_ref, w_ref, m_ref, v_ref, g_ref, d_ref, m2_ref, v2_ref):
        g = p_ref[0].astype(F32)
        for i in range(1, nd):
            g = g + p_ref[i].astype(F32)
        delta, m2, v2 = _adam_math(g, w_ref[...], m_ref[...], v_ref[...])
        g_ref[...] = g
        d_ref[...] = delta
        m2_ref[...] = m2
        v2_ref[...] = v2

    row = pl.BlockSpec((tr, c), lambda i: (i, 0))
    psp = pl.BlockSpec((nd, tr, c), lambda i: (0, i, 0))
    return pl.pallas_call(
        body, name=name, grid=(r_ // tr,), in_specs=[psp, row, row, row], out_specs=[row] * 4,
        out_shape=[jax.ShapeDtypeStruct((r_, c), F32)] * 4, compiler_params=_params("parallel"),
    )(parts, w, m, v)


def _adamw_small(parts, w, m, v, name="adamw_small"):
    nd = parts.shape[0]

    def body(p_ref, w_ref, m_ref, v_ref, g_ref, d_ref, m2_ref, v2_ref):
        g = p_ref[0]
        for i in range(1, nd):
            g = g + p_ref[i]
        delta, m2, v2 = _adam_math(g, w_ref[...], m_ref[...], v_ref[...])
        g_ref[...] = g
        d_ref[...] = delta
        m2_ref[...] = m2
        v2_ref[...] = v2

    return pl.pallas_call(
        body, name=name, out_shape=[jax.ShapeDtypeStruct(w.shape, F32)] * 4,
        compiler_params=pltpu.CompilerParams(vmem_limit_bytes=VMEM_LIMIT_BYTES),
    )(parts, w, m, v)


_HBM = pl.BlockSpec(memory_space=pltpu.HBM)
_MESH = pl.DeviceIdType.MESH


def _all_gather(xs, name):
    na = len(xs)

    def body(*refs):
        x_refs, out_refs = refs[:na], refs[na:2 * na]
        send_sems, recv_sems, local_sems = refs[2 * na:]
        x, y, c = lax.axis_index("x"), lax.axis_index("y"), lax.axis_index("c")
        me, sibling = (x, y, c), (x, y, 1 - c)
        chips = [(1 - x, y), (x, 1 - y), (1 - x, 1 - y)]

        def slot(a, px, py, pc):
            return out_refs[a].at[4 * px + 2 * py + pc]

        def copy(a, k, block, to, src=None):
            return pltpu.make_async_remote_copy(
                src_ref=slot(a, *block) if src is None else src, dst_ref=slot(a, *block),
                send_sem=send_sems.at[a, k], recv_sem=recv_sems.at[a, k], device_id=to, device_id_type=_MESH)

        mine = [pltpu.make_async_copy(x_refs[a], slot(a, *me), local_sems.at[a]) for a in range(na)]
        started = []
        for a in range(na):
            mine[a].start()
            first = [copy(a, 0, me, sibling, src=x_refs[a])]
            first += [copy(a, 1 + j, me, (*chip, c), src=x_refs[a]) for j, chip in enumerate(chips)]
            for cp in first:
                cp.start()
            started += first
        for a in range(na):
            for j, chip in enumerate(chips):
                copy(a, 1 + j, (*chip, c), me).wait_recv()
                fwd = copy(a, 4 + j, (*chip, c), sibling)
                fwd.start()
                started.append(fwd)
        for a in range(na):
            copy(a, 0, sibling, me).wait_recv()
            for j, chip in enumerate(chips):
                copy(a, 4 + j, (*chip, 1 - c), me).wait_recv()
        for cp in started:
            cp.wait_send()
        for cp in mine:
            cp.wait()

    return pl.pallas_call(
        body, name=name, out_shape=[jax.ShapeDtypeStruct((N_DEV,) + t.shape, t.dtype) for t in xs],
        in_specs=[_HBM] * na, out_specs=[_HBM] * na,
        scratch_shapes=[pltpu.SemaphoreType.DMA((na, 7)), pltpu.SemaphoreType.DMA((na, 7)),
                        pltpu.SemaphoreType.DMA((na,))],
    )(*xs)


_SEM = pl.BlockSpec(memory_space=pltpu.SEMAPHORE)
_EFFECT = pltpu.SideEffectType.DATAFLOW_SIDE_EFFECTING


def _peers(x, y, c):
    out = []
    for k in range(1, N_DEV):
        px = 1 - x if k & 4 else x
        py = 1 - y if k & 2 else y
        pc = 1 - c if k & 1 else c
        out.append(((px, py, pc), 4 * px + 2 * py + pc))
    return out


def _push_copies(scatter, src_refs, land_refs, send_sems, recv_sems):
    x, y, c = lax.axis_index("x"), lax.axis_index("y"), lax.axis_index("c")
    me = 4 * x + 2 * y + c
    pairs = []
    for a, (src, land) in enumerate(zip(src_refs, land_refs)):
        for k, (peer, slot) in enumerate(_peers(x, y, c)):
            out_src = src.at[slot] if scatter else src
            si = a * (N_DEV - 1) + k
            send = pltpu.make_async_remote_copy(src_ref=out_src, dst_ref=land.at[me], send_sem=send_sems.at[si],
                                                recv_sem=recv_sems.at[si], device_id=peer, device_id_type=_MESH)
            recv = pltpu.make_async_remote_copy(src_ref=out_src, dst_ref=land.at[slot], send_sem=send_sems.at[si],
                                                recv_sem=recv_sems.at[si], device_id=peer, device_id_type=_MESH)
            pairs.append((send, recv))
    return pairs


def _push_start(srcs, scatter, dep, name):
    na = len(srcs)
    shapes = [t.shape[1:] if scatter else t.shape for t in srcs]
    lands = [pltpu.with_memory_space_constraint(lax.empty((N_DEV,) + s, t.dtype), pltpu.HBM) for s, t in zip(shapes, srcs)]

    def body(*refs):
        src_refs, land_refs = refs[:na], refs[na:2 * na]
        send_sems, recv_sems = refs[2 * na + 1], refs[2 * na + 2]
        token = refs[-1]
        for send, _ in _push_copies(scatter, src_refs, land_refs, send_sems, recv_sems):
            send.start()
        token[...] = jnp.zeros_like(token)

    sem = pltpu.SemaphoreType.DMA((na * (N_DEV - 1),))
    outs = pl.pallas_call(
        body, name=name,
        out_shape=(sem, sem) + tuple(pltpu.HBM(t.shape, t.dtype) for t in srcs)
        + tuple(pltpu.HBM(t.shape, t.dtype) for t in lands) + (jax.ShapeDtypeStruct((8, LANE), F32),),
        in_specs=[_HBM] * (2 * na) + [pl.BlockSpec(memory_space=pl.ANY)],
        out_specs=(_SEM, _SEM) + (_HBM,) * (2 * na) + (pl.BlockSpec(memory_space=pltpu.VMEM),),
        input_output_aliases={i: 2 + i for i in range(2 * na)},
        compiler_params=pltpu.CompilerParams(has_side_effects=_EFFECT),
    )(*[pltpu.with_memory_space_constraint(t, pltpu.HBM) for t in srcs], *lands, dep)
    return outs[0], outs[1], outs[2:2 + na], outs[2 + na:2 + 2 * na], outs[-1]


def _push_wait(send_sems, recv_sems, src_thru, land_thru, scatter, after, name):
    na = len(src_thru)

    def body(*refs):
        src_refs, land_refs = refs[:na], refs[na:2 * na]
        ssem, rsem = refs[2 * na], refs[2 * na + 1]
        for send, recv in _push_copies(scatter, src_refs, land_refs, ssem, rsem):
            send.wait_send()
            recv.wait_recv()

    outs = pl.pallas_call(
        body, name=name,
        out_shape=tuple(pltpu.HBM(t.shape, t.dtype) for t in src_thru) + tuple(pltpu.HBM(t.shape, t.dtype) for t in land_thru),
        in_specs=[_HBM] * (2 * na) + [_SEM, _SEM, pl.BlockSpec(memory_space=pl.ANY)],
        out_specs=(_HBM,) * (2 * na),
        input_output_aliases={i: i for i in range(2 * na)},
        compiler_params=pltpu.CompilerParams(has_side_effects=_EFFECT),
    )(*src_thru, *land_thru, send_sems, recv_sems, after)
    return outs[:na], outs[na:]


def _exchange_behind(srcs, scatter, dep, name):
    send_sems, recv_sems, thru, lands, token = _push_start(srcs, scatter, dep, name + "_start")

    def finish(after):
        src_done, land_done = _push_wait(send_sems, recv_sems, thru, lands, scatter, after, name + "_wait")
        return _place_own(land_done, src_done, scatter, name + "_own")

    return token[0, 0], finish


def _place_own(lands, srcs, scatter, name):
    me = (4 * lax.axis_index("x") + 2 * lax.axis_index("y") + lax.axis_index("c")).astype(jnp.int32).reshape(1)
    outs = []
    for a, (land, src) in enumerate(zip(lands, srcs)):
        r_, c_ = land.shape[1:]
        tr = _pick(r_, (512, 256, 128, 64, 32, 16))

        def body(me_ref, land_ref, src_ref, out_ref):
            out_ref[...] = src_ref[...]

        src_spec = (pl.BlockSpec((None, tr, c_), lambda i, me_: (me_[0], i, 0)) if scatter
                    else pl.BlockSpec((tr, c_), lambda i, me_: (i, 0)))
        gs = pltpu.PrefetchScalarGridSpec(
            num_scalar_prefetch=1, grid=(r_ // tr,),
            in_specs=[pl.BlockSpec(memory_space=pl.ANY), src_spec],
            out_specs=pl.BlockSpec((None, tr, c_), lambda i, me_: (me_[0], i, 0)))
        outs.append(pl.pallas_call(
            body, name=f"{name}_{a}", grid_spec=gs, out_shape=jax.ShapeDtypeStruct(land.shape, land.dtype),
            input_output_aliases={1: 0}, compiler_params=_params("arbitrary"),
        )(me, land, src))
    return outs


_BIG = (("w_in", D_MODEL, D_IN, 1), ("w_uq", Q_RANK, HEADS * QK, 1), ("w_ukv", KV_RANK, HEADS * (NOPE + VDIM), 1),
        ("w_out", D_MODEL, D_MODEL, 0), ("w_gate", D_MODEL, D_FF, 1), ("w_up", D_MODEL, D_FF, 1),
        ("w_down", D_FF, D_MODEL, 0))
_CQKV = (0, Q_RANK + KV_RANK)
_KR = (_CQKV[1], _CQKV[1] + ROPE)
_Z = (_KR[1], _KR[1] + SSD_W)
_XBC = (_Z[1], _Z[1] + CONV_DIM)
_DT = (_XBC[1], _XBC[1] + SSD_H)


def _win_segments(w_in_g):
    w = jnp.transpose(w_in_g, (1, 0, 2)).reshape(D_MODEL, D_IN)
    small = jnp.concatenate([w[:, _KR[0]:_KR[1]], w[:, _DT[0]:_DT[1]],
                             jnp.zeros((D_MODEL, LANE - ROPE - SSD_H), w.dtype)], axis=1)
    return w[:, _CQKV[0]:_CQKV[1]], w[:, _Z[0]:_Z[1]], w[:, _XBC[0]:_XBC[1]], small


def _win_from_segments(g_cqkv, g_z, g_xbc, g_small):
    w = jnp.concatenate([g_cqkv, g_small[:, :ROPE], g_z, g_xbc, g_small[:, ROPE:ROPE + SSD_H]], axis=1)
    return jnp.transpose(w.reshape(D_MODEL, N_DEV, D_IN // N_DEV), (1, 0, 2))


_SMALL = (("q_norm_w", 512), ("kv_norm_w", 512), ("conv_b", CONV_DIM), ("dt_bias", SSD_H), ("a_log", SSD_H),
          ("d_skip", SSD_H), ("ssd_norm_w", SSD_W), ("attn_out_norm_w", 1024), ("pre_mix_norm_w", D_MODEL),
          ("post_mix_norm_w", D_MODEL), ("pre_ffn_norm_w", D_MODEL), ("post_ffn_norm_w", D_MODEL),
          ("conv_w", CONV_K * CONV_DIM))
_SMALL_ROWS = -(-sum(-(-n // LANE) for _, n in _SMALL) // 8) * 8


def _pack_small(vals):
    rows = []
    for name, n in _SMALL:
        v = vals[name].reshape(-1).astype(F32)
        pad = -(-n // LANE) * LANE
        rows.append(jnp.pad(v, (0, pad - n)).reshape(-1, LANE))
    m = jnp.concatenate(rows, axis=0)
    return jnp.pad(m, ((0, _SMALL_ROWS - m.shape[0]), (0, 0)))


def _unpack_small(m):
    out, r = {}, 0
    for name, n in _SMALL:
        nr = -(-n // LANE)
        out[name] = m[r:r + nr].reshape(-1)[:n]
        r += nr
    return out


def _head_row(v):
    return jnp.pad(v.reshape(1, -1).astype(F32), ((0, 0), (HEAD_LANE, LANE - HEAD_LANE - v.shape[-1])))


def _local_step(x, positions, target, wg, small, weights, on_grads):
    w_cqkv, w_z, w_xbc, w_small = _win_segments(wg["w_in"])
    conv_w = wg["conv_w"]
    conv_b = small["conv_b"].reshape(1, CONV_DIM)
    qkv_norm_w = jnp.concatenate([small["q_norm_w"], small["kv_norm_w"]])
    attn_norm_w = small["attn_out_norm_w"].reshape(HEADS, 1, VDIM)
    scale = QK ** -0.5

    inv_freq = ROPE_THETA ** (-jnp.arange(0, ROPE, 2, dtype=F32) / ROPE)
    ang = positions.astype(F32)[:, None] * inv_freq
    cos2 = jnp.tile(jnp.cos(ang), (1, 2))
    sin2 = jnp.tile(jnp.sin(ang), (1, 2))

    u = _rms_fwd(x, small["pre_mix_norm_w"], out_dtype=MXU_DTYPE, name="pre_mix_norm")
    cqkv = _mm(u, w_cqkv, "nn", name="in_proj_qkv")
    z = _mm(u, w_z, "nn", name="in_proj_z")
    xbc = _mm(u, w_xbc, "nn", name="in_proj_xbc")
    sm = _mm(u, w_small, "nn", name="in_proj_small")

    w_uq, w_ukv, w_out = weights("heads", cqkv)
    w_out = w_out.reshape(D_MODEL, D_MODEL)
    w_out_a = w_out[:HEADS * VDIM].reshape(HEADS, VDIM, D_MODEL)
    w_out_s = w_out[HEADS * VDIM:]
    qkvn = _rms_fwd(cqkv, qkv_norm_w, groups=2, out_dtype=MXU_DTYPE, name="qkv_norm")
    q = _mm(qkvn, w_uq, "nn", b_blk=True, out_blk=True, a_cols=(0, Q_RANK), name="q_up")
    kv = _mm(qkvn, w_ukv, "nn", b_blk=True, out_blk=True, a_cols=(Q_RANK, KV_RANK), name="kv_up")
    q_h = _q_prep(q, cos2, sin2, scale, name="q_prep")
    k_h, v_h = _kv_prep(kv, sm, cos2, sin2)
    o_h, lse = _flash_fwd(q_h, k_h, v_h)
    attn = _hnorm_fwd(o_h, attn_norm_w)

    xbc_act = _conv_fwd(xbc, conv_w, conv_b)
    dtt = jnp.transpose(sm[:, HEAD_LANE:HEAD_LANE + SSD_H])
    ssd_args = (xbc_act, sm, dtt, _head_row(small["dt_bias"]), small["dt_bias"].reshape(SSD_H, 1),
                _head_row(small["a_log"]), small["a_log"].reshape(SSD_H, 1),
                jnp.broadcast_to(small["d_skip"].reshape(SSD_H, 1), (SSD_H, SSD_P)).reshape(SSD_PAIRS, 1, LANE))
    y_ssd, prev = _ssd_fwd(*ssd_args)
    ssm = _gated_norm_fwd(y_ssd, z, small["ssd_norm_w"])

    mix = _mm(attn, w_out_a, "nn", a_blk=True, b_blk=True, fuse=HEADS, name="out_proj_attn")
    mix = _mm(ssm, w_out_s, "nn", add=mix, name="out_proj_ssm")
    h1 = _rms_fwd(mix, small["post_mix_norm_w"], res=x, name="post_mix_norm")

    w_gate, w_up, w_down = weights("ffn", mix)
    vv = _rms_fwd(h1, small["pre_ffn_norm_w"], out_dtype=MXU_DTYPE, name="pre_ffn_norm")
    gate, up, act = _ffn_fwd(vv, w_gate, w_up)
    ffn = _mm(act, w_down, "nn", a_blk=True, b_blk=True, fuse=2, name="ffn_down")
    loss_blk, dy, dffn, g_post_ffn = _loss_head(ffn, h1, target, small["post_ffn_norm_w"])

    g_down = _mm(act, dffn, "tn", a_blk=True, out_blk=True, out_dtype=MXU_DTYPE, name="g_down")
    dgate, dup = _ffn_bwd_act(dffn, w_down, gate, up)
    dvv = _ffn_bwd_in(dgate, w_gate, dup, w_up)
    g_gate = _mm(vv, dgate, "tn", b_blk=True, out_blk=True, out_dtype=MXU_DTYPE, name="g_gate")
    g_up = _mm(vv, dup, "tn", b_blk=True, out_blk=True, out_dtype=MXU_DTYPE, name="g_up")
    pre_ffn_w = small["pre_ffn_norm_w"] + on_grads("ffn", [g_gate, g_up, g_down])
    dh1, g_pre_ffn = _rms_bwd(h1, pre_ffn_w, [dvv], res=dy, name="pre_ffn_norm_bwd")

    dmix, g_post_mix = _rms_bwd(mix, small["post_mix_norm_w"], [dh1], out_dtype=MXU_DTYPE, name="post_mix_norm_bwd")
    dattn = _mm(dmix, w_out_a, "nt", b_blk=True, out_blk=True, name="d_attn")
    dssm = _mm(dmix, w_out_s, "nt", name="d_ssm")
    g_out_a = _mm(attn, dmix, "tn", a_blk=True, out_blk=True, out_dtype=MXU_DTYPE, name="g_out_attn")
    g_out_s = _mm(ssm, dmix, "tn", out_dtype=MXU_DTYPE, name="g_out_ssm")
    g_out = jnp.concatenate([g_out_a.reshape(HEADS * VDIM, D_MODEL), g_out_s], axis=0)

    do_h, delta, g_attn_norm = _hnorm_bwd(o_h, attn_norm_w, dattn)
    dq_h, dk_h, dv_h = _flash_bwd(q_h, k_h, v_h, do_h, lse, delta)
    dq = _q_prep(dq_h, cos2, -sin2, scale, name="dq_post")

    dy_ssd, dz, g_ssd_norm = _gated_norm_bwd(y_ssd, z, small["ssd_norm_w"], dssm)
    dxbc_act, ddt, dpar = _ssd_bwd(*ssd_args, prev, dy_ssd)
    dkv, dsm = _dkv_post(dk_h, dv_h, ddt, cos2, -sin2)
    dpre, dwb = _conv_bwd_pre(xbc, conv_w, conv_b, dxbc_act)
    dxbc = _conv_bwd_in(dpre, conv_w)

    dqn = _mm(dq, w_uq, "nt", a_blk=True, b_blk=True, fuse=HEADS, name="d_qn")
    dkvn = _mm(dkv, w_ukv, "nt", a_blk=True, b_blk=True, fuse=HEADS, name="d_kvn")
    g_uq = _mm(qkvn, dq, "tn", b_blk=True, out_blk=True, a_cols=(0, Q_RANK), out_dtype=MXU_DTYPE, name="g_uq")
    g_ukv = _mm(qkvn, dkv, "tn", b_blk=True, out_blk=True, a_cols=(Q_RANK, KV_RANK), out_dtype=MXU_DTYPE, name="g_ukv")
    heads_token = on_grads("heads", [g_uq, g_ukv, g_out.reshape(N_DEV, D_MODEL // N_DEV, D_MODEL)])
    dcqkv, g_qkv_norm = _rms_bwd(cqkv, qkv_norm_w + heads_token, [dqn, dkvn], out_dtype=MXU_DTYPE, name="qkv_norm_bwd")

    g_in = _win_from_segments(_mm(u, dcqkv, "tn", out_dtype=MXU_DTYPE, name="g_in_qkv"),
                              _mm(u, dz, "tn", out_dtype=MXU_DTYPE, name="g_in_z"),
                              _mm(u, dxbc, "tn", out_dtype=MXU_DTYPE, name="g_in_xbc"),
                              _mm(u, dsm, "tn", out_dtype=MXU_DTYPE, name="g_in_small"))
    in_token = on_grads("in", [g_in])
    du = _mm(dsm + in_token.astype(dsm.dtype), w_small, "nt", name="d_u_small")
    du = _mm(dcqkv, w_cqkv, "nt", add=du, name="d_u_qkv")
    du = _mm(dz, w_z, "nt", add=du, name="d_u_z")
    du = _mm(dxbc, w_xbc, "nt", add=du, name="d_u_xbc")
    dx, g_pre_mix = _rms_bwd(x, small["pre_mix_norm_w"], [du], res=dh1, name="pre_mix_norm_bwd")

    hl = slice(HEAD_LANE, HEAD_LANE + SSD_H)
    g_small = {"q_norm_w": g_qkv_norm[0, :Q_RANK], "kv_norm_w": g_qkv_norm[0, Q_RANK:], "conv_b": dwb[CONV_K],
               "dt_bias": dpar[0, hl], "a_log": dpar[1, hl], "d_skip": dpar[2, hl], "ssd_norm_w": g_ssd_norm,
               "attn_out_norm_w": g_attn_norm, "pre_mix_norm_w": g_pre_mix, "post_mix_norm_w": g_post_mix,
               "pre_ffn_norm_w": g_pre_ffn, "post_ffn_norm_w": g_post_ffn, "conv_w": dwb[:CONV_K]}
    return loss_blk[0, 0], dx, g_small


_WEIGHT_ORDER = ("w_in", "q_norm_w", "w_uq", "kv_norm_w", "w_ukv", "conv_w", "conv_b", "dt_bias", "a_log", "d_skip",
                 "ssd_norm_w", "attn_out_norm_w", "w_out", "pre_mix_norm_w", "post_mix_norm_w", "pre_ffn_norm_w",
                 "post_ffn_norm_w", "w_gate", "w_up", "w_down")


def kernel(x, positions, w_in, q_norm_w, w_uq, kv_norm_w, w_ukv, conv_w, conv_b, dt_bias, a_log, d_skip, ssd_norm_w, attn_out_norm_w, w_out, pre_mix_norm_w, post_mix_norm_w, pre_ffn_norm_w, post_ffn_norm_w, w_gate, w_up, w_down, loss_target, m_w_in, m_q_norm_w, m_w_uq, m_kv_norm_w, m_w_ukv, m_conv_w, m_conv_b, m_dt_bias, m_a_log, m_d_skip, m_ssd_norm_w, m_attn_out_norm_w, m_w_out, m_pre_mix_norm_w, m_post_mix_norm_w, m_pre_ffn_norm_w, m_post_ffn_norm_w, m_w_gate, m_w_up, m_w_down, v_w_in, v_q_norm_w, v_w_uq, v_kv_norm_w, v_w_ukv, v_conv_w, v_conv_b, v_dt_bias, v_a_log, v_d_skip, v_ssd_norm_w, v_attn_out_norm_w, v_w_out, v_pre_mix_norm_w, v_post_mix_norm_w, v_pre_ffn_norm_w, v_post_ffn_norm_w, v_w_gate, v_w_up, v_w_down):
    w = dict(w_in=w_in, q_norm_w=q_norm_w, w_uq=w_uq, kv_norm_w=kv_norm_w, w_ukv=w_ukv, conv_w=conv_w, conv_b=conv_b,
             dt_bias=dt_bias, a_log=a_log, d_skip=d_skip, ssd_norm_w=ssd_norm_w, attn_out_norm_w=attn_out_norm_w,
             w_out=w_out, pre_mix_norm_w=pre_mix_norm_w, post_mix_norm_w=post_mix_norm_w,
             pre_ffn_norm_w=pre_ffn_norm_w, post_ffn_norm_w=post_ffn_norm_w, w_gate=w_gate, w_up=w_up, w_down=w_down)
    m = dict(w_in=m_w_in, q_norm_w=m_q_norm_w, w_uq=m_w_uq, kv_norm_w=m_kv_norm_w, w_ukv=m_w_ukv, conv_w=m_conv_w,
             conv_b=m_conv_b, dt_bias=m_dt_bias, a_log=m_a_log, d_skip=m_d_skip, ssd_norm_w=m_ssd_norm_w,
             attn_out_norm_w=m_attn_out_norm_w, w_out=m_w_out, pre_mix_norm_w=m_pre_mix_norm_w,
             post_mix_norm_w=m_post_mix_norm_w, pre_ffn_norm_w=m_pre_ffn_norm_w, post_ffn_norm_w=m_post_ffn_norm_w,
             w_gate=m_w_gate, w_up=m_w_up, w_down=m_w_down)
    v = dict(w_in=v_w_in, q_norm_w=v_q_norm_w, w_uq=v_w_uq, kv_norm_w=v_kv_norm_w, w_ukv=v_w_ukv, conv_w=v_conv_w,
             conv_b=v_conv_b, dt_bias=v_dt_bias, a_log=v_a_log, d_skip=v_d_skip, ssd_norm_w=v_ssd_norm_w,
             attn_out_norm_w=v_attn_out_norm_w, w_out=v_w_out, pre_mix_norm_w=v_pre_mix_norm_w,
             post_mix_norm_w=v_post_mix_norm_w, pre_ffn_norm_w=v_pre_ffn_norm_w, post_ffn_norm_w=v_post_ffn_norm_w,
             w_gate=v_w_gate, w_up=v_w_up, w_down=v_w_down)
    w, m, v = ({k: t[0] for k, t in d.items()} for d in (w, m, v))
    me = 4 * lax.axis_index("x") + 2 * lax.axis_index("y") + lax.axis_index("c")
    groups = {"in": ("w_in",), "heads": ("w_uq", "w_ukv", "w_out"), "ffn": ("w_gate", "w_up", "w_down")}
    cshard = CONV_DIM // N_DEV

    shards = [w["w_in"].astype(MXU_DTYPE),
              jnp.stack(_split3(w["conv_w"])).reshape(3 * CONV_K, cshard).astype(MXU_DTYPE)]
    w_in_g, cw = _all_gather(shards, name="gather_weights")
    cw = cw.astype(F32).reshape(N_DEV, 3, CONV_K, cshard)
    wg = {"w_in": w_in_g, "conv_w": jnp.transpose(cw[:, 0] + cw[:, 1] + cw[:, 2], (1, 0, 2)).reshape(CONV_K, CONV_DIM)}
    arriving, dep, started = {}, wg["conv_w"], jnp.zeros((), F32)
    small = {name: w[name] for name, _ in _SMALL if name != "conv_w"}
    for group in ("heads", "ffn"):
        token, arriving[group] = _exchange_behind([w[name].astype(MXU_DTYPE) for name in groups[group]], False,
                                                  dep, group + "_weights")
        started = started + token
        dep = jnp.zeros((8, LANE), F32) + started
    small["pre_mix_norm_w"] = small["pre_mix_norm_w"] + started

    leaving = {}

    def on_grads(group, gs):
        token, leaving[group] = _exchange_behind(gs, True, jnp.zeros((8, LANE), F32), group + "_grads")
        return token

    loss_local, dx, g_small = _local_step(x[0], positions[0], loss_target[0], wg, small,
                                          lambda group, after: arriving[group](after), on_grads)
    loss = lax.psum(loss_local, ("x", "y", "c"))

    recv = {}
    for group in ("ffn", "heads", "in"):
        recv.update(zip(groups[group], leaving[group](dx)))
    grads, deltas, new_m, new_v = {}, {}, {}, {}
    for name, parts in recv.items():
        grads[name], deltas[name], new_m[name], new_v[name] = _adamw(parts, w[name], m[name], v[name],
                                                                     name="adamw_" + name)

    def embed(t):
        return lax.dynamic_update_slice(jnp.zeros((CONV_K, CONV_DIM), F32), t, (0, me * cshard))

    parts_s = _all_gather([_pack_small(g_small)], name="gather_small_grads")[0]
    packs = [_pack_small({**{n_: d[n_] for n_, _ in _SMALL if n_ != "conv_w"}, "conv_w": embed(d["conv_w"])})
             for d in (w, m, v)]
    outs = [_unpack_small(t) for t in _adamw_small(parts_s, *packs)]
    for name, n in _SMALL:
        for dst, src in zip((grads, deltas, new_m, new_v), outs):
            if name == "conv_w":
                dst[name] = lax.dynamic_slice(src[name].reshape(CONV_K, CONV_DIM), (0, me * cshard), (CONV_K, cshard))
            else:
                dst[name] = src[name]

    def lead(d):
        return [d[name][None] for name in _WEIGHT_ORDER]

    return (loss, dx[None], *lead(grads), *lead(deltas), *lead(new_m), *lead(new_v))
```

```python
import numpy as np

import jax
import jax.numpy as jnp
from jax import lax
from jax.experimental import pallas as pl
from jax.experimental.pallas import tpu as pltpu

F32 = jnp.float32
BF16 = jnp.bfloat16
MXU_DTYPE = jnp.bfloat16
EPS = 1e-6
VMEM_LIMIT_BYTES = 48 * 1024 * 1024
K_TILE_MAX = 2048

N_DEV = 8
D_MODEL = 2048
Q_RANK = 512
KV_RANK = 512
ROPE = 64
HALF = ROPE // 2
HEADS = 8
NOPE = 128
VDIM = 128
QK = NOPE + ROPE
SSD_W = 1024
SSD_H = 16
SSD_P = 64
SSD_G = 2
SSD_E = SSD_H // SSD_G
SSD_N = 128
CHUNK = 128
CONV_K = 4
CONV_DIM = SSD_W + 2 * SSD_G * SSD_N
B_OFF = SSD_W
C_OFF = SSD_W + SSD_G * SSD_N
D_FF = 5632
D_IN = Q_RANK + KV_RANK + ROPE + SSD_W + CONV_DIM + SSD_H
ROPE_THETA = 10000.0
LANE = 128
HEAD_LANE = ROPE

ADAM_LR = 0.001
ADAM_B1 = 0.9
ADAM_B2 = 0.999
ADAM_EPS = 1e-08
ADAM_WD = 0.01
ADAM_STEP = 10


def _pick(n, cands):
    for c in cands:
        if n % c == 0:
            return c
    return n


def _params(*sem):
    return pltpu.CompilerParams(dimension_semantics=sem, vmem_limit_bytes=VMEM_LIMIT_BYTES)


def _sigmoid(x):
    return 1.0 / (1.0 + jnp.exp(-x))


def _silu(x):
    return x * _sigmoid(x)


def _dsilu(x):
    s = _sigmoid(x)
    return s * (1.0 + x * (1.0 - s))


def _softplus(x):
    e = jnp.exp(-jnp.abs(x))
    small = e * (1.0 - e * (0.5 - e * (1.0 / 3.0)))
    return jnp.maximum(x, 0.0) + jnp.where(e < 0.01, small, jnp.log(1.0 + e))


def _dot(a, b, ca, cb):
    return lax.dot_general(a, b, (((ca,), (cb,)), ((), ())), preferred_element_type=F32)


def _mx(v):
    return v.astype(MXU_DTYPE)


def _split3(a):
    hi = a.astype(BF16)
    r1 = a - hi.astype(F32)
    mid = r1.astype(BF16)
    lo = (r1 - mid.astype(F32)).astype(BF16)
    return hi, mid, lo


def _exact_dot(a, b, ca, cb, split_a):
    if split_a:
        return sum(_dot(p, b, ca, cb) for p in _split3(a))
    return sum(_dot(a, p, ca, cb) for p in _split3(b))


def _mm(a, b, mode, *, a_blk=False, b_blk=False, out_blk=False, a_cols=None, add=None, out_dtype=F32, fuse=1,
        name="mm"):
    a2, b2 = a.shape[-2:], b.shape[-2:]
    a_last = a2[1] if a_cols is None else a_cols[1]
    a_start = 0 if a_cols is None else a_cols[0]
    if mode == "nn":
        m, k, (k2, n) = a2[0], a_last, b2
    elif mode == "nt":
        m, k, (n, k2) = a2[0], a_last, b2
    else:
        k, m, (k2, n) = a2[0], a_last, b2
    assert k == k2, (a.shape, b.shape, mode)
    tm = _pick(m, (1024, 704, 512, 256, 128))
    tn = _pick(n, (1024, 768, 704, 512, 256, 192, 128))
    tk = k if k <= K_TILE_MAX else _pick(k, (K_TILE_MAX, 1024, 512))
    nk = k // tk
    jo = N_DEV if out_blk else 1
    reduce_blocks = a_blk and b_blk and not out_blk
    assert fuse == 1 or reduce_blocks
    jr = N_DEV // fuse if reduce_blocks else 1
    ca, cb = {"nn": (1, 0), "nt": (1, 1), "tn": (0, 0)}[mode]
    has_add = add is not None
    single = jr * nk == 1
    if mode == "tn":
        assert a_start % tm == 0
        a_block, a_idx = (tk, tm), (lambda i, kk: (kk, i + a_start // tm))
    else:
        assert a_start % tk == 0
        a_block, a_idx = (tm, tk), (lambda i, kk: (i, kk + a_start // tk))
    b_block, b_idx = ((tn, tk), (lambda nn_, kk: (nn_, kk))) if mode == "nt" else ((tk, tn), (lambda nn_, kk: (kk, nn_)))

    def blk_specs(blocked, block, idx, of_a, t):
        def pos(o, i, nn_, kk):
            return idx(i, kk) if of_a else idx(nn_, kk)
        if blocked:
            return pl.BlockSpec((None,) + block,
                                lambda o, i, nn_, r, kk: ((o if out_blk else r * fuse + t),) + pos(o, i, nn_, kk))
        return pl.BlockSpec(block, lambda o, i, nn_, r, kk: pos(o, i, nn_, kk))

    a_specs = [blk_specs(a_blk, a_block, a_idx, True, t) for t in range(fuse)]
    b_specs = [blk_specs(b_blk, b_block, b_idx, False, t) for t in range(fuse)]
    o_spec = (pl.BlockSpec((None, tm, tn), lambda o, i, nn_, r, kk: (o, i, nn_)) if out_blk
              else pl.BlockSpec((tm, tn), lambda o, i, nn_, r, kk: (i, nn_)))

    def body(*refs):
        a_refs, b_refs = refs[:fuse], refs[fuse:2 * fuse]
        add_ref = refs[2 * fuse] if has_add else None
        o_ref = refs[2 * fuse + 1] if has_add else refs[2 * fuse]
        part = _dot(_mx(a_refs[0][...]), _mx(b_refs[0][...]), ca, cb)
        for t in range(1, fuse):
            part = part + _dot(_mx(a_refs[t][...]), _mx(b_refs[t][...]), ca, cb)
        if single:
            if has_add:
                part = part + add_ref[...]
            o_ref[...] = part.astype(o_ref.dtype)
            return
        acc = refs[-1]
        r, kk = pl.program_id(3), pl.program_id(4)
        first = jnp.logical_and(r == 0, kk == 0)
        last = jnp.logical_and(r == jr - 1, kk == nk - 1)

        @pl.when(first)
        def _():
            acc[...] = part

        @pl.when(jnp.logical_not(first))
        def _():
            acc[...] += part

        @pl.when(last)
        def _():
            res = acc[...]
            if has_add:
                res = res + add_ref[...]
            o_ref[...] = res.astype(o_ref.dtype)

    out_shape = ((N_DEV, m, n) if out_blk else (m, n))
    return pl.pallas_call(
        body, name=name, grid=(jo, m // tm, n // tn, jr, nk),
        in_specs=a_specs + b_specs + ([o_spec] if has_add else []), out_specs=o_spec,
        out_shape=jax.ShapeDtypeStruct(out_shape, out_dtype),
        scratch_shapes=[] if single else [pltpu.VMEM((tm, tn), F32)],
        compiler_params=_params("parallel", "parallel", "parallel", "arbitrary", "arbitrary"),
    )(*((a,) * fuse + (b,) * fuse + ((add,) if has_add else ())))


def _row_tile(r_):
    return _pick(r_, (256, 128, 64, 32, 16, 8))


def _rms_fwd(t, w, groups=1, res=None, out_dtype=F32, name="rms_fwd"):
    r_, f = t.shape
    fg = f // groups
    tr = _row_tile(r_)
    has_res = res is not None

    def body(*refs):
        t_ref, w_ref = refs[0], refs[1]
        res_ref = refs[2] if has_res else None
        o_ref = refs[-1]
        for g in range(groups):
            sl = slice(g * fg, (g + 1) * fg)
            tv = t_ref[:, sl].astype(F32)
            r = lax.rsqrt(jnp.mean(tv * tv, axis=-1, keepdims=True) + EPS)
            y = tv * r * w_ref[:, sl]
            if has_res:
                y = y + res_ref[:, sl]
            o_ref[:, sl] = y.astype(o_ref.dtype)

    row = pl.BlockSpec((tr, f), lambda i: (i, 0))
    wsp = pl.BlockSpec((1, f), lambda i: (0, 0))
    return pl.pallas_call(
        body, name=name, grid=(r_ // tr,),
        in_specs=[row, wsp] + ([row] if has_res else []), out_specs=row,
        out_shape=jax.ShapeDtypeStruct((r_, f), out_dtype),
        compiler_params=_params("parallel"),
    )(*((t, w.reshape(1, f)) + ((res,) if has_res else ())))


def _rms_bwd(t, w, dys, res=None, out_dtype=F32, name="rms_bwd"):
    r_, f = t.shape
    groups = len(dys)
    fg = f // groups
    tr = _row_tile(r_)
    has_res = res is not None

    def body(*refs):
        t_ref, w_ref = refs[0], refs[1]
        dy_refs = refs[2:2 + groups]
        res_ref = refs[2 + groups] if has_res else None
        dt_ref, dw_ref = refs[-2], refs[-1]

        @pl.when(pl.program_id(0) == 0)
        def _():
            dw_ref[...] = jnp.zeros_like(dw_ref)

        for g in range(groups):
            sl = slice(g * fg, (g + 1) * fg)
            tv = t_ref[:, sl].astype(F32)
            dyv = dy_refs[g][...].astype(F32)
            r = lax.rsqrt(jnp.mean(tv * tv, axis=-1, keepdims=True) + EPS)
            gw = dyv * w_ref[:, sl]
            c = jnp.mean(gw * tv, axis=-1, keepdims=True)
            dt = r * gw - tv * (r * r * r * c)
            if has_res:
                dt = dt + res_ref[:, sl]
            dt_ref[:, sl] = dt.astype(dt_ref.dtype)
            dw_ref[:, sl] += jnp.sum(dyv * tv * r, axis=0, keepdims=True)

    row = pl.BlockSpec((tr, f), lambda i: (i, 0))
    grow = pl.BlockSpec((tr, fg), lambda i: (i, 0))
    wsp = pl.BlockSpec((1, f), lambda i: (0, 0))
    return pl.pallas_call(
        body, name=name, grid=(r_ // tr,),
        in_specs=[row, wsp] + [grow] * groups + ([row] if has_res else []), out_specs=[row, wsp],
        out_shape=[jax.ShapeDtypeStruct((r_, f), out_dtype), jax.ShapeDtypeStruct((1, f), F32)],
        compiler_params=_params("arbitrary"),
    )(*((t, w.reshape(1, f)) + tuple(dys) + ((res,) if has_res else ())))


def _hnorm_fwd(o, w, name="attn_out_norm"):
    h, s_, v = o.shape
    tr = _row_tile(s_)

    def body(o_ref, w_ref, y_ref):
        ss = jnp.sum(o_ref[0] * o_ref[0], axis=-1, keepdims=True)
        for i in range(1, h):
            ss = ss + jnp.sum(o_ref[i] * o_ref[i], axis=-1, keepdims=True)
        r = lax.rsqrt(ss * (1.0 / (h * v)) + EPS)
        for i in range(h):
            y_ref[i] = (o_ref[i] * r * w_ref[i]).astype(y_ref.dtype)

    blk = pl.BlockSpec((h, tr, v), lambda i: (0, i, 0))
    wsp = pl.BlockSpec((h, 1, v), lambda i: (0, 0, 0))
    return pl.pallas_call(
        body, name=name, grid=(s_ // tr,), in_specs=[blk, wsp], out_specs=blk,
        out_shape=jax.ShapeDtypeStruct(o.shape, MXU_DTYPE), compiler_params=_params("parallel"),
    )(o, w)


def _hnorm_bwd(o, w, dy, name="attn_out_norm_bwd"):
    h, s_, v = o.shape
    tr = _row_tile(s_)

    def body(o_ref, w_ref, dy_ref, do_ref, delta_ref, dw_ref):
        @pl.when(pl.program_id(0) == 0)
        def _():
            dw_ref[...] = jnp.zeros_like(dw_ref)

        ss = jnp.zeros((tr, 1), F32)
        cc = jnp.zeros((tr, 1), F32)
        for i in range(h):
            ov = o_ref[i]
            ss = ss + jnp.sum(ov * ov, axis=-1, keepdims=True)
            cc = cc + jnp.sum(dy_ref[i] * w_ref[i] * ov, axis=-1, keepdims=True)
        r = lax.rsqrt(ss * (1.0 / (h * v)) + EPS)
        c = cc * (1.0 / (h * v))
        for i in range(h):
            ov = o_ref[i]
            dyv = dy_ref[i]
            dov = r * dyv * w_ref[i] - ov * (r * r * r * c)
            do_ref[i] = dov.astype(do_ref.dtype)
            delta_ref[i] = jnp.sum(dov * ov, axis=-1, keepdims=True)
            dw_ref[i] += jnp.sum(dyv * ov * r, axis=0, keepdims=True)

    blk = pl.BlockSpec((h, tr, v), lambda i: (0, i, 0))
    wsp = pl.BlockSpec((h, 1, v), lambda i: (0, 0, 0))
    return pl.pallas_call(
        body, name=name, grid=(s_ // tr,), in_specs=[blk, wsp, blk],
        out_specs=[blk, pl.BlockSpec((h, tr, 1), lambda i: (0, i, 0)), wsp],
        out_shape=[jax.ShapeDtypeStruct(o.shape, MXU_DTYPE), jax.ShapeDtypeStruct((h, s_, 1), F32),
                   jax.ShapeDtypeStruct((h, 1, v), F32)],
        compiler_params=_params("arbitrary"),
    )(o, w, dy)


def _loss_head(ffn, h1, target, w, name="loss_head"):
    r_, f = ffn.shape
    tr = _row_tile(r_)

    def body(ffn_ref, h1_ref, tg_ref, w_ref, loss_ref, dy_ref, dffn_ref, dw_ref):
        @pl.when(pl.program_id(0) == 0)
        def _():
            dw_ref[...] = jnp.zeros_like(dw_ref)
            loss_ref[...] = jnp.zeros_like(loss_ref)

        tv = ffn_ref[...]
        wv = w_ref[...]
        r = lax.rsqrt(jnp.mean(tv * tv, axis=-1, keepdims=True) + EPS)
        tn = tv * r
        e = h1_ref[...] + tn * wv - tg_ref[...]
        tot = jnp.sum(jnp.sum(e * e, axis=1, keepdims=True), axis=0, keepdims=True) * (0.5 / f)
        loss_ref[...] += tot + jnp.zeros_like(loss_ref)
        dyv = e * (1.0 / f)
        dy_ref[...] = dyv
        gw = dyv * wv
        c = jnp.mean(gw * tv, axis=-1, keepdims=True)
        dffn_ref[...] = (r * gw - tv * (r * r * r * c)).astype(dffn_ref.dtype)
        dw_ref[...] += jnp.sum(dyv * tn, axis=0, keepdims=True)

    row = pl.BlockSpec((tr, f), lambda i: (i, 0))
    wsp = pl.BlockSpec((1, f), lambda i: (0, 0))
    lsp = pl.BlockSpec((1, LANE), lambda i: (0, 0))
    return pl.pallas_call(
        body, name=name, grid=(r_ // tr,),
        in_specs=[row, row, row, wsp], out_specs=[lsp, row, row, wsp],
        out_shape=[jax.ShapeDtypeStruct((1, LANE), F32), jax.ShapeDtypeStruct((r_, f), F32),
                   jax.ShapeDtypeStruct((r_, f), MXU_DTYPE), jax.ShapeDtypeStruct((1, f), F32)],
        compiler_params=_params("arbitrary"),
    )(ffn, h1, target, w.reshape(1, f))


def _rot_matrix():
    p = np.zeros((ROPE, ROPE), np.float32)
    for i in range(HALF):
        p[i + HALF, i] = -1.0
        p[i, i + HALF] = 1.0
    return jnp.asarray(p, BF16)


def _rope_val(r, c2, s2, rot):
    return r * c2 + _exact_dot(r, rot, 1, 0, True) * s2


def _q_prep(q, cos2, sin2, scale, name):
    h, s_, _ = q.shape
    tr = _pick(s_, (1024, 512, 256, 128, 64, 32, 16, 8))

    def body(q_ref, c_ref, s_ref, rot_ref, o_ref):
        x = q_ref[...]
        o_ref[:, :NOPE] = (x[:, :NOPE] * scale).astype(o_ref.dtype)
        o_ref[:, NOPE:] = (_rope_val(x[:, NOPE:], c_ref[...], s_ref[...], rot_ref[...]) * scale).astype(o_ref.dtype)

    blk = pl.BlockSpec((None, tr, QK), lambda hh, i: (hh, i, 0))
    csp = pl.BlockSpec((tr, ROPE), lambda hh, i: (i, 0))
    return pl.pallas_call(
        body, name=name, grid=(h, s_ // tr),
        in_specs=[blk, csp, csp, pl.BlockSpec((ROPE, ROPE), lambda hh, i: (0, 0))], out_specs=blk,
        out_shape=jax.ShapeDtypeStruct(q.shape, MXU_DTYPE), compiler_params=_params("parallel", "parallel"),
    )(q, cos2, sin2, _rot_matrix())


def _kv_prep(kv, small, cos2, sin2, name="kv_prep"):
    h, s_, _ = kv.shape
    tr = _row_tile(s_)

    def body(kv_ref, sm_ref, c_ref, s_ref, rot_ref, k_ref, v_ref):
        kr = _rope_val(sm_ref[:, :ROPE], c_ref[...], s_ref[...], rot_ref[...]).astype(k_ref.dtype)
        for i in range(h):
            k_ref[i, :, :NOPE] = kv_ref[i, :, :NOPE].astype(k_ref.dtype)
            k_ref[i, :, NOPE:] = kr
            v_ref[i] = kv_ref[i, :, NOPE:].astype(v_ref.dtype)

    csp = pl.BlockSpec((tr, ROPE), lambda i: (i, 0))
    return pl.pallas_call(
        body, name=name, grid=(s_ // tr,),
        in_specs=[pl.BlockSpec((h, tr, NOPE + VDIM), lambda i: (0, i, 0)), pl.BlockSpec((tr, LANE), lambda i: (i, 0)),
                  csp, csp, pl.BlockSpec((ROPE, ROPE), lambda i: (0, 0))],
        out_specs=[pl.BlockSpec((h, tr, QK), lambda i: (0, i, 0)), pl.BlockSpec((h, tr, VDIM), lambda i: (0, i, 0))],
        out_shape=[jax.ShapeDtypeStruct((h, s_, QK), MXU_DTYPE), jax.ShapeDtypeStruct((h, s_, VDIM), MXU_DTYPE)],
        compiler_params=_params("parallel"),
    )(kv, small, cos2, sin2, _rot_matrix())


def _dkv_post(dk, dv, ddt, cos2, nsin2, name="dkv_post"):
    h, s_, _ = dk.shape
    tr = _row_tile(s_)

    def body(dk_ref, dv_ref, ddt_ref, c_ref, s_ref, rot_ref, dkv_ref, dsm_ref):
        acc = dk_ref[0, :, NOPE:]
        for i in range(1, h):
            acc = acc + dk_ref[i, :, NOPE:]
        dsm_ref[:, :ROPE] = _rope_val(acc, c_ref[...], s_ref[...], rot_ref[...]).astype(dsm_ref.dtype)
        dsm_ref[:, ROPE:] = ddt_ref[:, ROPE:].astype(dsm_ref.dtype)
        for i in range(h):
            dkv_ref[i, :, :NOPE] = dk_ref[i, :, :NOPE].astype(dkv_ref.dtype)
            dkv_ref[i, :, NOPE:] = dv_ref[i].astype(dkv_ref.dtype)

    csp = pl.BlockSpec((tr, ROPE), lambda i: (i, 0))
    return pl.pallas_call(
        body, name=name, grid=(s_ // tr,),
        in_specs=[pl.BlockSpec((h, tr, QK), lambda i: (0, i, 0)), pl.BlockSpec((h, tr, VDIM), lambda i: (0, i, 0)),
                  pl.BlockSpec((tr, LANE), lambda i: (i, 0)), csp, csp, pl.BlockSpec((ROPE, ROPE), lambda i: (0, 0))],
        out_specs=[pl.BlockSpec((h, tr, NOPE + VDIM), lambda i: (0, i, 0)), pl.BlockSpec((tr, LANE), lambda i: (i, 0))],
        out_shape=[jax.ShapeDtypeStruct((h, s_, NOPE + VDIM), MXU_DTYPE), jax.ShapeDtypeStruct((s_, LANE), MXU_DTYPE)],
        compiler_params=_params("parallel"),
    )(dk, dv, ddt, cos2, nsin2, _rot_matrix())


def _attn_tile(s):
    return 1024 if s % 2048 == 0 else s // 2


def _pairs(n, by_key):
    if by_key:
        pr = [(i, j) for j in range(n) for i in range(j, n)]
    else:
        pr = [(i, j) for i in range(n) for j in range(i + 1)]
    return (jnp.asarray([p[0] for p in pr], jnp.int32), jnp.asarray([p[1] for p in pr], jnp.int32))


ATTN_ROW_GROUPS = 4


def _row_groups(t, diag):
    tg = t // ATTN_ROW_GROUPS
    out = []
    for r in range(ATTN_ROW_GROUPS):
        nc = (r + 1) * tg if diag else t
        mask = None
        if diag:
            mask = (lax.broadcasted_iota(jnp.int32, (tg, nc), 1)
                    <= lax.broadcasted_iota(jnp.int32, (tg, nc), 0) + r * tg)
        out.append((slice(r * tg, (r + 1) * tg), nc, mask))
    return out


def _flash_specs(t, dk, dv):
    qsp = pl.BlockSpec((None, t, dk), lambda hh, p, qi, kj: (hh, qi[p], 0))
    ksp = pl.BlockSpec((None, t, dk), lambda hh, p, qi, kj: (hh, kj[p], 0))
    vsp = pl.BlockSpec((None, t, dv), lambda hh, p, qi, kj: (hh, kj[p], 0))
    osp = pl.BlockSpec((None, t, dv), lambda hh, p, qi, kj: (hh, qi[p], 0))
    lsp = pl.BlockSpec((None, t, 1), lambda hh, p, qi, kj: (hh, qi[p], 0))
    return qsp, ksp, vsp, osp, lsp


def _flash_fwd(q, k, v, name="flash_fwd"):
    h, s_, dk = q.shape
    dv = v.shape[-1]
    t = _attn_tile(s_)
    n = s_ // t
    qi, kj = _pairs(n, False)

    def body(qi_ref, kj_ref, q_ref, k_ref, v_ref, o_ref, lse_ref, m_s, l_s, acc):
        p_ = pl.program_id(1)
        i, j = qi_ref[p_], kj_ref[p_]

        @pl.when(j == 0)
        def _():
            m_s[...] = jnp.full_like(m_s, -jnp.inf)
            l_s[...] = jnp.zeros_like(l_s)
            acc[...] = jnp.zeros_like(acc)

        def update(diag):
            for rs, nc, mask in _row_groups(t, diag):
                sc = _dot(q_ref[rs, :], k_ref[0:nc, :], 1, 1)
                if mask is not None:
                    sc = jnp.where(mask, sc, -jnp.inf)
                m_old = m_s[rs, :]
                m_new = jnp.maximum(m_old, jnp.max(sc, axis=1, keepdims=True))
                alpha = jnp.exp(m_old - m_new)
                p = jnp.exp(sc - m_new)
                l_s[rs, :] = alpha * l_s[rs, :] + jnp.sum(p, axis=1, keepdims=True)
                acc[rs, :] = alpha * acc[rs, :] + _dot(_mx(p), v_ref[0:nc, :], 1, 0)
                m_s[rs, :] = m_new

        @pl.when(j < i)
        def _():
            update(False)

        @pl.when(j == i)
        def _():
            update(True)
            o_ref[...] = acc[...] / l_s[...]
            lse_ref[...] = m_s[...] + jnp.log(l_s[...])

    qsp, ksp, vsp, osp, lsp = _flash_specs(t, dk, dv)
    gs = pltpu.PrefetchScalarGridSpec(
        num_scalar_prefetch=2, grid=(h, qi.shape[0]), in_specs=[qsp, ksp, vsp], out_specs=[osp, lsp],
        scratch_shapes=[pltpu.VMEM((t, 1), F32), pltpu.VMEM((t, 1), F32), pltpu.VMEM((t, dv), F32)])
    return pl.pallas_call(
        body, name=name, grid_spec=gs,
        out_shape=[jax.ShapeDtypeStruct((h, s_, dv), F32), jax.ShapeDtypeStruct((h, s_, 1), F32)],
        compiler_params=_params("parallel", "arbitrary"),
    )(qi, kj, q, k, v)


def _flash_bwd(q, k, v, do, lse, delta, name="flash_bwd"):
    h, s_, dk = q.shape
    dv = v.shape[-1]
    t = _attn_tile(s_)
    tg = t // ATTN_ROW_GROUPS
    n = s_ // t
    qi, kj = _pairs(n, True)

    def body(qi_ref, kj_ref, q_ref, k_ref, v_ref, do_ref, lse_ref, delta_ref, dq_ref, dk_ref, dv_ref, dk_acc, dv_acc):
        p_ = pl.program_id(1)
        i, j = qi_ref[p_], kj_ref[p_]

        @pl.when(p_ == 0)
        def _():
            dq_ref[...] = jnp.zeros_like(dq_ref)

        def update(diag):
            for g, (rs, nc, mask) in enumerate(_row_groups(t, diag)):
                sc = _dot(q_ref[rs, :], k_ref[0:nc, :], 1, 1)
                if mask is not None:
                    sc = jnp.where(mask, sc, -jnp.inf)
                p = jnp.exp(sc - lse_ref[rs, :])
                dob = _mx(do_ref[rs, :])
                dv_acc[0:nc, :] += _dot(_mx(p), dob, 0, 0)
                dp = _dot(dob, v_ref[0:nc, :], 1, 1)
                dsb = _mx(p * (dp - delta_ref[rs, :]))
                dk_acc[0:nc, :] += _dot(dsb, q_ref[rs, :], 0, 0)
                rows = pl.ds(pl.multiple_of(i * t + g * tg, tg), tg)
                dq_ref[rows, :] += _dot(dsb, k_ref[0:nc, :], 1, 0)

        @pl.when(i == j)
        def _():
            dk_acc[...] = jnp.zeros_like(dk_acc)
            dv_acc[...] = jnp.zeros_like(dv_acc)
            update(True)

        @pl.when(i > j)
        def _():
            update(False)

        @pl.when(i == n - 1)
        def _():
            dk_ref[...] = dk_acc[...]
            dv_ref[...] = dv_acc[...]

    qsp, ksp, vsp, osp, lsp = _flash_specs(t, dk, dv)
    dqsp = pl.BlockSpec((None, s_, dk), lambda hh, p, qi, kj: (hh, 0, 0))
    gs = pltpu.PrefetchScalarGridSpec(
        num_scalar_prefetch=2, grid=(h, qi.shape[0]), in_specs=[qsp, ksp, vsp, osp, lsp, lsp],
        out_specs=[dqsp, ksp, vsp],
        scratch_shapes=[pltpu.VMEM((t, dk), F32), pltpu.VMEM((t, dv), F32)])
    return pl.pallas_call(
        body, name=name, grid_spec=gs,
        out_shape=[jax.ShapeDtypeStruct((h, s_, dk), F32), jax.ShapeDtypeStruct((h, s_, dk), F32),
                   jax.ShapeDtypeStruct((h, s_, dv), F32)],
        compiler_params=_params("parallel", "arbitrary"),
    )(qi, kj, q, k, v, do, lse, delta)


HALO = 8


def _conv_specs(s_, c, tr, after):
    main = pl.BlockSpec((tr, c), lambda i: (i, 0))
    per = tr // HALO
    if after:
        halo = pl.BlockSpec((HALO, c), lambda i: (jnp.minimum((i + 1) * per, s_ // HALO - 1), 0))
    else:
        halo = pl.BlockSpec((HALO, c), lambda i: (jnp.maximum(i * per - 1, 0), 0))
    return main, halo


def _fill_before(ext, t_ref, h_ref, tr):
    ext[0:HALO, :] = jnp.where(pl.program_id(0) > 0, h_ref[...], 0.0)
    ext[HALO:HALO + tr, :] = t_ref[...]


def _taps(ext, w_ref, tr):
    base = HALO - (CONV_K - 1)
    acc = ext[base:base + tr, :] * w_ref[0:1, :]
    for k in range(1, CONV_K):
        acc = acc + ext[base + k:base + k + tr, :] * w_ref[k:k + 1, :]
    return acc


def _conv_fwd(t, w, b, name="conv_fwd"):
    s_, c = t.shape
    tr = _row_tile(s_)

    def body(t_ref, h_ref, w_ref, b_ref, o_ref, ext):
        _fill_before(ext, t_ref, h_ref, tr)
        o_ref[...] = _silu(_taps(ext, w_ref, tr) + b_ref[...])

    main, halo = _conv_specs(s_, c, tr, False)
    return pl.pallas_call(
        body, name=name, grid=(s_ // tr,),
        in_specs=[main, halo, pl.BlockSpec((CONV_K, c), lambda i: (0, 0)), pl.BlockSpec((1, c), lambda i: (0, 0))],
        out_specs=main, out_shape=jax.ShapeDtypeStruct((s_, c), F32),
        scratch_shapes=[pltpu.VMEM((tr + HALO, c), F32)], compiler_params=_params("parallel"),
    )(t, t, w, b)


def _conv_bwd_pre(t, w, b, dact, name="conv_bwd_pre"):
    s_, c = t.shape
    tr = _row_tile(s_)

    def body(t_ref, h_ref, w_ref, b_ref, da_ref, dpre_ref, dwb_ref, ext):
        @pl.when(pl.program_id(0) == 0)
        def _():
            dwb_ref[...] = jnp.zeros_like(dwb_ref)

        _fill_before(ext, t_ref, h_ref, tr)
        dpre = da_ref[...] * _dsilu(_taps(ext, w_ref, tr) + b_ref[...])
        dpre_ref[...] = dpre
        base = HALO - (CONV_K - 1)
        for k in range(CONV_K):
            dwb_ref[k:k + 1, :] += jnp.sum(dpre * ext[base + k:base + k + tr, :], axis=0, keepdims=True)
        dwb_ref[CONV_K:CONV_K + 1, :] += jnp.sum(dpre, axis=0, keepdims=True)

    main, halo = _conv_specs(s_, c, tr, False)
    return pl.pallas_call(
        body, name=name, grid=(s_ // tr,),
        in_specs=[main, halo, pl.BlockSpec((CONV_K, c), lambda i: (0, 0)), pl.BlockSpec((1, c), lambda i: (0, 0)), main],
        out_specs=[main, pl.BlockSpec((8, c), lambda i: (0, 0))],
        out_shape=[jax.ShapeDtypeStruct((s_, c), F32), jax.ShapeDtypeStruct((8, c), F32)],
        scratch_shapes=[pltpu.VMEM((tr + HALO, c), F32)], compiler_params=_params("arbitrary"),
    )(t, t, w, b, dact)


def _conv_bwd_in(dpre, w, name="conv_bwd_in"):
    s_, c = dpre.shape
    tr = _row_tile(s_)
    nt = s_ // tr

    def body(d_ref, h_ref, w_ref, o_ref, ext):
        ext[0:tr, :] = d_ref[...]
        ext[tr:tr + HALO, :] = jnp.where(pl.program_id(0) < nt - 1, h_ref[...], 0.0)
        acc = ext[CONV_K - 1:CONV_K - 1 + tr, :] * w_ref[0:1, :]
        for k in range(1, CONV_K):
            acc = acc + ext[CONV_K - 1 - k:CONV_K - 1 - k + tr, :] * w_ref[k:k + 1, :]
        o_ref[...] = acc.astype(o_ref.dtype)

    main, halo = _conv_specs(s_, c, tr, True)
    return pl.pallas_call(
        body, name=name, grid=(nt,),
        in_specs=[main, halo, pl.BlockSpec((CONV_K, c), lambda i: (0, 0))],
        out_specs=main, out_shape=jax.ShapeDtypeStruct((s_, c), MXU_DTYPE),
        scratch_shapes=[pltpu.VMEM((tr + HALO, c), F32)], compiler_params=_params("parallel"),
    )(dpre, dpre, w)


def _ssd_chunk_common(dt_ref, dtt_ref, br_ref, bc_ref, ar_ref, ac_ref):
    li = lax.broadcasted_iota(jnp.int32, (CHUNK, CHUNK), 0)
    si = lax.broadcasted_iota(jnp.int32, (CHUNK, CHUNK), 1)
    lower = li >= si
    lower_b = lower.astype(BF16)
    upper_b = (li <= si).astype(BF16)
    zr = dt_ref[...] + br_ref[...]
    dtc = _softplus(zr)
    a_row = -jnp.exp(ar_ref[...])
    acum = _exact_dot(lower_b, dtc * a_row, 1, 0, False)
    dtt = _softplus(dtt_ref[...] + bc_ref[...])
    acum_t = _exact_dot(dtt * (-jnp.exp(ac_ref[...])), upper_b, 1, 0, True)
    return lower, upper_b, zr, dtc, a_row, acum, acum_t


def _head_terms(h, lower, dtc, acum, acum_t):
    lane = lax.broadcasted_iota(jnp.int32, (1, LANE), 1)
    sub = lax.broadcasted_iota(jnp.int32, (SSD_H, 1), 0)
    rowid = lax.broadcasted_iota(jnp.int32, (CHUNK, 1), 0)
    oh = (lane == HEAD_LANE + h).astype(F32)
    acol = jnp.sum(acum * oh, axis=1, keepdims=True)
    dcol = jnp.sum(dtc * oh, axis=1, keepdims=True)
    arow = jnp.sum(acum_t * (sub == h).astype(F32), axis=0, keepdims=True)
    alast = jnp.sum(jnp.where(rowid == CHUNK - 1, acol, 0.0), axis=0, keepdims=True)
    decay = jnp.exp(jnp.where(lower, acol - arow, -jnp.inf))
    return oh, acol, dcol, alast, decay


SSD_PAIRS = SSD_H // 2
PAIRS_PER_GROUP = SSD_E // 2


def _ps(q):
    return slice(q * LANE, (q + 1) * LANE)


def _gs(off, g):
    return slice(off + g * SSD_N, off + (g + 1) * SSD_N)


def _lanes(c0, c1):
    return jnp.where(lax.broadcasted_iota(jnp.int32, (1, LANE), 1) < SSD_P, c0, c1)


def _rows(c0, c1):
    return jnp.where(lax.broadcasted_iota(jnp.int32, (LANE, 1), 0) < SSD_P, c0, c1)


def _lane_halves(t):
    first = lax.broadcasted_iota(jnp.int32, (1, LANE), 1) < SSD_P
    return (jnp.sum(jnp.where(first, t, 0.0), axis=1, keepdims=True),
            jnp.sum(jnp.where(first, 0.0, t), axis=1, keepdims=True))


def _ssd_in_specs(rev):
    def ci(c):
        return c if rev is None else rev - c
    return [pl.BlockSpec((CHUNK, CONV_DIM), lambda c: (ci(c), 0)),
            pl.BlockSpec((CHUNK, LANE), lambda c: (ci(c), 0)),
            pl.BlockSpec((SSD_H, CHUNK), lambda c: (0, ci(c))),
            pl.BlockSpec((1, LANE), lambda c: (0, 0)), pl.BlockSpec((SSD_H, 1), lambda c: (0, 0)),
            pl.BlockSpec((1, LANE), lambda c: (0, 0)), pl.BlockSpec((SSD_H, 1), lambda c: (0, 0)),
            pl.BlockSpec((SSD_PAIRS, 1, LANE), lambda c: (0, 0, 0))]


def _ssd_fwd(xbc, small, dtt, bias_r, bias_c, alog_r, alog_c, dsk, name="ssd_fwd"):
    s_ = xbc.shape[0]
    nc = s_ // CHUNK

    def body(x_ref, dt_ref, dtt_ref, br_ref, bc_ref, ar_ref, ac_ref, dsk_ref, y_ref, prev_ref, state):
        @pl.when(pl.program_id(0) == 0)
        def _():
            state[...] = jnp.zeros_like(state)

        lower, _, _, dtc, _, acum, acum_t = _ssd_chunk_common(dt_ref, dtt_ref, br_ref, bc_ref, ar_ref, ac_ref)
        for g in range(SSD_G):
            bb = _mx(x_ref[:, _gs(B_OFF, g)])
            cb_ = _mx(x_ref[:, _gs(C_OFF, g)])
            cbm = _dot(cb_, bb, 1, 1)
            for e in range(PAIRS_PER_GROUP):
                q = g * PAIRS_PER_GROUP + e
                _, acol0, dcol0, alast0, decay0 = _head_terms(2 * q, lower, dtc, acum, acum_t)
                _, acol1, dcol1, alast1, decay1 = _head_terms(2 * q + 1, lower, dtc, acum, acum_t)
                x = x_ref[:, _ps(q)]
                xdt = x * _lanes(dcol0, dcol1)
                xb = _mx(xdt)
                yd = _lanes(_dot(_mx(cbm * decay0), xb, 1, 0), _dot(_mx(cbm * decay1), xb, 1, 0))
                prev = state[q]
                prev_ref[0, q] = prev
                yo = _dot(cb_, _mx(prev), 1, 1) * _lanes(jnp.exp(acol0), jnp.exp(acol1))
                ds = _lanes(jnp.exp(alast0 - acol0), jnp.exp(alast1 - acol1))
                st = _dot(_mx(xdt * ds), bb, 0, 0)
                state[q] = prev * _rows(jnp.exp(alast0), jnp.exp(alast1)) + st
                y_ref[:, _ps(q)] = yd + yo + x * dsk_ref[q]

    psp = pl.BlockSpec((1, SSD_PAIRS, LANE, SSD_N), lambda c: (c, 0, 0, 0))
    return pl.pallas_call(
        body, name=name, grid=(nc,),
        in_specs=_ssd_in_specs(None), out_specs=[pl.BlockSpec((CHUNK, SSD_W), lambda c: (c, 0)), psp],
        out_shape=[jax.ShapeDtypeStruct((s_, SSD_W), F32),
                   jax.ShapeDtypeStruct((nc, SSD_PAIRS, LANE, SSD_N), F32)],
        scratch_shapes=[pltpu.VMEM((SSD_PAIRS, LANE, SSD_N), F32)],
        compiler_params=_params("arbitrary"),
    )(xbc, small, dtt, bias_r, bias_c, alog_r, alog_c, dsk)


def _ssd_bwd(xbc, small, dtt, bias_r, bias_c, alog_r, alog_c, dsk, prev, dy, name="ssd_bwd"):
    s_ = xbc.shape[0]
    nc = s_ // CHUNK

    def body(x_ref, dt_ref, dtt_ref, br_ref, bc_ref, ar_ref, ac_ref, dsk_ref, prev_ref, dy_ref,
             dx_ref, ddt_ref, dpar_ref, dstate):
        @pl.when(pl.program_id(0) == 0)
        def _():
            dstate[...] = jnp.zeros_like(dstate)
            dpar_ref[...] = jnp.zeros_like(dpar_ref)

        lower, upper_b, zr, dtc, a_row, acum, acum_t = _ssd_chunk_common(
            dt_ref, dtt_ref, br_ref, bc_ref, ar_ref, ac_ref)
        strict = (lax.broadcasted_iota(jnp.int32, (CHUNK, CHUNK), 1)
                  < lax.broadcasted_iota(jnp.int32, (CHUNK, CHUNK), 0))
        strict_b = strict.astype(BF16)
        col2 = lax.broadcasted_iota(jnp.int32, (CHUNK, 2 * CHUNK), 1)
        strict2 = (jnp.where(col2 >= CHUNK, col2 - CHUNK, col2)
                   < lax.broadcasted_iota(jnp.int32, (CHUNK, 2 * CHUNK), 0))
        da_in = jnp.zeros((CHUNK, LANE), F32)
        r_off = jnp.zeros((CHUNK, LANE), F32)
        c_int = jnp.zeros((CHUNK, LANE), F32)
        c_row = jnp.zeros((1, LANE), F32)
        ddt = jnp.zeros((CHUNK, LANE), F32)
        dskip = jnp.zeros((1, LANE), F32)
        for g in range(SSD_G):
            bb = _mx(x_ref[:, _gs(B_OFF, g)])
            cb_ = _mx(x_ref[:, _gs(C_OFF, g)])
            cbm = _dot(cb_, bb, 1, 1)
            dcb = jnp.zeros((CHUNK, CHUNK), F32)
            dc_acc = jnp.zeros((CHUNK, SSD_N), F32)
            db_acc = jnp.zeros((CHUNK, SSD_N), F32)
            for e in range(PAIRS_PER_GROUP):
                q = g * PAIRS_PER_GROUP + e
                oh0, acol0, dcol0, alast0, decay0 = _head_terms(2 * q, lower, dtc, acum, acum_t)
                oh1, acol1, dcol1, alast1, decay1 = _head_terms(2 * q + 1, lower, dtc, acum, acum_t)
                x = x_ref[:, _ps(q)]
                dy = dy_ref[:, _ps(q)]
                dcol = _lanes(dcol0, dcol1)
                xdt = x * dcol
                xb = _mx(xdt)
                eacol = _lanes(jnp.exp(acol0), jnp.exp(acol1))
                ds = _lanes(jnp.exp(alast0 - acol0), jnp.exp(alast1 - acol1))
                ealast = _rows(jnp.exp(alast0), jnp.exp(alast1))
                dyb = _mx(dy)
                dyb0, dyb1 = _mx(_lanes(dy, 0.0)), _mx(_lanes(0.0, dy))
                dsh = dstate[q]
                dshb = _mx(dsh)
                prev = prev_ref[0, q]
                prevb = _mx(prev)
                dxdt_inter = ds * _dot(bb, dshb, 1, 1)
                dxdt = _lanes(_dot(_mx(cbm * decay0), dyb, 0, 0), _dot(_mx(cbm * decay1), dyb, 0, 0)) + dxdt_inter
                dwl0 = _dot(dyb0, xb, 1, 1) * decay0
                dwl1 = _dot(dyb1, xb, 1, 1) * decay1
                dcb = dcb + dwl0 + dwl1
                dyeb = _mx(dy * eacol)
                dc_acc = dc_acc + _dot(dyeb, prevb, 1, 0)
                db_acc = db_acc + _dot(_mx(xdt * ds), dshb, 1, 0)
                dstate[q] = _dot(dyeb, cb_, 0, 0) + ealast * dsh
                above = _exact_dot(upper_b, jnp.concatenate([dwl0 * cbm, dwl1 * cbm], axis=1), 1, 0, False)
                above = jnp.where(strict2, above, 0.0)
                da_in = (da_in + jnp.sum(above[:, :CHUNK], axis=1, keepdims=True) * oh0
                         + jnp.sum(above[:, CHUNK:], axis=1, keepdims=True) * oh1)
                y_off = _dot(cb_, prevb, 1, 1) * eacol
                r0, r1 = _lane_halves(dy * y_off)
                r_off = r_off + r0 * oh0 + r1 * oh1
                c0, c1 = _lane_halves(xdt * dxdt_inter)
                c_int = c_int + c0 * oh0 + c1 * oh1
                both = jnp.sum(dsh * prev, axis=1, keepdims=True) * ealast
                c_row = (c_row + jnp.sum(_rows(both, 0.0), axis=0, keepdims=True) * oh0
                         + jnp.sum(_rows(0.0, both), axis=0, keepdims=True) * oh1)
                t0, t1 = _lane_halves(dxdt * x)
                ddt = ddt + t0 * oh0 + t1 * oh1
                dx_ref[:, _ps(q)] = dxdt * dcol + dy * dsk_ref[q]
                k0, k1 = _lane_halves(dy * x)
                dskip = (dskip + jnp.sum(k0, axis=0, keepdims=True) * oh0 + jnp.sum(k1, axis=0, keepdims=True) * oh1)
            dcbb = _mx(dcb)
            dx_ref[:, _gs(C_OFF, g)] = dc_acc + _dot(dcbb, bb, 1, 0)
            dx_ref[:, _gs(B_OFF, g)] = db_acc + _dot(dcbb, cb_, 0, 0)
        da = (da_in + _exact_dot(upper_b, r_off, 1, 0, False) + _exact_dot(strict_b, c_int, 1, 0, False) + c_row)
        draw = (ddt + da * a_row) * _sigmoid(zr)
        ddt_ref[...] = draw
        dpar_ref[0:1, :] += jnp.sum(draw, axis=0, keepdims=True)
        dpar_ref[1:2, :] += jnp.sum(da * dtc, axis=0, keepdims=True) * a_row
        dpar_ref[2:3, :] += dskip

    rev = nc - 1
    psp = pl.BlockSpec((1, SSD_PAIRS, LANE, SSD_N), lambda c: (rev - c, 0, 0, 0))
    return pl.pallas_call(
        body, name=name, grid=(nc,),
        in_specs=_ssd_in_specs(rev) + [psp, pl.BlockSpec((CHUNK, SSD_W), lambda c: (rev - c, 0))],
        out_specs=[pl.BlockSpec((CHUNK, CONV_DIM), lambda c: (rev - c, 0)),
                   pl.BlockSpec((CHUNK, LANE), lambda c: (rev - c, 0)), pl.BlockSpec((8, LANE), lambda c: (0, 0))],
        out_shape=[jax.ShapeDtypeStruct((s_, CONV_DIM), F32), jax.ShapeDtypeStruct((s_, LANE), F32),
                   jax.ShapeDtypeStruct((8, LANE), F32)],
        scratch_shapes=[pltpu.VMEM((SSD_PAIRS, LANE, SSD_N), F32)],
        compiler_params=_params("arbitrary"),
    )(xbc, small, dtt, bias_r, bias_c, alog_r, alog_c, dsk, prev, dy)


GN = SSD_W // SSD_G


def _gated_norm_fwd(y, z, w, name="gated_norm_fwd"):
    s_, f = y.shape
    tr = _row_tile(s_)

    def body(y_ref, z_ref, w_ref, o_ref):
        for g in range(SSD_G):
            sl = slice(g * GN, (g + 1) * GN)
            gg = y_ref[:, sl] * _silu(z_ref[:, sl])
            r = lax.rsqrt(jnp.mean(gg * gg, axis=-1, keepdims=True) + EPS)
            o_ref[:, sl] = (gg * r * w_ref[:, sl]).astype(o_ref.dtype)

    row = pl.BlockSpec((tr, f), lambda i: (i, 0))
    wsp = pl.BlockSpec((1, f), lambda i: (0, 0))
    return pl.pallas_call(
        body, name=name, grid=(s_ // tr,), in_specs=[row, row, wsp], out_specs=row,
        out_shape=jax.ShapeDtypeStruct((s_, f), MXU_DTYPE), compiler_params=_params("parallel"),
    )(y, z, w.reshape(1, f))


def _gated_norm_bwd(y, z, w, dout, name="gated_norm_bwd"):
    s_, f = y.shape
    tr = _row_tile(s_)

    def body(y_ref, z_ref, w_ref, do_ref, dy_ref, dz_ref, dw_ref):
        @pl.when(pl.program_id(0) == 0)
        def _():
            dw_ref[...] = jnp.zeros_like(dw_ref)

        for g in range(SSD_G):
            sl = slice(g * GN, (g + 1) * GN)
            yv = y_ref[:, sl]
            zv = z_ref[:, sl]
            dov = do_ref[:, sl].astype(F32)
            sz = _silu(zv)
            gg = yv * sz
            r = lax.rsqrt(jnp.mean(gg * gg, axis=-1, keepdims=True) + EPS)
            gw = dov * w_ref[:, sl]
            c = jnp.mean(gw * gg, axis=-1, keepdims=True)
            dgg = r * gw - gg * (r * r * r * c)
            dy_ref[:, sl] = dgg * sz
            dz_ref[:, sl] = (dgg * yv * _dsilu(zv)).astype(dz_ref.dtype)
            dw_ref[:, sl] += jnp.sum(dov * gg * r, axis=0, keepdims=True)

    row = pl.BlockSpec((tr, f), lambda i: (i, 0))
    wsp = pl.BlockSpec((1, f), lambda i: (0, 0))
    return pl.pallas_call(
        body, name=name, grid=(s_ // tr,), in_specs=[row, row, wsp, row], out_specs=[row, row, wsp],
        out_shape=[jax.ShapeDtypeStruct((s_, f), F32), jax.ShapeDtypeStruct((s_, f), MXU_DTYPE),
                   jax.ShapeDtypeStruct((1, f), F32)],
        compiler_params=_params("arbitrary"),
    )(y, z, w.reshape(1, f), dout)


def _ffn_fwd(vv, w_gate, w_up, name="ffn_gate_up"):
    s_, d = vv.shape
    nb, _, f8 = w_gate.shape
    tm = _pick(s_, (1024, 512, 256, 128))

    def body(v_ref, wg_ref, wu_ref, g_ref, u_ref, a_ref):
        a = _mx(v_ref[...])
        g = _dot(a, _mx(wg_ref[...]), 1, 0)
        u = _dot(a, _mx(wu_ref[...]), 1, 0)
        g_ref[...] = g.astype(g_ref.dtype)
        u_ref[...] = u.astype(u_ref.dtype)
        a_ref[...] = (_silu(g) * u).astype(a_ref.dtype)

    wsp = pl.BlockSpec((None, d, f8), lambda j, i: (j, 0, 0))
    osp = pl.BlockSpec((None, tm, f8), lambda j, i: (j, i, 0))
    return pl.pallas_call(
        body, name=name, grid=(nb, s_ // tm),
        in_specs=[pl.BlockSpec((tm, d), lambda j, i: (i, 0)), wsp, wsp], out_specs=[osp] * 3,
        out_shape=[jax.ShapeDtypeStruct((nb, s_, f8), MXU_DTYPE)] * 3,
        compiler_params=_params("parallel", "parallel"),
    )(vv, w_gate, w_up)


def _ffn_bwd_act(dffn, w_down, gate, up, name="ffn_d_act"):
    s_, d = dffn.shape
    nb, f8, _ = w_down.shape
    tm = _pick(s_, (1024, 512, 256, 128))

    def body(d_ref, w_ref, g_ref, u_ref, dg_ref, du_ref):
        dact = _dot(_mx(d_ref[...]), _mx(w_ref[...]), 1, 1)
        g = g_ref[...].astype(F32)
        dg_ref[...] = (dact * u_ref[...].astype(F32) * _dsilu(g)).astype(dg_ref.dtype)
        du_ref[...] = (dact * _silu(g)).astype(du_ref.dtype)

    osp = pl.BlockSpec((None, tm, f8), lambda j, i: (j, i, 0))
    return pl.pallas_call(
        body, name=name, grid=(nb, s_ // tm),
        in_specs=[pl.BlockSpec((tm, d), lambda j, i: (i, 0)), pl.BlockSpec((None, f8, d), lambda j, i: (j, 0, 0)),
                  osp, osp],
        out_specs=[osp, osp], out_shape=[jax.ShapeDtypeStruct((nb, s_, f8), MXU_DTYPE)] * 2,
        compiler_params=_params("parallel", "parallel"),
    )(dffn, w_down, gate, up)


def _ffn_bwd_in(dgate, w_gate, dup, w_up, name="ffn_d_in"):
    nb, s_, f8 = dgate.shape
    d = w_gate.shape[1]
    tm = _pick(s_, (1024, 512, 256, 128))
    tn = _pick(d, (1024, 512, 256, 128))

    def body(dg_ref, wg_ref, du_ref, wu_ref, o_ref, acc):
        j = pl.program_id(2)
        part = _dot(_mx(dg_ref[...]), _mx(wg_ref[...]), 1, 1) + _dot(_mx(du_ref[...]), _mx(wu_ref[...]), 1, 1)

        @pl.when(j == 0)
        def _():
            acc[...] = part

        @pl.when(j > 0)
        def _():
            acc[...] += part

        @pl.when(j == nb - 1)
        def _():
            o_ref[...] = acc[...]

    asp = pl.BlockSpec((None, tm, f8), lambda i, n, j: (j, i, 0))
    wsp = pl.BlockSpec((None, tn, f8), lambda i, n, j: (j, n, 0))
    return pl.pallas_call(
        body, name=name, grid=(s_ // tm, d // tn, nb),
        in_specs=[asp, wsp, asp, wsp], out_specs=pl.BlockSpec((tm, tn), lambda i, n, j: (i, n)),
        out_shape=jax.ShapeDtypeStruct((s_, d), F32), scratch_shapes=[pltpu.VMEM((tm, tn), F32)],
        compiler_params=_params("parallel", "parallel", "arbitrary"),
    )(dgate, w_gate, dup, w_up)


def _adam_math(g, w, m, v):
    m2 = ADAM_B1 * m + (1.0 - ADAM_B1) * g
    v2 = ADAM_B2 * v + (1.0 - ADAM_B2) * (g * g)
    m_hat = m2 / (1.0 - ADAM_B1 ** ADAM_STEP)
    v_hat = v2 / (1.0 - ADAM_B2 ** ADAM_STEP)
    delta = -ADAM_LR * (m_hat / (jnp.sqrt(v_hat) + ADAM_EPS) + ADAM_WD * w)
    return delta, m2, v2


def _adamw(parts, w, m, v, name="adamw"):
    nd, r_, c = parts.shape
    tr = _pick(r_, (128, 64, 32, 16, 8))

    def body(p_ref, w_ref, m_ref, v_ref, g_ref, d_ref, m2_ref, v2_ref):
        g = p_ref[0].astype(F32)
        for i in range(1, nd):
            g = g + p_ref[i].astype(F32)
        delta, m2, v2 = _adam_math(g, w_ref[...], m_ref[...], v_ref[...])
        g_ref[...] = g
        d_ref[...] = delta
        m2_ref[...] = m2
        v2_ref[...] = v2

    row = pl.BlockSpec((tr, c), lambda i: (i, 0))
    psp = pl.BlockSpec((nd, tr, c), lambda i: (0, i, 0))
    return pl.pallas_call(
        body, name=name, grid=(r_ // tr,), in_specs=[psp, row, row, row], out_specs=[row] * 4,
        out_shape=[jax.ShapeDtypeStruct((r_, c), F32)] * 4, compiler_params=_params("parallel"),
    )(parts, w, m, v)


def _adamw_small(parts, w, m, v, name="adamw_small"):
    nd = parts.shape[0]

    def body(p_ref, w_ref, m_ref, v_ref, g_ref, d_ref, m2_ref, v2_ref):
        g = p_ref[0]
        for i in range(1, nd):
            g = g + p_ref[i]
        delta, m2, v2 = _adam_math(g, w_ref[...], m_ref[...], v_ref[...])
        g_ref[...] = g
        d_ref[...] = delta
        m2_ref[...] = m2
        v2_ref[...] = v2

    return pl.pallas_call(
        body, name=name, out_shape=[jax.ShapeDtypeStruct(w.shape, F32)] * 4,
        compiler_params=pltpu.CompilerParams(vmem_limit_bytes=VMEM_LIMIT_BYTES),
    )(parts, w, m, v)


_HBM = pl.BlockSpec(memory_space=pltpu.HBM)
_MESH = pl.DeviceIdType.MESH


def _all_gather(xs, name):
    na = len(xs)

    def body(*refs):
        x_refs, out_refs = refs[:na], refs[na:2 * na]
        send_sems, recv_sems, local_sems = refs[2 * na:]
        x, y, c = lax.axis_index("x"), lax.axis_index("y"), lax.axis_index("c")
        me, sibling = (x, y, c), (x, y, 1 - c)
        chips = [(1 - x, y), (x, 1 - y), (1 - x, 1 - y)]

        def slot(a, px, py, pc):
            return out_refs[a].at[4 * px + 2 * py + pc]

        def copy(a, k, block, to, src=None):
            return pltpu.make_async_remote_copy(
                src_ref=slot(a, *block) if src is None else src, dst_ref=slot(a, *block),
                send_sem=send_sems.at[a, k], recv_sem=recv_sems.at[a, k], device_id=to, device_id_type=_MESH)

        mine = [pltpu.make_async_copy(x_refs[a], slot(a, *me), local_sems.at[a]) for a in range(na)]
        started = []
        for a in range(na):
            mine[a].start()
            first = [copy(a, 0, me, sibling, src=x_refs[a])]
            first += [copy(a, 1 + j, me, (*chip, c), src=x_refs[a]) for j, chip in enumerate(chips)]
            for cp in first:
                cp.start()
            started += first
        for a in range(na):
            for j, chip in enumerate(chips):
                copy(a, 1 + j, (*chip, c), me).wait_recv()
                fwd = copy(a, 4 + j, (*chip, c), sibling)
                fwd.start()
                started.append(fwd)
        for a in range(na):
            copy(a, 0, sibling, me).wait_recv()
            for j, chip in enumerate(chips):
                copy(a, 4 + j, (*chip, 1 - c), me).wait_recv()
        for cp in started:
            cp.wait_send()
        for cp in mine:
            cp.wait()

    return pl.pallas_call(
        body, name=name, out_shape=[jax.ShapeDtypeStruct((N_DEV,) + t.shape, t.dtype) for t in xs],
        in_specs=[_HBM] * na, out_specs=[_HBM] * na,
        scratch_shapes=[pltpu.SemaphoreType.DMA((na, 7)), pltpu.SemaphoreType.DMA((na, 7)),
                        pltpu.SemaphoreType.DMA((na,))],
    )(*xs)


_SEM = pl.BlockSpec(memory_space=pltpu.SEMAPHORE)
_EFFECT = pltpu.SideEffectType.DATAFLOW_SIDE_EFFECTING


def _peers(x, y, c):
    out = []
    for k in range(1, N_DEV):
        px = 1 - x if k & 4 else x
        py = 1 - y if k & 2 else y
        pc = 1 - c if k & 1 else c
        out.append(((px, py, pc), 4 * px + 2 * py + pc))
    return out


def _push_copies(scatter, src_refs, land_refs, send_sems, recv_sems):
    x, y, c = lax.axis_index("x"), lax.axis_index("y"), lax.axis_index("c")
    me = 4 * x + 2 * y + c
    pairs = []
    for a, (src, land) in enumerate(zip(src_refs, land_refs)):
        for k, (peer, slot) in enumerate(_peers(x, y, c)):
            out_src = src.at[slot] if scatter else src
            si = a * (N_DEV - 1) + k
            send = pltpu.make_async_remote_copy(src_ref=out_src, dst_ref=land.at[me], send_sem=send_sems.at[si],
                                                recv_sem=recv_sems.at[si], device_id=peer, device_id_type=_MESH)
            recv = pltpu.make_async_remote_copy(src_ref=out_src, dst_ref=land.at[slot], send_sem=send_sems.at[si],
                                                recv_sem=recv_sems.at[si], device_id=peer, device_id_type=_MESH)
            pairs.append((send, recv))
    return pairs


def _push_start(srcs, scatter, dep, name):
    na = len(srcs)
    shapes = [t.shape[1:] if scatter else t.shape for t in srcs]
    lands = [pltpu.with_memory_space_constraint(lax.empty((N_DEV,) + s, t.dtype), pltpu.HBM) for s, t in zip(shapes, srcs)]

    def body(*refs):
        src_refs, land_refs = refs[:na], refs[na:2 * na]
        send_sems, recv_sems = refs[2 * na + 1], refs[2 * na + 2]
        token = refs[-1]
        for send, _ in _push_copies(scatter, src_refs, land_refs, send_sems, recv_sems):
            send.start()
        token[...] = jnp.zeros_like(token)

    sem = pltpu.SemaphoreType.DMA((na * (N_DEV - 1),))
    outs = pl.pallas_call(
        body, name=name,
        out_shape=(sem, sem) + tuple(pltpu.HBM(t.shape, t.dtype) for t in srcs)
        + tuple(pltpu.HBM(t.shape, t.dtype) for t in lands) + (jax.ShapeDtypeStruct((8, LANE), F32),),
        in_specs=[_HBM] * (2 * na) + [pl.BlockSpec(memory_space=pl.ANY)],
        out_specs=(_SEM, _SEM) + (_HBM,) * (2 * na) + (pl.BlockSpec(memory_space=pltpu.VMEM),),
        input_output_aliases={i: 2 + i for i in range(2 * na)},
        compiler_params=pltpu.CompilerParams(has_side_effects=_EFFECT),
    )(*[pltpu.with_memory_space_constraint(t, pltpu.HBM) for t in srcs], *lands, dep)
    return outs[0], outs[1], outs[2:2 + na], outs[2 + na:2 + 2 * na], outs[-1]


def _push_wait(send_sems, recv_sems, src_thru, land_thru, scatter, after, name):
    na = len(src_thru)

    def body(*refs):
        src_refs, land_refs = refs[:na], refs[na:2 * na]
        ssem, rsem = refs[2 * na], refs[2 * na + 1]
        for send, recv in _push_copies(scatter, src_refs, land_refs, ssem, rsem):
            send.wait_send()
            recv.wait_recv()

    outs = pl.pallas_call(
        body, name=name,
        out_shape=tuple(pltpu.HBM(t.shape, t.dtype) for t in src_thru) + tuple(pltpu.HBM(t.shape, t.dtype) for t in land_thru),
        in_specs=[_HBM] * (2 * na) + [_SEM, _SEM, pl.BlockSpec(memory_space=pl.ANY)],
        out_specs=(_HBM,) * (2 * na),
        input_output_aliases={i: i for i in range(2 * na)},
        compiler_params=pltpu.CompilerParams(has_side_effects=_EFFECT),
    )(*src_thru, *land_thru, send_sems, recv_sems, after)
    return outs[:na], outs[na:]


def _exchange_behind(srcs, scatter, dep, name):
    send_sems, recv_sems, thru, lands, token = _push_start(srcs, scatter, dep, name + "_start")

    def finish(after):
        src_done, land_done = _push_wait(send_sems, recv_sems, thru, lands, scatter, after, name + "_wait")
        return _place_own(land_done, src_done, scatter, name + "_own")

    return token[0, 0], finish


def _place_own(lands, srcs, scatter, name):
    me = (4 * lax.axis_index("x") + 2 * lax.axis_index("y") + lax.axis_index("c")).astype(jnp.int32).reshape(1)
    outs = []
    for a, (land, src) in enumerate(zip(lands, srcs)):
        r_, c_ = land.shape[1:]
        tr = _pick(r_, (512, 256, 128, 64, 32, 16))

        def body(me_ref, land_ref, src_ref, out_ref):
            out_ref[...] = src_ref[...]

        src_spec = (pl.BlockSpec((None, tr, c_), lambda i, me_: (me_[0], i, 0)) if scatter
                    else pl.BlockSpec((tr, c_), lambda i, me_: (i, 0)))
        gs = pltpu.PrefetchScalarGridSpec(
            num_scalar_prefetch=1, grid=(r_ // tr,),
            in_specs=[pl.BlockSpec(memory_space=pl.ANY), src_spec],
            out_specs=pl.BlockSpec((None, tr, c_), lambda i, me_: (me_[0], i, 0)))
        outs.append(pl.pallas_call(
            body, name=f"{name}_{a}", grid_spec=gs, out_shape=jax.ShapeDtypeStruct(land.shape, land.dtype),
            input_output_aliases={1: 0}, compiler_params=_params("arbitrary"),
        )(me, land, src))
    return outs


_BIG = (("w_in", D_MODEL, D_IN, 1), ("w_uq", Q_RANK, HEADS * QK, 1), ("w_ukv", KV_RANK, HEADS * (NOPE + VDIM), 1),
        ("w_out", D_MODEL, D_MODEL, 0), ("w_gate", D_MODEL, D_FF, 1), ("w_up", D_MODEL, D_FF, 1),
        ("w_down", D_FF, D_MODEL, 0))
_CQKV = (0, Q_RANK + KV_RANK)
_KR = (_CQKV[1], _CQKV[1] + ROPE)
_Z = (_KR[1], _KR[1] + SSD_W)
_XBC = (_Z[1], _Z[1] + CONV_DIM)
_DT = (_XBC[1], _XBC[1] + SSD_H)


def _win_segments(w_in_g):
    w = jnp.transpose(w_in_g, (1, 0, 2)).reshape(D_MODEL, D_IN)
    small = jnp.concatenate([w[:, _KR[0]:_KR[1]], w[:, _DT[0]:_DT[1]],
                             jnp.zeros((D_MODEL, LANE - ROPE - SSD_H), w.dtype)], axis=1)
    return w[:, _CQKV[0]:_CQKV[1]], w[:, _Z[0]:_Z[1]], w[:, _XBC[0]:_XBC[1]], small


def _win_from_segments(g_cqkv, g_z, g_xbc, g_small):
    w = jnp.concatenate([g_cqkv, g_small[:, :ROPE], g_z, g_xbc, g_small[:, ROPE:ROPE + SSD_H]], axis=1)
    return jnp.transpose(w.reshape(D_MODEL, N_DEV, D_IN // N_DEV), (1, 0, 2))


_SMALL = (("q_norm_w", 512), ("kv_norm_w", 512), ("conv_b", CONV_DIM), ("dt_bias", SSD_H), ("a_log", SSD_H),
          ("d_skip", SSD_H), ("ssd_norm_w", SSD_W), ("attn_out_norm_w", 1024), ("pre_mix_norm_w", D_MODEL),
          ("post_mix_norm_w", D_MODEL), ("pre_ffn_norm_w", D_MODEL), ("post_ffn_norm_w", D_MODEL),
          ("conv_w", CONV_K * CONV_DIM))
_SMALL_ROWS = -(-sum(-(-n // LANE) for _, n in _SMALL) // 8) * 8


def _pack_small(vals):
    rows = []
    for name, n in _SMALL:
        v = vals[name].reshape(-1).astype(F32)
        pad = -(-n // LANE) * LANE
        rows.append(jnp.pad(v, (0, pad - n)).reshape(-1, LANE))
    m = jnp.concatenate(rows, axis=0)
    return jnp.pad(m, ((0, _SMALL_ROWS - m.shape[0]), (0, 0)))


def _unpack_small(m):
    out, r = {}, 0
    for name, n in _SMALL:
        nr = -(-n // LANE)
        out[name] = m[r:r + nr].reshape(-1)[:n]
        r += nr
    return out


def _head_row(v):
    return jnp.pad(v.reshape(1, -1).astype(F32), ((0, 0), (HEAD_LANE, LANE - HEAD_LANE - v.shape[-1])))


def _local_step(x, positions, target, wg, small, weights, on_grads):
    w_cqkv, w_z, w_xbc, w_small = _win_segments(wg["w_in"])
    conv_w = wg["conv_w"]
    conv_b = small["conv_b"].reshape(1, CONV_DIM)
    qkv_norm_w = jnp.concatenate([small["q_norm_w"], small["kv_norm_w"]])
    attn_norm_w = small["attn_out_norm_w"].reshape(HEADS, 1, VDIM)
    scale = QK ** -0.5

    inv_freq = ROPE_THETA ** (-jnp.arange(0, ROPE, 2, dtype=F32) / ROPE)
    ang = positions.astype(F32)[:, None] * inv_freq
    cos2 = jnp.tile(jnp.cos(ang), (1, 2))
    sin2 = jnp.tile(jnp.sin(ang), (1, 2))

    u = _rms_fwd(x, small["pre_mix_norm_w"], out_dtype=MXU_DTYPE, name="pre_mix_norm")
    cqkv = _mm(u, w_cqkv, "nn", name="in_proj_qkv")
    z = _mm(u, w_z, "nn", name="in_proj_z")
    xbc = _mm(u, w_xbc, "nn", name="in_proj_xbc")
    sm = _mm(u, w_small, "nn", name="in_proj_small")

    w_uq, w_ukv, w_out = weights("heads", cqkv)
    w_out = w_out.reshape(D_MODEL, D_MODEL)
    w_out_a = w_out[:HEADS * VDIM].reshape(HEADS, VDIM, D_MODEL)
    w_out_s = w_out[HEADS * VDIM:]
    qkvn = _rms_fwd(cqkv, qkv_norm_w, groups=2, out_dtype=MXU_DTYPE, name="qkv_norm")
    q = _mm(qkvn, w_uq, "nn", b_blk=True, out_blk=True, a_cols=(0, Q_RANK), name="q_up")
    kv = _mm(qkvn, w_ukv, "nn", b_blk=True, out_blk=True, a_cols=(Q_RANK, KV_RANK), name="kv_up")
    q_h = _q_prep(q, cos2, sin2, scale, name="q_prep")
    k_h, v_h = _kv_prep(kv, sm, cos2, sin2)
    o_h, lse = _flash_fwd(q_h, k_h, v_h)
    attn = _hnorm_fwd(o_h, attn_norm_w)

    xbc_act = _conv_fwd(xbc, conv_w, conv_b)
    dtt = jnp.transpose(sm[:, HEAD_LANE:HEAD_LANE + SSD_H])
    ssd_args = (xbc_act, sm, dtt, _head_row(small["dt_bias"]), small["dt_bias"].reshape(SSD_H, 1),
                _head_row(small["a_log"]), small["a_log"].reshape(SSD_H, 1),
                jnp.broadcast_to(small["d_skip"].reshape(SSD_H, 1), (SSD_H, SSD_P)).reshape(SSD_PAIRS, 1, LANE))
    y_ssd, prev = _ssd_fwd(*ssd_args)
    ssm = _gated_norm_fwd(y_ssd, z, small["ssd_norm_w"])

    mix = _mm(attn, w_out_a, "nn", a_blk=True, b_blk=True, fuse=HEADS, name="out_proj_attn")
    mix = _mm(ssm, w_out_s, "nn", add=mix, name="out_proj_ssm")
    h1 = _rms_fwd(mix, small["post_mix_norm_w"], res=x, name="post_mix_norm")

    w_gate, w_up, w_down = weights("ffn", mix)
    vv = _rms_fwd(h1, small["pre_ffn_norm_w"], out_dtype=MXU_DTYPE, name="pre_ffn_norm")
    gate, up, act = _ffn_fwd(vv, w_gate, w_up)
    ffn = _mm(act, w_down, "nn", a_blk=True, b_blk=True, fuse=2, name="ffn_down")
    loss_blk, dy, dffn, g_post_ffn = _loss_head(ffn, h1, target, small["post_ffn_norm_w"])

    g_down = _mm(act, dffn, "tn", a_blk=True, out_blk=True, out_dtype=MXU_DTYPE, name="g_down")
    dgate, dup = _ffn_bwd_act(dffn, w_down, gate, up)
    dvv = _ffn_bwd_in(dgate, w_gate, dup, w_up)
    g_gate = _mm(vv, dgate, "tn", b_blk=True, out_blk=True, out_dtype=MXU_DTYPE, name="g_gate")
    g_up = _mm(vv, dup, "tn", b_blk=True, out_blk=True, out_dtype=MXU_DTYPE, name="g_up")
    pre_ffn_w = small["pre_ffn_norm_w"] + on_grads("ffn", [g_gate, g_up, g_down])
    dh1, g_pre_ffn = _rms_bwd(h1, pre_ffn_w, [dvv], res=dy, name="pre_ffn_norm_bwd")

    dmix, g_post_mix = _rms_bwd(mix, small["post_mix_norm_w"], [dh1], out_dtype=MXU_DTYPE, name="post_mix_norm_bwd")
    dattn = _mm(dmix, w_out_a, "nt", b_blk=True, out_blk=True, name="d_attn")
    dssm = _mm(dmix, w_out_s, "nt", name="d_ssm")
    g_out_a = _mm(attn, dmix, "tn", a_blk=True, out_blk=True, out_dtype=MXU_DTYPE, name="g_out_attn")
    g_out_s = _mm(ssm, dmix, "tn", out_dtype=MXU_DTYPE, name="g_out_ssm")
    g_out = jnp.concatenate([g_out_a.reshape(HEADS * VDIM, D_MODEL), g_out_s], axis=0)

    do_h, delta, g_attn_norm = _hnorm_bwd(o_h, attn_norm_w, dattn)
    dq_h, dk_h, dv_h = _flash_bwd(q_h, k_h, v_h, do_h, lse, delta)
    dq = _q_prep(dq_h, cos2, -sin2, scale, name="dq_post")

    dy_ssd, dz, g_ssd_norm = _gated_norm_bwd(y_ssd, z, small["ssd_norm_w"], dssm)
    dxbc_act, ddt, dpar = _ssd_bwd(*ssd_args, prev, dy_ssd)
    dkv, dsm = _dkv_post(dk_h, dv_h, ddt, cos2, -sin2)
    dpre, dwb = _conv_bwd_pre(xbc, conv_w, conv_b, dxbc_act)
    dxbc = _conv_bwd_in(dpre, conv_w)

    dqn = _mm(dq, w_uq, "nt", a_blk=True, b_blk=True, fuse=HEADS, name="d_qn")
    dkvn = _mm(dkv, w_ukv, "nt", a_blk=True, b_blk=True, fuse=HEADS, name="d_kvn")
    g_uq = _mm(qkvn, dq, "tn", b_blk=True, out_blk=True, a_cols=(0, Q_RANK), out_dtype=MXU_DTYPE, name="g_uq")
    g_ukv = _mm(qkvn, dkv, "tn", b_blk=True, out_blk=True, a_cols=(Q_RANK, KV_RANK), out_dtype=MXU_DTYPE, name="g_ukv")
    heads_token = on_grads("heads", [g_uq, g_ukv, g_out.reshape(N_DEV, D_MODEL // N_DEV, D_MODEL)])
    dcqkv, g_qkv_norm = _rms_bwd(cqkv, qkv_norm_w + heads_token, [dqn, dkvn], out_dtype=MXU_DTYPE, name="qkv_norm_bwd")

    g_in = _win_from_segments(_mm(u, dcqkv, "tn", out_dtype=MXU_DTYPE, name="g_in_qkv"),
                              _mm(u, dz, "tn", out_dtype=MXU_DTYPE, name="g_in_z"),
                              _mm(u, dxbc, "tn", out_dtype=MXU_DTYPE, name="g_in_xbc"),
                              _mm(u, dsm, "tn", out_dtype=MXU_DTYPE, name="g_in_small"))
    in_token = on_grads("in", [g_in])
    du = _mm(dsm + in_token.astype(dsm.dtype), w_small, "nt", name="d_u_small")
    du = _mm(dcqkv, w_cqkv, "nt", add=du, name="d_u_qkv")
    du = _mm(dz, w_z, "nt", add=du, name="d_u_z")
    du = _mm(dxbc, w_xbc, "nt", add=du, name="d_u_xbc")
    dx, g_pre_mix = _rms_bwd(x, small["pre_mix_norm_w"], [du], res=dh1, name="pre_mix_norm_bwd")

    hl = slice(HEAD_LANE, HEAD_LANE + SSD_H)
    g_small = {"q_norm_w": g_qkv_norm[0, :Q_RANK], "kv_norm_w": g_qkv_norm[0, Q_RANK:], "conv_b": dwb[CONV_K],
               "dt_bias": dpar[0, hl], "a_log": dpar[1, hl], "d_skip": dpar[2, hl], "ssd_norm_w": g_ssd_norm,
               "attn_out_norm_w": g_attn_norm, "pre_mix_norm_w": g_pre_mix, "post_mix_norm_w": g_post_mix,
               "pre_ffn_norm_w": g_pre_ffn, "post_ffn_norm_w": g_post_ffn, "conv_w": dwb[:CONV_K]}
    return loss_blk[0, 0], dx, g_small


_WEIGHT_ORDER = ("w_in", "q_norm_w", "w_uq", "kv_norm_w", "w_ukv", "conv_w", "conv_b", "dt_bias", "a_log", "d_skip",
                 "ssd_norm_w", "attn_out_norm_w", "w_out", "pre_mix_norm_w", "post_mix_norm_w", "pre_ffn_norm_w",
                 "post_ffn_norm_w", "w_gate", "w_up", "w_down")


def kernel(x, positions, w_in, q_norm_w, w_uq, kv_norm_w, w_ukv, conv_w, conv_b, dt_bias, a_log, d_skip, ssd_norm_w, attn_out_norm_w, w_out, pre_mix_norm_w, post_mix_norm_w, pre_ffn_norm_w, post_ffn_norm_w, w_gate, w_up, w_down, loss_target, m_w_in, m_q_norm_w, m_w_uq, m_kv_norm_w, m_w_ukv, m_conv_w, m_conv_b, m_dt_bias, m_a_log, m_d_skip, m_ssd_norm_w, m_attn_out_norm_w, m_w_out, m_pre_mix_norm_w, m_post_mix_norm_w, m_pre_ffn_norm_w, m_post_ffn_norm_w, m_w_gate, m_w_up, m_w_down, v_w_in, v_q_norm_w, v_w_uq, v_kv_norm_w, v_w_ukv, v_conv_w, v_conv_b, v_dt_bias, v_a_log, v_d_skip, v_ssd_norm_w, v_attn_out_norm_w, v_w_out, v_pre_mix_norm_w, v_post_mix_norm_w, v_pre_ffn_norm_w, v_post_ffn_norm_w, v_w_gate, v_w_up, v_w_down):
    w = dict(w_in=w_in, q_norm_w=q_norm_w, w_uq=w_uq, kv_norm_w=kv_norm_w, w_ukv=w_ukv, conv_w=conv_w, conv_b=conv_b,
             dt_bias=dt_bias, a_log=a_log, d_skip=d_skip, ssd_norm_w=ssd_norm_w, attn_out_norm_w=attn_out_norm_w,
             w_out=w_out, pre_mix_norm_w=pre_mix_norm_w, post_mix_norm_w=post_mix_norm_w,
             pre_ffn_norm_w=pre_ffn_norm_w, post_ffn_norm_w=post_ffn_norm_w, w_gate=w_gate, w_up=w_up, w_down=w_down)
    m = dict(w_in=m_w_in, q_norm_w=m_q_norm_w, w_uq=m_w_uq, kv_norm_w=m_kv_norm_w, w_ukv=m_w_ukv, conv_w=m_conv_w,
             conv_b=m_conv_b, dt_bias=m_dt_bias, a_log=m_a_log, d_skip=m_d_skip, ssd_norm_w=m_ssd_norm_w,
             attn_out_norm_w=m_attn_out_norm_w, w_out=m_w_out, pre_mix_norm_w=m_pre_mix_norm_w,
             post_mix_norm_w=m_post_mix_norm_w, pre_ffn_norm_w=m_pre_ffn_norm_w, post_ffn_norm_w=m_post_ffn_norm_w,
             w_gate=m_w_gate, w_up=m_w_up, w_down=m_w_down)
    v = dict(w_in=v_w_in, q_norm_w=v_q_norm_w, w_uq=v_w_uq, kv_norm_w=v_kv_norm_w, w_ukv=v_w_ukv, conv_w=v_conv_w,
             conv_b=v_conv_b, dt_bias=v_dt_bias, a_log=v_a_log, d_skip=v_d_skip, ssd_norm_w=v_ssd_norm_w,
             attn_out_norm_w=v_attn_out_norm_w, w_out=v_w_out, pre_mix_norm_w=v_pre_mix_norm_w,
             post_mix_norm_w=v_post_mix_norm_w, pre_ffn_norm_w=v_pre_ffn_norm_w, post_ffn_norm_w=v_post_ffn_norm_w,
             w_gate=v_w_gate, w_up=v_w_up, w_down=v_w_down)
    w, m, v = ({k: t[0] for k, t in d.items()} for d in (w, m, v))
    me = 4 * lax.axis_index("x") + 2 * lax.axis_index("y") + lax.axis_index("c")
    groups = {"in": ("w_in",), "heads": ("w_uq", "w_ukv", "w_out"), "ffn": ("w_gate", "w_up", "w_down")}
    cshard = CONV_DIM // N_DEV

    shards = [w["w_in"].astype(MXU_DTYPE),
              jnp.stack(_split3(w["conv_w"])).reshape(3 * CONV_K, cshard).astype(MXU_DTYPE)]
    w_in_g, cw = _all_gather(shards, name="gather_weights")
    cw = cw.astype(F32).reshape(N_DEV, 3, CONV_K, cshard)
    wg = {"w_in": w_in_g, "conv_w": jnp.transpose(cw[:, 0] + cw[:, 1] + cw[:, 2], (1, 0, 2)).reshape(CONV_K, CONV_DIM)}
    arriving, dep, started = {}, wg["conv_w"], jnp.zeros((), F32)
    small = {name: w[name] for name, _ in _SMALL if name != "conv_w"}
    for group in ("heads", "ffn"):
        token, arriving[group] = _exchange_behind([w[name].astype(MXU_DTYPE) for name in groups[group]], False,
                                                  dep, group + "_weights")
        started = started + token
        dep = jnp.zeros((8, LANE), F32) + started
    small["pre_mix_norm_w"] = small["pre_mix_norm_w"] + started

    leaving = {}

    def on_grads(group, gs):
        token, leaving[group] = _exchange_behind(gs, True, jnp.zeros((8, LANE), F32), group + "_grads")
        return token

    loss_local, dx, g_small = _local_step(x[0], positions[0], loss_target[0], wg, small,
                                          lambda group, after: arriving[group](after), on_grads)
    loss = lax.psum(loss_local, ("x", "y", "c"))

    recv = {}
    for group in ("ffn", "heads", "in"):
        recv.update(zip(groups[group], leaving[group](dx)))
    grads, deltas, new_m, new_v = {}, {}, {}, {}
    for name, parts in recv.items():
        grads[name], deltas[name], new_m[name], new_v[name] = _adamw(parts, w[name], m[name], v[name],
                                                                     name="adamw_" + name)

    def embed(t):
        return lax.dynamic_update_slice(jnp.zeros((CONV_K, CONV_DIM), F32), t, (0, me * cshard))

    parts_s = _all_gather([_pack_small(g_small)], name="gather_small_grads")[0]
    packs = [_pack_small({**{n_: d[n_] for n_, _ in _SMALL if n_ != "conv_w"}, "conv_w": embed(d["conv_w"])})
             for d in (w, m, v)]
    outs = [_unpack_small(t) for t in _adamw_small(parts_s, *packs)]
    for name, n in _SMALL:
        for dst, src in zip((grads, deltas, new_m, new_v), outs):
            if name == "conv_w":
                dst[name] = lax.dynamic_slice(src[name].reshape(CONV_K, CONV_DIM), (0, me * cshard), (CONV_K, cshard))
            else:
                dst[name] = src[name]

    def lead(d):
        return [d[name][None] for name in _WEIGHT_ORDER]

    return (loss, dx[None], *lead(grads), *lead(deltas), *lead(new_m), *lead(new_v))
```

```python
import numpy as np

import jax
import jax.numpy as jnp
from jax import lax
from jax.experimental import pallas as pl
from jax.experimental.pallas import tpu as pltpu

F32 = jnp.float32
BF16 = jnp.bfloat16
MXU_DTYPE = jnp.bfloat16
EPS = 1e-6
VMEM_LIMIT_BYTES = 48 * 1024 * 1024
K_TILE_MAX = 2048

N_DEV = 8
D_MODEL = 2048
Q_RANK = 512
KV_RANK = 512
ROPE = 64
HALF = ROPE // 2
HEADS = 8
NOPE = 128
VDIM = 128
QK = NOPE + ROPE
SSD_W = 1024
SSD_H = 16
SSD_P = 64
SSD_G = 2
SSD_E = SSD_H // SSD_G
SSD_N = 128
CHUNK = 128
CONV_K = 4
CONV_DIM = SSD_W + 2 * SSD_G * SSD_N
B_OFF = SSD_W
C_OFF = SSD_W + SSD_G * SSD_N
D_FF = 5632
D_IN = Q_RANK + KV_RANK + ROPE + SSD_W + CONV_DIM + SSD_H
ROPE_THETA = 10000.0
LANE = 128
HEAD_LANE = ROPE

ADAM_LR = 0.001
ADAM_B1 = 0.9
ADAM_B2 = 0.999
ADAM_EPS = 1e-08
ADAM_WD = 0.01
ADAM_STEP = 10


def _pick(n, cands):
    for c in cands:
        if n % c == 0:
            return c
    return n


def _params(*sem):
    return pltpu.CompilerParams(dimension_semantics=sem, vmem_limit_bytes=VMEM_LIMIT_BYTES)


def _sigmoid(x):
    return 1.0 / (1.0 + jnp.exp(-x))


def _silu(x):
    return x * _sigmoid(x)


def _dsilu(x):
    s = _sigmoid(x)
    return s * (1.0 + x * (1.0 - s))


def _softplus(x):
    e = jnp.exp(-jnp.abs(x))
    small = e * (1.0 - e * (0.5 - e * (1.0 / 3.0)))
    return jnp.maximum(x, 0.0) + jnp.where(e < 0.01, small, jnp.log(1.0 + e))


def _dot(a, b, ca, cb):
    return lax.dot_general(a, b, (((ca,), (cb,)), ((), ())), preferred_element_type=F32)


def _mx(v):
    return v.astype(MXU_DTYPE)


def _split3(a):
    hi = a.astype(BF16)
    r1 = a - hi.astype(F32)
    mid = r1.astype(BF16)
    lo = (r1 - mid.astype(F32)).astype(BF16)
    return hi, mid, lo


def _exact_dot(a, b, ca, cb, split_a):
    if split_a:
        return sum(_dot(p, b, ca, cb) for p in _split3(a))
    return sum(_dot(a, p, ca, cb) for p in _split3(b))


MM_ROW_GROUPS = 4


def _row_slices(tm, align):
    ng = MM_ROW_GROUPS
    while ng > 1 and (tm % ng or (tm // ng) % align):
        ng //= 2
    return [slice(g * (tm // ng), (g + 1) * (tm // ng)) for g in range(ng)]


def _mm(a, b, mode, *, a_blk=False, b_blk=False, out_blk=False, a_cols=None, add=None, out_dtype=F32, fuse=1,
        name="mm"):
    a2, b2 = a.shape[-2:], b.shape[-2:]
    a_last = a2[1] if a_cols is None else a_cols[1]
    a_start = 0 if a_cols is None else a_cols[0]
    if mode == "nn":
        m, k, (k2, n) = a2[0], a_last, b2
    elif mode == "nt":
        m, k, (n, k2) = a2[0], a_last, b2
    else:
        k, m, (k2, n) = a2[0], a_last, b2
    assert k == k2, (a.shape, b.shape, mode)
    tm = _pick(m, (1024, 704, 512, 256, 128))
    tn = _pick(n, (1024, 768, 704, 512, 256, 192, 128))
    tk = k if k <= K_TILE_MAX else _pick(k, (K_TILE_MAX, 1024, 512))
    nk = k // tk
    jo = N_DEV if out_blk else 1
    reduce_blocks = a_blk and b_blk and not out_blk
    assert fuse == 1 or reduce_blocks
    jr = N_DEV // fuse if reduce_blocks else 1
    ca, cb = {"nn": (1, 0), "nt": (1, 1), "tn": (0, 0)}[mode]
    has_add = add is not None
    single = jr * nk == 1
    if mode == "tn":
        assert a_start % tm == 0
        a_block, a_idx = (tk, tm), (lambda i, kk: (kk, i + a_start // tm))
    else:
        assert a_start % tk == 0
        a_block, a_idx = (tm, tk), (lambda i, kk: (i, kk + a_start // tk))
    b_block, b_idx = ((tn, tk), (lambda nn_, kk: (nn_, kk))) if mode == "nt" else ((tk, tn), (lambda nn_, kk: (kk, nn_)))

    def blk_specs(blocked, block, idx, of_a, t):
        def pos(o, i, nn_, kk):
            return idx(i, kk) if of_a else idx(nn_, kk)
        if blocked:
            return pl.BlockSpec((None,) + block,
                                lambda o, i, nn_, r, kk: ((o if out_blk else r * fuse + t),) + pos(o, i, nn_, kk))
        return pl.BlockSpec(block, lambda o, i, nn_, r, kk: pos(o, i, nn_, kk))

    a_specs = [blk_specs(a_blk, a_block, a_idx, True, t) for t in range(fuse)]
    b_specs = [blk_specs(b_blk, b_block, b_idx, False, t) for t in range(fuse)]
    o_spec = (pl.BlockSpec((None, tm, tn), lambda o, i, nn_, r, kk: (o, i, nn_)) if out_blk
              else pl.BlockSpec((tm, tn), lambda o, i, nn_, r, kk: (i, nn_)))

    groups = _row_slices(tm, LANE if mode == "tn" else 16)

    def body(*refs):
        a_refs, b_refs = refs[:fuse], refs[fuse:2 * fuse]
        add_ref = refs[2 * fuse] if has_add else None
        o_ref = refs[2 * fuse + 1] if has_add else refs[2 * fuse]

        def partial(rs):
            out = None
            for t in range(fuse):
                av = a_refs[t][:, rs] if mode == "tn" else a_refs[t][rs, :]
                d = _dot(_mx(av), _mx(b_refs[t][...]), ca, cb)
                out = d if out is None else out + d
            return out

        if single:
            for rs in groups:
                res = partial(rs)
                if has_add:
                    res = res + add_ref[rs, :]
                o_ref[rs, :] = res.astype(o_ref.dtype)
            return
        acc = refs[-1]
        r, kk = pl.program_id(3), pl.program_id(4)

        @pl.when(jnp.logical_and(r == 0, kk == 0))
        def _():
            acc[...] = jnp.zeros_like(acc)

        for rs in groups:
            acc[rs, :] += partial(rs)

        @pl.when(jnp.logical_and(r == jr - 1, kk == nk - 1))
        def _():
            res = acc[...]
            if has_add:
                res = res + add_ref[...]
            o_ref[...] = res.astype(o_ref.dtype)

    out_shape = ((N_DEV, m, n) if out_blk else (m, n))
    return pl.pallas_call(
        body, name=name, grid=(jo, m // tm, n // tn, jr, nk),
        in_specs=a_specs + b_specs + ([o_spec] if has_add else []), out_specs=o_spec,
        out_shape=jax.ShapeDtypeStruct(out_shape, out_dtype),
        scratch_shapes=[] if single else [pltpu.VMEM((tm, tn), F32)],
        compiler_params=_params("parallel", "parallel", "parallel", "arbitrary", "arbitrary"),
    )(*((a,) * fuse + (b,) * fuse + ((add,) if has_add else ())))


def _row_tile(r_):
    return _pick(r_, (256, 128, 64, 32, 16, 8))


def _rms_fwd(t, w, groups=1, res=None, out_dtype=F32, name="rms_fwd"):
    r_, f = t.shape
    fg = f // groups
    tr = _row_tile(r_)
    has_res = res is not None

    def body(*refs):
        t_ref, w_ref = refs[0], refs[1]
        res_ref = refs[2] if has_res else None
        o_ref = refs[-1]
        for g in range(groups):
            sl = slice(g * fg, (g + 1) * fg)
            tv = t_ref[:, sl].astype(F32)
            r = lax.rsqrt(jnp.mean(tv * tv, axis=-1, keepdims=True) + EPS)
            y = tv * r * w_ref[:, sl]
            if has_res:
                y = y + res_ref[:, sl]
            o_ref[:, sl] = y.astype(o_ref.dtype)

    row = pl.BlockSpec((tr, f), lambda i: (i, 0))
    wsp = pl.BlockSpec((1, f), lambda i: (0, 0))
    return pl.pallas_call(
        body, name=name, grid=(r_ // tr,),
        in_specs=[row, wsp] + ([row] if has_res else []), out_specs=row,
        out_shape=jax.ShapeDtypeStruct((r_, f), out_dtype),
        compiler_params=_params("parallel"),
    )(*((t, w.reshape(1, f)) + ((res,) if has_res else ())))


def _rms_bwd(t, w, dys, res=None, out_dtype=F32, name="rms_bwd"):
    r_, f = t.shape
    groups = len(dys)
    fg = f // groups
    tr = _row_tile(r_)
    has_res = res is not None

    def body(*refs):
        t_ref, w_ref = refs[0], refs[1]
        dy_refs = refs[2:2 + groups]
        res_ref = refs[2 + groups] if has_res else None
        dt_ref, dw_ref = refs[-2], refs[-1]

        @pl.when(pl.program_id(0) == 0)
        def _():
            dw_ref[...] = jnp.zeros_like(dw_ref)

        for g in range(groups):
            sl = slice(g * fg, (g + 1) * fg)
            tv = t_ref[:, sl].astype(F32)
            dyv = dy_refs[g][...].astype(F32)
            r = lax.rsqrt(jnp.mean(tv * tv, axis=-1, keepdims=True) + EPS)
            gw = dyv * w_ref[:, sl]
            c = jnp.mean(gw * tv, axis=-1, keepdims=True)
            dt = r * gw - tv * (r * r * r * c)
            if has_res:
                dt = dt + res_ref[:, sl]
            dt_ref[:, sl] = dt.astype(dt_ref.dtype)
            dw_ref[:, sl] += jnp.sum(dyv * tv * r, axis=0, keepdims=True)

    row = pl.BlockSpec((tr, f), lambda i: (i, 0))
    grow = pl.BlockSpec((tr, fg), lambda i: (i, 0))
    wsp = pl.BlockSpec((1, f), lambda i: (0, 0))
    return pl.pallas_call(
        body, name=name, grid=(r_ // tr,),
        in_specs=[row, wsp] + [grow] * groups + ([row] if has_res else []), out_specs=[row, wsp],
        out_shape=[jax.ShapeDtypeStruct((r_, f), out_dtype), jax.ShapeDtypeStruct((1, f), F32)],
        compiler_params=_params("arbitrary"),
    )(*((t, w.reshape(1, f)) + tuple(dys) + ((res,) if has_res else ())))


def _norm_res_norm(t, res, w1, w2, name="post_mix_pre_ffn_norm"):
    r_, f = t.shape
    tr = _row_tile(r_)

    def body(t_ref, res_ref, w1_ref, w2_ref, h_ref, v_ref):
        tv = t_ref[...]
        h = res_ref[...] + tv * lax.rsqrt(jnp.mean(tv * tv, axis=-1, keepdims=True) + EPS) * w1_ref[...]
        h_ref[...] = h
        v_ref[...] = (h * lax.rsqrt(jnp.mean(h * h, axis=-1, keepdims=True) + EPS) * w2_ref[...]).astype(v_ref.dtype)

    row = pl.BlockSpec((tr, f), lambda i: (i, 0))
    wsp = pl.BlockSpec((1, f), lambda i: (0, 0))
    return pl.pallas_call(
        body, name=name, grid=(r_ // tr,), in_specs=[row, row, wsp, wsp], out_specs=[row, row],
        out_shape=[jax.ShapeDtypeStruct((r_, f), F32), jax.ShapeDtypeStruct((r_, f), MXU_DTYPE)],
        compiler_params=_params("parallel"),
    )(t, res, w1.reshape(1, f), w2.reshape(1, f))


def _norm_res_norm_bwd(h, w2, dv, dres, t, w1, name="pre_ffn_post_mix_norm_bwd"):
    r_, f = h.shape
    tr = _row_tile(r_)

    def body(h_ref, w2_ref, dv_ref, dres_ref, t_ref, w1_ref, dh_ref, dt_ref, dw2_ref, dw1_ref):
        @pl.when(pl.program_id(0) == 0)
        def _():
            dw2_ref[...] = jnp.zeros_like(dw2_ref)
            dw1_ref[...] = jnp.zeros_like(dw1_ref)

        def rms_bwd(tv, wv, dyv):
            r = lax.rsqrt(jnp.mean(tv * tv, axis=-1, keepdims=True) + EPS)
            gw = dyv * wv
            c = jnp.mean(gw * tv, axis=-1, keepdims=True)
            return r * gw - tv * (r * r * r * c), jnp.sum(dyv * tv * r, axis=0, keepdims=True)

        d1, g2 = rms_bwd(h_ref[...], w2_ref[...], dv_ref[...])
        dh = d1 + dres_ref[...]
        dh_ref[...] = dh
        dw2_ref[...] += g2
        d2, g1 = rms_bwd(t_ref[...], w1_ref[...], dh)
        dt_ref[...] = d2.astype(dt_ref.dtype)
        dw1_ref[...] += g1

    row = pl.BlockSpec((tr, f), lambda i: (i, 0))
    wsp = pl.BlockSpec((1, f), lambda i: (0, 0))
    return pl.pallas_call(
        body, name=name, grid=(r_ // tr,), in_specs=[row, wsp, row, row, row, wsp], out_specs=[row, row, wsp, wsp],
        out_shape=[jax.ShapeDtypeStruct((r_, f), F32), jax.ShapeDtypeStruct((r_, f), MXU_DTYPE),
                   jax.ShapeDtypeStruct((1, f), F32), jax.ShapeDtypeStruct((1, f), F32)],
        compiler_params=_params("arbitrary"),
    )(h, w2.reshape(1, f), dv, dres, t, w1.reshape(1, f))


def _hnorm_fwd(o, w, width, name="attn_out_norm"):
    h, s_, v = o.shape
    tr = _row_tile(s_)

    def body(o_ref, w_ref, y_ref):
        ss = jnp.sum(o_ref[0] * o_ref[0], axis=-1, keepdims=True)
        for i in range(1, h):
            ss = ss + jnp.sum(o_ref[i] * o_ref[i], axis=-1, keepdims=True)
        r = lax.rsqrt(ss * (1.0 / (h * v)) + EPS)
        for i in range(h):
            sl = slice(i * v, (i + 1) * v)
            y_ref[:, sl] = (o_ref[i] * r * w_ref[:, sl]).astype(y_ref.dtype)

    return pl.pallas_call(
        body, name=name, grid=(s_ // tr,),
        in_specs=[pl.BlockSpec((h, tr, v), lambda i: (0, i, 0)), pl.BlockSpec((1, h * v), lambda i: (0, 0))],
        out_specs=pl.BlockSpec((tr, h * v), lambda i: (i, 0)),
        out_shape=jax.ShapeDtypeStruct((s_, width), MXU_DTYPE), compiler_params=_params("parallel"),
    )(o, w)


def _hnorm_bwd(o, w, dy, name="attn_out_norm_bwd"):
    h, s_, v = o.shape
    tr = _row_tile(s_)

    def body(o_ref, w_ref, dy_ref, do_ref, delta_ref, dw_ref):
        @pl.when(pl.program_id(0) == 0)
        def _():
            dw_ref[...] = jnp.zeros_like(dw_ref)

        ss = jnp.zeros((tr, 1), F32)
        cc = jnp.zeros((tr, 1), F32)
        for i in range(h):
            sl = slice(i * v, (i + 1) * v)
            ov = o_ref[i]
            ss = ss + jnp.sum(ov * ov, axis=-1, keepdims=True)
            cc = cc + jnp.sum(dy_ref[:, sl] * w_ref[:, sl] * ov, axis=-1, keepdims=True)
        r = lax.rsqrt(ss * (1.0 / (h * v)) + EPS)
        c = cc * (1.0 / (h * v))
        for i in range(h):
            sl = slice(i * v, (i + 1) * v)
            ov = o_ref[i]
            dyv = dy_ref[:, sl]
            dov = r * dyv * w_ref[:, sl] - ov * (r * r * r * c)
            do_ref[i] = dov.astype(do_ref.dtype)
            delta_ref[i] = jnp.sum(dov * ov, axis=-1, keepdims=True)
            dw_ref[:, sl] += jnp.sum(dyv * ov * r, axis=0, keepdims=True)

    blk = pl.BlockSpec((h, tr, v), lambda i: (0, i, 0))
    wsp = pl.BlockSpec((1, h * v), lambda i: (0, 0))
    return pl.pallas_call(
        body, name=name, grid=(s_ // tr,),
        in_specs=[blk, wsp, pl.BlockSpec((tr, h * v), lambda i: (i, 0))],
        out_specs=[blk, pl.BlockSpec((h, tr, 1), lambda i: (0, i, 0)), wsp],
        out_shape=[jax.ShapeDtypeStruct(o.shape, MXU_DTYPE), jax.ShapeDtypeStruct((h, s_, 1), F32),
                   jax.ShapeDtypeStruct((1, h * v), F32)],
        compiler_params=_params("arbitrary"),
    )(o, w, dy)


def _loss_head(ffn, h1, target, w, name="loss_head"):
    r_, f = ffn.shape
    tr = _row_tile(r_)

    def body(ffn_ref, h1_ref, tg_ref, w_ref, loss_ref, dy_ref, dffn_ref, dw_ref):
        @pl.when(pl.program_id(0) == 0)
        def _():
            dw_ref[...] = jnp.zeros_like(dw_ref)
            loss_ref[...] = jnp.zeros_like(loss_ref)

        tv = ffn_ref[...]
        wv = w_ref[...]
        r = lax.rsqrt(jnp.mean(tv * tv, axis=-1, keepdims=True) + EPS)
        tn = tv * r
        e = h1_ref[...] + tn * wv - tg_ref[...]
        tot = jnp.sum(jnp.sum(e * e, axis=1, keepdims=True), axis=0, keepdims=True) * (0.5 / f)
        loss_ref[...] += tot + jnp.zeros_like(loss_ref)
        dyv = e * (1.0 / f)
        dy_ref[...] = dyv
        gw = dyv * wv
        c = jnp.mean(gw * tv, axis=-1, keepdims=True)
        dffn_ref[...] = (r * gw - tv * (r * r * r * c)).astype(dffn_ref.dtype)
        dw_ref[...] += jnp.sum(dyv * tn, axis=0, keepdims=True)

    row = pl.BlockSpec((tr, f), lambda i: (i, 0))
    wsp = pl.BlockSpec((1, f), lambda i: (0, 0))
    lsp = pl.BlockSpec((1, LANE), lambda i: (0, 0))
    return pl.pallas_call(
        body, name=name, grid=(r_ // tr,),
        in_specs=[row, row, row, wsp], out_specs=[lsp, row, row, wsp],
        out_shape=[jax.ShapeDtypeStruct((1, LANE), F32), jax.ShapeDtypeStruct((r_, f), F32),
                   jax.ShapeDtypeStruct((r_, f), MXU_DTYPE), jax.ShapeDtypeStruct((1, f), F32)],
        compiler_params=_params("arbitrary"),
    )(ffn, h1, target, w.reshape(1, f))


def _rot_matrix():
    p = np.zeros((ROPE, ROPE), np.float32)
    for i in range(HALF):
        p[i + HALF, i] = -1.0
        p[i, i + HALF] = 1.0
    return jnp.asarray(p, BF16)


def _rope_val(r, c2, s2, rot):
    return r * c2 + _exact_dot(r, rot, 1, 0, True) * s2


def _q_prep(q, cos2, sin2, scale, name):
    h, s_, _ = q.shape
    tr = _pick(s_, (1024, 512, 256, 128, 64, 32, 16, 8))

    def body(q_ref, c_ref, s_ref, rot_ref, o_ref):
        x = q_ref[...]
        o_ref[:, :NOPE] = (x[:, :NOPE] * scale).astype(o_ref.dtype)
        o_ref[:, NOPE:] = (_rope_val(x[:, NOPE:], c_ref[...], s_ref[...], rot_ref[...]) * scale).astype(o_ref.dtype)

    blk = pl.BlockSpec((None, tr, QK), lambda hh, i: (hh, i, 0))
    csp = pl.BlockSpec((tr, ROPE), lambda hh, i: (i, 0))
    return pl.pallas_call(
        body, name=name, grid=(h, s_ // tr),
        in_specs=[blk, csp, csp, pl.BlockSpec((ROPE, ROPE), lambda hh, i: (0, 0))], out_specs=blk,
        out_shape=jax.ShapeDtypeStruct(q.shape, MXU_DTYPE), compiler_params=_params("parallel", "parallel"),
    )(q, cos2, sin2, _rot_matrix())


def _kv_prep(kv, small, cos2, sin2, name="kv_prep"):
    h, s_, _ = kv.shape
    tr = _row_tile(s_)

    def body(kv_ref, sm_ref, c_ref, s_ref, rot_ref, k_ref, v_ref):
        kr = _rope_val(sm_ref[:, :ROPE], c_ref[...], s_ref[...], rot_ref[...]).astype(k_ref.dtype)
        for i in range(h):
            k_ref[i, :, :NOPE] = kv_ref[i, :, :NOPE].astype(k_ref.dtype)
            k_ref[i, :, NOPE:] = kr
            v_ref[i] = kv_ref[i, :, NOPE:].astype(v_ref.dtype)

    csp = pl.BlockSpec((tr, ROPE), lambda i: (i, 0))
    return pl.pallas_call(
        body, name=name, grid=(s_ // tr,),
        in_specs=[pl.BlockSpec((h, tr, NOPE + VDIM), lambda i: (0, i, 0)), pl.BlockSpec((tr, LANE), lambda i: (i, 0)),
                  csp, csp, pl.BlockSpec((ROPE, ROPE), lambda i: (0, 0))],
        out_specs=[pl.BlockSpec((h, tr, QK), lambda i: (0, i, 0)), pl.BlockSpec((h, tr, VDIM), lambda i: (0, i, 0))],
        out_shape=[jax.ShapeDtypeStruct((h, s_, QK), MXU_DTYPE), jax.ShapeDtypeStruct((h, s_, VDIM), MXU_DTYPE)],
        compiler_params=_params("parallel"),
    )(kv, small, cos2, sin2, _rot_matrix())


def _dkv_post(dk, dv, ddt, cos2, nsin2, name="dkv_post"):
    h, s_, _ = dk.shape
    tr = _row_tile(s_)

    def body(dk_ref, dv_ref, ddt_ref, c_ref, s_ref, rot_ref, dkv_ref, dsm_ref):
        acc = dk_ref[0, :, NOPE:]
        for i in range(1, h):
            acc = acc + dk_ref[i, :, NOPE:]
        dsm_ref[:, :ROPE] = _rope_val(acc, c_ref[...], s_ref[...], rot_ref[...]).astype(dsm_ref.dtype)
        dsm_ref[:, ROPE:] = ddt_ref[:, ROPE:].astype(dsm_ref.dtype)
        for i in range(h):
            dkv_ref[i, :, :NOPE] = dk_ref[i, :, :NOPE].astype(dkv_ref.dtype)
            dkv_ref[i, :, NOPE:] = dv_ref[i].astype(dkv_ref.dtype)

    csp = pl.BlockSpec((tr, ROPE), lambda i: (i, 0))
    return pl.pallas_call(
        body, name=name, grid=(s_ // tr,),
        in_specs=[pl.BlockSpec((h, tr, QK), lambda i: (0, i, 0)), pl.BlockSpec((h, tr, VDIM), lambda i: (0, i, 0)),
                  pl.BlockSpec((tr, LANE), lambda i: (i, 0)), csp, csp, pl.BlockSpec((ROPE, ROPE), lambda i: (0, 0))],
        out_specs=[pl.BlockSpec((h, tr, NOPE + VDIM), lambda i: (0, i, 0)), pl.BlockSpec((tr, LANE), lambda i: (i, 0))],
        out_shape=[jax.ShapeDtypeStruct((h, s_, NOPE + VDIM), MXU_DTYPE), jax.ShapeDtypeStruct((s_, LANE), MXU_DTYPE)],
        compiler_params=_params("parallel"),
    )(dk, dv, ddt, cos2, nsin2, _rot_matrix())


def _attn_tile(s):
    return 1024 if s % 2048 == 0 else s // 2


def _pairs(n, by_key):
    if by_key:
        pr = [(i, j) for j in range(n) for i in range(j, n)]
    else:
        pr = [(i, j) for i in range(n) for j in range(i + 1)]
    return (jnp.asarray([p[0] for p in pr], jnp.int32), jnp.asarray([p[1] for p in pr], jnp.int32))


ATTN_ROW_GROUPS = 4


def _row_groups(t, diag):
    tg = t // ATTN_ROW_GROUPS
    out = []
    for r in range(ATTN_ROW_GROUPS):
        nc = (r + 1) * tg if diag else t
        mask = None
        if diag:
            mask = (lax.broadcasted_iota(jnp.int32, (tg, nc), 1)
                    <= lax.broadcasted_iota(jnp.int32, (tg, nc), 0) + r * tg)
        out.append((slice(r * tg, (r + 1) * tg), nc, mask))
    return out


def _flash_specs(t, dk, dv):
    qsp = pl.BlockSpec((None, t, dk), lambda hh, p, qi, kj: (hh, qi[p], 0))
    ksp = pl.BlockSpec((None, t, dk), lambda hh, p, qi, kj: (hh, kj[p], 0))
    vsp = pl.BlockSpec((None, t, dv), lambda hh, p, qi, kj: (hh, kj[p], 0))
    osp = pl.BlockSpec((None, t, dv), lambda hh, p, qi, kj: (hh, qi[p], 0))
    lsp = pl.BlockSpec((None, t, 1), lambda hh, p, qi, kj: (hh, qi[p], 0))
    return qsp, ksp, vsp, osp, lsp


def _flash_fwd(q, k, v, name="flash_fwd"):
    h, s_, dk = q.shape
    dv = v.shape[-1]
    t = _attn_tile(s_)
    n = s_ // t
    qi, kj = _pairs(n, False)

    def body(qi_ref, kj_ref, q_ref, k_ref, v_ref, o_ref, lse_ref, m_s, l_s, acc):
        p_ = pl.program_id(1)
        i, j = qi_ref[p_], kj_ref[p_]

        @pl.when(j == 0)
        def _():
            m_s[...] = jnp.full_like(m_s, -jnp.inf)
            l_s[...] = jnp.zeros_like(l_s)
            acc[...] = jnp.zeros_like(acc)

        def update(diag):
            for rs, nc, mask in _row_groups(t, diag):
                sc = _dot(q_ref[rs, :], k_ref[0:nc, :], 1, 1)
                if mask is not None:
                    sc = jnp.where(mask, sc, -jnp.inf)
                m_old = m_s[rs, :]
                m_new = jnp.maximum(m_old, jnp.max(sc, axis=1, keepdims=True))
                alpha = jnp.exp(m_old - m_new)
                p = jnp.exp(sc - m_new)
                l_s[rs, :] = alpha * l_s[rs, :] + jnp.sum(p, axis=1, keepdims=True)
                acc[rs, :] = alpha * acc[rs, :] + _dot(_mx(p), v_ref[0:nc, :], 1, 0)
                m_s[rs, :] = m_new

        @pl.when(j < i)
        def _():
            update(False)

        @pl.when(j == i)
        def _():
            update(True)
            o_ref[...] = acc[...] / l_s[...]
            lse_ref[...] = m_s[...] + jnp.log(l_s[...])

    qsp, ksp, vsp, osp, lsp = _flash_specs(t, dk, dv)
    gs = pltpu.PrefetchScalarGridSpec(
        num_scalar_prefetch=2, grid=(h, qi.shape[0]), in_specs=[qsp, ksp, vsp], out_specs=[osp, lsp],
        scratch_shapes=[pltpu.VMEM((t, 1), F32), pltpu.VMEM((t, 1), F32), pltpu.VMEM((t, dv), F32)])
    return pl.pallas_call(
        body, name=name, grid_spec=gs,
        out_shape=[jax.ShapeDtypeStruct((h, s_, dv), F32), jax.ShapeDtypeStruct((h, s_, 1), F32)],
        compiler_params=_params("parallel", "arbitrary"),
    )(qi, kj, q, k, v)


def _flash_bwd(q, k, v, do, lse, delta, name="flash_bwd"):
    h, s_, dk = q.shape
    dv = v.shape[-1]
    t = _attn_tile(s_)
    tg = t // ATTN_ROW_GROUPS
    n = s_ // t
    qi, kj = _pairs(n, True)

    def body(qi_ref, kj_ref, q_ref, k_ref, v_ref, do_ref, lse_ref, delta_ref, dq_ref, dk_ref, dv_ref, dk_acc, dv_acc):
        p_ = pl.program_id(1)
        i, j = qi_ref[p_], kj_ref[p_]

        @pl.when(p_ == 0)
        def _():
            dq_ref[...] = jnp.zeros_like(dq_ref)

        def update(diag):
            for g, (rs, nc, mask) in enumerate(_row_groups(t, diag)):
                sc = _dot(q_ref[rs, :], k_ref[0:nc, :], 1, 1)
                if mask is not None:
                    sc = jnp.where(mask, sc, -jnp.inf)
                p = jnp.exp(sc - lse_ref[rs, :])
                dob = _mx(do_ref[rs, :])
                dv_acc[0:nc, :] += _dot(_mx(p), dob, 0, 0)
                dp = _dot(dob, v_ref[0:nc, :], 1, 1)
                dsb = _mx(p * (dp - delta_ref[rs, :]))
                dk_acc[0:nc, :] += _dot(dsb, q_ref[rs, :], 0, 0)
                rows = pl.ds(pl.multiple_of(i * t + g * tg, tg), tg)
                dq_ref[rows, :] += _dot(dsb, k_ref[0:nc, :], 1, 0)

        @pl.when(i == j)
        def _():
            dk_acc[...] = jnp.zeros_like(dk_acc)
            dv_acc[...] = jnp.zeros_like(dv_acc)
            update(True)

        @pl.when(i > j)
        def _():
            update(False)

        @pl.when(i == n - 1)
        def _():
            dk_ref[...] = dk_acc[...]
            dv_ref[...] = dv_acc[...]

    qsp, ksp, vsp, osp, lsp = _flash_specs(t, dk, dv)
    dqsp = pl.BlockSpec((None, s_, dk), lambda hh, p, qi, kj: (hh, 0, 0))
    gs = pltpu.PrefetchScalarGridSpec(
        num_scalar_prefetch=2, grid=(h, qi.shape[0]), in_specs=[qsp, ksp, vsp, osp, lsp, lsp],
        out_specs=[dqsp, ksp, vsp],
        scratch_shapes=[pltpu.VMEM((t, dk), F32), pltpu.VMEM((t, dv), F32)])
    return pl.pallas_call(
        body, name=name, grid_spec=gs,
        out_shape=[jax.ShapeDtypeStruct((h, s_, dk), F32), jax.ShapeDtypeStruct((h, s_, dk), F32),
                   jax.ShapeDtypeStruct((h, s_, dv), F32)],
        compiler_params=_params("parallel", "arbitrary"),
    )(qi, kj, q, k, v, do, lse, delta)


HALO = 8


def _conv_specs(s_, c, tr, after):
    main = pl.BlockSpec((tr, c), lambda i: (i, 0))
    per = tr // HALO
    if after:
        halo = pl.BlockSpec((HALO, c), lambda i: (jnp.minimum((i + 1) * per, s_ // HALO - 1), 0))
    else:
        halo = pl.BlockSpec((HALO, c), lambda i: (jnp.maximum(i * per - 1, 0), 0))
    return main, halo


def _fill_before(ext, t_ref, h_ref, tr):
    ext[0:HALO, :] = jnp.where(pl.program_id(0) > 0, h_ref[...], 0.0)
    ext[HALO:HALO + tr, :] = t_ref[...]


def _taps(ext, w_ref, tr):
    base = HALO - (CONV_K - 1)
    acc = ext[base:base + tr, :] * w_ref[0:1, :]
    for k in range(1, CONV_K):
        acc = acc + ext[base + k:base + k + tr, :] * w_ref[k:k + 1, :]
    return acc


def _conv_fwd(t, w, b, name="conv_fwd"):
    s_, c = t.shape
    tr = _row_tile(s_)

    def body(t_ref, h_ref, w_ref, b_ref, o_ref, ext):
        _fill_before(ext, t_ref, h_ref, tr)
        o_ref[...] = _silu(_taps(ext, w_ref, tr) + b_ref[...])

    main, halo = _conv_specs(s_, c, tr, False)
    return pl.pallas_call(
        body, name=name, grid=(s_ // tr,),
        in_specs=[main, halo, pl.BlockSpec((CONV_K, c), lambda i: (0, 0)), pl.BlockSpec((1, c), lambda i: (0, 0))],
        out_specs=main, out_shape=jax.ShapeDtypeStruct((s_, c), F32),
        scratch_shapes=[pltpu.VMEM((tr + HALO, c), F32)], compiler_params=_params("parallel"),
    )(t, t, w, b)


def _conv_bwd_pre(t, w, b, dact, name="conv_bwd_pre"):
    s_, c = t.shape
    tr = _row_tile(s_)

    def body(t_ref, h_ref, w_ref, b_ref, da_ref, dpre_ref, dwb_ref, ext):
        @pl.when(pl.program_id(0) == 0)
        def _():
            dwb_ref[...] = jnp.zeros_like(dwb_ref)

        _fill_before(ext, t_ref, h_ref, tr)
        dpre = da_ref[...] * _dsilu(_taps(ext, w_ref, tr) + b_ref[...])
        dpre_ref[...] = dpre
        base = HALO - (CONV_K - 1)
        for k in range(CONV_K):
            dwb_ref[k:k + 1, :] += jnp.sum(dpre * ext[base + k:base + k + tr, :], axis=0, keepdims=True)
        dwb_ref[CONV_K:CONV_K + 1, :] += jnp.sum(dpre, axis=0, keepdims=True)

    main, halo = _conv_specs(s_, c, tr, False)
    return pl.pallas_call(
        body, name=name, grid=(s_ // tr,),
        in_specs=[main, halo, pl.BlockSpec((CONV_K, c), lambda i: (0, 0)), pl.BlockSpec((1, c), lambda i: (0, 0)), main],
        out_specs=[main, pl.BlockSpec((8, c), lambda i: (0, 0))],
        out_shape=[jax.ShapeDtypeStruct((s_, c), F32), jax.ShapeDtypeStruct((8, c), F32)],
        scratch_shapes=[pltpu.VMEM((tr + HALO, c), F32)], compiler_params=_params("arbitrary"),
    )(t, t, w, b, dact)


def _conv_bwd_in(dpre, w, name="conv_bwd_in"):
    s_, c = dpre.shape
    tr = _row_tile(s_)
    nt = s_ // tr

    def body(d_ref, h_ref, w_ref, o_ref, ext):
        ext[0:tr, :] = d_ref[...]
        ext[tr:tr + HALO, :] = jnp.where(pl.program_id(0) < nt - 1, h_ref[...], 0.0)
        acc = ext[CONV_K - 1:CONV_K - 1 + tr, :] * w_ref[0:1, :]
        for k in range(1, CONV_K):
            acc = acc + ext[CONV_K - 1 - k:CONV_K - 1 - k + tr, :] * w_ref[k:k + 1, :]
        o_ref[...] = acc.astype(o_ref.dtype)

    main, halo = _conv_specs(s_, c, tr, True)
    return pl.pallas_call(
        body, name=name, grid=(nt,),
        in_specs=[main, halo, pl.BlockSpec((CONV_K, c), lambda i: (0, 0))],
        out_specs=main, out_shape=jax.ShapeDtypeStruct((s_, c), MXU_DTYPE),
        scratch_shapes=[pltpu.VMEM((tr + HALO, c), F32)], compiler_params=_params("parallel"),
    )(dpre, dpre, w)


def _ssd_chunk_common(dt_ref, dtt_ref, br_ref, bc_ref, ar_ref, ac_ref):
    li = lax.broadcasted_iota(jnp.int32, (CHUNK, CHUNK), 0)
    si = lax.broadcasted_iota(jnp.int32, (CHUNK, CHUNK), 1)
    lower = li >= si
    lower_b = lower.astype(BF16)
    upper_b = (li <= si).astype(BF16)
    zr = dt_ref[...] + br_ref[...]
    dtc = _softplus(zr)
    a_row = -jnp.exp(ar_ref[...])
    acum = _exact_dot(lower_b, dtc * a_row, 1, 0, False)
    dtt = _softplus(dtt_ref[...] + bc_ref[...])
    acum_t = _exact_dot(dtt * (-jnp.exp(ac_ref[...])), upper_b, 1, 0, True)
    return lower, upper_b, zr, dtc, a_row, acum, acum_t


def _head_terms(h, lower, dtc, acum, acum_t):
    lane = lax.broadcasted_iota(jnp.int32, (1, LANE), 1)
    sub = lax.broadcasted_iota(jnp.int32, (SSD_H, 1), 0)
    rowid = lax.broadcasted_iota(jnp.int32, (CHUNK, 1), 0)
    oh = (lane == HEAD_LANE + h).astype(F32)
    acol = jnp.sum(acum * oh, axis=1, keepdims=True)
    dcol = jnp.sum(dtc * oh, axis=1, keepdims=True)
    arow = jnp.sum(acum_t * (sub == h).astype(F32), axis=0, keepdims=True)
    alast = jnp.sum(jnp.where(rowid == CHUNK - 1, acol, 0.0), axis=0, keepdims=True)
    decay = jnp.exp(jnp.where(lower, acol - arow, -jnp.inf))
    return oh, acol, dcol, alast, decay


SSD_PAIRS = SSD_H // 2
PAIRS_PER_GROUP = SSD_E // 2


def _ps(q):
    return slice(q * LANE, (q + 1) * LANE)


def _gs(off, g):
    return slice(off + g * SSD_N, off + (g + 1) * SSD_N)


def _lanes(c0, c1):
    return jnp.where(lax.broadcasted_iota(jnp.int32, (1, LANE), 1) < SSD_P, c0, c1)


def _rows(c0, c1):
    return jnp.where(lax.broadcasted_iota(jnp.int32, (LANE, 1), 0) < SSD_P, c0, c1)


def _lane_halves(t):
    first = lax.broadcasted_iota(jnp.int32, (1, LANE), 1) < SSD_P
    return (jnp.sum(jnp.where(first, t, 0.0), axis=1, keepdims=True),
            jnp.sum(jnp.where(first, 0.0, t), axis=1, keepdims=True))


def _ssd_in_specs(rev):
    def ci(c):
        return c if rev is None else rev - c
    return [pl.BlockSpec((CHUNK, CONV_DIM), lambda c: (ci(c), 0)),
            pl.BlockSpec((CHUNK, LANE), lambda c: (ci(c), 0)),
            pl.BlockSpec((SSD_H, CHUNK), lambda c: (0, ci(c))),
            pl.BlockSpec((1, LANE), lambda c: (0, 0)), pl.BlockSpec((SSD_H, 1), lambda c: (0, 0)),
            pl.BlockSpec((1, LANE), lambda c: (0, 0)), pl.BlockSpec((SSD_H, 1), lambda c: (0, 0)),
            pl.BlockSpec((SSD_PAIRS, 1, LANE), lambda c: (0, 0, 0))]


def _ssd_fwd(xbc, small, dtt, bias_r, bias_c, alog_r, alog_c, dsk, name="ssd_fwd"):
    s_ = xbc.shape[0]
    nc = s_ // CHUNK

    def body(x_ref, dt_ref, dtt_ref, br_ref, bc_ref, ar_ref, ac_ref, dsk_ref, y_ref, prev_ref, state):
        @pl.when(pl.program_id(0) == 0)
        def _():
            state[...] = jnp.zeros_like(state)

        lower, _, _, dtc, _, acum, acum_t = _ssd_chunk_common(dt_ref, dtt_ref, br_ref, bc_ref, ar_ref, ac_ref)
        for g in range(SSD_G):
            bb = _mx(x_ref[:, _gs(B_OFF, g)])
            cb_ = _mx(x_ref[:, _gs(C_OFF, g)])
            cbm = _dot(cb_, bb, 1, 1)
            for e in range(PAIRS_PER_GROUP):
                q = g * PAIRS_PER_GROUP + e
                _, acol0, dcol0, alast0, decay0 = _head_terms(2 * q, lower, dtc, acum, acum_t)
                _, acol1, dcol1, alast1, decay1 = _head_terms(2 * q + 1, lower, dtc, acum, acum_t)
                x = x_ref[:, _ps(q)]
                xdt = x * _lanes(dcol0, dcol1)
                xb = _mx(xdt)
                yd = _lanes(_dot(_mx(cbm * decay0), xb, 1, 0), _dot(_mx(cbm * decay1), xb, 1, 0))
                prev = state[q]
                prev_ref[0, q] = prev
                yo = _dot(cb_, _mx(prev), 1, 1) * _lanes(jnp.exp(acol0), jnp.exp(acol1))
                ds = _lanes(jnp.exp(alast0 - acol0), jnp.exp(alast1 - acol1))
                st = _dot(_mx(xdt * ds), bb, 0, 0)
                state[q] = prev * _rows(jnp.exp(alast0), jnp.exp(alast1)) + st
                y_ref[:, _ps(q)] = yd + yo + x * dsk_ref[q]

    psp = pl.BlockSpec((1, SSD_PAIRS, LANE, SSD_N), lambda c: (c, 0, 0, 0))
    return pl.pallas_call(
        body, name=name, grid=(nc,),
        in_specs=_ssd_in_specs(None), out_specs=[pl.BlockSpec((CHUNK, SSD_W), lambda c: (c, 0)), psp],
        out_shape=[jax.ShapeDtypeStruct((s_, SSD_W), F32),
                   jax.ShapeDtypeStruct((nc, SSD_PAIRS, LANE, SSD_N), F32)],
        scratch_shapes=[pltpu.VMEM((SSD_PAIRS, LANE, SSD_N), F32)],
        compiler_params=_params("arbitrary"),
    )(xbc, small, dtt, bias_r, bias_c, alog_r, alog_c, dsk)


def _ssd_bwd(xbc, small, dtt, bias_r, bias_c, alog_r, alog_c, dsk, prev, dy, name="ssd_bwd"):
    s_ = xbc.shape[0]
    nc = s_ // CHUNK

    def body(x_ref, dt_ref, dtt_ref, br_ref, bc_ref, ar_ref, ac_ref, dsk_ref, prev_ref, dy_ref,
             dx_ref, ddt_ref, dpar_ref, dstate):
        @pl.when(pl.program_id(0) == 0)
        def _():
            dstate[...] = jnp.zeros_like(dstate)
            dpar_ref[...] = jnp.zeros_like(dpar_ref)

        lower, upper_b, zr, dtc, a_row, acum, acum_t = _ssd_chunk_common(
            dt_ref, dtt_ref, br_ref, bc_ref, ar_ref, ac_ref)
        strict = (lax.broadcasted_iota(jnp.int32, (CHUNK, CHUNK), 1)
                  < lax.broadcasted_iota(jnp.int32, (CHUNK, CHUNK), 0))
        strict_b = strict.astype(BF16)
        col2 = lax.broadcasted_iota(jnp.int32, (CHUNK, 2 * CHUNK), 1)
        strict2 = (jnp.where(col2 >= CHUNK, col2 - CHUNK, col2)
                   < lax.broadcasted_iota(jnp.int32, (CHUNK, 2 * CHUNK), 0))
        da_in = jnp.zeros((CHUNK, LANE), F32)
        r_off = jnp.zeros((CHUNK, LANE), F32)
        c_int = jnp.zeros((CHUNK, LANE), F32)
        c_row = jnp.zeros((1, LANE), F32)
        ddt = jnp.zeros((CHUNK, LANE), F32)
        dskip = jnp.zeros((1, LANE), F32)
        for g in range(SSD_G):
            bb = _mx(x_ref[:, _gs(B_OFF, g)])
            cb_ = _mx(x_ref[:, _gs(C_OFF, g)])
            cbm = _dot(cb_, bb, 1, 1)
            dcb = jnp.zeros((CHUNK, CHUNK), F32)
            dc_acc = jnp.zeros((CHUNK, SSD_N), F32)
            db_acc = jnp.zeros((CHUNK, SSD_N), F32)
            for e in range(PAIRS_PER_GROUP):
                q = g * PAIRS_PER_GROUP + e
                oh0, acol0, dcol0, alast0, decay0 = _head_terms(2 * q, lower, dtc, acum, acum_t)
                oh1, acol1, dcol1, alast1, decay1 = _head_terms(2 * q + 1, lower, dtc, acum, acum_t)
                x = x_ref[:, _ps(q)]
                dy = dy_ref[:, _ps(q)]
                dcol = _lanes(dcol0, dcol1)
                xdt = x * dcol
                xb = _mx(xdt)
                eacol = _lanes(jnp.exp(acol0), jnp.exp(acol1))
                ds = _lanes(jnp.exp(alast0 - acol0), jnp.exp(alast1 - acol1))
                ealast = _rows(jnp.exp(alast0), jnp.exp(alast1))
                dyb = _mx(dy)
                dyb0, dyb1 = _mx(_lanes(dy, 0.0)), _mx(_lanes(0.0, dy))
                dsh = dstate[q]
                dshb = _mx(dsh)
                prev = prev_ref[0, q]
                prevb = _mx(prev)
                dxdt_inter = ds * _dot(bb, dshb, 1, 1)
                dxdt = _lanes(_dot(_mx(cbm * decay0), dyb, 0, 0), _dot(_mx(cbm * decay1), dyb, 0, 0)) + dxdt_inter
                dwl0 = _dot(dyb0, xb, 1, 1) * decay0
                dwl1 = _dot(dyb1, xb, 1, 1) * decay1
                dcb = dcb + dwl0 + dwl1
                dyeb = _mx(dy * eacol)
                dc_acc = dc_acc + _dot(dyeb, prevb, 1, 0)
                db_acc = db_acc + _dot(_mx(xdt * ds), dshb, 1, 0)
                dstate[q] = _dot(dyeb, cb_, 0, 0) + ealast * dsh
                above = _exact_dot(upper_b, jnp.concatenate([dwl0 * cbm, dwl1 * cbm], axis=1), 1, 0, False)
                above = jnp.where(strict2, above, 0.0)
                da_in = (da_in + jnp.sum(above[:, :CHUNK], axis=1, keepdims=True) * oh0
                         + jnp.sum(above[:, CHUNK:], axis=1, keepdims=True) * oh1)
                y_off = _dot(cb_, prevb, 1, 1) * eacol
                r0, r1 = _lane_halves(dy * y_off)
                r_off = r_off + r0 * oh0 + r1 * oh1
                c0, c1 = _lane_halves(xdt * dxdt_inter)
                c_int = c_int + c0 * oh0 + c1 * oh1
                both = jnp.sum(dsh * prev, axis=1, keepdims=True) * ealast
                c_row = (c_row + jnp.sum(_rows(both, 0.0), axis=0, keepdims=True) * oh0
                         + jnp.sum(_rows(0.0, both), axis=0, keepdims=True) * oh1)
                t0, t1 = _lane_halves(dxdt * x)
                ddt = ddt + t0 * oh0 + t1 * oh1
                dx_ref[:, _ps(q)] = dxdt * dcol + dy * dsk_ref[q]
                k0, k1 = _lane_halves(dy * x)
                dskip = (dskip + jnp.sum(k0, axis=0, keepdims=True) * oh0 + jnp.sum(k1, axis=0, keepdims=True) * oh1)
            dcbb = _mx(dcb)
            dx_ref[:, _gs(C_OFF, g)] = dc_acc + _dot(dcbb, bb, 1, 0)
            dx_ref[:, _gs(B_OFF, g)] = db_acc + _dot(dcbb, cb_, 0, 0)
        da = (da_in + _exact_dot(upper_b, r_off, 1, 0, False) + _exact_dot(strict_b, c_int, 1, 0, False) + c_row)
        draw = (ddt + da * a_row) * _sigmoid(zr)
        ddt_ref[...] = draw
        dpar_ref[0:1, :] += jnp.sum(draw, axis=0, keepdims=True)
        dpar_ref[1:2, :] += jnp.sum(da * dtc, axis=0, keepdims=True) * a_row
        dpar_ref[2:3, :] += dskip

    rev = nc - 1
    psp = pl.BlockSpec((1, SSD_PAIRS, LANE, SSD_N), lambda c: (rev - c, 0, 0, 0))
    return pl.pallas_call(
        body, name=name, grid=(nc,),
        in_specs=_ssd_in_specs(rev) + [psp, pl.BlockSpec((CHUNK, SSD_W), lambda c: (rev - c, 0))],
        out_specs=[pl.BlockSpec((CHUNK, CONV_DIM), lambda c: (rev - c, 0)),
                   pl.BlockSpec((CHUNK, LANE), lambda c: (rev - c, 0)), pl.BlockSpec((8, LANE), lambda c: (0, 0))],
        out_shape=[jax.ShapeDtypeStruct((s_, CONV_DIM), F32), jax.ShapeDtypeStruct((s_, LANE), F32),
                   jax.ShapeDtypeStruct((8, LANE), F32)],
        scratch_shapes=[pltpu.VMEM((SSD_PAIRS, LANE, SSD_N), F32)],
        compiler_params=_params("arbitrary"),
    )(xbc, small, dtt, bias_r, bias_c, alog_r, alog_c, dsk, prev, dy)


GN = SSD_W // SSD_G


def _gated_norm_fwd(y, z, w, cat, name="gated_norm_fwd"):
    s_, f = y.shape
    tr = _row_tile(s_)

    def body(y_ref, z_ref, w_ref, cat_ref, o_ref):
        for g in range(SSD_G):
            sl = slice(g * GN, (g + 1) * GN)
            gg = y_ref[:, sl] * _silu(z_ref[:, sl])
            r = lax.rsqrt(jnp.mean(gg * gg, axis=-1, keepdims=True) + EPS)
            o_ref[:, sl] = (gg * r * w_ref[:, sl]).astype(o_ref.dtype)

    row = pl.BlockSpec((tr, f), lambda i: (i, 0))
    wsp = pl.BlockSpec((1, f), lambda i: (0, 0))
    return pl.pallas_call(
        body, name=name, grid=(s_ // tr,),
        in_specs=[row, row, wsp, pl.BlockSpec(memory_space=pl.ANY)], out_specs=pl.BlockSpec((tr, f), lambda i: (i, 1)),
        out_shape=jax.ShapeDtypeStruct(cat.shape, cat.dtype), input_output_aliases={3: 0},
        compiler_params=_params("parallel"),
    )(y, z, w.reshape(1, f), cat)


def _gated_norm_bwd(y, z, w, dout, name="gated_norm_bwd"):
    s_, f = y.shape
    tr = _row_tile(s_)

    def body(y_ref, z_ref, w_ref, do_ref, dy_ref, dz_ref, dw_ref):
        @pl.when(pl.program_id(0) == 0)
        def _():
            dw_ref[...] = jnp.zeros_like(dw_ref)

        for g in range(SSD_G):
            sl = slice(g * GN, (g + 1) * GN)
            yv = y_ref[:, sl]
            zv = z_ref[:, sl]
            dov = do_ref[:, sl].astype(F32)
            sz = _silu(zv)
            gg = yv * sz
            r = lax.rsqrt(jnp.mean(gg * gg, axis=-1, keepdims=True) + EPS)
            gw = dov * w_ref[:, sl]
            c = jnp.mean(gw * gg, axis=-1, keepdims=True)
            dgg = r * gw - gg * (r * r * r * c)
            dy_ref[:, sl] = dgg * sz
            dz_ref[:, sl] = (dgg * yv * _dsilu(zv)).astype(dz_ref.dtype)
            dw_ref[:, sl] += jnp.sum(dov * gg * r, axis=0, keepdims=True)

    row = pl.BlockSpec((tr, f), lambda i: (i, 0))
    wsp = pl.BlockSpec((1, f), lambda i: (0, 0))
    return pl.pallas_call(
        body, name=name, grid=(s_ // tr,),
        in_specs=[row, row, wsp, pl.BlockSpec((tr, f), lambda i: (i, 1))], out_specs=[row, row, wsp],
        out_shape=[jax.ShapeDtypeStruct((s_, f), F32), jax.ShapeDtypeStruct((s_, f), MXU_DTYPE),
                   jax.ShapeDtypeStruct((1, f), F32)],
        compiler_params=_params("arbitrary"),
    )(y, z, w.reshape(1, f), dout)


def _ffn_fwd(vv, w_gate, w_up, name="ffn_gate_up"):
    s_, d = vv.shape
    nb, _, f8 = w_gate.shape
    tm = _pick(s_, (1024, 512, 256, 128))

    def body(v_ref, wg_ref, wu_ref, g_ref, u_ref, a_ref):
        for rs in _row_slices(tm, 16):
            a = _mx(v_ref[rs, :])
            g = _dot(a, _mx(wg_ref[...]), 1, 0)
            u = _dot(a, _mx(wu_ref[...]), 1, 0)
            g_ref[rs, :] = g.astype(g_ref.dtype)
            u_ref[rs, :] = u.astype(u_ref.dtype)
            a_ref[rs, :] = (_silu(g) * u).astype(a_ref.dtype)

    wsp = pl.BlockSpec((None, d, f8), lambda j, i: (j, 0, 0))
    osp = pl.BlockSpec((None, tm, f8), lambda j, i: (j, i, 0))
    return pl.pallas_call(
        body, name=name, grid=(nb, s_ // tm),
        in_specs=[pl.BlockSpec((tm, d), lambda j, i: (i, 0)), wsp, wsp], out_specs=[osp] * 3,
        out_shape=[jax.ShapeDtypeStruct((nb, s_, f8), MXU_DTYPE)] * 3,
        compiler_params=_params("parallel", "parallel"),
    )(vv, w_gate, w_up)


def _ffn_bwd_act(dffn, w_down, gate, up, name="ffn_d_act"):
    s_, d = dffn.shape
    nb, f8, _ = w_down.shape
    tm = _pick(s_, (1024, 512, 256, 128))

    def body(d_ref, w_ref, g_ref, u_ref, dg_ref, du_ref):
        for rs in _row_slices(tm, 16):
            dact = _dot(_mx(d_ref[rs, :]), _mx(w_ref[...]), 1, 1)
            g = g_ref[rs, :].astype(F32)
            s = _sigmoid(g)
            dg_ref[rs, :] = (dact * u_ref[rs, :].astype(F32) * (s * (1.0 + g * (1.0 - s)))).astype(dg_ref.dtype)
            du_ref[rs, :] = (dact * (g * s)).astype(du_ref.dtype)

    osp = pl.BlockSpec((None, tm, f8), lambda j, i: (j, i, 0))
    return pl.pallas_call(
        body, name=name, grid=(nb, s_ // tm),
        in_specs=[pl.BlockSpec((tm, d), lambda j, i: (i, 0)), pl.BlockSpec((None, f8, d), lambda j, i: (j, 0, 0)),
                  osp, osp],
        out_specs=[osp, osp], out_shape=[jax.ShapeDtypeStruct((nb, s_, f8), MXU_DTYPE)] * 2,
        compiler_params=_params("parallel", "parallel"),
    )(dffn, w_down, gate, up)


def _ffn_bwd_in(dgate, w_gate, dup, w_up, name="ffn_d_in"):
    nb, s_, f8 = dgate.shape
    d = w_gate.shape[1]
    tm = _pick(s_, (1024, 512, 256, 128))
    tn = _pick(d, (1024, 512, 256, 128))

    def body(dg_ref, wg_ref, du_ref, wu_ref, o_ref, acc):
        j = pl.program_id(2)

        @pl.when(j == 0)
        def _():
            acc[...] = jnp.zeros_like(acc)

        for rs in _row_slices(tm, 16):
            acc[rs, :] += (_dot(_mx(dg_ref[rs, :]), _mx(wg_ref[...]), 1, 1)
                           + _dot(_mx(du_ref[rs, :]), _mx(wu_ref[...]), 1, 1))

        @pl.when(j == nb - 1)
        def _():
            o_ref[...] = acc[...]

    asp = pl.BlockSpec((None, tm, f8), lambda i, n, j: (j, i, 0))
    wsp = pl.BlockSpec((None, tn, f8), lambda i, n, j: (j, n, 0))
    return pl.pallas_call(
        body, name=name, grid=(s_ // tm, d // tn, nb),
        in_specs=[asp, wsp, asp, wsp], out_specs=pl.BlockSpec((tm, tn), lambda i, n, j: (i, n)),
        out_shape=jax.ShapeDtypeStruct((s_, d), F32), scratch_shapes=[pltpu.VMEM((tm, tn), F32)],
        compiler_params=_params("parallel", "parallel", "arbitrary"),
    )(dgate, w_gate, dup, w_up)


def _adam_math(g, w, m, v):
    m2 = ADAM_B1 * m + (1.0 - ADAM_B1) * g
    v2 = ADAM_B2 * v + (1.0 - ADAM_B2) * (g * g)
    m_hat = m2 / (1.0 - ADAM_B1 ** ADAM_STEP)
    v_hat = v2 / (1.0 - ADAM_B2 ** ADAM_STEP)
    delta = -ADAM_LR * (m_hat / (jnp.sqrt(v_hat) + ADAM_EPS) + ADAM_WD * w)
    return delta, m2, v2


def _adamw(parts, w, m, v, name="adamw"):
    nd, r_, c = parts.shape
    tr = _pick(r_, (128, 64, 32, 16, 8))

    def body(p_ref, w_ref, m_ref, v_ref, g_ref, d_ref, m2_ref, v2_ref):
        g = p_ref[0].astype(F32)
        for i in range(1, nd):
            g = g + p_ref[i].astype(F32)
        delta, m2, v2 = _adam_math(g, w_ref[...], m_ref[...], v_ref[...])
        g_ref[...] = g
        d_ref[...] = delta
        m2_ref[...] = m2
        v2_ref[...] = v2

    row = pl.BlockSpec((tr, c), lambda i: (i, 0))
    psp = pl.BlockSpec((nd, tr, c), lambda i: (0, i, 0))
    return pl.pallas_call(
        body, name=name, grid=(r_ // tr,), in_specs=[psp, row, row, row], out_specs=[row] * 4,
        out_shape=[jax.ShapeDtypeStruct((r_, c), F32)] * 4, compiler_params=_params("parallel"),
    )(parts, w, m, v)


def _adamw_small(parts, w, m, v, name="adamw_small"):
    nd = parts.shape[0]

    def body(p_ref, w_ref, m_ref, v_ref, g_ref, d_ref, m2_ref, v2_ref):
        g = p_ref[0]
        for i in range(1, nd):
            g = g + p_ref[i]
        delta, m2, v2 = _adam_math(g, w_ref[...], m_ref[...], v_ref[...])
        g_ref[...] = g
        d_ref[...] = delta
        m2_ref[...] = m2
        v2_ref[...] = v2

    return pl.pallas_call(
        body, name=name, out_shape=[jax.ShapeDtypeStruct(w.shape, F32)] * 4,
        compiler_params=pltpu.CompilerParams(vmem_limit_bytes=VMEM_LIMIT_BYTES),
    )(parts, w, m, v)


_HBM = pl.BlockSpec(memory_space=pltpu.HBM)
_MESH = pl.DeviceIdType.MESH


def _all_gather(xs, name):
    na = len(xs)

    def body(*refs):
        x_refs, out_refs = refs[:na], refs[na:2 * na]
        send_sems, recv_sems, local_sems = refs[2 * na:]
        x, y, c = lax.axis_index("x"), lax.axis_index("y"), lax.axis_index("c")
        me, sibling = (x, y, c), (x, y, 1 - c)
        chips = [(1 - x, y), (x, 1 - y), (1 - x, 1 - y)]

        def slot(a, px, py, pc):
            return out_refs[a].at[4 * px + 2 * py + pc]

        def copy(a, k, block, to, src=None):
            return pltpu.make_async_remote_copy(
                src_ref=slot(a, *block) if src is None else src, dst_ref=slot(a, *block),
                send_sem=send_sems.at[a, k], recv_sem=recv_sems.at[a, k], device_id=to, device_id_type=_MESH)

        mine = [pltpu.make_async_copy(x_refs[a], slot(a, *me), local_sems.at[a]) for a in range(na)]
        started = []
        for a in range(na):
            mine[a].start()
            first = [copy(a, 0, me, sibling, src=x_refs[a])]
            first += [copy(a, 1 + j, me, (*chip, c), src=x_refs[a]) for j, chip in enumerate(chips)]
            for cp in first:
                cp.start()
            started += first
        for a in range(na):
            for j, chip in enumerate(chips):
                copy(a, 1 + j, (*chip, c), me).wait_recv()
                fwd = copy(a, 4 + j, (*chip, c), sibling)
                fwd.start()
                started.append(fwd)
        for a in range(na):
            copy(a, 0, sibling, me).wait_recv()
            for j, chip in enumerate(chips):
                copy(a, 4 + j, (*chip, 1 - c), me).wait_recv()
        for cp in started:
            cp.wait_send()
        for cp in mine:
            cp.wait()

    return pl.pallas_call(
        body, name=name, out_shape=[jax.ShapeDtypeStruct((N_DEV,) + t.shape, t.dtype) for t in xs],
        in_specs=[_HBM] * na, out_specs=[_HBM] * na,
        scratch_shapes=[pltpu.SemaphoreType.DMA((na, 7)), pltpu.SemaphoreType.DMA((na, 7)),
                        pltpu.SemaphoreType.DMA((na,))],
    )(*xs)


_SEM = pl.BlockSpec(memory_space=pltpu.SEMAPHORE)
_EFFECT = pltpu.SideEffectType.DATAFLOW_SIDE_EFFECTING


def _peers(x, y, c):
    out = []
    for k in range(1, N_DEV):
        px = 1 - x if k & 4 else x
        py = 1 - y if k & 2 else y
        pc = 1 - c if k & 1 else c
        out.append(((px, py, pc), 4 * px + 2 * py + pc))
    return out


def _push_copies(scatter, src_refs, land_refs, send_sems, recv_sems):
    x, y, c = lax.axis_index("x"), lax.axis_index("y"), lax.axis_index("c")
    me = 4 * x + 2 * y + c
    pairs = []
    for a, (src, land) in enumerate(zip(src_refs, land_refs)):
        for k, (peer, slot) in enumerate(_peers(x, y, c)):
            out_src = src.at[slot] if scatter else src
            si = a * (N_DEV - 1) + k
            send = pltpu.make_async_remote_copy(src_ref=out_src, dst_ref=land.at[me], send_sem=send_sems.at[si],
                                                recv_sem=recv_sems.at[si], device_id=peer, device_id_type=_MESH)
            recv = pltpu.make_async_remote_copy(src_ref=out_src, dst_ref=land.at[slot], send_sem=send_sems.at[si],
                                                recv_sem=recv_sems.at[si], device_id=peer, device_id_type=_MESH)
            pairs.append((send, recv))
    return pairs


def _push_start(srcs, scatter, dep, name):
    na = len(srcs)
    shapes = [t.shape[1:] if scatter else t.shape for t in srcs]
    lands = [pltpu.with_memory_space_constraint(lax.empty((N_DEV,) + s, t.dtype), pltpu.HBM) for s, t in zip(shapes, srcs)]

    def body(*refs):
        src_refs, land_refs = refs[:na], refs[na:2 * na]
        send_sems, recv_sems = refs[2 * na + 1], refs[2 * na + 2]
        token = refs[-1]
        for send, _ in _push_copies(scatter, src_refs, land_refs, send_sems, recv_sems):
            send.start()
        token[...] = jnp.zeros_like(token)

    sem = pltpu.SemaphoreType.DMA((na * (N_DEV - 1),))
    outs = pl.pallas_call(
        body, name=name,
        out_shape=(sem, sem) + tuple(pltpu.HBM(t.shape, t.dtype) for t in srcs)
        + tuple(pltpu.HBM(t.shape, t.dtype) for t in lands) + (jax.ShapeDtypeStruct((8, LANE), F32),),
        in_specs=[_HBM] * (2 * na) + [pl.BlockSpec(memory_space=pl.ANY)],
        out_specs=(_SEM, _SEM) + (_HBM,) * (2 * na) + (pl.BlockSpec(memory_space=pltpu.VMEM),),
        input_output_aliases={i: 2 + i for i in range(2 * na)},
        compiler_params=pltpu.CompilerParams(has_side_effects=_EFFECT),
    )(*[pltpu.with_memory_space_constraint(t, pltpu.HBM) for t in srcs], *lands, dep)
    return outs[0], outs[1], outs[2:2 + na], outs[2 + na:2 + 2 * na], outs[-1]


def _push_wait(send_sems, recv_sems, src_thru, land_thru, scatter, after, name):
    na = len(src_thru)

    def body(*refs):
        src_refs, land_refs = refs[:na], refs[na:2 * na]
        ssem, rsem = refs[2 * na], refs[2 * na + 1]
        for send, recv in _push_copies(scatter, src_refs, land_refs, ssem, rsem):
            send.wait_send()
            recv.wait_recv()

    outs = pl.pallas_call(
        body, name=name,
        out_shape=tuple(pltpu.HBM(t.shape, t.dtype) for t in src_thru) + tuple(pltpu.HBM(t.shape, t.dtype) for t in land_thru),
        in_specs=[_HBM] * (2 * na) + [_SEM, _SEM, pl.BlockSpec(memory_space=pl.ANY)],
        out_specs=(_HBM,) * (2 * na),
        input_output_aliases={i: i for i in range(2 * na)},
        compiler_params=pltpu.CompilerParams(has_side_effects=_EFFECT),
    )(*src_thru, *land_thru, send_sems, recv_sems, after)
    return outs[:na], outs[na:]


def _exchange_behind(srcs, scatter, dep, name):
    send_sems, recv_sems, thru, lands, token = _push_start(srcs, scatter, dep, name + "_start")

    def finish(after):
        src_done, land_done = _push_wait(send_sems, recv_sems, thru, lands, scatter, after, name + "_wait")
        return _place_own(land_done, src_done, scatter, name + "_own")

    return token[0, 0], finish


def _place_own(lands, srcs, scatter, name):
    me = (4 * lax.axis_index("x") + 2 * lax.axis_index("y") + lax.axis_index("c")).astype(jnp.int32).reshape(1)
    outs = []
    for a, (land, src) in enumerate(zip(lands, srcs)):
        r_, c_ = land.shape[1:]
        tr = _pick(r_, (512, 256, 128, 64, 32, 16))

        def body(me_ref, land_ref, src_ref, out_ref):
            out_ref[...] = src_ref[...]

        src_spec = (pl.BlockSpec((None, tr, c_), lambda i, me_: (me_[0], i, 0)) if scatter
                    else pl.BlockSpec((tr, c_), lambda i, me_: (i, 0)))
        gs = pltpu.PrefetchScalarGridSpec(
            num_scalar_prefetch=1, grid=(r_ // tr,),
            in_specs=[pl.BlockSpec(memory_space=pl.ANY), src_spec],
            out_specs=pl.BlockSpec((None, tr, c_), lambda i, me_: (me_[0], i, 0)))
        outs.append(pl.pallas_call(
            body, name=f"{name}_{a}", grid_spec=gs, out_shape=jax.ShapeDtypeStruct(land.shape, land.dtype),
            input_output_aliases={1: 0}, compiler_params=_params("arbitrary"),
        )(me, land, src))
    return outs


_BIG = (("w_in", D_MODEL, D_IN, 1), ("w_uq", Q_RANK, HEADS * QK, 1), ("w_ukv", KV_RANK, HEADS * (NOPE + VDIM), 1),
        ("w_out", D_MODEL, D_MODEL, 0), ("w_gate", D_MODEL, D_FF, 1), ("w_up", D_MODEL, D_FF, 1),
        ("w_down", D_FF, D_MODEL, 0))
_CQKV = (0, Q_RANK + KV_RANK)
_KR = (_CQKV[1], _CQKV[1] + ROPE)
_Z = (_KR[1], _KR[1] + SSD_W)
_XBC = (_Z[1], _Z[1] + CONV_DIM)
_DT = (_XBC[1], _XBC[1] + SSD_H)


def _win_segments(w_in_g):
    w = jnp.transpose(w_in_g, (1, 0, 2)).reshape(D_MODEL, D_IN)
    small = jnp.concatenate([w[:, _KR[0]:_KR[1]], w[:, _DT[0]:_DT[1]],
                             jnp.zeros((D_MODEL, LANE - ROPE - SSD_H), w.dtype)], axis=1)
    return w[:, _CQKV[0]:_CQKV[1]], w[:, _Z[0]:_Z[1]], w[:, _XBC[0]:_XBC[1]], small


def _win_from_segments(g_cqkv, g_z, g_xbc, g_small):
    w = jnp.concatenate([g_cqkv, g_small[:, :ROPE], g_z, g_xbc, g_small[:, ROPE:ROPE + SSD_H]], axis=1)
    return jnp.transpose(w.reshape(D_MODEL, N_DEV, D_IN // N_DEV), (1, 0, 2))


_SMALL = (("q_norm_w", 512), ("kv_norm_w", 512), ("conv_b", CONV_DIM), ("dt_bias", SSD_H), ("a_log", SSD_H),
          ("d_skip", SSD_H), ("ssd_norm_w", SSD_W), ("attn_out_norm_w", 1024), ("pre_mix_norm_w", D_MODEL),
          ("post_mix_norm_w", D_MODEL), ("pre_ffn_norm_w", D_MODEL), ("post_ffn_norm_w", D_MODEL),
          ("conv_w", CONV_K * CONV_DIM))
_SMALL_ROWS = -(-sum(-(-n // LANE) for _, n in _SMALL) // 8) * 8


def _pack_small(vals):
    rows = []
    for name, n in _SMALL:
        v = vals[name].reshape(-1).astype(F32)
        pad = -(-n // LANE) * LANE
        rows.append(jnp.pad(v, (0, pad - n)).reshape(-1, LANE))
    m = jnp.concatenate(rows, axis=0)
    return jnp.pad(m, ((0, _SMALL_ROWS - m.shape[0]), (0, 0)))


def _unpack_small(m):
    out, r = {}, 0
    for name, n in _SMALL:
        nr = -(-n // LANE)
        out[name] = m[r:r + nr].reshape(-1)[:n]
        r += nr
    return out


def _head_row(v):
    return jnp.pad(v.reshape(1, -1).astype(F32), ((0, 0), (HEAD_LANE, LANE - HEAD_LANE - v.shape[-1])))


def _local_step(x, positions, target, wg, small, weights, on_grads):
    w_cqkv, w_z, w_xbc, w_small = _win_segments(wg["w_in"])
    conv_w = wg["conv_w"]
    conv_b = small["conv_b"].reshape(1, CONV_DIM)
    qkv_norm_w = jnp.concatenate([small["q_norm_w"], small["kv_norm_w"]])
    attn_norm_w = small["attn_out_norm_w"].reshape(1, HEADS * VDIM)
    scale = QK ** -0.5

    inv_freq = ROPE_THETA ** (-jnp.arange(0, ROPE, 2, dtype=F32) / ROPE)
    ang = positions.astype(F32)[:, None] * inv_freq
    cos2 = jnp.tile(jnp.cos(ang), (1, 2))
    sin2 = jnp.tile(jnp.sin(ang), (1, 2))

    u = _rms_fwd(x, small["pre_mix_norm_w"], out_dtype=MXU_DTYPE, name="pre_mix_norm")
    cqkv = _mm(u, w_cqkv, "nn", name="in_proj_qkv")
    z = _mm(u, w_z, "nn", name="in_proj_z")
    xbc = _mm(u, w_xbc, "nn", name="in_proj_xbc")
    sm = _mm(u, w_small, "nn", name="in_proj_small")

    w_uq, w_ukv, w_out = weights("heads", cqkv)
    w_out = w_out.reshape(D_MODEL, D_MODEL)
    qkvn = _rms_fwd(cqkv, qkv_norm_w, groups=2, out_dtype=MXU_DTYPE, name="qkv_norm")
    q = _mm(qkvn, w_uq, "nn", b_blk=True, out_blk=True, a_cols=(0, Q_RANK), name="q_up")
    kv = _mm(qkvn, w_ukv, "nn", b_blk=True, out_blk=True, a_cols=(Q_RANK, KV_RANK), name="kv_up")
    q_h = _q_prep(q, cos2, sin2, scale, name="q_prep")
    k_h, v_h = _kv_prep(kv, sm, cos2, sin2)
    o_h, lse = _flash_fwd(q_h, k_h, v_h)
    cat = _hnorm_fwd(o_h, attn_norm_w, D_MODEL)

    xbc_act = _conv_fwd(xbc, conv_w, conv_b)
    dtt = jnp.transpose(sm[:, HEAD_LANE:HEAD_LANE + SSD_H])
    ssd_args = (xbc_act, sm, dtt, _head_row(small["dt_bias"]), small["dt_bias"].reshape(SSD_H, 1),
                _head_row(small["a_log"]), small["a_log"].reshape(SSD_H, 1),
                jnp.broadcast_to(small["d_skip"].reshape(SSD_H, 1), (SSD_H, SSD_P)).reshape(SSD_PAIRS, 1, LANE))
    y_ssd, prev = _ssd_fwd(*ssd_args)
    cat = _gated_norm_fwd(y_ssd, z, small["ssd_norm_w"], cat)

    mix = _mm(cat, w_out, "nn", name="out_proj")
    h1, vv = _norm_res_norm(mix, x, small["post_mix_norm_w"], small["pre_ffn_norm_w"])

    w_gate, w_up, w_down = weights("ffn", mix)
    gate, up, act = _ffn_fwd(vv, w_gate, w_up)
    ffn = _mm(act, w_down, "nn", a_blk=True, b_blk=True, fuse=2, name="ffn_down")
    loss_blk, dy, dffn, g_post_ffn = _loss_head(ffn, h1, target, small["post_ffn_norm_w"])

    g_down = _mm(act, dffn, "tn", a_blk=True, out_blk=True, out_dtype=MXU_DTYPE, name="g_down")
    dgate, dup = _ffn_bwd_act(dffn, w_down, gate, up)
    dvv = _ffn_bwd_in(dgate, w_gate, dup, w_up)
    g_gate = _mm(vv, dgate, "tn", b_blk=True, out_blk=True, out_dtype=MXU_DTYPE, name="g_gate")
    g_up = _mm(vv, dup, "tn", b_blk=True, out_blk=True, out_dtype=MXU_DTYPE, name="g_up")
    pre_ffn_w = small["pre_ffn_norm_w"] + on_grads("ffn", [g_gate, g_up, g_down])
    dh1, dmix, g_pre_ffn, g_post_mix = _norm_res_norm_bwd(h1, pre_ffn_w, dvv, dy, mix, small["post_mix_norm_w"])

    dcat = _mm(dmix, w_out, "nt", name="d_cat")
    g_out = _mm(cat, dmix, "tn", out_dtype=MXU_DTYPE, name="g_out")

    do_h, delta, g_attn_norm = _hnorm_bwd(o_h, attn_norm_w, dcat)
    dq_h, dk_h, dv_h = _flash_bwd(q_h, k_h, v_h, do_h, lse, delta)
    dq = _q_prep(dq_h, cos2, -sin2, scale, name="dq_post")

    dy_ssd, dz, g_ssd_norm = _gated_norm_bwd(y_ssd, z, small["ssd_norm_w"], dcat)
    dxbc_act, ddt, dpar = _ssd_bwd(*ssd_args, prev, dy_ssd)
    dkv, dsm = _dkv_post(dk_h, dv_h, ddt, cos2, -sin2)
    dpre, dwb = _conv_bwd_pre(xbc, conv_w, conv_b, dxbc_act)
    dxbc = _conv_bwd_in(dpre, conv_w)

    dqn = _mm(dq, w_uq, "nt", a_blk=True, b_blk=True, fuse=HEADS, name="d_qn")
    dkvn = _mm(dkv, w_ukv, "nt", a_blk=True, b_blk=True, fuse=HEADS, name="d_kvn")
    g_uq = _mm(qkvn, dq, "tn", b_blk=True, out_blk=True, a_cols=(0, Q_RANK), out_dtype=MXU_DTYPE, name="g_uq")
    g_ukv = _mm(qkvn, dkv, "tn", b_blk=True, out_blk=True, a_cols=(Q_RANK, KV_RANK), out_dtype=MXU_DTYPE, name="g_ukv")
    heads_token = on_grads("heads", [g_uq, g_ukv, g_out.reshape(N_DEV, D_MODEL // N_DEV, D_MODEL)])
    dcqkv, g_qkv_norm = _rms_bwd(cqkv, qkv_norm_w + heads_token, [dqn, dkvn], out_dtype=MXU_DTYPE, name="qkv_norm_bwd")

    g_in = _win_from_segments(_mm(u, dcqkv, "tn", out_dtype=MXU_DTYPE, name="g_in_qkv"),
                              _mm(u, dz, "tn", out_dtype=MXU_DTYPE, name="g_in_z"),
                              _mm(u, dxbc, "tn", out_dtype=MXU_DTYPE, name="g_in_xbc"),
                              _mm(u, dsm, "tn", out_dtype=MXU_DTYPE, name="g_in_small"))
    in_token = on_grads("in", [g_in])
    du = _mm(dsm + in_token.astype(dsm.dtype), w_small, "nt", name="d_u_small")
    du = _mm(dcqkv, w_cqkv, "nt", add=du, name="d_u_qkv")
    du = _mm(dz, w_z, "nt", add=du, name="d_u_z")
    du = _mm(dxbc, w_xbc, "nt", add=du, name="d_u_xbc")
    dx, g_pre_mix = _rms_bwd(x, small["pre_mix_norm_w"], [du], res=dh1, name="pre_mix_norm_bwd")

    hl = slice(HEAD_LANE, HEAD_LANE + SSD_H)
    g_small = {"q_norm_w": g_qkv_norm[0, :Q_RANK], "kv_norm_w": g_qkv_norm[0, Q_RANK:], "conv_b": dwb[CONV_K],
               "dt_bias": dpar[0, hl], "a_log": dpar[1, hl], "d_skip": dpar[2, hl], "ssd_norm_w": g_ssd_norm,
               "attn_out_norm_w": g_attn_norm, "pre_mix_norm_w": g_pre_mix, "post_mix_norm_w": g_post_mix,
               "pre_ffn_norm_w": g_pre_ffn, "post_ffn_norm_w": g_post_ffn, "conv_w": dwb[:CONV_K]}
    return loss_blk[0, 0], dx, g_small


_WEIGHT_ORDER = ("w_in", "q_norm_w", "w_uq", "kv_norm_w", "w_ukv", "conv_w", "conv_b", "dt_bias", "a_log", "d_skip",
                 "ssd_norm_w", "attn_out_norm_w", "w_out", "pre_mix_norm_w", "post_mix_norm_w", "pre_ffn_norm_w",
                 "post_ffn_norm_w", "w_gate", "w_up", "w_down")


def kernel(x, positions, w_in, q_norm_w, w_uq, kv_norm_w, w_ukv, conv_w, conv_b, dt_bias, a_log, d_skip, ssd_norm_w, attn_out_norm_w, w_out, pre_mix_norm_w, post_mix_norm_w, pre_ffn_norm_w, post_ffn_norm_w, w_gate, w_up, w_down, loss_target, m_w_in, m_q_norm_w, m_w_uq, m_kv_norm_w, m_w_ukv, m_conv_w, m_conv_b, m_dt_bias, m_a_log, m_d_skip, m_ssd_norm_w, m_attn_out_norm_w, m_w_out, m_pre_mix_norm_w, m_post_mix_norm_w, m_pre_ffn_norm_w, m_post_ffn_norm_w, m_w_gate, m_w_up, m_w_down, v_w_in, v_q_norm_w, v_w_uq, v_kv_norm_w, v_w_ukv, v_conv_w, v_conv_b, v_dt_bias, v_a_log, v_d_skip, v_ssd_norm_w, v_attn_out_norm_w, v_w_out, v_pre_mix_norm_w, v_post_mix_norm_w, v_pre_ffn_norm_w, v_post_ffn_norm_w, v_w_gate, v_w_up, v_w_down):
    w = dict(w_in=w_in, q_norm_w=q_norm_w, w_uq=w_uq, kv_norm_w=kv_norm_w, w_ukv=w_ukv, conv_w=conv_w, conv_b=conv_b,
             dt_bias=dt_bias, a_log=a_log, d_skip=d_skip, ssd_norm_w=ssd_norm_w, attn_out_norm_w=attn_out_norm_w,
             w_out=w_out, pre_mix_norm_w=pre_mix_norm_w, post_mix_norm_w=post_mix_norm_w,
             pre_ffn_norm_w=pre_ffn_norm_w, post_ffn_norm_w=post_ffn_norm_w, w_gate=w_gate, w_up=w_up, w_down=w_down)
    m = dict(w_in=m_w_in, q_norm_w=m_q_norm_w, w_uq=m_w_uq, kv_norm_w=m_kv_norm_w, w_ukv=m_w_ukv, conv_w=m_conv_w,
             conv_b=m_conv_b, dt_bias=m_dt_bias, a_log=m_a_log, d_skip=m_d_skip, ssd_norm_w=m_ssd_norm_w,
             attn_out_norm_w=m_attn_out_norm_w, w_out=m_w_out, pre_mix_norm_w=m_pre_mix_norm_w,
             post_mix_norm_w=m_post_mix_norm_w, pre_ffn_norm_w=m_pre_ffn_norm_w, post_ffn_norm_w=m_post_ffn_norm_w,
             w_gate=m_w_gate, w_up=m_w_up, w_down=m_w_down)
    v = dict(w_in=v_w_in, q_norm_w=v_q_norm_w, w_uq=v_w_uq, kv_norm_w=v_kv_norm_w, w_ukv=v_w_ukv, conv_w=v_conv_w,
             conv_b=v_conv_b, dt_bias=v_dt_bias, a_log=v_a_log, d_skip=v_d_skip, ssd_norm_w=v_ssd_norm_w,
             attn_out_norm_w=v_attn_out_norm_w, w_out=v_w_out, pre_mix_norm_w=v_pre_mix_norm_w,
             post_mix_norm_w=v_post_mix_norm_w, pre_ffn_norm_w=v_pre_ffn_norm_w, post_ffn_norm_w=v_post_ffn_norm_w,
             w_gate=v_w_gate, w_up=v_w_up, w_down=v_w_down)
    w, m, v = ({k: t[0] for k, t in d.items()} for d in (w, m, v))
    me = 4 * lax.axis_index("x") + 2 * lax.axis_index("y") + lax.axis_index("c")
    groups = {"in": ("w_in",), "heads": ("w_uq", "w_ukv", "w_out"), "ffn": ("w_gate", "w_up", "w_down")}
    cshard = CONV_DIM // N_DEV

    shards = [w["w_in"].astype(MXU_DTYPE),
              jnp.stack(_split3(w["conv_w"])).reshape(3 * CONV_K, cshard).astype(MXU_DTYPE)]
    w_in_g, cw = _all_gather(shards, name="gather_weights")
    cw = cw.astype(F32).reshape(N_DEV, 3, CONV_K, cshard)
    wg = {"w_in": w_in_g, "conv_w": jnp.transpose(cw[:, 0] + cw[:, 1] + cw[:, 2], (1, 0, 2)).reshape(CONV_K, CONV_DIM)}
    arriving, dep, started = {}, wg["conv_w"], jnp.zeros((), F32)
    small = {name: w[name] for name, _ in _SMALL if name != "conv_w"}
    for group in ("heads", "ffn"):
        token, arriving[group] = _exchange_behind([w[name].astype(MXU_DTYPE) for name in groups[group]], False,
                                                  dep, group + "_weights")
        started = started + token
        dep = jnp.zeros((8, LANE), F32) + started
    small["pre_mix_norm_w"] = small["pre_mix_norm_w"] + started

    leaving = {}

    def on_grads(group, gs):
        token, leaving[group] = _exchange_behind(gs, True, jnp.zeros((8, LANE), F32), group + "_grads")
        return token

    loss_local, dx, g_small = _local_step(x[0], positions[0], loss_target[0], wg, small,
                                          lambda group, after: arriving[group](after), on_grads)
    loss = lax.psum(loss_local, ("x", "y", "c"))

    recv = {}
    for group in ("ffn", "heads", "in"):
        recv.update(zip(groups[group], leaving[group](dx)))
    grads, deltas, new_m, new_v = {}, {}, {}, {}
    for name, parts in recv.items():
        grads[name], deltas[name], new_m[name], new_v[name] = _adamw(parts, w[name], m[name], v[name],
                                                                     name="adamw_" + name)

    def embed(t):
        return lax.dynamic_update_slice(jnp.zeros((CONV_K, CONV_DIM), F32), t, (0, me * cshard))

    parts_s = _all_gather([_pack_small(g_small)], name="gather_small_grads")[0]
    packs = [_pack_small({**{n_: d[n_] for n_, _ in _SMALL if n_ != "conv_w"}, "conv_w": embed(d["conv_w"])})
             for d in (w, m, v)]
    outs = [_unpack_small(t) for t in _adamw_small(parts_s, *packs)]
    for name, n in _SMALL:
        for dst, src in zip((grads, deltas, new_m, new_v), outs):
            if name == "conv_w":
                dst[name] = lax.dynamic_slice(src[name].reshape(CONV_K, CONV_DIM), (0, me * cshard), (CONV_K, cshard))
            else:
                dst[name] = src[name]

    def lead(d):
        return [d[name][None] for name in _WEIGHT_ORDER]

    return (loss, dx[None], *lead(grads), *lead(deltas), *lead(new_m), *lead(new_v))
```

```python
import numpy as np

import jax
import jax.numpy as jnp
from jax import lax
from jax.experimental import pallas as pl
from jax.experimental.pallas import tpu as pltpu

F32 = jnp.float32
BF16 = jnp.bfloat16
MXU_DTYPE = jnp.bfloat16
EPS = 1e-6
VMEM_LIMIT_BYTES = 48 * 1024 * 1024
K_TILE_MAX = 2048

N_DEV = 8
D_MODEL = 2048
Q_RANK = 512
KV_RANK = 512
ROPE = 64
HALF = ROPE // 2
HEADS = 8
NOPE = 128
VDIM = 128
QK = NOPE + ROPE
SSD_W = 1024
SSD_H = 16
SSD_P = 64
SSD_G = 2
SSD_E = SSD_H // SSD_G
SSD_N = 128
CHUNK = 128
CONV_K = 4
CONV_DIM = SSD_W + 2 * SSD_G * SSD_N
B_OFF = SSD_W
C_OFF = SSD_W + SSD_G * SSD_N
D_FF = 5632
D_IN = Q_RANK + KV_RANK + ROPE + SSD_W + CONV_DIM + SSD_H
ROPE_THETA = 10000.0
LANE = 128
HEAD_LANE = ROPE

ADAM_LR = 0.001
ADAM_B1 = 0.9
ADAM_B2 = 0.999
ADAM_EPS = 1e-08
ADAM_WD = 0.01
ADAM_STEP = 10


def _pick(n, cands):
    for c in cands:
        if n % c == 0:
            return c
    return n


def _params(*sem):
    return pltpu.CompilerParams(dimension_semantics=sem, vmem_limit_bytes=VMEM_LIMIT_BYTES)


def _sigmoid(x):
    return 1.0 / (1.0 + jnp.exp(-x))


def _silu(x):
    return x * _sigmoid(x)


def _dsilu(x):
    s = _sigmoid(x)
    return s * (1.0 + x * (1.0 - s))


def _softplus(x):
    e = jnp.exp(-jnp.abs(x))
    small = e * (1.0 - e * (0.5 - e * (1.0 / 3.0)))
    return jnp.maximum(x, 0.0) + jnp.where(e < 0.01, small, jnp.log(1.0 + e))


def _dot(a, b, ca, cb):
    return lax.dot_general(a, b, (((ca,), (cb,)), ((), ())), preferred_element_type=F32)


def _mx(v):
    return v.astype(MXU_DTYPE)


def _split3(a):
    hi = a.astype(BF16)
    r1 = a - hi.astype(F32)
    mid = r1.astype(BF16)
    lo = (r1 - mid.astype(F32)).astype(BF16)
    return hi, mid, lo


def _exact_dot(a, b, ca, cb, split_a):
    if split_a:
        return sum(_dot(p, b, ca, cb) for p in _split3(a))
    return sum(_dot(a, p, ca, cb) for p in _split3(b))


MM_ROW_GROUPS = 4


def _row_slices(tm, align):
    ng = MM_ROW_GROUPS
    while ng > 1 and (tm % ng or (tm // ng) % align):
        ng //= 2
    return [slice(g * (tm // ng), (g + 1) * (tm // ng)) for g in range(ng)]


def _mm(a, b, mode, *, a_blk=False, b_blk=False, out_blk=False, a_cols=None, b_cols=None, add=None, out_dtype=F32,
        fuse=1, name="mm"):
    a2, b2 = a.shape[-2:], b.shape[-2:]
    a_last = a2[1] if a_cols is None else a_cols[1]
    a_start = 0 if a_cols is None else a_cols[0]
    b_start = 0
    if b_cols is not None:
        assert mode != "nt"
        b_start, b2 = b_cols[0], (b2[0], b_cols[1])
    if mode == "nn":
        m, k, (k2, n) = a2[0], a_last, b2
    elif mode == "nt":
        m, k, (n, k2) = a2[0], a_last, b2
    else:
        k, m, (k2, n) = a2[0], a_last, b2
    assert k == k2, (a.shape, b.shape, mode)
    tm = _pick(m, (1024, 704, 512, 256, 128))
    tn = _pick(n, (1024, 768, 704, 512, 256, 192, 128))
    tk = k if k <= K_TILE_MAX else _pick(k, (K_TILE_MAX, 1024, 512))
    nk = k // tk
    jo = N_DEV if out_blk else 1
    reduce_blocks = a_blk and b_blk and not out_blk
    assert fuse == 1 or reduce_blocks
    jr = N_DEV // fuse if reduce_blocks else 1
    ca, cb = {"nn": (1, 0), "nt": (1, 1), "tn": (0, 0)}[mode]
    has_add = add is not None
    single = jr * nk == 1
    if mode == "tn":
        assert a_start % tm == 0
        a_block, a_idx = (tk, tm), (lambda i, kk: (kk, i + a_start // tm))
    else:
        assert a_start % tk == 0
        a_block, a_idx = (tm, tk), (lambda i, kk: (i, kk + a_start // tk))
    assert b_start % tn == 0
    b_block, b_idx = (((tn, tk), (lambda nn_, kk: (nn_, kk))) if mode == "nt"
                      else ((tk, tn), (lambda nn_, kk: (kk, nn_ + b_start // tn))))

    def blk_specs(blocked, block, idx, of_a, t):
        def pos(o, i, nn_, kk):
            return idx(i, kk) if of_a else idx(nn_, kk)
        if blocked:
            return pl.BlockSpec((None,) + block,
                                lambda o, i, nn_, r, kk: ((o if out_blk else r * fuse + t),) + pos(o, i, nn_, kk))
        return pl.BlockSpec(block, lambda o, i, nn_, r, kk: pos(o, i, nn_, kk))

    a_specs = [blk_specs(a_blk, a_block, a_idx, True, t) for t in range(fuse)]
    b_specs = [blk_specs(b_blk, b_block, b_idx, False, t) for t in range(fuse)]
    o_spec = (pl.BlockSpec((None, tm, tn), lambda o, i, nn_, r, kk: (o, i, nn_)) if out_blk
              else pl.BlockSpec((tm, tn), lambda o, i, nn_, r, kk: (i, nn_)))

    groups = _row_slices(tm, LANE if mode == "tn" else 16)

    def body(*refs):
        a_refs, b_refs = refs[:fuse], refs[fuse:2 * fuse]
        add_ref = refs[2 * fuse] if has_add else None
        o_ref = refs[2 * fuse + 1] if has_add else refs[2 * fuse]

        def partial(rs):
            out = None
            for t in range(fuse):
                av = a_refs[t][:, rs] if mode == "tn" else a_refs[t][rs, :]
                d = _dot(_mx(av), _mx(b_refs[t][...]), ca, cb)
                out = d if out is None else out + d
            return out

        if single:
            for rs in groups:
                res = partial(rs)
                if has_add:
                    res = res + add_ref[rs, :]
                o_ref[rs, :] = res.astype(o_ref.dtype)
            return
        acc = refs[-1]
        r, kk = pl.program_id(3), pl.program_id(4)

        @pl.when(jnp.logical_and(r == 0, kk == 0))
        def _():
            acc[...] = jnp.zeros_like(acc)

        for rs in groups:
            acc[rs, :] += partial(rs)

        @pl.when(jnp.logical_and(r == jr - 1, kk == nk - 1))
        def _():
            res = acc[...]
            if has_add:
                res = res + add_ref[...]
            o_ref[...] = res.astype(o_ref.dtype)

    out_shape = ((N_DEV, m, n) if out_blk else (m, n))
    return pl.pallas_call(
        body, name=name, grid=(jo, m // tm, n // tn, jr, nk),
        in_specs=a_specs + b_specs + ([o_spec] if has_add else []), out_specs=o_spec,
        out_shape=jax.ShapeDtypeStruct(out_shape, out_dtype),
        scratch_shapes=[] if single else [pltpu.VMEM((tm, tn), F32)],
        compiler_params=_params("parallel", "parallel", "parallel", "arbitrary", "arbitrary"),
    )(*((a,) * fuse + (b,) * fuse + ((add,) if has_add else ())))


def _row_tile(r_):
    return _pick(r_, (256, 128, 64, 32, 16, 8))


def _rms_fwd(t, w, groups=1, res=None, out_dtype=F32, name="rms_fwd"):
    r_, f = t.shape
    fg = f // groups
    tr = _row_tile(r_)
    has_res = res is not None

    def body(*refs):
        t_ref, w_ref = refs[0], refs[1]
        res_ref = refs[2] if has_res else None
        o_ref = refs[-1]
        for g in range(groups):
            sl = slice(g * fg, (g + 1) * fg)
            tv = t_ref[:, sl].astype(F32)
            r = lax.rsqrt(jnp.mean(tv * tv, axis=-1, keepdims=True) + EPS)
            y = tv * r * w_ref[:, sl]
            if has_res:
                y = y + res_ref[:, sl]
            o_ref[:, sl] = y.astype(o_ref.dtype)

    row = pl.BlockSpec((tr, f), lambda i: (i, 0))
    wsp = pl.BlockSpec((1, f), lambda i: (0, 0))
    return pl.pallas_call(
        body, name=name, grid=(r_ // tr,),
        in_specs=[row, wsp] + ([row] if has_res else []), out_specs=row,
        out_shape=jax.ShapeDtypeStruct((r_, f), out_dtype),
        compiler_params=_params("parallel"),
    )(*((t, w.reshape(1, f)) + ((res,) if has_res else ())))


def _rms_bwd(t, w, dys, res=None, out_dtype=F32, name="rms_bwd"):
    r_, f = t.shape
    groups = len(dys)
    fg = f // groups
    tr = _row_tile(r_)
    has_res = res is not None

    def body(*refs):
        t_ref, w_ref = refs[0], refs[1]
        dy_refs = refs[2:2 + groups]
        res_ref = refs[2 + groups] if has_res else None
        dt_ref, dw_ref = refs[-2], refs[-1]

        @pl.when(pl.program_id(0) == 0)
        def _():
            dw_ref[...] = jnp.zeros_like(dw_ref)

        for g in range(groups):
            sl = slice(g * fg, (g + 1) * fg)
            tv = t_ref[:, sl].astype(F32)
            dyv = dy_refs[g][...].astype(F32)
            r = lax.rsqrt(jnp.mean(tv * tv, axis=-1, keepdims=True) + EPS)
            gw = dyv * w_ref[:, sl]
            c = jnp.mean(gw * tv, axis=-1, keepdims=True)
            dt = r * gw - tv * (r * r * r * c)
            if has_res:
                dt = dt + res_ref[:, sl]
            dt_ref[:, sl] = dt.astype(dt_ref.dtype)
            dw_ref[:, sl] += jnp.sum(dyv * tv * r, axis=0, keepdims=True)

    row = pl.BlockSpec((tr, f), lambda i: (i, 0))
    grow = pl.BlockSpec((tr, fg), lambda i: (i, 0))
    wsp = pl.BlockSpec((1, f), lambda i: (0, 0))
    return pl.pallas_call(
        body, name=name, grid=(r_ // tr,),
        in_specs=[row, wsp] + [grow] * groups + ([row] if has_res else []), out_specs=[row, wsp],
        out_shape=[jax.ShapeDtypeStruct((r_, f), out_dtype), jax.ShapeDtypeStruct((1, f), F32)],
        compiler_params=_params("arbitrary"),
    )(*((t, w.reshape(1, f)) + tuple(dys) + ((res,) if has_res else ())))


def _norm_res_norm(t, res, w1, w2, name="post_mix_pre_ffn_norm"):
    r_, f = t.shape
    tr = _row_tile(r_)

    def body(t_ref, res_ref, w1_ref, w2_ref, h_ref, v_ref):
        tv = t_ref[...]
        h = res_ref[...] + tv * lax.rsqrt(jnp.mean(tv * tv, axis=-1, keepdims=True) + EPS) * w1_ref[...]
        h_ref[...] = h
        v_ref[...] = (h * lax.rsqrt(jnp.mean(h * h, axis=-1, keepdims=True) + EPS) * w2_ref[...]).astype(v_ref.dtype)

    row = pl.BlockSpec((tr, f), lambda i: (i, 0))
    wsp = pl.BlockSpec((1, f), lambda i: (0, 0))
    return pl.pallas_call(
        body, name=name, grid=(r_ // tr,), in_specs=[row, row, wsp, wsp], out_specs=[row, row],
        out_shape=[jax.ShapeDtypeStruct((r_, f), F32), jax.ShapeDtypeStruct((r_, f), MXU_DTYPE)],
        compiler_params=_params("parallel"),
    )(t, res, w1.reshape(1, f), w2.reshape(1, f))


def _norm_res_norm_bwd(h, w2, dv, dres, t, w1, name="pre_ffn_post_mix_norm_bwd"):
    r_, f = h.shape
    tr = _row_tile(r_)

    def body(h_ref, w2_ref, dv_ref, dres_ref, t_ref, w1_ref, dh_ref, dt_ref, dw2_ref, dw1_ref):
        @pl.when(pl.program_id(0) == 0)
        def _():
            dw2_ref[...] = jnp.zeros_like(dw2_ref)
            dw1_ref[...] = jnp.zeros_like(dw1_ref)

        def rms_bwd(tv, wv, dyv):
            r = lax.rsqrt(jnp.mean(tv * tv, axis=-1, keepdims=True) + EPS)
            gw = dyv * wv
            c = jnp.mean(gw * tv, axis=-1, keepdims=True)
            return r * gw - tv * (r * r * r * c), jnp.sum(dyv * tv * r, axis=0, keepdims=True)

        d1, g2 = rms_bwd(h_ref[...], w2_ref[...], dv_ref[...])
        dh = d1 + dres_ref[...]
        dh_ref[...] = dh
        dw2_ref[...] += g2
        d2, g1 = rms_bwd(t_ref[...], w1_ref[...], dh)
        dt_ref[...] = d2.astype(dt_ref.dtype)
        dw1_ref[...] += g1

    row = pl.BlockSpec((tr, f), lambda i: (i, 0))
    wsp = pl.BlockSpec((1, f), lambda i: (0, 0))
    return pl.pallas_call(
        body, name=name, grid=(r_ // tr,), in_specs=[row, wsp, row, row, row, wsp], out_specs=[row, row, wsp, wsp],
        out_shape=[jax.ShapeDtypeStruct((r_, f), F32), jax.ShapeDtypeStruct((r_, f), MXU_DTYPE),
                   jax.ShapeDtypeStruct((1, f), F32), jax.ShapeDtypeStruct((1, f), F32)],
        compiler_params=_params("arbitrary"),
    )(h, w2.reshape(1, f), dv, dres, t, w1.reshape(1, f))


def _hnorm_fwd(o, w, width, name="attn_out_norm"):
    h, s_, v = o.shape
    tr = _row_tile(s_)

    def body(o_ref, w_ref, y_ref):
        ss = jnp.sum(o_ref[0] * o_ref[0], axis=-1, keepdims=True)
        for i in range(1, h):
            ss = ss + jnp.sum(o_ref[i] * o_ref[i], axis=-1, keepdims=True)
        r = lax.rsqrt(ss * (1.0 / (h * v)) + EPS)
        for i in range(h):
            sl = slice(i * v, (i + 1) * v)
            y_ref[:, sl] = (o_ref[i] * r * w_ref[:, sl]).astype(y_ref.dtype)

    return pl.pallas_call(
        body, name=name, grid=(s_ // tr,),
        in_specs=[pl.BlockSpec((h, tr, v), lambda i: (0, i, 0)), pl.BlockSpec((1, h * v), lambda i: (0, 0))],
        out_specs=pl.BlockSpec((tr, h * v), lambda i: (i, 0)),
        out_shape=jax.ShapeDtypeStruct((s_, width), MXU_DTYPE), compiler_params=_params("parallel"),
    )(o, w)


def _hnorm_bwd(o, w, dy, name="attn_out_norm_bwd"):
    h, s_, v = o.shape
    tr = _row_tile(s_)

    def body(o_ref, w_ref, dy_ref, do_ref, delta_ref, dw_ref):
        @pl.when(pl.program_id(0) == 0)
        def _():
            dw_ref[...] = jnp.zeros_like(dw_ref)

        ss = jnp.zeros((tr, 1), F32)
        cc = jnp.zeros((tr, 1), F32)
        for i in range(h):
            sl = slice(i * v, (i + 1) * v)
            ov = o_ref[i]
            ss = ss + jnp.sum(ov * ov, axis=-1, keepdims=True)
            cc = cc + jnp.sum(dy_ref[:, sl] * w_ref[:, sl] * ov, axis=-1, keepdims=True)
        r = lax.rsqrt(ss * (1.0 / (h * v)) + EPS)
        c = cc * (1.0 / (h * v))
        for i in range(h):
            sl = slice(i * v, (i + 1) * v)
            ov = o_ref[i]
            dyv = dy_ref[:, sl]
            dov = r * dyv * w_ref[:, sl] - ov * (r * r * r * c)
            do_ref[i] = dov.astype(do_ref.dtype)
            delta_ref[i] = jnp.sum(dov * ov, axis=-1, keepdims=True)
            dw_ref[:, sl] += jnp.sum(dyv * ov * r, axis=0, keepdims=True)

    blk = pl.BlockSpec((h, tr, v), lambda i: (0, i, 0))
    wsp = pl.BlockSpec((1, h * v), lambda i: (0, 0))
    return pl.pallas_call(
        body, name=name, grid=(s_ // tr,),
        in_specs=[blk, wsp, pl.BlockSpec((tr, h * v), lambda i: (i, 0))],
        out_specs=[blk, pl.BlockSpec((h, tr, 1), lambda i: (0, i, 0)), wsp],
        out_shape=[jax.ShapeDtypeStruct(o.shape, MXU_DTYPE), jax.ShapeDtypeStruct((h, s_, 1), F32),
                   jax.ShapeDtypeStruct((1, h * v), F32)],
        compiler_params=_params("arbitrary"),
    )(o, w, dy)


def _loss_head(ffn, h1, target, w, name="loss_head"):
    r_, f = ffn.shape
    tr = _row_tile(r_)

    def body(ffn_ref, h1_ref, tg_ref, w_ref, loss_ref, dy_ref, dffn_ref, dw_ref):
        @pl.when(pl.program_id(0) == 0)
        def _():
            dw_ref[...] = jnp.zeros_like(dw_ref)
            loss_ref[...] = jnp.zeros_like(loss_ref)

        tv = ffn_ref[...]
        wv = w_ref[...]
        r = lax.rsqrt(jnp.mean(tv * tv, axis=-1, keepdims=True) + EPS)
        tn = tv * r
        e = h1_ref[...] + tn * wv - tg_ref[...]
        tot = jnp.sum(jnp.sum(e * e, axis=1, keepdims=True), axis=0, keepdims=True) * (0.5 / f)
        loss_ref[...] += tot + jnp.zeros_like(loss_ref)
        dyv = e * (1.0 / f)
        dy_ref[...] = dyv
        gw = dyv * wv
        c = jnp.mean(gw * tv, axis=-1, keepdims=True)
        dffn_ref[...] = (r * gw - tv * (r * r * r * c)).astype(dffn_ref.dtype)
        dw_ref[...] += jnp.sum(dyv * tn, axis=0, keepdims=True)

    row = pl.BlockSpec((tr, f), lambda i: (i, 0))
    wsp = pl.BlockSpec((1, f), lambda i: (0, 0))
    lsp = pl.BlockSpec((1, LANE), lambda i: (0, 0))
    return pl.pallas_call(
        body, name=name, grid=(r_ // tr,),
        in_specs=[row, row, row, wsp], out_specs=[lsp, row, row, wsp],
        out_shape=[jax.ShapeDtypeStruct((1, LANE), F32), jax.ShapeDtypeStruct((r_, f), F32),
                   jax.ShapeDtypeStruct((r_, f), MXU_DTYPE), jax.ShapeDtypeStruct((1, f), F32)],
        compiler_params=_params("arbitrary"),
    )(ffn, h1, target, w.reshape(1, f))


def _rot_matrix():
    p = np.zeros((ROPE, ROPE), np.float32)
    for i in range(HALF):
        p[i + HALF, i] = -1.0
        p[i, i + HALF] = 1.0
    return jnp.asarray(p, BF16)


def _rope_val(r, c2, s2, rot):
    return r * c2 + _exact_dot(r, rot, 1, 0, True) * s2


def _q_prep(q, cos2, sin2, scale, name):
    h, s_, _ = q.shape
    tr = _pick(s_, (1024, 512, 256, 128, 64, 32, 16, 8))

    def body(q_ref, c_ref, s_ref, rot_ref, o_ref):
        x = q_ref[...]
        o_ref[:, :NOPE] = (x[:, :NOPE] * scale).astype(o_ref.dtype)
        o_ref[:, NOPE:] = (_rope_val(x[:, NOPE:], c_ref[...], s_ref[...], rot_ref[...]) * scale).astype(o_ref.dtype)

    blk = pl.BlockSpec((None, tr, QK), lambda hh, i: (hh, i, 0))
    csp = pl.BlockSpec((tr, ROPE), lambda hh, i: (i, 0))
    return pl.pallas_call(
        body, name=name, grid=(h, s_ // tr),
        in_specs=[blk, csp, csp, pl.BlockSpec((ROPE, ROPE), lambda hh, i: (0, 0))], out_specs=blk,
        out_shape=jax.ShapeDtypeStruct(q.shape, MXU_DTYPE), compiler_params=_params("parallel", "parallel"),
    )(q, cos2, sin2, _rot_matrix())


def _kv_prep(kv, small, cos2, sin2, name="kv_prep"):
    h, s_, _ = kv.shape
    tr = _row_tile(s_)

    def body(kv_ref, sm_ref, c_ref, s_ref, rot_ref, k_ref, v_ref):
        kr = _rope_val(sm_ref[:, :ROPE], c_ref[...], s_ref[...], rot_ref[...]).astype(k_ref.dtype)
        for i in range(h):
            k_ref[i, :, :NOPE] = kv_ref[i, :, :NOPE].astype(k_ref.dtype)
            k_ref[i, :, NOPE:] = kr
            v_ref[i] = kv_ref[i, :, NOPE:].astype(v_ref.dtype)

    csp = pl.BlockSpec((tr, ROPE), lambda i: (i, 0))
    return pl.pallas_call(
        body, name=name, grid=(s_ // tr,),
        in_specs=[pl.BlockSpec((h, tr, NOPE + VDIM), lambda i: (0, i, 0)), pl.BlockSpec((tr, LANE), lambda i: (i, 0)),
                  csp, csp, pl.BlockSpec((ROPE, ROPE), lambda i: (0, 0))],
        out_specs=[pl.BlockSpec((h, tr, QK), lambda i: (0, i, 0)), pl.BlockSpec((h, tr, VDIM), lambda i: (0, i, 0))],
        out_shape=[jax.ShapeDtypeStruct((h, s_, QK), MXU_DTYPE), jax.ShapeDtypeStruct((h, s_, VDIM), MXU_DTYPE)],
        compiler_params=_params("parallel"),
    )(kv, small, cos2, sin2, _rot_matrix())


def _dkv_post(dk, dv, ddt, cos2, nsin2, name="dkv_post"):
    h, s_, _ = dk.shape
    tr = _row_tile(s_)

    def body(dk_ref, dv_ref, ddt_ref, c_ref, s_ref, rot_ref, dkv_ref, dsm_ref):
        acc = dk_ref[0, :, NOPE:]
        for i in range(1, h):
            acc = acc + dk_ref[i, :, NOPE:]
        dsm_ref[:, :ROPE] = _rope_val(acc, c_ref[...], s_ref[...], rot_ref[...]).astype(dsm_ref.dtype)
        dsm_ref[:, ROPE:] = ddt_ref[:, ROPE:].astype(dsm_ref.dtype)
        for i in range(h):
            dkv_ref[i, :, :NOPE] = dk_ref[i, :, :NOPE].astype(dkv_ref.dtype)
            dkv_ref[i, :, NOPE:] = dv_ref[i].astype(dkv_ref.dtype)

    csp = pl.BlockSpec((tr, ROPE), lambda i: (i, 0))
    return pl.pallas_call(
        body, name=name, grid=(s_ // tr,),
        in_specs=[pl.BlockSpec((h, tr, QK), lambda i: (0, i, 0)), pl.BlockSpec((h, tr, VDIM), lambda i: (0, i, 0)),
                  pl.BlockSpec((tr, LANE), lambda i: (i, 0)), csp, csp, pl.BlockSpec((ROPE, ROPE), lambda i: (0, 0))],
        out_specs=[pl.BlockSpec((h, tr, NOPE + VDIM), lambda i: (0, i, 0)), pl.BlockSpec((tr, LANE), lambda i: (i, 0))],
        out_shape=[jax.ShapeDtypeStruct((h, s_, NOPE + VDIM), MXU_DTYPE), jax.ShapeDtypeStruct((s_, LANE), MXU_DTYPE)],
        compiler_params=_params("parallel"),
    )(dk, dv, ddt, cos2, nsin2, _rot_matrix())


def _attn_tile(s):
    return 1024 if s % 2048 == 0 else s // 2


def _pairs(n, by_key):
    if by_key:
        pr = [(i, j) for j in range(n) for i in range(j, n)]
    else:
        pr = [(i, j) for i in range(n) for j in range(i + 1)]
    return (jnp.asarray([p[0] for p in pr], jnp.int32), jnp.asarray([p[1] for p in pr], jnp.int32))


ATTN_ROW_GROUPS = 4


def _row_groups(t, diag):
    tg = t // ATTN_ROW_GROUPS
    out = []
    for r in range(ATTN_ROW_GROUPS):
        nc = (r + 1) * tg if diag else t
        mask = None
        if diag:
            mask = (lax.broadcasted_iota(jnp.int32, (tg, nc), 1)
                    <= lax.broadcasted_iota(jnp.int32, (tg, nc), 0) + r * tg)
        out.append((slice(r * tg, (r + 1) * tg), nc, mask))
    return out


def _flash_specs(t, dk, dv):
    qsp = pl.BlockSpec((None, t, dk), lambda hh, p, qi, kj: (hh, qi[p], 0))
    ksp = pl.BlockSpec((None, t, dk), lambda hh, p, qi, kj: (hh, kj[p], 0))
    vsp = pl.BlockSpec((None, t, dv), lambda hh, p, qi, kj: (hh, kj[p], 0))
    osp = pl.BlockSpec((None, t, dv), lambda hh, p, qi, kj: (hh, qi[p], 0))
    lsp = pl.BlockSpec((None, t, 1), lambda hh, p, qi, kj: (hh, qi[p], 0))
    return qsp, ksp, vsp, osp, lsp


def _flash_fwd(q, k, v, name="flash_fwd"):
    h, s_, dk = q.shape
    dv = v.shape[-1]
    t = _attn_tile(s_)
    n = s_ // t
    qi, kj = _pairs(n, False)

    def body(qi_ref, kj_ref, q_ref, k_ref, v_ref, o_ref, lse_ref, m_s, l_s, acc):
        p_ = pl.program_id(1)
        i, j = qi_ref[p_], kj_ref[p_]

        @pl.when(j == 0)
        def _():
            m_s[...] = jnp.full_like(m_s, -jnp.inf)
            l_s[...] = jnp.zeros_like(l_s)
            acc[...] = jnp.zeros_like(acc)

        def update(diag):
            for rs, nc, mask in _row_groups(t, diag):
                sc = _dot(q_ref[rs, :], k_ref[0:nc, :], 1, 1)
                if mask is not None:
                    sc = jnp.where(mask, sc, -jnp.inf)
                m_old = m_s[rs, :]
                m_new = jnp.maximum(m_old, jnp.max(sc, axis=1, keepdims=True))
                alpha = jnp.exp(m_old - m_new)
                p = jnp.exp(sc - m_new)
                l_s[rs, :] = alpha * l_s[rs, :] + jnp.sum(p, axis=1, keepdims=True)
                acc[rs, :] = alpha * acc[rs, :] + _dot(_mx(p), v_ref[0:nc, :], 1, 0)
                m_s[rs, :] = m_new

        @pl.when(j < i)
        def _():
            update(False)

        @pl.when(j == i)
        def _():
            update(True)
            o_ref[...] = acc[...] / l_s[...]
            lse_ref[...] = m_s[...] + jnp.log(l_s[...])

    qsp, ksp, vsp, osp, lsp = _flash_specs(t, dk, dv)
    gs = pltpu.PrefetchScalarGridSpec(
        num_scalar_prefetch=2, grid=(h, qi.shape[0]), in_specs=[qsp, ksp, vsp], out_specs=[osp, lsp],
        scratch_shapes=[pltpu.VMEM((t, 1), F32), pltpu.VMEM((t, 1), F32), pltpu.VMEM((t, dv), F32)])
    return pl.pallas_call(
        body, name=name, grid_spec=gs,
        out_shape=[jax.ShapeDtypeStruct((h, s_, dv), F32), jax.ShapeDtypeStruct((h, s_, 1), F32)],
        compiler_params=_params("parallel", "arbitrary"),
    )(qi, kj, q, k, v)


def _flash_bwd(q, k, v, do, lse, delta, name="flash_bwd"):
    h, s_, dk = q.shape
    dv = v.shape[-1]
    t = _attn_tile(s_)
    tg = t // ATTN_ROW_GROUPS
    n = s_ // t
    qi, kj = _pairs(n, True)

    def body(qi_ref, kj_ref, q_ref, k_ref, v_ref, do_ref, lse_ref, delta_ref, dq_ref, dk_ref, dv_ref, dk_acc, dv_acc):
        p_ = pl.program_id(1)
        i, j = qi_ref[p_], kj_ref[p_]

        @pl.when(p_ == 0)
        def _():
            dq_ref[...] = jnp.zeros_like(dq_ref)

        def update(diag):
            for g, (rs, nc, mask) in enumerate(_row_groups(t, diag)):
                sc = _dot(q_ref[rs, :], k_ref[0:nc, :], 1, 1)
                if mask is not None:
                    sc = jnp.where(mask, sc, -jnp.inf)
                p = jnp.exp(sc - lse_ref[rs, :])
                dob = _mx(do_ref[rs, :])
                dv_acc[0:nc, :] += _dot(_mx(p), dob, 0, 0)
                dp = _dot(dob, v_ref[0:nc, :], 1, 1)
                dsb = _mx(p * (dp - delta_ref[rs, :]))
                dk_acc[0:nc, :] += _dot(dsb, q_ref[rs, :], 0, 0)
                rows = pl.ds(pl.multiple_of(i * t + g * tg, tg), tg)
                dq_ref[rows, :] += _dot(dsb, k_ref[0:nc, :], 1, 0)

        @pl.when(i == j)
        def _():
            dk_acc[...] = jnp.zeros_like(dk_acc)
            dv_acc[...] = jnp.zeros_like(dv_acc)
            update(True)

        @pl.when(i > j)
        def _():
            update(False)

        @pl.when(i == n - 1)
        def _():
            dk_ref[...] = dk_acc[...]
            dv_ref[...] = dv_acc[...]

    qsp, ksp, vsp, osp, lsp = _flash_specs(t, dk, dv)
    dqsp = pl.BlockSpec((None, s_, dk), lambda hh, p, qi, kj: (hh, 0, 0))
    gs = pltpu.PrefetchScalarGridSpec(
        num_scalar_prefetch=2, grid=(h, qi.shape[0]), in_specs=[qsp, ksp, vsp, osp, lsp, lsp],
        out_specs=[dqsp, ksp, vsp],
        scratch_shapes=[pltpu.VMEM((t, dk), F32), pltpu.VMEM((t, dv), F32)])
    return pl.pallas_call(
        body, name=name, grid_spec=gs,
        out_shape=[jax.ShapeDtypeStruct((h, s_, dk), F32), jax.ShapeDtypeStruct((h, s_, dk), F32),
                   jax.ShapeDtypeStruct((h, s_, dv), F32)],
        compiler_params=_params("parallel", "arbitrary"),
    )(qi, kj, q, k, v, do, lse, delta)


HALO = 8


def _conv_specs(s_, c, tr, after):
    main = pl.BlockSpec((tr, c), lambda i: (i, 0))
    per = tr // HALO
    if after:
        halo = pl.BlockSpec((HALO, c), lambda i: (jnp.minimum((i + 1) * per, s_ // HALO - 1), 0))
    else:
        halo = pl.BlockSpec((HALO, c), lambda i: (jnp.maximum(i * per - 1, 0), 0))
    return main, halo


def _fill_before(ext, t_ref, h_ref, tr):
    ext[0:HALO, :] = jnp.where(pl.program_id(0) > 0, h_ref[...], 0.0)
    ext[HALO:HALO + tr, :] = t_ref[...]


def _taps(ext, w_ref, tr):
    base = HALO - (CONV_K - 1)
    acc = ext[base:base + tr, :] * w_ref[0:1, :]
    for k in range(1, CONV_K):
        acc = acc + ext[base + k:base + k + tr, :] * w_ref[k:k + 1, :]
    return acc


def _conv_fwd(t, w, b, name="conv_fwd"):
    s_, c = t.shape
    tr = _row_tile(s_)

    def body(t_ref, h_ref, w_ref, b_ref, o_ref, ext):
        _fill_before(ext, t_ref, h_ref, tr)
        o_ref[...] = _silu(_taps(ext, w_ref, tr) + b_ref[...])

    main, halo = _conv_specs(s_, c, tr, False)
    return pl.pallas_call(
        body, name=name, grid=(s_ // tr,),
        in_specs=[main, halo, pl.BlockSpec((CONV_K, c), lambda i: (0, 0)), pl.BlockSpec((1, c), lambda i: (0, 0))],
        out_specs=main, out_shape=jax.ShapeDtypeStruct((s_, c), F32),
        scratch_shapes=[pltpu.VMEM((tr + HALO, c), F32)], compiler_params=_params("parallel"),
    )(t, t, w, b)


def _conv_bwd_pre(t, w, b, dact, name="conv_bwd_pre"):
    s_, c = t.shape
    tr = _row_tile(s_)

    def body(t_ref, h_ref, w_ref, b_ref, da_ref, dpre_ref, dwb_ref, ext):
        @pl.when(pl.program_id(0) == 0)
        def _():
            dwb_ref[...] = jnp.zeros_like(dwb_ref)

        _fill_before(ext, t_ref, h_ref, tr)
        dpre = da_ref[...] * _dsilu(_taps(ext, w_ref, tr) + b_ref[...])
        dpre_ref[...] = dpre
        base = HALO - (CONV_K - 1)
        for k in range(CONV_K):
            dwb_ref[k:k + 1, :] += jnp.sum(dpre * ext[base + k:base + k + tr, :], axis=0, keepdims=True)
        dwb_ref[CONV_K:CONV_K + 1, :] += jnp.sum(dpre, axis=0, keepdims=True)

    main, halo = _conv_specs(s_, c, tr, False)
    return pl.pallas_call(
        body, name=name, grid=(s_ // tr,),
        in_specs=[main, halo, pl.BlockSpec((CONV_K, c), lambda i: (0, 0)), pl.BlockSpec((1, c), lambda i: (0, 0)), main],
        out_specs=[main, pl.BlockSpec((8, c), lambda i: (0, 0))],
        out_shape=[jax.ShapeDtypeStruct((s_, c), F32), jax.ShapeDtypeStruct((8, c), F32)],
        scratch_shapes=[pltpu.VMEM((tr + HALO, c), F32)], compiler_params=_params("arbitrary"),
    )(t, t, w, b, dact)


def _conv_bwd_in(dpre, w, name="conv_bwd_in"):
    s_, c = dpre.shape
    tr = _row_tile(s_)
    nt = s_ // tr

    def body(d_ref, h_ref, w_ref, o_ref, ext):
        ext[0:tr, :] = d_ref[...]
        ext[tr:tr + HALO, :] = jnp.where(pl.program_id(0) < nt - 1, h_ref[...], 0.0)
        acc = ext[CONV_K - 1:CONV_K - 1 + tr, :] * w_ref[0:1, :]
        for k in range(1, CONV_K):
            acc = acc + ext[CONV_K - 1 - k:CONV_K - 1 - k + tr, :] * w_ref[k:k + 1, :]
        o_ref[...] = acc.astype(o_ref.dtype)

    main, halo = _conv_specs(s_, c, tr, True)
    return pl.pallas_call(
        body, name=name, grid=(nt,),
        in_specs=[main, halo, pl.BlockSpec((CONV_K, c), lambda i: (0, 0))],
        out_specs=main, out_shape=jax.ShapeDtypeStruct((s_, c), MXU_DTYPE),
        scratch_shapes=[pltpu.VMEM((tr + HALO, c), F32)], compiler_params=_params("parallel"),
    )(dpre, dpre, w)


def _ssd_chunk_common(dt_ref, dtt_ref, br_ref, bc_ref, ar_ref, ac_ref):
    li = lax.broadcasted_iota(jnp.int32, (CHUNK, CHUNK), 0)
    si = lax.broadcasted_iota(jnp.int32, (CHUNK, CHUNK), 1)
    lower = li >= si
    lower_b = lower.astype(BF16)
    upper_b = (li <= si).astype(BF16)
    zr = dt_ref[...] + br_ref[...]
    dtc = _softplus(zr)
    a_row = -jnp.exp(ar_ref[...])
    acum = _exact_dot(lower_b, dtc * a_row, 1, 0, False)
    dtt = _softplus(dtt_ref[...] + bc_ref[...])
    acum_t = _exact_dot(dtt * (-jnp.exp(ac_ref[...])), upper_b, 1, 0, True)
    return lower, upper_b, zr, dtc, a_row, acum, acum_t


def _head_terms(h, lower, dtc, acum, acum_t):
    lane = lax.broadcasted_iota(jnp.int32, (1, LANE), 1)
    sub = lax.broadcasted_iota(jnp.int32, (SSD_H, 1), 0)
    rowid = lax.broadcasted_iota(jnp.int32, (CHUNK, 1), 0)
    oh = (lane == HEAD_LANE + h).astype(F32)
    acol = jnp.sum(acum * oh, axis=1, keepdims=True)
    dcol = jnp.sum(dtc * oh, axis=1, keepdims=True)
    arow = jnp.sum(acum_t * (sub == h).astype(F32), axis=0, keepdims=True)
    alast = jnp.sum(jnp.where(rowid == CHUNK - 1, acol, 0.0), axis=0, keepdims=True)
    decay = jnp.exp(jnp.where(lower, acol - arow, -jnp.inf))
    return oh, acol, dcol, alast, decay


SSD_PAIRS = SSD_H // 2
PAIRS_PER_GROUP = SSD_E // 2


def _ps(q):
    return slice(q * LANE, (q + 1) * LANE)


def _gs(off, g):
    return slice(off + g * SSD_N, off + (g + 1) * SSD_N)


def _lanes(c0, c1):
    return jnp.where(lax.broadcasted_iota(jnp.int32, (1, LANE), 1) < SSD_P, c0, c1)


def _rows(c0, c1):
    return jnp.where(lax.broadcasted_iota(jnp.int32, (LANE, 1), 0) < SSD_P, c0, c1)


def _lane_halves(t):
    first = lax.broadcasted_iota(jnp.int32, (1, LANE), 1) < SSD_P
    return (jnp.sum(jnp.where(first, t, 0.0), axis=1, keepdims=True),
            jnp.sum(jnp.where(first, 0.0, t), axis=1, keepdims=True))


def _ssd_in_specs(rev):
    def ci(c):
        return c if rev is None else rev - c
    return [pl.BlockSpec((CHUNK, CONV_DIM), lambda c: (ci(c), 0)),
            pl.BlockSpec((CHUNK, LANE), lambda c: (ci(c), 0)),
            pl.BlockSpec((SSD_H, CHUNK), lambda c: (0, ci(c))),
            pl.BlockSpec((1, LANE), lambda c: (0, 0)), pl.BlockSpec((SSD_H, 1), lambda c: (0, 0)),
            pl.BlockSpec((1, LANE), lambda c: (0, 0)), pl.BlockSpec((SSD_H, 1), lambda c: (0, 0)),
            pl.BlockSpec((SSD_PAIRS, 1, LANE), lambda c: (0, 0, 0))]


def _ssd_fwd(xbc, small, dtt, bias_r, bias_c, alog_r, alog_c, dsk, name="ssd_fwd"):
    s_ = xbc.shape[0]
    nc = s_ // CHUNK

    def body(x_ref, dt_ref, dtt_ref, br_ref, bc_ref, ar_ref, ac_ref, dsk_ref, y_ref, prev_ref, state):
        @pl.when(pl.program_id(0) == 0)
        def _():
            state[...] = jnp.zeros_like(state)

        lower, _, _, dtc, _, acum, acum_t = _ssd_chunk_common(dt_ref, dtt_ref, br_ref, bc_ref, ar_ref, ac_ref)
        for g in range(SSD_G):
            bb = _mx(x_ref[:, _gs(B_OFF, g)])
            cb_ = _mx(x_ref[:, _gs(C_OFF, g)])
            cbm = _dot(cb_, bb, 1, 1)
            for e in range(PAIRS_PER_GROUP):
                q = g * PAIRS_PER_GROUP + e
                _, acol0, dcol0, alast0, decay0 = _head_terms(2 * q, lower, dtc, acum, acum_t)
                _, acol1, dcol1, alast1, decay1 = _head_terms(2 * q + 1, lower, dtc, acum, acum_t)
                x = x_ref[:, _ps(q)]
                xdt = x * _lanes(dcol0, dcol1)
                xb = _mx(xdt)
                yd = _lanes(_dot(_mx(cbm * decay0), xb, 1, 0), _dot(_mx(cbm * decay1), xb, 1, 0))
                prev = state[q]
                prev_ref[0, q] = prev
                yo = _dot(cb_, _mx(prev), 1, 1) * _lanes(jnp.exp(acol0), jnp.exp(acol1))
                ds = _lanes(jnp.exp(alast0 - acol0), jnp.exp(alast1 - acol1))
                st = _dot(_mx(xdt * ds), bb, 0, 0)
                state[q] = prev * _rows(jnp.exp(alast0), jnp.exp(alast1)) + st
                y_ref[:, _ps(q)] = yd + yo + x * dsk_ref[q]

    psp = pl.BlockSpec((1, SSD_PAIRS, LANE, SSD_N), lambda c: (c, 0, 0, 0))
    return pl.pallas_call(
        body, name=name, grid=(nc,),
        in_specs=_ssd_in_specs(None), out_specs=[pl.BlockSpec((CHUNK, SSD_W), lambda c: (c, 0)), psp],
        out_shape=[jax.ShapeDtypeStruct((s_, SSD_W), F32),
                   jax.ShapeDtypeStruct((nc, SSD_PAIRS, LANE, SSD_N), F32)],
        scratch_shapes=[pltpu.VMEM((SSD_PAIRS, LANE, SSD_N), F32)],
        compiler_params=_params("arbitrary"),
    )(xbc, small, dtt, bias_r, bias_c, alog_r, alog_c, dsk)


def _ssd_bwd(xbc, small, dtt, bias_r, bias_c, alog_r, alog_c, dsk, prev, dy, name="ssd_bwd"):
    s_ = xbc.shape[0]
    nc = s_ // CHUNK

    def body(x_ref, dt_ref, dtt_ref, br_ref, bc_ref, ar_ref, ac_ref, dsk_ref, prev_ref, dy_ref,
             dx_ref, ddt_ref, dpar_ref, dstate):
        @pl.when(pl.program_id(0) == 0)
        def _():
            dstate[...] = jnp.zeros_like(dstate)
            dpar_ref[...] = jnp.zeros_like(dpar_ref)

        lower, upper_b, zr, dtc, a_row, acum, acum_t = _ssd_chunk_common(
            dt_ref, dtt_ref, br_ref, bc_ref, ar_ref, ac_ref)
        strict = (lax.broadcasted_iota(jnp.int32, (CHUNK, CHUNK), 1)
                  < lax.broadcasted_iota(jnp.int32, (CHUNK, CHUNK), 0))
        strict_b = strict.astype(BF16)
        col2 = lax.broadcasted_iota(jnp.int32, (CHUNK, 2 * CHUNK), 1)
        strict2 = (jnp.where(col2 >= CHUNK, col2 - CHUNK, col2)
                   < lax.broadcasted_iota(jnp.int32, (CHUNK, 2 * CHUNK), 0))
        da_in = jnp.zeros((CHUNK, LANE), F32)
        r_off = jnp.zeros((CHUNK, LANE), F32)
        c_int = jnp.zeros((CHUNK, LANE), F32)
        c_row = jnp.zeros((1, LANE), F32)
        ddt = jnp.zeros((CHUNK, LANE), F32)
        dskip = jnp.zeros((1, LANE), F32)
        for g in range(SSD_G):
            bb = _mx(x_ref[:, _gs(B_OFF, g)])
            cb_ = _mx(x_ref[:, _gs(C_OFF, g)])
            cbm = _dot(cb_, bb, 1, 1)
            dcb = jnp.zeros((CHUNK, CHUNK), F32)
            dc_acc = jnp.zeros((CHUNK, SSD_N), F32)
            db_acc = jnp.zeros((CHUNK, SSD_N), F32)
            for e in range(PAIRS_PER_GROUP):
                q = g * PAIRS_PER_GROUP + e
                oh0, acol0, dcol0, alast0, decay0 = _head_terms(2 * q, lower, dtc, acum, acum_t)
                oh1, acol1, dcol1, alast1, decay1 = _head_terms(2 * q + 1, lower, dtc, acum, acum_t)
                x = x_ref[:, _ps(q)]
                dy = dy_ref[:, _ps(q)]
                dcol = _lanes(dcol0, dcol1)
                xdt = x * dcol
                xb = _mx(xdt)
                eacol = _lanes(jnp.exp(acol0), jnp.exp(acol1))
                ds = _lanes(jnp.exp(alast0 - acol0), jnp.exp(alast1 - acol1))
                ealast = _rows(jnp.exp(alast0), jnp.exp(alast1))
                dyb = _mx(dy)
                dyb0, dyb1 = _mx(_lanes(dy, 0.0)), _mx(_lanes(0.0, dy))
                dsh = dstate[q]
                dshb = _mx(dsh)
                prev = prev_ref[0, q]
                prevb = _mx(prev)
                dxdt_inter = ds * _dot(bb, dshb, 1, 1)
                dxdt = _lanes(_dot(_mx(cbm * decay0), dyb, 0, 0), _dot(_mx(cbm * decay1), dyb, 0, 0)) + dxdt_inter
                dwl0 = _dot(dyb0, xb, 1, 1) * decay0
                dwl1 = _dot(dyb1, xb, 1, 1) * decay1
                dcb = dcb + dwl0 + dwl1
                dyeb = _mx(dy * eacol)
                dc_acc = dc_acc + _dot(dyeb, prevb, 1, 0)
                db_acc = db_acc + _dot(_mx(xdt * ds), dshb, 1, 0)
                dstate[q] = _dot(dyeb, cb_, 0, 0) + ealast * dsh
                above = _exact_dot(upper_b, jnp.concatenate([dwl0 * cbm, dwl1 * cbm], axis=1), 1, 0, False)
                above = jnp.where(strict2, above, 0.0)
                da_in = (da_in + jnp.sum(above[:, :CHUNK], axis=1, keepdims=True) * oh0
                         + jnp.sum(above[:, CHUNK:], axis=1, keepdims=True) * oh1)
                y_off = _dot(cb_, prevb, 1, 1) * eacol
                r0, r1 = _lane_halves(dy * y_off)
                r_off = r_off + r0 * oh0 + r1 * oh1
                c0, c1 = _lane_halves(xdt * dxdt_inter)
                c_int = c_int + c0 * oh0 + c1 * oh1
                both = jnp.sum(dsh * prev, axis=1, keepdims=True) * ealast
                c_row = (c_row + jnp.sum(_rows(both, 0.0), axis=0, keepdims=True) * oh0
                         + jnp.sum(_rows(0.0, both), axis=0, keepdims=True) * oh1)
                t0, t1 = _lane_halves(dxdt * x)
                ddt = ddt + t0 * oh0 + t1 * oh1
                dx_ref[:, _ps(q)] = dxdt * dcol + dy * dsk_ref[q]
                k0, k1 = _lane_halves(dy * x)
                dskip = (dskip + jnp.sum(k0, axis=0, keepdims=True) * oh0 + jnp.sum(k1, axis=0, keepdims=True) * oh1)
            dcbb = _mx(dcb)
            dx_ref[:, _gs(C_OFF, g)] = dc_acc + _dot(dcbb, bb, 1, 0)
            dx_ref[:, _gs(B_OFF, g)] = db_acc + _dot(dcbb, cb_, 0, 0)
        da = (da_in + _exact_dot(upper_b, r_off, 1, 0, False) + _exact_dot(strict_b, c_int, 1, 0, False) + c_row)
        draw = (ddt + da * a_row) * _sigmoid(zr)
        ddt_ref[...] = draw
        dpar_ref[0:1, :] += jnp.sum(draw, axis=0, keepdims=True)
        dpar_ref[1:2, :] += jnp.sum(da * dtc, axis=0, keepdims=True) * a_row
        dpar_ref[2:3, :] += dskip

    rev = nc - 1
    psp = pl.BlockSpec((1, SSD_PAIRS, LANE, SSD_N), lambda c: (rev - c, 0, 0, 0))
    return pl.pallas_call(
        body, name=name, grid=(nc,),
        in_specs=_ssd_in_specs(rev) + [psp, pl.BlockSpec((CHUNK, SSD_W), lambda c: (rev - c, 0))],
        out_specs=[pl.BlockSpec((CHUNK, CONV_DIM), lambda c: (rev - c, 0)),
                   pl.BlockSpec((CHUNK, LANE), lambda c: (rev - c, 0)), pl.BlockSpec((8, LANE), lambda c: (0, 0))],
        out_shape=[jax.ShapeDtypeStruct((s_, CONV_DIM), F32), jax.ShapeDtypeStruct((s_, LANE), F32),
                   jax.ShapeDtypeStruct((8, LANE), F32)],
        scratch_shapes=[pltpu.VMEM((SSD_PAIRS, LANE, SSD_N), F32)],
        compiler_params=_params("arbitrary"),
    )(xbc, small, dtt, bias_r, bias_c, alog_r, alog_c, dsk, prev, dy)


GN = SSD_W // SSD_G


def _gated_norm_fwd(y, z, w, cat, name="gated_norm_fwd"):
    s_, f = y.shape
    tr = _row_tile(s_)

    def body(y_ref, z_ref, w_ref, cat_ref, o_ref):
        for g in range(SSD_G):
            sl = slice(g * GN, (g + 1) * GN)
            gg = y_ref[:, sl] * _silu(z_ref[:, sl])
            r = lax.rsqrt(jnp.mean(gg * gg, axis=-1, keepdims=True) + EPS)
            o_ref[:, sl] = (gg * r * w_ref[:, sl]).astype(o_ref.dtype)

    row = pl.BlockSpec((tr, f), lambda i: (i, 0))
    wsp = pl.BlockSpec((1, f), lambda i: (0, 0))
    return pl.pallas_call(
        body, name=name, grid=(s_ // tr,),
        in_specs=[row, row, wsp, pl.BlockSpec(memory_space=pl.ANY)], out_specs=pl.BlockSpec((tr, f), lambda i: (i, 1)),
        out_shape=jax.ShapeDtypeStruct(cat.shape, cat.dtype), input_output_aliases={3: 0},
        compiler_params=_params("parallel"),
    )(y, z, w.reshape(1, f), cat)


def _gated_norm_bwd(y, z, w, dout, name="gated_norm_bwd"):
    s_, f = y.shape
    tr = _row_tile(s_)

    def body(y_ref, z_ref, w_ref, do_ref, dy_ref, dz_ref, dw_ref):
        @pl.when(pl.program_id(0) == 0)
        def _():
            dw_ref[...] = jnp.zeros_like(dw_ref)

        for g in range(SSD_G):
            sl = slice(g * GN, (g + 1) * GN)
            yv = y_ref[:, sl]
            zv = z_ref[:, sl]
            dov = do_ref[:, sl].astype(F32)
            sz = _silu(zv)
            gg = yv * sz
            r = lax.rsqrt(jnp.mean(gg * gg, axis=-1, keepdims=True) + EPS)
            gw = dov * w_ref[:, sl]
            c = jnp.mean(gw * gg, axis=-1, keepdims=True)
            dgg = r * gw - gg * (r * r * r * c)
            dy_ref[:, sl] = dgg * sz
            dz_ref[:, sl] = (dgg * yv * _dsilu(zv)).astype(dz_ref.dtype)
            dw_ref[:, sl] += jnp.sum(dov * gg * r, axis=0, keepdims=True)

    row = pl.BlockSpec((tr, f), lambda i: (i, 0))
    wsp = pl.BlockSpec((1, f), lambda i: (0, 0))
    return pl.pallas_call(
        body, name=name, grid=(s_ // tr,),
        in_specs=[row, row, wsp, pl.BlockSpec((tr, f), lambda i: (i, 1))], out_specs=[row, row, wsp],
        out_shape=[jax.ShapeDtypeStruct((s_, f), F32), jax.ShapeDtypeStruct((s_, f), MXU_DTYPE),
                   jax.ShapeDtypeStruct((1, f), F32)],
        compiler_params=_params("arbitrary"),
    )(y, z, w.reshape(1, f), dout)


def _ffn_fwd(vv, w_gate, w_up, name="ffn_gate_up"):
    s_, d = vv.shape
    nb, f8, _ = w_gate.shape
    tm = _pick(s_, (1024, 512, 256, 128))

    def body(v_ref, wg_ref, wu_ref, g_ref, u_ref, a_ref):
        for rs in _row_slices(tm, 16):
            a = _mx(v_ref[rs, :])
            g = _dot(a, _mx(wg_ref[...]), 1, 1)
            u = _dot(a, _mx(wu_ref[...]), 1, 1)
            g_ref[rs, :] = g.astype(g_ref.dtype)
            u_ref[rs, :] = u.astype(u_ref.dtype)
            a_ref[rs, :] = (_silu(g) * u).astype(a_ref.dtype)

    wsp = pl.BlockSpec((None, f8, d), lambda j, i: (j, 0, 0))
    osp = pl.BlockSpec((None, tm, f8), lambda j, i: (j, i, 0))
    return pl.pallas_call(
        body, name=name, grid=(nb, s_ // tm),
        in_specs=[pl.BlockSpec((tm, d), lambda j, i: (i, 0)), wsp, wsp], out_specs=[osp] * 3,
        out_shape=[jax.ShapeDtypeStruct((nb, s_, f8), MXU_DTYPE)] * 3,
        compiler_params=_params("parallel", "parallel"),
    )(vv, w_gate, w_up)


def _ffn_bwd_act(dffn, w_down, gate, up, name="ffn_d_act"):
    s_, d = dffn.shape
    nb, f8, _ = w_down.shape
    tm = _pick(s_, (1024, 512, 256, 128))

    def body(d_ref, w_ref, g_ref, u_ref, dg_ref, du_ref):
        for rs in _row_slices(tm, 16):
            dact = _dot(_mx(d_ref[rs, :]), _mx(w_ref[...]), 1, 1)
            g = g_ref[rs, :].astype(F32)
            s = _sigmoid(g)
            dg_ref[rs, :] = (dact * u_ref[rs, :].astype(F32) * (s * (1.0 + g * (1.0 - s)))).astype(dg_ref.dtype)
            du_ref[rs, :] = (dact * (g * s)).astype(du_ref.dtype)

    osp = pl.BlockSpec((None, tm, f8), lambda j, i: (j, i, 0))
    return pl.pallas_call(
        body, name=name, grid=(nb, s_ // tm),
        in_specs=[pl.BlockSpec((tm, d), lambda j, i: (i, 0)), pl.BlockSpec((None, f8, d), lambda j, i: (j, 0, 0)),
                  osp, osp],
        out_specs=[osp, osp], out_shape=[jax.ShapeDtypeStruct((nb, s_, f8), MXU_DTYPE)] * 2,
        compiler_params=_params("parallel", "parallel"),
    )(dffn, w_down, gate, up)


def _ffn_bwd_in(dgate, w_gate, dup, w_up, name="ffn_d_in"):
    nb, s_, f8 = dgate.shape
    d = w_gate.shape[2]
    tm = _pick(s_, (1024, 512, 256, 128))
    tn = _pick(d, (1024, 512, 256, 128))

    def body(dg_ref, wg_ref, du_ref, wu_ref, o_ref, acc):
        j = pl.program_id(2)

        @pl.when(j == 0)
        def _():
            acc[...] = jnp.zeros_like(acc)

        for rs in _row_slices(tm, 16):
            acc[rs, :] += (_dot(_mx(dg_ref[rs, :]), _mx(wg_ref[...]), 1, 0)
                           + _dot(_mx(du_ref[rs, :]), _mx(wu_ref[...]), 1, 0))

        @pl.when(j == nb - 1)
        def _():
            o_ref[...] = acc[...]

    asp = pl.BlockSpec((None, tm, f8), lambda i, n, j: (j, i, 0))
    wsp = pl.BlockSpec((None, f8, tn), lambda i, n, j: (j, 0, n))
    return pl.pallas_call(
        body, name=name, grid=(s_ // tm, d // tn, nb),
        in_specs=[asp, wsp, asp, wsp], out_specs=pl.BlockSpec((tm, tn), lambda i, n, j: (i, n)),
        out_shape=jax.ShapeDtypeStruct((s_, d), F32), scratch_shapes=[pltpu.VMEM((tm, tn), F32)],
        compiler_params=_params("parallel", "parallel", "arbitrary"),
    )(dgate, w_gate, dup, w_up)


def _adam_math(g, w, m, v):
    m2 = ADAM_B1 * m + (1.0 - ADAM_B1) * g
    v2 = ADAM_B2 * v + (1.0 - ADAM_B2) * (g * g)
    m_hat = m2 / (1.0 - ADAM_B1 ** ADAM_STEP)
    v_hat = v2 / (1.0 - ADAM_B2 ** ADAM_STEP)
    delta = -ADAM_LR * (m_hat / (jnp.sqrt(v_hat) + ADAM_EPS) + ADAM_WD * w)
    return delta, m2, v2


def _adamw(parts, w, m, v, name="adamw"):
    nd, r_, c = parts.shape
    tr = _pick(r_, (128, 64, 32, 16))
    tc = c
    if tr == r_ and r_ > 128:
        tc = _pick(c, (256, 128))

    def body(p_ref, w_ref, m_ref, v_ref, g_ref, d_ref, m2_ref, v2_ref):
        g = p_ref[0].astype(F32)
        for i in range(1, nd):
            g = g + p_ref[i].astype(F32)
        delta, m2, v2 = _adam_math(g, w_ref[...], m_ref[...], v_ref[...])
        g_ref[...] = g
        d_ref[...] = delta
        m2_ref[...] = m2
        v2_ref[...] = v2

    row = pl.BlockSpec((tr, tc), lambda i, j: (i, j))
    psp = pl.BlockSpec((nd, tr, tc), lambda i, j: (0, i, j))
    return pl.pallas_call(
        body, name=name, grid=(r_ // tr, c // tc), in_specs=[psp, row, row, row], out_specs=[row] * 4,
        out_shape=[jax.ShapeDtypeStruct((r_, c), F32)] * 4, compiler_params=_params("parallel", "parallel"),
    )(parts, w, m, v)


def _adamw_small(parts, w, m, v, name="adamw_small"):
    nd = parts.shape[0]

    def body(p_ref, w_ref, m_ref, v_ref, g_ref, d_ref, m2_ref, v2_ref):
        g = p_ref[0]
        for i in range(1, nd):
            g = g + p_ref[i]
        delta, m2, v2 = _adam_math(g, w_ref[...], m_ref[...], v_ref[...])
        g_ref[...] = g
        d_ref[...] = delta
        m2_ref[...] = m2
        v2_ref[...] = v2

    return pl.pallas_call(
        body, name=name, out_shape=[jax.ShapeDtypeStruct(w.shape, F32)] * 4,
        compiler_params=pltpu.CompilerParams(vmem_limit_bytes=VMEM_LIMIT_BYTES),
    )(parts, w, m, v)


_HBM = pl.BlockSpec(memory_space=pltpu.HBM)
_MESH = pl.DeviceIdType.MESH


def _all_gather(xs, name):
    na = len(xs)

    def body(*refs):
        x_refs, out_refs = refs[:na], refs[na:2 * na]
        send_sems, recv_sems, local_sems = refs[2 * na:]
        x, y, c = lax.axis_index("x"), lax.axis_index("y"), lax.axis_index("c")
        me, sibling = (x, y, c), (x, y, 1 - c)
        chips = [(1 - x, y), (x, 1 - y), (1 - x, 1 - y)]

        def slot(a, px, py, pc):
            return out_refs[a].at[4 * px + 2 * py + pc]

        def copy(a, k, block, to, src=None):
            return pltpu.make_async_remote_copy(
                src_ref=slot(a, *block) if src is None else src, dst_ref=slot(a, *block),
                send_sem=send_sems.at[a, k], recv_sem=recv_sems.at[a, k], device_id=to, device_id_type=_MESH)

        mine = [pltpu.make_async_copy(x_refs[a], slot(a, *me), local_sems.at[a]) for a in range(na)]
        started = []
        for a in range(na):
            mine[a].start()
            first = [copy(a, 0, me, sibling, src=x_refs[a])]
            first += [copy(a, 1 + j, me, (*chip, c), src=x_refs[a]) for j, chip in enumerate(chips)]
            for cp in first:
                cp.start()
            started += first
        for a in range(na):
            for j, chip in enumerate(chips):
                copy(a, 1 + j, (*chip, c), me).wait_recv()
                fwd = copy(a, 4 + j, (*chip, c), sibling)
                fwd.start()
                started.append(fwd)
        for a in range(na):
            copy(a, 0, sibling, me).wait_recv()
            for j, chip in enumerate(chips):
                copy(a, 4 + j, (*chip, 1 - c), me).wait_recv()
        for cp in started:
            cp.wait_send()
        for cp in mine:
            cp.wait()

    return pl.pallas_call(
        body, name=name, out_shape=[jax.ShapeDtypeStruct((N_DEV,) + t.shape, t.dtype) for t in xs],
        in_specs=[_HBM] * na, out_specs=[_HBM] * na,
        scratch_shapes=[pltpu.SemaphoreType.DMA((na, 7)), pltpu.SemaphoreType.DMA((na, 7)),
                        pltpu.SemaphoreType.DMA((na,))],
    )(*xs)


_SEM = pl.BlockSpec(memory_space=pltpu.SEMAPHORE)
_EFFECT = pltpu.SideEffectType.DATAFLOW_SIDE_EFFECTING


def _peers(x, y, c):
    out = []
    for k in range(1, N_DEV):
        px = 1 - x if k & 4 else x
        py = 1 - y if k & 2 else y
        pc = 1 - c if k & 1 else c
        out.append(((px, py, pc), 4 * px + 2 * py + pc))
    return out


def _push_copies(scatter, src_refs, land_refs, send_sems, recv_sems):
    x, y, c = lax.axis_index("x"), lax.axis_index("y"), lax.axis_index("c")
    me = 4 * x + 2 * y + c
    pairs = []
    for a, (src, land) in enumerate(zip(src_refs, land_refs)):
        for k, (peer, slot) in enumerate(_peers(x, y, c)):
            out_src = src.at[slot] if scatter else src
            si = a * (N_DEV - 1) + k
            send = pltpu.make_async_remote_copy(src_ref=out_src, dst_ref=land.at[me], send_sem=send_sems.at[si],
                                                recv_sem=recv_sems.at[si], device_id=peer, device_id_type=_MESH)
            recv = pltpu.make_async_remote_copy(src_ref=out_src, dst_ref=land.at[slot], send_sem=send_sems.at[si],
                                                recv_sem=recv_sems.at[si], device_id=peer, device_id_type=_MESH)
            pairs.append((send, recv))
    return pairs


def _push_start(srcs, scatter, dep, name):
    na = len(srcs)
    shapes = [t.shape[1:] if scatter else t.shape for t in srcs]
    lands = [pltpu.with_memory_space_constraint(lax.empty((N_DEV,) + s, t.dtype), pltpu.HBM) for s, t in zip(shapes, srcs)]

    def body(*refs):
        src_refs, land_refs = refs[:na], refs[na:2 * na]
        send_sems, recv_sems = refs[2 * na + 1], refs[2 * na + 2]
        token = refs[-1]
        for send, _ in _push_copies(scatter, src_refs, land_refs, send_sems, recv_sems):
            send.start()
        token[...] = jnp.zeros_like(token)

    sem = pltpu.SemaphoreType.DMA((na * (N_DEV - 1),))
    outs = pl.pallas_call(
        body, name=name,
        out_shape=(sem, sem) + tuple(pltpu.HBM(t.shape, t.dtype) for t in srcs)
        + tuple(pltpu.HBM(t.shape, t.dtype) for t in lands) + (jax.ShapeDtypeStruct((8, LANE), F32),),
        in_specs=[_HBM] * (2 * na) + [pl.BlockSpec(memory_space=pl.ANY)],
        out_specs=(_SEM, _SEM) + (_HBM,) * (2 * na) + (pl.BlockSpec(memory_space=pltpu.VMEM),),
        input_output_aliases={i: 2 + i for i in range(2 * na)},
        compiler_params=pltpu.CompilerParams(has_side_effects=_EFFECT),
    )(*[pltpu.with_memory_space_constraint(t, pltpu.HBM) for t in srcs], *lands, dep)
    return outs[0], outs[1], outs[2:2 + na], outs[2 + na:2 + 2 * na], outs[-1]


def _push_wait(send_sems, recv_sems, src_thru, land_thru, scatter, after, name):
    na = len(src_thru)

    def body(*refs):
        src_refs, land_refs = refs[:na], refs[na:2 * na]
        ssem, rsem = refs[2 * na], refs[2 * na + 1]
        for send, recv in _push_copies(scatter, src_refs, land_refs, ssem, rsem):
            send.wait_send()
            recv.wait_recv()

    outs = pl.pallas_call(
        body, name=name,
        out_shape=tuple(pltpu.HBM(t.shape, t.dtype) for t in src_thru) + tuple(pltpu.HBM(t.shape, t.dtype) for t in land_thru),
        in_specs=[_HBM] * (2 * na) + [_SEM, _SEM, pl.BlockSpec(memory_space=pl.ANY)],
        out_specs=(_HBM,) * (2 * na),
        input_output_aliases={i: i for i in range(2 * na)},
        compiler_params=pltpu.CompilerParams(has_side_effects=_EFFECT),
    )(*src_thru, *land_thru, send_sems, recv_sems, after)
    return outs[:na], outs[na:]


def _exchange_behind(srcs, scatter, dep, name):
    send_sems, recv_sems, thru, lands, token = _push_start(srcs, scatter, dep, name + "_start")

    def finish(after):
        src_done, land_done = _push_wait(send_sems, recv_sems, thru, lands, scatter, after, name + "_wait")
        return _place_own(land_done, src_done, scatter, name + "_own")

    return token[0, 0], finish


def _place_own(lands, srcs, scatter, name):
    me = (4 * lax.axis_index("x") + 2 * lax.axis_index("y") + lax.axis_index("c")).astype(jnp.int32).reshape(1)
    outs = []
    for a, (land, src) in enumerate(zip(lands, srcs)):
        r_, c_ = land.shape[1:]
        tr = _pick(r_, (512, 256, 128, 64, 32, 16))

        def body(me_ref, land_ref, src_ref, out_ref):
            out_ref[...] = src_ref[...]

        src_spec = (pl.BlockSpec((None, tr, c_), lambda i, me_: (me_[0], i, 0)) if scatter
                    else pl.BlockSpec((tr, c_), lambda i, me_: (i, 0)))
        gs = pltpu.PrefetchScalarGridSpec(
            num_scalar_prefetch=1, grid=(r_ // tr,),
            in_specs=[pl.BlockSpec(memory_space=pl.ANY), src_spec],
            out_specs=pl.BlockSpec((None, tr, c_), lambda i, me_: (me_[0], i, 0)))
        outs.append(pl.pallas_call(
            body, name=f"{name}_{a}", grid_spec=gs, out_shape=jax.ShapeDtypeStruct(land.shape, land.dtype),
            input_output_aliases={1: 0}, compiler_params=_params("arbitrary"),
        )(me, land, src))
    return outs


_BIG = (("w_in", D_MODEL, D_IN, 1), ("w_uq", Q_RANK, HEADS * QK, 1), ("w_ukv", KV_RANK, HEADS * (NOPE + VDIM), 1),
        ("w_out", D_MODEL, D_MODEL, 0), ("w_gate", D_MODEL, D_FF, 1), ("w_up", D_MODEL, D_FF, 1),
        ("w_down", D_FF, D_MODEL, 0))
_TRANSPOSED = ("w_in", "w_uq", "w_gate", "w_up")
_CQKV = (0, Q_RANK + KV_RANK)
_KR = (_CQKV[1], _CQKV[1] + ROPE)
_Z = (_KR[1], _KR[1] + SSD_W)
_XBC = (_Z[1], _Z[1] + CONV_DIM)
_DT = (_XBC[1], _XBC[1] + SSD_H)


def _win_segments(w_in_t):
    w = w_in_t.reshape(D_IN, D_MODEL)
    small = jnp.concatenate([w[_KR[0]:_KR[1]], w[_DT[0]:_DT[1]],
                             jnp.zeros((LANE - ROPE - SSD_H, D_MODEL), w.dtype)], axis=0)
    return w[_CQKV[0]:_CQKV[1]], w[_Z[0]:_Z[1]], w[_XBC[0]:_XBC[1]], small


def _win_from_segments(g_cqkv, g_z, g_xbc, g_small):
    w = jnp.concatenate([g_cqkv, g_small[:ROPE], g_z, g_xbc, g_small[ROPE:ROPE + SSD_H]], axis=0)
    return w.reshape(N_DEV, D_IN // N_DEV, D_MODEL)


_SMALL = (("q_norm_w", 512), ("kv_norm_w", 512), ("conv_b", CONV_DIM), ("dt_bias", SSD_H), ("a_log", SSD_H),
          ("d_skip", SSD_H), ("ssd_norm_w", SSD_W), ("attn_out_norm_w", 1024), ("pre_mix_norm_w", D_MODEL),
          ("post_mix_norm_w", D_MODEL), ("pre_ffn_norm_w", D_MODEL), ("post_ffn_norm_w", D_MODEL),
          ("conv_w", CONV_K * CONV_DIM))
_SMALL_ROWS = -(-sum(-(-n // LANE) for _, n in _SMALL) // 8) * 8


def _pack_small(vals):
    rows = []
    for name, n in _SMALL:
        v = vals[name].reshape(-1).astype(F32)
        pad = -(-n // LANE) * LANE
        rows.append(jnp.pad(v, (0, pad - n)).reshape(-1, LANE))
    m = jnp.concatenate(rows, axis=0)
    return jnp.pad(m, ((0, _SMALL_ROWS - m.shape[0]), (0, 0)))


def _unpack_small(m):
    out, r = {}, 0
    for name, n in _SMALL:
        nr = -(-n // LANE)
        out[name] = m[r:r + nr].reshape(-1)[:n]
        r += nr
    return out


def _head_row(v):
    return jnp.pad(v.reshape(1, -1).astype(F32), ((0, 0), (HEAD_LANE, LANE - HEAD_LANE - v.shape[-1])))


def _local_step(x, positions, target, wg, small, weights, on_grads):
    w_cqkv, w_z, w_xbc, w_small = _win_segments(wg["w_in"])
    conv_w = wg["conv_w"]
    conv_b = small["conv_b"].reshape(1, CONV_DIM)
    qkv_norm_w = jnp.concatenate([small["q_norm_w"], small["kv_norm_w"]])
    attn_norm_w = small["attn_out_norm_w"].reshape(1, HEADS * VDIM)
    scale = QK ** -0.5

    inv_freq = ROPE_THETA ** (-jnp.arange(0, ROPE, 2, dtype=F32) / ROPE)
    ang = positions.astype(F32)[:, None] * inv_freq
    cos2 = jnp.tile(jnp.cos(ang), (1, 2))
    sin2 = jnp.tile(jnp.sin(ang), (1, 2))

    u = _rms_fwd(x, small["pre_mix_norm_w"], out_dtype=MXU_DTYPE, name="pre_mix_norm")
    cqkv = _mm(u, w_cqkv, "nt", name="in_proj_qkv")
    z = _mm(u, w_z, "nt", name="in_proj_z")
    xbc = _mm(u, w_xbc, "nt", name="in_proj_xbc")
    sm = _mm(u, w_small, "nt", name="in_proj_small")

    w_uq, w_ukv, w_out = weights("heads", cqkv)
    w_out = w_out.reshape(D_MODEL, D_MODEL)
    qkvn = _rms_fwd(cqkv, qkv_norm_w, groups=2, out_dtype=MXU_DTYPE, name="qkv_norm")
    q = _mm(qkvn, w_uq, "nt", b_blk=True, out_blk=True, a_cols=(0, Q_RANK), name="q_up")
    kv = _mm(qkvn, w_ukv, "nn", b_blk=True, out_blk=True, a_cols=(Q_RANK, KV_RANK), name="kv_up")
    q_h = _q_prep(q, cos2, sin2, scale, name="q_prep")
    k_h, v_h = _kv_prep(kv, sm, cos2, sin2)
    o_h, lse = _flash_fwd(q_h, k_h, v_h)
    cat = _hnorm_fwd(o_h, attn_norm_w, D_MODEL)

    xbc_act = _conv_fwd(xbc, conv_w, conv_b)
    dtt = jnp.transpose(sm[:, HEAD_LANE:HEAD_LANE + SSD_H])
    ssd_args = (xbc_act, sm, dtt, _head_row(small["dt_bias"]), small["dt_bias"].reshape(SSD_H, 1),
                _head_row(small["a_log"]), small["a_log"].reshape(SSD_H, 1),
                jnp.broadcast_to(small["d_skip"].reshape(SSD_H, 1), (SSD_H, SSD_P)).reshape(SSD_PAIRS, 1, LANE))
    y_ssd, prev = _ssd_fwd(*ssd_args)
    cat = _gated_norm_fwd(y_ssd, z, small["ssd_norm_w"], cat)

    mix = _mm(cat, w_out, "nn", name="out_proj")
    h1, vv = _norm_res_norm(mix, x, small["post_mix_norm_w"], small["pre_ffn_norm_w"])

    w_gate, w_up = weights("ffn_in", mix)
    gate, up, act = _ffn_fwd(vv, w_gate, w_up)
    w_down, = weights("ffn_out", act)
    ffn = _mm(act, w_down, "nn", a_blk=True, b_blk=True, fuse=2, name="ffn_down")
    loss_blk, dy, dffn, g_post_ffn = _loss_head(ffn, h1, target, small["post_ffn_norm_w"])

    g_down = _mm(act, dffn, "tn", a_blk=True, out_blk=True, out_dtype=MXU_DTYPE, name="g_down")
    dgate, dup = _ffn_bwd_act(dffn, w_down, gate, up)
    dvv = _ffn_bwd_in(dgate, w_gate, dup, w_up)
    g_gate = _mm(dgate, vv, "tn", a_blk=True, out_blk=True, out_dtype=MXU_DTYPE, name="g_gate")
    g_up = _mm(dup, vv, "tn", a_blk=True, out_blk=True, out_dtype=MXU_DTYPE, name="g_up")
    pre_ffn_w = small["pre_ffn_norm_w"] + on_grads("ffn", [g_gate, g_up, g_down])
    dh1, dmix, g_pre_ffn, g_post_mix = _norm_res_norm_bwd(h1, pre_ffn_w, dvv, dy, mix, small["post_mix_norm_w"])

    dcat = _mm(dmix, w_out, "nt", name="d_cat")
    g_out = _mm(cat, dmix, "tn", out_dtype=MXU_DTYPE, name="g_out")

    do_h, delta, g_attn_norm = _hnorm_bwd(o_h, attn_norm_w, dcat)
    dq_h, dk_h, dv_h = _flash_bwd(q_h, k_h, v_h, do_h, lse, delta)
    dq = _q_prep(dq_h, cos2, -sin2, scale, name="dq_post")

    dy_ssd, dz, g_ssd_norm = _gated_norm_bwd(y_ssd, z, small["ssd_norm_w"], dcat)
    dxbc_act, ddt, dpar = _ssd_bwd(*ssd_args, prev, dy_ssd)
    dkv, dsm = _dkv_post(dk_h, dv_h, ddt, cos2, -sin2)
    dpre, dwb = _conv_bwd_pre(xbc, conv_w, conv_b, dxbc_act)
    dxbc = _conv_bwd_in(dpre, conv_w)

    dqn = _mm(dq, w_uq, "nn", a_blk=True, b_blk=True, fuse=HEADS, name="d_qn")
    dkvn = _mm(dkv, w_ukv, "nt", a_blk=True, b_blk=True, fuse=HEADS, name="d_kvn")
    g_uq = _mm(dq, qkvn, "tn", a_blk=True, out_blk=True, b_cols=(0, Q_RANK), out_dtype=MXU_DTYPE, name="g_uq")
    g_ukv = _mm(qkvn, dkv, "tn", b_blk=True, out_blk=True, a_cols=(Q_RANK, KV_RANK), out_dtype=MXU_DTYPE, name="g_ukv")
    heads_token = on_grads("heads", [g_uq, g_ukv, g_out.reshape(N_DEV, D_MODEL // N_DEV, D_MODEL)])
    dcqkv, g_qkv_norm = _rms_bwd(cqkv, qkv_norm_w + heads_token, [dqn, dkvn], out_dtype=MXU_DTYPE, name="qkv_norm_bwd")

    g_in = _win_from_segments(_mm(dcqkv, u, "tn", out_dtype=MXU_DTYPE, name="g_in_qkv"),
                              _mm(dz, u, "tn", out_dtype=MXU_DTYPE, name="g_in_z"),
                              _mm(dxbc, u, "tn", out_dtype=MXU_DTYPE, name="g_in_xbc"),
                              _mm(dsm, u, "tn", out_dtype=MXU_DTYPE, name="g_in_small"))
    in_token = on_grads("in", [g_in])
    du = _mm(dsm + in_token.astype(dsm.dtype), w_small, "nn", name="d_u_small")
    du = _mm(dcqkv, w_cqkv, "nn", add=du, name="d_u_qkv")
    du = _mm(dz, w_z, "nn", add=du, name="d_u_z")
    du = _mm(dxbc, w_xbc, "nn", add=du, name="d_u_xbc")
    dx, g_pre_mix = _rms_bwd(x, small["pre_mix_norm_w"], [du], res=dh1, name="pre_mix_norm_bwd")

    hl = slice(HEAD_LANE, HEAD_LANE + SSD_H)
    g_small = {"q_norm_w": g_qkv_norm[0, :Q_RANK], "kv_norm_w": g_qkv_norm[0, Q_RANK:], "conv_b": dwb[CONV_K],
               "dt_bias": dpar[0, hl], "a_log": dpar[1, hl], "d_skip": dpar[2, hl], "ssd_norm_w": g_ssd_norm,
               "attn_out_norm_w": g_attn_norm, "pre_mix_norm_w": g_pre_mix, "post_mix_norm_w": g_post_mix,
               "pre_ffn_norm_w": g_pre_ffn, "post_ffn_norm_w": g_post_ffn, "conv_w": dwb[:CONV_K]}
    return loss_blk[0, 0], dx, g_small


_WEIGHT_ORDER = ("w_in", "q_norm_w", "w_uq", "kv_norm_w", "w_ukv", "conv_w", "conv_b", "dt_bias", "a_log", "d_skip",
                 "ssd_norm_w", "attn_out_norm_w", "w_out", "pre_mix_norm_w", "post_mix_norm_w", "pre_ffn_norm_w",
                 "post_ffn_norm_w", "w_gate", "w_up", "w_down")


def kernel(x, positions, w_in, q_norm_w, w_uq, kv_norm_w, w_ukv, conv_w, conv_b, dt_bias, a_log, d_skip, ssd_norm_w, attn_out_norm_w, w_out, pre_mix_norm_w, post_mix_norm_w, pre_ffn_norm_w, post_ffn_norm_w, w_gate, w_up, w_down, loss_target, m_w_in, m_q_norm_w, m_w_uq, m_kv_norm_w, m_w_ukv, m_conv_w, m_conv_b, m_dt_bias, m_a_log, m_d_skip, m_ssd_norm_w, m_attn_out_norm_w, m_w_out, m_pre_mix_norm_w, m_post_mix_norm_w, m_pre_ffn_norm_w, m_post_ffn_norm_w, m_w_gate, m_w_up, m_w_down, v_w_in, v_q_norm_w, v_w_uq, v_kv_norm_w, v_w_ukv, v_conv_w, v_conv_b, v_dt_bias, v_a_log, v_d_skip, v_ssd_norm_w, v_attn_out_norm_w, v_w_out, v_pre_mix_norm_w, v_post_mix_norm_w, v_pre_ffn_norm_w, v_post_ffn_norm_w, v_w_gate, v_w_up, v_w_down):
    w = dict(w_in=w_in, q_norm_w=q_norm_w, w_uq=w_uq, kv_norm_w=kv_norm_w, w_ukv=w_ukv, conv_w=conv_w, conv_b=conv_b,
             dt_bias=dt_bias, a_log=a_log, d_skip=d_skip, ssd_norm_w=ssd_norm_w, attn_out_norm_w=attn_out_norm_w,
             w_out=w_out, pre_mix_norm_w=pre_mix_norm_w, post_mix_norm_w=post_mix_norm_w,
             pre_ffn_norm_w=pre_ffn_norm_w, post_ffn_norm_w=post_ffn_norm_w, w_gate=w_gate, w_up=w_up, w_down=w_down)
    m = dict(w_in=m_w_in, q_norm_w=m_q_norm_w, w_uq=m_w_uq, kv_norm_w=m_kv_norm_w, w_ukv=m_w_ukv, conv_w=m_conv_w,
             conv_b=m_conv_b, dt_bias=m_dt_bias, a_log=m_a_log, d_skip=m_d_skip, ssd_norm_w=m_ssd_norm_w,
             attn_out_norm_w=m_attn_out_norm_w, w_out=m_w_out, pre_mix_norm_w=m_pre_mix_norm_w,
             post_mix_norm_w=m_post_mix_norm_w, pre_ffn_norm_w=m_pre_ffn_norm_w, post_ffn_norm_w=m_post_ffn_norm_w,
             w_gate=m_w_gate, w_up=m_w_up, w_down=m_w_down)
    v = dict(w_in=v_w_in, q_norm_w=v_q_norm_w, w_uq=v_w_uq, kv_norm_w=v_kv_norm_w, w_ukv=v_w_ukv, conv_w=v_conv_w,
             conv_b=v_conv_b, dt_bias=v_dt_bias, a_log=v_a_log, d_skip=v_d_skip, ssd_norm_w=v_ssd_norm_w,
             attn_out_norm_w=v_attn_out_norm_w, w_out=v_w_out, pre_mix_norm_w=v_pre_mix_norm_w,
             post_mix_norm_w=v_post_mix_norm_w, pre_ffn_norm_w=v_pre_ffn_norm_w, post_ffn_norm_w=v_post_ffn_norm_w,
             w_gate=v_w_gate, w_up=v_w_up, w_down=v_w_down)
    w, m, v = ({k: t[0] for k, t in d.items()} for d in (w, m, v))
    me = 4 * lax.axis_index("x") + 2 * lax.axis_index("y") + lax.axis_index("c")
    groups = {"in": ("w_in",), "heads": ("w_uq", "w_ukv", "w_out"), "ffn_in": ("w_gate", "w_up"),
              "ffn_out": ("w_down",)}
    cshard = CONV_DIM // N_DEV
    for name in _TRANSPOSED:
        w[name], m[name], v[name] = w[name].T, m[name].T, v[name].T

    shards = [w["w_in"].astype(MXU_DTYPE),
              jnp.stack(_split3(w["conv_w"])).reshape(3 * CONV_K, cshard).astype(MXU_DTYPE)]
    w_in_g, cw = _all_gather(shards, name="gather_weights")
    cw = cw.astype(F32).reshape(N_DEV, 3, CONV_K, cshard)
    wg = {"w_in": w_in_g, "conv_w": jnp.transpose(cw[:, 0] + cw[:, 1] + cw[:, 2], (1, 0, 2)).reshape(CONV_K, CONV_DIM)}
    arriving, dep, started = {}, wg["conv_w"], jnp.zeros((), F32)
    small = {name: w[name] for name, _ in _SMALL if name != "conv_w"}
    for group in ("heads", "ffn_in", "ffn_out"):
        token, arriving[group] = _exchange_behind([w[name].astype(MXU_DTYPE) for name in groups[group]], False,
                                                  dep, group + "_weights")
        started = started + token
        dep = jnp.zeros((8, LANE), F32) + started
    small["pre_mix_norm_w"] = small["pre_mix_norm_w"] + started

    leaving = {}

    def on_grads(group, gs):
        token, leaving[group] = _exchange_behind(gs, True, jnp.zeros((8, LANE), F32), group + "_grads")
        return token

    loss_local, dx, g_small = _local_step(x[0], positions[0], loss_target[0], wg, small,
                                          lambda group, after: arriving[group](after), on_grads)
    loss = lax.psum(loss_local, ("x", "y", "c"))

    recv = {}
    for group, names in (("ffn", ("w_gate", "w_up", "w_down")), ("heads", groups["heads"]), ("in", groups["in"])):
        recv.update(zip(names, leaving[group](dx)))
    grads, deltas, new_m, new_v = {}, {}, {}, {}
    for name, parts in recv.items():
        outs = _adamw(parts, w[name], m[name], v[name], name="adamw_" + name)
        if name in _TRANSPOSED:
            outs = [t.T for t in outs]
        grads[name], deltas[name], new_m[name], new_v[name] = outs

    def embed(t):
        return lax.dynamic_update_slice(jnp.zeros((CONV_K, CONV_DIM), F32), t, (0, me * cshard))

    parts_s = _all_gather([_pack_small(g_small)], name="gather_small_grads")[0]
    packs = [_pack_small({**{n_: d[n_] for n_, _ in _SMALL if n_ != "conv_w"}, "conv_w": embed(d["conv_w"])})
             for d in (w, m, v)]
    outs = [_unpack_small(t) for t in _adamw_small(parts_s, *packs)]
    for name, n in _SMALL:
        for dst, src in zip((grads, deltas, new_m, new_v), outs):
            if name == "conv_w":
                dst[name] = lax.dynamic_slice(src[name].reshape(CONV_K, CONV_DIM), (0, me * cshard), (CONV_K, cshard))
            else:
                dst[name] = src[name]

    def lead(d):
        return [d[name][None] for name in _WEIGHT_ORDER]

    return (loss, dx[None], *lead(grads), *lead(deltas), *lead(new_m), *lead(new_v))
```

```python
import numpy as np

import jax
import jax.numpy as jnp
from jax import lax
from jax.experimental import pallas as pl
from jax.experimental.pallas import tpu as pltpu

F32 = jnp.float32
BF16 = jnp.bfloat16
MXU_DTYPE = jnp.bfloat16
EPS = 1e-6
VMEM_LIMIT_BYTES = 48 * 1024 * 1024
K_TILE_MAX = 2048

N_DEV = 8
D_MODEL = 2048
Q_RANK = 512
KV_RANK = 512
ROPE = 64
HALF = ROPE // 2
HEADS = 8
NOPE = 128
VDIM = 128
QK = NOPE + ROPE
SSD_W = 1024
SSD_H = 16
SSD_P = 64
SSD_G = 2
SSD_E = SSD_H // SSD_G
SSD_N = 128
CHUNK = 128
CONV_K = 4
CONV_DIM = SSD_W + 2 * SSD_G * SSD_N
B_OFF = SSD_W
C_OFF = SSD_W + SSD_G * SSD_N
D_FF = 5632
D_IN = Q_RANK + KV_RANK + ROPE + SSD_W + CONV_DIM + SSD_H
ROPE_THETA = 10000.0
LANE = 128
HEAD_LANE = ROPE

ADAM_LR = 0.001
ADAM_B1 = 0.9
ADAM_B2 = 0.999
ADAM_EPS = 1e-08
ADAM_WD = 0.01
ADAM_STEP = 10


def _pick(n, cands):
    for c in cands:
        if n % c == 0:
            return c
    return n


def _params(*sem):
    return pltpu.CompilerParams(dimension_semantics=sem, vmem_limit_bytes=VMEM_LIMIT_BYTES)


def _sigmoid(x):
    return 1.0 / (1.0 + jnp.exp(-x))


def _silu(x):
    return x * _sigmoid(x)


def _dsilu(x):
    s = _sigmoid(x)
    return s * (1.0 + x * (1.0 - s))


def _softplus(x):
    e = jnp.exp(-jnp.abs(x))
    small = e * (1.0 - e * (0.5 - e * (1.0 / 3.0)))
    return jnp.maximum(x, 0.0) + jnp.where(e < 0.01, small, jnp.log(1.0 + e))


def _dot(a, b, ca, cb):
    return lax.dot_general(a, b, (((ca,), (cb,)), ((), ())), preferred_element_type=F32)


def _mx(v):
    return v.astype(MXU_DTYPE)


def _split3(a):
    hi = a.astype(BF16)
    r1 = a - hi.astype(F32)
    mid = r1.astype(BF16)
    lo = (r1 - mid.astype(F32)).astype(BF16)
    return hi, mid, lo


def _exact_dot(a, b, ca, cb, split_a):
    if split_a:
        return sum(_dot(p, b, ca, cb) for p in _split3(a))
    return sum(_dot(a, p, ca, cb) for p in _split3(b))


MM_ROW_GROUPS = 4


def _row_slices(tm, align):
    ng = MM_ROW_GROUPS
    while ng > 1 and (tm % ng or (tm // ng) % align):
        ng //= 2
    return [slice(g * (tm // ng), (g + 1) * (tm // ng)) for g in range(ng)]


def _mm(a, b, mode, *, a_blk=False, b_blk=False, out_blk=False, a_cols=None, b_cols=None, add=None, out_dtype=F32,
        fuse=1, name="mm"):
    a2, b2 = a.shape[-2:], b.shape[-2:]
    a_last = a2[1] if a_cols is None else a_cols[1]
    a_start = 0 if a_cols is None else a_cols[0]
    b_start = 0
    if b_cols is not None:
        assert mode != "nt"
        b_start, b2 = b_cols[0], (b2[0], b_cols[1])
    if mode == "nn":
        m, k, (k2, n) = a2[0], a_last, b2
    elif mode == "nt":
        m, k, (n, k2) = a2[0], a_last, b2
    else:
        k, m, (k2, n) = a2[0], a_last, b2
    assert k == k2, (a.shape, b.shape, mode)
    tm = _pick(m, (1024, 704, 512, 256, 128))
    tn = _pick(n, (1024, 768, 704, 512, 256, 192, 128))
    tk = k if k <= K_TILE_MAX else _pick(k, (K_TILE_MAX, 1024, 512))
    nk = k // tk
    jo = N_DEV if out_blk else 1
    reduce_blocks = a_blk and b_blk and not out_blk
    assert fuse == 1 or reduce_blocks
    jr = N_DEV // fuse if reduce_blocks else 1
    ca, cb = {"nn": (1, 0), "nt": (1, 1), "tn": (0, 0)}[mode]
    has_add = add is not None
    single = jr * nk == 1
    if mode == "tn":
        assert a_start % tm == 0
        a_block, a_idx = (tk, tm), (lambda i, kk: (kk, i + a_start // tm))
    else:
        assert a_start % tk == 0
        a_block, a_idx = (tm, tk), (lambda i, kk: (i, kk + a_start // tk))
    assert b_start % tn == 0
    b_block, b_idx = (((tn, tk), (lambda nn_, kk: (nn_, kk))) if mode == "nt"
                      else ((tk, tn), (lambda nn_, kk: (kk, nn_ + b_start // tn))))

    def blk_specs(blocked, block, idx, of_a, t):
        def pos(o, i, nn_, kk):
            return idx(i, kk) if of_a else idx(nn_, kk)
        if blocked:
            return pl.BlockSpec((None,) + block,
                                lambda o, i, nn_, r, kk: ((o if out_blk else r * fuse + t),) + pos(o, i, nn_, kk))
        return pl.BlockSpec(block, lambda o, i, nn_, r, kk: pos(o, i, nn_, kk))

    a_specs = [blk_specs(a_blk, a_block, a_idx, True, t) for t in range(fuse)]
    b_specs = [blk_specs(b_blk, b_block, b_idx, False, t) for t in range(fuse)]
    o_spec = (pl.BlockSpec((None, tm, tn), lambda o, i, nn_, r, kk: (o, i, nn_)) if out_blk
              else pl.BlockSpec((tm, tn), lambda o, i, nn_, r, kk: (i, nn_)))

    groups = _row_slices(tm, LANE if mode == "tn" else 16)

    def body(*refs):
        a_refs, b_refs = refs[:fuse], refs[fuse:2 * fuse]
        add_ref = refs[2 * fuse] if has_add else None
        o_ref = refs[2 * fuse + 1] if has_add else refs[2 * fuse]

        def partial(rs):
            out = None
            for t in range(fuse):
                av = a_refs[t][:, rs] if mode == "tn" else a_refs[t][rs, :]
                d = _dot(_mx(av), _mx(b_refs[t][...]), ca, cb)
                out = d if out is None else out + d
            return out

        if single:
            for rs in groups:
                res = partial(rs)
                if has_add:
                    res = res + add_ref[rs, :]
                o_ref[rs, :] = res.astype(o_ref.dtype)
            return
        acc = refs[-1]
        r, kk = pl.program_id(3), pl.program_id(4)

        @pl.when(jnp.logical_and(r == 0, kk == 0))
        def _():
            acc[...] = jnp.zeros_like(acc)

        for rs in groups:
            acc[rs, :] += partial(rs)

        @pl.when(jnp.logical_and(r == jr - 1, kk == nk - 1))
        def _():
            res = acc[...]
            if has_add:
                res = res + add_ref[...]
            o_ref[...] = res.astype(o_ref.dtype)

    out_shape = ((N_DEV, m, n) if out_blk else (m, n))
    return pl.pallas_call(
        body, name=name, grid=(jo, m // tm, n // tn, jr, nk),
        in_specs=a_specs + b_specs + ([o_spec] if has_add else []), out_specs=o_spec,
        out_shape=jax.ShapeDtypeStruct(out_shape, out_dtype),
        scratch_shapes=[] if single else [pltpu.VMEM((tm, tn), F32)],
        compiler_params=_params("parallel", "parallel", "parallel", "arbitrary", "arbitrary"),
    )(*((a,) * fuse + (b,) * fuse + ((add,) if has_add else ())))


def _row_tile(r_):
    return _pick(r_, (256, 128, 64, 32, 16, 8))


def _rms_fwd(t, w, groups=1, res=None, out_dtype=F32, name="rms_fwd"):
    r_, f = t.shape
    fg = f // groups
    tr = _row_tile(r_)
    has_res = res is not None

    def body(*refs):
        t_ref, w_ref = refs[0], refs[1]
        res_ref = refs[2] if has_res else None
        o_ref = refs[-1]
        for g in range(groups):
            sl = slice(g * fg, (g + 1) * fg)
            tv = t_ref[:, sl].astype(F32)
            r = lax.rsqrt(jnp.mean(tv * tv, axis=-1, keepdims=True) + EPS)
            y = tv * r * w_ref[:, sl]
            if has_res:
                y = y + res_ref[:, sl]
            o_ref[:, sl] = y.astype(o_ref.dtype)

    row = pl.BlockSpec((tr, f), lambda i: (i, 0))
    wsp = pl.BlockSpec((1, f), lambda i: (0, 0))
    return pl.pallas_call(
        body, name=name, grid=(r_ // tr,),
        in_specs=[row, wsp] + ([row] if has_res else []), out_specs=row,
        out_shape=jax.ShapeDtypeStruct((r_, f), out_dtype),
        compiler_params=_params("parallel"),
    )(*((t, w.reshape(1, f)) + ((res,) if has_res else ())))


def _rms_bwd(t, w, dys, res=None, out_dtype=F32, name="rms_bwd"):
    r_, f = t.shape
    groups = len(dys)
    fg = f // groups
    tr = _row_tile(r_)
    has_res = res is not None

    def body(*refs):
        t_ref, w_ref = refs[0], refs[1]
        dy_refs = refs[2:2 + groups]
        res_ref = refs[2 + groups] if has_res else None
        dt_ref, dw_ref = refs[-2], refs[-1]

        @pl.when(pl.program_id(0) == 0)
        def _():
            dw_ref[...] = jnp.zeros_like(dw_ref)

        for g in range(groups):
            sl = slice(g * fg, (g + 1) * fg)
            tv = t_ref[:, sl].astype(F32)
            dyv = dy_refs[g][...].astype(F32)
            r = lax.rsqrt(jnp.mean(tv * tv, axis=-1, keepdims=True) + EPS)
            gw = dyv * w_ref[:, sl]
            c = jnp.mean(gw * tv, axis=-1, keepdims=True)
            dt = r * gw - tv * (r * r * r * c)
            if has_res:
                dt = dt + res_ref[:, sl]
            dt_ref[:, sl] = dt.astype(dt_ref.dtype)
            dw_ref[:, sl] += jnp.sum(dyv * tv * r, axis=0, keepdims=True)

    row = pl.BlockSpec((tr, f), lambda i: (i, 0))
    grow = pl.BlockSpec((tr, fg), lambda i: (i, 0))
    wsp = pl.BlockSpec((1, f), lambda i: (0, 0))
    return pl.pallas_call(
        body, name=name, grid=(r_ // tr,),
        in_specs=[row, wsp] + [grow] * groups + ([row] if has_res else []), out_specs=[row, wsp],
        out_shape=[jax.ShapeDtypeStruct((r_, f), out_dtype), jax.ShapeDtypeStruct((1, f), F32)],
        compiler_params=_params("arbitrary"),
    )(*((t, w.reshape(1, f)) + tuple(dys) + ((res,) if has_res else ())))


def _norm_res_norm(t, res, w1, w2, name="post_mix_pre_ffn_norm"):
    r_, f = t.shape
    tr = _row_tile(r_)

    def body(t_ref, res_ref, w1_ref, w2_ref, h_ref, v_ref):
        tv = t_ref[...]
        h = res_ref[...] + tv * lax.rsqrt(jnp.mean(tv * tv, axis=-1, keepdims=True) + EPS) * w1_ref[...]
        h_ref[...] = h
        v_ref[...] = (h * lax.rsqrt(jnp.mean(h * h, axis=-1, keepdims=True) + EPS) * w2_ref[...]).astype(v_ref.dtype)

    row = pl.BlockSpec((tr, f), lambda i: (i, 0))
    wsp = pl.BlockSpec((1, f), lambda i: (0, 0))
    return pl.pallas_call(
        body, name=name, grid=(r_ // tr,), in_specs=[row, row, wsp, wsp], out_specs=[row, row],
        out_shape=[jax.ShapeDtypeStruct((r_, f), F32), jax.ShapeDtypeStruct((r_, f), MXU_DTYPE)],
        compiler_params=_params("parallel"),
    )(t, res, w1.reshape(1, f), w2.reshape(1, f))


def _norm_res_norm_bwd(h, w2, dv, dres, t, w1, name="pre_ffn_post_mix_norm_bwd"):
    r_, f = h.shape
    tr = _row_tile(r_)

    def body(h_ref, w2_ref, dv_ref, dres_ref, t_ref, w1_ref, dh_ref, dt_ref, dw2_ref, dw1_ref):
        @pl.when(pl.program_id(0) == 0)
        def _():
            dw2_ref[...] = jnp.zeros_like(dw2_ref)
            dw1_ref[...] = jnp.zeros_like(dw1_ref)

        def rms_bwd(tv, wv, dyv):
            r = lax.rsqrt(jnp.mean(tv * tv, axis=-1, keepdims=True) + EPS)
            gw = dyv * wv
            c = jnp.mean(gw * tv, axis=-1, keepdims=True)
            return r * gw - tv * (r * r * r * c), jnp.sum(dyv * tv * r, axis=0, keepdims=True)

        d1, g2 = rms_bwd(h_ref[...], w2_ref[...], dv_ref[...])
        dh = d1 + dres_ref[...]
        dh_ref[...] = dh
        dw2_ref[...] += g2
        d2, g1 = rms_bwd(t_ref[...], w1_ref[...], dh)
        dt_ref[...] = d2.astype(dt_ref.dtype)
        dw1_ref[...] += g1

    row = pl.BlockSpec((tr, f), lambda i: (i, 0))
    wsp = pl.BlockSpec((1, f), lambda i: (0, 0))
    return pl.pallas_call(
        body, name=name, grid=(r_ // tr,), in_specs=[row, wsp, row, row, row, wsp], out_specs=[row, row, wsp, wsp],
        out_shape=[jax.ShapeDtypeStruct((r_, f), F32), jax.ShapeDtypeStruct((r_, f), MXU_DTYPE),
                   jax.ShapeDtypeStruct((1, f), F32), jax.ShapeDtypeStruct((1, f), F32)],
        compiler_params=_params("arbitrary"),
    )(h, w2.reshape(1, f), dv, dres, t, w1.reshape(1, f))


def _hnorm_fwd(o, w, width, name="attn_out_norm"):
    h, s_, v = o.shape
    tr = _row_tile(s_)

    def body(o_ref, w_ref, y_ref):
        ss = jnp.sum(o_ref[0] * o_ref[0], axis=-1, keepdims=True)
        for i in range(1, h):
            ss = ss + jnp.sum(o_ref[i] * o_ref[i], axis=-1, keepdims=True)
        r = lax.rsqrt(ss * (1.0 / (h * v)) + EPS)
        for i in range(h):
            sl = slice(i * v, (i + 1) * v)
            y_ref[:, sl] = (o_ref[i] * r * w_ref[:, sl]).astype(y_ref.dtype)

    return pl.pallas_call(
        body, name=name, grid=(s_ // tr,),
        in_specs=[pl.BlockSpec((h, tr, v), lambda i: (0, i, 0)), pl.BlockSpec((1, h * v), lambda i: (0, 0))],
        out_specs=pl.BlockSpec((tr, h * v), lambda i: (i, 0)),
        out_shape=jax.ShapeDtypeStruct((s_, width), MXU_DTYPE), compiler_params=_params("parallel"),
    )(o, w)


def _hnorm_bwd(o, w, dy, name="attn_out_norm_bwd"):
    h, s_, v = o.shape
    tr = _row_tile(s_)

    def body(o_ref, w_ref, dy_ref, do_ref, delta_ref, dw_ref):
        @pl.when(pl.program_id(0) == 0)
        def _():
            dw_ref[...] = jnp.zeros_like(dw_ref)

        ss = jnp.zeros((tr, 1), F32)
        cc = jnp.zeros((tr, 1), F32)
        for i in range(h):
            sl = slice(i * v, (i + 1) * v)
            ov = o_ref[i]
            ss = ss + jnp.sum(ov * ov, axis=-1, keepdims=True)
            cc = cc + jnp.sum(dy_ref[:, sl] * w_ref[:, sl] * ov, axis=-1, keepdims=True)
        r = lax.rsqrt(ss * (1.0 / (h * v)) + EPS)
        c = cc * (1.0 / (h * v))
        for i in range(h):
            sl = slice(i * v, (i + 1) * v)
            ov = o_ref[i]
            dyv = dy_ref[:, sl]
            dov = r * dyv * w_ref[:, sl] - ov * (r * r * r * c)
            do_ref[i] = dov.astype(do_ref.dtype)
            delta_ref[i] = jnp.sum(dov * ov, axis=-1, keepdims=True)
            dw_ref[:, sl] += jnp.sum(dyv * ov * r, axis=0, keepdims=True)

    blk = pl.BlockSpec((h, tr, v), lambda i: (0, i, 0))
    wsp = pl.BlockSpec((1, h * v), lambda i: (0, 0))
    return pl.pallas_call(
        body, name=name, grid=(s_ // tr,),
        in_specs=[blk, wsp, pl.BlockSpec((tr, h * v), lambda i: (i, 0))],
        out_specs=[blk, pl.BlockSpec((h, tr, 1), lambda i: (0, i, 0)), wsp],
        out_shape=[jax.ShapeDtypeStruct(o.shape, MXU_DTYPE), jax.ShapeDtypeStruct((h, s_, 1), F32),
                   jax.ShapeDtypeStruct((1, h * v), F32)],
        compiler_params=_params("arbitrary"),
    )(o, w, dy)


def _loss_head(ffn, h1, target, w, name="loss_head"):
    r_, f = ffn.shape
    tr = _row_tile(r_)

    def body(ffn_ref, h1_ref, tg_ref, w_ref, loss_ref, dy_ref, dffn_ref, dw_ref):
        @pl.when(pl.program_id(0) == 0)
        def _():
            dw_ref[...] = jnp.zeros_like(dw_ref)
            loss_ref[...] = jnp.zeros_like(loss_ref)

        tv = ffn_ref[...]
        wv = w_ref[...]
        r = lax.rsqrt(jnp.mean(tv * tv, axis=-1, keepdims=True) + EPS)
        tn = tv * r
        e = h1_ref[...] + tn * wv - tg_ref[...]
        tot = jnp.sum(jnp.sum(e * e, axis=1, keepdims=True), axis=0, keepdims=True) * (0.5 / f)
        loss_ref[...] += tot + jnp.zeros_like(loss_ref)
        dyv = e * (1.0 / f)
        dy_ref[...] = dyv
        gw = dyv * wv
        c = jnp.mean(gw * tv, axis=-1, keepdims=True)
        dffn_ref[...] = (r * gw - tv * (r * r * r * c)).astype(dffn_ref.dtype)
        dw_ref[...] += jnp.sum(dyv * tn, axis=0, keepdims=True)

    row = pl.BlockSpec((tr, f), lambda i: (i, 0))
    wsp = pl.BlockSpec((1, f), lambda i: (0, 0))
    lsp = pl.BlockSpec((1, LANE), lambda i: (0, 0))
    return pl.pallas_call(
        body, name=name, grid=(r_ // tr,),
        in_specs=[row, row, row, wsp], out_specs=[lsp, row, row, wsp],
        out_shape=[jax.ShapeDtypeStruct((1, LANE), F32), jax.ShapeDtypeStruct((r_, f), F32),
                   jax.ShapeDtypeStruct((r_, f), MXU_DTYPE), jax.ShapeDtypeStruct((1, f), F32)],
        compiler_params=_params("arbitrary"),
    )(ffn, h1, target, w.reshape(1, f))


def _rot_matrix():
    p = np.zeros((ROPE, ROPE), np.float32)
    for i in range(HALF):
        p[i + HALF, i] = -1.0
        p[i, i + HALF] = 1.0
    return jnp.asarray(p, BF16)


def _rope_val(r, c2, s2, rot):
    return r * c2 + _exact_dot(r, rot, 1, 0, True) * s2


def _q_prep(q, cos2, sin2, scale, name):
    h, s_, _ = q.shape
    tr = _pick(s_, (1024, 512, 256, 128, 64, 32, 16, 8))

    def body(q_ref, c_ref, s_ref, rot_ref, o_ref):
        x = q_ref[...]
        o_ref[:, :NOPE] = (x[:, :NOPE] * scale).astype(o_ref.dtype)
        o_ref[:, NOPE:] = (_rope_val(x[:, NOPE:], c_ref[...], s_ref[...], rot_ref[...]) * scale).astype(o_ref.dtype)

    blk = pl.BlockSpec((None, tr, QK), lambda hh, i: (hh, i, 0))
    csp = pl.BlockSpec((tr, ROPE), lambda hh, i: (i, 0))
    return pl.pallas_call(
        body, name=name, grid=(h, s_ // tr),
        in_specs=[blk, csp, csp, pl.BlockSpec((ROPE, ROPE), lambda hh, i: (0, 0))], out_specs=blk,
        out_shape=jax.ShapeDtypeStruct(q.shape, MXU_DTYPE), compiler_params=_params("parallel", "parallel"),
    )(q, cos2, sin2, _rot_matrix())


def _q_up(qkvn, w_uq_t, cos2, sin2, scale, name="q_up"):
    s_ = qkvn.shape[0]
    h = w_uq_t.shape[0]
    tm = _pick(s_, (1024, 512, 256, 128))

    def body(a_ref, w_ref, c_ref, s_ref, rot_ref, o_ref):
        x = _dot(_mx(a_ref[...]), _mx(w_ref[...]), 1, 1)
        o_ref[:, :NOPE] = (x[:, :NOPE] * scale).astype(o_ref.dtype)
        o_ref[:, NOPE:] = (_rope_val(x[:, NOPE:], c_ref[...], s_ref[...], rot_ref[...]) * scale).astype(o_ref.dtype)

    csp = pl.BlockSpec((tm, ROPE), lambda j, i: (i, 0))
    return pl.pallas_call(
        body, name=name, grid=(h, s_ // tm),
        in_specs=[pl.BlockSpec((tm, Q_RANK), lambda j, i: (i, 0)), pl.BlockSpec((None, QK, Q_RANK), lambda j, i: (j, 0, 0)),
                  csp, csp, pl.BlockSpec((ROPE, ROPE), lambda j, i: (0, 0))],
        out_specs=pl.BlockSpec((None, tm, QK), lambda j, i: (j, i, 0)),
        out_shape=jax.ShapeDtypeStruct((h, s_, QK), MXU_DTYPE), compiler_params=_params("parallel", "parallel"),
    )(qkvn, w_uq_t, cos2, sin2, _rot_matrix())


def _kv_up(qkvn, w_ukv, small, cos2, sin2, name="kv_up"):
    s_ = qkvn.shape[0]
    h = w_ukv.shape[0]
    tm = _pick(s_, (1024, 512, 256, 128))

    def body(a_ref, w_ref, sm_ref, c_ref, s_ref, rot_ref, k_ref, v_ref):
        x = _dot(_mx(a_ref[...]), _mx(w_ref[...]), 1, 0)
        k_ref[:, :NOPE] = x[:, :NOPE].astype(k_ref.dtype)
        k_ref[:, NOPE:] = _rope_val(sm_ref[:, :ROPE], c_ref[...], s_ref[...], rot_ref[...]).astype(k_ref.dtype)
        v_ref[...] = x[:, NOPE:].astype(v_ref.dtype)

    csp = pl.BlockSpec((tm, ROPE), lambda j, i: (i, 0))
    return pl.pallas_call(
        body, name=name, grid=(h, s_ // tm),
        in_specs=[pl.BlockSpec((tm, KV_RANK), lambda j, i: (i, Q_RANK // KV_RANK)),
                  pl.BlockSpec((None, KV_RANK, NOPE + VDIM), lambda j, i: (j, 0, 0)),
                  pl.BlockSpec((tm, LANE), lambda j, i: (i, 0)), csp, csp, pl.BlockSpec((ROPE, ROPE), lambda j, i: (0, 0))],
        out_specs=[pl.BlockSpec((None, tm, QK), lambda j, i: (j, i, 0)), pl.BlockSpec((None, tm, VDIM), lambda j, i: (j, i, 0))],
        out_shape=[jax.ShapeDtypeStruct((h, s_, QK), MXU_DTYPE), jax.ShapeDtypeStruct((h, s_, VDIM), MXU_DTYPE)],
        compiler_params=_params("parallel", "parallel"),
    )(qkvn, w_ukv, small, cos2, sin2, _rot_matrix())


def _dkv_post(dk, dv, ddt, cos2, nsin2, name="dkv_post"):
    h, s_, _ = dk.shape
    tr = _row_tile(s_)

    def body(dk_ref, dv_ref, ddt_ref, c_ref, s_ref, rot_ref, dkv_ref, dsm_ref):
        acc = dk_ref[0, :, NOPE:]
        for i in range(1, h):
            acc = acc + dk_ref[i, :, NOPE:]
        dsm_ref[:, :ROPE] = _rope_val(acc, c_ref[...], s_ref[...], rot_ref[...]).astype(dsm_ref.dtype)
        dsm_ref[:, ROPE:] = ddt_ref[:, ROPE:].astype(dsm_ref.dtype)
        for i in range(h):
            dkv_ref[i, :, :NOPE] = dk_ref[i, :, :NOPE].astype(dkv_ref.dtype)
            dkv_ref[i, :, NOPE:] = dv_ref[i].astype(dkv_ref.dtype)

    csp = pl.BlockSpec((tr, ROPE), lambda i: (i, 0))
    return pl.pallas_call(
        body, name=name, grid=(s_ // tr,),
        in_specs=[pl.BlockSpec((h, tr, QK), lambda i: (0, i, 0)), pl.BlockSpec((h, tr, VDIM), lambda i: (0, i, 0)),
                  pl.BlockSpec((tr, LANE), lambda i: (i, 0)), csp, csp, pl.BlockSpec((ROPE, ROPE), lambda i: (0, 0))],
        out_specs=[pl.BlockSpec((h, tr, NOPE + VDIM), lambda i: (0, i, 0)), pl.BlockSpec((tr, LANE), lambda i: (i, 0))],
        out_shape=[jax.ShapeDtypeStruct((h, s_, NOPE + VDIM), MXU_DTYPE), jax.ShapeDtypeStruct((s_, LANE), MXU_DTYPE)],
        compiler_params=_params("parallel"),
    )(dk, dv, ddt, cos2, nsin2, _rot_matrix())


def _attn_tile(s):
    return 2048 if s % 4096 == 0 else s // 2


def _pairs(n, by_key):
    if by_key:
        pr = [(i, j) for j in range(n) for i in range(j, n)]
    else:
        pr = [(i, j) for i in range(n) for j in range(i + 1)]
    return (jnp.asarray([p[0] for p in pr], jnp.int32), jnp.asarray([p[1] for p in pr], jnp.int32))


ATTN_ROW_GROUPS = 8


def _row_groups(t, diag):
    tg = t // ATTN_ROW_GROUPS
    out = []
    for r in range(ATTN_ROW_GROUPS):
        nc = (r + 1) * tg if diag else t
        mask = None
        if diag:
            mask = (lax.broadcasted_iota(jnp.int32, (tg, nc), 1)
                    <= lax.broadcasted_iota(jnp.int32, (tg, nc), 0) + r * tg)
        out.append((slice(r * tg, (r + 1) * tg), nc, mask))
    return out


def _flash_specs(t, dk, dv):
    qsp = pl.BlockSpec((None, t, dk), lambda hh, p, qi, kj: (hh, qi[p], 0))
    ksp = pl.BlockSpec((None, t, dk), lambda hh, p, qi, kj: (hh, kj[p], 0))
    vsp = pl.BlockSpec((None, t, dv), lambda hh, p, qi, kj: (hh, kj[p], 0))
    osp = pl.BlockSpec((None, t, dv), lambda hh, p, qi, kj: (hh, qi[p], 0))
    lsp = pl.BlockSpec((None, t, 1), lambda hh, p, qi, kj: (hh, qi[p], 0))
    return qsp, ksp, vsp, osp, lsp


def _flash_fwd(q, k, v, name="flash_fwd"):
    h, s_, dk = q.shape
    dv = v.shape[-1]
    t = _attn_tile(s_)
    n = s_ // t
    qi, kj = _pairs(n, False)

    def body(qi_ref, kj_ref, q_ref, k_ref, v_ref, o_ref, lse_ref, m_s, l_s, acc):
        p_ = pl.program_id(1)
        i, j = qi_ref[p_], kj_ref[p_]

        @pl.when(j == 0)
        def _():
            m_s[...] = jnp.full_like(m_s, -jnp.inf)
            l_s[...] = jnp.zeros_like(l_s)
            acc[...] = jnp.zeros_like(acc)

        def update(diag):
            for rs, nc, mask in _row_groups(t, diag):
                sc = _dot(q_ref[rs, :], k_ref[0:nc, :], 1, 1)
                if mask is not None:
                    sc = jnp.where(mask, sc, -jnp.inf)
                m_old = m_s[rs, :]
                m_new = jnp.maximum(m_old, jnp.max(sc, axis=1, keepdims=True))
                alpha = jnp.exp(m_old - m_new)
                p = jnp.exp(sc - m_new)
                l_s[rs, :] = alpha * l_s[rs, :] + jnp.sum(p, axis=1, keepdims=True)
                acc[rs, :] = alpha * acc[rs, :] + _dot(_mx(p), v_ref[0:nc, :], 1, 0)
                m_s[rs, :] = m_new

        @pl.when(j < i)
        def _():
            update(False)

        @pl.when(j == i)
        def _():
            update(True)
            o_ref[...] = acc[...] / l_s[...]
            lse_ref[...] = m_s[...] + jnp.log(l_s[...])

    qsp, ksp, vsp, osp, lsp = _flash_specs(t, dk, dv)
    gs = pltpu.PrefetchScalarGridSpec(
        num_scalar_prefetch=2, grid=(h, qi.shape[0]), in_specs=[qsp, ksp, vsp], out_specs=[osp, lsp],
        scratch_shapes=[pltpu.VMEM((t, 1), F32), pltpu.VMEM((t, 1), F32), pltpu.VMEM((t, dv), F32)])
    return pl.pallas_call(
        body, name=name, grid_spec=gs,
        out_shape=[jax.ShapeDtypeStruct((h, s_, dv), F32), jax.ShapeDtypeStruct((h, s_, 1), F32)],
        compiler_params=_params("parallel", "arbitrary"),
    )(qi, kj, q, k, v)


def _flash_bwd(q, k, v, do, lse, delta, name="flash_bwd"):
    h, s_, dk = q.shape
    dv = v.shape[-1]
    t = _attn_tile(s_)
    tg = t // ATTN_ROW_GROUPS
    n = s_ // t
    qi, kj = _pairs(n, True)

    def body(qi_ref, kj_ref, q_ref, k_ref, v_ref, do_ref, lse_ref, delta_ref, dq_ref, dk_ref, dv_ref, dk_acc, dv_acc):
        p_ = pl.program_id(1)
        i, j = qi_ref[p_], kj_ref[p_]

        @pl.when(p_ == 0)
        def _():
            dq_ref[...] = jnp.zeros_like(dq_ref)

        def update(diag):
            for g, (rs, nc, mask) in enumerate(_row_groups(t, diag)):
                sc = _dot(q_ref[rs, :], k_ref[0:nc, :], 1, 1)
                if mask is not None:
                    sc = jnp.where(mask, sc, -jnp.inf)
                p = jnp.exp(sc - lse_ref[rs, :])
                dob = _mx(do_ref[rs, :])
                dv_acc[0:nc, :] += _dot(_mx(p), dob, 0, 0)
                dp = _dot(dob, v_ref[0:nc, :], 1, 1)
                dsb = _mx(p * (dp - delta_ref[rs, :]))
                dk_acc[0:nc, :] += _dot(dsb, q_ref[rs, :], 0, 0)
                rows = pl.ds(pl.multiple_of(i * t + g * tg, tg), tg)
                dq_ref[rows, :] += _dot(dsb, k_ref[0:nc, :], 1, 0)

        @pl.when(i == j)
        def _():
            dk_acc[...] = jnp.zeros_like(dk_acc)
            dv_acc[...] = jnp.zeros_like(dv_acc)
            update(True)

        @pl.when(i > j)
        def _():
            update(False)

        @pl.when(i == n - 1)
        def _():
            dk_ref[...] = dk_acc[...]
            dv_ref[...] = dv_acc[...]

    qsp, ksp, vsp, osp, lsp = _flash_specs(t, dk, dv)
    dqsp = pl.BlockSpec((None, s_, dk), lambda hh, p, qi, kj: (hh, 0, 0))
    gs = pltpu.PrefetchScalarGridSpec(
        num_scalar_prefetch=2, grid=(h, qi.shape[0]), in_specs=[qsp, ksp, vsp, osp, lsp, lsp],
        out_specs=[dqsp, ksp, vsp],
        scratch_shapes=[pltpu.VMEM((t, dk), F32), pltpu.VMEM((t, dv), F32)])
    return pl.pallas_call(
        body, name=name, grid_spec=gs,
        out_shape=[jax.ShapeDtypeStruct((h, s_, dk), F32), jax.ShapeDtypeStruct((h, s_, dk), F32),
                   jax.ShapeDtypeStruct((h, s_, dv), F32)],
        compiler_params=_params("parallel", "arbitrary"),
    )(qi, kj, q, k, v, do, lse, delta)


HALO = 8


def _conv_specs(s_, c, tr, after):
    main = pl.BlockSpec((tr, c), lambda i: (i, 0))
    per = tr // HALO
    if after:
        halo = pl.BlockSpec((HALO, c), lambda i: (jnp.minimum((i + 1) * per, s_ // HALO - 1), 0))
    else:
        halo = pl.BlockSpec((HALO, c), lambda i: (jnp.maximum(i * per - 1, 0), 0))
    return main, halo


def _fill_before(ext, t_ref, h_ref, tr):
    ext[0:HALO, :] = jnp.where(pl.program_id(0) > 0, h_ref[...], 0.0)
    ext[HALO:HALO + tr, :] = t_ref[...]


def _taps(ext, w_ref, tr):
    base = HALO - (CONV_K - 1)
    acc = ext[base:base + tr, :] * w_ref[0:1, :]
    for k in range(1, CONV_K):
        acc = acc + ext[base + k:base + k + tr, :] * w_ref[k:k + 1, :]
    return acc


def _conv_fwd(t, w, b, name="conv_fwd"):
    s_, c = t.shape
    tr = _row_tile(s_)

    def body(t_ref, h_ref, w_ref, b_ref, o_ref, ext):
        _fill_before(ext, t_ref, h_ref, tr)
        o_ref[...] = _silu(_taps(ext, w_ref, tr) + b_ref[...])

    main, halo = _conv_specs(s_, c, tr, False)
    return pl.pallas_call(
        body, name=name, grid=(s_ // tr,),
        in_specs=[main, halo, pl.BlockSpec((CONV_K, c), lambda i: (0, 0)), pl.BlockSpec((1, c), lambda i: (0, 0))],
        out_specs=main, out_shape=jax.ShapeDtypeStruct((s_, c), F32),
        scratch_shapes=[pltpu.VMEM((tr + HALO, c), F32)], compiler_params=_params("parallel"),
    )(t, t, w, b)


def _conv_bwd_pre(t, w, b, dact, name="conv_bwd_pre"):
    s_, c = t.shape
    tr = _row_tile(s_)

    def body(t_ref, h_ref, w_ref, b_ref, da_ref, dpre_ref, dwb_ref, ext):
        @pl.when(pl.program_id(0) == 0)
        def _():
            dwb_ref[...] = jnp.zeros_like(dwb_ref)

        _fill_before(ext, t_ref, h_ref, tr)
        dpre = da_ref[...] * _dsilu(_taps(ext, w_ref, tr) + b_ref[...])
        dpre_ref[...] = dpre
        base = HALO - (CONV_K - 1)
        for k in range(CONV_K):
            dwb_ref[k:k + 1, :] += jnp.sum(dpre * ext[base + k:base + k + tr, :], axis=0, keepdims=True)
        dwb_ref[CONV_K:CONV_K + 1, :] += jnp.sum(dpre, axis=0, keepdims=True)

    main, halo = _conv_specs(s_, c, tr, False)
    return pl.pallas_call(
        body, name=name, grid=(s_ // tr,),
        in_specs=[main, halo, pl.BlockSpec((CONV_K, c), lambda i: (0, 0)), pl.BlockSpec((1, c), lambda i: (0, 0)), main],
        out_specs=[main, pl.BlockSpec((8, c), lambda i: (0, 0))],
        out_shape=[jax.ShapeDtypeStruct((s_, c), F32), jax.ShapeDtypeStruct((8, c), F32)],
        scratch_shapes=[pltpu.VMEM((tr + HALO, c), F32)], compiler_params=_params("arbitrary"),
    )(t, t, w, b, dact)


def _conv_bwd_in(dpre, w, name="conv_bwd_in"):
    s_, c = dpre.shape
    tr = _row_tile(s_)
    nt = s_ // tr

    def body(d_ref, h_ref, w_ref, o_ref, ext):
        ext[0:tr, :] = d_ref[...]
        ext[tr:tr + HALO, :] = jnp.where(pl.program_id(0) < nt - 1, h_ref[...], 0.0)
        acc = ext[CONV_K - 1:CONV_K - 1 + tr, :] * w_ref[0:1, :]
        for k in range(1, CONV_K):
            acc = acc + ext[CONV_K - 1 - k:CONV_K - 1 - k + tr, :] * w_ref[k:k + 1, :]
        o_ref[...] = acc.astype(o_ref.dtype)

    main, halo = _conv_specs(s_, c, tr, True)
    return pl.pallas_call(
        body, name=name, grid=(nt,),
        in_specs=[main, halo, pl.BlockSpec((CONV_K, c), lambda i: (0, 0))],
        out_specs=main, out_shape=jax.ShapeDtypeStruct((s_, c), MXU_DTYPE),
        scratch_shapes=[pltpu.VMEM((tr + HALO, c), F32)], compiler_params=_params("parallel"),
    )(dpre, dpre, w)


def _ssd_chunk_common(dt_ref, dtt_ref, br_ref, bc_ref, ar_ref, ac_ref):
    li = lax.broadcasted_iota(jnp.int32, (CHUNK, CHUNK), 0)
    si = lax.broadcasted_iota(jnp.int32, (CHUNK, CHUNK), 1)
    lower = li >= si
    lower_b = lower.astype(BF16)
    upper_b = (li <= si).astype(BF16)
    zr = dt_ref[...] + br_ref[...]
    dtc = _softplus(zr)
    a_row = -jnp.exp(ar_ref[...])
    acum = _exact_dot(lower_b, dtc * a_row, 1, 0, False)
    dtt = _softplus(dtt_ref[...] + bc_ref[...])
    acum_t = _exact_dot(dtt * (-jnp.exp(ac_ref[...])), upper_b, 1, 0, True)
    return lower, upper_b, zr, dtc, a_row, acum, acum_t


def _head_terms(h, lower, dtc, acum, acum_t):
    lane = lax.broadcasted_iota(jnp.int32, (1, LANE), 1)
    sub = lax.broadcasted_iota(jnp.int32, (SSD_H, 1), 0)
    rowid = lax.broadcasted_iota(jnp.int32, (CHUNK, 1), 0)
    oh = (lane == HEAD_LANE + h).astype(F32)
    acol = jnp.sum(acum * oh, axis=1, keepdims=True)
    dcol = jnp.sum(dtc * oh, axis=1, keepdims=True)
    arow = jnp.sum(acum_t * (sub == h).astype(F32), axis=0, keepdims=True)
    alast = jnp.sum(jnp.where(rowid == CHUNK - 1, acol, 0.0), axis=0, keepdims=True)
    decay = jnp.exp(jnp.where(lower, acol - arow, -jnp.inf))
    return oh, acol, dcol, alast, decay


SSD_PAIRS = SSD_H // 2
PAIRS_PER_GROUP = SSD_E // 2


def _ps(q):
    return slice(q * LANE, (q + 1) * LANE)


def _gs(off, g):
    return slice(off + g * SSD_N, off + (g + 1) * SSD_N)


def _lanes(c0, c1):
    return jnp.where(lax.broadcasted_iota(jnp.int32, (1, LANE), 1) < SSD_P, c0, c1)


def _rows(c0, c1):
    return jnp.where(lax.broadcasted_iota(jnp.int32, (LANE, 1), 0) < SSD_P, c0, c1)


def _lane_halves(t):
    first = lax.broadcasted_iota(jnp.int32, (1, LANE), 1) < SSD_P
    return (jnp.sum(jnp.where(first, t, 0.0), axis=1, keepdims=True),
            jnp.sum(jnp.where(first, 0.0, t), axis=1, keepdims=True))


def _ssd_in_specs(rev):
    def ci(c):
        return c if rev is None else rev - c
    return [pl.BlockSpec((CHUNK, CONV_DIM), lambda c: (ci(c), 0)),
            pl.BlockSpec((CHUNK, LANE), lambda c: (ci(c), 0)),
            pl.BlockSpec((SSD_H, CHUNK), lambda c: (0, ci(c))),
            pl.BlockSpec((1, LANE), lambda c: (0, 0)), pl.BlockSpec((SSD_H, 1), lambda c: (0, 0)),
            pl.BlockSpec((1, LANE), lambda c: (0, 0)), pl.BlockSpec((SSD_H, 1), lambda c: (0, 0)),
            pl.BlockSpec((SSD_PAIRS, 1, LANE), lambda c: (0, 0, 0))]


def _ssd_fwd(xbc, small, dtt, bias_r, bias_c, alog_r, alog_c, dsk, name="ssd_fwd"):
    s_ = xbc.shape[0]
    nc = s_ // CHUNK

    def body(x_ref, dt_ref, dtt_ref, br_ref, bc_ref, ar_ref, ac_ref, dsk_ref, y_ref, prev_ref, state):
        @pl.when(pl.program_id(0) == 0)
        def _():
            state[...] = jnp.zeros_like(state)

        lower, _, _, dtc, _, acum, acum_t = _ssd_chunk_common(dt_ref, dtt_ref, br_ref, bc_ref, ar_ref, ac_ref)
        for g in range(SSD_G):
            bb = _mx(x_ref[:, _gs(B_OFF, g)])
            cb_ = _mx(x_ref[:, _gs(C_OFF, g)])
            cbm = _dot(cb_, bb, 1, 1)
            for e in range(PAIRS_PER_GROUP):
                q = g * PAIRS_PER_GROUP + e
                _, acol0, dcol0, alast0, decay0 = _head_terms(2 * q, lower, dtc, acum, acum_t)
                _, acol1, dcol1, alast1, decay1 = _head_terms(2 * q + 1, lower, dtc, acum, acum_t)
                x = x_ref[:, _ps(q)]
                xdt = x * _lanes(dcol0, dcol1)
                xb = _mx(xdt)
                yd = _lanes(_dot(_mx(cbm * decay0), xb, 1, 0), _dot(_mx(cbm * decay1), xb, 1, 0))
                prev = state[q]
                prev_ref[0, q] = prev
                yo = _dot(cb_, _mx(prev), 1, 1) * _lanes(jnp.exp(acol0), jnp.exp(acol1))
                ds = _lanes(jnp.exp(alast0 - acol0), jnp.exp(alast1 - acol1))
                st = _dot(_mx(xdt * ds), bb, 0, 0)
                state[q] = prev * _rows(jnp.exp(alast0), jnp.exp(alast1)) + st
                y_ref[:, _ps(q)] = yd + yo + x * dsk_ref[q]

    psp = pl.BlockSpec((1, SSD_PAIRS, LANE, SSD_N), lambda c: (c, 0, 0, 0))
    return pl.pallas_call(
        body, name=name, grid=(nc,),
        in_specs=_ssd_in_specs(None), out_specs=[pl.BlockSpec((CHUNK, SSD_W), lambda c: (c, 0)), psp],
        out_shape=[jax.ShapeDtypeStruct((s_, SSD_W), F32),
                   jax.ShapeDtypeStruct((nc, SSD_PAIRS, LANE, SSD_N), F32)],
        scratch_shapes=[pltpu.VMEM((SSD_PAIRS, LANE, SSD_N), F32)],
        compiler_params=_params("arbitrary"),
    )(xbc, small, dtt, bias_r, bias_c, alog_r, alog_c, dsk)


def _ssd_bwd(xbc, small, dtt, bias_r, bias_c, alog_r, alog_c, dsk, prev, dy, name="ssd_bwd"):
    s_ = xbc.shape[0]
    nc = s_ // CHUNK

    def body(x_ref, dt_ref, dtt_ref, br_ref, bc_ref, ar_ref, ac_ref, dsk_ref, prev_ref, dy_ref,
             dx_ref, ddt_ref, dpar_ref, dstate):
        @pl.when(pl.program_id(0) == 0)
        def _():
            dstate[...] = jnp.zeros_like(dstate)
            dpar_ref[...] = jnp.zeros_like(dpar_ref)

        lower, upper_b, zr, dtc, a_row, acum, acum_t = _ssd_chunk_common(
            dt_ref, dtt_ref, br_ref, bc_ref, ar_ref, ac_ref)
        strict = (lax.broadcasted_iota(jnp.int32, (CHUNK, CHUNK), 1)
                  < lax.broadcasted_iota(jnp.int32, (CHUNK, CHUNK), 0))
        strict_b = strict.astype(BF16)
        col2 = lax.broadcasted_iota(jnp.int32, (CHUNK, 2 * CHUNK), 1)
        strict2 = (jnp.where(col2 >= CHUNK, col2 - CHUNK, col2)
                   < lax.broadcasted_iota(jnp.int32, (CHUNK, 2 * CHUNK), 0))
        da_in = jnp.zeros((CHUNK, LANE), F32)
        r_off = jnp.zeros((CHUNK, LANE), F32)
        c_int = jnp.zeros((CHUNK, LANE), F32)
        c_row = jnp.zeros((1, LANE), F32)
        ddt = jnp.zeros((CHUNK, LANE), F32)
        dskip = jnp.zeros((1, LANE), F32)
        for g in range(SSD_G):
            bb = _mx(x_ref[:, _gs(B_OFF, g)])
            cb_ = _mx(x_ref[:, _gs(C_OFF, g)])
            cbm = _dot(cb_, bb, 1, 1)
            dcb = jnp.zeros((CHUNK, CHUNK), F32)
            dc_acc = jnp.zeros((CHUNK, SSD_N), F32)
            db_acc = jnp.zeros((CHUNK, SSD_N), F32)
            for e in range(PAIRS_PER_GROUP):
                q = g * PAIRS_PER_GROUP + e
                oh0, acol0, dcol0, alast0, decay0 = _head_terms(2 * q, lower, dtc, acum, acum_t)
                oh1, acol1, dcol1, alast1, decay1 = _head_terms(2 * q + 1, lower, dtc, acum, acum_t)
                x = x_ref[:, _ps(q)]
                dy = dy_ref[:, _ps(q)]
                dcol = _lanes(dcol0, dcol1)
                xdt = x * dcol
                xb = _mx(xdt)
                eacol = _lanes(jnp.exp(acol0), jnp.exp(acol1))
                ds = _lanes(jnp.exp(alast0 - acol0), jnp.exp(alast1 - acol1))
                ealast = _rows(jnp.exp(alast0), jnp.exp(alast1))
                dyb = _mx(dy)
                dyb0, dyb1 = _mx(_lanes(dy, 0.0)), _mx(_lanes(0.0, dy))
                dsh = dstate[q]
                dshb = _mx(dsh)
                prev = prev_ref[0, q]
                prevb = _mx(prev)
                dxdt_inter = ds * _dot(bb, dshb, 1, 1)
                dxdt = _lanes(_dot(_mx(cbm * decay0), dyb, 0, 0), _dot(_mx(cbm * decay1), dyb, 0, 0)) + dxdt_inter
                dwl0 = _dot(dyb0, xb, 1, 1) * decay0
                dwl1 = _dot(dyb1, xb, 1, 1) * decay1
                dcb = dcb + dwl0 + dwl1
                dyeb = _mx(dy * eacol)
                dc_acc = dc_acc + _dot(dyeb, prevb, 1, 0)
                db_acc = db_acc + _dot(_mx(xdt * ds), dshb, 1, 0)
                dstate[q] = _dot(dyeb, cb_, 0, 0) + ealast * dsh
                above = _exact_dot(upper_b, jnp.concatenate([dwl0 * cbm, dwl1 * cbm], axis=1), 1, 0, False)
                above = jnp.where(strict2, above, 0.0)
                da_in = (da_in + jnp.sum(above[:, :CHUNK], axis=1, keepdims=True) * oh0
                         + jnp.sum(above[:, CHUNK:], axis=1, keepdims=True) * oh1)
                y_off = _dot(cb_, prevb, 1, 1) * eacol
                r0, r1 = _lane_halves(dy * y_off)
                r_off = r_off + r0 * oh0 + r1 * oh1
                c0, c1 = _lane_halves(xdt * dxdt_inter)
                c_int = c_int + c0 * oh0 + c1 * oh1
                both = jnp.sum(dsh * prev, axis=1, keepdims=True) * ealast
                c_row = (c_row + jnp.sum(_rows(both, 0.0), axis=0, keepdims=True) * oh0
                         + jnp.sum(_rows(0.0, both), axis=0, keepdims=True) * oh1)
                t0, t1 = _lane_halves(dxdt * x)
                ddt = ddt + t0 * oh0 + t1 * oh1
                dx_ref[:, _ps(q)] = dxdt * dcol + dy * dsk_ref[q]
                k0, k1 = _lane_halves(dy * x)
                dskip = (dskip + jnp.sum(k0, axis=0, keepdims=True) * oh0 + jnp.sum(k1, axis=0, keepdims=True) * oh1)
            dcbb = _mx(dcb)
            dx_ref[:, _gs(C_OFF, g)] = dc_acc + _dot(dcbb, bb, 1, 0)
            dx_ref[:, _gs(B_OFF, g)] = db_acc + _dot(dcbb, cb_, 0, 0)
        da = (da_in + _exact_dot(upper_b, r_off, 1, 0, False) + _exact_dot(strict_b, c_int, 1, 0, False) + c_row)
        draw = (ddt + da * a_row) * _sigmoid(zr)
        ddt_ref[...] = draw
        dpar_ref[0:1, :] += jnp.sum(draw, axis=0, keepdims=True)
        dpar_ref[1:2, :] += jnp.sum(da * dtc, axis=0, keepdims=True) * a_row
        dpar_ref[2:3, :] += dskip

    rev = nc - 1
    psp = pl.BlockSpec((1, SSD_PAIRS, LANE, SSD_N), lambda c: (rev - c, 0, 0, 0))
    return pl.pallas_call(
        body, name=name, grid=(nc,),
        in_specs=_ssd_in_specs(rev) + [psp, pl.BlockSpec((CHUNK, SSD_W), lambda c: (rev - c, 0))],
        out_specs=[pl.BlockSpec((CHUNK, CONV_DIM), lambda c: (rev - c, 0)),
                   pl.BlockSpec((CHUNK, LANE), lambda c: (rev - c, 0)), pl.BlockSpec((8, LANE), lambda c: (0, 0))],
        out_shape=[jax.ShapeDtypeStruct((s_, CONV_DIM), F32), jax.ShapeDtypeStruct((s_, LANE), F32),
                   jax.ShapeDtypeStruct((8, LANE), F32)],
        scratch_shapes=[pltpu.VMEM((SSD_PAIRS, LANE, SSD_N), F32)],
        compiler_params=_params("arbitrary"),
    )(xbc, small, dtt, bias_r, bias_c, alog_r, alog_c, dsk, prev, dy)


GN = SSD_W // SSD_G


def _gated_norm_fwd(y, z, w, cat, name="gated_norm_fwd"):
    s_, f = y.shape
    tr = _row_tile(s_)

    def body(y_ref, z_ref, w_ref, cat_ref, o_ref):
        for g in range(SSD_G):
            sl = slice(g * GN, (g + 1) * GN)
            gg = y_ref[:, sl] * _silu(z_ref[:, sl])
            r = lax.rsqrt(jnp.mean(gg * gg, axis=-1, keepdims=True) + EPS)
            o_ref[:, sl] = (gg * r * w_ref[:, sl]).astype(o_ref.dtype)

    row = pl.BlockSpec((tr, f), lambda i: (i, 0))
    wsp = pl.BlockSpec((1, f), lambda i: (0, 0))
    return pl.pallas_call(
        body, name=name, grid=(s_ // tr,),
        in_specs=[row, row, wsp, pl.BlockSpec(memory_space=pl.ANY)], out_specs=pl.BlockSpec((tr, f), lambda i: (i, 1)),
        out_shape=jax.ShapeDtypeStruct(cat.shape, cat.dtype), input_output_aliases={3: 0},
        compiler_params=_params("parallel"),
    )(y, z, w.reshape(1, f), cat)


def _gated_norm_bwd(y, z, w, dout, name="gated_norm_bwd"):
    s_, f = y.shape
    tr = _row_tile(s_)

    def body(y_ref, z_ref, w_ref, do_ref, dy_ref, dz_ref, dw_ref):
        @pl.when(pl.program_id(0) == 0)
        def _():
            dw_ref[...] = jnp.zeros_like(dw_ref)

        for g in range(SSD_G):
            sl = slice(g * GN, (g + 1) * GN)
            yv = y_ref[:, sl]
            zv = z_ref[:, sl]
            dov = do_ref[:, sl].astype(F32)
            sz = _silu(zv)
            gg = yv * sz
            r = lax.rsqrt(jnp.mean(gg * gg, axis=-1, keepdims=True) + EPS)
            gw = dov * w_ref[:, sl]
            c = jnp.mean(gw * gg, axis=-1, keepdims=True)
            dgg = r * gw - gg * (r * r * r * c)
            dy_ref[:, sl] = dgg * sz
            dz_ref[:, sl] = (dgg * yv * _dsilu(zv)).astype(dz_ref.dtype)
            dw_ref[:, sl] += jnp.sum(dov * gg * r, axis=0, keepdims=True)

    row = pl.BlockSpec((tr, f), lambda i: (i, 0))
    wsp = pl.BlockSpec((1, f), lambda i: (0, 0))
    return pl.pallas_call(
        body, name=name, grid=(s_ // tr,),
        in_specs=[row, row, wsp, pl.BlockSpec((tr, f), lambda i: (i, 1))], out_specs=[row, row, wsp],
        out_shape=[jax.ShapeDtypeStruct((s_, f), F32), jax.ShapeDtypeStruct((s_, f), MXU_DTYPE),
                   jax.ShapeDtypeStruct((1, f), F32)],
        compiler_params=_params("arbitrary"),
    )(y, z, w.reshape(1, f), dout)


def _ffn_fwd(vv, w_gate, w_up, name="ffn_gate_up"):
    s_, d = vv.shape
    nb, f8, _ = w_gate.shape
    tm = _pick(s_, (1024, 512, 256, 128))

    def body(v_ref, wg_ref, wu_ref, g_ref, u_ref, a_ref):
        for rs in _row_slices(tm, 16):
            a = _mx(v_ref[rs, :])
            g = _dot(a, _mx(wg_ref[...]), 1, 1)
            u = _dot(a, _mx(wu_ref[...]), 1, 1)
            s = _sigmoid(g)
            gs = g * s
            g_ref[rs, :] = (u * (s * (1.0 + g * (1.0 - s)))).astype(g_ref.dtype)
            u_ref[rs, :] = gs.astype(u_ref.dtype)
            a_ref[rs, :] = (gs * u).astype(a_ref.dtype)

    wsp = pl.BlockSpec((None, f8, d), lambda j, i: (j, 0, 0))
    osp = pl.BlockSpec((None, tm, f8), lambda j, i: (j, i, 0))
    return pl.pallas_call(
        body, name=name, grid=(nb, s_ // tm),
        in_specs=[pl.BlockSpec((tm, d), lambda j, i: (i, 0)), wsp, wsp], out_specs=[osp] * 3,
        out_shape=[jax.ShapeDtypeStruct((nb, s_, f8), MXU_DTYPE)] * 3,
        compiler_params=_params("parallel", "parallel"),
    )(vv, w_gate, w_up)


def _ffn_bwd_act(dffn, w_down, gate, up, name="ffn_d_act"):
    s_, d = dffn.shape
    nb, f8, _ = w_down.shape
    tm = _pick(s_, (1024, 512, 256, 128))

    def body(d_ref, w_ref, g_ref, u_ref, dg_ref, du_ref):
        for rs in _row_slices(tm, 16):
            dact = _dot(_mx(d_ref[rs, :]), _mx(w_ref[...]), 1, 1)
            dg_ref[rs, :] = (dact * g_ref[rs, :].astype(F32)).astype(dg_ref.dtype)
            du_ref[rs, :] = (dact * u_ref[rs, :].astype(F32)).astype(du_ref.dtype)

    osp = pl.BlockSpec((None, tm, f8), lambda j, i: (j, i, 0))
    return pl.pallas_call(
        body, name=name, grid=(nb, s_ // tm),
        in_specs=[pl.BlockSpec((tm, d), lambda j, i: (i, 0)), pl.BlockSpec((None, f8, d), lambda j, i: (j, 0, 0)),
                  osp, osp],
        out_specs=[osp, osp], out_shape=[jax.ShapeDtypeStruct((nb, s_, f8), MXU_DTYPE)] * 2,
        compiler_params=_params("parallel", "parallel"),
    )(dffn, w_down, gate, up)


def _ffn_bwd_in(dgate, w_gate, dup, w_up, name="ffn_d_in"):
    nb, s_, f8 = dgate.shape
    d = w_gate.shape[2]
    tm = _pick(s_, (1024, 512, 256, 128))
    tn = _pick(d, (1024, 512, 256, 128))

    def body(dg_ref, wg_ref, du_ref, wu_ref, o_ref, acc):
        j = pl.program_id(2)

        @pl.when(j == 0)
        def _():
            acc[...] = jnp.zeros_like(acc)

        for rs in _row_slices(tm, 16):
            acc[rs, :] += (_dot(_mx(dg_ref[rs, :]), _mx(wg_ref[...]), 1, 0)
                           + _dot(_mx(du_ref[rs, :]), _mx(wu_ref[...]), 1, 0))

        @pl.when(j == nb - 1)
        def _():
            o_ref[...] = acc[...]

    asp = pl.BlockSpec((None, tm, f8), lambda i, n, j: (j, i, 0))
    wsp = pl.BlockSpec((None, f8, tn), lambda i, n, j: (j, 0, n))
    return pl.pallas_call(
        body, name=name, grid=(s_ // tm, d // tn, nb),
        in_specs=[asp, wsp, asp, wsp], out_specs=pl.BlockSpec((tm, tn), lambda i, n, j: (i, n)),
        out_shape=jax.ShapeDtypeStruct((s_, d), F32), scratch_shapes=[pltpu.VMEM((tm, tn), F32)],
        compiler_params=_params("parallel", "parallel", "arbitrary"),
    )(dgate, w_gate, dup, w_up)


def _adam_math(g, w, m, v):
    m2 = ADAM_B1 * m + (1.0 - ADAM_B1) * g
    v2 = ADAM_B2 * v + (1.0 - ADAM_B2) * (g * g)
    m_hat = m2 / (1.0 - ADAM_B1 ** ADAM_STEP)
    v_hat = v2 / (1.0 - ADAM_B2 ** ADAM_STEP)
    delta = -ADAM_LR * (m_hat / (jnp.sqrt(v_hat) + ADAM_EPS) + ADAM_WD * w)
    return delta, m2, v2


def _adamw(parts, w, m, v, name="adamw"):
    nd, r_, c = parts.shape
    tr = _pick(r_, (128, 64, 32, 16))
    tc = c
    if tr == r_ and r_ > 128:
        tc = _pick(c, (256, 128))

    def body(p_ref, w_ref, m_ref, v_ref, g_ref, d_ref, m2_ref, v2_ref):
        g = p_ref[0].astype(F32)
        for i in range(1, nd):
            g = g + p_ref[i].astype(F32)
        delta, m2, v2 = _adam_math(g, w_ref[...], m_ref[...], v_ref[...])
        g_ref[...] = g
        d_ref[...] = delta
        m2_ref[...] = m2
        v2_ref[...] = v2

    row = pl.BlockSpec((tr, tc), lambda i, j: (i, j))
    psp = pl.BlockSpec((nd, tr, tc), lambda i, j: (0, i, j))
    return pl.pallas_call(
        body, name=name, grid=(r_ // tr, c // tc), in_specs=[psp, row, row, row], out_specs=[row] * 4,
        out_shape=[jax.ShapeDtypeStruct((r_, c), F32)] * 4, compiler_params=_params("parallel", "parallel"),
    )(parts, w, m, v)


def _adamw_small(parts, w, m, v, name="adamw_small"):
    nd = parts.shape[0]

    def body(p_ref, w_ref, m_ref, v_ref, g_ref, d_ref, m2_ref, v2_ref):
        g = p_ref[0]
        for i in range(1, nd):
            g = g + p_ref[i]
        delta, m2, v2 = _adam_math(g, w_ref[...], m_ref[...], v_ref[...])
        g_ref[...] = g
        d_ref[...] = delta
        m2_ref[...] = m2
        v2_ref[...] = v2

    return pl.pallas_call(
        body, name=name, out_shape=[jax.ShapeDtypeStruct(w.shape, F32)] * 4,
        compiler_params=pltpu.CompilerParams(vmem_limit_bytes=VMEM_LIMIT_BYTES),
    )(parts, w, m, v)


_HBM = pl.BlockSpec(memory_space=pltpu.HBM)
_MESH = pl.DeviceIdType.MESH


def _all_gather(xs, name):
    na = len(xs)

    def body(*refs):
        x_refs, out_refs = refs[:na], refs[na:2 * na]
        send_sems, recv_sems, local_sems = refs[2 * na:]
        x, y, c = lax.axis_index("x"), lax.axis_index("y"), lax.axis_index("c")
        me, sibling = (x, y, c), (x, y, 1 - c)
        chips = [(1 - x, y), (x, 1 - y), (1 - x, 1 - y)]

        def slot(a, px, py, pc):
            return out_refs[a].at[4 * px + 2 * py + pc]

        def copy(a, k, block, to, src=None):
            return pltpu.make_async_remote_copy(
                src_ref=slot(a, *block) if src is None else src, dst_ref=slot(a, *block),
                send_sem=send_sems.at[a, k], recv_sem=recv_sems.at[a, k], device_id=to, device_id_type=_MESH)

        mine = [pltpu.make_async_copy(x_refs[a], slot(a, *me), local_sems.at[a]) for a in range(na)]
        started = []
        for a in range(na):
            mine[a].start()
            first = [copy(a, 0, me, sibling, src=x_refs[a])]
            first += [copy(a, 1 + j, me, (*chip, c), src=x_refs[a]) for j, chip in enumerate(chips)]
            for cp in first:
                cp.start()
            started += first
        for a in range(na):
            for j, chip in enumerate(chips):
                copy(a, 1 + j, (*chip, c), me).wait_recv()
                fwd = copy(a, 4 + j, (*chip, c), sibling)
                fwd.start()
                started.append(fwd)
        for a in range(na):
            copy(a, 0, sibling, me).wait_recv()
            for j, chip in enumerate(chips):
                copy(a, 4 + j, (*chip, 1 - c), me).wait_recv()
        for cp in started:
            cp.wait_send()
        for cp in mine:
            cp.wait()

    return pl.pallas_call(
        body, name=name, out_shape=[jax.ShapeDtypeStruct((N_DEV,) + t.shape, t.dtype) for t in xs],
        in_specs=[_HBM] * na, out_specs=[_HBM] * na,
        scratch_shapes=[pltpu.SemaphoreType.DMA((na, 7)), pltpu.SemaphoreType.DMA((na, 7)),
                        pltpu.SemaphoreType.DMA((na,))],
    )(*xs)


_SEM = pl.BlockSpec(memory_space=pltpu.SEMAPHORE)
_EFFECT = pltpu.SideEffectType.DATAFLOW_SIDE_EFFECTING


def _peers(x, y, c):
    out = []
    for k in range(1, N_DEV):
        px = 1 - x if k & 4 else x
        py = 1 - y if k & 2 else y
        pc = 1 - c if k & 1 else c
        out.append(((px, py, pc), 4 * px + 2 * py + pc))
    return out


def _push_copies(scatter, src_refs, land_refs, send_sems, recv_sems):
    x, y, c = lax.axis_index("x"), lax.axis_index("y"), lax.axis_index("c")
    me = 4 * x + 2 * y + c
    pairs = []
    for a, (src, land) in enumerate(zip(src_refs, land_refs)):
        for k, (peer, slot) in enumerate(_peers(x, y, c)):
            out_src = src.at[slot] if scatter else src
            si = a * (N_DEV - 1) + k
            send = pltpu.make_async_remote_copy(src_ref=out_src, dst_ref=land.at[me], send_sem=send_sems.at[si],
                                                recv_sem=recv_sems.at[si], device_id=peer, device_id_type=_MESH)
            recv = pltpu.make_async_remote_copy(src_ref=out_src, dst_ref=land.at[slot], send_sem=send_sems.at[si],
                                                recv_sem=recv_sems.at[si], device_id=peer, device_id_type=_MESH)
            pairs.append((send, recv))
    return pairs


def _push_start(srcs, scatter, dep, name):
    na = len(srcs)
    shapes = [t.shape[1:] if scatter else t.shape for t in srcs]
    lands = [pltpu.with_memory_space_constraint(lax.empty((N_DEV,) + s, t.dtype), pltpu.HBM) for s, t in zip(shapes, srcs)]

    def body(*refs):
        src_refs, land_refs = refs[:na], refs[na:2 * na]
        send_sems, recv_sems = refs[2 * na + 1], refs[2 * na + 2]
        token = refs[-1]
        for send, _ in _push_copies(scatter, src_refs, land_refs, send_sems, recv_sems):
            send.start()
        token[...] = jnp.zeros_like(token)

    sem = pltpu.SemaphoreType.DMA((na * (N_DEV - 1),))
    outs = pl.pallas_call(
        body, name=name,
        out_shape=(sem, sem) + tuple(pltpu.HBM(t.shape, t.dtype) for t in srcs)
        + tuple(pltpu.HBM(t.shape, t.dtype) for t in lands) + (jax.ShapeDtypeStruct((8, LANE), F32),),
        in_specs=[_HBM] * (2 * na) + [pl.BlockSpec(memory_space=pl.ANY)],
        out_specs=(_SEM, _SEM) + (_HBM,) * (2 * na) + (pl.BlockSpec(memory_space=pltpu.VMEM),),
        input_output_aliases={i: 2 + i for i in range(2 * na)},
        compiler_params=pltpu.CompilerParams(has_side_effects=_EFFECT),
    )(*[pltpu.with_memory_space_constraint(t, pltpu.HBM) for t in srcs], *lands, dep)
    return outs[0], outs[1], outs[2:2 + na], outs[2 + na:2 + 2 * na], outs[-1]


def _push_wait(send_sems, recv_sems, src_thru, land_thru, scatter, after, name):
    na = len(src_thru)

    def body(*refs):
        src_refs, land_refs = refs[:na], refs[na:2 * na]
        ssem, rsem = refs[2 * na], refs[2 * na + 1]
        for send, recv in _push_copies(scatter, src_refs, land_refs, ssem, rsem):
            send.wait_send()
            recv.wait_recv()

    outs = pl.pallas_call(
        body, name=name,
        out_shape=tuple(pltpu.HBM(t.shape, t.dtype) for t in src_thru) + tuple(pltpu.HBM(t.shape, t.dtype) for t in land_thru),
        in_specs=[_HBM] * (2 * na) + [_SEM, _SEM, pl.BlockSpec(memory_space=pl.ANY)],
        out_specs=(_HBM,) * (2 * na),
        input_output_aliases={i: i for i in range(2 * na)},
        compiler_params=pltpu.CompilerParams(has_side_effects=_EFFECT),
    )(*src_thru, *land_thru, send_sems, recv_sems, after)
    return outs[:na], outs[na:]


def _exchange_behind(srcs, scatter, dep, name):
    send_sems, recv_sems, thru, lands, token = _push_start(srcs, scatter, dep, name + "_start")

    def finish(after):
        src_done, land_done = _push_wait(send_sems, recv_sems, thru, lands, scatter, after, name + "_wait")
        return _place_own(land_done, src_done, scatter, name + "_own")

    return token[0, 0], finish


def _place_own(lands, srcs, scatter, name):
    me = (4 * lax.axis_index("x") + 2 * lax.axis_index("y") + lax.axis_index("c")).astype(jnp.int32).reshape(1)
    outs = []
    for a, (land, src) in enumerate(zip(lands, srcs)):
        r_, c_ = land.shape[1:]
        tr = _pick(r_, (512, 256, 128, 64, 32, 16))

        def body(me_ref, land_ref, src_ref, out_ref):
            out_ref[...] = src_ref[...]

        src_spec = (pl.BlockSpec((None, tr, c_), lambda i, me_: (me_[0], i, 0)) if scatter
                    else pl.BlockSpec((tr, c_), lambda i, me_: (i, 0)))
        gs = pltpu.PrefetchScalarGridSpec(
            num_scalar_prefetch=1, grid=(r_ // tr,),
            in_specs=[pl.BlockSpec(memory_space=pl.ANY), src_spec],
            out_specs=pl.BlockSpec((None, tr, c_), lambda i, me_: (me_[0], i, 0)))
        outs.append(pl.pallas_call(
            body, name=f"{name}_{a}", grid_spec=gs, out_shape=jax.ShapeDtypeStruct(land.shape, land.dtype),
            input_output_aliases={1: 0}, compiler_params=_params("arbitrary"),
        )(me, land, src))
    return outs


_BIG = (("w_in", D_MODEL, D_IN, 1), ("w_uq", Q_RANK, HEADS * QK, 1), ("w_ukv", KV_RANK, HEADS * (NOPE + VDIM), 1),
        ("w_out", D_MODEL, D_MODEL, 0), ("w_gate", D_MODEL, D_FF, 1), ("w_up", D_MODEL, D_FF, 1),
        ("w_down", D_FF, D_MODEL, 0))
_TRANSPOSED = ("w_in", "w_uq", "w_gate", "w_up")
_CQKV = (0, Q_RANK + KV_RANK)
_KR = (_CQKV[1], _CQKV[1] + ROPE)
_Z = (_KR[1], _KR[1] + SSD_W)
_XBC = (_Z[1], _Z[1] + CONV_DIM)
_DT = (_XBC[1], _XBC[1] + SSD_H)


def _win_segments(w_in_t):
    w = w_in_t.reshape(D_IN, D_MODEL)
    small = jnp.concatenate([w[_KR[0]:_KR[1]], w[_DT[0]:_DT[1]],
                             jnp.zeros((LANE - ROPE - SSD_H, D_MODEL), w.dtype)], axis=0)
    return w[_CQKV[0]:_CQKV[1]], w[_Z[0]:_Z[1]], w[_XBC[0]:_XBC[1]], small


def _win_from_segments(g_cqkv, g_z, g_xbc, g_small):
    w = jnp.concatenate([g_cqkv, g_small[:ROPE], g_z, g_xbc, g_small[ROPE:ROPE + SSD_H]], axis=0)
    return w.reshape(N_DEV, D_IN // N_DEV, D_MODEL)


_SMALL = (("q_norm_w", 512), ("kv_norm_w", 512), ("conv_b", CONV_DIM), ("dt_bias", SSD_H), ("a_log", SSD_H),
          ("d_skip", SSD_H), ("ssd_norm_w", SSD_W), ("attn_out_norm_w", 1024), ("pre_mix_norm_w", D_MODEL),
          ("post_mix_norm_w", D_MODEL), ("pre_ffn_norm_w", D_MODEL), ("post_ffn_norm_w", D_MODEL),
          ("conv_w", CONV_K * CONV_DIM))
_SMALL_ROWS = -(-sum(-(-n // LANE) for _, n in _SMALL) // 8) * 8


def _pack_small(vals):
    rows = []
    for name, n in _SMALL:
        v = vals[name].reshape(-1).astype(F32)
        pad = -(-n // LANE) * LANE
        rows.append(jnp.pad(v, (0, pad - n)).reshape(-1, LANE))
    m = jnp.concatenate(rows, axis=0)
    return jnp.pad(m, ((0, _SMALL_ROWS - m.shape[0]), (0, 0)))


def _unpack_small(m):
    out, r = {}, 0
    for name, n in _SMALL:
        nr = -(-n // LANE)
        out[name] = m[r:r + nr].reshape(-1)[:n]
        r += nr
    return out


def _head_row(v):
    return jnp.pad(v.reshape(1, -1).astype(F32), ((0, 0), (HEAD_LANE, LANE - HEAD_LANE - v.shape[-1])))


def _local_step(x, positions, target, wg, small, weights, on_grads):
    w_cqkv, w_z, w_xbc, w_small = _win_segments(wg["w_in"])
    conv_w = wg["conv_w"]
    conv_b = small["conv_b"].reshape(1, CONV_DIM)
    qkv_norm_w = jnp.concatenate([small["q_norm_w"], small["kv_norm_w"]])
    attn_norm_w = small["attn_out_norm_w"].reshape(1, HEADS * VDIM)
    scale = QK ** -0.5

    inv_freq = ROPE_THETA ** (-jnp.arange(0, ROPE, 2, dtype=F32) / ROPE)
    ang = positions.astype(F32)[:, None] * inv_freq
    cos2 = jnp.tile(jnp.cos(ang), (1, 2))
    sin2 = jnp.tile(jnp.sin(ang), (1, 2))

    u = _rms_fwd(x, small["pre_mix_norm_w"], out_dtype=MXU_DTYPE, name="pre_mix_norm")
    cqkv = _mm(u, w_cqkv, "nt", name="in_proj_qkv")
    z = _mm(u, w_z, "nt", name="in_proj_z")
    xbc = _mm(u, w_xbc, "nt", name="in_proj_xbc")
    sm = _mm(u, w_small, "nt", name="in_proj_small")

    w_uq, w_ukv = weights("qkv_up", cqkv)
    qkvn = _rms_fwd(cqkv, qkv_norm_w, groups=2, out_dtype=MXU_DTYPE, name="qkv_norm")
    q_h = _q_up(qkvn, w_uq, cos2, sin2, scale)
    k_h, v_h = _kv_up(qkvn, w_ukv, sm, cos2, sin2)
    o_h, lse = _flash_fwd(q_h, k_h, v_h)
    cat = _hnorm_fwd(o_h, attn_norm_w, D_MODEL)
    w_out = weights("out", o_h)[0].reshape(D_MODEL, D_MODEL)

    xbc_act = _conv_fwd(xbc, conv_w, conv_b)
    dtt = jnp.transpose(sm[:, HEAD_LANE:HEAD_LANE + SSD_H])
    ssd_args = (xbc_act, sm, dtt, _head_row(small["dt_bias"]), small["dt_bias"].reshape(SSD_H, 1),
                _head_row(small["a_log"]), small["a_log"].reshape(SSD_H, 1),
                jnp.broadcast_to(small["d_skip"].reshape(SSD_H, 1), (SSD_H, SSD_P)).reshape(SSD_PAIRS, 1, LANE))
    y_ssd, prev = _ssd_fwd(*ssd_args)
    cat = _gated_norm_fwd(y_ssd, z, small["ssd_norm_w"], cat)

    mix = _mm(cat, w_out, "nn", name="out_proj")
    h1, vv = _norm_res_norm(mix, x, small["post_mix_norm_w"], small["pre_ffn_norm_w"])

    w_gate, w_up = weights("ffn_in", mix)
    gate, up, act = _ffn_fwd(vv, w_gate, w_up)
    w_down, = weights("ffn_out", act)
    ffn = _mm(act, w_down, "nn", a_blk=True, b_blk=True, fuse=2, name="ffn_down")
    loss_blk, dy, dffn, g_post_ffn = _loss_head(ffn, h1, target, small["post_ffn_norm_w"])

    g_down = _mm(act, dffn, "tn", a_blk=True, out_blk=True, out_dtype=MXU_DTYPE, name="g_down")
    dgate, dup = _ffn_bwd_act(dffn, w_down, gate, up)
    dvv = _ffn_bwd_in(dgate, w_gate, dup, w_up)
    g_gate = _mm(dgate, vv, "tn", a_blk=True, out_blk=True, out_dtype=MXU_DTYPE, name="g_gate")
    g_up = _mm(dup, vv, "tn", a_blk=True, out_blk=True, out_dtype=MXU_DTYPE, name="g_up")
    pre_ffn_w = small["pre_ffn_norm_w"] + on_grads("ffn", [g_gate, g_up, g_down])
    dh1, dmix, g_pre_ffn, g_post_mix = _norm_res_norm_bwd(h1, pre_ffn_w, dvv, dy, mix, small["post_mix_norm_w"])

    dcat = _mm(dmix, w_out, "nt", name="d_cat")
    g_out = _mm(cat, dmix, "tn", out_dtype=MXU_DTYPE, name="g_out")

    do_h, delta, g_attn_norm = _hnorm_bwd(o_h, attn_norm_w, dcat)
    dq_h, dk_h, dv_h = _flash_bwd(q_h, k_h, v_h, do_h, lse, delta)
    dq = _q_prep(dq_h, cos2, -sin2, scale, name="dq_post")

    dy_ssd, dz, g_ssd_norm = _gated_norm_bwd(y_ssd, z, small["ssd_norm_w"], dcat)
    dxbc_act, ddt, dpar = _ssd_bwd(*ssd_args, prev, dy_ssd)
    dkv, dsm = _dkv_post(dk_h, dv_h, ddt, cos2, -sin2)
    dpre, dwb = _conv_bwd_pre(xbc, conv_w, conv_b, dxbc_act)
    dxbc = _conv_bwd_in(dpre, conv_w)

    dqn = _mm(dq, w_uq, "nn", a_blk=True, b_blk=True, fuse=HEADS, name="d_qn")
    dkvn = _mm(dkv, w_ukv, "nt", a_blk=True, b_blk=True, fuse=HEADS, name="d_kvn")
    g_uq = _mm(dq, qkvn, "tn", a_blk=True, out_blk=True, b_cols=(0, Q_RANK), out_dtype=MXU_DTYPE, name="g_uq")
    g_ukv = _mm(qkvn, dkv, "tn", b_blk=True, out_blk=True, a_cols=(Q_RANK, KV_RANK), out_dtype=MXU_DTYPE, name="g_ukv")
    heads_token = on_grads("heads", [g_uq, g_ukv, g_out.reshape(N_DEV, D_MODEL // N_DEV, D_MODEL)])
    dcqkv, g_qkv_norm = _rms_bwd(cqkv, qkv_norm_w + heads_token, [dqn, dkvn], out_dtype=MXU_DTYPE, name="qkv_norm_bwd")

    g_in = _win_from_segments(_mm(dcqkv, u, "tn", out_dtype=MXU_DTYPE, name="g_in_qkv"),
                              _mm(dz, u, "tn", out_dtype=MXU_DTYPE, name="g_in_z"),
                              _mm(dxbc, u, "tn", out_dtype=MXU_DTYPE, name="g_in_xbc"),
                              _mm(dsm, u, "tn", out_dtype=MXU_DTYPE, name="g_in_small"))
    in_token = on_grads("in", [g_in])
    du = _mm(dsm + in_token.astype(dsm.dtype), w_small, "nn", name="d_u_small")
    du = _mm(dcqkv, w_cqkv, "nn", add=du, name="d_u_qkv")
    du = _mm(dz, w_z, "nn", add=du, name="d_u_z")
    du = _mm(dxbc, w_xbc, "nn", add=du, name="d_u_xbc")
    dx, g_pre_mix = _rms_bwd(x, small["pre_mix_norm_w"], [du], res=dh1, name="pre_mix_norm_bwd")

    hl = slice(HEAD_LANE, HEAD_LANE + SSD_H)
    g_small = {"q_norm_w": g_qkv_norm[0, :Q_RANK], "kv_norm_w": g_qkv_norm[0, Q_RANK:], "conv_b": dwb[CONV_K],
               "dt_bias": dpar[0, hl], "a_log": dpar[1, hl], "d_skip": dpar[2, hl], "ssd_norm_w": g_ssd_norm,
               "attn_out_norm_w": g_attn_norm, "pre_mix_norm_w": g_pre_mix, "post_mix_norm_w": g_post_mix,
               "pre_ffn_norm_w": g_pre_ffn, "post_ffn_norm_w": g_post_ffn, "conv_w": dwb[:CONV_K]}
    return loss_blk[0, 0], dx, g_small


_WEIGHT_ORDER = ("w_in", "q_norm_w", "w_uq", "kv_norm_w", "w_ukv", "conv_w", "conv_b", "dt_bias", "a_log", "d_skip",
                 "ssd_norm_w", "attn_out_norm_w", "w_out", "pre_mix_norm_w", "post_mix_norm_w", "pre_ffn_norm_w",
                 "post_ffn_norm_w", "w_gate", "w_up", "w_down")


def kernel(x, positions, w_in, q_norm_w, w_uq, kv_norm_w, w_ukv, conv_w, conv_b, dt_bias, a_log, d_skip, ssd_norm_w, attn_out_norm_w, w_out, pre_mix_norm_w, post_mix_norm_w, pre_ffn_norm_w, post_ffn_norm_w, w_gate, w_up, w_down, loss_target, m_w_in, m_q_norm_w, m_w_uq, m_kv_norm_w, m_w_ukv, m_conv_w, m_conv_b, m_dt_bias, m_a_log, m_d_skip, m_ssd_norm_w, m_attn_out_norm_w, m_w_out, m_pre_mix_norm_w, m_post_mix_norm_w, m_pre_ffn_norm_w, m_post_ffn_norm_w, m_w_gate, m_w_up, m_w_down, v_w_in, v_q_norm_w, v_w_uq, v_kv_norm_w, v_w_ukv, v_conv_w, v_conv_b, v_dt_bias, v_a_log, v_d_skip, v_ssd_norm_w, v_attn_out_norm_w, v_w_out, v_pre_mix_norm_w, v_post_mix_norm_w, v_pre_ffn_norm_w, v_post_ffn_norm_w, v_w_gate, v_w_up, v_w_down):
    w = dict(w_in=w_in, q_norm_w=q_norm_w, w_uq=w_uq, kv_norm_w=kv_norm_w, w_ukv=w_ukv, conv_w=conv_w, conv_b=conv_b,
             dt_bias=dt_bias, a_log=a_log, d_skip=d_skip, ssd_norm_w=ssd_norm_w, attn_out_norm_w=attn_out_norm_w,
             w_out=w_out, pre_mix_norm_w=pre_mix_norm_w, post_mix_norm_w=post_mix_norm_w,
             pre_ffn_norm_w=pre_ffn_norm_w, post_ffn_norm_w=post_ffn_norm_w, w_gate=w_gate, w_up=w_up, w_down=w_down)
    m = dict(w_in=m_w_in, q_norm_w=m_q_norm_w, w_uq=m_w_uq, kv_norm_w=m_kv_norm_w, w_ukv=m_w_ukv, conv_w=m_conv_w,
             conv_b=m_conv_b, dt_bias=m_dt_bias, a_log=m_a_log, d_skip=m_d_skip, ssd_norm_w=m_ssd_norm_w,
             attn_out_norm_w=m_attn_out_norm_w, w_out=m_w_out, pre_mix_norm_w=m_pre_mix_norm_w,
             post_mix_norm_w=m_post_mix_norm_w, pre_ffn_norm_w=m_pre_ffn_norm_w, post_ffn_norm_w=m_post_ffn_norm_w,
             w_gate=m_w_gate, w_up=m_w_up, w_down=m_w_down)
    v = dict(w_in=v_w_in, q_norm_w=v_q_norm_w, w_uq=v_w_uq, kv_norm_w=v_kv_norm_w, w_ukv=v_w_ukv, conv_w=v_conv_w,
             conv_b=v_conv_b, dt_bias=v_dt_bias, a_log=v_a_log, d_skip=v_d_skip, ssd_norm_w=v_ssd_norm_w,
             attn_out_norm_w=v_attn_out_norm_w, w_out=v_w_out, pre_mix_norm_w=v_pre_mix_norm_w,
             post_mix_norm_w=v_post_mix_norm_w, pre_ffn_norm_w=v_pre_ffn_norm_w, post_ffn_norm_w=v_post_ffn_norm_w,
             w_gate=v_w_gate, w_up=v_w_up, w_down=v_w_down)
    w, m, v = ({k: t[0] for k, t in d.items()} for d in (w, m, v))
    me = 4 * lax.axis_index("x") + 2 * lax.axis_index("y") + lax.axis_index("c")
    groups = {"qkv_up": ("w_uq", "w_ukv"), "out": ("w_out",), "ffn_in": ("w_gate", "w_up"), "ffn_out": ("w_down",)}
    cshard = CONV_DIM // N_DEV
    for name in _TRANSPOSED:
        w[name], m[name], v[name] = w[name].T, m[name].T, v[name].T

    shards = [w["w_in"].astype(MXU_DTYPE),
              jnp.stack(_split3(w["conv_w"])).reshape(3 * CONV_K, cshard).astype(MXU_DTYPE)]
    w_in_g, cw = _all_gather(shards, name="gather_weights")
    cw = cw.astype(F32).reshape(N_DEV, 3, CONV_K, cshard)
    wg = {"w_in": w_in_g, "conv_w": jnp.transpose(cw[:, 0] + cw[:, 1] + cw[:, 2], (1, 0, 2)).reshape(CONV_K, CONV_DIM)}
    arriving, dep, started = {}, wg["conv_w"], jnp.zeros((), F32)
    small = {name: w[name] for name, _ in _SMALL if name != "conv_w"}
    for group in ("qkv_up", "out", "ffn_in", "ffn_out"):
        token, arriving[group] = _exchange_behind([w[name].astype(MXU_DTYPE) for name in groups[group]], False,
                                                  dep, group + "_weights")
        started = started + token
        dep = jnp.zeros((8, LANE), F32) + started
    small["pre_mix_norm_w"] = small["pre_mix_norm_w"] + started

    leaving = {}

    def on_grads(group, gs):
        token, leaving[group] = _exchange_behind(gs, True, jnp.zeros((8, LANE), F32), group + "_grads")
        return token

    loss_local, dx, g_small = _local_step(x[0], positions[0], loss_target[0], wg, small,
                                          lambda group, after: arriving[group](after), on_grads)
    loss = lax.psum(loss_local, ("x", "y", "c"))

    recv = {}
    for group, names in (("ffn", ("w_gate", "w_up", "w_down")), ("heads", ("w_uq", "w_ukv", "w_out")), ("in", ("w_in",))):
        recv.update(zip(names, leaving[group](dx)))
    grads, deltas, new_m, new_v = {}, {}, {}, {}
    for name, parts in recv.items():
        outs = _adamw(parts, w[name], m[name], v[name], name="adamw_" + name)
        if name in _TRANSPOSED:
            outs = [t.T for t in outs]
        grads[name], deltas[name], new_m[name], new_v[name] = outs

    def embed(t):
        return lax.dynamic_update_slice(jnp.zeros((CONV_K, CONV_DIM), F32), t, (0, me * cshard))

    parts_s = _all_gather([_pack_small(g_small)], name="gather_small_grads")[0]
    packs = [_pack_small({**{n_: d[n_] for n_, _ in _SMALL if n_ != "conv_w"}, "conv_w": embed(d["conv_w"])})
             for d in (w, m, v)]
    outs = [_unpack_small(t) for t in _adamw_small(parts_s, *packs)]
    for name, n in _SMALL:
        for dst, src in zip((grads, deltas, new_m, new_v), outs):
            if name == "conv_w":
                dst[name] = lax.dynamic_slice(src[name].reshape(CONV_K, CONV_DIM), (0, me * cshard), (CONV_K, cshard))
            else:
                dst[name] = src[name]

    def lead(d):
        return [d[name][None] for name in _WEIGHT_ORDER]

    return (loss, dx[None], *lead(grads), *lead(deltas), *lead(new_m), *lead(new_v))
```

```python
import numpy as np

import jax
import jax.numpy as jnp
from jax import lax
from jax.experimental import pallas as pl
from jax.experimental.pallas import tpu as pltpu

F32 = jnp.float32
BF16 = jnp.bfloat16
MXU_DTYPE = jnp.bfloat16
EPS = 1e-6
VMEM_LIMIT_BYTES = 48 * 1024 * 1024
K_TILE_MAX = 2048

N_DEV = 8
D_MODEL = 2048
Q_RANK = 512
KV_RANK = 512
ROPE = 64
HALF = ROPE // 2
HEADS = 8
NOPE = 128
VDIM = 128
QK = NOPE + ROPE
SSD_W = 1024
SSD_H = 16
SSD_P = 64
SSD_G = 2
SSD_E = SSD_H // SSD_G
SSD_N = 128
CHUNK = 128
CONV_K = 4
CONV_DIM = SSD_W + 2 * SSD_G * SSD_N
B_OFF = SSD_W
C_OFF = SSD_W + SSD_G * SSD_N
D_FF = 5632
D_IN = Q_RANK + KV_RANK + ROPE + SSD_W + CONV_DIM + SSD_H
ROPE_THETA = 10000.0
LANE = 128
HEAD_LANE = ROPE

ADAM_LR = 0.001
ADAM_B1 = 0.9
ADAM_B2 = 0.999
ADAM_EPS = 1e-08
ADAM_WD = 0.01
ADAM_STEP = 10


def _pick(n, cands):
    for c in cands:
        if n % c == 0:
            return c
    return n


def _params(*sem):
    return pltpu.CompilerParams(dimension_semantics=sem, vmem_limit_bytes=VMEM_LIMIT_BYTES)


def _sigmoid(x):
    return 1.0 / (1.0 + jnp.exp(-x))


def _silu(x):
    return x * _sigmoid(x)


def _dsilu(x):
    s = _sigmoid(x)
    return s * (1.0 + x * (1.0 - s))


def _softplus(x):
    e = jnp.exp(-jnp.abs(x))
    small = e * (1.0 - e * (0.5 - e * (1.0 / 3.0)))
    return jnp.maximum(x, 0.0) + jnp.where(e < 0.01, small, jnp.log(1.0 + e))


def _dot(a, b, ca, cb):
    return lax.dot_general(a, b, (((ca,), (cb,)), ((), ())), preferred_element_type=F32)


def _mx(v):
    return v.astype(MXU_DTYPE)


def _split3(a):
    hi = a.astype(BF16)
    r1 = a - hi.astype(F32)
    mid = r1.astype(BF16)
    lo = (r1 - mid.astype(F32)).astype(BF16)
    return hi, mid, lo


def _exact_dot(a, b, ca, cb, split_a):
    if split_a:
        return sum(_dot(p, b, ca, cb) for p in _split3(a))
    return sum(_dot(a, p, ca, cb) for p in _split3(b))


MM_ROW_GROUPS = 4


def _row_slices(tm, align):
    ng = MM_ROW_GROUPS
    while ng > 1 and (tm % ng or (tm // ng) % align):
        ng //= 2
    return [slice(g * (tm // ng), (g + 1) * (tm // ng)) for g in range(ng)]


def _mm(a, b, mode, *, a_blk=False, b_blk=False, out_blk=False, a_cols=None, b_cols=None, add=None, out_dtype=F32,
        fuse=1, name="mm"):
    a2, b2 = a.shape[-2:], b.shape[-2:]
    a_last = a2[1] if a_cols is None else a_cols[1]
    a_start = 0 if a_cols is None else a_cols[0]
    b_start = 0
    if b_cols is not None:
        assert mode != "nt"
        b_start, b2 = b_cols[0], (b2[0], b_cols[1])
    if mode == "nn":
        m, k, (k2, n) = a2[0], a_last, b2
    elif mode == "nt":
        m, k, (n, k2) = a2[0], a_last, b2
    else:
        k, m, (k2, n) = a2[0], a_last, b2
    assert k == k2, (a.shape, b.shape, mode)
    tm = _pick(m, (1024, 704, 512, 256, 128))
    tn = _pick(n, (1024, 768, 704, 512, 256, 192, 128))
    tk = k if k <= K_TILE_MAX else _pick(k, (K_TILE_MAX, 1024, 512))
    nk = k // tk
    jo = N_DEV if out_blk else 1
    reduce_blocks = a_blk and b_blk and not out_blk
    assert fuse == 1 or reduce_blocks
    jr = N_DEV // fuse if reduce_blocks else 1
    ca, cb = {"nn": (1, 0), "nt": (1, 1), "tn": (0, 0)}[mode]
    has_add = add is not None
    single = jr * nk == 1
    if mode == "tn":
        assert a_start % tm == 0
        a_block, a_idx = (tk, tm), (lambda i, kk: (kk, i + a_start // tm))
    else:
        assert a_start % tk == 0
        a_block, a_idx = (tm, tk), (lambda i, kk: (i, kk + a_start // tk))
    assert b_start % tn == 0
    b_block, b_idx = (((tn, tk), (lambda nn_, kk: (nn_, kk))) if mode == "nt"
                      else ((tk, tn), (lambda nn_, kk: (kk, nn_ + b_start // tn))))

    def blk_specs(blocked, block, idx, of_a, t):
        def pos(o, i, nn_, kk):
            return idx(i, kk) if of_a else idx(nn_, kk)
        if blocked:
            return pl.BlockSpec((None,) + block,
                                lambda o, i, nn_, r, kk: ((o if out_blk else r * fuse + t),) + pos(o, i, nn_, kk))
        return pl.BlockSpec(block, lambda o, i, nn_, r, kk: pos(o, i, nn_, kk))

    a_specs = [blk_specs(a_blk, a_block, a_idx, True, t) for t in range(fuse)]
    b_specs = [blk_specs(b_blk, b_block, b_idx, False, t) for t in range(fuse)]
    o_spec = (pl.BlockSpec((None, tm, tn), lambda o, i, nn_, r, kk: (o, i, nn_)) if out_blk
              else pl.BlockSpec((tm, tn), lambda o, i, nn_, r, kk: (i, nn_)))

    groups = _row_slices(tm, LANE if mode == "tn" else 16)

    def body(*refs):
        a_refs, b_refs = refs[:fuse], refs[fuse:2 * fuse]
        add_ref = refs[2 * fuse] if has_add else None
        o_ref = refs[2 * fuse + 1] if has_add else refs[2 * fuse]

        def partial(rs):
            out = None
            for t in range(fuse):
                av = a_refs[t][:, rs] if mode == "tn" else a_refs[t][rs, :]
                d = _dot(_mx(av), _mx(b_refs[t][...]), ca, cb)
                out = d if out is None else out + d
            return out

        if single:
            for rs in groups:
                res = partial(rs)
                if has_add:
                    res = res + add_ref[rs, :]
                o_ref[rs, :] = res.astype(o_ref.dtype)
            return
        acc = refs[-1]
        r, kk = pl.program_id(3), pl.program_id(4)

        @pl.when(jnp.logical_and(r == 0, kk == 0))
        def _():
            acc[...] = jnp.zeros_like(acc)

        for rs in groups:
            acc[rs, :] += partial(rs)

        @pl.when(jnp.logical_and(r == jr - 1, kk == nk - 1))
        def _():
            res = acc[...]
            if has_add:
                res = res + add_ref[...]
            o_ref[...] = res.astype(o_ref.dtype)

    out_shape = ((N_DEV, m, n) if out_blk else (m, n))
    return pl.pallas_call(
        body, name=name, grid=(jo, m // tm, n // tn, jr, nk),
        in_specs=a_specs + b_specs + ([o_spec] if has_add else []), out_specs=o_spec,
        out_shape=jax.ShapeDtypeStruct(out_shape, out_dtype),
        scratch_shapes=[] if single else [pltpu.VMEM((tm, tn), F32)],
        compiler_params=_params("parallel", "parallel", "parallel", "arbitrary", "arbitrary"),
    )(*((a,) * fuse + (b,) * fuse + ((add,) if has_add else ())))


def _mm_sum(a_list, b_list, name="mm_sum"):
    m, n = a_list[0].shape[0], b_list[0].shape[1]
    ns = len(a_list)
    tm = _pick(m, (1024, 512, 256, 128))
    tn = _pick(n, (1024, 512, 256, 128))
    groups = _row_slices(tm, 16)

    def body(*refs):
        a_refs, b_refs, o_ref = refs[:ns], refs[ns:2 * ns], refs[2 * ns]
        for rs in groups:
            acc = _dot(_mx(a_refs[0][rs, :]), _mx(b_refs[0][...]), 1, 0)
            for s in range(1, ns):
                acc = acc + _dot(_mx(a_refs[s][rs, :]), _mx(b_refs[s][...]), 1, 0)
            o_ref[rs, :] = acc

    return pl.pallas_call(
        body, name=name, grid=(m // tm, n // tn),
        in_specs=([pl.BlockSpec((tm, a.shape[1]), lambda i, j: (i, 0)) for a in a_list]
                  + [pl.BlockSpec((b.shape[0], tn), lambda i, j: (0, j)) for b in b_list]),
        out_specs=pl.BlockSpec((tm, tn), lambda i, j: (i, j)),
        out_shape=jax.ShapeDtypeStruct((m, n), F32), compiler_params=_params("parallel", "parallel"),
    )(*a_list, *b_list)


def _row_tile(r_):
    return _pick(r_, (256, 128, 64, 32, 16, 8))


def _rms_fwd(t, w, groups=1, res=None, out_dtype=F32, name="rms_fwd"):
    r_, f = t.shape
    fg = f // groups
    tr = _row_tile(r_)
    has_res = res is not None

    def body(*refs):
        t_ref, w_ref = refs[0], refs[1]
        res_ref = refs[2] if has_res else None
        o_ref = refs[-1]
        for g in range(groups):
            sl = slice(g * fg, (g + 1) * fg)
            tv = t_ref[:, sl].astype(F32)
            r = lax.rsqrt(jnp.mean(tv * tv, axis=-1, keepdims=True) + EPS)
            y = tv * r * w_ref[:, sl]
            if has_res:
                y = y + res_ref[:, sl]
            o_ref[:, sl] = y.astype(o_ref.dtype)

    row = pl.BlockSpec((tr, f), lambda i: (i, 0))
    wsp = pl.BlockSpec((1, f), lambda i: (0, 0))
    return pl.pallas_call(
        body, name=name, grid=(r_ // tr,),
        in_specs=[row, wsp] + ([row] if has_res else []), out_specs=row,
        out_shape=jax.ShapeDtypeStruct((r_, f), out_dtype),
        compiler_params=_params("parallel"),
    )(*((t, w.reshape(1, f)) + ((res,) if has_res else ())))


def _rms_bwd(t, w, dys, res=None, out_dtype=F32, name="rms_bwd"):
    r_, f = t.shape
    groups = len(dys)
    fg = f // groups
    tr = _row_tile(r_)
    has_res = res is not None

    def body(*refs):
        t_ref, w_ref = refs[0], refs[1]
        dy_refs = refs[2:2 + groups]
        res_ref = refs[2 + groups] if has_res else None
        dt_ref, dw_ref = refs[-2], refs[-1]

        @pl.when(pl.program_id(0) == 0)
        def _():
            dw_ref[...] = jnp.zeros_like(dw_ref)

        for g in range(groups):
            sl = slice(g * fg, (g + 1) * fg)
            tv = t_ref[:, sl].astype(F32)
            dyv = dy_refs[g][...].astype(F32)
            r = lax.rsqrt(jnp.mean(tv * tv, axis=-1, keepdims=True) + EPS)
            gw = dyv * w_ref[:, sl]
            c = jnp.mean(gw * tv, axis=-1, keepdims=True)
            dt = r * gw - tv * (r * r * r * c)
            if has_res:
                dt = dt + res_ref[:, sl]
            dt_ref[:, sl] = dt.astype(dt_ref.dtype)
            dw_ref[:, sl] += jnp.sum(dyv * tv * r, axis=0, keepdims=True)

    row = pl.BlockSpec((tr, f), lambda i: (i, 0))
    grow = pl.BlockSpec((tr, fg), lambda i: (i, 0))
    wsp = pl.BlockSpec((1, f), lambda i: (0, 0))
    return pl.pallas_call(
        body, name=name, grid=(r_ // tr,),
        in_specs=[row, wsp] + [grow] * groups + ([row] if has_res else []), out_specs=[row, wsp],
        out_shape=[jax.ShapeDtypeStruct((r_, f), out_dtype), jax.ShapeDtypeStruct((1, f), F32)],
        compiler_params=_params("arbitrary"),
    )(*((t, w.reshape(1, f)) + tuple(dys) + ((res,) if has_res else ())))


def _norm_res_norm(t, res, w1, w2, name="post_mix_pre_ffn_norm"):
    r_, f = t.shape
    tr = _row_tile(r_)

    def body(t_ref, res_ref, w1_ref, w2_ref, h_ref, v_ref):
        tv = t_ref[...]
        h = res_ref[...] + tv * lax.rsqrt(jnp.mean(tv * tv, axis=-1, keepdims=True) + EPS) * w1_ref[...]
        h_ref[...] = h
        v_ref[...] = (h * lax.rsqrt(jnp.mean(h * h, axis=-1, keepdims=True) + EPS) * w2_ref[...]).astype(v_ref.dtype)

    row = pl.BlockSpec((tr, f), lambda i: (i, 0))
    wsp = pl.BlockSpec((1, f), lambda i: (0, 0))
    return pl.pallas_call(
        body, name=name, grid=(r_ // tr,), in_specs=[row, row, wsp, wsp], out_specs=[row, row],
        out_shape=[jax.ShapeDtypeStruct((r_, f), F32), jax.ShapeDtypeStruct((r_, f), MXU_DTYPE)],
        compiler_params=_params("parallel"),
    )(t, res, w1.reshape(1, f), w2.reshape(1, f))


def _norm_res_norm_bwd(h, w2, dv, dres, t, w1, name="pre_ffn_post_mix_norm_bwd"):
    r_, f = h.shape
    tr = _row_tile(r_)

    def body(h_ref, w2_ref, dv_ref, dres_ref, t_ref, w1_ref, dh_ref, dt_ref, dw2_ref, dw1_ref):
        @pl.when(pl.program_id(0) == 0)
        def _():
            dw2_ref[...] = jnp.zeros_like(dw2_ref)
            dw1_ref[...] = jnp.zeros_like(dw1_ref)

        def rms_bwd(tv, wv, dyv):
            r = lax.rsqrt(jnp.mean(tv * tv, axis=-1, keepdims=True) + EPS)
            gw = dyv * wv
            c = jnp.mean(gw * tv, axis=-1, keepdims=True)
            return r * gw - tv * (r * r * r * c), jnp.sum(dyv * tv * r, axis=0, keepdims=True)

        d1, g2 = rms_bwd(h_ref[...], w2_ref[...], dv_ref[...])
        dh = d1 + dres_ref[...]
        dh_ref[...] = dh
        dw2_ref[...] += g2
        d2, g1 = rms_bwd(t_ref[...], w1_ref[...], dh)
        dt_ref[...] = d2.astype(dt_ref.dtype)
        dw1_ref[...] += g1

    row = pl.BlockSpec((tr, f), lambda i: (i, 0))
    wsp = pl.BlockSpec((1, f), lambda i: (0, 0))
    return pl.pallas_call(
        body, name=name, grid=(r_ // tr,), in_specs=[row, wsp, row, row, row, wsp], out_specs=[row, row, wsp, wsp],
        out_shape=[jax.ShapeDtypeStruct((r_, f), F32), jax.ShapeDtypeStruct((r_, f), MXU_DTYPE),
                   jax.ShapeDtypeStruct((1, f), F32), jax.ShapeDtypeStruct((1, f), F32)],
        compiler_params=_params("arbitrary"),
    )(h, w2.reshape(1, f), dv, dres, t, w1.reshape(1, f))


def _hnorm_fwd(o, w, width, name="attn_out_norm"):
    h, s_, v = o.shape
    tr = _row_tile(s_)

    def body(o_ref, w_ref, y_ref):
        ss = jnp.sum(o_ref[0] * o_ref[0], axis=-1, keepdims=True)
        for i in range(1, h):
            ss = ss + jnp.sum(o_ref[i] * o_ref[i], axis=-1, keepdims=True)
        r = lax.rsqrt(ss * (1.0 / (h * v)) + EPS)
        for i in range(h):
            sl = slice(i * v, (i + 1) * v)
            y_ref[:, sl] = (o_ref[i] * r * w_ref[:, sl]).astype(y_ref.dtype)

    return pl.pallas_call(
        body, name=name, grid=(s_ // tr,),
        in_specs=[pl.BlockSpec((h, tr, v), lambda i: (0, i, 0)), pl.BlockSpec((1, h * v), lambda i: (0, 0))],
        out_specs=pl.BlockSpec((tr, h * v), lambda i: (i, 0)),
        out_shape=jax.ShapeDtypeStruct((s_, width), MXU_DTYPE), compiler_params=_params("parallel"),
    )(o, w)


def _hnorm_bwd(o, w, dy, name="attn_out_norm_bwd"):
    h, s_, v = o.shape
    tr = _row_tile(s_)

    def body(o_ref, w_ref, dy_ref, do_ref, delta_ref, dw_ref):
        @pl.when(pl.program_id(0) == 0)
        def _():
            dw_ref[...] = jnp.zeros_like(dw_ref)

        ss = jnp.zeros((tr, 1), F32)
        cc = jnp.zeros((tr, 1), F32)
        for i in range(h):
            sl = slice(i * v, (i + 1) * v)
            ov = o_ref[i]
            ss = ss + jnp.sum(ov * ov, axis=-1, keepdims=True)
            cc = cc + jnp.sum(dy_ref[:, sl] * w_ref[:, sl] * ov, axis=-1, keepdims=True)
        r = lax.rsqrt(ss * (1.0 / (h * v)) + EPS)
        c = cc * (1.0 / (h * v))
        for i in range(h):
            sl = slice(i * v, (i + 1) * v)
            ov = o_ref[i]
            dyv = dy_ref[:, sl]
            dov = r * dyv * w_ref[:, sl] - ov * (r * r * r * c)
            do_ref[i] = dov.astype(do_ref.dtype)
            delta_ref[i] = jnp.sum(dov * ov, axis=-1, keepdims=True)
            dw_ref[:, sl] += jnp.sum(dyv * ov * r, axis=0, keepdims=True)

    blk = pl.BlockSpec((h, tr, v), lambda i: (0, i, 0))
    wsp = pl.BlockSpec((1, h * v), lambda i: (0, 0))
    return pl.pallas_call(
        body, name=name, grid=(s_ // tr,),
        in_specs=[blk, wsp, pl.BlockSpec((tr, h * v), lambda i: (i, 0))],
        out_specs=[blk, pl.BlockSpec((h, tr, 1), lambda i: (0, i, 0)), wsp],
        out_shape=[jax.ShapeDtypeStruct(o.shape, MXU_DTYPE), jax.ShapeDtypeStruct((h, s_, 1), F32),
                   jax.ShapeDtypeStruct((1, h * v), F32)],
        compiler_params=_params("arbitrary"),
    )(o, w, dy)


def _loss_head(ffn, h1, target, w, name="loss_head"):
    r_, f = ffn.shape
    tr = _row_tile(r_)

    def body(ffn_ref, h1_ref, tg_ref, w_ref, loss_ref, dy_ref, dffn_ref, dw_ref):
        @pl.when(pl.program_id(0) == 0)
        def _():
            dw_ref[...] = jnp.zeros_like(dw_ref)
            loss_ref[...] = jnp.zeros_like(loss_ref)

        tv = ffn_ref[...]
        wv = w_ref[...]
        r = lax.rsqrt(jnp.mean(tv * tv, axis=-1, keepdims=True) + EPS)
        tn = tv * r
        e = h1_ref[...] + tn * wv - tg_ref[...]
        tot = jnp.sum(jnp.sum(e * e, axis=1, keepdims=True), axis=0, keepdims=True) * (0.5 / f)
        loss_ref[...] += tot + jnp.zeros_like(loss_ref)
        dyv = e * (1.0 / f)
        dy_ref[...] = dyv
        gw = dyv * wv
        c = jnp.mean(gw * tv, axis=-1, keepdims=True)
        dffn_ref[...] = (r * gw - tv * (r * r * r * c)).astype(dffn_ref.dtype)
        dw_ref[...] += jnp.sum(dyv * tn, axis=0, keepdims=True)

    row = pl.BlockSpec((tr, f), lambda i: (i, 0))
    wsp = pl.BlockSpec((1, f), lambda i: (0, 0))
    lsp = pl.BlockSpec((1, LANE), lambda i: (0, 0))
    return pl.pallas_call(
        body, name=name, grid=(r_ // tr,),
        in_specs=[row, row, row, wsp], out_specs=[lsp, row, row, wsp],
        out_shape=[jax.ShapeDtypeStruct((1, LANE), F32), jax.ShapeDtypeStruct((r_, f), F32),
                   jax.ShapeDtypeStruct((r_, f), MXU_DTYPE), jax.ShapeDtypeStruct((1, f), F32)],
        compiler_params=_params("arbitrary"),
    )(ffn, h1, target, w.reshape(1, f))


def _rot_matrix():
    p = np.zeros((ROPE, ROPE), np.float32)
    for i in range(HALF):
        p[i + HALF, i] = -1.0
        p[i, i + HALF] = 1.0
    return jnp.asarray(p, BF16)


def _rope_val(r, c2, s2, rot):
    hi, mid, _ = _split3(r)
    return r * c2 + (_dot(hi, rot, 1, 0) + _dot(mid, rot, 1, 0)) * s2


def _q_prep(q, cos2, sin2, scale, name):
    h, s_, _ = q.shape
    tr = _pick(s_, (1024, 512, 256, 128, 64, 32, 16, 8))

    def body(q_ref, c_ref, s_ref, rot_ref, o_ref):
        x = q_ref[...]
        o_ref[:, :NOPE] = (x[:, :NOPE] * scale).astype(o_ref.dtype)
        o_ref[:, NOPE:] = (_rope_val(x[:, NOPE:], c_ref[...], s_ref[...], rot_ref[...]) * scale).astype(o_ref.dtype)

    blk = pl.BlockSpec((None, tr, QK), lambda hh, i: (hh, i, 0))
    csp = pl.BlockSpec((tr, ROPE), lambda hh, i: (i, 0))
    return pl.pallas_call(
        body, name=name, grid=(h, s_ // tr),
        in_specs=[blk, csp, csp, pl.BlockSpec((ROPE, ROPE), lambda hh, i: (0, 0))], out_specs=blk,
        out_shape=jax.ShapeDtypeStruct(q.shape, MXU_DTYPE), compiler_params=_params("parallel", "parallel"),
    )(q, cos2, sin2, _rot_matrix())


def _q_up(qkvn, w_uq_t, cos2, sin2, scale, name="q_up"):
    s_ = qkvn.shape[0]
    h = w_uq_t.shape[0]
    tm = _pick(s_, (4096, 2048, 1024, 512, 256, 128))

    def body(a_ref, w_ref, c_ref, s_ref, rot_ref, o_ref):
        for rs in _row_slices(tm, 16):
            x = _dot(_mx(a_ref[rs, :]), _mx(w_ref[...]), 1, 1)
            o_ref[rs, :NOPE] = (x[:, :NOPE] * scale).astype(o_ref.dtype)
            o_ref[rs, NOPE:] = (_rope_val(x[:, NOPE:], c_ref[rs, :], s_ref[rs, :], rot_ref[...]) * scale).astype(o_ref.dtype)

    csp = pl.BlockSpec((tm, ROPE), lambda j, i: (i, 0))
    return pl.pallas_call(
        body, name=name, grid=(h, s_ // tm),
        in_specs=[pl.BlockSpec((tm, Q_RANK), lambda j, i: (i, 0)), pl.BlockSpec((None, QK, Q_RANK), lambda j, i: (j, 0, 0)),
                  csp, csp, pl.BlockSpec((ROPE, ROPE), lambda j, i: (0, 0))],
        out_specs=pl.BlockSpec((None, tm, QK), lambda j, i: (j, i, 0)),
        out_shape=jax.ShapeDtypeStruct((h, s_, QK), MXU_DTYPE), compiler_params=_params("parallel", "parallel"),
    )(qkvn, w_uq_t, cos2, sin2, _rot_matrix())


def _kv_up(qkvn, w_ukv, small, cos2, sin2, name="kv_up"):
    s_ = qkvn.shape[0]
    h = w_ukv.shape[0]
    tm = _pick(s_, (4096, 2048, 1024, 512, 256, 128))

    def body(a_ref, w_ref, sm_ref, c_ref, s_ref, rot_ref, k_ref, v_ref):
        for rs in _row_slices(tm, 16):
            x = _dot(_mx(a_ref[rs, :]), _mx(w_ref[...]), 1, 0)
            k_ref[rs, :NOPE] = x[:, :NOPE].astype(k_ref.dtype)
            k_ref[rs, NOPE:] = _rope_val(sm_ref[rs, :ROPE], c_ref[rs, :], s_ref[rs, :], rot_ref[...]).astype(k_ref.dtype)
            v_ref[rs, :] = x[:, NOPE:].astype(v_ref.dtype)

    csp = pl.BlockSpec((tm, ROPE), lambda j, i: (i, 0))
    return pl.pallas_call(
        body, name=name, grid=(h, s_ // tm),
        in_specs=[pl.BlockSpec((tm, KV_RANK), lambda j, i: (i, Q_RANK // KV_RANK)),
                  pl.BlockSpec((None, KV_RANK, NOPE + VDIM), lambda j, i: (j, 0, 0)),
                  pl.BlockSpec((tm, LANE), lambda j, i: (i, 0)), csp, csp, pl.BlockSpec((ROPE, ROPE), lambda j, i: (0, 0))],
        out_specs=[pl.BlockSpec((None, tm, QK), lambda j, i: (j, i, 0)), pl.BlockSpec((None, tm, VDIM), lambda j, i: (j, i, 0))],
        out_shape=[jax.ShapeDtypeStruct((h, s_, QK), MXU_DTYPE), jax.ShapeDtypeStruct((h, s_, VDIM), MXU_DTYPE)],
        compiler_params=_params("parallel", "parallel"),
    )(qkvn, w_ukv, small, cos2, sin2, _rot_matrix())


def _dkv_post(dk, dv, ddt, cos2, nsin2, name="dkv_post"):
    h, s_, _ = dk.shape
    tr = _row_tile(s_)

    def body(dk_ref, dv_ref, ddt_ref, c_ref, s_ref, rot_ref, dkv_ref, dsm_ref):
        acc = dk_ref[0, :, NOPE:]
        for i in range(1, h):
            acc = acc + dk_ref[i, :, NOPE:]
        dsm_ref[:, :ROPE] = _rope_val(acc, c_ref[...], s_ref[...], rot_ref[...]).astype(dsm_ref.dtype)
        dsm_ref[:, ROPE:] = ddt_ref[:, ROPE:].astype(dsm_ref.dtype)
        for i in range(h):
            dkv_ref[i, :, :NOPE] = dk_ref[i, :, :NOPE].astype(dkv_ref.dtype)
            dkv_ref[i, :, NOPE:] = dv_ref[i].astype(dkv_ref.dtype)

    csp = pl.BlockSpec((tr, ROPE), lambda i: (i, 0))
    return pl.pallas_call(
        body, name=name, grid=(s_ // tr,),
        in_specs=[pl.BlockSpec((h, tr, QK), lambda i: (0, i, 0)), pl.BlockSpec((h, tr, VDIM), lambda i: (0, i, 0)),
                  pl.BlockSpec((tr, LANE), lambda i: (i, 0)), csp, csp, pl.BlockSpec((ROPE, ROPE), lambda i: (0, 0))],
        out_specs=[pl.BlockSpec((h, tr, NOPE + VDIM), lambda i: (0, i, 0)), pl.BlockSpec((tr, LANE), lambda i: (i, 0))],
        out_shape=[jax.ShapeDtypeStruct((h, s_, NOPE + VDIM), MXU_DTYPE), jax.ShapeDtypeStruct((s_, LANE), MXU_DTYPE)],
        compiler_params=_params("parallel"),
    )(dk, dv, ddt, cos2, nsin2, _rot_matrix())


def _attn_tile(s):
    return 2048 if s % 4096 == 0 else s // 2


def _pairs(n, by_key):
    if by_key:
        pr = [(i, j) for j in range(n) for i in range(j, n)]
    else:
        pr = [(i, j) for i in range(n) for j in range(i + 1)]
    return (jnp.asarray([p[0] for p in pr], jnp.int32), jnp.asarray([p[1] for p in pr], jnp.int32))


ATTN_ROW_GROUPS = 8


def _row_groups(t, diag):
    tg = t // ATTN_ROW_GROUPS
    out = []
    for r in range(ATTN_ROW_GROUPS):
        nc = (r + 1) * tg if diag else t
        mask = None
        if diag:
            mask = (lax.broadcasted_iota(jnp.int32, (tg, nc), 1)
                    <= lax.broadcasted_iota(jnp.int32, (tg, nc), 0) + r * tg)
        out.append((slice(r * tg, (r + 1) * tg), nc, mask))
    return out


def _flash_specs(t, dk, dv):
    qsp = pl.BlockSpec((None, t, dk), lambda hh, p, qi, kj: (hh, qi[p], 0))
    ksp = pl.BlockSpec((None, t, dk), lambda hh, p, qi, kj: (hh, kj[p], 0))
    vsp = pl.BlockSpec((None, t, dv), lambda hh, p, qi, kj: (hh, kj[p], 0))
    osp = pl.BlockSpec((None, t, dv), lambda hh, p, qi, kj: (hh, qi[p], 0))
    lsp = pl.BlockSpec((None, t, 1), lambda hh, p, qi, kj: (hh, qi[p], 0))
    return qsp, ksp, vsp, osp, lsp


def _flash_fwd(q, k, v, name="flash_fwd"):
    h, s_, dk = q.shape
    dv = v.shape[-1]
    t = _attn_tile(s_)
    n = s_ // t
    qi, kj = _pairs(n, False)

    def body(qi_ref, kj_ref, q_ref, k_ref, v_ref, o_ref, lse_ref, m_s, l_s, acc):
        p_ = pl.program_id(1)
        i, j = qi_ref[p_], kj_ref[p_]

        @pl.when(j == 0)
        def _():
            m_s[...] = jnp.full_like(m_s, -jnp.inf)
            l_s[...] = jnp.zeros_like(l_s)
            acc[...] = jnp.zeros_like(acc)

        def update(diag):
            for rs, nc, mask in _row_groups(t, diag):
                sc = _dot(q_ref[rs, :], k_ref[0:nc, :], 1, 1)
                if mask is not None:
                    sc = jnp.where(mask, sc, -jnp.inf)
                m_old = m_s[rs, :]
                m_new = jnp.maximum(m_old, jnp.max(sc, axis=1, keepdims=True))
                alpha = jnp.exp(m_old - m_new)
                p = jnp.exp(sc - m_new)
                l_s[rs, :] = alpha * l_s[rs, :] + jnp.sum(p, axis=1, keepdims=True)
                acc[rs, :] = alpha * acc[rs, :] + _dot(_mx(p), v_ref[0:nc, :], 1, 0)
                m_s[rs, :] = m_new

        @pl.when(j < i)
        def _():
            update(False)

        @pl.when(j == i)
        def _():
            update(True)
            o_ref[...] = acc[...] / l_s[...]
            lse_ref[...] = m_s[...] + jnp.log(l_s[...])

    qsp, ksp, vsp, osp, lsp = _flash_specs(t, dk, dv)
    gs = pltpu.PrefetchScalarGridSpec(
        num_scalar_prefetch=2, grid=(h, qi.shape[0]), in_specs=[qsp, ksp, vsp], out_specs=[osp, lsp],
        scratch_shapes=[pltpu.VMEM((t, 1), F32), pltpu.VMEM((t, 1), F32), pltpu.VMEM((t, dv), F32)])
    return pl.pallas_call(
        body, name=name, grid_spec=gs,
        out_shape=[jax.ShapeDtypeStruct((h, s_, dv), F32), jax.ShapeDtypeStruct((h, s_, 1), F32)],
        compiler_params=_params("parallel", "arbitrary"),
    )(qi, kj, q, k, v)


def _flash_bwd(q, k, v, do, lse, delta, name="flash_bwd"):
    h, s_, dk = q.shape
    dv = v.shape[-1]
    t = _attn_tile(s_)
    tg = t // ATTN_ROW_GROUPS
    n = s_ // t
    qi, kj = _pairs(n, True)

    def body(qi_ref, kj_ref, q_ref, k_ref, v_ref, do_ref, lse_ref, delta_ref, dq_ref, dk_ref, dv_ref, dk_acc, dv_acc):
        p_ = pl.program_id(1)
        i, j = qi_ref[p_], kj_ref[p_]

        @pl.when(p_ == 0)
        def _():
            dq_ref[...] = jnp.zeros_like(dq_ref)

        def update(diag):
            for g, (rs, nc, mask) in enumerate(_row_groups(t, diag)):
                sc = _dot(q_ref[rs, :], k_ref[0:nc, :], 1, 1)
                if mask is not None:
                    sc = jnp.where(mask, sc, -jnp.inf)
                p = jnp.exp(sc - lse_ref[rs, :])
                dob = _mx(do_ref[rs, :])
                dv_acc[0:nc, :] += _dot(_mx(p), dob, 0, 0)
                dp = _dot(dob, v_ref[0:nc, :], 1, 1)
                dsb = _mx(p * (dp - delta_ref[rs, :]))
                dk_acc[0:nc, :] += _dot(dsb, q_ref[rs, :], 0, 0)
                rows = pl.ds(pl.multiple_of(i * t + g * tg, tg), tg)
                dq_ref[rows, :] += _dot(dsb, k_ref[0:nc, :], 1, 0)

        @pl.when(i == j)
        def _():
            dk_acc[...] = jnp.zeros_like(dk_acc)
            dv_acc[...] = jnp.zeros_like(dv_acc)
            update(True)

        @pl.when(i > j)
        def _():
            update(False)

        @pl.when(i == n - 1)
        def _():
            dk_ref[...] = dk_acc[...]
            dv_ref[...] = dv_acc[...]

    qsp, ksp, vsp, osp, lsp = _flash_specs(t, dk, dv)
    dqsp = pl.BlockSpec((None, s_, dk), lambda hh, p, qi, kj: (hh, 0, 0))
    gs = pltpu.PrefetchScalarGridSpec(
        num_scalar_prefetch=2, grid=(h, qi.shape[0]), in_specs=[qsp, ksp, vsp, osp, lsp, lsp],
        out_specs=[dqsp, ksp, vsp],
        scratch_shapes=[pltpu.VMEM((t, dk), F32), pltpu.VMEM((t, dv), F32)])
    return pl.pallas_call(
        body, name=name, grid_spec=gs,
        out_shape=[jax.ShapeDtypeStruct((h, s_, dk), F32), jax.ShapeDtypeStruct((h, s_, dk), F32),
                   jax.ShapeDtypeStruct((h, s_, dv), F32)],
        compiler_params=_params("parallel", "arbitrary"),
    )(qi, kj, q, k, v, do, lse, delta)


HALO = 8


def _conv_specs(s_, c, tr, after):
    main = pl.BlockSpec((tr, c), lambda i: (i, 0))
    per = tr // HALO
    if after:
        halo = pl.BlockSpec((HALO, c), lambda i: (jnp.minimum((i + 1) * per, s_ // HALO - 1), 0))
    else:
        halo = pl.BlockSpec((HALO, c), lambda i: (jnp.maximum(i * per - 1, 0), 0))
    return main, halo


def _fill_before(ext, t_ref, h_ref, tr):
    ext[0:HALO, :] = jnp.where(pl.program_id(0) > 0, h_ref[...], 0.0)
    ext[HALO:HALO + tr, :] = t_ref[...]


def _taps(ext, w_ref, tr):
    base = HALO - (CONV_K - 1)
    acc = ext[base:base + tr, :] * w_ref[0:1, :]
    for k in range(1, CONV_K):
        acc = acc + ext[base + k:base + k + tr, :] * w_ref[k:k + 1, :]
    return acc


def _conv_fwd(t, w, b, name="conv_fwd"):
    s_, c = t.shape
    tr = _row_tile(s_)

    def body(t_ref, h_ref, w_ref, b_ref, o_ref, ext):
        _fill_before(ext, t_ref, h_ref, tr)
        o_ref[...] = _silu(_taps(ext, w_ref, tr) + b_ref[...])

    main, halo = _conv_specs(s_, c, tr, False)
    return pl.pallas_call(
        body, name=name, grid=(s_ // tr,),
        in_specs=[main, halo, pl.BlockSpec((CONV_K, c), lambda i: (0, 0)), pl.BlockSpec((1, c), lambda i: (0, 0))],
        out_specs=main, out_shape=jax.ShapeDtypeStruct((s_, c), F32),
        scratch_shapes=[pltpu.VMEM((tr + HALO, c), F32)], compiler_params=_params("parallel"),
    )(t, t, w, b)


def _conv_bwd_pre(t, w, b, dact, name="conv_bwd_pre"):
    s_, c = t.shape
    tr = _row_tile(s_)

    def body(t_ref, h_ref, w_ref, b_ref, da_ref, dpre_ref, dwb_ref, ext):
        @pl.when(pl.program_id(0) == 0)
        def _():
            dwb_ref[...] = jnp.zeros_like(dwb_ref)

        _fill_before(ext, t_ref, h_ref, tr)
        dpre = da_ref[...] * _dsilu(_taps(ext, w_ref, tr) + b_ref[...])
        dpre_ref[...] = dpre
        base = HALO - (CONV_K - 1)
        for k in range(CONV_K):
            dwb_ref[k:k + 1, :] += jnp.sum(dpre * ext[base + k:base + k + tr, :], axis=0, keepdims=True)
        dwb_ref[CONV_K:CONV_K + 1, :] += jnp.sum(dpre, axis=0, keepdims=True)

    main, halo = _conv_specs(s_, c, tr, False)
    return pl.pallas_call(
        body, name=name, grid=(s_ // tr,),
        in_specs=[main, halo, pl.BlockSpec((CONV_K, c), lambda i: (0, 0)), pl.BlockSpec((1, c), lambda i: (0, 0)), main],
        out_specs=[main, pl.BlockSpec((8, c), lambda i: (0, 0))],
        out_shape=[jax.ShapeDtypeStruct((s_, c), F32), jax.ShapeDtypeStruct((8, c), F32)],
        scratch_shapes=[pltpu.VMEM((tr + HALO, c), F32)], compiler_params=_params("arbitrary"),
    )(t, t, w, b, dact)


def _conv_bwd_in(dpre, w, name="conv_bwd_in"):
    s_, c = dpre.shape
    tr = _row_tile(s_)
    nt = s_ // tr

    def body(d_ref, h_ref, w_ref, o_ref, ext):
        ext[0:tr, :] = d_ref[...]
        ext[tr:tr + HALO, :] = jnp.where(pl.program_id(0) < nt - 1, h_ref[...], 0.0)
        acc = ext[CONV_K - 1:CONV_K - 1 + tr, :] * w_ref[0:1, :]
        for k in range(1, CONV_K):
            acc = acc + ext[CONV_K - 1 - k:CONV_K - 1 - k + tr, :] * w_ref[k:k + 1, :]
        o_ref[...] = acc.astype(o_ref.dtype)

    main, halo = _conv_specs(s_, c, tr, True)
    return pl.pallas_call(
        body, name=name, grid=(nt,),
        in_specs=[main, halo, pl.BlockSpec((CONV_K, c), lambda i: (0, 0))],
        out_specs=main, out_shape=jax.ShapeDtypeStruct((s_, c), MXU_DTYPE),
        scratch_shapes=[pltpu.VMEM((tr + HALO, c), F32)], compiler_params=_params("parallel"),
    )(dpre, dpre, w)


def _ssd_chunk_common(dt_ref, dtt_ref, br_ref, bc_ref, ar_ref, ac_ref):
    li = lax.broadcasted_iota(jnp.int32, (CHUNK, CHUNK), 0)
    si = lax.broadcasted_iota(jnp.int32, (CHUNK, CHUNK), 1)
    lower = li >= si
    lower_b = lower.astype(BF16)
    upper_b = (li <= si).astype(BF16)
    zr = dt_ref[...] + br_ref[...]
    dtc = _softplus(zr)
    a_row = -jnp.exp(ar_ref[...])
    acum = _exact_dot(lower_b, dtc * a_row, 1, 0, False)
    dtt = _softplus(dtt_ref[...] + bc_ref[...])
    acum_t = _exact_dot(dtt * (-jnp.exp(ac_ref[...])), upper_b, 1, 0, True)
    return lower, upper_b, zr, dtc, a_row, acum, acum_t


def _head_terms(h, lower, dtc, acum, acum_t):
    lane = lax.broadcasted_iota(jnp.int32, (1, LANE), 1)
    sub = lax.broadcasted_iota(jnp.int32, (SSD_H, 1), 0)
    rowid = lax.broadcasted_iota(jnp.int32, (CHUNK, 1), 0)
    oh = (lane == HEAD_LANE + h).astype(F32)
    acol = jnp.sum(acum * oh, axis=1, keepdims=True)
    dcol = jnp.sum(dtc * oh, axis=1, keepdims=True)
    arow = jnp.sum(acum_t * (sub == h).astype(F32), axis=0, keepdims=True)
    alast = jnp.sum(jnp.where(rowid == CHUNK - 1, acol, 0.0), axis=0, keepdims=True)
    decay = jnp.exp(jnp.where(lower, acol - arow, -jnp.inf))
    return oh, acol, dcol, alast, decay


SSD_PAIRS = SSD_H // 2
PAIRS_PER_GROUP = SSD_E // 2


def _ps(q):
    return slice(q * LANE, (q + 1) * LANE)


def _gs(off, g):
    return slice(off + g * SSD_N, off + (g + 1) * SSD_N)


def _lanes(c0, c1):
    return jnp.where(lax.broadcasted_iota(jnp.int32, (1, LANE), 1) < SSD_P, c0, c1)


def _rows(c0, c1):
    return jnp.where(lax.broadcasted_iota(jnp.int32, (LANE, 1), 0) < SSD_P, c0, c1)


def _lane_halves(t):
    first = lax.broadcasted_iota(jnp.int32, (1, LANE), 1) < SSD_P
    return (jnp.sum(jnp.where(first, t, 0.0), axis=1, keepdims=True),
            jnp.sum(jnp.where(first, 0.0, t), axis=1, keepdims=True))


def _ssd_in_specs(rev):
    def ci(c):
        return c if rev is None else rev - c
    return [pl.BlockSpec((CHUNK, CONV_DIM), lambda c: (ci(c), 0)),
            pl.BlockSpec((CHUNK, LANE), lambda c: (ci(c), 0)),
            pl.BlockSpec((SSD_H, CHUNK), lambda c: (0, ci(c))),
            pl.BlockSpec((1, LANE), lambda c: (0, 0)), pl.BlockSpec((SSD_H, 1), lambda c: (0, 0)),
            pl.BlockSpec((1, LANE), lambda c: (0, 0)), pl.BlockSpec((SSD_H, 1), lambda c: (0, 0)),
            pl.BlockSpec((SSD_PAIRS, 1, LANE), lambda c: (0, 0, 0))]


def _ssd_fwd(xbc, small, dtt, bias_r, bias_c, alog_r, alog_c, dsk, name="ssd_fwd"):
    s_ = xbc.shape[0]
    nc = s_ // CHUNK

    def body(x_ref, dt_ref, dtt_ref, br_ref, bc_ref, ar_ref, ac_ref, dsk_ref, y_ref, prev_ref, state):
        @pl.when(pl.program_id(0) == 0)
        def _():
            state[...] = jnp.zeros_like(state)

        lower, _, _, dtc, _, acum, acum_t = _ssd_chunk_common(dt_ref, dtt_ref, br_ref, bc_ref, ar_ref, ac_ref)
        for g in range(SSD_G):
            bb = _mx(x_ref[:, _gs(B_OFF, g)])
            cb_ = _mx(x_ref[:, _gs(C_OFF, g)])
            cbm = _dot(cb_, bb, 1, 1)
            for e in range(PAIRS_PER_GROUP):
                q = g * PAIRS_PER_GROUP + e
                _, acol0, dcol0, alast0, decay0 = _head_terms(2 * q, lower, dtc, acum, acum_t)
                _, acol1, dcol1, alast1, decay1 = _head_terms(2 * q + 1, lower, dtc, acum, acum_t)
                x = x_ref[:, _ps(q)]
                xdt = x * _lanes(dcol0, dcol1)
                xb = _mx(xdt)
                yd = _lanes(_dot(_mx(cbm * decay0), xb, 1, 0), _dot(_mx(cbm * decay1), xb, 1, 0))
                prev = state[q]
                prev_ref[0, q] = prev
                yo = _dot(cb_, _mx(prev), 1, 1) * _lanes(jnp.exp(acol0), jnp.exp(acol1))
                ds = _lanes(jnp.exp(alast0 - acol0), jnp.exp(alast1 - acol1))
                st = _dot(_mx(xdt * ds), bb, 0, 0)
                state[q] = prev * _rows(jnp.exp(alast0), jnp.exp(alast1)) + st
                y_ref[:, _ps(q)] = yd + yo + x * dsk_ref[q]

    psp = pl.BlockSpec((1, SSD_PAIRS, LANE, SSD_N), lambda c: (c, 0, 0, 0))
    return pl.pallas_call(
        body, name=name, grid=(nc,),
        in_specs=_ssd_in_specs(None), out_specs=[pl.BlockSpec((CHUNK, SSD_W), lambda c: (c, 0)), psp],
        out_shape=[jax.ShapeDtypeStruct((s_, SSD_W), F32),
                   jax.ShapeDtypeStruct((nc, SSD_PAIRS, LANE, SSD_N), F32)],
        scratch_shapes=[pltpu.VMEM((SSD_PAIRS, LANE, SSD_N), F32)],
        compiler_params=_params("arbitrary"),
    )(xbc, small, dtt, bias_r, bias_c, alog_r, alog_c, dsk)


def _ssd_bwd(xbc, small, dtt, bias_r, bias_c, alog_r, alog_c, dsk, prev, dy, name="ssd_bwd"):
    s_ = xbc.shape[0]
    nc = s_ // CHUNK

    def body(x_ref, dt_ref, dtt_ref, br_ref, bc_ref, ar_ref, ac_ref, dsk_ref, prev_ref, dy_ref,
             dx_ref, ddt_ref, dpar_ref, dstate):
        @pl.when(pl.program_id(0) == 0)
        def _():
            dstate[...] = jnp.zeros_like(dstate)
            dpar_ref[...] = jnp.zeros_like(dpar_ref)

        lower, upper_b, zr, dtc, a_row, acum, acum_t = _ssd_chunk_common(
            dt_ref, dtt_ref, br_ref, bc_ref, ar_ref, ac_ref)
        strict = (lax.broadcasted_iota(jnp.int32, (CHUNK, CHUNK), 1)
                  < lax.broadcasted_iota(jnp.int32, (CHUNK, CHUNK), 0))
        strict_b = strict.astype(BF16)
        col2 = lax.broadcasted_iota(jnp.int32, (CHUNK, 2 * CHUNK), 1)
        strict2 = (jnp.where(col2 >= CHUNK, col2 - CHUNK, col2)
                   < lax.broadcasted_iota(jnp.int32, (CHUNK, 2 * CHUNK), 0))
        da_in = jnp.zeros((CHUNK, LANE), F32)
        r_off = jnp.zeros((CHUNK, LANE), F32)
        c_int = jnp.zeros((CHUNK, LANE), F32)
        c_row = jnp.zeros((1, LANE), F32)
        ddt = jnp.zeros((CHUNK, LANE), F32)
        dskip = jnp.zeros((1, LANE), F32)
        for g in range(SSD_G):
            bb = _mx(x_ref[:, _gs(B_OFF, g)])
            cb_ = _mx(x_ref[:, _gs(C_OFF, g)])
            cbm = _dot(cb_, bb, 1, 1)
            dcb = jnp.zeros((CHUNK, CHUNK), F32)
            dc_acc = jnp.zeros((CHUNK, SSD_N), F32)
            db_acc = jnp.zeros((CHUNK, SSD_N), F32)
            for e in range(PAIRS_PER_GROUP):
                q = g * PAIRS_PER_GROUP + e
                oh0, acol0, dcol0, alast0, decay0 = _head_terms(2 * q, lower, dtc, acum, acum_t)
                oh1, acol1, dcol1, alast1, decay1 = _head_terms(2 * q + 1, lower, dtc, acum, acum_t)
                x = x_ref[:, _ps(q)]
                dy = dy_ref[:, _ps(q)]
                dcol = _lanes(dcol0, dcol1)
                xdt = x * dcol
                xb = _mx(xdt)
                eacol = _lanes(jnp.exp(acol0), jnp.exp(acol1))
                ds = _lanes(jnp.exp(alast0 - acol0), jnp.exp(alast1 - acol1))
                ealast = _rows(jnp.exp(alast0), jnp.exp(alast1))
                dyb = _mx(dy)
                dyb0, dyb1 = _mx(_lanes(dy, 0.0)), _mx(_lanes(0.0, dy))
                dsh = dstate[q]
                dshb = _mx(dsh)
                prev = prev_ref[0, q]
                prevb = _mx(prev)
                dxdt_inter = ds * _dot(bb, dshb, 1, 1)
                dxdt = _lanes(_dot(_mx(cbm * decay0), dyb, 0, 0), _dot(_mx(cbm * decay1), dyb, 0, 0)) + dxdt_inter
                dwl0 = _dot(dyb0, xb, 1, 1) * decay0
                dwl1 = _dot(dyb1, xb, 1, 1) * decay1
                dcb = dcb + dwl0 + dwl1
                dyeb = _mx(dy * eacol)
                dc_acc = dc_acc + _dot(dyeb, prevb, 1, 0)
                db_acc = db_acc + _dot(_mx(xdt * ds), dshb, 1, 0)
                dstate[q] = _dot(dyeb, cb_, 0, 0) + ealast * dsh
                above = _exact_dot(upper_b, jnp.concatenate([dwl0 * cbm, dwl1 * cbm], axis=1), 1, 0, False)
                above = jnp.where(strict2, above, 0.0)
                da_in = (da_in + jnp.sum(above[:, :CHUNK], axis=1, keepdims=True) * oh0
                         + jnp.sum(above[:, CHUNK:], axis=1, keepdims=True) * oh1)
                y_off = _dot(cb_, prevb, 1, 1) * eacol
                r0, r1 = _lane_halves(dy * y_off)
                r_off = r_off + r0 * oh0 + r1 * oh1
                c0, c1 = _lane_halves(xdt * dxdt_inter)
                c_int = c_int + c0 * oh0 + c1 * oh1
                both = jnp.sum(dsh * prev, axis=1, keepdims=True) * ealast
                c_row = (c_row + jnp.sum(_rows(both, 0.0), axis=0, keepdims=True) * oh0
                         + jnp.sum(_rows(0.0, both), axis=0, keepdims=True) * oh1)
                t0, t1 = _lane_halves(dxdt * x)
                ddt = ddt + t0 * oh0 + t1 * oh1
                dx_ref[:, _ps(q)] = dxdt * dcol + dy * dsk_ref[q]
                k0, k1 = _lane_halves(dy * x)
                dskip = (dskip + jnp.sum(k0, axis=0, keepdims=True) * oh0 + jnp.sum(k1, axis=0, keepdims=True) * oh1)
            dcbb = _mx(dcb)
            dx_ref[:, _gs(C_OFF, g)] = dc_acc + _dot(dcbb, bb, 1, 0)
            dx_ref[:, _gs(B_OFF, g)] = db_acc + _dot(dcbb, cb_, 0, 0)
        da = (da_in + _exact_dot(upper_b, r_off, 1, 0, False) + _exact_dot(strict_b, c_int, 1, 0, False) + c_row)
        draw = (ddt + da * a_row) * _sigmoid(zr)
        ddt_ref[...] = draw
        dpar_ref[0:1, :] += jnp.sum(draw, axis=0, keepdims=True)
        dpar_ref[1:2, :] += jnp.sum(da * dtc, axis=0, keepdims=True) * a_row
        dpar_ref[2:3, :] += dskip

    rev = nc - 1
    psp = pl.BlockSpec((1, SSD_PAIRS, LANE, SSD_N), lambda c: (rev - c, 0, 0, 0))
    return pl.pallas_call(
        body, name=name, grid=(nc,),
        in_specs=_ssd_in_specs(rev) + [psp, pl.BlockSpec((CHUNK, SSD_W), lambda c: (rev - c, 0))],
        out_specs=[pl.BlockSpec((CHUNK, CONV_DIM), lambda c: (rev - c, 0)),
                   pl.BlockSpec((CHUNK, LANE), lambda c: (rev - c, 0)), pl.BlockSpec((8, LANE), lambda c: (0, 0))],
        out_shape=[jax.ShapeDtypeStruct((s_, CONV_DIM), F32), jax.ShapeDtypeStruct((s_, LANE), F32),
                   jax.ShapeDtypeStruct((8, LANE), F32)],
        scratch_shapes=[pltpu.VMEM((SSD_PAIRS, LANE, SSD_N), F32)],
        compiler_params=_params("arbitrary"),
    )(xbc, small, dtt, bias_r, bias_c, alog_r, alog_c, dsk, prev, dy)


GN = SSD_W // SSD_G


def _gated_norm_fwd(y, z, w, cat, name="gated_norm_fwd"):
    s_, f = y.shape
    tr = _row_tile(s_)

    def body(y_ref, z_ref, w_ref, cat_ref, o_ref):
        for g in range(SSD_G):
            sl = slice(g * GN, (g + 1) * GN)
            gg = y_ref[:, sl] * _silu(z_ref[:, sl])
            r = lax.rsqrt(jnp.mean(gg * gg, axis=-1, keepdims=True) + EPS)
            o_ref[:, sl] = (gg * r * w_ref[:, sl]).astype(o_ref.dtype)

    row = pl.BlockSpec((tr, f), lambda i: (i, 0))
    wsp = pl.BlockSpec((1, f), lambda i: (0, 0))
    return pl.pallas_call(
        body, name=name, grid=(s_ // tr,),
        in_specs=[row, row, wsp, pl.BlockSpec(memory_space=pl.ANY)], out_specs=pl.BlockSpec((tr, f), lambda i: (i, 1)),
        out_shape=jax.ShapeDtypeStruct(cat.shape, cat.dtype), input_output_aliases={3: 0},
        compiler_params=_params("parallel"),
    )(y, z, w.reshape(1, f), cat)


def _gated_norm_bwd(y, z, w, dout, name="gated_norm_bwd"):
    s_, f = y.shape
    tr = _row_tile(s_)

    def body(y_ref, z_ref, w_ref, do_ref, dy_ref, dz_ref, dw_ref):
        @pl.when(pl.program_id(0) == 0)
        def _():
            dw_ref[...] = jnp.zeros_like(dw_ref)

        for g in range(SSD_G):
            sl = slice(g * GN, (g + 1) * GN)
            yv = y_ref[:, sl]
            zv = z_ref[:, sl]
            dov = do_ref[:, sl].astype(F32)
            sz = _silu(zv)
            gg = yv * sz
            r = lax.rsqrt(jnp.mean(gg * gg, axis=-1, keepdims=True) + EPS)
            gw = dov * w_ref[:, sl]
            c = jnp.mean(gw * gg, axis=-1, keepdims=True)
            dgg = r * gw - gg * (r * r * r * c)
            dy_ref[:, sl] = dgg * sz
            dz_ref[:, sl] = (dgg * yv * _dsilu(zv)).astype(dz_ref.dtype)
            dw_ref[:, sl] += jnp.sum(dov * gg * r, axis=0, keepdims=True)

    row = pl.BlockSpec((tr, f), lambda i: (i, 0))
    wsp = pl.BlockSpec((1, f), lambda i: (0, 0))
    return pl.pallas_call(
        body, name=name, grid=(s_ // tr,),
        in_specs=[row, row, wsp, pl.BlockSpec((tr, f), lambda i: (i, 1))], out_specs=[row, row, wsp],
        out_shape=[jax.ShapeDtypeStruct((s_, f), F32), jax.ShapeDtypeStruct((s_, f), MXU_DTYPE),
                   jax.ShapeDtypeStruct((1, f), F32)],
        compiler_params=_params("arbitrary"),
    )(y, z, w.reshape(1, f), dout)


def _ffn_fwd(vv, w_gate, w_up, name="ffn_gate_up"):
    s_, d = vv.shape
    nb, f8, _ = w_gate.shape
    tm = _pick(s_, (1024, 512, 256, 128))

    def body(v_ref, wg_ref, wu_ref, g_ref, u_ref, a_ref):
        for rs in _row_slices(tm, 16):
            a = _mx(v_ref[rs, :])
            g = _dot(a, _mx(wg_ref[...]), 1, 1)
            u = _dot(a, _mx(wu_ref[...]), 1, 1)
            s = _sigmoid(g)
            gs = g * s
            g_ref[rs, :] = (u * (s * (1.0 + g * (1.0 - s)))).astype(g_ref.dtype)
            u_ref[rs, :] = gs.astype(u_ref.dtype)
            a_ref[rs, :] = (gs * u).astype(a_ref.dtype)

    wsp = pl.BlockSpec((None, f8, d), lambda j, i: (j, 0, 0))
    osp = pl.BlockSpec((None, tm, f8), lambda j, i: (j, i, 0))
    return pl.pallas_call(
        body, name=name, grid=(nb, s_ // tm),
        in_specs=[pl.BlockSpec((tm, d), lambda j, i: (i, 0)), wsp, wsp], out_specs=[osp] * 3,
        out_shape=[jax.ShapeDtypeStruct((nb, s_, f8), MXU_DTYPE)] * 3,
        compiler_params=_params("parallel", "parallel"),
    )(vv, w_gate, w_up)


def _ffn_bwd_act(dffn, w_down, gate, up, name="ffn_d_act"):
    s_, d = dffn.shape
    nb, f8, _ = w_down.shape
    tm = _pick(s_, (1024, 512, 256, 128))

    def body(d_ref, w_ref, g_ref, u_ref, dg_ref, du_ref):
        for rs in _row_slices(tm, 16):
            dact = _dot(_mx(d_ref[rs, :]), _mx(w_ref[...]), 1, 1)
            dg_ref[rs, :] = (dact * g_ref[rs, :].astype(F32)).astype(dg_ref.dtype)
            du_ref[rs, :] = (dact * u_ref[rs, :].astype(F32)).astype(du_ref.dtype)

    osp = pl.BlockSpec((None, tm, f8), lambda j, i: (j, i, 0))
    return pl.pallas_call(
        body, name=name, grid=(nb, s_ // tm),
        in_specs=[pl.BlockSpec((tm, d), lambda j, i: (i, 0)), pl.BlockSpec((None, f8, d), lambda j, i: (j, 0, 0)),
                  osp, osp],
        out_specs=[osp, osp], out_shape=[jax.ShapeDtypeStruct((nb, s_, f8), MXU_DTYPE)] * 2,
        compiler_params=_params("parallel", "parallel"),
    )(dffn, w_down, gate, up)


def _ffn_bwd_in(dgate, w_gate, dup, w_up, name="ffn_d_in"):
    nb, s_, f8 = dgate.shape
    d = w_gate.shape[2]
    tm = _pick(s_, (1024, 512, 256, 128))
    tn = _pick(d, (1024, 512, 256, 128))

    def body(dg_ref, wg_ref, du_ref, wu_ref, o_ref, acc):
        j = pl.program_id(2)

        @pl.when(j == 0)
        def _():
            acc[...] = jnp.zeros_like(acc)

        for rs in _row_slices(tm, 16):
            acc[rs, :] += (_dot(_mx(dg_ref[rs, :]), _mx(wg_ref[...]), 1, 0)
                           + _dot(_mx(du_ref[rs, :]), _mx(wu_ref[...]), 1, 0))

        @pl.when(j == nb - 1)
        def _():
            o_ref[...] = acc[...]

    asp = pl.BlockSpec((None, tm, f8), lambda i, n, j: (j, i, 0))
    wsp = pl.BlockSpec((None, f8, tn), lambda i, n, j: (j, 0, n))
    return pl.pallas_call(
        body, name=name, grid=(s_ // tm, d // tn, nb),
        in_specs=[asp, wsp, asp, wsp], out_specs=pl.BlockSpec((tm, tn), lambda i, n, j: (i, n)),
        out_shape=jax.ShapeDtypeStruct((s_, d), F32), scratch_shapes=[pltpu.VMEM((tm, tn), F32)],
        compiler_params=_params("parallel", "parallel", "arbitrary"),
    )(dgate, w_gate, dup, w_up)


def _adam_math(g, w, m, v):
    m2 = ADAM_B1 * m + (1.0 - ADAM_B1) * g
    v2 = ADAM_B2 * v + (1.0 - ADAM_B2) * (g * g)
    m_hat = m2 / (1.0 - ADAM_B1 ** ADAM_STEP)
    v_hat = v2 / (1.0 - ADAM_B2 ** ADAM_STEP)
    delta = -ADAM_LR * (m_hat / (jnp.sqrt(v_hat) + ADAM_EPS) + ADAM_WD * w)
    return delta, m2, v2


def _adamw(parts, w, m, v, name="adamw"):
    nd, r_, c = parts.shape
    tr = _pick(r_, (128, 64, 32, 16))
    tc = c
    if tr == r_ and r_ > 128:
        tc = _pick(c, (256, 128))

    def body(p_ref, w_ref, m_ref, v_ref, g_ref, d_ref, m2_ref, v2_ref):
        g = p_ref[0].astype(F32)
        for i in range(1, nd):
            g = g + p_ref[i].astype(F32)
        delta, m2, v2 = _adam_math(g, w_ref[...], m_ref[...], v_ref[...])
        g_ref[...] = g
        d_ref[...] = delta
        m2_ref[...] = m2
        v2_ref[...] = v2

    row = pl.BlockSpec((tr, tc), lambda i, j: (i, j))
    psp = pl.BlockSpec((nd, tr, tc), lambda i, j: (0, i, j))
    return pl.pallas_call(
        body, name=name, grid=(r_ // tr, c // tc), in_specs=[psp, row, row, row], out_specs=[row] * 4,
        out_shape=[jax.ShapeDtypeStruct((r_, c), F32)] * 4, compiler_params=_params("parallel", "parallel"),
    )(parts, w, m, v)


def _adamw_small(parts, w, m, v, name="adamw_small"):
    nd = parts.shape[0]

    def body(p_ref, w_ref, m_ref, v_ref, g_ref, d_ref, m2_ref, v2_ref):
        g = p_ref[0]
        for i in range(1, nd):
            g = g + p_ref[i]
        delta, m2, v2 = _adam_math(g, w_ref[...], m_ref[...], v_ref[...])
        g_ref[...] = g
        d_ref[...] = delta
        m2_ref[...] = m2
        v2_ref[...] = v2

    return pl.pallas_call(
        body, name=name, out_shape=[jax.ShapeDtypeStruct(w.shape, F32)] * 4,
        compiler_params=pltpu.CompilerParams(vmem_limit_bytes=VMEM_LIMIT_BYTES),
    )(parts, w, m, v)


_HBM = pl.BlockSpec(memory_space=pltpu.HBM)
_MESH = pl.DeviceIdType.MESH


def _all_gather(xs, name):
    na = len(xs)

    def body(*refs):
        x_refs, out_refs = refs[:na], refs[na:2 * na]
        send_sems, recv_sems, local_sems = refs[2 * na:]
        x, y, c = lax.axis_index("x"), lax.axis_index("y"), lax.axis_index("c")
        me, sibling = (x, y, c), (x, y, 1 - c)
        chips = [(1 - x, y), (x, 1 - y), (1 - x, 1 - y)]

        def slot(a, px, py, pc):
            return out_refs[a].at[4 * px + 2 * py + pc]

        def copy(a, k, block, to, src=None):
            return pltpu.make_async_remote_copy(
                src_ref=slot(a, *block) if src is None else src, dst_ref=slot(a, *block),
                send_sem=send_sems.at[a, k], recv_sem=recv_sems.at[a, k], device_id=to, device_id_type=_MESH)

        mine = [pltpu.make_async_copy(x_refs[a], slot(a, *me), local_sems.at[a]) for a in range(na)]
        started = []
        for a in range(na):
            mine[a].start()
            first = [copy(a, 0, me, sibling, src=x_refs[a])]
            first += [copy(a, 1 + j, me, (*chip, c), src=x_refs[a]) for j, chip in enumerate(chips)]
            for cp in first:
                cp.start()
            started += first
        for a in range(na):
            for j, chip in enumerate(chips):
                copy(a, 1 + j, (*chip, c), me).wait_recv()
                fwd = copy(a, 4 + j, (*chip, c), sibling)
                fwd.start()
                started.append(fwd)
        for a in range(na):
            copy(a, 0, sibling, me).wait_recv()
            for j, chip in enumerate(chips):
                copy(a, 4 + j, (*chip, 1 - c), me).wait_recv()
        for cp in started:
            cp.wait_send()
        for cp in mine:
            cp.wait()

    return pl.pallas_call(
        body, name=name, out_shape=[jax.ShapeDtypeStruct((N_DEV,) + t.shape, t.dtype) for t in xs],
        in_specs=[_HBM] * na, out_specs=[_HBM] * na,
        scratch_shapes=[pltpu.SemaphoreType.DMA((na, 7)), pltpu.SemaphoreType.DMA((na, 7)),
                        pltpu.SemaphoreType.DMA((na,))],
    )(*xs)


_SEM = pl.BlockSpec(memory_space=pltpu.SEMAPHORE)
_EFFECT = pltpu.SideEffectType.DATAFLOW_SIDE_EFFECTING


def _peers(x, y, c):
    out = []
    for k in range(1, N_DEV):
        px = 1 - x if k & 4 else x
        py = 1 - y if k & 2 else y
        pc = 1 - c if k & 1 else c
        out.append(((px, py, pc), 4 * px + 2 * py + pc))
    return out


def _push_copies(scatter, src_refs, land_refs, send_sems, recv_sems):
    x, y, c = lax.axis_index("x"), lax.axis_index("y"), lax.axis_index("c")
    me = 4 * x + 2 * y + c
    pairs = []
    for a, (src, land) in enumerate(zip(src_refs, land_refs)):
        for k, (peer, slot) in enumerate(_peers(x, y, c)):
            out_src = src.at[slot] if scatter else src
            si = a * (N_DEV - 1) + k
            send = pltpu.make_async_remote_copy(src_ref=out_src, dst_ref=land.at[me], send_sem=send_sems.at[si],
                                                recv_sem=recv_sems.at[si], device_id=peer, device_id_type=_MESH)
            recv = pltpu.make_async_remote_copy(src_ref=out_src, dst_ref=land.at[slot], send_sem=send_sems.at[si],
                                                recv_sem=recv_sems.at[si], device_id=peer, device_id_type=_MESH)
            pairs.append((send, recv))
    return pairs


def _push_start(srcs, scatter, dep, name):
    na = len(srcs)
    shapes = [t.shape[1:] if scatter else t.shape for t in srcs]
    lands = [pltpu.with_memory_space_constraint(lax.empty((N_DEV,) + s, t.dtype), pltpu.HBM) for s, t in zip(shapes, srcs)]

    def body(*refs):
        src_refs, land_refs = refs[:na], refs[na:2 * na]
        send_sems, recv_sems = refs[2 * na + 1], refs[2 * na + 2]
        token = refs[-1]
        for send, _ in _push_copies(scatter, src_refs, land_refs, send_sems, recv_sems):
            send.start()
        token[...] = jnp.zeros_like(token)

    sem = pltpu.SemaphoreType.DMA((na * (N_DEV - 1),))
    outs = pl.pallas_call(
        body, name=name,
        out_shape=(sem, sem) + tuple(pltpu.HBM(t.shape, t.dtype) for t in srcs)
        + tuple(pltpu.HBM(t.shape, t.dtype) for t in lands) + (jax.ShapeDtypeStruct((8, LANE), F32),),
        in_specs=[_HBM] * (2 * na) + [pl.BlockSpec(memory_space=pl.ANY)],
        out_specs=(_SEM, _SEM) + (_HBM,) * (2 * na) + (pl.BlockSpec(memory_space=pltpu.VMEM),),
        input_output_aliases={i: 2 + i for i in range(2 * na)},
        compiler_params=pltpu.CompilerParams(has_side_effects=_EFFECT),
    )(*[pltpu.with_memory_space_constraint(t, pltpu.HBM) for t in srcs], *lands, dep)
    return outs[0], outs[1], outs[2:2 + na], outs[2 + na:2 + 2 * na], outs[-1]


def _push_wait(send_sems, recv_sems, src_thru, land_thru, scatter, after, name):
    na = len(src_thru)

    def body(*refs):
        src_refs, land_refs = refs[:na], refs[na:2 * na]
        ssem, rsem = refs[2 * na], refs[2 * na + 1]
        for send, recv in _push_copies(scatter, src_refs, land_refs, ssem, rsem):
            send.wait_send()
            recv.wait_recv()

    outs = pl.pallas_call(
        body, name=name,
        out_shape=tuple(pltpu.HBM(t.shape, t.dtype) for t in src_thru) + tuple(pltpu.HBM(t.shape, t.dtype) for t in land_thru),
        in_specs=[_HBM] * (2 * na) + [_SEM, _SEM, pl.BlockSpec(memory_space=pl.ANY)],
        out_specs=(_HBM,) * (2 * na),
        input_output_aliases={i: i for i in range(2 * na)},
        compiler_params=pltpu.CompilerParams(has_side_effects=_EFFECT),
    )(*src_thru, *land_thru, send_sems, recv_sems, after)
    return outs[:na], outs[na:]


def _exchange_behind(srcs, scatter, dep, name):
    send_sems, recv_sems, thru, lands, token = _push_start(srcs, scatter, dep, name + "_start")

    def finish(after):
        src_done, land_done = _push_wait(send_sems, recv_sems, thru, lands, scatter, after, name + "_wait")
        return _place_own(land_done, src_done, scatter, name + "_own")

    return token[0, 0], finish


def _place_own(lands, srcs, scatter, name):
    me = (4 * lax.axis_index("x") + 2 * lax.axis_index("y") + lax.axis_index("c")).astype(jnp.int32).reshape(1)
    outs = []
    for a, (land, src) in enumerate(zip(lands, srcs)):
        r_, c_ = land.shape[1:]
        tr = _pick(r_, (512, 256, 128, 64, 32, 16))

        def body(me_ref, land_ref, src_ref, out_ref):
            out_ref[...] = src_ref[...]

        src_spec = (pl.BlockSpec((None, tr, c_), lambda i, me_: (me_[0], i, 0)) if scatter
                    else pl.BlockSpec((tr, c_), lambda i, me_: (i, 0)))
        gs = pltpu.PrefetchScalarGridSpec(
            num_scalar_prefetch=1, grid=(r_ // tr,),
            in_specs=[pl.BlockSpec(memory_space=pl.ANY), src_spec],
            out_specs=pl.BlockSpec((None, tr, c_), lambda i, me_: (me_[0], i, 0)))
        outs.append(pl.pallas_call(
            body, name=f"{name}_{a}", grid_spec=gs, out_shape=jax.ShapeDtypeStruct(land.shape, land.dtype),
            input_output_aliases={1: 0}, compiler_params=_params("arbitrary"),
        )(me, land, src))
    return outs


_BIG = (("w_in", D_MODEL, D_IN, 1), ("w_uq", Q_RANK, HEADS * QK, 1), ("w_ukv", KV_RANK, HEADS * (NOPE + VDIM), 1),
        ("w_out", D_MODEL, D_MODEL, 0), ("w_gate", D_MODEL, D_FF, 1), ("w_up", D_MODEL, D_FF, 1),
        ("w_down", D_FF, D_MODEL, 0))
_TRANSPOSED = ("w_in", "w_uq", "w_gate", "w_up")
_CQKV = (0, Q_RANK + KV_RANK)
_KR = (_CQKV[1], _CQKV[1] + ROPE)
_Z = (_KR[1], _KR[1] + SSD_W)
_XBC = (_Z[1], _Z[1] + CONV_DIM)
_DT = (_XBC[1], _XBC[1] + SSD_H)


def _win_segments(w_in_t):
    w = w_in_t.reshape(D_IN, D_MODEL)
    small = jnp.concatenate([w[_KR[0]:_KR[1]], w[_DT[0]:_DT[1]],
                             jnp.zeros((LANE - ROPE - SSD_H, D_MODEL), w.dtype)], axis=0)
    return w[_CQKV[0]:_CQKV[1]], w[_Z[0]:_Z[1]], w[_XBC[0]:_XBC[1]], small


def _win_from_segments(g_cqkv, g_z, g_xbc, g_small):
    w = jnp.concatenate([g_cqkv, g_small[:ROPE], g_z, g_xbc, g_small[ROPE:ROPE + SSD_H]], axis=0)
    return w.reshape(N_DEV, D_IN // N_DEV, D_MODEL)


_SMALL = (("q_norm_w", 512), ("kv_norm_w", 512), ("conv_b", CONV_DIM), ("dt_bias", SSD_H), ("a_log", SSD_H),
          ("d_skip", SSD_H), ("ssd_norm_w", SSD_W), ("attn_out_norm_w", 1024), ("pre_mix_norm_w", D_MODEL),
          ("post_mix_norm_w", D_MODEL), ("pre_ffn_norm_w", D_MODEL), ("post_ffn_norm_w", D_MODEL),
          ("conv_w", CONV_K * CONV_DIM))
_SMALL_ROWS = -(-sum(-(-n // LANE) for _, n in _SMALL) // 8) * 8


def _pack_small(vals):
    rows = []
    for name, n in _SMALL:
        v = vals[name].reshape(-1).astype(F32)
        pad = -(-n // LANE) * LANE
        rows.append(jnp.pad(v, (0, pad - n)).reshape(-1, LANE))
    m = jnp.concatenate(rows, axis=0)
    return jnp.pad(m, ((0, _SMALL_ROWS - m.shape[0]), (0, 0)))


def _unpack_small(m):
    out, r = {}, 0
    for name, n in _SMALL:
        nr = -(-n // LANE)
        out[name] = m[r:r + nr].reshape(-1)[:n]
        r += nr
    return out


def _head_row(v):
    return jnp.pad(v.reshape(1, -1).astype(F32), ((0, 0), (HEAD_LANE, LANE - HEAD_LANE - v.shape[-1])))


def _local_step(x, positions, target, wg, small, weights, on_grads):
    w_cqkv, w_z, w_xbc, w_small = _win_segments(wg["w_in"])
    conv_w = wg["conv_w"]
    conv_b = small["conv_b"].reshape(1, CONV_DIM)
    qkv_norm_w = jnp.concatenate([small["q_norm_w"], small["kv_norm_w"]])
    attn_norm_w = small["attn_out_norm_w"].reshape(1, HEADS * VDIM)
    scale = QK ** -0.5

    inv_freq = ROPE_THETA ** (-jnp.arange(0, ROPE, 2, dtype=F32) / ROPE)
    ang = positions.astype(F32)[:, None] * inv_freq
    cos2 = jnp.tile(jnp.cos(ang), (1, 2))
    sin2 = jnp.tile(jnp.sin(ang), (1, 2))

    u = _rms_fwd(x, small["pre_mix_norm_w"], out_dtype=MXU_DTYPE, name="pre_mix_norm")
    cqkv = _mm(u, w_cqkv, "nt", name="in_proj_qkv")
    z = _mm(u, w_z, "nt", name="in_proj_z")
    xbc = _mm(u, w_xbc, "nt", name="in_proj_xbc")
    sm = _mm(u, w_small, "nt", name="in_proj_small")

    w_uq, w_ukv = weights("qkv_up", cqkv)
    qkvn = _rms_fwd(cqkv, qkv_norm_w, groups=2, out_dtype=MXU_DTYPE, name="qkv_norm")
    q_h = _q_up(qkvn, w_uq, cos2, sin2, scale)
    k_h, v_h = _kv_up(qkvn, w_ukv, sm, cos2, sin2)
    o_h, lse = _flash_fwd(q_h, k_h, v_h)
    cat = _hnorm_fwd(o_h, attn_norm_w, D_MODEL)
    w_out = weights("out", o_h)[0].reshape(D_MODEL, D_MODEL)

    xbc_act = _conv_fwd(xbc, conv_w, conv_b)
    dtt = jnp.transpose(sm[:, HEAD_LANE:HEAD_LANE + SSD_H])
    ssd_args = (xbc_act, sm, dtt, _head_row(small["dt_bias"]), small["dt_bias"].reshape(SSD_H, 1),
                _head_row(small["a_log"]), small["a_log"].reshape(SSD_H, 1),
                jnp.broadcast_to(small["d_skip"].reshape(SSD_H, 1), (SSD_H, SSD_P)).reshape(SSD_PAIRS, 1, LANE))
    y_ssd, prev = _ssd_fwd(*ssd_args)
    cat = _gated_norm_fwd(y_ssd, z, small["ssd_norm_w"], cat)

    mix = _mm(cat, w_out, "nn", name="out_proj")
    h1, vv = _norm_res_norm(mix, x, small["post_mix_norm_w"], small["pre_ffn_norm_w"])

    w_gate, w_up = weights("ffn_in", mix)
    gate, up, act = _ffn_fwd(vv, w_gate, w_up)
    w_down, = weights("ffn_out", act)
    ffn = _mm(act, w_down, "nn", a_blk=True, b_blk=True, fuse=2, name="ffn_down")
    loss_blk, dy, dffn, g_post_ffn = _loss_head(ffn, h1, target, small["post_ffn_norm_w"])

    g_down = _mm(act, dffn, "tn", a_blk=True, out_blk=True, out_dtype=MXU_DTYPE, name="g_down")
    dgate, dup = _ffn_bwd_act(dffn, w_down, gate, up)
    dvv = _ffn_bwd_in(dgate, w_gate, dup, w_up)
    g_gate = _mm(dgate, vv, "tn", a_blk=True, out_blk=True, out_dtype=MXU_DTYPE, name="g_gate")
    g_up = _mm(dup, vv, "tn", a_blk=True, out_blk=True, out_dtype=MXU_DTYPE, name="g_up")
    pre_ffn_w = small["pre_ffn_norm_w"] + on_grads("ffn", [g_gate, g_up, g_down])
    dh1, dmix, g_pre_ffn, g_post_mix = _norm_res_norm_bwd(h1, pre_ffn_w, dvv, dy, mix, small["post_mix_norm_w"])

    dcat = _mm(dmix, w_out, "nt", name="d_cat")
    g_out = _mm(cat, dmix, "tn", out_dtype=MXU_DTYPE, name="g_out")

    do_h, delta, g_attn_norm = _hnorm_bwd(o_h, attn_norm_w, dcat)
    dq_h, dk_h, dv_h = _flash_bwd(q_h, k_h, v_h, do_h, lse, delta)
    dq = _q_prep(dq_h, cos2, -sin2, scale, name="dq_post")

    dy_ssd, dz, g_ssd_norm = _gated_norm_bwd(y_ssd, z, small["ssd_norm_w"], dcat)
    dxbc_act, ddt, dpar = _ssd_bwd(*ssd_args, prev, dy_ssd)
    dkv, dsm = _dkv_post(dk_h, dv_h, ddt, cos2, -sin2)
    dpre, dwb = _conv_bwd_pre(xbc, conv_w, conv_b, dxbc_act)
    dxbc = _conv_bwd_in(dpre, conv_w)

    dqn = _mm(dq, w_uq, "nn", a_blk=True, b_blk=True, fuse=HEADS, name="d_qn")
    dkvn = _mm(dkv, w_ukv, "nt", a_blk=True, b_blk=True, fuse=HEADS, name="d_kvn")
    g_uq = _mm(dq, qkvn, "tn", a_blk=True, out_blk=True, b_cols=(0, Q_RANK), out_dtype=MXU_DTYPE, name="g_uq")
    g_ukv = _mm(qkvn, dkv, "tn", b_blk=True, out_blk=True, a_cols=(Q_RANK, KV_RANK), out_dtype=MXU_DTYPE, name="g_ukv")
    heads_token = on_grads("heads", [g_uq, g_ukv, g_out.reshape(N_DEV, D_MODEL // N_DEV, D_MODEL)])
    dcqkv, g_qkv_norm = _rms_bwd(cqkv, qkv_norm_w + heads_token, [dqn, dkvn], out_dtype=MXU_DTYPE, name="qkv_norm_bwd")

    g_in = _win_from_segments(_mm(dcqkv, u, "tn", out_dtype=MXU_DTYPE, name="g_in_qkv"),
                              _mm(dz, u, "tn", out_dtype=MXU_DTYPE, name="g_in_z"),
                              _mm(dxbc, u, "tn", out_dtype=MXU_DTYPE, name="g_in_xbc"),
                              _mm(dsm, u, "tn", out_dtype=MXU_DTYPE, name="g_in_small"))
    in_token = on_grads("in", [g_in])
    du = _mm_sum([dsm + in_token.astype(dsm.dtype), dcqkv, dz, dxbc], [w_small, w_cqkv, w_z, w_xbc], name="d_u")
    dx, g_pre_mix = _rms_bwd(x, small["pre_mix_norm_w"], [du], res=dh1, name="pre_mix_norm_bwd")

    hl = slice(HEAD_LANE, HEAD_LANE + SSD_H)
    g_small = {"q_norm_w": g_qkv_norm[0, :Q_RANK], "kv_norm_w": g_qkv_norm[0, Q_RANK:], "conv_b": dwb[CONV_K],
               "dt_bias": dpar[0, hl], "a_log": dpar[1, hl], "d_skip": dpar[2, hl], "ssd_norm_w": g_ssd_norm,
               "attn_out_norm_w": g_attn_norm, "pre_mix_norm_w": g_pre_mix, "post_mix_norm_w": g_post_mix,
               "pre_ffn_norm_w": g_pre_ffn, "post_ffn_norm_w": g_post_ffn, "conv_w": dwb[:CONV_K]}
    return loss_blk[0, 0], dx, g_small


_WEIGHT_ORDER = ("w_in", "q_norm_w", "w_uq", "kv_norm_w", "w_ukv", "conv_w", "conv_b", "dt_bias", "a_log", "d_skip",
                 "ssd_norm_w", "attn_out_norm_w", "w_out", "pre_mix_norm_w", "post_mix_norm_w", "pre_ffn_norm_w",
                 "post_ffn_norm_w", "w_gate", "w_up", "w_down")


def kernel(x, positions, w_in, q_norm_w, w_uq, kv_norm_w, w_ukv, conv_w, conv_b, dt_bias, a_log, d_skip, ssd_norm_w, attn_out_norm_w, w_out, pre_mix_norm_w, post_mix_norm_w, pre_ffn_norm_w, post_ffn_norm_w, w_gate, w_up, w_down, loss_target, m_w_in, m_q_norm_w, m_w_uq, m_kv_norm_w, m_w_ukv, m_conv_w, m_conv_b, m_dt_bias, m_a_log, m_d_skip, m_ssd_norm_w, m_attn_out_norm_w, m_w_out, m_pre_mix_norm_w, m_post_mix_norm_w, m_pre_ffn_norm_w, m_post_ffn_norm_w, m_w_gate, m_w_up, m_w_down, v_w_in, v_q_norm_w, v_w_uq, v_kv_norm_w, v_w_ukv, v_conv_w, v_conv_b, v_dt_bias, v_a_log, v_d_skip, v_ssd_norm_w, v_attn_out_norm_w, v_w_out, v_pre_mix_norm_w, v_post_mix_norm_w, v_pre_ffn_norm_w, v_post_ffn_norm_w, v_w_gate, v_w_up, v_w_down):
    w = dict(w_in=w_in, q_norm_w=q_norm_w, w_uq=w_uq, kv_norm_w=kv_norm_w, w_ukv=w_ukv, conv_w=conv_w, conv_b=conv_b,
             dt_bias=dt_bias, a_log=a_log, d_skip=d_skip, ssd_norm_w=ssd_norm_w, attn_out_norm_w=attn_out_norm_w,
             w_out=w_out, pre_mix_norm_w=pre_mix_norm_w, post_mix_norm_w=post_mix_norm_w,
             pre_ffn_norm_w=pre_ffn_norm_w, post_ffn_norm_w=post_ffn_norm_w, w_gate=w_gate, w_up=w_up, w_down=w_down)
    m = dict(w_in=m_w_in, q_norm_w=m_q_norm_w, w_uq=m_w_uq, kv_norm_w=m_kv_norm_w, w_ukv=m_w_ukv, conv_w=m_conv_w,
             conv_b=m_conv_b, dt_bias=m_dt_bias, a_log=m_a_log, d_skip=m_d_skip, ssd_norm_w=m_ssd_norm_w,
             attn_out_norm_w=m_attn_out_norm_w, w_out=m_w_out, pre_mix_norm_w=m_pre_mix_norm_w,
             post_mix_norm_w=m_post_mix_norm_w, pre_ffn_norm_w=m_pre_ffn_norm_w, post_ffn_norm_w=m_post_ffn_norm_w,
             w_gate=m_w_gate, w_up=m_w_up, w_down=m_w_down)
    v = dict(w_in=v_w_in, q_norm_w=v_q_norm_w, w_uq=v_w_uq, kv_norm_w=v_kv_norm_w, w_ukv=v_w_ukv, conv_w=v_conv_w,
             conv_b=v_conv_b, dt_bias=v_dt_bias, a_log=v_a_log, d_skip=v_d_skip, ssd_norm_w=v_ssd_norm_w,
             attn_out_norm_w=v_attn_out_norm_w, w_out=v_w_out, pre_mix_norm_w=v_pre_mix_norm_w,
             post_mix_norm_w=v_post_mix_norm_w, pre_ffn_norm_w=v_pre_ffn_norm_w, post_ffn_norm_w=v_post_ffn_norm_w,
             w_gate=v_w_gate, w_up=v_w_up, w_down=v_w_down)
    w, m, v = ({k: t[0] for k, t in d.items()} for d in (w, m, v))
    me = 4 * lax.axis_index("x") + 2 * lax.axis_index("y") + lax.axis_index("c")
    groups = {"qkv_up": ("w_uq", "w_ukv"), "out": ("w_out",), "ffn_in": ("w_gate", "w_up"), "ffn_out": ("w_down",)}
    cshard = CONV_DIM // N_DEV
    for name in _TRANSPOSED:
        w[name], m[name], v[name] = w[name].T, m[name].T, v[name].T

    shards = [w["w_in"].astype(MXU_DTYPE),
              jnp.stack(_split3(w["conv_w"])).reshape(3 * CONV_K, cshard).astype(MXU_DTYPE)]
    w_in_g, cw = _all_gather(shards, name="gather_weights")
    cw = cw.astype(F32).reshape(N_DEV, 3, CONV_K, cshard)
    wg = {"w_in": w_in_g, "conv_w": jnp.transpose(cw[:, 0] + cw[:, 1] + cw[:, 2], (1, 0, 2)).reshape(CONV_K, CONV_DIM)}
    arriving, dep, started = {}, wg["conv_w"], jnp.zeros((), F32)
    small = {name: w[name] for name, _ in _SMALL if name != "conv_w"}
    for group in ("qkv_up", "out", "ffn_in", "ffn_out"):
        token, arriving[group] = _exchange_behind([w[name].astype(MXU_DTYPE) for name in groups[group]], False,
                                                  dep, group + "_weights")
        started = started + token
        dep = jnp.zeros((8, LANE), F32) + started
    small["pre_mix_norm_w"] = small["pre_mix_norm_w"] + started

    leaving = {}

    def on_grads(group, gs):
        token, leaving[group] = _exchange_behind(gs, True, jnp.zeros((8, LANE), F32), group + "_grads")
        return token

    loss_local, dx, g_small = _local_step(x[0], positions[0], loss_target[0], wg, small,
                                          lambda group, after: arriving[group](after), on_grads)
    loss = lax.psum(loss_local, ("x", "y", "c"))

    recv = {}
    for group, names in (("ffn", ("w_gate", "w_up", "w_down")), ("heads", ("w_uq", "w_ukv", "w_out")), ("in", ("w_in",))):
        recv.update(zip(names, leaving[group](dx)))
    grads, deltas, new_m, new_v = {}, {}, {}, {}
    for name, parts in recv.items():
        outs = _adamw(parts, w[name], m[name], v[name], name="adamw_" + name)
        if name in _TRANSPOSED:
            outs = [t.T for t in outs]
        grads[name], deltas[name], new_m[name], new_v[name] = outs

    def embed(t):
        return lax.dynamic_update_slice(jnp.zeros((CONV_K, CONV_DIM), F32), t, (0, me * cshard))

    parts_s = _all_gather([_pack_small(g_small)], name="gather_small_grads")[0]
    packs = [_pack_small({**{n_: d[n_] for n_, _ in _SMALL if n_ != "conv_w"}, "conv_w": embed(d["conv_w"])})
             for d in (w, m, v)]
    outs = [_unpack_small(t) for t in _adamw_small(parts_s, *packs)]
    for name, n in _SMALL:
        for dst, src in zip((grads, deltas, new_m, new_v), outs):
            if name == "conv_w":
                dst[name] = lax.dynamic_slice(src[name].reshape(CONV_K, CONV_DIM), (0, me * cshard), (CONV_K, cshard))
            else:
                dst[name] = src[name]

    def lead(d):
        return [d[name][None] for name in _WEIGHT_ORDER]

    return (loss, dx[None], *lead(grads), *lead(deltas), *lead(new_m), *lead(new_v))
```

```python
import numpy as np

import jax
import jax.numpy as jnp
from jax import lax
from jax.experimental import pallas as pl
from jax.experimental.pallas import tpu as pltpu

F32 = jnp.float32
BF16 = jnp.bfloat16
MXU_DTYPE = jnp.bfloat16
EPS = 1e-6
VMEM_LIMIT_BYTES = 48 * 1024 * 1024
K_TILE_MAX = 2048

N_DEV = 8
D_MODEL = 2048
Q_RANK = 512
KV_RANK = 512
ROPE = 64
HALF = ROPE // 2
HEADS = 8
NOPE = 128
VDIM = 128
QK = NOPE + ROPE
SSD_W = 1024
SSD_H = 16
SSD_P = 64
SSD_G = 2
SSD_E = SSD_H // SSD_G
SSD_N = 128
CHUNK = 128
CONV_K = 4
CONV_DIM = SSD_W + 2 * SSD_G * SSD_N
B_OFF = SSD_W
C_OFF = SSD_W + SSD_G * SSD_N
D_FF = 5632
D_IN = Q_RANK + KV_RANK + ROPE + SSD_W + CONV_DIM + SSD_H
ROPE_THETA = 10000.0
LANE = 128
HEAD_LANE = ROPE

ADAM_LR = 0.001
ADAM_B1 = 0.9
ADAM_B2 = 0.999
ADAM_EPS = 1e-08
ADAM_WD = 0.01
ADAM_STEP = 10


def _pick(n, cands):
    for c in cands:
        if n % c == 0:
            return c
    return n


def _params(*sem):
    return pltpu.CompilerParams(dimension_semantics=sem, vmem_limit_bytes=VMEM_LIMIT_BYTES)


def _sigmoid(x):
    return 1.0 / (1.0 + jnp.exp(-x))


def _silu(x):
    return x * _sigmoid(x)


def _dsilu(x):
    s = _sigmoid(x)
    return s * (1.0 + x * (1.0 - s))


def _softplus(x):
    e = jnp.exp(-jnp.abs(x))
    small = e * (1.0 - e * (0.5 - e * (1.0 / 3.0)))
    return jnp.maximum(x, 0.0) + jnp.where(e < 0.01, small, jnp.log(1.0 + e))


def _dot(a, b, ca, cb):
    return lax.dot_general(a, b, (((ca,), (cb,)), ((), ())), preferred_element_type=F32)


def _mx(v):
    return v.astype(MXU_DTYPE)


def _split3(a):
    hi = a.astype(BF16)
    r1 = a - hi.astype(F32)
    mid = r1.astype(BF16)
    lo = (r1 - mid.astype(F32)).astype(BF16)
    return hi, mid, lo


def _exact_dot(a, b, ca, cb, split_a):
    if split_a:
        return sum(_dot(p, b, ca, cb) for p in _split3(a))
    return sum(_dot(a, p, ca, cb) for p in _split3(b))


MM_ROW_GROUPS = 4


def _row_slices(tm, align):
    ng = MM_ROW_GROUPS
    while ng > 1 and (tm % ng or (tm // ng) % align):
        ng //= 2
    return [slice(g * (tm // ng), (g + 1) * (tm // ng)) for g in range(ng)]


def _mm(a, b, mode, *, a_blk=False, b_blk=False, out_blk=False, a_cols=None, b_cols=None, add=None, out_dtype=F32,
        fuse=1, wide=False, name="mm"):
    a2, b2 = a.shape[-2:], b.shape[-2:]
    a_last = a2[1] if a_cols is None else a_cols[1]
    a_start = 0 if a_cols is None else a_cols[0]
    b_start = 0
    if b_cols is not None:
        assert mode != "nt"
        b_start, b2 = b_cols[0], (b2[0], b_cols[1])
    if mode == "nn":
        m, k, (k2, n) = a2[0], a_last, b2
    elif mode == "nt":
        m, k, (n, k2) = a2[0], a_last, b2
    else:
        k, m, (k2, n) = a2[0], a_last, b2
    assert k == k2, (a.shape, b.shape, mode)
    tm = _pick(m, (1024, 704, 512, 256, 128))
    tn = _pick(n, ((2048,) if wide else ()) + (1024, 768, 704, 512, 256, 192, 128))
    tk = k if k <= K_TILE_MAX else _pick(k, (K_TILE_MAX, 1024, 512))
    nk = k // tk
    jo = N_DEV if out_blk else 1
    reduce_blocks = a_blk and b_blk and not out_blk
    assert fuse == 1 or reduce_blocks
    jr = N_DEV // fuse if reduce_blocks else 1
    ca, cb = {"nn": (1, 0), "nt": (1, 1), "tn": (0, 0)}[mode]
    has_add = add is not None
    single = jr * nk == 1
    if mode == "tn":
        assert a_start % tm == 0
        a_block, a_idx = (tk, tm), (lambda i, kk: (kk, i + a_start // tm))
    else:
        assert a_start % tk == 0
        a_block, a_idx = (tm, tk), (lambda i, kk: (i, kk + a_start // tk))
    assert b_start % tn == 0
    b_block, b_idx = (((tn, tk), (lambda nn_, kk: (nn_, kk))) if mode == "nt"
                      else ((tk, tn), (lambda nn_, kk: (kk, nn_ + b_start // tn))))

    def blk_specs(blocked, block, idx, of_a, t):
        def pos(o, i, nn_, kk):
            return idx(i, kk) if of_a else idx(nn_, kk)
        if blocked:
            return pl.BlockSpec((None,) + block,
                                lambda o, i, nn_, r, kk: ((o if out_blk else r * fuse + t),) + pos(o, i, nn_, kk))
        return pl.BlockSpec(block, lambda o, i, nn_, r, kk: pos(o, i, nn_, kk))

    a_specs = [blk_specs(a_blk, a_block, a_idx, True, t) for t in range(fuse)]
    b_specs = [blk_specs(b_blk, b_block, b_idx, False, t) for t in range(fuse)]
    o_spec = (pl.BlockSpec((None, tm, tn), lambda o, i, nn_, r, kk: (o, i, nn_)) if out_blk
              else pl.BlockSpec((tm, tn), lambda o, i, nn_, r, kk: (i, nn_)))

    groups = _row_slices(tm, LANE if mode == "tn" else 16)

    def body(*refs):
        a_refs, b_refs = refs[:fuse], refs[fuse:2 * fuse]
        add_ref = refs[2 * fuse] if has_add else None
        o_ref = refs[2 * fuse + 1] if has_add else refs[2 * fuse]

        def partial(rs):
            out = None
            for t in range(fuse):
                av = a_refs[t][:, rs] if mode == "tn" else a_refs[t][rs, :]
                d = _dot(_mx(av), _mx(b_refs[t][...]), ca, cb)
                out = d if out is None else out + d
            return out

        if single:
            for rs in groups:
                res = partial(rs)
                if has_add:
                    res = res + add_ref[rs, :]
                o_ref[rs, :] = res.astype(o_ref.dtype)
            return
        acc = refs[-1]
        r, kk = pl.program_id(3), pl.program_id(4)

        @pl.when(jnp.logical_and(r == 0, kk == 0))
        def _():
            acc[...] = jnp.zeros_like(acc)

        for rs in groups:
            acc[rs, :] += partial(rs)

        @pl.when(jnp.logical_and(r == jr - 1, kk == nk - 1))
        def _():
            res = acc[...]
            if has_add:
                res = res + add_ref[...]
            o_ref[...] = res.astype(o_ref.dtype)

    out_shape = ((N_DEV, m, n) if out_blk else (m, n))
    return pl.pallas_call(
        body, name=name, grid=(jo, m // tm, n // tn, jr, nk),
        in_specs=a_specs + b_specs + ([o_spec] if has_add else []), out_specs=o_spec,
        out_shape=jax.ShapeDtypeStruct(out_shape, out_dtype),
        scratch_shapes=[] if single else [pltpu.VMEM((tm, tn), F32)],
        compiler_params=_params("parallel", "parallel", "parallel", "arbitrary", "arbitrary"),
    )(*((a,) * fuse + (b,) * fuse + ((add,) if has_add else ())))


def _mm_sum(a_list, b_list, name="mm_sum"):
    m, n = a_list[0].shape[0], b_list[0].shape[1]
    ns = len(a_list)
    tm = _pick(m, (1024, 512, 256, 128))
    tn = _pick(n, (1024, 512, 256, 128))
    groups = _row_slices(tm, 16)

    def body(*refs):
        a_refs, b_refs, o_ref = refs[:ns], refs[ns:2 * ns], refs[2 * ns]
        for rs in groups:
            acc = _dot(_mx(a_refs[0][rs, :]), _mx(b_refs[0][...]), 1, 0)
            for s in range(1, ns):
                acc = acc + _dot(_mx(a_refs[s][rs, :]), _mx(b_refs[s][...]), 1, 0)
            o_ref[rs, :] = acc

    return pl.pallas_call(
        body, name=name, grid=(m // tm, n // tn),
        in_specs=([pl.BlockSpec((tm, a.shape[1]), lambda i, j: (i, 0)) for a in a_list]
                  + [pl.BlockSpec((b.shape[0], tn), lambda i, j: (0, j)) for b in b_list]),
        out_specs=pl.BlockSpec((tm, tn), lambda i, j: (i, j)),
        out_shape=jax.ShapeDtypeStruct((m, n), F32), compiler_params=_params("parallel", "parallel"),
    )(*a_list, *b_list)


def _row_tile(r_):
    return _pick(r_, (256, 128, 64, 32, 16, 8))


def _rms_fwd(t, w, groups=1, res=None, out_dtype=F32, name="rms_fwd"):
    r_, f = t.shape
    fg = f // groups
    tr = _row_tile(r_)
    has_res = res is not None

    def body(*refs):
        t_ref, w_ref = refs[0], refs[1]
        res_ref = refs[2] if has_res else None
        o_ref = refs[-1]
        for g in range(groups):
            sl = slice(g * fg, (g + 1) * fg)
            tv = t_ref[:, sl].astype(F32)
            r = lax.rsqrt(jnp.mean(tv * tv, axis=-1, keepdims=True) + EPS)
            y = tv * r * w_ref[:, sl]
            if has_res:
                y = y + res_ref[:, sl]
            o_ref[:, sl] = y.astype(o_ref.dtype)

    row = pl.BlockSpec((tr, f), lambda i: (i, 0))
    wsp = pl.BlockSpec((1, f), lambda i: (0, 0))
    return pl.pallas_call(
        body, name=name, grid=(r_ // tr,),
        in_specs=[row, wsp] + ([row] if has_res else []), out_specs=row,
        out_shape=jax.ShapeDtypeStruct((r_, f), out_dtype),
        compiler_params=_params("parallel"),
    )(*((t, w.reshape(1, f)) + ((res,) if has_res else ())))


def _rms_bwd(t, w, dys, res=None, out_dtype=F32, name="rms_bwd"):
    r_, f = t.shape
    groups = len(dys)
    fg = f // groups
    tr = _row_tile(r_)
    has_res = res is not None

    def body(*refs):
        t_ref, w_ref = refs[0], refs[1]
        dy_refs = refs[2:2 + groups]
        res_ref = refs[2 + groups] if has_res else None
        dt_ref, dw_ref = refs[-2], refs[-1]

        @pl.when(pl.program_id(0) == 0)
        def _():
            dw_ref[...] = jnp.zeros_like(dw_ref)

        for g in range(groups):
            sl = slice(g * fg, (g + 1) * fg)
            tv = t_ref[:, sl].astype(F32)
            dyv = dy_refs[g][...].astype(F32)
            r = lax.rsqrt(jnp.mean(tv * tv, axis=-1, keepdims=True) + EPS)
            gw = dyv * w_ref[:, sl]
            c = jnp.mean(gw * tv, axis=-1, keepdims=True)
            dt = r * gw - tv * (r * r * r * c)
            if has_res:
                dt = dt + res_ref[:, sl]
            dt_ref[:, sl] = dt.astype(dt_ref.dtype)
            dw_ref[:, sl] += jnp.sum(dyv * tv * r, axis=0, keepdims=True)

    row = pl.BlockSpec((tr, f), lambda i: (i, 0))
    grow = pl.BlockSpec((tr, fg), lambda i: (i, 0))
    wsp = pl.BlockSpec((1, f), lambda i: (0, 0))
    return pl.pallas_call(
        body, name=name, grid=(r_ // tr,),
        in_specs=[row, wsp] + [grow] * groups + ([row] if has_res else []), out_specs=[row, wsp],
        out_shape=[jax.ShapeDtypeStruct((r_, f), out_dtype), jax.ShapeDtypeStruct((1, f), F32)],
        compiler_params=_params("arbitrary"),
    )(*((t, w.reshape(1, f)) + tuple(dys) + ((res,) if has_res else ())))


def _norm_res_norm(t, res, w1, w2, name="post_mix_pre_ffn_norm"):
    r_, f = t.shape
    tr = _row_tile(r_)

    def body(t_ref, res_ref, w1_ref, w2_ref, h_ref, v_ref):
        tv = t_ref[...]
        h = res_ref[...] + tv * lax.rsqrt(jnp.mean(tv * tv, axis=-1, keepdims=True) + EPS) * w1_ref[...]
        h_ref[...] = h
        v_ref[...] = (h * lax.rsqrt(jnp.mean(h * h, axis=-1, keepdims=True) + EPS) * w2_ref[...]).astype(v_ref.dtype)

    row = pl.BlockSpec((tr, f), lambda i: (i, 0))
    wsp = pl.BlockSpec((1, f), lambda i: (0, 0))
    return pl.pallas_call(
        body, name=name, grid=(r_ // tr,), in_specs=[row, row, wsp, wsp], out_specs=[row, row],
        out_shape=[jax.ShapeDtypeStruct((r_, f), F32), jax.ShapeDtypeStruct((r_, f), MXU_DTYPE)],
        compiler_params=_params("parallel"),
    )(t, res, w1.reshape(1, f), w2.reshape(1, f))


def _norm_res_norm_bwd(h, w2, dv, dres, t, w1, name="pre_ffn_post_mix_norm_bwd"):
    r_, f = h.shape
    tr = _row_tile(r_)

    def body(h_ref, w2_ref, dv_ref, dres_ref, t_ref, w1_ref, dh_ref, dt_ref, dw2_ref, dw1_ref):
        @pl.when(pl.program_id(0) == 0)
        def _():
            dw2_ref[...] = jnp.zeros_like(dw2_ref)
            dw1_ref[...] = jnp.zeros_like(dw1_ref)

        def rms_bwd(tv, wv, dyv):
            r = lax.rsqrt(jnp.mean(tv * tv, axis=-1, keepdims=True) + EPS)
            gw = dyv * wv
            c = jnp.mean(gw * tv, axis=-1, keepdims=True)
            return r * gw - tv * (r * r * r * c), jnp.sum(dyv * tv * r, axis=0, keepdims=True)

        d1, g2 = rms_bwd(h_ref[...], w2_ref[...], dv_ref[...])
        dh = d1 + dres_ref[...]
        dh_ref[...] = dh
        dw2_ref[...] += g2
        d2, g1 = rms_bwd(t_ref[...], w1_ref[...], dh)
        dt_ref[...] = d2.astype(dt_ref.dtype)
        dw1_ref[...] += g1

    row = pl.BlockSpec((tr, f), lambda i: (i, 0))
    wsp = pl.BlockSpec((1, f), lambda i: (0, 0))
    return pl.pallas_call(
        body, name=name, grid=(r_ // tr,), in_specs=[row, wsp, row, row, row, wsp], out_specs=[row, row, wsp, wsp],
        out_shape=[jax.ShapeDtypeStruct((r_, f), F32), jax.ShapeDtypeStruct((r_, f), MXU_DTYPE),
                   jax.ShapeDtypeStruct((1, f), F32), jax.ShapeDtypeStruct((1, f), F32)],
        compiler_params=_params("arbitrary"),
    )(h, w2.reshape(1, f), dv, dres, t, w1.reshape(1, f))


def _hnorm_fwd(o, w, width, name="attn_out_norm"):
    h, s_, v = o.shape
    tr = _row_tile(s_)

    def body(o_ref, w_ref, y_ref):
        ss = jnp.sum(o_ref[0] * o_ref[0], axis=-1, keepdims=True)
        for i in range(1, h):
            ss = ss + jnp.sum(o_ref[i] * o_ref[i], axis=-1, keepdims=True)
        r = lax.rsqrt(ss * (1.0 / (h * v)) + EPS)
        for i in range(h):
            sl = slice(i * v, (i + 1) * v)
            y_ref[:, sl] = (o_ref[i] * r * w_ref[:, sl]).astype(y_ref.dtype)

    return pl.pallas_call(
        body, name=name, grid=(s_ // tr,),
        in_specs=[pl.BlockSpec((h, tr, v), lambda i: (0, i, 0)), pl.BlockSpec((1, h * v), lambda i: (0, 0))],
        out_specs=pl.BlockSpec((tr, h * v), lambda i: (i, 0)),
        out_shape=jax.ShapeDtypeStruct((s_, width), MXU_DTYPE), compiler_params=_params("parallel"),
    )(o, w)


def _hnorm_bwd(o, w, dy, name="attn_out_norm_bwd"):
    h, s_, v = o.shape
    tr = _row_tile(s_)

    def body(o_ref, w_ref, dy_ref, do_ref, delta_ref, dw_ref):
        @pl.when(pl.program_id(0) == 0)
        def _():
            dw_ref[...] = jnp.zeros_like(dw_ref)

        ss = jnp.zeros((tr, 1), F32)
        cc = jnp.zeros((tr, 1), F32)
        for i in range(h):
            sl = slice(i * v, (i + 1) * v)
            ov = o_ref[i]
            ss = ss + jnp.sum(ov * ov, axis=-1, keepdims=True)
            cc = cc + jnp.sum(dy_ref[:, sl] * w_ref[:, sl] * ov, axis=-1, keepdims=True)
        r = lax.rsqrt(ss * (1.0 / (h * v)) + EPS)
        c = cc * (1.0 / (h * v))
        for i in range(h):
            sl = slice(i * v, (i + 1) * v)
            ov = o_ref[i]
            dyv = dy_ref[:, sl]
            dov = r * dyv * w_ref[:, sl] - ov * (r * r * r * c)
            do_ref[i] = dov.astype(do_ref.dtype)
            delta_ref[i] = jnp.sum(dov * ov, axis=-1, keepdims=True)
            dw_ref[:, sl] += jnp.sum(dyv * ov * r, axis=0, keepdims=True)

    blk = pl.BlockSpec((h, tr, v), lambda i: (0, i, 0))
    wsp = pl.BlockSpec((1, h * v), lambda i: (0, 0))
    return pl.pallas_call(
        body, name=name, grid=(s_ // tr,),
        in_specs=[blk, wsp, pl.BlockSpec((tr, h * v), lambda i: (i, 0))],
        out_specs=[blk, pl.BlockSpec((h, tr, 1), lambda i: (0, i, 0)), wsp],
        out_shape=[jax.ShapeDtypeStruct(o.shape, MXU_DTYPE), jax.ShapeDtypeStruct((h, s_, 1), F32),
                   jax.ShapeDtypeStruct((1, h * v), F32)],
        compiler_params=_params("arbitrary"),
    )(o, w, dy)


def _loss_head(ffn, h1, target, w, name="loss_head"):
    r_, f = ffn.shape
    tr = _row_tile(r_)

    def body(ffn_ref, h1_ref, tg_ref, w_ref, loss_ref, dy_ref, dffn_ref, dw_ref):
        @pl.when(pl.program_id(0) == 0)
        def _():
            dw_ref[...] = jnp.zeros_like(dw_ref)
            loss_ref[...] = jnp.zeros_like(loss_ref)

        tv = ffn_ref[...]
        wv = w_ref[...]
        r = lax.rsqrt(jnp.mean(tv * tv, axis=-1, keepdims=True) + EPS)
        tn = tv * r
        e = h1_ref[...] + tn * wv - tg_ref[...]
        tot = jnp.sum(jnp.sum(e * e, axis=1, keepdims=True), axis=0, keepdims=True) * (0.5 / f)
        loss_ref[...] += tot + jnp.zeros_like(loss_ref)
        dyv = e * (1.0 / f)
        dy_ref[...] = dyv
        gw = dyv * wv
        c = jnp.mean(gw * tv, axis=-1, keepdims=True)
        dffn_ref[...] = (r * gw - tv * (r * r * r * c)).astype(dffn_ref.dtype)
        dw_ref[...] += jnp.sum(dyv * tn, axis=0, keepdims=True)

    row = pl.BlockSpec((tr, f), lambda i: (i, 0))
    wsp = pl.BlockSpec((1, f), lambda i: (0, 0))
    lsp = pl.BlockSpec((1, LANE), lambda i: (0, 0))
    return pl.pallas_call(
        body, name=name, grid=(r_ // tr,),
        in_specs=[row, row, row, wsp], out_specs=[lsp, row, row, wsp],
        out_shape=[jax.ShapeDtypeStruct((1, LANE), F32), jax.ShapeDtypeStruct((r_, f), F32),
                   jax.ShapeDtypeStruct((r_, f), MXU_DTYPE), jax.ShapeDtypeStruct((1, f), F32)],
        compiler_params=_params("arbitrary"),
    )(ffn, h1, target, w.reshape(1, f))


def _rot_matrix():
    p = np.zeros((ROPE, ROPE), np.float32)
    for i in range(HALF):
        p[i + HALF, i] = -1.0
        p[i, i + HALF] = 1.0
    return jnp.asarray(p, BF16)


def _rope_val(r, c2, s2, rot):
    hi, mid, _ = _split3(r)
    return r * c2 + (_dot(hi, rot, 1, 0) + _dot(mid, rot, 1, 0)) * s2


def _q_prep(q, cos2, sin2, scale, name):
    h, s_, _ = q.shape
    tr = _pick(s_, (4096, 2048, 1024, 512, 256, 128, 64, 32, 16))

    def body(q_ref, c_ref, s_ref, rot_ref, o_ref):
        for rs in _row_slices(tr, 16):
            x = q_ref[rs, :]
            o_ref[rs, :NOPE] = (x[:, :NOPE] * scale).astype(o_ref.dtype)
            o_ref[rs, NOPE:] = (_rope_val(x[:, NOPE:], c_ref[rs, :], s_ref[rs, :], rot_ref[...]) * scale).astype(o_ref.dtype)

    blk = pl.BlockSpec((None, tr, QK), lambda hh, i: (hh, i, 0))
    csp = pl.BlockSpec((tr, ROPE), lambda hh, i: (i, 0))
    return pl.pallas_call(
        body, name=name, grid=(h, s_ // tr),
        in_specs=[blk, csp, csp, pl.BlockSpec((ROPE, ROPE), lambda hh, i: (0, 0))], out_specs=blk,
        out_shape=jax.ShapeDtypeStruct(q.shape, MXU_DTYPE), compiler_params=_params("parallel", "parallel"),
    )(q, cos2, sin2, _rot_matrix())


def _q_up(qkvn, w_uq_t, cos2, sin2, scale, name="q_up"):
    s_ = qkvn.shape[0]
    h = w_uq_t.shape[0]
    tm = _pick(s_, (4096, 2048, 1024, 512, 256, 128))

    def body(a_ref, w_ref, c_ref, s_ref, rot_ref, o_ref):
        for rs in _row_slices(tm, 16):
            x = _dot(_mx(a_ref[rs, :]), _mx(w_ref[...]), 1, 1)
            o_ref[rs, :NOPE] = (x[:, :NOPE] * scale).astype(o_ref.dtype)
            o_ref[rs, NOPE:] = (_rope_val(x[:, NOPE:], c_ref[rs, :], s_ref[rs, :], rot_ref[...]) * scale).astype(o_ref.dtype)

    csp = pl.BlockSpec((tm, ROPE), lambda j, i: (i, 0))
    return pl.pallas_call(
        body, name=name, grid=(h, s_ // tm),
        in_specs=[pl.BlockSpec((tm, Q_RANK), lambda j, i: (i, 0)), pl.BlockSpec((None, QK, Q_RANK), lambda j, i: (j, 0, 0)),
                  csp, csp, pl.BlockSpec((ROPE, ROPE), lambda j, i: (0, 0))],
        out_specs=pl.BlockSpec((None, tm, QK), lambda j, i: (j, i, 0)),
        out_shape=jax.ShapeDtypeStruct((h, s_, QK), MXU_DTYPE), compiler_params=_params("parallel", "parallel"),
    )(qkvn, w_uq_t, cos2, sin2, _rot_matrix())


def _kv_up(qkvn, w_ukv, small, cos2, sin2, name="kv_up"):
    s_ = qkvn.shape[0]
    h = w_ukv.shape[0]
    tm = _pick(s_, (4096, 2048, 1024, 512, 256, 128))

    def body(a_ref, w_ref, sm_ref, c_ref, s_ref, rot_ref, k_ref, v_ref):
        for rs in _row_slices(tm, 16):
            x = _dot(_mx(a_ref[rs, :]), _mx(w_ref[...]), 1, 0)
            k_ref[rs, :NOPE] = x[:, :NOPE].astype(k_ref.dtype)
            k_ref[rs, NOPE:] = _rope_val(sm_ref[rs, :ROPE], c_ref[rs, :], s_ref[rs, :], rot_ref[...]).astype(k_ref.dtype)
            v_ref[rs, :] = x[:, NOPE:].astype(v_ref.dtype)

    csp = pl.BlockSpec((tm, ROPE), lambda j, i: (i, 0))
    return pl.pallas_call(
        body, name=name, grid=(h, s_ // tm),
        in_specs=[pl.BlockSpec((tm, KV_RANK), lambda j, i: (i, Q_RANK // KV_RANK)),
                  pl.BlockSpec((None, KV_RANK, NOPE + VDIM), lambda j, i: (j, 0, 0)),
                  pl.BlockSpec((tm, LANE), lambda j, i: (i, 0)), csp, csp, pl.BlockSpec((ROPE, ROPE), lambda j, i: (0, 0))],
        out_specs=[pl.BlockSpec((None, tm, QK), lambda j, i: (j, i, 0)), pl.BlockSpec((None, tm, VDIM), lambda j, i: (j, i, 0))],
        out_shape=[jax.ShapeDtypeStruct((h, s_, QK), MXU_DTYPE), jax.ShapeDtypeStruct((h, s_, VDIM), MXU_DTYPE)],
        compiler_params=_params("parallel", "parallel"),
    )(qkvn, w_ukv, small, cos2, sin2, _rot_matrix())


def _dkv_post(dk, dv, ddt, cos2, nsin2, name="dkv_post"):
    h, s_, _ = dk.shape
    tr = _row_tile(s_)

    def body(dk_ref, dv_ref, ddt_ref, c_ref, s_ref, rot_ref, dkv_ref, dsm_ref):
        acc = dk_ref[0, :, NOPE:]
        for i in range(1, h):
            acc = acc + dk_ref[i, :, NOPE:]
        dsm_ref[:, :ROPE] = _rope_val(acc, c_ref[...], s_ref[...], rot_ref[...]).astype(dsm_ref.dtype)
        dsm_ref[:, ROPE:] = ddt_ref[:, ROPE:].astype(dsm_ref.dtype)
        for i in range(h):
            dkv_ref[i, :, :NOPE] = dk_ref[i, :, :NOPE].astype(dkv_ref.dtype)
            dkv_ref[i, :, NOPE:] = dv_ref[i].astype(dkv_ref.dtype)

    csp = pl.BlockSpec((tr, ROPE), lambda i: (i, 0))
    return pl.pallas_call(
        body, name=name, grid=(s_ // tr,),
        in_specs=[pl.BlockSpec((h, tr, QK), lambda i: (0, i, 0)), pl.BlockSpec((h, tr, VDIM), lambda i: (0, i, 0)),
                  pl.BlockSpec((tr, LANE), lambda i: (i, 0)), csp, csp, pl.BlockSpec((ROPE, ROPE), lambda i: (0, 0))],
        out_specs=[pl.BlockSpec((h, tr, NOPE + VDIM), lambda i: (0, i, 0)), pl.BlockSpec((tr, LANE), lambda i: (i, 0))],
        out_shape=[jax.ShapeDtypeStruct((h, s_, NOPE + VDIM), MXU_DTYPE), jax.ShapeDtypeStruct((s_, LANE), MXU_DTYPE)],
        compiler_params=_params("parallel"),
    )(dk, dv, ddt, cos2, nsin2, _rot_matrix())


def _attn_tile(s):
    return 2048 if s % 4096 == 0 else s // 2


def _pairs(n, by_key):
    if by_key:
        pr = [(i, j) for j in range(n) for i in range(j, n)]
    else:
        pr = [(i, j) for i in range(n) for j in range(i + 1)]
    return (jnp.asarray([p[0] for p in pr], jnp.int32), jnp.asarray([p[1] for p in pr], jnp.int32))


ATTN_ROW_GROUPS = 8


def _row_groups(t, diag):
    tg = t // ATTN_ROW_GROUPS
    out = []
    for r in range(ATTN_ROW_GROUPS):
        nc = (r + 1) * tg if diag else t
        mask = None
        if diag:
            mask = (lax.broadcasted_iota(jnp.int32, (tg, nc), 1)
                    <= lax.broadcasted_iota(jnp.int32, (tg, nc), 0) + r * tg)
        out.append((slice(r * tg, (r + 1) * tg), nc, mask))
    return out


def _flash_specs(t, dk, dv):
    qsp = pl.BlockSpec((None, t, dk), lambda hh, p, qi, kj: (hh, qi[p], 0))
    ksp = pl.BlockSpec((None, t, dk), lambda hh, p, qi, kj: (hh, kj[p], 0))
    vsp = pl.BlockSpec((None, t, dv), lambda hh, p, qi, kj: (hh, kj[p], 0))
    osp = pl.BlockSpec((None, t, dv), lambda hh, p, qi, kj: (hh, qi[p], 0))
    lsp = pl.BlockSpec((None, t, 1), lambda hh, p, qi, kj: (hh, qi[p], 0))
    return qsp, ksp, vsp, osp, lsp


def _flash_fwd(q, k, v, name="flash_fwd"):
    h, s_, dk = q.shape
    dv = v.shape[-1]
    t = _attn_tile(s_)
    n = s_ // t
    qi, kj = _pairs(n, False)

    def body(qi_ref, kj_ref, q_ref, k_ref, v_ref, o_ref, lse_ref, m_s, l_s, acc):
        p_ = pl.program_id(1)
        i, j = qi_ref[p_], kj_ref[p_]

        @pl.when(j == 0)
        def _():
            m_s[...] = jnp.full_like(m_s, -jnp.inf)
            l_s[...] = jnp.zeros_like(l_s)
            acc[...] = jnp.zeros_like(acc)

        def update(diag):
            for rs, nc, mask in _row_groups(t, diag):
                sc = _dot(q_ref[rs, :], k_ref[0:nc, :], 1, 1)
                if mask is not None:
                    sc = jnp.where(mask, sc, -jnp.inf)
                m_old = m_s[rs, :]
                m_new = jnp.maximum(m_old, jnp.max(sc, axis=1, keepdims=True))
                alpha = jnp.exp(m_old - m_new)
                p = jnp.exp(sc - m_new)
                l_s[rs, :] = alpha * l_s[rs, :] + jnp.sum(p, axis=1, keepdims=True)
                acc[rs, :] = alpha * acc[rs, :] + _dot(_mx(p), v_ref[0:nc, :], 1, 0)
                m_s[rs, :] = m_new

        @pl.when(j < i)
        def _():
            update(False)

        @pl.when(j == i)
        def _():
            update(True)
            o_ref[...] = acc[...] / l_s[...]
            lse_ref[...] = m_s[...] + jnp.log(l_s[...])

    qsp, ksp, vsp, osp, lsp = _flash_specs(t, dk, dv)
    gs = pltpu.PrefetchScalarGridSpec(
        num_scalar_prefetch=2, grid=(h, qi.shape[0]), in_specs=[qsp, ksp, vsp], out_specs=[osp, lsp],
        scratch_shapes=[pltpu.VMEM((t, 1), F32), pltpu.VMEM((t, 1), F32), pltpu.VMEM((t, dv), F32)])
    return pl.pallas_call(
        body, name=name, grid_spec=gs,
        out_shape=[jax.ShapeDtypeStruct((h, s_, dv), F32), jax.ShapeDtypeStruct((h, s_, 1), F32)],
        compiler_params=_params("parallel", "arbitrary"),
    )(qi, kj, q, k, v)


def _flash_bwd(q, k, v, do, lse, delta, name="flash_bwd"):
    h, s_, dk = q.shape
    dv = v.shape[-1]
    t = _attn_tile(s_)
    tg = t // ATTN_ROW_GROUPS
    n = s_ // t
    qi, kj = _pairs(n, True)

    def body(qi_ref, kj_ref, q_ref, k_ref, v_ref, do_ref, lse_ref, delta_ref, dq_ref, dk_ref, dv_ref, dk_acc, dv_acc):
        p_ = pl.program_id(1)
        i, j = qi_ref[p_], kj_ref[p_]

        @pl.when(p_ == 0)
        def _():
            dq_ref[...] = jnp.zeros_like(dq_ref)

        def update(diag):
            for g, (rs, nc, mask) in enumerate(_row_groups(t, diag)):
                sc = _dot(q_ref[rs, :], k_ref[0:nc, :], 1, 1)
                if mask is not None:
                    sc = jnp.where(mask, sc, -jnp.inf)
                p = jnp.exp(sc - lse_ref[rs, :])
                dob = _mx(do_ref[rs, :])
                dv_acc[0:nc, :] += _dot(_mx(p), dob, 0, 0)
                dp = _dot(dob, v_ref[0:nc, :], 1, 1)
                dsb = _mx(p * (dp - delta_ref[rs, :]))
                dk_acc[0:nc, :] += _dot(dsb, q_ref[rs, :], 0, 0)
                rows = pl.ds(pl.multiple_of(i * t + g * tg, tg), tg)
                dq_ref[rows, :] += _dot(dsb, k_ref[0:nc, :], 1, 0)

        @pl.when(i == j)
        def _():
            dk_acc[...] = jnp.zeros_like(dk_acc)
            dv_acc[...] = jnp.zeros_like(dv_acc)
            update(True)

        @pl.when(i > j)
        def _():
            update(False)

        @pl.when(i == n - 1)
        def _():
            dk_ref[...] = dk_acc[...]
            dv_ref[...] = dv_acc[...]

    qsp, ksp, vsp, osp, lsp = _flash_specs(t, dk, dv)
    dqsp = pl.BlockSpec((None, s_, dk), lambda hh, p, qi, kj: (hh, 0, 0))
    gs = pltpu.PrefetchScalarGridSpec(
        num_scalar_prefetch=2, grid=(h, qi.shape[0]), in_specs=[qsp, ksp, vsp, osp, lsp, lsp],
        out_specs=[dqsp, ksp, vsp],
        scratch_shapes=[pltpu.VMEM((t, dk), F32), pltpu.VMEM((t, dv), F32)])
    return pl.pallas_call(
        body, name=name, grid_spec=gs,
        out_shape=[jax.ShapeDtypeStruct((h, s_, dk), F32), jax.ShapeDtypeStruct((h, s_, dk), F32),
                   jax.ShapeDtypeStruct((h, s_, dv), F32)],
        compiler_params=_params("parallel", "arbitrary"),
    )(qi, kj, q, k, v, do, lse, delta)


HALO = 8


def _conv_specs(s_, c, tr, after):
    main = pl.BlockSpec((tr, c), lambda i: (i, 0))
    per = tr // HALO
    if after:
        halo = pl.BlockSpec((HALO, c), lambda i: (jnp.minimum((i + 1) * per, s_ // HALO - 1), 0))
    else:
        halo = pl.BlockSpec((HALO, c), lambda i: (jnp.maximum(i * per - 1, 0), 0))
    return main, halo


def _fill_before(ext, t_ref, h_ref, tr):
    ext[0:HALO, :] = jnp.where(pl.program_id(0) > 0, h_ref[...], 0.0)
    ext[HALO:HALO + tr, :] = t_ref[...]


def _taps(ext, w_ref, tr):
    base = HALO - (CONV_K - 1)
    acc = ext[base:base + tr, :] * w_ref[0:1, :]
    for k in range(1, CONV_K):
        acc = acc + ext[base + k:base + k + tr, :] * w_ref[k:k + 1, :]
    return acc


def _conv_fwd(t, w, b, name="conv_fwd"):
    s_, c = t.shape
    tr = _row_tile(s_)

    def body(t_ref, h_ref, w_ref, b_ref, o_ref, ext):
        _fill_before(ext, t_ref, h_ref, tr)
        o_ref[...] = _silu(_taps(ext, w_ref, tr) + b_ref[...])

    main, halo = _conv_specs(s_, c, tr, False)
    return pl.pallas_call(
        body, name=name, grid=(s_ // tr,),
        in_specs=[main, halo, pl.BlockSpec((CONV_K, c), lambda i: (0, 0)), pl.BlockSpec((1, c), lambda i: (0, 0))],
        out_specs=main, out_shape=jax.ShapeDtypeStruct((s_, c), F32),
        scratch_shapes=[pltpu.VMEM((tr + HALO, c), F32)], compiler_params=_params("parallel"),
    )(t, t, w, b)


def _conv_bwd_pre(t, w, b, dact, name="conv_bwd_pre"):
    s_, c = t.shape
    tr = _row_tile(s_)

    def body(t_ref, h_ref, w_ref, b_ref, da_ref, dpre_ref, dwb_ref, ext):
        @pl.when(pl.program_id(0) == 0)
        def _():
            dwb_ref[...] = jnp.zeros_like(dwb_ref)

        _fill_before(ext, t_ref, h_ref, tr)
        dpre = da_ref[...] * _dsilu(_taps(ext, w_ref, tr) + b_ref[...])
        dpre_ref[...] = dpre
        base = HALO - (CONV_K - 1)
        for k in range(CONV_K):
            dwb_ref[k:k + 1, :] += jnp.sum(dpre * ext[base + k:base + k + tr, :], axis=0, keepdims=True)
        dwb_ref[CONV_K:CONV_K + 1, :] += jnp.sum(dpre, axis=0, keepdims=True)

    main, halo = _conv_specs(s_, c, tr, False)
    return pl.pallas_call(
        body, name=name, grid=(s_ // tr,),
        in_specs=[main, halo, pl.BlockSpec((CONV_K, c), lambda i: (0, 0)), pl.BlockSpec((1, c), lambda i: (0, 0)), main],
        out_specs=[main, pl.BlockSpec((8, c), lambda i: (0, 0))],
        out_shape=[jax.ShapeDtypeStruct((s_, c), F32), jax.ShapeDtypeStruct((8, c), F32)],
        scratch_shapes=[pltpu.VMEM((tr + HALO, c), F32)], compiler_params=_params("arbitrary"),
    )(t, t, w, b, dact)


def _conv_bwd_in(dpre, w, name="conv_bwd_in"):
    s_, c = dpre.shape
    tr = _row_tile(s_)
    nt = s_ // tr

    def body(d_ref, h_ref, w_ref, o_ref, ext):
        ext[0:tr, :] = d_ref[...]
        ext[tr:tr + HALO, :] = jnp.where(pl.program_id(0) < nt - 1, h_ref[...], 0.0)
        acc = ext[CONV_K - 1:CONV_K - 1 + tr, :] * w_ref[0:1, :]
        for k in range(1, CONV_K):
            acc = acc + ext[CONV_K - 1 - k:CONV_K - 1 - k + tr, :] * w_ref[k:k + 1, :]
        o_ref[...] = acc.astype(o_ref.dtype)

    main, halo = _conv_specs(s_, c, tr, True)
    return pl.pallas_call(
        body, name=name, grid=(nt,),
        in_specs=[main, halo, pl.BlockSpec((CONV_K, c), lambda i: (0, 0))],
        out_specs=main, out_shape=jax.ShapeDtypeStruct((s_, c), MXU_DTYPE),
        scratch_shapes=[pltpu.VMEM((tr + HALO, c), F32)], compiler_params=_params("parallel"),
    )(dpre, dpre, w)


def _ssd_chunk_common(dt_ref, dtt_ref, br_ref, bc_ref, ar_ref, ac_ref):
    li = lax.broadcasted_iota(jnp.int32, (CHUNK, CHUNK), 0)
    si = lax.broadcasted_iota(jnp.int32, (CHUNK, CHUNK), 1)
    lower = li >= si
    lower_b = lower.astype(BF16)
    upper_b = (li <= si).astype(BF16)
    zr = dt_ref[...] + br_ref[...]
    dtc = _softplus(zr)
    a_row = -jnp.exp(ar_ref[...])
    acum = _exact_dot(lower_b, dtc * a_row, 1, 0, False)
    dtt = _softplus(dtt_ref[...] + bc_ref[...])
    acum_t = _exact_dot(dtt * (-jnp.exp(ac_ref[...])), upper_b, 1, 0, True)
    return lower, upper_b, zr, dtc, a_row, acum, acum_t


def _head_terms(h, lower, dtc, acum, acum_t):
    lane = lax.broadcasted_iota(jnp.int32, (1, LANE), 1)
    sub = lax.broadcasted_iota(jnp.int32, (SSD_H, 1), 0)
    rowid = lax.broadcasted_iota(jnp.int32, (CHUNK, 1), 0)
    oh = (lane == HEAD_LANE + h).astype(F32)
    acol = jnp.sum(acum * oh, axis=1, keepdims=True)
    dcol = jnp.sum(dtc * oh, axis=1, keepdims=True)
    arow = jnp.sum(acum_t * (sub == h).astype(F32), axis=0, keepdims=True)
    alast = jnp.sum(jnp.where(rowid == CHUNK - 1, acol, 0.0), axis=0, keepdims=True)
    decay = jnp.exp(jnp.where(lower, acol - arow, -jnp.inf))
    return oh, acol, dcol, alast, decay


SSD_PAIRS = SSD_H // 2
PAIRS_PER_GROUP = SSD_E // 2


def _ps(q):
    return slice(q * LANE, (q + 1) * LANE)


def _gs(off, g):
    return slice(off + g * SSD_N, off + (g + 1) * SSD_N)


def _lanes(c0, c1):
    return jnp.where(lax.broadcasted_iota(jnp.int32, (1, LANE), 1) < SSD_P, c0, c1)


def _rows(c0, c1):
    return jnp.where(lax.broadcasted_iota(jnp.int32, (LANE, 1), 0) < SSD_P, c0, c1)


def _lane_halves(t):
    first = lax.broadcasted_iota(jnp.int32, (1, LANE), 1) < SSD_P
    return (jnp.sum(jnp.where(first, t, 0.0), axis=1, keepdims=True),
            jnp.sum(jnp.where(first, 0.0, t), axis=1, keepdims=True))


def _ssd_in_specs(rev):
    def ci(c):
        return c if rev is None else rev - c
    return [pl.BlockSpec((CHUNK, CONV_DIM), lambda c: (ci(c), 0)),
            pl.BlockSpec((CHUNK, LANE), lambda c: (ci(c), 0)),
            pl.BlockSpec((SSD_H, CHUNK), lambda c: (0, ci(c))),
            pl.BlockSpec((1, LANE), lambda c: (0, 0)), pl.BlockSpec((SSD_H, 1), lambda c: (0, 0)),
            pl.BlockSpec((1, LANE), lambda c: (0, 0)), pl.BlockSpec((SSD_H, 1), lambda c: (0, 0)),
            pl.BlockSpec((SSD_PAIRS, 1, LANE), lambda c: (0, 0, 0))]


def _ssd_fwd(xbc, small, dtt, bias_r, bias_c, alog_r, alog_c, dsk, name="ssd_fwd"):
    s_ = xbc.shape[0]
    nc = s_ // CHUNK

    def body(x_ref, dt_ref, dtt_ref, br_ref, bc_ref, ar_ref, ac_ref, dsk_ref, y_ref, prev_ref, state):
        @pl.when(pl.program_id(0) == 0)
        def _():
            state[...] = jnp.zeros_like(state)

        lower, _, _, dtc, _, acum, acum_t = _ssd_chunk_common(dt_ref, dtt_ref, br_ref, bc_ref, ar_ref, ac_ref)
        for g in range(SSD_G):
            bb = _mx(x_ref[:, _gs(B_OFF, g)])
            cb_ = _mx(x_ref[:, _gs(C_OFF, g)])
            cbm = _dot(cb_, bb, 1, 1)
            for e in range(PAIRS_PER_GROUP):
                q = g * PAIRS_PER_GROUP + e
                _, acol0, dcol0, alast0, decay0 = _head_terms(2 * q, lower, dtc, acum, acum_t)
                _, acol1, dcol1, alast1, decay1 = _head_terms(2 * q + 1, lower, dtc, acum, acum_t)
                x = x_ref[:, _ps(q)]
                xdt = x * _lanes(dcol0, dcol1)
                xb = _mx(xdt)
                yd = _lanes(_dot(_mx(cbm * decay0), xb, 1, 0), _dot(_mx(cbm * decay1), xb, 1, 0))
                prev = state[q]
                prev_ref[0, q] = prev
                yo = _dot(cb_, _mx(prev), 1, 1) * _lanes(jnp.exp(acol0), jnp.exp(acol1))
                ds = _lanes(jnp.exp(alast0 - acol0), jnp.exp(alast1 - acol1))
                st = _dot(_mx(xdt * ds), bb, 0, 0)
                state[q] = prev * _rows(jnp.exp(alast0), jnp.exp(alast1)) + st
                y_ref[:, _ps(q)] = yd + yo + x * dsk_ref[q]

    psp = pl.BlockSpec((1, SSD_PAIRS, LANE, SSD_N), lambda c: (c, 0, 0, 0))
    return pl.pallas_call(
        body, name=name, grid=(nc,),
        in_specs=_ssd_in_specs(None), out_specs=[pl.BlockSpec((CHUNK, SSD_W), lambda c: (c, 0)), psp],
        out_shape=[jax.ShapeDtypeStruct((s_, SSD_W), F32),
                   jax.ShapeDtypeStruct((nc, SSD_PAIRS, LANE, SSD_N), F32)],
        scratch_shapes=[pltpu.VMEM((SSD_PAIRS, LANE, SSD_N), F32)],
        compiler_params=_params("arbitrary"),
    )(xbc, small, dtt, bias_r, bias_c, alog_r, alog_c, dsk)


def _ssd_bwd(xbc, small, dtt, bias_r, bias_c, alog_r, alog_c, dsk, prev, dy, name="ssd_bwd"):
    s_ = xbc.shape[0]
    nc = s_ // CHUNK

    def body(x_ref, dt_ref, dtt_ref, br_ref, bc_ref, ar_ref, ac_ref, dsk_ref, prev_ref, dy_ref,
             dx_ref, ddt_ref, dpar_ref, dstate):
        @pl.when(pl.program_id(0) == 0)
        def _():
            dstate[...] = jnp.zeros_like(dstate)
            dpar_ref[...] = jnp.zeros_like(dpar_ref)

        lower, upper_b, zr, dtc, a_row, acum, acum_t = _ssd_chunk_common(
            dt_ref, dtt_ref, br_ref, bc_ref, ar_ref, ac_ref)
        strict = (lax.broadcasted_iota(jnp.int32, (CHUNK, CHUNK), 1)
                  < lax.broadcasted_iota(jnp.int32, (CHUNK, CHUNK), 0))
        strict_b = strict.astype(BF16)
        col2 = lax.broadcasted_iota(jnp.int32, (CHUNK, 2 * CHUNK), 1)
        strict2 = (jnp.where(col2 >= CHUNK, col2 - CHUNK, col2)
                   < lax.broadcasted_iota(jnp.int32, (CHUNK, 2 * CHUNK), 0))
        da_in = jnp.zeros((CHUNK, LANE), F32)
        r_off = jnp.zeros((CHUNK, LANE), F32)
        c_int = jnp.zeros((CHUNK, LANE), F32)
        c_row = jnp.zeros((1, LANE), F32)
        ddt = jnp.zeros((CHUNK, LANE), F32)
        dskip = jnp.zeros((1, LANE), F32)
        for g in range(SSD_G):
            bb = _mx(x_ref[:, _gs(B_OFF, g)])
            cb_ = _mx(x_ref[:, _gs(C_OFF, g)])
            cbm = _dot(cb_, bb, 1, 1)
            dcb = jnp.zeros((CHUNK, CHUNK), F32)
            dc_acc = jnp.zeros((CHUNK, SSD_N), F32)
            db_acc = jnp.zeros((CHUNK, SSD_N), F32)
            for e in range(PAIRS_PER_GROUP):
                q = g * PAIRS_PER_GROUP + e
                oh0, acol0, dcol0, alast0, decay0 = _head_terms(2 * q, lower, dtc, acum, acum_t)
                oh1, acol1, dcol1, alast1, decay1 = _head_terms(2 * q + 1, lower, dtc, acum, acum_t)
                x = x_ref[:, _ps(q)]
                dy = dy_ref[:, _ps(q)]
                dcol = _lanes(dcol0, dcol1)
                xdt = x * dcol
                xb = _mx(xdt)
                eacol = _lanes(jnp.exp(acol0), jnp.exp(acol1))
                ds = _lanes(jnp.exp(alast0 - acol0), jnp.exp(alast1 - acol1))
                ealast = _rows(jnp.exp(alast0), jnp.exp(alast1))
                dyb = _mx(dy)
                dyb0, dyb1 = _mx(_lanes(dy, 0.0)), _mx(_lanes(0.0, dy))
                dsh = dstate[q]
                dshb = _mx(dsh)
                prev = prev_ref[0, q]
                prevb = _mx(prev)
                dxdt_inter = ds * _dot(bb, dshb, 1, 1)
                dxdt = _lanes(_dot(_mx(cbm * decay0), dyb, 0, 0), _dot(_mx(cbm * decay1), dyb, 0, 0)) + dxdt_inter
                dwl0 = _dot(dyb0, xb, 1, 1) * decay0
                dwl1 = _dot(dyb1, xb, 1, 1) * decay1
                dcb = dcb + dwl0 + dwl1
                dyeb = _mx(dy * eacol)
                dc_acc = dc_acc + _dot(dyeb, prevb, 1, 0)
                db_acc = db_acc + _dot(_mx(xdt * ds), dshb, 1, 0)
                dstate[q] = _dot(dyeb, cb_, 0, 0) + ealast * dsh
                above = _exact_dot(upper_b, jnp.concatenate([dwl0 * cbm, dwl1 * cbm], axis=1), 1, 0, False)
                above = jnp.where(strict2, above, 0.0)
                da_in = (da_in + jnp.sum(above[:, :CHUNK], axis=1, keepdims=True) * oh0
                         + jnp.sum(above[:, CHUNK:], axis=1, keepdims=True) * oh1)
                y_off = _dot(cb_, prevb, 1, 1) * eacol
                r0, r1 = _lane_halves(dy * y_off)
                r_off = r_off + r0 * oh0 + r1 * oh1
                c0, c1 = _lane_halves(xdt * dxdt_inter)
                c_int = c_int + c0 * oh0 + c1 * oh1
                both = jnp.sum(dsh * prev, axis=1, keepdims=True) * ealast
                c_row = (c_row + jnp.sum(_rows(both, 0.0), axis=0, keepdims=True) * oh0
                         + jnp.sum(_rows(0.0, both), axis=0, keepdims=True) * oh1)
                t0, t1 = _lane_halves(dxdt * x)
                ddt = ddt + t0 * oh0 + t1 * oh1
                dx_ref[:, _ps(q)] = dxdt * dcol + dy * dsk_ref[q]
                k0, k1 = _lane_halves(dy * x)
                dskip = (dskip + jnp.sum(k0, axis=0, keepdims=True) * oh0 + jnp.sum(k1, axis=0, keepdims=True) * oh1)
            dcbb = _mx(dcb)
            dx_ref[:, _gs(C_OFF, g)] = dc_acc + _dot(dcbb, bb, 1, 0)
            dx_ref[:, _gs(B_OFF, g)] = db_acc + _dot(dcbb, cb_, 0, 0)
        da = (da_in + _exact_dot(upper_b, r_off, 1, 0, False) + _exact_dot(strict_b, c_int, 1, 0, False) + c_row)
        draw = (ddt + da * a_row) * _sigmoid(zr)
        ddt_ref[...] = draw
        dpar_ref[0:1, :] += jnp.sum(draw, axis=0, keepdims=True)
        dpar_ref[1:2, :] += jnp.sum(da * dtc, axis=0, keepdims=True) * a_row
        dpar_ref[2:3, :] += dskip

    rev = nc - 1
    psp = pl.BlockSpec((1, SSD_PAIRS, LANE, SSD_N), lambda c: (rev - c, 0, 0, 0))
    return pl.pallas_call(
        body, name=name, grid=(nc,),
        in_specs=_ssd_in_specs(rev) + [psp, pl.BlockSpec((CHUNK, SSD_W), lambda c: (rev - c, 0))],
        out_specs=[pl.BlockSpec((CHUNK, CONV_DIM), lambda c: (rev - c, 0)),
                   pl.BlockSpec((CHUNK, LANE), lambda c: (rev - c, 0)), pl.BlockSpec((8, LANE), lambda c: (0, 0))],
        out_shape=[jax.ShapeDtypeStruct((s_, CONV_DIM), F32), jax.ShapeDtypeStruct((s_, LANE), F32),
                   jax.ShapeDtypeStruct((8, LANE), F32)],
        scratch_shapes=[pltpu.VMEM((SSD_PAIRS, LANE, SSD_N), F32)],
        compiler_params=_params("arbitrary"),
    )(xbc, small, dtt, bias_r, bias_c, alog_r, alog_c, dsk, prev, dy)


GN = SSD_W // SSD_G


def _gated_norm_fwd(y, z, w, cat, name="gated_norm_fwd"):
    s_, f = y.shape
    tr = _row_tile(s_)

    def body(y_ref, z_ref, w_ref, cat_ref, o_ref):
        for g in range(SSD_G):
            sl = slice(g * GN, (g + 1) * GN)
            gg = y_ref[:, sl] * _silu(z_ref[:, sl])
            r = lax.rsqrt(jnp.mean(gg * gg, axis=-1, keepdims=True) + EPS)
            o_ref[:, sl] = (gg * r * w_ref[:, sl]).astype(o_ref.dtype)

    row = pl.BlockSpec((tr, f), lambda i: (i, 0))
    wsp = pl.BlockSpec((1, f), lambda i: (0, 0))
    return pl.pallas_call(
        body, name=name, grid=(s_ // tr,),
        in_specs=[row, row, wsp, pl.BlockSpec(memory_space=pl.ANY)], out_specs=pl.BlockSpec((tr, f), lambda i: (i, 1)),
        out_shape=jax.ShapeDtypeStruct(cat.shape, cat.dtype), input_output_aliases={3: 0},
        compiler_params=_params("parallel"),
    )(y, z, w.reshape(1, f), cat)


def _gated_norm_bwd(y, z, w, dout, name="gated_norm_bwd"):
    s_, f = y.shape
    tr = _row_tile(s_)

    def body(y_ref, z_ref, w_ref, do_ref, dy_ref, dz_ref, dw_ref):
        @pl.when(pl.program_id(0) == 0)
        def _():
            dw_ref[...] = jnp.zeros_like(dw_ref)

        for g in range(SSD_G):
            sl = slice(g * GN, (g + 1) * GN)
            yv = y_ref[:, sl]
            zv = z_ref[:, sl]
            dov = do_ref[:, sl].astype(F32)
            sz = _silu(zv)
            gg = yv * sz
            r = lax.rsqrt(jnp.mean(gg * gg, axis=-1, keepdims=True) + EPS)
            gw = dov * w_ref[:, sl]
            c = jnp.mean(gw * gg, axis=-1, keepdims=True)
            dgg = r * gw - gg * (r * r * r * c)
            dy_ref[:, sl] = dgg * sz
            dz_ref[:, sl] = (dgg * yv * _dsilu(zv)).astype(dz_ref.dtype)
            dw_ref[:, sl] += jnp.sum(dov * gg * r, axis=0, keepdims=True)

    row = pl.BlockSpec((tr, f), lambda i: (i, 0))
    wsp = pl.BlockSpec((1, f), lambda i: (0, 0))
    return pl.pallas_call(
        body, name=name, grid=(s_ // tr,),
        in_specs=[row, row, wsp, pl.BlockSpec((tr, f), lambda i: (i, 1))], out_specs=[row, row, wsp],
        out_shape=[jax.ShapeDtypeStruct((s_, f), F32), jax.ShapeDtypeStruct((s_, f), MXU_DTYPE),
                   jax.ShapeDtypeStruct((1, f), F32)],
        compiler_params=_params("arbitrary"),
    )(y, z, w.reshape(1, f), dout)


def _ffn_fwd(vv, w_gate, w_up, name="ffn_gate_up"):
    s_, d = vv.shape
    nb, f8, _ = w_gate.shape
    tm = _pick(s_, (1024, 512, 256, 128))

    def body(v_ref, wg_ref, wu_ref, g_ref, u_ref, a_ref):
        for rs in _row_slices(tm, 16):
            a = _mx(v_ref[rs, :])
            g = _dot(a, _mx(wg_ref[...]), 1, 1)
            u = _dot(a, _mx(wu_ref[...]), 1, 1)
            s = _sigmoid(g)
            gs = g * s
            g_ref[rs, :] = (u * (s * (1.0 + g * (1.0 - s)))).astype(g_ref.dtype)
            u_ref[rs, :] = gs.astype(u_ref.dtype)
            a_ref[rs, :] = (gs * u).astype(a_ref.dtype)

    wsp = pl.BlockSpec((None, f8, d), lambda j, i: (j, 0, 0))
    osp = pl.BlockSpec((None, tm, f8), lambda j, i: (j, i, 0))
    return pl.pallas_call(
        body, name=name, grid=(nb, s_ // tm),
        in_specs=[pl.BlockSpec((tm, d), lambda j, i: (i, 0)), wsp, wsp], out_specs=[osp] * 3,
        out_shape=[jax.ShapeDtypeStruct((nb, s_, f8), MXU_DTYPE)] * 3,
        compiler_params=_params("parallel", "parallel"),
    )(vv, w_gate, w_up)


def _ffn_bwd_act(dffn, w_down, gate, up, name="ffn_d_act"):
    s_, d = dffn.shape
    nb, f8, _ = w_down.shape
    tm = _pick(s_, (1024, 512, 256, 128))

    def body(d_ref, w_ref, g_ref, u_ref, dg_ref, du_ref):
        for rs in _row_slices(tm, 16):
            dact = _dot(_mx(d_ref[rs, :]), _mx(w_ref[...]), 1, 1)
            dg_ref[rs, :] = (dact * g_ref[rs, :].astype(F32)).astype(dg_ref.dtype)
            du_ref[rs, :] = (dact * u_ref[rs, :].astype(F32)).astype(du_ref.dtype)

    osp = pl.BlockSpec((None, tm, f8), lambda j, i: (j, i, 0))
    return pl.pallas_call(
        body, name=name, grid=(nb, s_ // tm),
        in_specs=[pl.BlockSpec((tm, d), lambda j, i: (i, 0)), pl.BlockSpec((None, f8, d), lambda j, i: (j, 0, 0)),
                  osp, osp],
        out_specs=[osp, osp], out_shape=[jax.ShapeDtypeStruct((nb, s_, f8), MXU_DTYPE)] * 2,
        compiler_params=_params("parallel", "parallel"),
    )(dffn, w_down, gate, up)


def _ffn_bwd_in(dgate, w_gate, dup, w_up, name="ffn_d_in"):
    nb, s_, f8 = dgate.shape
    d = w_gate.shape[2]
    tm = _pick(s_, (1024, 512, 256, 128))
    tn = _pick(d, (2048, 1024, 512, 256, 128))

    def body(dg_ref, wg_ref, du_ref, wu_ref, o_ref, acc):
        j = pl.program_id(2)

        @pl.when(j == 0)
        def _():
            acc[...] = jnp.zeros_like(acc)

        for rs in _row_slices(tm, 16):
            acc[rs, :] += (_dot(_mx(dg_ref[rs, :]), _mx(wg_ref[...]), 1, 0)
                           + _dot(_mx(du_ref[rs, :]), _mx(wu_ref[...]), 1, 0))

        @pl.when(j == nb - 1)
        def _():
            o_ref[...] = acc[...]

    asp = pl.BlockSpec((None, tm, f8), lambda i, n, j: (j, i, 0))
    wsp = pl.BlockSpec((None, f8, tn), lambda i, n, j: (j, 0, n))
    return pl.pallas_call(
        body, name=name, grid=(s_ // tm, d // tn, nb),
        in_specs=[asp, wsp, asp, wsp], out_specs=pl.BlockSpec((tm, tn), lambda i, n, j: (i, n)),
        out_shape=jax.ShapeDtypeStruct((s_, d), F32), scratch_shapes=[pltpu.VMEM((tm, tn), F32)],
        compiler_params=_params("parallel", "parallel", "arbitrary"),
    )(dgate, w_gate, dup, w_up)


def _adam_math(g, w, m, v):
    m2 = ADAM_B1 * m + (1.0 - ADAM_B1) * g
    v2 = ADAM_B2 * v + (1.0 - ADAM_B2) * (g * g)
    m_hat = m2 / (1.0 - ADAM_B1 ** ADAM_STEP)
    v_hat = v2 / (1.0 - ADAM_B2 ** ADAM_STEP)
    delta = -ADAM_LR * (m_hat / (jnp.sqrt(v_hat) + ADAM_EPS) + ADAM_WD * w)
    return delta, m2, v2


def _adamw(parts, own, me, w, m, v, name="adamw"):
    nd, r_, c = parts.shape
    tr = _pick(r_, (128, 64, 32, 16))
    tc = c
    if tr == r_ and r_ > 128:
        tc = _pick(c, (256, 128))

    def body(me_ref, p_ref, own_ref, w_ref, m_ref, v_ref, g_ref, d_ref, m2_ref, v2_ref):
        mine = me_ref[0]
        g = jnp.zeros((tr, tc), F32)
        for i in range(nd):
            g = g + jnp.where(mine == i, own_ref[...], p_ref[i]).astype(F32)
        delta, m2, v2 = _adam_math(g, w_ref[...], m_ref[...], v_ref[...])
        g_ref[...] = g
        d_ref[...] = delta
        m2_ref[...] = m2
        v2_ref[...] = v2

    row = pl.BlockSpec((tr, tc), lambda i, j, me_: (i, j))
    gs = pltpu.PrefetchScalarGridSpec(
        num_scalar_prefetch=1, grid=(r_ // tr, c // tc),
        in_specs=[pl.BlockSpec((nd, tr, tc), lambda i, j, me_: (0, i, j)),
                  pl.BlockSpec((None, tr, tc), lambda i, j, me_: (me_[0], i, j)), row, row, row],
        out_specs=[row] * 4)
    return pl.pallas_call(
        body, name=name, grid_spec=gs, out_shape=[jax.ShapeDtypeStruct((r_, c), F32)] * 4,
        compiler_params=_params("parallel", "parallel"),
    )(me, parts, own, w, m, v)


def _adamw_small(parts, w, m, v, name="adamw_small"):
    nd = parts.shape[0]

    def body(p_ref, w_ref, m_ref, v_ref, g_ref, d_ref, m2_ref, v2_ref):
        g = p_ref[0]
        for i in range(1, nd):
            g = g + p_ref[i]
        delta, m2, v2 = _adam_math(g, w_ref[...], m_ref[...], v_ref[...])
        g_ref[...] = g
        d_ref[...] = delta
        m2_ref[...] = m2
        v2_ref[...] = v2

    return pl.pallas_call(
        body, name=name, out_shape=[jax.ShapeDtypeStruct(w.shape, F32)] * 4,
        compiler_params=pltpu.CompilerParams(vmem_limit_bytes=VMEM_LIMIT_BYTES),
    )(parts, w, m, v)


_HBM = pl.BlockSpec(memory_space=pltpu.HBM)
_MESH = pl.DeviceIdType.MESH


def _all_gather(xs, name):
    na = len(xs)

    def body(*refs):
        x_refs, out_refs = refs[:na], refs[na:2 * na]
        send_sems, recv_sems, local_sems = refs[2 * na:]
        x, y, c = lax.axis_index("x"), lax.axis_index("y"), lax.axis_index("c")
        me, sibling = (x, y, c), (x, y, 1 - c)
        chips = [(1 - x, y), (x, 1 - y), (1 - x, 1 - y)]

        def slot(a, px, py, pc):
            return out_refs[a].at[4 * px + 2 * py + pc]

        def copy(a, k, block, to, src=None):
            return pltpu.make_async_remote_copy(
                src_ref=slot(a, *block) if src is None else src, dst_ref=slot(a, *block),
                send_sem=send_sems.at[a, k], recv_sem=recv_sems.at[a, k], device_id=to, device_id_type=_MESH)

        mine = [pltpu.make_async_copy(x_refs[a], slot(a, *me), local_sems.at[a]) for a in range(na)]
        started = []
        for a in range(na):
            mine[a].start()
            first = [copy(a, 0, me, sibling, src=x_refs[a])]
            first += [copy(a, 1 + j, me, (*chip, c), src=x_refs[a]) for j, chip in enumerate(chips)]
            for cp in first:
                cp.start()
            started += first
        for a in range(na):
            for j, chip in enumerate(chips):
                copy(a, 1 + j, (*chip, c), me).wait_recv()
                fwd = copy(a, 4 + j, (*chip, c), sibling)
                fwd.start()
                started.append(fwd)
        for a in range(na):
            copy(a, 0, sibling, me).wait_recv()
            for j, chip in enumerate(chips):
                copy(a, 4 + j, (*chip, 1 - c), me).wait_recv()
        for cp in started:
            cp.wait_send()
        for cp in mine:
            cp.wait()

    return pl.pallas_call(
        body, name=name, out_shape=[jax.ShapeDtypeStruct((N_DEV,) + t.shape, t.dtype) for t in xs],
        in_specs=[_HBM] * na, out_specs=[_HBM] * na,
        scratch_shapes=[pltpu.SemaphoreType.DMA((na, 7)), pltpu.SemaphoreType.DMA((na, 7)),
                        pltpu.SemaphoreType.DMA((na,))],
    )(*xs)


_SEM = pl.BlockSpec(memory_space=pltpu.SEMAPHORE)
_EFFECT = pltpu.SideEffectType.DATAFLOW_SIDE_EFFECTING


def _peers(x, y, c):
    out = []
    for k in range(1, N_DEV):
        px = 1 - x if k & 4 else x
        py = 1 - y if k & 2 else y
        pc = 1 - c if k & 1 else c
        out.append(((px, py, pc), 4 * px + 2 * py + pc))
    return out


def _push_copies(scatter, src_refs, land_refs, send_sems, recv_sems):
    x, y, c = lax.axis_index("x"), lax.axis_index("y"), lax.axis_index("c")
    me = 4 * x + 2 * y + c
    pairs = []
    for a, (src, land) in enumerate(zip(src_refs, land_refs)):
        for k, (peer, slot) in enumerate(_peers(x, y, c)):
            out_src = src.at[slot] if scatter else src
            si = a * (N_DEV - 1) + k
            send = pltpu.make_async_remote_copy(src_ref=out_src, dst_ref=land.at[me], send_sem=send_sems.at[si],
                                                recv_sem=recv_sems.at[si], device_id=peer, device_id_type=_MESH)
            recv = pltpu.make_async_remote_copy(src_ref=out_src, dst_ref=land.at[slot], send_sem=send_sems.at[si],
                                                recv_sem=recv_sems.at[si], device_id=peer, device_id_type=_MESH)
            pairs.append((send, recv))
    return pairs


def _push_start(srcs, scatter, dep, name):
    na = len(srcs)
    shapes = [t.shape[1:] if scatter else t.shape for t in srcs]
    lands = [pltpu.with_memory_space_constraint(lax.empty((N_DEV,) + s, t.dtype), pltpu.HBM) for s, t in zip(shapes, srcs)]

    def body(*refs):
        src_refs, land_refs = refs[:na], refs[na:2 * na]
        send_sems, recv_sems = refs[2 * na + 1], refs[2 * na + 2]
        token = refs[-1]
        for send, _ in _push_copies(scatter, src_refs, land_refs, send_sems, recv_sems):
            send.start()
        token[...] = jnp.zeros_like(token)

    sem = pltpu.SemaphoreType.DMA((na * (N_DEV - 1),))
    outs = pl.pallas_call(
        body, name=name,
        out_shape=(sem, sem) + tuple(pltpu.HBM(t.shape, t.dtype) for t in srcs)
        + tuple(pltpu.HBM(t.shape, t.dtype) for t in lands) + (jax.ShapeDtypeStruct((8, LANE), F32),),
        in_specs=[_HBM] * (2 * na) + [pl.BlockSpec(memory_space=pl.ANY)],
        out_specs=(_SEM, _SEM) + (_HBM,) * (2 * na) + (pl.BlockSpec(memory_space=pltpu.VMEM),),
        input_output_aliases={i: 2 + i for i in range(2 * na)},
        compiler_params=pltpu.CompilerParams(has_side_effects=_EFFECT),
    )(*[pltpu.with_memory_space_constraint(t, pltpu.HBM) for t in srcs], *lands, dep)
    return outs[0], outs[1], outs[2:2 + na], outs[2 + na:2 + 2 * na], outs[-1]


def _push_wait(send_sems, recv_sems, src_thru, land_thru, scatter, after, name):
    na = len(src_thru)

    def body(*refs):
        src_refs, land_refs = refs[:na], refs[na:2 * na]
        ssem, rsem = refs[2 * na], refs[2 * na + 1]
        for send, recv in _push_copies(scatter, src_refs, land_refs, ssem, rsem):
            send.wait_send()
            recv.wait_recv()

    outs = pl.pallas_call(
        body, name=name,
        out_shape=tuple(pltpu.HBM(t.shape, t.dtype) for t in src_thru) + tuple(pltpu.HBM(t.shape, t.dtype) for t in land_thru),
        in_specs=[_HBM] * (2 * na) + [_SEM, _SEM, pl.BlockSpec(memory_space=pl.ANY)],
        out_specs=(_HBM,) * (2 * na),
        input_output_aliases={i: i for i in range(2 * na)},
        compiler_params=pltpu.CompilerParams(has_side_effects=_EFFECT),
    )(*src_thru, *land_thru, send_sems, recv_sems, after)
    return outs[:na], outs[na:]


def _exchange_behind(srcs, scatter, dep, name):
    send_sems, recv_sems, thru, lands, token = _push_start(srcs, scatter, dep, name + "_start")

    def finish(after, place=True):
        src_done, land_done = _push_wait(send_sems, recv_sems, thru, lands, scatter, after, name + "_wait")
        if not place:
            return land_done, src_done
        return _place_own(land_done, src_done, scatter, name + "_own")

    return token[0, 0], finish


def _place_own(lands, srcs, scatter, name):
    me = (4 * lax.axis_index("x") + 2 * lax.axis_index("y") + lax.axis_index("c")).astype(jnp.int32).reshape(1)
    outs = []
    for a, (land, src) in enumerate(zip(lands, srcs)):
        r_, c_ = land.shape[1:]
        tr = _pick(r_, (512, 256, 128, 64, 32, 16))

        def body(me_ref, land_ref, src_ref, out_ref):
            out_ref[...] = src_ref[...]

        src_spec = (pl.BlockSpec((None, tr, c_), lambda i, me_: (me_[0], i, 0)) if scatter
                    else pl.BlockSpec((tr, c_), lambda i, me_: (i, 0)))
        gs = pltpu.PrefetchScalarGridSpec(
            num_scalar_prefetch=1, grid=(r_ // tr,),
            in_specs=[pl.BlockSpec(memory_space=pl.ANY), src_spec],
            out_specs=pl.BlockSpec((None, tr, c_), lambda i, me_: (me_[0], i, 0)))
        outs.append(pl.pallas_call(
            body, name=f"{name}_{a}", grid_spec=gs, out_shape=jax.ShapeDtypeStruct(land.shape, land.dtype),
            input_output_aliases={1: 0}, compiler_params=_params("arbitrary"),
        )(me, land, src))
    return outs


_BIG = (("w_in", D_MODEL, D_IN, 1), ("w_uq", Q_RANK, HEADS * QK, 1), ("w_ukv", KV_RANK, HEADS * (NOPE + VDIM), 1),
        ("w_out", D_MODEL, D_MODEL, 0), ("w_gate", D_MODEL, D_FF, 1), ("w_up", D_MODEL, D_FF, 1),
        ("w_down", D_FF, D_MODEL, 0))
_TRANSPOSED = ("w_in", "w_uq", "w_gate", "w_up")
_CQKV = (0, Q_RANK + KV_RANK)
_KR = (_CQKV[1], _CQKV[1] + ROPE)
_Z = (_KR[1], _KR[1] + SSD_W)
_XBC = (_Z[1], _Z[1] + CONV_DIM)
_DT = (_XBC[1], _XBC[1] + SSD_H)


def _win_segments(w_in_t):
    w = w_in_t.reshape(D_IN, D_MODEL)
    small = jnp.concatenate([w[_KR[0]:_KR[1]], w[_DT[0]:_DT[1]],
                             jnp.zeros((LANE - ROPE - SSD_H, D_MODEL), w.dtype)], axis=0)
    return w[_CQKV[0]:_CQKV[1]], w[_Z[0]:_Z[1]], w[_XBC[0]:_XBC[1]], small


def _win_from_segments(g_cqkv, g_z, g_xbc, g_small):
    w = jnp.concatenate([g_cqkv, g_small[:ROPE], g_z, g_xbc, g_small[ROPE:ROPE + SSD_H]], axis=0)
    return w.reshape(N_DEV, D_IN // N_DEV, D_MODEL)


_SMALL = (("q_norm_w", 512), ("kv_norm_w", 512), ("conv_b", CONV_DIM), ("dt_bias", SSD_H), ("a_log", SSD_H),
          ("d_skip", SSD_H), ("ssd_norm_w", SSD_W), ("attn_out_norm_w", 1024), ("pre_mix_norm_w", D_MODEL),
          ("post_mix_norm_w", D_MODEL), ("pre_ffn_norm_w", D_MODEL), ("post_ffn_norm_w", D_MODEL),
          ("conv_w", CONV_K * CONV_DIM))
_SMALL_ROWS = -(-sum(-(-n // LANE) for _, n in _SMALL) // 8) * 8


def _pack_small(vals):
    rows = []
    for name, n in _SMALL:
        v = vals[name].reshape(-1).astype(F32)
        pad = -(-n // LANE) * LANE
        rows.append(jnp.pad(v, (0, pad - n)).reshape(-1, LANE))
    m = jnp.concatenate(rows, axis=0)
    return jnp.pad(m, ((0, _SMALL_ROWS - m.shape[0]), (0, 0)))


def _unpack_small(m):
    out, r = {}, 0
    for name, n in _SMALL:
        nr = -(-n // LANE)
        out[name] = m[r:r + nr].reshape(-1)[:n]
        r += nr
    return out


def _head_row(v):
    return jnp.pad(v.reshape(1, -1).astype(F32), ((0, 0), (HEAD_LANE, LANE - HEAD_LANE - v.shape[-1])))


def _local_step(x, positions, target, wg, small, weights, on_grads):
    w_cqkv, w_z, w_xbc, w_small = _win_segments(wg["w_in"])
    conv_w = wg["conv_w"]
    conv_b = small["conv_b"].reshape(1, CONV_DIM)
    qkv_norm_w = jnp.concatenate([small["q_norm_w"], small["kv_norm_w"]])
    attn_norm_w = small["attn_out_norm_w"].reshape(1, HEADS * VDIM)
    scale = QK ** -0.5

    inv_freq = ROPE_THETA ** (-jnp.arange(0, ROPE, 2, dtype=F32) / ROPE)
    ang = positions.astype(F32)[:, None] * inv_freq
    cos2 = jnp.tile(jnp.cos(ang), (1, 2))
    sin2 = jnp.tile(jnp.sin(ang), (1, 2))

    u = _rms_fwd(x, small["pre_mix_norm_w"], out_dtype=MXU_DTYPE, name="pre_mix_norm")
    cqkv = _mm(u, w_cqkv, "nt", name="in_proj_qkv")
    z = _mm(u, w_z, "nt", name="in_proj_z")
    xbc = _mm(u, w_xbc, "nt", name="in_proj_xbc")
    sm = _mm(u, w_small, "nt", name="in_proj_small")

    w_uq, w_ukv = weights("qkv_up", cqkv)
    qkvn = _rms_fwd(cqkv, qkv_norm_w, groups=2, out_dtype=MXU_DTYPE, name="qkv_norm")
    q_h = _q_up(qkvn, w_uq, cos2, sin2, scale)
    k_h, v_h = _kv_up(qkvn, w_ukv, sm, cos2, sin2)
    o_h, lse = _flash_fwd(q_h, k_h, v_h)
    cat = _hnorm_fwd(o_h, attn_norm_w, D_MODEL)
    w_out = weights("out", o_h)[0].reshape(D_MODEL, D_MODEL)

    xbc_act = _conv_fwd(xbc, conv_w, conv_b)
    dtt = jnp.transpose(sm[:, HEAD_LANE:HEAD_LANE + SSD_H])
    ssd_args = (xbc_act, sm, dtt, _head_row(small["dt_bias"]), small["dt_bias"].reshape(SSD_H, 1),
                _head_row(small["a_log"]), small["a_log"].reshape(SSD_H, 1),
                jnp.broadcast_to(small["d_skip"].reshape(SSD_H, 1), (SSD_H, SSD_P)).reshape(SSD_PAIRS, 1, LANE))
    y_ssd, prev = _ssd_fwd(*ssd_args)
    cat = _gated_norm_fwd(y_ssd, z, small["ssd_norm_w"], cat)

    mix = _mm(cat, w_out, "nn", name="out_proj")
    h1, vv = _norm_res_norm(mix, x, small["post_mix_norm_w"], small["pre_ffn_norm_w"])

    w_gate, w_up = weights("ffn_in", mix)
    gate, up, act = _ffn_fwd(vv, w_gate, w_up)
    w_down, = weights("ffn_out", act)
    ffn = _mm(act, w_down, "nn", a_blk=True, b_blk=True, fuse=2, wide=True, name="ffn_down")
    loss_blk, dy, dffn, g_post_ffn = _loss_head(ffn, h1, target, small["post_ffn_norm_w"])

    g_down = _mm(act, dffn, "tn", a_blk=True, out_blk=True, out_dtype=MXU_DTYPE, name="g_down")
    dgate, dup = _ffn_bwd_act(dffn, w_down, gate, up)
    dvv = _ffn_bwd_in(dgate, w_gate, dup, w_up)
    g_gate = _mm(dgate, vv, "tn", a_blk=True, out_blk=True, out_dtype=MXU_DTYPE, name="g_gate")
    g_up = _mm(dup, vv, "tn", a_blk=True, out_blk=True, out_dtype=MXU_DTYPE, name="g_up")
    pre_ffn_w = small["pre_ffn_norm_w"] + on_grads("ffn", [g_gate, g_up, g_down])
    dh1, dmix, g_pre_ffn, g_post_mix = _norm_res_norm_bwd(h1, pre_ffn_w, dvv, dy, mix, small["post_mix_norm_w"])

    dcat = _mm(dmix, w_out, "nt", name="d_cat")
    g_out = _mm(cat, dmix, "tn", out_dtype=MXU_DTYPE, name="g_out")

    do_h, delta, g_attn_norm = _hnorm_bwd(o_h, attn_norm_w, dcat)
    dq_h, dk_h, dv_h = _flash_bwd(q_h, k_h, v_h, do_h, lse, delta)
    dq = _q_prep(dq_h, cos2, -sin2, scale, name="dq_post")

    dy_ssd, dz, g_ssd_norm = _gated_norm_bwd(y_ssd, z, small["ssd_norm_w"], dcat)
    dxbc_act, ddt, dpar = _ssd_bwd(*ssd_args, prev, dy_ssd)
    dkv, dsm = _dkv_post(dk_h, dv_h, ddt, cos2, -sin2)
    dpre, dwb = _conv_bwd_pre(xbc, conv_w, conv_b, dxbc_act)
    dxbc = _conv_bwd_in(dpre, conv_w)

    dqn = _mm(dq, w_uq, "nn", a_blk=True, b_blk=True, fuse=HEADS, name="d_qn")
    dkvn = _mm(dkv, w_ukv, "nt", a_blk=True, b_blk=True, fuse=HEADS, name="d_kvn")
    g_uq = _mm(dq, qkvn, "tn", a_blk=True, out_blk=True, b_cols=(0, Q_RANK), out_dtype=MXU_DTYPE, name="g_uq")
    g_ukv = _mm(qkvn, dkv, "tn", b_blk=True, out_blk=True, a_cols=(Q_RANK, KV_RANK), out_dtype=MXU_DTYPE, name="g_ukv")
    heads_token = on_grads("heads", [g_uq, g_ukv, g_out.reshape(N_DEV, D_MODEL // N_DEV, D_MODEL)])
    dcqkv, g_qkv_norm = _rms_bwd(cqkv, qkv_norm_w + heads_token, [dqn, dkvn], out_dtype=MXU_DTYPE, name="qkv_norm_bwd")

    g_in = _win_from_segments(_mm(dcqkv, u, "tn", out_dtype=MXU_DTYPE, name="g_in_qkv"),
                              _mm(dz, u, "tn", out_dtype=MXU_DTYPE, name="g_in_z"),
                              _mm(dxbc, u, "tn", out_dtype=MXU_DTYPE, name="g_in_xbc"),
                              _mm(dsm, u, "tn", out_dtype=MXU_DTYPE, name="g_in_small"))
    in_token = on_grads("in", [g_in])
    du = _mm_sum([dsm + in_token.astype(dsm.dtype), dcqkv, dz, dxbc], [w_small, w_cqkv, w_z, w_xbc], name="d_u")
    dx, g_pre_mix = _rms_bwd(x, small["pre_mix_norm_w"], [du], res=dh1, name="pre_mix_norm_bwd")

    hl = slice(HEAD_LANE, HEAD_LANE + SSD_H)
    g_small = {"q_norm_w": g_qkv_norm[0, :Q_RANK], "kv_norm_w": g_qkv_norm[0, Q_RANK:], "conv_b": dwb[CONV_K],
               "dt_bias": dpar[0, hl], "a_log": dpar[1, hl], "d_skip": dpar[2, hl], "ssd_norm_w": g_ssd_norm,
               "attn_out_norm_w": g_attn_norm, "pre_mix_norm_w": g_pre_mix, "post_mix_norm_w": g_post_mix,
               "pre_ffn_norm_w": g_pre_ffn, "post_ffn_norm_w": g_post_ffn, "conv_w": dwb[:CONV_K]}
    return loss_blk[0, 0], dx, g_small


_WEIGHT_ORDER = ("w_in", "q_norm_w", "w_uq", "kv_norm_w", "w_ukv", "conv_w", "conv_b", "dt_bias", "a_log", "d_skip",
                 "ssd_norm_w", "attn_out_norm_w", "w_out", "pre_mix_norm_w", "post_mix_norm_w", "pre_ffn_norm_w",
                 "post_ffn_norm_w", "w_gate", "w_up", "w_down")


def kernel(x, positions, w_in, q_norm_w, w_uq, kv_norm_w, w_ukv, conv_w, conv_b, dt_bias, a_log, d_skip, ssd_norm_w, attn_out_norm_w, w_out, pre_mix_norm_w, post_mix_norm_w, pre_ffn_norm_w, post_ffn_norm_w, w_gate, w_up, w_down, loss_target, m_w_in, m_q_norm_w, m_w_uq, m_kv_norm_w, m_w_ukv, m_conv_w, m_conv_b, m_dt_bias, m_a_log, m_d_skip, m_ssd_norm_w, m_attn_out_norm_w, m_w_out, m_pre_mix_norm_w, m_post_mix_norm_w, m_pre_ffn_norm_w, m_post_ffn_norm_w, m_w_gate, m_w_up, m_w_down, v_w_in, v_q_norm_w, v_w_uq, v_kv_norm_w, v_w_ukv, v_conv_w, v_conv_b, v_dt_bias, v_a_log, v_d_skip, v_ssd_norm_w, v_attn_out_norm_w, v_w_out, v_pre_mix_norm_w, v_post_mix_norm_w, v_pre_ffn_norm_w, v_post_ffn_norm_w, v_w_gate, v_w_up, v_w_down):
    w = dict(w_in=w_in, q_norm_w=q_norm_w, w_uq=w_uq, kv_norm_w=kv_norm_w, w_ukv=w_ukv, conv_w=conv_w, conv_b=conv_b,
             dt_bias=dt_bias, a_log=a_log, d_skip=d_skip, ssd_norm_w=ssd_norm_w, attn_out_norm_w=attn_out_norm_w,
             w_out=w_out, pre_mix_norm_w=pre_mix_norm_w, post_mix_norm_w=post_mix_norm_w,
             pre_ffn_norm_w=pre_ffn_norm_w, post_ffn_norm_w=post_ffn_norm_w, w_gate=w_gate, w_up=w_up, w_down=w_down)
    m = dict(w_in=m_w_in, q_norm_w=m_q_norm_w, w_uq=m_w_uq, kv_norm_w=m_kv_norm_w, w_ukv=m_w_ukv, conv_w=m_conv_w,
             conv_b=m_conv_b, dt_bias=m_dt_bias, a_log=m_a_log, d_skip=m_d_skip, ssd_norm_w=m_ssd_norm_w,
             attn_out_norm_w=m_attn_out_norm_w, w_out=m_w_out, pre_mix_norm_w=m_pre_mix_norm_w,
             post_mix_norm_w=m_post_mix_norm_w, pre_ffn_norm_w=m_pre_ffn_norm_w, post_ffn_norm_w=m_post_ffn_norm_w,
             w_gate=m_w_gate, w_up=m_w_up, w_down=m_w_down)
    v = dict(w_in=v_w_in, q_norm_w=v_q_norm_w, w_uq=v_w_uq, kv_norm_w=v_kv_norm_w, w_ukv=v_w_ukv, conv_w=v_conv_w,
             conv_b=v_conv_b, dt_bias=v_dt_bias, a_log=v_a_log, d_skip=v_d_skip, ssd_norm_w=v_ssd_norm_w,
             attn_out_norm_w=v_attn_out_norm_w, w_out=v_w_out, pre_mix_norm_w=v_pre_mix_norm_w,
             post_mix_norm_w=v_post_mix_norm_w, pre_ffn_norm_w=v_pre_ffn_norm_w, post_ffn_norm_w=v_post_ffn_norm_w,
             w_gate=v_w_gate, w_up=v_w_up, w_down=v_w_down)
    w, m, v = ({k: t[0] for k, t in d.items()} for d in (w, m, v))
    me = 4 * lax.axis_index("x") + 2 * lax.axis_index("y") + lax.axis_index("c")
    groups = {"qkv_up": ("w_uq", "w_ukv"), "out": ("w_out",), "ffn_in": ("w_gate", "w_up"), "ffn_out": ("w_down",)}
    cshard = CONV_DIM // N_DEV
    for name in _TRANSPOSED:
        w[name], m[name], v[name] = w[name].T, m[name].T, v[name].T

    shards = [w["w_in"].astype(MXU_DTYPE),
              jnp.stack(_split3(w["conv_w"])).reshape(3 * CONV_K, cshard).astype(MXU_DTYPE)]
    w_in_g, cw = _all_gather(shards, name="gather_weights")
    cw = cw.astype(F32).reshape(N_DEV, 3, CONV_K, cshard)
    wg = {"w_in": w_in_g, "conv_w": jnp.transpose(cw[:, 0] + cw[:, 1] + cw[:, 2], (1, 0, 2)).reshape(CONV_K, CONV_DIM)}
    arriving, dep, started = {}, wg["conv_w"], jnp.zeros((), F32)
    small = {name: w[name] for name, _ in _SMALL if name != "conv_w"}
    for group in ("qkv_up", "out", "ffn_in", "ffn_out"):
        token, arriving[group] = _exchange_behind([w[name].astype(MXU_DTYPE) for name in groups[group]], False,
                                                  dep, group + "_weights")
        started = started + token
        dep = jnp.zeros((8, LANE), F32) + started
    small["pre_mix_norm_w"] = small["pre_mix_norm_w"] + started

    leaving = {}

    def on_grads(group, gs):
        token, leaving[group] = _exchange_behind(gs, True, jnp.zeros((8, LANE), F32), group + "_grads")
        return token

    loss_local, dx, g_small = _local_step(x[0], positions[0], loss_target[0], wg, small,
                                          lambda group, after: arriving[group](after), on_grads)
    loss = lax.psum(loss_local, ("x", "y", "c"))

    recv = {}
    for group, names in (("ffn", ("w_gate", "w_up", "w_down")), ("heads", ("w_uq", "w_ukv", "w_out")), ("in", ("w_in",))):
        recv.update(zip(names, zip(*leaving[group](dx, place=False))))
    grads, deltas, new_m, new_v = {}, {}, {}, {}
    me1 = me.astype(jnp.int32).reshape(1)
    for name, (parts, own) in recv.items():
        outs = _adamw(parts, own, me1, w[name], m[name], v[name], name="adamw_" + name)
        if name in _TRANSPOSED:
            outs = [t.T for t in outs]
        grads[name], deltas[name], new_m[name], new_v[name] = outs

    def embed(t):
        return lax.dynamic_update_slice(jnp.zeros((CONV_K, CONV_DIM), F32), t, (0, me * cshard))

    parts_s = _all_gather([_pack_small(g_small)], name="gather_small_grads")[0]
    packs = [_pack_small({**{n_: d[n_] for n_, _ in _SMALL if n_ != "conv_w"}, "conv_w": embed(d["conv_w"])})
             for d in (w, m, v)]
    outs = [_unpack_small(t) for t in _adamw_small(parts_s, *packs)]
    for name, n in _SMALL:
        for dst, src in zip((grads, deltas, new_m, new_v), outs):
            if name == "conv_w":
                dst[name] = lax.dynamic_slice(src[name].reshape(CONV_K, CONV_DIM), (0, me * cshard), (CONV_K, cshard))
            else:
                dst[name] = src[name]

    def lead(d):
        return [d[name][None] for name in _WEIGHT_ORDER]

    return (loss, dx[None], *lead(grads), *lead(deltas), *lead(new_m), *lead(new_v))
```

```python
import numpy as np

import jax
import jax.numpy as jnp
from jax import lax
from jax.experimental import pallas as pl
from jax.experimental.pallas import tpu as pltpu

F32 = jnp.float32
BF16 = jnp.bfloat16
MXU_DTYPE = jnp.bfloat16
EPS = 1e-6
VMEM_LIMIT_BYTES = 48 * 1024 * 1024
K_TILE_MAX = 2048

N_DEV = 8
D_MODEL = 2048
Q_RANK = 512
KV_RANK = 512
ROPE = 64
HALF = ROPE // 2
HEADS = 8
NOPE = 128
VDIM = 128
QK = NOPE + ROPE
SSD_W = 1024
SSD_H = 16
SSD_P = 64
SSD_G = 2
SSD_E = SSD_H // SSD_G
SSD_N = 128
CHUNK = 128
CONV_K = 4
CONV_DIM = SSD_W + 2 * SSD_G * SSD_N
B_OFF = SSD_W
C_OFF = SSD_W + SSD_G * SSD_N
D_FF = 5632
D_IN = Q_RANK + KV_RANK + ROPE + SSD_W + CONV_DIM + SSD_H
ROPE_THETA = 10000.0
LANE = 128
HEAD_LANE = ROPE

ADAM_LR = 0.001
ADAM_B1 = 0.9
ADAM_B2 = 0.999
ADAM_EPS = 1e-08
ADAM_WD = 0.01
ADAM_STEP = 10


def _pick(n, cands):
    for c in cands:
        if n % c == 0:
            return c
    return n


def _params(*sem):
    return pltpu.CompilerParams(dimension_semantics=sem, vmem_limit_bytes=VMEM_LIMIT_BYTES)


def _sigmoid(x):
    return 1.0 / (1.0 + jnp.exp(-x))


def _silu(x):
    return x * _sigmoid(x)


def _dsilu(x):
    s = _sigmoid(x)
    return s * (1.0 + x * (1.0 - s))


def _softplus(x):
    e = jnp.exp(-jnp.abs(x))
    small = e * (1.0 - e * (0.5 - e * (1.0 / 3.0)))
    return jnp.maximum(x, 0.0) + jnp.where(e < 0.01, small, jnp.log(1.0 + e))


def _dot(a, b, ca, cb):
    return lax.dot_general(a, b, (((ca,), (cb,)), ((), ())), preferred_element_type=F32)


def _mx(v):
    return v.astype(MXU_DTYPE)


def _split3(a):
    hi = a.astype(BF16)
    r1 = a - hi.astype(F32)
    mid = r1.astype(BF16)
    lo = (r1 - mid.astype(F32)).astype(BF16)
    return hi, mid, lo


def _exact_dot(a, b, ca, cb, split_a):
    if split_a:
        return sum(_dot(p, b, ca, cb) for p in _split3(a))
    return sum(_dot(a, p, ca, cb) for p in _split3(b))


MM_ROW_GROUPS = 4


def _row_slices(tm, align):
    ng = MM_ROW_GROUPS
    while ng > 1 and (tm % ng or (tm // ng) % align):
        ng //= 2
    return [slice(g * (tm // ng), (g + 1) * (tm // ng)) for g in range(ng)]


def _mm(a, b, mode, *, a_blk=False, b_blk=False, out_blk=False, a_cols=None, b_cols=None, add=None, out_dtype=F32,
        fuse=1, wide=False, name="mm"):
    a2, b2 = a.shape[-2:], b.shape[-2:]
    a_last = a2[1] if a_cols is None else a_cols[1]
    a_start = 0 if a_cols is None else a_cols[0]
    b_start = 0
    if b_cols is not None:
        assert mode != "nt"
        b_start, b2 = b_cols[0], (b2[0], b_cols[1])
    if mode == "nn":
        m, k, (k2, n) = a2[0], a_last, b2
    elif mode == "nt":
        m, k, (n, k2) = a2[0], a_last, b2
    else:
        k, m, (k2, n) = a2[0], a_last, b2
    assert k == k2, (a.shape, b.shape, mode)
    tm = _pick(m, (1024, 704, 512, 256, 128))
    tn = _pick(n, ((2048,) if wide else ()) + (1024, 768, 704, 512, 256, 192, 128))
    tk = k if k <= K_TILE_MAX else _pick(k, (K_TILE_MAX, 1024, 512))
    nk = k // tk
    jo = N_DEV if out_blk else 1
    reduce_blocks = a_blk and b_blk and not out_blk
    assert fuse == 1 or reduce_blocks
    jr = N_DEV // fuse if reduce_blocks else 1
    ca, cb = {"nn": (1, 0), "nt": (1, 1), "tn": (0, 0)}[mode]
    has_add = add is not None
    single = jr * nk == 1
    if mode == "tn":
        assert a_start % tm == 0
        a_block, a_idx = (tk, tm), (lambda i, kk: (kk, i + a_start // tm))
    else:
        assert a_start % tk == 0
        a_block, a_idx = (tm, tk), (lambda i, kk: (i, kk + a_start // tk))
    assert b_start % tn == 0
    b_block, b_idx = (((tn, tk), (lambda nn_, kk: (nn_, kk))) if mode == "nt"
                      else ((tk, tn), (lambda nn_, kk: (kk, nn_ + b_start // tn))))

    def blk_specs(blocked, block, idx, of_a, t):
        def pos(o, i, nn_, kk):
            return idx(i, kk) if of_a else idx(nn_, kk)
        if blocked:
            return pl.BlockSpec((None,) + block,
                                lambda o, i, nn_, r, kk: ((o if out_blk else r * fuse + t),) + pos(o, i, nn_, kk))
        return pl.BlockSpec(block, lambda o, i, nn_, r, kk: pos(o, i, nn_, kk))

    a_specs = [blk_specs(a_blk, a_block, a_idx, True, t) for t in range(fuse)]
    b_specs = [blk_specs(b_blk, b_block, b_idx, False, t) for t in range(fuse)]
    o_spec = (pl.BlockSpec((None, tm, tn), lambda o, i, nn_, r, kk: (o, i, nn_)) if out_blk
              else pl.BlockSpec((tm, tn), lambda o, i, nn_, r, kk: (i, nn_)))

    groups = _row_slices(tm, LANE if mode == "tn" else 16)

    def body(*refs):
        a_refs, b_refs = refs[:fuse], refs[fuse:2 * fuse]
        add_ref = refs[2 * fuse] if has_add else None
        o_ref = refs[2 * fuse + 1] if has_add else refs[2 * fuse]

        def partial(rs):
            out = None
            for t in range(fuse):
                av = a_refs[t][:, rs] if mode == "tn" else a_refs[t][rs, :]
                d = _dot(_mx(av), _mx(b_refs[t][...]), ca, cb)
                out = d if out is None else out + d
            return out

        if single:
            for rs in groups:
                res = partial(rs)
                if has_add:
                    res = res + add_ref[rs, :]
                o_ref[rs, :] = res.astype(o_ref.dtype)
            return
        acc = refs[-1]
        r, kk = pl.program_id(3), pl.program_id(4)

        @pl.when(jnp.logical_and(r == 0, kk == 0))
        def _():
            acc[...] = jnp.zeros_like(acc)

        for rs in groups:
            acc[rs, :] += partial(rs)

        @pl.when(jnp.logical_and(r == jr - 1, kk == nk - 1))
        def _():
            res = acc[...]
            if has_add:
                res = res + add_ref[...]
            o_ref[...] = res.astype(o_ref.dtype)

    out_shape = ((N_DEV, m, n) if out_blk else (m, n))
    return pl.pallas_call(
        body, name=name, grid=(jo, m // tm, n // tn, jr, nk),
        in_specs=a_specs + b_specs + ([o_spec] if has_add else []), out_specs=o_spec,
        out_shape=jax.ShapeDtypeStruct(out_shape, out_dtype),
        scratch_shapes=[] if single else [pltpu.VMEM((tm, tn), F32)],
        compiler_params=_params("parallel", "parallel", "parallel", "arbitrary", "arbitrary"),
    )(*((a,) * fuse + (b,) * fuse + ((add,) if has_add else ())))


def _mm_sum(a_list, b_list, name="mm_sum"):
    m, n = a_list[0].shape[0], b_list[0].shape[1]
    ns = len(a_list)
    tm = _pick(m, (1024, 512, 256, 128))
    tn = _pick(n, (1024, 512, 256, 128))
    groups = _row_slices(tm, 16)

    def body(*refs):
        a_refs, b_refs, o_ref = refs[:ns], refs[ns:2 * ns], refs[2 * ns]
        for rs in groups:
            acc = _dot(_mx(a_refs[0][rs, :]), _mx(b_refs[0][...]), 1, 0)
            for s in range(1, ns):
                acc = acc + _dot(_mx(a_refs[s][rs, :]), _mx(b_refs[s][...]), 1, 0)
            o_ref[rs, :] = acc

    return pl.pallas_call(
        body, name=name, grid=(m // tm, n // tn),
        in_specs=([pl.BlockSpec((tm, a.shape[1]), lambda i, j: (i, 0)) for a in a_list]
                  + [pl.BlockSpec((b.shape[0], tn), lambda i, j: (0, j)) for b in b_list]),
        out_specs=pl.BlockSpec((tm, tn), lambda i, j: (i, j)),
        out_shape=jax.ShapeDtypeStruct((m, n), F32), compiler_params=_params("parallel", "parallel"),
    )(*a_list, *b_list)


def _row_tile(r_, streams=4):
    return _pick(r_, ((512,) if streams <= 4 else ()) + (256, 128, 64, 32, 16, 8))


def _rms_fwd(t, w, groups=1, res=None, out_dtype=F32, name="rms_fwd"):
    r_, f = t.shape
    fg = f // groups
    tr = _row_tile(r_)
    has_res = res is not None

    def body(*refs):
        t_ref, w_ref = refs[0], refs[1]
        res_ref = refs[2] if has_res else None
        o_ref = refs[-1]
        for g in range(groups):
            sl = slice(g * fg, (g + 1) * fg)
            tv = t_ref[:, sl].astype(F32)
            r = lax.rsqrt(jnp.mean(tv * tv, axis=-1, keepdims=True) + EPS)
            y = tv * r * w_ref[:, sl]
            if has_res:
                y = y + res_ref[:, sl]
            o_ref[:, sl] = y.astype(o_ref.dtype)

    row = pl.BlockSpec((tr, f), lambda i: (i, 0))
    wsp = pl.BlockSpec((1, f), lambda i: (0, 0))
    return pl.pallas_call(
        body, name=name, grid=(r_ // tr,),
        in_specs=[row, wsp] + ([row] if has_res else []), out_specs=row,
        out_shape=jax.ShapeDtypeStruct((r_, f), out_dtype),
        compiler_params=_params("parallel"),
    )(*((t, w.reshape(1, f)) + ((res,) if has_res else ())))


def _rms_bwd(t, w, dys, res=None, out_dtype=F32, name="rms_bwd"):
    r_, f = t.shape
    groups = len(dys)
    fg = f // groups
    tr = _row_tile(r_)
    has_res = res is not None

    def body(*refs):
        t_ref, w_ref = refs[0], refs[1]
        dy_refs = refs[2:2 + groups]
        res_ref = refs[2 + groups] if has_res else None
        dt_ref, dw_ref = refs[-2], refs[-1]

        @pl.when(pl.program_id(0) == 0)
        def _():
            dw_ref[...] = jnp.zeros_like(dw_ref)

        for g in range(groups):
            sl = slice(g * fg, (g + 1) * fg)
            tv = t_ref[:, sl].astype(F32)
            dyv = dy_refs[g][...].astype(F32)
            r = lax.rsqrt(jnp.mean(tv * tv, axis=-1, keepdims=True) + EPS)
            gw = dyv * w_ref[:, sl]
            c = jnp.mean(gw * tv, axis=-1, keepdims=True)
            dt = r * gw - tv * (r * r * r * c)
            if has_res:
                dt = dt + res_ref[:, sl]
            dt_ref[:, sl] = dt.astype(dt_ref.dtype)
            dw_ref[:, sl] += jnp.sum(dyv * tv * r, axis=0, keepdims=True)

    row = pl.BlockSpec((tr, f), lambda i: (i, 0))
    grow = pl.BlockSpec((tr, fg), lambda i: (i, 0))
    wsp = pl.BlockSpec((1, f), lambda i: (0, 0))
    return pl.pallas_call(
        body, name=name, grid=(r_ // tr,),
        in_specs=[row, wsp] + [grow] * groups + ([row] if has_res else []), out_specs=[row, wsp],
        out_shape=[jax.ShapeDtypeStruct((r_, f), out_dtype), jax.ShapeDtypeStruct((1, f), F32)],
        compiler_params=_params("arbitrary"),
    )(*((t, w.reshape(1, f)) + tuple(dys) + ((res,) if has_res else ())))


def _norm_res_norm(t, res, w1, w2, name="post_mix_pre_ffn_norm"):
    r_, f = t.shape
    tr = _row_tile(r_)

    def body(t_ref, res_ref, w1_ref, w2_ref, h_ref, v_ref):
        tv = t_ref[...]
        h = res_ref[...] + tv * lax.rsqrt(jnp.mean(tv * tv, axis=-1, keepdims=True) + EPS) * w1_ref[...]
        h_ref[...] = h
        v_ref[...] = (h * lax.rsqrt(jnp.mean(h * h, axis=-1, keepdims=True) + EPS) * w2_ref[...]).astype(v_ref.dtype)

    row = pl.BlockSpec((tr, f), lambda i: (i, 0))
    wsp = pl.BlockSpec((1, f), lambda i: (0, 0))
    return pl.pallas_call(
        body, name=name, grid=(r_ // tr,), in_specs=[row, row, wsp, wsp], out_specs=[row, row],
        out_shape=[jax.ShapeDtypeStruct((r_, f), F32), jax.ShapeDtypeStruct((r_, f), MXU_DTYPE)],
        compiler_params=_params("parallel"),
    )(t, res, w1.reshape(1, f), w2.reshape(1, f))


def _norm_res_norm_bwd(h, w2, dv, dres, t, w1, name="pre_ffn_post_mix_norm_bwd"):
    r_, f = h.shape
    tr = _row_tile(r_, streams=6)

    def body(h_ref, w2_ref, dv_ref, dres_ref, t_ref, w1_ref, dh_ref, dt_ref, dw2_ref, dw1_ref):
        @pl.when(pl.program_id(0) == 0)
        def _():
            dw2_ref[...] = jnp.zeros_like(dw2_ref)
            dw1_ref[...] = jnp.zeros_like(dw1_ref)

        def rms_bwd(tv, wv, dyv):
            r = lax.rsqrt(jnp.mean(tv * tv, axis=-1, keepdims=True) + EPS)
            gw = dyv * wv
            c = jnp.mean(gw * tv, axis=-1, keepdims=True)
            return r * gw - tv * (r * r * r * c), jnp.sum(dyv * tv * r, axis=0, keepdims=True)

        d1, g2 = rms_bwd(h_ref[...], w2_ref[...], dv_ref[...])
        dh = d1 + dres_ref[...]
        dh_ref[...] = dh
        dw2_ref[...] += g2
        d2, g1 = rms_bwd(t_ref[...], w1_ref[...], dh)
        dt_ref[...] = d2.astype(dt_ref.dtype)
        dw1_ref[...] += g1

    row = pl.BlockSpec((tr, f), lambda i: (i, 0))
    wsp = pl.BlockSpec((1, f), lambda i: (0, 0))
    return pl.pallas_call(
        body, name=name, grid=(r_ // tr,), in_specs=[row, wsp, row, row, row, wsp], out_specs=[row, row, wsp, wsp],
        out_shape=[jax.ShapeDtypeStruct((r_, f), F32), jax.ShapeDtypeStruct((r_, f), MXU_DTYPE),
                   jax.ShapeDtypeStruct((1, f), F32), jax.ShapeDtypeStruct((1, f), F32)],
        compiler_params=_params("arbitrary"),
    )(h, w2.reshape(1, f), dv, dres, t, w1.reshape(1, f))


def _hnorm_fwd(o, w, width, name="attn_out_norm"):
    h, s_, v = o.shape
    tr = _row_tile(s_)

    def body(o_ref, w_ref, y_ref):
        ss = jnp.sum(o_ref[0] * o_ref[0], axis=-1, keepdims=True)
        for i in range(1, h):
            ss = ss + jnp.sum(o_ref[i] * o_ref[i], axis=-1, keepdims=True)
        r = lax.rsqrt(ss * (1.0 / (h * v)) + EPS)
        for i in range(h):
            sl = slice(i * v, (i + 1) * v)
            y_ref[:, sl] = (o_ref[i] * r * w_ref[:, sl]).astype(y_ref.dtype)

    return pl.pallas_call(
        body, name=name, grid=(s_ // tr,),
        in_specs=[pl.BlockSpec((h, tr, v), lambda i: (0, i, 0)), pl.BlockSpec((1, h * v), lambda i: (0, 0))],
        out_specs=pl.BlockSpec((tr, h * v), lambda i: (i, 0)),
        out_shape=jax.ShapeDtypeStruct((s_, width), MXU_DTYPE), compiler_params=_params("parallel"),
    )(o, w)


def _hnorm_bwd(o, w, dy, name="attn_out_norm_bwd"):
    h, s_, v = o.shape
    tr = _row_tile(s_)

    def body(o_ref, w_ref, dy_ref, do_ref, delta_ref, dw_ref):
        @pl.when(pl.program_id(0) == 0)
        def _():
            dw_ref[...] = jnp.zeros_like(dw_ref)

        ss = jnp.zeros((tr, 1), F32)
        cc = jnp.zeros((tr, 1), F32)
        for i in range(h):
            sl = slice(i * v, (i + 1) * v)
            ov = o_ref[i]
            ss = ss + jnp.sum(ov * ov, axis=-1, keepdims=True)
            cc = cc + jnp.sum(dy_ref[:, sl] * w_ref[:, sl] * ov, axis=-1, keepdims=True)
        r = lax.rsqrt(ss * (1.0 / (h * v)) + EPS)
        c = cc * (1.0 / (h * v))
        for i in range(h):
            sl = slice(i * v, (i + 1) * v)
            ov = o_ref[i]
            dyv = dy_ref[:, sl]
            dov = r * dyv * w_ref[:, sl] - ov * (r * r * r * c)
            do_ref[i] = dov.astype(do_ref.dtype)
            delta_ref[i] = jnp.sum(dov * ov, axis=-1, keepdims=True)
            dw_ref[:, sl] += jnp.sum(dyv * ov * r, axis=0, keepdims=True)

    blk = pl.BlockSpec((h, tr, v), lambda i: (0, i, 0))
    wsp = pl.BlockSpec((1, h * v), lambda i: (0, 0))
    return pl.pallas_call(
        body, name=name, grid=(s_ // tr,),
        in_specs=[blk, wsp, pl.BlockSpec((tr, h * v), lambda i: (i, 0))],
        out_specs=[blk, pl.BlockSpec((h, tr, 1), lambda i: (0, i, 0)), wsp],
        out_shape=[jax.ShapeDtypeStruct(o.shape, MXU_DTYPE), jax.ShapeDtypeStruct((h, s_, 1), F32),
                   jax.ShapeDtypeStruct((1, h * v), F32)],
        compiler_params=_params("arbitrary"),
    )(o, w, dy)


def _loss_head(ffn, h1, target, w, name="loss_head"):
    r_, f = ffn.shape
    tr = _row_tile(r_)

    def body(ffn_ref, h1_ref, tg_ref, w_ref, loss_ref, dy_ref, dffn_ref, dw_ref):
        @pl.when(pl.program_id(0) == 0)
        def _():
            dw_ref[...] = jnp.zeros_like(dw_ref)
            loss_ref[...] = jnp.zeros_like(loss_ref)

        tv = ffn_ref[...]
        wv = w_ref[...]
        r = lax.rsqrt(jnp.mean(tv * tv, axis=-1, keepdims=True) + EPS)
        tn = tv * r
        e = h1_ref[...] + tn * wv - tg_ref[...]
        tot = jnp.sum(jnp.sum(e * e, axis=1, keepdims=True), axis=0, keepdims=True) * (0.5 / f)
        loss_ref[...] += tot + jnp.zeros_like(loss_ref)
        dyv = e * (1.0 / f)
        dy_ref[...] = dyv
        gw = dyv * wv
        c = jnp.mean(gw * tv, axis=-1, keepdims=True)
        dffn_ref[...] = (r * gw - tv * (r * r * r * c)).astype(dffn_ref.dtype)
        dw_ref[...] += jnp.sum(dyv * tn, axis=0, keepdims=True)

    row = pl.BlockSpec((tr, f), lambda i: (i, 0))
    wsp = pl.BlockSpec((1, f), lambda i: (0, 0))
    lsp = pl.BlockSpec((1, LANE), lambda i: (0, 0))
    return pl.pallas_call(
        body, name=name, grid=(r_ // tr,),
        in_specs=[row, row, row, wsp], out_specs=[lsp, row, row, wsp],
        out_shape=[jax.ShapeDtypeStruct((1, LANE), F32), jax.ShapeDtypeStruct((r_, f), F32),
                   jax.ShapeDtypeStruct((r_, f), MXU_DTYPE), jax.ShapeDtypeStruct((1, f), F32)],
        compiler_params=_params("arbitrary"),
    )(ffn, h1, target, w.reshape(1, f))


def _rot_matrix():
    p = np.zeros((ROPE, ROPE), np.float32)
    for i in range(HALF):
        p[i + HALF, i] = -1.0
        p[i, i + HALF] = 1.0
    return jnp.asarray(p, BF16)


def _rope_val(r, c2, s2, rot):
    hi, mid, _ = _split3(r)
    return r * c2 + (_dot(hi, rot, 1, 0) + _dot(mid, rot, 1, 0)) * s2


def _q_prep(q, cos2, sin2, scale, name):
    h, s_, _ = q.shape
    tr = _pick(s_, (4096, 2048, 1024, 512, 256, 128, 64, 32, 16))

    def body(q_ref, c_ref, s_ref, rot_ref, o_ref):
        for rs in _row_slices(tr, 16):
            x = q_ref[rs, :]
            o_ref[rs, :NOPE] = (x[:, :NOPE] * scale).astype(o_ref.dtype)
            o_ref[rs, NOPE:] = (_rope_val(x[:, NOPE:], c_ref[rs, :], s_ref[rs, :], rot_ref[...]) * scale).astype(o_ref.dtype)

    blk = pl.BlockSpec((None, tr, QK), lambda hh, i: (hh, i, 0))
    csp = pl.BlockSpec((tr, ROPE), lambda hh, i: (i, 0))
    return pl.pallas_call(
        body, name=name, grid=(h, s_ // tr),
        in_specs=[blk, csp, csp, pl.BlockSpec((ROPE, ROPE), lambda hh, i: (0, 0))], out_specs=blk,
        out_shape=jax.ShapeDtypeStruct(q.shape, MXU_DTYPE), compiler_params=_params("parallel", "parallel"),
    )(q, cos2, sin2, _rot_matrix())


def _q_up(qkvn, w_uq_t, cos2, sin2, scale, name="q_up"):
    s_ = qkvn.shape[0]
    h = w_uq_t.shape[0]
    tm = _pick(s_, (4096, 2048, 1024, 512, 256, 128))

    def body(a_ref, w_ref, c_ref, s_ref, rot_ref, o_ref):
        for rs in _row_slices(tm, 16):
            x = _dot(_mx(a_ref[rs, :]), _mx(w_ref[...]), 1, 1)
            o_ref[rs, :NOPE] = (x[:, :NOPE] * scale).astype(o_ref.dtype)
            o_ref[rs, NOPE:] = (_rope_val(x[:, NOPE:], c_ref[rs, :], s_ref[rs, :], rot_ref[...]) * scale).astype(o_ref.dtype)

    csp = pl.BlockSpec((tm, ROPE), lambda j, i: (i, 0))
    return pl.pallas_call(
        body, name=name, grid=(h, s_ // tm),
        in_specs=[pl.BlockSpec((tm, Q_RANK), lambda j, i: (i, 0)), pl.BlockSpec((None, QK, Q_RANK), lambda j, i: (j, 0, 0)),
                  csp, csp, pl.BlockSpec((ROPE, ROPE), lambda j, i: (0, 0))],
        out_specs=pl.BlockSpec((None, tm, QK), lambda j, i: (j, i, 0)),
        out_shape=jax.ShapeDtypeStruct((h, s_, QK), MXU_DTYPE), compiler_params=_params("parallel", "parallel"),
    )(qkvn, w_uq_t, cos2, sin2, _rot_matrix())


def _kv_up(qkvn, w_ukv, small, cos2, sin2, name="kv_up"):
    s_ = qkvn.shape[0]
    h = w_ukv.shape[0]
    tm = _pick(s_, (4096, 2048, 1024, 512, 256, 128))

    def body(a_ref, w_ref, sm_ref, c_ref, s_ref, rot_ref, k_ref, v_ref):
        for rs in _row_slices(tm, 16):
            x = _dot(_mx(a_ref[rs, :]), _mx(w_ref[...]), 1, 0)
            k_ref[rs, :NOPE] = x[:, :NOPE].astype(k_ref.dtype)
            k_ref[rs, NOPE:] = _rope_val(sm_ref[rs, :ROPE], c_ref[rs, :], s_ref[rs, :], rot_ref[...]).astype(k_ref.dtype)
            v_ref[rs, :] = x[:, NOPE:].astype(v_ref.dtype)

    csp = pl.BlockSpec((tm, ROPE), lambda j, i: (i, 0))
    return pl.pallas_call(
        body, name=name, grid=(h, s_ // tm),
        in_specs=[pl.BlockSpec((tm, KV_RANK), lambda j, i: (i, Q_RANK // KV_RANK)),
                  pl.BlockSpec((None, KV_RANK, NOPE + VDIM), lambda j, i: (j, 0, 0)),
                  pl.BlockSpec((tm, LANE), lambda j, i: (i, 0)), csp, csp, pl.BlockSpec((ROPE, ROPE), lambda j, i: (0, 0))],
        out_specs=[pl.BlockSpec((None, tm, QK), lambda j, i: (j, i, 0)), pl.BlockSpec((None, tm, VDIM), lambda j, i: (j, i, 0))],
        out_shape=[jax.ShapeDtypeStruct((h, s_, QK), MXU_DTYPE), jax.ShapeDtypeStruct((h, s_, VDIM), MXU_DTYPE)],
        compiler_params=_params("parallel", "parallel"),
    )(qkvn, w_ukv, small, cos2, sin2, _rot_matrix())


def _dkv_post(dk, dv, ddt, cos2, nsin2, name="dkv_post"):
    h, s_, _ = dk.shape
    tr = _row_tile(s_)

    def body(dk_ref, dv_ref, ddt_ref, c_ref, s_ref, rot_ref, dkv_ref, dsm_ref):
        acc = dk_ref[0, :, NOPE:]
        for i in range(1, h):
            acc = acc + dk_ref[i, :, NOPE:]
        dsm_ref[:, :ROPE] = _rope_val(acc, c_ref[...], s_ref[...], rot_ref[...]).astype(dsm_ref.dtype)
        dsm_ref[:, ROPE:] = ddt_ref[:, ROPE:].astype(dsm_ref.dtype)
        for i in range(h):
            dkv_ref[i, :, :NOPE] = dk_ref[i, :, :NOPE].astype(dkv_ref.dtype)
            dkv_ref[i, :, NOPE:] = dv_ref[i].astype(dkv_ref.dtype)

    csp = pl.BlockSpec((tr, ROPE), lambda i: (i, 0))
    return pl.pallas_call(
        body, name=name, grid=(s_ // tr,),
        in_specs=[pl.BlockSpec((h, tr, QK), lambda i: (0, i, 0)), pl.BlockSpec((h, tr, VDIM), lambda i: (0, i, 0)),
                  pl.BlockSpec((tr, LANE), lambda i: (i, 0)), csp, csp, pl.BlockSpec((ROPE, ROPE), lambda i: (0, 0))],
        out_specs=[pl.BlockSpec((h, tr, NOPE + VDIM), lambda i: (0, i, 0)), pl.BlockSpec((tr, LANE), lambda i: (i, 0))],
        out_shape=[jax.ShapeDtypeStruct((h, s_, NOPE + VDIM), MXU_DTYPE), jax.ShapeDtypeStruct((s_, LANE), MXU_DTYPE)],
        compiler_params=_params("parallel"),
    )(dk, dv, ddt, cos2, nsin2, _rot_matrix())


def _attn_tile(s):
    return 2048 if s % 4096 == 0 else s // 2


def _pairs(n, by_key):
    if by_key:
        pr = [(i, j) for j in range(n) for i in range(j, n)]
    else:
        pr = [(i, j) for i in range(n) for j in range(i + 1)]
    return (jnp.asarray([p[0] for p in pr], jnp.int32), jnp.asarray([p[1] for p in pr], jnp.int32))


ATTN_ROW_GROUPS = 8


def _row_groups(t, diag):
    tg = t // ATTN_ROW_GROUPS
    out = []
    for r in range(ATTN_ROW_GROUPS):
        nc = (r + 1) * tg if diag else t
        mask = None
        if diag:
            mask = (lax.broadcasted_iota(jnp.int32, (tg, nc), 1)
                    <= lax.broadcasted_iota(jnp.int32, (tg, nc), 0) + r * tg)
        out.append((slice(r * tg, (r + 1) * tg), nc, mask))
    return out


def _flash_specs(t, dk, dv):
    qsp = pl.BlockSpec((None, t, dk), lambda hh, p, qi, kj: (hh, qi[p], 0))
    ksp = pl.BlockSpec((None, t, dk), lambda hh, p, qi, kj: (hh, kj[p], 0))
    vsp = pl.BlockSpec((None, t, dv), lambda hh, p, qi, kj: (hh, kj[p], 0))
    osp = pl.BlockSpec((None, t, dv), lambda hh, p, qi, kj: (hh, qi[p], 0))
    lsp = pl.BlockSpec((None, t, 1), lambda hh, p, qi, kj: (hh, qi[p], 0))
    return qsp, ksp, vsp, osp, lsp


def _flash_fwd(q, k, v, name="flash_fwd"):
    h, s_, dk = q.shape
    dv = v.shape[-1]
    t = _attn_tile(s_)
    n = s_ // t
    qi, kj = _pairs(n, False)

    def body(qi_ref, kj_ref, q_ref, k_ref, v_ref, o_ref, lse_ref, m_s, l_s, acc):
        p_ = pl.program_id(1)
        i, j = qi_ref[p_], kj_ref[p_]

        @pl.when(j == 0)
        def _():
            m_s[...] = jnp.full_like(m_s, -jnp.inf)
            l_s[...] = jnp.zeros_like(l_s)
            acc[...] = jnp.zeros_like(acc)

        def update(diag):
            for rs, nc, mask in _row_groups(t, diag):
                sc = _dot(q_ref[rs, :], k_ref[0:nc, :], 1, 1)
                if mask is not None:
                    sc = jnp.where(mask, sc, -jnp.inf)
                m_old = m_s[rs, :]
                m_new = jnp.maximum(m_old, jnp.max(sc, axis=1, keepdims=True))
                alpha = jnp.exp(m_old - m_new)
                p = jnp.exp(sc - m_new)
                l_s[rs, :] = alpha * l_s[rs, :] + jnp.sum(p, axis=1, keepdims=True)
                acc[rs, :] = alpha * acc[rs, :] + _dot(_mx(p), v_ref[0:nc, :], 1, 0)
                m_s[rs, :] = m_new

        @pl.when(j < i)
        def _():
            update(False)

        @pl.when(j == i)
        def _():
            update(True)
            o_ref[...] = acc[...] / l_s[...]
            lse_ref[...] = m_s[...] + jnp.log(l_s[...])

    qsp, ksp, vsp, osp, lsp = _flash_specs(t, dk, dv)
    gs = pltpu.PrefetchScalarGridSpec(
        num_scalar_prefetch=2, grid=(h, qi.shape[0]), in_specs=[qsp, ksp, vsp], out_specs=[osp, lsp],
        scratch_shapes=[pltpu.VMEM((t, 1), F32), pltpu.VMEM((t, 1), F32), pltpu.VMEM((t, dv), F32)])
    return pl.pallas_call(
        body, name=name, grid_spec=gs,
        out_shape=[jax.ShapeDtypeStruct((h, s_, dv), F32), jax.ShapeDtypeStruct((h, s_, 1), F32)],
        compiler_params=_params("parallel", "arbitrary"),
    )(qi, kj, q, k, v)


def _flash_bwd(q, k, v, do, lse, delta, name="flash_bwd"):
    h, s_, dk = q.shape
    dv = v.shape[-1]
    t = _attn_tile(s_)
    tg = t // ATTN_ROW_GROUPS
    n = s_ // t
    qi, kj = _pairs(n, True)

    def body(qi_ref, kj_ref, q_ref, k_ref, v_ref, do_ref, lse_ref, delta_ref, dq_ref, dk_ref, dv_ref, dk_acc, dv_acc):
        p_ = pl.program_id(1)
        i, j = qi_ref[p_], kj_ref[p_]

        @pl.when(p_ == 0)
        def _():
            dq_ref[...] = jnp.zeros_like(dq_ref)

        def update(diag):
            for g, (rs, nc, mask) in enumerate(_row_groups(t, diag)):
                sc = _dot(q_ref[rs, :], k_ref[0:nc, :], 1, 1)
                if mask is not None:
                    sc = jnp.where(mask, sc, -jnp.inf)
                p = jnp.exp(sc - lse_ref[rs, :])
                dob = _mx(do_ref[rs, :])
                dv_acc[0:nc, :] += _dot(_mx(p), dob, 0, 0)
                dp = _dot(dob, v_ref[0:nc, :], 1, 1)
                dsb = _mx(p * (dp - delta_ref[rs, :]))
                dk_acc[0:nc, :] += _dot(dsb, q_ref[rs, :], 0, 0)
                rows = pl.ds(pl.multiple_of(i * t + g * tg, tg), tg)
                dq_ref[rows, :] += _dot(dsb, k_ref[0:nc, :], 1, 0)

        @pl.when(i == j)
        def _():
            dk_acc[...] = jnp.zeros_like(dk_acc)
            dv_acc[...] = jnp.zeros_like(dv_acc)
            update(True)

        @pl.when(i > j)
        def _():
            update(False)

        @pl.when(i == n - 1)
        def _():
            dk_ref[...] = dk_acc[...]
            dv_ref[...] = dv_acc[...]

    qsp, ksp, vsp, osp, lsp = _flash_specs(t, dk, dv)
    dqsp = pl.BlockSpec((None, s_, dk), lambda hh, p, qi, kj: (hh, 0, 0))
    gs = pltpu.PrefetchScalarGridSpec(
        num_scalar_prefetch=2, grid=(h, qi.shape[0]), in_specs=[qsp, ksp, vsp, osp, lsp, lsp],
        out_specs=[dqsp, ksp, vsp],
        scratch_shapes=[pltpu.VMEM((t, dk), F32), pltpu.VMEM((t, dv), F32)])
    return pl.pallas_call(
        body, name=name, grid_spec=gs,
        out_shape=[jax.ShapeDtypeStruct((h, s_, dk), F32), jax.ShapeDtypeStruct((h, s_, dk), F32),
                   jax.ShapeDtypeStruct((h, s_, dv), F32)],
        compiler_params=_params("parallel", "arbitrary"),
    )(qi, kj, q, k, v, do, lse, delta)


HALO = 8


def _conv_specs(s_, c, tr, after):
    main = pl.BlockSpec((tr, c), lambda i: (i, 0))
    per = tr // HALO
    if after:
        halo = pl.BlockSpec((HALO, c), lambda i: (jnp.minimum((i + 1) * per, s_ // HALO - 1), 0))
    else:
        halo = pl.BlockSpec((HALO, c), lambda i: (jnp.maximum(i * per - 1, 0), 0))
    return main, halo


def _fill_before(ext, t_ref, h_ref, tr):
    ext[0:HALO, :] = jnp.where(pl.program_id(0) > 0, h_ref[...], 0.0)
    ext[HALO:HALO + tr, :] = t_ref[...]


def _taps(ext, w_ref, tr):
    base = HALO - (CONV_K - 1)
    acc = ext[base:base + tr, :] * w_ref[0:1, :]
    for k in range(1, CONV_K):
        acc = acc + ext[base + k:base + k + tr, :] * w_ref[k:k + 1, :]
    return acc


def _conv_fwd(t, w, b, name="conv_fwd"):
    s_, c = t.shape
    tr = _row_tile(s_)

    def body(t_ref, h_ref, w_ref, b_ref, o_ref, ext):
        _fill_before(ext, t_ref, h_ref, tr)
        o_ref[...] = _silu(_taps(ext, w_ref, tr) + b_ref[...])

    main, halo = _conv_specs(s_, c, tr, False)
    return pl.pallas_call(
        body, name=name, grid=(s_ // tr,),
        in_specs=[main, halo, pl.BlockSpec((CONV_K, c), lambda i: (0, 0)), pl.BlockSpec((1, c), lambda i: (0, 0))],
        out_specs=main, out_shape=jax.ShapeDtypeStruct((s_, c), F32),
        scratch_shapes=[pltpu.VMEM((tr + HALO, c), F32)], compiler_params=_params("parallel"),
    )(t, t, w, b)


def _conv_bwd_pre(t, w, b, dact, name="conv_bwd_pre"):
    s_, c = t.shape
    tr = _row_tile(s_)

    def body(t_ref, h_ref, w_ref, b_ref, da_ref, dpre_ref, dwb_ref, ext):
        @pl.when(pl.program_id(0) == 0)
        def _():
            dwb_ref[...] = jnp.zeros_like(dwb_ref)

        _fill_before(ext, t_ref, h_ref, tr)
        dpre = da_ref[...] * _dsilu(_taps(ext, w_ref, tr) + b_ref[...])
        dpre_ref[...] = dpre
        base = HALO - (CONV_K - 1)
        for k in range(CONV_K):
            dwb_ref[k:k + 1, :] += jnp.sum(dpre * ext[base + k:base + k + tr, :], axis=0, keepdims=True)
        dwb_ref[CONV_K:CONV_K + 1, :] += jnp.sum(dpre, axis=0, keepdims=True)

    main, halo = _conv_specs(s_, c, tr, False)
    return pl.pallas_call(
        body, name=name, grid=(s_ // tr,),
        in_specs=[main, halo, pl.BlockSpec((CONV_K, c), lambda i: (0, 0)), pl.BlockSpec((1, c), lambda i: (0, 0)), main],
        out_specs=[main, pl.BlockSpec((8, c), lambda i: (0, 0))],
        out_shape=[jax.ShapeDtypeStruct((s_, c), F32), jax.ShapeDtypeStruct((8, c), F32)],
        scratch_shapes=[pltpu.VMEM((tr + HALO, c), F32)], compiler_params=_params("arbitrary"),
    )(t, t, w, b, dact)


def _conv_bwd_in(dpre, w, name="conv_bwd_in"):
    s_, c = dpre.shape
    tr = _row_tile(s_)
    nt = s_ // tr

    def body(d_ref, h_ref, w_ref, o_ref, ext):
        ext[0:tr, :] = d_ref[...]
        ext[tr:tr + HALO, :] = jnp.where(pl.program_id(0) < nt - 1, h_ref[...], 0.0)
        acc = ext[CONV_K - 1:CONV_K - 1 + tr, :] * w_ref[0:1, :]
        for k in range(1, CONV_K):
            acc = acc + ext[CONV_K - 1 - k:CONV_K - 1 - k + tr, :] * w_ref[k:k + 1, :]
        o_ref[...] = acc.astype(o_ref.dtype)

    main, halo = _conv_specs(s_, c, tr, True)
    return pl.pallas_call(
        body, name=name, grid=(nt,),
        in_specs=[main, halo, pl.BlockSpec((CONV_K, c), lambda i: (0, 0))],
        out_specs=main, out_shape=jax.ShapeDtypeStruct((s_, c), MXU_DTYPE),
        scratch_shapes=[pltpu.VMEM((tr + HALO, c), F32)], compiler_params=_params("parallel"),
    )(dpre, dpre, w)


def _ssd_chunk_common(dt_ref, dtt_ref, br_ref, bc_ref, ar_ref, ac_ref):
    li = lax.broadcasted_iota(jnp.int32, (CHUNK, CHUNK), 0)
    si = lax.broadcasted_iota(jnp.int32, (CHUNK, CHUNK), 1)
    lower = li >= si
    lower_b = lower.astype(BF16)
    upper_b = (li <= si).astype(BF16)
    zr = dt_ref[...] + br_ref[...]
    dtc = _softplus(zr)
    a_row = -jnp.exp(ar_ref[...])
    acum = _exact_dot(lower_b, dtc * a_row, 1, 0, False)
    dtt = _softplus(dtt_ref[...] + bc_ref[...])
    acum_t = _exact_dot(dtt * (-jnp.exp(ac_ref[...])), upper_b, 1, 0, True)
    return lower, upper_b, zr, dtc, a_row, acum, acum_t


def _head_terms(h, lower, dtc, acum, acum_t):
    lane = lax.broadcasted_iota(jnp.int32, (1, LANE), 1)
    sub = lax.broadcasted_iota(jnp.int32, (SSD_H, 1), 0)
    rowid = lax.broadcasted_iota(jnp.int32, (CHUNK, 1), 0)
    oh = (lane == HEAD_LANE + h).astype(F32)
    acol = jnp.sum(acum * oh, axis=1, keepdims=True)
    dcol = jnp.sum(dtc * oh, axis=1, keepdims=True)
    arow = jnp.sum(acum_t * (sub == h).astype(F32), axis=0, keepdims=True)
    alast = jnp.sum(jnp.where(rowid == CHUNK - 1, acol, 0.0), axis=0, keepdims=True)
    decay = jnp.exp(jnp.where(lower, acol - arow, -jnp.inf))
    return oh, acol, dcol, alast, decay


SSD_PAIRS = SSD_H // 2
PAIRS_PER_GROUP = SSD_E // 2


def _ps(q):
    return slice(q * LANE, (q + 1) * LANE)


def _gs(off, g):
    return slice(off + g * SSD_N, off + (g + 1) * SSD_N)


def _lanes(c0, c1):
    return jnp.where(lax.broadcasted_iota(jnp.int32, (1, LANE), 1) < SSD_P, c0, c1)


def _rows(c0, c1):
    return jnp.where(lax.broadcasted_iota(jnp.int32, (LANE, 1), 0) < SSD_P, c0, c1)


def _lane_halves(t):
    first = lax.broadcasted_iota(jnp.int32, (1, LANE), 1) < SSD_P
    return (jnp.sum(jnp.where(first, t, 0.0), axis=1, keepdims=True),
            jnp.sum(jnp.where(first, 0.0, t), axis=1, keepdims=True))


def _ssd_in_specs(rev):
    def ci(c):
        return c if rev is None else rev - c
    return [pl.BlockSpec((CHUNK, CONV_DIM), lambda c: (ci(c), 0)),
            pl.BlockSpec((CHUNK, LANE), lambda c: (ci(c), 0)),
            pl.BlockSpec((SSD_H, CHUNK), lambda c: (0, ci(c))),
            pl.BlockSpec((1, LANE), lambda c: (0, 0)), pl.BlockSpec((SSD_H, 1), lambda c: (0, 0)),
            pl.BlockSpec((1, LANE), lambda c: (0, 0)), pl.BlockSpec((SSD_H, 1), lambda c: (0, 0)),
            pl.BlockSpec((SSD_PAIRS, 1, LANE), lambda c: (0, 0, 0))]


def _ssd_fwd(xbc, small, dtt, bias_r, bias_c, alog_r, alog_c, dsk, name="ssd_fwd"):
    s_ = xbc.shape[0]
    nc = s_ // CHUNK

    def body(x_ref, dt_ref, dtt_ref, br_ref, bc_ref, ar_ref, ac_ref, dsk_ref, y_ref, prev_ref, state):
        @pl.when(pl.program_id(0) == 0)
        def _():
            state[...] = jnp.zeros_like(state)

        lower, _, _, dtc, _, acum, acum_t = _ssd_chunk_common(dt_ref, dtt_ref, br_ref, bc_ref, ar_ref, ac_ref)
        for g in range(SSD_G):
            bb = _mx(x_ref[:, _gs(B_OFF, g)])
            cb_ = _mx(x_ref[:, _gs(C_OFF, g)])
            cbm = _dot(cb_, bb, 1, 1)
            for e in range(PAIRS_PER_GROUP):
                q = g * PAIRS_PER_GROUP + e
                _, acol0, dcol0, alast0, decay0 = _head_terms(2 * q, lower, dtc, acum, acum_t)
                _, acol1, dcol1, alast1, decay1 = _head_terms(2 * q + 1, lower, dtc, acum, acum_t)
                x = x_ref[:, _ps(q)]
                xdt = x * _lanes(dcol0, dcol1)
                xb = _mx(xdt)
                yd = _lanes(_dot(_mx(cbm * decay0), xb, 1, 0), _dot(_mx(cbm * decay1), xb, 1, 0))
                prev = state[q]
                prev_ref[0, q] = prev
                yo = _dot(cb_, _mx(prev), 1, 1) * _lanes(jnp.exp(acol0), jnp.exp(acol1))
                ds = _lanes(jnp.exp(alast0 - acol0), jnp.exp(alast1 - acol1))
                st = _dot(_mx(xdt * ds), bb, 0, 0)
                state[q] = prev * _rows(jnp.exp(alast0), jnp.exp(alast1)) + st
                y_ref[:, _ps(q)] = yd + yo + x * dsk_ref[q]

    psp = pl.BlockSpec((1, SSD_PAIRS, LANE, SSD_N), lambda c: (c, 0, 0, 0))
    return pl.pallas_call(
        body, name=name, grid=(nc,),
        in_specs=_ssd_in_specs(None), out_specs=[pl.BlockSpec((CHUNK, SSD_W), lambda c: (c, 0)), psp],
        out_shape=[jax.ShapeDtypeStruct((s_, SSD_W), F32),
                   jax.ShapeDtypeStruct((nc, SSD_PAIRS, LANE, SSD_N), F32)],
        scratch_shapes=[pltpu.VMEM((SSD_PAIRS, LANE, SSD_N), F32)],
        compiler_params=_params("arbitrary"),
    )(xbc, small, dtt, bias_r, bias_c, alog_r, alog_c, dsk)


def _ssd_bwd(xbc, small, dtt, bias_r, bias_c, alog_r, alog_c, dsk, prev, dy, name="ssd_bwd"):
    s_ = xbc.shape[0]
    nc = s_ // CHUNK

    def body(x_ref, dt_ref, dtt_ref, br_ref, bc_ref, ar_ref, ac_ref, dsk_ref, prev_ref, dy_ref,
             dx_ref, ddt_ref, dpar_ref, dstate):
        @pl.when(pl.program_id(0) == 0)
        def _():
            dstate[...] = jnp.zeros_like(dstate)
            dpar_ref[...] = jnp.zeros_like(dpar_ref)

        lower, upper_b, zr, dtc, a_row, acum, acum_t = _ssd_chunk_common(
            dt_ref, dtt_ref, br_ref, bc_ref, ar_ref, ac_ref)
        strict = (lax.broadcasted_iota(jnp.int32, (CHUNK, CHUNK), 1)
                  < lax.broadcasted_iota(jnp.int32, (CHUNK, CHUNK), 0))
        strict_b = strict.astype(BF16)
        col2 = lax.broadcasted_iota(jnp.int32, (CHUNK, 2 * CHUNK), 1)
        strict2 = (jnp.where(col2 >= CHUNK, col2 - CHUNK, col2)
                   < lax.broadcasted_iota(jnp.int32, (CHUNK, 2 * CHUNK), 0))
        da_in = jnp.zeros((CHUNK, LANE), F32)
        r_off = jnp.zeros((CHUNK, LANE), F32)
        c_int = jnp.zeros((CHUNK, LANE), F32)
        c_row = jnp.zeros((1, LANE), F32)
        ddt = jnp.zeros((CHUNK, LANE), F32)
        dskip = jnp.zeros((1, LANE), F32)
        for g in range(SSD_G):
            bb = _mx(x_ref[:, _gs(B_OFF, g)])
            cb_ = _mx(x_ref[:, _gs(C_OFF, g)])
            cbm = _dot(cb_, bb, 1, 1)
            dcb = jnp.zeros((CHUNK, CHUNK), F32)
            dc_acc = jnp.zeros((CHUNK, SSD_N), F32)
            db_acc = jnp.zeros((CHUNK, SSD_N), F32)
            for e in range(PAIRS_PER_GROUP):
                q = g * PAIRS_PER_GROUP + e
                oh0, acol0, dcol0, alast0, decay0 = _head_terms(2 * q, lower, dtc, acum, acum_t)
                oh1, acol1, dcol1, alast1, decay1 = _head_terms(2 * q + 1, lower, dtc, acum, acum_t)
                x = x_ref[:, _ps(q)]
                dy = dy_ref[:, _ps(q)]
                dcol = _lanes(dcol0, dcol1)
                xdt = x * dcol
                xb = _mx(xdt)
                eacol = _lanes(jnp.exp(acol0), jnp.exp(acol1))
                ds = _lanes(jnp.exp(alast0 - acol0), jnp.exp(alast1 - acol1))
                ealast = _rows(jnp.exp(alast0), jnp.exp(alast1))
                dyb = _mx(dy)
                dyb0, dyb1 = _mx(_lanes(dy, 0.0)), _mx(_lanes(0.0, dy))
                dsh = dstate[q]
                dshb = _mx(dsh)
                prev = prev_ref[0, q]
                prevb = _mx(prev)
                dxdt_inter = ds * _dot(bb, dshb, 1, 1)
                dxdt = _lanes(_dot(_mx(cbm * decay0), dyb, 0, 0), _dot(_mx(cbm * decay1), dyb, 0, 0)) + dxdt_inter
                dwl0 = _dot(dyb0, xb, 1, 1) * decay0
                dwl1 = _dot(dyb1, xb, 1, 1) * decay1
                dcb = dcb + dwl0 + dwl1
                dyeb = _mx(dy * eacol)
                dc_acc = dc_acc + _dot(dyeb, prevb, 1, 0)
                db_acc = db_acc + _dot(_mx(xdt * ds), dshb, 1, 0)
                dstate[q] = _dot(dyeb, cb_, 0, 0) + ealast * dsh
                above = _exact_dot(upper_b, jnp.concatenate([dwl0 * cbm, dwl1 * cbm], axis=1), 1, 0, False)
                above = jnp.where(strict2, above, 0.0)
                da_in = (da_in + jnp.sum(above[:, :CHUNK], axis=1, keepdims=True) * oh0
                         + jnp.sum(above[:, CHUNK:], axis=1, keepdims=True) * oh1)
                y_off = _dot(cb_, prevb, 1, 1) * eacol
                r0, r1 = _lane_halves(dy * y_off)
                r_off = r_off + r0 * oh0 + r1 * oh1
                c0, c1 = _lane_halves(xdt * dxdt_inter)
                c_int = c_int + c0 * oh0 + c1 * oh1
                both = jnp.sum(dsh * prev, axis=1, keepdims=True) * ealast
                c_row = (c_row + jnp.sum(_rows(both, 0.0), axis=0, keepdims=True) * oh0
                         + jnp.sum(_rows(0.0, both), axis=0, keepdims=True) * oh1)
                t0, t1 = _lane_halves(dxdt * x)
                ddt = ddt + t0 * oh0 + t1 * oh1
                dx_ref[:, _ps(q)] = dxdt * dcol + dy * dsk_ref[q]
                k0, k1 = _lane_halves(dy * x)
                dskip = (dskip + jnp.sum(k0, axis=0, keepdims=True) * oh0 + jnp.sum(k1, axis=0, keepdims=True) * oh1)
            dcbb = _mx(dcb)
            dx_ref[:, _gs(C_OFF, g)] = dc_acc + _dot(dcbb, bb, 1, 0)
            dx_ref[:, _gs(B_OFF, g)] = db_acc + _dot(dcbb, cb_, 0, 0)
        da = (da_in + _exact_dot(upper_b, r_off, 1, 0, False) + _exact_dot(strict_b, c_int, 1, 0, False) + c_row)
        draw = (ddt + da * a_row) * _sigmoid(zr)
        ddt_ref[...] = draw
        dpar_ref[0:1, :] += jnp.sum(draw, axis=0, keepdims=True)
        dpar_ref[1:2, :] += jnp.sum(da * dtc, axis=0, keepdims=True) * a_row
        dpar_ref[2:3, :] += dskip

    rev = nc - 1
    psp = pl.BlockSpec((1, SSD_PAIRS, LANE, SSD_N), lambda c: (rev - c, 0, 0, 0))
    return pl.pallas_call(
        body, name=name, grid=(nc,),
        in_specs=_ssd_in_specs(rev) + [psp, pl.BlockSpec((CHUNK, SSD_W), lambda c: (rev - c, 0))],
        out_specs=[pl.BlockSpec((CHUNK, CONV_DIM), lambda c: (rev - c, 0)),
                   pl.BlockSpec((CHUNK, LANE), lambda c: (rev - c, 0)), pl.BlockSpec((8, LANE), lambda c: (0, 0))],
        out_shape=[jax.ShapeDtypeStruct((s_, CONV_DIM), F32), jax.ShapeDtypeStruct((s_, LANE), F32),
                   jax.ShapeDtypeStruct((8, LANE), F32)],
        scratch_shapes=[pltpu.VMEM((SSD_PAIRS, LANE, SSD_N), F32)],
        compiler_params=_params("arbitrary"),
    )(xbc, small, dtt, bias_r, bias_c, alog_r, alog_c, dsk, prev, dy)


GN = SSD_W // SSD_G


def _gated_norm_fwd(y, z, w, cat, name="gated_norm_fwd"):
    s_, f = y.shape
    tr = _row_tile(s_)

    def body(y_ref, z_ref, w_ref, cat_ref, o_ref):
        for g in range(SSD_G):
            sl = slice(g * GN, (g + 1) * GN)
            gg = y_ref[:, sl] * _silu(z_ref[:, sl])
            r = lax.rsqrt(jnp.mean(gg * gg, axis=-1, keepdims=True) + EPS)
            o_ref[:, sl] = (gg * r * w_ref[:, sl]).astype(o_ref.dtype)

    row = pl.BlockSpec((tr, f), lambda i: (i, 0))
    wsp = pl.BlockSpec((1, f), lambda i: (0, 0))
    return pl.pallas_call(
        body, name=name, grid=(s_ // tr,),
        in_specs=[row, row, wsp, pl.BlockSpec(memory_space=pl.ANY)], out_specs=pl.BlockSpec((tr, f), lambda i: (i, 1)),
        out_shape=jax.ShapeDtypeStruct(cat.shape, cat.dtype), input_output_aliases={3: 0},
        compiler_params=_params("parallel"),
    )(y, z, w.reshape(1, f), cat)


def _gated_norm_bwd(y, z, w, dout, name="gated_norm_bwd"):
    s_, f = y.shape
    tr = _row_tile(s_)

    def body(y_ref, z_ref, w_ref, do_ref, dy_ref, dz_ref, dw_ref):
        @pl.when(pl.program_id(0) == 0)
        def _():
            dw_ref[...] = jnp.zeros_like(dw_ref)

        for g in range(SSD_G):
            sl = slice(g * GN, (g + 1) * GN)
            yv = y_ref[:, sl]
            zv = z_ref[:, sl]
            dov = do_ref[:, sl].astype(F32)
            sz = _silu(zv)
            gg = yv * sz
            r = lax.rsqrt(jnp.mean(gg * gg, axis=-1, keepdims=True) + EPS)
            gw = dov * w_ref[:, sl]
            c = jnp.mean(gw * gg, axis=-1, keepdims=True)
            dgg = r * gw - gg * (r * r * r * c)
            dy_ref[:, sl] = dgg * sz
            dz_ref[:, sl] = (dgg * yv * _dsilu(zv)).astype(dz_ref.dtype)
            dw_ref[:, sl] += jnp.sum(dov * gg * r, axis=0, keepdims=True)

    row = pl.BlockSpec((tr, f), lambda i: (i, 0))
    wsp = pl.BlockSpec((1, f), lambda i: (0, 0))
    return pl.pallas_call(
        body, name=name, grid=(s_ // tr,),
        in_specs=[row, row, wsp, pl.BlockSpec((tr, f), lambda i: (i, 1))], out_specs=[row, row, wsp],
        out_shape=[jax.ShapeDtypeStruct((s_, f), F32), jax.ShapeDtypeStruct((s_, f), MXU_DTYPE),
                   jax.ShapeDtypeStruct((1, f), F32)],
        compiler_params=_params("arbitrary"),
    )(y, z, w.reshape(1, f), dout)


def _ffn_fwd(vv, w_gate, w_up, name="ffn_gate_up"):
    s_, d = vv.shape
    nb, f8, _ = w_gate.shape
    tm = _pick(s_, (1024, 512, 256, 128))

    def body(v_ref, wg_ref, wu_ref, g_ref, u_ref, a_ref):
        for rs in _row_slices(tm, 16):
            a = _mx(v_ref[rs, :])
            g = _dot(a, _mx(wg_ref[...]), 1, 1)
            u = _dot(a, _mx(wu_ref[...]), 1, 1)
            s = _sigmoid(g)
            gs = g * s
            g_ref[rs, :] = (u * (s * (1.0 + g * (1.0 - s)))).astype(g_ref.dtype)
            u_ref[rs, :] = gs.astype(u_ref.dtype)
            a_ref[rs, :] = (gs * u).astype(a_ref.dtype)

    wsp = pl.BlockSpec((None, f8, d), lambda j, i: (j, 0, 0))
    osp = pl.BlockSpec((None, tm, f8), lambda j, i: (j, i, 0))
    return pl.pallas_call(
        body, name=name, grid=(nb, s_ // tm),
        in_specs=[pl.BlockSpec((tm, d), lambda j, i: (i, 0)), wsp, wsp], out_specs=[osp] * 3,
        out_shape=[jax.ShapeDtypeStruct((nb, s_, f8), MXU_DTYPE)] * 3,
        compiler_params=_params("parallel", "parallel"),
    )(vv, w_gate, w_up)


def _ffn_bwd_act(dffn, w_down, gate, up, name="ffn_d_act"):
    s_, d = dffn.shape
    nb, f8, _ = w_down.shape
    tm = _pick(s_, (1024, 512, 256, 128))

    def body(d_ref, w_ref, g_ref, u_ref, dg_ref, du_ref):
        for rs in _row_slices(tm, 16):
            dact = _dot(_mx(d_ref[rs, :]), _mx(w_ref[...]), 1, 1)
            dg_ref[rs, :] = (dact * g_ref[rs, :].astype(F32)).astype(dg_ref.dtype)
            du_ref[rs, :] = (dact * u_ref[rs, :].astype(F32)).astype(du_ref.dtype)

    osp = pl.BlockSpec((None, tm, f8), lambda j, i: (j, i, 0))
    return pl.pallas_call(
        body, name=name, grid=(nb, s_ // tm),
        in_specs=[pl.BlockSpec((tm, d), lambda j, i: (i, 0)), pl.BlockSpec((None, f8, d), lambda j, i: (j, 0, 0)),
                  osp, osp],
        out_specs=[osp, osp], out_shape=[jax.ShapeDtypeStruct((nb, s_, f8), MXU_DTYPE)] * 2,
        compiler_params=_params("parallel", "parallel"),
    )(dffn, w_down, gate, up)


def _ffn_bwd_in(dgate, w_gate, dup, w_up, name="ffn_d_in"):
    nb, s_, f8 = dgate.shape
    d = w_gate.shape[2]
    tm = _pick(s_, (1024, 512, 256, 128))
    tn = _pick(d, (2048, 1024, 512, 256, 128))

    def body(dg_ref, wg_ref, du_ref, wu_ref, o_ref, acc):
        j = pl.program_id(2)

        @pl.when(j == 0)
        def _():
            acc[...] = jnp.zeros_like(acc)

        for rs in _row_slices(tm, 16):
            acc[rs, :] += (_dot(_mx(dg_ref[rs, :]), _mx(wg_ref[...]), 1, 0)
                           + _dot(_mx(du_ref[rs, :]), _mx(wu_ref[...]), 1, 0))

        @pl.when(j == nb - 1)
        def _():
            o_ref[...] = acc[...]

    asp = pl.BlockSpec((None, tm, f8), lambda i, n, j: (j, i, 0))
    wsp = pl.BlockSpec((None, f8, tn), lambda i, n, j: (j, 0, n))
    return pl.pallas_call(
        body, name=name, grid=(s_ // tm, d // tn, nb),
        in_specs=[asp, wsp, asp, wsp], out_specs=pl.BlockSpec((tm, tn), lambda i, n, j: (i, n)),
        out_shape=jax.ShapeDtypeStruct((s_, d), F32), scratch_shapes=[pltpu.VMEM((tm, tn), F32)],
        compiler_params=_params("parallel", "parallel", "arbitrary"),
    )(dgate, w_gate, dup, w_up)


def _adam_math(g, w, m, v):
    m2 = ADAM_B1 * m + (1.0 - ADAM_B1) * g
    v2 = ADAM_B2 * v + (1.0 - ADAM_B2) * (g * g)
    m_hat = m2 / (1.0 - ADAM_B1 ** ADAM_STEP)
    v_hat = v2 / (1.0 - ADAM_B2 ** ADAM_STEP)
    delta = -ADAM_LR * (m_hat / (jnp.sqrt(v_hat) + ADAM_EPS) + ADAM_WD * w)
    return delta, m2, v2


def _adamw(parts, own, me, w, m, v, name="adamw"):
    nd, r_, c = parts.shape
    tr = _pick(r_, (128, 64, 32, 16))
    tc = c
    if tr == r_ and r_ > 128:
        tc = _pick(c, (256, 128))

    def body(me_ref, p_ref, own_ref, w_ref, m_ref, v_ref, g_ref, d_ref, m2_ref, v2_ref):
        mine = me_ref[0]
        g = jnp.zeros((tr, tc), F32)
        for i in range(nd):
            g = g + jnp.where(mine == i, own_ref[...], p_ref[i]).astype(F32)
        delta, m2, v2 = _adam_math(g, w_ref[...], m_ref[...], v_ref[...])
        g_ref[...] = g
        d_ref[...] = delta
        m2_ref[...] = m2
        v2_ref[...] = v2

    row = pl.BlockSpec((tr, tc), lambda i, j, me_: (i, j))
    gs = pltpu.PrefetchScalarGridSpec(
        num_scalar_prefetch=1, grid=(r_ // tr, c // tc),
        in_specs=[pl.BlockSpec((nd, tr, tc), lambda i, j, me_: (0, i, j)),
                  pl.BlockSpec((None, tr, tc), lambda i, j, me_: (me_[0], i, j)), row, row, row],
        out_specs=[row] * 4)
    return pl.pallas_call(
        body, name=name, grid_spec=gs, out_shape=[jax.ShapeDtypeStruct((r_, c), F32)] * 4,
        compiler_params=_params("parallel", "parallel"),
    )(me, parts, own, w, m, v)


def _adamw_small(parts, w, m, v, name="adamw_small"):
    nd = parts.shape[0]

    def body(p_ref, w_ref, m_ref, v_ref, g_ref, d_ref, m2_ref, v2_ref):
        g = p_ref[0]
        for i in range(1, nd):
            g = g + p_ref[i]
        delta, m2, v2 = _adam_math(g, w_ref[...], m_ref[...], v_ref[...])
        g_ref[...] = g
        d_ref[...] = delta
        m2_ref[...] = m2
        v2_ref[...] = v2

    return pl.pallas_call(
        body, name=name, out_shape=[jax.ShapeDtypeStruct(w.shape, F32)] * 4,
        compiler_params=pltpu.CompilerParams(vmem_limit_bytes=VMEM_LIMIT_BYTES),
    )(parts, w, m, v)


_HBM = pl.BlockSpec(memory_space=pltpu.HBM)
_MESH = pl.DeviceIdType.MESH


def _all_gather(xs, name):
    na = len(xs)

    def body(*refs):
        x_refs, out_refs = refs[:na], refs[na:2 * na]
        send_sems, recv_sems, local_sems = refs[2 * na:]
        x, y, c = lax.axis_index("x"), lax.axis_index("y"), lax.axis_index("c")
        me, sibling = (x, y, c), (x, y, 1 - c)
        chips = [(1 - x, y), (x, 1 - y), (1 - x, 1 - y)]

        def slot(a, px, py, pc):
            return out_refs[a].at[4 * px + 2 * py + pc]

        def copy(a, k, block, to, src=None):
            return pltpu.make_async_remote_copy(
                src_ref=slot(a, *block) if src is None else src, dst_ref=slot(a, *block),
                send_sem=send_sems.at[a, k], recv_sem=recv_sems.at[a, k], device_id=to, device_id_type=_MESH)

        mine = [pltpu.make_async_copy(x_refs[a], slot(a, *me), local_sems.at[a]) for a in range(na)]
        started = []
        for a in range(na):
            mine[a].start()
            first = [copy(a, 0, me, sibling, src=x_refs[a])]
            first += [copy(a, 1 + j, me, (*chip, c), src=x_refs[a]) for j, chip in enumerate(chips)]
            for cp in first:
                cp.start()
            started += first
        for a in range(na):
            for j, chip in enumerate(chips):
                copy(a, 1 + j, (*chip, c), me).wait_recv()
                fwd = copy(a, 4 + j, (*chip, c), sibling)
                fwd.start()
                started.append(fwd)
        for a in range(na):
            copy(a, 0, sibling, me).wait_recv()
            for j, chip in enumerate(chips):
                copy(a, 4 + j, (*chip, 1 - c), me).wait_recv()
        for cp in started:
            cp.wait_send()
        for cp in mine:
            cp.wait()

    return pl.pallas_call(
        body, name=name, out_shape=[jax.ShapeDtypeStruct((N_DEV,) + t.shape, t.dtype) for t in xs],
        in_specs=[_HBM] * na, out_specs=[_HBM] * na,
        scratch_shapes=[pltpu.SemaphoreType.DMA((na, 7)), pltpu.SemaphoreType.DMA((na, 7)),
                        pltpu.SemaphoreType.DMA((na,))],
    )(*xs)


_SEM = pl.BlockSpec(memory_space=pltpu.SEMAPHORE)
_EFFECT = pltpu.SideEffectType.DATAFLOW_SIDE_EFFECTING


def _peers(x, y, c):
    out = []
    for k in range(1, N_DEV):
        px = 1 - x if k & 4 else x
        py = 1 - y if k & 2 else y
        pc = 1 - c if k & 1 else c
        out.append(((px, py, pc), 4 * px + 2 * py + pc))
    return out


def _push_copies(scatter, src_refs, land_refs, send_sems, recv_sems):
    x, y, c = lax.axis_index("x"), lax.axis_index("y"), lax.axis_index("c")
    me = 4 * x + 2 * y + c
    pairs = []
    for a, (src, land) in enumerate(zip(src_refs, land_refs)):
        for k, (peer, slot) in enumerate(_peers(x, y, c)):
            out_src = src.at[slot] if scatter else src
            si = a * (N_DEV - 1) + k
            send = pltpu.make_async_remote_copy(src_ref=out_src, dst_ref=land.at[me], send_sem=send_sems.at[si],
                                                recv_sem=recv_sems.at[si], device_id=peer, device_id_type=_MESH)
            recv = pltpu.make_async_remote_copy(src_ref=out_src, dst_ref=land.at[slot], send_sem=send_sems.at[si],
                                                recv_sem=recv_sems.at[si], device_id=peer, device_id_type=_MESH)
            pairs.append((send, recv))
    return pairs


def _push_start(srcs, scatter, dep, name):
    na = len(srcs)
    shapes = [t.shape[1:] if scatter else t.shape for t in srcs]
    lands = [pltpu.with_memory_space_constraint(lax.empty((N_DEV,) + s, t.dtype), pltpu.HBM) for s, t in zip(shapes, srcs)]

    def body(*refs):
        src_refs, land_refs = refs[:na], refs[na:2 * na]
        send_sems, recv_sems = refs[2 * na + 1], refs[2 * na + 2]
        token = refs[-1]
        for send, _ in _push_copies(scatter, src_refs, land_refs, send_sems, recv_sems):
            send.start()
        token[...] = jnp.zeros_like(token)

    sem = pltpu.SemaphoreType.DMA((na * (N_DEV - 1),))
    outs = pl.pallas_call(
        body, name=name,
        out_shape=(sem, sem) + tuple(pltpu.HBM(t.shape, t.dtype) for t in srcs)
        + tuple(pltpu.HBM(t.shape, t.dtype) for t in lands) + (jax.ShapeDtypeStruct((8, LANE), F32),),
        in_specs=[_HBM] * (2 * na) + [pl.BlockSpec(memory_space=pl.ANY)],
        out_specs=(_SEM, _SEM) + (_HBM,) * (2 * na) + (pl.BlockSpec(memory_space=pltpu.VMEM),),
        input_output_aliases={i: 2 + i for i in range(2 * na)},
        compiler_params=pltpu.CompilerParams(has_side_effects=_EFFECT),
    )(*[pltpu.with_memory_space_constraint(t, pltpu.HBM) for t in srcs], *lands, dep)
    return outs[0], outs[1], outs[2:2 + na], outs[2 + na:2 + 2 * na], outs[-1]


def _push_wait(send_sems, recv_sems, src_thru, land_thru, scatter, after, name):
    na = len(src_thru)

    def body(*refs):
        src_refs, land_refs = refs[:na], refs[na:2 * na]
        ssem, rsem = refs[2 * na], refs[2 * na + 1]
        for send, recv in _push_copies(scatter, src_refs, land_refs, ssem, rsem):
            send.wait_send()
            recv.wait_recv()

    outs = pl.pallas_call(
        body, name=name,
        out_shape=tuple(pltpu.HBM(t.shape, t.dtype) for t in src_thru) + tuple(pltpu.HBM(t.shape, t.dtype) for t in land_thru),
        in_specs=[_HBM] * (2 * na) + [_SEM, _SEM, pl.BlockSpec(memory_space=pl.ANY)],
        out_specs=(_HBM,) * (2 * na),
        input_output_aliases={i: i for i in range(2 * na)},
        compiler_params=pltpu.CompilerParams(has_side_effects=_EFFECT),
    )(*src_thru, *land_thru, send_sems, recv_sems, after)
    return outs[:na], outs[na:]


def _exchange_behind(srcs, scatter, dep, name):
    send_sems, recv_sems, thru, lands, token = _push_start(srcs, scatter, dep, name + "_start")

    def finish(after, place=True):
        src_done, land_done = _push_wait(send_sems, recv_sems, thru, lands, scatter, after, name + "_wait")
        if not place:
            return land_done, src_done
        return _place_own(land_done, src_done, scatter, name + "_own")

    return token[0, 0], finish


def _place_own(lands, srcs, scatter, name):
    me = (4 * lax.axis_index("x") + 2 * lax.axis_index("y") + lax.axis_index("c")).astype(jnp.int32).reshape(1)
    outs = []
    for a, (land, src) in enumerate(zip(lands, srcs)):
        r_, c_ = land.shape[1:]
        tr = _pick(r_, (512, 256, 128, 64, 32, 16))

        def body(me_ref, land_ref, src_ref, out_ref):
            out_ref[...] = src_ref[...]

        src_spec = (pl.BlockSpec((None, tr, c_), lambda i, me_: (me_[0], i, 0)) if scatter
                    else pl.BlockSpec((tr, c_), lambda i, me_: (i, 0)))
        gs = pltpu.PrefetchScalarGridSpec(
            num_scalar_prefetch=1, grid=(r_ // tr,),
            in_specs=[pl.BlockSpec(memory_space=pl.ANY), src_spec],
            out_specs=pl.BlockSpec((None, tr, c_), lambda i, me_: (me_[0], i, 0)))
        outs.append(pl.pallas_call(
            body, name=f"{name}_{a}", grid_spec=gs, out_shape=jax.ShapeDtypeStruct(land.shape, land.dtype),
            input_output_aliases={1: 0}, compiler_params=_params("arbitrary"),
        )(me, land, src))
    return outs


_TRANSPOSED = ("w_in", "w_uq", "w_gate", "w_up")
_CQKV = (0, Q_RANK + KV_RANK)
_KR = (_CQKV[1], _CQKV[1] + ROPE)
_Z = (_KR[1], _KR[1] + SSD_W)
_XBC = (_Z[1], _Z[1] + CONV_DIM)
_DT = (_XBC[1], _XBC[1] + SSD_H)


def _win_segments(w_in_t):
    w = w_in_t.reshape(D_IN, D_MODEL)
    small = jnp.concatenate([w[_KR[0]:_KR[1]], w[_DT[0]:_DT[1]],
                             jnp.zeros((LANE - ROPE - SSD_H, D_MODEL), w.dtype)], axis=0)
    return w[_CQKV[0]:_CQKV[1]], w[_Z[0]:_Z[1]], w[_XBC[0]:_XBC[1]], small


def _win_from_segments(g_cqkv, g_z, g_xbc, g_small):
    w = jnp.concatenate([g_cqkv, g_small[:ROPE], g_z, g_xbc, g_small[ROPE:ROPE + SSD_H]], axis=0)
    return w.reshape(N_DEV, D_IN // N_DEV, D_MODEL)


_SMALL = (("q_norm_w", 512), ("kv_norm_w", 512), ("conv_b", CONV_DIM), ("dt_bias", SSD_H), ("a_log", SSD_H),
          ("d_skip", SSD_H), ("ssd_norm_w", SSD_W), ("attn_out_norm_w", 1024), ("pre_mix_norm_w", D_MODEL),
          ("post_mix_norm_w", D_MODEL), ("pre_ffn_norm_w", D_MODEL), ("post_ffn_norm_w", D_MODEL),
          ("conv_w", CONV_K * CONV_DIM))
_SMALL_ROWS = -(-sum(-(-n // LANE) for _, n in _SMALL) // 8) * 8


def _pack_small(vals):
    rows = []
    for name, n in _SMALL:
        v = vals[name].reshape(-1).astype(F32)
        pad = -(-n // LANE) * LANE
        rows.append(jnp.pad(v, (0, pad - n)).reshape(-1, LANE))
    m = jnp.concatenate(rows, axis=0)
    return jnp.pad(m, ((0, _SMALL_ROWS - m.shape[0]), (0, 0)))


def _unpack_small(m):
    out, r = {}, 0
    for name, n in _SMALL:
        nr = -(-n // LANE)
        out[name] = m[r:r + nr].reshape(-1)[:n]
        r += nr
    return out


def _head_row(v):
    return jnp.pad(v.reshape(1, -1).astype(F32), ((0, 0), (HEAD_LANE, LANE - HEAD_LANE - v.shape[-1])))


def _local_step(x, positions, target, wg, small, weights, on_grads):
    w_cqkv, w_z, w_xbc, w_small = _win_segments(wg["w_in"])
    conv_w = wg["conv_w"]
    conv_b = small["conv_b"].reshape(1, CONV_DIM)
    qkv_norm_w = jnp.concatenate([small["q_norm_w"], small["kv_norm_w"]])
    attn_norm_w = small["attn_out_norm_w"].reshape(1, HEADS * VDIM)
    scale = QK ** -0.5

    inv_freq = ROPE_THETA ** (-jnp.arange(0, ROPE, 2, dtype=F32) / ROPE)
    ang = positions.astype(F32)[:, None] * inv_freq
    cos2 = jnp.tile(jnp.cos(ang), (1, 2))
    sin2 = jnp.tile(jnp.sin(ang), (1, 2))

    u = _rms_fwd(x, small["pre_mix_norm_w"], out_dtype=MXU_DTYPE, name="pre_mix_norm")
    cqkv = _mm(u, w_cqkv, "nt", name="in_proj_qkv")
    z = _mm(u, w_z, "nt", name="in_proj_z")
    xbc = _mm(u, w_xbc, "nt", name="in_proj_xbc")
    sm = _mm(u, w_small, "nt", name="in_proj_small")

    w_uq, w_ukv = weights("qkv_up", cqkv)
    qkvn = _rms_fwd(cqkv, qkv_norm_w, groups=2, out_dtype=MXU_DTYPE, name="qkv_norm")
    q_h = _q_up(qkvn, w_uq, cos2, sin2, scale)
    k_h, v_h = _kv_up(qkvn, w_ukv, sm, cos2, sin2)
    o_h, lse = _flash_fwd(q_h, k_h, v_h)
    cat = _hnorm_fwd(o_h, attn_norm_w, D_MODEL)
    w_out = weights("out", o_h)[0].reshape(D_MODEL, D_MODEL)

    xbc_act = _conv_fwd(xbc, conv_w, conv_b)
    dtt = jnp.transpose(sm[:, HEAD_LANE:HEAD_LANE + SSD_H])
    ssd_args = (xbc_act, sm, dtt, _head_row(small["dt_bias"]), small["dt_bias"].reshape(SSD_H, 1),
                _head_row(small["a_log"]), small["a_log"].reshape(SSD_H, 1),
                jnp.broadcast_to(small["d_skip"].reshape(SSD_H, 1), (SSD_H, SSD_P)).reshape(SSD_PAIRS, 1, LANE))
    y_ssd, prev = _ssd_fwd(*ssd_args)
    cat = _gated_norm_fwd(y_ssd, z, small["ssd_norm_w"], cat)

    mix = _mm(cat, w_out, "nn", name="out_proj")
    h1, vv = _norm_res_norm(mix, x, small["post_mix_norm_w"], small["pre_ffn_norm_w"])

    w_gate, w_up = weights("ffn_in", mix)
    gate, up, act = _ffn_fwd(vv, w_gate, w_up)
    w_down, = weights("ffn_out", act)
    ffn = _mm(act, w_down, "nn", a_blk=True, b_blk=True, fuse=2, wide=True, name="ffn_down")
    loss_blk, dy, dffn, g_post_ffn = _loss_head(ffn, h1, target, small["post_ffn_norm_w"])

    g_down = _mm(act, dffn, "tn", a_blk=True, out_blk=True, out_dtype=MXU_DTYPE, name="g_down")
    dgate, dup = _ffn_bwd_act(dffn, w_down, gate, up)
    dvv = _ffn_bwd_in(dgate, w_gate, dup, w_up)
    g_gate = _mm(dgate, vv, "tn", a_blk=True, out_blk=True, out_dtype=MXU_DTYPE, name="g_gate")
    g_up = _mm(dup, vv, "tn", a_blk=True, out_blk=True, out_dtype=MXU_DTYPE, name="g_up")
    pre_ffn_w = small["pre_ffn_norm_w"] + on_grads("ffn", [g_gate, g_up, g_down])
    dh1, dmix, g_pre_ffn, g_post_mix = _norm_res_norm_bwd(h1, pre_ffn_w, dvv, dy, mix, small["post_mix_norm_w"])

    dcat = _mm(dmix, w_out, "nt", name="d_cat")
    g_out = _mm(cat, dmix, "tn", out_dtype=MXU_DTYPE, name="g_out")

    do_h, delta, g_attn_norm = _hnorm_bwd(o_h, attn_norm_w, dcat)
    dq_h, dk_h, dv_h = _flash_bwd(q_h, k_h, v_h, do_h, lse, delta)
    dq = _q_prep(dq_h, cos2, -sin2, scale, name="dq_post")

    dy_ssd, dz, g_ssd_norm = _gated_norm_bwd(y_ssd, z, small["ssd_norm_w"], dcat)
    dxbc_act, ddt, dpar = _ssd_bwd(*ssd_args, prev, dy_ssd)
    dkv, dsm = _dkv_post(dk_h, dv_h, ddt, cos2, -sin2)
    dpre, dwb = _conv_bwd_pre(xbc, conv_w, conv_b, dxbc_act)
    dxbc = _conv_bwd_in(dpre, conv_w)

    dqn = _mm(dq, w_uq, "nn", a_blk=True, b_blk=True, fuse=HEADS, name="d_qn")
    dkvn = _mm(dkv, w_ukv, "nt", a_blk=True, b_blk=True, fuse=HEADS, name="d_kvn")
    g_uq = _mm(dq, qkvn, "tn", a_blk=True, out_blk=True, b_cols=(0, Q_RANK), out_dtype=MXU_DTYPE, name="g_uq")
    g_ukv = _mm(qkvn, dkv, "tn", b_blk=True, out_blk=True, a_cols=(Q_RANK, KV_RANK), out_dtype=MXU_DTYPE, name="g_ukv")
    heads_token = on_grads("heads", [g_uq, g_ukv, g_out.reshape(N_DEV, D_MODEL // N_DEV, D_MODEL)])
    dcqkv, g_qkv_norm = _rms_bwd(cqkv, qkv_norm_w + heads_token, [dqn, dkvn], out_dtype=MXU_DTYPE, name="qkv_norm_bwd")

    g_in = _win_from_segments(_mm(dcqkv, u, "tn", out_dtype=MXU_DTYPE, name="g_in_qkv"),
                              _mm(dz, u, "tn", out_dtype=MXU_DTYPE, name="g_in_z"),
                              _mm(dxbc, u, "tn", out_dtype=MXU_DTYPE, name="g_in_xbc"),
                              _mm(dsm, u, "tn", out_dtype=MXU_DTYPE, name="g_in_small"))
    in_token = on_grads("in", [g_in])
    du = _mm_sum([dsm + in_token.astype(dsm.dtype), dcqkv, dz, dxbc], [w_small, w_cqkv, w_z, w_xbc], name="d_u")
    dx, g_pre_mix = _rms_bwd(x, small["pre_mix_norm_w"], [du], res=dh1, name="pre_mix_norm_bwd")

    hl = slice(HEAD_LANE, HEAD_LANE + SSD_H)
    g_small = {"q_norm_w": g_qkv_norm[0, :Q_RANK], "kv_norm_w": g_qkv_norm[0, Q_RANK:], "conv_b": dwb[CONV_K],
               "dt_bias": dpar[0, hl], "a_log": dpar[1, hl], "d_skip": dpar[2, hl], "ssd_norm_w": g_ssd_norm,
               "attn_out_norm_w": g_attn_norm, "pre_mix_norm_w": g_pre_mix, "post_mix_norm_w": g_post_mix,
               "pre_ffn_norm_w": g_pre_ffn, "post_ffn_norm_w": g_post_ffn, "conv_w": dwb[:CONV_K]}
    return loss_blk[0, 0], dx, g_small


_WEIGHT_ORDER = ("w_in", "q_norm_w", "w_uq", "kv_norm_w", "w_ukv", "conv_w", "conv_b", "dt_bias", "a_log", "d_skip",
                 "ssd_norm_w", "attn_out_norm_w", "w_out", "pre_mix_norm_w", "post_mix_norm_w", "pre_ffn_norm_w",
                 "post_ffn_norm_w", "w_gate", "w_up", "w_down")


def kernel(x, positions, w_in, q_norm_w, w_uq, kv_norm_w, w_ukv, conv_w, conv_b, dt_bias, a_log, d_skip, ssd_norm_w, attn_out_norm_w, w_out, pre_mix_norm_w, post_mix_norm_w, pre_ffn_norm_w, post_ffn_norm_w, w_gate, w_up, w_down, loss_target, m_w_in, m_q_norm_w, m_w_uq, m_kv_norm_w, m_w_ukv, m_conv_w, m_conv_b, m_dt_bias, m_a_log, m_d_skip, m_ssd_norm_w, m_attn_out_norm_w, m_w_out, m_pre_mix_norm_w, m_post_mix_norm_w, m_pre_ffn_norm_w, m_post_ffn_norm_w, m_w_gate, m_w_up, m_w_down, v_w_in, v_q_norm_w, v_w_uq, v_kv_norm_w, v_w_ukv, v_conv_w, v_conv_b, v_dt_bias, v_a_log, v_d_skip, v_ssd_norm_w, v_attn_out_norm_w, v_w_out, v_pre_mix_norm_w, v_post_mix_norm_w, v_pre_ffn_norm_w, v_post_ffn_norm_w, v_w_gate, v_w_up, v_w_down):
    w = dict(w_in=w_in, q_norm_w=q_norm_w, w_uq=w_uq, kv_norm_w=kv_norm_w, w_ukv=w_ukv, conv_w=conv_w, conv_b=conv_b,
             dt_bias=dt_bias, a_log=a_log, d_skip=d_skip, ssd_norm_w=ssd_norm_w, attn_out_norm_w=attn_out_norm_w,
             w_out=w_out, pre_mix_norm_w=pre_mix_norm_w, post_mix_norm_w=post_mix_norm_w,
             pre_ffn_norm_w=pre_ffn_norm_w, post_ffn_norm_w=post_ffn_norm_w, w_gate=w_gate, w_up=w_up, w_down=w_down)
    m = dict(w_in=m_w_in, q_norm_w=m_q_norm_w, w_uq=m_w_uq, kv_norm_w=m_kv_norm_w, w_ukv=m_w_ukv, conv_w=m_conv_w,
             conv_b=m_conv_b, dt_bias=m_dt_bias, a_log=m_a_log, d_skip=m_d_skip, ssd_norm_w=m_ssd_norm_w,
             attn_out_norm_w=m_attn_out_norm_w, w_out=m_w_out, pre_mix_norm_w=m_pre_mix_norm_w,
             post_mix_norm_w=m_post_mix_norm_w, pre_ffn_norm_w=m_pre_ffn_norm_w, post_ffn_norm_w=m_post_ffn_norm_w,
             w_gate=m_w_gate, w_up=m_w_up, w_down=m_w_down)
    v = dict(w_in=v_w_in, q_norm_w=v_q_norm_w, w_uq=v_w_uq, kv_norm_w=v_kv_norm_w, w_ukv=v_w_ukv, conv_w=v_conv_w,
             conv_b=v_conv_b, dt_bias=v_dt_bias, a_log=v_a_log, d_skip=v_d_skip, ssd_norm_w=v_ssd_norm_w,
             attn_out_norm_w=v_attn_out_norm_w, w_out=v_w_out, pre_mix_norm_w=v_pre_mix_norm_w,
             post_mix_norm_w=v_post_mix_norm_w, pre_ffn_norm_w=v_pre_ffn_norm_w, post_ffn_norm_w=v_post_ffn_norm_w,
             w_gate=v_w_gate, w_up=v_w_up, w_down=v_w_down)
    w, m, v = ({k: t[0] for k, t in d.items()} for d in (w, m, v))
    me = 4 * lax.axis_index("x") + 2 * lax.axis_index("y") + lax.axis_index("c")
    groups = {"qkv_up": ("w_uq", "w_ukv"), "out": ("w_out",), "ffn_in": ("w_gate", "w_up"), "ffn_out": ("w_down",)}
    cshard = CONV_DIM // N_DEV
    for name in _TRANSPOSED:
        w[name], m[name], v[name] = w[name].T, m[name].T, v[name].T

    shards = [w["w_in"].astype(MXU_DTYPE),
              jnp.stack(_split3(w["conv_w"])).reshape(3 * CONV_K, cshard).astype(MXU_DTYPE)]
    w_in_g, cw = _all_gather(shards, name="gather_weights")
    cw = cw.astype(F32).reshape(N_DEV, 3, CONV_K, cshard)
    wg = {"w_in": w_in_g, "conv_w": jnp.transpose(cw[:, 0] + cw[:, 1] + cw[:, 2], (1, 0, 2)).reshape(CONV_K, CONV_DIM)}
    arriving, dep, started = {}, wg["conv_w"], jnp.zeros((), F32)
    small = {name: w[name] for name, _ in _SMALL if name != "conv_w"}
    for group in ("qkv_up", "out", "ffn_in", "ffn_out"):
        token, arriving[group] = _exchange_behind([w[name].astype(MXU_DTYPE) for name in groups[group]], False,
                                                  dep, group + "_weights")
        started = started + token
        dep = jnp.zeros((8, LANE), F32) + started
    small["pre_mix_norm_w"] = small["pre_mix_norm_w"] + started

    leaving = {}

    def on_grads(group, gs):
        token, leaving[group] = _exchange_behind(gs, True, jnp.zeros((8, LANE), F32), group + "_grads")
        return token

    loss_local, dx, g_small = _local_step(x[0], positions[0], loss_target[0], wg, small,
                                          lambda group, after: arriving[group](after), on_grads)
    loss = lax.psum(loss_local, ("x", "y", "c"))

    recv = {}
    for group, names in (("ffn", ("w_gate", "w_up", "w_down")), ("heads", ("w_uq", "w_ukv", "w_out")), ("in", ("w_in",))):
        recv.update(zip(names, zip(*leaving[group](dx, place=False))))
    grads, deltas, new_m, new_v = {}, {}, {}, {}
    me1 = me.astype(jnp.int32).reshape(1)
    for name, (parts, own) in recv.items():
        outs = _adamw(parts, own, me1, w[name], m[name], v[name], name="adamw_" + name)
        if name in _TRANSPOSED:
            outs = [t.T for t in outs]
        grads[name], deltas[name], new_m[name], new_v[name] = outs

    def embed(t):
        return lax.dynamic_update_slice(jnp.zeros((CONV_K, CONV_DIM), F32), t, (0, me * cshard))

    parts_s = _all_gather([_pack_small(g_small)], name="gather_small_grads")[0]
    packs = [_pack_small({**{n_: d[n_] for n_, _ in _SMALL if n_ != "conv_w"}, "conv_w": embed(d["conv_w"])})
             for d in (w, m, v)]
    outs = [_unpack_small(t) for t in _adamw_small(parts_s, *packs)]
    for name, n in _SMALL:
        for dst, src in zip((grads, deltas, new_m, new_v), outs):
            if name == "conv_w":
                dst[name] = lax.dynamic_slice(src[name].reshape(CONV_K, CONV_DIM), (0, me * cshard), (CONV_K, cshard))
            else:
                dst[name] = src[name]

    def lead(d):
        return [d[name][None] for name in _WEIGHT_ORDER]

    return (loss, dx[None], *lead(grads), *lead(deltas), *lead(new_m), *lead(new_v))
```

```python
import numpy as np

import jax
import jax.numpy as jnp
from jax import lax
from jax.experimental import pallas as pl
from jax.experimental.pallas import tpu as pltpu

F32 = jnp.float32
BF16 = jnp.bfloat16
MXU_DTYPE = jnp.bfloat16
EPS = 1e-6
VMEM_LIMIT_BYTES = 48 * 1024 * 1024
K_TILE_MAX = 2048

N_DEV = 8
D_MODEL = 2048
Q_RANK = 512
KV_RANK = 512
ROPE = 64
HALF = ROPE // 2
HEADS = 8
NOPE = 128
VDIM = 128
QK = NOPE + ROPE
SSD_W = 1024
SSD_H = 16
SSD_P = 64
SSD_G = 2
SSD_E = SSD_H // SSD_G
SSD_N = 128
CHUNK = 128
CONV_K = 4
CONV_DIM = SSD_W + 2 * SSD_G * SSD_N
B_OFF = SSD_W
C_OFF = SSD_W + SSD_G * SSD_N
D_FF = 5632
D_IN = Q_RANK + KV_RANK + ROPE + SSD_W + CONV_DIM + SSD_H
ROPE_THETA = 10000.0
LANE = 128
HEAD_LANE = ROPE

ADAM_LR = 0.001
ADAM_B1 = 0.9
ADAM_B2 = 0.999
ADAM_EPS = 1e-08
ADAM_WD = 0.01
ADAM_STEP = 10


def _pick(n, cands):
    for c in cands:
        if n % c == 0:
            return c
    return n


def _params(*sem):
    return pltpu.CompilerParams(dimension_semantics=sem, vmem_limit_bytes=VMEM_LIMIT_BYTES)


def _sigmoid(x):
    return 1.0 / (1.0 + jnp.exp(-x))


def _silu(x):
    return x * _sigmoid(x)


def _dsilu(x):
    s = _sigmoid(x)
    return s * (1.0 + x * (1.0 - s))


def _softplus(x):
    e = jnp.exp(-jnp.abs(x))
    small = e * (1.0 - e * (0.5 - e * (1.0 / 3.0)))
    return jnp.maximum(x, 0.0) + jnp.where(e < 0.01, small, jnp.log(1.0 + e))


def _dot(a, b, ca, cb):
    return lax.dot_general(a, b, (((ca,), (cb,)), ((), ())), preferred_element_type=F32)


def _mx(v):
    return v.astype(MXU_DTYPE)


def _split3(a):
    hi = a.astype(BF16)
    r1 = a - hi.astype(F32)
    mid = r1.astype(BF16)
    lo = (r1 - mid.astype(F32)).astype(BF16)
    return hi, mid, lo


def _exact_dot(a, b, ca, cb, split_a):
    if split_a:
        return sum(_dot(p, b, ca, cb) for p in _split3(a))
    return sum(_dot(a, p, ca, cb) for p in _split3(b))


MM_ROW_GROUPS = 4


def _row_slices(tm, align):
    ng = MM_ROW_GROUPS
    while ng > 1 and (tm % ng or (tm // ng) % align):
        ng //= 2
    return [slice(g * (tm // ng), (g + 1) * (tm // ng)) for g in range(ng)]


def _mm(a, b, mode, *, a_blk=False, b_blk=False, out_blk=False, a_cols=None, b_cols=None, add=None, out_dtype=F32,
        fuse=1, wide=False, name="mm"):
    a2, b2 = a.shape[-2:], b.shape[-2:]
    a_last = a2[1] if a_cols is None else a_cols[1]
    a_start = 0 if a_cols is None else a_cols[0]
    b_start = 0
    if b_cols is not None:
        assert mode != "nt"
        b_start, b2 = b_cols[0], (b2[0], b_cols[1])
    if mode == "nn":
        m, k, (k2, n) = a2[0], a_last, b2
    elif mode == "nt":
        m, k, (n, k2) = a2[0], a_last, b2
    else:
        k, m, (k2, n) = a2[0], a_last, b2
    assert k == k2, (a.shape, b.shape, mode)
    tm = _pick(m, (1024, 704, 512, 256, 128))
    tn = _pick(n, ((2048,) if wide else ()) + (1024, 768, 704, 512, 256, 192, 128))
    tk = k if k <= K_TILE_MAX else _pick(k, (K_TILE_MAX, 1024, 512))
    nk = k // tk
    jo = N_DEV if out_blk else 1
    reduce_blocks = a_blk and b_blk and not out_blk
    assert fuse == 1 or reduce_blocks
    jr = N_DEV // fuse if reduce_blocks else 1
    ca, cb = {"nn": (1, 0), "nt": (1, 1), "tn": (0, 0)}[mode]
    has_add = add is not None
    single = jr * nk == 1
    if mode == "tn":
        assert a_start % tm == 0
        a_block, a_idx = (tk, tm), (lambda i, kk: (kk, i + a_start // tm))
    else:
        assert a_start % tk == 0
        a_block, a_idx = (tm, tk), (lambda i, kk: (i, kk + a_start // tk))
    assert b_start % tn == 0
    b_block, b_idx = (((tn, tk), (lambda nn_, kk: (nn_, kk))) if mode == "nt"
                      else ((tk, tn), (lambda nn_, kk: (kk, nn_ + b_start // tn))))

    def blk_specs(blocked, block, idx, of_a, t):
        def pos(o, i, nn_, kk):
            return idx(i, kk) if of_a else idx(nn_, kk)
        if blocked:
            return pl.BlockSpec((None,) + block,
                                lambda o, i, nn_, r, kk: ((o if out_blk else r * fuse + t),) + pos(o, i, nn_, kk))
        return pl.BlockSpec(block, lambda o, i, nn_, r, kk: pos(o, i, nn_, kk))

    a_specs = [blk_specs(a_blk, a_block, a_idx, True, t) for t in range(fuse)]
    b_specs = [blk_specs(b_blk, b_block, b_idx, False, t) for t in range(fuse)]
    o_spec = (pl.BlockSpec((None, tm, tn), lambda o, i, nn_, r, kk: (o, i, nn_)) if out_blk
              else pl.BlockSpec((tm, tn), lambda o, i, nn_, r, kk: (i, nn_)))

    groups = _row_slices(tm, LANE if mode == "tn" else 16)

    def body(*refs):
        a_refs, b_refs = refs[:fuse], refs[fuse:2 * fuse]
        add_ref = refs[2 * fuse] if has_add else None
        o_ref = refs[2 * fuse + 1] if has_add else refs[2 * fuse]

        def partial(rs):
            out = None
            for t in range(fuse):
                av = a_refs[t][:, rs] if mode == "tn" else a_refs[t][rs, :]
                d = _dot(_mx(av), _mx(b_refs[t][...]), ca, cb)
                out = d if out is None else out + d
            return out

        if single:
            for rs in groups:
                res = partial(rs)
                if has_add:
                    res = res + add_ref[rs, :]
                o_ref[rs, :] = res.astype(o_ref.dtype)
            return
        acc = refs[-1]
        r, kk = pl.program_id(3), pl.program_id(4)

        @pl.when(jnp.logical_and(r == 0, kk == 0))
        def _():
            acc[...] = jnp.zeros_like(acc)

        for rs in groups:
            acc[rs, :] += partial(rs)

        @pl.when(jnp.logical_and(r == jr - 1, kk == nk - 1))
        def _():
            res = acc[...]
            if has_add:
                res = res + add_ref[...]
            o_ref[...] = res.astype(o_ref.dtype)

    out_shape = ((N_DEV, m, n) if out_blk else (m, n))
    return pl.pallas_call(
        body, name=name, grid=(jo, m // tm, n // tn, jr, nk),
        in_specs=a_specs + b_specs + ([o_spec] if has_add else []), out_specs=o_spec,
        out_shape=jax.ShapeDtypeStruct(out_shape, out_dtype),
        scratch_shapes=[] if single else [pltpu.VMEM((tm, tn), F32)],
        compiler_params=_params("parallel", "parallel", "parallel", "arbitrary", "arbitrary"),
    )(*((a,) * fuse + (b,) * fuse + ((add,) if has_add else ())))


def _mm_sum(a_list, b_list, name="mm_sum"):
    m, n = a_list[0].shape[0], b_list[0].shape[1]
    ns = len(a_list)
    tm = _pick(m, (1024, 512, 256, 128))
    tn = _pick(n, (1024, 512, 256, 128))
    groups = _row_slices(tm, 16)

    def body(*refs):
        a_refs, b_refs, o_ref = refs[:ns], refs[ns:2 * ns], refs[2 * ns]
        for rs in groups:
            acc = _dot(_mx(a_refs[0][rs, :]), _mx(b_refs[0][...]), 1, 0)
            for s in range(1, ns):
                acc = acc + _dot(_mx(a_refs[s][rs, :]), _mx(b_refs[s][...]), 1, 0)
            o_ref[rs, :] = acc

    return pl.pallas_call(
        body, name=name, grid=(m // tm, n // tn),
        in_specs=([pl.BlockSpec((tm, a.shape[1]), lambda i, j: (i, 0)) for a in a_list]
                  + [pl.BlockSpec((b.shape[0], tn), lambda i, j: (0, j)) for b in b_list]),
        out_specs=pl.BlockSpec((tm, tn), lambda i, j: (i, j)),
        out_shape=jax.ShapeDtypeStruct((m, n), F32), compiler_params=_params("parallel", "parallel"),
    )(*a_list, *b_list)


def _row_tile(r_, streams=4):
    return _pick(r_, ((512,) if streams <= 4 else ()) + (256, 128, 64, 32, 16, 8))


def _rms_fwd(t, w, groups=1, res=None, out_dtype=F32, name="rms_fwd"):
    r_, f = t.shape
    fg = f // groups
    tr = _row_tile(r_)
    has_res = res is not None

    def body(*refs):
        t_ref, w_ref = refs[0], refs[1]
        res_ref = refs[2] if has_res else None
        o_ref = refs[-1]
        for g in range(groups):
            sl = slice(g * fg, (g + 1) * fg)
            tv = t_ref[:, sl].astype(F32)
            r = lax.rsqrt(jnp.mean(tv * tv, axis=-1, keepdims=True) + EPS)
            y = tv * r * w_ref[:, sl]
            if has_res:
                y = y + res_ref[:, sl]
            o_ref[:, sl] = y.astype(o_ref.dtype)

    row = pl.BlockSpec((tr, f), lambda i: (i, 0))
    wsp = pl.BlockSpec((1, f), lambda i: (0, 0))
    return pl.pallas_call(
        body, name=name, grid=(r_ // tr,),
        in_specs=[row, wsp] + ([row] if has_res else []), out_specs=row,
        out_shape=jax.ShapeDtypeStruct((r_, f), out_dtype),
        compiler_params=_params("parallel"),
    )(*((t, w.reshape(1, f)) + ((res,) if has_res else ())))


def _rms_bwd(t, w, dys, res=None, out_dtype=F32, name="rms_bwd"):
    r_, f = t.shape
    groups = len(dys)
    fg = f // groups
    tr = _row_tile(r_)
    has_res = res is not None

    def body(*refs):
        t_ref, w_ref = refs[0], refs[1]
        dy_refs = refs[2:2 + groups]
        res_ref = refs[2 + groups] if has_res else None
        dt_ref, dw_ref = refs[-2], refs[-1]

        @pl.when(pl.program_id(0) == 0)
        def _():
            dw_ref[...] = jnp.zeros_like(dw_ref)

        for g in range(groups):
            sl = slice(g * fg, (g + 1) * fg)
            tv = t_ref[:, sl].astype(F32)
            dyv = dy_refs[g][...].astype(F32)
            r = lax.rsqrt(jnp.mean(tv * tv, axis=-1, keepdims=True) + EPS)
            gw = dyv * w_ref[:, sl]
            c = jnp.mean(gw * tv, axis=-1, keepdims=True)
            dt = r * gw - tv * (r * r * r * c)
            if has_res:
                dt = dt + res_ref[:, sl]
            dt_ref[:, sl] = dt.astype(dt_ref.dtype)
            dw_ref[:, sl] += jnp.sum(dyv * tv * r, axis=0, keepdims=True)

    row = pl.BlockSpec((tr, f), lambda i: (i, 0))
    grow = pl.BlockSpec((tr, fg), lambda i: (i, 0))
    wsp = pl.BlockSpec((1, f), lambda i: (0, 0))
    return pl.pallas_call(
        body, name=name, grid=(r_ // tr,),
        in_specs=[row, wsp] + [grow] * groups + ([row] if has_res else []), out_specs=[row, wsp],
        out_shape=[jax.ShapeDtypeStruct((r_, f), out_dtype), jax.ShapeDtypeStruct((1, f), F32)],
        compiler_params=_params("arbitrary"),
    )(*((t, w.reshape(1, f)) + tuple(dys) + ((res,) if has_res else ())))


def _norm_res_norm(t, res, w1, w2, name="post_mix_pre_ffn_norm"):
    r_, f = t.shape
    tr = _row_tile(r_)

    def body(t_ref, res_ref, w1_ref, w2_ref, h_ref, v_ref):
        tv = t_ref[...]
        h = res_ref[...] + tv * lax.rsqrt(jnp.mean(tv * tv, axis=-1, keepdims=True) + EPS) * w1_ref[...]
        h_ref[...] = h
        v_ref[...] = (h * lax.rsqrt(jnp.mean(h * h, axis=-1, keepdims=True) + EPS) * w2_ref[...]).astype(v_ref.dtype)

    row = pl.BlockSpec((tr, f), lambda i: (i, 0))
    wsp = pl.BlockSpec((1, f), lambda i: (0, 0))
    return pl.pallas_call(
        body, name=name, grid=(r_ // tr,), in_specs=[row, row, wsp, wsp], out_specs=[row, row],
        out_shape=[jax.ShapeDtypeStruct((r_, f), F32), jax.ShapeDtypeStruct((r_, f), MXU_DTYPE)],
        compiler_params=_params("parallel"),
    )(t, res, w1.reshape(1, f), w2.reshape(1, f))


def _norm_res_norm_bwd(h, w2, dv, dres, t, w1, name="pre_ffn_post_mix_norm_bwd"):
    r_, f = h.shape
    tr = _row_tile(r_, streams=6)

    def body(h_ref, w2_ref, dv_ref, dres_ref, t_ref, w1_ref, dh_ref, dt_ref, dw2_ref, dw1_ref):
        @pl.when(pl.program_id(0) == 0)
        def _():
            dw2_ref[...] = jnp.zeros_like(dw2_ref)
            dw1_ref[...] = jnp.zeros_like(dw1_ref)

        def rms_bwd(tv, wv, dyv):
            r = lax.rsqrt(jnp.mean(tv * tv, axis=-1, keepdims=True) + EPS)
            gw = dyv * wv
            c = jnp.mean(gw * tv, axis=-1, keepdims=True)
            return r * gw - tv * (r * r * r * c), jnp.sum(dyv * tv * r, axis=0, keepdims=True)

        d1, g2 = rms_bwd(h_ref[...], w2_ref[...], dv_ref[...])
        dh = d1 + dres_ref[...]
        dh_ref[...] = dh
        dw2_ref[...] += g2
        d2, g1 = rms_bwd(t_ref[...], w1_ref[...], dh)
        dt_ref[...] = d2.astype(dt_ref.dtype)
        dw1_ref[...] += g1

    row = pl.BlockSpec((tr, f), lambda i: (i, 0))
    wsp = pl.BlockSpec((1, f), lambda i: (0, 0))
    return pl.pallas_call(
        body, name=name, grid=(r_ // tr,), in_specs=[row, wsp, row, row, row, wsp], out_specs=[row, row, wsp, wsp],
        out_shape=[jax.ShapeDtypeStruct((r_, f), F32), jax.ShapeDtypeStruct((r_, f), MXU_DTYPE),
                   jax.ShapeDtypeStruct((1, f), F32), jax.ShapeDtypeStruct((1, f), F32)],
        compiler_params=_params("arbitrary"),
    )(h, w2.reshape(1, f), dv, dres, t, w1.reshape(1, f))


def _hnorm_fwd(o, w, width, name="attn_out_norm"):
    h, s_, v = o.shape
    tr = _row_tile(s_)

    def body(o_ref, w_ref, y_ref):
        ss = jnp.sum(o_ref[0] * o_ref[0], axis=-1, keepdims=True)
        for i in range(1, h):
            ss = ss + jnp.sum(o_ref[i] * o_ref[i], axis=-1, keepdims=True)
        r = lax.rsqrt(ss * (1.0 / (h * v)) + EPS)
        for i in range(h):
            sl = slice(i * v, (i + 1) * v)
            y_ref[:, sl] = (o_ref[i] * r * w_ref[:, sl]).astype(y_ref.dtype)

    return pl.pallas_call(
        body, name=name, grid=(s_ // tr,),
        in_specs=[pl.BlockSpec((h, tr, v), lambda i: (0, i, 0)), pl.BlockSpec((1, h * v), lambda i: (0, 0))],
        out_specs=pl.BlockSpec((tr, h * v), lambda i: (i, 0)),
        out_shape=jax.ShapeDtypeStruct((s_, width), MXU_DTYPE), compiler_params=_params("parallel"),
    )(o, w)


def _hnorm_bwd(o, w, dy, name="attn_out_norm_bwd"):
    h, s_, v = o.shape
    tr = _row_tile(s_)

    def body(o_ref, w_ref, dy_ref, do_ref, delta_ref, dw_ref):
        @pl.when(pl.program_id(0) == 0)
        def _():
            dw_ref[...] = jnp.zeros_like(dw_ref)

        ss = jnp.zeros((tr, 1), F32)
        cc = jnp.zeros((tr, 1), F32)
        for i in range(h):
            sl = slice(i * v, (i + 1) * v)
            ov = o_ref[i]
            ss = ss + jnp.sum(ov * ov, axis=-1, keepdims=True)
            cc = cc + jnp.sum(dy_ref[:, sl] * w_ref[:, sl] * ov, axis=-1, keepdims=True)
        r = lax.rsqrt(ss * (1.0 / (h * v)) + EPS)
        c = cc * (1.0 / (h * v))
        for i in range(h):
            sl = slice(i * v, (i + 1) * v)
            ov = o_ref[i]
            dyv = dy_ref[:, sl]
            dov = r * dyv * w_ref[:, sl] - ov * (r * r * r * c)
            do_ref[i] = dov.astype(do_ref.dtype)
            delta_ref[i] = jnp.sum(dov * ov, axis=-1, keepdims=True)
            dw_ref[:, sl] += jnp.sum(dyv * ov * r, axis=0, keepdims=True)

    blk = pl.BlockSpec((h, tr, v), lambda i: (0, i, 0))
    wsp = pl.BlockSpec((1, h * v), lambda i: (0, 0))
    return pl.pallas_call(
        body, name=name, grid=(s_ // tr,),
        in_specs=[blk, wsp, pl.BlockSpec((tr, h * v), lambda i: (i, 0))],
        out_specs=[blk, pl.BlockSpec((h, tr, 1), lambda i: (0, i, 0)), wsp],
        out_shape=[jax.ShapeDtypeStruct(o.shape, MXU_DTYPE), jax.ShapeDtypeStruct((h, s_, 1), F32),
                   jax.ShapeDtypeStruct((1, h * v), F32)],
        compiler_params=_params("arbitrary"),
    )(o, w, dy)


def _loss_head(ffn, h1, target, w, name="loss_head"):
    r_, f = ffn.shape
    tr = _row_tile(r_)

    def body(ffn_ref, h1_ref, tg_ref, w_ref, loss_ref, dy_ref, dffn_ref, dw_ref):
        @pl.when(pl.program_id(0) == 0)
        def _():
            dw_ref[...] = jnp.zeros_like(dw_ref)
            loss_ref[...] = jnp.zeros_like(loss_ref)

        tv = ffn_ref[...]
        wv = w_ref[...]
        r = lax.rsqrt(jnp.mean(tv * tv, axis=-1, keepdims=True) + EPS)
        tn = tv * r
        e = h1_ref[...] + tn * wv - tg_ref[...]
        tot = jnp.sum(jnp.sum(e * e, axis=1, keepdims=True), axis=0, keepdims=True) * (0.5 / f)
        loss_ref[...] += tot + jnp.zeros_like(loss_ref)
        dyv = e * (1.0 / f)
        dy_ref[...] = dyv
        gw = dyv * wv
        c = jnp.mean(gw * tv, axis=-1, keepdims=True)
        dffn_ref[...] = (r * gw - tv * (r * r * r * c)).astype(dffn_ref.dtype)
        dw_ref[...] += jnp.sum(dyv * tn, axis=0, keepdims=True)

    row = pl.BlockSpec((tr, f), lambda i: (i, 0))
    wsp = pl.BlockSpec((1, f), lambda i: (0, 0))
    lsp = pl.BlockSpec((1, LANE), lambda i: (0, 0))
    return pl.pallas_call(
        body, name=name, grid=(r_ // tr,),
        in_specs=[row, row, row, wsp], out_specs=[lsp, row, row, wsp],
        out_shape=[jax.ShapeDtypeStruct((1, LANE), F32), jax.ShapeDtypeStruct((r_, f), F32),
                   jax.ShapeDtypeStruct((r_, f), MXU_DTYPE), jax.ShapeDtypeStruct((1, f), F32)],
        compiler_params=_params("arbitrary"),
    )(ffn, h1, target, w.reshape(1, f))


def _rot_matrix():
    p = np.zeros((ROPE, ROPE), np.float32)
    for i in range(HALF):
        p[i + HALF, i] = -1.0
        p[i, i + HALF] = 1.0
    return jnp.asarray(p, BF16)


def _rope_val(r, c2, s2, rot):
    hi, mid, _ = _split3(r)
    return r * c2 + (_dot(hi, rot, 1, 0) + _dot(mid, rot, 1, 0)) * s2


def _q_prep(q, cos2, sin2, scale, name):
    h, s_, _ = q.shape
    tr = _pick(s_, (4096, 2048, 1024, 512, 256, 128, 64, 32, 16))

    def body(q_ref, c_ref, s_ref, rot_ref, o_ref):
        for rs in _row_slices(tr, 16):
            x = q_ref[rs, :]
            o_ref[rs, :NOPE] = (x[:, :NOPE] * scale).astype(o_ref.dtype)
            o_ref[rs, NOPE:] = (_rope_val(x[:, NOPE:], c_ref[rs, :], s_ref[rs, :], rot_ref[...]) * scale).astype(o_ref.dtype)

    blk = pl.BlockSpec((None, tr, QK), lambda hh, i: (hh, i, 0))
    csp = pl.BlockSpec((tr, ROPE), lambda hh, i: (i, 0))
    return pl.pallas_call(
        body, name=name, grid=(h, s_ // tr),
        in_specs=[blk, csp, csp, pl.BlockSpec((ROPE, ROPE), lambda hh, i: (0, 0))], out_specs=blk,
        out_shape=jax.ShapeDtypeStruct(q.shape, MXU_DTYPE), compiler_params=_params("parallel", "parallel"),
    )(q, cos2, sin2, _rot_matrix())


def _q_up(qkvn, w_uq_t, cos2, sin2, scale, name="q_up"):
    s_ = qkvn.shape[0]
    h = w_uq_t.shape[0]
    tm = _pick(s_, (4096, 2048, 1024, 512, 256, 128))

    def body(a_ref, w_ref, c_ref, s_ref, rot_ref, o_ref):
        for rs in _row_slices(tm, 16):
            x = _dot(_mx(a_ref[rs, :]), _mx(w_ref[...]), 1, 1)
            o_ref[rs, :NOPE] = (x[:, :NOPE] * scale).astype(o_ref.dtype)
            o_ref[rs, NOPE:] = (_rope_val(x[:, NOPE:], c_ref[rs, :], s_ref[rs, :], rot_ref[...]) * scale).astype(o_ref.dtype)

    csp = pl.BlockSpec((tm, ROPE), lambda j, i: (i, 0))
    return pl.pallas_call(
        body, name=name, grid=(h, s_ // tm),
        in_specs=[pl.BlockSpec((tm, Q_RANK), lambda j, i: (i, 0)), pl.BlockSpec((None, QK, Q_RANK), lambda j, i: (j, 0, 0)),
                  csp, csp, pl.BlockSpec((ROPE, ROPE), lambda j, i: (0, 0))],
        out_specs=pl.BlockSpec((None, tm, QK), lambda j, i: (j, i, 0)),
        out_shape=jax.ShapeDtypeStruct((h, s_, QK), MXU_DTYPE), compiler_params=_params("parallel", "parallel"),
    )(qkvn, w_uq_t, cos2, sin2, _rot_matrix())


def _kv_up(qkvn, w_ukv, small, cos2, sin2, name="kv_up"):
    s_ = qkvn.shape[0]
    h = w_ukv.shape[0]
    tm = _pick(s_, (4096, 2048, 1024, 512, 256, 128))

    def body(a_ref, w_ref, sm_ref, c_ref, s_ref, rot_ref, k_ref, v_ref):
        for rs in _row_slices(tm, 16):
            x = _dot(_mx(a_ref[rs, :]), _mx(w_ref[...]), 1, 0)
            k_ref[rs, :NOPE] = x[:, :NOPE].astype(k_ref.dtype)
            k_ref[rs, NOPE:] = _rope_val(sm_ref[rs, :ROPE], c_ref[rs, :], s_ref[rs, :], rot_ref[...]).astype(k_ref.dtype)
            v_ref[rs, :] = x[:, NOPE:].astype(v_ref.dtype)

    csp = pl.BlockSpec((tm, ROPE), lambda j, i: (i, 0))
    return pl.pallas_call(
        body, name=name, grid=(h, s_ // tm),
        in_specs=[pl.BlockSpec((tm, KV_RANK), lambda j, i: (i, Q_RANK // KV_RANK)),
                  pl.BlockSpec((None, KV_RANK, NOPE + VDIM), lambda j, i: (j, 0, 0)),
                  pl.BlockSpec((tm, LANE), lambda j, i: (i, 0)), csp, csp, pl.BlockSpec((ROPE, ROPE), lambda j, i: (0, 0))],
        out_specs=[pl.BlockSpec((None, tm, QK), lambda j, i: (j, i, 0)), pl.BlockSpec((None, tm, VDIM), lambda j, i: (j, i, 0))],
        out_shape=[jax.ShapeDtypeStruct((h, s_, QK), MXU_DTYPE), jax.ShapeDtypeStruct((h, s_, VDIM), MXU_DTYPE)],
        compiler_params=_params("parallel", "parallel"),
    )(qkvn, w_ukv, small, cos2, sin2, _rot_matrix())


def _dkv_post(dk, dv, ddt, cos2, nsin2, name="dkv_post"):
    h, s_, _ = dk.shape
    tr = _row_tile(s_)

    def body(dk_ref, dv_ref, ddt_ref, c_ref, s_ref, rot_ref, dkv_ref, dsm_ref):
        acc = dk_ref[0, :, NOPE:]
        for i in range(1, h):
            acc = acc + dk_ref[i, :, NOPE:]
        dsm_ref[:, :ROPE] = _rope_val(acc, c_ref[...], s_ref[...], rot_ref[...]).astype(dsm_ref.dtype)
        dsm_ref[:, ROPE:] = ddt_ref[:, ROPE:].astype(dsm_ref.dtype)
        for i in range(h):
            dkv_ref[i, :, :NOPE] = dk_ref[i, :, :NOPE].astype(dkv_ref.dtype)
            dkv_ref[i, :, NOPE:] = dv_ref[i].astype(dkv_ref.dtype)

    csp = pl.BlockSpec((tr, ROPE), lambda i: (i, 0))
    return pl.pallas_call(
        body, name=name, grid=(s_ // tr,),
        in_specs=[pl.BlockSpec((h, tr, QK), lambda i: (0, i, 0)), pl.BlockSpec((h, tr, VDIM), lambda i: (0, i, 0)),
                  pl.BlockSpec((tr, LANE), lambda i: (i, 0)), csp, csp, pl.BlockSpec((ROPE, ROPE), lambda i: (0, 0))],
        out_specs=[pl.BlockSpec((h, tr, NOPE + VDIM), lambda i: (0, i, 0)), pl.BlockSpec((tr, LANE), lambda i: (i, 0))],
        out_shape=[jax.ShapeDtypeStruct((h, s_, NOPE + VDIM), MXU_DTYPE), jax.ShapeDtypeStruct((s_, LANE), MXU_DTYPE)],
        compiler_params=_params("parallel"),
    )(dk, dv, ddt, cos2, nsin2, _rot_matrix())


def _attn_tile(s):
    return 2048 if s % 4096 == 0 else s // 2


def _pairs(n, by_key):
    if by_key:
        pr = [(i, j) for j in range(n) for i in range(j, n)]
    else:
        pr = [(i, j) for i in range(n) for j in range(i + 1)]
    return (jnp.asarray([p[0] for p in pr], jnp.int32), jnp.asarray([p[1] for p in pr], jnp.int32))


ATTN_ROW_GROUPS = 8


def _row_groups(t, diag):
    tg = t // ATTN_ROW_GROUPS
    out = []
    for r in range(ATTN_ROW_GROUPS):
        nc = (r + 1) * tg if diag else t
        mask = None
        if diag:
            mask = (lax.broadcasted_iota(jnp.int32, (tg, nc), 1)
                    <= lax.broadcasted_iota(jnp.int32, (tg, nc), 0) + r * tg)
        out.append((slice(r * tg, (r + 1) * tg), nc, mask))
    return out


def _flash_specs(t, dk, dv):
    qsp = pl.BlockSpec((None, t, dk), lambda hh, p, qi, kj: (hh, qi[p], 0))
    ksp = pl.BlockSpec((None, t, dk), lambda hh, p, qi, kj: (hh, kj[p], 0))
    vsp = pl.BlockSpec((None, t, dv), lambda hh, p, qi, kj: (hh, kj[p], 0))
    osp = pl.BlockSpec((None, t, dv), lambda hh, p, qi, kj: (hh, qi[p], 0))
    lsp = pl.BlockSpec((None, t, 1), lambda hh, p, qi, kj: (hh, qi[p], 0))
    return qsp, ksp, vsp, osp, lsp


def _flash_fwd(q, k, v, name="flash_fwd"):
    h, s_, dk = q.shape
    dv = v.shape[-1]
    t = _attn_tile(s_)
    n = s_ // t
    qi, kj = _pairs(n, False)

    def body(qi_ref, kj_ref, q_ref, k_ref, v_ref, o_ref, lse_ref, m_s, l_s, acc):
        p_ = pl.program_id(1)
        i, j = qi_ref[p_], kj_ref[p_]

        @pl.when(j == 0)
        def _():
            m_s[...] = jnp.full_like(m_s, -jnp.inf)
            l_s[...] = jnp.zeros_like(l_s)
            acc[...] = jnp.zeros_like(acc)

        def update(diag):
            for rs, nc, mask in _row_groups(t, diag):
                sc = _dot(q_ref[rs, :], k_ref[0:nc, :], 1, 1)
                if mask is not None:
                    sc = jnp.where(mask, sc, -jnp.inf)
                m_old = m_s[rs, :]
                m_new = jnp.maximum(m_old, jnp.max(sc, axis=1, keepdims=True))
                alpha = jnp.exp(m_old - m_new)
                p = jnp.exp(sc - m_new)
                l_s[rs, :] = alpha * l_s[rs, :] + jnp.sum(p, axis=1, keepdims=True)
                acc[rs, :] = alpha * acc[rs, :] + _dot(_mx(p), v_ref[0:nc, :], 1, 0)
                m_s[rs, :] = m_new

        @pl.when(j < i)
        def _():
            update(False)

        @pl.when(j == i)
        def _():
            update(True)
            o_ref[...] = acc[...] / l_s[...]
            lse_ref[...] = m_s[...] + jnp.log(l_s[...])

    qsp, ksp, vsp, osp, lsp = _flash_specs(t, dk, dv)
    gs = pltpu.PrefetchScalarGridSpec(
        num_scalar_prefetch=2, grid=(h, qi.shape[0]), in_specs=[qsp, ksp, vsp], out_specs=[osp, lsp],
        scratch_shapes=[pltpu.VMEM((t, 1), F32), pltpu.VMEM((t, 1), F32), pltpu.VMEM((t, dv), F32)])
    return pl.pallas_call(
        body, name=name, grid_spec=gs,
        out_shape=[jax.ShapeDtypeStruct((h, s_, dv), F32), jax.ShapeDtypeStruct((h, s_, 1), F32)],
        compiler_params=_params("parallel", "arbitrary"),
    )(qi, kj, q, k, v)


def _flash_bwd(q, k, v, do, lse, delta, name="flash_bwd"):
    h, s_, dk = q.shape
    dv = v.shape[-1]
    t = _attn_tile(s_)
    tg = t // ATTN_ROW_GROUPS
    n = s_ // t
    qi, kj = _pairs(n, True)

    def body(qi_ref, kj_ref, q_ref, k_ref, v_ref, do_ref, lse_ref, delta_ref, dq_ref, dk_ref, dv_ref, dk_acc, dv_acc):
        p_ = pl.program_id(1)
        i, j = qi_ref[p_], kj_ref[p_]

        @pl.when(p_ == 0)
        def _():
            dq_ref[...] = jnp.zeros_like(dq_ref)

        def update(diag):
            for g, (rs, nc, mask) in enumerate(_row_groups(t, diag)):
                sc = _dot(q_ref[rs, :], k_ref[0:nc, :], 1, 1)
                if mask is not None:
                    sc = jnp.where(mask, sc, -jnp.inf)
                p = jnp.exp(sc - lse_ref[rs, :])
                dob = _mx(do_ref[rs, :])
                dv_acc[0:nc, :] += _dot(_mx(p), dob, 0, 0)
                dp = _dot(dob, v_ref[0:nc, :], 1, 1)
                dsb = _mx(p * (dp - delta_ref[rs, :]))
                dk_acc[0:nc, :] += _dot(dsb, q_ref[rs, :], 0, 0)
                rows = pl.ds(pl.multiple_of(i * t + g * tg, tg), tg)
                dq_ref[rows, :] += _dot(dsb, k_ref[0:nc, :], 1, 0)

        @pl.when(i == j)
        def _():
            dk_acc[...] = jnp.zeros_like(dk_acc)
            dv_acc[...] = jnp.zeros_like(dv_acc)
            update(True)

        @pl.when(i > j)
        def _():
            update(False)

        @pl.when(i == n - 1)
        def _():
            dk_ref[...] = dk_acc[...]
            dv_ref[...] = dv_acc[...]

    qsp, ksp, vsp, osp, lsp = _flash_specs(t, dk, dv)
    dqsp = pl.BlockSpec((None, s_, dk), lambda hh, p, qi, kj: (hh, 0, 0))
    gs = pltpu.PrefetchScalarGridSpec(
        num_scalar_prefetch=2, grid=(h, qi.shape[0]), in_specs=[qsp, ksp, vsp, osp, lsp, lsp],
        out_specs=[dqsp, ksp, vsp],
        scratch_shapes=[pltpu.VMEM((t, dk), F32), pltpu.VMEM((t, dv), F32)])
    return pl.pallas_call(
        body, name=name, grid_spec=gs,
        out_shape=[jax.ShapeDtypeStruct((h, s_, dk), F32), jax.ShapeDtypeStruct((h, s_, dk), F32),
                   jax.ShapeDtypeStruct((h, s_, dv), F32)],
        compiler_params=_params("parallel", "arbitrary"),
    )(qi, kj, q, k, v, do, lse, delta)


HALO = 8


def _conv_specs(s_, c, tr, after):
    main = pl.BlockSpec((tr, c), lambda i: (i, 0))
    per = tr // HALO
    if after:
        halo = pl.BlockSpec((HALO, c), lambda i: (jnp.minimum((i + 1) * per, s_ // HALO - 1), 0))
    else:
        halo = pl.BlockSpec((HALO, c), lambda i: (jnp.maximum(i * per - 1, 0), 0))
    return main, halo


def _fill_before(ext, t_ref, h_ref, tr):
    ext[0:HALO, :] = jnp.where(pl.program_id(0) > 0, h_ref[...], 0.0)
    ext[HALO:HALO + tr, :] = t_ref[...]


def _taps(ext, w_ref, tr):
    base = HALO - (CONV_K - 1)
    acc = ext[base:base + tr, :] * w_ref[0:1, :]
    for k in range(1, CONV_K):
        acc = acc + ext[base + k:base + k + tr, :] * w_ref[k:k + 1, :]
    return acc


def _conv_fwd(t, w, b, name="conv_fwd"):
    s_, c = t.shape
    tr = _row_tile(s_)

    def body(t_ref, h_ref, w_ref, b_ref, o_ref, ext):
        _fill_before(ext, t_ref, h_ref, tr)
        o_ref[...] = _silu(_taps(ext, w_ref, tr) + b_ref[...])

    main, halo = _conv_specs(s_, c, tr, False)
    return pl.pallas_call(
        body, name=name, grid=(s_ // tr,),
        in_specs=[main, halo, pl.BlockSpec((CONV_K, c), lambda i: (0, 0)), pl.BlockSpec((1, c), lambda i: (0, 0))],
        out_specs=main, out_shape=jax.ShapeDtypeStruct((s_, c), F32),
        scratch_shapes=[pltpu.VMEM((tr + HALO, c), F32)], compiler_params=_params("parallel"),
    )(t, t, w, b)


def _conv_bwd_pre(t, w, b, dact, name="conv_bwd_pre"):
    s_, c = t.shape
    tr = _row_tile(s_)

    def body(t_ref, h_ref, w_ref, b_ref, da_ref, dpre_ref, dwb_ref, ext):
        @pl.when(pl.program_id(0) == 0)
        def _():
            dwb_ref[...] = jnp.zeros_like(dwb_ref)

        _fill_before(ext, t_ref, h_ref, tr)
        dpre = da_ref[...] * _dsilu(_taps(ext, w_ref, tr) + b_ref[...])
        dpre_ref[...] = dpre
        base = HALO - (CONV_K - 1)
        for k in range(CONV_K):
            dwb_ref[k:k + 1, :] += jnp.sum(dpre * ext[base + k:base + k + tr, :], axis=0, keepdims=True)
        dwb_ref[CONV_K:CONV_K + 1, :] += jnp.sum(dpre, axis=0, keepdims=True)

    main, halo = _conv_specs(s_, c, tr, False)
    return pl.pallas_call(
        body, name=name, grid=(s_ // tr,),
        in_specs=[main, halo, pl.BlockSpec((CONV_K, c), lambda i: (0, 0)), pl.BlockSpec((1, c), lambda i: (0, 0)), main],
        out_specs=[main, pl.BlockSpec((8, c), lambda i: (0, 0))],
        out_shape=[jax.ShapeDtypeStruct((s_, c), F32), jax.ShapeDtypeStruct((8, c), F32)],
        scratch_shapes=[pltpu.VMEM((tr + HALO, c), F32)], compiler_params=_params("arbitrary"),
    )(t, t, w, b, dact)


def _conv_bwd_in(dpre, w, name="conv_bwd_in"):
    s_, c = dpre.shape
    tr = _row_tile(s_)
    nt = s_ // tr

    def body(d_ref, h_ref, w_ref, o_ref, ext):
        ext[0:tr, :] = d_ref[...]
        ext[tr:tr + HALO, :] = jnp.where(pl.program_id(0) < nt - 1, h_ref[...], 0.0)
        acc = ext[CONV_K - 1:CONV_K - 1 + tr, :] * w_ref[0:1, :]
        for k in range(1, CONV_K):
            acc = acc + ext[CONV_K - 1 - k:CONV_K - 1 - k + tr, :] * w_ref[k:k + 1, :]
        o_ref[...] = acc.astype(o_ref.dtype)

    main, halo = _conv_specs(s_, c, tr, True)
    return pl.pallas_call(
        body, name=name, grid=(nt,),
        in_specs=[main, halo, pl.BlockSpec((CONV_K, c), lambda i: (0, 0))],
        out_specs=main, out_shape=jax.ShapeDtypeStruct((s_, c), MXU_DTYPE),
        scratch_shapes=[pltpu.VMEM((tr + HALO, c), F32)], compiler_params=_params("parallel"),
    )(dpre, dpre, w)


def _ssd_chunk_common(dt_ref, dtt_ref, br_ref, bc_ref, ar_ref, ac_ref):
    li = lax.broadcasted_iota(jnp.int32, (CHUNK, CHUNK), 0)
    si = lax.broadcasted_iota(jnp.int32, (CHUNK, CHUNK), 1)
    lower = li >= si
    lower_b = lower.astype(BF16)
    upper_b = (li <= si).astype(BF16)
    zr = dt_ref[...] + br_ref[...]
    dtc = _softplus(zr)
    a_row = -jnp.exp(ar_ref[...])
    acum = _exact_dot(lower_b, dtc * a_row, 1, 0, False)
    dtt = _softplus(dtt_ref[...] + bc_ref[...])
    acum_t = _exact_dot(dtt * (-jnp.exp(ac_ref[...])), upper_b, 1, 0, True)
    return lower, upper_b, zr, dtc, a_row, acum, acum_t


def _head_terms(h, lower, dtc, acum, acum_t):
    lane = lax.broadcasted_iota(jnp.int32, (1, LANE), 1)
    sub = lax.broadcasted_iota(jnp.int32, (SSD_H, 1), 0)
    rowid = lax.broadcasted_iota(jnp.int32, (CHUNK, 1), 0)
    oh = (lane == HEAD_LANE + h).astype(F32)
    acol = jnp.sum(acum * oh, axis=1, keepdims=True)
    dcol = jnp.sum(dtc * oh, axis=1, keepdims=True)
    arow = jnp.sum(acum_t * (sub == h).astype(F32), axis=0, keepdims=True)
    alast = jnp.sum(jnp.where(rowid == CHUNK - 1, acol, 0.0), axis=0, keepdims=True)
    decay = jnp.exp(jnp.where(lower, acol - arow, -jnp.inf))
    return oh, acol, dcol, alast, decay


SSD_PAIRS = SSD_H // 2
PAIRS_PER_GROUP = SSD_E // 2


def _ps(q):
    return slice(q * LANE, (q + 1) * LANE)


def _gs(off, g):
    return slice(off + g * SSD_N, off + (g + 1) * SSD_N)


def _lanes(c0, c1):
    return jnp.where(lax.broadcasted_iota(jnp.int32, (1, LANE), 1) < SSD_P, c0, c1)


def _rows(c0, c1):
    return jnp.where(lax.broadcasted_iota(jnp.int32, (LANE, 1), 0) < SSD_P, c0, c1)


def _lane_halves(t):
    first = lax.broadcasted_iota(jnp.int32, (1, LANE), 1) < SSD_P
    return (jnp.sum(jnp.where(first, t, 0.0), axis=1, keepdims=True),
            jnp.sum(jnp.where(first, 0.0, t), axis=1, keepdims=True))


def _ssd_in_specs(rev):
    def ci(c):
        return c if rev is None else rev - c
    return [pl.BlockSpec((CHUNK, CONV_DIM), lambda c: (ci(c), 0)),
            pl.BlockSpec((CHUNK, LANE), lambda c: (ci(c), 0)),
            pl.BlockSpec((SSD_H, CHUNK), lambda c: (0, ci(c))),
            pl.BlockSpec((1, LANE), lambda c: (0, 0)), pl.BlockSpec((SSD_H, 1), lambda c: (0, 0)),
            pl.BlockSpec((1, LANE), lambda c: (0, 0)), pl.BlockSpec((SSD_H, 1), lambda c: (0, 0)),
            pl.BlockSpec((SSD_PAIRS, 1, LANE), lambda c: (0, 0, 0))]


def _ssd_fwd(xbc, small, dtt, bias_r, bias_c, alog_r, alog_c, dsk, name="ssd_fwd"):
    s_ = xbc.shape[0]
    nc = s_ // CHUNK

    def body(x_ref, dt_ref, dtt_ref, br_ref, bc_ref, ar_ref, ac_ref, dsk_ref, y_ref, prev_ref, state):
        @pl.when(pl.program_id(0) == 0)
        def _():
            state[...] = jnp.zeros_like(state)

        lower, _, _, dtc, _, acum, acum_t = _ssd_chunk_common(dt_ref, dtt_ref, br_ref, bc_ref, ar_ref, ac_ref)
        for g in range(SSD_G):
            bb = _mx(x_ref[:, _gs(B_OFF, g)])
            cb_ = _mx(x_ref[:, _gs(C_OFF, g)])
            cbm = _dot(cb_, bb, 1, 1)
            for e in range(PAIRS_PER_GROUP):
                q = g * PAIRS_PER_GROUP + e
                _, acol0, dcol0, alast0, decay0 = _head_terms(2 * q, lower, dtc, acum, acum_t)
                _, acol1, dcol1, alast1, decay1 = _head_terms(2 * q + 1, lower, dtc, acum, acum_t)
                x = x_ref[:, _ps(q)]
                xdt = x * _lanes(dcol0, dcol1)
                xb = _mx(xdt)
                yd = _lanes(_dot(_mx(cbm * decay0), xb, 1, 0), _dot(_mx(cbm * decay1), xb, 1, 0))
                prev = state[q]
                prev_ref[0, q] = prev
                yo = _dot(cb_, _mx(prev), 1, 1) * _lanes(jnp.exp(acol0), jnp.exp(acol1))
                ds = _lanes(jnp.exp(alast0 - acol0), jnp.exp(alast1 - acol1))
                st = _dot(_mx(xdt * ds), bb, 0, 0)
                state[q] = prev * _rows(jnp.exp(alast0), jnp.exp(alast1)) + st
                y_ref[:, _ps(q)] = yd + yo + x * dsk_ref[q]

    psp = pl.BlockSpec((1, SSD_PAIRS, LANE, SSD_N), lambda c: (c, 0, 0, 0))
    return pl.pallas_call(
        body, name=name, grid=(nc,),
        in_specs=_ssd_in_specs(None), out_specs=[pl.BlockSpec((CHUNK, SSD_W), lambda c: (c, 0)), psp],
        out_shape=[jax.ShapeDtypeStruct((s_, SSD_W), F32),
                   jax.ShapeDtypeStruct((nc, SSD_PAIRS, LANE, SSD_N), F32)],
        scratch_shapes=[pltpu.VMEM((SSD_PAIRS, LANE, SSD_N), F32)],
        compiler_params=_params("arbitrary"),
    )(xbc, small, dtt, bias_r, bias_c, alog_r, alog_c, dsk)


def _ssd_bwd(xbc, small, dtt, bias_r, bias_c, alog_r, alog_c, dsk, prev, dy, name="ssd_bwd"):
    s_ = xbc.shape[0]
    nc = s_ // CHUNK

    def body(x_ref, dt_ref, dtt_ref, br_ref, bc_ref, ar_ref, ac_ref, dsk_ref, prev_ref, dy_ref,
             dx_ref, ddt_ref, dpar_ref, dstate):
        @pl.when(pl.program_id(0) == 0)
        def _():
            dstate[...] = jnp.zeros_like(dstate)
            dpar_ref[...] = jnp.zeros_like(dpar_ref)

        lower, upper_b, zr, dtc, a_row, acum, acum_t = _ssd_chunk_common(
            dt_ref, dtt_ref, br_ref, bc_ref, ar_ref, ac_ref)
        strict = (lax.broadcasted_iota(jnp.int32, (CHUNK, CHUNK), 1)
                  < lax.broadcasted_iota(jnp.int32, (CHUNK, CHUNK), 0))
        strict_b = strict.astype(BF16)
        col2 = lax.broadcasted_iota(jnp.int32, (CHUNK, 2 * CHUNK), 1)
        strict2 = (jnp.where(col2 >= CHUNK, col2 - CHUNK, col2)
                   < lax.broadcasted_iota(jnp.int32, (CHUNK, 2 * CHUNK), 0))
        da_in = jnp.zeros((CHUNK, LANE), F32)
        r_off = jnp.zeros((CHUNK, LANE), F32)
        c_int = jnp.zeros((CHUNK, LANE), F32)
        c_row = jnp.zeros((1, LANE), F32)
        ddt = jnp.zeros((CHUNK, LANE), F32)
        dskip = jnp.zeros((1, LANE), F32)
        for g in range(SSD_G):
            bb = _mx(x_ref[:, _gs(B_OFF, g)])
            cb_ = _mx(x_ref[:, _gs(C_OFF, g)])
            cbm = _dot(cb_, bb, 1, 1)
            dcb = jnp.zeros((CHUNK, CHUNK), F32)
            dc_acc = jnp.zeros((CHUNK, SSD_N), F32)
            db_acc = jnp.zeros((CHUNK, SSD_N), F32)
            for e in range(PAIRS_PER_GROUP):
                q = g * PAIRS_PER_GROUP + e
                oh0, acol0, dcol0, alast0, decay0 = _head_terms(2 * q, lower, dtc, acum, acum_t)
                oh1, acol1, dcol1, alast1, decay1 = _head_terms(2 * q + 1, lower, dtc, acum, acum_t)
                x = x_ref[:, _ps(q)]
                dy = dy_ref[:, _ps(q)]
                dcol = _lanes(dcol0, dcol1)
                xdt = x * dcol
                xb = _mx(xdt)
                eacol = _lanes(jnp.exp(acol0), jnp.exp(acol1))
                ds = _lanes(jnp.exp(alast0 - acol0), jnp.exp(alast1 - acol1))
                ealast = _rows(jnp.exp(alast0), jnp.exp(alast1))
                dyb = _mx(dy)
                dyb0, dyb1 = _mx(_lanes(dy, 0.0)), _mx(_lanes(0.0, dy))
                dsh = dstate[q]
                dshb = _mx(dsh)
                prev = prev_ref[0, q]
                prevb = _mx(prev)
                dxdt_inter = ds * _dot(bb, dshb, 1, 1)
                dxdt = _lanes(_dot(_mx(cbm * decay0), dyb, 0, 0), _dot(_mx(cbm * decay1), dyb, 0, 0)) + dxdt_inter
                dwl0 = _dot(dyb0, xb, 1, 1) * decay0
                dwl1 = _dot(dyb1, xb, 1, 1) * decay1
                dcb = dcb + dwl0 + dwl1
                dyeb = _mx(dy * eacol)
                dc_acc = dc_acc + _dot(dyeb, prevb, 1, 0)
                db_acc = db_acc + _dot(_mx(xdt * ds), dshb, 1, 0)
                dstate[q] = _dot(dyeb, cb_, 0, 0) + ealast * dsh
                above = _exact_dot(upper_b, jnp.concatenate([dwl0 * cbm, dwl1 * cbm], axis=1), 1, 0, False)
                above = jnp.where(strict2, above, 0.0)
                da_in = (da_in + jnp.sum(above[:, :CHUNK], axis=1, keepdims=True) * oh0
                         + jnp.sum(above[:, CHUNK:], axis=1, keepdims=True) * oh1)
                y_off = _dot(cb_, prevb, 1, 1) * eacol
                r0, r1 = _lane_halves(dy * y_off)
                r_off = r_off + r0 * oh0 + r1 * oh1
                c0, c1 = _lane_halves(xdt * dxdt_inter)
                c_int = c_int + c0 * oh0 + c1 * oh1
                both = jnp.sum(dsh * prev, axis=1, keepdims=True) * ealast
                c_row = (c_row + jnp.sum(_rows(both, 0.0), axis=0, keepdims=True) * oh0
                         + jnp.sum(_rows(0.0, both), axis=0, keepdims=True) * oh1)
                t0, t1 = _lane_halves(dxdt * x)
                ddt = ddt + t0 * oh0 + t1 * oh1
                dx_ref[:, _ps(q)] = dxdt * dcol + dy * dsk_ref[q]
                k0, k1 = _lane_halves(dy * x)
                dskip = (dskip + jnp.sum(k0, axis=0, keepdims=True) * oh0 + jnp.sum(k1, axis=0, keepdims=True) * oh1)
            dcbb = _mx(dcb)
            dx_ref[:, _gs(C_OFF, g)] = dc_acc + _dot(dcbb, bb, 1, 0)
            dx_ref[:, _gs(B_OFF, g)] = db_acc + _dot(dcbb, cb_, 0, 0)
        da = (da_in + _exact_dot(upper_b, r_off, 1, 0, False) + _exact_dot(strict_b, c_int, 1, 0, False) + c_row)
        draw = (ddt + da * a_row) * _sigmoid(zr)
        ddt_ref[...] = draw
        dpar_ref[0:1, :] += jnp.sum(draw, axis=0, keepdims=True)
        dpar_ref[1:2, :] += jnp.sum(da * dtc, axis=0, keepdims=True) * a_row
        dpar_ref[2:3, :] += dskip

    rev = nc - 1
    psp = pl.BlockSpec((1, SSD_PAIRS, LANE, SSD_N), lambda c: (rev - c, 0, 0, 0))
    return pl.pallas_call(
        body, name=name, grid=(nc,),
        in_specs=_ssd_in_specs(rev) + [psp, pl.BlockSpec((CHUNK, SSD_W), lambda c: (rev - c, 0))],
        out_specs=[pl.BlockSpec((CHUNK, CONV_DIM), lambda c: (rev - c, 0)),
                   pl.BlockSpec((CHUNK, LANE), lambda c: (rev - c, 0)), pl.BlockSpec((8, LANE), lambda c: (0, 0))],
        out_shape=[jax.ShapeDtypeStruct((s_, CONV_DIM), F32), jax.ShapeDtypeStruct((s_, LANE), F32),
                   jax.ShapeDtypeStruct((8, LANE), F32)],
        scratch_shapes=[pltpu.VMEM((SSD_PAIRS, LANE, SSD_N), F32)],
        compiler_params=_params("arbitrary"),
    )(xbc, small, dtt, bias_r, bias_c, alog_r, alog_c, dsk, prev, dy)


GN = SSD_W // SSD_G


def _gated_norm_fwd(y, z, w, cat, name="gated_norm_fwd"):
    s_, f = y.shape
    tr = _row_tile(s_)

    def body(y_ref, z_ref, w_ref, cat_ref, o_ref):
        for g in range(SSD_G):
            sl = slice(g * GN, (g + 1) * GN)
            gg = y_ref[:, sl] * _silu(z_ref[:, sl])
            r = lax.rsqrt(jnp.mean(gg * gg, axis=-1, keepdims=True) + EPS)
            o_ref[:, sl] = (gg * r * w_ref[:, sl]).astype(o_ref.dtype)

    row = pl.BlockSpec((tr, f), lambda i: (i, 0))
    wsp = pl.BlockSpec((1, f), lambda i: (0, 0))
    return pl.pallas_call(
        body, name=name, grid=(s_ // tr,),
        in_specs=[row, row, wsp, pl.BlockSpec(memory_space=pl.ANY)], out_specs=pl.BlockSpec((tr, f), lambda i: (i, 1)),
        out_shape=jax.ShapeDtypeStruct(cat.shape, cat.dtype), input_output_aliases={3: 0},
        compiler_params=_params("parallel"),
    )(y, z, w.reshape(1, f), cat)


def _gated_norm_bwd(y, z, w, dout, name="gated_norm_bwd"):
    s_, f = y.shape
    tr = _row_tile(s_)

    def body(y_ref, z_ref, w_ref, do_ref, dy_ref, dz_ref, dw_ref):
        @pl.when(pl.program_id(0) == 0)
        def _():
            dw_ref[...] = jnp.zeros_like(dw_ref)

        for g in range(SSD_G):
            sl = slice(g * GN, (g + 1) * GN)
            yv = y_ref[:, sl]
            zv = z_ref[:, sl]
            dov = do_ref[:, sl].astype(F32)
            sz = _silu(zv)
            gg = yv * sz
            r = lax.rsqrt(jnp.mean(gg * gg, axis=-1, keepdims=True) + EPS)
            gw = dov * w_ref[:, sl]
            c = jnp.mean(gw * gg, axis=-1, keepdims=True)
            dgg = r * gw - gg * (r * r * r * c)
            dy_ref[:, sl] = dgg * sz
            dz_ref[:, sl] = (dgg * yv * _dsilu(zv)).astype(dz_ref.dtype)
            dw_ref[:, sl] += jnp.sum(dov * gg * r, axis=0, keepdims=True)

    row = pl.BlockSpec((tr, f), lambda i: (i, 0))
    wsp = pl.BlockSpec((1, f), lambda i: (0, 0))
    return pl.pallas_call(
        body, name=name, grid=(s_ // tr,),
        in_specs=[row, row, wsp, pl.BlockSpec((tr, f), lambda i: (i, 1))], out_specs=[row, row, wsp],
        out_shape=[jax.ShapeDtypeStruct((s_, f), F32), jax.ShapeDtypeStruct((s_, f), MXU_DTYPE),
                   jax.ShapeDtypeStruct((1, f), F32)],
        compiler_params=_params("arbitrary"),
    )(y, z, w.reshape(1, f), dout)


def _ffn_fwd(vv, w_gate, w_up, name="ffn_gate_up"):
    s_, d = vv.shape
    nb, f8, _ = w_gate.shape
    tm = _pick(s_, (1024, 512, 256, 128))

    def body(v_ref, wg_ref, wu_ref, g_ref, u_ref, a_ref):
        for rs in _row_slices(tm, 16):
            a = _mx(v_ref[rs, :])
            g = _dot(a, _mx(wg_ref[...]), 1, 1)
            u = _dot(a, _mx(wu_ref[...]), 1, 1)
            s = _sigmoid(g)
            gs = g * s
            g_ref[rs, :] = (u * (s * (1.0 + g * (1.0 - s)))).astype(g_ref.dtype)
            u_ref[rs, :] = gs.astype(u_ref.dtype)
            a_ref[rs, :] = (gs * u).astype(a_ref.dtype)

    wsp = pl.BlockSpec((None, f8, d), lambda j, i: (j, 0, 0))
    osp = pl.BlockSpec((None, tm, f8), lambda j, i: (j, i, 0))
    return pl.pallas_call(
        body, name=name, grid=(nb, s_ // tm),
        in_specs=[pl.BlockSpec((tm, d), lambda j, i: (i, 0)), wsp, wsp], out_specs=[osp] * 3,
        out_shape=[jax.ShapeDtypeStruct((nb, s_, f8), MXU_DTYPE)] * 3,
        compiler_params=_params("parallel", "parallel"),
    )(vv, w_gate, w_up)


def _ffn_bwd_act(dffn, w_down, gate, up, name="ffn_d_act"):
    s_, d = dffn.shape
    nb, f8, _ = w_down.shape
    tm = _pick(s_, (1024, 512, 256, 128))

    def body(d_ref, w_ref, g_ref, u_ref, dg_ref, du_ref):
        for rs in _row_slices(tm, 16):
            dact = _dot(_mx(d_ref[rs, :]), _mx(w_ref[...]), 1, 1)
            dg_ref[rs, :] = (dact * g_ref[rs, :].astype(F32)).astype(dg_ref.dtype)
            du_ref[rs, :] = (dact * u_ref[rs, :].astype(F32)).astype(du_ref.dtype)

    osp = pl.BlockSpec((None, tm, f8), lambda j, i: (j, i, 0))
    return pl.pallas_call(
        body, name=name, grid=(nb, s_ // tm),
        in_specs=[pl.BlockSpec((tm, d), lambda j, i: (i, 0)), pl.BlockSpec((None, f8, d), lambda j, i: (j, 0, 0)),
                  osp, osp],
        out_specs=[osp, osp], out_shape=[jax.ShapeDtypeStruct((nb, s_, f8), MXU_DTYPE)] * 2,
        compiler_params=_params("parallel", "parallel"),
    )(dffn, w_down, gate, up)


def _ffn_bwd_in(dgate, w_gate, dup, w_up, name="ffn_d_in"):
    nb, s_, f8 = dgate.shape
    d = w_gate.shape[2]
    tm = _pick(s_, (1024, 512, 256, 128))
    tn = _pick(d, (2048, 1024, 512, 256, 128))

    def body(dg_ref, wg_ref, du_ref, wu_ref, o_ref, acc):
        j = pl.program_id(2)

        @pl.when(j == 0)
        def _():
            acc[...] = jnp.zeros_like(acc)

        for rs in _row_slices(tm, 16):
            acc[rs, :] += (_dot(_mx(dg_ref[rs, :]), _mx(wg_ref[...]), 1, 0)
                           + _dot(_mx(du_ref[rs, :]), _mx(wu_ref[...]), 1, 0))

        @pl.when(j == nb - 1)
        def _():
            o_ref[...] = acc[...]

    asp = pl.BlockSpec((None, tm, f8), lambda i, n, j: (j, i, 0))
    wsp = pl.BlockSpec((None, f8, tn), lambda i, n, j: (j, 0, n))
    return pl.pallas_call(
        body, name=name, grid=(s_ // tm, d // tn, nb),
        in_specs=[asp, wsp, asp, wsp], out_specs=pl.BlockSpec((tm, tn), lambda i, n, j: (i, n)),
        out_shape=jax.ShapeDtypeStruct((s_, d), F32), scratch_shapes=[pltpu.VMEM((tm, tn), F32)],
        compiler_params=_params("parallel", "parallel", "arbitrary"),
    )(dgate, w_gate, dup, w_up)


def _adam_math(g, w, m, v):
    m2 = ADAM_B1 * m + (1.0 - ADAM_B1) * g
    v2 = ADAM_B2 * v + (1.0 - ADAM_B2) * (g * g)
    m_hat = m2 / (1.0 - ADAM_B1 ** ADAM_STEP)
    v_hat = v2 / (1.0 - ADAM_B2 ** ADAM_STEP)
    delta = -ADAM_LR * (m_hat / (jnp.sqrt(v_hat) + ADAM_EPS) + ADAM_WD * w)
    return delta, m2, v2


def _adamw(parts, own, me, w, m, v, name="adamw"):
    nd, r_, c = parts.shape
    tr = _pick(r_, (128, 64, 32, 16))
    tc = c
    if tr == r_ and r_ > 128:
        tc = _pick(c, (256, 128))

    def body(me_ref, p_ref, own_ref, w_ref, m_ref, v_ref, g_ref, d_ref, m2_ref, v2_ref):
        mine = me_ref[0]
        g = jnp.zeros((tr, tc), F32)
        for i in range(nd):
            g = g + jnp.where(mine == i, own_ref[...], p_ref[i]).astype(F32)
        delta, m2, v2 = _adam_math(g, w_ref[...], m_ref[...], v_ref[...])
        g_ref[...] = g
        d_ref[...] = delta
        m2_ref[...] = m2
        v2_ref[...] = v2

    row = pl.BlockSpec((tr, tc), lambda i, j, me_: (i, j))
    gs = pltpu.PrefetchScalarGridSpec(
        num_scalar_prefetch=1, grid=(r_ // tr, c // tc),
        in_specs=[pl.BlockSpec((nd, tr, tc), lambda i, j, me_: (0, i, j)),
                  pl.BlockSpec((None, tr, tc), lambda i, j, me_: (me_[0], i, j)), row, row, row],
        out_specs=[row] * 4)
    return pl.pallas_call(
        body, name=name, grid_spec=gs, out_shape=[jax.ShapeDtypeStruct((r_, c), F32)] * 4,
        compiler_params=_params("parallel", "parallel"),
    )(me, parts, own, w, m, v)


def _adamw_small(parts, w, m, v, name="adamw_small"):
    nd = parts.shape[0]

    def body(p_ref, w_ref, m_ref, v_ref, g_ref, d_ref, m2_ref, v2_ref):
        g = p_ref[0]
        for i in range(1, nd):
            g = g + p_ref[i]
        delta, m2, v2 = _adam_math(g, w_ref[...], m_ref[...], v_ref[...])
        g_ref[...] = g
        d_ref[...] = delta
        m2_ref[...] = m2
        v2_ref[...] = v2

    return pl.pallas_call(
        body, name=name, out_shape=[jax.ShapeDtypeStruct(w.shape, F32)] * 4,
        compiler_params=pltpu.CompilerParams(vmem_limit_bytes=VMEM_LIMIT_BYTES),
    )(parts, w, m, v)


_HBM = pl.BlockSpec(memory_space=pltpu.HBM)
_MESH = pl.DeviceIdType.MESH


def _all_gather(xs, name):
    na = len(xs)

    def body(*refs):
        x_refs, out_refs = refs[:na], refs[na:2 * na]
        send_sems, recv_sems, local_sems = refs[2 * na:]
        x, y, c = lax.axis_index("x"), lax.axis_index("y"), lax.axis_index("c")
        me, sibling = (x, y, c), (x, y, 1 - c)
        near = [(1 - x, y), (x, 1 - y)]
        chips = near + [(1 - x, 1 - y)]
        relay_from = (x + c * (1 - 2 * x), y + (1 - c) * (1 - 2 * y))
        relay_to = (x + (1 - c) * (1 - 2 * x), y + c * (1 - 2 * y))

        def slot(a, px, py, pc):
            return out_refs[a].at[4 * px + 2 * py + pc]

        def copy(a, k, block, to, src=None):
            return pltpu.make_async_remote_copy(
                src_ref=slot(a, *block) if src is None else src, dst_ref=slot(a, *block),
                send_sem=send_sems.at[a, k], recv_sem=recv_sems.at[a, k], device_id=to, device_id_type=_MESH)

        mine = [pltpu.make_async_copy(x_refs[a], slot(a, *me), local_sems.at[a]) for a in range(na)]
        started = []
        for a in range(na):
            mine[a].start()
            first = [copy(a, 0, me, sibling, src=x_refs[a])]
            first += [copy(a, 1 + j, me, (*chip, c), src=x_refs[a]) for j, chip in enumerate(near)]
            for cp in first:
                cp.start()
            started += first
        for a in range(na):
            for j, chip in enumerate(chips):
                copy(a, 1 + j, (*chip, c), me).wait_recv()
                fwd = copy(a, 4 + j, (*chip, c), sibling)
                fwd.start()
                started.append(fwd)
                if j == len(near) - 1:
                    relay = copy(a, 1 + len(near), (*relay_from, c), (*relay_to, c))
                    relay.start()
                    started.append(relay)
        for a in range(na):
            copy(a, 0, sibling, me).wait_recv()
            for j, chip in enumerate(chips):
                copy(a, 4 + j, (*chip, 1 - c), me).wait_recv()
        for cp in started:
            cp.wait_send()
        for cp in mine:
            cp.wait()

    return pl.pallas_call(
        body, name=name, out_shape=[jax.ShapeDtypeStruct((N_DEV,) + t.shape, t.dtype) for t in xs],
        in_specs=[_HBM] * na, out_specs=[_HBM] * na,
        scratch_shapes=[pltpu.SemaphoreType.DMA((na, 7)), pltpu.SemaphoreType.DMA((na, 7)),
                        pltpu.SemaphoreType.DMA((na,))],
    )(*xs)


_SEM = pl.BlockSpec(memory_space=pltpu.SEMAPHORE)
_EFFECT = pltpu.SideEffectType.DATAFLOW_SIDE_EFFECTING


def _peers(x, y, c):
    out = []
    for k in range(1, N_DEV):
        px = 1 - x if k & 4 else x
        py = 1 - y if k & 2 else y
        pc = 1 - c if k & 1 else c
        out.append(((px, py, pc), 4 * px + 2 * py + pc))
    return out


def _push_copies(scatter, src_refs, land_refs, send_sems, recv_sems):
    x, y, c = lax.axis_index("x"), lax.axis_index("y"), lax.axis_index("c")
    me = 4 * x + 2 * y + c
    pairs = []
    for a, (src, land) in enumerate(zip(src_refs, land_refs)):
        for k, (peer, slot) in enumerate(_peers(x, y, c)):
            out_src = src.at[slot] if scatter else src
            si = a * (N_DEV - 1) + k
            send = pltpu.make_async_remote_copy(src_ref=out_src, dst_ref=land.at[me], send_sem=send_sems.at[si],
                                                recv_sem=recv_sems.at[si], device_id=peer, device_id_type=_MESH)
            recv = pltpu.make_async_remote_copy(src_ref=out_src, dst_ref=land.at[slot], send_sem=send_sems.at[si],
                                                recv_sem=recv_sems.at[si], device_id=peer, device_id_type=_MESH)
            pairs.append((send, recv))
    return pairs


def _push_start(srcs, scatter, dep, name):
    na = len(srcs)
    shapes = [t.shape[1:] if scatter else t.shape for t in srcs]
    lands = [pltpu.with_memory_space_constraint(lax.empty((N_DEV,) + s, t.dtype), pltpu.HBM) for s, t in zip(shapes, srcs)]

    def body(*refs):
        src_refs, land_refs = refs[:na], refs[na:2 * na]
        send_sems, recv_sems = refs[2 * na + 1], refs[2 * na + 2]
        token = refs[-1]
        for send, _ in _push_copies(scatter, src_refs, land_refs, send_sems, recv_sems):
            send.start()
        token[...] = jnp.zeros_like(token)

    sem = pltpu.SemaphoreType.DMA((na * (N_DEV - 1),))
    outs = pl.pallas_call(
        body, name=name,
        out_shape=(sem, sem) + tuple(pltpu.HBM(t.shape, t.dtype) for t in srcs)
        + tuple(pltpu.HBM(t.shape, t.dtype) for t in lands) + (jax.ShapeDtypeStruct((8, LANE), F32),),
        in_specs=[_HBM] * (2 * na) + [pl.BlockSpec(memory_space=pl.ANY)],
        out_specs=(_SEM, _SEM) + (_HBM,) * (2 * na) + (pl.BlockSpec(memory_space=pltpu.VMEM),),
        input_output_aliases={i: 2 + i for i in range(2 * na)},
        compiler_params=pltpu.CompilerParams(has_side_effects=_EFFECT),
    )(*[pltpu.with_memory_space_constraint(t, pltpu.HBM) for t in srcs], *lands, dep)
    return outs[0], outs[1], outs[2:2 + na], outs[2 + na:2 + 2 * na], outs[-1]


def _push_wait(send_sems, recv_sems, src_thru, land_thru, scatter, after, name):
    na = len(src_thru)

    def body(*refs):
        src_refs, land_refs = refs[:na], refs[na:2 * na]
        ssem, rsem = refs[2 * na], refs[2 * na + 1]
        for send, recv in _push_copies(scatter, src_refs, land_refs, ssem, rsem):
            send.wait_send()
            recv.wait_recv()

    outs = pl.pallas_call(
        body, name=name,
        out_shape=tuple(pltpu.HBM(t.shape, t.dtype) for t in src_thru) + tuple(pltpu.HBM(t.shape, t.dtype) for t in land_thru),
        in_specs=[_HBM] * (2 * na) + [_SEM, _SEM, pl.BlockSpec(memory_space=pl.ANY)],
        out_specs=(_HBM,) * (2 * na),
        input_output_aliases={i: i for i in range(2 * na)},
        compiler_params=pltpu.CompilerParams(has_side_effects=_EFFECT),
    )(*src_thru, *land_thru, send_sems, recv_sems, after)
    return outs[:na], outs[na:]


def _exchange_behind(srcs, scatter, dep, name):
    send_sems, recv_sems, thru, lands, token = _push_start(srcs, scatter, dep, name + "_start")

    def finish(after, place=True):
        src_done, land_done = _push_wait(send_sems, recv_sems, thru, lands, scatter, after, name + "_wait")
        if not place:
            return land_done, src_done
        return _place_own(land_done, src_done, scatter, name + "_own")

    return token[0, 0], finish


def _place_own(lands, srcs, scatter, name):
    me = (4 * lax.axis_index("x") + 2 * lax.axis_index("y") + lax.axis_index("c")).astype(jnp.int32).reshape(1)
    outs = []
    for a, (land, src) in enumerate(zip(lands, srcs)):
        r_, c_ = land.shape[1:]
        tr = _pick(r_, (512, 256, 128, 64, 32, 16))

        def body(me_ref, land_ref, src_ref, out_ref):
            out_ref[...] = src_ref[...]

        src_spec = (pl.BlockSpec((None, tr, c_), lambda i, me_: (me_[0], i, 0)) if scatter
                    else pl.BlockSpec((tr, c_), lambda i, me_: (i, 0)))
        gs = pltpu.PrefetchScalarGridSpec(
            num_scalar_prefetch=1, grid=(r_ // tr,),
            in_specs=[pl.BlockSpec(memory_space=pl.ANY), src_spec],
            out_specs=pl.BlockSpec((None, tr, c_), lambda i, me_: (me_[0], i, 0)))
        outs.append(pl.pallas_call(
            body, name=f"{name}_{a}", grid_spec=gs, out_shape=jax.ShapeDtypeStruct(land.shape, land.dtype),
            input_output_aliases={1: 0}, compiler_params=_params("arbitrary"),
        )(me, land, src))
    return outs


_TRANSPOSED = ("w_in", "w_uq", "w_gate", "w_up")
_CQKV = (0, Q_RANK + KV_RANK)
_KR = (_CQKV[1], _CQKV[1] + ROPE)
_Z = (_KR[1], _KR[1] + SSD_W)
_XBC = (_Z[1], _Z[1] + CONV_DIM)
_DT = (_XBC[1], _XBC[1] + SSD_H)


def _win_segments(w_in_t):
    w = w_in_t.reshape(D_IN, D_MODEL)
    small = jnp.concatenate([w[_KR[0]:_KR[1]], w[_DT[0]:_DT[1]],
                             jnp.zeros((LANE - ROPE - SSD_H, D_MODEL), w.dtype)], axis=0)
    return w[_CQKV[0]:_CQKV[1]], w[_Z[0]:_Z[1]], w[_XBC[0]:_XBC[1]], small


def _win_from_segments(g_cqkv, g_z, g_xbc, g_small):
    w = jnp.concatenate([g_cqkv, g_small[:ROPE], g_z, g_xbc, g_small[ROPE:ROPE + SSD_H]], axis=0)
    return w.reshape(N_DEV, D_IN // N_DEV, D_MODEL)


_SMALL = (("q_norm_w", 512), ("kv_norm_w", 512), ("conv_b", CONV_DIM), ("dt_bias", SSD_H), ("a_log", SSD_H),
          ("d_skip", SSD_H), ("ssd_norm_w", SSD_W), ("attn_out_norm_w", 1024), ("pre_mix_norm_w", D_MODEL),
          ("post_mix_norm_w", D_MODEL), ("pre_ffn_norm_w", D_MODEL), ("post_ffn_norm_w", D_MODEL),
          ("conv_w", CONV_K * CONV_DIM))
_SMALL_ROWS = -(-sum(-(-n // LANE) for _, n in _SMALL) // 8) * 8


def _pack_small(vals):
    rows = []
    for name, n in _SMALL:
        v = vals[name].reshape(-1).astype(F32)
        pad = -(-n // LANE) * LANE
        rows.append(jnp.pad(v, (0, pad - n)).reshape(-1, LANE))
    m = jnp.concatenate(rows, axis=0)
    return jnp.pad(m, ((0, _SMALL_ROWS - m.shape[0]), (0, 0)))


def _unpack_small(m):
    out, r = {}, 0
    for name, n in _SMALL:
        nr = -(-n // LANE)
        out[name] = m[r:r + nr].reshape(-1)[:n]
        r += nr
    return out


def _head_row(v):
    return jnp.pad(v.reshape(1, -1).astype(F32), ((0, 0), (HEAD_LANE, LANE - HEAD_LANE - v.shape[-1])))


def _local_step(x, positions, target, wg, small, weights, on_grads):
    w_cqkv, w_z, w_xbc, w_small = _win_segments(wg["w_in"])
    conv_w = wg["conv_w"]
    conv_b = small["conv_b"].reshape(1, CONV_DIM)
    qkv_norm_w = jnp.concatenate([small["q_norm_w"], small["kv_norm_w"]])
    attn_norm_w = small["attn_out_norm_w"].reshape(1, HEADS * VDIM)
    scale = QK ** -0.5

    inv_freq = ROPE_THETA ** (-jnp.arange(0, ROPE, 2, dtype=F32) / ROPE)
    ang = positions.astype(F32)[:, None] * inv_freq
    cos2 = jnp.tile(jnp.cos(ang), (1, 2))
    sin2 = jnp.tile(jnp.sin(ang), (1, 2))

    u = _rms_fwd(x, small["pre_mix_norm_w"], out_dtype=MXU_DTYPE, name="pre_mix_norm")
    cqkv = _mm(u, w_cqkv, "nt", name="in_proj_qkv")
    z = _mm(u, w_z, "nt", name="in_proj_z")
    xbc = _mm(u, w_xbc, "nt", name="in_proj_xbc")
    sm = _mm(u, w_small, "nt", name="in_proj_small")

    w_uq, w_ukv = weights("qkv_up", cqkv)
    qkvn = _rms_fwd(cqkv, qkv_norm_w, groups=2, out_dtype=MXU_DTYPE, name="qkv_norm")
    q_h = _q_up(qkvn, w_uq, cos2, sin2, scale)
    k_h, v_h = _kv_up(qkvn, w_ukv, sm, cos2, sin2)
    o_h, lse = _flash_fwd(q_h, k_h, v_h)
    cat = _hnorm_fwd(o_h, attn_norm_w, D_MODEL)
    w_out = weights("out", o_h)[0].reshape(D_MODEL, D_MODEL)

    xbc_act = _conv_fwd(xbc, conv_w, conv_b)
    dtt = jnp.transpose(sm[:, HEAD_LANE:HEAD_LANE + SSD_H])
    ssd_args = (xbc_act, sm, dtt, _head_row(small["dt_bias"]), small["dt_bias"].reshape(SSD_H, 1),
                _head_row(small["a_log"]), small["a_log"].reshape(SSD_H, 1),
                jnp.broadcast_to(small["d_skip"].reshape(SSD_H, 1), (SSD_H, SSD_P)).reshape(SSD_PAIRS, 1, LANE))
    y_ssd, prev = _ssd_fwd(*ssd_args)
    cat = _gated_norm_fwd(y_ssd, z, small["ssd_norm_w"], cat)

    mix = _mm(cat, w_out, "nn", name="out_proj")
    h1, vv = _norm_res_norm(mix, x, small["post_mix_norm_w"], small["pre_ffn_norm_w"])

    w_gate, w_up = weights("ffn_in", mix)
    gate, up, act = _ffn_fwd(vv, w_gate, w_up)
    w_down, = weights("ffn_out", act)
    ffn = _mm(act, w_down, "nn", a_blk=True, b_blk=True, fuse=2, wide=True, name="ffn_down")
    loss_blk, dy, dffn, g_post_ffn = _loss_head(ffn, h1, target, small["post_ffn_norm_w"])

    g_down = _mm(act, dffn, "tn", a_blk=True, out_blk=True, out_dtype=MXU_DTYPE, name="g_down")
    dgate, dup = _ffn_bwd_act(dffn, w_down, gate, up)
    dvv = _ffn_bwd_in(dgate, w_gate, dup, w_up)
    g_gate = _mm(dgate, vv, "tn", a_blk=True, out_blk=True, out_dtype=MXU_DTYPE, name="g_gate")
    g_up = _mm(dup, vv, "tn", a_blk=True, out_blk=True, out_dtype=MXU_DTYPE, name="g_up")
    pre_ffn_w = small["pre_ffn_norm_w"] + on_grads("ffn", [g_gate, g_up, g_down])
    dh1, dmix, g_pre_ffn, g_post_mix = _norm_res_norm_bwd(h1, pre_ffn_w, dvv, dy, mix, small["post_mix_norm_w"])

    dcat = _mm(dmix, w_out, "nt", name="d_cat")
    g_out = _mm(cat, dmix, "tn", out_dtype=MXU_DTYPE, name="g_out")

    do_h, delta, g_attn_norm = _hnorm_bwd(o_h, attn_norm_w, dcat)
    dq_h, dk_h, dv_h = _flash_bwd(q_h, k_h, v_h, do_h, lse, delta)
    dq = _q_prep(dq_h, cos2, -sin2, scale, name="dq_post")

    dy_ssd, dz, g_ssd_norm = _gated_norm_bwd(y_ssd, z, small["ssd_norm_w"], dcat)
    dxbc_act, ddt, dpar = _ssd_bwd(*ssd_args, prev, dy_ssd)
    dkv, dsm = _dkv_post(dk_h, dv_h, ddt, cos2, -sin2)
    dpre, dwb = _conv_bwd_pre(xbc, conv_w, conv_b, dxbc_act)
    dxbc = _conv_bwd_in(dpre, conv_w)

    dqn = _mm(dq, w_uq, "nn", a_blk=True, b_blk=True, fuse=HEADS, name="d_qn")
    dkvn = _mm(dkv, w_ukv, "nt", a_blk=True, b_blk=True, fuse=HEADS, name="d_kvn")
    g_uq = _mm(dq, qkvn, "tn", a_blk=True, out_blk=True, b_cols=(0, Q_RANK), out_dtype=MXU_DTYPE, name="g_uq")
    g_ukv = _mm(qkvn, dkv, "tn", b_blk=True, out_blk=True, a_cols=(Q_RANK, KV_RANK), out_dtype=MXU_DTYPE, name="g_ukv")
    heads_token = on_grads("heads", [g_uq, g_ukv, g_out.reshape(N_DEV, D_MODEL // N_DEV, D_MODEL)])
    dcqkv, g_qkv_norm = _rms_bwd(cqkv, qkv_norm_w + heads_token, [dqn, dkvn], out_dtype=MXU_DTYPE, name="qkv_norm_bwd")

    g_in = _win_from_segments(_mm(dcqkv, u, "tn", out_dtype=MXU_DTYPE, name="g_in_qkv"),
                              _mm(dz, u, "tn", out_dtype=MXU_DTYPE, name="g_in_z"),
                              _mm(dxbc, u, "tn", out_dtype=MXU_DTYPE, name="g_in_xbc"),
                              _mm(dsm, u, "tn", out_dtype=MXU_DTYPE, name="g_in_small"))
    in_token = on_grads("in", [g_in])
    du = _mm_sum([dsm + in_token.astype(dsm.dtype), dcqkv, dz, dxbc], [w_small, w_cqkv, w_z, w_xbc], name="d_u")
    dx, g_pre_mix = _rms_bwd(x, small["pre_mix_norm_w"], [du], res=dh1, name="pre_mix_norm_bwd")

    hl = slice(HEAD_LANE, HEAD_LANE + SSD_H)
    g_small = {"q_norm_w": g_qkv_norm[0, :Q_RANK], "kv_norm_w": g_qkv_norm[0, Q_RANK:], "conv_b": dwb[CONV_K],
               "dt_bias": dpar[0, hl], "a_log": dpar[1, hl], "d_skip": dpar[2, hl], "ssd_norm_w": g_ssd_norm,
               "attn_out_norm_w": g_attn_norm, "pre_mix_norm_w": g_pre_mix, "post_mix_norm_w": g_post_mix,
               "pre_ffn_norm_w": g_pre_ffn, "post_ffn_norm_w": g_post_ffn, "conv_w": dwb[:CONV_K]}
    return loss_blk[0, 0], dx, g_small


_WEIGHT_ORDER = ("w_in", "q_norm_w", "w_uq", "kv_norm_w", "w_ukv", "conv_w", "conv_b", "dt_bias", "a_log", "d_skip",
                 "ssd_norm_w", "attn_out_norm_w", "w_out", "pre_mix_norm_w", "post_mix_norm_w", "pre_ffn_norm_w",
                 "post_ffn_norm_w", "w_gate", "w_up", "w_down")


def kernel(x, positions, w_in, q_norm_w, w_uq, kv_norm_w, w_ukv, conv_w, conv_b, dt_bias, a_log, d_skip, ssd_norm_w, attn_out_norm_w, w_out, pre_mix_norm_w, post_mix_norm_w, pre_ffn_norm_w, post_ffn_norm_w, w_gate, w_up, w_down, loss_target, m_w_in, m_q_norm_w, m_w_uq, m_kv_norm_w, m_w_ukv, m_conv_w, m_conv_b, m_dt_bias, m_a_log, m_d_skip, m_ssd_norm_w, m_attn_out_norm_w, m_w_out, m_pre_mix_norm_w, m_post_mix_norm_w, m_pre_ffn_norm_w, m_post_ffn_norm_w, m_w_gate, m_w_up, m_w_down, v_w_in, v_q_norm_w, v_w_uq, v_kv_norm_w, v_w_ukv, v_conv_w, v_conv_b, v_dt_bias, v_a_log, v_d_skip, v_ssd_norm_w, v_attn_out_norm_w, v_w_out, v_pre_mix_norm_w, v_post_mix_norm_w, v_pre_ffn_norm_w, v_post_ffn_norm_w, v_w_gate, v_w_up, v_w_down):
    w = dict(w_in=w_in, q_norm_w=q_norm_w, w_uq=w_uq, kv_norm_w=kv_norm_w, w_ukv=w_ukv, conv_w=conv_w, conv_b=conv_b,
             dt_bias=dt_bias, a_log=a_log, d_skip=d_skip, ssd_norm_w=ssd_norm_w, attn_out_norm_w=attn_out_norm_w,
             w_out=w_out, pre_mix_norm_w=pre_mix_norm_w, post_mix_norm_w=post_mix_norm_w,
             pre_ffn_norm_w=pre_ffn_norm_w, post_ffn_norm_w=post_ffn_norm_w, w_gate=w_gate, w_up=w_up, w_down=w_down)
    m = dict(w_in=m_w_in, q_norm_w=m_q_norm_w, w_uq=m_w_uq, kv_norm_w=m_kv_norm_w, w_ukv=m_w_ukv, conv_w=m_conv_w,
             conv_b=m_conv_b, dt_bias=m_dt_bias, a_log=m_a_log, d_skip=m_d_skip, ssd_norm_w=m_ssd_norm_w,
             attn_out_norm_w=m_attn_out_norm_w, w_out=m_w_out, pre_mix_norm_w=m_pre_mix_norm_w,
             post_mix_norm_w=m_post_mix_norm_w, pre_ffn_norm_w=m_pre_ffn_norm_w, post_ffn_norm_w=m_post_ffn_norm_w,
             w_gate=m_w_gate, w_up=m_w_up, w_down=m_w_down)
    v = dict(w_in=v_w_in, q_norm_w=v_q_norm_w, w_uq=v_w_uq, kv_norm_w=v_kv_norm_w, w_ukv=v_w_ukv, conv_w=v_conv_w,
             conv_b=v_conv_b, dt_bias=v_dt_bias, a_log=v_a_log, d_skip=v_d_skip, ssd_norm_w=v_ssd_norm_w,
             attn_out_norm_w=v_attn_out_norm_w, w_out=v_w_out, pre_mix_norm_w=v_pre_mix_norm_w,
             post_mix_norm_w=v_post_mix_norm_w, pre_ffn_norm_w=v_pre_ffn_norm_w, post_ffn_norm_w=v_post_ffn_norm_w,
             w_gate=v_w_gate, w_up=v_w_up, w_down=v_w_down)
    w, m, v = ({k: t[0] for k, t in d.items()} for d in (w, m, v))
    me = 4 * lax.axis_index("x") + 2 * lax.axis_index("y") + lax.axis_index("c")
    groups = {"qkv_up": ("w_uq", "w_ukv"), "out": ("w_out",), "ffn_in": ("w_gate", "w_up"), "ffn_out": ("w_down",)}
    cshard = CONV_DIM // N_DEV
    for name in _TRANSPOSED:
        w[name], m[name], v[name] = w[name].T, m[name].T, v[name].T

    shards = [w["w_in"].astype(MXU_DTYPE),
              jnp.stack(_split3(w["conv_w"])).reshape(3 * CONV_K, cshard).astype(MXU_DTYPE)]
    w_in_g, cw = _all_gather(shards, name="gather_weights")
    cw = cw.astype(F32).reshape(N_DEV, 3, CONV_K, cshard)
    wg = {"w_in": w_in_g, "conv_w": jnp.transpose(cw[:, 0] + cw[:, 1] + cw[:, 2], (1, 0, 2)).reshape(CONV_K, CONV_DIM)}
    arriving, dep, started = {}, wg["conv_w"], jnp.zeros((), F32)
    small = {name: w[name] for name, _ in _SMALL if name != "conv_w"}
    for group in ("qkv_up", "out", "ffn_in", "ffn_out"):
        token, arriving[group] = _exchange_behind([w[name].astype(MXU_DTYPE) for name in groups[group]], False,
                                                  dep, group + "_weights")
        started = started + token
        dep = jnp.zeros((8, LANE), F32) + started
    small["pre_mix_norm_w"] = small["pre_mix_norm_w"] + started

    leaving = {}

    def on_grads(group, gs):
        token, leaving[group] = _exchange_behind(gs, True, jnp.zeros((8, LANE), F32), group + "_grads")
        return token

    loss_local, dx, g_small = _local_step(x[0], positions[0], loss_target[0], wg, small,
                                          lambda group, after: arriving[group](after), on_grads)
    loss = lax.psum(loss_local, ("x", "y", "c"))

    recv = {}
    for group, names in (("ffn", ("w_gate", "w_up", "w_down")), ("heads", ("w_uq", "w_ukv", "w_out")), ("in", ("w_in",))):
        recv.update(zip(names, zip(*leaving[group](dx, place=False))))
    grads, deltas, new_m, new_v = {}, {}, {}, {}
    me1 = me.astype(jnp.int32).reshape(1)
    for name, (parts, own) in recv.items():
        outs = _adamw(parts, own, me1, w[name], m[name], v[name], name="adamw_" + name)
        if name in _TRANSPOSED:
            outs = [t.T for t in outs]
        grads[name], deltas[name], new_m[name], new_v[name] = outs

    def embed(t):
        return lax.dynamic_update_slice(jnp.zeros((CONV_K, CONV_DIM), F32), t, (0, me * cshard))

    parts_s = _all_gather([_pack_small(g_small)], name="gather_small_grads")[0]
    packs = [_pack_small({**{n_: d[n_] for n_, _ in _SMALL if n_ != "conv_w"}, "conv_w": embed(d["conv_w"])})
             for d in (w, m, v)]
    outs = [_unpack_small(t) for t in _adamw_small(parts_s, *packs)]
    for name, n in _SMALL:
        for dst, src in zip((grads, deltas, new_m, new_v), outs):
            if name == "conv_w":
                dst[name] = lax.dynamic_slice(src[name].reshape(CONV_K, CONV_DIM), (0, me * cshard), (CONV_K, cshard))
            else:
                dst[name] = src[name]

    def lead(d):
        return [d[name][None] for name in _WEIGHT_ORDER]

    return (loss, dx[None], *lead(grads), *lead(deltas), *lead(new_m), *lead(new_v))
```

```python
import numpy as np

import jax
import jax.numpy as jnp
from jax import lax
from jax.experimental import pallas as pl
from jax.experimental.pallas import tpu as pltpu

F32 = jnp.float32
BF16 = jnp.bfloat16
MXU_DTYPE = jnp.bfloat16
EPS = 1e-6
VMEM_LIMIT_BYTES = 48 * 1024 * 1024
K_TILE_MAX = 2048

N_DEV = 8
D_MODEL = 2048
Q_RANK = 512
KV_RANK = 512
ROPE = 64
HALF = ROPE // 2
HEADS = 8
NOPE = 128
VDIM = 128
QK = NOPE + ROPE
SSD_W = 1024
SSD_H = 16
SSD_P = 64
SSD_G = 2
SSD_E = SSD_H // SSD_G
SSD_N = 128
CHUNK = 128
CONV_K = 4
CONV_DIM = SSD_W + 2 * SSD_G * SSD_N
B_OFF = SSD_W
C_OFF = SSD_W + SSD_G * SSD_N
D_FF = 5632
D_IN = Q_RANK + KV_RANK + ROPE + SSD_W + CONV_DIM + SSD_H
ROPE_THETA = 10000.0
LANE = 128
HEAD_LANE = ROPE

ADAM_LR = 0.001
ADAM_B1 = 0.9
ADAM_B2 = 0.999
ADAM_EPS = 1e-08
ADAM_WD = 0.01
ADAM_STEP = 10


def _pick(n, cands):
    for c in cands:
        if n % c == 0:
            return c
    return n


def _params(*sem):
    return pltpu.CompilerParams(dimension_semantics=sem, vmem_limit_bytes=VMEM_LIMIT_BYTES)


def _sigmoid(x):
    return 1.0 / (1.0 + jnp.exp(-x))


def _silu(x):
    return x * _sigmoid(x)


def _dsilu(x):
    s = _sigmoid(x)
    return s * (1.0 + x * (1.0 - s))


def _softplus(x):
    e = jnp.exp(-jnp.abs(x))
    small = e * (1.0 - e * (0.5 - e * (1.0 / 3.0)))
    return jnp.maximum(x, 0.0) + jnp.where(e < 0.01, small, jnp.log(1.0 + e))


def _dot(a, b, ca, cb):
    return lax.dot_general(a, b, (((ca,), (cb,)), ((), ())), preferred_element_type=F32)


def _mx(v):
    return v.astype(MXU_DTYPE)


def _split3(a):
    hi = a.astype(BF16)
    r1 = a - hi.astype(F32)
    mid = r1.astype(BF16)
    lo = (r1 - mid.astype(F32)).astype(BF16)
    return hi, mid, lo


def _exact_dot(a, b, ca, cb, split_a):
    if split_a:
        return sum(_dot(p, b, ca, cb) for p in _split3(a))
    return sum(_dot(a, p, ca, cb) for p in _split3(b))


MM_ROW_GROUPS = 4


def _row_slices(tm, align):
    ng = MM_ROW_GROUPS
    while ng > 1 and (tm % ng or (tm // ng) % align):
        ng //= 2
    return [slice(g * (tm // ng), (g + 1) * (tm // ng)) for g in range(ng)]


def _mm(a, b, mode, *, a_blk=False, b_blk=False, out_blk=False, a_cols=None, b_cols=None, add=None, out_dtype=F32,
        fuse=1, wide=False, name="mm"):
    a2, b2 = a.shape[-2:], b.shape[-2:]
    a_last = a2[1] if a_cols is None else a_cols[1]
    a_start = 0 if a_cols is None else a_cols[0]
    b_start = 0
    if b_cols is not None:
        assert mode != "nt"
        b_start, b2 = b_cols[0], (b2[0], b_cols[1])
    if mode == "nn":
        m, k, (k2, n) = a2[0], a_last, b2
    elif mode == "nt":
        m, k, (n, k2) = a2[0], a_last, b2
    else:
        k, m, (k2, n) = a2[0], a_last, b2
    assert k == k2, (a.shape, b.shape, mode)
    tm = _pick(m, (1024, 704, 512, 256, 128))
    tn = _pick(n, ((2048,) if wide else ()) + (1024, 768, 704, 512, 256, 192, 128))
    tk = k if k <= K_TILE_MAX else _pick(k, (K_TILE_MAX, 1024, 512))
    nk = k // tk
    jo = N_DEV if out_blk else 1
    reduce_blocks = a_blk and b_blk and not out_blk
    assert fuse == 1 or reduce_blocks
    jr = N_DEV // fuse if reduce_blocks else 1
    ca, cb = {"nn": (1, 0), "nt": (1, 1), "tn": (0, 0)}[mode]
    has_add = add is not None
    single = jr * nk == 1
    if mode == "tn":
        assert a_start % tm == 0
        a_block, a_idx = (tk, tm), (lambda i, kk: (kk, i + a_start // tm))
    else:
        assert a_start % tk == 0
        a_block, a_idx = (tm, tk), (lambda i, kk: (i, kk + a_start // tk))
    assert b_start % tn == 0
    b_block, b_idx = (((tn, tk), (lambda nn_, kk: (nn_, kk))) if mode == "nt"
                      else ((tk, tn), (lambda nn_, kk: (kk, nn_ + b_start // tn))))

    def blk_specs(blocked, block, idx, of_a, t):
        def pos(o, i, nn_, kk):
            return idx(i, kk) if of_a else idx(nn_, kk)
        if blocked:
            return pl.BlockSpec((None,) + block,
                                lambda o, i, nn_, r, kk: ((o if out_blk else r * fuse + t),) + pos(o, i, nn_, kk))
        return pl.BlockSpec(block, lambda o, i, nn_, r, kk: pos(o, i, nn_, kk))

    a_specs = [blk_specs(a_blk, a_block, a_idx, True, t) for t in range(fuse)]
    b_specs = [blk_specs(b_blk, b_block, b_idx, False, t) for t in range(fuse)]
    o_spec = (pl.BlockSpec((None, tm, tn), lambda o, i, nn_, r, kk: (o, i, nn_)) if out_blk
              else pl.BlockSpec((tm, tn), lambda o, i, nn_, r, kk: (i, nn_)))

    groups = _row_slices(tm, LANE if mode == "tn" else 16)

    def body(*refs):
        a_refs, b_refs = refs[:fuse], refs[fuse:2 * fuse]
        add_ref = refs[2 * fuse] if has_add else None
        o_ref = refs[2 * fuse + 1] if has_add else refs[2 * fuse]

        def partial(rs):
            out = None
            for t in range(fuse):
                av = a_refs[t][:, rs] if mode == "tn" else a_refs[t][rs, :]
                d = _dot(_mx(av), _mx(b_refs[t][...]), ca, cb)
                out = d if out is None else out + d
            return out

        if single:
            for rs in groups:
                res = partial(rs)
                if has_add:
                    res = res + add_ref[rs, :]
                o_ref[rs, :] = res.astype(o_ref.dtype)
            return
        acc = refs[-1]
        r, kk = pl.program_id(3), pl.program_id(4)

        @pl.when(jnp.logical_and(r == 0, kk == 0))
        def _():
            acc[...] = jnp.zeros_like(acc)

        for rs in groups:
            acc[rs, :] += partial(rs)

        @pl.when(jnp.logical_and(r == jr - 1, kk == nk - 1))
        def _():
            res = acc[...]
            if has_add:
                res = res + add_ref[...]
            o_ref[...] = res.astype(o_ref.dtype)

    out_shape = ((N_DEV, m, n) if out_blk else (m, n))
    return pl.pallas_call(
        body, name=name, grid=(jo, m // tm, n // tn, jr, nk),
        in_specs=a_specs + b_specs + ([o_spec] if has_add else []), out_specs=o_spec,
        out_shape=jax.ShapeDtypeStruct(out_shape, out_dtype),
        scratch_shapes=[] if single else [pltpu.VMEM((tm, tn), F32)],
        compiler_params=_params("parallel", "parallel", "parallel", "arbitrary", "arbitrary"),
    )(*((a,) * fuse + (b,) * fuse + ((add,) if has_add else ())))


def _mm_sum(a_list, b_list, out_dtype=F32, name="mm_sum"):
    m, n = a_list[0].shape[0], b_list[0].shape[1]
    ns = len(a_list)
    tm = _pick(m, (1024, 512, 256, 128))
    tn = _pick(n, (1024, 512, 256, 128))
    groups = _row_slices(tm, 16)

    def body(*refs):
        a_refs, b_refs, o_ref = refs[:ns], refs[ns:2 * ns], refs[2 * ns]
        for rs in groups:
            acc = _dot(_mx(a_refs[0][rs, :]), _mx(b_refs[0][...]), 1, 0)
            for s in range(1, ns):
                acc = acc + _dot(_mx(a_refs[s][rs, :]), _mx(b_refs[s][...]), 1, 0)
            o_ref[rs, :] = acc.astype(o_ref.dtype)

    return pl.pallas_call(
        body, name=name, grid=(m // tm, n // tn),
        in_specs=([pl.BlockSpec((tm, a.shape[1]), lambda i, j: (i, 0)) for a in a_list]
                  + [pl.BlockSpec((b.shape[0], tn), lambda i, j: (0, j)) for b in b_list]),
        out_specs=pl.BlockSpec((tm, tn), lambda i, j: (i, j)),
        out_shape=jax.ShapeDtypeStruct((m, n), out_dtype), compiler_params=_params("parallel", "parallel"),
    )(*a_list, *b_list)


def _row_tile(r_, streams=4):
    return _pick(r_, ((512,) if streams <= 4 else ()) + (256, 128, 64, 32, 16, 8))


def _rms_fwd(t, w, groups=1, res=None, out_dtype=F32, name="rms_fwd"):
    r_, f = t.shape
    fg = f // groups
    tr = _row_tile(r_)
    has_res = res is not None

    def body(*refs):
        t_ref, w_ref = refs[0], refs[1]
        res_ref = refs[2] if has_res else None
        o_ref = refs[-1]
        for g in range(groups):
            sl = slice(g * fg, (g + 1) * fg)
            tv = t_ref[:, sl].astype(F32)
            r = lax.rsqrt(jnp.mean(tv * tv, axis=-1, keepdims=True) + EPS)
            y = tv * r * w_ref[:, sl]
            if has_res:
                y = y + res_ref[:, sl]
            o_ref[:, sl] = y.astype(o_ref.dtype)

    row = pl.BlockSpec((tr, f), lambda i: (i, 0))
    wsp = pl.BlockSpec((1, f), lambda i: (0, 0))
    return pl.pallas_call(
        body, name=name, grid=(r_ // tr,),
        in_specs=[row, wsp] + ([row] if has_res else []), out_specs=row,
        out_shape=jax.ShapeDtypeStruct((r_, f), out_dtype),
        compiler_params=_params("parallel"),
    )(*((t, w.reshape(1, f)) + ((res,) if has_res else ())))


def _rms_bwd(t, w, dys, res=None, out_dtype=F32, name="rms_bwd"):
    r_, f = t.shape
    groups = len(dys)
    fg = f // groups
    tr = _row_tile(r_)
    has_res = res is not None

    def body(*refs):
        t_ref, w_ref = refs[0], refs[1]
        dy_refs = refs[2:2 + groups]
        res_ref = refs[2 + groups] if has_res else None
        dt_ref, dw_ref = refs[-2], refs[-1]

        @pl.when(pl.program_id(0) == 0)
        def _():
            dw_ref[...] = jnp.zeros_like(dw_ref)

        for g in range(groups):
            sl = slice(g * fg, (g + 1) * fg)
            tv = t_ref[:, sl].astype(F32)
            dyv = dy_refs[g][...].astype(F32)
            r = lax.rsqrt(jnp.mean(tv * tv, axis=-1, keepdims=True) + EPS)
            gw = dyv * w_ref[:, sl]
            c = jnp.mean(gw * tv, axis=-1, keepdims=True)
            dt = r * gw - tv * (r * r * r * c)
            if has_res:
                dt = dt + res_ref[:, sl]
            dt_ref[:, sl] = dt.astype(dt_ref.dtype)
            dw_ref[:, sl] += jnp.sum(dyv * tv * r, axis=0, keepdims=True)

    row = pl.BlockSpec((tr, f), lambda i: (i, 0))
    grow = pl.BlockSpec((tr, fg), lambda i: (i, 0))
    wsp = pl.BlockSpec((1, f), lambda i: (0, 0))
    return pl.pallas_call(
        body, name=name, grid=(r_ // tr,),
        in_specs=[row, wsp] + [grow] * groups + ([row] if has_res else []), out_specs=[row, wsp],
        out_shape=[jax.ShapeDtypeStruct((r_, f), out_dtype), jax.ShapeDtypeStruct((1, f), F32)],
        compiler_params=_params("arbitrary"),
    )(*((t, w.reshape(1, f)) + tuple(dys) + ((res,) if has_res else ())))


def _norm_res_norm(t, res, w1, w2, name="post_mix_pre_ffn_norm"):
    r_, f = t.shape
    tr = _row_tile(r_)

    def body(t_ref, res_ref, w1_ref, w2_ref, h_ref, v_ref):
        tv = t_ref[...].astype(F32)
        h = res_ref[...] + tv * lax.rsqrt(jnp.mean(tv * tv, axis=-1, keepdims=True) + EPS) * w1_ref[...]
        h_ref[...] = h
        v_ref[...] = (h * lax.rsqrt(jnp.mean(h * h, axis=-1, keepdims=True) + EPS) * w2_ref[...]).astype(v_ref.dtype)

    row = pl.BlockSpec((tr, f), lambda i: (i, 0))
    wsp = pl.BlockSpec((1, f), lambda i: (0, 0))
    return pl.pallas_call(
        body, name=name, grid=(r_ // tr,), in_specs=[row, row, wsp, wsp], out_specs=[row, row],
        out_shape=[jax.ShapeDtypeStruct((r_, f), F32), jax.ShapeDtypeStruct((r_, f), MXU_DTYPE)],
        compiler_params=_params("parallel"),
    )(t, res, w1.reshape(1, f), w2.reshape(1, f))


def _norm_res_norm_bwd(h, w2, dv, dres, t, w1, name="pre_ffn_post_mix_norm_bwd"):
    r_, f = h.shape
    tr = _row_tile(r_, streams=6)

    def body(h_ref, w2_ref, dv_ref, dres_ref, t_ref, w1_ref, dh_ref, dt_ref, dw2_ref, dw1_ref):
        @pl.when(pl.program_id(0) == 0)
        def _():
            dw2_ref[...] = jnp.zeros_like(dw2_ref)
            dw1_ref[...] = jnp.zeros_like(dw1_ref)

        def rms_bwd(tv, wv, dyv):
            r = lax.rsqrt(jnp.mean(tv * tv, axis=-1, keepdims=True) + EPS)
            gw = dyv * wv
            c = jnp.mean(gw * tv, axis=-1, keepdims=True)
            return r * gw - tv * (r * r * r * c), jnp.sum(dyv * tv * r, axis=0, keepdims=True)

        d1, g2 = rms_bwd(h_ref[...], w2_ref[...], dv_ref[...].astype(F32))
        dh = d1 + dres_ref[...]
        dh_ref[...] = dh
        dw2_ref[...] += g2
        d2, g1 = rms_bwd(t_ref[...].astype(F32), w1_ref[...], dh)
        dt_ref[...] = d2.astype(dt_ref.dtype)
        dw1_ref[...] += g1

    row = pl.BlockSpec((tr, f), lambda i: (i, 0))
    wsp = pl.BlockSpec((1, f), lambda i: (0, 0))
    return pl.pallas_call(
        body, name=name, grid=(r_ // tr,), in_specs=[row, wsp, row, row, row, wsp], out_specs=[row, row, wsp, wsp],
        out_shape=[jax.ShapeDtypeStruct((r_, f), F32), jax.ShapeDtypeStruct((r_, f), MXU_DTYPE),
                   jax.ShapeDtypeStruct((1, f), F32), jax.ShapeDtypeStruct((1, f), F32)],
        compiler_params=_params("arbitrary"),
    )(h, w2.reshape(1, f), dv, dres, t, w1.reshape(1, f))


def _hnorm_fwd(o, w, width, name="attn_out_norm"):
    h, s_, v = o.shape
    tr = _row_tile(s_)

    def body(o_ref, w_ref, y_ref):
        ss = jnp.sum(o_ref[0] * o_ref[0], axis=-1, keepdims=True)
        for i in range(1, h):
            ss = ss + jnp.sum(o_ref[i] * o_ref[i], axis=-1, keepdims=True)
        r = lax.rsqrt(ss * (1.0 / (h * v)) + EPS)
        for i in range(h):
            sl = slice(i * v, (i + 1) * v)
            y_ref[:, sl] = (o_ref[i] * r * w_ref[:, sl]).astype(y_ref.dtype)

    return pl.pallas_call(
        body, name=name, grid=(s_ // tr,),
        in_specs=[pl.BlockSpec((h, tr, v), lambda i: (0, i, 0)), pl.BlockSpec((1, h * v), lambda i: (0, 0))],
        out_specs=pl.BlockSpec((tr, h * v), lambda i: (i, 0)),
        out_shape=jax.ShapeDtypeStruct((s_, width), MXU_DTYPE), compiler_params=_params("parallel"),
    )(o, w)


def _hnorm_bwd(o, w, dy, name="attn_out_norm_bwd"):
    h, s_, v = o.shape
    tr = _row_tile(s_)

    def body(o_ref, w_ref, dy_ref, do_ref, delta_ref, dw_ref):
        @pl.when(pl.program_id(0) == 0)
        def _():
            dw_ref[...] = jnp.zeros_like(dw_ref)

        ss = jnp.zeros((tr, 1), F32)
        cc = jnp.zeros((tr, 1), F32)
        for i in range(h):
            sl = slice(i * v, (i + 1) * v)
            ov = o_ref[i]
            ss = ss + jnp.sum(ov * ov, axis=-1, keepdims=True)
            cc = cc + jnp.sum(dy_ref[:, sl].astype(F32) * w_ref[:, sl] * ov, axis=-1, keepdims=True)
        r = lax.rsqrt(ss * (1.0 / (h * v)) + EPS)
        c = cc * (1.0 / (h * v))
        for i in range(h):
            sl = slice(i * v, (i + 1) * v)
            ov = o_ref[i]
            dyv = dy_ref[:, sl].astype(F32)
            dov = r * dyv * w_ref[:, sl] - ov * (r * r * r * c)
            do_ref[i] = dov.astype(do_ref.dtype)
            delta_ref[i] = jnp.sum(dov * ov, axis=-1, keepdims=True)
            dw_ref[:, sl] += jnp.sum(dyv * ov * r, axis=0, keepdims=True)

    blk = pl.BlockSpec((h, tr, v), lambda i: (0, i, 0))
    wsp = pl.BlockSpec((1, h * v), lambda i: (0, 0))
    return pl.pallas_call(
        body, name=name, grid=(s_ // tr,),
        in_specs=[blk, wsp, pl.BlockSpec((tr, h * v), lambda i: (i, 0))],
        out_specs=[blk, pl.BlockSpec((h, tr, 1), lambda i: (0, i, 0)), wsp],
        out_shape=[jax.ShapeDtypeStruct(o.shape, MXU_DTYPE), jax.ShapeDtypeStruct((h, s_, 1), F32),
                   jax.ShapeDtypeStruct((1, h * v), F32)],
        compiler_params=_params("arbitrary"),
    )(o, w, dy)


def _loss_head(ffn, h1, target, w, name="loss_head"):
    r_, f = ffn.shape
    tr = _row_tile(r_)

    def body(ffn_ref, h1_ref, tg_ref, w_ref, loss_ref, dy_ref, dffn_ref, dw_ref):
        @pl.when(pl.program_id(0) == 0)
        def _():
            dw_ref[...] = jnp.zeros_like(dw_ref)
            loss_ref[...] = jnp.zeros_like(loss_ref)

        tv = ffn_ref[...].astype(F32)
        wv = w_ref[...]
        r = lax.rsqrt(jnp.mean(tv * tv, axis=-1, keepdims=True) + EPS)
        tn = tv * r
        e = h1_ref[...] + tn * wv - tg_ref[...]
        tot = jnp.sum(jnp.sum(e * e, axis=1, keepdims=True), axis=0, keepdims=True) * (0.5 / f)
        loss_ref[...] += tot + jnp.zeros_like(loss_ref)
        dyv = e * (1.0 / f)
        dy_ref[...] = dyv
        gw = dyv * wv
        c = jnp.mean(gw * tv, axis=-1, keepdims=True)
        dffn_ref[...] = (r * gw - tv * (r * r * r * c)).astype(dffn_ref.dtype)
        dw_ref[...] += jnp.sum(dyv * tn, axis=0, keepdims=True)

    row = pl.BlockSpec((tr, f), lambda i: (i, 0))
    wsp = pl.BlockSpec((1, f), lambda i: (0, 0))
    lsp = pl.BlockSpec((1, LANE), lambda i: (0, 0))
    return pl.pallas_call(
        body, name=name, grid=(r_ // tr,),
        in_specs=[row, row, row, wsp], out_specs=[lsp, row, row, wsp],
        out_shape=[jax.ShapeDtypeStruct((1, LANE), F32), jax.ShapeDtypeStruct((r_, f), F32),
                   jax.ShapeDtypeStruct((r_, f), MXU_DTYPE), jax.ShapeDtypeStruct((1, f), F32)],
        compiler_params=_params("arbitrary"),
    )(ffn, h1, target, w.reshape(1, f))


def _rot_matrix():
    p = np.zeros((ROPE, ROPE), np.float32)
    for i in range(HALF):
        p[i + HALF, i] = -1.0
        p[i, i + HALF] = 1.0
    return jnp.asarray(p, BF16)


def _rope_val(r, c2, s2, rot):
    hi, mid, _ = _split3(r)
    return r * c2 + (_dot(hi, rot, 1, 0) + _dot(mid, rot, 1, 0)) * s2


def _q_prep(q, cos2, sin2, scale, name):
    h, s_, _ = q.shape
    tr = _pick(s_, (4096, 2048, 1024, 512, 256, 128, 64, 32, 16))

    def body(q_ref, c_ref, s_ref, rot_ref, o_ref):
        for rs in _row_slices(tr, 16):
            x = q_ref[rs, :]
            o_ref[rs, :NOPE] = (x[:, :NOPE] * scale).astype(o_ref.dtype)
            o_ref[rs, NOPE:] = (_rope_val(x[:, NOPE:], c_ref[rs, :], s_ref[rs, :], rot_ref[...]) * scale).astype(o_ref.dtype)

    blk = pl.BlockSpec((None, tr, QK), lambda hh, i: (hh, i, 0))
    csp = pl.BlockSpec((tr, ROPE), lambda hh, i: (i, 0))
    return pl.pallas_call(
        body, name=name, grid=(h, s_ // tr),
        in_specs=[blk, csp, csp, pl.BlockSpec((ROPE, ROPE), lambda hh, i: (0, 0))], out_specs=blk,
        out_shape=jax.ShapeDtypeStruct(q.shape, MXU_DTYPE), compiler_params=_params("parallel", "parallel"),
    )(q, cos2, sin2, _rot_matrix())


def _q_up(qkvn, w_uq_t, cos2, sin2, scale, name="q_up"):
    s_ = qkvn.shape[0]
    h = w_uq_t.shape[0]
    tm = _pick(s_, (4096, 2048, 1024, 512, 256, 128))

    def body(a_ref, w_ref, c_ref, s_ref, rot_ref, o_ref):
        for rs in _row_slices(tm, 16):
            x = _dot(_mx(a_ref[rs, :]), _mx(w_ref[...]), 1, 1)
            o_ref[rs, :NOPE] = (x[:, :NOPE] * scale).astype(o_ref.dtype)
            o_ref[rs, NOPE:] = (_rope_val(x[:, NOPE:], c_ref[rs, :], s_ref[rs, :], rot_ref[...]) * scale).astype(o_ref.dtype)

    csp = pl.BlockSpec((tm, ROPE), lambda j, i: (i, 0))
    return pl.pallas_call(
        body, name=name, grid=(h, s_ // tm),
        in_specs=[pl.BlockSpec((tm, Q_RANK), lambda j, i: (i, 0)), pl.BlockSpec((None, QK, Q_RANK), lambda j, i: (j, 0, 0)),
                  csp, csp, pl.BlockSpec((ROPE, ROPE), lambda j, i: (0, 0))],
        out_specs=pl.BlockSpec((None, tm, QK), lambda j, i: (j, i, 0)),
        out_shape=jax.ShapeDtypeStruct((h, s_, QK), MXU_DTYPE), compiler_params=_params("parallel", "parallel"),
    )(qkvn, w_uq_t, cos2, sin2, _rot_matrix())


def _kv_up(qkvn, w_ukv, small, cos2, sin2, name="kv_up"):
    s_ = qkvn.shape[0]
    h = w_ukv.shape[0]
    tm = _pick(s_, (4096, 2048, 1024, 512, 256, 128))

    def body(a_ref, w_ref, sm_ref, c_ref, s_ref, rot_ref, k_ref, v_ref):
        for rs in _row_slices(tm, 16):
            x = _dot(_mx(a_ref[rs, :]), _mx(w_ref[...]), 1, 0)
            k_ref[rs, :NOPE] = x[:, :NOPE].astype(k_ref.dtype)
            k_ref[rs, NOPE:] = _rope_val(sm_ref[rs, :ROPE], c_ref[rs, :], s_ref[rs, :], rot_ref[...]).astype(k_ref.dtype)
            v_ref[rs, :] = x[:, NOPE:].astype(v_ref.dtype)

    csp = pl.BlockSpec((tm, ROPE), lambda j, i: (i, 0))
    return pl.pallas_call(
        body, name=name, grid=(h, s_ // tm),
        in_specs=[pl.BlockSpec((tm, KV_RANK), lambda j, i: (i, Q_RANK // KV_RANK)),
                  pl.BlockSpec((None, KV_RANK, NOPE + VDIM), lambda j, i: (j, 0, 0)),
                  pl.BlockSpec((tm, LANE), lambda j, i: (i, 0)), csp, csp, pl.BlockSpec((ROPE, ROPE), lambda j, i: (0, 0))],
        out_specs=[pl.BlockSpec((None, tm, QK), lambda j, i: (j, i, 0)), pl.BlockSpec((None, tm, VDIM), lambda j, i: (j, i, 0))],
        out_shape=[jax.ShapeDtypeStruct((h, s_, QK), MXU_DTYPE), jax.ShapeDtypeStruct((h, s_, VDIM), MXU_DTYPE)],
        compiler_params=_params("parallel", "parallel"),
    )(qkvn, w_ukv, small, cos2, sin2, _rot_matrix())


def _dkv_post(dk, dv, ddt, cos2, nsin2, name="dkv_post"):
    h, s_, _ = dk.shape
    tr = _row_tile(s_)

    def body(dk_ref, dv_ref, ddt_ref, c_ref, s_ref, rot_ref, dkv_ref, dsm_ref):
        acc = dk_ref[0, :, NOPE:]
        for i in range(1, h):
            acc = acc + dk_ref[i, :, NOPE:]
        dsm_ref[:, :ROPE] = _rope_val(acc, c_ref[...], s_ref[...], rot_ref[...]).astype(dsm_ref.dtype)
        dsm_ref[:, ROPE:] = ddt_ref[:, ROPE:].astype(dsm_ref.dtype)
        for i in range(h):
            dkv_ref[i, :, :NOPE] = dk_ref[i, :, :NOPE].astype(dkv_ref.dtype)
            dkv_ref[i, :, NOPE:] = dv_ref[i].astype(dkv_ref.dtype)

    csp = pl.BlockSpec((tr, ROPE), lambda i: (i, 0))
    return pl.pallas_call(
        body, name=name, grid=(s_ // tr,),
        in_specs=[pl.BlockSpec((h, tr, QK), lambda i: (0, i, 0)), pl.BlockSpec((h, tr, VDIM), lambda i: (0, i, 0)),
                  pl.BlockSpec((tr, LANE), lambda i: (i, 0)), csp, csp, pl.BlockSpec((ROPE, ROPE), lambda i: (0, 0))],
        out_specs=[pl.BlockSpec((h, tr, NOPE + VDIM), lambda i: (0, i, 0)), pl.BlockSpec((tr, LANE), lambda i: (i, 0))],
        out_shape=[jax.ShapeDtypeStruct((h, s_, NOPE + VDIM), MXU_DTYPE), jax.ShapeDtypeStruct((s_, LANE), MXU_DTYPE)],
        compiler_params=_params("parallel"),
    )(dk, dv, ddt, cos2, nsin2, _rot_matrix())


def _attn_tile(s):
    return 2048 if s % 4096 == 0 else s // 2


def _pairs(n, by_key):
    if by_key:
        pr = [(i, j) for j in range(n) for i in range(j, n)]
    else:
        pr = [(i, j) for i in range(n) for j in range(i + 1)]
    return (jnp.asarray([p[0] for p in pr], jnp.int32), jnp.asarray([p[1] for p in pr], jnp.int32))


ATTN_ROW_GROUPS = 8


def _row_groups(t, diag):
    tg = t // ATTN_ROW_GROUPS
    out = []
    for r in range(ATTN_ROW_GROUPS):
        nc = (r + 1) * tg if diag else t
        mask = None
        if diag:
            mask = (lax.broadcasted_iota(jnp.int32, (tg, nc), 1)
                    <= lax.broadcasted_iota(jnp.int32, (tg, nc), 0) + r * tg)
        out.append((slice(r * tg, (r + 1) * tg), nc, mask))
    return out


def _flash_specs(t, dk, dv):
    qsp = pl.BlockSpec((None, t, dk), lambda hh, p, qi, kj: (hh, qi[p], 0))
    ksp = pl.BlockSpec((None, t, dk), lambda hh, p, qi, kj: (hh, kj[p], 0))
    vsp = pl.BlockSpec((None, t, dv), lambda hh, p, qi, kj: (hh, kj[p], 0))
    osp = pl.BlockSpec((None, t, dv), lambda hh, p, qi, kj: (hh, qi[p], 0))
    lsp = pl.BlockSpec((None, t, 1), lambda hh, p, qi, kj: (hh, qi[p], 0))
    return qsp, ksp, vsp, osp, lsp


def _flash_fwd(q, k, v, name="flash_fwd"):
    h, s_, dk = q.shape
    dv = v.shape[-1]
    t = _attn_tile(s_)
    n = s_ // t
    qi, kj = _pairs(n, False)

    def body(qi_ref, kj_ref, q_ref, k_ref, v_ref, o_ref, lse_ref, m_s, l_s, acc):
        p_ = pl.program_id(1)
        i, j = qi_ref[p_], kj_ref[p_]

        @pl.when(j == 0)
        def _():
            m_s[...] = jnp.full_like(m_s, -jnp.inf)
            l_s[...] = jnp.zeros_like(l_s)
            acc[...] = jnp.zeros_like(acc)

        def update(diag):
            for rs, nc, mask in _row_groups(t, diag):
                sc = _dot(q_ref[rs, :], k_ref[0:nc, :], 1, 1)
                if mask is not None:
                    sc = jnp.where(mask, sc, -jnp.inf)
                m_old = m_s[rs, :]
                m_new = jnp.maximum(m_old, jnp.max(sc, axis=1, keepdims=True))
                alpha = jnp.exp(m_old - m_new)
                p = jnp.exp(sc - m_new)
                l_s[rs, :] = alpha * l_s[rs, :] + jnp.sum(p, axis=1, keepdims=True)
                acc[rs, :] = alpha * acc[rs, :] + _dot(_mx(p), v_ref[0:nc, :], 1, 0)
                m_s[rs, :] = m_new

        @pl.when(j < i)
        def _():
            update(False)

        @pl.when(j == i)
        def _():
            update(True)
            o_ref[...] = acc[...] / l_s[...]
            lse_ref[...] = m_s[...] + jnp.log(l_s[...])

    qsp, ksp, vsp, osp, lsp = _flash_specs(t, dk, dv)
    gs = pltpu.PrefetchScalarGridSpec(
        num_scalar_prefetch=2, grid=(h, qi.shape[0]), in_specs=[qsp, ksp, vsp], out_specs=[osp, lsp],
        scratch_shapes=[pltpu.VMEM((t, 1), F32), pltpu.VMEM((t, 1), F32), pltpu.VMEM((t, dv), F32)])
    return pl.pallas_call(
        body, name=name, grid_spec=gs,
        out_shape=[jax.ShapeDtypeStruct((h, s_, dv), F32), jax.ShapeDtypeStruct((h, s_, 1), F32)],
        compiler_params=_params("parallel", "arbitrary"),
    )(qi, kj, q, k, v)


def _flash_bwd(q, k, v, do, lse, delta, name="flash_bwd"):
    h, s_, dk = q.shape
    dv = v.shape[-1]
    t = _attn_tile(s_)
    tg = t // ATTN_ROW_GROUPS
    n = s_ // t
    qi, kj = _pairs(n, True)

    def body(qi_ref, kj_ref, q_ref, k_ref, v_ref, do_ref, lse_ref, delta_ref, dq_ref, dk_ref, dv_ref, dk_acc, dv_acc):
        p_ = pl.program_id(1)
        i, j = qi_ref[p_], kj_ref[p_]

        @pl.when(p_ == 0)
        def _():
            dq_ref[...] = jnp.zeros_like(dq_ref)

        def update(diag):
            for g, (rs, nc, mask) in enumerate(_row_groups(t, diag)):
                sc = _dot(q_ref[rs, :], k_ref[0:nc, :], 1, 1)
                if mask is not None:
                    sc = jnp.where(mask, sc, -jnp.inf)
                p = jnp.exp(sc - lse_ref[rs, :])
                dob = _mx(do_ref[rs, :])
                dv_acc[0:nc, :] += _dot(_mx(p), dob, 0, 0)
                dp = _dot(dob, v_ref[0:nc, :], 1, 1)
                dsb = _mx(p * (dp - delta_ref[rs, :]))
                dk_acc[0:nc, :] += _dot(dsb, q_ref[rs, :], 0, 0)
                rows = pl.ds(pl.multiple_of(i * t + g * tg, tg), tg)
                dq_ref[rows, :] += _dot(dsb, k_ref[0:nc, :], 1, 0)

        @pl.when(i == j)
        def _():
            dk_acc[...] = jnp.zeros_like(dk_acc)
            dv_acc[...] = jnp.zeros_like(dv_acc)
            update(True)

        @pl.when(i > j)
        def _():
            update(False)

        @pl.when(i == n - 1)
        def _():
            dk_ref[...] = dk_acc[...]
            dv_ref[...] = dv_acc[...]

    qsp, ksp, vsp, osp, lsp = _flash_specs(t, dk, dv)
    dqsp = pl.BlockSpec((None, s_, dk), lambda hh, p, qi, kj: (hh, 0, 0))
    gs = pltpu.PrefetchScalarGridSpec(
        num_scalar_prefetch=2, grid=(h, qi.shape[0]), in_specs=[qsp, ksp, vsp, osp, lsp, lsp],
        out_specs=[dqsp, ksp, vsp],
        scratch_shapes=[pltpu.VMEM((t, dk), F32), pltpu.VMEM((t, dv), F32)])
    return pl.pallas_call(
        body, name=name, grid_spec=gs,
        out_shape=[jax.ShapeDtypeStruct((h, s_, dk), F32), jax.ShapeDtypeStruct((h, s_, dk), F32),
                   jax.ShapeDtypeStruct((h, s_, dv), F32)],
        compiler_params=_params("parallel", "arbitrary"),
    )(qi, kj, q, k, v, do, lse, delta)


HALO = 8


def _conv_specs(s_, c, tr, after):
    main = pl.BlockSpec((tr, c), lambda i: (i, 0))
    per = tr // HALO
    if after:
        halo = pl.BlockSpec((HALO, c), lambda i: (jnp.minimum((i + 1) * per, s_ // HALO - 1), 0))
    else:
        halo = pl.BlockSpec((HALO, c), lambda i: (jnp.maximum(i * per - 1, 0), 0))
    return main, halo


def _fill_before(ext, t_ref, h_ref, tr):
    ext[0:HALO, :] = jnp.where(pl.program_id(0) > 0, h_ref[...], 0.0)
    ext[HALO:HALO + tr, :] = t_ref[...]


def _taps(ext, w_ref, tr):
    base = HALO - (CONV_K - 1)
    acc = ext[base:base + tr, :] * w_ref[0:1, :]
    for k in range(1, CONV_K):
        acc = acc + ext[base + k:base + k + tr, :] * w_ref[k:k + 1, :]
    return acc


def _conv_fwd(t, w, b, name="conv_fwd"):
    s_, c = t.shape
    tr = _row_tile(s_)

    def body(t_ref, h_ref, w_ref, b_ref, o_ref, ext):
        _fill_before(ext, t_ref, h_ref, tr)
        o_ref[...] = _silu(_taps(ext, w_ref, tr) + b_ref[...])

    main, halo = _conv_specs(s_, c, tr, False)
    return pl.pallas_call(
        body, name=name, grid=(s_ // tr,),
        in_specs=[main, halo, pl.BlockSpec((CONV_K, c), lambda i: (0, 0)), pl.BlockSpec((1, c), lambda i: (0, 0))],
        out_specs=main, out_shape=jax.ShapeDtypeStruct((s_, c), F32),
        scratch_shapes=[pltpu.VMEM((tr + HALO, c), F32)], compiler_params=_params("parallel"),
    )(t, t, w, b)


def _conv_bwd_pre(t, w, b, dact, name="conv_bwd_pre"):
    s_, c = t.shape
    tr = _row_tile(s_)

    def body(t_ref, h_ref, w_ref, b_ref, da_ref, dpre_ref, dwb_ref, ext):
        @pl.when(pl.program_id(0) == 0)
        def _():
            dwb_ref[...] = jnp.zeros_like(dwb_ref)

        _fill_before(ext, t_ref, h_ref, tr)
        dpre = da_ref[...] * _dsilu(_taps(ext, w_ref, tr) + b_ref[...])
        dpre_ref[...] = dpre
        base = HALO - (CONV_K - 1)
        for k in range(CONV_K):
            dwb_ref[k:k + 1, :] += jnp.sum(dpre * ext[base + k:base + k + tr, :], axis=0, keepdims=True)
        dwb_ref[CONV_K:CONV_K + 1, :] += jnp.sum(dpre, axis=0, keepdims=True)

    main, halo = _conv_specs(s_, c, tr, False)
    return pl.pallas_call(
        body, name=name, grid=(s_ // tr,),
        in_specs=[main, halo, pl.BlockSpec((CONV_K, c), lambda i: (0, 0)), pl.BlockSpec((1, c), lambda i: (0, 0)), main],
        out_specs=[main, pl.BlockSpec((8, c), lambda i: (0, 0))],
        out_shape=[jax.ShapeDtypeStruct((s_, c), F32), jax.ShapeDtypeStruct((8, c), F32)],
        scratch_shapes=[pltpu.VMEM((tr + HALO, c), F32)], compiler_params=_params("arbitrary"),
    )(t, t, w, b, dact)


def _conv_bwd_in(dpre, w, name="conv_bwd_in"):
    s_, c = dpre.shape
    tr = _row_tile(s_)
    nt = s_ // tr

    def body(d_ref, h_ref, w_ref, o_ref, ext):
        ext[0:tr, :] = d_ref[...]
        ext[tr:tr + HALO, :] = jnp.where(pl.program_id(0) < nt - 1, h_ref[...], 0.0)
        acc = ext[CONV_K - 1:CONV_K - 1 + tr, :] * w_ref[0:1, :]
        for k in range(1, CONV_K):
            acc = acc + ext[CONV_K - 1 - k:CONV_K - 1 - k + tr, :] * w_ref[k:k + 1, :]
        o_ref[...] = acc.astype(o_ref.dtype)

    main, halo = _conv_specs(s_, c, tr, True)
    return pl.pallas_call(
        body, name=name, grid=(nt,),
        in_specs=[main, halo, pl.BlockSpec((CONV_K, c), lambda i: (0, 0))],
        out_specs=main, out_shape=jax.ShapeDtypeStruct((s_, c), MXU_DTYPE),
        scratch_shapes=[pltpu.VMEM((tr + HALO, c), F32)], compiler_params=_params("parallel"),
    )(dpre, dpre, w)


def _ssd_chunk_common(dt_ref, dtt_ref, br_ref, bc_ref, ar_ref, ac_ref):
    li = lax.broadcasted_iota(jnp.int32, (CHUNK, CHUNK), 0)
    si = lax.broadcasted_iota(jnp.int32, (CHUNK, CHUNK), 1)
    lower = li >= si
    lower_b = lower.astype(BF16)
    upper_b = (li <= si).astype(BF16)
    zr = dt_ref[...] + br_ref[...]
    dtc = _softplus(zr)
    a_row = -jnp.exp(ar_ref[...])
    acum = _exact_dot(lower_b, dtc * a_row, 1, 0, False)
    dtt = _softplus(dtt_ref[...] + bc_ref[...])
    acum_t = _exact_dot(dtt * (-jnp.exp(ac_ref[...])), upper_b, 1, 0, True)
    return lower, upper_b, zr, dtc, a_row, acum, acum_t


def _head_terms(h, lower, dtc, acum, acum_t):
    lane = lax.broadcasted_iota(jnp.int32, (1, LANE), 1)
    sub = lax.broadcasted_iota(jnp.int32, (SSD_H, 1), 0)
    rowid = lax.broadcasted_iota(jnp.int32, (CHUNK, 1), 0)
    oh = (lane == HEAD_LANE + h).astype(F32)
    acol = jnp.sum(acum * oh, axis=1, keepdims=True)
    dcol = jnp.sum(dtc * oh, axis=1, keepdims=True)
    arow = jnp.sum(acum_t * (sub == h).astype(F32), axis=0, keepdims=True)
    alast = jnp.sum(jnp.where(rowid == CHUNK - 1, acol, 0.0), axis=0, keepdims=True)
    decay = jnp.exp(jnp.where(lower, acol - arow, -jnp.inf))
    return oh, acol, dcol, alast, decay


SSD_PAIRS = SSD_H // 2
PAIRS_PER_GROUP = SSD_E // 2


def _ps(q):
    return slice(q * LANE, (q + 1) * LANE)


def _gs(off, g):
    return slice(off + g * SSD_N, off + (g + 1) * SSD_N)


def _lanes(c0, c1):
    return jnp.where(lax.broadcasted_iota(jnp.int32, (1, LANE), 1) < SSD_P, c0, c1)


def _rows(c0, c1):
    return jnp.where(lax.broadcasted_iota(jnp.int32, (LANE, 1), 0) < SSD_P, c0, c1)


def _lane_halves(t):
    first = lax.broadcasted_iota(jnp.int32, (1, LANE), 1) < SSD_P
    return (jnp.sum(jnp.where(first, t, 0.0), axis=1, keepdims=True),
            jnp.sum(jnp.where(first, 0.0, t), axis=1, keepdims=True))


def _ssd_in_specs(rev):
    def ci(c):
        return c if rev is None else rev - c
    return [pl.BlockSpec((CHUNK, CONV_DIM), lambda c: (ci(c), 0)),
            pl.BlockSpec((CHUNK, LANE), lambda c: (ci(c), 0)),
            pl.BlockSpec((SSD_H, CHUNK), lambda c: (0, ci(c))),
            pl.BlockSpec((1, LANE), lambda c: (0, 0)), pl.BlockSpec((SSD_H, 1), lambda c: (0, 0)),
            pl.BlockSpec((1, LANE), lambda c: (0, 0)), pl.BlockSpec((SSD_H, 1), lambda c: (0, 0)),
            pl.BlockSpec((SSD_PAIRS, 1, LANE), lambda c: (0, 0, 0))]


def _ssd_fwd(xbc, small, dtt, bias_r, bias_c, alog_r, alog_c, dsk, name="ssd_fwd"):
    s_ = xbc.shape[0]
    nc = s_ // CHUNK

    def body(x_ref, dt_ref, dtt_ref, br_ref, bc_ref, ar_ref, ac_ref, dsk_ref, y_ref, prev_ref, state):
        @pl.when(pl.program_id(0) == 0)
        def _():
            state[...] = jnp.zeros_like(state)

        lower, _, _, dtc, _, acum, acum_t = _ssd_chunk_common(dt_ref, dtt_ref, br_ref, bc_ref, ar_ref, ac_ref)
        for g in range(SSD_G):
            bb = _mx(x_ref[:, _gs(B_OFF, g)])
            cb_ = _mx(x_ref[:, _gs(C_OFF, g)])
            cbm = _dot(cb_, bb, 1, 1)
            for e in range(PAIRS_PER_GROUP):
                q = g * PAIRS_PER_GROUP + e
                _, acol0, dcol0, alast0, decay0 = _head_terms(2 * q, lower, dtc, acum, acum_t)
                _, acol1, dcol1, alast1, decay1 = _head_terms(2 * q + 1, lower, dtc, acum, acum_t)
                x = x_ref[:, _ps(q)]
                xdt = x * _lanes(dcol0, dcol1)
                xb = _mx(xdt)
                yd = _lanes(_dot(_mx(cbm * decay0), xb, 1, 0), _dot(_mx(cbm * decay1), xb, 1, 0))
                prev = state[q]
                prev_ref[0, q] = prev
                yo = _dot(cb_, _mx(prev), 1, 1) * _lanes(jnp.exp(acol0), jnp.exp(acol1))
                ds = _lanes(jnp.exp(alast0 - acol0), jnp.exp(alast1 - acol1))
                st = _dot(_mx(xdt * ds), bb, 0, 0)
                state[q] = prev * _rows(jnp.exp(alast0), jnp.exp(alast1)) + st
                y_ref[:, _ps(q)] = yd + yo + x * dsk_ref[q]

    psp = pl.BlockSpec((1, SSD_PAIRS, LANE, SSD_N), lambda c: (c, 0, 0, 0))
    return pl.pallas_call(
        body, name=name, grid=(nc,),
        in_specs=_ssd_in_specs(None), out_specs=[pl.BlockSpec((CHUNK, SSD_W), lambda c: (c, 0)), psp],
        out_shape=[jax.ShapeDtypeStruct((s_, SSD_W), F32),
                   jax.ShapeDtypeStruct((nc, SSD_PAIRS, LANE, SSD_N), F32)],
        scratch_shapes=[pltpu.VMEM((SSD_PAIRS, LANE, SSD_N), F32)],
        compiler_params=_params("arbitrary"),
    )(xbc, small, dtt, bias_r, bias_c, alog_r, alog_c, dsk)


def _ssd_bwd(xbc, small, dtt, bias_r, bias_c, alog_r, alog_c, dsk, prev, dy, name="ssd_bwd"):
    s_ = xbc.shape[0]
    nc = s_ // CHUNK

    def body(x_ref, dt_ref, dtt_ref, br_ref, bc_ref, ar_ref, ac_ref, dsk_ref, prev_ref, dy_ref,
             dx_ref, ddt_ref, dpar_ref, dstate):
        @pl.when(pl.program_id(0) == 0)
        def _():
            dstate[...] = jnp.zeros_like(dstate)
            dpar_ref[...] = jnp.zeros_like(dpar_ref)

        lower, upper_b, zr, dtc, a_row, acum, acum_t = _ssd_chunk_common(
            dt_ref, dtt_ref, br_ref, bc_ref, ar_ref, ac_ref)
        strict = (lax.broadcasted_iota(jnp.int32, (CHUNK, CHUNK), 1)
                  < lax.broadcasted_iota(jnp.int32, (CHUNK, CHUNK), 0))
        strict_b = strict.astype(BF16)
        col2 = lax.broadcasted_iota(jnp.int32, (CHUNK, 2 * CHUNK), 1)
        strict2 = (jnp.where(col2 >= CHUNK, col2 - CHUNK, col2)
                   < lax.broadcasted_iota(jnp.int32, (CHUNK, 2 * CHUNK), 0))
        da_in = jnp.zeros((CHUNK, LANE), F32)
        r_off = jnp.zeros((CHUNK, LANE), F32)
        c_int = jnp.zeros((CHUNK, LANE), F32)
        c_row = jnp.zeros((1, LANE), F32)
        ddt = jnp.zeros((CHUNK, LANE), F32)
        dskip = jnp.zeros((1, LANE), F32)
        for g in range(SSD_G):
            bb = _mx(x_ref[:, _gs(B_OFF, g)])
            cb_ = _mx(x_ref[:, _gs(C_OFF, g)])
            cbm = _dot(cb_, bb, 1, 1)
            dcb = jnp.zeros((CHUNK, CHUNK), F32)
            dc_acc = jnp.zeros((CHUNK, SSD_N), F32)
            db_acc = jnp.zeros((CHUNK, SSD_N), F32)
            for e in range(PAIRS_PER_GROUP):
                q = g * PAIRS_PER_GROUP + e
                oh0, acol0, dcol0, alast0, decay0 = _head_terms(2 * q, lower, dtc, acum, acum_t)
                oh1, acol1, dcol1, alast1, decay1 = _head_terms(2 * q + 1, lower, dtc, acum, acum_t)
                x = x_ref[:, _ps(q)]
                dy = dy_ref[:, _ps(q)]
                dcol = _lanes(dcol0, dcol1)
                xdt = x * dcol
                xb = _mx(xdt)
                eacol = _lanes(jnp.exp(acol0), jnp.exp(acol1))
                ds = _lanes(jnp.exp(alast0 - acol0), jnp.exp(alast1 - acol1))
                ealast = _rows(jnp.exp(alast0), jnp.exp(alast1))
                dyb = _mx(dy)
                dyb0, dyb1 = _mx(_lanes(dy, 0.0)), _mx(_lanes(0.0, dy))
                dsh = dstate[q]
                dshb = _mx(dsh)
                prev = prev_ref[0, q]
                prevb = _mx(prev)
                dxdt_inter = ds * _dot(bb, dshb, 1, 1)
                dxdt = _lanes(_dot(_mx(cbm * decay0), dyb, 0, 0), _dot(_mx(cbm * decay1), dyb, 0, 0)) + dxdt_inter
                dwl0 = _dot(dyb0, xb, 1, 1) * decay0
                dwl1 = _dot(dyb1, xb, 1, 1) * decay1
                dcb = dcb + dwl0 + dwl1
                dyeb = _mx(dy * eacol)
                dc_acc = dc_acc + _dot(dyeb, prevb, 1, 0)
                db_acc = db_acc + _dot(_mx(xdt * ds), dshb, 1, 0)
                dstate[q] = _dot(dyeb, cb_, 0, 0) + ealast * dsh
                above = _exact_dot(upper_b, jnp.concatenate([dwl0 * cbm, dwl1 * cbm], axis=1), 1, 0, False)
                above = jnp.where(strict2, above, 0.0)
                da_in = (da_in + jnp.sum(above[:, :CHUNK], axis=1, keepdims=True) * oh0
                         + jnp.sum(above[:, CHUNK:], axis=1, keepdims=True) * oh1)
                y_off = _dot(cb_, prevb, 1, 1) * eacol
                r0, r1 = _lane_halves(dy * y_off)
                r_off = r_off + r0 * oh0 + r1 * oh1
                c0, c1 = _lane_halves(xdt * dxdt_inter)
                c_int = c_int + c0 * oh0 + c1 * oh1
                both = jnp.sum(dsh * prev, axis=1, keepdims=True) * ealast
                c_row = (c_row + jnp.sum(_rows(both, 0.0), axis=0, keepdims=True) * oh0
                         + jnp.sum(_rows(0.0, both), axis=0, keepdims=True) * oh1)
                t0, t1 = _lane_halves(dxdt * x)
                ddt = ddt + t0 * oh0 + t1 * oh1
                dx_ref[:, _ps(q)] = dxdt * dcol + dy * dsk_ref[q]
                k0, k1 = _lane_halves(dy * x)
                dskip = (dskip + jnp.sum(k0, axis=0, keepdims=True) * oh0 + jnp.sum(k1, axis=0, keepdims=True) * oh1)
            dcbb = _mx(dcb)
            dx_ref[:, _gs(C_OFF, g)] = dc_acc + _dot(dcbb, bb, 1, 0)
            dx_ref[:, _gs(B_OFF, g)] = db_acc + _dot(dcbb, cb_, 0, 0)
        da = (da_in + _exact_dot(upper_b, r_off, 1, 0, False) + _exact_dot(strict_b, c_int, 1, 0, False) + c_row)
        draw = (ddt + da * a_row) * _sigmoid(zr)
        ddt_ref[...] = draw
        dpar_ref[0:1, :] += jnp.sum(draw, axis=0, keepdims=True)
        dpar_ref[1:2, :] += jnp.sum(da * dtc, axis=0, keepdims=True) * a_row
        dpar_ref[2:3, :] += dskip

    rev = nc - 1
    psp = pl.BlockSpec((1, SSD_PAIRS, LANE, SSD_N), lambda c: (rev - c, 0, 0, 0))
    return pl.pallas_call(
        body, name=name, grid=(nc,),
        in_specs=_ssd_in_specs(rev) + [psp, pl.BlockSpec((CHUNK, SSD_W), lambda c: (rev - c, 0))],
        out_specs=[pl.BlockSpec((CHUNK, CONV_DIM), lambda c: (rev - c, 0)),
                   pl.BlockSpec((CHUNK, LANE), lambda c: (rev - c, 0)), pl.BlockSpec((8, LANE), lambda c: (0, 0))],
        out_shape=[jax.ShapeDtypeStruct((s_, CONV_DIM), F32), jax.ShapeDtypeStruct((s_, LANE), F32),
                   jax.ShapeDtypeStruct((8, LANE), F32)],
        scratch_shapes=[pltpu.VMEM((SSD_PAIRS, LANE, SSD_N), F32)],
        compiler_params=_params("arbitrary"),
    )(xbc, small, dtt, bias_r, bias_c, alog_r, alog_c, dsk, prev, dy)


GN = SSD_W // SSD_G


def _gated_norm_fwd(y, z, w, cat, name="gated_norm_fwd"):
    s_, f = y.shape
    tr = _row_tile(s_)

    def body(y_ref, z_ref, w_ref, cat_ref, o_ref):
        for g in range(SSD_G):
            sl = slice(g * GN, (g + 1) * GN)
            gg = y_ref[:, sl] * _silu(z_ref[:, sl])
            r = lax.rsqrt(jnp.mean(gg * gg, axis=-1, keepdims=True) + EPS)
            o_ref[:, sl] = (gg * r * w_ref[:, sl]).astype(o_ref.dtype)

    row = pl.BlockSpec((tr, f), lambda i: (i, 0))
    wsp = pl.BlockSpec((1, f), lambda i: (0, 0))
    return pl.pallas_call(
        body, name=name, grid=(s_ // tr,),
        in_specs=[row, row, wsp, pl.BlockSpec(memory_space=pl.ANY)], out_specs=pl.BlockSpec((tr, f), lambda i: (i, 1)),
        out_shape=jax.ShapeDtypeStruct(cat.shape, cat.dtype), input_output_aliases={3: 0},
        compiler_params=_params("parallel"),
    )(y, z, w.reshape(1, f), cat)


def _gated_norm_bwd(y, z, w, dout, name="gated_norm_bwd"):
    s_, f = y.shape
    tr = _row_tile(s_)

    def body(y_ref, z_ref, w_ref, do_ref, dy_ref, dz_ref, dw_ref):
        @pl.when(pl.program_id(0) == 0)
        def _():
            dw_ref[...] = jnp.zeros_like(dw_ref)

        for g in range(SSD_G):
            sl = slice(g * GN, (g + 1) * GN)
            yv = y_ref[:, sl]
            zv = z_ref[:, sl]
            dov = do_ref[:, sl].astype(F32)
            sz = _silu(zv)
            gg = yv * sz
            r = lax.rsqrt(jnp.mean(gg * gg, axis=-1, keepdims=True) + EPS)
            gw = dov * w_ref[:, sl]
            c = jnp.mean(gw * gg, axis=-1, keepdims=True)
            dgg = r * gw - gg * (r * r * r * c)
            dy_ref[:, sl] = dgg * sz
            dz_ref[:, sl] = (dgg * yv * _dsilu(zv)).astype(dz_ref.dtype)
            dw_ref[:, sl] += jnp.sum(dov * gg * r, axis=0, keepdims=True)

    row = pl.BlockSpec((tr, f), lambda i: (i, 0))
    wsp = pl.BlockSpec((1, f), lambda i: (0, 0))
    return pl.pallas_call(
        body, name=name, grid=(s_ // tr,),
        in_specs=[row, row, wsp, pl.BlockSpec((tr, f), lambda i: (i, 1))], out_specs=[row, row, wsp],
        out_shape=[jax.ShapeDtypeStruct((s_, f), F32), jax.ShapeDtypeStruct((s_, f), MXU_DTYPE),
                   jax.ShapeDtypeStruct((1, f), F32)],
        compiler_params=_params("arbitrary"),
    )(y, z, w.reshape(1, f), dout)


def _ffn_fwd(vv, w_gate, w_up, name="ffn_gate_up"):
    s_, d = vv.shape
    nb, f8, _ = w_gate.shape
    tm = _pick(s_, (1024, 512, 256, 128))

    def body(v_ref, wg_ref, wu_ref, g_ref, u_ref, a_ref):
        for rs in _row_slices(tm, 16):
            a = _mx(v_ref[rs, :])
            g = _dot(a, _mx(wg_ref[...]), 1, 1)
            u = _dot(a, _mx(wu_ref[...]), 1, 1)
            s = _sigmoid(g)
            gs = g * s
            g_ref[rs, :] = (u * (s * (1.0 + g * (1.0 - s)))).astype(g_ref.dtype)
            u_ref[rs, :] = gs.astype(u_ref.dtype)
            a_ref[rs, :] = (gs * u).astype(a_ref.dtype)

    wsp = pl.BlockSpec((None, f8, d), lambda j, i: (j, 0, 0))
    osp = pl.BlockSpec((None, tm, f8), lambda j, i: (j, i, 0))
    return pl.pallas_call(
        body, name=name, grid=(nb, s_ // tm),
        in_specs=[pl.BlockSpec((tm, d), lambda j, i: (i, 0)), wsp, wsp], out_specs=[osp] * 3,
        out_shape=[jax.ShapeDtypeStruct((nb, s_, f8), MXU_DTYPE)] * 3,
        compiler_params=_params("parallel", "parallel"),
    )(vv, w_gate, w_up)


def _ffn_bwd_act(dffn, w_down, gate, up, name="ffn_d_act"):
    s_, d = dffn.shape
    nb, f8, _ = w_down.shape
    tm = _pick(s_, (1024, 512, 256, 128))

    def body(d_ref, w_ref, g_ref, u_ref, dg_ref, du_ref):
        for rs in _row_slices(tm, 16):
            dact = _dot(_mx(d_ref[rs, :]), _mx(w_ref[...]), 1, 1)
            dg_ref[rs, :] = (dact * g_ref[rs, :].astype(F32)).astype(dg_ref.dtype)
            du_ref[rs, :] = (dact * u_ref[rs, :].astype(F32)).astype(du_ref.dtype)

    osp = pl.BlockSpec((None, tm, f8), lambda j, i: (j, i, 0))
    return pl.pallas_call(
        body, name=name, grid=(nb, s_ // tm),
        in_specs=[pl.BlockSpec((tm, d), lambda j, i: (i, 0)), pl.BlockSpec((None, f8, d), lambda j, i: (j, 0, 0)),
                  osp, osp],
        out_specs=[osp, osp], out_shape=[jax.ShapeDtypeStruct((nb, s_, f8), MXU_DTYPE)] * 2,
        compiler_params=_params("parallel", "parallel"),
    )(dffn, w_down, gate, up)


def _ffn_bwd_in(dgate, w_gate, dup, w_up, name="ffn_d_in"):
    nb, s_, f8 = dgate.shape
    d = w_gate.shape[2]
    tm = _pick(s_, (1024, 512, 256, 128))
    tn = _pick(d, (2048, 1024, 512, 256, 128))

    def body(dg_ref, wg_ref, du_ref, wu_ref, o_ref, acc):
        j = pl.program_id(2)

        @pl.when(j == 0)
        def _():
            acc[...] = jnp.zeros_like(acc)

        for rs in _row_slices(tm, 16):
            acc[rs, :] += (_dot(_mx(dg_ref[rs, :]), _mx(wg_ref[...]), 1, 0)
                           + _dot(_mx(du_ref[rs, :]), _mx(wu_ref[...]), 1, 0))

        @pl.when(j == nb - 1)
        def _():
            o_ref[...] = acc[...].astype(o_ref.dtype)

    asp = pl.BlockSpec((None, tm, f8), lambda i, n, j: (j, i, 0))
    wsp = pl.BlockSpec((None, f8, tn), lambda i, n, j: (j, 0, n))
    return pl.pallas_call(
        body, name=name, grid=(s_ // tm, d // tn, nb),
        in_specs=[asp, wsp, asp, wsp], out_specs=pl.BlockSpec((tm, tn), lambda i, n, j: (i, n)),
        out_shape=jax.ShapeDtypeStruct((s_, d), MXU_DTYPE), scratch_shapes=[pltpu.VMEM((tm, tn), F32)],
        compiler_params=_params("parallel", "parallel", "arbitrary"),
    )(dgate, w_gate, dup, w_up)


def _adam_math(g, w, m, v):
    m2 = ADAM_B1 * m + (1.0 - ADAM_B1) * g
    v2 = ADAM_B2 * v + (1.0 - ADAM_B2) * (g * g)
    m_hat = m2 / (1.0 - ADAM_B1 ** ADAM_STEP)
    v_hat = v2 / (1.0 - ADAM_B2 ** ADAM_STEP)
    delta = -ADAM_LR * (m_hat / (jnp.sqrt(v_hat) + ADAM_EPS) + ADAM_WD * w)
    return delta, m2, v2


def _adamw(parts, own, me, w, m, v, name="adamw"):
    nd, r_, c = parts.shape
    tr = _pick(r_, (128, 64, 32, 16))
    tc = c
    if tr == r_ and r_ > 128:
        tc = _pick(c, (256, 128))

    def body(me_ref, p_ref, own_ref, w_ref, m_ref, v_ref, g_ref, d_ref, m2_ref, v2_ref):
        mine = me_ref[0]
        g = jnp.zeros((tr, tc), F32)
        for i in range(nd):
            g = g + jnp.where(mine == i, own_ref[...], p_ref[i]).astype(F32)
        delta, m2, v2 = _adam_math(g, w_ref[...], m_ref[...], v_ref[...])
        g_ref[...] = g
        d_ref[...] = delta
        m2_ref[...] = m2
        v2_ref[...] = v2

    row = pl.BlockSpec((tr, tc), lambda i, j, me_: (i, j))
    gs = pltpu.PrefetchScalarGridSpec(
        num_scalar_prefetch=1, grid=(r_ // tr, c // tc),
        in_specs=[pl.BlockSpec((nd, tr, tc), lambda i, j, me_: (0, i, j)),
                  pl.BlockSpec((None, tr, tc), lambda i, j, me_: (me_[0], i, j)), row, row, row],
        out_specs=[row] * 4)
    return pl.pallas_call(
        body, name=name, grid_spec=gs, out_shape=[jax.ShapeDtypeStruct((r_, c), F32)] * 4,
        compiler_params=_params("parallel", "parallel"),
    )(me, parts, own, w, m, v)


def _adamw_small(parts, w, m, v, name="adamw_small"):
    nd = parts.shape[0]

    def body(p_ref, w_ref, m_ref, v_ref, g_ref, d_ref, m2_ref, v2_ref):
        g = p_ref[0]
        for i in range(1, nd):
            g = g + p_ref[i]
        delta, m2, v2 = _adam_math(g, w_ref[...], m_ref[...], v_ref[...])
        g_ref[...] = g
        d_ref[...] = delta
        m2_ref[...] = m2
        v2_ref[...] = v2

    return pl.pallas_call(
        body, name=name, out_shape=[jax.ShapeDtypeStruct(w.shape, F32)] * 4,
        compiler_params=pltpu.CompilerParams(vmem_limit_bytes=VMEM_LIMIT_BYTES),
    )(parts, w, m, v)


_HBM = pl.BlockSpec(memory_space=pltpu.HBM)
_MESH = pl.DeviceIdType.MESH


def _all_gather(xs, name):
    na = len(xs)

    def body(*refs):
        x_refs, out_refs = refs[:na], refs[na:2 * na]
        send_sems, recv_sems, local_sems = refs[2 * na:]
        x, y, c = lax.axis_index("x"), lax.axis_index("y"), lax.axis_index("c")
        me, sibling = (x, y, c), (x, y, 1 - c)
        near = [(1 - x, y), (x, 1 - y)]
        chips = near + [(1 - x, 1 - y)]
        relay_from = (x + c * (1 - 2 * x), y + (1 - c) * (1 - 2 * y))
        relay_to = (x + (1 - c) * (1 - 2 * x), y + c * (1 - 2 * y))

        def slot(a, px, py, pc):
            return out_refs[a].at[4 * px + 2 * py + pc]

        def copy(a, k, block, to, src=None):
            return pltpu.make_async_remote_copy(
                src_ref=slot(a, *block) if src is None else src, dst_ref=slot(a, *block),
                send_sem=send_sems.at[a, k], recv_sem=recv_sems.at[a, k], device_id=to, device_id_type=_MESH)

        mine = [pltpu.make_async_copy(x_refs[a], slot(a, *me), local_sems.at[a]) for a in range(na)]
        started = []
        for a in range(na):
            mine[a].start()
            first = [copy(a, 0, me, sibling, src=x_refs[a])]
            first += [copy(a, 1 + j, me, (*chip, c), src=x_refs[a]) for j, chip in enumerate(near)]
            for cp in first:
                cp.start()
            started += first
        for a in range(na):
            for j, chip in enumerate(chips):
                copy(a, 1 + j, (*chip, c), me).wait_recv()
                fwd = copy(a, 4 + j, (*chip, c), sibling)
                fwd.start()
                started.append(fwd)
                if j == len(near) - 1:
                    relay = copy(a, 1 + len(near), (*relay_from, c), (*relay_to, c))
                    relay.start()
                    started.append(relay)
        for a in range(na):
            copy(a, 0, sibling, me).wait_recv()
            for j, chip in enumerate(chips):
                copy(a, 4 + j, (*chip, 1 - c), me).wait_recv()
        for cp in started:
            cp.wait_send()
        for cp in mine:
            cp.wait()

    return pl.pallas_call(
        body, name=name, out_shape=[jax.ShapeDtypeStruct((N_DEV,) + t.shape, t.dtype) for t in xs],
        in_specs=[_HBM] * na, out_specs=[_HBM] * na,
        scratch_shapes=[pltpu.SemaphoreType.DMA((na, 7)), pltpu.SemaphoreType.DMA((na, 7)),
                        pltpu.SemaphoreType.DMA((na,))],
    )(*xs)


_SEM = pl.BlockSpec(memory_space=pltpu.SEMAPHORE)
_EFFECT = pltpu.SideEffectType.DATAFLOW_SIDE_EFFECTING


def _peers(x, y, c):
    out = []
    for k in range(1, N_DEV):
        px = 1 - x if k & 4 else x
        py = 1 - y if k & 2 else y
        pc = 1 - c if k & 1 else c
        out.append(((px, py, pc), 4 * px + 2 * py + pc))
    return out


def _push_copies(scatter, src_refs, land_refs, send_sems, recv_sems):
    x, y, c = lax.axis_index("x"), lax.axis_index("y"), lax.axis_index("c")
    me = 4 * x + 2 * y + c
    pairs = []
    for a, (src, land) in enumerate(zip(src_refs, land_refs)):
        for k, (peer, slot) in enumerate(_peers(x, y, c)):
            out_src = src.at[slot] if scatter else src
            si = a * (N_DEV - 1) + k
            send = pltpu.make_async_remote_copy(src_ref=out_src, dst_ref=land.at[me], send_sem=send_sems.at[si],
                                                recv_sem=recv_sems.at[si], device_id=peer, device_id_type=_MESH)
            recv = pltpu.make_async_remote_copy(src_ref=out_src, dst_ref=land.at[slot], send_sem=send_sems.at[si],
                                                recv_sem=recv_sems.at[si], device_id=peer, device_id_type=_MESH)
            pairs.append((send, recv))
    return pairs


def _push_start(srcs, scatter, dep, name):
    na = len(srcs)
    shapes = [t.shape[1:] if scatter else t.shape for t in srcs]
    lands = [pltpu.with_memory_space_constraint(lax.empty((N_DEV,) + s, t.dtype), pltpu.HBM) for s, t in zip(shapes, srcs)]

    def body(*refs):
        src_refs, land_refs = refs[:na], refs[na:2 * na]
        send_sems, recv_sems = refs[2 * na + 1], refs[2 * na + 2]
        token = refs[-1]
        for send, _ in _push_copies(scatter, src_refs, land_refs, send_sems, recv_sems):
            send.start()
        token[...] = jnp.zeros_like(token)

    sem = pltpu.SemaphoreType.DMA((na * (N_DEV - 1),))
    outs = pl.pallas_call(
        body, name=name,
        out_shape=(sem, sem) + tuple(pltpu.HBM(t.shape, t.dtype) for t in srcs)
        + tuple(pltpu.HBM(t.shape, t.dtype) for t in lands) + (jax.ShapeDtypeStruct((8, LANE), F32),),
        in_specs=[_HBM] * (2 * na) + [pl.BlockSpec(memory_space=pl.ANY)],
        out_specs=(_SEM, _SEM) + (_HBM,) * (2 * na) + (pl.BlockSpec(memory_space=pltpu.VMEM),),
        input_output_aliases={i: 2 + i for i in range(2 * na)},
        compiler_params=pltpu.CompilerParams(has_side_effects=_EFFECT),
    )(*[pltpu.with_memory_space_constraint(t, pltpu.HBM) for t in srcs], *lands, dep)
    return outs[0], outs[1], outs[2:2 + na], outs[2 + na:2 + 2 * na], outs[-1]


def _push_wait(send_sems, recv_sems, src_thru, land_thru, scatter, after, name):
    na = len(src_thru)

    def body(*refs):
        src_refs, land_refs = refs[:na], refs[na:2 * na]
        ssem, rsem = refs[2 * na], refs[2 * na + 1]
        for send, recv in _push_copies(scatter, src_refs, land_refs, ssem, rsem):
            send.wait_send()
            recv.wait_recv()

    outs = pl.pallas_call(
        body, name=name,
        out_shape=tuple(pltpu.HBM(t.shape, t.dtype) for t in src_thru) + tuple(pltpu.HBM(t.shape, t.dtype) for t in land_thru),
        in_specs=[_HBM] * (2 * na) + [_SEM, _SEM, pl.BlockSpec(memory_space=pl.ANY)],
        out_specs=(_HBM,) * (2 * na),
        input_output_aliases={i: i for i in range(2 * na)},
        compiler_params=pltpu.CompilerParams(has_side_effects=_EFFECT),
    )(*src_thru, *land_thru, send_sems, recv_sems, after)
    return outs[:na], outs[na:]


def _exchange_behind(srcs, scatter, dep, name):
    send_sems, recv_sems, thru, lands, token = _push_start(srcs, scatter, dep, name + "_start")

    def finish(after, place=True):
        src_done, land_done = _push_wait(send_sems, recv_sems, thru, lands, scatter, after, name + "_wait")
        if not place:
            return land_done, src_done
        return _place_own(land_done, src_done, scatter, name + "_own")

    return token[0, 0], finish


def _place_own(lands, srcs, scatter, name):
    me = (4 * lax.axis_index("x") + 2 * lax.axis_index("y") + lax.axis_index("c")).astype(jnp.int32).reshape(1)
    outs = []
    for a, (land, src) in enumerate(zip(lands, srcs)):
        r_, c_ = land.shape[1:]
        tr = _pick(r_, (512, 256, 128, 64, 32, 16))

        def body(me_ref, land_ref, src_ref, out_ref):
            out_ref[...] = src_ref[...]

        src_spec = (pl.BlockSpec((None, tr, c_), lambda i, me_: (me_[0], i, 0)) if scatter
                    else pl.BlockSpec((tr, c_), lambda i, me_: (i, 0)))
        gs = pltpu.PrefetchScalarGridSpec(
            num_scalar_prefetch=1, grid=(r_ // tr,),
            in_specs=[pl.BlockSpec(memory_space=pl.ANY), src_spec],
            out_specs=pl.BlockSpec((None, tr, c_), lambda i, me_: (me_[0], i, 0)))
        outs.append(pl.pallas_call(
            body, name=f"{name}_{a}", grid_spec=gs, out_shape=jax.ShapeDtypeStruct(land.shape, land.dtype),
            input_output_aliases={1: 0}, compiler_params=_params("arbitrary"),
        )(me, land, src))
    return outs


_TRANSPOSED = ("w_in", "w_uq", "w_gate", "w_up")
_CQKV = (0, Q_RANK + KV_RANK)
_KR = (_CQKV[1], _CQKV[1] + ROPE)
_Z = (_KR[1], _KR[1] + SSD_W)
_XBC = (_Z[1], _Z[1] + CONV_DIM)
_DT = (_XBC[1], _XBC[1] + SSD_H)


def _win_segments(w_in_t):
    w = w_in_t.reshape(D_IN, D_MODEL)
    small = jnp.concatenate([w[_KR[0]:_KR[1]], w[_DT[0]:_DT[1]],
                             jnp.zeros((LANE - ROPE - SSD_H, D_MODEL), w.dtype)], axis=0)
    return w[_CQKV[0]:_CQKV[1]], w[_Z[0]:_Z[1]], w[_XBC[0]:_XBC[1]], small


def _win_from_segments(g_cqkv, g_z, g_xbc, g_small):
    w = jnp.concatenate([g_cqkv, g_small[:ROPE], g_z, g_xbc, g_small[ROPE:ROPE + SSD_H]], axis=0)
    return w.reshape(N_DEV, D_IN // N_DEV, D_MODEL)


_SMALL = (("q_norm_w", 512), ("kv_norm_w", 512), ("conv_b", CONV_DIM), ("dt_bias", SSD_H), ("a_log", SSD_H),
          ("d_skip", SSD_H), ("ssd_norm_w", SSD_W), ("attn_out_norm_w", 1024), ("pre_mix_norm_w", D_MODEL),
          ("post_mix_norm_w", D_MODEL), ("pre_ffn_norm_w", D_MODEL), ("post_ffn_norm_w", D_MODEL),
          ("conv_w", CONV_K * CONV_DIM))
_SMALL_ROWS = -(-sum(-(-n // LANE) for _, n in _SMALL) // 8) * 8


def _pack_small(vals):
    rows = []
    for name, n in _SMALL:
        v = vals[name].reshape(-1).astype(F32)
        pad = -(-n // LANE) * LANE
        rows.append(jnp.pad(v, (0, pad - n)).reshape(-1, LANE))
    m = jnp.concatenate(rows, axis=0)
    return jnp.pad(m, ((0, _SMALL_ROWS - m.shape[0]), (0, 0)))


def _unpack_small(m):
    out, r = {}, 0
    for name, n in _SMALL:
        nr = -(-n // LANE)
        out[name] = m[r:r + nr].reshape(-1)[:n]
        r += nr
    return out


def _head_row(v):
    return jnp.pad(v.reshape(1, -1).astype(F32), ((0, 0), (HEAD_LANE, LANE - HEAD_LANE - v.shape[-1])))


def _local_step(x, positions, target, wg, small, weights, on_grads):
    w_cqkv, w_z, w_xbc, w_small = _win_segments(wg["w_in"])
    conv_w = wg["conv_w"]
    conv_b = small["conv_b"].reshape(1, CONV_DIM)
    qkv_norm_w = jnp.concatenate([small["q_norm_w"], small["kv_norm_w"]])
    attn_norm_w = small["attn_out_norm_w"].reshape(1, HEADS * VDIM)
    scale = QK ** -0.5

    inv_freq = ROPE_THETA ** (-jnp.arange(0, ROPE, 2, dtype=F32) / ROPE)
    ang = positions.astype(F32)[:, None] * inv_freq
    cos2 = jnp.tile(jnp.cos(ang), (1, 2))
    sin2 = jnp.tile(jnp.sin(ang), (1, 2))

    u = _rms_fwd(x, small["pre_mix_norm_w"], out_dtype=MXU_DTYPE, name="pre_mix_norm")
    cqkv = _mm(u, w_cqkv, "nt", name="in_proj_qkv")
    z = _mm(u, w_z, "nt", name="in_proj_z")
    xbc = _mm(u, w_xbc, "nt", name="in_proj_xbc")
    sm = _mm(u, w_small, "nt", name="in_proj_small")

    w_uq, w_ukv = weights("qkv_up", cqkv)
    qkvn = _rms_fwd(cqkv, qkv_norm_w, groups=2, out_dtype=MXU_DTYPE, name="qkv_norm")
    q_h = _q_up(qkvn, w_uq, cos2, sin2, scale)
    k_h, v_h = _kv_up(qkvn, w_ukv, sm, cos2, sin2)
    o_h, lse = _flash_fwd(q_h, k_h, v_h)
    cat = _hnorm_fwd(o_h, attn_norm_w, D_MODEL)
    w_out = weights("out", o_h)[0].reshape(D_MODEL, D_MODEL)

    xbc_act = _conv_fwd(xbc, conv_w, conv_b)
    dtt = jnp.transpose(sm[:, HEAD_LANE:HEAD_LANE + SSD_H])
    ssd_args = (xbc_act, sm, dtt, _head_row(small["dt_bias"]), small["dt_bias"].reshape(SSD_H, 1),
                _head_row(small["a_log"]), small["a_log"].reshape(SSD_H, 1),
                jnp.broadcast_to(small["d_skip"].reshape(SSD_H, 1), (SSD_H, SSD_P)).reshape(SSD_PAIRS, 1, LANE))
    y_ssd, prev = _ssd_fwd(*ssd_args)
    cat = _gated_norm_fwd(y_ssd, z, small["ssd_norm_w"], cat)

    mix = _mm(cat, w_out, "nn", out_dtype=MXU_DTYPE, name="out_proj")
    h1, vv = _norm_res_norm(mix, x, small["post_mix_norm_w"], small["pre_ffn_norm_w"])

    w_gate, w_up = weights("ffn_in", mix)
    gate, up, act = _ffn_fwd(vv, w_gate, w_up)
    w_down, = weights("ffn_out", act)
    ffn = _mm(act, w_down, "nn", a_blk=True, b_blk=True, fuse=2, wide=True, out_dtype=MXU_DTYPE, name="ffn_down")
    loss_blk, dy, dffn, g_post_ffn = _loss_head(ffn, h1, target, small["post_ffn_norm_w"])

    g_down = _mm(act, dffn, "tn", a_blk=True, out_blk=True, out_dtype=MXU_DTYPE, name="g_down")
    dgate, dup = _ffn_bwd_act(dffn, w_down, gate, up)
    dvv = _ffn_bwd_in(dgate, w_gate, dup, w_up)
    g_gate = _mm(dgate, vv, "tn", a_blk=True, out_blk=True, out_dtype=MXU_DTYPE, name="g_gate")
    g_up = _mm(dup, vv, "tn", a_blk=True, out_blk=True, out_dtype=MXU_DTYPE, name="g_up")
    pre_ffn_w = small["pre_ffn_norm_w"] + on_grads("ffn", [g_gate, g_up, g_down])
    dh1, dmix, g_pre_ffn, g_post_mix = _norm_res_norm_bwd(h1, pre_ffn_w, dvv, dy, mix, small["post_mix_norm_w"])

    dcat = _mm(dmix, w_out, "nt", out_dtype=MXU_DTYPE, name="d_cat")
    g_out = _mm(cat, dmix, "tn", out_dtype=MXU_DTYPE, name="g_out")

    do_h, delta, g_attn_norm = _hnorm_bwd(o_h, attn_norm_w, dcat)
    dq_h, dk_h, dv_h = _flash_bwd(q_h, k_h, v_h, do_h, lse, delta)
    dq = _q_prep(dq_h, cos2, -sin2, scale, name="dq_post")

    dy_ssd, dz, g_ssd_norm = _gated_norm_bwd(y_ssd, z, small["ssd_norm_w"], dcat)
    dxbc_act, ddt, dpar = _ssd_bwd(*ssd_args, prev, dy_ssd)
    dkv, dsm = _dkv_post(dk_h, dv_h, ddt, cos2, -sin2)
    dpre, dwb = _conv_bwd_pre(xbc, conv_w, conv_b, dxbc_act)
    dxbc = _conv_bwd_in(dpre, conv_w)

    dqn = _mm(dq, w_uq, "nn", a_blk=True, b_blk=True, fuse=HEADS, name="d_qn")
    dkvn = _mm(dkv, w_ukv, "nt", a_blk=True, b_blk=True, fuse=HEADS, name="d_kvn")
    g_uq = _mm(dq, qkvn, "tn", a_blk=True, out_blk=True, b_cols=(0, Q_RANK), out_dtype=MXU_DTYPE, name="g_uq")
    g_ukv = _mm(qkvn, dkv, "tn", b_blk=True, out_blk=True, a_cols=(Q_RANK, KV_RANK), out_dtype=MXU_DTYPE, name="g_ukv")
    heads_token = on_grads("heads", [g_uq, g_ukv, g_out.reshape(N_DEV, D_MODEL // N_DEV, D_MODEL)])
    dcqkv, g_qkv_norm = _rms_bwd(cqkv, qkv_norm_w + heads_token, [dqn, dkvn], out_dtype=MXU_DTYPE, name="qkv_norm_bwd")

    g_in = _win_from_segments(_mm(dcqkv, u, "tn", out_dtype=MXU_DTYPE, name="g_in_qkv"),
                              _mm(dz, u, "tn", out_dtype=MXU_DTYPE, name="g_in_z"),
                              _mm(dxbc, u, "tn", out_dtype=MXU_DTYPE, name="g_in_xbc"),
                              _mm(dsm, u, "tn", out_dtype=MXU_DTYPE, name="g_in_small"))
    in_token = on_grads("in", [g_in])
    du = _mm_sum([dsm + in_token.astype(dsm.dtype), dcqkv, dz, dxbc], [w_small, w_cqkv, w_z, w_xbc],
                 out_dtype=MXU_DTYPE, name="d_u")
    dx, g_pre_mix = _rms_bwd(x, small["pre_mix_norm_w"], [du], res=dh1, name="pre_mix_norm_bwd")

    hl = slice(HEAD_LANE, HEAD_LANE + SSD_H)
    g_small = {"q_norm_w": g_qkv_norm[0, :Q_RANK], "kv_norm_w": g_qkv_norm[0, Q_RANK:], "conv_b": dwb[CONV_K],
               "dt_bias": dpar[0, hl], "a_log": dpar[1, hl], "d_skip": dpar[2, hl], "ssd_norm_w": g_ssd_norm,
               "attn_out_norm_w": g_attn_norm, "pre_mix_norm_w": g_pre_mix, "post_mix_norm_w": g_post_mix,
               "pre_ffn_norm_w": g_pre_ffn, "post_ffn_norm_w": g_post_ffn, "conv_w": dwb[:CONV_K]}
    return loss_blk[0, 0], dx, g_small


_WEIGHT_ORDER = ("w_in", "q_norm_w", "w_uq", "kv_norm_w", "w_ukv", "conv_w", "conv_b", "dt_bias", "a_log", "d_skip",
                 "ssd_norm_w", "attn_out_norm_w", "w_out", "pre_mix_norm_w", "post_mix_norm_w", "pre_ffn_norm_w",
                 "post_ffn_norm_w", "w_gate", "w_up", "w_down")


def kernel(x, positions, w_in, q_norm_w, w_uq, kv_norm_w, w_ukv, conv_w, conv_b, dt_bias, a_log, d_skip, ssd_norm_w, attn_out_norm_w, w_out, pre_mix_norm_w, post_mix_norm_w, pre_ffn_norm_w, post_ffn_norm_w, w_gate, w_up, w_down, loss_target, m_w_in, m_q_norm_w, m_w_uq, m_kv_norm_w, m_w_ukv, m_conv_w, m_conv_b, m_dt_bias, m_a_log, m_d_skip, m_ssd_norm_w, m_attn_out_norm_w, m_w_out, m_pre_mix_norm_w, m_post_mix_norm_w, m_pre_ffn_norm_w, m_post_ffn_norm_w, m_w_gate, m_w_up, m_w_down, v_w_in, v_q_norm_w, v_w_uq, v_kv_norm_w, v_w_ukv, v_conv_w, v_conv_b, v_dt_bias, v_a_log, v_d_skip, v_ssd_norm_w, v_attn_out_norm_w, v_w_out, v_pre_mix_norm_w, v_post_mix_norm_w, v_pre_ffn_norm_w, v_post_ffn_norm_w, v_w_gate, v_w_up, v_w_down):
    w = dict(w_in=w_in, q_norm_w=q_norm_w, w_uq=w_uq, kv_norm_w=kv_norm_w, w_ukv=w_ukv, conv_w=conv_w, conv_b=conv_b,
             dt_bias=dt_bias, a_log=a_log, d_skip=d_skip, ssd_norm_w=ssd_norm_w, attn_out_norm_w=attn_out_norm_w,
             w_out=w_out, pre_mix_norm_w=pre_mix_norm_w, post_mix_norm_w=post_mix_norm_w,
             pre_ffn_norm_w=pre_ffn_norm_w, post_ffn_norm_w=post_ffn_norm_w, w_gate=w_gate, w_up=w_up, w_down=w_down)
    m = dict(w_in=m_w_in, q_norm_w=m_q_norm_w, w_uq=m_w_uq, kv_norm_w=m_kv_norm_w, w_ukv=m_w_ukv, conv_w=m_conv_w,
             conv_b=m_conv_b, dt_bias=m_dt_bias, a_log=m_a_log, d_skip=m_d_skip, ssd_norm_w=m_ssd_norm_w,
             attn_out_norm_w=m_attn_out_norm_w, w_out=m_w_out, pre_mix_norm_w=m_pre_mix_norm_w,
             post_mix_norm_w=m_post_mix_norm_w, pre_ffn_norm_w=m_pre_ffn_norm_w, post_ffn_norm_w=m_post_ffn_norm_w,
             w_gate=m_w_gate, w_up=m_w_up, w_down=m_w_down)
    v = dict(w_in=v_w_in, q_norm_w=v_q_norm_w, w_uq=v_w_uq, kv_norm_w=v_kv_norm_w, w_ukv=v_w_ukv, conv_w=v_conv_w,
             conv_b=v_conv_b, dt_bias=v_dt_bias, a_log=v_a_log, d_skip=v_d_skip, ssd_norm_w=v_ssd_norm_w,
             attn_out_norm_w=v_attn_out_norm_w, w_out=v_w_out, pre_mix_norm_w=v_pre_mix_norm_w,
             post_mix_norm_w=v_post_mix_norm_w, pre_ffn_norm_w=v_pre_ffn_norm_w, post_ffn_norm_w=v_post_ffn_norm_w,
             w_gate=v_w_gate, w_up=v_w_up, w_down=v_w_down)
    w, m, v = ({k: t[0] for k, t in d.items()} for d in (w, m, v))
    me = 4 * lax.axis_index("x") + 2 * lax.axis_index("y") + lax.axis_index("c")
    groups = {"qkv_up": ("w_uq", "w_ukv"), "out": ("w_out",), "ffn_in": ("w_gate", "w_up"), "ffn_out": ("w_down",)}
    cshard = CONV_DIM // N_DEV
    for name in _TRANSPOSED:
        w[name], m[name], v[name] = w[name].T, m[name].T, v[name].T

    shards = [w["w_in"].astype(MXU_DTYPE),
              jnp.stack(_split3(w["conv_w"])).reshape(3 * CONV_K, cshard).astype(MXU_DTYPE)]
    w_in_g, cw = _all_gather(shards, name="gather_weights")
    cw = cw.astype(F32).reshape(N_DEV, 3, CONV_K, cshard)
    wg = {"w_in": w_in_g, "conv_w": jnp.transpose(cw[:, 0] + cw[:, 1] + cw[:, 2], (1, 0, 2)).reshape(CONV_K, CONV_DIM)}
    arriving, dep, started = {}, wg["conv_w"], jnp.zeros((), F32)
    small = {name: w[name] for name, _ in _SMALL if name != "conv_w"}
    for group in ("qkv_up", "out", "ffn_in", "ffn_out"):
        token, arriving[group] = _exchange_behind([w[name].astype(MXU_DTYPE) for name in groups[group]], False,
                                                  dep, group + "_weights")
        started = started + token
        dep = jnp.zeros((8, LANE), F32) + started
    small["pre_mix_norm_w"] = small["pre_mix_norm_w"] + started

    leaving = {}

    def on_grads(group, gs):
        token, leaving[group] = _exchange_behind(gs, True, jnp.zeros((8, LANE), F32), group + "_grads")
        return token

    loss_local, dx, g_small = _local_step(x[0], positions[0], loss_target[0], wg, small,
                                          lambda group, after: arriving[group](after), on_grads)
    loss = lax.psum(loss_local, ("x", "y", "c"))

    recv = {}
    for group, names in (("ffn", ("w_gate", "w_up", "w_down")), ("heads", ("w_uq", "w_ukv", "w_out")), ("in", ("w_in",))):
        recv.update(zip(names, zip(*leaving[group](dx, place=False))))
    grads, deltas, new_m, new_v = {}, {}, {}, {}
    me1 = me.astype(jnp.int32).reshape(1)
    for name, (parts, own) in recv.items():
        outs = _adamw(parts, own, me1, w[name], m[name], v[name], name="adamw_" + name)
        if name in _TRANSPOSED:
            outs = [t.T for t in outs]
        grads[name], deltas[name], new_m[name], new_v[name] = outs

    def embed(t):
        return lax.dynamic_update_slice(jnp.zeros((CONV_K, CONV_DIM), F32), t, (0, me * cshard))

    parts_s = _all_gather([_pack_small(g_small)], name="gather_small_grads")[0]
    packs = [_pack_small({**{n_: d[n_] for n_, _ in _SMALL if n_ != "conv_w"}, "conv_w": embed(d["conv_w"])})
             for d in (w, m, v)]
    outs = [_unpack_small(t) for t in _adamw_small(parts_s, *packs)]
    for name, n in _SMALL:
        for dst, src in zip((grads, deltas, new_m, new_v), outs):
            if name == "conv_w":
                dst[name] = lax.dynamic_slice(src[name].reshape(CONV_K, CONV_DIM), (0, me * cshard), (CONV_K, cshard))
            else:
                dst[name] = src[name]

    def lead(d):
        return [d[name][None] for name in _WEIGHT_ORDER]

    return (loss, dx[None], *lead(grads), *lead(deltas), *lead(new_m), *lead(new_v))
```

```python
import numpy as np

import jax
import jax.numpy as jnp
from jax import lax
from jax.experimental import pallas as pl
from jax.experimental.pallas import tpu as pltpu

F32 = jnp.float32
BF16 = jnp.bfloat16
MXU_DTYPE = jnp.bfloat16
EPS = 1e-6
VMEM_LIMIT_BYTES = 48 * 1024 * 1024
K_TILE_MAX = 2048

N_DEV = 8
D_MODEL = 2048
Q_RANK = 512
KV_RANK = 512
ROPE = 64
HALF = ROPE // 2
HEADS = 8
NOPE = 128
VDIM = 128
QK = NOPE + ROPE
SSD_W = 1024
SSD_H = 16
SSD_P = 64
SSD_G = 2
SSD_E = SSD_H // SSD_G
SSD_N = 128
CHUNK = 128
CONV_K = 4
CONV_DIM = SSD_W + 2 * SSD_G * SSD_N
B_OFF = SSD_W
C_OFF = SSD_W + SSD_G * SSD_N
D_FF = 5632
D_IN = Q_RANK + KV_RANK + ROPE + SSD_W + CONV_DIM + SSD_H
ROPE_THETA = 10000.0
LANE = 128
HEAD_LANE = ROPE

ADAM_LR = 0.001
ADAM_B1 = 0.9
ADAM_B2 = 0.999
ADAM_EPS = 1e-08
ADAM_WD = 0.01
ADAM_STEP = 10


def _pick(n, cands):
    for c in cands:
        if n % c == 0:
            return c
    return n


def _params(*sem):
    return pltpu.CompilerParams(dimension_semantics=sem, vmem_limit_bytes=VMEM_LIMIT_BYTES)


def _sigmoid(x):
    return 1.0 / (1.0 + jnp.exp(-x))


def _silu(x):
    return x * _sigmoid(x)


def _dsilu(x):
    s = _sigmoid(x)
    return s * (1.0 + x * (1.0 - s))


def _softplus(x):
    e = jnp.exp(-jnp.abs(x))
    small = e * (1.0 - e * (0.5 - e * (1.0 / 3.0)))
    return jnp.maximum(x, 0.0) + jnp.where(e < 0.01, small, jnp.log(1.0 + e))


def _dot(a, b, ca, cb):
    return lax.dot_general(a, b, (((ca,), (cb,)), ((), ())), preferred_element_type=F32)


def _mx(v):
    return v.astype(MXU_DTYPE)


def _split3(a):
    hi = a.astype(BF16)
    r1 = a - hi.astype(F32)
    mid = r1.astype(BF16)
    lo = (r1 - mid.astype(F32)).astype(BF16)
    return hi, mid, lo


def _exact_dot(a, b, ca, cb, split_a):
    if split_a:
        return sum(_dot(p, b, ca, cb) for p in _split3(a))
    return sum(_dot(a, p, ca, cb) for p in _split3(b))


MM_ROW_GROUPS = 4


def _row_slices(tm, align):
    ng = MM_ROW_GROUPS
    while ng > 1 and (tm % ng or (tm // ng) % align):
        ng //= 2
    return [slice(g * (tm // ng), (g + 1) * (tm // ng)) for g in range(ng)]


def _mm(a, b, mode, *, a_blk=False, b_blk=False, out_blk=False, a_cols=None, b_cols=None, add=None, out_dtype=F32,
        fuse=1, wide=False, name="mm"):
    a2, b2 = a.shape[-2:], b.shape[-2:]
    a_last = a2[1] if a_cols is None else a_cols[1]
    a_start = 0 if a_cols is None else a_cols[0]
    b_start = 0
    if b_cols is not None:
        assert mode != "nt"
        b_start, b2 = b_cols[0], (b2[0], b_cols[1])
    if mode == "nn":
        m, k, (k2, n) = a2[0], a_last, b2
    elif mode == "nt":
        m, k, (n, k2) = a2[0], a_last, b2
    else:
        k, m, (k2, n) = a2[0], a_last, b2
    assert k == k2, (a.shape, b.shape, mode)
    tm = _pick(m, (1024, 704, 512, 256, 128))
    tn = _pick(n, ((2048,) if wide else ()) + (1024, 768, 704, 512, 256, 192, 128))
    k_max = 2 * K_TILE_MAX if mode == "tn" else K_TILE_MAX
    tk = k if k <= k_max else _pick(k, (K_TILE_MAX, 1024, 512))
    nk = k // tk
    jo = N_DEV if out_blk else 1
    reduce_blocks = a_blk and b_blk and not out_blk
    assert fuse == 1 or reduce_blocks
    jr = N_DEV // fuse if reduce_blocks else 1
    ca, cb = {"nn": (1, 0), "nt": (1, 1), "tn": (0, 0)}[mode]
    has_add = add is not None
    single = jr * nk == 1
    if mode == "tn":
        assert a_start % tm == 0
        a_block, a_idx = (tk, tm), (lambda i, kk: (kk, i + a_start // tm))
    else:
        assert a_start % tk == 0
        a_block, a_idx = (tm, tk), (lambda i, kk: (i, kk + a_start // tk))
    assert b_start % tn == 0
    b_block, b_idx = (((tn, tk), (lambda nn_, kk: (nn_, kk))) if mode == "nt"
                      else ((tk, tn), (lambda nn_, kk: (kk, nn_ + b_start // tn))))

    def blk_specs(blocked, block, idx, of_a, t):
        def pos(o, i, nn_, kk):
            return idx(i, kk) if of_a else idx(nn_, kk)
        if blocked:
            return pl.BlockSpec((None,) + block,
                                lambda o, i, nn_, r, kk: ((o if out_blk else r * fuse + t),) + pos(o, i, nn_, kk))
        return pl.BlockSpec(block, lambda o, i, nn_, r, kk: pos(o, i, nn_, kk))

    a_specs = [blk_specs(a_blk, a_block, a_idx, True, t) for t in range(fuse)]
    b_specs = [blk_specs(b_blk, b_block, b_idx, False, t) for t in range(fuse)]
    o_spec = (pl.BlockSpec((None, tm, tn), lambda o, i, nn_, r, kk: (o, i, nn_)) if out_blk
              else pl.BlockSpec((tm, tn), lambda o, i, nn_, r, kk: (i, nn_)))

    groups = _row_slices(tm, LANE if mode == "tn" else 16)

    def body(*refs):
        a_refs, b_refs = refs[:fuse], refs[fuse:2 * fuse]
        add_ref = refs[2 * fuse] if has_add else None
        o_ref = refs[2 * fuse + 1] if has_add else refs[2 * fuse]

        def partial(rs):
            out = None
            for t in range(fuse):
                av = a_refs[t][:, rs] if mode == "tn" else a_refs[t][rs, :]
                d = _dot(_mx(av), _mx(b_refs[t][...]), ca, cb)
                out = d if out is None else out + d
            return out

        if single:
            for rs in groups:
                res = partial(rs)
                if has_add:
                    res = res + add_ref[rs, :]
                o_ref[rs, :] = res.astype(o_ref.dtype)
            return
        acc = refs[-1]
        r, kk = pl.program_id(3), pl.program_id(4)

        @pl.when(jnp.logical_and(r == 0, kk == 0))
        def _():
            acc[...] = jnp.zeros_like(acc)

        for rs in groups:
            acc[rs, :] += partial(rs)

        @pl.when(jnp.logical_and(r == jr - 1, kk == nk - 1))
        def _():
            res = acc[...]
            if has_add:
                res = res + add_ref[...]
            o_ref[...] = res.astype(o_ref.dtype)

    out_shape = ((N_DEV, m, n) if out_blk else (m, n))
    return pl.pallas_call(
        body, name=name, grid=(jo, m // tm, n // tn, jr, nk),
        in_specs=a_specs + b_specs + ([o_spec] if has_add else []), out_specs=o_spec,
        out_shape=jax.ShapeDtypeStruct(out_shape, out_dtype),
        scratch_shapes=[] if single else [pltpu.VMEM((tm, tn), F32)],
        compiler_params=_params("parallel", "parallel", "parallel", "arbitrary", "arbitrary"),
    )(*((a,) * fuse + (b,) * fuse + ((add,) if has_add else ())))


def _mm_sum(a_list, b_list, name="mm_sum"):
    m, n = a_list[0].shape[0], b_list[0].shape[1]
    ns = len(a_list)
    tm = _pick(m, (1024, 512, 256, 128))
    tn = _pick(n, (1024, 512, 256, 128))
    groups = _row_slices(tm, 16)

    def body(*refs):
        a_refs, b_refs, o_ref = refs[:ns], refs[ns:2 * ns], refs[2 * ns]
        for rs in groups:
            acc = _dot(_mx(a_refs[0][rs, :]), _mx(b_refs[0][...]), 1, 0)
            for s in range(1, ns):
                acc = acc + _dot(_mx(a_refs[s][rs, :]), _mx(b_refs[s][...]), 1, 0)
            o_ref[rs, :] = acc

    return pl.pallas_call(
        body, name=name, grid=(m // tm, n // tn),
        in_specs=([pl.BlockSpec((tm, a.shape[1]), lambda i, j: (i, 0)) for a in a_list]
                  + [pl.BlockSpec((b.shape[0], tn), lambda i, j: (0, j)) for b in b_list]),
        out_specs=pl.BlockSpec((tm, tn), lambda i, j: (i, j)),
        out_shape=jax.ShapeDtypeStruct((m, n), F32), compiler_params=_params("parallel", "parallel"),
    )(*a_list, *b_list)


def _row_tile(r_, streams=4):
    return _pick(r_, ((512,) if streams <= 4 else ()) + (256, 128, 64, 32, 16, 8))


def _rms_fwd(t, w, groups=1, res=None, out_dtype=F32, name="rms_fwd"):
    r_, f = t.shape
    fg = f // groups
    tr = _row_tile(r_)
    has_res = res is not None

    def body(*refs):
        t_ref, w_ref = refs[0], refs[1]
        res_ref = refs[2] if has_res else None
        o_ref = refs[-1]
        for g in range(groups):
            sl = slice(g * fg, (g + 1) * fg)
            tv = t_ref[:, sl].astype(F32)
            r = lax.rsqrt(jnp.mean(tv * tv, axis=-1, keepdims=True) + EPS)
            y = tv * r * w_ref[:, sl]
            if has_res:
                y = y + res_ref[:, sl]
            o_ref[:, sl] = y.astype(o_ref.dtype)

    row = pl.BlockSpec((tr, f), lambda i: (i, 0))
    wsp = pl.BlockSpec((1, f), lambda i: (0, 0))
    return pl.pallas_call(
        body, name=name, grid=(r_ // tr,),
        in_specs=[row, wsp] + ([row] if has_res else []), out_specs=row,
        out_shape=jax.ShapeDtypeStruct((r_, f), out_dtype),
        compiler_params=_params("parallel"),
    )(*((t, w.reshape(1, f)) + ((res,) if has_res else ())))


def _rms_bwd(t, w, dys, res=None, out_dtype=F32, name="rms_bwd"):
    r_, f = t.shape
    groups = len(dys)
    fg = f // groups
    tr = _row_tile(r_)
    has_res = res is not None

    def body(*refs):
        t_ref, w_ref = refs[0], refs[1]
        dy_refs = refs[2:2 + groups]
        res_ref = refs[2 + groups] if has_res else None
        dt_ref, dw_ref = refs[-2], refs[-1]

        @pl.when(pl.program_id(0) == 0)
        def _():
            dw_ref[...] = jnp.zeros_like(dw_ref)

        for g in range(groups):
            sl = slice(g * fg, (g + 1) * fg)
            tv = t_ref[:, sl].astype(F32)
            dyv = dy_refs[g][...].astype(F32)
            r = lax.rsqrt(jnp.mean(tv * tv, axis=-1, keepdims=True) + EPS)
            gw = dyv * w_ref[:, sl]
            c = jnp.mean(gw * tv, axis=-1, keepdims=True)
            dt = r * gw - tv * (r * r * r * c)
            if has_res:
                dt = dt + res_ref[:, sl]
            dt_ref[:, sl] = dt.astype(dt_ref.dtype)
            dw_ref[:, sl] += jnp.sum(dyv * tv * r, axis=0, keepdims=True)

    row = pl.BlockSpec((tr, f), lambda i: (i, 0))
    grow = pl.BlockSpec((tr, fg), lambda i: (i, 0))
    wsp = pl.BlockSpec((1, f), lambda i: (0, 0))
    return pl.pallas_call(
        body, name=name, grid=(r_ // tr,),
        in_specs=[row, wsp] + [grow] * groups + ([row] if has_res else []), out_specs=[row, wsp],
        out_shape=[jax.ShapeDtypeStruct((r_, f), out_dtype), jax.ShapeDtypeStruct((1, f), F32)],
        compiler_params=_params("arbitrary"),
    )(*((t, w.reshape(1, f)) + tuple(dys) + ((res,) if has_res else ())))


def _norm_res_norm(t, res, w1, w2, name="post_mix_pre_ffn_norm"):
    r_, f = t.shape
    tr = _row_tile(r_)

    def body(t_ref, res_ref, w1_ref, w2_ref, h_ref, v_ref):
        tv = t_ref[...]
        h = res_ref[...] + tv * lax.rsqrt(jnp.mean(tv * tv, axis=-1, keepdims=True) + EPS) * w1_ref[...]
        h_ref[...] = h
        v_ref[...] = (h * lax.rsqrt(jnp.mean(h * h, axis=-1, keepdims=True) + EPS) * w2_ref[...]).astype(v_ref.dtype)

    row = pl.BlockSpec((tr, f), lambda i: (i, 0))
    wsp = pl.BlockSpec((1, f), lambda i: (0, 0))
    return pl.pallas_call(
        body, name=name, grid=(r_ // tr,), in_specs=[row, row, wsp, wsp], out_specs=[row, row],
        out_shape=[jax.ShapeDtypeStruct((r_, f), F32), jax.ShapeDtypeStruct((r_, f), MXU_DTYPE)],
        compiler_params=_params("parallel"),
    )(t, res, w1.reshape(1, f), w2.reshape(1, f))


def _norm_res_norm_bwd(h, w2, dv, dres, t, w1, name="pre_ffn_post_mix_norm_bwd"):
    r_, f = h.shape
    tr = _row_tile(r_, streams=6)

    def body(h_ref, w2_ref, dv_ref, dres_ref, t_ref, w1_ref, dh_ref, dt_ref, dw2_ref, dw1_ref):
        @pl.when(pl.program_id(0) == 0)
        def _():
            dw2_ref[...] = jnp.zeros_like(dw2_ref)
            dw1_ref[...] = jnp.zeros_like(dw1_ref)

        def rms_bwd(tv, wv, dyv):
            r = lax.rsqrt(jnp.mean(tv * tv, axis=-1, keepdims=True) + EPS)
            gw = dyv * wv
            c = jnp.mean(gw * tv, axis=-1, keepdims=True)
            return r * gw - tv * (r * r * r * c), jnp.sum(dyv * tv * r, axis=0, keepdims=True)

        d1, g2 = rms_bwd(h_ref[...], w2_ref[...], dv_ref[...])
        dh = d1 + dres_ref[...]
        dh_ref[...] = dh
        dw2_ref[...] += g2
        d2, g1 = rms_bwd(t_ref[...], w1_ref[...], dh)
        dt_ref[...] = d2.astype(dt_ref.dtype)
        dw1_ref[...] += g1

    row = pl.BlockSpec((tr, f), lambda i: (i, 0))
    wsp = pl.BlockSpec((1, f), lambda i: (0, 0))
    return pl.pallas_call(
        body, name=name, grid=(r_ // tr,), in_specs=[row, wsp, row, row, row, wsp], out_specs=[row, row, wsp, wsp],
        out_shape=[jax.ShapeDtypeStruct((r_, f), F32), jax.ShapeDtypeStruct((r_, f), MXU_DTYPE),
                   jax.ShapeDtypeStruct((1, f), F32), jax.ShapeDtypeStruct((1, f), F32)],
        compiler_params=_params("arbitrary"),
    )(h, w2.reshape(1, f), dv, dres, t, w1.reshape(1, f))


def _hnorm_fwd(o, w, width, name="attn_out_norm"):
    h, s_, v = o.shape
    tr = _row_tile(s_)

    def body(o_ref, w_ref, y_ref):
        ss = jnp.sum(o_ref[0] * o_ref[0], axis=-1, keepdims=True)
        for i in range(1, h):
            ss = ss + jnp.sum(o_ref[i] * o_ref[i], axis=-1, keepdims=True)
        r = lax.rsqrt(ss * (1.0 / (h * v)) + EPS)
        for i in range(h):
            sl = slice(i * v, (i + 1) * v)
            y_ref[:, sl] = (o_ref[i] * r * w_ref[:, sl]).astype(y_ref.dtype)

    return pl.pallas_call(
        body, name=name, grid=(s_ // tr,),
        in_specs=[pl.BlockSpec((h, tr, v), lambda i: (0, i, 0)), pl.BlockSpec((1, h * v), lambda i: (0, 0))],
        out_specs=pl.BlockSpec((tr, h * v), lambda i: (i, 0)),
        out_shape=jax.ShapeDtypeStruct((s_, width), MXU_DTYPE), compiler_params=_params("parallel"),
    )(o, w)


def _hnorm_bwd(o, w, dy, name="attn_out_norm_bwd"):
    h, s_, v = o.shape
    tr = _row_tile(s_)

    def body(o_ref, w_ref, dy_ref, do_ref, delta_ref, dw_ref):
        @pl.when(pl.program_id(0) == 0)
        def _():
            dw_ref[...] = jnp.zeros_like(dw_ref)

        ss = jnp.zeros((tr, 1), F32)
        cc = jnp.zeros((tr, 1), F32)
        for i in range(h):
            sl = slice(i * v, (i + 1) * v)
            ov = o_ref[i]
            ss = ss + jnp.sum(ov * ov, axis=-1, keepdims=True)
            cc = cc + jnp.sum(dy_ref[:, sl] * w_ref[:, sl] * ov, axis=-1, keepdims=True)
        r = lax.rsqrt(ss * (1.0 / (h * v)) + EPS)
        c = cc * (1.0 / (h * v))
        for i in range(h):
            sl = slice(i * v, (i + 1) * v)
            ov = o_ref[i]
            dyv = dy_ref[:, sl]
            dov = r * dyv * w_ref[:, sl] - ov * (r * r * r * c)
            do_ref[i] = dov.astype(do_ref.dtype)
            delta_ref[i] = jnp.sum(dov * ov, axis=-1, keepdims=True)
            dw_ref[:, sl] += jnp.sum(dyv * ov * r, axis=0, keepdims=True)

    blk = pl.BlockSpec((h, tr, v), lambda i: (0, i, 0))
    wsp = pl.BlockSpec((1, h * v), lambda i: (0, 0))
    return pl.pallas_call(
        body, name=name, grid=(s_ // tr,),
        in_specs=[blk, wsp, pl.BlockSpec((tr, h * v), lambda i: (i, 0))],
        out_specs=[blk, pl.BlockSpec((h, tr, 1), lambda i: (0, i, 0)), wsp],
        out_shape=[jax.ShapeDtypeStruct(o.shape, MXU_DTYPE), jax.ShapeDtypeStruct((h, s_, 1), F32),
                   jax.ShapeDtypeStruct((1, h * v), F32)],
        compiler_params=_params("arbitrary"),
    )(o, w, dy)


def _loss_head(ffn, h1, target, w, name="loss_head"):
    r_, f = ffn.shape
    tr = _row_tile(r_)

    def body(ffn_ref, h1_ref, tg_ref, w_ref, loss_ref, dy_ref, dffn_ref, dw_ref):
        @pl.when(pl.program_id(0) == 0)
        def _():
            dw_ref[...] = jnp.zeros_like(dw_ref)
            loss_ref[...] = jnp.zeros_like(loss_ref)

        tv = ffn_ref[...]
        wv = w_ref[...]
        r = lax.rsqrt(jnp.mean(tv * tv, axis=-1, keepdims=True) + EPS)
        tn = tv * r
        e = h1_ref[...] + tn * wv - tg_ref[...]
        tot = jnp.sum(jnp.sum(e * e, axis=1, keepdims=True), axis=0, keepdims=True) * (0.5 / f)
        loss_ref[...] += tot + jnp.zeros_like(loss_ref)
        dyv = e * (1.0 / f)
        dy_ref[...] = dyv
        gw = dyv * wv
        c = jnp.mean(gw * tv, axis=-1, keepdims=True)
        dffn_ref[...] = (r * gw - tv * (r * r * r * c)).astype(dffn_ref.dtype)
        dw_ref[...] += jnp.sum(dyv * tn, axis=0, keepdims=True)

    row = pl.BlockSpec((tr, f), lambda i: (i, 0))
    wsp = pl.BlockSpec((1, f), lambda i: (0, 0))
    lsp = pl.BlockSpec((1, LANE), lambda i: (0, 0))
    return pl.pallas_call(
        body, name=name, grid=(r_ // tr,),
        in_specs=[row, row, row, wsp], out_specs=[lsp, row, row, wsp],
        out_shape=[jax.ShapeDtypeStruct((1, LANE), F32), jax.ShapeDtypeStruct((r_, f), F32),
                   jax.ShapeDtypeStruct((r_, f), MXU_DTYPE), jax.ShapeDtypeStruct((1, f), F32)],
        compiler_params=_params("arbitrary"),
    )(ffn, h1, target, w.reshape(1, f))


def _rot_matrix():
    p = np.zeros((ROPE, ROPE), np.float32)
    for i in range(HALF):
        p[i + HALF, i] = -1.0
        p[i, i + HALF] = 1.0
    return jnp.asarray(p, BF16)


def _rope_val(r, c2, s2, rot):
    hi, mid, _ = _split3(r)
    return r * c2 + (_dot(hi, rot, 1, 0) + _dot(mid, rot, 1, 0)) * s2


def _q_prep(q, cos2, sin2, scale, name):
    h, s_, _ = q.shape
    tr = _pick(s_, (4096, 2048, 1024, 512, 256, 128, 64, 32, 16))

    def body(q_ref, c_ref, s_ref, rot_ref, o_ref):
        for rs in _row_slices(tr, 16):
            x = q_ref[rs, :]
            o_ref[rs, :NOPE] = (x[:, :NOPE] * scale).astype(o_ref.dtype)
            o_ref[rs, NOPE:] = (_rope_val(x[:, NOPE:], c_ref[rs, :], s_ref[rs, :], rot_ref[...]) * scale).astype(o_ref.dtype)

    blk = pl.BlockSpec((None, tr, QK), lambda hh, i: (hh, i, 0))
    csp = pl.BlockSpec((tr, ROPE), lambda hh, i: (i, 0))
    return pl.pallas_call(
        body, name=name, grid=(h, s_ // tr),
        in_specs=[blk, csp, csp, pl.BlockSpec((ROPE, ROPE), lambda hh, i: (0, 0))], out_specs=blk,
        out_shape=jax.ShapeDtypeStruct(q.shape, MXU_DTYPE), compiler_params=_params("parallel", "parallel"),
    )(q, cos2, sin2, _rot_matrix())


def _q_up(qkvn, w_uq_t, cos2, sin2, scale, name="q_up"):
    s_ = qkvn.shape[0]
    h = w_uq_t.shape[0]
    tm = _pick(s_, (4096, 2048, 1024, 512, 256, 128))

    def body(a_ref, w_ref, c_ref, s_ref, rot_ref, o_ref):
        for rs in _row_slices(tm, 16):
            x = _dot(_mx(a_ref[rs, :]), _mx(w_ref[...]), 1, 1)
            o_ref[rs, :NOPE] = (x[:, :NOPE] * scale).astype(o_ref.dtype)
            o_ref[rs, NOPE:] = (_rope_val(x[:, NOPE:], c_ref[rs, :], s_ref[rs, :], rot_ref[...]) * scale).astype(o_ref.dtype)

    csp = pl.BlockSpec((tm, ROPE), lambda j, i: (i, 0))
    return pl.pallas_call(
        body, name=name, grid=(h, s_ // tm),
        in_specs=[pl.BlockSpec((tm, Q_RANK), lambda j, i: (i, 0)), pl.BlockSpec((None, QK, Q_RANK), lambda j, i: (j, 0, 0)),
                  csp, csp, pl.BlockSpec((ROPE, ROPE), lambda j, i: (0, 0))],
        out_specs=pl.BlockSpec((None, tm, QK), lambda j, i: (j, i, 0)),
        out_shape=jax.ShapeDtypeStruct((h, s_, QK), MXU_DTYPE), compiler_params=_params("parallel", "parallel"),
    )(qkvn, w_uq_t, cos2, sin2, _rot_matrix())


def _kv_up(qkvn, w_ukv, small, cos2, sin2, name="kv_up"):
    s_ = qkvn.shape[0]
    h = w_ukv.shape[0]
    tm = _pick(s_, (4096, 2048, 1024, 512, 256, 128))

    def body(a_ref, w_ref, sm_ref, c_ref, s_ref, rot_ref, k_ref, v_ref):
        for rs in _row_slices(tm, 16):
            x = _dot(_mx(a_ref[rs, :]), _mx(w_ref[...]), 1, 0)
            k_ref[rs, :NOPE] = x[:, :NOPE].astype(k_ref.dtype)
            k_ref[rs, NOPE:] = _rope_val(sm_ref[rs, :ROPE], c_ref[rs, :], s_ref[rs, :], rot_ref[...]).astype(k_ref.dtype)
            v_ref[rs, :] = x[:, NOPE:].astype(v_ref.dtype)

    csp = pl.BlockSpec((tm, ROPE), lambda j, i: (i, 0))
    return pl.pallas_call(
        body, name=name, grid=(h, s_ // tm),
        in_specs=[pl.BlockSpec((tm, KV_RANK), lambda j, i: (i, Q_RANK // KV_RANK)),
                  pl.BlockSpec((None, KV_RANK, NOPE + VDIM), lambda j, i: (j, 0, 0)),
                  pl.BlockSpec((tm, LANE), lambda j, i: (i, 0)), csp, csp, pl.BlockSpec((ROPE, ROPE), lambda j, i: (0, 0))],
        out_specs=[pl.BlockSpec((None, tm, QK), lambda j, i: (j, i, 0)), pl.BlockSpec((None, tm, VDIM), lambda j, i: (j, i, 0))],
        out_shape=[jax.ShapeDtypeStruct((h, s_, QK), MXU_DTYPE), jax.ShapeDtypeStruct((h, s_, VDIM), MXU_DTYPE)],
        compiler_params=_params("parallel", "parallel"),
    )(qkvn, w_ukv, small, cos2, sin2, _rot_matrix())


def _dkv_post(dk, dv, ddt, cos2, nsin2, name="dkv_post"):
    h, s_, _ = dk.shape
    tr = _row_tile(s_)

    def body(dk_ref, dv_ref, ddt_ref, c_ref, s_ref, rot_ref, dkv_ref, dsm_ref):
        acc = dk_ref[0, :, NOPE:]
        for i in range(1, h):
            acc = acc + dk_ref[i, :, NOPE:]
        dsm_ref[:, :ROPE] = _rope_val(acc, c_ref[...], s_ref[...], rot_ref[...]).astype(dsm_ref.dtype)
        dsm_ref[:, ROPE:] = ddt_ref[:, ROPE:].astype(dsm_ref.dtype)
        for i in range(h):
            dkv_ref[i, :, :NOPE] = dk_ref[i, :, :NOPE].astype(dkv_ref.dtype)
            dkv_ref[i, :, NOPE:] = dv_ref[i].astype(dkv_ref.dtype)

    csp = pl.BlockSpec((tr, ROPE), lambda i: (i, 0))
    return pl.pallas_call(
        body, name=name, grid=(s_ // tr,),
        in_specs=[pl.BlockSpec((h, tr, QK), lambda i: (0, i, 0)), pl.BlockSpec((h, tr, VDIM), lambda i: (0, i, 0)),
                  pl.BlockSpec((tr, LANE), lambda i: (i, 0)), csp, csp, pl.BlockSpec((ROPE, ROPE), lambda i: (0, 0))],
        out_specs=[pl.BlockSpec((h, tr, NOPE + VDIM), lambda i: (0, i, 0)), pl.BlockSpec((tr, LANE), lambda i: (i, 0))],
        out_shape=[jax.ShapeDtypeStruct((h, s_, NOPE + VDIM), MXU_DTYPE), jax.ShapeDtypeStruct((s_, LANE), MXU_DTYPE)],
        compiler_params=_params("parallel"),
    )(dk, dv, ddt, cos2, nsin2, _rot_matrix())


def _attn_tile(s):
    return 2048 if s % 4096 == 0 else s // 2


def _pairs(n, by_key):
    if by_key:
        pr = [(i, j) for j in range(n) for i in range(j, n)]
    else:
        pr = [(i, j) for i in range(n) for j in range(i + 1)]
    return (jnp.asarray([p[0] for p in pr], jnp.int32), jnp.asarray([p[1] for p in pr], jnp.int32))


ATTN_ROW_GROUPS = 8


def _row_groups(t, diag):
    tg = t // ATTN_ROW_GROUPS
    out = []
    for r in range(ATTN_ROW_GROUPS):
        nc = (r + 1) * tg if diag else t
        mask = None
        if diag:
            mask = (lax.broadcasted_iota(jnp.int32, (tg, nc), 1)
                    <= lax.broadcasted_iota(jnp.int32, (tg, nc), 0) + r * tg)
        out.append((slice(r * tg, (r + 1) * tg), nc, mask))
    return out


def _flash_specs(t, dk, dv):
    qsp = pl.BlockSpec((None, t, dk), lambda hh, p, qi, kj: (hh, qi[p], 0))
    ksp = pl.BlockSpec((None, t, dk), lambda hh, p, qi, kj: (hh, kj[p], 0))
    vsp = pl.BlockSpec((None, t, dv), lambda hh, p, qi, kj: (hh, kj[p], 0))
    osp = pl.BlockSpec((None, t, dv), lambda hh, p, qi, kj: (hh, qi[p], 0))
    lsp = pl.BlockSpec((None, t, 1), lambda hh, p, qi, kj: (hh, qi[p], 0))
    return qsp, ksp, vsp, osp, lsp


def _flash_fwd(q, k, v, name="flash_fwd"):
    h, s_, dk = q.shape
    dv = v.shape[-1]
    t = _attn_tile(s_)
    n = s_ // t
    qi, kj = _pairs(n, False)

    def body(qi_ref, kj_ref, q_ref, k_ref, v_ref, o_ref, lse_ref, m_s, l_s, acc):
        p_ = pl.program_id(1)
        i, j = qi_ref[p_], kj_ref[p_]

        @pl.when(j == 0)
        def _():
            m_s[...] = jnp.full_like(m_s, -jnp.inf)
            l_s[...] = jnp.zeros_like(l_s)
            acc[...] = jnp.zeros_like(acc)

        def update(diag):
            for rs, nc, mask in _row_groups(t, diag):
                sc = _dot(q_ref[rs, :], k_ref[0:nc, :], 1, 1)
                if mask is not None:
                    sc = jnp.where(mask, sc, -jnp.inf)
                m_old = m_s[rs, :]
                m_new = jnp.maximum(m_old, jnp.max(sc, axis=1, keepdims=True))
                alpha = jnp.exp(m_old - m_new)
                p = jnp.exp(sc - m_new)
                l_s[rs, :] = alpha * l_s[rs, :] + jnp.sum(p, axis=1, keepdims=True)
                acc[rs, :] = alpha * acc[rs, :] + _dot(_mx(p), v_ref[0:nc, :], 1, 0)
                m_s[rs, :] = m_new

        @pl.when(j < i)
        def _():
            update(False)

        @pl.when(j == i)
        def _():
            update(True)
            o_ref[...] = acc[...] / l_s[...]
            lse_ref[...] = m_s[...] + jnp.log(l_s[...])

    qsp, ksp, vsp, osp, lsp = _flash_specs(t, dk, dv)
    gs = pltpu.PrefetchScalarGridSpec(
        num_scalar_prefetch=2, grid=(h, qi.shape[0]), in_specs=[qsp, ksp, vsp], out_specs=[osp, lsp],
        scratch_shapes=[pltpu.VMEM((t, 1), F32), pltpu.VMEM((t, 1), F32), pltpu.VMEM((t, dv), F32)])
    return pl.pallas_call(
        body, name=name, grid_spec=gs,
        out_shape=[jax.ShapeDtypeStruct((h, s_, dv), F32), jax.ShapeDtypeStruct((h, s_, 1), F32)],
        compiler_params=_params("parallel", "arbitrary"),
    )(qi, kj, q, k, v)


def _flash_bwd(q, k, v, do, lse, delta, name="flash_bwd"):
    h, s_, dk = q.shape
    dv = v.shape[-1]
    t = _attn_tile(s_)
    tg = t // ATTN_ROW_GROUPS
    n = s_ // t
    qi, kj = _pairs(n, True)

    def body(qi_ref, kj_ref, q_ref, k_ref, v_ref, do_ref, lse_ref, delta_ref, dq_ref, dk_ref, dv_ref, dk_acc, dv_acc):
        p_ = pl.program_id(1)
        i, j = qi_ref[p_], kj_ref[p_]

        @pl.when(p_ == 0)
        def _():
            dq_ref[...] = jnp.zeros_like(dq_ref)

        def update(diag):
            for g, (rs, nc, mask) in enumerate(_row_groups(t, diag)):
                sc = _dot(q_ref[rs, :], k_ref[0:nc, :], 1, 1)
                if mask is not None:
                    sc = jnp.where(mask, sc, -jnp.inf)
                p = jnp.exp(sc - lse_ref[rs, :])
                dob = _mx(do_ref[rs, :])
                dv_acc[0:nc, :] += _dot(_mx(p), dob, 0, 0)
                dp = _dot(dob, v_ref[0:nc, :], 1, 1)
                dsb = _mx(p * (dp - delta_ref[rs, :]))
                dk_acc[0:nc, :] += _dot(dsb, q_ref[rs, :], 0, 0)
                rows = pl.ds(pl.multiple_of(i * t + g * tg, tg), tg)
                dq_ref[rows, :] += _dot(dsb, k_ref[0:nc, :], 1, 0)

        @pl.when(i == j)
        def _():
            dk_acc[...] = jnp.zeros_like(dk_acc)
            dv_acc[...] = jnp.zeros_like(dv_acc)
            update(True)

        @pl.when(i > j)
        def _():
            update(False)

        @pl.when(i == n - 1)
        def _():
            dk_ref[...] = dk_acc[...]
            dv_ref[...] = dv_acc[...]

    qsp, ksp, vsp, osp, lsp = _flash_specs(t, dk, dv)
    dqsp = pl.BlockSpec((None, s_, dk), lambda hh, p, qi, kj: (hh, 0, 0))
    gs = pltpu.PrefetchScalarGridSpec(
        num_scalar_prefetch=2, grid=(h, qi.shape[0]), in_specs=[qsp, ksp, vsp, osp, lsp, lsp],
        out_specs=[dqsp, ksp, vsp],
        scratch_shapes=[pltpu.VMEM((t, dk), F32), pltpu.VMEM((t, dv), F32)])
    return pl.pallas_call(
        body, name=name, grid_spec=gs,
        out_shape=[jax.ShapeDtypeStruct((h, s_, dk), F32), jax.ShapeDtypeStruct((h, s_, dk), F32),
                   jax.ShapeDtypeStruct((h, s_, dv), F32)],
        compiler_params=_params("parallel", "arbitrary"),
    )(qi, kj, q, k, v, do, lse, delta)


HALO = 8


def _conv_specs(s_, c, tr, after):
    main = pl.BlockSpec((tr, c), lambda i: (i, 0))
    per = tr // HALO
    if after:
        halo = pl.BlockSpec((HALO, c), lambda i: (jnp.minimum((i + 1) * per, s_ // HALO - 1), 0))
    else:
        halo = pl.BlockSpec((HALO, c), lambda i: (jnp.maximum(i * per - 1, 0), 0))
    return main, halo


def _fill_before(ext, t_ref, h_ref, tr):
    ext[0:HALO, :] = jnp.where(pl.program_id(0) > 0, h_ref[...], 0.0)
    ext[HALO:HALO + tr, :] = t_ref[...]


def _taps(ext, w_ref, tr):
    base = HALO - (CONV_K - 1)
    acc = ext[base:base + tr, :] * w_ref[0:1, :]
    for k in range(1, CONV_K):
        acc = acc + ext[base + k:base + k + tr, :] * w_ref[k:k + 1, :]
    return acc


def _conv_fwd(t, w, b, name="conv_fwd"):
    s_, c = t.shape
    tr = _row_tile(s_)

    def body(t_ref, h_ref, w_ref, b_ref, o_ref, ext):
        _fill_before(ext, t_ref, h_ref, tr)
        o_ref[...] = _silu(_taps(ext, w_ref, tr) + b_ref[...])

    main, halo = _conv_specs(s_, c, tr, False)
    return pl.pallas_call(
        body, name=name, grid=(s_ // tr,),
        in_specs=[main, halo, pl.BlockSpec((CONV_K, c), lambda i: (0, 0)), pl.BlockSpec((1, c), lambda i: (0, 0))],
        out_specs=main, out_shape=jax.ShapeDtypeStruct((s_, c), F32),
        scratch_shapes=[pltpu.VMEM((tr + HALO, c), F32)], compiler_params=_params("parallel"),
    )(t, t, w, b)


def _conv_bwd_pre(t, w, b, dact, name="conv_bwd_pre"):
    s_, c = t.shape
    tr = _row_tile(s_)

    def body(t_ref, h_ref, w_ref, b_ref, da_ref, dpre_ref, dwb_ref, ext):
        @pl.when(pl.program_id(0) == 0)
        def _():
            dwb_ref[...] = jnp.zeros_like(dwb_ref)

        _fill_before(ext, t_ref, h_ref, tr)
        dpre = da_ref[...] * _dsilu(_taps(ext, w_ref, tr) + b_ref[...])
        dpre_ref[...] = dpre
        base = HALO - (CONV_K - 1)
        for k in range(CONV_K):
            dwb_ref[k:k + 1, :] += jnp.sum(dpre * ext[base + k:base + k + tr, :], axis=0, keepdims=True)
        dwb_ref[CONV_K:CONV_K + 1, :] += jnp.sum(dpre, axis=0, keepdims=True)

    main, halo = _conv_specs(s_, c, tr, False)
    return pl.pallas_call(
        body, name=name, grid=(s_ // tr,),
        in_specs=[main, halo, pl.BlockSpec((CONV_K, c), lambda i: (0, 0)), pl.BlockSpec((1, c), lambda i: (0, 0)), main],
        out_specs=[main, pl.BlockSpec((8, c), lambda i: (0, 0))],
        out_shape=[jax.ShapeDtypeStruct((s_, c), F32), jax.ShapeDtypeStruct((8, c), F32)],
        scratch_shapes=[pltpu.VMEM((tr + HALO, c), F32)], compiler_params=_params("arbitrary"),
    )(t, t, w, b, dact)


def _conv_bwd_in(dpre, w, name="conv_bwd_in"):
    s_, c = dpre.shape
    tr = _row_tile(s_)
    nt = s_ // tr

    def body(d_ref, h_ref, w_ref, o_ref, ext):
        ext[0:tr, :] = d_ref[...]
        ext[tr:tr + HALO, :] = jnp.where(pl.program_id(0) < nt - 1, h_ref[...], 0.0)
        acc = ext[CONV_K - 1:CONV_K - 1 + tr, :] * w_ref[0:1, :]
        for k in range(1, CONV_K):
            acc = acc + ext[CONV_K - 1 - k:CONV_K - 1 - k + tr, :] * w_ref[k:k + 1, :]
        o_ref[...] = acc.astype(o_ref.dtype)

    main, halo = _conv_specs(s_, c, tr, True)
    return pl.pallas_call(
        body, name=name, grid=(nt,),
        in_specs=[main, halo, pl.BlockSpec((CONV_K, c), lambda i: (0, 0))],
        out_specs=main, out_shape=jax.ShapeDtypeStruct((s_, c), MXU_DTYPE),
        scratch_shapes=[pltpu.VMEM((tr + HALO, c), F32)], compiler_params=_params("parallel"),
    )(dpre, dpre, w)


def _ssd_chunk_common(dt_ref, dtt_ref, br_ref, bc_ref, ar_ref, ac_ref):
    li = lax.broadcasted_iota(jnp.int32, (CHUNK, CHUNK), 0)
    si = lax.broadcasted_iota(jnp.int32, (CHUNK, CHUNK), 1)
    lower = li >= si
    lower_b = lower.astype(BF16)
    upper_b = (li <= si).astype(BF16)
    zr = dt_ref[...] + br_ref[...]
    dtc = _softplus(zr)
    a_row = -jnp.exp(ar_ref[...])
    acum = _exact_dot(lower_b, dtc * a_row, 1, 0, False)
    dtt = _softplus(dtt_ref[...] + bc_ref[...])
    acum_t = _exact_dot(dtt * (-jnp.exp(ac_ref[...])), upper_b, 1, 0, True)
    return lower, upper_b, zr, dtc, a_row, acum, acum_t


def _head_terms(h, lower, dtc, acum, acum_t):
    lane = lax.broadcasted_iota(jnp.int32, (1, LANE), 1)
    sub = lax.broadcasted_iota(jnp.int32, (SSD_H, 1), 0)
    rowid = lax.broadcasted_iota(jnp.int32, (CHUNK, 1), 0)
    oh = (lane == HEAD_LANE + h).astype(F32)
    acol = jnp.sum(acum * oh, axis=1, keepdims=True)
    dcol = jnp.sum(dtc * oh, axis=1, keepdims=True)
    arow = jnp.sum(acum_t * (sub == h).astype(F32), axis=0, keepdims=True)
    alast = jnp.sum(jnp.where(rowid == CHUNK - 1, acol, 0.0), axis=0, keepdims=True)
    decay = jnp.exp(jnp.where(lower, acol - arow, -jnp.inf))
    return oh, acol, dcol, alast, decay


SSD_PAIRS = SSD_H // 2
PAIRS_PER_GROUP = SSD_E // 2


def _ps(q):
    return slice(q * LANE, (q + 1) * LANE)


def _gs(off, g):
    return slice(off + g * SSD_N, off + (g + 1) * SSD_N)


def _lanes(c0, c1):
    return jnp.where(lax.broadcasted_iota(jnp.int32, (1, LANE), 1) < SSD_P, c0, c1)


def _rows(c0, c1):
    return jnp.where(lax.broadcasted_iota(jnp.int32, (LANE, 1), 0) < SSD_P, c0, c1)


def _lane_halves(t):
    first = lax.broadcasted_iota(jnp.int32, (1, LANE), 1) < SSD_P
    return (jnp.sum(jnp.where(first, t, 0.0), axis=1, keepdims=True),
            jnp.sum(jnp.where(first, 0.0, t), axis=1, keepdims=True))


def _ssd_in_specs(rev):
    def ci(c):
        return c if rev is None else rev - c
    return [pl.BlockSpec((CHUNK, CONV_DIM), lambda c: (ci(c), 0)),
            pl.BlockSpec((CHUNK, LANE), lambda c: (ci(c), 0)),
            pl.BlockSpec((SSD_H, CHUNK), lambda c: (0, ci(c))),
            pl.BlockSpec((1, LANE), lambda c: (0, 0)), pl.BlockSpec((SSD_H, 1), lambda c: (0, 0)),
            pl.BlockSpec((1, LANE), lambda c: (0, 0)), pl.BlockSpec((SSD_H, 1), lambda c: (0, 0)),
            pl.BlockSpec((SSD_PAIRS, 1, LANE), lambda c: (0, 0, 0))]


def _ssd_fwd(xbc, small, dtt, bias_r, bias_c, alog_r, alog_c, dsk, name="ssd_fwd"):
    s_ = xbc.shape[0]
    nc = s_ // CHUNK

    def body(x_ref, dt_ref, dtt_ref, br_ref, bc_ref, ar_ref, ac_ref, dsk_ref, y_ref, prev_ref, state):
        @pl.when(pl.program_id(0) == 0)
        def _():
            state[...] = jnp.zeros_like(state)

        lower, _, _, dtc, _, acum, acum_t = _ssd_chunk_common(dt_ref, dtt_ref, br_ref, bc_ref, ar_ref, ac_ref)
        for g in range(SSD_G):
            bb = _mx(x_ref[:, _gs(B_OFF, g)])
            cb_ = _mx(x_ref[:, _gs(C_OFF, g)])
            cbm = _dot(cb_, bb, 1, 1)
            for e in range(PAIRS_PER_GROUP):
                q = g * PAIRS_PER_GROUP + e
                _, acol0, dcol0, alast0, decay0 = _head_terms(2 * q, lower, dtc, acum, acum_t)
                _, acol1, dcol1, alast1, decay1 = _head_terms(2 * q + 1, lower, dtc, acum, acum_t)
                x = x_ref[:, _ps(q)]
                xdt = x * _lanes(dcol0, dcol1)
                xb = _mx(xdt)
                yd = _lanes(_dot(_mx(cbm * decay0), xb, 1, 0), _dot(_mx(cbm * decay1), xb, 1, 0))
                prev = state[q]
                prev_ref[0, q] = prev
                yo = _dot(cb_, _mx(prev), 1, 1) * _lanes(jnp.exp(acol0), jnp.exp(acol1))
                ds = _lanes(jnp.exp(alast0 - acol0), jnp.exp(alast1 - acol1))
                st = _dot(_mx(xdt * ds), bb, 0, 0)
                state[q] = prev * _rows(jnp.exp(alast0), jnp.exp(alast1)) + st
                y_ref[:, _ps(q)] = yd + yo + x * dsk_ref[q]

    psp = pl.BlockSpec((1, SSD_PAIRS, LANE, SSD_N), lambda c: (c, 0, 0, 0))
    return pl.pallas_call(
        body, name=name, grid=(nc,),
        in_specs=_ssd_in_specs(None), out_specs=[pl.BlockSpec((CHUNK, SSD_W), lambda c: (c, 0)), psp],
        out_shape=[jax.ShapeDtypeStruct((s_, SSD_W), F32),
                   jax.ShapeDtypeStruct((nc, SSD_PAIRS, LANE, SSD_N), F32)],
        scratch_shapes=[pltpu.VMEM((SSD_PAIRS, LANE, SSD_N), F32)],
        compiler_params=_params("arbitrary"),
    )(xbc, small, dtt, bias_r, bias_c, alog_r, alog_c, dsk)


def _ssd_bwd(xbc, small, dtt, bias_r, bias_c, alog_r, alog_c, dsk, prev, dy, name="ssd_bwd"):
    s_ = xbc.shape[0]
    nc = s_ // CHUNK

    def body(x_ref, dt_ref, dtt_ref, br_ref, bc_ref, ar_ref, ac_ref, dsk_ref, prev_ref, dy_ref,
             dx_ref, ddt_ref, dpar_ref, dstate):
        @pl.when(pl.program_id(0) == 0)
        def _():
            dstate[...] = jnp.zeros_like(dstate)
            dpar_ref[...] = jnp.zeros_like(dpar_ref)

        lower, upper_b, zr, dtc, a_row, acum, acum_t = _ssd_chunk_common(
            dt_ref, dtt_ref, br_ref, bc_ref, ar_ref, ac_ref)
        strict = (lax.broadcasted_iota(jnp.int32, (CHUNK, CHUNK), 1)
                  < lax.broadcasted_iota(jnp.int32, (CHUNK, CHUNK), 0))
        strict_b = strict.astype(BF16)
        col2 = lax.broadcasted_iota(jnp.int32, (CHUNK, 2 * CHUNK), 1)
        strict2 = (jnp.where(col2 >= CHUNK, col2 - CHUNK, col2)
                   < lax.broadcasted_iota(jnp.int32, (CHUNK, 2 * CHUNK), 0))
        da_in = jnp.zeros((CHUNK, LANE), F32)
        r_off = jnp.zeros((CHUNK, LANE), F32)
        c_int = jnp.zeros((CHUNK, LANE), F32)
        c_row = jnp.zeros((1, LANE), F32)
        ddt = jnp.zeros((CHUNK, LANE), F32)
        dskip = jnp.zeros((1, LANE), F32)
        for g in range(SSD_G):
            bb = _mx(x_ref[:, _gs(B_OFF, g)])
            cb_ = _mx(x_ref[:, _gs(C_OFF, g)])
            cbm = _dot(cb_, bb, 1, 1)
            dcb = jnp.zeros((CHUNK, CHUNK), F32)
            dc_acc = jnp.zeros((CHUNK, SSD_N), F32)
            db_acc = jnp.zeros((CHUNK, SSD_N), F32)
            for e in range(PAIRS_PER_GROUP):
                q = g * PAIRS_PER_GROUP + e
                oh0, acol0, dcol0, alast0, decay0 = _head_terms(2 * q, lower, dtc, acum, acum_t)
                oh1, acol1, dcol1, alast1, decay1 = _head_terms(2 * q + 1, lower, dtc, acum, acum_t)
                x = x_ref[:, _ps(q)]
                dy = dy_ref[:, _ps(q)]
                dcol = _lanes(dcol0, dcol1)
                xdt = x * dcol
                xb = _mx(xdt)
                eacol = _lanes(jnp.exp(acol0), jnp.exp(acol1))
                ds = _lanes(jnp.exp(alast0 - acol0), jnp.exp(alast1 - acol1))
                ealast = _rows(jnp.exp(alast0), jnp.exp(alast1))
                dyb = _mx(dy)
                dyb0, dyb1 = _mx(_lanes(dy, 0.0)), _mx(_lanes(0.0, dy))
                dsh = dstate[q]
                dshb = _mx(dsh)
                prev = prev_ref[0, q]
                prevb = _mx(prev)
                dxdt_inter = ds * _dot(bb, dshb, 1, 1)
                dxdt = _lanes(_dot(_mx(cbm * decay0), dyb, 0, 0), _dot(_mx(cbm * decay1), dyb, 0, 0)) + dxdt_inter
                dwl0 = _dot(dyb0, xb, 1, 1) * decay0
                dwl1 = _dot(dyb1, xb, 1, 1) * decay1
                dcb = dcb + dwl0 + dwl1
                dyeb = _mx(dy * eacol)
                dc_acc = dc_acc + _dot(dyeb, prevb, 1, 0)
                db_acc = db_acc + _dot(_mx(xdt * ds), dshb, 1, 0)
                dstate[q] = _dot(dyeb, cb_, 0, 0) + ealast * dsh
                above = _exact_dot(upper_b, jnp.concatenate([dwl0 * cbm, dwl1 * cbm], axis=1), 1, 0, False)
                above = jnp.where(strict2, above, 0.0)
                da_in = (da_in + jnp.sum(above[:, :CHUNK], axis=1, keepdims=True) * oh0
                         + jnp.sum(above[:, CHUNK:], axis=1, keepdims=True) * oh1)
                y_off = _dot(cb_, prevb, 1, 1) * eacol
                r0, r1 = _lane_halves(dy * y_off)
                r_off = r_off + r0 * oh0 + r1 * oh1
                c0, c1 = _lane_halves(xdt * dxdt_inter)
                c_int = c_int + c0 * oh0 + c1 * oh1
                both = jnp.sum(dsh * prev, axis=1, keepdims=True) * ealast
                c_row = (c_row + jnp.sum(_rows(both, 0.0), axis=0, keepdims=True) * oh0
                         + jnp.sum(_rows(0.0, both), axis=0, keepdims=True) * oh1)
                t0, t1 = _lane_halves(dxdt * x)
                ddt = ddt + t0 * oh0 + t1 * oh1
                dx_ref[:, _ps(q)] = dxdt * dcol + dy * dsk_ref[q]
                k0, k1 = _lane_halves(dy * x)
                dskip = (dskip + jnp.sum(k0, axis=0, keepdims=True) * oh0 + jnp.sum(k1, axis=0, keepdims=True) * oh1)
            dcbb = _mx(dcb)
            dx_ref[:, _gs(C_OFF, g)] = dc_acc + _dot(dcbb, bb, 1, 0)
            dx_ref[:, _gs(B_OFF, g)] = db_acc + _dot(dcbb, cb_, 0, 0)
        da = (da_in + _exact_dot(upper_b, r_off, 1, 0, False) + _exact_dot(strict_b, c_int, 1, 0, False) + c_row)
        draw = (ddt + da * a_row) * _sigmoid(zr)
        ddt_ref[...] = draw
        dpar_ref[0:1, :] += jnp.sum(draw, axis=0, keepdims=True)
        dpar_ref[1:2, :] += jnp.sum(da * dtc, axis=0, keepdims=True) * a_row
        dpar_ref[2:3, :] += dskip

    rev = nc - 1
    psp = pl.BlockSpec((1, SSD_PAIRS, LANE, SSD_N), lambda c: (rev - c, 0, 0, 0))
    return pl.pallas_call(
        body, name=name, grid=(nc,),
        in_specs=_ssd_in_specs(rev) + [psp, pl.BlockSpec((CHUNK, SSD_W), lambda c: (rev - c, 0))],
        out_specs=[pl.BlockSpec((CHUNK, CONV_DIM), lambda c: (rev - c, 0)),
                   pl.BlockSpec((CHUNK, LANE), lambda c: (rev - c, 0)), pl.BlockSpec((8, LANE), lambda c: (0, 0))],
        out_shape=[jax.ShapeDtypeStruct((s_, CONV_DIM), F32), jax.ShapeDtypeStruct((s_, LANE), F32),
                   jax.ShapeDtypeStruct((8, LANE), F32)],
        scratch_shapes=[pltpu.VMEM((SSD_PAIRS, LANE, SSD_N), F32)],
        compiler_params=_params("arbitrary"),
    )(xbc, small, dtt, bias_r, bias_c, alog_r, alog_c, dsk, prev, dy)


GN = SSD_W // SSD_G


def _gated_norm_fwd(y, z, w, cat, name="gated_norm_fwd"):
    s_, f = y.shape
    tr = _row_tile(s_)

    def body(y_ref, z_ref, w_ref, cat_ref, o_ref):
        for g in range(SSD_G):
            sl = slice(g * GN, (g + 1) * GN)
            gg = y_ref[:, sl] * _silu(z_ref[:, sl])
            r = lax.rsqrt(jnp.mean(gg * gg, axis=-1, keepdims=True) + EPS)
            o_ref[:, sl] = (gg * r * w_ref[:, sl]).astype(o_ref.dtype)

    row = pl.BlockSpec((tr, f), lambda i: (i, 0))
    wsp = pl.BlockSpec((1, f), lambda i: (0, 0))
    return pl.pallas_call(
        body, name=name, grid=(s_ // tr,),
        in_specs=[row, row, wsp, pl.BlockSpec(memory_space=pl.ANY)], out_specs=pl.BlockSpec((tr, f), lambda i: (i, 1)),
        out_shape=jax.ShapeDtypeStruct(cat.shape, cat.dtype), input_output_aliases={3: 0},
        compiler_params=_params("parallel"),
    )(y, z, w.reshape(1, f), cat)


def _gated_norm_bwd(y, z, w, dout, name="gated_norm_bwd"):
    s_, f = y.shape
    tr = _row_tile(s_)

    def body(y_ref, z_ref, w_ref, do_ref, dy_ref, dz_ref, dw_ref):
        @pl.when(pl.program_id(0) == 0)
        def _():
            dw_ref[...] = jnp.zeros_like(dw_ref)

        for g in range(SSD_G):
            sl = slice(g * GN, (g + 1) * GN)
            yv = y_ref[:, sl]
            zv = z_ref[:, sl]
            dov = do_ref[:, sl].astype(F32)
            sz = _silu(zv)
            gg = yv * sz
            r = lax.rsqrt(jnp.mean(gg * gg, axis=-1, keepdims=True) + EPS)
            gw = dov * w_ref[:, sl]
            c = jnp.mean(gw * gg, axis=-1, keepdims=True)
            dgg = r * gw - gg * (r * r * r * c)
            dy_ref[:, sl] = dgg * sz
            dz_ref[:, sl] = (dgg * yv * _dsilu(zv)).astype(dz_ref.dtype)
            dw_ref[:, sl] += jnp.sum(dov * gg * r, axis=0, keepdims=True)

    row = pl.BlockSpec((tr, f), lambda i: (i, 0))
    wsp = pl.BlockSpec((1, f), lambda i: (0, 0))
    return pl.pallas_call(
        body, name=name, grid=(s_ // tr,),
        in_specs=[row, row, wsp, pl.BlockSpec((tr, f), lambda i: (i, 1))], out_specs=[row, row, wsp],
        out_shape=[jax.ShapeDtypeStruct((s_, f), F32), jax.ShapeDtypeStruct((s_, f), MXU_DTYPE),
                   jax.ShapeDtypeStruct((1, f), F32)],
        compiler_params=_params("arbitrary"),
    )(y, z, w.reshape(1, f), dout)


def _ffn_fwd(vv, w_gate, w_up, name="ffn_gate_up"):
    s_, d = vv.shape
    nb, f8, _ = w_gate.shape
    tm = _pick(s_, (1024, 512, 256, 128))

    def body(v_ref, wg_ref, wu_ref, g_ref, u_ref, a_ref):
        for rs in _row_slices(tm, 16):
            a = _mx(v_ref[rs, :])
            g = _dot(a, _mx(wg_ref[...]), 1, 1)
            u = _dot(a, _mx(wu_ref[...]), 1, 1)
            s = _sigmoid(g)
            gs = g * s
            g_ref[rs, :] = (u * (s * (1.0 + g * (1.0 - s)))).astype(g_ref.dtype)
            u_ref[rs, :] = gs.astype(u_ref.dtype)
            a_ref[rs, :] = (gs * u).astype(a_ref.dtype)

    wsp = pl.BlockSpec((None, f8, d), lambda j, i: (j, 0, 0))
    osp = pl.BlockSpec((None, tm, f8), lambda j, i: (j, i, 0))
    return pl.pallas_call(
        body, name=name, grid=(nb, s_ // tm),
        in_specs=[pl.BlockSpec((tm, d), lambda j, i: (i, 0)), wsp, wsp], out_specs=[osp] * 3,
        out_shape=[jax.ShapeDtypeStruct((nb, s_, f8), MXU_DTYPE)] * 3,
        compiler_params=_params("parallel", "parallel"),
    )(vv, w_gate, w_up)


def _ffn_bwd_act(dffn, w_down, gate, up, name="ffn_d_act"):
    s_, d = dffn.shape
    nb, f8, _ = w_down.shape
    tm = _pick(s_, (1024, 512, 256, 128))

    def body(d_ref, w_ref, g_ref, u_ref, dg_ref, du_ref):
        for rs in _row_slices(tm, 16):
            dact = _dot(_mx(d_ref[rs, :]), _mx(w_ref[...]), 1, 1)
            dg_ref[rs, :] = (dact * g_ref[rs, :].astype(F32)).astype(dg_ref.dtype)
            du_ref[rs, :] = (dact * u_ref[rs, :].astype(F32)).astype(du_ref.dtype)

    osp = pl.BlockSpec((None, tm, f8), lambda j, i: (j, i, 0))
    return pl.pallas_call(
        body, name=name, grid=(nb, s_ // tm),
        in_specs=[pl.BlockSpec((tm, d), lambda j, i: (i, 0)), pl.BlockSpec((None, f8, d), lambda j, i: (j, 0, 0)),
                  osp, osp],
        out_specs=[osp, osp], out_shape=[jax.ShapeDtypeStruct((nb, s_, f8), MXU_DTYPE)] * 2,
        compiler_params=_params("parallel", "parallel"),
    )(dffn, w_down, gate, up)


def _ffn_bwd_in(dgate, w_gate, dup, w_up, name="ffn_d_in"):
    nb, s_, f8 = dgate.shape
    d = w_gate.shape[2]
    tm = _pick(s_, (1024, 512, 256, 128))
    tn = _pick(d, (2048, 1024, 512, 256, 128))

    def body(dg_ref, wg_ref, du_ref, wu_ref, o_ref, acc):
        j = pl.program_id(2)

        @pl.when(j == 0)
        def _():
            acc[...] = jnp.zeros_like(acc)

        for rs in _row_slices(tm, 16):
            acc[rs, :] += (_dot(_mx(dg_ref[rs, :]), _mx(wg_ref[...]), 1, 0)
                           + _dot(_mx(du_ref[rs, :]), _mx(wu_ref[...]), 1, 0))

        @pl.when(j == nb - 1)
        def _():
            o_ref[...] = acc[...]

    asp = pl.BlockSpec((None, tm, f8), lambda i, n, j: (j, i, 0))
    wsp = pl.BlockSpec((None, f8, tn), lambda i, n, j: (j, 0, n))
    return pl.pallas_call(
        body, name=name, grid=(s_ // tm, d // tn, nb),
        in_specs=[asp, wsp, asp, wsp], out_specs=pl.BlockSpec((tm, tn), lambda i, n, j: (i, n)),
        out_shape=jax.ShapeDtypeStruct((s_, d), F32), scratch_shapes=[pltpu.VMEM((tm, tn), F32)],
        compiler_params=_params("parallel", "parallel", "arbitrary"),
    )(dgate, w_gate, dup, w_up)


def _adam_math(g, w, m, v):
    m2 = ADAM_B1 * m + (1.0 - ADAM_B1) * g
    v2 = ADAM_B2 * v + (1.0 - ADAM_B2) * (g * g)
    m_hat = m2 / (1.0 - ADAM_B1 ** ADAM_STEP)
    v_hat = v2 / (1.0 - ADAM_B2 ** ADAM_STEP)
    delta = -ADAM_LR * (m_hat / (jnp.sqrt(v_hat) + ADAM_EPS) + ADAM_WD * w)
    return delta, m2, v2


def _adamw(parts, own, me, w, m, v, name="adamw"):
    nd, r_, c = parts.shape
    tr = _pick(r_, (128, 64, 32, 16))
    tc = c
    if tr == r_ and r_ > 128:
        tc = _pick(c, (256, 128))

    def body(me_ref, p_ref, own_ref, w_ref, m_ref, v_ref, g_ref, d_ref, m2_ref, v2_ref):
        mine = me_ref[0]
        g = jnp.zeros((tr, tc), F32)
        for i in range(nd):
            g = g + jnp.where(mine == i, own_ref[...], p_ref[i]).astype(F32)
        delta, m2, v2 = _adam_math(g, w_ref[...], m_ref[...], v_ref[...])
        g_ref[...] = g
        d_ref[...] = delta
        m2_ref[...] = m2
        v2_ref[...] = v2

    row = pl.BlockSpec((tr, tc), lambda i, j, me_: (i, j))
    gs = pltpu.PrefetchScalarGridSpec(
        num_scalar_prefetch=1, grid=(r_ // tr, c // tc),
        in_specs=[pl.BlockSpec((nd, tr, tc), lambda i, j, me_: (0, i, j)),
                  pl.BlockSpec((None, tr, tc), lambda i, j, me_: (me_[0], i, j)), row, row, row],
        out_specs=[row] * 4)
    return pl.pallas_call(
        body, name=name, grid_spec=gs, out_shape=[jax.ShapeDtypeStruct((r_, c), F32)] * 4,
        compiler_params=_params("parallel", "parallel"),
    )(me, parts, own, w, m, v)


def _adamw_small(parts, w, m, v, name="adamw_small"):
    nd = parts.shape[0]

    def body(p_ref, w_ref, m_ref, v_ref, g_ref, d_ref, m2_ref, v2_ref):
        g = p_ref[0]
        for i in range(1, nd):
            g = g + p_ref[i]
        delta, m2, v2 = _adam_math(g, w_ref[...], m_ref[...], v_ref[...])
        g_ref[...] = g
        d_ref[...] = delta
        m2_ref[...] = m2
        v2_ref[...] = v2

    return pl.pallas_call(
        body, name=name, out_shape=[jax.ShapeDtypeStruct(w.shape, F32)] * 4,
        compiler_params=pltpu.CompilerParams(vmem_limit_bytes=VMEM_LIMIT_BYTES),
    )(parts, w, m, v)


_HBM = pl.BlockSpec(memory_space=pltpu.HBM)
_MESH = pl.DeviceIdType.MESH


def _all_gather(xs, name):
    na = len(xs)

    def body(*refs):
        x_refs, out_refs = refs[:na], refs[na:2 * na]
        send_sems, recv_sems, local_sems = refs[2 * na:]
        x, y, c = lax.axis_index("x"), lax.axis_index("y"), lax.axis_index("c")
        me, sibling = (x, y, c), (x, y, 1 - c)
        near = [(1 - x, y), (x, 1 - y)]
        chips = near + [(1 - x, 1 - y)]
        relay_from = (x + c * (1 - 2 * x), y + (1 - c) * (1 - 2 * y))
        relay_to = (x + (1 - c) * (1 - 2 * x), y + c * (1 - 2 * y))

        def slot(a, px, py, pc):
            return out_refs[a].at[4 * px + 2 * py + pc]

        def copy(a, k, block, to, src=None):
            return pltpu.make_async_remote_copy(
                src_ref=slot(a, *block) if src is None else src, dst_ref=slot(a, *block),
                send_sem=send_sems.at[a, k], recv_sem=recv_sems.at[a, k], device_id=to, device_id_type=_MESH)

        mine = [pltpu.make_async_copy(x_refs[a], slot(a, *me), local_sems.at[a]) for a in range(na)]
        started = []
        for a in range(na):
            mine[a].start()
            first = [copy(a, 0, me, sibling, src=x_refs[a])]
            first += [copy(a, 1 + j, me, (*chip, c), src=x_refs[a]) for j, chip in enumerate(near)]
            for cp in first:
                cp.start()
            started += first
        for a in range(na):
            for j, chip in enumerate(chips):
                copy(a, 1 + j, (*chip, c), me).wait_recv()
                fwd = copy(a, 4 + j, (*chip, c), sibling)
                fwd.start()
                started.append(fwd)
                if j == len(near) - 1:
                    relay = copy(a, 1 + len(near), (*relay_from, c), (*relay_to, c))
                    relay.start()
                    started.append(relay)
        for a in range(na):
            copy(a, 0, sibling, me).wait_recv()
            for j, chip in enumerate(chips):
                copy(a, 4 + j, (*chip, 1 - c), me).wait_recv()
        for cp in started:
            cp.wait_send()
        for cp in mine:
            cp.wait()

    return pl.pallas_call(
        body, name=name, out_shape=[jax.ShapeDtypeStruct((N_DEV,) + t.shape, t.dtype) for t in xs],
        in_specs=[_HBM] * na, out_specs=[_HBM] * na,
        scratch_shapes=[pltpu.SemaphoreType.DMA((na, 7)), pltpu.SemaphoreType.DMA((na, 7)),
                        pltpu.SemaphoreType.DMA((na,))],
    )(*xs)


_SEM = pl.BlockSpec(memory_space=pltpu.SEMAPHORE)
_EFFECT = pltpu.SideEffectType.DATAFLOW_SIDE_EFFECTING


def _peers(x, y, c):
    out = []
    for k in range(1, N_DEV):
        px = 1 - x if k & 4 else x
        py = 1 - y if k & 2 else y
        pc = 1 - c if k & 1 else c
        out.append(((px, py, pc), 4 * px + 2 * py + pc))
    return out


def _push_copies(scatter, src_refs, land_refs, send_sems, recv_sems):
    x, y, c = lax.axis_index("x"), lax.axis_index("y"), lax.axis_index("c")
    me = 4 * x + 2 * y + c
    pairs = []
    for a, (src, land) in enumerate(zip(src_refs, land_refs)):
        for k, (peer, slot) in enumerate(_peers(x, y, c)):
            out_src = src.at[slot] if scatter else src
            si = a * (N_DEV - 1) + k
            send = pltpu.make_async_remote_copy(src_ref=out_src, dst_ref=land.at[me], send_sem=send_sems.at[si],
                                                recv_sem=recv_sems.at[si], device_id=peer, device_id_type=_MESH)
            recv = pltpu.make_async_remote_copy(src_ref=out_src, dst_ref=land.at[slot], send_sem=send_sems.at[si],
                                                recv_sem=recv_sems.at[si], device_id=peer, device_id_type=_MESH)
            pairs.append((send, recv))
    return pairs


def _push_start(srcs, scatter, dep, name):
    na = len(srcs)
    shapes = [t.shape[1:] if scatter else t.shape for t in srcs]
    lands = [pltpu.with_memory_space_constraint(lax.empty((N_DEV,) + s, t.dtype), pltpu.HBM) for s, t in zip(shapes, srcs)]

    def body(*refs):
        src_refs, land_refs = refs[:na], refs[na:2 * na]
        send_sems, recv_sems = refs[2 * na + 1], refs[2 * na + 2]
        token = refs[-1]
        for send, _ in _push_copies(scatter, src_refs, land_refs, send_sems, recv_sems):
            send.start()
        token[...] = jnp.zeros_like(token)

    sem = pltpu.SemaphoreType.DMA((na * (N_DEV - 1),))
    outs = pl.pallas_call(
        body, name=name,
        out_shape=(sem, sem) + tuple(pltpu.HBM(t.shape, t.dtype) for t in srcs)
        + tuple(pltpu.HBM(t.shape, t.dtype) for t in lands) + (jax.ShapeDtypeStruct((8, LANE), F32),),
        in_specs=[_HBM] * (2 * na) + [pl.BlockSpec(memory_space=pl.ANY)],
        out_specs=(_SEM, _SEM) + (_HBM,) * (2 * na) + (pl.BlockSpec(memory_space=pltpu.VMEM),),
        input_output_aliases={i: 2 + i for i in range(2 * na)},
        compiler_params=pltpu.CompilerParams(has_side_effects=_EFFECT),
    )(*[pltpu.with_memory_space_constraint(t, pltpu.HBM) for t in srcs], *lands, dep)
    return outs[0], outs[1], outs[2:2 + na], outs[2 + na:2 + 2 * na], outs[-1]


def _push_wait(send_sems, recv_sems, src_thru, land_thru, scatter, after, name):
    na = len(src_thru)

    def body(*refs):
        src_refs, land_refs = refs[:na], refs[na:2 * na]
        ssem, rsem = refs[2 * na], refs[2 * na + 1]
        for send, recv in _push_copies(scatter, src_refs, land_refs, ssem, rsem):
            send.wait_send()
            recv.wait_recv()

    outs = pl.pallas_call(
        body, name=name,
        out_shape=tuple(pltpu.HBM(t.shape, t.dtype) for t in src_thru) + tuple(pltpu.HBM(t.shape, t.dtype) for t in land_thru),
        in_specs=[_HBM] * (2 * na) + [_SEM, _SEM, pl.BlockSpec(memory_space=pl.ANY)],
        out_specs=(_HBM,) * (2 * na),
        input_output_aliases={i: i for i in range(2 * na)},
        compiler_params=pltpu.CompilerParams(has_side_effects=_EFFECT),
    )(*src_thru, *land_thru, send_sems, recv_sems, after)
    return outs[:na], outs[na:]


def _exchange_behind(srcs, scatter, dep, name):
    send_sems, recv_sems, thru, lands, token = _push_start(srcs, scatter, dep, name + "_start")

    def finish(after, place=True):
        src_done, land_done = _push_wait(send_sems, recv_sems, thru, lands, scatter, after, name + "_wait")
        if not place:
            return land_done, src_done
        return _place_own(land_done, src_done, scatter, name + "_own")

    return token[0, 0], finish


def _place_own(lands, srcs, scatter, name):
    me = (4 * lax.axis_index("x") + 2 * lax.axis_index("y") + lax.axis_index("c")).astype(jnp.int32).reshape(1)
    outs = []
    for a, (land, src) in enumerate(zip(lands, srcs)):
        r_, c_ = land.shape[1:]
        tr = _pick(r_, (512, 256, 128, 64, 32, 16))

        def body(me_ref, land_ref, src_ref, out_ref):
            out_ref[...] = src_ref[...]

        src_spec = (pl.BlockSpec((None, tr, c_), lambda i, me_: (me_[0], i, 0)) if scatter
                    else pl.BlockSpec((tr, c_), lambda i, me_: (i, 0)))
        gs = pltpu.PrefetchScalarGridSpec(
            num_scalar_prefetch=1, grid=(r_ // tr,),
            in_specs=[pl.BlockSpec(memory_space=pl.ANY), src_spec],
            out_specs=pl.BlockSpec((None, tr, c_), lambda i, me_: (me_[0], i, 0)))
        outs.append(pl.pallas_call(
            body, name=f"{name}_{a}", grid_spec=gs, out_shape=jax.ShapeDtypeStruct(land.shape, land.dtype),
            input_output_aliases={1: 0}, compiler_params=_params("arbitrary"),
        )(me, land, src))
    return outs


_TRANSPOSED = ("w_in", "w_uq", "w_gate", "w_up")
_CQKV = (0, Q_RANK + KV_RANK)
_KR = (_CQKV[1], _CQKV[1] + ROPE)
_Z = (_KR[1], _KR[1] + SSD_W)
_XBC = (_Z[1], _Z[1] + CONV_DIM)
_DT = (_XBC[1], _XBC[1] + SSD_H)


def _win_segments(w_in_t):
    w = w_in_t.reshape(D_IN, D_MODEL)
    small = jnp.concatenate([w[_KR[0]:_KR[1]], w[_DT[0]:_DT[1]],
                             jnp.zeros((LANE - ROPE - SSD_H, D_MODEL), w.dtype)], axis=0)
    return w[_CQKV[0]:_CQKV[1]], w[_Z[0]:_Z[1]], w[_XBC[0]:_XBC[1]], small


def _win_from_segments(g_cqkv, g_z, g_xbc, g_small):
    w = jnp.concatenate([g_cqkv, g_small[:ROPE], g_z, g_xbc, g_small[ROPE:ROPE + SSD_H]], axis=0)
    return w.reshape(N_DEV, D_IN // N_DEV, D_MODEL)


_SMALL = (("q_norm_w", 512), ("kv_norm_w", 512), ("conv_b", CONV_DIM), ("dt_bias", SSD_H), ("a_log", SSD_H),
          ("d_skip", SSD_H), ("ssd_norm_w", SSD_W), ("attn_out_norm_w", 1024), ("pre_mix_norm_w", D_MODEL),
          ("post_mix_norm_w", D_MODEL), ("pre_ffn_norm_w", D_MODEL), ("post_ffn_norm_w", D_MODEL),
          ("conv_w", CONV_K * CONV_DIM))
_SMALL_ROWS = -(-sum(-(-n // LANE) for _, n in _SMALL) // 8) * 8


def _pack_small(vals):
    rows = []
    for name, n in _SMALL:
        v = vals[name].reshape(-1).astype(F32)
        pad = -(-n // LANE) * LANE
        rows.append(jnp.pad(v, (0, pad - n)).reshape(-1, LANE))
    m = jnp.concatenate(rows, axis=0)
    return jnp.pad(m, ((0, _SMALL_ROWS - m.shape[0]), (0, 0)))


def _unpack_small(m):
    out, r = {}, 0
    for name, n in _SMALL:
        nr = -(-n // LANE)
        out[name] = m[r:r + nr].reshape(-1)[:n]
        r += nr
    return out


def _head_row(v):
    return jnp.pad(v.reshape(1, -1).astype(F32), ((0, 0), (HEAD_LANE, LANE - HEAD_LANE - v.shape[-1])))


def _local_step(x, positions, target, wg, small, weights, on_grads):
    w_cqkv, w_z, w_xbc, w_small = _win_segments(wg["w_in"])
    conv_w = wg["conv_w"]
    conv_b = small["conv_b"].reshape(1, CONV_DIM)
    qkv_norm_w = jnp.concatenate([small["q_norm_w"], small["kv_norm_w"]])
    attn_norm_w = small["attn_out_norm_w"].reshape(1, HEADS * VDIM)
    scale = QK ** -0.5

    inv_freq = ROPE_THETA ** (-jnp.arange(0, ROPE, 2, dtype=F32) / ROPE)
    ang = positions.astype(F32)[:, None] * inv_freq
    cos2 = jnp.tile(jnp.cos(ang), (1, 2))
    sin2 = jnp.tile(jnp.sin(ang), (1, 2))

    u = _rms_fwd(x, small["pre_mix_norm_w"], out_dtype=MXU_DTYPE, name="pre_mix_norm")
    cqkv = _mm(u, w_cqkv, "nt", name="in_proj_qkv")
    z = _mm(u, w_z, "nt", name="in_proj_z")
    xbc = _mm(u, w_xbc, "nt", name="in_proj_xbc")
    sm = _mm(u, w_small, "nt", name="in_proj_small")

    w_uq, w_ukv = weights("qkv_up", cqkv)
    qkvn = _rms_fwd(cqkv, qkv_norm_w, groups=2, out_dtype=MXU_DTYPE, name="qkv_norm")
    q_h = _q_up(qkvn, w_uq, cos2, sin2, scale)
    k_h, v_h = _kv_up(qkvn, w_ukv, sm, cos2, sin2)
    o_h, lse = _flash_fwd(q_h, k_h, v_h)
    cat = _hnorm_fwd(o_h, attn_norm_w, D_MODEL)
    w_out = weights("out", o_h)[0].reshape(D_MODEL, D_MODEL)

    xbc_act = _conv_fwd(xbc, conv_w, conv_b)
    dtt = jnp.transpose(sm[:, HEAD_LANE:HEAD_LANE + SSD_H])
    ssd_args = (xbc_act, sm, dtt, _head_row(small["dt_bias"]), small["dt_bias"].reshape(SSD_H, 1),
                _head_row(small["a_log"]), small["a_log"].reshape(SSD_H, 1),
                jnp.broadcast_to(small["d_skip"].reshape(SSD_H, 1), (SSD_H, SSD_P)).reshape(SSD_PAIRS, 1, LANE))
    y_ssd, prev = _ssd_fwd(*ssd_args)
    cat = _gated_norm_fwd(y_ssd, z, small["ssd_norm_w"], cat)

    mix = _mm(cat, w_out, "nn", name="out_proj")
    h1, vv = _norm_res_norm(mix, x, small["post_mix_norm_w"], small["pre_ffn_norm_w"])

    w_gate, w_up = weights("ffn_in", mix)
    gate, up, act = _ffn_fwd(vv, w_gate, w_up)
    w_down, = weights("ffn_out", act)
    ffn = _mm(act, w_down, "nn", a_blk=True, b_blk=True, fuse=2, wide=True, name="ffn_down")
    loss_blk, dy, dffn, g_post_ffn = _loss_head(ffn, h1, target, small["post_ffn_norm_w"])

    g_down = _mm(act, dffn, "tn", a_blk=True, out_blk=True, out_dtype=MXU_DTYPE, name="g_down")
    dgate, dup = _ffn_bwd_act(dffn, w_down, gate, up)
    dvv = _ffn_bwd_in(dgate, w_gate, dup, w_up)
    g_gate = _mm(dgate, vv, "tn", a_blk=True, out_blk=True, out_dtype=MXU_DTYPE, name="g_gate")
    g_up = _mm(dup, vv, "tn", a_blk=True, out_blk=True, out_dtype=MXU_DTYPE, name="g_up")
    pre_ffn_w = small["pre_ffn_norm_w"] + on_grads("ffn", [g_gate, g_up, g_down])
    dh1, dmix, g_pre_ffn, g_post_mix = _norm_res_norm_bwd(h1, pre_ffn_w, dvv, dy, mix, small["post_mix_norm_w"])

    dcat = _mm(dmix, w_out, "nt", name="d_cat")
    g_out = _mm(cat, dmix, "tn", out_dtype=MXU_DTYPE, name="g_out")

    do_h, delta, g_attn_norm = _hnorm_bwd(o_h, attn_norm_w, dcat)
    dq_h, dk_h, dv_h = _flash_bwd(q_h, k_h, v_h, do_h, lse, delta)
    dq = _q_prep(dq_h, cos2, -sin2, scale, name="dq_post")

    dy_ssd, dz, g_ssd_norm = _gated_norm_bwd(y_ssd, z, small["ssd_norm_w"], dcat)
    dxbc_act, ddt, dpar = _ssd_bwd(*ssd_args, prev, dy_ssd)
    dkv, dsm = _dkv_post(dk_h, dv_h, ddt, cos2, -sin2)
    dpre, dwb = _conv_bwd_pre(xbc, conv_w, conv_b, dxbc_act)
    dxbc = _conv_bwd_in(dpre, conv_w)

    dqn = _mm(dq, w_uq, "nn", a_blk=True, b_blk=True, fuse=HEADS, name="d_qn")
    dkvn = _mm(dkv, w_ukv, "nt", a_blk=True, b_blk=True, fuse=HEADS, name="d_kvn")
    g_uq = _mm(dq, qkvn, "tn", a_blk=True, out_blk=True, b_cols=(0, Q_RANK), out_dtype=MXU_DTYPE, name="g_uq")
    g_ukv = _mm(qkvn, dkv, "tn", b_blk=True, out_blk=True, a_cols=(Q_RANK, KV_RANK), out_dtype=MXU_DTYPE, name="g_ukv")
    heads_token = on_grads("heads", [g_uq, g_ukv, g_out.reshape(N_DEV, D_MODEL // N_DEV, D_MODEL)])
    dcqkv, g_qkv_norm = _rms_bwd(cqkv, qkv_norm_w + heads_token, [dqn, dkvn], out_dtype=MXU_DTYPE, name="qkv_norm_bwd")

    g_in = _win_from_segments(_mm(dcqkv, u, "tn", out_dtype=MXU_DTYPE, name="g_in_qkv"),
                              _mm(dz, u, "tn", out_dtype=MXU_DTYPE, name="g_in_z"),
                              _mm(dxbc, u, "tn", out_dtype=MXU_DTYPE, name="g_in_xbc"),
                              _mm(dsm, u, "tn", out_dtype=MXU_DTYPE, name="g_in_small"))
    in_token = on_grads("in", [g_in])
    du = _mm_sum([dsm + in_token.astype(dsm.dtype), dcqkv, dz, dxbc], [w_small, w_cqkv, w_z, w_xbc], name="d_u")
    dx, g_pre_mix = _rms_bwd(x, small["pre_mix_norm_w"], [du], res=dh1, name="pre_mix_norm_bwd")

    hl = slice(HEAD_LANE, HEAD_LANE + SSD_H)
    g_small = {"q_norm_w": g_qkv_norm[0, :Q_RANK], "kv_norm_w": g_qkv_norm[0, Q_RANK:], "conv_b": dwb[CONV_K],
               "dt_bias": dpar[0, hl], "a_log": dpar[1, hl], "d_skip": dpar[2, hl], "ssd_norm_w": g_ssd_norm,
               "attn_out_norm_w": g_attn_norm, "pre_mix_norm_w": g_pre_mix, "post_mix_norm_w": g_post_mix,
               "pre_ffn_norm_w": g_pre_ffn, "post_ffn_norm_w": g_post_ffn, "conv_w": dwb[:CONV_K]}
    return loss_blk[0, 0], dx, g_small


_WEIGHT_ORDER = ("w_in", "q_norm_w", "w_uq", "kv_norm_w", "w_ukv", "conv_w", "conv_b", "dt_bias", "a_log", "d_skip",
                 "ssd_norm_w", "attn_out_norm_w", "w_out", "pre_mix_norm_w", "post_mix_norm_w", "pre_ffn_norm_w",
                 "post_ffn_norm_w", "w_gate", "w_up", "w_down")


def kernel(x, positions, w_in, q_norm_w, w_uq, kv_norm_w, w_ukv, conv_w, conv_b, dt_bias, a_log, d_skip, ssd_norm_w, attn_out_norm_w, w_out, pre_mix_norm_w, post_mix_norm_w, pre_ffn_norm_w, post_ffn_norm_w, w_gate, w_up, w_down, loss_target, m_w_in, m_q_norm_w, m_w_uq, m_kv_norm_w, m_w_ukv, m_conv_w, m_conv_b, m_dt_bias, m_a_log, m_d_skip, m_ssd_norm_w, m_attn_out_norm_w, m_w_out, m_pre_mix_norm_w, m_post_mix_norm_w, m_pre_ffn_norm_w, m_post_ffn_norm_w, m_w_gate, m_w_up, m_w_down, v_w_in, v_q_norm_w, v_w_uq, v_kv_norm_w, v_w_ukv, v_conv_w, v_conv_b, v_dt_bias, v_a_log, v_d_skip, v_ssd_norm_w, v_attn_out_norm_w, v_w_out, v_pre_mix_norm_w, v_post_mix_norm_w, v_pre_ffn_norm_w, v_post_ffn_norm_w, v_w_gate, v_w_up, v_w_down):
    w = dict(w_in=w_in, q_norm_w=q_norm_w, w_uq=w_uq, kv_norm_w=kv_norm_w, w_ukv=w_ukv, conv_w=conv_w, conv_b=conv_b,
             dt_bias=dt_bias, a_log=a_log, d_skip=d_skip, ssd_norm_w=ssd_norm_w, attn_out_norm_w=attn_out_norm_w,
             w_out=w_out, pre_mix_norm_w=pre_mix_norm_w, post_mix_norm_w=post_mix_norm_w,
             pre_ffn_norm_w=pre_ffn_norm_w, post_ffn_norm_w=post_ffn_norm_w, w_gate=w_gate, w_up=w_up, w_down=w_down)
    m = dict(w_in=m_w_in, q_norm_w=m_q_norm_w, w_uq=m_w_uq, kv_norm_w=m_kv_norm_w, w_ukv=m_w_ukv, conv_w=m_conv_w,
             conv_b=m_conv_b, dt_bias=m_dt_bias, a_log=m_a_log, d_skip=m_d_skip, ssd_norm_w=m_ssd_norm_w,
             attn_out_norm_w=m_attn_out_norm_w, w_out=m_w_out, pre_mix_norm_w=m_pre_mix_norm_w,
             post_mix_norm_w=m_post_mix_norm_w, pre_ffn_norm_w=m_pre_ffn_norm_w, post_ffn_norm_w=m_post_ffn_norm_w,
             w_gate=m_w_gate, w_up=m_w_up, w_down=m_w_down)
    v = dict(w_in=v_w_in, q_norm_w=v_q_norm_w, w_uq=v_w_uq, kv_norm_w=v_kv_norm_w, w_ukv=v_w_ukv, conv_w=v_conv_w,
             conv_b=v_conv_b, dt_bias=v_dt_bias, a_log=v_a_log, d_skip=v_d_skip, ssd_norm_w=v_ssd_norm_w,
             attn_out_norm_w=v_attn_out_norm_w, w_out=v_w_out, pre_mix_norm_w=v_pre_mix_norm_w,
             post_mix_norm_w=v_post_mix_norm_w, pre_ffn_norm_w=v_pre_ffn_norm_w, post_ffn_norm_w=v_post_ffn_norm_w,
             w_gate=v_w_gate, w_up=v_w_up, w_down=v_w_down)
    w, m, v = ({k: t[0] for k, t in d.items()} for d in (w, m, v))
    me = 4 * lax.axis_index("x") + 2 * lax.axis_index("y") + lax.axis_index("c")
    groups = {"qkv_up": ("w_uq", "w_ukv"), "out": ("w_out",), "ffn_in": ("w_gate", "w_up"), "ffn_out": ("w_down",)}
    cshard = CONV_DIM // N_DEV
    for name in _TRANSPOSED:
        w[name], m[name], v[name] = w[name].T, m[name].T, v[name].T

    shards = [w["w_in"].astype(MXU_DTYPE),
              jnp.stack(_split3(w["conv_w"])).reshape(3 * CONV_K, cshard).astype(MXU_DTYPE)]
    w_in_g, cw = _all_gather(shards, name="gather_weights")
    cw = cw.astype(F32).reshape(N_DEV, 3, CONV_K, cshard)
    wg = {"w_in": w_in_g, "conv_w": jnp.transpose(cw[:, 0] + cw[:, 1] + cw[:, 2], (1, 0, 2)).reshape(CONV_K, CONV_DIM)}
    arriving, dep, started = {}, wg["conv_w"], jnp.zeros((), F32)
    small = {name: w[name] for name, _ in _SMALL if name != "conv_w"}
    for group in ("qkv_up", "out", "ffn_in", "ffn_out"):
        token, arriving[group] = _exchange_behind([w[name].astype(MXU_DTYPE) for name in groups[group]], False,
                                                  dep, group + "_weights")
        started = started + token
        dep = jnp.zeros((8, LANE), F32) + started
    small["pre_mix_norm_w"] = small["pre_mix_norm_w"] + started

    leaving = {}

    def on_grads(group, gs):
        token, leaving[group] = _exchange_behind(gs, True, jnp.zeros((8, LANE), F32), group + "_grads")
        return token

    loss_local, dx, g_small = _local_step(x[0], positions[0], loss_target[0], wg, small,
                                          lambda group, after: arriving[group](after), on_grads)
    loss = lax.psum(loss_local, ("x", "y", "c"))

    recv = {}
    for group, names in (("ffn", ("w_gate", "w_up", "w_down")), ("heads", ("w_uq", "w_ukv", "w_out")), ("in", ("w_in",))):
        recv.update(zip(names, zip(*leaving[group](dx, place=False))))
    grads, deltas, new_m, new_v = {}, {}, {}, {}
    me1 = me.astype(jnp.int32).reshape(1)
    for name, (parts, own) in recv.items():
        outs = _adamw(parts, own, me1, w[name], m[name], v[name], name="adamw_" + name)
        if name in _TRANSPOSED:
            outs = [t.T for t in outs]
        grads[name], deltas[name], new_m[name], new_v[name] = outs

    def embed(t):
        return lax.dynamic_update_slice(jnp.zeros((CONV_K, CONV_DIM), F32), t, (0, me * cshard))

    parts_s = _all_gather([_pack_small(g_small)], name="gather_small_grads")[0]
    packs = [_pack_small({**{n_: d[n_] for n_, _ in _SMALL if n_ != "conv_w"}, "conv_w": embed(d["conv_w"])})
             for d in (w, m, v)]
    outs = [_unpack_small(t) for t in _adamw_small(parts_s, *packs)]
    for name, n in _SMALL:
        for dst, src in zip((grads, deltas, new_m, new_v), outs):
            if name == "conv_w":
                dst[name] = lax.dynamic_slice(src[name].reshape(CONV_K, CONV_DIM), (0, me * cshard), (CONV_K, cshard))
            else:
                dst[name] = src[name]

    def lead(d):
        return [d[name][None] for name in _WEIGHT_ORDER]

    return (loss, dx[None], *lead(grads), *lead(deltas), *lead(new_m), *lead(new_v))
```

```python
import numpy as np

import jax
import jax.numpy as jnp
from jax import lax
from jax.experimental import pallas as pl
from jax.experimental.pallas import tpu as pltpu

F32 = jnp.float32
BF16 = jnp.bfloat16
MXU_DTYPE = jnp.bfloat16
EPS = 1e-6
VMEM_LIMIT_BYTES = 48 * 1024 * 1024
K_TILE_MAX = 2048

N_DEV = 8
D_MODEL = 2048
Q_RANK = 512
KV_RANK = 512
ROPE = 64
HALF = ROPE // 2
HEADS = 8
NOPE = 128
VDIM = 128
QK = NOPE + ROPE
SSD_W = 1024
SSD_H = 16
SSD_P = 64
SSD_G = 2
SSD_E = SSD_H // SSD_G
SSD_N = 128
CHUNK = 128
CONV_K = 4
CONV_DIM = SSD_W + 2 * SSD_G * SSD_N
B_OFF = SSD_W
C_OFF = SSD_W + SSD_G * SSD_N
D_FF = 5632
D_IN = Q_RANK + KV_RANK + ROPE + SSD_W + CONV_DIM + SSD_H
ROPE_THETA = 10000.0
LANE = 128
HEAD_LANE = ROPE

ADAM_LR = 0.001
ADAM_B1 = 0.9
ADAM_B2 = 0.999
ADAM_EPS = 1e-08
ADAM_WD = 0.01
ADAM_STEP = 10


def _pick(n, cands):
    for c in cands:
        if n % c == 0:
            return c
    return n


def _params(*sem):
    return pltpu.CompilerParams(dimension_semantics=sem, vmem_limit_bytes=VMEM_LIMIT_BYTES)


def _sigmoid(x):
    return 1.0 / (1.0 + jnp.exp(-x))


def _silu(x):
    return x * _sigmoid(x)


def _dsilu(x):
    s = _sigmoid(x)
    return s * (1.0 + x * (1.0 - s))


def _softplus(x):
    e = jnp.exp(-jnp.abs(x))
    small = e * (1.0 - e * (0.5 - e * (1.0 / 3.0)))
    return jnp.maximum(x, 0.0) + jnp.where(e < 0.01, small, jnp.log(1.0 + e))


def _dot(a, b, ca, cb):
    return lax.dot_general(a, b, (((ca,), (cb,)), ((), ())), preferred_element_type=F32)


def _mx(v):
    return v.astype(MXU_DTYPE)


def _split3(a):
    hi = a.astype(BF16)
    r1 = a - hi.astype(F32)
    mid = r1.astype(BF16)
    lo = (r1 - mid.astype(F32)).astype(BF16)
    return hi, mid, lo


def _exact_dot(a, b, ca, cb, split_a):
    if split_a:
        return sum(_dot(p, b, ca, cb) for p in _split3(a))
    return sum(_dot(a, p, ca, cb) for p in _split3(b))


MM_ROW_GROUPS = 4


def _row_slices(tm, align):
    ng = MM_ROW_GROUPS
    while ng > 1 and (tm % ng or (tm // ng) % align):
        ng //= 2
    return [slice(g * (tm // ng), (g + 1) * (tm // ng)) for g in range(ng)]


def _mm(a, b, mode, *, a_blk=False, b_blk=False, out_blk=False, a_cols=None, b_cols=None, add=None, out_dtype=F32,
        fuse=1, wide=False, tm_max=1024, name="mm"):
    a2, b2 = a.shape[-2:], b.shape[-2:]
    a_last = a2[1] if a_cols is None else a_cols[1]
    a_start = 0 if a_cols is None else a_cols[0]
    b_start = 0
    if b_cols is not None:
        assert mode != "nt"
        b_start, b2 = b_cols[0], (b2[0], b_cols[1])
    if mode == "nn":
        m, k, (k2, n) = a2[0], a_last, b2
    elif mode == "nt":
        m, k, (n, k2) = a2[0], a_last, b2
    else:
        k, m, (k2, n) = a2[0], a_last, b2
    assert k == k2, (a.shape, b.shape, mode)
    tm = _pick(m, tuple(c for c in (1024, 704, 512, 256, 128) if c <= tm_max))
    tn = _pick(n, ((2048,) if wide else ()) + (1024, 768, 704, 512, 256, 192, 128))
    k_max = 2 * K_TILE_MAX if mode == "tn" else K_TILE_MAX
    tk = k if k <= k_max else _pick(k, (K_TILE_MAX, 1024, 512))
    nk = k // tk
    jo = N_DEV if out_blk else 1
    reduce_blocks = a_blk and b_blk and not out_blk
    assert fuse == 1 or reduce_blocks
    jr = N_DEV // fuse if reduce_blocks else 1
    ca, cb = {"nn": (1, 0), "nt": (1, 1), "tn": (0, 0)}[mode]
    has_add = add is not None
    single = jr * nk == 1
    if mode == "tn":
        assert a_start % tm == 0
        a_block, a_idx = (tk, tm), (lambda i, kk: (kk, i + a_start // tm))
    else:
        assert a_start % tk == 0
        a_block, a_idx = (tm, tk), (lambda i, kk: (i, kk + a_start // tk))
    assert b_start % tn == 0
    b_block, b_idx = (((tn, tk), (lambda nn_, kk: (nn_, kk))) if mode == "nt"
                      else ((tk, tn), (lambda nn_, kk: (kk, nn_ + b_start // tn))))

    def blk_specs(blocked, block, idx, of_a, t):
        def pos(o, i, nn_, kk):
            return idx(i, kk) if of_a else idx(nn_, kk)
        if blocked:
            return pl.BlockSpec((None,) + block,
                                lambda o, i, nn_, r, kk: ((o if out_blk else r * fuse + t),) + pos(o, i, nn_, kk))
        return pl.BlockSpec(block, lambda o, i, nn_, r, kk: pos(o, i, nn_, kk))

    a_specs = [blk_specs(a_blk, a_block, a_idx, True, t) for t in range(fuse)]
    b_specs = [blk_specs(b_blk, b_block, b_idx, False, t) for t in range(fuse)]
    o_spec = (pl.BlockSpec((None, tm, tn), lambda o, i, nn_, r, kk: (o, i, nn_)) if out_blk
              else pl.BlockSpec((tm, tn), lambda o, i, nn_, r, kk: (i, nn_)))

    groups = _row_slices(tm, LANE if mode == "tn" else 16)

    def body(*refs):
        a_refs, b_refs = refs[:fuse], refs[fuse:2 * fuse]
        add_ref = refs[2 * fuse] if has_add else None
        o_ref = refs[2 * fuse + 1] if has_add else refs[2 * fuse]

        def partial(rs):
            out = None
            for t in range(fuse):
                av = a_refs[t][:, rs] if mode == "tn" else a_refs[t][rs, :]
                d = _dot(_mx(av), _mx(b_refs[t][...]), ca, cb)
                out = d if out is None else out + d
            return out

        if single:
            for rs in groups:
                res = partial(rs)
                if has_add:
                    res = res + add_ref[rs, :]
                o_ref[rs, :] = res.astype(o_ref.dtype)
            return
        acc = refs[-1]
        r, kk = pl.program_id(3), pl.program_id(4)

        @pl.when(jnp.logical_and(r == 0, kk == 0))
        def _():
            acc[...] = jnp.zeros_like(acc)

        for rs in groups:
            acc[rs, :] += partial(rs)

        @pl.when(jnp.logical_and(r == jr - 1, kk == nk - 1))
        def _():
            res = acc[...]
            if has_add:
                res = res + add_ref[...]
            o_ref[...] = res.astype(o_ref.dtype)

    out_shape = ((N_DEV, m, n) if out_blk else (m, n))
    return pl.pallas_call(
        body, name=name, grid=(jo, m // tm, n // tn, jr, nk),
        in_specs=a_specs + b_specs + ([o_spec] if has_add else []), out_specs=o_spec,
        out_shape=jax.ShapeDtypeStruct(out_shape, out_dtype),
        scratch_shapes=[] if single else [pltpu.VMEM((tm, tn), F32)],
        compiler_params=_params("parallel", "parallel", "parallel", "arbitrary", "arbitrary"),
    )(*((a,) * fuse + (b,) * fuse + ((add,) if has_add else ())))


def _mm_sum(a_list, b_list, name="mm_sum"):
    m, n = a_list[0].shape[0], b_list[0].shape[1]
    ns = len(a_list)
    tm = _pick(m, (1024, 512, 256, 128))
    tn = _pick(n, (1024, 512, 256, 128))
    groups = _row_slices(tm, 16)

    def body(*refs):
        a_refs, b_refs, o_ref = refs[:ns], refs[ns:2 * ns], refs[2 * ns]
        for rs in groups:
            acc = _dot(_mx(a_refs[0][rs, :]), _mx(b_refs[0][...]), 1, 0)
            for s in range(1, ns):
                acc = acc + _dot(_mx(a_refs[s][rs, :]), _mx(b_refs[s][...]), 1, 0)
            o_ref[rs, :] = acc

    return pl.pallas_call(
        body, name=name, grid=(m // tm, n // tn),
        in_specs=([pl.BlockSpec((tm, a.shape[1]), lambda i, j: (i, 0)) for a in a_list]
                  + [pl.BlockSpec((b.shape[0], tn), lambda i, j: (0, j)) for b in b_list]),
        out_specs=pl.BlockSpec((tm, tn), lambda i, j: (i, j)),
        out_shape=jax.ShapeDtypeStruct((m, n), F32), compiler_params=_params("parallel", "parallel"),
    )(*a_list, *b_list)


def _row_tile(r_, streams=4):
    return _pick(r_, ((512,) if streams <= 4 else ()) + (256, 128, 64, 32, 16, 8))


def _rms_fwd(t, w, groups=1, res=None, out_dtype=F32, name="rms_fwd"):
    r_, f = t.shape
    fg = f // groups
    tr = _row_tile(r_)
    has_res = res is not None

    def body(*refs):
        t_ref, w_ref = refs[0], refs[1]
        res_ref = refs[2] if has_res else None
        o_ref = refs[-1]
        for g in range(groups):
            sl = slice(g * fg, (g + 1) * fg)
            tv = t_ref[:, sl].astype(F32)
            r = lax.rsqrt(jnp.mean(tv * tv, axis=-1, keepdims=True) + EPS)
            y = tv * r * w_ref[:, sl]
            if has_res:
                y = y + res_ref[:, sl]
            o_ref[:, sl] = y.astype(o_ref.dtype)

    row = pl.BlockSpec((tr, f), lambda i: (i, 0))
    wsp = pl.BlockSpec((1, f), lambda i: (0, 0))
    return pl.pallas_call(
        body, name=name, grid=(r_ // tr,),
        in_specs=[row, wsp] + ([row] if has_res else []), out_specs=row,
        out_shape=jax.ShapeDtypeStruct((r_, f), out_dtype),
        compiler_params=_params("parallel"),
    )(*((t, w.reshape(1, f)) + ((res,) if has_res else ())))


def _rms_bwd(t, w, dys, res=None, out_dtype=F32, name="rms_bwd"):
    r_, f = t.shape
    groups = len(dys)
    fg = f // groups
    tr = _row_tile(r_)
    has_res = res is not None

    def body(*refs):
        t_ref, w_ref = refs[0], refs[1]
        dy_refs = refs[2:2 + groups]
        res_ref = refs[2 + groups] if has_res else None
        dt_ref, dw_ref = refs[-2], refs[-1]

        @pl.when(pl.program_id(0) == 0)
        def _():
            dw_ref[...] = jnp.zeros_like(dw_ref)

        for g in range(groups):
            sl = slice(g * fg, (g + 1) * fg)
            tv = t_ref[:, sl].astype(F32)
            dyv = dy_refs[g][...].astype(F32)
            r = lax.rsqrt(jnp.mean(tv * tv, axis=-1, keepdims=True) + EPS)
            gw = dyv * w_ref[:, sl]
            c = jnp.mean(gw * tv, axis=-1, keepdims=True)
            dt = r * gw - tv * (r * r * r * c)
            if has_res:
                dt = dt + res_ref[:, sl]
            dt_ref[:, sl] = dt.astype(dt_ref.dtype)
            dw_ref[:, sl] += jnp.sum(dyv * tv * r, axis=0, keepdims=True)

    row = pl.BlockSpec((tr, f), lambda i: (i, 0))
    grow = pl.BlockSpec((tr, fg), lambda i: (i, 0))
    wsp = pl.BlockSpec((1, f), lambda i: (0, 0))
    return pl.pallas_call(
        body, name=name, grid=(r_ // tr,),
        in_specs=[row, wsp] + [grow] * groups + ([row] if has_res else []), out_specs=[row, wsp],
        out_shape=[jax.ShapeDtypeStruct((r_, f), out_dtype), jax.ShapeDtypeStruct((1, f), F32)],
        compiler_params=_params("arbitrary"),
    )(*((t, w.reshape(1, f)) + tuple(dys) + ((res,) if has_res else ())))


def _norm_res_norm(t, res, w1, w2, name="post_mix_pre_ffn_norm"):
    r_, f = t.shape
    tr = _row_tile(r_)

    def body(t_ref, res_ref, w1_ref, w2_ref, h_ref, v_ref):
        tv = t_ref[...]
        h = res_ref[...] + tv * lax.rsqrt(jnp.mean(tv * tv, axis=-1, keepdims=True) + EPS) * w1_ref[...]
        h_ref[...] = h
        v_ref[...] = (h * lax.rsqrt(jnp.mean(h * h, axis=-1, keepdims=True) + EPS) * w2_ref[...]).astype(v_ref.dtype)

    row = pl.BlockSpec((tr, f), lambda i: (i, 0))
    wsp = pl.BlockSpec((1, f), lambda i: (0, 0))
    return pl.pallas_call(
        body, name=name, grid=(r_ // tr,), in_specs=[row, row, wsp, wsp], out_specs=[row, row],
        out_shape=[jax.ShapeDtypeStruct((r_, f), F32), jax.ShapeDtypeStruct((r_, f), MXU_DTYPE)],
        compiler_params=_params("parallel"),
    )(t, res, w1.reshape(1, f), w2.reshape(1, f))


def _norm_res_norm_bwd(h, w2, dv, dres, t, w1, name="pre_ffn_post_mix_norm_bwd"):
    r_, f = h.shape
    tr = _row_tile(r_, streams=6)

    def body(h_ref, w2_ref, dv_ref, dres_ref, t_ref, w1_ref, dh_ref, dt_ref, dw2_ref, dw1_ref):
        @pl.when(pl.program_id(0) == 0)
        def _():
            dw2_ref[...] = jnp.zeros_like(dw2_ref)
            dw1_ref[...] = jnp.zeros_like(dw1_ref)

        def rms_bwd(tv, wv, dyv):
            r = lax.rsqrt(jnp.mean(tv * tv, axis=-1, keepdims=True) + EPS)
            gw = dyv * wv
            c = jnp.mean(gw * tv, axis=-1, keepdims=True)
            return r * gw - tv * (r * r * r * c), jnp.sum(dyv * tv * r, axis=0, keepdims=True)

        d1, g2 = rms_bwd(h_ref[...], w2_ref[...], dv_ref[...])
        dh = d1 + dres_ref[...]
        dh_ref[...] = dh
        dw2_ref[...] += g2
        d2, g1 = rms_bwd(t_ref[...], w1_ref[...], dh)
        dt_ref[...] = d2.astype(dt_ref.dtype)
        dw1_ref[...] += g1

    row = pl.BlockSpec((tr, f), lambda i: (i, 0))
    wsp = pl.BlockSpec((1, f), lambda i: (0, 0))
    return pl.pallas_call(
        body, name=name, grid=(r_ // tr,), in_specs=[row, wsp, row, row, row, wsp], out_specs=[row, row, wsp, wsp],
        out_shape=[jax.ShapeDtypeStruct((r_, f), F32), jax.ShapeDtypeStruct((r_, f), MXU_DTYPE),
                   jax.ShapeDtypeStruct((1, f), F32), jax.ShapeDtypeStruct((1, f), F32)],
        compiler_params=_params("arbitrary"),
    )(h, w2.reshape(1, f), dv, dres, t, w1.reshape(1, f))


def _hnorm_fwd(o, w, width, name="attn_out_norm"):
    h, s_, v = o.shape
    tr = _row_tile(s_)

    def body(o_ref, w_ref, y_ref):
        ss = jnp.sum(o_ref[0] * o_ref[0], axis=-1, keepdims=True)
        for i in range(1, h):
            ss = ss + jnp.sum(o_ref[i] * o_ref[i], axis=-1, keepdims=True)
        r = lax.rsqrt(ss * (1.0 / (h * v)) + EPS)
        for i in range(h):
            sl = slice(i * v, (i + 1) * v)
            y_ref[:, sl] = (o_ref[i] * r * w_ref[:, sl]).astype(y_ref.dtype)

    return pl.pallas_call(
        body, name=name, grid=(s_ // tr,),
        in_specs=[pl.BlockSpec((h, tr, v), lambda i: (0, i, 0)), pl.BlockSpec((1, h * v), lambda i: (0, 0))],
        out_specs=pl.BlockSpec((tr, h * v), lambda i: (i, 0)),
        out_shape=jax.ShapeDtypeStruct((s_, width), MXU_DTYPE), compiler_params=_params("parallel"),
    )(o, w)


def _hnorm_bwd(o, w, dy, name="attn_out_norm_bwd"):
    h, s_, v = o.shape
    tr = _row_tile(s_)

    def body(o_ref, w_ref, dy_ref, do_ref, delta_ref, dw_ref):
        @pl.when(pl.program_id(0) == 0)
        def _():
            dw_ref[...] = jnp.zeros_like(dw_ref)

        ss = jnp.zeros((tr, 1), F32)
        cc = jnp.zeros((tr, 1), F32)
        for i in range(h):
            sl = slice(i * v, (i + 1) * v)
            ov = o_ref[i]
            ss = ss + jnp.sum(ov * ov, axis=-1, keepdims=True)
            cc = cc + jnp.sum(dy_ref[:, sl] * w_ref[:, sl] * ov, axis=-1, keepdims=True)
        r = lax.rsqrt(ss * (1.0 / (h * v)) + EPS)
        c = cc * (1.0 / (h * v))
        for i in range(h):
            sl = slice(i * v, (i + 1) * v)
            ov = o_ref[i]
            dyv = dy_ref[:, sl]
            dov = r * dyv * w_ref[:, sl] - ov * (r * r * r * c)
            do_ref[i] = dov.astype(do_ref.dtype)
            delta_ref[i] = jnp.sum(dov * ov, axis=-1, keepdims=True)
            dw_ref[:, sl] += jnp.sum(dyv * ov * r, axis=0, keepdims=True)

    blk = pl.BlockSpec((h, tr, v), lambda i: (0, i, 0))
    wsp = pl.BlockSpec((1, h * v), lambda i: (0, 0))
    return pl.pallas_call(
        body, name=name, grid=(s_ // tr,),
        in_specs=[blk, wsp, pl.BlockSpec((tr, h * v), lambda i: (i, 0))],
        out_specs=[blk, pl.BlockSpec((h, tr, 1), lambda i: (0, i, 0)), wsp],
        out_shape=[jax.ShapeDtypeStruct(o.shape, MXU_DTYPE), jax.ShapeDtypeStruct((h, s_, 1), F32),
                   jax.ShapeDtypeStruct((1, h * v), F32)],
        compiler_params=_params("arbitrary"),
    )(o, w, dy)


def _loss_head(ffn, h1, target, w, name="loss_head"):
    r_, f = ffn.shape
    tr = _row_tile(r_)

    def body(ffn_ref, h1_ref, tg_ref, w_ref, loss_ref, dy_ref, dffn_ref, dw_ref):
        @pl.when(pl.program_id(0) == 0)
        def _():
            dw_ref[...] = jnp.zeros_like(dw_ref)
            loss_ref[...] = jnp.zeros_like(loss_ref)

        tv = ffn_ref[...]
        wv = w_ref[...]
        r = lax.rsqrt(jnp.mean(tv * tv, axis=-1, keepdims=True) + EPS)
        tn = tv * r
        e = h1_ref[...] + tn * wv - tg_ref[...]
        tot = jnp.sum(jnp.sum(e * e, axis=1, keepdims=True), axis=0, keepdims=True) * (0.5 / f)
        loss_ref[...] += tot + jnp.zeros_like(loss_ref)
        dyv = e * (1.0 / f)
        dy_ref[...] = dyv
        gw = dyv * wv
        c = jnp.mean(gw * tv, axis=-1, keepdims=True)
        dffn_ref[...] = (r * gw - tv * (r * r * r * c)).astype(dffn_ref.dtype)
        dw_ref[...] += jnp.sum(dyv * tn, axis=0, keepdims=True)

    row = pl.BlockSpec((tr, f), lambda i: (i, 0))
    wsp = pl.BlockSpec((1, f), lambda i: (0, 0))
    lsp = pl.BlockSpec((1, LANE), lambda i: (0, 0))
    return pl.pallas_call(
        body, name=name, grid=(r_ // tr,),
        in_specs=[row, row, row, wsp], out_specs=[lsp, row, row, wsp],
        out_shape=[jax.ShapeDtypeStruct((1, LANE), F32), jax.ShapeDtypeStruct((r_, f), F32),
                   jax.ShapeDtypeStruct((r_, f), MXU_DTYPE), jax.ShapeDtypeStruct((1, f), F32)],
        compiler_params=_params("arbitrary"),
    )(ffn, h1, target, w.reshape(1, f))


def _rot_matrix():
    p = np.zeros((ROPE, ROPE), np.float32)
    for i in range(HALF):
        p[i + HALF, i] = -1.0
        p[i, i + HALF] = 1.0
    return jnp.asarray(p, BF16)


def _rope_val(r, c2, s2, rot):
    hi, mid, _ = _split3(r)
    return r * c2 + (_dot(hi, rot, 1, 0) + _dot(mid, rot, 1, 0)) * s2


def _q_prep(q, cos2, sin2, scale, name):
    h, s_, _ = q.shape
    tr = _pick(s_, (4096, 2048, 1024, 512, 256, 128, 64, 32, 16))

    def body(q_ref, c_ref, s_ref, rot_ref, o_ref):
        for rs in _row_slices(tr, 16):
            x = q_ref[rs, :]
            o_ref[rs, :NOPE] = (x[:, :NOPE] * scale).astype(o_ref.dtype)
            o_ref[rs, NOPE:] = (_rope_val(x[:, NOPE:], c_ref[rs, :], s_ref[rs, :], rot_ref[...]) * scale).astype(o_ref.dtype)

    blk = pl.BlockSpec((None, tr, QK), lambda hh, i: (hh, i, 0))
    csp = pl.BlockSpec((tr, ROPE), lambda hh, i: (i, 0))
    return pl.pallas_call(
        body, name=name, grid=(h, s_ // tr),
        in_specs=[blk, csp, csp, pl.BlockSpec((ROPE, ROPE), lambda hh, i: (0, 0))], out_specs=blk,
        out_shape=jax.ShapeDtypeStruct(q.shape, MXU_DTYPE), compiler_params=_params("parallel", "parallel"),
    )(q, cos2, sin2, _rot_matrix())


def _q_up(qkvn, w_uq_t, cos2, sin2, scale, name="q_up"):
    s_ = qkvn.shape[0]
    h = w_uq_t.shape[0]
    tm = _pick(s_, (4096, 2048, 1024, 512, 256, 128))

    def body(a_ref, w_ref, c_ref, s_ref, rot_ref, o_ref):
        for rs in _row_slices(tm, 16):
            x = _dot(_mx(a_ref[rs, :]), _mx(w_ref[...]), 1, 1)
            o_ref[rs, :NOPE] = (x[:, :NOPE] * scale).astype(o_ref.dtype)
            o_ref[rs, NOPE:] = (_rope_val(x[:, NOPE:], c_ref[rs, :], s_ref[rs, :], rot_ref[...]) * scale).astype(o_ref.dtype)

    csp = pl.BlockSpec((tm, ROPE), lambda j, i: (i, 0))
    return pl.pallas_call(
        body, name=name, grid=(h, s_ // tm),
        in_specs=[pl.BlockSpec((tm, Q_RANK), lambda j, i: (i, 0)), pl.BlockSpec((None, QK, Q_RANK), lambda j, i: (j, 0, 0)),
                  csp, csp, pl.BlockSpec((ROPE, ROPE), lambda j, i: (0, 0))],
        out_specs=pl.BlockSpec((None, tm, QK), lambda j, i: (j, i, 0)),
        out_shape=jax.ShapeDtypeStruct((h, s_, QK), MXU_DTYPE), compiler_params=_params("parallel", "parallel"),
    )(qkvn, w_uq_t, cos2, sin2, _rot_matrix())


def _kv_up(qkvn, w_ukv, small, cos2, sin2, name="kv_up"):
    s_ = qkvn.shape[0]
    h = w_ukv.shape[0]
    tm = _pick(s_, (4096, 2048, 1024, 512, 256, 128))

    def body(a_ref, w_ref, sm_ref, c_ref, s_ref, rot_ref, k_ref, v_ref):
        for rs in _row_slices(tm, 16):
            x = _dot(_mx(a_ref[rs, :]), _mx(w_ref[...]), 1, 0)
            k_ref[rs, :NOPE] = x[:, :NOPE].astype(k_ref.dtype)
            k_ref[rs, NOPE:] = _rope_val(sm_ref[rs, :ROPE], c_ref[rs, :], s_ref[rs, :], rot_ref[...]).astype(k_ref.dtype)
            v_ref[rs, :] = x[:, NOPE:].astype(v_ref.dtype)

    csp = pl.BlockSpec((tm, ROPE), lambda j, i: (i, 0))
    return pl.pallas_call(
        body, name=name, grid=(h, s_ // tm),
        in_specs=[pl.BlockSpec((tm, KV_RANK), lambda j, i: (i, Q_RANK // KV_RANK)),
                  pl.BlockSpec((None, KV_RANK, NOPE + VDIM), lambda j, i: (j, 0, 0)),
                  pl.BlockSpec((tm, LANE), lambda j, i: (i, 0)), csp, csp, pl.BlockSpec((ROPE, ROPE), lambda j, i: (0, 0))],
        out_specs=[pl.BlockSpec((None, tm, QK), lambda j, i: (j, i, 0)), pl.BlockSpec((None, tm, VDIM), lambda j, i: (j, i, 0))],
        out_shape=[jax.ShapeDtypeStruct((h, s_, QK), MXU_DTYPE), jax.ShapeDtypeStruct((h, s_, VDIM), MXU_DTYPE)],
        compiler_params=_params("parallel", "parallel"),
    )(qkvn, w_ukv, small, cos2, sin2, _rot_matrix())


def _dkv_post(dk, dv, ddt, cos2, nsin2, name="dkv_post"):
    h, s_, _ = dk.shape
    tr = _row_tile(s_)

    def body(dk_ref, dv_ref, ddt_ref, c_ref, s_ref, rot_ref, dkv_ref, dsm_ref):
        acc = dk_ref[0, :, NOPE:]
        for i in range(1, h):
            acc = acc + dk_ref[i, :, NOPE:]
        dsm_ref[:, :ROPE] = _rope_val(acc, c_ref[...], s_ref[...], rot_ref[...]).astype(dsm_ref.dtype)
        dsm_ref[:, ROPE:] = ddt_ref[:, ROPE:].astype(dsm_ref.dtype)
        for i in range(h):
            dkv_ref[i, :, :NOPE] = dk_ref[i, :, :NOPE].astype(dkv_ref.dtype)
            dkv_ref[i, :, NOPE:] = dv_ref[i].astype(dkv_ref.dtype)

    csp = pl.BlockSpec((tr, ROPE), lambda i: (i, 0))
    return pl.pallas_call(
        body, name=name, grid=(s_ // tr,),
        in_specs=[pl.BlockSpec((h, tr, QK), lambda i: (0, i, 0)), pl.BlockSpec((h, tr, VDIM), lambda i: (0, i, 0)),
                  pl.BlockSpec((tr, LANE), lambda i: (i, 0)), csp, csp, pl.BlockSpec((ROPE, ROPE), lambda i: (0, 0))],
        out_specs=[pl.BlockSpec((h, tr, NOPE + VDIM), lambda i: (0, i, 0)), pl.BlockSpec((tr, LANE), lambda i: (i, 0))],
        out_shape=[jax.ShapeDtypeStruct((h, s_, NOPE + VDIM), MXU_DTYPE), jax.ShapeDtypeStruct((s_, LANE), MXU_DTYPE)],
        compiler_params=_params("parallel"),
    )(dk, dv, ddt, cos2, nsin2, _rot_matrix())


def _attn_tile(s):
    return 2048 if s % 4096 == 0 else s // 2


def _pairs(n, by_key):
    if by_key:
        pr = [(i, j) for j in range(n) for i in range(j, n)]
    else:
        pr = [(i, j) for i in range(n) for j in range(i + 1)]
    return (jnp.asarray([p[0] for p in pr], jnp.int32), jnp.asarray([p[1] for p in pr], jnp.int32))


ATTN_ROW_GROUPS = 8


def _row_groups(t, diag):
    tg = t // ATTN_ROW_GROUPS
    out = []
    for r in range(ATTN_ROW_GROUPS):
        nc = (r + 1) * tg if diag else t
        mask = None
        if diag:
            mask = (lax.broadcasted_iota(jnp.int32, (tg, nc), 1)
                    <= lax.broadcasted_iota(jnp.int32, (tg, nc), 0) + r * tg)
        out.append((slice(r * tg, (r + 1) * tg), nc, mask))
    return out


def _flash_specs(t, dk, dv):
    qsp = pl.BlockSpec((None, t, dk), lambda hh, p, qi, kj: (hh, qi[p], 0))
    ksp = pl.BlockSpec((None, t, dk), lambda hh, p, qi, kj: (hh, kj[p], 0))
    vsp = pl.BlockSpec((None, t, dv), lambda hh, p, qi, kj: (hh, kj[p], 0))
    osp = pl.BlockSpec((None, t, dv), lambda hh, p, qi, kj: (hh, qi[p], 0))
    lsp = pl.BlockSpec((None, t, 1), lambda hh, p, qi, kj: (hh, qi[p], 0))
    return qsp, ksp, vsp, osp, lsp


def _flash_fwd(q, k, v, name="flash_fwd"):
    h, s_, dk = q.shape
    dv = v.shape[-1]
    t = _attn_tile(s_)
    n = s_ // t
    qi, kj = _pairs(n, False)

    def body(qi_ref, kj_ref, q_ref, k_ref, v_ref, o_ref, lse_ref, m_s, l_s, acc):
        p_ = pl.program_id(1)
        i, j = qi_ref[p_], kj_ref[p_]

        @pl.when(j == 0)
        def _():
            m_s[...] = jnp.full_like(m_s, -jnp.inf)
            l_s[...] = jnp.zeros_like(l_s)
            acc[...] = jnp.zeros_like(acc)

        def update(diag):
            for rs, nc, mask in _row_groups(t, diag):
                sc = _dot(q_ref[rs, :], k_ref[0:nc, :], 1, 1)
                if mask is not None:
                    sc = jnp.where(mask, sc, -jnp.inf)
                m_old = m_s[rs, :]
                m_new = jnp.maximum(m_old, jnp.max(sc, axis=1, keepdims=True))
                alpha = jnp.exp(m_old - m_new)
                p = jnp.exp(sc - m_new)
                l_s[rs, :] = alpha * l_s[rs, :] + jnp.sum(p, axis=1, keepdims=True)
                acc[rs, :] = alpha * acc[rs, :] + _dot(_mx(p), v_ref[0:nc, :], 1, 0)
                m_s[rs, :] = m_new

        @pl.when(j < i)
        def _():
            update(False)

        @pl.when(j == i)
        def _():
            update(True)
            o_ref[...] = acc[...] / l_s[...]
            lse_ref[...] = m_s[...] + jnp.log(l_s[...])

    qsp, ksp, vsp, osp, lsp = _flash_specs(t, dk, dv)
    gs = pltpu.PrefetchScalarGridSpec(
        num_scalar_prefetch=2, grid=(h, qi.shape[0]), in_specs=[qsp, ksp, vsp], out_specs=[osp, lsp],
        scratch_shapes=[pltpu.VMEM((t, 1), F32), pltpu.VMEM((t, 1), F32), pltpu.VMEM((t, dv), F32)])
    return pl.pallas_call(
        body, name=name, grid_spec=gs,
        out_shape=[jax.ShapeDtypeStruct((h, s_, dv), F32), jax.ShapeDtypeStruct((h, s_, 1), F32)],
        compiler_params=_params("parallel", "arbitrary"),
    )(qi, kj, q, k, v)


def _flash_bwd(q, k, v, do, lse, delta, name="flash_bwd"):
    h, s_, dk = q.shape
    dv = v.shape[-1]
    t = _attn_tile(s_)
    tg = t // ATTN_ROW_GROUPS
    n = s_ // t
    qi, kj = _pairs(n, True)

    def body(qi_ref, kj_ref, q_ref, k_ref, v_ref, do_ref, lse_ref, delta_ref, dq_ref, dk_ref, dv_ref, dk_acc, dv_acc):
        p_ = pl.program_id(1)
        i, j = qi_ref[p_], kj_ref[p_]

        @pl.when(p_ == 0)
        def _():
            dq_ref[...] = jnp.zeros_like(dq_ref)

        def update(diag):
            for g, (rs, nc, mask) in enumerate(_row_groups(t, diag)):
                sc = _dot(q_ref[rs, :], k_ref[0:nc, :], 1, 1)
                if mask is not None:
                    sc = jnp.where(mask, sc, -jnp.inf)
                p = jnp.exp(sc - lse_ref[rs, :])
                dob = _mx(do_ref[rs, :])
                dv_acc[0:nc, :] += _dot(_mx(p), dob, 0, 0)
                dp = _dot(dob, v_ref[0:nc, :], 1, 1)
                dsb = _mx(p * (dp - delta_ref[rs, :]))
                dk_acc[0:nc, :] += _dot(dsb, q_ref[rs, :], 0, 0)
                rows = pl.ds(pl.multiple_of(i * t + g * tg, tg), tg)
                dq_ref[rows, :] += _dot(dsb, k_ref[0:nc, :], 1, 0)

        @pl.when(i == j)
        def _():
            dk_acc[...] = jnp.zeros_like(dk_acc)
            dv_acc[...] = jnp.zeros_like(dv_acc)
            update(True)

        @pl.when(i > j)
        def _():
            update(False)

        @pl.when(i == n - 1)
        def _():
            dk_ref[...] = dk_acc[...]
            dv_ref[...] = dv_acc[...]

    qsp, ksp, vsp, osp, lsp = _flash_specs(t, dk, dv)
    dqsp = pl.BlockSpec((None, s_, dk), lambda hh, p, qi, kj: (hh, 0, 0))
    gs = pltpu.PrefetchScalarGridSpec(
        num_scalar_prefetch=2, grid=(h, qi.shape[0]), in_specs=[qsp, ksp, vsp, osp, lsp, lsp],
        out_specs=[dqsp, ksp, vsp],
        scratch_shapes=[pltpu.VMEM((t, dk), F32), pltpu.VMEM((t, dv), F32)])
    return pl.pallas_call(
        body, name=name, grid_spec=gs,
        out_shape=[jax.ShapeDtypeStruct((h, s_, dk), F32), jax.ShapeDtypeStruct((h, s_, dk), F32),
                   jax.ShapeDtypeStruct((h, s_, dv), F32)],
        compiler_params=_params("parallel", "arbitrary"),
    )(qi, kj, q, k, v, do, lse, delta)


HALO = 8


def _conv_specs(s_, c, tr, after):
    main = pl.BlockSpec((tr, c), lambda i: (i, 0))
    per = tr // HALO
    if after:
        halo = pl.BlockSpec((HALO, c), lambda i: (jnp.minimum((i + 1) * per, s_ // HALO - 1), 0))
    else:
        halo = pl.BlockSpec((HALO, c), lambda i: (jnp.maximum(i * per - 1, 0), 0))
    return main, halo


def _fill_before(ext, t_ref, h_ref, tr):
    ext[0:HALO, :] = jnp.where(pl.program_id(0) > 0, h_ref[...], 0.0)
    ext[HALO:HALO + tr, :] = t_ref[...]


def _taps(ext, w_ref, tr):
    base = HALO - (CONV_K - 1)
    acc = ext[base:base + tr, :] * w_ref[0:1, :]
    for k in range(1, CONV_K):
        acc = acc + ext[base + k:base + k + tr, :] * w_ref[k:k + 1, :]
    return acc


def _conv_fwd(t, w, b, name="conv_fwd"):
    s_, c = t.shape
    tr = _row_tile(s_)

    def body(t_ref, h_ref, w_ref, b_ref, o_ref, ext):
        _fill_before(ext, t_ref, h_ref, tr)
        o_ref[...] = _silu(_taps(ext, w_ref, tr) + b_ref[...])

    main, halo = _conv_specs(s_, c, tr, False)
    return pl.pallas_call(
        body, name=name, grid=(s_ // tr,),
        in_specs=[main, halo, pl.BlockSpec((CONV_K, c), lambda i: (0, 0)), pl.BlockSpec((1, c), lambda i: (0, 0))],
        out_specs=main, out_shape=jax.ShapeDtypeStruct((s_, c), F32),
        scratch_shapes=[pltpu.VMEM((tr + HALO, c), F32)], compiler_params=_params("parallel"),
    )(t, t, w, b)


def _conv_bwd_pre(t, w, b, dact, name="conv_bwd_pre"):
    s_, c = t.shape
    tr = _row_tile(s_)

    def body(t_ref, h_ref, w_ref, b_ref, da_ref, dpre_ref, dwb_ref, ext):
        @pl.when(pl.program_id(0) == 0)
        def _():
            dwb_ref[...] = jnp.zeros_like(dwb_ref)

        _fill_before(ext, t_ref, h_ref, tr)
        dpre = da_ref[...] * _dsilu(_taps(ext, w_ref, tr) + b_ref[...])
        dpre_ref[...] = dpre
        base = HALO - (CONV_K - 1)
        for k in range(CONV_K):
            dwb_ref[k:k + 1, :] += jnp.sum(dpre * ext[base + k:base + k + tr, :], axis=0, keepdims=True)
        dwb_ref[CONV_K:CONV_K + 1, :] += jnp.sum(dpre, axis=0, keepdims=True)

    main, halo = _conv_specs(s_, c, tr, False)
    return pl.pallas_call(
        body, name=name, grid=(s_ // tr,),
        in_specs=[main, halo, pl.BlockSpec((CONV_K, c), lambda i: (0, 0)), pl.BlockSpec((1, c), lambda i: (0, 0)), main],
        out_specs=[main, pl.BlockSpec((8, c), lambda i: (0, 0))],
        out_shape=[jax.ShapeDtypeStruct((s_, c), F32), jax.ShapeDtypeStruct((8, c), F32)],
        scratch_shapes=[pltpu.VMEM((tr + HALO, c), F32)], compiler_params=_params("arbitrary"),
    )(t, t, w, b, dact)


def _conv_bwd_in(dpre, w, name="conv_bwd_in"):
    s_, c = dpre.shape
    tr = _row_tile(s_)
    nt = s_ // tr

    def body(d_ref, h_ref, w_ref, o_ref, ext):
        ext[0:tr, :] = d_ref[...]
        ext[tr:tr + HALO, :] = jnp.where(pl.program_id(0) < nt - 1, h_ref[...], 0.0)
        acc = ext[CONV_K - 1:CONV_K - 1 + tr, :] * w_ref[0:1, :]
        for k in range(1, CONV_K):
            acc = acc + ext[CONV_K - 1 - k:CONV_K - 1 - k + tr, :] * w_ref[k:k + 1, :]
        o_ref[...] = acc.astype(o_ref.dtype)

    main, halo = _conv_specs(s_, c, tr, True)
    return pl.pallas_call(
        body, name=name, grid=(nt,),
        in_specs=[main, halo, pl.BlockSpec((CONV_K, c), lambda i: (0, 0))],
        out_specs=main, out_shape=jax.ShapeDtypeStruct((s_, c), MXU_DTYPE),
        scratch_shapes=[pltpu.VMEM((tr + HALO, c), F32)], compiler_params=_params("parallel"),
    )(dpre, dpre, w)


def _ssd_chunk_common(dt_ref, dtt_ref, br_ref, bc_ref, ar_ref, ac_ref):
    li = lax.broadcasted_iota(jnp.int32, (CHUNK, CHUNK), 0)
    si = lax.broadcasted_iota(jnp.int32, (CHUNK, CHUNK), 1)
    lower = li >= si
    lower_b = lower.astype(BF16)
    upper_b = (li <= si).astype(BF16)
    zr = dt_ref[...] + br_ref[...]
    dtc = _softplus(zr)
    a_row = -jnp.exp(ar_ref[...])
    acum = _exact_dot(lower_b, dtc * a_row, 1, 0, False)
    dtt = _softplus(dtt_ref[...] + bc_ref[...])
    acum_t = _exact_dot(dtt * (-jnp.exp(ac_ref[...])), upper_b, 1, 0, True)
    return lower, upper_b, zr, dtc, a_row, acum, acum_t


def _head_terms(h, lower, dtc, acum, acum_t):
    lane = lax.broadcasted_iota(jnp.int32, (1, LANE), 1)
    sub = lax.broadcasted_iota(jnp.int32, (SSD_H, 1), 0)
    rowid = lax.broadcasted_iota(jnp.int32, (CHUNK, 1), 0)
    oh = (lane == HEAD_LANE + h).astype(F32)
    acol = jnp.sum(acum * oh, axis=1, keepdims=True)
    dcol = jnp.sum(dtc * oh, axis=1, keepdims=True)
    arow = jnp.sum(acum_t * (sub == h).astype(F32), axis=0, keepdims=True)
    alast = jnp.sum(jnp.where(rowid == CHUNK - 1, acol, 0.0), axis=0, keepdims=True)
    decay = jnp.exp(jnp.where(lower, acol - arow, -jnp.inf))
    return oh, acol, dcol, alast, decay


SSD_PAIRS = SSD_H // 2
PAIRS_PER_GROUP = SSD_E // 2


def _ps(q):
    return slice(q * LANE, (q + 1) * LANE)


def _gs(off, g):
    return slice(off + g * SSD_N, off + (g + 1) * SSD_N)


def _lanes(c0, c1):
    return jnp.where(lax.broadcasted_iota(jnp.int32, (1, LANE), 1) < SSD_P, c0, c1)


def _rows(c0, c1):
    return jnp.where(lax.broadcasted_iota(jnp.int32, (LANE, 1), 0) < SSD_P, c0, c1)


def _lane_halves(t):
    first = lax.broadcasted_iota(jnp.int32, (1, LANE), 1) < SSD_P
    return (jnp.sum(jnp.where(first, t, 0.0), axis=1, keepdims=True),
            jnp.sum(jnp.where(first, 0.0, t), axis=1, keepdims=True))


def _ssd_in_specs(rev):
    def ci(c):
        return c if rev is None else rev - c
    return [pl.BlockSpec((CHUNK, CONV_DIM), lambda c: (ci(c), 0)),
            pl.BlockSpec((CHUNK, LANE), lambda c: (ci(c), 0)),
            pl.BlockSpec((SSD_H, CHUNK), lambda c: (0, ci(c))),
            pl.BlockSpec((1, LANE), lambda c: (0, 0)), pl.BlockSpec((SSD_H, 1), lambda c: (0, 0)),
            pl.BlockSpec((1, LANE), lambda c: (0, 0)), pl.BlockSpec((SSD_H, 1), lambda c: (0, 0)),
            pl.BlockSpec((SSD_PAIRS, 1, LANE), lambda c: (0, 0, 0))]


def _ssd_fwd(xbc, small, dtt, bias_r, bias_c, alog_r, alog_c, dsk, name="ssd_fwd"):
    s_ = xbc.shape[0]
    nc = s_ // CHUNK

    def body(x_ref, dt_ref, dtt_ref, br_ref, bc_ref, ar_ref, ac_ref, dsk_ref, y_ref, prev_ref, state):
        @pl.when(pl.program_id(0) == 0)
        def _():
            state[...] = jnp.zeros_like(state)

        lower, _, _, dtc, _, acum, acum_t = _ssd_chunk_common(dt_ref, dtt_ref, br_ref, bc_ref, ar_ref, ac_ref)
        for g in range(SSD_G):
            bb = _mx(x_ref[:, _gs(B_OFF, g)])
            cb_ = _mx(x_ref[:, _gs(C_OFF, g)])
            cbm = _dot(cb_, bb, 1, 1)
            for e in range(PAIRS_PER_GROUP):
                q = g * PAIRS_PER_GROUP + e
                _, acol0, dcol0, alast0, decay0 = _head_terms(2 * q, lower, dtc, acum, acum_t)
                _, acol1, dcol1, alast1, decay1 = _head_terms(2 * q + 1, lower, dtc, acum, acum_t)
                x = x_ref[:, _ps(q)]
                xdt = x * _lanes(dcol0, dcol1)
                xb = _mx(xdt)
                yd = _lanes(_dot(_mx(cbm * decay0), xb, 1, 0), _dot(_mx(cbm * decay1), xb, 1, 0))
                prev = state[q]
                prev_ref[0, q] = prev
                yo = _dot(cb_, _mx(prev), 1, 1) * _lanes(jnp.exp(acol0), jnp.exp(acol1))
                ds = _lanes(jnp.exp(alast0 - acol0), jnp.exp(alast1 - acol1))
                st = _dot(_mx(xdt * ds), bb, 0, 0)
                state[q] = prev * _rows(jnp.exp(alast0), jnp.exp(alast1)) + st
                y_ref[:, _ps(q)] = yd + yo + x * dsk_ref[q]

    psp = pl.BlockSpec((1, SSD_PAIRS, LANE, SSD_N), lambda c: (c, 0, 0, 0))
    return pl.pallas_call(
        body, name=name, grid=(nc,),
        in_specs=_ssd_in_specs(None), out_specs=[pl.BlockSpec((CHUNK, SSD_W), lambda c: (c, 0)), psp],
        out_shape=[jax.ShapeDtypeStruct((s_, SSD_W), F32),
                   jax.ShapeDtypeStruct((nc, SSD_PAIRS, LANE, SSD_N), F32)],
        scratch_shapes=[pltpu.VMEM((SSD_PAIRS, LANE, SSD_N), F32)],
        compiler_params=_params("arbitrary"),
    )(xbc, small, dtt, bias_r, bias_c, alog_r, alog_c, dsk)


def _ssd_bwd(xbc, small, dtt, bias_r, bias_c, alog_r, alog_c, dsk, prev, dy, name="ssd_bwd"):
    s_ = xbc.shape[0]
    nc = s_ // CHUNK

    def body(x_ref, dt_ref, dtt_ref, br_ref, bc_ref, ar_ref, ac_ref, dsk_ref, prev_ref, dy_ref,
             dx_ref, ddt_ref, dpar_ref, dstate):
        @pl.when(pl.program_id(0) == 0)
        def _():
            dstate[...] = jnp.zeros_like(dstate)
            dpar_ref[...] = jnp.zeros_like(dpar_ref)

        lower, upper_b, zr, dtc, a_row, acum, acum_t = _ssd_chunk_common(
            dt_ref, dtt_ref, br_ref, bc_ref, ar_ref, ac_ref)
        strict = (lax.broadcasted_iota(jnp.int32, (CHUNK, CHUNK), 1)
                  < lax.broadcasted_iota(jnp.int32, (CHUNK, CHUNK), 0))
        strict_b = strict.astype(BF16)
        col2 = lax.broadcasted_iota(jnp.int32, (CHUNK, 2 * CHUNK), 1)
        strict2 = (jnp.where(col2 >= CHUNK, col2 - CHUNK, col2)
                   < lax.broadcasted_iota(jnp.int32, (CHUNK, 2 * CHUNK), 0))
        da_in = jnp.zeros((CHUNK, LANE), F32)
        r_off = jnp.zeros((CHUNK, LANE), F32)
        c_int = jnp.zeros((CHUNK, LANE), F32)
        c_row = jnp.zeros((1, LANE), F32)
        ddt = jnp.zeros((CHUNK, LANE), F32)
        dskip = jnp.zeros((1, LANE), F32)
        for g in range(SSD_G):
            bb = _mx(x_ref[:, _gs(B_OFF, g)])
            cb_ = _mx(x_ref[:, _gs(C_OFF, g)])
            cbm = _dot(cb_, bb, 1, 1)
            dcb = jnp.zeros((CHUNK, CHUNK), F32)
            dc_acc = jnp.zeros((CHUNK, SSD_N), F32)
            db_acc = jnp.zeros((CHUNK, SSD_N), F32)
            for e in range(PAIRS_PER_GROUP):
                q = g * PAIRS_PER_GROUP + e
                oh0, acol0, dcol0, alast0, decay0 = _head_terms(2 * q, lower, dtc, acum, acum_t)
                oh1, acol1, dcol1, alast1, decay1 = _head_terms(2 * q + 1, lower, dtc, acum, acum_t)
                x = x_ref[:, _ps(q)]
                dy = dy_ref[:, _ps(q)]
                dcol = _lanes(dcol0, dcol1)
                xdt = x * dcol
                xb = _mx(xdt)
                eacol = _lanes(jnp.exp(acol0), jnp.exp(acol1))
                ds = _lanes(jnp.exp(alast0 - acol0), jnp.exp(alast1 - acol1))
                ealast = _rows(jnp.exp(alast0), jnp.exp(alast1))
                dyb = _mx(dy)
                dyb0, dyb1 = _mx(_lanes(dy, 0.0)), _mx(_lanes(0.0, dy))
                dsh = dstate[q]
                dshb = _mx(dsh)
                prev = prev_ref[0, q]
                prevb = _mx(prev)
                dxdt_inter = ds * _dot(bb, dshb, 1, 1)
                dxdt = _lanes(_dot(_mx(cbm * decay0), dyb, 0, 0), _dot(_mx(cbm * decay1), dyb, 0, 0)) + dxdt_inter
                dwl0 = _dot(dyb0, xb, 1, 1) * decay0
                dwl1 = _dot(dyb1, xb, 1, 1) * decay1
                dcb = dcb + dwl0 + dwl1
                dyeb = _mx(dy * eacol)
                dc_acc = dc_acc + _dot(dyeb, prevb, 1, 0)
                db_acc = db_acc + _dot(_mx(xdt * ds), dshb, 1, 0)
                dstate[q] = _dot(dyeb, cb_, 0, 0) + ealast * dsh
                above = _exact_dot(upper_b, jnp.concatenate([dwl0 * cbm, dwl1 * cbm], axis=1), 1, 0, False)
                above = jnp.where(strict2, above, 0.0)
                da_in = (da_in + jnp.sum(above[:, :CHUNK], axis=1, keepdims=True) * oh0
                         + jnp.sum(above[:, CHUNK:], axis=1, keepdims=True) * oh1)
                y_off = _dot(cb_, prevb, 1, 1) * eacol
                r0, r1 = _lane_halves(dy * y_off)
                r_off = r_off + r0 * oh0 + r1 * oh1
                c0, c1 = _lane_halves(xdt * dxdt_inter)
                c_int = c_int + c0 * oh0 + c1 * oh1
                both = jnp.sum(dsh * prev, axis=1, keepdims=True) * ealast
                c_row = (c_row + jnp.sum(_rows(both, 0.0), axis=0, keepdims=True) * oh0
                         + jnp.sum(_rows(0.0, both), axis=0, keepdims=True) * oh1)
                t0, t1 = _lane_halves(dxdt * x)
                ddt = ddt + t0 * oh0 + t1 * oh1
                dx_ref[:, _ps(q)] = dxdt * dcol + dy * dsk_ref[q]
                k0, k1 = _lane_halves(dy * x)
                dskip = (dskip + jnp.sum(k0, axis=0, keepdims=True) * oh0 + jnp.sum(k1, axis=0, keepdims=True) * oh1)
            dcbb = _mx(dcb)
            dx_ref[:, _gs(C_OFF, g)] = dc_acc + _dot(dcbb, bb, 1, 0)
            dx_ref[:, _gs(B_OFF, g)] = db_acc + _dot(dcbb, cb_, 0, 0)
        da = (da_in + _exact_dot(upper_b, r_off, 1, 0, False) + _exact_dot(strict_b, c_int, 1, 0, False) + c_row)
        draw = (ddt + da * a_row) * _sigmoid(zr)
        ddt_ref[...] = draw
        dpar_ref[0:1, :] += jnp.sum(draw, axis=0, keepdims=True)
        dpar_ref[1:2, :] += jnp.sum(da * dtc, axis=0, keepdims=True) * a_row
        dpar_ref[2:3, :] += dskip

    rev = nc - 1
    psp = pl.BlockSpec((1, SSD_PAIRS, LANE, SSD_N), lambda c: (rev - c, 0, 0, 0))
    return pl.pallas_call(
        body, name=name, grid=(nc,),
        in_specs=_ssd_in_specs(rev) + [psp, pl.BlockSpec((CHUNK, SSD_W), lambda c: (rev - c, 0))],
        out_specs=[pl.BlockSpec((CHUNK, CONV_DIM), lambda c: (rev - c, 0)),
                   pl.BlockSpec((CHUNK, LANE), lambda c: (rev - c, 0)), pl.BlockSpec((8, LANE), lambda c: (0, 0))],
        out_shape=[jax.ShapeDtypeStruct((s_, CONV_DIM), F32), jax.ShapeDtypeStruct((s_, LANE), F32),
                   jax.ShapeDtypeStruct((8, LANE), F32)],
        scratch_shapes=[pltpu.VMEM((SSD_PAIRS, LANE, SSD_N), F32)],
        compiler_params=_params("arbitrary"),
    )(xbc, small, dtt, bias_r, bias_c, alog_r, alog_c, dsk, prev, dy)


GN = SSD_W // SSD_G


def _gated_norm_fwd(y, z, w, cat, name="gated_norm_fwd"):
    s_, f = y.shape
    tr = _row_tile(s_)

    def body(y_ref, z_ref, w_ref, cat_ref, o_ref):
        for g in range(SSD_G):
            sl = slice(g * GN, (g + 1) * GN)
            gg = y_ref[:, sl] * _silu(z_ref[:, sl])
            r = lax.rsqrt(jnp.mean(gg * gg, axis=-1, keepdims=True) + EPS)
            o_ref[:, sl] = (gg * r * w_ref[:, sl]).astype(o_ref.dtype)

    row = pl.BlockSpec((tr, f), lambda i: (i, 0))
    wsp = pl.BlockSpec((1, f), lambda i: (0, 0))
    return pl.pallas_call(
        body, name=name, grid=(s_ // tr,),
        in_specs=[row, row, wsp, pl.BlockSpec(memory_space=pl.ANY)], out_specs=pl.BlockSpec((tr, f), lambda i: (i, 1)),
        out_shape=jax.ShapeDtypeStruct(cat.shape, cat.dtype), input_output_aliases={3: 0},
        compiler_params=_params("parallel"),
    )(y, z, w.reshape(1, f), cat)


def _gated_norm_bwd(y, z, w, dout, name="gated_norm_bwd"):
    s_, f = y.shape
    tr = _row_tile(s_)

    def body(y_ref, z_ref, w_ref, do_ref, dy_ref, dz_ref, dw_ref):
        @pl.when(pl.program_id(0) == 0)
        def _():
            dw_ref[...] = jnp.zeros_like(dw_ref)

        for g in range(SSD_G):
            sl = slice(g * GN, (g + 1) * GN)
            yv = y_ref[:, sl]
            zv = z_ref[:, sl]
            dov = do_ref[:, sl].astype(F32)
            sz = _silu(zv)
            gg = yv * sz
            r = lax.rsqrt(jnp.mean(gg * gg, axis=-1, keepdims=True) + EPS)
            gw = dov * w_ref[:, sl]
            c = jnp.mean(gw * gg, axis=-1, keepdims=True)
            dgg = r * gw - gg * (r * r * r * c)
            dy_ref[:, sl] = dgg * sz
            dz_ref[:, sl] = (dgg * yv * _dsilu(zv)).astype(dz_ref.dtype)
            dw_ref[:, sl] += jnp.sum(dov * gg * r, axis=0, keepdims=True)

    row = pl.BlockSpec((tr, f), lambda i: (i, 0))
    wsp = pl.BlockSpec((1, f), lambda i: (0, 0))
    return pl.pallas_call(
        body, name=name, grid=(s_ // tr,),
        in_specs=[row, row, wsp, pl.BlockSpec((tr, f), lambda i: (i, 1))], out_specs=[row, row, wsp],
        out_shape=[jax.ShapeDtypeStruct((s_, f), F32), jax.ShapeDtypeStruct((s_, f), MXU_DTYPE),
                   jax.ShapeDtypeStruct((1, f), F32)],
        compiler_params=_params("arbitrary"),
    )(y, z, w.reshape(1, f), dout)


def _ffn_fwd(vv, w_gate, w_up, name="ffn_gate_up"):
    s_, d = vv.shape
    nb, f8, _ = w_gate.shape
    tm = _pick(s_, (1024, 512, 256, 128))

    def body(v_ref, wg_ref, wu_ref, g_ref, u_ref, a_ref):
        for rs in _row_slices(tm, 16):
            a = _mx(v_ref[rs, :])
            g = _dot(a, _mx(wg_ref[...]), 1, 1)
            u = _dot(a, _mx(wu_ref[...]), 1, 1)
            s = _sigmoid(g)
            gs = g * s
            g_ref[rs, :] = (u * (s * (1.0 + g * (1.0 - s)))).astype(g_ref.dtype)
            u_ref[rs, :] = gs.astype(u_ref.dtype)
            a_ref[rs, :] = (gs * u).astype(a_ref.dtype)

    wsp = pl.BlockSpec((None, f8, d), lambda j, i: (j, 0, 0))
    osp = pl.BlockSpec((None, tm, f8), lambda j, i: (j, i, 0))
    return pl.pallas_call(
        body, name=name, grid=(nb, s_ // tm),
        in_specs=[pl.BlockSpec((tm, d), lambda j, i: (i, 0)), wsp, wsp], out_specs=[osp] * 3,
        out_shape=[jax.ShapeDtypeStruct((nb, s_, f8), MXU_DTYPE)] * 3,
        compiler_params=_params("parallel", "parallel"),
    )(vv, w_gate, w_up)


def _ffn_bwd_act(dffn, w_down, gate, up, name="ffn_d_act"):
    s_, d = dffn.shape
    nb, f8, _ = w_down.shape
    tm = _pick(s_, (1024, 512, 256, 128))

    def body(d_ref, w_ref, g_ref, u_ref, dg_ref, du_ref):
        for rs in _row_slices(tm, 16):
            dact = _dot(_mx(d_ref[rs, :]), _mx(w_ref[...]), 1, 1)
            dg_ref[rs, :] = (dact * g_ref[rs, :].astype(F32)).astype(dg_ref.dtype)
            du_ref[rs, :] = (dact * u_ref[rs, :].astype(F32)).astype(du_ref.dtype)

    osp = pl.BlockSpec((None, tm, f8), lambda j, i: (j, i, 0))
    return pl.pallas_call(
        body, name=name, grid=(nb, s_ // tm),
        in_specs=[pl.BlockSpec((tm, d), lambda j, i: (i, 0)), pl.BlockSpec((None, f8, d), lambda j, i: (j, 0, 0)),
                  osp, osp],
        out_specs=[osp, osp], out_shape=[jax.ShapeDtypeStruct((nb, s_, f8), MXU_DTYPE)] * 2,
        compiler_params=_params("parallel", "parallel"),
    )(dffn, w_down, gate, up)


def _ffn_bwd_in(dgate, w_gate, dup, w_up, name="ffn_d_in"):
    nb, s_, f8 = dgate.shape
    d = w_gate.shape[2]
    tm = _pick(s_, (1024, 512, 256, 128))
    tn = _pick(d, (1024, 512, 256, 128))
    per = 2
    steps = nb // per

    def body(*refs):
        ins, o_ref, acc = refs[:4 * per], refs[4 * per], refs[4 * per + 1]
        j = pl.program_id(2)

        @pl.when(j == 0)
        def _():
            acc[...] = jnp.zeros_like(acc)

        for rs in _row_slices(tm, 16):
            part = None
            for t in range(per):
                dg_ref, wg_ref, du_ref, wu_ref = ins[4 * t:4 * t + 4]
                d_ = (_dot(_mx(dg_ref[rs, :]), _mx(wg_ref[...]), 1, 0)
                      + _dot(_mx(du_ref[rs, :]), _mx(wu_ref[...]), 1, 0))
                part = d_ if part is None else part + d_
            acc[rs, :] += part

        @pl.when(j == steps - 1)
        def _():
            o_ref[...] = acc[...]

    def specs(t):
        asp = pl.BlockSpec((None, tm, f8), lambda i, n, j: (j * per + t, i, 0))
        wsp = pl.BlockSpec((None, f8, tn), lambda i, n, j: (j * per + t, 0, n))
        return [asp, wsp, asp, wsp]

    return pl.pallas_call(
        body, name=name, grid=(s_ // tm, d // tn, steps),
        in_specs=[sp for t in range(per) for sp in specs(t)],
        out_specs=pl.BlockSpec((tm, tn), lambda i, n, j: (i, n)),
        out_shape=jax.ShapeDtypeStruct((s_, d), F32), scratch_shapes=[pltpu.VMEM((tm, tn), F32)],
        compiler_params=_params("parallel", "parallel", "arbitrary"),
    )(*((dgate, w_gate, dup, w_up) * per))


def _adam_math(g, w, m, v):
    m2 = ADAM_B1 * m + (1.0 - ADAM_B1) * g
    v2 = ADAM_B2 * v + (1.0 - ADAM_B2) * (g * g)
    m_hat = m2 / (1.0 - ADAM_B1 ** ADAM_STEP)
    v_hat = v2 / (1.0 - ADAM_B2 ** ADAM_STEP)
    delta = -ADAM_LR * (m_hat / (jnp.sqrt(v_hat) + ADAM_EPS) + ADAM_WD * w)
    return delta, m2, v2


def _adamw(parts, own, me, w, m, v, name="adamw"):
    nd, r_, c = parts.shape
    tr = _pick(r_, (128, 64, 32, 16))
    tc = c
    if tr == r_ and r_ > 128:
        tc = _pick(c, (256, 128))

    def body(me_ref, p_ref, own_ref, w_ref, m_ref, v_ref, g_ref, d_ref, m2_ref, v2_ref):
        mine = me_ref[0]
        g = jnp.zeros((tr, tc), F32)
        for i in range(nd):
            g = g + jnp.where(mine == i, own_ref[...], p_ref[i]).astype(F32)
        delta, m2, v2 = _adam_math(g, w_ref[...], m_ref[...], v_ref[...])
        g_ref[...] = g
        d_ref[...] = delta
        m2_ref[...] = m2
        v2_ref[...] = v2

    row = pl.BlockSpec((tr, tc), lambda i, j, me_: (i, j))
    gs = pltpu.PrefetchScalarGridSpec(
        num_scalar_prefetch=1, grid=(r_ // tr, c // tc),
        in_specs=[pl.BlockSpec((nd, tr, tc), lambda i, j, me_: (0, i, j)),
                  pl.BlockSpec((None, tr, tc), lambda i, j, me_: (me_[0], i, j)), row, row, row],
        out_specs=[row] * 4)
    return pl.pallas_call(
        body, name=name, grid_spec=gs, out_shape=[jax.ShapeDtypeStruct((r_, c), F32)] * 4,
        compiler_params=_params("parallel", "parallel"),
    )(me, parts, own, w, m, v)


def _adamw_small(parts, w, m, v, name="adamw_small"):
    nd = parts.shape[0]

    def body(p_ref, w_ref, m_ref, v_ref, g_ref, d_ref, m2_ref, v2_ref):
        g = p_ref[0]
        for i in range(1, nd):
            g = g + p_ref[i]
        delta, m2, v2 = _adam_math(g, w_ref[...], m_ref[...], v_ref[...])
        g_ref[...] = g
        d_ref[...] = delta
        m2_ref[...] = m2
        v2_ref[...] = v2

    return pl.pallas_call(
        body, name=name, out_shape=[jax.ShapeDtypeStruct(w.shape, F32)] * 4,
        compiler_params=pltpu.CompilerParams(vmem_limit_bytes=VMEM_LIMIT_BYTES),
    )(parts, w, m, v)


_HBM = pl.BlockSpec(memory_space=pltpu.HBM)
_MESH = pl.DeviceIdType.MESH


def _all_gather(xs, name):
    na = len(xs)

    def body(*refs):
        x_refs, out_refs = refs[:na], refs[na:2 * na]
        send_sems, recv_sems, local_sems = refs[2 * na:]
        x, y, c = lax.axis_index("x"), lax.axis_index("y"), lax.axis_index("c")
        me, sibling = (x, y, c), (x, y, 1 - c)
        near = [(1 - x, y), (x, 1 - y)]
        chips = near + [(1 - x, 1 - y)]
        relay_from = (x + c * (1 - 2 * x), y + (1 - c) * (1 - 2 * y))
        relay_to = (x + (1 - c) * (1 - 2 * x), y + c * (1 - 2 * y))

        def slot(a, px, py, pc):
            return out_refs[a].at[4 * px + 2 * py + pc]

        def copy(a, k, block, to, src=None):
            return pltpu.make_async_remote_copy(
                src_ref=slot(a, *block) if src is None else src, dst_ref=slot(a, *block),
                send_sem=send_sems.at[a, k], recv_sem=recv_sems.at[a, k], device_id=to, device_id_type=_MESH)

        mine = [pltpu.make_async_copy(x_refs[a], slot(a, *me), local_sems.at[a]) for a in range(na)]
        started = []
        for a in range(na):
            mine[a].start()
            first = [copy(a, 0, me, sibling, src=x_refs[a])]
            first += [copy(a, 1 + j, me, (*chip, c), src=x_refs[a]) for j, chip in enumerate(near)]
            for cp in first:
                cp.start()
            started += first
        for a in range(na):
            for j, chip in enumerate(chips):
                copy(a, 1 + j, (*chip, c), me).wait_recv()
                fwd = copy(a, 4 + j, (*chip, c), sibling)
                fwd.start()
                started.append(fwd)
                if j == len(near) - 1:
                    relay = copy(a, 1 + len(near), (*relay_from, c), (*relay_to, c))
                    relay.start()
                    started.append(relay)
        for a in range(na):
            copy(a, 0, sibling, me).wait_recv()
            for j, chip in enumerate(chips):
                copy(a, 4 + j, (*chip, 1 - c), me).wait_recv()
        for cp in started:
            cp.wait_send()
        for cp in mine:
            cp.wait()

    return pl.pallas_call(
        body, name=name, out_shape=[jax.ShapeDtypeStruct((N_DEV,) + t.shape, t.dtype) for t in xs],
        in_specs=[_HBM] * na, out_specs=[_HBM] * na,
        scratch_shapes=[pltpu.SemaphoreType.DMA((na, 7)), pltpu.SemaphoreType.DMA((na, 7)),
                        pltpu.SemaphoreType.DMA((na,))],
    )(*xs)


_SEM = pl.BlockSpec(memory_space=pltpu.SEMAPHORE)
_EFFECT = pltpu.SideEffectType.DATAFLOW_SIDE_EFFECTING


def _peers(x, y, c):
    out = []
    for k in range(1, N_DEV):
        px = 1 - x if k & 4 else x
        py = 1 - y if k & 2 else y
        pc = 1 - c if k & 1 else c
        out.append(((px, py, pc), 4 * px + 2 * py + pc))
    return out


def _push_copies(scatter, src_refs, land_refs, send_sems, recv_sems):
    x, y, c = lax.axis_index("x"), lax.axis_index("y"), lax.axis_index("c")
    me = 4 * x + 2 * y + c
    pairs = []
    for a, (src, land) in enumerate(zip(src_refs, land_refs)):
        for k, (peer, slot) in enumerate(_peers(x, y, c)):
            out_src = src.at[slot] if scatter else src
            si = a * (N_DEV - 1) + k
            send = pltpu.make_async_remote_copy(src_ref=out_src, dst_ref=land.at[me], send_sem=send_sems.at[si],
                                                recv_sem=recv_sems.at[si], device_id=peer, device_id_type=_MESH)
            recv = pltpu.make_async_remote_copy(src_ref=out_src, dst_ref=land.at[slot], send_sem=send_sems.at[si],
                                                recv_sem=recv_sems.at[si], device_id=peer, device_id_type=_MESH)
            pairs.append((send, recv))
    return pairs


def _push_start(srcs, scatter, dep, name):
    na = len(srcs)
    shapes = [t.shape[1:] if scatter else t.shape for t in srcs]
    lands = [pltpu.with_memory_space_constraint(lax.empty((N_DEV,) + s, t.dtype), pltpu.HBM) for s, t in zip(shapes, srcs)]

    def body(*refs):
        src_refs, land_refs = refs[:na], refs[na:2 * na]
        send_sems, recv_sems = refs[2 * na + 1], refs[2 * na + 2]
        token = refs[-1]
        for send, _ in _push_copies(scatter, src_refs, land_refs, send_sems, recv_sems):
            send.start()
        token[...] = jnp.zeros_like(token)

    sem = pltpu.SemaphoreType.DMA((na * (N_DEV - 1),))
    outs = pl.pallas_call(
        body, name=name,
        out_shape=(sem, sem) + tuple(pltpu.HBM(t.shape, t.dtype) for t in srcs)
        + tuple(pltpu.HBM(t.shape, t.dtype) for t in lands) + (jax.ShapeDtypeStruct((8, LANE), F32),),
        in_specs=[_HBM] * (2 * na) + [pl.BlockSpec(memory_space=pl.ANY)],
        out_specs=(_SEM, _SEM) + (_HBM,) * (2 * na) + (pl.BlockSpec(memory_space=pltpu.VMEM),),
        input_output_aliases={i: 2 + i for i in range(2 * na)},
        compiler_params=pltpu.CompilerParams(has_side_effects=_EFFECT),
    )(*[pltpu.with_memory_space_constraint(t, pltpu.HBM) for t in srcs], *lands, dep)
    return outs[0], outs[1], outs[2:2 + na], outs[2 + na:2 + 2 * na], outs[-1]


def _push_wait(send_sems, recv_sems, src_thru, land_thru, scatter, after, name):
    na = len(src_thru)

    def body(*refs):
        src_refs, land_refs = refs[:na], refs[na:2 * na]
        ssem, rsem = refs[2 * na], refs[2 * na + 1]
        for send, recv in _push_copies(scatter, src_refs, land_refs, ssem, rsem):
            send.wait_send()
            recv.wait_recv()

    outs = pl.pallas_call(
        body, name=name,
        out_shape=tuple(pltpu.HBM(t.shape, t.dtype) for t in src_thru) + tuple(pltpu.HBM(t.shape, t.dtype) for t in land_thru),
        in_specs=[_HBM] * (2 * na) + [_SEM, _SEM, pl.BlockSpec(memory_space=pl.ANY)],
        out_specs=(_HBM,) * (2 * na),
        input_output_aliases={i: i for i in range(2 * na)},
        compiler_params=pltpu.CompilerParams(has_side_effects=_EFFECT),
    )(*src_thru, *land_thru, send_sems, recv_sems, after)
    return outs[:na], outs[na:]


def _exchange_behind(srcs, scatter, dep, name):
    send_sems, recv_sems, thru, lands, token = _push_start(srcs, scatter, dep, name + "_start")

    def finish(after, place=True):
        src_done, land_done = _push_wait(send_sems, recv_sems, thru, lands, scatter, after, name + "_wait")
        if not place:
            return land_done, src_done
        return _place_own(land_done, src_done, scatter, name + "_own")

    return token[0, 0], finish


def _place_own(lands, srcs, scatter, name):
    me = (4 * lax.axis_index("x") + 2 * lax.axis_index("y") + lax.axis_index("c")).astype(jnp.int32).reshape(1)
    outs = []
    for a, (land, src) in enumerate(zip(lands, srcs)):
        r_, c_ = land.shape[1:]
        tr = _pick(r_, (512, 256, 128, 64, 32, 16))

        def body(me_ref, land_ref, src_ref, out_ref):
            out_ref[...] = src_ref[...]

        src_spec = (pl.BlockSpec((None, tr, c_), lambda i, me_: (me_[0], i, 0)) if scatter
                    else pl.BlockSpec((tr, c_), lambda i, me_: (i, 0)))
        gs = pltpu.PrefetchScalarGridSpec(
            num_scalar_prefetch=1, grid=(r_ // tr,),
            in_specs=[pl.BlockSpec(memory_space=pl.ANY), src_spec],
            out_specs=pl.BlockSpec((None, tr, c_), lambda i, me_: (me_[0], i, 0)))
        outs.append(pl.pallas_call(
            body, name=f"{name}_{a}", grid_spec=gs, out_shape=jax.ShapeDtypeStruct(land.shape, land.dtype),
            input_output_aliases={1: 0}, compiler_params=_params("arbitrary"),
        )(me, land, src))
    return outs


_TRANSPOSED = ("w_in", "w_uq", "w_gate", "w_up")
_CQKV = (0, Q_RANK + KV_RANK)
_KR = (_CQKV[1], _CQKV[1] + ROPE)
_Z = (_KR[1], _KR[1] + SSD_W)
_XBC = (_Z[1], _Z[1] + CONV_DIM)
_DT = (_XBC[1], _XBC[1] + SSD_H)


def _win_segments(w_in_t):
    w = w_in_t.reshape(D_IN, D_MODEL)
    small = jnp.concatenate([w[_KR[0]:_KR[1]], w[_DT[0]:_DT[1]],
                             jnp.zeros((LANE - ROPE - SSD_H, D_MODEL), w.dtype)], axis=0)
    return w[_CQKV[0]:_CQKV[1]], w[_Z[0]:_Z[1]], w[_XBC[0]:_XBC[1]], small


def _win_from_segments(g_cqkv, g_z, g_xbc, g_small):
    w = jnp.concatenate([g_cqkv, g_small[:ROPE], g_z, g_xbc, g_small[ROPE:ROPE + SSD_H]], axis=0)
    return w.reshape(N_DEV, D_IN // N_DEV, D_MODEL)


_SMALL = (("q_norm_w", 512), ("kv_norm_w", 512), ("conv_b", CONV_DIM), ("dt_bias", SSD_H), ("a_log", SSD_H),
          ("d_skip", SSD_H), ("ssd_norm_w", SSD_W), ("attn_out_norm_w", 1024), ("pre_mix_norm_w", D_MODEL),
          ("post_mix_norm_w", D_MODEL), ("pre_ffn_norm_w", D_MODEL), ("post_ffn_norm_w", D_MODEL),
          ("conv_w", CONV_K * CONV_DIM))
_SMALL_ROWS = -(-sum(-(-n // LANE) for _, n in _SMALL) // 8) * 8


def _pack_small(vals):
    rows = []
    for name, n in _SMALL:
        v = vals[name].reshape(-1).astype(F32)
        pad = -(-n // LANE) * LANE
        rows.append(jnp.pad(v, (0, pad - n)).reshape(-1, LANE))
    m = jnp.concatenate(rows, axis=0)
    return jnp.pad(m, ((0, _SMALL_ROWS - m.shape[0]), (0, 0)))


def _unpack_small(m):
    out, r = {}, 0
    for name, n in _SMALL:
        nr = -(-n // LANE)
        out[name] = m[r:r + nr].reshape(-1)[:n]
        r += nr
    return out


def _head_row(v):
    return jnp.pad(v.reshape(1, -1).astype(F32), ((0, 0), (HEAD_LANE, LANE - HEAD_LANE - v.shape[-1])))


def _local_step(x, positions, target, wg, small, weights, on_grads):
    w_cqkv, w_z, w_xbc, w_small = _win_segments(wg["w_in"])
    conv_w = wg["conv_w"]
    conv_b = small["conv_b"].reshape(1, CONV_DIM)
    qkv_norm_w = jnp.concatenate([small["q_norm_w"], small["kv_norm_w"]])
    attn_norm_w = small["attn_out_norm_w"].reshape(1, HEADS * VDIM)
    scale = QK ** -0.5

    inv_freq = ROPE_THETA ** (-jnp.arange(0, ROPE, 2, dtype=F32) / ROPE)
    ang = positions.astype(F32)[:, None] * inv_freq
    cos2 = jnp.tile(jnp.cos(ang), (1, 2))
    sin2 = jnp.tile(jnp.sin(ang), (1, 2))

    u = _rms_fwd(x, small["pre_mix_norm_w"], out_dtype=MXU_DTYPE, name="pre_mix_norm")
    cqkv = _mm(u, w_cqkv, "nt", name="in_proj_qkv")
    z = _mm(u, w_z, "nt", name="in_proj_z")
    xbc = _mm(u, w_xbc, "nt", name="in_proj_xbc")
    sm = _mm(u, w_small, "nt", name="in_proj_small")

    w_uq, w_ukv = weights("qkv_up", cqkv)
    qkvn = _rms_fwd(cqkv, qkv_norm_w, groups=2, out_dtype=MXU_DTYPE, name="qkv_norm")
    q_h = _q_up(qkvn, w_uq, cos2, sin2, scale)
    k_h, v_h = _kv_up(qkvn, w_ukv, sm, cos2, sin2)
    o_h, lse = _flash_fwd(q_h, k_h, v_h)
    cat = _hnorm_fwd(o_h, attn_norm_w, D_MODEL)
    w_out = weights("out", o_h)[0].reshape(D_MODEL, D_MODEL)

    xbc_act = _conv_fwd(xbc, conv_w, conv_b)
    dtt = jnp.transpose(sm[:, HEAD_LANE:HEAD_LANE + SSD_H])
    ssd_args = (xbc_act, sm, dtt, _head_row(small["dt_bias"]), small["dt_bias"].reshape(SSD_H, 1),
                _head_row(small["a_log"]), small["a_log"].reshape(SSD_H, 1),
                jnp.broadcast_to(small["d_skip"].reshape(SSD_H, 1), (SSD_H, SSD_P)).reshape(SSD_PAIRS, 1, LANE))
    y_ssd, prev = _ssd_fwd(*ssd_args)
    cat = _gated_norm_fwd(y_ssd, z, small["ssd_norm_w"], cat)

    mix = _mm(cat, w_out, "nn", name="out_proj")
    h1, vv = _norm_res_norm(mix, x, small["post_mix_norm_w"], small["pre_ffn_norm_w"])

    w_gate, w_up = weights("ffn_in", mix)
    gate, up, act = _ffn_fwd(vv, w_gate, w_up)
    w_down, = weights("ffn_out", act)
    ffn = _mm(act, w_down, "nn", a_blk=True, b_blk=True, fuse=N_DEV, tm_max=512, name="ffn_down")
    loss_blk, dy, dffn, g_post_ffn = _loss_head(ffn, h1, target, small["post_ffn_norm_w"])

    g_down = _mm(act, dffn, "tn", a_blk=True, out_blk=True, out_dtype=MXU_DTYPE, name="g_down")
    dgate, dup = _ffn_bwd_act(dffn, w_down, gate, up)
    dvv = _ffn_bwd_in(dgate, w_gate, dup, w_up)
    g_gate = _mm(dgate, vv, "tn", a_blk=True, out_blk=True, out_dtype=MXU_DTYPE, name="g_gate")
    g_up = _mm(dup, vv, "tn", a_blk=True, out_blk=True, out_dtype=MXU_DTYPE, name="g_up")
    pre_ffn_w = small["pre_ffn_norm_w"] + on_grads("ffn", [g_gate, g_up, g_down])
    dh1, dmix, g_pre_ffn, g_post_mix = _norm_res_norm_bwd(h1, pre_ffn_w, dvv, dy, mix, small["post_mix_norm_w"])

    dcat = _mm(dmix, w_out, "nt", name="d_cat")
    g_out = _mm(cat, dmix, "tn", out_dtype=MXU_DTYPE, name="g_out")

    do_h, delta, g_attn_norm = _hnorm_bwd(o_h, attn_norm_w, dcat)
    dq_h, dk_h, dv_h = _flash_bwd(q_h, k_h, v_h, do_h, lse, delta)
    dq = _q_prep(dq_h, cos2, -sin2, scale, name="dq_post")

    dy_ssd, dz, g_ssd_norm = _gated_norm_bwd(y_ssd, z, small["ssd_norm_w"], dcat)
    dxbc_act, ddt, dpar = _ssd_bwd(*ssd_args, prev, dy_ssd)
    dkv, dsm = _dkv_post(dk_h, dv_h, ddt, cos2, -sin2)
    dpre, dwb = _conv_bwd_pre(xbc, conv_w, conv_b, dxbc_act)
    dxbc = _conv_bwd_in(dpre, conv_w)

    dqn = _mm(dq, w_uq, "nn", a_blk=True, b_blk=True, fuse=HEADS, name="d_qn")
    dkvn = _mm(dkv, w_ukv, "nt", a_blk=True, b_blk=True, fuse=HEADS, name="d_kvn")
    g_uq = _mm(dq, qkvn, "tn", a_blk=True, out_blk=True, b_cols=(0, Q_RANK), out_dtype=MXU_DTYPE, name="g_uq")
    g_ukv = _mm(qkvn, dkv, "tn", b_blk=True, out_blk=True, a_cols=(Q_RANK, KV_RANK), out_dtype=MXU_DTYPE, name="g_ukv")
    heads_token = on_grads("heads", [g_uq, g_ukv, g_out.reshape(N_DEV, D_MODEL // N_DEV, D_MODEL)])
    dcqkv, g_qkv_norm = _rms_bwd(cqkv, qkv_norm_w + heads_token, [dqn, dkvn], out_dtype=MXU_DTYPE, name="qkv_norm_bwd")

    g_in = _win_from_segments(_mm(dcqkv, u, "tn", out_dtype=MXU_DTYPE, name="g_in_qkv"),
                              _mm(dz, u, "tn", out_dtype=MXU_DTYPE, name="g_in_z"),
                              _mm(dxbc, u, "tn", out_dtype=MXU_DTYPE, name="g_in_xbc"),
                              _mm(dsm, u, "tn", out_dtype=MXU_DTYPE, name="g_in_small"))
    in_token = on_grads("in", [g_in])
    du = _mm_sum([dsm + in_token.astype(dsm.dtype), dcqkv, dz, dxbc], [w_small, w_cqkv, w_z, w_xbc], name="d_u")
    dx, g_pre_mix = _rms_bwd(x, small["pre_mix_norm_w"], [du], res=dh1, name="pre_mix_norm_bwd")

    hl = slice(HEAD_LANE, HEAD_LANE + SSD_H)
    g_small = {"q_norm_w": g_qkv_norm[0, :Q_RANK], "kv_norm_w": g_qkv_norm[0, Q_RANK:], "conv_b": dwb[CONV_K],
               "dt_bias": dpar[0, hl], "a_log": dpar[1, hl], "d_skip": dpar[2, hl], "ssd_norm_w": g_ssd_norm,
               "attn_out_norm_w": g_attn_norm, "pre_mix_norm_w": g_pre_mix, "post_mix_norm_w": g_post_mix,
               "pre_ffn_norm_w": g_pre_ffn, "post_ffn_norm_w": g_post_ffn, "conv_w": dwb[:CONV_K]}
    return loss_blk[0, 0], dx, g_small


_WEIGHT_ORDER = ("w_in", "q_norm_w", "w_uq", "kv_norm_w", "w_ukv", "conv_w", "conv_b", "dt_bias", "a_log", "d_skip",
                 "ssd_norm_w", "attn_out_norm_w", "w_out", "pre_mix_norm_w", "post_mix_norm_w", "pre_ffn_norm_w",
                 "post_ffn_norm_w", "w_gate", "w_up", "w_down")


def kernel(x, positions, w_in, q_norm_w, w_uq, kv_norm_w, w_ukv, conv_w, conv_b, dt_bias, a_log, d_skip, ssd_norm_w, attn_out_norm_w, w_out, pre_mix_norm_w, post_mix_norm_w, pre_ffn_norm_w, post_ffn_norm_w, w_gate, w_up, w_down, loss_target, m_w_in, m_q_norm_w, m_w_uq, m_kv_norm_w, m_w_ukv, m_conv_w, m_conv_b, m_dt_bias, m_a_log, m_d_skip, m_ssd_norm_w, m_attn_out_norm_w, m_w_out, m_pre_mix_norm_w, m_post_mix_norm_w, m_pre_ffn_norm_w, m_post_ffn_norm_w, m_w_gate, m_w_up, m_w_down, v_w_in, v_q_norm_w, v_w_uq, v_kv_norm_w, v_w_ukv, v_conv_w, v_conv_b, v_dt_bias, v_a_log, v_d_skip, v_ssd_norm_w, v_attn_out_norm_w, v_w_out, v_pre_mix_norm_w, v_post_mix_norm_w, v_pre_ffn_norm_w, v_post_ffn_norm_w, v_w_gate, v_w_up, v_w_down):
    w = dict(w_in=w_in, q_norm_w=q_norm_w, w_uq=w_uq, kv_norm_w=kv_norm_w, w_ukv=w_ukv, conv_w=conv_w, conv_b=conv_b,
             dt_bias=dt_bias, a_log=a_log, d_skip=d_skip, ssd_norm_w=ssd_norm_w, attn_out_norm_w=attn_out_norm_w,
             w_out=w_out, pre_mix_norm_w=pre_mix_norm_w, post_mix_norm_w=post_mix_norm_w,
             pre_ffn_norm_w=pre_ffn_norm_w, post_ffn_norm_w=post_ffn_norm_w, w_gate=w_gate, w_up=w_up, w_down=w_down)
    m = dict(w_in=m_w_in, q_norm_w=m_q_norm_w, w_uq=m_w_uq, kv_norm_w=m_kv_norm_w, w_ukv=m_w_ukv, conv_w=m_conv_w,
             conv_b=m_conv_b, dt_bias=m_dt_bias, a_log=m_a_log, d_skip=m_d_skip, ssd_norm_w=m_ssd_norm_w,
             attn_out_norm_w=m_attn_out_norm_w, w_out=m_w_out, pre_mix_norm_w=m_pre_mix_norm_w,
             post_mix_norm_w=m_post_mix_norm_w, pre_ffn_norm_w=m_pre_ffn_norm_w, post_ffn_norm_w=m_post_ffn_norm_w,
             w_gate=m_w_gate, w_up=m_w_up, w_down=m_w_down)
    v = dict(w_in=v_w_in, q_norm_w=v_q_norm_w, w_uq=v_w_uq, kv_norm_w=v_kv_norm_w, w_ukv=v_w_ukv, conv_w=v_conv_w,
             conv_b=v_conv_b, dt_bias=v_dt_bias, a_log=v_a_log, d_skip=v_d_skip, ssd_norm_w=v_ssd_norm_w,
             attn_out_norm_w=v_attn_out_norm_w, w_out=v_w_out, pre_mix_norm_w=v_pre_mix_norm_w,
             post_mix_norm_w=v_post_mix_norm_w, pre_ffn_norm_w=v_pre_ffn_norm_w, post_ffn_norm_w=v_post_ffn_norm_w,
             w_gate=v_w_gate, w_up=v_w_up, w_down=v_w_down)
    w, m, v = ({k: t[0] for k, t in d.items()} for d in (w, m, v))
    me = 4 * lax.axis_index("x") + 2 * lax.axis_index("y") + lax.axis_index("c")
    groups = {"qkv_up": ("w_uq", "w_ukv"), "out": ("w_out",), "ffn_in": ("w_gate", "w_up"), "ffn_out": ("w_down",)}
    cshard = CONV_DIM // N_DEV
    for name in _TRANSPOSED:
        w[name], m[name], v[name] = w[name].T, m[name].T, v[name].T

    shards = [w["w_in"].astype(MXU_DTYPE),
              jnp.stack(_split3(w["conv_w"])).reshape(3 * CONV_K, cshard).astype(MXU_DTYPE)]
    w_in_g, cw = _all_gather(shards, name="gather_weights")
    cw = cw.astype(F32).reshape(N_DEV, 3, CONV_K, cshard)
    wg = {"w_in": w_in_g, "conv_w": jnp.transpose(cw[:, 0] + cw[:, 1] + cw[:, 2], (1, 0, 2)).reshape(CONV_K, CONV_DIM)}
    arriving, dep, started = {}, wg["conv_w"], jnp.zeros((), F32)
    small = {name: w[name] for name, _ in _SMALL if name != "conv_w"}
    for group in ("qkv_up", "out", "ffn_in", "ffn_out"):
        token, arriving[group] = _exchange_behind([w[name].astype(MXU_DTYPE) for name in groups[group]], False,
                                                  dep, group + "_weights")
        started = started + token
        dep = jnp.zeros((8, LANE), F32) + started
    small["pre_mix_norm_w"] = small["pre_mix_norm_w"] + started

    leaving = {}

    def on_grads(group, gs):
        token, leaving[group] = _exchange_behind(gs, True, jnp.zeros((8, LANE), F32), group + "_grads")
        return token

    loss_local, dx, g_small = _local_step(x[0], positions[0], loss_target[0], wg, small,
                                          lambda group, after: arriving[group](after), on_grads)
    loss = lax.psum(loss_local, ("x", "y", "c"))

    recv = {}
    for group, names in (("ffn", ("w_gate", "w_up", "w_down")), ("heads", ("w_uq", "w_ukv", "w_out")), ("in", ("w_in",))):
        recv.update(zip(names, zip(*leaving[group](dx, place=False))))
    grads, deltas, new_m, new_v = {}, {}, {}, {}
    me1 = me.astype(jnp.int32).reshape(1)
    for name, (parts, own) in recv.items():
        outs = _adamw(parts, own, me1, w[name], m[name], v[name], name="adamw_" + name)
        if name in _TRANSPOSED:
            outs = [t.T for t in outs]
        grads[name], deltas[name], new_m[name], new_v[name] = outs

    def embed(t):
        return lax.dynamic_update_slice(jnp.zeros((CONV_K, CONV_DIM), F32), t, (0, me * cshard))

    parts_s = _all_gather([_pack_small(g_small)], name="gather_small_grads")[0]
    packs = [_pack_small({**{n_: d[n_] for n_, _ in _SMALL if n_ != "conv_w"}, "conv_w": embed(d["conv_w"])})
             for d in (w, m, v)]
    outs = [_unpack_small(t) for t in _adamw_small(parts_s, *packs)]
    for name, n in _SMALL:
        for dst, src in zip((grads, deltas, new_m, new_v), outs):
            if name == "conv_w":
                dst[name] = lax.dynamic_slice(src[name].reshape(CONV_K, CONV_DIM), (0, me * cshard), (CONV_K, cshard))
            else:
                dst[name] = src[name]

    def lead(d):
        return [d[name][None] for name in _WEIGHT_ORDER]

    return (loss, dx[None], *lead(grads), *lead(deltas), *lead(new_m), *lead(new_v))
```

```python
import numpy as np

import jax
import jax.numpy as jnp
from jax import lax
from jax.experimental import pallas as pl
from jax.experimental.pallas import tpu as pltpu

F32 = jnp.float32
BF16 = jnp.bfloat16
MXU_DTYPE = jnp.bfloat16
EPS = 1e-6
VMEM_LIMIT_BYTES = 48 * 1024 * 1024
K_TILE_MAX = 2048

N_DEV = 8
D_MODEL = 2048
Q_RANK = 512
KV_RANK = 512
ROPE = 64
HALF = ROPE // 2
HEADS = 8
NOPE = 128
VDIM = 128
QK = NOPE + ROPE
SSD_W = 1024
SSD_H = 16
SSD_P = 64
SSD_G = 2
SSD_E = SSD_H // SSD_G
SSD_N = 128
CHUNK = 128
CONV_K = 4
CONV_DIM = SSD_W + 2 * SSD_G * SSD_N
B_OFF = SSD_W
C_OFF = SSD_W + SSD_G * SSD_N
D_FF = 5632
D_IN = Q_RANK + KV_RANK + ROPE + SSD_W + CONV_DIM + SSD_H
ROPE_THETA = 10000.0
LANE = 128
HEAD_LANE = ROPE

ADAM_LR = 0.001
ADAM_B1 = 0.9
ADAM_B2 = 0.999
ADAM_EPS = 1e-08
ADAM_WD = 0.01
ADAM_STEP = 10


def _pick(n, cands):
    for c in cands:
        if n % c == 0:
            return c
    return n


def _params(*sem):
    return pltpu.CompilerParams(dimension_semantics=sem, vmem_limit_bytes=VMEM_LIMIT_BYTES)


def _sigmoid(x):
    return 1.0 / (1.0 + jnp.exp(-x))


def _silu(x):
    return x * _sigmoid(x)


def _dsilu(x):
    s = _sigmoid(x)
    return s * (1.0 + x * (1.0 - s))


def _softplus(x):
    e = jnp.exp(-jnp.abs(x))
    small = e * (1.0 - e * (0.5 - e * (1.0 / 3.0)))
    return jnp.maximum(x, 0.0) + jnp.where(e < 0.01, small, jnp.log(1.0 + e))


def _dot(a, b, ca, cb):
    return lax.dot_general(a, b, (((ca,), (cb,)), ((), ())), preferred_element_type=F32)


def _mx(v):
    return v.astype(MXU_DTYPE)


def _split3(a):
    hi = a.astype(BF16)
    r1 = a - hi.astype(F32)
    mid = r1.astype(BF16)
    lo = (r1 - mid.astype(F32)).astype(BF16)
    return hi, mid, lo


def _exact_dot(a, b, ca, cb, split_a):
    if split_a:
        return sum(_dot(p, b, ca, cb) for p in _split3(a))
    return sum(_dot(a, p, ca, cb) for p in _split3(b))


MM_ROW_GROUPS = 4


def _row_slices(tm, align):
    ng = MM_ROW_GROUPS
    while ng > 1 and (tm % ng or (tm // ng) % align):
        ng //= 2
    return [slice(g * (tm // ng), (g + 1) * (tm // ng)) for g in range(ng)]


def _mm(a, b, mode, *, a_blk=False, b_blk=False, out_blk=False, a_cols=None, b_cols=None, add=None, out_dtype=F32,
        fuse=1, wide=False, tm_max=1024, name="mm"):
    a2, b2 = a.shape[-2:], b.shape[-2:]
    a_last = a2[1] if a_cols is None else a_cols[1]
    a_start = 0 if a_cols is None else a_cols[0]
    b_start = 0
    if b_cols is not None:
        assert mode != "nt"
        b_start, b2 = b_cols[0], (b2[0], b_cols[1])
    if mode == "nn":
        m, k, (k2, n) = a2[0], a_last, b2
    elif mode == "nt":
        m, k, (n, k2) = a2[0], a_last, b2
    else:
        k, m, (k2, n) = a2[0], a_last, b2
    assert k == k2, (a.shape, b.shape, mode)
    tm = _pick(m, tuple(c for c in (1024, 704, 512, 256, 128) if c <= tm_max))
    tn = _pick(n, ((2048,) if wide else ()) + (1024, 768, 704, 512, 256, 192, 128))
    k_max = 2 * K_TILE_MAX if mode == "tn" else K_TILE_MAX
    tk = k if k <= k_max else _pick(k, (K_TILE_MAX, 1024, 512))
    nk = k // tk
    jo = N_DEV if out_blk else 1
    reduce_blocks = a_blk and b_blk and not out_blk
    assert fuse == 1 or reduce_blocks
    jr = N_DEV // fuse if reduce_blocks else 1
    ca, cb = {"nn": (1, 0), "nt": (1, 1), "tn": (0, 0)}[mode]
    has_add = add is not None
    single = jr * nk == 1
    if mode == "tn":
        assert a_start % tm == 0
        a_block, a_idx = (tk, tm), (lambda i, kk: (kk, i + a_start // tm))
    else:
        assert a_start % tk == 0
        a_block, a_idx = (tm, tk), (lambda i, kk: (i, kk + a_start // tk))
    assert b_start % tn == 0
    b_block, b_idx = (((tn, tk), (lambda nn_, kk: (nn_, kk))) if mode == "nt"
                      else ((tk, tn), (lambda nn_, kk: (kk, nn_ + b_start // tn))))

    def blk_specs(blocked, block, idx, of_a, t):
        def pos(o, i, nn_, kk):
            return idx(i, kk) if of_a else idx(nn_, kk)
        if blocked:
            return pl.BlockSpec((None,) + block,
                                lambda o, i, nn_, r, kk: ((o if out_blk else r * fuse + t),) + pos(o, i, nn_, kk))
        return pl.BlockSpec(block, lambda o, i, nn_, r, kk: pos(o, i, nn_, kk))

    a_specs = [blk_specs(a_blk, a_block, a_idx, True, t) for t in range(fuse)]
    b_specs = [blk_specs(b_blk, b_block, b_idx, False, t) for t in range(fuse)]
    o_spec = (pl.BlockSpec((None, tm, tn), lambda o, i, nn_, r, kk: (o, i, nn_)) if out_blk
              else pl.BlockSpec((tm, tn), lambda o, i, nn_, r, kk: (i, nn_)))

    groups = _row_slices(tm, LANE if mode == "tn" else 16)

    def body(*refs):
        a_refs, b_refs = refs[:fuse], refs[fuse:2 * fuse]
        add_ref = refs[2 * fuse] if has_add else None
        o_ref = refs[2 * fuse + 1] if has_add else refs[2 * fuse]

        def partial(rs):
            out = None
            for t in range(fuse):
                av = a_refs[t][:, rs] if mode == "tn" else a_refs[t][rs, :]
                d = _dot(_mx(av), _mx(b_refs[t][...]), ca, cb)
                out = d if out is None else out + d
            return out

        if single:
            for rs in groups:
                res = partial(rs)
                if has_add:
                    res = res + add_ref[rs, :]
                o_ref[rs, :] = res.astype(o_ref.dtype)
            return
        acc = refs[-1]
        r, kk = pl.program_id(3), pl.program_id(4)

        @pl.when(jnp.logical_and(r == 0, kk == 0))
        def _():
            acc[...] = jnp.zeros_like(acc)

        for rs in groups:
            acc[rs, :] += partial(rs)

        @pl.when(jnp.logical_and(r == jr - 1, kk == nk - 1))
        def _():
            res = acc[...]
            if has_add:
                res = res + add_ref[...]
            o_ref[...] = res.astype(o_ref.dtype)

    out_shape = ((N_DEV, m, n) if out_blk else (m, n))
    return pl.pallas_call(
        body, name=name, grid=(jo, m // tm, n // tn, jr, nk),
        in_specs=a_specs + b_specs + ([o_spec] if has_add else []), out_specs=o_spec,
        out_shape=jax.ShapeDtypeStruct(out_shape, out_dtype),
        scratch_shapes=[] if single else [pltpu.VMEM((tm, tn), F32)],
        compiler_params=_params("parallel", "parallel", "parallel", "arbitrary", "arbitrary"),
    )(*((a,) * fuse + (b,) * fuse + ((add,) if has_add else ())))


def _mm_sum(a_list, b_list, name="mm_sum"):
    m, n = a_list[0].shape[0], b_list[0].shape[1]
    ns = len(a_list)
    tm = _pick(m, (1024, 512, 256, 128))
    tn = _pick(n, (1024, 512, 256, 128))
    groups = _row_slices(tm, 16)

    def body(*refs):
        a_refs, b_refs, o_ref = refs[:ns], refs[ns:2 * ns], refs[2 * ns]
        for rs in groups:
            acc = _dot(_mx(a_refs[0][rs, :]), _mx(b_refs[0][...]), 1, 0)
            for s in range(1, ns):
                acc = acc + _dot(_mx(a_refs[s][rs, :]), _mx(b_refs[s][...]), 1, 0)
            o_ref[rs, :] = acc

    return pl.pallas_call(
        body, name=name, grid=(m // tm, n // tn),
        in_specs=([pl.BlockSpec((tm, a.shape[1]), lambda i, j: (i, 0)) for a in a_list]
                  + [pl.BlockSpec((b.shape[0], tn), lambda i, j: (0, j)) for b in b_list]),
        out_specs=pl.BlockSpec((tm, tn), lambda i, j: (i, j)),
        out_shape=jax.ShapeDtypeStruct((m, n), F32), compiler_params=_params("parallel", "parallel"),
    )(*a_list, *b_list)


def _row_tile(r_, streams=4):
    return _pick(r_, ((512,) if streams <= 4 else ()) + (256, 128, 64, 32, 16, 8))


def _rms_fwd(t, w, groups=1, res=None, out_dtype=F32, name="rms_fwd"):
    r_, f = t.shape
    fg = f // groups
    tr = _row_tile(r_)
    has_res = res is not None

    def body(*refs):
        t_ref, w_ref = refs[0], refs[1]
        res_ref = refs[2] if has_res else None
        o_ref = refs[-1]
        for g in range(groups):
            sl = slice(g * fg, (g + 1) * fg)
            tv = t_ref[:, sl].astype(F32)
            r = lax.rsqrt(jnp.mean(tv * tv, axis=-1, keepdims=True) + EPS)
            y = tv * r * w_ref[:, sl]
            if has_res:
                y = y + res_ref[:, sl]
            o_ref[:, sl] = y.astype(o_ref.dtype)

    row = pl.BlockSpec((tr, f), lambda i: (i, 0))
    wsp = pl.BlockSpec((1, f), lambda i: (0, 0))
    return pl.pallas_call(
        body, name=name, grid=(r_ // tr,),
        in_specs=[row, wsp] + ([row] if has_res else []), out_specs=row,
        out_shape=jax.ShapeDtypeStruct((r_, f), out_dtype),
        compiler_params=_params("parallel"),
    )(*((t, w.reshape(1, f)) + ((res,) if has_res else ())))


def _rms_bwd(t, w, dys, res=None, out_dtype=F32, name="rms_bwd"):
    r_, f = t.shape
    groups = len(dys)
    fg = f // groups
    tr = _row_tile(r_)
    has_res = res is not None

    def body(*refs):
        t_ref, w_ref = refs[0], refs[1]
        dy_refs = refs[2:2 + groups]
        res_ref = refs[2 + groups] if has_res else None
        dt_ref, dw_ref = refs[-2], refs[-1]

        @pl.when(pl.program_id(0) == 0)
        def _():
            dw_ref[...] = jnp.zeros_like(dw_ref)

        for g in range(groups):
            sl = slice(g * fg, (g + 1) * fg)
            tv = t_ref[:, sl].astype(F32)
            dyv = dy_refs[g][...].astype(F32)
            r = lax.rsqrt(jnp.mean(tv * tv, axis=-1, keepdims=True) + EPS)
            gw = dyv * w_ref[:, sl]
            c = jnp.mean(gw * tv, axis=-1, keepdims=True)
            dt = r * gw - tv * (r * r * r * c)
            if has_res:
                dt = dt + res_ref[:, sl]
            dt_ref[:, sl] = dt.astype(dt_ref.dtype)
            dw_ref[:, sl] += jnp.sum(dyv * tv * r, axis=0, keepdims=True)

    row = pl.BlockSpec((tr, f), lambda i: (i, 0))
    grow = pl.BlockSpec((tr, fg), lambda i: (i, 0))
    wsp = pl.BlockSpec((1, f), lambda i: (0, 0))
    return pl.pallas_call(
        body, name=name, grid=(r_ // tr,),
        in_specs=[row, wsp] + [grow] * groups + ([row] if has_res else []), out_specs=[row, wsp],
        out_shape=[jax.ShapeDtypeStruct((r_, f), out_dtype), jax.ShapeDtypeStruct((1, f), F32)],
        compiler_params=_params("arbitrary"),
    )(*((t, w.reshape(1, f)) + tuple(dys) + ((res,) if has_res else ())))


def _norm_res_norm(t, res, w1, w2, name="post_mix_pre_ffn_norm"):
    r_, f = t.shape
    tr = _row_tile(r_)

    def body(t_ref, res_ref, w1_ref, w2_ref, h_ref, v_ref):
        tv = t_ref[...]
        h = res_ref[...] + tv * lax.rsqrt(jnp.mean(tv * tv, axis=-1, keepdims=True) + EPS) * w1_ref[...]
        h_ref[...] = h
        v_ref[...] = (h * lax.rsqrt(jnp.mean(h * h, axis=-1, keepdims=True) + EPS) * w2_ref[...]).astype(v_ref.dtype)

    row = pl.BlockSpec((tr, f), lambda i: (i, 0))
    wsp = pl.BlockSpec((1, f), lambda i: (0, 0))
    return pl.pallas_call(
        body, name=name, grid=(r_ // tr,), in_specs=[row, row, wsp, wsp], out_specs=[row, row],
        out_shape=[jax.ShapeDtypeStruct((r_, f), F32), jax.ShapeDtypeStruct((r_, f), MXU_DTYPE)],
        compiler_params=_params("parallel"),
    )(t, res, w1.reshape(1, f), w2.reshape(1, f))


def _norm_res_norm_bwd(h, w2, dv, dres, t, w1, name="pre_ffn_post_mix_norm_bwd"):
    r_, f = h.shape
    tr = _row_tile(r_, streams=6)

    def body(h_ref, w2_ref, dv_ref, dres_ref, t_ref, w1_ref, dh_ref, dt_ref, dw2_ref, dw1_ref):
        @pl.when(pl.program_id(0) == 0)
        def _():
            dw2_ref[...] = jnp.zeros_like(dw2_ref)
            dw1_ref[...] = jnp.zeros_like(dw1_ref)

        def rms_bwd(tv, wv, dyv):
            r = lax.rsqrt(jnp.mean(tv * tv, axis=-1, keepdims=True) + EPS)
            gw = dyv * wv
            c = jnp.mean(gw * tv, axis=-1, keepdims=True)
            return r * gw - tv * (r * r * r * c), jnp.sum(dyv * tv * r, axis=0, keepdims=True)

        d1, g2 = rms_bwd(h_ref[...], w2_ref[...], dv_ref[...])
        dh = d1 + dres_ref[...]
        dh_ref[...] = dh
        dw2_ref[...] += g2
        d2, g1 = rms_bwd(t_ref[...], w1_ref[...], dh)
        dt_ref[...] = d2.astype(dt_ref.dtype)
        dw1_ref[...] += g1

    row = pl.BlockSpec((tr, f), lambda i: (i, 0))
    wsp = pl.BlockSpec((1, f), lambda i: (0, 0))
    return pl.pallas_call(
        body, name=name, grid=(r_ // tr,), in_specs=[row, wsp, row, row, row, wsp], out_specs=[row, row, wsp, wsp],
        out_shape=[jax.ShapeDtypeStruct((r_, f), F32), jax.ShapeDtypeStruct((r_, f), MXU_DTYPE),
                   jax.ShapeDtypeStruct((1, f), F32), jax.ShapeDtypeStruct((1, f), F32)],
        compiler_params=_params("arbitrary"),
    )(h, w2.reshape(1, f), dv, dres, t, w1.reshape(1, f))


def _hnorm_fwd(o, w, width, name="attn_out_norm"):
    h, s_, v = o.shape
    tr = _row_tile(s_)

    def body(o_ref, w_ref, y_ref):
        ss = jnp.sum(o_ref[0] * o_ref[0], axis=-1, keepdims=True)
        for i in range(1, h):
            ss = ss + jnp.sum(o_ref[i] * o_ref[i], axis=-1, keepdims=True)
        r = lax.rsqrt(ss * (1.0 / (h * v)) + EPS)
        for i in range(h):
            sl = slice(i * v, (i + 1) * v)
            y_ref[:, sl] = (o_ref[i] * r * w_ref[:, sl]).astype(y_ref.dtype)

    return pl.pallas_call(
        body, name=name, grid=(s_ // tr,),
        in_specs=[pl.BlockSpec((h, tr, v), lambda i: (0, i, 0)), pl.BlockSpec((1, h * v), lambda i: (0, 0))],
        out_specs=pl.BlockSpec((tr, h * v), lambda i: (i, 0)),
        out_shape=jax.ShapeDtypeStruct((s_, width), MXU_DTYPE), compiler_params=_params("parallel"),
    )(o, w)


def _hnorm_bwd(o, w, dy, name="attn_out_norm_bwd"):
    h, s_, v = o.shape
    tr = _row_tile(s_)

    def body(o_ref, w_ref, dy_ref, do_ref, delta_ref, dw_ref):
        @pl.when(pl.program_id(0) == 0)
        def _():
            dw_ref[...] = jnp.zeros_like(dw_ref)

        ss = jnp.zeros((tr, 1), F32)
        cc = jnp.zeros((tr, 1), F32)
        for i in range(h):
            sl = slice(i * v, (i + 1) * v)
            ov = o_ref[i]
            ss = ss + jnp.sum(ov * ov, axis=-1, keepdims=True)
            cc = cc + jnp.sum(dy_ref[:, sl] * w_ref[:, sl] * ov, axis=-1, keepdims=True)
        r = lax.rsqrt(ss * (1.0 / (h * v)) + EPS)
        c = cc * (1.0 / (h * v))
        for i in range(h):
            sl = slice(i * v, (i + 1) * v)
            ov = o_ref[i]
            dyv = dy_ref[:, sl]
            dov = r * dyv * w_ref[:, sl] - ov * (r * r * r * c)
            do_ref[i] = dov.astype(do_ref.dtype)
            delta_ref[i] = jnp.sum(dov * ov, axis=-1, keepdims=True)
            dw_ref[:, sl] += jnp.sum(dyv * ov * r, axis=0, keepdims=True)

    blk = pl.BlockSpec((h, tr, v), lambda i: (0, i, 0))
    wsp = pl.BlockSpec((1, h * v), lambda i: (0, 0))
    return pl.pallas_call(
        body, name=name, grid=(s_ // tr,),
        in_specs=[blk, wsp, pl.BlockSpec((tr, h * v), lambda i: (i, 0))],
        out_specs=[blk, pl.BlockSpec((h, tr, 1), lambda i: (0, i, 0)), wsp],
        out_shape=[jax.ShapeDtypeStruct(o.shape, MXU_DTYPE), jax.ShapeDtypeStruct((h, s_, 1), F32),
                   jax.ShapeDtypeStruct((1, h * v), F32)],
        compiler_params=_params("arbitrary"),
    )(o, w, dy)


def _loss_head(ffn, h1, target, w, name="loss_head"):
    r_, f = ffn.shape
    tr = _row_tile(r_)

    def body(ffn_ref, h1_ref, tg_ref, w_ref, loss_ref, dy_ref, dffn_ref, dw_ref):
        @pl.when(pl.program_id(0) == 0)
        def _():
            dw_ref[...] = jnp.zeros_like(dw_ref)
            loss_ref[...] = jnp.zeros_like(loss_ref)

        tv = ffn_ref[...]
        wv = w_ref[...]
        r = lax.rsqrt(jnp.mean(tv * tv, axis=-1, keepdims=True) + EPS)
        tn = tv * r
        e = h1_ref[...] + tn * wv - tg_ref[...]
        tot = jnp.sum(jnp.sum(e * e, axis=1, keepdims=True), axis=0, keepdims=True) * (0.5 / f)
        loss_ref[...] += tot + jnp.zeros_like(loss_ref)
        dyv = e * (1.0 / f)
        dy_ref[...] = dyv
        gw = dyv * wv
        c = jnp.mean(gw * tv, axis=-1, keepdims=True)
        dffn_ref[...] = (r * gw - tv * (r * r * r * c)).astype(dffn_ref.dtype)
        dw_ref[...] += jnp.sum(dyv * tn, axis=0, keepdims=True)

    row = pl.BlockSpec((tr, f), lambda i: (i, 0))
    wsp = pl.BlockSpec((1, f), lambda i: (0, 0))
    lsp = pl.BlockSpec((1, LANE), lambda i: (0, 0))
    return pl.pallas_call(
        body, name=name, grid=(r_ // tr,),
        in_specs=[row, row, row, wsp], out_specs=[lsp, row, row, wsp],
        out_shape=[jax.ShapeDtypeStruct((1, LANE), F32), jax.ShapeDtypeStruct((r_, f), F32),
                   jax.ShapeDtypeStruct((r_, f), MXU_DTYPE), jax.ShapeDtypeStruct((1, f), F32)],
        compiler_params=_params("arbitrary"),
    )(ffn, h1, target, w.reshape(1, f))


def _rot_matrix():
    p = np.zeros((ROPE, ROPE), np.float32)
    for i in range(HALF):
        p[i + HALF, i] = -1.0
        p[i, i + HALF] = 1.0
    return jnp.asarray(p, BF16)


def _rope_val(r, c2, s2, rot):
    hi, mid, _ = _split3(r)
    return r * c2 + (_dot(hi, rot, 1, 0) + _dot(mid, rot, 1, 0)) * s2


def _q_prep(q, cos2, sin2, scale, name):
    h, s_, _ = q.shape
    tr = _pick(s_, (4096, 2048, 1024, 512, 256, 128, 64, 32, 16))

    def body(q_ref, c_ref, s_ref, rot_ref, o_ref):
        for rs in _row_slices(tr, 16):
            x = q_ref[rs, :]
            o_ref[rs, :NOPE] = (x[:, :NOPE] * scale).astype(o_ref.dtype)
            o_ref[rs, NOPE:] = (_rope_val(x[:, NOPE:], c_ref[rs, :], s_ref[rs, :], rot_ref[...]) * scale).astype(o_ref.dtype)

    blk = pl.BlockSpec((None, tr, QK), lambda hh, i: (hh, i, 0))
    csp = pl.BlockSpec((tr, ROPE), lambda hh, i: (i, 0))
    return pl.pallas_call(
        body, name=name, grid=(h, s_ // tr),
        in_specs=[blk, csp, csp, pl.BlockSpec((ROPE, ROPE), lambda hh, i: (0, 0))], out_specs=blk,
        out_shape=jax.ShapeDtypeStruct(q.shape, MXU_DTYPE), compiler_params=_params("parallel", "parallel"),
    )(q, cos2, sin2, _rot_matrix())


def _q_up(qkvn, w_uq_t, cos2, sin2, scale, name="q_up"):
    s_ = qkvn.shape[0]
    h = w_uq_t.shape[0]
    tm = _pick(s_, (4096, 2048, 1024, 512, 256, 128))

    def body(a_ref, w_ref, c_ref, s_ref, rot_ref, o_ref):
        for rs in _row_slices(tm, 16):
            x = _dot(_mx(a_ref[rs, :]), _mx(w_ref[...]), 1, 1)
            o_ref[rs, :NOPE] = (x[:, :NOPE] * scale).astype(o_ref.dtype)
            o_ref[rs, NOPE:] = (_rope_val(x[:, NOPE:], c_ref[rs, :], s_ref[rs, :], rot_ref[...]) * scale).astype(o_ref.dtype)

    csp = pl.BlockSpec((tm, ROPE), lambda j, i: (i, 0))
    return pl.pallas_call(
        body, name=name, grid=(h, s_ // tm),
        in_specs=[pl.BlockSpec((tm, Q_RANK), lambda j, i: (i, 0)), pl.BlockSpec((None, QK, Q_RANK), lambda j, i: (j, 0, 0)),
                  csp, csp, pl.BlockSpec((ROPE, ROPE), lambda j, i: (0, 0))],
        out_specs=pl.BlockSpec((None, tm, QK), lambda j, i: (j, i, 0)),
        out_shape=jax.ShapeDtypeStruct((h, s_, QK), MXU_DTYPE), compiler_params=_params("parallel", "parallel"),
    )(qkvn, w_uq_t, cos2, sin2, _rot_matrix())


def _kv_up(qkvn, w_ukv, small, cos2, sin2, name="kv_up"):
    s_ = qkvn.shape[0]
    h = w_ukv.shape[0]
    tm = _pick(s_, (4096, 2048, 1024, 512, 256, 128))

    def body(a_ref, w_ref, sm_ref, c_ref, s_ref, rot_ref, k_ref, v_ref):
        for rs in _row_slices(tm, 16):
            x = _dot(_mx(a_ref[rs, :]), _mx(w_ref[...]), 1, 0)
            k_ref[rs, :NOPE] = x[:, :NOPE].astype(k_ref.dtype)
            k_ref[rs, NOPE:] = _rope_val(sm_ref[rs, :ROPE], c_ref[rs, :], s_ref[rs, :], rot_ref[...]).astype(k_ref.dtype)
            v_ref[rs, :] = x[:, NOPE:].astype(v_ref.dtype)

    csp = pl.BlockSpec((tm, ROPE), lambda j, i: (i, 0))
    return pl.pallas_call(
        body, name=name, grid=(h, s_ // tm),
        in_specs=[pl.BlockSpec((tm, KV_RANK), lambda j, i: (i, Q_RANK // KV_RANK)),
                  pl.BlockSpec((None, KV_RANK, NOPE + VDIM), lambda j, i: (j, 0, 0)),
                  pl.BlockSpec((tm, LANE), lambda j, i: (i, 0)), csp, csp, pl.BlockSpec((ROPE, ROPE), lambda j, i: (0, 0))],
        out_specs=[pl.BlockSpec((None, tm, QK), lambda j, i: (j, i, 0)), pl.BlockSpec((None, tm, VDIM), lambda j, i: (j, i, 0))],
        out_shape=[jax.ShapeDtypeStruct((h, s_, QK), MXU_DTYPE), jax.ShapeDtypeStruct((h, s_, VDIM), MXU_DTYPE)],
        compiler_params=_params("parallel", "parallel"),
    )(qkvn, w_ukv, small, cos2, sin2, _rot_matrix())


def _dkv_post(dk, dv, ddt, cos2, nsin2, name="dkv_post"):
    h, s_, _ = dk.shape
    tr = _row_tile(s_)

    def body(dk_ref, dv_ref, ddt_ref, c_ref, s_ref, rot_ref, dkv_ref, dsm_ref):
        acc = dk_ref[0, :, NOPE:]
        for i in range(1, h):
            acc = acc + dk_ref[i, :, NOPE:]
        dsm_ref[:, :ROPE] = _rope_val(acc, c_ref[...], s_ref[...], rot_ref[...]).astype(dsm_ref.dtype)
        dsm_ref[:, ROPE:] = ddt_ref[:, ROPE:].astype(dsm_ref.dtype)
        for i in range(h):
            dkv_ref[i, :, :NOPE] = dk_ref[i, :, :NOPE].astype(dkv_ref.dtype)
            dkv_ref[i, :, NOPE:] = dv_ref[i].astype(dkv_ref.dtype)

    csp = pl.BlockSpec((tr, ROPE), lambda i: (i, 0))
    return pl.pallas_call(
        body, name=name, grid=(s_ // tr,),
        in_specs=[pl.BlockSpec((h, tr, QK), lambda i: (0, i, 0)), pl.BlockSpec((h, tr, VDIM), lambda i: (0, i, 0)),
                  pl.BlockSpec((tr, LANE), lambda i: (i, 0)), csp, csp, pl.BlockSpec((ROPE, ROPE), lambda i: (0, 0))],
        out_specs=[pl.BlockSpec((h, tr, NOPE + VDIM), lambda i: (0, i, 0)), pl.BlockSpec((tr, LANE), lambda i: (i, 0))],
        out_shape=[jax.ShapeDtypeStruct((h, s_, NOPE + VDIM), MXU_DTYPE), jax.ShapeDtypeStruct((s_, LANE), MXU_DTYPE)],
        compiler_params=_params("parallel"),
    )(dk, dv, ddt, cos2, nsin2, _rot_matrix())


def _attn_tile(s):
    return 2048 if s % 4096 == 0 else s // 2


def _pairs(n, by_key):
    if by_key:
        pr = [(i, j) for j in range(n) for i in range(j, n)]
    else:
        pr = [(i, j) for i in range(n) for j in range(i + 1)]
    return (jnp.asarray([p[0] for p in pr], jnp.int32), jnp.asarray([p[1] for p in pr], jnp.int32))


ATTN_ROW_GROUPS = 8


def _row_groups(t, diag):
    tg = t // ATTN_ROW_GROUPS
    out = []
    for r in range(ATTN_ROW_GROUPS):
        nc = (r + 1) * tg if diag else t
        mask = None
        if diag:
            mask = (lax.broadcasted_iota(jnp.int32, (tg, nc), 1)
                    <= lax.broadcasted_iota(jnp.int32, (tg, nc), 0) + r * tg)
        out.append((slice(r * tg, (r + 1) * tg), nc, mask))
    return out


def _flash_specs(t, dk, dv):
    qsp = pl.BlockSpec((None, t, dk), lambda hh, p, qi, kj: (hh, qi[p], 0))
    ksp = pl.BlockSpec((None, t, dk), lambda hh, p, qi, kj: (hh, kj[p], 0))
    vsp = pl.BlockSpec((None, t, dv), lambda hh, p, qi, kj: (hh, kj[p], 0))
    osp = pl.BlockSpec((None, t, dv), lambda hh, p, qi, kj: (hh, qi[p], 0))
    lsp = pl.BlockSpec((None, t, 1), lambda hh, p, qi, kj: (hh, qi[p], 0))
    return qsp, ksp, vsp, osp, lsp


def _flash_fwd(q, k, v, name="flash_fwd"):
    h, s_, dk = q.shape
    dv = v.shape[-1]
    t = _attn_tile(s_)
    n = s_ // t
    qi, kj = _pairs(n, False)

    def body(qi_ref, kj_ref, q_ref, k_ref, v_ref, o_ref, lse_ref, m_s, l_s, acc):
        p_ = pl.program_id(1)
        i, j = qi_ref[p_], kj_ref[p_]

        @pl.when(j == 0)
        def _():
            m_s[...] = jnp.full_like(m_s, -jnp.inf)
            l_s[...] = jnp.zeros_like(l_s)
            acc[...] = jnp.zeros_like(acc)

        def update(diag):
            for rs, nc, mask in _row_groups(t, diag):
                sc = _dot(q_ref[rs, :], k_ref[0:nc, :], 1, 1)
                if mask is not None:
                    sc = jnp.where(mask, sc, -jnp.inf)
                m_old = m_s[rs, :]
                m_new = jnp.maximum(m_old, jnp.max(sc, axis=1, keepdims=True))
                alpha = jnp.exp(m_old - m_new)
                p = jnp.exp(sc - m_new)
                l_s[rs, :] = alpha * l_s[rs, :] + jnp.sum(p, axis=1, keepdims=True)
                acc[rs, :] = alpha * acc[rs, :] + _dot(_mx(p), v_ref[0:nc, :], 1, 0)
                m_s[rs, :] = m_new

        @pl.when(j < i)
        def _():
            update(False)

        @pl.when(j == i)
        def _():
            update(True)
            o_ref[...] = acc[...] / l_s[...]
            lse_ref[...] = m_s[...] + jnp.log(l_s[...])

    qsp, ksp, vsp, osp, lsp = _flash_specs(t, dk, dv)
    gs = pltpu.PrefetchScalarGridSpec(
        num_scalar_prefetch=2, grid=(h, qi.shape[0]), in_specs=[qsp, ksp, vsp], out_specs=[osp, lsp],
        scratch_shapes=[pltpu.VMEM((t, 1), F32), pltpu.VMEM((t, 1), F32), pltpu.VMEM((t, dv), F32)])
    return pl.pallas_call(
        body, name=name, grid_spec=gs,
        out_shape=[jax.ShapeDtypeStruct((h, s_, dv), F32), jax.ShapeDtypeStruct((h, s_, 1), F32)],
        compiler_params=_params("parallel", "arbitrary"),
    )(qi, kj, q, k, v)


def _flash_bwd(q, k, v, do, lse, delta, name="flash_bwd"):
    h, s_, dk = q.shape
    dv = v.shape[-1]
    t = _attn_tile(s_)
    tg = t // ATTN_ROW_GROUPS
    n = s_ // t
    qi, kj = _pairs(n, True)

    def body(qi_ref, kj_ref, q_ref, k_ref, v_ref, do_ref, lse_ref, delta_ref, dq_ref, dk_ref, dv_ref, dk_acc, dv_acc):
        p_ = pl.program_id(1)
        i, j = qi_ref[p_], kj_ref[p_]

        @pl.when(p_ == 0)
        def _():
            dq_ref[...] = jnp.zeros_like(dq_ref)

        def update(diag):
            for g, (rs, nc, mask) in enumerate(_row_groups(t, diag)):
                sc = _dot(q_ref[rs, :], k_ref[0:nc, :], 1, 1)
                if mask is not None:
                    sc = jnp.where(mask, sc, -jnp.inf)
                p = jnp.exp(sc - lse_ref[rs, :])
                dob = _mx(do_ref[rs, :])
                dv_acc[0:nc, :] += _dot(_mx(p), dob, 0, 0)
                dp = _dot(dob, v_ref[0:nc, :], 1, 1)
                dsb = _mx(p * (dp - delta_ref[rs, :]))
                dk_acc[0:nc, :] += _dot(dsb, q_ref[rs, :], 0, 0)
                rows = pl.ds(pl.multiple_of(i * t + g * tg, tg), tg)
                dq_ref[rows, :] += _dot(dsb, k_ref[0:nc, :], 1, 0)

        @pl.when(i == j)
        def _():
            dk_acc[...] = jnp.zeros_like(dk_acc)
            dv_acc[...] = jnp.zeros_like(dv_acc)
            update(True)

        @pl.when(i > j)
        def _():
            update(False)

        @pl.when(i == n - 1)
        def _():
            dk_ref[...] = dk_acc[...]
            dv_ref[...] = dv_acc[...]

    qsp, ksp, vsp, osp, lsp = _flash_specs(t, dk, dv)
    dqsp = pl.BlockSpec((None, s_, dk), lambda hh, p, qi, kj: (hh, 0, 0))
    gs = pltpu.PrefetchScalarGridSpec(
        num_scalar_prefetch=2, grid=(h, qi.shape[0]), in_specs=[qsp, ksp, vsp, osp, lsp, lsp],
        out_specs=[dqsp, ksp, vsp],
        scratch_shapes=[pltpu.VMEM((t, dk), F32), pltpu.VMEM((t, dv), F32)])
    return pl.pallas_call(
        body, name=name, grid_spec=gs,
        out_shape=[jax.ShapeDtypeStruct((h, s_, dk), F32), jax.ShapeDtypeStruct((h, s_, dk), F32),
                   jax.ShapeDtypeStruct((h, s_, dv), F32)],
        compiler_params=_params("parallel", "arbitrary"),
    )(qi, kj, q, k, v, do, lse, delta)


HALO = 8


def _conv_specs(s_, c, tr, after):
    main = pl.BlockSpec((tr, c), lambda i: (i, 0))
    per = tr // HALO
    if after:
        halo = pl.BlockSpec((HALO, c), lambda i: (jnp.minimum((i + 1) * per, s_ // HALO - 1), 0))
    else:
        halo = pl.BlockSpec((HALO, c), lambda i: (jnp.maximum(i * per - 1, 0), 0))
    return main, halo


def _fill_before(ext, t_ref, h_ref, tr):
    ext[0:HALO, :] = jnp.where(pl.program_id(0) > 0, h_ref[...], 0.0)
    ext[HALO:HALO + tr, :] = t_ref[...]


def _taps(ext, w_ref, tr):
    base = HALO - (CONV_K - 1)
    acc = ext[base:base + tr, :] * w_ref[0:1, :]
    for k in range(1, CONV_K):
        acc = acc + ext[base + k:base + k + tr, :] * w_ref[k:k + 1, :]
    return acc


def _conv_fwd(t, w, b, name="conv_fwd"):
    s_, c = t.shape
    tr = _row_tile(s_)

    def body(t_ref, h_ref, w_ref, b_ref, o_ref, ext):
        _fill_before(ext, t_ref, h_ref, tr)
        o_ref[...] = _silu(_taps(ext, w_ref, tr) + b_ref[...])

    main, halo = _conv_specs(s_, c, tr, False)
    return pl.pallas_call(
        body, name=name, grid=(s_ // tr,),
        in_specs=[main, halo, pl.BlockSpec((CONV_K, c), lambda i: (0, 0)), pl.BlockSpec((1, c), lambda i: (0, 0))],
        out_specs=main, out_shape=jax.ShapeDtypeStruct((s_, c), F32),
        scratch_shapes=[pltpu.VMEM((tr + HALO, c), F32)], compiler_params=_params("parallel"),
    )(t, t, w, b)


def _conv_bwd_pre(t, w, b, dact, name="conv_bwd_pre"):
    s_, c = t.shape
    tr = _row_tile(s_)

    def body(t_ref, h_ref, w_ref, b_ref, da_ref, dpre_ref, dwb_ref, ext):
        @pl.when(pl.program_id(0) == 0)
        def _():
            dwb_ref[...] = jnp.zeros_like(dwb_ref)

        _fill_before(ext, t_ref, h_ref, tr)
        dpre = da_ref[...] * _dsilu(_taps(ext, w_ref, tr) + b_ref[...])
        dpre_ref[...] = dpre
        base = HALO - (CONV_K - 1)
        for k in range(CONV_K):
            dwb_ref[k:k + 1, :] += jnp.sum(dpre * ext[base + k:base + k + tr, :], axis=0, keepdims=True)
        dwb_ref[CONV_K:CONV_K + 1, :] += jnp.sum(dpre, axis=0, keepdims=True)

    main, halo = _conv_specs(s_, c, tr, False)
    return pl.pallas_call(
        body, name=name, grid=(s_ // tr,),
        in_specs=[main, halo, pl.BlockSpec((CONV_K, c), lambda i: (0, 0)), pl.BlockSpec((1, c), lambda i: (0, 0)), main],
        out_specs=[main, pl.BlockSpec((8, c), lambda i: (0, 0))],
        out_shape=[jax.ShapeDtypeStruct((s_, c), F32), jax.ShapeDtypeStruct((8, c), F32)],
        scratch_shapes=[pltpu.VMEM((tr + HALO, c), F32)], compiler_params=_params("arbitrary"),
    )(t, t, w, b, dact)


def _conv_bwd_in(dpre, w, name="conv_bwd_in"):
    s_, c = dpre.shape
    tr = _row_tile(s_)
    nt = s_ // tr

    def body(d_ref, h_ref, w_ref, o_ref, ext):
        ext[0:tr, :] = d_ref[...]
        ext[tr:tr + HALO, :] = jnp.where(pl.program_id(0) < nt - 1, h_ref[...], 0.0)
        acc = ext[CONV_K - 1:CONV_K - 1 + tr, :] * w_ref[0:1, :]
        for k in range(1, CONV_K):
            acc = acc + ext[CONV_K - 1 - k:CONV_K - 1 - k + tr, :] * w_ref[k:k + 1, :]
        o_ref[...] = acc.astype(o_ref.dtype)

    main, halo = _conv_specs(s_, c, tr, True)
    return pl.pallas_call(
        body, name=name, grid=(nt,),
        in_specs=[main, halo, pl.BlockSpec((CONV_K, c), lambda i: (0, 0))],
        out_specs=main, out_shape=jax.ShapeDtypeStruct((s_, c), MXU_DTYPE),
        scratch_shapes=[pltpu.VMEM((tr + HALO, c), F32)], compiler_params=_params("parallel"),
    )(dpre, dpre, w)


def _ssd_chunk_common(dt_ref, dtt_ref, br_ref, bc_ref, ar_ref, ac_ref):
    li = lax.broadcasted_iota(jnp.int32, (CHUNK, CHUNK), 0)
    si = lax.broadcasted_iota(jnp.int32, (CHUNK, CHUNK), 1)
    lower = li >= si
    lower_b = lower.astype(BF16)
    upper_b = (li <= si).astype(BF16)
    zr = dt_ref[...] + br_ref[...]
    dtc = _softplus(zr)
    a_row = -jnp.exp(ar_ref[...])
    acum = _exact_dot(lower_b, dtc * a_row, 1, 0, False)
    dtt = _softplus(dtt_ref[...] + bc_ref[...])
    acum_t = _exact_dot(dtt * (-jnp.exp(ac_ref[...])), upper_b, 1, 0, True)
    return lower, upper_b, zr, dtc, a_row, acum, acum_t


def _head_terms(h, lower, dtc, acum, acum_t):
    lane = lax.broadcasted_iota(jnp.int32, (1, LANE), 1)
    sub = lax.broadcasted_iota(jnp.int32, (SSD_H, 1), 0)
    rowid = lax.broadcasted_iota(jnp.int32, (CHUNK, 1), 0)
    oh = (lane == HEAD_LANE + h).astype(F32)
    acol = jnp.sum(acum * oh, axis=1, keepdims=True)
    dcol = jnp.sum(dtc * oh, axis=1, keepdims=True)
    arow = jnp.sum(acum_t * (sub == h).astype(F32), axis=0, keepdims=True)
    alast = jnp.sum(jnp.where(rowid == CHUNK - 1, acol, 0.0), axis=0, keepdims=True)
    decay = jnp.exp(jnp.where(lower, acol - arow, -jnp.inf))
    return oh, acol, dcol, alast, decay


SSD_PAIRS = SSD_H // 2
PAIRS_PER_GROUP = SSD_E // 2


def _ps(q):
    return slice(q * LANE, (q + 1) * LANE)


def _gs(off, g):
    return slice(off + g * SSD_N, off + (g + 1) * SSD_N)


def _lanes(c0, c1):
    return jnp.where(lax.broadcasted_iota(jnp.int32, (1, LANE), 1) < SSD_P, c0, c1)


def _rows(c0, c1):
    return jnp.where(lax.broadcasted_iota(jnp.int32, (LANE, 1), 0) < SSD_P, c0, c1)


def _lane_halves(t):
    first = lax.broadcasted_iota(jnp.int32, (1, LANE), 1) < SSD_P
    return (jnp.sum(jnp.where(first, t, 0.0), axis=1, keepdims=True),
            jnp.sum(jnp.where(first, 0.0, t), axis=1, keepdims=True))


def _ssd_in_specs(rev):
    def ci(c):
        return c if rev is None else rev - c
    return [pl.BlockSpec((CHUNK, CONV_DIM), lambda c: (ci(c), 0)),
            pl.BlockSpec((CHUNK, LANE), lambda c: (ci(c), 0)),
            pl.BlockSpec((SSD_H, CHUNK), lambda c: (0, ci(c))),
            pl.BlockSpec((1, LANE), lambda c: (0, 0)), pl.BlockSpec((SSD_H, 1), lambda c: (0, 0)),
            pl.BlockSpec((1, LANE), lambda c: (0, 0)), pl.BlockSpec((SSD_H, 1), lambda c: (0, 0)),
            pl.BlockSpec((SSD_PAIRS, 1, LANE), lambda c: (0, 0, 0))]


def _ssd_fwd(xbc, small, dtt, bias_r, bias_c, alog_r, alog_c, dsk, name="ssd_fwd"):
    s_ = xbc.shape[0]
    nc = s_ // CHUNK

    def body(x_ref, dt_ref, dtt_ref, br_ref, bc_ref, ar_ref, ac_ref, dsk_ref, y_ref, prev_ref, state):
        @pl.when(pl.program_id(0) == 0)
        def _():
            state[...] = jnp.zeros_like(state)

        lower, _, _, dtc, _, acum, acum_t = _ssd_chunk_common(dt_ref, dtt_ref, br_ref, bc_ref, ar_ref, ac_ref)
        for g in range(SSD_G):
            bb = _mx(x_ref[:, _gs(B_OFF, g)])
            cb_ = _mx(x_ref[:, _gs(C_OFF, g)])
            cbm = _dot(cb_, bb, 1, 1)
            for e in range(PAIRS_PER_GROUP):
                q = g * PAIRS_PER_GROUP + e
                _, acol0, dcol0, alast0, decay0 = _head_terms(2 * q, lower, dtc, acum, acum_t)
                _, acol1, dcol1, alast1, decay1 = _head_terms(2 * q + 1, lower, dtc, acum, acum_t)
                x = x_ref[:, _ps(q)]
                xdt = x * _lanes(dcol0, dcol1)
                xb = _mx(xdt)
                yd = _lanes(_dot(_mx(cbm * decay0), xb, 1, 0), _dot(_mx(cbm * decay1), xb, 1, 0))
                prev = state[q]
                prev_ref[0, q] = prev
                yo = _dot(cb_, _mx(prev), 1, 1) * _lanes(jnp.exp(acol0), jnp.exp(acol1))
                ds = _lanes(jnp.exp(alast0 - acol0), jnp.exp(alast1 - acol1))
                st = _dot(_mx(xdt * ds), bb, 0, 0)
                state[q] = prev * _rows(jnp.exp(alast0), jnp.exp(alast1)) + st
                y_ref[:, _ps(q)] = yd + yo + x * dsk_ref[q]

    psp = pl.BlockSpec((1, SSD_PAIRS, LANE, SSD_N), lambda c: (c, 0, 0, 0))
    return pl.pallas_call(
        body, name=name, grid=(nc,),
        in_specs=_ssd_in_specs(None), out_specs=[pl.BlockSpec((CHUNK, SSD_W), lambda c: (c, 0)), psp],
        out_shape=[jax.ShapeDtypeStruct((s_, SSD_W), F32),
                   jax.ShapeDtypeStruct((nc, SSD_PAIRS, LANE, SSD_N), F32)],
        scratch_shapes=[pltpu.VMEM((SSD_PAIRS, LANE, SSD_N), F32)],
        compiler_params=_params("arbitrary"),
    )(xbc, small, dtt, bias_r, bias_c, alog_r, alog_c, dsk)


def _ssd_bwd(xbc, small, dtt, bias_r, bias_c, alog_r, alog_c, dsk, prev, dy, name="ssd_bwd"):
    s_ = xbc.shape[0]
    nc = s_ // CHUNK

    def body(x_ref, dt_ref, dtt_ref, br_ref, bc_ref, ar_ref, ac_ref, dsk_ref, prev_ref, dy_ref,
             dx_ref, ddt_ref, dpar_ref, dstate):
        @pl.when(pl.program_id(0) == 0)
        def _():
            dstate[...] = jnp.zeros_like(dstate)
            dpar_ref[...] = jnp.zeros_like(dpar_ref)

        lower, upper_b, zr, dtc, a_row, acum, acum_t = _ssd_chunk_common(
            dt_ref, dtt_ref, br_ref, bc_ref, ar_ref, ac_ref)
        strict = (lax.broadcasted_iota(jnp.int32, (CHUNK, CHUNK), 1)
                  < lax.broadcasted_iota(jnp.int32, (CHUNK, CHUNK), 0))
        strict_b = strict.astype(BF16)
        col2 = lax.broadcasted_iota(jnp.int32, (CHUNK, 2 * CHUNK), 1)
        strict2 = (jnp.where(col2 >= CHUNK, col2 - CHUNK, col2)
                   < lax.broadcasted_iota(jnp.int32, (CHUNK, 2 * CHUNK), 0))
        da_in = jnp.zeros((CHUNK, LANE), F32)
        r_off = jnp.zeros((CHUNK, LANE), F32)
        c_int = jnp.zeros((CHUNK, LANE), F32)
        c_row = jnp.zeros((1, LANE), F32)
        ddt = jnp.zeros((CHUNK, LANE), F32)
        dskip = jnp.zeros((1, LANE), F32)
        for g in range(SSD_G):
            bb = _mx(x_ref[:, _gs(B_OFF, g)])
            cb_ = _mx(x_ref[:, _gs(C_OFF, g)])
            cbm = _dot(cb_, bb, 1, 1)
            dcb = jnp.zeros((CHUNK, CHUNK), F32)
            dc_acc = jnp.zeros((CHUNK, SSD_N), F32)
            db_acc = jnp.zeros((CHUNK, SSD_N), F32)
            for e in range(PAIRS_PER_GROUP):
                q = g * PAIRS_PER_GROUP + e
                oh0, acol0, dcol0, alast0, decay0 = _head_terms(2 * q, lower, dtc, acum, acum_t)
                oh1, acol1, dcol1, alast1, decay1 = _head_terms(2 * q + 1, lower, dtc, acum, acum_t)
                x = x_ref[:, _ps(q)]
                dy = dy_ref[:, _ps(q)]
                dcol = _lanes(dcol0, dcol1)
                xdt = x * dcol
                xb = _mx(xdt)
                eacol = _lanes(jnp.exp(acol0), jnp.exp(acol1))
                ds = _lanes(jnp.exp(alast0 - acol0), jnp.exp(alast1 - acol1))
                ealast = _rows(jnp.exp(alast0), jnp.exp(alast1))
                dyb = _mx(dy)
                dyb0, dyb1 = _mx(_lanes(dy, 0.0)), _mx(_lanes(0.0, dy))
                dsh = dstate[q]
                dshb = _mx(dsh)
                prev = prev_ref[0, q]
                prevb = _mx(prev)
                dxdt_inter = ds * _dot(bb, dshb, 1, 1)
                dxdt = _lanes(_dot(_mx(cbm * decay0), dyb, 0, 0), _dot(_mx(cbm * decay1), dyb, 0, 0)) + dxdt_inter
                dwl0 = _dot(dyb0, xb, 1, 1) * decay0
                dwl1 = _dot(dyb1, xb, 1, 1) * decay1
                dcb = dcb + dwl0 + dwl1
                dyeb = _mx(dy * eacol)
                dc_acc = dc_acc + _dot(dyeb, prevb, 1, 0)
                db_acc = db_acc + _dot(_mx(xdt * ds), dshb, 1, 0)
                dstate[q] = _dot(dyeb, cb_, 0, 0) + ealast * dsh
                above = _exact_dot(upper_b, jnp.concatenate([dwl0 * cbm, dwl1 * cbm], axis=1), 1, 0, False)
                above = jnp.where(strict2, above, 0.0)
                da_in = (da_in + jnp.sum(above[:, :CHUNK], axis=1, keepdims=True) * oh0
                         + jnp.sum(above[:, CHUNK:], axis=1, keepdims=True) * oh1)
                y_off = _dot(cb_, prevb, 1, 1) * eacol
                r0, r1 = _lane_halves(dy * y_off)
                r_off = r_off + r0 * oh0 + r1 * oh1
                c0, c1 = _lane_halves(xdt * dxdt_inter)
                c_int = c_int + c0 * oh0 + c1 * oh1
                both = jnp.sum(dsh * prev, axis=1, keepdims=True) * ealast
                c_row = (c_row + jnp.sum(_rows(both, 0.0), axis=0, keepdims=True) * oh0
                         + jnp.sum(_rows(0.0, both), axis=0, keepdims=True) * oh1)
                t0, t1 = _lane_halves(dxdt * x)
                ddt = ddt + t0 * oh0 + t1 * oh1
                dx_ref[:, _ps(q)] = dxdt * dcol + dy * dsk_ref[q]
                k0, k1 = _lane_halves(dy * x)
                dskip = (dskip + jnp.sum(k0, axis=0, keepdims=True) * oh0 + jnp.sum(k1, axis=0, keepdims=True) * oh1)
            dcbb = _mx(dcb)
            dx_ref[:, _gs(C_OFF, g)] = dc_acc + _dot(dcbb, bb, 1, 0)
            dx_ref[:, _gs(B_OFF, g)] = db_acc + _dot(dcbb, cb_, 0, 0)
        da = (da_in + _exact_dot(upper_b, r_off, 1, 0, False) + _exact_dot(strict_b, c_int, 1, 0, False) + c_row)
        draw = (ddt + da * a_row) * _sigmoid(zr)
        ddt_ref[...] = draw
        dpar_ref[0:1, :] += jnp.sum(draw, axis=0, keepdims=True)
        dpar_ref[1:2, :] += jnp.sum(da * dtc, axis=0, keepdims=True) * a_row
        dpar_ref[2:3, :] += dskip

    rev = nc - 1
    psp = pl.BlockSpec((1, SSD_PAIRS, LANE, SSD_N), lambda c: (rev - c, 0, 0, 0))
    return pl.pallas_call(
        body, name=name, grid=(nc,),
        in_specs=_ssd_in_specs(rev) + [psp, pl.BlockSpec((CHUNK, SSD_W), lambda c: (rev - c, 0))],
        out_specs=[pl.BlockSpec((CHUNK, CONV_DIM), lambda c: (rev - c, 0)),
                   pl.BlockSpec((CHUNK, LANE), lambda c: (rev - c, 0)), pl.BlockSpec((8, LANE), lambda c: (0, 0))],
        out_shape=[jax.ShapeDtypeStruct((s_, CONV_DIM), F32), jax.ShapeDtypeStruct((s_, LANE), F32),
                   jax.ShapeDtypeStruct((8, LANE), F32)],
        scratch_shapes=[pltpu.VMEM((SSD_PAIRS, LANE, SSD_N), F32)],
        compiler_params=_params("arbitrary"),
    )(xbc, small, dtt, bias_r, bias_c, alog_r, alog_c, dsk, prev, dy)


GN = SSD_W // SSD_G


def _gated_norm_fwd(y, z, w, cat, name="gated_norm_fwd"):
    s_, f = y.shape
    tr = _row_tile(s_)

    def body(y_ref, z_ref, w_ref, cat_ref, o_ref):
        for g in range(SSD_G):
            sl = slice(g * GN, (g + 1) * GN)
            gg = y_ref[:, sl] * _silu(z_ref[:, sl])
            r = lax.rsqrt(jnp.mean(gg * gg, axis=-1, keepdims=True) + EPS)
            o_ref[:, sl] = (gg * r * w_ref[:, sl]).astype(o_ref.dtype)

    row = pl.BlockSpec((tr, f), lambda i: (i, 0))
    wsp = pl.BlockSpec((1, f), lambda i: (0, 0))
    return pl.pallas_call(
        body, name=name, grid=(s_ // tr,),
        in_specs=[row, row, wsp, pl.BlockSpec(memory_space=pl.ANY)], out_specs=pl.BlockSpec((tr, f), lambda i: (i, 1)),
        out_shape=jax.ShapeDtypeStruct(cat.shape, cat.dtype), input_output_aliases={3: 0},
        compiler_params=_params("parallel"),
    )(y, z, w.reshape(1, f), cat)


def _gated_norm_bwd(y, z, w, dout, name="gated_norm_bwd"):
    s_, f = y.shape
    tr = _row_tile(s_)

    def body(y_ref, z_ref, w_ref, do_ref, dy_ref, dz_ref, dw_ref):
        @pl.when(pl.program_id(0) == 0)
        def _():
            dw_ref[...] = jnp.zeros_like(dw_ref)

        for g in range(SSD_G):
            sl = slice(g * GN, (g + 1) * GN)
            yv = y_ref[:, sl]
            zv = z_ref[:, sl]
            dov = do_ref[:, sl].astype(F32)
            sz = _silu(zv)
            gg = yv * sz
            r = lax.rsqrt(jnp.mean(gg * gg, axis=-1, keepdims=True) + EPS)
            gw = dov * w_ref[:, sl]
            c = jnp.mean(gw * gg, axis=-1, keepdims=True)
            dgg = r * gw - gg * (r * r * r * c)
            dy_ref[:, sl] = dgg * sz
            dz_ref[:, sl] = (dgg * yv * _dsilu(zv)).astype(dz_ref.dtype)
            dw_ref[:, sl] += jnp.sum(dov * gg * r, axis=0, keepdims=True)

    row = pl.BlockSpec((tr, f), lambda i: (i, 0))
    wsp = pl.BlockSpec((1, f), lambda i: (0, 0))
    return pl.pallas_call(
        body, name=name, grid=(s_ // tr,),
        in_specs=[row, row, wsp, pl.BlockSpec((tr, f), lambda i: (i, 1))], out_specs=[row, row, wsp],
        out_shape=[jax.ShapeDtypeStruct((s_, f), F32), jax.ShapeDtypeStruct((s_, f), MXU_DTYPE),
                   jax.ShapeDtypeStruct((1, f), F32)],
        compiler_params=_params("arbitrary"),
    )(y, z, w.reshape(1, f), dout)


def _ffn_fwd(vv, w_gate, w_up, name="ffn_gate_up"):
    s_, d = vv.shape
    nb, f8, _ = w_gate.shape
    tm = _pick(s_, (1024, 512, 256, 128))

    def body(v_ref, wg_ref, wu_ref, g_ref, u_ref, a_ref):
        for rs in _row_slices(tm, 16):
            a = _mx(v_ref[rs, :])
            g = _dot(a, _mx(wg_ref[...]), 1, 1)
            u = _dot(a, _mx(wu_ref[...]), 1, 1)
            s = _sigmoid(g)
            gs = g * s
            g_ref[rs, :] = (u * (s * (1.0 + g * (1.0 - s)))).astype(g_ref.dtype)
            u_ref[rs, :] = gs.astype(u_ref.dtype)
            a_ref[rs, :] = (gs * u).astype(a_ref.dtype)

    wsp = pl.BlockSpec((None, f8, d), lambda j, i: (j, 0, 0))
    osp = pl.BlockSpec((None, tm, f8), lambda j, i: (j, i, 0))
    return pl.pallas_call(
        body, name=name, grid=(nb, s_ // tm),
        in_specs=[pl.BlockSpec((tm, d), lambda j, i: (i, 0)), wsp, wsp], out_specs=[osp] * 3,
        out_shape=[jax.ShapeDtypeStruct((nb, s_, f8), MXU_DTYPE)] * 3,
        compiler_params=_params("parallel", "parallel"),
    )(vv, w_gate, w_up)


def _ffn_bwd_act(dffn, w_down, gate, up, name="ffn_d_act"):
    s_, d = dffn.shape
    nb, f8, _ = w_down.shape
    tm = _pick(s_, (1024, 512, 256, 128))

    def body(d_ref, w_ref, g_ref, u_ref, dg_ref, du_ref):
        for rs in _row_slices(tm, 16):
            dact = _dot(_mx(d_ref[rs, :]), _mx(w_ref[...]), 1, 1)
            dg_ref[rs, :] = (dact * g_ref[rs, :].astype(F32)).astype(dg_ref.dtype)
            du_ref[rs, :] = (dact * u_ref[rs, :].astype(F32)).astype(du_ref.dtype)

    osp = pl.BlockSpec((None, tm, f8), lambda i, j: (j, i, 0))
    return pl.pallas_call(
        body, name=name, grid=(s_ // tm, nb),
        in_specs=[pl.BlockSpec((tm, d), lambda i, j: (i, 0)), pl.BlockSpec((None, f8, d), lambda i, j: (j, 0, 0)),
                  osp, osp],
        out_specs=[osp, osp], out_shape=[jax.ShapeDtypeStruct((nb, s_, f8), MXU_DTYPE)] * 2,
        compiler_params=_params("parallel", "parallel"),
    )(dffn, w_down, gate, up)


def _ffn_bwd_in(dgate, w_gate, dup, w_up, name="ffn_d_in"):
    nb, s_, f8 = dgate.shape
    d = w_gate.shape[2]
    tm = _pick(s_, (1024, 512, 256, 128))
    tn = _pick(d, (1024, 512, 256, 128))
    per = 2
    steps = nb // per

    def body(*refs):
        ins, o_ref, acc = refs[:4 * per], refs[4 * per], refs[4 * per + 1]
        j = pl.program_id(2)

        @pl.when(j == 0)
        def _():
            acc[...] = jnp.zeros_like(acc)

        for rs in _row_slices(tm, 16):
            part = None
            for t in range(per):
                dg_ref, wg_ref, du_ref, wu_ref = ins[4 * t:4 * t + 4]
                d_ = (_dot(_mx(dg_ref[rs, :]), _mx(wg_ref[...]), 1, 0)
                      + _dot(_mx(du_ref[rs, :]), _mx(wu_ref[...]), 1, 0))
                part = d_ if part is None else part + d_
            acc[rs, :] += part

        @pl.when(j == steps - 1)
        def _():
            o_ref[...] = acc[...]

    def specs(t):
        asp = pl.BlockSpec((None, tm, f8), lambda i, n, j: (j * per + t, i, 0))
        wsp = pl.BlockSpec((None, f8, tn), lambda i, n, j: (j * per + t, 0, n))
        return [asp, wsp, asp, wsp]

    return pl.pallas_call(
        body, name=name, grid=(s_ // tm, d // tn, steps),
        in_specs=[sp for t in range(per) for sp in specs(t)],
        out_specs=pl.BlockSpec((tm, tn), lambda i, n, j: (i, n)),
        out_shape=jax.ShapeDtypeStruct((s_, d), F32), scratch_shapes=[pltpu.VMEM((tm, tn), F32)],
        compiler_params=_params("parallel", "parallel", "arbitrary"),
    )(*((dgate, w_gate, dup, w_up) * per))


def _adam_math(g, w, m, v):
    m2 = ADAM_B1 * m + (1.0 - ADAM_B1) * g
    v2 = ADAM_B2 * v + (1.0 - ADAM_B2) * (g * g)
    m_hat = m2 / (1.0 - ADAM_B1 ** ADAM_STEP)
    v_hat = v2 / (1.0 - ADAM_B2 ** ADAM_STEP)
    delta = -ADAM_LR * (m_hat / (jnp.sqrt(v_hat) + ADAM_EPS) + ADAM_WD * w)
    return delta, m2, v2


def _adamw(parts, own, me, w, m, v, name="adamw"):
    nd, r_, c = parts.shape
    tr = _pick(r_, (128, 64, 32, 16))
    tc = c
    if tr == r_ and r_ > 128:
        tc = _pick(c, (256, 128))

    def body(me_ref, p_ref, own_ref, w_ref, m_ref, v_ref, g_ref, d_ref, m2_ref, v2_ref):
        mine = me_ref[0]
        g = jnp.zeros((tr, tc), F32)
        for i in range(nd):
            g = g + jnp.where(mine == i, own_ref[...], p_ref[i]).astype(F32)
        delta, m2, v2 = _adam_math(g, w_ref[...], m_ref[...], v_ref[...])
        g_ref[...] = g
        d_ref[...] = delta
        m2_ref[...] = m2
        v2_ref[...] = v2

    row = pl.BlockSpec((tr, tc), lambda i, j, me_: (i, j))
    gs = pltpu.PrefetchScalarGridSpec(
        num_scalar_prefetch=1, grid=(r_ // tr, c // tc),
        in_specs=[pl.BlockSpec((nd, tr, tc), lambda i, j, me_: (0, i, j)),
                  pl.BlockSpec((None, tr, tc), lambda i, j, me_: (me_[0], i, j)), row, row, row],
        out_specs=[row] * 4)
    return pl.pallas_call(
        body, name=name, grid_spec=gs, out_shape=[jax.ShapeDtypeStruct((r_, c), F32)] * 4,
        compiler_params=_params("parallel", "parallel"),
    )(me, parts, own, w, m, v)


def _adamw_small(parts, w, m, v, name="adamw_small"):
    nd = parts.shape[0]

    def body(p_ref, w_ref, m_ref, v_ref, g_ref, d_ref, m2_ref, v2_ref):
        g = p_ref[0]
        for i in range(1, nd):
            g = g + p_ref[i]
        delta, m2, v2 = _adam_math(g, w_ref[...], m_ref[...], v_ref[...])
        g_ref[...] = g
        d_ref[...] = delta
        m2_ref[...] = m2
        v2_ref[...] = v2

    return pl.pallas_call(
        body, name=name, out_shape=[jax.ShapeDtypeStruct(w.shape, F32)] * 4,
        compiler_params=pltpu.CompilerParams(vmem_limit_bytes=VMEM_LIMIT_BYTES),
    )(parts, w, m, v)


_HBM = pl.BlockSpec(memory_space=pltpu.HBM)
_MESH = pl.DeviceIdType.MESH


def _all_gather(xs, name):
    na = len(xs)

    def body(*refs):
        x_refs, out_refs = refs[:na], refs[na:2 * na]
        send_sems, recv_sems, local_sems = refs[2 * na:]
        x, y, c = lax.axis_index("x"), lax.axis_index("y"), lax.axis_index("c")
        me, sibling = (x, y, c), (x, y, 1 - c)
        near = [(1 - x, y), (x, 1 - y)]
        chips = near + [(1 - x, 1 - y)]
        relay_from = (x + c * (1 - 2 * x), y + (1 - c) * (1 - 2 * y))
        relay_to = (x + (1 - c) * (1 - 2 * x), y + c * (1 - 2 * y))

        def slot(a, px, py, pc):
            return out_refs[a].at[4 * px + 2 * py + pc]

        def copy(a, k, block, to, src=None):
            return pltpu.make_async_remote_copy(
                src_ref=slot(a, *block) if src is None else src, dst_ref=slot(a, *block),
                send_sem=send_sems.at[a, k], recv_sem=recv_sems.at[a, k], device_id=to, device_id_type=_MESH)

        mine = [pltpu.make_async_copy(x_refs[a], slot(a, *me), local_sems.at[a]) for a in range(na)]
        started = []
        for a in range(na):
            mine[a].start()
            first = [copy(a, 0, me, sibling, src=x_refs[a])]
            first += [copy(a, 1 + j, me, (*chip, c), src=x_refs[a]) for j, chip in enumerate(near)]
            for cp in first:
                cp.start()
            started += first
        for a in range(na):
            for j, chip in enumerate(chips):
                copy(a, 1 + j, (*chip, c), me).wait_recv()
                fwd = copy(a, 4 + j, (*chip, c), sibling)
                fwd.start()
                started.append(fwd)
                if j == len(near) - 1:
                    relay = copy(a, 1 + len(near), (*relay_from, c), (*relay_to, c))
                    relay.start()
                    started.append(relay)
        for a in range(na):
            copy(a, 0, sibling, me).wait_recv()
            for j, chip in enumerate(chips):
                copy(a, 4 + j, (*chip, 1 - c), me).wait_recv()
        for cp in started:
            cp.wait_send()
        for cp in mine:
            cp.wait()

    return pl.pallas_call(
        body, name=name, out_shape=[jax.ShapeDtypeStruct((N_DEV,) + t.shape, t.dtype) for t in xs],
        in_specs=[_HBM] * na, out_specs=[_HBM] * na,
        scratch_shapes=[pltpu.SemaphoreType.DMA((na, 7)), pltpu.SemaphoreType.DMA((na, 7)),
                        pltpu.SemaphoreType.DMA((na,))],
    )(*xs)


_SEM = pl.BlockSpec(memory_space=pltpu.SEMAPHORE)
_EFFECT = pltpu.SideEffectType.DATAFLOW_SIDE_EFFECTING


def _peers(x, y, c):
    out = []
    for k in range(1, N_DEV):
        px = 1 - x if k & 4 else x
        py = 1 - y if k & 2 else y
        pc = 1 - c if k & 1 else c
        out.append(((px, py, pc), 4 * px + 2 * py + pc))
    return out


def _push_copies(scatter, src_refs, land_refs, send_sems, recv_sems):
    x, y, c = lax.axis_index("x"), lax.axis_index("y"), lax.axis_index("c")
    me = 4 * x + 2 * y + c
    pairs = []
    for a, (src, land) in enumerate(zip(src_refs, land_refs)):
        for k, (peer, slot) in enumerate(_peers(x, y, c)):
            out_src = src.at[slot] if scatter else src
            si = a * (N_DEV - 1) + k
            send = pltpu.make_async_remote_copy(src_ref=out_src, dst_ref=land.at[me], send_sem=send_sems.at[si],
                                                recv_sem=recv_sems.at[si], device_id=peer, device_id_type=_MESH)
            recv = pltpu.make_async_remote_copy(src_ref=out_src, dst_ref=land.at[slot], send_sem=send_sems.at[si],
                                                recv_sem=recv_sems.at[si], device_id=peer, device_id_type=_MESH)
            pairs.append((send, recv))
    return pairs


def _push_start(srcs, scatter, dep, name):
    na = len(srcs)
    shapes = [t.shape[1:] if scatter else t.shape for t in srcs]
    lands = [pltpu.with_memory_space_constraint(lax.empty((N_DEV,) + s, t.dtype), pltpu.HBM) for s, t in zip(shapes, srcs)]

    def body(*refs):
        src_refs, land_refs = refs[:na], refs[na:2 * na]
        send_sems, recv_sems = refs[2 * na + 1], refs[2 * na + 2]
        token = refs[-1]
        for send, _ in _push_copies(scatter, src_refs, land_refs, send_sems, recv_sems):
            send.start()
        token[...] = jnp.zeros_like(token)

    sem = pltpu.SemaphoreType.DMA((na * (N_DEV - 1),))
    outs = pl.pallas_call(
        body, name=name,
        out_shape=(sem, sem) + tuple(pltpu.HBM(t.shape, t.dtype) for t in srcs)
        + tuple(pltpu.HBM(t.shape, t.dtype) for t in lands) + (jax.ShapeDtypeStruct((8, LANE), F32),),
        in_specs=[_HBM] * (2 * na) + [pl.BlockSpec(memory_space=pl.ANY)],
        out_specs=(_SEM, _SEM) + (_HBM,) * (2 * na) + (pl.BlockSpec(memory_space=pltpu.VMEM),),
        input_output_aliases={i: 2 + i for i in range(2 * na)},
        compiler_params=pltpu.CompilerParams(has_side_effects=_EFFECT),
    )(*[pltpu.with_memory_space_constraint(t, pltpu.HBM) for t in srcs], *lands, dep)
    return outs[0], outs[1], outs[2:2 + na], outs[2 + na:2 + 2 * na], outs[-1]


def _push_wait(send_sems, recv_sems, src_thru, land_thru, scatter, after, name):
    na = len(src_thru)

    def body(*refs):
        src_refs, land_refs = refs[:na], refs[na:2 * na]
        ssem, rsem = refs[2 * na], refs[2 * na + 1]
        for send, recv in _push_copies(scatter, src_refs, land_refs, ssem, rsem):
            send.wait_send()
            recv.wait_recv()

    outs = pl.pallas_call(
        body, name=name,
        out_shape=tuple(pltpu.HBM(t.shape, t.dtype) for t in src_thru) + tuple(pltpu.HBM(t.shape, t.dtype) for t in land_thru),
        in_specs=[_HBM] * (2 * na) + [_SEM, _SEM, pl.BlockSpec(memory_space=pl.ANY)],
        out_specs=(_HBM,) * (2 * na),
        input_output_aliases={i: i for i in range(2 * na)},
        compiler_params=pltpu.CompilerParams(has_side_effects=_EFFECT),
    )(*src_thru, *land_thru, send_sems, recv_sems, after)
    return outs[:na], outs[na:]


def _exchange_behind(srcs, scatter, dep, name):
    send_sems, recv_sems, thru, lands, token = _push_start(srcs, scatter, dep, name + "_start")

    def finish(after, place=True):
        src_done, land_done = _push_wait(send_sems, recv_sems, thru, lands, scatter, after, name + "_wait")
        if not place:
            return land_done, src_done
        return _place_own(land_done, src_done, scatter, name + "_own")

    return token[0, 0], finish


def _place_own(lands, srcs, scatter, name):
    me = (4 * lax.axis_index("x") + 2 * lax.axis_index("y") + lax.axis_index("c")).astype(jnp.int32).reshape(1)
    outs = []
    for a, (land, src) in enumerate(zip(lands, srcs)):
        r_, c_ = land.shape[1:]
        tr = _pick(r_, (512, 256, 128, 64, 32, 16))

        def body(me_ref, land_ref, src_ref, out_ref):
            out_ref[...] = src_ref[...]

        src_spec = (pl.BlockSpec((None, tr, c_), lambda i, me_: (me_[0], i, 0)) if scatter
                    else pl.BlockSpec((tr, c_), lambda i, me_: (i, 0)))
        gs = pltpu.PrefetchScalarGridSpec(
            num_scalar_prefetch=1, grid=(r_ // tr,),
            in_specs=[pl.BlockSpec(memory_space=pl.ANY), src_spec],
            out_specs=pl.BlockSpec((None, tr, c_), lambda i, me_: (me_[0], i, 0)))
        outs.append(pl.pallas_call(
            body, name=f"{name}_{a}", grid_spec=gs, out_shape=jax.ShapeDtypeStruct(land.shape, land.dtype),
            input_output_aliases={1: 0}, compiler_params=_params("arbitrary"),
        )(me, land, src))
    return outs


_TRANSPOSED = ("w_in", "w_uq", "w_gate", "w_up")
_CQKV = (0, Q_RANK + KV_RANK)
_KR = (_CQKV[1], _CQKV[1] + ROPE)
_Z = (_KR[1], _KR[1] + SSD_W)
_XBC = (_Z[1], _Z[1] + CONV_DIM)
_DT = (_XBC[1], _XBC[1] + SSD_H)


def _win_segments(w_in_t):
    w = w_in_t.reshape(D_IN, D_MODEL)
    small = jnp.concatenate([w[_KR[0]:_KR[1]], w[_DT[0]:_DT[1]],
                             jnp.zeros((LANE - ROPE - SSD_H, D_MODEL), w.dtype)], axis=0)
    return w[_CQKV[0]:_CQKV[1]], w[_Z[0]:_Z[1]], w[_XBC[0]:_XBC[1]], small


def _win_from_segments(g_cqkv, g_z, g_xbc, g_small):
    w = jnp.concatenate([g_cqkv, g_small[:ROPE], g_z, g_xbc, g_small[ROPE:ROPE + SSD_H]], axis=0)
    return w.reshape(N_DEV, D_IN // N_DEV, D_MODEL)


_SMALL = (("q_norm_w", 512), ("kv_norm_w", 512), ("conv_b", CONV_DIM), ("dt_bias", SSD_H), ("a_log", SSD_H),
          ("d_skip", SSD_H), ("ssd_norm_w", SSD_W), ("attn_out_norm_w", 1024), ("pre_mix_norm_w", D_MODEL),
          ("post_mix_norm_w", D_MODEL), ("pre_ffn_norm_w", D_MODEL), ("post_ffn_norm_w", D_MODEL),
          ("conv_w", CONV_K * CONV_DIM))
_SMALL_ROWS = -(-sum(-(-n // LANE) for _, n in _SMALL) // 8) * 8


def _pack_small(vals):
    rows = []
    for name, n in _SMALL:
        v = vals[name].reshape(-1).astype(F32)
        pad = -(-n // LANE) * LANE
        rows.append(jnp.pad(v, (0, pad - n)).reshape(-1, LANE))
    m = jnp.concatenate(rows, axis=0)
    return jnp.pad(m, ((0, _SMALL_ROWS - m.shape[0]), (0, 0)))


def _unpack_small(m):
    out, r = {}, 0
    for name, n in _SMALL:
        nr = -(-n // LANE)
        out[name] = m[r:r + nr].reshape(-1)[:n]
        r += nr
    return out


def _head_row(v):
    return jnp.pad(v.reshape(1, -1).astype(F32), ((0, 0), (HEAD_LANE, LANE - HEAD_LANE - v.shape[-1])))


def _local_step(x, positions, target, wg, small, weights, on_grads):
    w_cqkv, w_z, w_xbc, w_small = _win_segments(wg["w_in"])
    conv_w = wg["conv_w"]
    conv_b = small["conv_b"].reshape(1, CONV_DIM)
    qkv_norm_w = jnp.concatenate([small["q_norm_w"], small["kv_norm_w"]])
    attn_norm_w = small["attn_out_norm_w"].reshape(1, HEADS * VDIM)
    scale = QK ** -0.5

    inv_freq = ROPE_THETA ** (-jnp.arange(0, ROPE, 2, dtype=F32) / ROPE)
    ang = positions.astype(F32)[:, None] * inv_freq
    cos2 = jnp.tile(jnp.cos(ang), (1, 2))
    sin2 = jnp.tile(jnp.sin(ang), (1, 2))

    u = _rms_fwd(x, small["pre_mix_norm_w"], out_dtype=MXU_DTYPE, name="pre_mix_norm")
    cqkv = _mm(u, w_cqkv, "nt", name="in_proj_qkv")
    z = _mm(u, w_z, "nt", name="in_proj_z")
    xbc = _mm(u, w_xbc, "nt", name="in_proj_xbc")
    sm = _mm(u, w_small, "nt", name="in_proj_small")

    w_uq, w_ukv = weights("qkv_up", cqkv)
    qkvn = _rms_fwd(cqkv, qkv_norm_w, groups=2, out_dtype=MXU_DTYPE, name="qkv_norm")
    q_h = _q_up(qkvn, w_uq, cos2, sin2, scale)
    k_h, v_h = _kv_up(qkvn, w_ukv, sm, cos2, sin2)
    o_h, lse = _flash_fwd(q_h, k_h, v_h)
    cat = _hnorm_fwd(o_h, attn_norm_w, D_MODEL)
    w_out = weights("out", o_h)[0].reshape(D_MODEL, D_MODEL)

    xbc_act = _conv_fwd(xbc, conv_w, conv_b)
    dtt = jnp.transpose(sm[:, HEAD_LANE:HEAD_LANE + SSD_H])
    ssd_args = (xbc_act, sm, dtt, _head_row(small["dt_bias"]), small["dt_bias"].reshape(SSD_H, 1),
                _head_row(small["a_log"]), small["a_log"].reshape(SSD_H, 1),
                jnp.broadcast_to(small["d_skip"].reshape(SSD_H, 1), (SSD_H, SSD_P)).reshape(SSD_PAIRS, 1, LANE))
    y_ssd, prev = _ssd_fwd(*ssd_args)
    cat = _gated_norm_fwd(y_ssd, z, small["ssd_norm_w"], cat)

    mix = _mm(cat, w_out, "nn", name="out_proj")
    h1, vv = _norm_res_norm(mix, x, small["post_mix_norm_w"], small["pre_ffn_norm_w"])

    w_gate, w_up = weights("ffn_in", mix)
    gate, up, act = _ffn_fwd(vv, w_gate, w_up)
    w_down, = weights("ffn_out", act)
    ffn = _mm(act, w_down, "nn", a_blk=True, b_blk=True, fuse=N_DEV, tm_max=512, name="ffn_down")
    loss_blk, dy, dffn, g_post_ffn = _loss_head(ffn, h1, target, small["post_ffn_norm_w"])

    g_down = _mm(act, dffn, "tn", a_blk=True, out_blk=True, out_dtype=MXU_DTYPE, name="g_down")
    dgate, dup = _ffn_bwd_act(dffn, w_down, gate, up)
    dvv = _ffn_bwd_in(dgate, w_gate, dup, w_up)
    g_gate = _mm(dgate, vv, "tn", a_blk=True, out_blk=True, out_dtype=MXU_DTYPE, name="g_gate")
    g_up = _mm(dup, vv, "tn", a_blk=True, out_blk=True, out_dtype=MXU_DTYPE, name="g_up")
    pre_ffn_w = small["pre_ffn_norm_w"] + on_grads("ffn", [g_gate, g_up, g_down])
    dh1, dmix, g_pre_ffn, g_post_mix = _norm_res_norm_bwd(h1, pre_ffn_w, dvv, dy, mix, small["post_mix_norm_w"])

    dcat = _mm(dmix, w_out, "nt", name="d_cat")
    g_out = _mm(cat, dmix, "tn", out_dtype=MXU_DTYPE, name="g_out")

    do_h, delta, g_attn_norm = _hnorm_bwd(o_h, attn_norm_w, dcat)
    dq_h, dk_h, dv_h = _flash_bwd(q_h, k_h, v_h, do_h, lse, delta)
    dq = _q_prep(dq_h, cos2, -sin2, scale, name="dq_post")

    dy_ssd, dz, g_ssd_norm = _gated_norm_bwd(y_ssd, z, small["ssd_norm_w"], dcat)
    dxbc_act, ddt, dpar = _ssd_bwd(*ssd_args, prev, dy_ssd)
    dkv, dsm = _dkv_post(dk_h, dv_h, ddt, cos2, -sin2)
    dpre, dwb = _conv_bwd_pre(xbc, conv_w, conv_b, dxbc_act)
    dxbc = _conv_bwd_in(dpre, conv_w)

    dqn = _mm(dq, w_uq, "nn", a_blk=True, b_blk=True, fuse=HEADS, name="d_qn")
    dkvn = _mm(dkv, w_ukv, "nt", a_blk=True, b_blk=True, fuse=HEADS, name="d_kvn")
    g_uq = _mm(dq, qkvn, "tn", a_blk=True, out_blk=True, b_cols=(0, Q_RANK), out_dtype=MXU_DTYPE, name="g_uq")
    g_ukv = _mm(qkvn, dkv, "tn", b_blk=True, out_blk=True, a_cols=(Q_RANK, KV_RANK), out_dtype=MXU_DTYPE, name="g_ukv")
    heads_token = on_grads("heads", [g_uq, g_ukv, g_out.reshape(N_DEV, D_MODEL // N_DEV, D_MODEL)])
    dcqkv, g_qkv_norm = _rms_bwd(cqkv, qkv_norm_w + heads_token, [dqn, dkvn], out_dtype=MXU_DTYPE, name="qkv_norm_bwd")

    g_in = _win_from_segments(_mm(dcqkv, u, "tn", out_dtype=MXU_DTYPE, name="g_in_qkv"),
                              _mm(dz, u, "tn", out_dtype=MXU_DTYPE, name="g_in_z"),
                              _mm(dxbc, u, "tn", out_dtype=MXU_DTYPE, name="g_in_xbc"),
                              _mm(dsm, u, "tn", out_dtype=MXU_DTYPE, name="g_in_small"))
    in_token = on_grads("in", [g_in])
    du = _mm_sum([dsm + in_token.astype(dsm.dtype), dcqkv, dz, dxbc], [w_small, w_cqkv, w_z, w_xbc], name="d_u")
    dx, g_pre_mix = _rms_bwd(x, small["pre_mix_norm_w"], [du], res=dh1, name="pre_mix_norm_bwd")

    hl = slice(HEAD_LANE, HEAD_LANE + SSD_H)
    g_small = {"q_norm_w": g_qkv_norm[0, :Q_RANK], "kv_norm_w": g_qkv_norm[0, Q_RANK:], "conv_b": dwb[CONV_K],
               "dt_bias": dpar[0, hl], "a_log": dpar[1, hl], "d_skip": dpar[2, hl], "ssd_norm_w": g_ssd_norm,
               "attn_out_norm_w": g_attn_norm, "pre_mix_norm_w": g_pre_mix, "post_mix_norm_w": g_post_mix,
               "pre_ffn_norm_w": g_pre_ffn, "post_ffn_norm_w": g_post_ffn, "conv_w": dwb[:CONV_K]}
    return loss_blk[0, 0], dx, g_small


_WEIGHT_ORDER = ("w_in", "q_norm_w", "w_uq", "kv_norm_w", "w_ukv", "conv_w", "conv_b", "dt_bias", "a_log", "d_skip",
                 "ssd_norm_w", "attn_out_norm_w", "w_out", "pre_mix_norm_w", "post_mix_norm_w", "pre_ffn_norm_w",
                 "post_ffn_norm_w", "w_gate", "w_up", "w_down")


def kernel(x, positions, w_in, q_norm_w, w_uq, kv_norm_w, w_ukv, conv_w, conv_b, dt_bias, a_log, d_skip, ssd_norm_w, attn_out_norm_w, w_out, pre_mix_norm_w, post_mix_norm_w, pre_ffn_norm_w, post_ffn_norm_w, w_gate, w_up, w_down, loss_target, m_w_in, m_q_norm_w, m_w_uq, m_kv_norm_w, m_w_ukv, m_conv_w, m_conv_b, m_dt_bias, m_a_log, m_d_skip, m_ssd_norm_w, m_attn_out_norm_w, m_w_out, m_pre_mix_norm_w, m_post_mix_norm_w, m_pre_ffn_norm_w, m_post_ffn_norm_w, m_w_gate, m_w_up, m_w_down, v_w_in, v_q_norm_w, v_w_uq, v_kv_norm_w, v_w_ukv, v_conv_w, v_conv_b, v_dt_bias, v_a_log, v_d_skip, v_ssd_norm_w, v_attn_out_norm_w, v_w_out, v_pre_mix_norm_w, v_post_mix_norm_w, v_pre_ffn_norm_w, v_post_ffn_norm_w, v_w_gate, v_w_up, v_w_down):
    w = dict(w_in=w_in, q_norm_w=q_norm_w, w_uq=w_uq, kv_norm_w=kv_norm_w, w_ukv=w_ukv, conv_w=conv_w, conv_b=conv_b,
             dt_bias=dt_bias, a_log=a_log, d_skip=d_skip, ssd_norm_w=ssd_norm_w, attn_out_norm_w=attn_out_norm_w,
             w_out=w_out, pre_mix_norm_w=pre_mix_norm_w, post_mix_norm_w=post_mix_norm_w,
             pre_ffn_norm_w=pre_ffn_norm_w, post_ffn_norm_w=post_ffn_norm_w, w_gate=w_gate, w_up=w_up, w_down=w_down)
    m = dict(w_in=m_w_in, q_norm_w=m_q_norm_w, w_uq=m_w_uq, kv_norm_w=m_kv_norm_w, w_ukv=m_w_ukv, conv_w=m_conv_w,
             conv_b=m_conv_b, dt_bias=m_dt_bias, a_log=m_a_log, d_skip=m_d_skip, ssd_norm_w=m_ssd_norm_w,
             attn_out_norm_w=m_attn_out_norm_w, w_out=m_w_out, pre_mix_norm_w=m_pre_mix_norm_w,
             post_mix_norm_w=m_post_mix_norm_w, pre_ffn_norm_w=m_pre_ffn_norm_w, post_ffn_norm_w=m_post_ffn_norm_w,
             w_gate=m_w_gate, w_up=m_w_up, w_down=m_w_down)
    v = dict(w_in=v_w_in, q_norm_w=v_q_norm_w, w_uq=v_w_uq, kv_norm_w=v_kv_norm_w, w_ukv=v_w_ukv, conv_w=v_conv_w,
             conv_b=v_conv_b, dt_bias=v_dt_bias, a_log=v_a_log, d_skip=v_d_skip, ssd_norm_w=v_ssd_norm_w,
             attn_out_norm_w=v_attn_out_norm_w, w_out=v_w_out, pre_mix_norm_w=v_pre_mix_norm_w,
             post_mix_norm_w=v_post_mix_norm_w, pre_ffn_norm_w=v_pre_ffn_norm_w, post_ffn_norm_w=v_post_ffn_norm_w,
             w_gate=v_w_gate, w_up=v_w_up, w_down=v_w_down)
    w, m, v = ({k: t[0] for k, t in d.items()} for d in (w, m, v))
    me = 4 * lax.axis_index("x") + 2 * lax.axis_index("y") + lax.axis_index("c")
    groups = {"qkv_up": ("w_uq", "w_ukv"), "out": ("w_out",), "ffn_in": ("w_gate", "w_up"), "ffn_out": ("w_down",)}
    cshard = CONV_DIM // N_DEV
    for name in _TRANSPOSED:
        w[name], m[name], v[name] = w[name].T, m[name].T, v[name].T

    shards = [w["w_in"].astype(MXU_DTYPE),
              jnp.stack(_split3(w["conv_w"])).reshape(3 * CONV_K, cshard).astype(MXU_DTYPE)]
    w_in_g, cw = _all_gather(shards, name="gather_weights")
    cw = cw.astype(F32).reshape(N_DEV, 3, CONV_K, cshard)
    wg = {"w_in": w_in_g, "conv_w": jnp.transpose(cw[:, 0] + cw[:, 1] + cw[:, 2], (1, 0, 2)).reshape(CONV_K, CONV_DIM)}
    arriving, dep, started = {}, wg["conv_w"], jnp.zeros((), F32)
    small = {name: w[name] for name, _ in _SMALL if name != "conv_w"}
    for group in ("qkv_up", "out", "ffn_in", "ffn_out"):
        token, arriving[group] = _exchange_behind([w[name].astype(MXU_DTYPE) for name in groups[group]], False,
                                                  dep, group + "_weights")
        started = started + token
        dep = jnp.zeros((8, LANE), F32) + started
    small["pre_mix_norm_w"] = small["pre_mix_norm_w"] + started

    leaving = {}

    def on_grads(group, gs):
        token, leaving[group] = _exchange_behind(gs, True, jnp.zeros((8, LANE), F32), group + "_grads")
        return token

    loss_local, dx, g_small = _local_step(x[0], positions[0], loss_target[0], wg, small,
                                          lambda group, after: arriving[group](after), on_grads)
    loss = lax.psum(loss_local, ("x", "y", "c"))

    recv = {}
    for group, names in (("ffn", ("w_gate", "w_up", "w_down")), ("heads", ("w_uq", "w_ukv", "w_out")), ("in", ("w_in",))):
        recv.update(zip(names, zip(*leaving[group](dx, place=False))))
    grads, deltas, new_m, new_v = {}, {}, {}, {}
    me1 = me.astype(jnp.int32).reshape(1)
    for name, (parts, own) in recv.items():
        outs = _adamw(parts, own, me1, w[name], m[name], v[name], name="adamw_" + name)
        if name in _TRANSPOSED:
            outs = [t.T for t in outs]
        grads[name], deltas[name], new_m[name], new_v[name] = outs

    def embed(t):
        return lax.dynamic_update_slice(jnp.zeros((CONV_K, CONV_DIM), F32), t, (0, me * cshard))

    parts_s = _all_gather([_pack_small(g_small)], name="gather_small_grads")[0]
    packs = [_pack_small({**{n_: d[n_] for n_, _ in _SMALL if n_ != "conv_w"}, "conv_w": embed(d["conv_w"])})
             for d in (w, m, v)]
    outs = [_unpack_small(t) for t in _adamw_small(parts_s, *packs)]
    for name, n in _SMALL:
        for dst, src in zip((grads, deltas, new_m, new_v), outs):
            if name == "conv_w":
                dst[name] = lax.dynamic_slice(src[name].reshape(CONV_K, CONV_DIM), (0, me * cshard), (CONV_K, cshard))
            else:
                dst[name] = src[name]

    def lead(d):
        return [d[name][None] for name in _WEIGHT_ORDER]

    return (loss, dx[None], *lead(grads), *lead(deltas), *lead(new_m), *lead(new_v))
```

```python
import numpy as np

import jax
import jax.numpy as jnp
from jax import lax
from jax.experimental import pallas as pl
from jax.experimental.pallas import tpu as pltpu

F32 = jnp.float32
BF16 = jnp.bfloat16
MXU_DTYPE = jnp.bfloat16
EPS = 1e-6
VMEM_LIMIT_BYTES = 48 * 1024 * 1024
K_TILE_MAX = 2048

N_DEV = 8
D_MODEL = 2048
Q_RANK = 512
KV_RANK = 512
ROPE = 64
HALF = ROPE // 2
HEADS = 8
NOPE = 128
VDIM = 128
QK = NOPE + ROPE
SSD_W = 1024
SSD_H = 16
SSD_P = 64
SSD_G = 2
SSD_E = SSD_H // SSD_G
SSD_N = 128
CHUNK = 128
CONV_K = 4
CONV_DIM = SSD_W + 2 * SSD_G * SSD_N
B_OFF = SSD_W
C_OFF = SSD_W + SSD_G * SSD_N
D_FF = 5632
D_IN = Q_RANK + KV_RANK + ROPE + SSD_W + CONV_DIM + SSD_H
ROPE_THETA = 10000.0
LANE = 128
HEAD_LANE = ROPE

ADAM_LR = 0.001
ADAM_B1 = 0.9
ADAM_B2 = 0.999
ADAM_EPS = 1e-08
ADAM_WD = 0.01
ADAM_STEP = 10


def _pick(n, cands):
    for c in cands:
        if n % c == 0:
            return c
    return n


def _params(*sem):
    return pltpu.CompilerParams(dimension_semantics=sem, vmem_limit_bytes=VMEM_LIMIT_BYTES)


def _sigmoid(x):
    return 1.0 / (1.0 + jnp.exp(-x))


def _silu(x):
    return x * _sigmoid(x)


def _dsilu(x):
    s = _sigmoid(x)
    return s * (1.0 + x * (1.0 - s))


def _softplus(x):
    e = jnp.exp(-jnp.abs(x))
    small = e * (1.0 - e * (0.5 - e * (1.0 / 3.0)))
    return jnp.maximum(x, 0.0) + jnp.where(e < 0.01, small, jnp.log(1.0 + e))


def _dot(a, b, ca, cb):
    return lax.dot_general(a, b, (((ca,), (cb,)), ((), ())), preferred_element_type=F32)


def _mx(v):
    return v.astype(MXU_DTYPE)


def _split3(a):
    hi = a.astype(BF16)
    r1 = a - hi.astype(F32)
    mid = r1.astype(BF16)
    lo = (r1 - mid.astype(F32)).astype(BF16)
    return hi, mid, lo


def _exact_dot(a, b, ca, cb, split_a):
    if split_a:
        return sum(_dot(p, b, ca, cb) for p in _split3(a))
    return sum(_dot(a, p, ca, cb) for p in _split3(b))


MM_ROW_GROUPS = 4


def _row_slices(tm, align):
    ng = MM_ROW_GROUPS
    while ng > 1 and (tm % ng or (tm // ng) % align):
        ng //= 2
    return [slice(g * (tm // ng), (g + 1) * (tm // ng)) for g in range(ng)]


def _mm(a, b, mode, *, a_blk=False, b_blk=False, out_blk=False, a_cols=None, b_cols=None, add=None, out_dtype=F32,
        fuse=1, wide=False, tm_max=1024, name="mm"):
    a2, b2 = a.shape[-2:], b.shape[-2:]
    a_last = a2[1] if a_cols is None else a_cols[1]
    a_start = 0 if a_cols is None else a_cols[0]
    b_start = 0
    if b_cols is not None:
        assert mode != "nt"
        b_start, b2 = b_cols[0], (b2[0], b_cols[1])
    if mode == "nn":
        m, k, (k2, n) = a2[0], a_last, b2
    elif mode == "nt":
        m, k, (n, k2) = a2[0], a_last, b2
    else:
        k, m, (k2, n) = a2[0], a_last, b2
    assert k == k2, (a.shape, b.shape, mode)
    tm = _pick(m, tuple(c for c in (1024, 704, 512, 256, 128) if c <= tm_max))
    tn = _pick(n, ((2048,) if wide else ()) + (1024, 768, 704, 512, 256, 192, 128))
    k_max = 2 * K_TILE_MAX if mode == "tn" else K_TILE_MAX
    tk = k if k <= k_max else _pick(k, (K_TILE_MAX, 1024, 512))
    nk = k // tk
    jo = N_DEV if out_blk else 1
    reduce_blocks = a_blk and b_blk and not out_blk
    assert fuse == 1 or reduce_blocks
    jr = N_DEV // fuse if reduce_blocks else 1
    ca, cb = {"nn": (1, 0), "nt": (1, 1), "tn": (0, 0)}[mode]
    has_add = add is not None
    single = jr * nk == 1
    if mode == "tn":
        assert a_start % tm == 0
        a_block, a_idx = (tk, tm), (lambda i, kk: (kk, i + a_start // tm))
    else:
        assert a_start % tk == 0
        a_block, a_idx = (tm, tk), (lambda i, kk: (i, kk + a_start // tk))
    assert b_start % tn == 0
    b_block, b_idx = (((tn, tk), (lambda nn_, kk: (nn_, kk))) if mode == "nt"
                      else ((tk, tn), (lambda nn_, kk: (kk, nn_ + b_start // tn))))

    def blk_specs(blocked, block, idx, of_a, t):
        def pos(o, i, nn_, kk):
            return idx(i, kk) if of_a else idx(nn_, kk)
        if blocked:
            return pl.BlockSpec((None,) + block,
                                lambda o, i, nn_, r, kk: ((o if out_blk else r * fuse + t),) + pos(o, i, nn_, kk))
        return pl.BlockSpec(block, lambda o, i, nn_, r, kk: pos(o, i, nn_, kk))

    a_specs = [blk_specs(a_blk, a_block, a_idx, True, t) for t in range(fuse)]
    b_specs = [blk_specs(b_blk, b_block, b_idx, False, t) for t in range(fuse)]
    o_spec = (pl.BlockSpec((None, tm, tn), lambda o, i, nn_, r, kk: (o, i, nn_)) if out_blk
              else pl.BlockSpec((tm, tn), lambda o, i, nn_, r, kk: (i, nn_)))

    groups = _row_slices(tm, LANE if mode == "tn" else 16)

    def body(*refs):
        a_refs, b_refs = refs[:fuse], refs[fuse:2 * fuse]
        add_ref = refs[2 * fuse] if has_add else None
        o_ref = refs[2 * fuse + 1] if has_add else refs[2 * fuse]

        def partial(rs):
            out = None
            for t in range(fuse):
                av = a_refs[t][:, rs] if mode == "tn" else a_refs[t][rs, :]
                d = _dot(_mx(av), _mx(b_refs[t][...]), ca, cb)
                out = d if out is None else out + d
            return out

        if single:
            for rs in groups:
                res = partial(rs)
                if has_add:
                    res = res + add_ref[rs, :]
                o_ref[rs, :] = res.astype(o_ref.dtype)
            return
        acc = refs[-1]
        r, kk = pl.program_id(3), pl.program_id(4)

        @pl.when(jnp.logical_and(r == 0, kk == 0))
        def _():
            acc[...] = jnp.zeros_like(acc)

        for rs in groups:
            acc[rs, :] += partial(rs)

        @pl.when(jnp.logical_and(r == jr - 1, kk == nk - 1))
        def _():
            res = acc[...]
            if has_add:
                res = res + add_ref[...]
            o_ref[...] = res.astype(o_ref.dtype)

    out_shape = ((N_DEV, m, n) if out_blk else (m, n))
    return pl.pallas_call(
        body, name=name, grid=(jo, m // tm, n // tn, jr, nk),
        in_specs=a_specs + b_specs + ([o_spec] if has_add else []), out_specs=o_spec,
        out_shape=jax.ShapeDtypeStruct(out_shape, out_dtype),
        scratch_shapes=[] if single else [pltpu.VMEM((tm, tn), F32)],
        compiler_params=_params("parallel", "parallel", "parallel", "arbitrary", "arbitrary"),
    )(*((a,) * fuse + (b,) * fuse + ((add,) if has_add else ())))


def _mm_sum(a_list, b_list, name="mm_sum"):
    m, n = a_list[0].shape[0], b_list[0].shape[1]
    ns = len(a_list)
    tm = _pick(m, (1024, 512, 256, 128))
    tn = _pick(n, (1024, 512, 256, 128))
    groups = _row_slices(tm, 16)

    def body(*refs):
        a_refs, b_refs, o_ref = refs[:ns], refs[ns:2 * ns], refs[2 * ns]
        for rs in groups:
            acc = _dot(_mx(a_refs[0][rs, :]), _mx(b_refs[0][...]), 1, 0)
            for s in range(1, ns):
                acc = acc + _dot(_mx(a_refs[s][rs, :]), _mx(b_refs[s][...]), 1, 0)
            o_ref[rs, :] = acc

    return pl.pallas_call(
        body, name=name, grid=(m // tm, n // tn),
        in_specs=([pl.BlockSpec((tm, a.shape[1]), lambda i, j: (i, 0)) for a in a_list]
                  + [pl.BlockSpec((b.shape[0], tn), lambda i, j: (0, j)) for b in b_list]),
        out_specs=pl.BlockSpec((tm, tn), lambda i, j: (i, j)),
        out_shape=jax.ShapeDtypeStruct((m, n), F32), compiler_params=_params("parallel", "parallel"),
    )(*a_list, *b_list)


def _row_tile(r_, streams=4):
    return _pick(r_, ((512,) if streams <= 4 else ()) + (256, 128, 64, 32, 16, 8))


def _rms_fwd(t, w, groups=1, res=None, out_dtype=F32, name="rms_fwd"):
    r_, f = t.shape
    fg = f // groups
    tr = _row_tile(r_)
    has_res = res is not None

    def body(*refs):
        t_ref, w_ref = refs[0], refs[1]
        res_ref = refs[2] if has_res else None
        o_ref = refs[-1]
        for g in range(groups):
            sl = slice(g * fg, (g + 1) * fg)
            tv = t_ref[:, sl].astype(F32)
            r = lax.rsqrt(jnp.mean(tv * tv, axis=-1, keepdims=True) + EPS)
            y = tv * r * w_ref[:, sl]
            if has_res:
                y = y + res_ref[:, sl]
            o_ref[:, sl] = y.astype(o_ref.dtype)

    row = pl.BlockSpec((tr, f), lambda i: (i, 0))
    wsp = pl.BlockSpec((1, f), lambda i: (0, 0))
    return pl.pallas_call(
        body, name=name, grid=(r_ // tr,),
        in_specs=[row, wsp] + ([row] if has_res else []), out_specs=row,
        out_shape=jax.ShapeDtypeStruct((r_, f), out_dtype),
        compiler_params=_params("parallel"),
    )(*((t, w.reshape(1, f)) + ((res,) if has_res else ())))


def _rms_bwd(t, w, dys, res=None, out_dtype=F32, name="rms_bwd"):
    r_, f = t.shape
    groups = len(dys)
    fg = f // groups
    tr = _row_tile(r_)
    has_res = res is not None

    def body(*refs):
        t_ref, w_ref = refs[0], refs[1]
        dy_refs = refs[2:2 + groups]
        res_ref = refs[2 + groups] if has_res else None
        dt_ref, dw_ref = refs[-2], refs[-1]

        @pl.when(pl.program_id(0) == 0)
        def _():
            dw_ref[...] = jnp.zeros_like(dw_ref)

        for g in range(groups):
            sl = slice(g * fg, (g + 1) * fg)
            tv = t_ref[:, sl].astype(F32)
            dyv = dy_refs[g][...].astype(F32)
            r = lax.rsqrt(jnp.mean(tv * tv, axis=-1, keepdims=True) + EPS)
            gw = dyv * w_ref[:, sl]
            c = jnp.mean(gw * tv, axis=-1, keepdims=True)
            dt = r * gw - tv * (r * r * r * c)
            if has_res:
                dt = dt + res_ref[:, sl]
            dt_ref[:, sl] = dt.astype(dt_ref.dtype)
            dw_ref[:, sl] += jnp.sum(dyv * tv * r, axis=0, keepdims=True)

    row = pl.BlockSpec((tr, f), lambda i: (i, 0))
    grow = pl.BlockSpec((tr, fg), lambda i: (i, 0))
    wsp = pl.BlockSpec((1, f), lambda i: (0, 0))
    return pl.pallas_call(
        body, name=name, grid=(r_ // tr,),
        in_specs=[row, wsp] + [grow] * groups + ([row] if has_res else []), out_specs=[row, wsp],
        out_shape=[jax.ShapeDtypeStruct((r_, f), out_dtype), jax.ShapeDtypeStruct((1, f), F32)],
        compiler_params=_params("arbitrary"),
    )(*((t, w.reshape(1, f)) + tuple(dys) + ((res,) if has_res else ())))


def _norm_res_norm(t, res, w1, w2, name="post_mix_pre_ffn_norm"):
    r_, f = t.shape
    tr = _row_tile(r_)

    def body(t_ref, res_ref, w1_ref, w2_ref, h_ref, v_ref):
        tv = t_ref[...]
        h = res_ref[...] + tv * lax.rsqrt(jnp.mean(tv * tv, axis=-1, keepdims=True) + EPS) * w1_ref[...]
        h_ref[...] = h
        v_ref[...] = (h * lax.rsqrt(jnp.mean(h * h, axis=-1, keepdims=True) + EPS) * w2_ref[...]).astype(v_ref.dtype)

    row = pl.BlockSpec((tr, f), lambda i: (i, 0))
    wsp = pl.BlockSpec((1, f), lambda i: (0, 0))
    return pl.pallas_call(
        body, name=name, grid=(r_ // tr,), in_specs=[row, row, wsp, wsp], out_specs=[row, row],
        out_shape=[jax.ShapeDtypeStruct((r_, f), F32), jax.ShapeDtypeStruct((r_, f), MXU_DTYPE)],
        compiler_params=_params("parallel"),
    )(t, res, w1.reshape(1, f), w2.reshape(1, f))


def _norm_res_norm_bwd(h, w2, dv, dres, t, w1, name="pre_ffn_post_mix_norm_bwd"):
    r_, f = h.shape
    tr = _row_tile(r_, streams=6)

    def body(h_ref, w2_ref, dv_ref, dres_ref, t_ref, w1_ref, dh_ref, dt_ref, dw2_ref, dw1_ref):
        @pl.when(pl.program_id(0) == 0)
        def _():
            dw2_ref[...] = jnp.zeros_like(dw2_ref)
            dw1_ref[...] = jnp.zeros_like(dw1_ref)

        def rms_bwd(tv, wv, dyv):
            r = lax.rsqrt(jnp.mean(tv * tv, axis=-1, keepdims=True) + EPS)
            gw = dyv * wv
            c = jnp.mean(gw * tv, axis=-1, keepdims=True)
            return r * gw - tv * (r * r * r * c), jnp.sum(dyv * tv * r, axis=0, keepdims=True)

        d1, g2 = rms_bwd(h_ref[...], w2_ref[...], dv_ref[...])
        dh = d1 + dres_ref[...]
        dh_ref[...] = dh
        dw2_ref[...] += g2
        d2, g1 = rms_bwd(t_ref[...], w1_ref[...], dh)
        dt_ref[...] = d2.astype(dt_ref.dtype)
        dw1_ref[...] += g1

    row = pl.BlockSpec((tr, f), lambda i: (i, 0))
    wsp = pl.BlockSpec((1, f), lambda i: (0, 0))
    return pl.pallas_call(
        body, name=name, grid=(r_ // tr,), in_specs=[row, wsp, row, row, row, wsp], out_specs=[row, row, wsp, wsp],
        out_shape=[jax.ShapeDtypeStruct((r_, f), F32), jax.ShapeDtypeStruct((r_, f), MXU_DTYPE),
                   jax.ShapeDtypeStruct((1, f), F32), jax.ShapeDtypeStruct((1, f), F32)],
        compiler_params=_params("arbitrary"),
    )(h, w2.reshape(1, f), dv, dres, t, w1.reshape(1, f))


def _hnorm_fwd(o, w, width, name="attn_out_norm"):
    h, s_, v = o.shape
    tr = _row_tile(s_)

    def body(o_ref, w_ref, y_ref):
        ss = jnp.sum(o_ref[0] * o_ref[0], axis=-1, keepdims=True)
        for i in range(1, h):
            ss = ss + jnp.sum(o_ref[i] * o_ref[i], axis=-1, keepdims=True)
        r = lax.rsqrt(ss * (1.0 / (h * v)) + EPS)
        for i in range(h):
            sl = slice(i * v, (i + 1) * v)
            y_ref[:, sl] = (o_ref[i] * r * w_ref[:, sl]).astype(y_ref.dtype)

    return pl.pallas_call(
        body, name=name, grid=(s_ // tr,),
        in_specs=[pl.BlockSpec((h, tr, v), lambda i: (0, i, 0)), pl.BlockSpec((1, h * v), lambda i: (0, 0))],
        out_specs=pl.BlockSpec((tr, h * v), lambda i: (i, 0)),
        out_shape=jax.ShapeDtypeStruct((s_, width), MXU_DTYPE), compiler_params=_params("parallel"),
    )(o, w)


def _hnorm_bwd(o, w, dy, name="attn_out_norm_bwd"):
    h, s_, v = o.shape
    tr = _row_tile(s_)

    def body(o_ref, w_ref, dy_ref, do_ref, delta_ref, dw_ref):
        @pl.when(pl.program_id(0) == 0)
        def _():
            dw_ref[...] = jnp.zeros_like(dw_ref)

        ss = jnp.zeros((tr, 1), F32)
        cc = jnp.zeros((tr, 1), F32)
        for i in range(h):
            sl = slice(i * v, (i + 1) * v)
            ov = o_ref[i]
            ss = ss + jnp.sum(ov * ov, axis=-1, keepdims=True)
            cc = cc + jnp.sum(dy_ref[:, sl] * w_ref[:, sl] * ov, axis=-1, keepdims=True)
        r = lax.rsqrt(ss * (1.0 / (h * v)) + EPS)
        c = cc * (1.0 / (h * v))
        for i in range(h):
            sl = slice(i * v, (i + 1) * v)
            ov = o_ref[i]
            dyv = dy_ref[:, sl]
            dov = r * dyv * w_ref[:, sl] - ov * (r * r * r * c)
            do_ref[i] = dov.astype(do_ref.dtype)
            delta_ref[i] = jnp.sum(dov * ov, axis=-1, keepdims=True)
            dw_ref[:, sl] += jnp.sum(dyv * ov * r, axis=0, keepdims=True)

    blk = pl.BlockSpec((h, tr, v), lambda i: (0, i, 0))
    wsp = pl.BlockSpec((1, h * v), lambda i: (0, 0))
    return pl.pallas_call(
        body, name=name, grid=(s_ // tr,),
        in_specs=[blk, wsp, pl.BlockSpec((tr, h * v), lambda i: (i, 0))],
        out_specs=[blk, pl.BlockSpec((h, tr, 1), lambda i: (0, i, 0)), wsp],
        out_shape=[jax.ShapeDtypeStruct(o.shape, MXU_DTYPE), jax.ShapeDtypeStruct((h, s_, 1), F32),
                   jax.ShapeDtypeStruct((1, h * v), F32)],
        compiler_params=_params("arbitrary"),
    )(o, w, dy)


def _loss_head(ffn, h1, target, w, name="loss_head"):
    r_, f = ffn.shape
    tr = _row_tile(r_)

    def body(ffn_ref, h1_ref, tg_ref, w_ref, loss_ref, dy_ref, dffn_ref, dw_ref):
        @pl.when(pl.program_id(0) == 0)
        def _():
            dw_ref[...] = jnp.zeros_like(dw_ref)
            loss_ref[...] = jnp.zeros_like(loss_ref)

        tv = ffn_ref[...]
        wv = w_ref[...]
        r = lax.rsqrt(jnp.mean(tv * tv, axis=-1, keepdims=True) + EPS)
        tn = tv * r
        e = h1_ref[...] + tn * wv - tg_ref[...]
        tot = jnp.sum(jnp.sum(e * e, axis=1, keepdims=True), axis=0, keepdims=True) * (0.5 / f)
        loss_ref[...] += tot + jnp.zeros_like(loss_ref)
        dyv = e * (1.0 / f)
        dy_ref[...] = dyv
        gw = dyv * wv
        c = jnp.mean(gw * tv, axis=-1, keepdims=True)
        dffn_ref[...] = (r * gw - tv * (r * r * r * c)).astype(dffn_ref.dtype)
        dw_ref[...] += jnp.sum(dyv * tn, axis=0, keepdims=True)

    row = pl.BlockSpec((tr, f), lambda i: (i, 0))
    wsp = pl.BlockSpec((1, f), lambda i: (0, 0))
    lsp = pl.BlockSpec((1, LANE), lambda i: (0, 0))
    return pl.pallas_call(
        body, name=name, grid=(r_ // tr,),
        in_specs=[row, row, row, wsp], out_specs=[lsp, row, row, wsp],
        out_shape=[jax.ShapeDtypeStruct((1, LANE), F32), jax.ShapeDtypeStruct((r_, f), F32),
                   jax.ShapeDtypeStruct((r_, f), MXU_DTYPE), jax.ShapeDtypeStruct((1, f), F32)],
        compiler_params=_params("arbitrary"),
    )(ffn, h1, target, w.reshape(1, f))


def _rot_matrix():
    p = np.zeros((ROPE, ROPE), np.float32)
    for i in range(HALF):
        p[i + HALF, i] = -1.0
        p[i, i + HALF] = 1.0
    return jnp.asarray(p, BF16)


def _rope_val(r, c2, s2, rot):
    hi, mid, _ = _split3(r)
    return r * c2 + (_dot(hi, rot, 1, 0) + _dot(mid, rot, 1, 0)) * s2


def _q_prep(q, cos2, sin2, scale, name):
    h, s_, _ = q.shape
    tr = _pick(s_, (4096, 2048, 1024, 512, 256, 128, 64, 32, 16))

    def body(q_ref, c_ref, s_ref, rot_ref, o_ref):
        for rs in _row_slices(tr, 16):
            x = q_ref[rs, :]
            o_ref[rs, :NOPE] = (x[:, :NOPE] * scale).astype(o_ref.dtype)
            o_ref[rs, NOPE:] = (_rope_val(x[:, NOPE:], c_ref[rs, :], s_ref[rs, :], rot_ref[...]) * scale).astype(o_ref.dtype)

    blk = pl.BlockSpec((None, tr, QK), lambda hh, i: (hh, i, 0))
    csp = pl.BlockSpec((tr, ROPE), lambda hh, i: (i, 0))
    return pl.pallas_call(
        body, name=name, grid=(h, s_ // tr),
        in_specs=[blk, csp, csp, pl.BlockSpec((ROPE, ROPE), lambda hh, i: (0, 0))], out_specs=blk,
        out_shape=jax.ShapeDtypeStruct(q.shape, MXU_DTYPE), compiler_params=_params("parallel", "parallel"),
    )(q, cos2, sin2, _rot_matrix())


def _q_up(qkvn, w_uq_t, cos2, sin2, scale, name="q_up"):
    s_ = qkvn.shape[0]
    h = w_uq_t.shape[0]
    tm = _pick(s_, (4096, 2048, 1024, 512, 256, 128))

    def body(a_ref, w_ref, c_ref, s_ref, rot_ref, o_ref):
        for rs in _row_slices(tm, 16):
            x = _dot(_mx(a_ref[rs, :]), _mx(w_ref[...]), 1, 1)
            o_ref[rs, :NOPE] = (x[:, :NOPE] * scale).astype(o_ref.dtype)
            o_ref[rs, NOPE:] = (_rope_val(x[:, NOPE:], c_ref[rs, :], s_ref[rs, :], rot_ref[...]) * scale).astype(o_ref.dtype)

    csp = pl.BlockSpec((tm, ROPE), lambda j, i: (i, 0))
    return pl.pallas_call(
        body, name=name, grid=(h, s_ // tm),
        in_specs=[pl.BlockSpec((tm, Q_RANK), lambda j, i: (i, 0)), pl.BlockSpec((None, QK, Q_RANK), lambda j, i: (j, 0, 0)),
                  csp, csp, pl.BlockSpec((ROPE, ROPE), lambda j, i: (0, 0))],
        out_specs=pl.BlockSpec((None, tm, QK), lambda j, i: (j, i, 0)),
        out_shape=jax.ShapeDtypeStruct((h, s_, QK), MXU_DTYPE), compiler_params=_params("parallel", "parallel"),
    )(qkvn, w_uq_t, cos2, sin2, _rot_matrix())


def _kv_up(qkvn, w_ukv, small, cos2, sin2, name="kv_up"):
    s_ = qkvn.shape[0]
    h = w_ukv.shape[0]
    tm = _pick(s_, (4096, 2048, 1024, 512, 256, 128))

    def body(a_ref, w_ref, sm_ref, c_ref, s_ref, rot_ref, k_ref, v_ref):
        for rs in _row_slices(tm, 16):
            x = _dot(_mx(a_ref[rs, :]), _mx(w_ref[...]), 1, 0)
            k_ref[rs, :NOPE] = x[:, :NOPE].astype(k_ref.dtype)
            k_ref[rs, NOPE:] = _rope_val(sm_ref[rs, :ROPE], c_ref[rs, :], s_ref[rs, :], rot_ref[...]).astype(k_ref.dtype)
            v_ref[rs, :] = x[:, NOPE:].astype(v_ref.dtype)

    csp = pl.BlockSpec((tm, ROPE), lambda j, i: (i, 0))
    return pl.pallas_call(
        body, name=name, grid=(h, s_ // tm),
        in_specs=[pl.BlockSpec((tm, KV_RANK), lambda j, i: (i, Q_RANK // KV_RANK)),
                  pl.BlockSpec((None, KV_RANK, NOPE + VDIM), lambda j, i: (j, 0, 0)),
                  pl.BlockSpec((tm, LANE), lambda j, i: (i, 0)), csp, csp, pl.BlockSpec((ROPE, ROPE), lambda j, i: (0, 0))],
        out_specs=[pl.BlockSpec((None, tm, QK), lambda j, i: (j, i, 0)), pl.BlockSpec((None, tm, VDIM), lambda j, i: (j, i, 0))],
        out_shape=[jax.ShapeDtypeStruct((h, s_, QK), MXU_DTYPE), jax.ShapeDtypeStruct((h, s_, VDIM), MXU_DTYPE)],
        compiler_params=_params("parallel", "parallel"),
    )(qkvn, w_ukv, small, cos2, sin2, _rot_matrix())


def _dkv_post(dk, dv, ddt, cos2, nsin2, name="dkv_post"):
    h, s_, _ = dk.shape
    tr = _row_tile(s_)

    def body(dk_ref, dv_ref, ddt_ref, c_ref, s_ref, rot_ref, dkv_ref, dsm_ref):
        acc = dk_ref[0, :, NOPE:]
        for i in range(1, h):
            acc = acc + dk_ref[i, :, NOPE:]
        dsm_ref[:, :ROPE] = _rope_val(acc, c_ref[...], s_ref[...], rot_ref[...]).astype(dsm_ref.dtype)
        dsm_ref[:, ROPE:] = ddt_ref[:, ROPE:].astype(dsm_ref.dtype)
        for i in range(h):
            dkv_ref[i, :, :NOPE] = dk_ref[i, :, :NOPE].astype(dkv_ref.dtype)
            dkv_ref[i, :, NOPE:] = dv_ref[i].astype(dkv_ref.dtype)

    csp = pl.BlockSpec((tr, ROPE), lambda i: (i, 0))
    return pl.pallas_call(
        body, name=name, grid=(s_ // tr,),
        in_specs=[pl.BlockSpec((h, tr, QK), lambda i: (0, i, 0)), pl.BlockSpec((h, tr, VDIM), lambda i: (0, i, 0)),
                  pl.BlockSpec((tr, LANE), lambda i: (i, 0)), csp, csp, pl.BlockSpec((ROPE, ROPE), lambda i: (0, 0))],
        out_specs=[pl.BlockSpec((h, tr, NOPE + VDIM), lambda i: (0, i, 0)), pl.BlockSpec((tr, LANE), lambda i: (i, 0))],
        out_shape=[jax.ShapeDtypeStruct((h, s_, NOPE + VDIM), MXU_DTYPE), jax.ShapeDtypeStruct((s_, LANE), MXU_DTYPE)],
        compiler_params=_params("parallel"),
    )(dk, dv, ddt, cos2, nsin2, _rot_matrix())


def _attn_tile(s):
    return 2048 if s % 4096 == 0 else s // 2


def _pairs(n, by_key):
    if by_key:
        pr = [(i, j) for j in range(n) for i in range(j, n)]
    else:
        pr = [(i, j) for i in range(n) for j in range(i + 1)]
    return (jnp.asarray([p[0] for p in pr], jnp.int32), jnp.asarray([p[1] for p in pr], jnp.int32))


ATTN_ROW_GROUPS = 8


def _row_groups(t, diag):
    tg = t // ATTN_ROW_GROUPS
    out = []
    for r in range(ATTN_ROW_GROUPS):
        nc = (r + 1) * tg if diag else t
        mask = None
        if diag:
            mask = (lax.broadcasted_iota(jnp.int32, (tg, nc), 1)
                    <= lax.broadcasted_iota(jnp.int32, (tg, nc), 0) + r * tg)
        out.append((slice(r * tg, (r + 1) * tg), nc, mask))
    return out


def _flash_specs(t, dk, dv):
    qsp = pl.BlockSpec((None, t, dk), lambda hh, p, qi, kj: (hh, qi[p], 0))
    ksp = pl.BlockSpec((None, t, dk), lambda hh, p, qi, kj: (hh, kj[p], 0))
    vsp = pl.BlockSpec((None, t, dv), lambda hh, p, qi, kj: (hh, kj[p], 0))
    osp = pl.BlockSpec((None, t, dv), lambda hh, p, qi, kj: (hh, qi[p], 0))
    lsp = pl.BlockSpec((None, t, 1), lambda hh, p, qi, kj: (hh, qi[p], 0))
    return qsp, ksp, vsp, osp, lsp


def _flash_fwd(q, k, v, name="flash_fwd"):
    h, s_, dk = q.shape
    dv = v.shape[-1]
    t = _attn_tile(s_)
    n = s_ // t
    qi, kj = _pairs(n, False)

    def body(qi_ref, kj_ref, q_ref, k_ref, v_ref, o_ref, lse_ref, m_s, l_s, acc):
        p_ = pl.program_id(1)
        i, j = qi_ref[p_], kj_ref[p_]

        @pl.when(j == 0)
        def _():
            m_s[...] = jnp.full_like(m_s, -jnp.inf)
            l_s[...] = jnp.zeros_like(l_s)
            acc[...] = jnp.zeros_like(acc)

        def update(diag):
            for rs, nc, mask in _row_groups(t, diag):
                sc = _dot(q_ref[rs, :], k_ref[0:nc, :], 1, 1)
                if mask is not None:
                    sc = jnp.where(mask, sc, -jnp.inf)
                m_old = m_s[rs, :]
                m_new = jnp.maximum(m_old, jnp.max(sc, axis=1, keepdims=True))
                alpha = jnp.exp(m_old - m_new)
                p = jnp.exp(sc - m_new)
                l_s[rs, :] = alpha * l_s[rs, :] + jnp.sum(p, axis=1, keepdims=True)
                acc[rs, :] = alpha * acc[rs, :] + _dot(_mx(p), v_ref[0:nc, :], 1, 0)
                m_s[rs, :] = m_new

        @pl.when(j < i)
        def _():
            update(False)

        @pl.when(j == i)
        def _():
            update(True)
            o_ref[...] = acc[...] / l_s[...]
            lse_ref[...] = m_s[...] + jnp.log(l_s[...])

    qsp, ksp, vsp, osp, lsp = _flash_specs(t, dk, dv)
    gs = pltpu.PrefetchScalarGridSpec(
        num_scalar_prefetch=2, grid=(h, qi.shape[0]), in_specs=[qsp, ksp, vsp], out_specs=[osp, lsp],
        scratch_shapes=[pltpu.VMEM((t, 1), F32), pltpu.VMEM((t, 1), F32), pltpu.VMEM((t, dv), F32)])
    return pl.pallas_call(
        body, name=name, grid_spec=gs,
        out_shape=[jax.ShapeDtypeStruct((h, s_, dv), F32), jax.ShapeDtypeStruct((h, s_, 1), F32)],
        compiler_params=_params("parallel", "arbitrary"),
    )(qi, kj, q, k, v)


def _flash_bwd(q, k, v, do, lse, delta, name="flash_bwd"):
    h, s_, dk = q.shape
    dv = v.shape[-1]
    t = _attn_tile(s_)
    tg = t // ATTN_ROW_GROUPS
    n = s_ // t
    qi, kj = _pairs(n, True)

    def body(qi_ref, kj_ref, q_ref, k_ref, v_ref, do_ref, lse_ref, delta_ref, dq_ref, dk_ref, dv_ref, dk_acc, dv_acc):
        p_ = pl.program_id(1)
        i, j = qi_ref[p_], kj_ref[p_]

        @pl.when(p_ == 0)
        def _():
            dq_ref[...] = jnp.zeros_like(dq_ref)

        def update(diag):
            for g, (rs, nc, mask) in enumerate(_row_groups(t, diag)):
                sc = _dot(q_ref[rs, :], k_ref[0:nc, :], 1, 1)
                if mask is not None:
                    sc = jnp.where(mask, sc, -jnp.inf)
                p = jnp.exp(sc - lse_ref[rs, :])
                dob = _mx(do_ref[rs, :])
                dv_acc[0:nc, :] += _dot(_mx(p), dob, 0, 0)
                dp = _dot(dob, v_ref[0:nc, :], 1, 1)
                dsb = _mx(p * (dp - delta_ref[rs, :]))
                dk_acc[0:nc, :] += _dot(dsb, q_ref[rs, :], 0, 0)
                rows = pl.ds(pl.multiple_of(i * t + g * tg, tg), tg)
                dq_ref[rows, :] += _dot(dsb, k_ref[0:nc, :], 1, 0)

        @pl.when(i == j)
        def _():
            dk_acc[...] = jnp.zeros_like(dk_acc)
            dv_acc[...] = jnp.zeros_like(dv_acc)
            update(True)

        @pl.when(i > j)
        def _():
            update(False)

        @pl.when(i == n - 1)
        def _():
            dk_ref[...] = dk_acc[...]
            dv_ref[...] = dv_acc[...]

    qsp, ksp, vsp, osp, lsp = _flash_specs(t, dk, dv)
    dqsp = pl.BlockSpec((None, s_, dk), lambda hh, p, qi, kj: (hh, 0, 0))
    gs = pltpu.PrefetchScalarGridSpec(
        num_scalar_prefetch=2, grid=(h, qi.shape[0]), in_specs=[qsp, ksp, vsp, osp, lsp, lsp],
        out_specs=[dqsp, ksp, vsp],
        scratch_shapes=[pltpu.VMEM((t, dk), F32), pltpu.VMEM((t, dv), F32)])
    return pl.pallas_call(
        body, name=name, grid_spec=gs,
        out_shape=[jax.ShapeDtypeStruct((h, s_, dk), F32), jax.ShapeDtypeStruct((h, s_, dk), F32),
                   jax.ShapeDtypeStruct((h, s_, dv), F32)],
        compiler_params=_params("parallel", "arbitrary"),
    )(qi, kj, q, k, v, do, lse, delta)


HALO = 8


def _conv_specs(s_, c, tr, after):
    main = pl.BlockSpec((tr, c), lambda i: (i, 0))
    per = tr // HALO
    if after:
        halo = pl.BlockSpec((HALO, c), lambda i: (jnp.minimum((i + 1) * per, s_ // HALO - 1), 0))
    else:
        halo = pl.BlockSpec((HALO, c), lambda i: (jnp.maximum(i * per - 1, 0), 0))
    return main, halo


def _fill_before(ext, t_ref, h_ref, tr):
    ext[0:HALO, :] = jnp.where(pl.program_id(0) > 0, h_ref[...], 0.0)
    ext[HALO:HALO + tr, :] = t_ref[...]


def _taps(ext, w_ref, tr):
    base = HALO - (CONV_K - 1)
    acc = ext[base:base + tr, :] * w_ref[0:1, :]
    for k in range(1, CONV_K):
        acc = acc + ext[base + k:base + k + tr, :] * w_ref[k:k + 1, :]
    return acc


def _conv_fwd(t, w, b, name="conv_fwd"):
    s_, c = t.shape
    tr = _row_tile(s_)

    def body(t_ref, h_ref, w_ref, b_ref, o_ref, ext):
        _fill_before(ext, t_ref, h_ref, tr)
        o_ref[...] = _silu(_taps(ext, w_ref, tr) + b_ref[...])

    main, halo = _conv_specs(s_, c, tr, False)
    return pl.pallas_call(
        body, name=name, grid=(s_ // tr,),
        in_specs=[main, halo, pl.BlockSpec((CONV_K, c), lambda i: (0, 0)), pl.BlockSpec((1, c), lambda i: (0, 0))],
        out_specs=main, out_shape=jax.ShapeDtypeStruct((s_, c), F32),
        scratch_shapes=[pltpu.VMEM((tr + HALO, c), F32)], compiler_params=_params("parallel"),
    )(t, t, w, b)


def _conv_bwd_pre(t, w, b, dact, name="conv_bwd_pre"):
    s_, c = t.shape
    tr = _row_tile(s_)

    def body(t_ref, h_ref, w_ref, b_ref, da_ref, dpre_ref, dwb_ref, ext):
        @pl.when(pl.program_id(0) == 0)
        def _():
            dwb_ref[...] = jnp.zeros_like(dwb_ref)

        _fill_before(ext, t_ref, h_ref, tr)
        dpre = da_ref[...] * _dsilu(_taps(ext, w_ref, tr) + b_ref[...])
        dpre_ref[...] = dpre
        base = HALO - (CONV_K - 1)
        for k in range(CONV_K):
            dwb_ref[k:k + 1, :] += jnp.sum(dpre * ext[base + k:base + k + tr, :], axis=0, keepdims=True)
        dwb_ref[CONV_K:CONV_K + 1, :] += jnp.sum(dpre, axis=0, keepdims=True)

    main, halo = _conv_specs(s_, c, tr, False)
    return pl.pallas_call(
        body, name=name, grid=(s_ // tr,),
        in_specs=[main, halo, pl.BlockSpec((CONV_K, c), lambda i: (0, 0)), pl.BlockSpec((1, c), lambda i: (0, 0)), main],
        out_specs=[main, pl.BlockSpec((8, c), lambda i: (0, 0))],
        out_shape=[jax.ShapeDtypeStruct((s_, c), F32), jax.ShapeDtypeStruct((8, c), F32)],
        scratch_shapes=[pltpu.VMEM((tr + HALO, c), F32)], compiler_params=_params("arbitrary"),
    )(t, t, w, b, dact)


def _conv_bwd_in(dpre, w, name="conv_bwd_in"):
    s_, c = dpre.shape
    tr = _row_tile(s_)
    nt = s_ // tr

    def body(d_ref, h_ref, w_ref, o_ref, ext):
        ext[0:tr, :] = d_ref[...]
        ext[tr:tr + HALO, :] = jnp.where(pl.program_id(0) < nt - 1, h_ref[...], 0.0)
        acc = ext[CONV_K - 1:CONV_K - 1 + tr, :] * w_ref[0:1, :]
        for k in range(1, CONV_K):
            acc = acc + ext[CONV_K - 1 - k:CONV_K - 1 - k + tr, :] * w_ref[k:k + 1, :]
        o_ref[...] = acc.astype(o_ref.dtype)

    main, halo = _conv_specs(s_, c, tr, True)
    return pl.pallas_call(
        body, name=name, grid=(nt,),
        in_specs=[main, halo, pl.BlockSpec((CONV_K, c), lambda i: (0, 0))],
        out_specs=main, out_shape=jax.ShapeDtypeStruct((s_, c), MXU_DTYPE),
        scratch_shapes=[pltpu.VMEM((tr + HALO, c), F32)], compiler_params=_params("parallel"),
    )(dpre, dpre, w)


def _ssd_chunk_common(dt_ref, dtt_ref, br_ref, bc_ref, ar_ref, ac_ref):
    li = lax.broadcasted_iota(jnp.int32, (CHUNK, CHUNK), 0)
    si = lax.broadcasted_iota(jnp.int32, (CHUNK, CHUNK), 1)
    lower = li >= si
    lower_b = lower.astype(BF16)
    upper_b = (li <= si).astype(BF16)
    zr = dt_ref[...] + br_ref[...]
    dtc = _softplus(zr)
    a_row = -jnp.exp(ar_ref[...])
    acum = _exact_dot(lower_b, dtc * a_row, 1, 0, False)
    dtt = _softplus(dtt_ref[...] + bc_ref[...])
    acum_t = _exact_dot(dtt * (-jnp.exp(ac_ref[...])), upper_b, 1, 0, True)
    return lower, upper_b, zr, dtc, a_row, acum, acum_t


def _head_terms(h, lower, dtc, acum, acum_t):
    lane = lax.broadcasted_iota(jnp.int32, (1, LANE), 1)
    sub = lax.broadcasted_iota(jnp.int32, (SSD_H, 1), 0)
    rowid = lax.broadcasted_iota(jnp.int32, (CHUNK, 1), 0)
    oh = (lane == HEAD_LANE + h).astype(F32)
    acol = jnp.sum(acum * oh, axis=1, keepdims=True)
    dcol = jnp.sum(dtc * oh, axis=1, keepdims=True)
    arow = jnp.sum(acum_t * (sub == h).astype(F32), axis=0, keepdims=True)
    alast = jnp.sum(jnp.where(rowid == CHUNK - 1, acol, 0.0), axis=0, keepdims=True)
    decay = jnp.exp(jnp.where(lower, acol - arow, -jnp.inf))
    return oh, acol, dcol, alast, decay


SSD_PAIRS = SSD_H // 2
PAIRS_PER_GROUP = SSD_E // 2


def _ps(q):
    return slice(q * LANE, (q + 1) * LANE)


def _gs(off, g):
    return slice(off + g * SSD_N, off + (g + 1) * SSD_N)


def _lanes(c0, c1):
    return jnp.where(lax.broadcasted_iota(jnp.int32, (1, LANE), 1) < SSD_P, c0, c1)


def _rows(c0, c1):
    return jnp.where(lax.broadcasted_iota(jnp.int32, (LANE, 1), 0) < SSD_P, c0, c1)


def _lane_halves(t):
    first = lax.broadcasted_iota(jnp.int32, (1, LANE), 1) < SSD_P
    return (jnp.sum(jnp.where(first, t, 0.0), axis=1, keepdims=True),
            jnp.sum(jnp.where(first, 0.0, t), axis=1, keepdims=True))


def _ssd_in_specs(rev):
    def ci(c):
        return c if rev is None else rev - c
    return [pl.BlockSpec((CHUNK, CONV_DIM), lambda c: (ci(c), 0)),
            pl.BlockSpec((CHUNK, LANE), lambda c: (ci(c), 0)),
            pl.BlockSpec((SSD_H, CHUNK), lambda c: (0, ci(c))),
            pl.BlockSpec((1, LANE), lambda c: (0, 0)), pl.BlockSpec((SSD_H, 1), lambda c: (0, 0)),
            pl.BlockSpec((1, LANE), lambda c: (0, 0)), pl.BlockSpec((SSD_H, 1), lambda c: (0, 0)),
            pl.BlockSpec((SSD_PAIRS, 1, LANE), lambda c: (0, 0, 0))]


def _ssd_fwd(xbc, small, dtt, bias_r, bias_c, alog_r, alog_c, dsk, name="ssd_fwd"):
    s_ = xbc.shape[0]
    nc = s_ // CHUNK

    def body(x_ref, dt_ref, dtt_ref, br_ref, bc_ref, ar_ref, ac_ref, dsk_ref, y_ref, prev_ref, state):
        @pl.when(pl.program_id(0) == 0)
        def _():
            state[...] = jnp.zeros_like(state)

        lower, _, _, dtc, _, acum, acum_t = _ssd_chunk_common(dt_ref, dtt_ref, br_ref, bc_ref, ar_ref, ac_ref)
        for g in range(SSD_G):
            bb = _mx(x_ref[:, _gs(B_OFF, g)])
            cb_ = _mx(x_ref[:, _gs(C_OFF, g)])
            cbm = _dot(cb_, bb, 1, 1)
            for e in range(PAIRS_PER_GROUP):
                q = g * PAIRS_PER_GROUP + e
                _, acol0, dcol0, alast0, decay0 = _head_terms(2 * q, lower, dtc, acum, acum_t)
                _, acol1, dcol1, alast1, decay1 = _head_terms(2 * q + 1, lower, dtc, acum, acum_t)
                x = x_ref[:, _ps(q)]
                xdt = x * _lanes(dcol0, dcol1)
                xb = _mx(xdt)
                yd = _lanes(_dot(_mx(cbm * decay0), xb, 1, 0), _dot(_mx(cbm * decay1), xb, 1, 0))
                prev = state[q]
                prev_ref[0, q] = prev
                yo = _dot(cb_, _mx(prev), 1, 1) * _lanes(jnp.exp(acol0), jnp.exp(acol1))
                ds = _lanes(jnp.exp(alast0 - acol0), jnp.exp(alast1 - acol1))
                st = _dot(_mx(xdt * ds), bb, 0, 0)
                state[q] = prev * _rows(jnp.exp(alast0), jnp.exp(alast1)) + st
                y_ref[:, _ps(q)] = yd + yo + x * dsk_ref[q]

    psp = pl.BlockSpec((1, SSD_PAIRS, LANE, SSD_N), lambda c: (c, 0, 0, 0))
    return pl.pallas_call(
        body, name=name, grid=(nc,),
        in_specs=_ssd_in_specs(None), out_specs=[pl.BlockSpec((CHUNK, SSD_W), lambda c: (c, 0)), psp],
        out_shape=[jax.ShapeDtypeStruct((s_, SSD_W), F32),
                   jax.ShapeDtypeStruct((nc, SSD_PAIRS, LANE, SSD_N), F32)],
        scratch_shapes=[pltpu.VMEM((SSD_PAIRS, LANE, SSD_N), F32)],
        compiler_params=_params("arbitrary"),
    )(xbc, small, dtt, bias_r, bias_c, alog_r, alog_c, dsk)


def _ssd_bwd(xbc, small, dtt, bias_r, bias_c, alog_r, alog_c, dsk, prev, dy, name="ssd_bwd"):
    s_ = xbc.shape[0]
    nc = s_ // CHUNK

    def body(x_ref, dt_ref, dtt_ref, br_ref, bc_ref, ar_ref, ac_ref, dsk_ref, prev_ref, dy_ref,
             dx_ref, ddt_ref, dpar_ref, dstate):
        @pl.when(pl.program_id(0) == 0)
        def _():
            dstate[...] = jnp.zeros_like(dstate)
            dpar_ref[...] = jnp.zeros_like(dpar_ref)

        lower, upper_b, zr, dtc, a_row, acum, acum_t = _ssd_chunk_common(
            dt_ref, dtt_ref, br_ref, bc_ref, ar_ref, ac_ref)
        strict = (lax.broadcasted_iota(jnp.int32, (CHUNK, CHUNK), 1)
                  < lax.broadcasted_iota(jnp.int32, (CHUNK, CHUNK), 0))
        strict_b = strict.astype(BF16)
        col2 = lax.broadcasted_iota(jnp.int32, (CHUNK, 2 * CHUNK), 1)
        strict2 = (jnp.where(col2 >= CHUNK, col2 - CHUNK, col2)
                   < lax.broadcasted_iota(jnp.int32, (CHUNK, 2 * CHUNK), 0))
        da_in = jnp.zeros((CHUNK, LANE), F32)
        r_off = jnp.zeros((CHUNK, LANE), F32)
        c_int = jnp.zeros((CHUNK, LANE), F32)
        c_row = jnp.zeros((1, LANE), F32)
        ddt = jnp.zeros((CHUNK, LANE), F32)
        dskip = jnp.zeros((1, LANE), F32)
        for g in range(SSD_G):
            bb = _mx(x_ref[:, _gs(B_OFF, g)])
            cb_ = _mx(x_ref[:, _gs(C_OFF, g)])
            cbm = _dot(cb_, bb, 1, 1)
            dcb = jnp.zeros((CHUNK, CHUNK), F32)
            dc_acc = jnp.zeros((CHUNK, SSD_N), F32)
            db_acc = jnp.zeros((CHUNK, SSD_N), F32)
            for e in range(PAIRS_PER_GROUP):
                q = g * PAIRS_PER_GROUP + e
                oh0, acol0, dcol0, alast0, decay0 = _head_terms(2 * q, lower, dtc, acum, acum_t)
                oh1, acol1, dcol1, alast1, decay1 = _head_terms(2 * q + 1, lower, dtc, acum, acum_t)
                x = x_ref[:, _ps(q)]
                dy = dy_ref[:, _ps(q)]
                dcol = _lanes(dcol0, dcol1)
                xdt = x * dcol
                xb = _mx(xdt)
                eacol = _lanes(jnp.exp(acol0), jnp.exp(acol1))
                ds = _lanes(jnp.exp(alast0 - acol0), jnp.exp(alast1 - acol1))
                ealast = _rows(jnp.exp(alast0), jnp.exp(alast1))
                dyb = _mx(dy)
                dyb0, dyb1 = _mx(_lanes(dy, 0.0)), _mx(_lanes(0.0, dy))
                dsh = dstate[q]
                dshb = _mx(dsh)
                prev = prev_ref[0, q]
                prevb = _mx(prev)
                dxdt_inter = ds * _dot(bb, dshb, 1, 1)
                dxdt = _lanes(_dot(_mx(cbm * decay0), dyb, 0, 0), _dot(_mx(cbm * decay1), dyb, 0, 0)) + dxdt_inter
                dwl0 = _dot(dyb0, xb, 1, 1) * decay0
                dwl1 = _dot(dyb1, xb, 1, 1) * decay1
                dcb = dcb + dwl0 + dwl1
                dyeb = _mx(dy * eacol)
                dc_acc = dc_acc + _dot(dyeb, prevb, 1, 0)
                db_acc = db_acc + _dot(_mx(xdt * ds), dshb, 1, 0)
                dstate[q] = _dot(dyeb, cb_, 0, 0) + ealast * dsh
                above = _exact_dot(upper_b, jnp.concatenate([dwl0 * cbm, dwl1 * cbm], axis=1), 1, 0, False)
                above = jnp.where(strict2, above, 0.0)
                da_in = (da_in + jnp.sum(above[:, :CHUNK], axis=1, keepdims=True) * oh0
                         + jnp.sum(above[:, CHUNK:], axis=1, keepdims=True) * oh1)
                y_off = _dot(cb_, prevb, 1, 1) * eacol
                r0, r1 = _lane_halves(dy * y_off)
                r_off = r_off + r0 * oh0 + r1 * oh1
                c0, c1 = _lane_halves(xdt * dxdt_inter)
                c_int = c_int + c0 * oh0 + c1 * oh1
                both = jnp.sum(dsh * prev, axis=1, keepdims=True) * ealast
                c_row = (c_row + jnp.sum(_rows(both, 0.0), axis=0, keepdims=True) * oh0
                         + jnp.sum(_rows(0.0, both), axis=0, keepdims=True) * oh1)
                t0, t1 = _lane_halves(dxdt * x)
                ddt = ddt + t0 * oh0 + t1 * oh1
                dx_ref[:, _ps(q)] = dxdt * dcol + dy * dsk_ref[q]
                k0, k1 = _lane_halves(dy * x)
                dskip = (dskip + jnp.sum(k0, axis=0, keepdims=True) * oh0 + jnp.sum(k1, axis=0, keepdims=True) * oh1)
            dcbb = _mx(dcb)
            dx_ref[:, _gs(C_OFF, g)] = dc_acc + _dot(dcbb, bb, 1, 0)
            dx_ref[:, _gs(B_OFF, g)] = db_acc + _dot(dcbb, cb_, 0, 0)
        da = (da_in + _exact_dot(upper_b, r_off, 1, 0, False) + _exact_dot(strict_b, c_int, 1, 0, False) + c_row)
        draw = (ddt + da * a_row) * _sigmoid(zr)
        ddt_ref[...] = draw
        dpar_ref[0:1, :] += jnp.sum(draw, axis=0, keepdims=True)
        dpar_ref[1:2, :] += jnp.sum(da * dtc, axis=0, keepdims=True) * a_row
        dpar_ref[2:3, :] += dskip

    rev = nc - 1
    psp = pl.BlockSpec((1, SSD_PAIRS, LANE, SSD_N), lambda c: (rev - c, 0, 0, 0))
    return pl.pallas_call(
        body, name=name, grid=(nc,),
        in_specs=_ssd_in_specs(rev) + [psp, pl.BlockSpec((CHUNK, SSD_W), lambda c: (rev - c, 0))],
        out_specs=[pl.BlockSpec((CHUNK, CONV_DIM), lambda c: (rev - c, 0)),
                   pl.BlockSpec((CHUNK, LANE), lambda c: (rev - c, 0)), pl.BlockSpec((8, LANE), lambda c: (0, 0))],
        out_shape=[jax.ShapeDtypeStruct((s_, CONV_DIM), F32), jax.ShapeDtypeStruct((s_, LANE), F32),
                   jax.ShapeDtypeStruct((8, LANE), F32)],
        scratch_shapes=[pltpu.VMEM((SSD_PAIRS, LANE, SSD_N), F32)],
        compiler_params=_params("arbitrary"),
    )(xbc, small, dtt, bias_r, bias_c, alog_r, alog_c, dsk, prev, dy)


GN = SSD_W // SSD_G


def _gated_norm_fwd(y, z, w, cat, name="gated_norm_fwd"):
    s_, f = y.shape
    tr = _row_tile(s_)

    def body(y_ref, z_ref, w_ref, cat_ref, o_ref):
        for g in range(SSD_G):
            sl = slice(g * GN, (g + 1) * GN)
            gg = y_ref[:, sl] * _silu(z_ref[:, sl])
            r = lax.rsqrt(jnp.mean(gg * gg, axis=-1, keepdims=True) + EPS)
            o_ref[:, sl] = (gg * r * w_ref[:, sl]).astype(o_ref.dtype)

    row = pl.BlockSpec((tr, f), lambda i: (i, 0))
    wsp = pl.BlockSpec((1, f), lambda i: (0, 0))
    return pl.pallas_call(
        body, name=name, grid=(s_ // tr,),
        in_specs=[row, row, wsp, pl.BlockSpec(memory_space=pl.ANY)], out_specs=pl.BlockSpec((tr, f), lambda i: (i, 1)),
        out_shape=jax.ShapeDtypeStruct(cat.shape, cat.dtype), input_output_aliases={3: 0},
        compiler_params=_params("parallel"),
    )(y, z, w.reshape(1, f), cat)


def _gated_norm_bwd(y, z, w, dout, name="gated_norm_bwd"):
    s_, f = y.shape
    tr = _row_tile(s_)

    def body(y_ref, z_ref, w_ref, do_ref, dy_ref, dz_ref, dw_ref):
        @pl.when(pl.program_id(0) == 0)
        def _():
            dw_ref[...] = jnp.zeros_like(dw_ref)

        for g in range(SSD_G):
            sl = slice(g * GN, (g + 1) * GN)
            yv = y_ref[:, sl]
            zv = z_ref[:, sl]
            dov = do_ref[:, sl].astype(F32)
            sz = _silu(zv)
            gg = yv * sz
            r = lax.rsqrt(jnp.mean(gg * gg, axis=-1, keepdims=True) + EPS)
            gw = dov * w_ref[:, sl]
            c = jnp.mean(gw * gg, axis=-1, keepdims=True)
            dgg = r * gw - gg * (r * r * r * c)
            dy_ref[:, sl] = dgg * sz
            dz_ref[:, sl] = (dgg * yv * _dsilu(zv)).astype(dz_ref.dtype)
            dw_ref[:, sl] += jnp.sum(dov * gg * r, axis=0, keepdims=True)

    row = pl.BlockSpec((tr, f), lambda i: (i, 0))
    wsp = pl.BlockSpec((1, f), lambda i: (0, 0))
    return pl.pallas_call(
        body, name=name, grid=(s_ // tr,),
        in_specs=[row, row, wsp, pl.BlockSpec((tr, f), lambda i: (i, 1))], out_specs=[row, row, wsp],
        out_shape=[jax.ShapeDtypeStruct((s_, f), F32), jax.ShapeDtypeStruct((s_, f), MXU_DTYPE),
                   jax.ShapeDtypeStruct((1, f), F32)],
        compiler_params=_params("arbitrary"),
    )(y, z, w.reshape(1, f), dout)


def _ffn_fwd(vv, w_gate, w_up, name="ffn_gate_up"):
    s_, d = vv.shape
    nb, f8, _ = w_gate.shape
    tm = _pick(s_, (1024, 512, 256, 128))

    def body(v_ref, wg_ref, wu_ref, g_ref, u_ref, a_ref):
        for rs in _row_slices(tm, 16):
            a = _mx(v_ref[rs, :])
            g = _dot(a, _mx(wg_ref[...]), 1, 1)
            u = _dot(a, _mx(wu_ref[...]), 1, 1)
            s = _sigmoid(g)
            gs = g * s
            g_ref[rs, :] = (u * (s * (1.0 + g * (1.0 - s)))).astype(g_ref.dtype)
            u_ref[rs, :] = gs.astype(u_ref.dtype)
            a_ref[rs, :] = (gs * u).astype(a_ref.dtype)

    wsp = pl.BlockSpec((None, f8, d), lambda j, i: (j, 0, 0))
    osp = pl.BlockSpec((None, tm, f8), lambda j, i: (j, i, 0))
    return pl.pallas_call(
        body, name=name, grid=(nb, s_ // tm),
        in_specs=[pl.BlockSpec((tm, d), lambda j, i: (i, 0)), wsp, wsp], out_specs=[osp] * 3,
        out_shape=[jax.ShapeDtypeStruct((nb, s_, f8), MXU_DTYPE)] * 3,
        compiler_params=_params("parallel", "parallel"),
    )(vv, w_gate, w_up)


def _ffn_bwd_act(dffn, w_down, gate, up, name="ffn_d_act"):
    s_, d = dffn.shape
    nb, f8, _ = w_down.shape
    tm = _pick(s_, (1024, 512, 256, 128))

    def body(d_ref, w_ref, g_ref, u_ref, dg_ref, du_ref):
        for rs in _row_slices(tm, 16):
            dact = _dot(_mx(d_ref[rs, :]), _mx(w_ref[...]), 1, 1)
            dg_ref[rs, :] = (dact * g_ref[rs, :].astype(F32)).astype(dg_ref.dtype)
            du_ref[rs, :] = (dact * u_ref[rs, :].astype(F32)).astype(du_ref.dtype)

    osp = pl.BlockSpec((None, tm, f8), lambda i, j: (j, i, 0))
    return pl.pallas_call(
        body, name=name, grid=(s_ // tm, nb),
        in_specs=[pl.BlockSpec((tm, d), lambda i, j: (i, 0)), pl.BlockSpec((None, f8, d), lambda i, j: (j, 0, 0)),
                  osp, osp],
        out_specs=[osp, osp], out_shape=[jax.ShapeDtypeStruct((nb, s_, f8), MXU_DTYPE)] * 2,
        compiler_params=_params("parallel", "parallel"),
    )(dffn, w_down, gate, up)


def _ffn_bwd_in(dgate, w_gate, dup, w_up, name="ffn_d_in"):
    nb, s_, f8 = dgate.shape
    d = w_gate.shape[2]
    tm = _pick(s_, (1024, 512, 256, 128))
    tn = _pick(d, (1024, 512, 256, 128))
    per = 2
    steps = nb // per

    def body(*refs):
        ins, o_ref, acc = refs[:4 * per], refs[4 * per], refs[4 * per + 1]
        j = pl.program_id(2)

        @pl.when(j == 0)
        def _():
            acc[...] = jnp.zeros_like(acc)

        for rs in _row_slices(tm, 16):
            part = None
            for t in range(per):
                dg_ref, wg_ref, du_ref, wu_ref = ins[4 * t:4 * t + 4]
                d_ = (_dot(_mx(dg_ref[rs, :]), _mx(wg_ref[...]), 1, 0)
                      + _dot(_mx(du_ref[rs, :]), _mx(wu_ref[...]), 1, 0))
                part = d_ if part is None else part + d_
            acc[rs, :] += part

        @pl.when(j == steps - 1)
        def _():
            o_ref[...] = acc[...]

    def specs(t):
        asp = pl.BlockSpec((None, tm, f8), lambda i, n, j: (j * per + t, i, 0))
        wsp = pl.BlockSpec((None, f8, tn), lambda i, n, j: (j * per + t, 0, n))
        return [asp, wsp, asp, wsp]

    return pl.pallas_call(
        body, name=name, grid=(s_ // tm, d // tn, steps),
        in_specs=[sp for t in range(per) for sp in specs(t)],
        out_specs=pl.BlockSpec((tm, tn), lambda i, n, j: (i, n)),
        out_shape=jax.ShapeDtypeStruct((s_, d), F32), scratch_shapes=[pltpu.VMEM((tm, tn), F32)],
        compiler_params=_params("parallel", "parallel", "arbitrary"),
    )(*((dgate, w_gate, dup, w_up) * per))


def _adam_math(g, w, m, v):
    m2 = ADAM_B1 * m + (1.0 - ADAM_B1) * g
    v2 = ADAM_B2 * v + (1.0 - ADAM_B2) * (g * g)
    m_hat = m2 / (1.0 - ADAM_B1 ** ADAM_STEP)
    v_hat = v2 / (1.0 - ADAM_B2 ** ADAM_STEP)
    delta = -ADAM_LR * (m_hat / (jnp.sqrt(v_hat) + ADAM_EPS) + ADAM_WD * w)
    return delta, m2, v2


def _adamw(parts, own, me, w, m, v, name="adamw"):
    nd, r_, c = parts.shape
    tr = _pick(r_, (128, 64, 32, 16))
    tc = c
    if tr == r_ and r_ > 128:
        tc = _pick(c, (256, 128))

    def body(me_ref, p_ref, own_ref, w_ref, m_ref, v_ref, g_ref, d_ref, m2_ref, v2_ref):
        mine = me_ref[0]
        g = jnp.zeros((tr, tc), F32)
        for i in range(nd):
            g = g + jnp.where(mine == i, own_ref[...], p_ref[i]).astype(F32)
        delta, m2, v2 = _adam_math(g, w_ref[...], m_ref[...], v_ref[...])
        g_ref[...] = g
        d_ref[...] = delta
        m2_ref[...] = m2
        v2_ref[...] = v2

    row = pl.BlockSpec((tr, tc), lambda i, j, me_: (i, j))
    gs = pltpu.PrefetchScalarGridSpec(
        num_scalar_prefetch=1, grid=(r_ // tr, c // tc),
        in_specs=[pl.BlockSpec((nd, tr, tc), lambda i, j, me_: (0, i, j)),
                  pl.BlockSpec((None, tr, tc), lambda i, j, me_: (me_[0], i, j)), row, row, row],
        out_specs=[row] * 4)
    return pl.pallas_call(
        body, name=name, grid_spec=gs, out_shape=[jax.ShapeDtypeStruct((r_, c), F32)] * 4,
        compiler_params=_params("parallel", "parallel"),
    )(me, parts, own, w, m, v)


def _adamw_small(parts, w, m, v, name="adamw_small"):
    nd = parts.shape[0]

    def body(p_ref, w_ref, m_ref, v_ref, g_ref, d_ref, m2_ref, v2_ref):
        g = p_ref[0]
        for i in range(1, nd):
            g = g + p_ref[i]
        delta, m2, v2 = _adam_math(g, w_ref[...], m_ref[...], v_ref[...])
        g_ref[...] = g
        d_ref[...] = delta
        m2_ref[...] = m2
        v2_ref[...] = v2

    return pl.pallas_call(
        body, name=name, out_shape=[jax.ShapeDtypeStruct(w.shape, F32)] * 4,
        compiler_params=pltpu.CompilerParams(vmem_limit_bytes=VMEM_LIMIT_BYTES),
    )(parts, w, m, v)


_HBM = pl.BlockSpec(memory_space=pltpu.HBM)
_MESH = pl.DeviceIdType.MESH


def _all_gather(xs, name):
    na = len(xs)

    def body(*refs):
        x_refs, out_refs = refs[:na], refs[na:2 * na]
        send_sems, recv_sems, local_sems = refs[2 * na:]
        x, y, c = lax.axis_index("x"), lax.axis_index("y"), lax.axis_index("c")
        me, sibling = (x, y, c), (x, y, 1 - c)
        near = [(1 - x, y), (x, 1 - y)]
        chips = near + [(1 - x, 1 - y)]
        relay_from = (x + c * (1 - 2 * x), y + (1 - c) * (1 - 2 * y))
        relay_to = (x + (1 - c) * (1 - 2 * x), y + c * (1 - 2 * y))

        def slot(a, px, py, pc):
            return out_refs[a].at[4 * px + 2 * py + pc]

        def copy(a, k, block, to, src=None):
            return pltpu.make_async_remote_copy(
                src_ref=slot(a, *block) if src is None else src, dst_ref=slot(a, *block),
                send_sem=send_sems.at[a, k], recv_sem=recv_sems.at[a, k], device_id=to, device_id_type=_MESH)

        mine = [pltpu.make_async_copy(x_refs[a], slot(a, *me), local_sems.at[a]) for a in range(na)]
        started = []
        for a in range(na):
            mine[a].start()
            first = [copy(a, 0, me, sibling, src=x_refs[a])]
            first += [copy(a, 1 + j, me, (*chip, c), src=x_refs[a]) for j, chip in enumerate(near)]
            for cp in first:
                cp.start()
            started += first
        for a in range(na):
            for j, chip in enumerate(chips):
                copy(a, 1 + j, (*chip, c), me).wait_recv()
                fwd = copy(a, 4 + j, (*chip, c), sibling)
                fwd.start()
                started.append(fwd)
                if j == len(near) - 1:
                    relay = copy(a, 1 + len(near), (*relay_from, c), (*relay_to, c))
                    relay.start()
                    started.append(relay)
        for a in range(na):
            copy(a, 0, sibling, me).wait_recv()
            for j, chip in enumerate(chips):
                copy(a, 4 + j, (*chip, 1 - c), me).wait_recv()
        for cp in started:
            cp.wait_send()
        for cp in mine:
            cp.wait()

    return pl.pallas_call(
        body, name=name, out_shape=[jax.ShapeDtypeStruct((N_DEV,) + t.shape, t.dtype) for t in xs],
        in_specs=[_HBM] * na, out_specs=[_HBM] * na,
        scratch_shapes=[pltpu.SemaphoreType.DMA((na, 7)), pltpu.SemaphoreType.DMA((na, 7)),
                        pltpu.SemaphoreType.DMA((na,))],
    )(*xs)


_SEM = pl.BlockSpec(memory_space=pltpu.SEMAPHORE)
_EFFECT = pltpu.SideEffectType.DATAFLOW_SIDE_EFFECTING


def _peers(x, y, c):
    out = []
    for k in range(1, N_DEV):
        px = 1 - x if k & 4 else x
        py = 1 - y if k & 2 else y
        pc = 1 - c if k & 1 else c
        out.append(((px, py, pc), 4 * px + 2 * py + pc))
    return out


def _push_copies(scatter, src_refs, land_refs, send_sems, recv_sems):
    x, y, c = lax.axis_index("x"), lax.axis_index("y"), lax.axis_index("c")
    me = 4 * x + 2 * y + c
    pairs = []
    for a, (src, land) in enumerate(zip(src_refs, land_refs)):
        for k, (peer, slot) in enumerate(_peers(x, y, c)):
            out_src = src.at[slot] if scatter else src
            si = a * (N_DEV - 1) + k
            send = pltpu.make_async_remote_copy(src_ref=out_src, dst_ref=land.at[me], send_sem=send_sems.at[si],
                                                recv_sem=recv_sems.at[si], device_id=peer, device_id_type=_MESH)
            recv = pltpu.make_async_remote_copy(src_ref=out_src, dst_ref=land.at[slot], send_sem=send_sems.at[si],
                                                recv_sem=recv_sems.at[si], device_id=peer, device_id_type=_MESH)
            pairs.append((send, recv))
    return pairs


def _push_start(srcs, scatter, dep, name):
    na = len(srcs)
    shapes = [t.shape[1:] if scatter else t.shape for t in srcs]
    lands = [pltpu.with_memory_space_constraint(lax.empty((N_DEV,) + s, t.dtype), pltpu.HBM) for s, t in zip(shapes, srcs)]

    def body(*refs):
        src_refs, land_refs = refs[:na], refs[na:2 * na]
        send_sems, recv_sems = refs[2 * na + 1], refs[2 * na + 2]
        token = refs[-1]
        for send, _ in _push_copies(scatter, src_refs, land_refs, send_sems, recv_sems):
            send.start()
        token[...] = jnp.zeros_like(token)

    sem = pltpu.SemaphoreType.DMA((na * (N_DEV - 1),))
    outs = pl.pallas_call(
        body, name=name,
        out_shape=(sem, sem) + tuple(pltpu.HBM(t.shape, t.dtype) for t in srcs)
        + tuple(pltpu.HBM(t.shape, t.dtype) for t in lands) + (jax.ShapeDtypeStruct((8, LANE), F32),),
        in_specs=[_HBM] * (2 * na) + [pl.BlockSpec(memory_space=pl.ANY)],
        out_specs=(_SEM, _SEM) + (_HBM,) * (2 * na) + (pl.BlockSpec(memory_space=pltpu.VMEM),),
        input_output_aliases={i: 2 + i for i in range(2 * na)},
        compiler_params=pltpu.CompilerParams(has_side_effects=_EFFECT),
    )(*[pltpu.with_memory_space_constraint(t, pltpu.HBM) for t in srcs], *lands, dep)
    return outs[0], outs[1], outs[2:2 + na], outs[2 + na:2 + 2 * na], outs[-1]


def _push_wait(send_sems, recv_sems, src_thru, land_thru, scatter, after, name):
    na = len(src_thru)

    def body(*refs):
        src_refs, land_refs = refs[:na], refs[na:2 * na]
        ssem, rsem = refs[2 * na], refs[2 * na + 1]
        for send, recv in _push_copies(scatter, src_refs, land_refs, ssem, rsem):
            send.wait_send()
            recv.wait_recv()

    outs = pl.pallas_call(
        body, name=name,
        out_shape=tuple(pltpu.HBM(t.shape, t.dtype) for t in src_thru) + tuple(pltpu.HBM(t.shape, t.dtype) for t in land_thru),
        in_specs=[_HBM] * (2 * na) + [_SEM, _SEM, pl.BlockSpec(memory_space=pl.ANY)],
        out_specs=(_HBM,) * (2 * na),
        input_output_aliases={i: i for i in range(2 * na)},
        compiler_params=pltpu.CompilerParams(has_side_effects=_EFFECT),
    )(*src_thru, *land_thru, send_sems, recv_sems, after)
    return outs[:na], outs[na:]


def _exchange_behind(srcs, scatter, dep, name):
    send_sems, recv_sems, thru, lands, token = _push_start(srcs, scatter, dep, name + "_start")

    def finish(after, place=True):
        src_done, land_done = _push_wait(send_sems, recv_sems, thru, lands, scatter, after, name + "_wait")
        if not place:
            return land_done, src_done
        return _place_own(land_done, src_done, scatter, name + "_own")

    return token[0, 0], finish


def _place_own(lands, srcs, scatter, name):
    me = (4 * lax.axis_index("x") + 2 * lax.axis_index("y") + lax.axis_index("c")).astype(jnp.int32).reshape(1)
    outs = []
    for a, (land, src) in enumerate(zip(lands, srcs)):
        r_, c_ = land.shape[1:]
        tr = _pick(r_, (512, 256, 128, 64, 32, 16))

        def body(me_ref, land_ref, src_ref, out_ref):
            out_ref[...] = src_ref[...]

        src_spec = (pl.BlockSpec((None, tr, c_), lambda i, me_: (me_[0], i, 0)) if scatter
                    else pl.BlockSpec((tr, c_), lambda i, me_: (i, 0)))
        gs = pltpu.PrefetchScalarGridSpec(
            num_scalar_prefetch=1, grid=(r_ // tr,),
            in_specs=[pl.BlockSpec(memory_space=pl.ANY), src_spec],
            out_specs=pl.BlockSpec((None, tr, c_), lambda i, me_: (me_[0], i, 0)))
        outs.append(pl.pallas_call(
            body, name=f"{name}_{a}", grid_spec=gs, out_shape=jax.ShapeDtypeStruct(land.shape, land.dtype),
            input_output_aliases={1: 0}, compiler_params=_params("arbitrary"),
        )(me, land, src))
    return outs


_TRANSPOSED = ("w_in", "w_uq", "w_gate", "w_up")
_CQKV = (0, Q_RANK + KV_RANK)
_KR = (_CQKV[1], _CQKV[1] + ROPE)
_Z = (_KR[1], _KR[1] + SSD_W)
_XBC = (_Z[1], _Z[1] + CONV_DIM)
_DT = (_XBC[1], _XBC[1] + SSD_H)


def _win_segments(w_in_t):
    w = w_in_t.reshape(D_IN, D_MODEL)
    small = jnp.concatenate([w[_KR[0]:_KR[1]], w[_DT[0]:_DT[1]],
                             jnp.zeros((LANE - ROPE - SSD_H, D_MODEL), w.dtype)], axis=0)
    return w[_CQKV[0]:_CQKV[1]], w[_Z[0]:_Z[1]], w[_XBC[0]:_XBC[1]], small


def _win_from_segments(g_cqkv, g_z, g_xbc, g_small):
    w = jnp.concatenate([g_cqkv, g_small[:ROPE], g_z, g_xbc, g_small[ROPE:ROPE + SSD_H]], axis=0)
    return w.reshape(N_DEV, D_IN // N_DEV, D_MODEL)


_SMALL = (("q_norm_w", 512), ("kv_norm_w", 512), ("conv_b", CONV_DIM), ("dt_bias", SSD_H), ("a_log", SSD_H),
          ("d_skip", SSD_H), ("ssd_norm_w", SSD_W), ("attn_out_norm_w", 1024), ("pre_mix_norm_w", D_MODEL),
          ("post_mix_norm_w", D_MODEL), ("pre_ffn_norm_w", D_MODEL), ("post_ffn_norm_w", D_MODEL),
          ("conv_w", CONV_K * CONV_DIM))
_SMALL_ROWS = -(-(sum(-(-n // LANE) for _, n in _SMALL) + 1) // 8) * 8


def _pack_small(vals):
    rows = []
    for name, n in _SMALL:
        v = vals[name].reshape(-1).astype(F32)
        pad = -(-n // LANE) * LANE
        rows.append(jnp.pad(v, (0, pad - n)).reshape(-1, LANE))
    m = jnp.concatenate(rows, axis=0)
    return jnp.pad(m, ((0, _SMALL_ROWS - m.shape[0]), (0, 0)))


def _unpack_small(m):
    out, r = {}, 0
    for name, n in _SMALL:
        nr = -(-n // LANE)
        out[name] = m[r:r + nr].reshape(-1)[:n]
        r += nr
    return out


def _head_row(v):
    return jnp.pad(v.reshape(1, -1).astype(F32), ((0, 0), (HEAD_LANE, LANE - HEAD_LANE - v.shape[-1])))


def _local_step(x, positions, target, wg, small, weights, on_grads):
    w_cqkv, w_z, w_xbc, w_small = _win_segments(wg["w_in"])
    conv_w = wg["conv_w"]
    conv_b = small["conv_b"].reshape(1, CONV_DIM)
    qkv_norm_w = jnp.concatenate([small["q_norm_w"], small["kv_norm_w"]])
    attn_norm_w = small["attn_out_norm_w"].reshape(1, HEADS * VDIM)
    scale = QK ** -0.5

    inv_freq = ROPE_THETA ** (-jnp.arange(0, ROPE, 2, dtype=F32) / ROPE)
    ang = positions.astype(F32)[:, None] * inv_freq
    cos2 = jnp.tile(jnp.cos(ang), (1, 2))
    sin2 = jnp.tile(jnp.sin(ang), (1, 2))

    u = _rms_fwd(x, small["pre_mix_norm_w"], out_dtype=MXU_DTYPE, name="pre_mix_norm")
    cqkv = _mm(u, w_cqkv, "nt", name="in_proj_qkv")
    z = _mm(u, w_z, "nt", name="in_proj_z")
    xbc = _mm(u, w_xbc, "nt", name="in_proj_xbc")
    sm = _mm(u, w_small, "nt", name="in_proj_small")

    w_uq, w_ukv = weights("qkv_up", cqkv)
    qkvn = _rms_fwd(cqkv, qkv_norm_w, groups=2, out_dtype=MXU_DTYPE, name="qkv_norm")
    q_h = _q_up(qkvn, w_uq, cos2, sin2, scale)
    k_h, v_h = _kv_up(qkvn, w_ukv, sm, cos2, sin2)
    o_h, lse = _flash_fwd(q_h, k_h, v_h)
    cat = _hnorm_fwd(o_h, attn_norm_w, D_MODEL)
    w_out = weights("out", o_h)[0].reshape(D_MODEL, D_MODEL)

    xbc_act = _conv_fwd(xbc, conv_w, conv_b)
    dtt = jnp.transpose(sm[:, HEAD_LANE:HEAD_LANE + SSD_H])
    ssd_args = (xbc_act, sm, dtt, _head_row(small["dt_bias"]), small["dt_bias"].reshape(SSD_H, 1),
                _head_row(small["a_log"]), small["a_log"].reshape(SSD_H, 1),
                jnp.broadcast_to(small["d_skip"].reshape(SSD_H, 1), (SSD_H, SSD_P)).reshape(SSD_PAIRS, 1, LANE))
    y_ssd, prev = _ssd_fwd(*ssd_args)
    cat = _gated_norm_fwd(y_ssd, z, small["ssd_norm_w"], cat)

    mix = _mm(cat, w_out, "nn", name="out_proj")
    h1, vv = _norm_res_norm(mix, x, small["post_mix_norm_w"], small["pre_ffn_norm_w"])

    w_gate, w_up = weights("ffn_in", mix)
    gate, up, act = _ffn_fwd(vv, w_gate, w_up)
    w_down, = weights("ffn_out", act)
    ffn = _mm(act, w_down, "nn", a_blk=True, b_blk=True, fuse=N_DEV, tm_max=512, name="ffn_down")
    loss_blk, dy, dffn, g_post_ffn = _loss_head(ffn, h1, target, small["post_ffn_norm_w"])

    g_down = _mm(act, dffn, "tn", a_blk=True, out_blk=True, out_dtype=MXU_DTYPE, name="g_down")
    dgate, dup = _ffn_bwd_act(dffn, w_down, gate, up)
    dvv = _ffn_bwd_in(dgate, w_gate, dup, w_up)
    g_gate = _mm(dgate, vv, "tn", a_blk=True, out_blk=True, out_dtype=MXU_DTYPE, name="g_gate")
    g_up = _mm(dup, vv, "tn", a_blk=True, out_blk=True, out_dtype=MXU_DTYPE, name="g_up")
    pre_ffn_w = small["pre_ffn_norm_w"] + on_grads("ffn", [g_gate, g_up, g_down])
    dh1, dmix, g_pre_ffn, g_post_mix = _norm_res_norm_bwd(h1, pre_ffn_w, dvv, dy, mix, small["post_mix_norm_w"])

    dcat = _mm(dmix, w_out, "nt", name="d_cat")
    g_out = _mm(cat, dmix, "tn", out_dtype=MXU_DTYPE, name="g_out")

    do_h, delta, g_attn_norm = _hnorm_bwd(o_h, attn_norm_w, dcat)
    dq_h, dk_h, dv_h = _flash_bwd(q_h, k_h, v_h, do_h, lse, delta)
    dq = _q_prep(dq_h, cos2, -sin2, scale, name="dq_post")

    dy_ssd, dz, g_ssd_norm = _gated_norm_bwd(y_ssd, z, small["ssd_norm_w"], dcat)
    dxbc_act, ddt, dpar = _ssd_bwd(*ssd_args, prev, dy_ssd)
    dkv, dsm = _dkv_post(dk_h, dv_h, ddt, cos2, -sin2)
    dpre, dwb = _conv_bwd_pre(xbc, conv_w, conv_b, dxbc_act)
    dxbc = _conv_bwd_in(dpre, conv_w)

    dqn = _mm(dq, w_uq, "nn", a_blk=True, b_blk=True, fuse=HEADS, name="d_qn")
    dkvn = _mm(dkv, w_ukv, "nt", a_blk=True, b_blk=True, fuse=HEADS, name="d_kvn")
    g_uq = _mm(dq, qkvn, "tn", a_blk=True, out_blk=True, b_cols=(0, Q_RANK), out_dtype=MXU_DTYPE, name="g_uq")
    g_ukv = _mm(qkvn, dkv, "tn", b_blk=True, out_blk=True, a_cols=(Q_RANK, KV_RANK), out_dtype=MXU_DTYPE, name="g_ukv")
    heads_token = on_grads("heads", [g_uq, g_ukv, g_out.reshape(N_DEV, D_MODEL // N_DEV, D_MODEL)])
    dcqkv, g_qkv_norm = _rms_bwd(cqkv, qkv_norm_w + heads_token, [dqn, dkvn], out_dtype=MXU_DTYPE, name="qkv_norm_bwd")

    g_in = _win_from_segments(_mm(dcqkv, u, "tn", out_dtype=MXU_DTYPE, name="g_in_qkv"),
                              _mm(dz, u, "tn", out_dtype=MXU_DTYPE, name="g_in_z"),
                              _mm(dxbc, u, "tn", out_dtype=MXU_DTYPE, name="g_in_xbc"),
                              _mm(dsm, u, "tn", out_dtype=MXU_DTYPE, name="g_in_small"))
    in_token = on_grads("in", [g_in])
    du = _mm_sum([dsm + in_token.astype(dsm.dtype), dcqkv, dz, dxbc], [w_small, w_cqkv, w_z, w_xbc], name="d_u")
    dx, g_pre_mix = _rms_bwd(x, small["pre_mix_norm_w"], [du], res=dh1, name="pre_mix_norm_bwd")

    hl = slice(HEAD_LANE, HEAD_LANE + SSD_H)
    g_small = {"q_norm_w": g_qkv_norm[0, :Q_RANK], "kv_norm_w": g_qkv_norm[0, Q_RANK:], "conv_b": dwb[CONV_K],
               "dt_bias": dpar[0, hl], "a_log": dpar[1, hl], "d_skip": dpar[2, hl], "ssd_norm_w": g_ssd_norm,
               "attn_out_norm_w": g_attn_norm, "pre_mix_norm_w": g_pre_mix, "post_mix_norm_w": g_post_mix,
               "pre_ffn_norm_w": g_pre_ffn, "post_ffn_norm_w": g_post_ffn, "conv_w": dwb[:CONV_K]}
    return loss_blk[0, 0], dx, g_small


_WEIGHT_ORDER = ("w_in", "q_norm_w", "w_uq", "kv_norm_w", "w_ukv", "conv_w", "conv_b", "dt_bias", "a_log", "d_skip",
                 "ssd_norm_w", "attn_out_norm_w", "w_out", "pre_mix_norm_w", "post_mix_norm_w", "pre_ffn_norm_w",
                 "post_ffn_norm_w", "w_gate", "w_up", "w_down")


def kernel(x, positions, w_in, q_norm_w, w_uq, kv_norm_w, w_ukv, conv_w, conv_b, dt_bias, a_log, d_skip, ssd_norm_w, attn_out_norm_w, w_out, pre_mix_norm_w, post_mix_norm_w, pre_ffn_norm_w, post_ffn_norm_w, w_gate, w_up, w_down, loss_target, m_w_in, m_q_norm_w, m_w_uq, m_kv_norm_w, m_w_ukv, m_conv_w, m_conv_b, m_dt_bias, m_a_log, m_d_skip, m_ssd_norm_w, m_attn_out_norm_w, m_w_out, m_pre_mix_norm_w, m_post_mix_norm_w, m_pre_ffn_norm_w, m_post_ffn_norm_w, m_w_gate, m_w_up, m_w_down, v_w_in, v_q_norm_w, v_w_uq, v_kv_norm_w, v_w_ukv, v_conv_w, v_conv_b, v_dt_bias, v_a_log, v_d_skip, v_ssd_norm_w, v_attn_out_norm_w, v_w_out, v_pre_mix_norm_w, v_post_mix_norm_w, v_pre_ffn_norm_w, v_post_ffn_norm_w, v_w_gate, v_w_up, v_w_down):
    w = dict(w_in=w_in, q_norm_w=q_norm_w, w_uq=w_uq, kv_norm_w=kv_norm_w, w_ukv=w_ukv, conv_w=conv_w, conv_b=conv_b,
             dt_bias=dt_bias, a_log=a_log, d_skip=d_skip, ssd_norm_w=ssd_norm_w, attn_out_norm_w=attn_out_norm_w,
             w_out=w_out, pre_mix_norm_w=pre_mix_norm_w, post_mix_norm_w=post_mix_norm_w,
             pre_ffn_norm_w=pre_ffn_norm_w, post_ffn_norm_w=post_ffn_norm_w, w_gate=w_gate, w_up=w_up, w_down=w_down)
    m = dict(w_in=m_w_in, q_norm_w=m_q_norm_w, w_uq=m_w_uq, kv_norm_w=m_kv_norm_w, w_ukv=m_w_ukv, conv_w=m_conv_w,
             conv_b=m_conv_b, dt_bias=m_dt_bias, a_log=m_a_log, d_skip=m_d_skip, ssd_norm_w=m_ssd_norm_w,
             attn_out_norm_w=m_attn_out_norm_w, w_out=m_w_out, pre_mix_norm_w=m_pre_mix_norm_w,
             post_mix_norm_w=m_post_mix_norm_w, pre_ffn_norm_w=m_pre_ffn_norm_w, post_ffn_norm_w=m_post_ffn_norm_w,
             w_gate=m_w_gate, w_up=m_w_up, w_down=m_w_down)
    v = dict(w_in=v_w_in, q_norm_w=v_q_norm_w, w_uq=v_w_uq, kv_norm_w=v_kv_norm_w, w_ukv=v_w_ukv, conv_w=v_conv_w,
             conv_b=v_conv_b, dt_bias=v_dt_bias, a_log=v_a_log, d_skip=v_d_skip, ssd_norm_w=v_ssd_norm_w,
             attn_out_norm_w=v_attn_out_norm_w, w_out=v_w_out, pre_mix_norm_w=v_pre_mix_norm_w,
             post_mix_norm_w=v_post_mix_norm_w, pre_ffn_norm_w=v_pre_ffn_norm_w, post_ffn_norm_w=v_post_ffn_norm_w,
             w_gate=v_w_gate, w_up=v_w_up, w_down=v_w_down)
    w, m, v = ({k: t[0] for k, t in d.items()} for d in (w, m, v))
    me = 4 * lax.axis_index("x") + 2 * lax.axis_index("y") + lax.axis_index("c")
    groups = {"qkv_up": ("w_uq", "w_ukv"), "out": ("w_out",), "ffn_in": ("w_gate", "w_up"), "ffn_out": ("w_down",)}
    cshard = CONV_DIM // N_DEV
    for name in _TRANSPOSED:
        w[name], m[name], v[name] = w[name].T, m[name].T, v[name].T

    shards = [w["w_in"].astype(MXU_DTYPE),
              jnp.stack(_split3(w["conv_w"])).reshape(3 * CONV_K, cshard).astype(MXU_DTYPE)]
    w_in_g, cw = _all_gather(shards, name="gather_weights")
    cw = cw.astype(F32).reshape(N_DEV, 3, CONV_K, cshard)
    wg = {"w_in": w_in_g, "conv_w": jnp.transpose(cw[:, 0] + cw[:, 1] + cw[:, 2], (1, 0, 2)).reshape(CONV_K, CONV_DIM)}
    arriving, dep, started = {}, wg["conv_w"], jnp.zeros((), F32)
    small = {name: w[name] for name, _ in _SMALL if name != "conv_w"}
    for group in ("qkv_up", "out", "ffn_in", "ffn_out"):
        token, arriving[group] = _exchange_behind([w[name].astype(MXU_DTYPE) for name in groups[group]], False,
                                                  dep, group + "_weights")
        started = started + token
        dep = jnp.zeros((8, LANE), F32) + started
    small["pre_mix_norm_w"] = small["pre_mix_norm_w"] + started

    leaving = {}

    def on_grads(group, gs):
        token, leaving[group] = _exchange_behind(gs, True, jnp.zeros((8, LANE), F32), group + "_grads")
        return token

    loss_local, dx, g_small = _local_step(x[0], positions[0], loss_target[0], wg, small,
                                          lambda group, after: arriving[group](after), on_grads)
    recv = {}
    for group, names in (("ffn", ("w_gate", "w_up", "w_down")), ("heads", ("w_uq", "w_ukv", "w_out")), ("in", ("w_in",))):
        recv.update(zip(names, zip(*leaving[group](dx, place=False))))
    grads, deltas, new_m, new_v = {}, {}, {}, {}
    me1 = me.astype(jnp.int32).reshape(1)
    for name, (parts, own) in recv.items():
        outs = _adamw(parts, own, me1, w[name], m[name], v[name], name="adamw_" + name)
        if name in _TRANSPOSED:
            outs = [t.T for t in outs]
        grads[name], deltas[name], new_m[name], new_v[name] = outs

    def embed(t):
        return lax.dynamic_update_slice(jnp.zeros((CONV_K, CONV_DIM), F32), t, (0, me * cshard))

    mine_s = _pack_small(g_small).at[_SMALL_ROWS - 1, 0].set(loss_local)
    parts_s = _all_gather([mine_s], name="gather_small_grads")[0]
    packs = [_pack_small({**{n_: d[n_] for n_, _ in _SMALL if n_ != "conv_w"}, "conv_w": embed(d["conv_w"])})
             for d in (w, m, v)]
    summed = _adamw_small(parts_s, *packs)
    loss = summed[0][_SMALL_ROWS - 1, 0]
    outs = [_unpack_small(t) for t in summed]
    for name, n in _SMALL:
        for dst, src in zip((grads, deltas, new_m, new_v), outs):
            if name == "conv_w":
                dst[name] = lax.dynamic_slice(src[name].reshape(CONV_K, CONV_DIM), (0, me * cshard), (CONV_K, cshard))
            else:
                dst[name] = src[name]

    def lead(d):
        return [d[name][None] for name in _WEIGHT_ORDER]

    return (loss, dx[None], *lead(grads), *lead(deltas), *lead(new_m), *lead(new_v))
```

```python
import numpy as np

import jax
import jax.numpy as jnp
from jax import lax
from jax.experimental import pallas as pl
from jax.experimental.pallas import tpu as pltpu

F32 = jnp.float32
BF16 = jnp.bfloat16
MXU_DTYPE = jnp.bfloat16
EPS = 1e-6
VMEM_LIMIT_BYTES = 48 * 1024 * 1024
VMEM_LIMIT_WIDE_BYTES = 56 * 1024 * 1024
K_TILE_MAX = 2048

N_DEV = 8
D_MODEL = 2048
Q_RANK = 512
KV_RANK = 512
ROPE = 64
HALF = ROPE // 2
HEADS = 8
NOPE = 128
VDIM = 128
QK = NOPE + ROPE
SSD_W = 1024
SSD_H = 16
SSD_P = 64
SSD_G = 2
SSD_E = SSD_H // SSD_G
SSD_N = 128
CHUNK = 128
CONV_K = 4
CONV_DIM = SSD_W + 2 * SSD_G * SSD_N
B_OFF = SSD_W
C_OFF = SSD_W + SSD_G * SSD_N
D_FF = 5632
D_IN = Q_RANK + KV_RANK + ROPE + SSD_W + CONV_DIM + SSD_H
ROPE_THETA = 10000.0
LANE = 128
HEAD_LANE = ROPE

ADAM_LR = 0.001
ADAM_B1 = 0.9
ADAM_B2 = 0.999
ADAM_EPS = 1e-08
ADAM_WD = 0.01
ADAM_STEP = 10


def _pick(n, cands):
    for c in cands:
        if n % c == 0:
            return c
    return n


def _params(*sem, vmem=VMEM_LIMIT_BYTES):
    return pltpu.CompilerParams(dimension_semantics=sem, vmem_limit_bytes=vmem)


def _sigmoid(x):
    return 1.0 / (1.0 + jnp.exp(-x))


def _silu(x):
    return x * _sigmoid(x)


def _dsilu(x):
    s = _sigmoid(x)
    return s * (1.0 + x * (1.0 - s))


def _softplus(x):
    e = jnp.exp(-jnp.abs(x))
    small = e * (1.0 - e * (0.5 - e * (1.0 / 3.0)))
    return jnp.maximum(x, 0.0) + jnp.where(e < 0.01, small, jnp.log(1.0 + e))


def _dot(a, b, ca, cb):
    return lax.dot_general(a, b, (((ca,), (cb,)), ((), ())), preferred_element_type=F32)


def _mx(v):
    return v.astype(MXU_DTYPE)


def _split3(a):
    hi = a.astype(BF16)
    r1 = a - hi.astype(F32)
    mid = r1.astype(BF16)
    lo = (r1 - mid.astype(F32)).astype(BF16)
    return hi, mid, lo


def _exact_dot(a, b, ca, cb, split_a):
    if split_a:
        return sum(_dot(p, b, ca, cb) for p in _split3(a))
    return sum(_dot(a, p, ca, cb) for p in _split3(b))


MM_ROW_GROUPS = 4


def _row_slices(tm, align):
    ng = MM_ROW_GROUPS
    while ng > 1 and (tm % ng or (tm // ng) % align):
        ng //= 2
    return [slice(g * (tm // ng), (g + 1) * (tm // ng)) for g in range(ng)]


def _mm(a, b, mode, *, a_blk=False, b_blk=False, out_blk=False, a_cols=None, b_cols=None, add=None, out_dtype=F32,
        fuse=1, wide=False, tm_max=1024, name="mm"):
    a2, b2 = a.shape[-2:], b.shape[-2:]
    a_last = a2[1] if a_cols is None else a_cols[1]
    a_start = 0 if a_cols is None else a_cols[0]
    b_start = 0
    if b_cols is not None:
        assert mode != "nt"
        b_start, b2 = b_cols[0], (b2[0], b_cols[1])
    if mode == "nn":
        m, k, (k2, n) = a2[0], a_last, b2
    elif mode == "nt":
        m, k, (n, k2) = a2[0], a_last, b2
    else:
        k, m, (k2, n) = a2[0], a_last, b2
    assert k == k2, (a.shape, b.shape, mode)
    tm = _pick(m, tuple(c for c in (1024, 704, 512, 256, 128) if c <= tm_max))
    tn = _pick(n, ((2048,) if wide else ()) + (1024, 768, 704, 512, 256, 192, 128))
    k_max = 2 * K_TILE_MAX if mode == "tn" else K_TILE_MAX
    tk = k if k <= k_max else _pick(k, (K_TILE_MAX, 1024, 512))
    nk = k // tk
    jo = N_DEV if out_blk else 1
    reduce_blocks = a_blk and b_blk and not out_blk
    assert fuse == 1 or reduce_blocks
    jr = N_DEV // fuse if reduce_blocks else 1
    ca, cb = {"nn": (1, 0), "nt": (1, 1), "tn": (0, 0)}[mode]
    has_add = add is not None
    single = jr * nk == 1
    if mode == "tn":
        assert a_start % tm == 0
        a_block, a_idx = (tk, tm), (lambda i, kk: (kk, i + a_start // tm))
    else:
        assert a_start % tk == 0
        a_block, a_idx = (tm, tk), (lambda i, kk: (i, kk + a_start // tk))
    assert b_start % tn == 0
    b_block, b_idx = (((tn, tk), (lambda nn_, kk: (nn_, kk))) if mode == "nt"
                      else ((tk, tn), (lambda nn_, kk: (kk, nn_ + b_start // tn))))

    def blk_specs(blocked, block, idx, of_a, t):
        def pos(o, i, nn_, kk):
            return idx(i, kk) if of_a else idx(nn_, kk)
        if blocked:
            return pl.BlockSpec((None,) + block,
                                lambda o, i, nn_, r, kk: ((o if out_blk else r * fuse + t),) + pos(o, i, nn_, kk))
        return pl.BlockSpec(block, lambda o, i, nn_, r, kk: pos(o, i, nn_, kk))

    a_specs = [blk_specs(a_blk, a_block, a_idx, True, t) for t in range(fuse)]
    b_specs = [blk_specs(b_blk, b_block, b_idx, False, t) for t in range(fuse)]
    o_spec = (pl.BlockSpec((None, tm, tn), lambda o, i, nn_, r, kk: (o, i, nn_)) if out_blk
              else pl.BlockSpec((tm, tn), lambda o, i, nn_, r, kk: (i, nn_)))

    groups = _row_slices(tm, LANE if mode == "tn" else 16)

    def body(*refs):
        a_refs, b_refs = refs[:fuse], refs[fuse:2 * fuse]
        add_ref = refs[2 * fuse] if has_add else None
        o_ref = refs[2 * fuse + 1] if has_add else refs[2 * fuse]

        def partial(rs):
            out = None
            for t in range(fuse):
                av = a_refs[t][:, rs] if mode == "tn" else a_refs[t][rs, :]
                d = _dot(_mx(av), _mx(b_refs[t][...]), ca, cb)
                out = d if out is None else out + d
            return out

        if single:
            for rs in groups:
                res = partial(rs)
                if has_add:
                    res = res + add_ref[rs, :]
                o_ref[rs, :] = res.astype(o_ref.dtype)
            return
        acc = refs[-1]
        r, kk = pl.program_id(3), pl.program_id(4)

        @pl.when(jnp.logical_and(r == 0, kk == 0))
        def _():
            acc[...] = jnp.zeros_like(acc)

        for rs in groups:
            acc[rs, :] += partial(rs)

        @pl.when(jnp.logical_and(r == jr - 1, kk == nk - 1))
        def _():
            res = acc[...]
            if has_add:
                res = res + add_ref[...]
            o_ref[...] = res.astype(o_ref.dtype)

    out_shape = ((N_DEV, m, n) if out_blk else (m, n))
    return pl.pallas_call(
        body, name=name, grid=(jo, m // tm, n // tn, jr, nk),
        in_specs=a_specs + b_specs + ([o_spec] if has_add else []), out_specs=o_spec,
        out_shape=jax.ShapeDtypeStruct(out_shape, out_dtype),
        scratch_shapes=[] if single else [pltpu.VMEM((tm, tn), F32)],
        compiler_params=_params("parallel", "parallel", "parallel", "arbitrary", "arbitrary"),
    )(*((a,) * fuse + (b,) * fuse + ((add,) if has_add else ())))


def _mm_sum(a_list, b_list, name="mm_sum"):
    m, n = a_list[0].shape[0], b_list[0].shape[1]
    ns = len(a_list)
    tm = _pick(m, (1024, 512, 256, 128))
    tn = _pick(n, (1024, 512, 256, 128))
    groups = _row_slices(tm, 16)

    def body(*refs):
        a_refs, b_refs, o_ref = refs[:ns], refs[ns:2 * ns], refs[2 * ns]
        for rs in groups:
            acc = _dot(_mx(a_refs[0][rs, :]), _mx(b_refs[0][...]), 1, 0)
            for s in range(1, ns):
                acc = acc + _dot(_mx(a_refs[s][rs, :]), _mx(b_refs[s][...]), 1, 0)
            o_ref[rs, :] = acc

    return pl.pallas_call(
        body, name=name, grid=(m // tm, n // tn),
        in_specs=([pl.BlockSpec((tm, a.shape[1]), lambda i, j: (i, 0)) for a in a_list]
                  + [pl.BlockSpec((b.shape[0], tn), lambda i, j: (0, j)) for b in b_list]),
        out_specs=pl.BlockSpec((tm, tn), lambda i, j: (i, j)),
        out_shape=jax.ShapeDtypeStruct((m, n), F32), compiler_params=_params("parallel", "parallel"),
    )(*a_list, *b_list)


def _row_tile(r_, streams=4):
    return _pick(r_, ((512,) if streams <= 4 else ()) + (256, 128, 64, 32, 16, 8))


def _rms_fwd(t, w, groups=1, res=None, out_dtype=F32, name="rms_fwd"):
    r_, f = t.shape
    fg = f // groups
    tr = _row_tile(r_)
    has_res = res is not None

    def body(*refs):
        t_ref, w_ref = refs[0], refs[1]
        res_ref = refs[2] if has_res else None
        o_ref = refs[-1]
        for g in range(groups):
            sl = slice(g * fg, (g + 1) * fg)
            tv = t_ref[:, sl].astype(F32)
            r = lax.rsqrt(jnp.mean(tv * tv, axis=-1, keepdims=True) + EPS)
            y = tv * r * w_ref[:, sl]
            if has_res:
                y = y + res_ref[:, sl]
            o_ref[:, sl] = y.astype(o_ref.dtype)

    row = pl.BlockSpec((tr, f), lambda i: (i, 0))
    wsp = pl.BlockSpec((1, f), lambda i: (0, 0))
    return pl.pallas_call(
        body, name=name, grid=(r_ // tr,),
        in_specs=[row, wsp] + ([row] if has_res else []), out_specs=row,
        out_shape=jax.ShapeDtypeStruct((r_, f), out_dtype),
        compiler_params=_params("parallel"),
    )(*((t, w.reshape(1, f)) + ((res,) if has_res else ())))


def _rms_bwd(t, w, dys, res=None, out_dtype=F32, name="rms_bwd"):
    r_, f = t.shape
    groups = len(dys)
    fg = f // groups
    tr = _row_tile(r_)
    has_res = res is not None

    def body(*refs):
        t_ref, w_ref = refs[0], refs[1]
        dy_refs = refs[2:2 + groups]
        res_ref = refs[2 + groups] if has_res else None
        dt_ref, dw_ref = refs[-2], refs[-1]

        @pl.when(pl.program_id(0) == 0)
        def _():
            dw_ref[...] = jnp.zeros_like(dw_ref)

        for g in range(groups):
            sl = slice(g * fg, (g + 1) * fg)
            tv = t_ref[:, sl].astype(F32)
            dyv = dy_refs[g][...].astype(F32)
            r = lax.rsqrt(jnp.mean(tv * tv, axis=-1, keepdims=True) + EPS)
            gw = dyv * w_ref[:, sl]
            c = jnp.mean(gw * tv, axis=-1, keepdims=True)
            dt = r * gw - tv * (r * r * r * c)
            if has_res:
                dt = dt + res_ref[:, sl]
            dt_ref[:, sl] = dt.astype(dt_ref.dtype)
            dw_ref[:, sl] += jnp.sum(dyv * tv * r, axis=0, keepdims=True)

    row = pl.BlockSpec((tr, f), lambda i: (i, 0))
    grow = pl.BlockSpec((tr, fg), lambda i: (i, 0))
    wsp = pl.BlockSpec((1, f), lambda i: (0, 0))
    return pl.pallas_call(
        body, name=name, grid=(r_ // tr,),
        in_specs=[row, wsp] + [grow] * groups + ([row] if has_res else []), out_specs=[row, wsp],
        out_shape=[jax.ShapeDtypeStruct((r_, f), out_dtype), jax.ShapeDtypeStruct((1, f), F32)],
        compiler_params=_params("arbitrary"),
    )(*((t, w.reshape(1, f)) + tuple(dys) + ((res,) if has_res else ())))


def _norm_res_norm(t, res, w1, w2, name="post_mix_pre_ffn_norm"):
    r_, f = t.shape
    tr = _row_tile(r_)

    def body(t_ref, res_ref, w1_ref, w2_ref, h_ref, v_ref):
        tv = t_ref[...]
        h = res_ref[...] + tv * lax.rsqrt(jnp.mean(tv * tv, axis=-1, keepdims=True) + EPS) * w1_ref[...]
        h_ref[...] = h
        v_ref[...] = (h * lax.rsqrt(jnp.mean(h * h, axis=-1, keepdims=True) + EPS) * w2_ref[...]).astype(v_ref.dtype)

    row = pl.BlockSpec((tr, f), lambda i: (i, 0))
    wsp = pl.BlockSpec((1, f), lambda i: (0, 0))
    return pl.pallas_call(
        body, name=name, grid=(r_ // tr,), in_specs=[row, row, wsp, wsp], out_specs=[row, row],
        out_shape=[jax.ShapeDtypeStruct((r_, f), F32), jax.ShapeDtypeStruct((r_, f), MXU_DTYPE)],
        compiler_params=_params("parallel"),
    )(t, res, w1.reshape(1, f), w2.reshape(1, f))


def _norm_res_norm_bwd(h, w2, dv, dres, t, w1, name="pre_ffn_post_mix_norm_bwd"):
    r_, f = h.shape
    tr = _row_tile(r_, streams=6)

    def body(h_ref, w2_ref, dv_ref, dres_ref, t_ref, w1_ref, dh_ref, dt_ref, dw2_ref, dw1_ref):
        @pl.when(pl.program_id(0) == 0)
        def _():
            dw2_ref[...] = jnp.zeros_like(dw2_ref)
            dw1_ref[...] = jnp.zeros_like(dw1_ref)

        def rms_bwd(tv, wv, dyv):
            r = lax.rsqrt(jnp.mean(tv * tv, axis=-1, keepdims=True) + EPS)
            gw = dyv * wv
            c = jnp.mean(gw * tv, axis=-1, keepdims=True)
            return r * gw - tv * (r * r * r * c), jnp.sum(dyv * tv * r, axis=0, keepdims=True)

        d1, g2 = rms_bwd(h_ref[...], w2_ref[...], dv_ref[...])
        dh = d1 + dres_ref[...]
        dh_ref[...] = dh
        dw2_ref[...] += g2
        d2, g1 = rms_bwd(t_ref[...], w1_ref[...], dh)
        dt_ref[...] = d2.astype(dt_ref.dtype)
        dw1_ref[...] += g1

    row = pl.BlockSpec((tr, f), lambda i: (i, 0))
    wsp = pl.BlockSpec((1, f), lambda i: (0, 0))
    return pl.pallas_call(
        body, name=name, grid=(r_ // tr,), in_specs=[row, wsp, row, row, row, wsp], out_specs=[row, row, wsp, wsp],
        out_shape=[jax.ShapeDtypeStruct((r_, f), F32), jax.ShapeDtypeStruct((r_, f), MXU_DTYPE),
                   jax.ShapeDtypeStruct((1, f), F32), jax.ShapeDtypeStruct((1, f), F32)],
        compiler_params=_params("arbitrary"),
    )(h, w2.reshape(1, f), dv, dres, t, w1.reshape(1, f))


def _hnorm_fwd(o, w, width, name="attn_out_norm"):
    h, s_, v = o.shape
    tr = _row_tile(s_)

    def body(o_ref, w_ref, y_ref):
        ss = jnp.sum(o_ref[0] * o_ref[0], axis=-1, keepdims=True)
        for i in range(1, h):
            ss = ss + jnp.sum(o_ref[i] * o_ref[i], axis=-1, keepdims=True)
        r = lax.rsqrt(ss * (1.0 / (h * v)) + EPS)
        for i in range(h):
            sl = slice(i * v, (i + 1) * v)
            y_ref[:, sl] = (o_ref[i] * r * w_ref[:, sl]).astype(y_ref.dtype)

    return pl.pallas_call(
        body, name=name, grid=(s_ // tr,),
        in_specs=[pl.BlockSpec((h, tr, v), lambda i: (0, i, 0)), pl.BlockSpec((1, h * v), lambda i: (0, 0))],
        out_specs=pl.BlockSpec((tr, h * v), lambda i: (i, 0)),
        out_shape=jax.ShapeDtypeStruct((s_, width), MXU_DTYPE), compiler_params=_params("parallel"),
    )(o, w)


def _hnorm_bwd(o, w, dy, name="attn_out_norm_bwd"):
    h, s_, v = o.shape
    tr = _row_tile(s_)

    def body(o_ref, w_ref, dy_ref, do_ref, delta_ref, dw_ref):
        @pl.when(pl.program_id(0) == 0)
        def _():
            dw_ref[...] = jnp.zeros_like(dw_ref)

        ss = jnp.zeros((tr, 1), F32)
        cc = jnp.zeros((tr, 1), F32)
        for i in range(h):
            sl = slice(i * v, (i + 1) * v)
            ov = o_ref[i]
            ss = ss + jnp.sum(ov * ov, axis=-1, keepdims=True)
            cc = cc + jnp.sum(dy_ref[:, sl] * w_ref[:, sl] * ov, axis=-1, keepdims=True)
        r = lax.rsqrt(ss * (1.0 / (h * v)) + EPS)
        c = cc * (1.0 / (h * v))
        for i in range(h):
            sl = slice(i * v, (i + 1) * v)
            ov = o_ref[i]
            dyv = dy_ref[:, sl]
            dov = r * dyv * w_ref[:, sl] - ov * (r * r * r * c)
            do_ref[i] = dov.astype(do_ref.dtype)
            delta_ref[i] = jnp.sum(dov * ov, axis=-1, keepdims=True)
            dw_ref[:, sl] += jnp.sum(dyv * ov * r, axis=0, keepdims=True)

    blk = pl.BlockSpec((h, tr, v), lambda i: (0, i, 0))
    wsp = pl.BlockSpec((1, h * v), lambda i: (0, 0))
    return pl.pallas_call(
        body, name=name, grid=(s_ // tr,),
        in_specs=[blk, wsp, pl.BlockSpec((tr, h * v), lambda i: (i, 0))],
        out_specs=[blk, pl.BlockSpec((h, tr, 1), lambda i: (0, i, 0)), wsp],
        out_shape=[jax.ShapeDtypeStruct(o.shape, MXU_DTYPE), jax.ShapeDtypeStruct((h, s_, 1), F32),
                   jax.ShapeDtypeStruct((1, h * v), F32)],
        compiler_params=_params("arbitrary"),
    )(o, w, dy)


def _loss_head(ffn, h1, target, w, name="loss_head"):
    r_, f = ffn.shape
    tr = _row_tile(r_)

    def body(ffn_ref, h1_ref, tg_ref, w_ref, loss_ref, dy_ref, dffn_ref, dw_ref):
        @pl.when(pl.program_id(0) == 0)
        def _():
            dw_ref[...] = jnp.zeros_like(dw_ref)
            loss_ref[...] = jnp.zeros_like(loss_ref)

        tv = ffn_ref[...]
        wv = w_ref[...]
        r = lax.rsqrt(jnp.mean(tv * tv, axis=-1, keepdims=True) + EPS)
        tn = tv * r
        e = h1_ref[...] + tn * wv - tg_ref[...]
        tot = jnp.sum(jnp.sum(e * e, axis=1, keepdims=True), axis=0, keepdims=True) * (0.5 / f)
        loss_ref[...] += tot + jnp.zeros_like(loss_ref)
        dyv = e * (1.0 / f)
        dy_ref[...] = dyv
        gw = dyv * wv
        c = jnp.mean(gw * tv, axis=-1, keepdims=True)
        dffn_ref[...] = (r * gw - tv * (r * r * r * c)).astype(dffn_ref.dtype)
        dw_ref[...] += jnp.sum(dyv * tn, axis=0, keepdims=True)

    row = pl.BlockSpec((tr, f), lambda i: (i, 0))
    wsp = pl.BlockSpec((1, f), lambda i: (0, 0))
    lsp = pl.BlockSpec((1, LANE), lambda i: (0, 0))
    return pl.pallas_call(
        body, name=name, grid=(r_ // tr,),
        in_specs=[row, row, row, wsp], out_specs=[lsp, row, row, wsp],
        out_shape=[jax.ShapeDtypeStruct((1, LANE), F32), jax.ShapeDtypeStruct((r_, f), F32),
                   jax.ShapeDtypeStruct((r_, f), MXU_DTYPE), jax.ShapeDtypeStruct((1, f), F32)],
        compiler_params=_params("arbitrary"),
    )(ffn, h1, target, w.reshape(1, f))


def _rot_matrix():
    p = np.zeros((ROPE, ROPE), np.float32)
    for i in range(HALF):
        p[i + HALF, i] = -1.0
        p[i, i + HALF] = 1.0
    return jnp.asarray(p, BF16)


def _rope_val(r, c2, s2, rot):
    hi, mid, _ = _split3(r)
    return r * c2 + (_dot(hi, rot, 1, 0) + _dot(mid, rot, 1, 0)) * s2


def _q_prep(q, cos2, sin2, scale, name):
    h, s_, _ = q.shape
    tr = _pick(s_, (4096, 2048, 1024, 512, 256, 128, 64, 32, 16))

    def body(q_ref, c_ref, s_ref, rot_ref, o_ref):
        for rs in _row_slices(tr, 16):
            x = q_ref[rs, :]
            o_ref[rs, :NOPE] = (x[:, :NOPE] * scale).astype(o_ref.dtype)
            o_ref[rs, NOPE:] = (_rope_val(x[:, NOPE:], c_ref[rs, :], s_ref[rs, :], rot_ref[...]) * scale).astype(o_ref.dtype)

    blk = pl.BlockSpec((None, tr, QK), lambda hh, i: (hh, i, 0))
    csp = pl.BlockSpec((tr, ROPE), lambda hh, i: (i, 0))
    return pl.pallas_call(
        body, name=name, grid=(h, s_ // tr),
        in_specs=[blk, csp, csp, pl.BlockSpec((ROPE, ROPE), lambda hh, i: (0, 0))], out_specs=blk,
        out_shape=jax.ShapeDtypeStruct(q.shape, MXU_DTYPE), compiler_params=_params("parallel", "parallel"),
    )(q, cos2, sin2, _rot_matrix())


def _q_up(qkvn, w_uq_t, cos2, sin2, scale, name="q_up"):
    s_ = qkvn.shape[0]
    h = w_uq_t.shape[0]
    tm = _pick(s_, (4096, 2048, 1024, 512, 256, 128))

    def body(a_ref, w_ref, c_ref, s_ref, rot_ref, o_ref):
        for rs in _row_slices(tm, 16):
            x = _dot(_mx(a_ref[rs, :]), _mx(w_ref[...]), 1, 1)
            o_ref[rs, :NOPE] = (x[:, :NOPE] * scale).astype(o_ref.dtype)
            o_ref[rs, NOPE:] = (_rope_val(x[:, NOPE:], c_ref[rs, :], s_ref[rs, :], rot_ref[...]) * scale).astype(o_ref.dtype)

    csp = pl.BlockSpec((tm, ROPE), lambda j, i: (i, 0))
    return pl.pallas_call(
        body, name=name, grid=(h, s_ // tm),
        in_specs=[pl.BlockSpec((tm, Q_RANK), lambda j, i: (i, 0)), pl.BlockSpec((None, QK, Q_RANK), lambda j, i: (j, 0, 0)),
                  csp, csp, pl.BlockSpec((ROPE, ROPE), lambda j, i: (0, 0))],
        out_specs=pl.BlockSpec((None, tm, QK), lambda j, i: (j, i, 0)),
        out_shape=jax.ShapeDtypeStruct((h, s_, QK), MXU_DTYPE), compiler_params=_params("parallel", "parallel"),
    )(qkvn, w_uq_t, cos2, sin2, _rot_matrix())


def _kv_up(qkvn, w_ukv, small, cos2, sin2, name="kv_up"):
    s_ = qkvn.shape[0]
    h = w_ukv.shape[0]
    tm = _pick(s_, (4096, 2048, 1024, 512, 256, 128))

    def body(a_ref, w_ref, sm_ref, c_ref, s_ref, rot_ref, k_ref, v_ref):
        for rs in _row_slices(tm, 16):
            x = _dot(_mx(a_ref[rs, :]), _mx(w_ref[...]), 1, 0)
            k_ref[rs, :NOPE] = x[:, :NOPE].astype(k_ref.dtype)
            k_ref[rs, NOPE:] = _rope_val(sm_ref[rs, :ROPE], c_ref[rs, :], s_ref[rs, :], rot_ref[...]).astype(k_ref.dtype)
            v_ref[rs, :] = x[:, NOPE:].astype(v_ref.dtype)

    csp = pl.BlockSpec((tm, ROPE), lambda j, i: (i, 0))
    return pl.pallas_call(
        body, name=name, grid=(h, s_ // tm),
        in_specs=[pl.BlockSpec((tm, KV_RANK), lambda j, i: (i, Q_RANK // KV_RANK)),
                  pl.BlockSpec((None, KV_RANK, NOPE + VDIM), lambda j, i: (j, 0, 0)),
                  pl.BlockSpec((tm, LANE), lambda j, i: (i, 0)), csp, csp, pl.BlockSpec((ROPE, ROPE), lambda j, i: (0, 0))],
        out_specs=[pl.BlockSpec((None, tm, QK), lambda j, i: (j, i, 0)), pl.BlockSpec((None, tm, VDIM), lambda j, i: (j, i, 0))],
        out_shape=[jax.ShapeDtypeStruct((h, s_, QK), MXU_DTYPE), jax.ShapeDtypeStruct((h, s_, VDIM), MXU_DTYPE)],
        compiler_params=_params("parallel", "parallel"),
    )(qkvn, w_ukv, small, cos2, sin2, _rot_matrix())


def _dkv_post(dk, dv, ddt, cos2, nsin2, name="dkv_post"):
    h, s_, _ = dk.shape
    tr = _row_tile(s_)

    def body(dk_ref, dv_ref, ddt_ref, c_ref, s_ref, rot_ref, dkv_ref, dsm_ref):
        acc = dk_ref[0, :, NOPE:]
        for i in range(1, h):
            acc = acc + dk_ref[i, :, NOPE:]
        dsm_ref[:, :ROPE] = _rope_val(acc, c_ref[...], s_ref[...], rot_ref[...]).astype(dsm_ref.dtype)
        dsm_ref[:, ROPE:] = ddt_ref[:, ROPE:].astype(dsm_ref.dtype)
        for i in range(h):
            dkv_ref[i, :, :NOPE] = dk_ref[i, :, :NOPE].astype(dkv_ref.dtype)
            dkv_ref[i, :, NOPE:] = dv_ref[i].astype(dkv_ref.dtype)

    csp = pl.BlockSpec((tr, ROPE), lambda i: (i, 0))
    return pl.pallas_call(
        body, name=name, grid=(s_ // tr,),
        in_specs=[pl.BlockSpec((h, tr, QK), lambda i: (0, i, 0)), pl.BlockSpec((h, tr, VDIM), lambda i: (0, i, 0)),
                  pl.BlockSpec((tr, LANE), lambda i: (i, 0)), csp, csp, pl.BlockSpec((ROPE, ROPE), lambda i: (0, 0))],
        out_specs=[pl.BlockSpec((h, tr, NOPE + VDIM), lambda i: (0, i, 0)), pl.BlockSpec((tr, LANE), lambda i: (i, 0))],
        out_shape=[jax.ShapeDtypeStruct((h, s_, NOPE + VDIM), MXU_DTYPE), jax.ShapeDtypeStruct((s_, LANE), MXU_DTYPE)],
        compiler_params=_params("parallel"),
    )(dk, dv, ddt, cos2, nsin2, _rot_matrix())


def _attn_tile(s):
    return 2048 if s % 4096 == 0 else s // 2


def _pairs(n, by_key):
    if by_key:
        pr = [(i, j) for j in range(n) for i in range(j, n)]
    else:
        pr = [(i, j) for i in range(n) for j in range(i + 1)]
    return (jnp.asarray([p[0] for p in pr], jnp.int32), jnp.asarray([p[1] for p in pr], jnp.int32))


ATTN_ROW_GROUPS = 8


def _row_groups(t, diag):
    tg = t // ATTN_ROW_GROUPS
    out = []
    for r in range(ATTN_ROW_GROUPS):
        nc = (r + 1) * tg if diag else t
        mask = None
        if diag:
            mask = (lax.broadcasted_iota(jnp.int32, (tg, nc), 1)
                    <= lax.broadcasted_iota(jnp.int32, (tg, nc), 0) + r * tg)
        out.append((slice(r * tg, (r + 1) * tg), nc, mask))
    return out


def _flash_specs(t, dk, dv):
    qsp = pl.BlockSpec((None, t, dk), lambda hh, p, qi, kj: (hh, qi[p], 0))
    ksp = pl.BlockSpec((None, t, dk), lambda hh, p, qi, kj: (hh, kj[p], 0))
    vsp = pl.BlockSpec((None, t, dv), lambda hh, p, qi, kj: (hh, kj[p], 0))
    osp = pl.BlockSpec((None, t, dv), lambda hh, p, qi, kj: (hh, qi[p], 0))
    lsp = pl.BlockSpec((None, t, 1), lambda hh, p, qi, kj: (hh, qi[p], 0))
    return qsp, ksp, vsp, osp, lsp


def _flash_fwd(q, k, v, name="flash_fwd"):
    h, s_, dk = q.shape
    dv = v.shape[-1]
    t = _attn_tile(s_)
    n = s_ // t
    qi, kj = _pairs(n, False)

    def body(qi_ref, kj_ref, q_ref, k_ref, v_ref, o_ref, lse_ref, m_s, l_s, acc):
        p_ = pl.program_id(1)
        i, j = qi_ref[p_], kj_ref[p_]

        @pl.when(j == 0)
        def _():
            m_s[...] = jnp.full_like(m_s, -jnp.inf)
            l_s[...] = jnp.zeros_like(l_s)
            acc[...] = jnp.zeros_like(acc)

        def update(diag):
            for rs, nc, mask in _row_groups(t, diag):
                sc = _dot(q_ref[rs, :], k_ref[0:nc, :], 1, 1)
                if mask is not None:
                    sc = jnp.where(mask, sc, -jnp.inf)
                m_old = m_s[rs, :]
                m_new = jnp.maximum(m_old, jnp.max(sc, axis=1, keepdims=True))
                alpha = jnp.exp(m_old - m_new)
                p = jnp.exp(sc - m_new)
                l_s[rs, :] = alpha * l_s[rs, :] + jnp.sum(p, axis=1, keepdims=True)
                acc[rs, :] = alpha * acc[rs, :] + _dot(_mx(p), v_ref[0:nc, :], 1, 0)
                m_s[rs, :] = m_new

        @pl.when(j < i)
        def _():
            update(False)

        @pl.when(j == i)
        def _():
            update(True)
            o_ref[...] = acc[...] / l_s[...]
            lse_ref[...] = m_s[...] + jnp.log(l_s[...])

    qsp, ksp, vsp, osp, lsp = _flash_specs(t, dk, dv)
    gs = pltpu.PrefetchScalarGridSpec(
        num_scalar_prefetch=2, grid=(h, qi.shape[0]), in_specs=[qsp, ksp, vsp], out_specs=[osp, lsp],
        scratch_shapes=[pltpu.VMEM((t, 1), F32), pltpu.VMEM((t, 1), F32), pltpu.VMEM((t, dv), F32)])
    return pl.pallas_call(
        body, name=name, grid_spec=gs,
        out_shape=[jax.ShapeDtypeStruct((h, s_, dv), F32), jax.ShapeDtypeStruct((h, s_, 1), F32)],
        compiler_params=_params("parallel", "arbitrary"),
    )(qi, kj, q, k, v)


def _flash_bwd(q, k, v, do, lse, delta, name="flash_bwd"):
    h, s_, dk = q.shape
    dv = v.shape[-1]
    t = _attn_tile(s_)
    tg = t // ATTN_ROW_GROUPS
    n = s_ // t
    qi, kj = _pairs(n, True)

    def body(qi_ref, kj_ref, q_ref, k_ref, v_ref, do_ref, lse_ref, delta_ref, dq_ref, dk_ref, dv_ref, dk_acc, dv_acc):
        p_ = pl.program_id(1)
        i, j = qi_ref[p_], kj_ref[p_]

        @pl.when(p_ == 0)
        def _():
            dq_ref[...] = jnp.zeros_like(dq_ref)

        def update(diag):
            for g, (rs, nc, mask) in enumerate(_row_groups(t, diag)):
                sc = _dot(q_ref[rs, :], k_ref[0:nc, :], 1, 1)
                if mask is not None:
                    sc = jnp.where(mask, sc, -jnp.inf)
                p = jnp.exp(sc - lse_ref[rs, :])
                dob = _mx(do_ref[rs, :])
                dv_acc[0:nc, :] += _dot(_mx(p), dob, 0, 0)
                dp = _dot(dob, v_ref[0:nc, :], 1, 1)
                dsb = _mx(p * (dp - delta_ref[rs, :]))
                dk_acc[0:nc, :] += _dot(dsb, q_ref[rs, :], 0, 0)
                rows = pl.ds(pl.multiple_of(i * t + g * tg, tg), tg)
                dq_ref[rows, :] += _dot(dsb, k_ref[0:nc, :], 1, 0)

        @pl.when(i == j)
        def _():
            dk_acc[...] = jnp.zeros_like(dk_acc)
            dv_acc[...] = jnp.zeros_like(dv_acc)
            update(True)

        @pl.when(i > j)
        def _():
            update(False)

        @pl.when(i == n - 1)
        def _():
            dk_ref[...] = dk_acc[...]
            dv_ref[...] = dv_acc[...]

    qsp, ksp, vsp, osp, lsp = _flash_specs(t, dk, dv)
    dqsp = pl.BlockSpec((None, s_, dk), lambda hh, p, qi, kj: (hh, 0, 0))
    gs = pltpu.PrefetchScalarGridSpec(
        num_scalar_prefetch=2, grid=(h, qi.shape[0]), in_specs=[qsp, ksp, vsp, osp, lsp, lsp],
        out_specs=[dqsp, ksp, vsp],
        scratch_shapes=[pltpu.VMEM((t, dk), F32), pltpu.VMEM((t, dv), F32)])
    return pl.pallas_call(
        body, name=name, grid_spec=gs,
        out_shape=[jax.ShapeDtypeStruct((h, s_, dk), F32), jax.ShapeDtypeStruct((h, s_, dk), F32),
                   jax.ShapeDtypeStruct((h, s_, dv), F32)],
        compiler_params=_params("parallel", "arbitrary"),
    )(qi, kj, q, k, v, do, lse, delta)


HALO = 8


def _conv_specs(s_, c, tr, after):
    main = pl.BlockSpec((tr, c), lambda i: (i, 0))
    per = tr // HALO
    if after:
        halo = pl.BlockSpec((HALO, c), lambda i: (jnp.minimum((i + 1) * per, s_ // HALO - 1), 0))
    else:
        halo = pl.BlockSpec((HALO, c), lambda i: (jnp.maximum(i * per - 1, 0), 0))
    return main, halo


def _fill_before(ext, t_ref, h_ref, tr):
    ext[0:HALO, :] = jnp.where(pl.program_id(0) > 0, h_ref[...], 0.0)
    ext[HALO:HALO + tr, :] = t_ref[...]


def _taps(ext, w_ref, tr):
    base = HALO - (CONV_K - 1)
    acc = ext[base:base + tr, :] * w_ref[0:1, :]
    for k in range(1, CONV_K):
        acc = acc + ext[base + k:base + k + tr, :] * w_ref[k:k + 1, :]
    return acc


def _conv_fwd(t, w, b, name="conv_fwd"):
    s_, c = t.shape
    tr = _row_tile(s_)

    def body(t_ref, h_ref, w_ref, b_ref, o_ref, ext):
        _fill_before(ext, t_ref, h_ref, tr)
        o_ref[...] = _silu(_taps(ext, w_ref, tr) + b_ref[...])

    main, halo = _conv_specs(s_, c, tr, False)
    return pl.pallas_call(
        body, name=name, grid=(s_ // tr,),
        in_specs=[main, halo, pl.BlockSpec((CONV_K, c), lambda i: (0, 0)), pl.BlockSpec((1, c), lambda i: (0, 0))],
        out_specs=main, out_shape=jax.ShapeDtypeStruct((s_, c), F32),
        scratch_shapes=[pltpu.VMEM((tr + HALO, c), F32)], compiler_params=_params("parallel"),
    )(t, t, w, b)


def _conv_bwd_pre(t, w, b, dact, name="conv_bwd_pre"):
    s_, c = t.shape
    tr = _row_tile(s_)

    def body(t_ref, h_ref, w_ref, b_ref, da_ref, dpre_ref, dwb_ref, ext):
        @pl.when(pl.program_id(0) == 0)
        def _():
            dwb_ref[...] = jnp.zeros_like(dwb_ref)

        _fill_before(ext, t_ref, h_ref, tr)
        dpre = da_ref[...] * _dsilu(_taps(ext, w_ref, tr) + b_ref[...])
        dpre_ref[...] = dpre
        base = HALO - (CONV_K - 1)
        for k in range(CONV_K):
            dwb_ref[k:k + 1, :] += jnp.sum(dpre * ext[base + k:base + k + tr, :], axis=0, keepdims=True)
        dwb_ref[CONV_K:CONV_K + 1, :] += jnp.sum(dpre, axis=0, keepdims=True)

    main, halo = _conv_specs(s_, c, tr, False)
    return pl.pallas_call(
        body, name=name, grid=(s_ // tr,),
        in_specs=[main, halo, pl.BlockSpec((CONV_K, c), lambda i: (0, 0)), pl.BlockSpec((1, c), lambda i: (0, 0)), main],
        out_specs=[main, pl.BlockSpec((8, c), lambda i: (0, 0))],
        out_shape=[jax.ShapeDtypeStruct((s_, c), F32), jax.ShapeDtypeStruct((8, c), F32)],
        scratch_shapes=[pltpu.VMEM((tr + HALO, c), F32)], compiler_params=_params("arbitrary"),
    )(t, t, w, b, dact)


def _conv_bwd_in(dpre, w, name="conv_bwd_in"):
    s_, c = dpre.shape
    tr = _row_tile(s_)
    nt = s_ // tr

    def body(d_ref, h_ref, w_ref, o_ref, ext):
        ext[0:tr, :] = d_ref[...]
        ext[tr:tr + HALO, :] = jnp.where(pl.program_id(0) < nt - 1, h_ref[...], 0.0)
        acc = ext[CONV_K - 1:CONV_K - 1 + tr, :] * w_ref[0:1, :]
        for k in range(1, CONV_K):
            acc = acc + ext[CONV_K - 1 - k:CONV_K - 1 - k + tr, :] * w_ref[k:k + 1, :]
        o_ref[...] = acc.astype(o_ref.dtype)

    main, halo = _conv_specs(s_, c, tr, True)
    return pl.pallas_call(
        body, name=name, grid=(nt,),
        in_specs=[main, halo, pl.BlockSpec((CONV_K, c), lambda i: (0, 0))],
        out_specs=main, out_shape=jax.ShapeDtypeStruct((s_, c), MXU_DTYPE),
        scratch_shapes=[pltpu.VMEM((tr + HALO, c), F32)], compiler_params=_params("parallel"),
    )(dpre, dpre, w)


def _ssd_chunk_common(dt_ref, dtt_ref, br_ref, bc_ref, ar_ref, ac_ref):
    li = lax.broadcasted_iota(jnp.int32, (CHUNK, CHUNK), 0)
    si = lax.broadcasted_iota(jnp.int32, (CHUNK, CHUNK), 1)
    lower = li >= si
    lower_b = lower.astype(BF16)
    upper_b = (li <= si).astype(BF16)
    zr = dt_ref[...] + br_ref[...]
    dtc = _softplus(zr)
    a_row = -jnp.exp(ar_ref[...])
    acum = _exact_dot(lower_b, dtc * a_row, 1, 0, False)
    dtt = _softplus(dtt_ref[...] + bc_ref[...])
    acum_t = _exact_dot(dtt * (-jnp.exp(ac_ref[...])), upper_b, 1, 0, True)
    return lower, upper_b, zr, dtc, a_row, acum, acum_t


def _head_terms(h, lower, dtc, acum, acum_t):
    lane = lax.broadcasted_iota(jnp.int32, (1, LANE), 1)
    sub = lax.broadcasted_iota(jnp.int32, (SSD_H, 1), 0)
    rowid = lax.broadcasted_iota(jnp.int32, (CHUNK, 1), 0)
    oh = (lane == HEAD_LANE + h).astype(F32)
    acol = jnp.sum(acum * oh, axis=1, keepdims=True)
    dcol = jnp.sum(dtc * oh, axis=1, keepdims=True)
    arow = jnp.sum(acum_t * (sub == h).astype(F32), axis=0, keepdims=True)
    alast = jnp.sum(jnp.where(rowid == CHUNK - 1, acol, 0.0), axis=0, keepdims=True)
    decay = jnp.exp(jnp.where(lower, acol - arow, -jnp.inf))
    return oh, acol, dcol, alast, decay


SSD_PAIRS = SSD_H // 2
PAIRS_PER_GROUP = SSD_E // 2


def _ps(q):
    return slice(q * LANE, (q + 1) * LANE)


def _gs(off, g):
    return slice(off + g * SSD_N, off + (g + 1) * SSD_N)


def _lanes(c0, c1):
    return jnp.where(lax.broadcasted_iota(jnp.int32, (1, LANE), 1) < SSD_P, c0, c1)


def _rows(c0, c1):
    return jnp.where(lax.broadcasted_iota(jnp.int32, (LANE, 1), 0) < SSD_P, c0, c1)


def _lane_halves(t):
    first = lax.broadcasted_iota(jnp.int32, (1, LANE), 1) < SSD_P
    return (jnp.sum(jnp.where(first, t, 0.0), axis=1, keepdims=True),
            jnp.sum(jnp.where(first, 0.0, t), axis=1, keepdims=True))


def _ssd_in_specs(rev):
    def ci(c):
        return c if rev is None else rev - c
    return [pl.BlockSpec((CHUNK, CONV_DIM), lambda c: (ci(c), 0)),
            pl.BlockSpec((CHUNK, LANE), lambda c: (ci(c), 0)),
            pl.BlockSpec((SSD_H, CHUNK), lambda c: (0, ci(c))),
            pl.BlockSpec((1, LANE), lambda c: (0, 0)), pl.BlockSpec((SSD_H, 1), lambda c: (0, 0)),
            pl.BlockSpec((1, LANE), lambda c: (0, 0)), pl.BlockSpec((SSD_H, 1), lambda c: (0, 0)),
            pl.BlockSpec((SSD_PAIRS, 1, LANE), lambda c: (0, 0, 0))]


def _ssd_fwd(xbc, small, dtt, bias_r, bias_c, alog_r, alog_c, dsk, name="ssd_fwd"):
    s_ = xbc.shape[0]
    nc = s_ // CHUNK

    def body(x_ref, dt_ref, dtt_ref, br_ref, bc_ref, ar_ref, ac_ref, dsk_ref, y_ref, prev_ref, state):
        @pl.when(pl.program_id(0) == 0)
        def _():
            state[...] = jnp.zeros_like(state)

        lower, _, _, dtc, _, acum, acum_t = _ssd_chunk_common(dt_ref, dtt_ref, br_ref, bc_ref, ar_ref, ac_ref)
        for g in range(SSD_G):
            bb = _mx(x_ref[:, _gs(B_OFF, g)])
            cb_ = _mx(x_ref[:, _gs(C_OFF, g)])
            cbm = _dot(cb_, bb, 1, 1)
            for e in range(PAIRS_PER_GROUP):
                q = g * PAIRS_PER_GROUP + e
                _, acol0, dcol0, alast0, decay0 = _head_terms(2 * q, lower, dtc, acum, acum_t)
                _, acol1, dcol1, alast1, decay1 = _head_terms(2 * q + 1, lower, dtc, acum, acum_t)
                x = x_ref[:, _ps(q)]
                xdt = x * _lanes(dcol0, dcol1)
                xb = _mx(xdt)
                yd = _lanes(_dot(_mx(cbm * decay0), xb, 1, 0), _dot(_mx(cbm * decay1), xb, 1, 0))
                prev = state[q]
                prev_ref[0, q] = prev
                yo = _dot(cb_, _mx(prev), 1, 1) * _lanes(jnp.exp(acol0), jnp.exp(acol1))
                ds = _lanes(jnp.exp(alast0 - acol0), jnp.exp(alast1 - acol1))
                st = _dot(_mx(xdt * ds), bb, 0, 0)
                state[q] = prev * _rows(jnp.exp(alast0), jnp.exp(alast1)) + st
                y_ref[:, _ps(q)] = yd + yo + x * dsk_ref[q]

    psp = pl.BlockSpec((1, SSD_PAIRS, LANE, SSD_N), lambda c: (c, 0, 0, 0))
    return pl.pallas_call(
        body, name=name, grid=(nc,),
        in_specs=_ssd_in_specs(None), out_specs=[pl.BlockSpec((CHUNK, SSD_W), lambda c: (c, 0)), psp],
        out_shape=[jax.ShapeDtypeStruct((s_, SSD_W), F32),
                   jax.ShapeDtypeStruct((nc, SSD_PAIRS, LANE, SSD_N), F32)],
        scratch_shapes=[pltpu.VMEM((SSD_PAIRS, LANE, SSD_N), F32)],
        compiler_params=_params("arbitrary"),
    )(xbc, small, dtt, bias_r, bias_c, alog_r, alog_c, dsk)


def _ssd_bwd(xbc, small, dtt, bias_r, bias_c, alog_r, alog_c, dsk, prev, dy, name="ssd_bwd"):
    s_ = xbc.shape[0]
    nc = s_ // CHUNK

    def body(x_ref, dt_ref, dtt_ref, br_ref, bc_ref, ar_ref, ac_ref, dsk_ref, prev_ref, dy_ref,
             dx_ref, ddt_ref, dpar_ref, dstate):
        @pl.when(pl.program_id(0) == 0)
        def _():
            dstate[...] = jnp.zeros_like(dstate)
            dpar_ref[...] = jnp.zeros_like(dpar_ref)

        lower, upper_b, zr, dtc, a_row, acum, acum_t = _ssd_chunk_common(
            dt_ref, dtt_ref, br_ref, bc_ref, ar_ref, ac_ref)
        strict = (lax.broadcasted_iota(jnp.int32, (CHUNK, CHUNK), 1)
                  < lax.broadcasted_iota(jnp.int32, (CHUNK, CHUNK), 0))
        strict_b = strict.astype(BF16)
        col2 = lax.broadcasted_iota(jnp.int32, (CHUNK, 2 * CHUNK), 1)
        strict2 = (jnp.where(col2 >= CHUNK, col2 - CHUNK, col2)
                   < lax.broadcasted_iota(jnp.int32, (CHUNK, 2 * CHUNK), 0))
        da_in = jnp.zeros((CHUNK, LANE), F32)
        r_off = jnp.zeros((CHUNK, LANE), F32)
        c_int = jnp.zeros((CHUNK, LANE), F32)
        c_row = jnp.zeros((1, LANE), F32)
        ddt = jnp.zeros((CHUNK, LANE), F32)
        dskip = jnp.zeros((1, LANE), F32)
        for g in range(SSD_G):
            bb = _mx(x_ref[:, _gs(B_OFF, g)])
            cb_ = _mx(x_ref[:, _gs(C_OFF, g)])
            cbm = _dot(cb_, bb, 1, 1)
            dcb = jnp.zeros((CHUNK, CHUNK), F32)
            dc_acc = jnp.zeros((CHUNK, SSD_N), F32)
            db_acc = jnp.zeros((CHUNK, SSD_N), F32)
            for e in range(PAIRS_PER_GROUP):
                q = g * PAIRS_PER_GROUP + e
                oh0, acol0, dcol0, alast0, decay0 = _head_terms(2 * q, lower, dtc, acum, acum_t)
                oh1, acol1, dcol1, alast1, decay1 = _head_terms(2 * q + 1, lower, dtc, acum, acum_t)
                x = x_ref[:, _ps(q)]
                dy = dy_ref[:, _ps(q)]
                dcol = _lanes(dcol0, dcol1)
                xdt = x * dcol
                xb = _mx(xdt)
                eacol = _lanes(jnp.exp(acol0), jnp.exp(acol1))
                ds = _lanes(jnp.exp(alast0 - acol0), jnp.exp(alast1 - acol1))
                ealast = _rows(jnp.exp(alast0), jnp.exp(alast1))
                dyb = _mx(dy)
                dyb0, dyb1 = _mx(_lanes(dy, 0.0)), _mx(_lanes(0.0, dy))
                dsh = dstate[q]
                dshb = _mx(dsh)
                prev = prev_ref[0, q]
                prevb = _mx(prev)
                dxdt_inter = ds * _dot(bb, dshb, 1, 1)
                dxdt = _lanes(_dot(_mx(cbm * decay0), dyb, 0, 0), _dot(_mx(cbm * decay1), dyb, 0, 0)) + dxdt_inter
                dwl0 = _dot(dyb0, xb, 1, 1) * decay0
                dwl1 = _dot(dyb1, xb, 1, 1) * decay1
                dcb = dcb + dwl0 + dwl1
                dyeb = _mx(dy * eacol)
                dc_acc = dc_acc + _dot(dyeb, prevb, 1, 0)
                db_acc = db_acc + _dot(_mx(xdt * ds), dshb, 1, 0)
                dstate[q] = _dot(dyeb, cb_, 0, 0) + ealast * dsh
                above = _exact_dot(upper_b, jnp.concatenate([dwl0 * cbm, dwl1 * cbm], axis=1), 1, 0, False)
                above = jnp.where(strict2, above, 0.0)
                da_in = (da_in + jnp.sum(above[:, :CHUNK], axis=1, keepdims=True) * oh0
                         + jnp.sum(above[:, CHUNK:], axis=1, keepdims=True) * oh1)
                y_off = _dot(cb_, prevb, 1, 1) * eacol
                r0, r1 = _lane_halves(dy * y_off)
                r_off = r_off + r0 * oh0 + r1 * oh1
                c0, c1 = _lane_halves(xdt * dxdt_inter)
                c_int = c_int + c0 * oh0 + c1 * oh1
                both = jnp.sum(dsh * prev, axis=1, keepdims=True) * ealast
                c_row = (c_row + jnp.sum(_rows(both, 0.0), axis=0, keepdims=True) * oh0
                         + jnp.sum(_rows(0.0, both), axis=0, keepdims=True) * oh1)
                t0, t1 = _lane_halves(dxdt * x)
                ddt = ddt + t0 * oh0 + t1 * oh1
                dx_ref[:, _ps(q)] = dxdt * dcol + dy * dsk_ref[q]
                k0, k1 = _lane_halves(dy * x)
                dskip = (dskip + jnp.sum(k0, axis=0, keepdims=True) * oh0 + jnp.sum(k1, axis=0, keepdims=True) * oh1)
            dcbb = _mx(dcb)
            dx_ref[:, _gs(C_OFF, g)] = dc_acc + _dot(dcbb, bb, 1, 0)
            dx_ref[:, _gs(B_OFF, g)] = db_acc + _dot(dcbb, cb_, 0, 0)
        da = (da_in + _exact_dot(upper_b, r_off, 1, 0, False) + _exact_dot(strict_b, c_int, 1, 0, False) + c_row)
        draw = (ddt + da * a_row) * _sigmoid(zr)
        ddt_ref[...] = draw
        dpar_ref[0:1, :] += jnp.sum(draw, axis=0, keepdims=True)
        dpar_ref[1:2, :] += jnp.sum(da * dtc, axis=0, keepdims=True) * a_row
        dpar_ref[2:3, :] += dskip

    rev = nc - 1
    psp = pl.BlockSpec((1, SSD_PAIRS, LANE, SSD_N), lambda c: (rev - c, 0, 0, 0))
    return pl.pallas_call(
        body, name=name, grid=(nc,),
        in_specs=_ssd_in_specs(rev) + [psp, pl.BlockSpec((CHUNK, SSD_W), lambda c: (rev - c, 0))],
        out_specs=[pl.BlockSpec((CHUNK, CONV_DIM), lambda c: (rev - c, 0)),
                   pl.BlockSpec((CHUNK, LANE), lambda c: (rev - c, 0)), pl.BlockSpec((8, LANE), lambda c: (0, 0))],
        out_shape=[jax.ShapeDtypeStruct((s_, CONV_DIM), F32), jax.ShapeDtypeStruct((s_, LANE), F32),
                   jax.ShapeDtypeStruct((8, LANE), F32)],
        scratch_shapes=[pltpu.VMEM((SSD_PAIRS, LANE, SSD_N), F32)],
        compiler_params=_params("arbitrary"),
    )(xbc, small, dtt, bias_r, bias_c, alog_r, alog_c, dsk, prev, dy)


GN = SSD_W // SSD_G


def _gated_norm_fwd(y, z, w, cat, name="gated_norm_fwd"):
    s_, f = y.shape
    tr = _row_tile(s_)

    def body(y_ref, z_ref, w_ref, cat_ref, o_ref):
        for g in range(SSD_G):
            sl = slice(g * GN, (g + 1) * GN)
            gg = y_ref[:, sl] * _silu(z_ref[:, sl])
            r = lax.rsqrt(jnp.mean(gg * gg, axis=-1, keepdims=True) + EPS)
            o_ref[:, sl] = (gg * r * w_ref[:, sl]).astype(o_ref.dtype)

    row = pl.BlockSpec((tr, f), lambda i: (i, 0))
    wsp = pl.BlockSpec((1, f), lambda i: (0, 0))
    return pl.pallas_call(
        body, name=name, grid=(s_ // tr,),
        in_specs=[row, row, wsp, pl.BlockSpec(memory_space=pl.ANY)], out_specs=pl.BlockSpec((tr, f), lambda i: (i, 1)),
        out_shape=jax.ShapeDtypeStruct(cat.shape, cat.dtype), input_output_aliases={3: 0},
        compiler_params=_params("parallel"),
    )(y, z, w.reshape(1, f), cat)


def _gated_norm_bwd(y, z, w, dout, name="gated_norm_bwd"):
    s_, f = y.shape
    tr = _row_tile(s_)

    def body(y_ref, z_ref, w_ref, do_ref, dy_ref, dz_ref, dw_ref):
        @pl.when(pl.program_id(0) == 0)
        def _():
            dw_ref[...] = jnp.zeros_like(dw_ref)

        for g in range(SSD_G):
            sl = slice(g * GN, (g + 1) * GN)
            yv = y_ref[:, sl]
            zv = z_ref[:, sl]
            dov = do_ref[:, sl].astype(F32)
            sz = _silu(zv)
            gg = yv * sz
            r = lax.rsqrt(jnp.mean(gg * gg, axis=-1, keepdims=True) + EPS)
            gw = dov * w_ref[:, sl]
            c = jnp.mean(gw * gg, axis=-1, keepdims=True)
            dgg = r * gw - gg * (r * r * r * c)
            dy_ref[:, sl] = dgg * sz
            dz_ref[:, sl] = (dgg * yv * _dsilu(zv)).astype(dz_ref.dtype)
            dw_ref[:, sl] += jnp.sum(dov * gg * r, axis=0, keepdims=True)

    row = pl.BlockSpec((tr, f), lambda i: (i, 0))
    wsp = pl.BlockSpec((1, f), lambda i: (0, 0))
    return pl.pallas_call(
        body, name=name, grid=(s_ // tr,),
        in_specs=[row, row, wsp, pl.BlockSpec((tr, f), lambda i: (i, 1))], out_specs=[row, row, wsp],
        out_shape=[jax.ShapeDtypeStruct((s_, f), F32), jax.ShapeDtypeStruct((s_, f), MXU_DTYPE),
                   jax.ShapeDtypeStruct((1, f), F32)],
        compiler_params=_params("arbitrary"),
    )(y, z, w.reshape(1, f), dout)


def _ffn_fwd(vv, w_gate, w_up, name="ffn_gate_up"):
    s_, d = vv.shape
    nb, f8, _ = w_gate.shape
    tm = _pick(s_, (1024, 512, 256, 128))

    def body(v_ref, wg_ref, wu_ref, g_ref, u_ref, a_ref):
        for rs in _row_slices(tm, 16):
            a = _mx(v_ref[rs, :])
            g = _dot(a, _mx(wg_ref[...]), 1, 1)
            u = _dot(a, _mx(wu_ref[...]), 1, 1)
            s = _sigmoid(g)
            gs = g * s
            g_ref[rs, :] = (u * (s * (1.0 + g * (1.0 - s)))).astype(g_ref.dtype)
            u_ref[rs, :] = gs.astype(u_ref.dtype)
            a_ref[rs, :] = (gs * u).astype(a_ref.dtype)

    wsp = pl.BlockSpec((None, f8, d), lambda j, i: (j, 0, 0))
    osp = pl.BlockSpec((None, tm, f8), lambda j, i: (j, i, 0))
    return pl.pallas_call(
        body, name=name, grid=(nb, s_ // tm),
        in_specs=[pl.BlockSpec((tm, d), lambda j, i: (i, 0)), wsp, wsp], out_specs=[osp] * 3,
        out_shape=[jax.ShapeDtypeStruct((nb, s_, f8), MXU_DTYPE)] * 3,
        compiler_params=_params("parallel", "parallel"),
    )(vv, w_gate, w_up)


def _ffn_bwd_act(dffn, w_down, gate, up, name="ffn_d_act"):
    s_, d = dffn.shape
    nb, f8, _ = w_down.shape
    tm = _pick(s_, (2048, 1024, 512, 256, 128))

    def body(d_ref, w_ref, g_ref, u_ref, dg_ref, du_ref):
        for rs in _row_slices(tm, 16):
            dact = _dot(_mx(d_ref[rs, :]), _mx(w_ref[...]), 1, 1)
            dg_ref[rs, :] = (dact * g_ref[rs, :].astype(F32)).astype(dg_ref.dtype)
            du_ref[rs, :] = (dact * u_ref[rs, :].astype(F32)).astype(du_ref.dtype)

    osp = pl.BlockSpec((None, tm, f8), lambda i, j: (j, i, 0))
    return pl.pallas_call(
        body, name=name, grid=(s_ // tm, nb),
        in_specs=[pl.BlockSpec((tm, d), lambda i, j: (i, 0)), pl.BlockSpec((None, f8, d), lambda i, j: (j, 0, 0)),
                  osp, osp],
        out_specs=[osp, osp], out_shape=[jax.ShapeDtypeStruct((nb, s_, f8), MXU_DTYPE)] * 2,
        compiler_params=_params("parallel", "parallel", vmem=VMEM_LIMIT_WIDE_BYTES),
    )(dffn, w_down, gate, up)


def _ffn_bwd_in(dgate, w_gate, dup, w_up, name="ffn_d_in"):
    nb, s_, f8 = dgate.shape
    d = w_gate.shape[2]
    tm = _pick(s_, (1024, 512, 256, 128))
    tn = _pick(d, (1024, 512, 256, 128))
    per = 2
    steps = nb // per

    def body(*refs):
        ins, o_ref, acc = refs[:4 * per], refs[4 * per], refs[4 * per + 1]
        j = pl.program_id(2)

        @pl.when(j == 0)
        def _():
            acc[...] = jnp.zeros_like(acc)

        for rs in _row_slices(tm, 16):
            part = None
            for t in range(per):
                dg_ref, wg_ref, du_ref, wu_ref = ins[4 * t:4 * t + 4]
                d_ = (_dot(_mx(dg_ref[rs, :]), _mx(wg_ref[...]), 1, 0)
                      + _dot(_mx(du_ref[rs, :]), _mx(wu_ref[...]), 1, 0))
                part = d_ if part is None else part + d_
            acc[rs, :] += part

        @pl.when(j == steps - 1)
        def _():
            o_ref[...] = acc[...]

    def specs(t):
        asp = pl.BlockSpec((None, tm, f8), lambda i, n, j: (j * per + t, i, 0))
        wsp = pl.BlockSpec((None, f8, tn), lambda i, n, j: (j * per + t, 0, n))
        return [asp, wsp, asp, wsp]

    return pl.pallas_call(
        body, name=name, grid=(s_ // tm, d // tn, steps),
        in_specs=[sp for t in range(per) for sp in specs(t)],
        out_specs=pl.BlockSpec((tm, tn), lambda i, n, j: (i, n)),
        out_shape=jax.ShapeDtypeStruct((s_, d), F32), scratch_shapes=[pltpu.VMEM((tm, tn), F32)],
        compiler_params=_params("parallel", "parallel", "arbitrary"),
    )(*((dgate, w_gate, dup, w_up) * per))


def _adam_math(g, w, m, v):
    m2 = ADAM_B1 * m + (1.0 - ADAM_B1) * g
    v2 = ADAM_B2 * v + (1.0 - ADAM_B2) * (g * g)
    m_hat = m2 / (1.0 - ADAM_B1 ** ADAM_STEP)
    v_hat = v2 / (1.0 - ADAM_B2 ** ADAM_STEP)
    delta = -ADAM_LR * (m_hat / (jnp.sqrt(v_hat) + ADAM_EPS) + ADAM_WD * w)
    return delta, m2, v2


def _adamw(parts, own, me, w, m, v, name="adamw"):
    nd, r_, c = parts.shape
    tr = _pick(r_, (128, 64, 32, 16))
    tc = c
    if tr == r_ and r_ > 128:
        tc = _pick(c, (256, 128))

    def body(me_ref, p_ref, own_ref, w_ref, m_ref, v_ref, g_ref, d_ref, m2_ref, v2_ref):
        mine = me_ref[0]
        g = jnp.zeros((tr, tc), F32)
        for i in range(nd):
            g = g + jnp.where(mine == i, own_ref[...], p_ref[i]).astype(F32)
        delta, m2, v2 = _adam_math(g, w_ref[...], m_ref[...], v_ref[...])
        g_ref[...] = g
        d_ref[...] = delta
        m2_ref[...] = m2
        v2_ref[...] = v2

    row = pl.BlockSpec((tr, tc), lambda i, j, me_: (i, j))
    gs = pltpu.PrefetchScalarGridSpec(
        num_scalar_prefetch=1, grid=(r_ // tr, c // tc),
        in_specs=[pl.BlockSpec((nd, tr, tc), lambda i, j, me_: (0, i, j)),
                  pl.BlockSpec((None, tr, tc), lambda i, j, me_: (me_[0], i, j)), row, row, row],
        out_specs=[row] * 4)
    return pl.pallas_call(
        body, name=name, grid_spec=gs, out_shape=[jax.ShapeDtypeStruct((r_, c), F32)] * 4,
        compiler_params=_params("parallel", "parallel"),
    )(me, parts, own, w, m, v)


def _adamw_small(parts, w, m, v, name="adamw_small"):
    nd = parts.shape[0]

    def body(p_ref, w_ref, m_ref, v_ref, g_ref, d_ref, m2_ref, v2_ref):
        g = p_ref[0]
        for i in range(1, nd):
            g = g + p_ref[i]
        delta, m2, v2 = _adam_math(g, w_ref[...], m_ref[...], v_ref[...])
        g_ref[...] = g
        d_ref[...] = delta
        m2_ref[...] = m2
        v2_ref[...] = v2

    return pl.pallas_call(
        body, name=name, out_shape=[jax.ShapeDtypeStruct(w.shape, F32)] * 4,
        compiler_params=pltpu.CompilerParams(vmem_limit_bytes=VMEM_LIMIT_BYTES),
    )(parts, w, m, v)


_HBM = pl.BlockSpec(memory_space=pltpu.HBM)
_MESH = pl.DeviceIdType.MESH


def _all_gather(xs, name):
    na = len(xs)

    def body(*refs):
        x_refs, out_refs = refs[:na], refs[na:2 * na]
        send_sems, recv_sems, local_sems = refs[2 * na:]
        x, y, c = lax.axis_index("x"), lax.axis_index("y"), lax.axis_index("c")
        me, sibling = (x, y, c), (x, y, 1 - c)
        near = [(1 - x, y), (x, 1 - y)]
        chips = near + [(1 - x, 1 - y)]
        relay_from = (x + c * (1 - 2 * x), y + (1 - c) * (1 - 2 * y))
        relay_to = (x + (1 - c) * (1 - 2 * x), y + c * (1 - 2 * y))

        def slot(a, px, py, pc):
            return out_refs[a].at[4 * px + 2 * py + pc]

        def copy(a, k, block, to, src=None):
            return pltpu.make_async_remote_copy(
                src_ref=slot(a, *block) if src is None else src, dst_ref=slot(a, *block),
                send_sem=send_sems.at[a, k], recv_sem=recv_sems.at[a, k], device_id=to, device_id_type=_MESH)

        mine = [pltpu.make_async_copy(x_refs[a], slot(a, *me), local_sems.at[a]) for a in range(na)]
        started = []
        for a in range(na):
            mine[a].start()
            first = [copy(a, 0, me, sibling, src=x_refs[a])]
            first += [copy(a, 1 + j, me, (*chip, c), src=x_refs[a]) for j, chip in enumerate(near)]
            for cp in first:
                cp.start()
            started += first
        for a in range(na):
            for j, chip in enumerate(chips):
                copy(a, 1 + j, (*chip, c), me).wait_recv()
                fwd = copy(a, 4 + j, (*chip, c), sibling)
                fwd.start()
                started.append(fwd)
                if j == len(near) - 1:
                    relay = copy(a, 1 + len(near), (*relay_from, c), (*relay_to, c))
                    relay.start()
                    started.append(relay)
        for a in range(na):
            copy(a, 0, sibling, me).wait_recv()
            for j, chip in enumerate(chips):
                copy(a, 4 + j, (*chip, 1 - c), me).wait_recv()
        for cp in started:
            cp.wait_send()
        for cp in mine:
            cp.wait()

    return pl.pallas_call(
        body, name=name, out_shape=[jax.ShapeDtypeStruct((N_DEV,) + t.shape, t.dtype) for t in xs],
        in_specs=[_HBM] * na, out_specs=[_HBM] * na,
        scratch_shapes=[pltpu.SemaphoreType.DMA((na, 7)), pltpu.SemaphoreType.DMA((na, 7)),
                        pltpu.SemaphoreType.DMA((na,))],
    )(*xs)


_SEM = pl.BlockSpec(memory_space=pltpu.SEMAPHORE)
_EFFECT = pltpu.SideEffectType.DATAFLOW_SIDE_EFFECTING


def _peers(x, y, c):
    out = []
    for k in range(1, N_DEV):
        px = 1 - x if k & 4 else x
        py = 1 - y if k & 2 else y
        pc = 1 - c if k & 1 else c
        out.append(((px, py, pc), 4 * px + 2 * py + pc))
    return out


def _push_copies(scatter, src_refs, land_refs, send_sems, recv_sems):
    x, y, c = lax.axis_index("x"), lax.axis_index("y"), lax.axis_index("c")
    me = 4 * x + 2 * y + c
    pairs = []
    for a, (src, land) in enumerate(zip(src_refs, land_refs)):
        for k, (peer, slot) in enumerate(_peers(x, y, c)):
            out_src = src.at[slot] if scatter else src
            si = a * (N_DEV - 1) + k
            send = pltpu.make_async_remote_copy(src_ref=out_src, dst_ref=land.at[me], send_sem=send_sems.at[si],
                                                recv_sem=recv_sems.at[si], device_id=peer, device_id_type=_MESH)
            recv = pltpu.make_async_remote_copy(src_ref=out_src, dst_ref=land.at[slot], send_sem=send_sems.at[si],
                                                recv_sem=recv_sems.at[si], device_id=peer, device_id_type=_MESH)
            pairs.append((send, recv))
    return pairs


def _push_start(srcs, scatter, dep, name):
    na = len(srcs)
    shapes = [t.shape[1:] if scatter else t.shape for t in srcs]
    lands = [pltpu.with_memory_space_constraint(lax.empty((N_DEV,) + s, t.dtype), pltpu.HBM) for s, t in zip(shapes, srcs)]

    def body(*refs):
        src_refs, land_refs = refs[:na], refs[na:2 * na]
        send_sems, recv_sems = refs[2 * na + 1], refs[2 * na + 2]
        token = refs[-1]
        for send, _ in _push_copies(scatter, src_refs, land_refs, send_sems, recv_sems):
            send.start()
        token[...] = jnp.zeros_like(token)

    sem = pltpu.SemaphoreType.DMA((na * (N_DEV - 1),))
    outs = pl.pallas_call(
        body, name=name,
        out_shape=(sem, sem) + tuple(pltpu.HBM(t.shape, t.dtype) for t in srcs)
        + tuple(pltpu.HBM(t.shape, t.dtype) for t in lands) + (jax.ShapeDtypeStruct((8, LANE), F32),),
        in_specs=[_HBM] * (2 * na) + [pl.BlockSpec(memory_space=pl.ANY)],
        out_specs=(_SEM, _SEM) + (_HBM,) * (2 * na) + (pl.BlockSpec(memory_space=pltpu.VMEM),),
        input_output_aliases={i: 2 + i for i in range(2 * na)},
        compiler_params=pltpu.CompilerParams(has_side_effects=_EFFECT),
    )(*[pltpu.with_memory_space_constraint(t, pltpu.HBM) for t in srcs], *lands, dep)
    return outs[0], outs[1], outs[2:2 + na], outs[2 + na:2 + 2 * na], outs[-1]


def _push_wait(send_sems, recv_sems, src_thru, land_thru, scatter, after, name):
    na = len(src_thru)

    def body(*refs):
        src_refs, land_refs = refs[:na], refs[na:2 * na]
        ssem, rsem = refs[2 * na], refs[2 * na + 1]
        for send, recv in _push_copies(scatter, src_refs, land_refs, ssem, rsem):
            send.wait_send()
            recv.wait_recv()

    outs = pl.pallas_call(
        body, name=name,
        out_shape=tuple(pltpu.HBM(t.shape, t.dtype) for t in src_thru) + tuple(pltpu.HBM(t.shape, t.dtype) for t in land_thru),
        in_specs=[_HBM] * (2 * na) + [_SEM, _SEM, pl.BlockSpec(memory_space=pl.ANY)],
        out_specs=(_HBM,) * (2 * na),
        input_output_aliases={i: i for i in range(2 * na)},
        compiler_params=pltpu.CompilerParams(has_side_effects=_EFFECT),
    )(*src_thru, *land_thru, send_sems, recv_sems, after)
    return outs[:na], outs[na:]


def _exchange_behind(srcs, scatter, dep, name):
    send_sems, recv_sems, thru, lands, token = _push_start(srcs, scatter, dep, name + "_start")

    def finish(after, place=True):
        src_done, land_done = _push_wait(send_sems, recv_sems, thru, lands, scatter, after, name + "_wait")
        if not place:
            return land_done, src_done
        return _place_own(land_done, src_done, scatter, name + "_own")

    return token[0, 0], finish


def _place_own(lands, srcs, scatter, name):
    me = (4 * lax.axis_index("x") + 2 * lax.axis_index("y") + lax.axis_index("c")).astype(jnp.int32).reshape(1)
    outs = []
    for a, (land, src) in enumerate(zip(lands, srcs)):
        r_, c_ = land.shape[1:]
        tr = _pick(r_, (512, 256, 128, 64, 32, 16))

        def body(me_ref, land_ref, src_ref, out_ref):
            out_ref[...] = src_ref[...]

        src_spec = (pl.BlockSpec((None, tr, c_), lambda i, me_: (me_[0], i, 0)) if scatter
                    else pl.BlockSpec((tr, c_), lambda i, me_: (i, 0)))
        gs = pltpu.PrefetchScalarGridSpec(
            num_scalar_prefetch=1, grid=(r_ // tr,),
            in_specs=[pl.BlockSpec(memory_space=pl.ANY), src_spec],
            out_specs=pl.BlockSpec((None, tr, c_), lambda i, me_: (me_[0], i, 0)))
        outs.append(pl.pallas_call(
            body, name=f"{name}_{a}", grid_spec=gs, out_shape=jax.ShapeDtypeStruct(land.shape, land.dtype),
            input_output_aliases={1: 0}, compiler_params=_params("arbitrary"),
        )(me, land, src))
    return outs


_TRANSPOSED = ("w_in", "w_uq", "w_gate", "w_up")
_CQKV = (0, Q_RANK + KV_RANK)
_KR = (_CQKV[1], _CQKV[1] + ROPE)
_Z = (_KR[1], _KR[1] + SSD_W)
_XBC = (_Z[1], _Z[1] + CONV_DIM)
_DT = (_XBC[1], _XBC[1] + SSD_H)


def _win_segments(w_in_t):
    w = w_in_t.reshape(D_IN, D_MODEL)
    small = jnp.concatenate([w[_KR[0]:_KR[1]], w[_DT[0]:_DT[1]],
                             jnp.zeros((LANE - ROPE - SSD_H, D_MODEL), w.dtype)], axis=0)
    return w[_CQKV[0]:_CQKV[1]], w[_Z[0]:_Z[1]], w[_XBC[0]:_XBC[1]], small


def _win_from_segments(g_cqkv, g_z, g_xbc, g_small):
    w = jnp.concatenate([g_cqkv, g_small[:ROPE], g_z, g_xbc, g_small[ROPE:ROPE + SSD_H]], axis=0)
    return w.reshape(N_DEV, D_IN // N_DEV, D_MODEL)


_SMALL = (("q_norm_w", 512), ("kv_norm_w", 512), ("conv_b", CONV_DIM), ("dt_bias", SSD_H), ("a_log", SSD_H),
          ("d_skip", SSD_H), ("ssd_norm_w", SSD_W), ("attn_out_norm_w", 1024), ("pre_mix_norm_w", D_MODEL),
          ("post_mix_norm_w", D_MODEL), ("pre_ffn_norm_w", D_MODEL), ("post_ffn_norm_w", D_MODEL),
          ("conv_w", CONV_K * CONV_DIM))
_SMALL_ROWS = -(-(sum(-(-n // LANE) for _, n in _SMALL) + 1) // 8) * 8


def _pack_small(vals):
    rows = []
    for name, n in _SMALL:
        v = vals[name].reshape(-1).astype(F32)
        pad = -(-n // LANE) * LANE
        rows.append(jnp.pad(v, (0, pad - n)).reshape(-1, LANE))
    m = jnp.concatenate(rows, axis=0)
    return jnp.pad(m, ((0, _SMALL_ROWS - m.shape[0]), (0, 0)))


def _unpack_small(m):
    out, r = {}, 0
    for name, n in _SMALL:
        nr = -(-n // LANE)
        out[name] = m[r:r + nr].reshape(-1)[:n]
        r += nr
    return out


def _head_row(v):
    return jnp.pad(v.reshape(1, -1).astype(F32), ((0, 0), (HEAD_LANE, LANE - HEAD_LANE - v.shape[-1])))


def _local_step(x, positions, target, wg, small, weights, on_grads):
    w_cqkv, w_z, w_xbc, w_small = _win_segments(wg["w_in"])
    conv_w = wg["conv_w"]
    conv_b = small["conv_b"].reshape(1, CONV_DIM)
    qkv_norm_w = jnp.concatenate([small["q_norm_w"], small["kv_norm_w"]])
    attn_norm_w = small["attn_out_norm_w"].reshape(1, HEADS * VDIM)
    scale = QK ** -0.5

    inv_freq = ROPE_THETA ** (-jnp.arange(0, ROPE, 2, dtype=F32) / ROPE)
    ang = positions.astype(F32)[:, None] * inv_freq
    cos2 = jnp.tile(jnp.cos(ang), (1, 2))
    sin2 = jnp.tile(jnp.sin(ang), (1, 2))

    u = _rms_fwd(x, small["pre_mix_norm_w"], out_dtype=MXU_DTYPE, name="pre_mix_norm")
    cqkv = _mm(u, w_cqkv, "nt", name="in_proj_qkv")
    z = _mm(u, w_z, "nt", name="in_proj_z")
    xbc = _mm(u, w_xbc, "nt", name="in_proj_xbc")
    sm = _mm(u, w_small, "nt", name="in_proj_small")

    w_uq, w_ukv = weights("qkv_up", cqkv)
    qkvn = _rms_fwd(cqkv, qkv_norm_w, groups=2, out_dtype=MXU_DTYPE, name="qkv_norm")
    q_h = _q_up(qkvn, w_uq, cos2, sin2, scale)
    k_h, v_h = _kv_up(qkvn, w_ukv, sm, cos2, sin2)
    o_h, lse = _flash_fwd(q_h, k_h, v_h)
    cat = _hnorm_fwd(o_h, attn_norm_w, D_MODEL)
    w_out = weights("out", o_h)[0].reshape(D_MODEL, D_MODEL)

    xbc_act = _conv_fwd(xbc, conv_w, conv_b)
    dtt = jnp.transpose(sm[:, HEAD_LANE:HEAD_LANE + SSD_H])
    ssd_args = (xbc_act, sm, dtt, _head_row(small["dt_bias"]), small["dt_bias"].reshape(SSD_H, 1),
                _head_row(small["a_log"]), small["a_log"].reshape(SSD_H, 1),
                jnp.broadcast_to(small["d_skip"].reshape(SSD_H, 1), (SSD_H, SSD_P)).reshape(SSD_PAIRS, 1, LANE))
    y_ssd, prev = _ssd_fwd(*ssd_args)
    cat = _gated_norm_fwd(y_ssd, z, small["ssd_norm_w"], cat)

    mix = _mm(cat, w_out, "nn", name="out_proj")
    h1, vv = _norm_res_norm(mix, x, small["post_mix_norm_w"], small["pre_ffn_norm_w"])

    w_gate, w_up = weights("ffn_in", mix)
    gate, up, act = _ffn_fwd(vv, w_gate, w_up)
    w_down, = weights("ffn_out", act)
    ffn = _mm(act, w_down, "nn", a_blk=True, b_blk=True, fuse=N_DEV, tm_max=512, name="ffn_down")
    loss_blk, dy, dffn, g_post_ffn = _loss_head(ffn, h1, target, small["post_ffn_norm_w"])

    g_down = _mm(act, dffn, "tn", a_blk=True, out_blk=True, out_dtype=MXU_DTYPE, name="g_down")
    dgate, dup = _ffn_bwd_act(dffn, w_down, gate, up)
    dvv = _ffn_bwd_in(dgate, w_gate, dup, w_up)
    g_gate = _mm(dgate, vv, "tn", a_blk=True, out_blk=True, out_dtype=MXU_DTYPE, name="g_gate")
    g_up = _mm(dup, vv, "tn", a_blk=True, out_blk=True, out_dtype=MXU_DTYPE, name="g_up")
    pre_ffn_w = small["pre_ffn_norm_w"] + on_grads("ffn", [g_gate, g_up, g_down])
    dh1, dmix, g_pre_ffn, g_post_mix = _norm_res_norm_bwd(h1, pre_ffn_w, dvv, dy, mix, small["post_mix_norm_w"])

    dcat = _mm(dmix, w_out, "nt", name="d_cat")
    g_out = _mm(cat, dmix, "tn", out_dtype=MXU_DTYPE, name="g_out")

    do_h, delta, g_attn_norm = _hnorm_bwd(o_h, attn_norm_w, dcat)
    dq_h, dk_h, dv_h = _flash_bwd(q_h, k_h, v_h, do_h, lse, delta)
    dq = _q_prep(dq_h, cos2, -sin2, scale, name="dq_post")

    dy_ssd, dz, g_ssd_norm = _gated_norm_bwd(y_ssd, z, small["ssd_norm_w"], dcat)
    dxbc_act, ddt, dpar = _ssd_bwd(*ssd_args, prev, dy_ssd)
    dkv, dsm = _dkv_post(dk_h, dv_h, ddt, cos2, -sin2)
    dpre, dwb = _conv_bwd_pre(xbc, conv_w, conv_b, dxbc_act)
    dxbc = _conv_bwd_in(dpre, conv_w)

    dqn = _mm(dq, w_uq, "nn", a_blk=True, b_blk=True, fuse=HEADS, name="d_qn")
    dkvn = _mm(dkv, w_ukv, "nt", a_blk=True, b_blk=True, fuse=HEADS, name="d_kvn")
    g_uq = _mm(dq, qkvn, "tn", a_blk=True, out_blk=True, b_cols=(0, Q_RANK), out_dtype=MXU_DTYPE, name="g_uq")
    g_ukv = _mm(qkvn, dkv, "tn", b_blk=True, out_blk=True, a_cols=(Q_RANK, KV_RANK), out_dtype=MXU_DTYPE, name="g_ukv")
    heads_token = on_grads("heads", [g_uq, g_ukv, g_out.reshape(N_DEV, D_MODEL // N_DEV, D_MODEL)])
    dcqkv, g_qkv_norm = _rms_bwd(cqkv, qkv_norm_w + heads_token, [dqn, dkvn], out_dtype=MXU_DTYPE, name="qkv_norm_bwd")

    g_in = _win_from_segments(_mm(dcqkv, u, "tn", out_dtype=MXU_DTYPE, name="g_in_qkv"),
                              _mm(dz, u, "tn", out_dtype=MXU_DTYPE, name="g_in_z"),
                              _mm(dxbc, u, "tn", out_dtype=MXU_DTYPE, name="g_in_xbc"),
                              _mm(dsm, u, "tn", out_dtype=MXU_DTYPE, name="g_in_small"))
    in_token = on_grads("in", [g_in])
    du = _mm_sum([dsm + in_token.astype(dsm.dtype), dcqkv, dz, dxbc], [w_small, w_cqkv, w_z, w_xbc], name="d_u")
    dx, g_pre_mix = _rms_bwd(x, small["pre_mix_norm_w"], [du], res=dh1, name="pre_mix_norm_bwd")

    hl = slice(HEAD_LANE, HEAD_LANE + SSD_H)
    g_small = {"q_norm_w": g_qkv_norm[0, :Q_RANK], "kv_norm_w": g_qkv_norm[0, Q_RANK:], "conv_b": dwb[CONV_K],
               "dt_bias": dpar[0, hl], "a_log": dpar[1, hl], "d_skip": dpar[2, hl], "ssd_norm_w": g_ssd_norm,
               "attn_out_norm_w": g_attn_norm, "pre_mix_norm_w": g_pre_mix, "post_mix_norm_w": g_post_mix,
               "pre_ffn_norm_w": g_pre_ffn, "post_ffn_norm_w": g_post_ffn, "conv_w": dwb[:CONV_K]}
    return loss_blk[0, 0], dx, g_small


_WEIGHT_ORDER = ("w_in", "q_norm_w", "w_uq", "kv_norm_w", "w_ukv", "conv_w", "conv_b", "dt_bias", "a_log", "d_skip",
                 "ssd_norm_w", "attn_out_norm_w", "w_out", "pre_mix_norm_w", "post_mix_norm_w", "pre_ffn_norm_w",
                 "post_ffn_norm_w", "w_gate", "w_up", "w_down")


def kernel(x, positions, w_in, q_norm_w, w_uq, kv_norm_w, w_ukv, conv_w, conv_b, dt_bias, a_log, d_skip, ssd_norm_w, attn_out_norm_w, w_out, pre_mix_norm_w, post_mix_norm_w, pre_ffn_norm_w, post_ffn_norm_w, w_gate, w_up, w_down, loss_target, m_w_in, m_q_norm_w, m_w_uq, m_kv_norm_w, m_w_ukv, m_conv_w, m_conv_b, m_dt_bias, m_a_log, m_d_skip, m_ssd_norm_w, m_attn_out_norm_w, m_w_out, m_pre_mix_norm_w, m_post_mix_norm_w, m_pre_ffn_norm_w, m_post_ffn_norm_w, m_w_gate, m_w_up, m_w_down, v_w_in, v_q_norm_w, v_w_uq, v_kv_norm_w, v_w_ukv, v_conv_w, v_conv_b, v_dt_bias, v_a_log, v_d_skip, v_ssd_norm_w, v_attn_out_norm_w, v_w_out, v_pre_mix_norm_w, v_post_mix_norm_w, v_pre_ffn_norm_w, v_post_ffn_norm_w, v_w_gate, v_w_up, v_w_down):
    w = dict(w_in=w_in, q_norm_w=q_norm_w, w_uq=w_uq, kv_norm_w=kv_norm_w, w_ukv=w_ukv, conv_w=conv_w, conv_b=conv_b,
             dt_bias=dt_bias, a_log=a_log, d_skip=d_skip, ssd_norm_w=ssd_norm_w, attn_out_norm_w=attn_out_norm_w,
             w_out=w_out, pre_mix_norm_w=pre_mix_norm_w, post_mix_norm_w=post_mix_norm_w,
             pre_ffn_norm_w=pre_ffn_norm_w, post_ffn_norm_w=post_ffn_norm_w, w_gate=w_gate, w_up=w_up, w_down=w_down)
    m = dict(w_in=m_w_in, q_norm_w=m_q_norm_w, w_uq=m_w_uq, kv_norm_w=m_kv_norm_w, w_ukv=m_w_ukv, conv_w=m_conv_w,
             conv_b=m_conv_b, dt_bias=m_dt_bias, a_log=m_a_log, d_skip=m_d_skip, ssd_norm_w=m_ssd_norm_w,
             attn_out_norm_w=m_attn_out_norm_w, w_out=m_w_out, pre_mix_norm_w=m_pre_mix_norm_w,
             post_mix_norm_w=m_post_mix_norm_w, pre_ffn_norm_w=m_pre_ffn_norm_w, post_ffn_norm_w=m_post_ffn_norm_w,
             w_gate=m_w_gate, w_up=m_w_up, w_down=m_w_down)
    v = dict(w_in=v_w_in, q_norm_w=v_q_norm_w, w_uq=v_w_uq, kv_norm_w=v_kv_norm_w, w_ukv=v_w_ukv, conv_w=v_conv_w,
             conv_b=v_conv_b, dt_bias=v_dt_bias, a_log=v_a_log, d_skip=v_d_skip, ssd_norm_w=v_ssd_norm_w,
             attn_out_norm_w=v_attn_out_norm_w, w_out=v_w_out, pre_mix_norm_w=v_pre_mix_norm_w,
             post_mix_norm_w=v_post_mix_norm_w, pre_ffn_norm_w=v_pre_ffn_norm_w, post_ffn_norm_w=v_post_ffn_norm_w,
             w_gate=v_w_gate, w_up=v_w_up, w_down=v_w_down)
    w, m, v = ({k: t[0] for k, t in d.items()} for d in (w, m, v))
    me = 4 * lax.axis_index("x") + 2 * lax.axis_index("y") + lax.axis_index("c")
    groups = {"qkv_up": ("w_uq", "w_ukv"), "out": ("w_out",), "ffn_in": ("w_gate", "w_up"), "ffn_out": ("w_down",)}
    cshard = CONV_DIM // N_DEV
    for name in _TRANSPOSED:
        w[name], m[name], v[name] = w[name].T, m[name].T, v[name].T

    shards = [w["w_in"].astype(MXU_DTYPE),
              jnp.stack(_split3(w["conv_w"])).reshape(3 * CONV_K, cshard).astype(MXU_DTYPE)]
    w_in_g, cw = _all_gather(shards, name="gather_weights")
    cw = cw.astype(F32).reshape(N_DEV, 3, CONV_K, cshard)
    wg = {"w_in": w_in_g, "conv_w": jnp.transpose(cw[:, 0] + cw[:, 1] + cw[:, 2], (1, 0, 2)).reshape(CONV_K, CONV_DIM)}
    arriving, dep, started = {}, wg["conv_w"], jnp.zeros((), F32)
    small = {name: w[name] for name, _ in _SMALL if name != "conv_w"}
    for group in ("qkv_up", "out", "ffn_in", "ffn_out"):
        token, arriving[group] = _exchange_behind([w[name].astype(MXU_DTYPE) for name in groups[group]], False,
                                                  dep, group + "_weights")
        started = started + token
        dep = jnp.zeros((8, LANE), F32) + started
    small["pre_mix_norm_w"] = small["pre_mix_norm_w"] + started

    leaving = {}

    def on_grads(group, gs):
        token, leaving[group] = _exchange_behind(gs, True, jnp.zeros((8, LANE), F32), group + "_grads")
        return token

    loss_local, dx, g_small = _local_step(x[0], positions[0], loss_target[0], wg, small,
                                          lambda group, after: arriving[group](after), on_grads)
    recv = {}
    for group, names in (("ffn", ("w_gate", "w_up", "w_down")), ("heads", ("w_uq", "w_ukv", "w_out")), ("in", ("w_in",))):
        recv.update(zip(names, zip(*leaving[group](dx, place=False))))
    grads, deltas, new_m, new_v = {}, {}, {}, {}
    me1 = me.astype(jnp.int32).reshape(1)
    for name, (parts, own) in recv.items():
        outs = _adamw(parts, own, me1, w[name], m[name], v[name], name="adamw_" + name)
        if name in _TRANSPOSED:
            outs = [t.T for t in outs]
        grads[name], deltas[name], new_m[name], new_v[name] = outs

    def embed(t):
        return lax.dynamic_update_slice(jnp.zeros((CONV_K, CONV_DIM), F32), t, (0, me * cshard))

    mine_s = _pack_small(g_small).at[_SMALL_ROWS - 1, 0].set(loss_local)
    parts_s = _all_gather([mine_s], name="gather_small_grads")[0]
    packs = [_pack_small({**{n_: d[n_] for n_, _ in _SMALL if n_ != "conv_w"}, "conv_w": embed(d["conv_w"])})
             for d in (w, m, v)]
    summed = _adamw_small(parts_s, *packs)
    loss = summed[0][_SMALL_ROWS - 1, 0]
    outs = [_unpack_small(t) for t in summed]
    for name, n in _SMALL:
        for dst, src in zip((grads, deltas, new_m, new_v), outs):
            if name == "conv_w":
                dst[name] = lax.dynamic_slice(src[name].reshape(CONV_K, CONV_DIM), (0, me * cshard), (CONV_K, cshard))
            else:
                dst[name] = src[name]

    def lead(d):
        return [d[name][None] for name in _WEIGHT_ORDER]

    return (loss, dx[None], *lead(grads), *lead(deltas), *lead(new_m), *lead(new_v))
```

```python
import numpy as np

import jax
import jax.numpy as jnp
from jax import lax
from jax.experimental import pallas as pl
from jax.experimental.pallas import tpu as pltpu

F32 = jnp.float32
BF16 = jnp.bfloat16
MXU_DTYPE = jnp.bfloat16
EPS = 1e-6
VMEM_LIMIT_BYTES = 48 * 1024 * 1024
VMEM_LIMIT_WIDE_BYTES = 56 * 1024 * 1024
K_TILE_MAX = 2048

N_DEV = 8
D_MODEL = 2048
Q_RANK = 512
KV_RANK = 512
ROPE = 64
HALF = ROPE // 2
HEADS = 8
NOPE = 128
VDIM = 128
QK = NOPE + ROPE
SSD_W = 1024
SSD_H = 16
SSD_P = 64
SSD_G = 2
SSD_E = SSD_H // SSD_G
SSD_N = 128
CHUNK = 128
CONV_K = 4
CONV_DIM = SSD_W + 2 * SSD_G * SSD_N
B_OFF = SSD_W
C_OFF = SSD_W + SSD_G * SSD_N
D_FF = 5632
D_IN = Q_RANK + KV_RANK + ROPE + SSD_W + CONV_DIM + SSD_H
ROPE_THETA = 10000.0
LANE = 128
HEAD_LANE = ROPE

ADAM_LR = 0.001
ADAM_B1 = 0.9
ADAM_B2 = 0.999
ADAM_EPS = 1e-08
ADAM_WD = 0.01
ADAM_STEP = 10


def _pick(n, cands):
    for c in cands:
        if n % c == 0:
            return c
    return n


def _params(*sem, vmem=VMEM_LIMIT_BYTES):
    return pltpu.CompilerParams(dimension_semantics=sem, vmem_limit_bytes=vmem)


def _sigmoid(x):
    return 1.0 / (1.0 + jnp.exp(-x))


def _silu(x):
    return x * _sigmoid(x)


def _dsilu(x):
    s = _sigmoid(x)
    return s * (1.0 + x * (1.0 - s))


def _softplus(x):
    e = jnp.exp(-jnp.abs(x))
    small = e * (1.0 - e * (0.5 - e * (1.0 / 3.0)))
    return jnp.maximum(x, 0.0) + jnp.where(e < 0.01, small, jnp.log(1.0 + e))


def _dot(a, b, ca, cb):
    return lax.dot_general(a, b, (((ca,), (cb,)), ((), ())), preferred_element_type=F32)


def _mx(v):
    return v.astype(MXU_DTYPE)


def _split3(a):
    hi = a.astype(BF16)
    r1 = a - hi.astype(F32)
    mid = r1.astype(BF16)
    lo = (r1 - mid.astype(F32)).astype(BF16)
    return hi, mid, lo


def _exact_dot(a, b, ca, cb, split_a):
    if split_a:
        return sum(_dot(p, b, ca, cb) for p in _split3(a))
    return sum(_dot(a, p, ca, cb) for p in _split3(b))


MM_ROW_GROUPS = 4


def _row_slices(tm, align):
    ng = MM_ROW_GROUPS
    while ng > 1 and (tm % ng or (tm // ng) % align):
        ng //= 2
    return [slice(g * (tm // ng), (g + 1) * (tm // ng)) for g in range(ng)]


def _mm(a, b, mode, *, a_blk=False, b_blk=False, out_blk=False, a_cols=None, b_cols=None, add=None, out_dtype=F32,
        fuse=1, wide=False, tm_max=1024, name="mm"):
    a2, b2 = a.shape[-2:], b.shape[-2:]
    a_last = a2[1] if a_cols is None else a_cols[1]
    a_start = 0 if a_cols is None else a_cols[0]
    b_start = 0
    if b_cols is not None:
        assert mode != "nt"
        b_start, b2 = b_cols[0], (b2[0], b_cols[1])
    if mode == "nn":
        m, k, (k2, n) = a2[0], a_last, b2
    elif mode == "nt":
        m, k, (n, k2) = a2[0], a_last, b2
    else:
        k, m, (k2, n) = a2[0], a_last, b2
    assert k == k2, (a.shape, b.shape, mode)
    tm = _pick(m, tuple(c for c in (1024, 704, 512, 256, 128) if c <= tm_max))
    tn = _pick(n, ((2048,) if wide else ()) + (1024, 768, 704, 512, 256, 192, 128))
    k_max = 2 * K_TILE_MAX if mode == "tn" else K_TILE_MAX
    tk = k if k <= k_max else _pick(k, (K_TILE_MAX, 1024, 512))
    nk = k // tk
    jo = N_DEV if out_blk else 1
    reduce_blocks = a_blk and b_blk and not out_blk
    assert fuse == 1 or reduce_blocks
    jr = N_DEV // fuse if reduce_blocks else 1
    ca, cb = {"nn": (1, 0), "nt": (1, 1), "tn": (0, 0)}[mode]
    has_add = add is not None
    single = jr * nk == 1
    if mode == "tn":
        assert a_start % tm == 0
        a_block, a_idx = (tk, tm), (lambda i, kk: (kk, i + a_start // tm))
    else:
        assert a_start % tk == 0
        a_block, a_idx = (tm, tk), (lambda i, kk: (i, kk + a_start // tk))
    assert b_start % tn == 0
    b_block, b_idx = (((tn, tk), (lambda nn_, kk: (nn_, kk))) if mode == "nt"
                      else ((tk, tn), (lambda nn_, kk: (kk, nn_ + b_start // tn))))

    def blk_specs(blocked, block, idx, of_a, t):
        def pos(o, i, nn_, kk):
            return idx(i, kk) if of_a else idx(nn_, kk)
        if blocked:
            return pl.BlockSpec((None,) + block,
                                lambda o, i, nn_, r, kk: ((o if out_blk else r * fuse + t),) + pos(o, i, nn_, kk))
        return pl.BlockSpec(block, lambda o, i, nn_, r, kk: pos(o, i, nn_, kk))

    a_specs = [blk_specs(a_blk, a_block, a_idx, True, t) for t in range(fuse)]
    b_specs = [blk_specs(b_blk, b_block, b_idx, False, t) for t in range(fuse)]
    o_spec = (pl.BlockSpec((None, tm, tn), lambda o, i, nn_, r, kk: (o, i, nn_)) if out_blk
              else pl.BlockSpec((tm, tn), lambda o, i, nn_, r, kk: (i, nn_)))

    groups = _row_slices(tm, LANE if mode == "tn" else 16)

    def body(*refs):
        a_refs, b_refs = refs[:fuse], refs[fuse:2 * fuse]
        add_ref = refs[2 * fuse] if has_add else None
        o_ref = refs[2 * fuse + 1] if has_add else refs[2 * fuse]

        def partial(rs):
            out = None
            for t in range(fuse):
                av = a_refs[t][:, rs] if mode == "tn" else a_refs[t][rs, :]
                d = _dot(_mx(av), _mx(b_refs[t][...]), ca, cb)
                out = d if out is None else out + d
            return out

        if single:
            for rs in groups:
                res = partial(rs)
                if has_add:
                    res = res + add_ref[rs, :]
                o_ref[rs, :] = res.astype(o_ref.dtype)
            return
        acc = refs[-1]
        r, kk = pl.program_id(3), pl.program_id(4)

        @pl.when(jnp.logical_and(r == 0, kk == 0))
        def _():
            acc[...] = jnp.zeros_like(acc)

        for rs in groups:
            acc[rs, :] += partial(rs)

        @pl.when(jnp.logical_and(r == jr - 1, kk == nk - 1))
        def _():
            res = acc[...]
            if has_add:
                res = res + add_ref[...]
            o_ref[...] = res.astype(o_ref.dtype)

    out_shape = ((N_DEV, m, n) if out_blk else (m, n))
    return pl.pallas_call(
        body, name=name, grid=(jo, m // tm, n // tn, jr, nk),
        in_specs=a_specs + b_specs + ([o_spec] if has_add else []), out_specs=o_spec,
        out_shape=jax.ShapeDtypeStruct(out_shape, out_dtype),
        scratch_shapes=[] if single else [pltpu.VMEM((tm, tn), F32)],
        compiler_params=_params("parallel", "parallel", "parallel", "arbitrary", "arbitrary",
                                vmem=VMEM_LIMIT_WIDE_BYTES if wide else VMEM_LIMIT_BYTES),
    )(*((a,) * fuse + (b,) * fuse + ((add,) if has_add else ())))


def _mm_sum(a_list, b_list, name="mm_sum"):
    m, n = a_list[0].shape[0], b_list[0].shape[1]
    ns = len(a_list)
    tm = _pick(m, (1024, 512, 256, 128))
    tn = _pick(n, (1024, 512, 256, 128))
    groups = _row_slices(tm, 16)

    def body(*refs):
        a_refs, b_refs, o_ref = refs[:ns], refs[ns:2 * ns], refs[2 * ns]
        for rs in groups:
            acc = _dot(_mx(a_refs[0][rs, :]), _mx(b_refs[0][...]), 1, 0)
            for s in range(1, ns):
                acc = acc + _dot(_mx(a_refs[s][rs, :]), _mx(b_refs[s][...]), 1, 0)
            o_ref[rs, :] = acc

    return pl.pallas_call(
        body, name=name, grid=(m // tm, n // tn),
        in_specs=([pl.BlockSpec((tm, a.shape[1]), lambda i, j: (i, 0)) for a in a_list]
                  + [pl.BlockSpec((b.shape[0], tn), lambda i, j: (0, j)) for b in b_list]),
        out_specs=pl.BlockSpec((tm, tn), lambda i, j: (i, j)),
        out_shape=jax.ShapeDtypeStruct((m, n), F32), compiler_params=_params("parallel", "parallel"),
    )(*a_list, *b_list)


def _row_tile(r_, streams=4):
    return _pick(r_, ((512,) if streams <= 4 else ()) + (256, 128, 64, 32, 16, 8))


def _rms_fwd(t, w, groups=1, res=None, out_dtype=F32, name="rms_fwd"):
    r_, f = t.shape
    fg = f // groups
    tr = _row_tile(r_)
    has_res = res is not None

    def body(*refs):
        t_ref, w_ref = refs[0], refs[1]
        res_ref = refs[2] if has_res else None
        o_ref = refs[-1]
        for g in range(groups):
            sl = slice(g * fg, (g + 1) * fg)
            tv = t_ref[:, sl].astype(F32)
            r = lax.rsqrt(jnp.mean(tv * tv, axis=-1, keepdims=True) + EPS)
            y = tv * r * w_ref[:, sl]
            if has_res:
                y = y + res_ref[:, sl]
            o_ref[:, sl] = y.astype(o_ref.dtype)

    row = pl.BlockSpec((tr, f), lambda i: (i, 0))
    wsp = pl.BlockSpec((1, f), lambda i: (0, 0))
    return pl.pallas_call(
        body, name=name, grid=(r_ // tr,),
        in_specs=[row, wsp] + ([row] if has_res else []), out_specs=row,
        out_shape=jax.ShapeDtypeStruct((r_, f), out_dtype),
        compiler_params=_params("parallel"),
    )(*((t, w.reshape(1, f)) + ((res,) if has_res else ())))


def _rms_bwd(t, w, dys, res=None, out_dtype=F32, name="rms_bwd"):
    r_, f = t.shape
    groups = len(dys)
    fg = f // groups
    tr = _row_tile(r_)
    has_res = res is not None

    def body(*refs):
        t_ref, w_ref = refs[0], refs[1]
        dy_refs = refs[2:2 + groups]
        res_ref = refs[2 + groups] if has_res else None
        dt_ref, dw_ref = refs[-2], refs[-1]

        @pl.when(pl.program_id(0) == 0)
        def _():
            dw_ref[...] = jnp.zeros_like(dw_ref)

        for g in range(groups):
            sl = slice(g * fg, (g + 1) * fg)
            tv = t_ref[:, sl].astype(F32)
            dyv = dy_refs[g][...].astype(F32)
            r = lax.rsqrt(jnp.mean(tv * tv, axis=-1, keepdims=True) + EPS)
            gw = dyv * w_ref[:, sl]
            c = jnp.mean(gw * tv, axis=-1, keepdims=True)
            dt = r * gw - tv * (r * r * r * c)
            if has_res:
                dt = dt + res_ref[:, sl]
            dt_ref[:, sl] = dt.astype(dt_ref.dtype)
            dw_ref[:, sl] += jnp.sum(dyv * tv * r, axis=0, keepdims=True)

    row = pl.BlockSpec((tr, f), lambda i: (i, 0))
    grow = pl.BlockSpec((tr, fg), lambda i: (i, 0))
    wsp = pl.BlockSpec((1, f), lambda i: (0, 0))
    return pl.pallas_call(
        body, name=name, grid=(r_ // tr,),
        in_specs=[row, wsp] + [grow] * groups + ([row] if has_res else []), out_specs=[row, wsp],
        out_shape=[jax.ShapeDtypeStruct((r_, f), out_dtype), jax.ShapeDtypeStruct((1, f), F32)],
        compiler_params=_params("arbitrary"),
    )(*((t, w.reshape(1, f)) + tuple(dys) + ((res,) if has_res else ())))


def _norm_res_norm(t, res, w1, w2, name="post_mix_pre_ffn_norm"):
    r_, f = t.shape
    tr = _row_tile(r_)

    def body(t_ref, res_ref, w1_ref, w2_ref, h_ref, v_ref):
        tv = t_ref[...]
        h = res_ref[...] + tv * lax.rsqrt(jnp.mean(tv * tv, axis=-1, keepdims=True) + EPS) * w1_ref[...]
        h_ref[...] = h
        v_ref[...] = (h * lax.rsqrt(jnp.mean(h * h, axis=-1, keepdims=True) + EPS) * w2_ref[...]).astype(v_ref.dtype)

    row = pl.BlockSpec((tr, f), lambda i: (i, 0))
    wsp = pl.BlockSpec((1, f), lambda i: (0, 0))
    return pl.pallas_call(
        body, name=name, grid=(r_ // tr,), in_specs=[row, row, wsp, wsp], out_specs=[row, row],
        out_shape=[jax.ShapeDtypeStruct((r_, f), F32), jax.ShapeDtypeStruct((r_, f), MXU_DTYPE)],
        compiler_params=_params("parallel"),
    )(t, res, w1.reshape(1, f), w2.reshape(1, f))


def _norm_res_norm_bwd(h, w2, dv, dres, t, w1, name="pre_ffn_post_mix_norm_bwd"):
    r_, f = h.shape
    tr = _row_tile(r_, streams=6)

    def body(h_ref, w2_ref, dv_ref, dres_ref, t_ref, w1_ref, dh_ref, dt_ref, dw2_ref, dw1_ref):
        @pl.when(pl.program_id(0) == 0)
        def _():
            dw2_ref[...] = jnp.zeros_like(dw2_ref)
            dw1_ref[...] = jnp.zeros_like(dw1_ref)

        def rms_bwd(tv, wv, dyv):
            r = lax.rsqrt(jnp.mean(tv * tv, axis=-1, keepdims=True) + EPS)
            gw = dyv * wv
            c = jnp.mean(gw * tv, axis=-1, keepdims=True)
            return r * gw - tv * (r * r * r * c), jnp.sum(dyv * tv * r, axis=0, keepdims=True)

        d1, g2 = rms_bwd(h_ref[...], w2_ref[...], dv_ref[...])
        dh = d1 + dres_ref[...]
        dh_ref[...] = dh
        dw2_ref[...] += g2
        d2, g1 = rms_bwd(t_ref[...], w1_ref[...], dh)
        dt_ref[...] = d2.astype(dt_ref.dtype)
        dw1_ref[...] += g1

    row = pl.BlockSpec((tr, f), lambda i: (i, 0))
    wsp = pl.BlockSpec((1, f), lambda i: (0, 0))
    return pl.pallas_call(
        body, name=name, grid=(r_ // tr,), in_specs=[row, wsp, row, row, row, wsp], out_specs=[row, row, wsp, wsp],
        out_shape=[jax.ShapeDtypeStruct((r_, f), F32), jax.ShapeDtypeStruct((r_, f), MXU_DTYPE),
                   jax.ShapeDtypeStruct((1, f), F32), jax.ShapeDtypeStruct((1, f), F32)],
        compiler_params=_params("arbitrary"),
    )(h, w2.reshape(1, f), dv, dres, t, w1.reshape(1, f))


def _hnorm_fwd(o, w, width, name="attn_out_norm"):
    h, s_, v = o.shape
    tr = _row_tile(s_)

    def body(o_ref, w_ref, y_ref):
        ss = jnp.sum(o_ref[0] * o_ref[0], axis=-1, keepdims=True)
        for i in range(1, h):
            ss = ss + jnp.sum(o_ref[i] * o_ref[i], axis=-1, keepdims=True)
        r = lax.rsqrt(ss * (1.0 / (h * v)) + EPS)
        for i in range(h):
            sl = slice(i * v, (i + 1) * v)
            y_ref[:, sl] = (o_ref[i] * r * w_ref[:, sl]).astype(y_ref.dtype)

    return pl.pallas_call(
        body, name=name, grid=(s_ // tr,),
        in_specs=[pl.BlockSpec((h, tr, v), lambda i: (0, i, 0)), pl.BlockSpec((1, h * v), lambda i: (0, 0))],
        out_specs=pl.BlockSpec((tr, h * v), lambda i: (i, 0)),
        out_shape=jax.ShapeDtypeStruct((s_, width), MXU_DTYPE), compiler_params=_params("parallel"),
    )(o, w)


def _hnorm_bwd(o, w, dy, name="attn_out_norm_bwd"):
    h, s_, v = o.shape
    tr = _row_tile(s_)

    def body(o_ref, w_ref, dy_ref, do_ref, delta_ref, dw_ref):
        @pl.when(pl.program_id(0) == 0)
        def _():
            dw_ref[...] = jnp.zeros_like(dw_ref)

        ss = jnp.zeros((tr, 1), F32)
        cc = jnp.zeros((tr, 1), F32)
        for i in range(h):
            sl = slice(i * v, (i + 1) * v)
            ov = o_ref[i]
            ss = ss + jnp.sum(ov * ov, axis=-1, keepdims=True)
            cc = cc + jnp.sum(dy_ref[:, sl] * w_ref[:, sl] * ov, axis=-1, keepdims=True)
        r = lax.rsqrt(ss * (1.0 / (h * v)) + EPS)
        c = cc * (1.0 / (h * v))
        for i in range(h):
            sl = slice(i * v, (i + 1) * v)
            ov = o_ref[i]
            dyv = dy_ref[:, sl]
            dov = r * dyv * w_ref[:, sl] - ov * (r * r * r * c)
            do_ref[i] = dov.astype(do_ref.dtype)
            delta_ref[i] = jnp.sum(dov * ov, axis=-1, keepdims=True)
            dw_ref[:, sl] += jnp.sum(dyv * ov * r, axis=0, keepdims=True)

    blk = pl.BlockSpec((h, tr, v), lambda i: (0, i, 0))
    wsp = pl.BlockSpec((1, h * v), lambda i: (0, 0))
    return pl.pallas_call(
        body, name=name, grid=(s_ // tr,),
        in_specs=[blk, wsp, pl.BlockSpec((tr, h * v), lambda i: (i, 0))],
        out_specs=[blk, pl.BlockSpec((h, tr, 1), lambda i: (0, i, 0)), wsp],
        out_shape=[jax.ShapeDtypeStruct(o.shape, MXU_DTYPE), jax.ShapeDtypeStruct((h, s_, 1), F32),
                   jax.ShapeDtypeStruct((1, h * v), F32)],
        compiler_params=_params("arbitrary"),
    )(o, w, dy)


def _loss_head(ffn, h1, target, w, name="loss_head"):
    r_, f = ffn.shape
    tr = _row_tile(r_)

    def body(ffn_ref, h1_ref, tg_ref, w_ref, loss_ref, dy_ref, dffn_ref, dw_ref):
        @pl.when(pl.program_id(0) == 0)
        def _():
            dw_ref[...] = jnp.zeros_like(dw_ref)
            loss_ref[...] = jnp.zeros_like(loss_ref)

        tv = ffn_ref[...]
        wv = w_ref[...]
        r = lax.rsqrt(jnp.mean(tv * tv, axis=-1, keepdims=True) + EPS)
        tn = tv * r
        e = h1_ref[...] + tn * wv - tg_ref[...]
        tot = jnp.sum(jnp.sum(e * e, axis=1, keepdims=True), axis=0, keepdims=True) * (0.5 / f)
        loss_ref[...] += tot + jnp.zeros_like(loss_ref)
        dyv = e * (1.0 / f)
        dy_ref[...] = dyv
        gw = dyv * wv
        c = jnp.mean(gw * tv, axis=-1, keepdims=True)
        dffn_ref[...] = (r * gw - tv * (r * r * r * c)).astype(dffn_ref.dtype)
        dw_ref[...] += jnp.sum(dyv * tn, axis=0, keepdims=True)

    row = pl.BlockSpec((tr, f), lambda i: (i, 0))
    wsp = pl.BlockSpec((1, f), lambda i: (0, 0))
    lsp = pl.BlockSpec((1, LANE), lambda i: (0, 0))
    return pl.pallas_call(
        body, name=name, grid=(r_ // tr,),
        in_specs=[row, row, row, wsp], out_specs=[lsp, row, row, wsp],
        out_shape=[jax.ShapeDtypeStruct((1, LANE), F32), jax.ShapeDtypeStruct((r_, f), F32),
                   jax.ShapeDtypeStruct((r_, f), MXU_DTYPE), jax.ShapeDtypeStruct((1, f), F32)],
        compiler_params=_params("arbitrary"),
    )(ffn, h1, target, w.reshape(1, f))


def _rot_matrix():
    p = np.zeros((ROPE, ROPE), np.float32)
    for i in range(HALF):
        p[i + HALF, i] = -1.0
        p[i, i + HALF] = 1.0
    return jnp.asarray(p, BF16)


def _rope_val(r, c2, s2, rot):
    hi, mid, _ = _split3(r)
    return r * c2 + (_dot(hi, rot, 1, 0) + _dot(mid, rot, 1, 0)) * s2


def _q_prep(q, cos2, sin2, scale, name):
    h, s_, _ = q.shape
    tr = _pick(s_, (4096, 2048, 1024, 512, 256, 128, 64, 32, 16))

    def body(q_ref, c_ref, s_ref, rot_ref, o_ref):
        for rs in _row_slices(tr, 16):
            x = q_ref[rs, :]
            o_ref[rs, :NOPE] = (x[:, :NOPE] * scale).astype(o_ref.dtype)
            o_ref[rs, NOPE:] = (_rope_val(x[:, NOPE:], c_ref[rs, :], s_ref[rs, :], rot_ref[...]) * scale).astype(o_ref.dtype)

    blk = pl.BlockSpec((None, tr, QK), lambda hh, i: (hh, i, 0))
    csp = pl.BlockSpec((tr, ROPE), lambda hh, i: (i, 0))
    return pl.pallas_call(
        body, name=name, grid=(h, s_ // tr),
        in_specs=[blk, csp, csp, pl.BlockSpec((ROPE, ROPE), lambda hh, i: (0, 0))], out_specs=blk,
        out_shape=jax.ShapeDtypeStruct(q.shape, MXU_DTYPE), compiler_params=_params("parallel", "parallel"),
    )(q, cos2, sin2, _rot_matrix())


def _q_up(qkvn, w_uq_t, cos2, sin2, scale, name="q_up"):
    s_ = qkvn.shape[0]
    h = w_uq_t.shape[0]
    tm = _pick(s_, (4096, 2048, 1024, 512, 256, 128))

    def body(a_ref, w_ref, c_ref, s_ref, rot_ref, o_ref):
        for rs in _row_slices(tm, 16):
            x = _dot(_mx(a_ref[rs, :]), _mx(w_ref[...]), 1, 1)
            o_ref[rs, :NOPE] = (x[:, :NOPE] * scale).astype(o_ref.dtype)
            o_ref[rs, NOPE:] = (_rope_val(x[:, NOPE:], c_ref[rs, :], s_ref[rs, :], rot_ref[...]) * scale).astype(o_ref.dtype)

    csp = pl.BlockSpec((tm, ROPE), lambda j, i: (i, 0))
    return pl.pallas_call(
        body, name=name, grid=(h, s_ // tm),
        in_specs=[pl.BlockSpec((tm, Q_RANK), lambda j, i: (i, 0)), pl.BlockSpec((None, QK, Q_RANK), lambda j, i: (j, 0, 0)),
                  csp, csp, pl.BlockSpec((ROPE, ROPE), lambda j, i: (0, 0))],
        out_specs=pl.BlockSpec((None, tm, QK), lambda j, i: (j, i, 0)),
        out_shape=jax.ShapeDtypeStruct((h, s_, QK), MXU_DTYPE), compiler_params=_params("parallel", "parallel"),
    )(qkvn, w_uq_t, cos2, sin2, _rot_matrix())


def _kv_up(qkvn, w_ukv, small, cos2, sin2, name="kv_up"):
    s_ = qkvn.shape[0]
    h = w_ukv.shape[0]
    tm = _pick(s_, (4096, 2048, 1024, 512, 256, 128))

    def body(a_ref, w_ref, sm_ref, c_ref, s_ref, rot_ref, k_ref, v_ref):
        for rs in _row_slices(tm, 16):
            x = _dot(_mx(a_ref[rs, :]), _mx(w_ref[...]), 1, 0)
            k_ref[rs, :NOPE] = x[:, :NOPE].astype(k_ref.dtype)
            k_ref[rs, NOPE:] = _rope_val(sm_ref[rs, :ROPE], c_ref[rs, :], s_ref[rs, :], rot_ref[...]).astype(k_ref.dtype)
            v_ref[rs, :] = x[:, NOPE:].astype(v_ref.dtype)

    csp = pl.BlockSpec((tm, ROPE), lambda j, i: (i, 0))
    return pl.pallas_call(
        body, name=name, grid=(h, s_ // tm),
        in_specs=[pl.BlockSpec((tm, KV_RANK), lambda j, i: (i, Q_RANK // KV_RANK)),
                  pl.BlockSpec((None, KV_RANK, NOPE + VDIM), lambda j, i: (j, 0, 0)),
                  pl.BlockSpec((tm, LANE), lambda j, i: (i, 0)), csp, csp, pl.BlockSpec((ROPE, ROPE), lambda j, i: (0, 0))],
        out_specs=[pl.BlockSpec((None, tm, QK), lambda j, i: (j, i, 0)), pl.BlockSpec((None, tm, VDIM), lambda j, i: (j, i, 0))],
        out_shape=[jax.ShapeDtypeStruct((h, s_, QK), MXU_DTYPE), jax.ShapeDtypeStruct((h, s_, VDIM), MXU_DTYPE)],
        compiler_params=_params("parallel", "parallel"),
    )(qkvn, w_ukv, small, cos2, sin2, _rot_matrix())


def _dkv_post(dk, dv, ddt, cos2, nsin2, name="dkv_post"):
    h, s_, _ = dk.shape
    tr = _row_tile(s_)

    def body(dk_ref, dv_ref, ddt_ref, c_ref, s_ref, rot_ref, dkv_ref, dsm_ref):
        acc = dk_ref[0, :, NOPE:]
        for i in range(1, h):
            acc = acc + dk_ref[i, :, NOPE:]
        dsm_ref[:, :ROPE] = _rope_val(acc, c_ref[...], s_ref[...], rot_ref[...]).astype(dsm_ref.dtype)
        dsm_ref[:, ROPE:] = ddt_ref[:, ROPE:].astype(dsm_ref.dtype)
        for i in range(h):
            dkv_ref[i, :, :NOPE] = dk_ref[i, :, :NOPE].astype(dkv_ref.dtype)
            dkv_ref[i, :, NOPE:] = dv_ref[i].astype(dkv_ref.dtype)

    csp = pl.BlockSpec((tr, ROPE), lambda i: (i, 0))
    return pl.pallas_call(
        body, name=name, grid=(s_ // tr,),
        in_specs=[pl.BlockSpec((h, tr, QK), lambda i: (0, i, 0)), pl.BlockSpec((h, tr, VDIM), lambda i: (0, i, 0)),
                  pl.BlockSpec((tr, LANE), lambda i: (i, 0)), csp, csp, pl.BlockSpec((ROPE, ROPE), lambda i: (0, 0))],
        out_specs=[pl.BlockSpec((h, tr, NOPE + VDIM), lambda i: (0, i, 0)), pl.BlockSpec((tr, LANE), lambda i: (i, 0))],
        out_shape=[jax.ShapeDtypeStruct((h, s_, NOPE + VDIM), MXU_DTYPE), jax.ShapeDtypeStruct((s_, LANE), MXU_DTYPE)],
        compiler_params=_params("parallel"),
    )(dk, dv, ddt, cos2, nsin2, _rot_matrix())


def _attn_tile(s):
    return 2048 if s % 4096 == 0 else s // 2


def _pairs(n, by_key):
    if by_key:
        pr = [(i, j) for j in range(n) for i in range(j, n)]
    else:
        pr = [(i, j) for i in range(n) for j in range(i + 1)]
    return (jnp.asarray([p[0] for p in pr], jnp.int32), jnp.asarray([p[1] for p in pr], jnp.int32))


ATTN_ROW_GROUPS = 8


def _row_groups(t, diag):
    tg = t // ATTN_ROW_GROUPS
    out = []
    for r in range(ATTN_ROW_GROUPS):
        nc = (r + 1) * tg if diag else t
        mask = None
        if diag:
            mask = (lax.broadcasted_iota(jnp.int32, (tg, nc), 1)
                    <= lax.broadcasted_iota(jnp.int32, (tg, nc), 0) + r * tg)
        out.append((slice(r * tg, (r + 1) * tg), nc, mask))
    return out


def _flash_specs(t, dk, dv):
    qsp = pl.BlockSpec((None, t, dk), lambda hh, p, qi, kj: (hh, qi[p], 0))
    ksp = pl.BlockSpec((None, t, dk), lambda hh, p, qi, kj: (hh, kj[p], 0))
    vsp = pl.BlockSpec((None, t, dv), lambda hh, p, qi, kj: (hh, kj[p], 0))
    osp = pl.BlockSpec((None, t, dv), lambda hh, p, qi, kj: (hh, qi[p], 0))
    lsp = pl.BlockSpec((None, t, 1), lambda hh, p, qi, kj: (hh, qi[p], 0))
    return qsp, ksp, vsp, osp, lsp


def _flash_fwd(q, k, v, name="flash_fwd"):
    h, s_, dk = q.shape
    dv = v.shape[-1]
    t = _attn_tile(s_)
    n = s_ // t
    qi, kj = _pairs(n, False)

    def body(qi_ref, kj_ref, q_ref, k_ref, v_ref, o_ref, lse_ref, m_s, l_s, acc):
        p_ = pl.program_id(1)
        i, j = qi_ref[p_], kj_ref[p_]

        @pl.when(j == 0)
        def _():
            m_s[...] = jnp.full_like(m_s, -jnp.inf)
            l_s[...] = jnp.zeros_like(l_s)
            acc[...] = jnp.zeros_like(acc)

        def update(diag):
            for rs, nc, mask in _row_groups(t, diag):
                sc = _dot(q_ref[rs, :], k_ref[0:nc, :], 1, 1)
                if mask is not None:
                    sc = jnp.where(mask, sc, -jnp.inf)
                m_old = m_s[rs, :]
                m_new = jnp.maximum(m_old, jnp.max(sc, axis=1, keepdims=True))
                alpha = jnp.exp(m_old - m_new)
                p = jnp.exp(sc - m_new)
                l_s[rs, :] = alpha * l_s[rs, :] + jnp.sum(p, axis=1, keepdims=True)
                acc[rs, :] = alpha * acc[rs, :] + _dot(_mx(p), v_ref[0:nc, :], 1, 0)
                m_s[rs, :] = m_new

        @pl.when(j < i)
        def _():
            update(False)

        @pl.when(j == i)
        def _():
            update(True)
            o_ref[...] = acc[...] / l_s[...]
            lse_ref[...] = m_s[...] + jnp.log(l_s[...])

    qsp, ksp, vsp, osp, lsp = _flash_specs(t, dk, dv)
    gs = pltpu.PrefetchScalarGridSpec(
        num_scalar_prefetch=2, grid=(h, qi.shape[0]), in_specs=[qsp, ksp, vsp], out_specs=[osp, lsp],
        scratch_shapes=[pltpu.VMEM((t, 1), F32), pltpu.VMEM((t, 1), F32), pltpu.VMEM((t, dv), F32)])
    return pl.pallas_call(
        body, name=name, grid_spec=gs,
        out_shape=[jax.ShapeDtypeStruct((h, s_, dv), F32), jax.ShapeDtypeStruct((h, s_, 1), F32)],
        compiler_params=_params("parallel", "arbitrary"),
    )(qi, kj, q, k, v)


def _flash_bwd(q, k, v, do, lse, delta, name="flash_bwd"):
    h, s_, dk = q.shape
    dv = v.shape[-1]
    t = _attn_tile(s_)
    tg = t // ATTN_ROW_GROUPS
    n = s_ // t
    qi, kj = _pairs(n, True)

    def body(qi_ref, kj_ref, q_ref, k_ref, v_ref, do_ref, lse_ref, delta_ref, dq_ref, dk_ref, dv_ref, dk_acc, dv_acc):
        p_ = pl.program_id(1)
        i, j = qi_ref[p_], kj_ref[p_]

        @pl.when(p_ == 0)
        def _():
            dq_ref[...] = jnp.zeros_like(dq_ref)

        def update(diag):
            for g, (rs, nc, mask) in enumerate(_row_groups(t, diag)):
                sc = _dot(q_ref[rs, :], k_ref[0:nc, :], 1, 1)
                if mask is not None:
                    sc = jnp.where(mask, sc, -jnp.inf)
                p = jnp.exp(sc - lse_ref[rs, :])
                dob = _mx(do_ref[rs, :])
                dv_acc[0:nc, :] += _dot(_mx(p), dob, 0, 0)
                dp = _dot(dob, v_ref[0:nc, :], 1, 1)
                dsb = _mx(p * (dp - delta_ref[rs, :]))
                dk_acc[0:nc, :] += _dot(dsb, q_ref[rs, :], 0, 0)
                rows = pl.ds(pl.multiple_of(i * t + g * tg, tg), tg)
                dq_ref[rows, :] += _dot(dsb, k_ref[0:nc, :], 1, 0)

        @pl.when(i == j)
        def _():
            dk_acc[...] = jnp.zeros_like(dk_acc)
            dv_acc[...] = jnp.zeros_like(dv_acc)
            update(True)

        @pl.when(i > j)
        def _():
            update(False)

        @pl.when(i == n - 1)
        def _():
            dk_ref[...] = dk_acc[...]
            dv_ref[...] = dv_acc[...]

    qsp, ksp, vsp, osp, lsp = _flash_specs(t, dk, dv)
    dqsp = pl.BlockSpec((None, s_, dk), lambda hh, p, qi, kj: (hh, 0, 0))
    gs = pltpu.PrefetchScalarGridSpec(
        num_scalar_prefetch=2, grid=(h, qi.shape[0]), in_specs=[qsp, ksp, vsp, osp, lsp, lsp],
        out_specs=[dqsp, ksp, vsp],
        scratch_shapes=[pltpu.VMEM((t, dk), F32), pltpu.VMEM((t, dv), F32)])
    return pl.pallas_call(
        body, name=name, grid_spec=gs,
        out_shape=[jax.ShapeDtypeStruct((h, s_, dk), F32), jax.ShapeDtypeStruct((h, s_, dk), F32),
                   jax.ShapeDtypeStruct((h, s_, dv), F32)],
        compiler_params=_params("parallel", "arbitrary"),
    )(qi, kj, q, k, v, do, lse, delta)


HALO = 8


def _conv_specs(s_, c, tr, after):
    main = pl.BlockSpec((tr, c), lambda i: (i, 0))
    per = tr // HALO
    if after:
        halo = pl.BlockSpec((HALO, c), lambda i: (jnp.minimum((i + 1) * per, s_ // HALO - 1), 0))
    else:
        halo = pl.BlockSpec((HALO, c), lambda i: (jnp.maximum(i * per - 1, 0), 0))
    return main, halo


def _fill_before(ext, t_ref, h_ref, tr):
    ext[0:HALO, :] = jnp.where(pl.program_id(0) > 0, h_ref[...], 0.0)
    ext[HALO:HALO + tr, :] = t_ref[...]


def _taps(ext, w_ref, tr):
    base = HALO - (CONV_K - 1)
    acc = ext[base:base + tr, :] * w_ref[0:1, :]
    for k in range(1, CONV_K):
        acc = acc + ext[base + k:base + k + tr, :] * w_ref[k:k + 1, :]
    return acc


def _conv_fwd(t, w, b, name="conv_fwd"):
    s_, c = t.shape
    tr = _row_tile(s_)

    def body(t_ref, h_ref, w_ref, b_ref, o_ref, ext):
        _fill_before(ext, t_ref, h_ref, tr)
        o_ref[...] = _silu(_taps(ext, w_ref, tr) + b_ref[...])

    main, halo = _conv_specs(s_, c, tr, False)
    return pl.pallas_call(
        body, name=name, grid=(s_ // tr,),
        in_specs=[main, halo, pl.BlockSpec((CONV_K, c), lambda i: (0, 0)), pl.BlockSpec((1, c), lambda i: (0, 0))],
        out_specs=main, out_shape=jax.ShapeDtypeStruct((s_, c), F32),
        scratch_shapes=[pltpu.VMEM((tr + HALO, c), F32)], compiler_params=_params("parallel"),
    )(t, t, w, b)


def _conv_bwd_pre(t, w, b, dact, name="conv_bwd_pre"):
    s_, c = t.shape
    tr = _row_tile(s_)

    def body(t_ref, h_ref, w_ref, b_ref, da_ref, dpre_ref, dwb_ref, ext):
        @pl.when(pl.program_id(0) == 0)
        def _():
            dwb_ref[...] = jnp.zeros_like(dwb_ref)

        _fill_before(ext, t_ref, h_ref, tr)
        dpre = da_ref[...] * _dsilu(_taps(ext, w_ref, tr) + b_ref[...])
        dpre_ref[...] = dpre
        base = HALO - (CONV_K - 1)
        for k in range(CONV_K):
            dwb_ref[k:k + 1, :] += jnp.sum(dpre * ext[base + k:base + k + tr, :], axis=0, keepdims=True)
        dwb_ref[CONV_K:CONV_K + 1, :] += jnp.sum(dpre, axis=0, keepdims=True)

    main, halo = _conv_specs(s_, c, tr, False)
    return pl.pallas_call(
        body, name=name, grid=(s_ // tr,),
        in_specs=[main, halo, pl.BlockSpec((CONV_K, c), lambda i: (0, 0)), pl.BlockSpec((1, c), lambda i: (0, 0)), main],
        out_specs=[main, pl.BlockSpec((8, c), lambda i: (0, 0))],
        out_shape=[jax.ShapeDtypeStruct((s_, c), F32), jax.ShapeDtypeStruct((8, c), F32)],
        scratch_shapes=[pltpu.VMEM((tr + HALO, c), F32)], compiler_params=_params("arbitrary"),
    )(t, t, w, b, dact)


def _conv_bwd_in(dpre, w, name="conv_bwd_in"):
    s_, c = dpre.shape
    tr = _row_tile(s_)
    nt = s_ // tr

    def body(d_ref, h_ref, w_ref, o_ref, ext):
        ext[0:tr, :] = d_ref[...]
        ext[tr:tr + HALO, :] = jnp.where(pl.program_id(0) < nt - 1, h_ref[...], 0.0)
        acc = ext[CONV_K - 1:CONV_K - 1 + tr, :] * w_ref[0:1, :]
        for k in range(1, CONV_K):
            acc = acc + ext[CONV_K - 1 - k:CONV_K - 1 - k + tr, :] * w_ref[k:k + 1, :]
        o_ref[...] = acc.astype(o_ref.dtype)

    main, halo = _conv_specs(s_, c, tr, True)
    return pl.pallas_call(
        body, name=name, grid=(nt,),
        in_specs=[main, halo, pl.BlockSpec((CONV_K, c), lambda i: (0, 0))],
        out_specs=main, out_shape=jax.ShapeDtypeStruct((s_, c), MXU_DTYPE),
        scratch_shapes=[pltpu.VMEM((tr + HALO, c), F32)], compiler_params=_params("parallel"),
    )(dpre, dpre, w)


def _ssd_chunk_common(dt_ref, dtt_ref, br_ref, bc_ref, ar_ref, ac_ref):
    li = lax.broadcasted_iota(jnp.int32, (CHUNK, CHUNK), 0)
    si = lax.broadcasted_iota(jnp.int32, (CHUNK, CHUNK), 1)
    lower = li >= si
    lower_b = lower.astype(BF16)
    upper_b = (li <= si).astype(BF16)
    zr = dt_ref[...] + br_ref[...]
    dtc = _softplus(zr)
    a_row = -jnp.exp(ar_ref[...])
    acum = _exact_dot(lower_b, dtc * a_row, 1, 0, False)
    dtt = _softplus(dtt_ref[...] + bc_ref[...])
    acum_t = _exact_dot(dtt * (-jnp.exp(ac_ref[...])), upper_b, 1, 0, True)
    return lower, upper_b, zr, dtc, a_row, acum, acum_t


def _head_terms(h, lower, dtc, acum, acum_t):
    lane = lax.broadcasted_iota(jnp.int32, (1, LANE), 1)
    sub = lax.broadcasted_iota(jnp.int32, (SSD_H, 1), 0)
    rowid = lax.broadcasted_iota(jnp.int32, (CHUNK, 1), 0)
    oh = (lane == HEAD_LANE + h).astype(F32)
    acol = jnp.sum(acum * oh, axis=1, keepdims=True)
    dcol = jnp.sum(dtc * oh, axis=1, keepdims=True)
    arow = jnp.sum(acum_t * (sub == h).astype(F32), axis=0, keepdims=True)
    alast = jnp.sum(jnp.where(rowid == CHUNK - 1, acol, 0.0), axis=0, keepdims=True)
    decay = jnp.exp(jnp.where(lower, acol - arow, -jnp.inf))
    return oh, acol, dcol, alast, decay


SSD_PAIRS = SSD_H // 2
PAIRS_PER_GROUP = SSD_E // 2


def _ps(q):
    return slice(q * LANE, (q + 1) * LANE)


def _gs(off, g):
    return slice(off + g * SSD_N, off + (g + 1) * SSD_N)


def _lanes(c0, c1):
    return jnp.where(lax.broadcasted_iota(jnp.int32, (1, LANE), 1) < SSD_P, c0, c1)


def _rows(c0, c1):
    return jnp.where(lax.broadcasted_iota(jnp.int32, (LANE, 1), 0) < SSD_P, c0, c1)


def _lane_halves(t):
    first = lax.broadcasted_iota(jnp.int32, (1, LANE), 1) < SSD_P
    return (jnp.sum(jnp.where(first, t, 0.0), axis=1, keepdims=True),
            jnp.sum(jnp.where(first, 0.0, t), axis=1, keepdims=True))


def _ssd_in_specs(rev):
    def ci(c):
        return c if rev is None else rev - c
    return [pl.BlockSpec((CHUNK, CONV_DIM), lambda c: (ci(c), 0)),
            pl.BlockSpec((CHUNK, LANE), lambda c: (ci(c), 0)),
            pl.BlockSpec((SSD_H, CHUNK), lambda c: (0, ci(c))),
            pl.BlockSpec((1, LANE), lambda c: (0, 0)), pl.BlockSpec((SSD_H, 1), lambda c: (0, 0)),
            pl.BlockSpec((1, LANE), lambda c: (0, 0)), pl.BlockSpec((SSD_H, 1), lambda c: (0, 0)),
            pl.BlockSpec((SSD_PAIRS, 1, LANE), lambda c: (0, 0, 0))]


def _ssd_fwd(xbc, small, dtt, bias_r, bias_c, alog_r, alog_c, dsk, name="ssd_fwd"):
    s_ = xbc.shape[0]
    nc = s_ // CHUNK

    def body(x_ref, dt_ref, dtt_ref, br_ref, bc_ref, ar_ref, ac_ref, dsk_ref, y_ref, prev_ref, state):
        @pl.when(pl.program_id(0) == 0)
        def _():
            state[...] = jnp.zeros_like(state)

        lower, _, _, dtc, _, acum, acum_t = _ssd_chunk_common(dt_ref, dtt_ref, br_ref, bc_ref, ar_ref, ac_ref)
        for g in range(SSD_G):
            bb = _mx(x_ref[:, _gs(B_OFF, g)])
            cb_ = _mx(x_ref[:, _gs(C_OFF, g)])
            cbm = _dot(cb_, bb, 1, 1)
            for e in range(PAIRS_PER_GROUP):
                q = g * PAIRS_PER_GROUP + e
                _, acol0, dcol0, alast0, decay0 = _head_terms(2 * q, lower, dtc, acum, acum_t)
                _, acol1, dcol1, alast1, decay1 = _head_terms(2 * q + 1, lower, dtc, acum, acum_t)
                x = x_ref[:, _ps(q)]
                xdt = x * _lanes(dcol0, dcol1)
                xb = _mx(xdt)
                yd = _lanes(_dot(_mx(cbm * decay0), xb, 1, 0), _dot(_mx(cbm * decay1), xb, 1, 0))
                prev = state[q]
                prev_ref[0, q] = prev
                yo = _dot(cb_, _mx(prev), 1, 1) * _lanes(jnp.exp(acol0), jnp.exp(acol1))
                ds = _lanes(jnp.exp(alast0 - acol0), jnp.exp(alast1 - acol1))
                st = _dot(_mx(xdt * ds), bb, 0, 0)
                state[q] = prev * _rows(jnp.exp(alast0), jnp.exp(alast1)) + st
                y_ref[:, _ps(q)] = yd + yo + x * dsk_ref[q]

    psp = pl.BlockSpec((1, SSD_PAIRS, LANE, SSD_N), lambda c: (c, 0, 0, 0))
    return pl.pallas_call(
        body, name=name, grid=(nc,),
        in_specs=_ssd_in_specs(None), out_specs=[pl.BlockSpec((CHUNK, SSD_W), lambda c: (c, 0)), psp],
        out_shape=[jax.ShapeDtypeStruct((s_, SSD_W), F32),
                   jax.ShapeDtypeStruct((nc, SSD_PAIRS, LANE, SSD_N), F32)],
        scratch_shapes=[pltpu.VMEM((SSD_PAIRS, LANE, SSD_N), F32)],
        compiler_params=_params("arbitrary"),
    )(xbc, small, dtt, bias_r, bias_c, alog_r, alog_c, dsk)


def _ssd_bwd(xbc, small, dtt, bias_r, bias_c, alog_r, alog_c, dsk, prev, dy, name="ssd_bwd"):
    s_ = xbc.shape[0]
    nc = s_ // CHUNK

    def body(x_ref, dt_ref, dtt_ref, br_ref, bc_ref, ar_ref, ac_ref, dsk_ref, prev_ref, dy_ref,
             dx_ref, ddt_ref, dpar_ref, dstate):
        @pl.when(pl.program_id(0) == 0)
        def _():
            dstate[...] = jnp.zeros_like(dstate)
            dpar_ref[...] = jnp.zeros_like(dpar_ref)

        lower, upper_b, zr, dtc, a_row, acum, acum_t = _ssd_chunk_common(
            dt_ref, dtt_ref, br_ref, bc_ref, ar_ref, ac_ref)
        strict = (lax.broadcasted_iota(jnp.int32, (CHUNK, CHUNK), 1)
                  < lax.broadcasted_iota(jnp.int32, (CHUNK, CHUNK), 0))
        strict_b = strict.astype(BF16)
        col2 = lax.broadcasted_iota(jnp.int32, (CHUNK, 2 * CHUNK), 1)
        strict2 = (jnp.where(col2 >= CHUNK, col2 - CHUNK, col2)
                   < lax.broadcasted_iota(jnp.int32, (CHUNK, 2 * CHUNK), 0))
        da_in = jnp.zeros((CHUNK, LANE), F32)
        r_off = jnp.zeros((CHUNK, LANE), F32)
        c_int = jnp.zeros((CHUNK, LANE), F32)
        c_row = jnp.zeros((1, LANE), F32)
        ddt = jnp.zeros((CHUNK, LANE), F32)
        dskip = jnp.zeros((1, LANE), F32)
        for g in range(SSD_G):
            bb = _mx(x_ref[:, _gs(B_OFF, g)])
            cb_ = _mx(x_ref[:, _gs(C_OFF, g)])
            cbm = _dot(cb_, bb, 1, 1)
            dcb = jnp.zeros((CHUNK, CHUNK), F32)
            dc_acc = jnp.zeros((CHUNK, SSD_N), F32)
            db_acc = jnp.zeros((CHUNK, SSD_N), F32)
            for e in range(PAIRS_PER_GROUP):
                q = g * PAIRS_PER_GROUP + e
                oh0, acol0, dcol0, alast0, decay0 = _head_terms(2 * q, lower, dtc, acum, acum_t)
                oh1, acol1, dcol1, alast1, decay1 = _head_terms(2 * q + 1, lower, dtc, acum, acum_t)
                x = x_ref[:, _ps(q)]
                dy = dy_ref[:, _ps(q)]
                dcol = _lanes(dcol0, dcol1)
                xdt = x * dcol
                xb = _mx(xdt)
                eacol = _lanes(jnp.exp(acol0), jnp.exp(acol1))
                ds = _lanes(jnp.exp(alast0 - acol0), jnp.exp(alast1 - acol1))
                ealast = _rows(jnp.exp(alast0), jnp.exp(alast1))
                dyb = _mx(dy)
                dyb0, dyb1 = _mx(_lanes(dy, 0.0)), _mx(_lanes(0.0, dy))
                dsh = dstate[q]
                dshb = _mx(dsh)
                prev = prev_ref[0, q]
                prevb = _mx(prev)
                dxdt_inter = ds * _dot(bb, dshb, 1, 1)
                dxdt = _lanes(_dot(_mx(cbm * decay0), dyb, 0, 0), _dot(_mx(cbm * decay1), dyb, 0, 0)) + dxdt_inter
                dwl0 = _dot(dyb0, xb, 1, 1) * decay0
                dwl1 = _dot(dyb1, xb, 1, 1) * decay1
                dcb = dcb + dwl0 + dwl1
                dyeb = _mx(dy * eacol)
                dc_acc = dc_acc + _dot(dyeb, prevb, 1, 0)
                db_acc = db_acc + _dot(_mx(xdt * ds), dshb, 1, 0)
                dstate[q] = _dot(dyeb, cb_, 0, 0) + ealast * dsh
                above = _exact_dot(upper_b, jnp.concatenate([dwl0 * cbm, dwl1 * cbm], axis=1), 1, 0, False)
                above = jnp.where(strict2, above, 0.0)
                da_in = (da_in + jnp.sum(above[:, :CHUNK], axis=1, keepdims=True) * oh0
                         + jnp.sum(above[:, CHUNK:], axis=1, keepdims=True) * oh1)
                y_off = _dot(cb_, prevb, 1, 1) * eacol
                r0, r1 = _lane_halves(dy * y_off)
                r_off = r_off + r0 * oh0 + r1 * oh1
                c0, c1 = _lane_halves(xdt * dxdt_inter)
                c_int = c_int + c0 * oh0 + c1 * oh1
                both = jnp.sum(dsh * prev, axis=1, keepdims=True) * ealast
                c_row = (c_row + jnp.sum(_rows(both, 0.0), axis=0, keepdims=True) * oh0
                         + jnp.sum(_rows(0.0, both), axis=0, keepdims=True) * oh1)
                t0, t1 = _lane_halves(dxdt * x)
                ddt = ddt + t0 * oh0 + t1 * oh1
                dx_ref[:, _ps(q)] = dxdt * dcol + dy * dsk_ref[q]
                k0, k1 = _lane_halves(dy * x)
                dskip = (dskip + jnp.sum(k0, axis=0, keepdims=True) * oh0 + jnp.sum(k1, axis=0, keepdims=True) * oh1)
            dcbb = _mx(dcb)
            dx_ref[:, _gs(C_OFF, g)] = dc_acc + _dot(dcbb, bb, 1, 0)
            dx_ref[:, _gs(B_OFF, g)] = db_acc + _dot(dcbb, cb_, 0, 0)
        da = (da_in + _exact_dot(upper_b, r_off, 1, 0, False) + _exact_dot(strict_b, c_int, 1, 0, False) + c_row)
        draw = (ddt + da * a_row) * _sigmoid(zr)
        ddt_ref[...] = draw
        dpar_ref[0:1, :] += jnp.sum(draw, axis=0, keepdims=True)
        dpar_ref[1:2, :] += jnp.sum(da * dtc, axis=0, keepdims=True) * a_row
        dpar_ref[2:3, :] += dskip

    rev = nc - 1
    psp = pl.BlockSpec((1, SSD_PAIRS, LANE, SSD_N), lambda c: (rev - c, 0, 0, 0))
    return pl.pallas_call(
        body, name=name, grid=(nc,),
        in_specs=_ssd_in_specs(rev) + [psp, pl.BlockSpec((CHUNK, SSD_W), lambda c: (rev - c, 0))],
        out_specs=[pl.BlockSpec((CHUNK, CONV_DIM), lambda c: (rev - c, 0)),
                   pl.BlockSpec((CHUNK, LANE), lambda c: (rev - c, 0)), pl.BlockSpec((8, LANE), lambda c: (0, 0))],
        out_shape=[jax.ShapeDtypeStruct((s_, CONV_DIM), F32), jax.ShapeDtypeStruct((s_, LANE), F32),
                   jax.ShapeDtypeStruct((8, LANE), F32)],
        scratch_shapes=[pltpu.VMEM((SSD_PAIRS, LANE, SSD_N), F32)],
        compiler_params=_params("arbitrary"),
    )(xbc, small, dtt, bias_r, bias_c, alog_r, alog_c, dsk, prev, dy)


GN = SSD_W // SSD_G


def _gated_norm_fwd(y, z, w, cat, name="gated_norm_fwd"):
    s_, f = y.shape
    tr = _row_tile(s_)

    def body(y_ref, z_ref, w_ref, cat_ref, o_ref):
        for g in range(SSD_G):
            sl = slice(g * GN, (g + 1) * GN)
            gg = y_ref[:, sl] * _silu(z_ref[:, sl])
            r = lax.rsqrt(jnp.mean(gg * gg, axis=-1, keepdims=True) + EPS)
            o_ref[:, sl] = (gg * r * w_ref[:, sl]).astype(o_ref.dtype)

    row = pl.BlockSpec((tr, f), lambda i: (i, 0))
    wsp = pl.BlockSpec((1, f), lambda i: (0, 0))
    return pl.pallas_call(
        body, name=name, grid=(s_ // tr,),
        in_specs=[row, row, wsp, pl.BlockSpec(memory_space=pl.ANY)], out_specs=pl.BlockSpec((tr, f), lambda i: (i, 1)),
        out_shape=jax.ShapeDtypeStruct(cat.shape, cat.dtype), input_output_aliases={3: 0},
        compiler_params=_params("parallel"),
    )(y, z, w.reshape(1, f), cat)


def _gated_norm_bwd(y, z, w, dout, name="gated_norm_bwd"):
    s_, f = y.shape
    tr = _row_tile(s_)

    def body(y_ref, z_ref, w_ref, do_ref, dy_ref, dz_ref, dw_ref):
        @pl.when(pl.program_id(0) == 0)
        def _():
            dw_ref[...] = jnp.zeros_like(dw_ref)

        for g in range(SSD_G):
            sl = slice(g * GN, (g + 1) * GN)
            yv = y_ref[:, sl]
            zv = z_ref[:, sl]
            dov = do_ref[:, sl].astype(F32)
            sz = _silu(zv)
            gg = yv * sz
            r = lax.rsqrt(jnp.mean(gg * gg, axis=-1, keepdims=True) + EPS)
            gw = dov * w_ref[:, sl]
            c = jnp.mean(gw * gg, axis=-1, keepdims=True)
            dgg = r * gw - gg * (r * r * r * c)
            dy_ref[:, sl] = dgg * sz
            dz_ref[:, sl] = (dgg * yv * _dsilu(zv)).astype(dz_ref.dtype)
            dw_ref[:, sl] += jnp.sum(dov * gg * r, axis=0, keepdims=True)

    row = pl.BlockSpec((tr, f), lambda i: (i, 0))
    wsp = pl.BlockSpec((1, f), lambda i: (0, 0))
    return pl.pallas_call(
        body, name=name, grid=(s_ // tr,),
        in_specs=[row, row, wsp, pl.BlockSpec((tr, f), lambda i: (i, 1))], out_specs=[row, row, wsp],
        out_shape=[jax.ShapeDtypeStruct((s_, f), F32), jax.ShapeDtypeStruct((s_, f), MXU_DTYPE),
                   jax.ShapeDtypeStruct((1, f), F32)],
        compiler_params=_params("arbitrary"),
    )(y, z, w.reshape(1, f), dout)


def _ffn_fwd(vv, w_gate, w_up, name="ffn_gate_up"):
    s_, d = vv.shape
    nb, f8, _ = w_gate.shape
    tm = _pick(s_, (1024, 512, 256, 128))

    def body(v_ref, wg_ref, wu_ref, g_ref, u_ref, a_ref):
        for rs in _row_slices(tm, 16):
            a = _mx(v_ref[rs, :])
            g = _dot(a, _mx(wg_ref[...]), 1, 1)
            u = _dot(a, _mx(wu_ref[...]), 1, 1)
            s = _sigmoid(g)
            gs = g * s
            g_ref[rs, :] = (u * (s * (1.0 + g * (1.0 - s)))).astype(g_ref.dtype)
            u_ref[rs, :] = gs.astype(u_ref.dtype)
            a_ref[rs, :] = (gs * u).astype(a_ref.dtype)

    wsp = pl.BlockSpec((None, f8, d), lambda j, i: (j, 0, 0))
    osp = pl.BlockSpec((None, tm, f8), lambda j, i: (j, i, 0))
    return pl.pallas_call(
        body, name=name, grid=(nb, s_ // tm),
        in_specs=[pl.BlockSpec((tm, d), lambda j, i: (i, 0)), wsp, wsp], out_specs=[osp] * 3,
        out_shape=[jax.ShapeDtypeStruct((nb, s_, f8), MXU_DTYPE)] * 3,
        compiler_params=_params("parallel", "parallel"),
    )(vv, w_gate, w_up)


def _ffn_bwd_act(dffn, w_down, gate, up, name="ffn_d_act"):
    s_, d = dffn.shape
    nb, f8, _ = w_down.shape
    tm = _pick(s_, (2048, 1024, 512, 256, 128))

    def body(d_ref, w_ref, g_ref, u_ref, dg_ref, du_ref):
        for rs in _row_slices(tm, 16):
            dact = _dot(_mx(d_ref[rs, :]), _mx(w_ref[...]), 1, 1)
            dg_ref[rs, :] = (dact * g_ref[rs, :].astype(F32)).astype(dg_ref.dtype)
            du_ref[rs, :] = (dact * u_ref[rs, :].astype(F32)).astype(du_ref.dtype)

    osp = pl.BlockSpec((None, tm, f8), lambda i, j: (j, i, 0))
    return pl.pallas_call(
        body, name=name, grid=(s_ // tm, nb),
        in_specs=[pl.BlockSpec((tm, d), lambda i, j: (i, 0)), pl.BlockSpec((None, f8, d), lambda i, j: (j, 0, 0)),
                  osp, osp],
        out_specs=[osp, osp], out_shape=[jax.ShapeDtypeStruct((nb, s_, f8), MXU_DTYPE)] * 2,
        compiler_params=_params("parallel", "parallel", vmem=VMEM_LIMIT_WIDE_BYTES),
    )(dffn, w_down, gate, up)


def _ffn_bwd_in(dgate, w_gate, dup, w_up, name="ffn_d_in"):
    nb, s_, f8 = dgate.shape
    d = w_gate.shape[2]
    tm = _pick(s_, (1024, 512, 256, 128))
    tn = _pick(d, (1024, 512, 256, 128))
    per = 2
    steps = nb // per

    def body(*refs):
        ins, o_ref, acc = refs[:4 * per], refs[4 * per], refs[4 * per + 1]
        j = pl.program_id(2)

        @pl.when(j == 0)
        def _():
            acc[...] = jnp.zeros_like(acc)

        for rs in _row_slices(tm, 16):
            part = None
            for t in range(per):
                dg_ref, wg_ref, du_ref, wu_ref = ins[4 * t:4 * t + 4]
                d_ = (_dot(_mx(dg_ref[rs, :]), _mx(wg_ref[...]), 1, 0)
                      + _dot(_mx(du_ref[rs, :]), _mx(wu_ref[...]), 1, 0))
                part = d_ if part is None else part + d_
            acc[rs, :] += part

        @pl.when(j == steps - 1)
        def _():
            o_ref[...] = acc[...]

    def specs(t):
        asp = pl.BlockSpec((None, tm, f8), lambda i, n, j: (j * per + t, i, 0))
        wsp = pl.BlockSpec((None, f8, tn), lambda i, n, j: (j * per + t, 0, n))
        return [asp, wsp, asp, wsp]

    return pl.pallas_call(
        body, name=name, grid=(s_ // tm, d // tn, steps),
        in_specs=[sp for t in range(per) for sp in specs(t)],
        out_specs=pl.BlockSpec((tm, tn), lambda i, n, j: (i, n)),
        out_shape=jax.ShapeDtypeStruct((s_, d), F32), scratch_shapes=[pltpu.VMEM((tm, tn), F32)],
        compiler_params=_params("parallel", "parallel", "arbitrary"),
    )(*((dgate, w_gate, dup, w_up) * per))


def _adam_math(g, w, m, v):
    m2 = ADAM_B1 * m + (1.0 - ADAM_B1) * g
    v2 = ADAM_B2 * v + (1.0 - ADAM_B2) * (g * g)
    m_hat = m2 / (1.0 - ADAM_B1 ** ADAM_STEP)
    v_hat = v2 / (1.0 - ADAM_B2 ** ADAM_STEP)
    delta = -ADAM_LR * (m_hat / (jnp.sqrt(v_hat) + ADAM_EPS) + ADAM_WD * w)
    return delta, m2, v2


def _adamw(parts, own, me, w, m, v, name="adamw"):
    nd, r_, c = parts.shape
    tr = _pick(r_, (128, 64, 32, 16))
    tc = c
    if tr == r_ and r_ > 128:
        tc = _pick(c, (256, 128))

    def body(me_ref, p_ref, own_ref, w_ref, m_ref, v_ref, g_ref, d_ref, m2_ref, v2_ref):
        mine = me_ref[0]
        g = jnp.zeros((tr, tc), F32)
        for i in range(nd):
            g = g + jnp.where(mine == i, own_ref[...], p_ref[i]).astype(F32)
        delta, m2, v2 = _adam_math(g, w_ref[...], m_ref[...], v_ref[...])
        g_ref[...] = g
        d_ref[...] = delta
        m2_ref[...] = m2
        v2_ref[...] = v2

    row = pl.BlockSpec((tr, tc), lambda i, j, me_: (i, j))
    gs = pltpu.PrefetchScalarGridSpec(
        num_scalar_prefetch=1, grid=(r_ // tr, c // tc),
        in_specs=[pl.BlockSpec((nd, tr, tc), lambda i, j, me_: (0, i, j)),
                  pl.BlockSpec((None, tr, tc), lambda i, j, me_: (me_[0], i, j)), row, row, row],
        out_specs=[row] * 4)
    return pl.pallas_call(
        body, name=name, grid_spec=gs, out_shape=[jax.ShapeDtypeStruct((r_, c), F32)] * 4,
        compiler_params=_params("parallel", "parallel"),
    )(me, parts, own, w, m, v)


def _adamw_small(parts, w, m, v, name="adamw_small"):
    nd = parts.shape[0]

    def body(p_ref, w_ref, m_ref, v_ref, g_ref, d_ref, m2_ref, v2_ref):
        g = p_ref[0]
        for i in range(1, nd):
            g = g + p_ref[i]
        delta, m2, v2 = _adam_math(g, w_ref[...], m_ref[...], v_ref[...])
        g_ref[...] = g
        d_ref[...] = delta
        m2_ref[...] = m2
        v2_ref[...] = v2

    return pl.pallas_call(
        body, name=name, out_shape=[jax.ShapeDtypeStruct(w.shape, F32)] * 4,
        compiler_params=pltpu.CompilerParams(vmem_limit_bytes=VMEM_LIMIT_BYTES),
    )(parts, w, m, v)


_HBM = pl.BlockSpec(memory_space=pltpu.HBM)
_MESH = pl.DeviceIdType.MESH


def _all_gather(xs, name):
    na = len(xs)

    def body(*refs):
        x_refs, out_refs = refs[:na], refs[na:2 * na]
        send_sems, recv_sems, local_sems = refs[2 * na:]
        x, y, c = lax.axis_index("x"), lax.axis_index("y"), lax.axis_index("c")
        me, sibling = (x, y, c), (x, y, 1 - c)
        near = [(1 - x, y), (x, 1 - y)]
        chips = near + [(1 - x, 1 - y)]
        relay_from = (x + c * (1 - 2 * x), y + (1 - c) * (1 - 2 * y))
        relay_to = (x + (1 - c) * (1 - 2 * x), y + c * (1 - 2 * y))

        def slot(a, px, py, pc):
            return out_refs[a].at[4 * px + 2 * py + pc]

        def copy(a, k, block, to, src=None):
            return pltpu.make_async_remote_copy(
                src_ref=slot(a, *block) if src is None else src, dst_ref=slot(a, *block),
                send_sem=send_sems.at[a, k], recv_sem=recv_sems.at[a, k], device_id=to, device_id_type=_MESH)

        mine = [pltpu.make_async_copy(x_refs[a], slot(a, *me), local_sems.at[a]) for a in range(na)]
        started = []
        for a in range(na):
            mine[a].start()
            first = [copy(a, 0, me, sibling, src=x_refs[a])]
            first += [copy(a, 1 + j, me, (*chip, c), src=x_refs[a]) for j, chip in enumerate(near)]
            for cp in first:
                cp.start()
            started += first
        for a in range(na):
            for j, chip in enumerate(chips):
                copy(a, 1 + j, (*chip, c), me).wait_recv()
                fwd = copy(a, 4 + j, (*chip, c), sibling)
                fwd.start()
                started.append(fwd)
                if j == len(near) - 1:
                    relay = copy(a, 1 + len(near), (*relay_from, c), (*relay_to, c))
                    relay.start()
                    started.append(relay)
        for a in range(na):
            copy(a, 0, sibling, me).wait_recv()
            for j, chip in enumerate(chips):
                copy(a, 4 + j, (*chip, 1 - c), me).wait_recv()
        for cp in started:
            cp.wait_send()
        for cp in mine:
            cp.wait()

    return pl.pallas_call(
        body, name=name, out_shape=[jax.ShapeDtypeStruct((N_DEV,) + t.shape, t.dtype) for t in xs],
        in_specs=[_HBM] * na, out_specs=[_HBM] * na,
        scratch_shapes=[pltpu.SemaphoreType.DMA((na, 7)), pltpu.SemaphoreType.DMA((na, 7)),
                        pltpu.SemaphoreType.DMA((na,))],
    )(*xs)


_SEM = pl.BlockSpec(memory_space=pltpu.SEMAPHORE)
_EFFECT = pltpu.SideEffectType.DATAFLOW_SIDE_EFFECTING


def _peers(x, y, c):
    out = []
    for k in range(1, N_DEV):
        px = 1 - x if k & 4 else x
        py = 1 - y if k & 2 else y
        pc = 1 - c if k & 1 else c
        out.append(((px, py, pc), 4 * px + 2 * py + pc))
    return out


def _push_copies(scatter, src_refs, land_refs, send_sems, recv_sems):
    x, y, c = lax.axis_index("x"), lax.axis_index("y"), lax.axis_index("c")
    me = 4 * x + 2 * y + c
    pairs = []
    for a, (src, land) in enumerate(zip(src_refs, land_refs)):
        for k, (peer, slot) in enumerate(_peers(x, y, c)):
            out_src = src.at[slot] if scatter else src
            si = a * (N_DEV - 1) + k
            send = pltpu.make_async_remote_copy(src_ref=out_src, dst_ref=land.at[me], send_sem=send_sems.at[si],
                                                recv_sem=recv_sems.at[si], device_id=peer, device_id_type=_MESH)
            recv = pltpu.make_async_remote_copy(src_ref=out_src, dst_ref=land.at[slot], send_sem=send_sems.at[si],
                                                recv_sem=recv_sems.at[si], device_id=peer, device_id_type=_MESH)
            pairs.append((send, recv))
    return pairs


def _push_start(srcs, scatter, dep, name):
    na = len(srcs)
    shapes = [t.shape[1:] if scatter else t.shape for t in srcs]
    lands = [pltpu.with_memory_space_constraint(lax.empty((N_DEV,) + s, t.dtype), pltpu.HBM) for s, t in zip(shapes, srcs)]

    def body(*refs):
        src_refs, land_refs = refs[:na], refs[na:2 * na]
        send_sems, recv_sems = refs[2 * na + 1], refs[2 * na + 2]
        token = refs[-1]
        for send, _ in _push_copies(scatter, src_refs, land_refs, send_sems, recv_sems):
            send.start()
        token[...] = jnp.zeros_like(token)

    sem = pltpu.SemaphoreType.DMA((na * (N_DEV - 1),))
    outs = pl.pallas_call(
        body, name=name,
        out_shape=(sem, sem) + tuple(pltpu.HBM(t.shape, t.dtype) for t in srcs)
        + tuple(pltpu.HBM(t.shape, t.dtype) for t in lands) + (jax.ShapeDtypeStruct((8, LANE), F32),),
        in_specs=[_HBM] * (2 * na) + [pl.BlockSpec(memory_space=pl.ANY)],
        out_specs=(_SEM, _SEM) + (_HBM,) * (2 * na) + (pl.BlockSpec(memory_space=pltpu.VMEM),),
        input_output_aliases={i: 2 + i for i in range(2 * na)},
        compiler_params=pltpu.CompilerParams(has_side_effects=_EFFECT),
    )(*[pltpu.with_memory_space_constraint(t, pltpu.HBM) for t in srcs], *lands, dep)
    return outs[0], outs[1], outs[2:2 + na], outs[2 + na:2 + 2 * na], outs[-1]


def _push_wait(send_sems, recv_sems, src_thru, land_thru, scatter, after, name):
    na = len(src_thru)

    def body(*refs):
        src_refs, land_refs = refs[:na], refs[na:2 * na]
        ssem, rsem = refs[2 * na], refs[2 * na + 1]
        for send, recv in _push_copies(scatter, src_refs, land_refs, ssem, rsem):
            send.wait_send()
            recv.wait_recv()

    outs = pl.pallas_call(
        body, name=name,
        out_shape=tuple(pltpu.HBM(t.shape, t.dtype) for t in src_thru) + tuple(pltpu.HBM(t.shape, t.dtype) for t in land_thru),
        in_specs=[_HBM] * (2 * na) + [_SEM, _SEM, pl.BlockSpec(memory_space=pl.ANY)],
        out_specs=(_HBM,) * (2 * na),
        input_output_aliases={i: i for i in range(2 * na)},
        compiler_params=pltpu.CompilerParams(has_side_effects=_EFFECT),
    )(*src_thru, *land_thru, send_sems, recv_sems, after)
    return outs[:na], outs[na:]


def _exchange_behind(srcs, scatter, dep, name):
    send_sems, recv_sems, thru, lands, token = _push_start(srcs, scatter, dep, name + "_start")

    def finish(after, place=True):
        src_done, land_done = _push_wait(send_sems, recv_sems, thru, lands, scatter, after, name + "_wait")
        if not place:
            return land_done, src_done
        return _place_own(land_done, src_done, scatter, name + "_own")

    return token[0, 0], finish


def _place_own(lands, srcs, scatter, name):
    me = (4 * lax.axis_index("x") + 2 * lax.axis_index("y") + lax.axis_index("c")).astype(jnp.int32).reshape(1)
    outs = []
    for a, (land, src) in enumerate(zip(lands, srcs)):
        r_, c_ = land.shape[1:]
        tr = _pick(r_, (512, 256, 128, 64, 32, 16))

        def body(me_ref, land_ref, src_ref, out_ref):
            out_ref[...] = src_ref[...]

        src_spec = (pl.BlockSpec((None, tr, c_), lambda i, me_: (me_[0], i, 0)) if scatter
                    else pl.BlockSpec((tr, c_), lambda i, me_: (i, 0)))
        gs = pltpu.PrefetchScalarGridSpec(
            num_scalar_prefetch=1, grid=(r_ // tr,),
            in_specs=[pl.BlockSpec(memory_space=pl.ANY), src_spec],
            out_specs=pl.BlockSpec((None, tr, c_), lambda i, me_: (me_[0], i, 0)))
        outs.append(pl.pallas_call(
            body, name=f"{name}_{a}", grid_spec=gs, out_shape=jax.ShapeDtypeStruct(land.shape, land.dtype),
            input_output_aliases={1: 0}, compiler_params=_params("arbitrary"),
        )(me, land, src))
    return outs


_TRANSPOSED = ("w_in", "w_uq", "w_gate", "w_up")
_CQKV = (0, Q_RANK + KV_RANK)
_KR = (_CQKV[1], _CQKV[1] + ROPE)
_Z = (_KR[1], _KR[1] + SSD_W)
_XBC = (_Z[1], _Z[1] + CONV_DIM)
_DT = (_XBC[1], _XBC[1] + SSD_H)


def _win_segments(w_in_t):
    w = w_in_t.reshape(D_IN, D_MODEL)
    small = jnp.concatenate([w[_KR[0]:_KR[1]], w[_DT[0]:_DT[1]],
                             jnp.zeros((LANE - ROPE - SSD_H, D_MODEL), w.dtype)], axis=0)
    return w[_CQKV[0]:_CQKV[1]], w[_Z[0]:_Z[1]], w[_XBC[0]:_XBC[1]], small


def _win_from_segments(g_cqkv, g_z, g_xbc, g_small):
    w = jnp.concatenate([g_cqkv, g_small[:ROPE], g_z, g_xbc, g_small[ROPE:ROPE + SSD_H]], axis=0)
    return w.reshape(N_DEV, D_IN // N_DEV, D_MODEL)


_SMALL = (("q_norm_w", 512), ("kv_norm_w", 512), ("conv_b", CONV_DIM), ("dt_bias", SSD_H), ("a_log", SSD_H),
          ("d_skip", SSD_H), ("ssd_norm_w", SSD_W), ("attn_out_norm_w", 1024), ("pre_mix_norm_w", D_MODEL),
          ("post_mix_norm_w", D_MODEL), ("pre_ffn_norm_w", D_MODEL), ("post_ffn_norm_w", D_MODEL),
          ("conv_w", CONV_K * CONV_DIM))
_SMALL_ROWS = -(-(sum(-(-n // LANE) for _, n in _SMALL) + 1) // 8) * 8


def _pack_small(vals):
    rows = []
    for name, n in _SMALL:
        v = vals[name].reshape(-1).astype(F32)
        pad = -(-n // LANE) * LANE
        rows.append(jnp.pad(v, (0, pad - n)).reshape(-1, LANE))
    m = jnp.concatenate(rows, axis=0)
    return jnp.pad(m, ((0, _SMALL_ROWS - m.shape[0]), (0, 0)))


def _unpack_small(m):
    out, r = {}, 0
    for name, n in _SMALL:
        nr = -(-n // LANE)
        out[name] = m[r:r + nr].reshape(-1)[:n]
        r += nr
    return out


def _head_row(v):
    return jnp.pad(v.reshape(1, -1).astype(F32), ((0, 0), (HEAD_LANE, LANE - HEAD_LANE - v.shape[-1])))


def _local_step(x, positions, target, wg, small, weights, on_grads):
    w_cqkv, w_z, w_xbc, w_small = _win_segments(wg["w_in"])
    conv_w = wg["conv_w"]
    conv_b = small["conv_b"].reshape(1, CONV_DIM)
    qkv_norm_w = jnp.concatenate([small["q_norm_w"], small["kv_norm_w"]])
    attn_norm_w = small["attn_out_norm_w"].reshape(1, HEADS * VDIM)
    scale = QK ** -0.5

    inv_freq = ROPE_THETA ** (-jnp.arange(0, ROPE, 2, dtype=F32) / ROPE)
    ang = positions.astype(F32)[:, None] * inv_freq
    cos2 = jnp.tile(jnp.cos(ang), (1, 2))
    sin2 = jnp.tile(jnp.sin(ang), (1, 2))

    u = _rms_fwd(x, small["pre_mix_norm_w"], out_dtype=MXU_DTYPE, name="pre_mix_norm")
    cqkv = _mm(u, w_cqkv, "nt", name="in_proj_qkv")
    z = _mm(u, w_z, "nt", name="in_proj_z")
    xbc = _mm(u, w_xbc, "nt", name="in_proj_xbc")
    sm = _mm(u, w_small, "nt", name="in_proj_small")

    w_uq, w_ukv = weights("qkv_up", cqkv)
    qkvn = _rms_fwd(cqkv, qkv_norm_w, groups=2, out_dtype=MXU_DTYPE, name="qkv_norm")
    q_h = _q_up(qkvn, w_uq, cos2, sin2, scale)
    k_h, v_h = _kv_up(qkvn, w_ukv, sm, cos2, sin2)
    o_h, lse = _flash_fwd(q_h, k_h, v_h)
    cat = _hnorm_fwd(o_h, attn_norm_w, D_MODEL)
    w_out = weights("out", o_h)[0].reshape(D_MODEL, D_MODEL)

    xbc_act = _conv_fwd(xbc, conv_w, conv_b)
    dtt = jnp.transpose(sm[:, HEAD_LANE:HEAD_LANE + SSD_H])
    ssd_args = (xbc_act, sm, dtt, _head_row(small["dt_bias"]), small["dt_bias"].reshape(SSD_H, 1),
                _head_row(small["a_log"]), small["a_log"].reshape(SSD_H, 1),
                jnp.broadcast_to(small["d_skip"].reshape(SSD_H, 1), (SSD_H, SSD_P)).reshape(SSD_PAIRS, 1, LANE))
    y_ssd, prev = _ssd_fwd(*ssd_args)
    cat = _gated_norm_fwd(y_ssd, z, small["ssd_norm_w"], cat)

    mix = _mm(cat, w_out, "nn", name="out_proj")
    h1, vv = _norm_res_norm(mix, x, small["post_mix_norm_w"], small["pre_ffn_norm_w"])

    w_gate, w_up = weights("ffn_in", mix)
    gate, up, act = _ffn_fwd(vv, w_gate, w_up)
    w_down, = weights("ffn_out", act)
    ffn = _mm(act, w_down, "nn", a_blk=True, b_blk=True, fuse=N_DEV, tm_max=512, name="ffn_down")
    loss_blk, dy, dffn, g_post_ffn = _loss_head(ffn, h1, target, small["post_ffn_norm_w"])

    g_down = _mm(act, dffn, "tn", a_blk=True, out_blk=True, out_dtype=MXU_DTYPE, wide=True, name="g_down")
    dgate, dup = _ffn_bwd_act(dffn, w_down, gate, up)
    dvv = _ffn_bwd_in(dgate, w_gate, dup, w_up)
    g_gate = _mm(dgate, vv, "tn", a_blk=True, out_blk=True, out_dtype=MXU_DTYPE, wide=True, name="g_gate")
    g_up = _mm(dup, vv, "tn", a_blk=True, out_blk=True, out_dtype=MXU_DTYPE, wide=True, name="g_up")
    pre_ffn_w = small["pre_ffn_norm_w"] + on_grads("ffn", [g_gate, g_up, g_down])
    dh1, dmix, g_pre_ffn, g_post_mix = _norm_res_norm_bwd(h1, pre_ffn_w, dvv, dy, mix, small["post_mix_norm_w"])

    dcat = _mm(dmix, w_out, "nt", name="d_cat")
    g_out = _mm(cat, dmix, "tn", out_dtype=MXU_DTYPE, name="g_out")

    do_h, delta, g_attn_norm = _hnorm_bwd(o_h, attn_norm_w, dcat)
    dq_h, dk_h, dv_h = _flash_bwd(q_h, k_h, v_h, do_h, lse, delta)
    dq = _q_prep(dq_h, cos2, -sin2, scale, name="dq_post")

    dy_ssd, dz, g_ssd_norm = _gated_norm_bwd(y_ssd, z, small["ssd_norm_w"], dcat)
    dxbc_act, ddt, dpar = _ssd_bwd(*ssd_args, prev, dy_ssd)
    dkv, dsm = _dkv_post(dk_h, dv_h, ddt, cos2, -sin2)
    dpre, dwb = _conv_bwd_pre(xbc, conv_w, conv_b, dxbc_act)
    dxbc = _conv_bwd_in(dpre, conv_w)

    dqn = _mm(dq, w_uq, "nn", a_blk=True, b_blk=True, fuse=HEADS, name="d_qn")
    dkvn = _mm(dkv, w_ukv, "nt", a_blk=True, b_blk=True, fuse=HEADS, name="d_kvn")
    g_uq = _mm(dq, qkvn, "tn", a_blk=True, out_blk=True, b_cols=(0, Q_RANK), out_dtype=MXU_DTYPE, name="g_uq")
    g_ukv = _mm(qkvn, dkv, "tn", b_blk=True, out_blk=True, a_cols=(Q_RANK, KV_RANK), out_dtype=MXU_DTYPE, name="g_ukv")
    heads_token = on_grads("heads", [g_uq, g_ukv, g_out.reshape(N_DEV, D_MODEL // N_DEV, D_MODEL)])
    dcqkv, g_qkv_norm = _rms_bwd(cqkv, qkv_norm_w + heads_token, [dqn, dkvn], out_dtype=MXU_DTYPE, name="qkv_norm_bwd")

    g_in = _win_from_segments(_mm(dcqkv, u, "tn", out_dtype=MXU_DTYPE, name="g_in_qkv"),
                              _mm(dz, u, "tn", out_dtype=MXU_DTYPE, name="g_in_z"),
                              _mm(dxbc, u, "tn", out_dtype=MXU_DTYPE, name="g_in_xbc"),
                              _mm(dsm, u, "tn", out_dtype=MXU_DTYPE, name="g_in_small"))
    in_token = on_grads("in", [g_in])
    du = _mm_sum([dsm + in_token.astype(dsm.dtype), dcqkv, dz, dxbc], [w_small, w_cqkv, w_z, w_xbc], name="d_u")
    dx, g_pre_mix = _rms_bwd(x, small["pre_mix_norm_w"], [du], res=dh1, name="pre_mix_norm_bwd")

    hl = slice(HEAD_LANE, HEAD_LANE + SSD_H)
    g_small = {"q_norm_w": g_qkv_norm[0, :Q_RANK], "kv_norm_w": g_qkv_norm[0, Q_RANK:], "conv_b": dwb[CONV_K],
               "dt_bias": dpar[0, hl], "a_log": dpar[1, hl], "d_skip": dpar[2, hl], "ssd_norm_w": g_ssd_norm,
               "attn_out_norm_w": g_attn_norm, "pre_mix_norm_w": g_pre_mix, "post_mix_norm_w": g_post_mix,
               "pre_ffn_norm_w": g_pre_ffn, "post_ffn_norm_w": g_post_ffn, "conv_w": dwb[:CONV_K]}
    return loss_blk[0, 0], dx, g_small


_WEIGHT_ORDER = ("w_in", "q_norm_w", "w_uq", "kv_norm_w", "w_ukv", "conv_w", "conv_b", "dt_bias", "a_log", "d_skip",
                 "ssd_norm_w", "attn_out_norm_w", "w_out", "pre_mix_norm_w", "post_mix_norm_w", "pre_ffn_norm_w",
                 "post_ffn_norm_w", "w_gate", "w_up", "w_down")


def kernel(x, positions, w_in, q_norm_w, w_uq, kv_norm_w, w_ukv, conv_w, conv_b, dt_bias, a_log, d_skip, ssd_norm_w, attn_out_norm_w, w_out, pre_mix_norm_w, post_mix_norm_w, pre_ffn_norm_w, post_ffn_norm_w, w_gate, w_up, w_down, loss_target, m_w_in, m_q_norm_w, m_w_uq, m_kv_norm_w, m_w_ukv, m_conv_w, m_conv_b, m_dt_bias, m_a_log, m_d_skip, m_ssd_norm_w, m_attn_out_norm_w, m_w_out, m_pre_mix_norm_w, m_post_mix_norm_w, m_pre_ffn_norm_w, m_post_ffn_norm_w, m_w_gate, m_w_up, m_w_down, v_w_in, v_q_norm_w, v_w_uq, v_kv_norm_w, v_w_ukv, v_conv_w, v_conv_b, v_dt_bias, v_a_log, v_d_skip, v_ssd_norm_w, v_attn_out_norm_w, v_w_out, v_pre_mix_norm_w, v_post_mix_norm_w, v_pre_ffn_norm_w, v_post_ffn_norm_w, v_w_gate, v_w_up, v_w_down):
    w = dict(w_in=w_in, q_norm_w=q_norm_w, w_uq=w_uq, kv_norm_w=kv_norm_w, w_ukv=w_ukv, conv_w=conv_w, conv_b=conv_b,
             dt_bias=dt_bias, a_log=a_log, d_skip=d_skip, ssd_norm_w=ssd_norm_w, attn_out_norm_w=attn_out_norm_w,
             w_out=w_out, pre_mix_norm_w=pre_mix_norm_w, post_mix_norm_w=post_mix_norm_w,
             pre_ffn_norm_w=pre_ffn_norm_w, post_ffn_norm_w=post_ffn_norm_w, w_gate=w_gate, w_up=w_up, w_down=w_down)
    m = dict(w_in=m_w_in, q_norm_w=m_q_norm_w, w_uq=m_w_uq, kv_norm_w=m_kv_norm_w, w_ukv=m_w_ukv, conv_w=m_conv_w,
             conv_b=m_conv_b, dt_bias=m_dt_bias, a_log=m_a_log, d_skip=m_d_skip, ssd_norm_w=m_ssd_norm_w,
             attn_out_norm_w=m_attn_out_norm_w, w_out=m_w_out, pre_mix_norm_w=m_pre_mix_norm_w,
             post_mix_norm_w=m_post_mix_norm_w, pre_ffn_norm_w=m_pre_ffn_norm_w, post_ffn_norm_w=m_post_ffn_norm_w,
             w_gate=m_w_gate, w_up=m_w_up, w_down=m_w_down)
    v = dict(w_in=v_w_in, q_norm_w=v_q_norm_w, w_uq=v_w_uq, kv_norm_w=v_kv_norm_w, w_ukv=v_w_ukv, conv_w=v_conv_w,
             conv_b=v_conv_b, dt_bias=v_dt_bias, a_log=v_a_log, d_skip=v_d_skip, ssd_norm_w=v_ssd_norm_w,
             attn_out_norm_w=v_attn_out_norm_w, w_out=v_w_out, pre_mix_norm_w=v_pre_mix_norm_w,
             post_mix_norm_w=v_post_mix_norm_w, pre_ffn_norm_w=v_pre_ffn_norm_w, post_ffn_norm_w=v_post_ffn_norm_w,
             w_gate=v_w_gate, w_up=v_w_up, w_down=v_w_down)
    w, m, v = ({k: t[0] for k, t in d.items()} for d in (w, m, v))
    me = 4 * lax.axis_index("x") + 2 * lax.axis_index("y") + lax.axis_index("c")
    groups = {"qkv_up": ("w_uq", "w_ukv"), "out": ("w_out",), "ffn_in": ("w_gate", "w_up"), "ffn_out": ("w_down",)}
    cshard = CONV_DIM // N_DEV
    for name in _TRANSPOSED:
        w[name], m[name], v[name] = w[name].T, m[name].T, v[name].T

    shards = [w["w_in"].astype(MXU_DTYPE),
              jnp.stack(_split3(w["conv_w"])).reshape(3 * CONV_K, cshard).astype(MXU_DTYPE)]
    w_in_g, cw = _all_gather(shards, name="gather_weights")
    cw = cw.astype(F32).reshape(N_DEV, 3, CONV_K, cshard)
    wg = {"w_in": w_in_g, "conv_w": jnp.transpose(cw[:, 0] + cw[:, 1] + cw[:, 2], (1, 0, 2)).reshape(CONV_K, CONV_DIM)}
    arriving, dep, started = {}, wg["conv_w"], jnp.zeros((), F32)
    small = {name: w[name] for name, _ in _SMALL if name != "conv_w"}
    for group in ("qkv_up", "out", "ffn_in", "ffn_out"):
        token, arriving[group] = _exchange_behind([w[name].astype(MXU_DTYPE) for name in groups[group]], False,
                                                  dep, group + "_weights")
        started = started + token
        dep = jnp.zeros((8, LANE), F32) + started
    small["pre_mix_norm_w"] = small["pre_mix_norm_w"] + started

    leaving = {}

    def on_grads(group, gs):
        token, leaving[group] = _exchange_behind(gs, True, jnp.zeros((8, LANE), F32), group + "_grads")
        return token

    loss_local, dx, g_small = _local_step(x[0], positions[0], loss_target[0], wg, small,
                                          lambda group, after: arriving[group](after), on_grads)
    recv = {}
    for group, names in (("ffn", ("w_gate", "w_up", "w_down")), ("heads", ("w_uq", "w_ukv", "w_out")), ("in", ("w_in",))):
        recv.update(zip(names, zip(*leaving[group](dx, place=False))))
    grads, deltas, new_m, new_v = {}, {}, {}, {}
    me1 = me.astype(jnp.int32).reshape(1)
    for name, (parts, own) in recv.items():
        outs = _adamw(parts, own, me1, w[name], m[name], v[name], name="adamw_" + name)
        if name in _TRANSPOSED:
            outs = [t.T for t in outs]
        grads[name], deltas[name], new_m[name], new_v[name] = outs

    def embed(t):
        return lax.dynamic_update_slice(jnp.zeros((CONV_K, CONV_DIM), F32), t, (0, me * cshard))

    mine_s = _pack_small(g_small).at[_SMALL_ROWS - 1, 0].set(loss_local)
    parts_s = _all_gather([mine_s], name="gather_small_grads")[0]
    packs = [_pack_small({**{n_: d[n_] for n_, _ in _SMALL if n_ != "conv_w"}, "conv_w": embed(d["conv_w"])})
             for d in (w, m, v)]
    summed = _adamw_small(parts_s, *packs)
    loss = summed[0][_SMALL_ROWS - 1, 0]
    outs = [_unpack_small(t) for t in summed]
    for name, n in _SMALL:
        for dst, src in zip((grads, deltas, new_m, new_v), outs):
            if name == "conv_w":
                dst[name] = lax.dynamic_slice(src[name].reshape(CONV_K, CONV_DIM), (0, me * cshard), (CONV_K, cshard))
            else:
                dst[name] = src[name]

    def lead(d):
        return [d[name][None] for name in _WEIGHT_ORDER]

    return (loss, dx[None], *lead(grads), *lead(deltas), *lead(new_m), *lead(new_v))
```

```python
import numpy as np

import jax
import jax.numpy as jnp
from jax import lax
from jax.experimental import pallas as pl
from jax.experimental.pallas import tpu as pltpu

F32 = jnp.float32
BF16 = jnp.bfloat16
MXU_DTYPE = jnp.bfloat16
EPS = 1e-6
VMEM_LIMIT_BYTES = 48 * 1024 * 1024
VMEM_LIMIT_WIDE_BYTES = 56 * 1024 * 1024
K_TILE_MAX = 2048

N_DEV = 8
D_MODEL = 2048
Q_RANK = 512
KV_RANK = 512
ROPE = 64
HALF = ROPE // 2
HEADS = 8
NOPE = 128
VDIM = 128
QK = NOPE + ROPE
SSD_W = 1024
SSD_H = 16
SSD_P = 64
SSD_G = 2
SSD_E = SSD_H // SSD_G
SSD_N = 128
CHUNK = 128
CONV_K = 4
CONV_DIM = SSD_W + 2 * SSD_G * SSD_N
B_OFF = SSD_W
C_OFF = SSD_W + SSD_G * SSD_N
D_FF = 5632
D_IN = Q_RANK + KV_RANK + ROPE + SSD_W + CONV_DIM + SSD_H
ROPE_THETA = 10000.0
LANE = 128
HEAD_LANE = ROPE

ADAM_LR = 0.001
ADAM_B1 = 0.9
ADAM_B2 = 0.999
ADAM_EPS = 1e-08
ADAM_WD = 0.01
ADAM_STEP = 10


def _pick(n, cands):
    for c in cands:
        if n % c == 0:
            return c
    return n


def _params(*sem, vmem=VMEM_LIMIT_BYTES):
    return pltpu.CompilerParams(dimension_semantics=sem, vmem_limit_bytes=vmem)


def _sigmoid(x):
    return 1.0 / (1.0 + jnp.exp(-x))


def _silu(x):
    return x * _sigmoid(x)


def _dsilu(x):
    s = _sigmoid(x)
    return s * (1.0 + x * (1.0 - s))


def _softplus(x):
    e = jnp.exp(-jnp.abs(x))
    small = e * (1.0 - e * (0.5 - e * (1.0 / 3.0)))
    return jnp.maximum(x, 0.0) + jnp.where(e < 0.01, small, jnp.log(1.0 + e))


def _dot(a, b, ca, cb):
    return lax.dot_general(a, b, (((ca,), (cb,)), ((), ())), preferred_element_type=F32)


def _mx(v):
    return v.astype(MXU_DTYPE)


def _split3(a):
    hi = a.astype(BF16)
    r1 = a - hi.astype(F32)
    mid = r1.astype(BF16)
    lo = (r1 - mid.astype(F32)).astype(BF16)
    return hi, mid, lo


def _exact_dot(a, b, ca, cb, split_a):
    if split_a:
        return sum(_dot(p, b, ca, cb) for p in _split3(a))
    return sum(_dot(a, p, ca, cb) for p in _split3(b))


MM_ROW_GROUPS = 4


def _row_slices(tm, align):
    ng = MM_ROW_GROUPS
    while ng > 1 and (tm % ng or (tm // ng) % align):
        ng //= 2
    return [slice(g * (tm // ng), (g + 1) * (tm // ng)) for g in range(ng)]


def _mm(a, b, mode, *, a_blk=False, b_blk=False, out_blk=False, a_cols=None, b_cols=None, add=None, out_dtype=F32,
        fuse=1, wide=False, tm_max=1024, name="mm"):
    a2, b2 = a.shape[-2:], b.shape[-2:]
    a_last = a2[1] if a_cols is None else a_cols[1]
    a_start = 0 if a_cols is None else a_cols[0]
    b_start = 0
    if b_cols is not None:
        assert mode != "nt"
        b_start, b2 = b_cols[0], (b2[0], b_cols[1])
    if mode == "nn":
        m, k, (k2, n) = a2[0], a_last, b2
    elif mode == "nt":
        m, k, (n, k2) = a2[0], a_last, b2
    else:
        k, m, (k2, n) = a2[0], a_last, b2
    assert k == k2, (a.shape, b.shape, mode)
    tm = _pick(m, tuple(c for c in (1024, 704, 512, 256, 128) if c <= tm_max))
    tn = _pick(n, ((2048,) if wide else ()) + (1024, 768, 704, 512, 256, 192, 128))
    k_max = 2 * K_TILE_MAX if mode == "tn" else K_TILE_MAX
    tk = k if k <= k_max else _pick(k, (K_TILE_MAX, 1024, 512))
    nk = k // tk
    jo = N_DEV if out_blk else 1
    reduce_blocks = a_blk and b_blk and not out_blk
    assert fuse == 1 or reduce_blocks
    jr = N_DEV // fuse if reduce_blocks else 1
    ca, cb = {"nn": (1, 0), "nt": (1, 1), "tn": (0, 0)}[mode]
    has_add = add is not None
    single = jr * nk == 1
    if mode == "tn":
        assert a_start % tm == 0
        a_block, a_idx = (tk, tm), (lambda i, kk: (kk, i + a_start // tm))
    else:
        assert a_start % tk == 0
        a_block, a_idx = (tm, tk), (lambda i, kk: (i, kk + a_start // tk))
    assert b_start % tn == 0
    b_block, b_idx = (((tn, tk), (lambda nn_, kk: (nn_, kk))) if mode == "nt"
                      else ((tk, tn), (lambda nn_, kk: (kk, nn_ + b_start // tn))))

    def blk_specs(blocked, block, idx, of_a, t):
        def pos(o, i, nn_, kk):
            return idx(i, kk) if of_a else idx(nn_, kk)
        if blocked:
            return pl.BlockSpec((None,) + block,
                                lambda o, i, nn_, r, kk: ((o if out_blk else r * fuse + t),) + pos(o, i, nn_, kk))
        return pl.BlockSpec(block, lambda o, i, nn_, r, kk: pos(o, i, nn_, kk))

    a_specs = [blk_specs(a_blk, a_block, a_idx, True, t) for t in range(fuse)]
    b_specs = [blk_specs(b_blk, b_block, b_idx, False, t) for t in range(fuse)]
    o_spec = (pl.BlockSpec((None, tm, tn), lambda o, i, nn_, r, kk: (o, i, nn_)) if out_blk
              else pl.BlockSpec((tm, tn), lambda o, i, nn_, r, kk: (i, nn_)))

    groups = _row_slices(tm, LANE if mode == "tn" else 16)

    def body(*refs):
        a_refs, b_refs = refs[:fuse], refs[fuse:2 * fuse]
        add_ref = refs[2 * fuse] if has_add else None
        o_ref = refs[2 * fuse + 1] if has_add else refs[2 * fuse]

        def partial(rs):
            out = None
            for t in range(fuse):
                av = a_refs[t][:, rs] if mode == "tn" else a_refs[t][rs, :]
                d = _dot(_mx(av), _mx(b_refs[t][...]), ca, cb)
                out = d if out is None else out + d
            return out

        if single:
            for rs in groups:
                res = partial(rs)
                if has_add:
                    res = res + add_ref[rs, :]
                o_ref[rs, :] = res.astype(o_ref.dtype)
            return
        acc = refs[-1]
        r, kk = pl.program_id(3), pl.program_id(4)

        @pl.when(jnp.logical_and(r == 0, kk == 0))
        def _():
            acc[...] = jnp.zeros_like(acc)

        for rs in groups:
            acc[rs, :] += partial(rs)

        @pl.when(jnp.logical_and(r == jr - 1, kk == nk - 1))
        def _():
            res = acc[...]
            if has_add:
                res = res + add_ref[...]
            o_ref[...] = res.astype(o_ref.dtype)

    out_shape = ((N_DEV, m, n) if out_blk else (m, n))
    return pl.pallas_call(
        body, name=name, grid=(jo, m // tm, n // tn, jr, nk),
        in_specs=a_specs + b_specs + ([o_spec] if has_add else []), out_specs=o_spec,
        out_shape=jax.ShapeDtypeStruct(out_shape, out_dtype),
        scratch_shapes=[] if single else [pltpu.VMEM((tm, tn), F32)],
        compiler_params=_params("parallel", "parallel", "parallel", "arbitrary", "arbitrary"),
    )(*((a,) * fuse + (b,) * fuse + ((add,) if has_add else ())))


def _mm_sum(a_list, b_list, name="mm_sum"):
    m, n = a_list[0].shape[0], b_list[0].shape[1]
    ns = len(a_list)
    tm = _pick(m, (1024, 512, 256, 128))
    tn = _pick(n, (1024, 512, 256, 128))
    groups = _row_slices(tm, 16)

    def body(*refs):
        a_refs, b_refs, o_ref = refs[:ns], refs[ns:2 * ns], refs[2 * ns]
        for rs in groups:
            acc = _dot(_mx(a_refs[0][rs, :]), _mx(b_refs[0][...]), 1, 0)
            for s in range(1, ns):
                acc = acc + _dot(_mx(a_refs[s][rs, :]), _mx(b_refs[s][...]), 1, 0)
            o_ref[rs, :] = acc

    return pl.pallas_call(
        body, name=name, grid=(m // tm, n // tn),
        in_specs=([pl.BlockSpec((tm, a.shape[1]), lambda i, j: (i, 0)) for a in a_list]
                  + [pl.BlockSpec((b.shape[0], tn), lambda i, j: (0, j)) for b in b_list]),
        out_specs=pl.BlockSpec((tm, tn), lambda i, j: (i, j)),
        out_shape=jax.ShapeDtypeStruct((m, n), F32), compiler_params=_params("parallel", "parallel"),
    )(*a_list, *b_list)


def _row_tile(r_, streams=4):
    return _pick(r_, ((512,) if streams <= 4 else ()) + (256, 128, 64, 32, 16, 8))


def _rms_fwd(t, w, groups=1, res=None, out_dtype=F32, name="rms_fwd"):
    r_, f = t.shape
    fg = f // groups
    tr = _row_tile(r_)
    has_res = res is not None

    def body(*refs):
        t_ref, w_ref = refs[0], refs[1]
        res_ref = refs[2] if has_res else None
        o_ref = refs[-1]
        for g in range(groups):
            sl = slice(g * fg, (g + 1) * fg)
            tv = t_ref[:, sl].astype(F32)
            r = lax.rsqrt(jnp.mean(tv * tv, axis=-1, keepdims=True) + EPS)
            y = tv * r * w_ref[:, sl]
            if has_res:
                y = y + res_ref[:, sl]
            o_ref[:, sl] = y.astype(o_ref.dtype)

    row = pl.BlockSpec((tr, f), lambda i: (i, 0))
    wsp = pl.BlockSpec((1, f), lambda i: (0, 0))
    return pl.pallas_call(
        body, name=name, grid=(r_ // tr,),
        in_specs=[row, wsp] + ([row] if has_res else []), out_specs=row,
        out_shape=jax.ShapeDtypeStruct((r_, f), out_dtype),
        compiler_params=_params("parallel"),
    )(*((t, w.reshape(1, f)) + ((res,) if has_res else ())))


def _rms_bwd(t, w, dys, res=None, out_dtype=F32, name="rms_bwd"):
    r_, f = t.shape
    groups = len(dys)
    fg = f // groups
    tr = _row_tile(r_)
    has_res = res is not None

    def body(*refs):
        t_ref, w_ref = refs[0], refs[1]
        dy_refs = refs[2:2 + groups]
        res_ref = refs[2 + groups] if has_res else None
        dt_ref, dw_ref = refs[-2], refs[-1]

        @pl.when(pl.program_id(0) == 0)
        def _():
            dw_ref[...] = jnp.zeros_like(dw_ref)

        for g in range(groups):
            sl = slice(g * fg, (g + 1) * fg)
            tv = t_ref[:, sl].astype(F32)
            dyv = dy_refs[g][...].astype(F32)
            r = lax.rsqrt(jnp.mean(tv * tv, axis=-1, keepdims=True) + EPS)
            gw = dyv * w_ref[:, sl]
            c = jnp.mean(gw * tv, axis=-1, keepdims=True)
            dt = r * gw - tv * (r * r * r * c)
            if has_res:
                dt = dt + res_ref[:, sl]
            dt_ref[:, sl] = dt.astype(dt_ref.dtype)
            dw_ref[:, sl] += jnp.sum(dyv * tv * r, axis=0, keepdims=True)

    row = pl.BlockSpec((tr, f), lambda i: (i, 0))
    grow = pl.BlockSpec((tr, fg), lambda i: (i, 0))
    wsp = pl.BlockSpec((1, f), lambda i: (0, 0))
    return pl.pallas_call(
        body, name=name, grid=(r_ // tr,),
        in_specs=[row, wsp] + [grow] * groups + ([row] if has_res else []), out_specs=[row, wsp],
        out_shape=[jax.ShapeDtypeStruct((r_, f), out_dtype), jax.ShapeDtypeStruct((1, f), F32)],
        compiler_params=_params("arbitrary"),
    )(*((t, w.reshape(1, f)) + tuple(dys) + ((res,) if has_res else ())))


def _norm_res_norm(t, res, w1, w2, name="post_mix_pre_ffn_norm"):
    r_, f = t.shape
    tr = _row_tile(r_)

    def body(t_ref, res_ref, w1_ref, w2_ref, h_ref, v_ref):
        tv = t_ref[...]
        h = res_ref[...] + tv * lax.rsqrt(jnp.mean(tv * tv, axis=-1, keepdims=True) + EPS) * w1_ref[...]
        h_ref[...] = h
        v_ref[...] = (h * lax.rsqrt(jnp.mean(h * h, axis=-1, keepdims=True) + EPS) * w2_ref[...]).astype(v_ref.dtype)

    row = pl.BlockSpec((tr, f), lambda i: (i, 0))
    wsp = pl.BlockSpec((1, f), lambda i: (0, 0))
    return pl.pallas_call(
        body, name=name, grid=(r_ // tr,), in_specs=[row, row, wsp, wsp], out_specs=[row, row],
        out_shape=[jax.ShapeDtypeStruct((r_, f), F32), jax.ShapeDtypeStruct((r_, f), MXU_DTYPE)],
        compiler_params=_params("parallel"),
    )(t, res, w1.reshape(1, f), w2.reshape(1, f))


def _norm_res_norm_bwd(h, w2, dv, dres, t, w1, name="pre_ffn_post_mix_norm_bwd"):
    r_, f = h.shape
    tr = _row_tile(r_, streams=6)

    def body(h_ref, w2_ref, dv_ref, dres_ref, t_ref, w1_ref, dh_ref, dt_ref, dw2_ref, dw1_ref):
        @pl.when(pl.program_id(0) == 0)
        def _():
            dw2_ref[...] = jnp.zeros_like(dw2_ref)
            dw1_ref[...] = jnp.zeros_like(dw1_ref)

        def rms_bwd(tv, wv, dyv):
            r = lax.rsqrt(jnp.mean(tv * tv, axis=-1, keepdims=True) + EPS)
            gw = dyv * wv
            c = jnp.mean(gw * tv, axis=-1, keepdims=True)
            return r * gw - tv * (r * r * r * c), jnp.sum(dyv * tv * r, axis=0, keepdims=True)

        d1, g2 = rms_bwd(h_ref[...], w2_ref[...], dv_ref[...])
        dh = d1 + dres_ref[...]
        dh_ref[...] = dh
        dw2_ref[...] += g2
        d2, g1 = rms_bwd(t_ref[...], w1_ref[...], dh)
        dt_ref[...] = d2.astype(dt_ref.dtype)
        dw1_ref[...] += g1

    row = pl.BlockSpec((tr, f), lambda i: (i, 0))
    wsp = pl.BlockSpec((1, f), lambda i: (0, 0))
    return pl.pallas_call(
        body, name=name, grid=(r_ // tr,), in_specs=[row, wsp, row, row, row, wsp], out_specs=[row, row, wsp, wsp],
        out_shape=[jax.ShapeDtypeStruct((r_, f), F32), jax.ShapeDtypeStruct((r_, f), MXU_DTYPE),
                   jax.ShapeDtypeStruct((1, f), F32), jax.ShapeDtypeStruct((1, f), F32)],
        compiler_params=_params("arbitrary"),
    )(h, w2.reshape(1, f), dv, dres, t, w1.reshape(1, f))


def _hnorm_fwd(o, w, width, name="attn_out_norm"):
    h, s_, v = o.shape
    tr = _row_tile(s_)

    def body(o_ref, w_ref, y_ref):
        ss = jnp.sum(o_ref[0] * o_ref[0], axis=-1, keepdims=True)
        for i in range(1, h):
            ss = ss + jnp.sum(o_ref[i] * o_ref[i], axis=-1, keepdims=True)
        r = lax.rsqrt(ss * (1.0 / (h * v)) + EPS)
        for i in range(h):
            sl = slice(i * v, (i + 1) * v)
            y_ref[:, sl] = (o_ref[i] * r * w_ref[:, sl]).astype(y_ref.dtype)

    return pl.pallas_call(
        body, name=name, grid=(s_ // tr,),
        in_specs=[pl.BlockSpec((h, tr, v), lambda i: (0, i, 0)), pl.BlockSpec((1, h * v), lambda i: (0, 0))],
        out_specs=pl.BlockSpec((tr, h * v), lambda i: (i, 0)),
        out_shape=jax.ShapeDtypeStruct((s_, width), MXU_DTYPE), compiler_params=_params("parallel"),
    )(o, w)


def _hnorm_bwd(o, w, dy, name="attn_out_norm_bwd"):
    h, s_, v = o.shape
    tr = _row_tile(s_)

    def body(o_ref, w_ref, dy_ref, do_ref, delta_ref, dw_ref):
        @pl.when(pl.program_id(0) == 0)
        def _():
            dw_ref[...] = jnp.zeros_like(dw_ref)

        ss = jnp.zeros((tr, 1), F32)
        cc = jnp.zeros((tr, 1), F32)
        for i in range(h):
            sl = slice(i * v, (i + 1) * v)
            ov = o_ref[i]
            ss = ss + jnp.sum(ov * ov, axis=-1, keepdims=True)
            cc = cc + jnp.sum(dy_ref[:, sl] * w_ref[:, sl] * ov, axis=-1, keepdims=True)
        r = lax.rsqrt(ss * (1.0 / (h * v)) + EPS)
        c = cc * (1.0 / (h * v))
        for i in range(h):
            sl = slice(i * v, (i + 1) * v)
            ov = o_ref[i]
            dyv = dy_ref[:, sl]
            dov = r * dyv * w_ref[:, sl] - ov * (r * r * r * c)
            do_ref[i] = dov.astype(do_ref.dtype)
            delta_ref[i] = jnp.sum(dov * ov, axis=-1, keepdims=True)
            dw_ref[:, sl] += jnp.sum(dyv * ov * r, axis=0, keepdims=True)

    blk = pl.BlockSpec((h, tr, v), lambda i: (0, i, 0))
    wsp = pl.BlockSpec((1, h * v), lambda i: (0, 0))
    return pl.pallas_call(
        body, name=name, grid=(s_ // tr,),
        in_specs=[blk, wsp, pl.BlockSpec((tr, h * v), lambda i: (i, 0))],
        out_specs=[blk, pl.BlockSpec((h, tr, 1), lambda i: (0, i, 0)), wsp],
        out_shape=[jax.ShapeDtypeStruct(o.shape, MXU_DTYPE), jax.ShapeDtypeStruct((h, s_, 1), F32),
                   jax.ShapeDtypeStruct((1, h * v), F32)],
        compiler_params=_params("arbitrary"),
    )(o, w, dy)


def _loss_head(ffn, h1, target, w, name="loss_head"):
    r_, f = ffn.shape
    tr = _row_tile(r_)

    def body(ffn_ref, h1_ref, tg_ref, w_ref, loss_ref, dy_ref, dffn_ref, dw_ref):
        @pl.when(pl.program_id(0) == 0)
        def _():
            dw_ref[...] = jnp.zeros_like(dw_ref)
            loss_ref[...] = jnp.zeros_like(loss_ref)

        tv = ffn_ref[...]
        wv = w_ref[...]
        r = lax.rsqrt(jnp.mean(tv * tv, axis=-1, keepdims=True) + EPS)
        tn = tv * r
        e = h1_ref[...] + tn * wv - tg_ref[...]
        tot = jnp.sum(jnp.sum(e * e, axis=1, keepdims=True), axis=0, keepdims=True) * (0.5 / f)
        loss_ref[...] += tot + jnp.zeros_like(loss_ref)
        dyv = e * (1.0 / f)
        dy_ref[...] = dyv
        gw = dyv * wv
        c = jnp.mean(gw * tv, axis=-1, keepdims=True)
        dffn_ref[...] = (r * gw - tv * (r * r * r * c)).astype(dffn_ref.dtype)
        dw_ref[...] += jnp.sum(dyv * tn, axis=0, keepdims=True)

    row = pl.BlockSpec((tr, f), lambda i: (i, 0))
    wsp = pl.BlockSpec((1, f), lambda i: (0, 0))
    lsp = pl.BlockSpec((1, LANE), lambda i: (0, 0))
    return pl.pallas_call(
        body, name=name, grid=(r_ // tr,),
        in_specs=[row, row, row, wsp], out_specs=[lsp, row, row, wsp],
        out_shape=[jax.ShapeDtypeStruct((1, LANE), F32), jax.ShapeDtypeStruct((r_, f), F32),
                   jax.ShapeDtypeStruct((r_, f), MXU_DTYPE), jax.ShapeDtypeStruct((1, f), F32)],
        compiler_params=_params("arbitrary"),
    )(ffn, h1, target, w.reshape(1, f))


def _rot_matrix():
    p = np.zeros((ROPE, ROPE), np.float32)
    for i in range(HALF):
        p[i + HALF, i] = -1.0
        p[i, i + HALF] = 1.0
    return jnp.asarray(p, BF16)


def _rope_val(r, c2, s2, rot):
    hi, mid, _ = _split3(r)
    return r * c2 + (_dot(hi, rot, 1, 0) + _dot(mid, rot, 1, 0)) * s2


def _q_prep(q, cos2, sin2, scale, name):
    h, s_, _ = q.shape
    tr = _pick(s_, (4096, 2048, 1024, 512, 256, 128, 64, 32, 16))

    def body(q_ref, c_ref, s_ref, rot_ref, o_ref):
        for rs in _row_slices(tr, 16):
            x = q_ref[rs, :]
            o_ref[rs, :NOPE] = (x[:, :NOPE] * scale).astype(o_ref.dtype)
            o_ref[rs, NOPE:] = (_rope_val(x[:, NOPE:], c_ref[rs, :], s_ref[rs, :], rot_ref[...]) * scale).astype(o_ref.dtype)

    blk = pl.BlockSpec((None, tr, QK), lambda hh, i: (hh, i, 0))
    csp = pl.BlockSpec((tr, ROPE), lambda hh, i: (i, 0))
    return pl.pallas_call(
        body, name=name, grid=(h, s_ // tr),
        in_specs=[blk, csp, csp, pl.BlockSpec((ROPE, ROPE), lambda hh, i: (0, 0))], out_specs=blk,
        out_shape=jax.ShapeDtypeStruct(q.shape, MXU_DTYPE), compiler_params=_params("parallel", "parallel"),
    )(q, cos2, sin2, _rot_matrix())


def _q_up(qkvn, w_uq_t, cos2, sin2, scale, name="q_up"):
    s_ = qkvn.shape[0]
    h = w_uq_t.shape[0]
    tm = _pick(s_, (4096, 2048, 1024, 512, 256, 128))

    def body(a_ref, w_ref, c_ref, s_ref, rot_ref, o_ref):
        for rs in _row_slices(tm, 16):
            x = _dot(_mx(a_ref[rs, :]), _mx(w_ref[...]), 1, 1)
            o_ref[rs, :NOPE] = (x[:, :NOPE] * scale).astype(o_ref.dtype)
            o_ref[rs, NOPE:] = (_rope_val(x[:, NOPE:], c_ref[rs, :], s_ref[rs, :], rot_ref[...]) * scale).astype(o_ref.dtype)

    csp = pl.BlockSpec((tm, ROPE), lambda j, i: (i, 0))
    return pl.pallas_call(
        body, name=name, grid=(h, s_ // tm),
        in_specs=[pl.BlockSpec((tm, Q_RANK), lambda j, i: (i, 0)), pl.BlockSpec((None, QK, Q_RANK), lambda j, i: (j, 0, 0)),
                  csp, csp, pl.BlockSpec((ROPE, ROPE), lambda j, i: (0, 0))],
        out_specs=pl.BlockSpec((None, tm, QK), lambda j, i: (j, i, 0)),
        out_shape=jax.ShapeDtypeStruct((h, s_, QK), MXU_DTYPE), compiler_params=_params("parallel", "parallel"),
    )(qkvn, w_uq_t, cos2, sin2, _rot_matrix())


def _kv_up(qkvn, w_ukv, small, cos2, sin2, name="kv_up"):
    s_ = qkvn.shape[0]
    h = w_ukv.shape[0]
    tm = _pick(s_, (4096, 2048, 1024, 512, 256, 128))

    def body(a_ref, w_ref, sm_ref, c_ref, s_ref, rot_ref, k_ref, v_ref):
        for rs in _row_slices(tm, 16):
            x = _dot(_mx(a_ref[rs, :]), _mx(w_ref[...]), 1, 0)
            k_ref[rs, :NOPE] = x[:, :NOPE].astype(k_ref.dtype)
            k_ref[rs, NOPE:] = _rope_val(sm_ref[rs, :ROPE], c_ref[rs, :], s_ref[rs, :], rot_ref[...]).astype(k_ref.dtype)
            v_ref[rs, :] = x[:, NOPE:].astype(v_ref.dtype)

    csp = pl.BlockSpec((tm, ROPE), lambda j, i: (i, 0))
    return pl.pallas_call(
        body, name=name, grid=(h, s_ // tm),
        in_specs=[pl.BlockSpec((tm, KV_RANK), lambda j, i: (i, Q_RANK // KV_RANK)),
                  pl.BlockSpec((None, KV_RANK, NOPE + VDIM), lambda j, i: (j, 0, 0)),
                  pl.BlockSpec((tm, LANE), lambda j, i: (i, 0)), csp, csp, pl.BlockSpec((ROPE, ROPE), lambda j, i: (0, 0))],
        out_specs=[pl.BlockSpec((None, tm, QK), lambda j, i: (j, i, 0)), pl.BlockSpec((None, tm, VDIM), lambda j, i: (j, i, 0))],
        out_shape=[jax.ShapeDtypeStruct((h, s_, QK), MXU_DTYPE), jax.ShapeDtypeStruct((h, s_, VDIM), MXU_DTYPE)],
        compiler_params=_params("parallel", "parallel"),
    )(qkvn, w_ukv, small, cos2, sin2, _rot_matrix())


def _dkv_post(dk, dv, ddt, cos2, nsin2, name="dkv_post"):
    h, s_, _ = dk.shape
    tr = _row_tile(s_)

    def body(dk_ref, dv_ref, ddt_ref, c_ref, s_ref, rot_ref, dkv_ref, dsm_ref):
        acc = dk_ref[0, :, NOPE:]
        for i in range(1, h):
            acc = acc + dk_ref[i, :, NOPE:]
        dsm_ref[:, :ROPE] = _rope_val(acc, c_ref[...], s_ref[...], rot_ref[...]).astype(dsm_ref.dtype)
        dsm_ref[:, ROPE:] = ddt_ref[:, ROPE:].astype(dsm_ref.dtype)
        for i in range(h):
            dkv_ref[i, :, :NOPE] = dk_ref[i, :, :NOPE].astype(dkv_ref.dtype)
            dkv_ref[i, :, NOPE:] = dv_ref[i].astype(dkv_ref.dtype)

    csp = pl.BlockSpec((tr, ROPE), lambda i: (i, 0))
    return pl.pallas_call(
        body, name=name, grid=(s_ // tr,),
        in_specs=[pl.BlockSpec((h, tr, QK), lambda i: (0, i, 0)), pl.BlockSpec((h, tr, VDIM), lambda i: (0, i, 0)),
                  pl.BlockSpec((tr, LANE), lambda i: (i, 0)), csp, csp, pl.BlockSpec((ROPE, ROPE), lambda i: (0, 0))],
        out_specs=[pl.BlockSpec((h, tr, NOPE + VDIM), lambda i: (0, i, 0)), pl.BlockSpec((tr, LANE), lambda i: (i, 0))],
        out_shape=[jax.ShapeDtypeStruct((h, s_, NOPE + VDIM), MXU_DTYPE), jax.ShapeDtypeStruct((s_, LANE), MXU_DTYPE)],
        compiler_params=_params("parallel"),
    )(dk, dv, ddt, cos2, nsin2, _rot_matrix())


def _attn_tile(s):
    return 2048 if s % 4096 == 0 else s // 2


def _pairs(n, by_key):
    if by_key:
        pr = [(i, j) for j in range(n) for i in range(j, n)]
    else:
        pr = [(i, j) for i in range(n) for j in range(i + 1)]
    return (jnp.asarray([p[0] for p in pr], jnp.int32), jnp.asarray([p[1] for p in pr], jnp.int32))


ATTN_ROW_GROUPS = 8


def _row_groups(t, diag):
    tg = t // ATTN_ROW_GROUPS
    out = []
    for r in range(ATTN_ROW_GROUPS):
        nc = (r + 1) * tg if diag else t
        mask = None
        if diag:
            mask = (lax.broadcasted_iota(jnp.int32, (tg, nc), 1)
                    <= lax.broadcasted_iota(jnp.int32, (tg, nc), 0) + r * tg)
        out.append((slice(r * tg, (r + 1) * tg), nc, mask))
    return out


def _flash_specs(t, dk, dv):
    qsp = pl.BlockSpec((None, t, dk), lambda hh, p, qi, kj: (hh, qi[p], 0))
    ksp = pl.BlockSpec((None, t, dk), lambda hh, p, qi, kj: (hh, kj[p], 0))
    vsp = pl.BlockSpec((None, t, dv), lambda hh, p, qi, kj: (hh, kj[p], 0))
    osp = pl.BlockSpec((None, t, dv), lambda hh, p, qi, kj: (hh, qi[p], 0))
    lsp = pl.BlockSpec((None, t, 1), lambda hh, p, qi, kj: (hh, qi[p], 0))
    return qsp, ksp, vsp, osp, lsp


def _flash_fwd(q, k, v, name="flash_fwd"):
    h, s_, dk = q.shape
    dv = v.shape[-1]
    t = _attn_tile(s_)
    n = s_ // t
    qi, kj = _pairs(n, False)

    def body(qi_ref, kj_ref, q_ref, k_ref, v_ref, o_ref, lse_ref, m_s, l_s, acc):
        p_ = pl.program_id(1)
        i, j = qi_ref[p_], kj_ref[p_]

        @pl.when(j == 0)
        def _():
            m_s[...] = jnp.full_like(m_s, -jnp.inf)
            l_s[...] = jnp.zeros_like(l_s)
            acc[...] = jnp.zeros_like(acc)

        def update(diag):
            for rs, nc, mask in _row_groups(t, diag):
                sc = _dot(q_ref[rs, :], k_ref[0:nc, :], 1, 1)
                if mask is not None:
                    sc = jnp.where(mask, sc, -jnp.inf)
                m_old = m_s[rs, :]
                m_new = jnp.maximum(m_old, jnp.max(sc, axis=1, keepdims=True))
                alpha = jnp.exp(m_old - m_new)
                p = jnp.exp(sc - m_new)
                l_s[rs, :] = alpha * l_s[rs, :] + jnp.sum(p, axis=1, keepdims=True)
                acc[rs, :] = alpha * acc[rs, :] + _dot(_mx(p), v_ref[0:nc, :], 1, 0)
                m_s[rs, :] = m_new

        @pl.when(j < i)
        def _():
            update(False)

        @pl.when(j == i)
        def _():
            update(True)
            o_ref[...] = acc[...] / l_s[...]
            lse_ref[...] = m_s[...] + jnp.log(l_s[...])

    qsp, ksp, vsp, osp, lsp = _flash_specs(t, dk, dv)
    gs = pltpu.PrefetchScalarGridSpec(
        num_scalar_prefetch=2, grid=(h, qi.shape[0]), in_specs=[qsp, ksp, vsp], out_specs=[osp, lsp],
        scratch_shapes=[pltpu.VMEM((t, 1), F32), pltpu.VMEM((t, 1), F32), pltpu.VMEM((t, dv), F32)])
    return pl.pallas_call(
        body, name=name, grid_spec=gs,
        out_shape=[jax.ShapeDtypeStruct((h, s_, dv), F32), jax.ShapeDtypeStruct((h, s_, 1), F32)],
        compiler_params=_params("parallel", "arbitrary"),
    )(qi, kj, q, k, v)


def _flash_bwd(q, k, v, do, lse, delta, name="flash_bwd"):
    h, s_, dk = q.shape
    dv = v.shape[-1]
    t = _attn_tile(s_)
    tg = t // ATTN_ROW_GROUPS
    n = s_ // t
    qi, kj = _pairs(n, True)

    def body(qi_ref, kj_ref, q_ref, k_ref, v_ref, do_ref, lse_ref, delta_ref, dq_ref, dk_ref, dv_ref, dk_acc, dv_acc):
        p_ = pl.program_id(1)
        i, j = qi_ref[p_], kj_ref[p_]

        @pl.when(p_ == 0)
        def _():
            dq_ref[...] = jnp.zeros_like(dq_ref)

        def update(diag):
            for g, (rs, nc, mask) in enumerate(_row_groups(t, diag)):
                sc = _dot(q_ref[rs, :], k_ref[0:nc, :], 1, 1)
                if mask is not None:
                    sc = jnp.where(mask, sc, -jnp.inf)
                p = jnp.exp(sc - lse_ref[rs, :])
                dob = _mx(do_ref[rs, :])
                dv_acc[0:nc, :] += _dot(_mx(p), dob, 0, 0)
                dp = _dot(dob, v_ref[0:nc, :], 1, 1)
                dsb = _mx(p * (dp - delta_ref[rs, :]))
                dk_acc[0:nc, :] += _dot(dsb, q_ref[rs, :], 0, 0)
                rows = pl.ds(pl.multiple_of(i * t + g * tg, tg), tg)
                dq_ref[rows, :] += _dot(dsb, k_ref[0:nc, :], 1, 0)

        @pl.when(i == j)
        def _():
            dk_acc[...] = jnp.zeros_like(dk_acc)
            dv_acc[...] = jnp.zeros_like(dv_acc)
            update(True)

        @pl.when(i > j)
        def _():
            update(False)

        @pl.when(i == n - 1)
        def _():
            dk_ref[...] = dk_acc[...]
            dv_ref[...] = dv_acc[...]

    qsp, ksp, vsp, osp, lsp = _flash_specs(t, dk, dv)
    dqsp = pl.BlockSpec((None, s_, dk), lambda hh, p, qi, kj: (hh, 0, 0))
    gs = pltpu.PrefetchScalarGridSpec(
        num_scalar_prefetch=2, grid=(h, qi.shape[0]), in_specs=[qsp, ksp, vsp, osp, lsp, lsp],
        out_specs=[dqsp, ksp, vsp],
        scratch_shapes=[pltpu.VMEM((t, dk), F32), pltpu.VMEM((t, dv), F32)])
    return pl.pallas_call(
        body, name=name, grid_spec=gs,
        out_shape=[jax.ShapeDtypeStruct((h, s_, dk), F32), jax.ShapeDtypeStruct((h, s_, dk), F32),
                   jax.ShapeDtypeStruct((h, s_, dv), F32)],
        compiler_params=_params("parallel", "arbitrary"),
    )(qi, kj, q, k, v, do, lse, delta)


HALO = 8


def _conv_specs(s_, c, tr, after):
    main = pl.BlockSpec((tr, c), lambda i: (i, 0))
    per = tr // HALO
    if after:
        halo = pl.BlockSpec((HALO, c), lambda i: (jnp.minimum((i + 1) * per, s_ // HALO - 1), 0))
    else:
        halo = pl.BlockSpec((HALO, c), lambda i: (jnp.maximum(i * per - 1, 0), 0))
    return main, halo


def _fill_before(ext, t_ref, h_ref, tr):
    ext[0:HALO, :] = jnp.where(pl.program_id(0) > 0, h_ref[...], 0.0)
    ext[HALO:HALO + tr, :] = t_ref[...]


def _taps(ext, w_ref, tr):
    base = HALO - (CONV_K - 1)
    acc = ext[base:base + tr, :] * w_ref[0:1, :]
    for k in range(1, CONV_K):
        acc = acc + ext[base + k:base + k + tr, :] * w_ref[k:k + 1, :]
    return acc


def _conv_fwd(t, w, b, name="conv_fwd"):
    s_, c = t.shape
    tr = _row_tile(s_)

    def body(t_ref, h_ref, w_ref, b_ref, o_ref, ext):
        _fill_before(ext, t_ref, h_ref, tr)
        o_ref[...] = _silu(_taps(ext, w_ref, tr) + b_ref[...])

    main, halo = _conv_specs(s_, c, tr, False)
    return pl.pallas_call(
        body, name=name, grid=(s_ // tr,),
        in_specs=[main, halo, pl.BlockSpec((CONV_K, c), lambda i: (0, 0)), pl.BlockSpec((1, c), lambda i: (0, 0))],
        out_specs=main, out_shape=jax.ShapeDtypeStruct((s_, c), F32),
        scratch_shapes=[pltpu.VMEM((tr + HALO, c), F32)], compiler_params=_params("parallel"),
    )(t, t, w, b)


def _conv_bwd_pre(t, w, b, dact, name="conv_bwd_pre"):
    s_, c = t.shape
    tr = _row_tile(s_)

    def body(t_ref, h_ref, w_ref, b_ref, da_ref, dpre_ref, dwb_ref, ext):
        @pl.when(pl.program_id(0) == 0)
        def _():
            dwb_ref[...] = jnp.zeros_like(dwb_ref)

        _fill_before(ext, t_ref, h_ref, tr)
        dpre = da_ref[...] * _dsilu(_taps(ext, w_ref, tr) + b_ref[...])
        dpre_ref[...] = dpre
        base = HALO - (CONV_K - 1)
        for k in range(CONV_K):
            dwb_ref[k:k + 1, :] += jnp.sum(dpre * ext[base + k:base + k + tr, :], axis=0, keepdims=True)
        dwb_ref[CONV_K:CONV_K + 1, :] += jnp.sum(dpre, axis=0, keepdims=True)

    main, halo = _conv_specs(s_, c, tr, False)
    return pl.pallas_call(
        body, name=name, grid=(s_ // tr,),
        in_specs=[main, halo, pl.BlockSpec((CONV_K, c), lambda i: (0, 0)), pl.BlockSpec((1, c), lambda i: (0, 0)), main],
        out_specs=[main, pl.BlockSpec((8, c), lambda i: (0, 0))],
        out_shape=[jax.ShapeDtypeStruct((s_, c), F32), jax.ShapeDtypeStruct((8, c), F32)],
        scratch_shapes=[pltpu.VMEM((tr + HALO, c), F32)], compiler_params=_params("arbitrary"),
    )(t, t, w, b, dact)


def _conv_bwd_in(dpre, w, name="conv_bwd_in"):
    s_, c = dpre.shape
    tr = _row_tile(s_)
    nt = s_ // tr

    def body(d_ref, h_ref, w_ref, o_ref, ext):
        ext[0:tr, :] = d_ref[...]
        ext[tr:tr + HALO, :] = jnp.where(pl.program_id(0) < nt - 1, h_ref[...], 0.0)
        acc = ext[CONV_K - 1:CONV_K - 1 + tr, :] * w_ref[0:1, :]
        for k in range(1, CONV_K):
            acc = acc + ext[CONV_K - 1 - k:CONV_K - 1 - k + tr, :] * w_ref[k:k + 1, :]
        o_ref[...] = acc.astype(o_ref.dtype)

    main, halo = _conv_specs(s_, c, tr, True)
    return pl.pallas_call(
        body, name=name, grid=(nt,),
        in_specs=[main, halo, pl.BlockSpec((CONV_K, c), lambda i: (0, 0))],
        out_specs=main, out_shape=jax.ShapeDtypeStruct((s_, c), MXU_DTYPE),
        scratch_shapes=[pltpu.VMEM((tr + HALO, c), F32)], compiler_params=_params("parallel"),
    )(dpre, dpre, w)


def _ssd_chunk_common(dt_ref, dtt_ref, br_ref, bc_ref, ar_ref, ac_ref):
    li = lax.broadcasted_iota(jnp.int32, (CHUNK, CHUNK), 0)
    si = lax.broadcasted_iota(jnp.int32, (CHUNK, CHUNK), 1)
    lower = li >= si
    lower_b = lower.astype(BF16)
    upper_b = (li <= si).astype(BF16)
    zr = dt_ref[...] + br_ref[...]
    dtc = _softplus(zr)
    a_row = -jnp.exp(ar_ref[...])
    acum = _exact_dot(lower_b, dtc * a_row, 1, 0, False)
    dtt = _softplus(dtt_ref[...] + bc_ref[...])
    acum_t = _exact_dot(dtt * (-jnp.exp(ac_ref[...])), upper_b, 1, 0, True)
    return lower, upper_b, zr, dtc, a_row, acum, acum_t


def _head_terms(h, lower, dtc, acum, acum_t):
    lane = lax.broadcasted_iota(jnp.int32, (1, LANE), 1)
    sub = lax.broadcasted_iota(jnp.int32, (SSD_H, 1), 0)
    rowid = lax.broadcasted_iota(jnp.int32, (CHUNK, 1), 0)
    oh = (lane == HEAD_LANE + h).astype(F32)
    acol = jnp.sum(acum * oh, axis=1, keepdims=True)
    dcol = jnp.sum(dtc * oh, axis=1, keepdims=True)
    arow = jnp.sum(acum_t * (sub == h).astype(F32), axis=0, keepdims=True)
    alast = jnp.sum(jnp.where(rowid == CHUNK - 1, acol, 0.0), axis=0, keepdims=True)
    decay = jnp.exp(jnp.where(lower, acol - arow, -jnp.inf))
    return oh, acol, dcol, alast, decay


SSD_PAIRS = SSD_H // 2
PAIRS_PER_GROUP = SSD_E // 2


def _ps(q):
    return slice(q * LANE, (q + 1) * LANE)


def _gs(off, g):
    return slice(off + g * SSD_N, off + (g + 1) * SSD_N)


def _lanes(c0, c1):
    return jnp.where(lax.broadcasted_iota(jnp.int32, (1, LANE), 1) < SSD_P, c0, c1)


def _rows(c0, c1):
    return jnp.where(lax.broadcasted_iota(jnp.int32, (LANE, 1), 0) < SSD_P, c0, c1)


def _lane_halves(t):
    first = lax.broadcasted_iota(jnp.int32, (1, LANE), 1) < SSD_P
    return (jnp.sum(jnp.where(first, t, 0.0), axis=1, keepdims=True),
            jnp.sum(jnp.where(first, 0.0, t), axis=1, keepdims=True))


def _ssd_in_specs(rev):
    def ci(c):
        return c if rev is None else rev - c
    return [pl.BlockSpec((CHUNK, CONV_DIM), lambda c: (ci(c), 0)),
            pl.BlockSpec((CHUNK, LANE), lambda c: (ci(c), 0)),
            pl.BlockSpec((SSD_H, CHUNK), lambda c: (0, ci(c))),
            pl.BlockSpec((1, LANE), lambda c: (0, 0)), pl.BlockSpec((SSD_H, 1), lambda c: (0, 0)),
            pl.BlockSpec((1, LANE), lambda c: (0, 0)), pl.BlockSpec((SSD_H, 1), lambda c: (0, 0)),
            pl.BlockSpec((SSD_PAIRS, 1, LANE), lambda c: (0, 0, 0))]


def _ssd_fwd(xbc, small, dtt, bias_r, bias_c, alog_r, alog_c, dsk, name="ssd_fwd"):
    s_ = xbc.shape[0]
    nc = s_ // CHUNK

    def body(x_ref, dt_ref, dtt_ref, br_ref, bc_ref, ar_ref, ac_ref, dsk_ref, y_ref, prev_ref, state):
        @pl.when(pl.program_id(0) == 0)
        def _():
            state[...] = jnp.zeros_like(state)

        lower, _, _, dtc, _, acum, acum_t = _ssd_chunk_common(dt_ref, dtt_ref, br_ref, bc_ref, ar_ref, ac_ref)
        for g in range(SSD_G):
            bb = _mx(x_ref[:, _gs(B_OFF, g)])
            cb_ = _mx(x_ref[:, _gs(C_OFF, g)])
            cbm = _dot(cb_, bb, 1, 1)
            for e in range(PAIRS_PER_GROUP):
                q = g * PAIRS_PER_GROUP + e
                _, acol0, dcol0, alast0, decay0 = _head_terms(2 * q, lower, dtc, acum, acum_t)
                _, acol1, dcol1, alast1, decay1 = _head_terms(2 * q + 1, lower, dtc, acum, acum_t)
                x = x_ref[:, _ps(q)]
                xdt = x * _lanes(dcol0, dcol1)
                xb = _mx(xdt)
                yd = _lanes(_dot(_mx(cbm * decay0), xb, 1, 0), _dot(_mx(cbm * decay1), xb, 1, 0))
                prev = state[q]
                prev_ref[0, q] = prev
                yo = _dot(cb_, _mx(prev), 1, 1) * _lanes(jnp.exp(acol0), jnp.exp(acol1))
                ds = _lanes(jnp.exp(alast0 - acol0), jnp.exp(alast1 - acol1))
                st = _dot(_mx(xdt * ds), bb, 0, 0)
                state[q] = prev * _rows(jnp.exp(alast0), jnp.exp(alast1)) + st
                y_ref[:, _ps(q)] = yd + yo + x * dsk_ref[q]

    psp = pl.BlockSpec((1, SSD_PAIRS, LANE, SSD_N), lambda c: (c, 0, 0, 0))
    return pl.pallas_call(
        body, name=name, grid=(nc,),
        in_specs=_ssd_in_specs(None), out_specs=[pl.BlockSpec((CHUNK, SSD_W), lambda c: (c, 0)), psp],
        out_shape=[jax.ShapeDtypeStruct((s_, SSD_W), F32),
                   jax.ShapeDtypeStruct((nc, SSD_PAIRS, LANE, SSD_N), F32)],
        scratch_shapes=[pltpu.VMEM((SSD_PAIRS, LANE, SSD_N), F32)],
        compiler_params=_params("arbitrary"),
    )(xbc, small, dtt, bias_r, bias_c, alog_r, alog_c, dsk)


def _ssd_bwd(xbc, small, dtt, bias_r, bias_c, alog_r, alog_c, dsk, prev, dy, name="ssd_bwd"):
    s_ = xbc.shape[0]
    nc = s_ // CHUNK

    def body(x_ref, dt_ref, dtt_ref, br_ref, bc_ref, ar_ref, ac_ref, dsk_ref, prev_ref, dy_ref,
             dx_ref, ddt_ref, dpar_ref, dstate):
        @pl.when(pl.program_id(0) == 0)
        def _():
            dstate[...] = jnp.zeros_like(dstate)
            dpar_ref[...] = jnp.zeros_like(dpar_ref)

        lower, upper_b, zr, dtc, a_row, acum, acum_t = _ssd_chunk_common(
            dt_ref, dtt_ref, br_ref, bc_ref, ar_ref, ac_ref)
        strict = (lax.broadcasted_iota(jnp.int32, (CHUNK, CHUNK), 1)
                  < lax.broadcasted_iota(jnp.int32, (CHUNK, CHUNK), 0))
        strict_b = strict.astype(BF16)
        col2 = lax.broadcasted_iota(jnp.int32, (CHUNK, 2 * CHUNK), 1)
        strict2 = (jnp.where(col2 >= CHUNK, col2 - CHUNK, col2)
                   < lax.broadcasted_iota(jnp.int32, (CHUNK, 2 * CHUNK), 0))
        da_in = jnp.zeros((CHUNK, LANE), F32)
        r_off = jnp.zeros((CHUNK, LANE), F32)
        c_int = jnp.zeros((CHUNK, LANE), F32)
        c_row = jnp.zeros((1, LANE), F32)
        ddt = jnp.zeros((CHUNK, LANE), F32)
        dskip = jnp.zeros((1, LANE), F32)
        for g in range(SSD_G):
            bb = _mx(x_ref[:, _gs(B_OFF, g)])
            cb_ = _mx(x_ref[:, _gs(C_OFF, g)])
            cbm = _dot(cb_, bb, 1, 1)
            dcb = jnp.zeros((CHUNK, CHUNK), F32)
            dc_acc = jnp.zeros((CHUNK, SSD_N), F32)
            db_acc = jnp.zeros((CHUNK, SSD_N), F32)
            for e in range(PAIRS_PER_GROUP):
                q = g * PAIRS_PER_GROUP + e
                oh0, acol0, dcol0, alast0, decay0 = _head_terms(2 * q, lower, dtc, acum, acum_t)
                oh1, acol1, dcol1, alast1, decay1 = _head_terms(2 * q + 1, lower, dtc, acum, acum_t)
                x = x_ref[:, _ps(q)]
                dy = dy_ref[:, _ps(q)]
                dcol = _lanes(dcol0, dcol1)
                xdt = x * dcol
                xb = _mx(xdt)
                eacol = _lanes(jnp.exp(acol0), jnp.exp(acol1))
                ds = _lanes(jnp.exp(alast0 - acol0), jnp.exp(alast1 - acol1))
                ealast = _rows(jnp.exp(alast0), jnp.exp(alast1))
                dyb = _mx(dy)
                dyb0, dyb1 = _mx(_lanes(dy, 0.0)), _mx(_lanes(0.0, dy))
                dsh = dstate[q]
                dshb = _mx(dsh)
                prev = prev_ref[0, q]
                prevb = _mx(prev)
                dxdt_inter = ds * _dot(bb, dshb, 1, 1)
                dxdt = _lanes(_dot(_mx(cbm * decay0), dyb, 0, 0), _dot(_mx(cbm * decay1), dyb, 0, 0)) + dxdt_inter
                dwl0 = _dot(dyb0, xb, 1, 1) * decay0
                dwl1 = _dot(dyb1, xb, 1, 1) * decay1
                dcb = dcb + dwl0 + dwl1
                dyeb = _mx(dy * eacol)
                dc_acc = dc_acc + _dot(dyeb, prevb, 1, 0)
                db_acc = db_acc + _dot(_mx(xdt * ds), dshb, 1, 0)
                dstate[q] = _dot(dyeb, cb_, 0, 0) + ealast * dsh
                above = _exact_dot(upper_b, jnp.concatenate([dwl0 * cbm, dwl1 * cbm], axis=1), 1, 0, False)
                above = jnp.where(strict2, above, 0.0)
                da_in = (da_in + jnp.sum(above[:, :CHUNK], axis=1, keepdims=True) * oh0
                         + jnp.sum(above[:, CHUNK:], axis=1, keepdims=True) * oh1)
                y_off = _dot(cb_, prevb, 1, 1) * eacol
                r0, r1 = _lane_halves(dy * y_off)
                r_off = r_off + r0 * oh0 + r1 * oh1
                c0, c1 = _lane_halves(xdt * dxdt_inter)
                c_int = c_int + c0 * oh0 + c1 * oh1
                both = jnp.sum(dsh * prev, axis=1, keepdims=True) * ealast
                c_row = (c_row + jnp.sum(_rows(both, 0.0), axis=0, keepdims=True) * oh0
                         + jnp.sum(_rows(0.0, both), axis=0, keepdims=True) * oh1)
                t0, t1 = _lane_halves(dxdt * x)
                ddt = ddt + t0 * oh0 + t1 * oh1
                dx_ref[:, _ps(q)] = dxdt * dcol + dy * dsk_ref[q]
                k0, k1 = _lane_halves(dy * x)
                dskip = (dskip + jnp.sum(k0, axis=0, keepdims=True) * oh0 + jnp.sum(k1, axis=0, keepdims=True) * oh1)
            dcbb = _mx(dcb)
            dx_ref[:, _gs(C_OFF, g)] = dc_acc + _dot(dcbb, bb, 1, 0)
            dx_ref[:, _gs(B_OFF, g)] = db_acc + _dot(dcbb, cb_, 0, 0)
        da = (da_in + _exact_dot(upper_b, r_off, 1, 0, False) + _exact_dot(strict_b, c_int, 1, 0, False) + c_row)
        draw = (ddt + da * a_row) * _sigmoid(zr)
        ddt_ref[...] = draw
        dpar_ref[0:1, :] += jnp.sum(draw, axis=0, keepdims=True)
        dpar_ref[1:2, :] += jnp.sum(da * dtc, axis=0, keepdims=True) * a_row
        dpar_ref[2:3, :] += dskip

    rev = nc - 1
    psp = pl.BlockSpec((1, SSD_PAIRS, LANE, SSD_N), lambda c: (rev - c, 0, 0, 0))
    return pl.pallas_call(
        body, name=name, grid=(nc,),
        in_specs=_ssd_in_specs(rev) + [psp, pl.BlockSpec((CHUNK, SSD_W), lambda c: (rev - c, 0))],
        out_specs=[pl.BlockSpec((CHUNK, CONV_DIM), lambda c: (rev - c, 0)),
                   pl.BlockSpec((CHUNK, LANE), lambda c: (rev - c, 0)), pl.BlockSpec((8, LANE), lambda c: (0, 0))],
        out_shape=[jax.ShapeDtypeStruct((s_, CONV_DIM), F32), jax.ShapeDtypeStruct((s_, LANE), F32),
                   jax.ShapeDtypeStruct((8, LANE), F32)],
        scratch_shapes=[pltpu.VMEM((SSD_PAIRS, LANE, SSD_N), F32)],
        compiler_params=_params("arbitrary"),
    )(xbc, small, dtt, bias_r, bias_c, alog_r, alog_c, dsk, prev, dy)


GN = SSD_W // SSD_G


def _gated_norm_fwd(y, z, w, cat, name="gated_norm_fwd"):
    s_, f = y.shape
    tr = _row_tile(s_)

    def body(y_ref, z_ref, w_ref, cat_ref, o_ref):
        for g in range(SSD_G):
            sl = slice(g * GN, (g + 1) * GN)
            gg = y_ref[:, sl] * _silu(z_ref[:, sl])
            r = lax.rsqrt(jnp.mean(gg * gg, axis=-1, keepdims=True) + EPS)
            o_ref[:, sl] = (gg * r * w_ref[:, sl]).astype(o_ref.dtype)

    row = pl.BlockSpec((tr, f), lambda i: (i, 0))
    wsp = pl.BlockSpec((1, f), lambda i: (0, 0))
    return pl.pallas_call(
        body, name=name, grid=(s_ // tr,),
        in_specs=[row, row, wsp, pl.BlockSpec(memory_space=pl.ANY)], out_specs=pl.BlockSpec((tr, f), lambda i: (i, 1)),
        out_shape=jax.ShapeDtypeStruct(cat.shape, cat.dtype), input_output_aliases={3: 0},
        compiler_params=_params("parallel"),
    )(y, z, w.reshape(1, f), cat)


def _gated_norm_bwd(y, z, w, dout, name="gated_norm_bwd"):
    s_, f = y.shape
    tr = _row_tile(s_)

    def body(y_ref, z_ref, w_ref, do_ref, dy_ref, dz_ref, dw_ref):
        @pl.when(pl.program_id(0) == 0)
        def _():
            dw_ref[...] = jnp.zeros_like(dw_ref)

        for g in range(SSD_G):
            sl = slice(g * GN, (g + 1) * GN)
            yv = y_ref[:, sl]
            zv = z_ref[:, sl]
            dov = do_ref[:, sl].astype(F32)
            sz = _silu(zv)
            gg = yv * sz
            r = lax.rsqrt(jnp.mean(gg * gg, axis=-1, keepdims=True) + EPS)
            gw = dov * w_ref[:, sl]
            c = jnp.mean(gw * gg, axis=-1, keepdims=True)
            dgg = r * gw - gg * (r * r * r * c)
            dy_ref[:, sl] = dgg * sz
            dz_ref[:, sl] = (dgg * yv * _dsilu(zv)).astype(dz_ref.dtype)
            dw_ref[:, sl] += jnp.sum(dov * gg * r, axis=0, keepdims=True)

    row = pl.BlockSpec((tr, f), lambda i: (i, 0))
    wsp = pl.BlockSpec((1, f), lambda i: (0, 0))
    return pl.pallas_call(
        body, name=name, grid=(s_ // tr,),
        in_specs=[row, row, wsp, pl.BlockSpec((tr, f), lambda i: (i, 1))], out_specs=[row, row, wsp],
        out_shape=[jax.ShapeDtypeStruct((s_, f), F32), jax.ShapeDtypeStruct((s_, f), MXU_DTYPE),
                   jax.ShapeDtypeStruct((1, f), F32)],
        compiler_params=_params("arbitrary"),
    )(y, z, w.reshape(1, f), dout)


def _ffn_fwd(vv, w_gate, w_up, name="ffn_gate_up"):
    s_, d = vv.shape
    nb, f8, _ = w_gate.shape
    tm = _pick(s_, (2048, 1024, 512, 256, 128))

    def body(v_ref, wg_ref, wu_ref, g_ref, u_ref, a_ref):
        for rs in _row_slices(tm, 16):
            a = _mx(v_ref[rs, :])
            g = _dot(a, _mx(wg_ref[...]), 1, 1)
            u = _dot(a, _mx(wu_ref[...]), 1, 1)
            s = _sigmoid(g)
            gs = g * s
            g_ref[rs, :] = (u * (s * (1.0 + g * (1.0 - s)))).astype(g_ref.dtype)
            u_ref[rs, :] = gs.astype(u_ref.dtype)
            a_ref[rs, :] = (gs * u).astype(a_ref.dtype)

    wsp = pl.BlockSpec((None, f8, d), lambda j, i: (j, 0, 0))
    osp = pl.BlockSpec((None, tm, f8), lambda j, i: (j, i, 0))
    return pl.pallas_call(
        body, name=name, grid=(nb, s_ // tm),
        in_specs=[pl.BlockSpec((tm, d), lambda j, i: (i, 0)), wsp, wsp], out_specs=[osp] * 3,
        out_shape=[jax.ShapeDtypeStruct((nb, s_, f8), MXU_DTYPE)] * 3,
        compiler_params=_params("parallel", "parallel", vmem=VMEM_LIMIT_WIDE_BYTES),
    )(vv, w_gate, w_up)


def _ffn_bwd_act(dffn, w_down, gate, up, name="ffn_d_act"):
    s_, d = dffn.shape
    nb, f8, _ = w_down.shape
    tm = _pick(s_, (2048, 1024, 512, 256, 128))

    def body(d_ref, w_ref, g_ref, u_ref, dg_ref, du_ref):
        for rs in _row_slices(tm, 16):
            dact = _dot(_mx(d_ref[rs, :]), _mx(w_ref[...]), 1, 1)
            dg_ref[rs, :] = (dact * g_ref[rs, :].astype(F32)).astype(dg_ref.dtype)
            du_ref[rs, :] = (dact * u_ref[rs, :].astype(F32)).astype(du_ref.dtype)

    osp = pl.BlockSpec((None, tm, f8), lambda i, j: (j, i, 0))
    return pl.pallas_call(
        body, name=name, grid=(s_ // tm, nb),
        in_specs=[pl.BlockSpec((tm, d), lambda i, j: (i, 0)), pl.BlockSpec((None, f8, d), lambda i, j: (j, 0, 0)),
                  osp, osp],
        out_specs=[osp, osp], out_shape=[jax.ShapeDtypeStruct((nb, s_, f8), MXU_DTYPE)] * 2,
        compiler_params=_params("parallel", "parallel", vmem=VMEM_LIMIT_WIDE_BYTES),
    )(dffn, w_down, gate, up)


def _ffn_bwd_in(dgate, w_gate, dup, w_up, name="ffn_d_in"):
    nb, s_, f8 = dgate.shape
    d = w_gate.shape[2]
    tm = _pick(s_, (1024, 512, 256, 128))
    tn = _pick(d, (1024, 512, 256, 128))
    per = 2
    steps = nb // per

    def body(*refs):
        ins, o_ref, acc = refs[:4 * per], refs[4 * per], refs[4 * per + 1]
        j = pl.program_id(2)

        @pl.when(j == 0)
        def _():
            acc[...] = jnp.zeros_like(acc)

        for rs in _row_slices(tm, 16):
            part = None
            for t in range(per):
                dg_ref, wg_ref, du_ref, wu_ref = ins[4 * t:4 * t + 4]
                d_ = (_dot(_mx(dg_ref[rs, :]), _mx(wg_ref[...]), 1, 0)
                      + _dot(_mx(du_ref[rs, :]), _mx(wu_ref[...]), 1, 0))
                part = d_ if part is None else part + d_
            acc[rs, :] += part

        @pl.when(j == steps - 1)
        def _():
            o_ref[...] = acc[...]

    def specs(t):
        asp = pl.BlockSpec((None, tm, f8), lambda i, n, j: (j * per + t, i, 0))
        wsp = pl.BlockSpec((None, f8, tn), lambda i, n, j: (j * per + t, 0, n))
        return [asp, wsp, asp, wsp]

    return pl.pallas_call(
        body, name=name, grid=(s_ // tm, d // tn, steps),
        in_specs=[sp for t in range(per) for sp in specs(t)],
        out_specs=pl.BlockSpec((tm, tn), lambda i, n, j: (i, n)),
        out_shape=jax.ShapeDtypeStruct((s_, d), F32), scratch_shapes=[pltpu.VMEM((tm, tn), F32)],
        compiler_params=_params("parallel", "parallel", "arbitrary"),
    )(*((dgate, w_gate, dup, w_up) * per))


def _adam_math(g, w, m, v):
    m2 = ADAM_B1 * m + (1.0 - ADAM_B1) * g
    v2 = ADAM_B2 * v + (1.0 - ADAM_B2) * (g * g)
    m_hat = m2 / (1.0 - ADAM_B1 ** ADAM_STEP)
    v_hat = v2 / (1.0 - ADAM_B2 ** ADAM_STEP)
    delta = -ADAM_LR * (m_hat / (jnp.sqrt(v_hat) + ADAM_EPS) + ADAM_WD * w)
    return delta, m2, v2


def _adamw(parts, own, me, w, m, v, name="adamw"):
    nd, r_, c = parts.shape
    tr = _pick(r_, (128, 64, 32, 16))
    tc = c
    if tr == r_ and r_ > 128:
        tc = _pick(c, (256, 128))

    def body(me_ref, p_ref, own_ref, w_ref, m_ref, v_ref, g_ref, d_ref, m2_ref, v2_ref):
        mine = me_ref[0]
        g = jnp.zeros((tr, tc), F32)
        for i in range(nd):
            g = g + jnp.where(mine == i, own_ref[...], p_ref[i]).astype(F32)
        delta, m2, v2 = _adam_math(g, w_ref[...], m_ref[...], v_ref[...])
        g_ref[...] = g
        d_ref[...] = delta
        m2_ref[...] = m2
        v2_ref[...] = v2

    row = pl.BlockSpec((tr, tc), lambda i, j, me_: (i, j))
    gs = pltpu.PrefetchScalarGridSpec(
        num_scalar_prefetch=1, grid=(r_ // tr, c // tc),
        in_specs=[pl.BlockSpec((nd, tr, tc), lambda i, j, me_: (0, i, j)),
                  pl.BlockSpec((None, tr, tc), lambda i, j, me_: (me_[0], i, j)), row, row, row],
        out_specs=[row] * 4)
    return pl.pallas_call(
        body, name=name, grid_spec=gs, out_shape=[jax.ShapeDtypeStruct((r_, c), F32)] * 4,
        compiler_params=_params("parallel", "parallel"),
    )(me, parts, own, w, m, v)


def _adamw_small(parts, w, m, v, name="adamw_small"):
    nd = parts.shape[0]

    def body(p_ref, w_ref, m_ref, v_ref, g_ref, d_ref, m2_ref, v2_ref):
        g = p_ref[0]
        for i in range(1, nd):
            g = g + p_ref[i]
        delta, m2, v2 = _adam_math(g, w_ref[...], m_ref[...], v_ref[...])
        g_ref[...] = g
        d_ref[...] = delta
        m2_ref[...] = m2
        v2_ref[...] = v2

    return pl.pallas_call(
        body, name=name, out_shape=[jax.ShapeDtypeStruct(w.shape, F32)] * 4,
        compiler_params=pltpu.CompilerParams(vmem_limit_bytes=VMEM_LIMIT_BYTES),
    )(parts, w, m, v)


_HBM = pl.BlockSpec(memory_space=pltpu.HBM)
_MESH = pl.DeviceIdType.MESH


def _all_gather(xs, name):
    na = len(xs)

    def body(*refs):
        x_refs, out_refs = refs[:na], refs[na:2 * na]
        send_sems, recv_sems, local_sems = refs[2 * na:]
        x, y, c = lax.axis_index("x"), lax.axis_index("y"), lax.axis_index("c")
        me, sibling = (x, y, c), (x, y, 1 - c)
        near = [(1 - x, y), (x, 1 - y)]
        chips = near + [(1 - x, 1 - y)]
        relay_from = (x + c * (1 - 2 * x), y + (1 - c) * (1 - 2 * y))
        relay_to = (x + (1 - c) * (1 - 2 * x), y + c * (1 - 2 * y))

        def slot(a, px, py, pc):
            return out_refs[a].at[4 * px + 2 * py + pc]

        def copy(a, k, block, to, src=None):
            return pltpu.make_async_remote_copy(
                src_ref=slot(a, *block) if src is None else src, dst_ref=slot(a, *block),
                send_sem=send_sems.at[a, k], recv_sem=recv_sems.at[a, k], device_id=to, device_id_type=_MESH)

        mine = [pltpu.make_async_copy(x_refs[a], slot(a, *me), local_sems.at[a]) for a in range(na)]
        started = []
        for a in range(na):
            mine[a].start()
            first = [copy(a, 0, me, sibling, src=x_refs[a])]
            first += [copy(a, 1 + j, me, (*chip, c), src=x_refs[a]) for j, chip in enumerate(near)]
            for cp in first:
                cp.start()
            started += first
        for a in range(na):
            for j, chip in enumerate(chips):
                copy(a, 1 + j, (*chip, c), me).wait_recv()
                fwd = copy(a, 4 + j, (*chip, c), sibling)
                fwd.start()
                started.append(fwd)
                if j == len(near) - 1:
                    relay = copy(a, 1 + len(near), (*relay_from, c), (*relay_to, c))
                    relay.start()
                    started.append(relay)
        for a in range(na):
            copy(a, 0, sibling, me).wait_recv()
            for j, chip in enumerate(chips):
                copy(a, 4 + j, (*chip, 1 - c), me).wait_recv()
        for cp in started:
            cp.wait_send()
        for cp in mine:
            cp.wait()

    return pl.pallas_call(
        body, name=name, out_shape=[jax.ShapeDtypeStruct((N_DEV,) + t.shape, t.dtype) for t in xs],
        in_specs=[_HBM] * na, out_specs=[_HBM] * na,
        scratch_shapes=[pltpu.SemaphoreType.DMA((na, 7)), pltpu.SemaphoreType.DMA((na, 7)),
                        pltpu.SemaphoreType.DMA((na,))],
    )(*xs)


_SEM = pl.BlockSpec(memory_space=pltpu.SEMAPHORE)
_EFFECT = pltpu.SideEffectType.DATAFLOW_SIDE_EFFECTING


def _peers(x, y, c):
    out = []
    for k in range(1, N_DEV):
        px = 1 - x if k & 4 else x
        py = 1 - y if k & 2 else y
        pc = 1 - c if k & 1 else c
        out.append(((px, py, pc), 4 * px + 2 * py + pc))
    return out


def _push_copies(scatter, src_refs, land_refs, send_sems, recv_sems):
    x, y, c = lax.axis_index("x"), lax.axis_index("y"), lax.axis_index("c")
    me = 4 * x + 2 * y + c
    pairs = []
    for a, (src, land) in enumerate(zip(src_refs, land_refs)):
        for k, (peer, slot) in enumerate(_peers(x, y, c)):
            out_src = src.at[slot] if scatter else src
            si = a * (N_DEV - 1) + k
            send = pltpu.make_async_remote_copy(src_ref=out_src, dst_ref=land.at[me], send_sem=send_sems.at[si],
                                                recv_sem=recv_sems.at[si], device_id=peer, device_id_type=_MESH)
            recv = pltpu.make_async_remote_copy(src_ref=out_src, dst_ref=land.at[slot], send_sem=send_sems.at[si],
                                                recv_sem=recv_sems.at[si], device_id=peer, device_id_type=_MESH)
            pairs.append((send, recv))
    return pairs


def _push_start(srcs, scatter, dep, name):
    na = len(srcs)
    shapes = [t.shape[1:] if scatter else t.shape for t in srcs]
    lands = [pltpu.with_memory_space_constraint(lax.empty((N_DEV,) + s, t.dtype), pltpu.HBM) for s, t in zip(shapes, srcs)]

    def body(*refs):
        src_refs, land_refs = refs[:na], refs[na:2 * na]
        send_sems, recv_sems = refs[2 * na + 1], refs[2 * na + 2]
        token = refs[-1]
        for send, _ in _push_copies(scatter, src_refs, land_refs, send_sems, recv_sems):
            send.start()
        token[...] = jnp.zeros_like(token)

    sem = pltpu.SemaphoreType.DMA((na * (N_DEV - 1),))
    outs = pl.pallas_call(
        body, name=name,
        out_shape=(sem, sem) + tuple(pltpu.HBM(t.shape, t.dtype) for t in srcs)
        + tuple(pltpu.HBM(t.shape, t.dtype) for t in lands) + (jax.ShapeDtypeStruct((8, LANE), F32),),
        in_specs=[_HBM] * (2 * na) + [pl.BlockSpec(memory_space=pl.ANY)],
        out_specs=(_SEM, _SEM) + (_HBM,) * (2 * na) + (pl.BlockSpec(memory_space=pltpu.VMEM),),
        input_output_aliases={i: 2 + i for i in range(2 * na)},
        compiler_params=pltpu.CompilerParams(has_side_effects=_EFFECT),
    )(*[pltpu.with_memory_space_constraint(t, pltpu.HBM) for t in srcs], *lands, dep)
    return outs[0], outs[1], outs[2:2 + na], outs[2 + na:2 + 2 * na], outs[-1]


def _push_wait(send_sems, recv_sems, src_thru, land_thru, scatter, after, name):
    na = len(src_thru)

    def body(*refs):
        src_refs, land_refs = refs[:na], refs[na:2 * na]
        ssem, rsem = refs[2 * na], refs[2 * na + 1]
        for send, recv in _push_copies(scatter, src_refs, land_refs, ssem, rsem):
            send.wait_send()
            recv.wait_recv()

    outs = pl.pallas_call(
        body, name=name,
        out_shape=tuple(pltpu.HBM(t.shape, t.dtype) for t in src_thru) + tuple(pltpu.HBM(t.shape, t.dtype) for t in land_thru),
        in_specs=[_HBM] * (2 * na) + [_SEM, _SEM, pl.BlockSpec(memory_space=pl.ANY)],
        out_specs=(_HBM,) * (2 * na),
        input_output_aliases={i: i for i in range(2 * na)},
        compiler_params=pltpu.CompilerParams(has_side_effects=_EFFECT),
    )(*src_thru, *land_thru, send_sems, recv_sems, after)
    return outs[:na], outs[na:]


def _exchange_behind(srcs, scatter, dep, name):
    send_sems, recv_sems, thru, lands, token = _push_start(srcs, scatter, dep, name + "_start")

    def finish(after, place=True):
        src_done, land_done = _push_wait(send_sems, recv_sems, thru, lands, scatter, after, name + "_wait")
        if not place:
            return land_done, src_done
        return _place_own(land_done, src_done, scatter, name + "_own")

    return token[0, 0], finish


def _place_own(lands, srcs, scatter, name):
    me = (4 * lax.axis_index("x") + 2 * lax.axis_index("y") + lax.axis_index("c")).astype(jnp.int32).reshape(1)
    outs = []
    for a, (land, src) in enumerate(zip(lands, srcs)):
        r_, c_ = land.shape[1:]
        tr = _pick(r_, (512, 256, 128, 64, 32, 16))

        def body(me_ref, land_ref, src_ref, out_ref):
            out_ref[...] = src_ref[...]

        src_spec = (pl.BlockSpec((None, tr, c_), lambda i, me_: (me_[0], i, 0)) if scatter
                    else pl.BlockSpec((tr, c_), lambda i, me_: (i, 0)))
        gs = pltpu.PrefetchScalarGridSpec(
            num_scalar_prefetch=1, grid=(r_ // tr,),
            in_specs=[pl.BlockSpec(memory_space=pl.ANY), src_spec],
            out_specs=pl.BlockSpec((None, tr, c_), lambda i, me_: (me_[0], i, 0)))
        outs.append(pl.pallas_call(
            body, name=f"{name}_{a}", grid_spec=gs, out_shape=jax.ShapeDtypeStruct(land.shape, land.dtype),
            input_output_aliases={1: 0}, compiler_params=_params("arbitrary"),
        )(me, land, src))
    return outs


_TRANSPOSED = ("w_in", "w_uq", "w_gate", "w_up")
_CQKV = (0, Q_RANK + KV_RANK)
_KR = (_CQKV[1], _CQKV[1] + ROPE)
_Z = (_KR[1], _KR[1] + SSD_W)
_XBC = (_Z[1], _Z[1] + CONV_DIM)
_DT = (_XBC[1], _XBC[1] + SSD_H)


def _win_segments(w_in_t):
    w = w_in_t.reshape(D_IN, D_MODEL)
    small = jnp.concatenate([w[_KR[0]:_KR[1]], w[_DT[0]:_DT[1]],
                             jnp.zeros((LANE - ROPE - SSD_H, D_MODEL), w.dtype)], axis=0)
    return w[_CQKV[0]:_CQKV[1]], w[_Z[0]:_Z[1]], w[_XBC[0]:_XBC[1]], small


def _win_from_segments(g_cqkv, g_z, g_xbc, g_small):
    w = jnp.concatenate([g_cqkv, g_small[:ROPE], g_z, g_xbc, g_small[ROPE:ROPE + SSD_H]], axis=0)
    return w.reshape(N_DEV, D_IN // N_DEV, D_MODEL)


_SMALL = (("q_norm_w", 512), ("kv_norm_w", 512), ("conv_b", CONV_DIM), ("dt_bias", SSD_H), ("a_log", SSD_H),
          ("d_skip", SSD_H), ("ssd_norm_w", SSD_W), ("attn_out_norm_w", 1024), ("pre_mix_norm_w", D_MODEL),
          ("post_mix_norm_w", D_MODEL), ("pre_ffn_norm_w", D_MODEL), ("post_ffn_norm_w", D_MODEL),
          ("conv_w", CONV_K * CONV_DIM))
_SMALL_ROWS = -(-(sum(-(-n // LANE) for _, n in _SMALL) + 1) // 8) * 8


def _pack_small(vals):
    rows = []
    for name, n in _SMALL:
        v = vals[name].reshape(-1).astype(F32)
        pad = -(-n // LANE) * LANE
        rows.append(jnp.pad(v, (0, pad - n)).reshape(-1, LANE))
    m = jnp.concatenate(rows, axis=0)
    return jnp.pad(m, ((0, _SMALL_ROWS - m.shape[0]), (0, 0)))


def _unpack_small(m):
    out, r = {}, 0
    for name, n in _SMALL:
        nr = -(-n // LANE)
        out[name] = m[r:r + nr].reshape(-1)[:n]
        r += nr
    return out


def _head_row(v):
    return jnp.pad(v.reshape(1, -1).astype(F32), ((0, 0), (HEAD_LANE, LANE - HEAD_LANE - v.shape[-1])))


def _local_step(x, positions, target, wg, small, weights, on_grads):
    w_cqkv, w_z, w_xbc, w_small = _win_segments(wg["w_in"])
    conv_w = wg["conv_w"]
    conv_b = small["conv_b"].reshape(1, CONV_DIM)
    qkv_norm_w = jnp.concatenate([small["q_norm_w"], small["kv_norm_w"]])
    attn_norm_w = small["attn_out_norm_w"].reshape(1, HEADS * VDIM)
    scale = QK ** -0.5

    inv_freq = ROPE_THETA ** (-jnp.arange(0, ROPE, 2, dtype=F32) / ROPE)
    ang = positions.astype(F32)[:, None] * inv_freq
    cos2 = jnp.tile(jnp.cos(ang), (1, 2))
    sin2 = jnp.tile(jnp.sin(ang), (1, 2))

    u = _rms_fwd(x, small["pre_mix_norm_w"], out_dtype=MXU_DTYPE, name="pre_mix_norm")
    cqkv = _mm(u, w_cqkv, "nt", name="in_proj_qkv")
    z = _mm(u, w_z, "nt", name="in_proj_z")
    xbc = _mm(u, w_xbc, "nt", name="in_proj_xbc")
    sm = _mm(u, w_small, "nt", name="in_proj_small")

    w_uq, w_ukv = weights("qkv_up", cqkv)
    qkvn = _rms_fwd(cqkv, qkv_norm_w, groups=2, out_dtype=MXU_DTYPE, name="qkv_norm")
    q_h = _q_up(qkvn, w_uq, cos2, sin2, scale)
    k_h, v_h = _kv_up(qkvn, w_ukv, sm, cos2, sin2)
    o_h, lse = _flash_fwd(q_h, k_h, v_h)
    cat = _hnorm_fwd(o_h, attn_norm_w, D_MODEL)
    w_out = weights("out", o_h)[0].reshape(D_MODEL, D_MODEL)

    xbc_act = _conv_fwd(xbc, conv_w, conv_b)
    dtt = jnp.transpose(sm[:, HEAD_LANE:HEAD_LANE + SSD_H])
    ssd_args = (xbc_act, sm, dtt, _head_row(small["dt_bias"]), small["dt_bias"].reshape(SSD_H, 1),
                _head_row(small["a_log"]), small["a_log"].reshape(SSD_H, 1),
                jnp.broadcast_to(small["d_skip"].reshape(SSD_H, 1), (SSD_H, SSD_P)).reshape(SSD_PAIRS, 1, LANE))
    y_ssd, prev = _ssd_fwd(*ssd_args)
    cat = _gated_norm_fwd(y_ssd, z, small["ssd_norm_w"], cat)

    mix = _mm(cat, w_out, "nn", name="out_proj")
    h1, vv = _norm_res_norm(mix, x, small["post_mix_norm_w"], small["pre_ffn_norm_w"])

    w_gate, w_up = weights("ffn_in", mix)
    gate, up, act = _ffn_fwd(vv, w_gate, w_up)
    w_down, = weights("ffn_out", act)
    ffn = _mm(act, w_down, "nn", a_blk=True, b_blk=True, fuse=N_DEV, tm_max=512, name="ffn_down")
    loss_blk, dy, dffn, g_post_ffn = _loss_head(ffn, h1, target, small["post_ffn_norm_w"])

    g_down = _mm(act, dffn, "tn", a_blk=True, out_blk=True, out_dtype=MXU_DTYPE, name="g_down")
    dgate, dup = _ffn_bwd_act(dffn, w_down, gate, up)
    dvv = _ffn_bwd_in(dgate, w_gate, dup, w_up)
    g_gate = _mm(dgate, vv, "tn", a_blk=True, out_blk=True, out_dtype=MXU_DTYPE, name="g_gate")
    g_up = _mm(dup, vv, "tn", a_blk=True, out_blk=True, out_dtype=MXU_DTYPE, name="g_up")
    pre_ffn_w = small["pre_ffn_norm_w"] + on_grads("ffn", [g_gate, g_up, g_down])
    dh1, dmix, g_pre_ffn, g_post_mix = _norm_res_norm_bwd(h1, pre_ffn_w, dvv, dy, mix, small["post_mix_norm_w"])

    dcat = _mm(dmix, w_out, "nt", name="d_cat")
    g_out = _mm(cat, dmix, "tn", out_dtype=MXU_DTYPE, name="g_out")

    do_h, delta, g_attn_norm = _hnorm_bwd(o_h, attn_norm_w, dcat)
    dq_h, dk_h, dv_h = _flash_bwd(q_h, k_h, v_h, do_h, lse, delta)
    dq = _q_prep(dq_h, cos2, -sin2, scale, name="dq_post")

    dy_ssd, dz, g_ssd_norm = _gated_norm_bwd(y_ssd, z, small["ssd_norm_w"], dcat)
    dxbc_act, ddt, dpar = _ssd_bwd(*ssd_args, prev, dy_ssd)
    dkv, dsm = _dkv_post(dk_h, dv_h, ddt, cos2, -sin2)
    dpre, dwb = _conv_bwd_pre(xbc, conv_w, conv_b, dxbc_act)
    dxbc = _conv_bwd_in(dpre, conv_w)

    dqn = _mm(dq, w_uq, "nn", a_blk=True, b_blk=True, fuse=HEADS, name="d_qn")
    dkvn = _mm(dkv, w_ukv, "nt", a_blk=True, b_blk=True, fuse=HEADS, name="d_kvn")
    g_uq = _mm(dq, qkvn, "tn", a_blk=True, out_blk=True, b_cols=(0, Q_RANK), out_dtype=MXU_DTYPE, name="g_uq")
    g_ukv = _mm(qkvn, dkv, "tn", b_blk=True, out_blk=True, a_cols=(Q_RANK, KV_RANK), out_dtype=MXU_DTYPE, name="g_ukv")
    heads_token = on_grads("heads", [g_uq, g_ukv, g_out.reshape(N_DEV, D_MODEL // N_DEV, D_MODEL)])
    dcqkv, g_qkv_norm = _rms_bwd(cqkv, qkv_norm_w + heads_token, [dqn, dkvn], out_dtype=MXU_DTYPE, name="qkv_norm_bwd")

    g_in = _win_from_segments(_mm(dcqkv, u, "tn", out_dtype=MXU_DTYPE, name="g_in_qkv"),
                              _mm(dz, u, "tn", out_dtype=MXU_DTYPE, name="g_in_z"),
                              _mm(dxbc, u, "tn", out_dtype=MXU_DTYPE, name="g_in_xbc"),
                              _mm(dsm, u, "tn", out_dtype=MXU_DTYPE, name="g_in_small"))
    in_token = on_grads("in", [g_in])
    du = _mm_sum([dsm + in_token.astype(dsm.dtype), dcqkv, dz, dxbc], [w_small, w_cqkv, w_z, w_xbc], name="d_u")
    dx, g_pre_mix = _rms_bwd(x, small["pre_mix_norm_w"], [du], res=dh1, name="pre_mix_norm_bwd")

    hl = slice(HEAD_LANE, HEAD_LANE + SSD_H)
    g_small = {"q_norm_w": g_qkv_norm[0, :Q_RANK], "kv_norm_w": g_qkv_norm[0, Q_RANK:], "conv_b": dwb[CONV_K],
               "dt_bias": dpar[0, hl], "a_log": dpar[1, hl], "d_skip": dpar[2, hl], "ssd_norm_w": g_ssd_norm,
               "attn_out_norm_w": g_attn_norm, "pre_mix_norm_w": g_pre_mix, "post_mix_norm_w": g_post_mix,
               "pre_ffn_norm_w": g_pre_ffn, "post_ffn_norm_w": g_post_ffn, "conv_w": dwb[:CONV_K]}
    return loss_blk[0, 0], dx, g_small


_WEIGHT_ORDER = ("w_in", "q_norm_w", "w_uq", "kv_norm_w", "w_ukv", "conv_w", "conv_b", "dt_bias", "a_log", "d_skip",
                 "ssd_norm_w", "attn_out_norm_w", "w_out", "pre_mix_norm_w", "post_mix_norm_w", "pre_ffn_norm_w",
                 "post_ffn_norm_w", "w_gate", "w_up", "w_down")


def kernel(x, positions, w_in, q_norm_w, w_uq, kv_norm_w, w_ukv, conv_w, conv_b, dt_bias, a_log, d_skip, ssd_norm_w, attn_out_norm_w, w_out, pre_mix_norm_w, post_mix_norm_w, pre_ffn_norm_w, post_ffn_norm_w, w_gate, w_up, w_down, loss_target, m_w_in, m_q_norm_w, m_w_uq, m_kv_norm_w, m_w_ukv, m_conv_w, m_conv_b, m_dt_bias, m_a_log, m_d_skip, m_ssd_norm_w, m_attn_out_norm_w, m_w_out, m_pre_mix_norm_w, m_post_mix_norm_w, m_pre_ffn_norm_w, m_post_ffn_norm_w, m_w_gate, m_w_up, m_w_down, v_w_in, v_q_norm_w, v_w_uq, v_kv_norm_w, v_w_ukv, v_conv_w, v_conv_b, v_dt_bias, v_a_log, v_d_skip, v_ssd_norm_w, v_attn_out_norm_w, v_w_out, v_pre_mix_norm_w, v_post_mix_norm_w, v_pre_ffn_norm_w, v_post_ffn_norm_w, v_w_gate, v_w_up, v_w_down):
    w = dict(w_in=w_in, q_norm_w=q_norm_w, w_uq=w_uq, kv_norm_w=kv_norm_w, w_ukv=w_ukv, conv_w=conv_w, conv_b=conv_b,
             dt_bias=dt_bias, a_log=a_log, d_skip=d_skip, ssd_norm_w=ssd_norm_w, attn_out_norm_w=attn_out_norm_w,
             w_out=w_out, pre_mix_norm_w=pre_mix_norm_w, post_mix_norm_w=post_mix_norm_w,
             pre_ffn_norm_w=pre_ffn_norm_w, post_ffn_norm_w=post_ffn_norm_w, w_gate=w_gate, w_up=w_up, w_down=w_down)
    m = dict(w_in=m_w_in, q_norm_w=m_q_norm_w, w_uq=m_w_uq, kv_norm_w=m_kv_norm_w, w_ukv=m_w_ukv, conv_w=m_conv_w,
             conv_b=m_conv_b, dt_bias=m_dt_bias, a_log=m_a_log, d_skip=m_d_skip, ssd_norm_w=m_ssd_norm_w,
             attn_out_norm_w=m_attn_out_norm_w, w_out=m_w_out, pre_mix_norm_w=m_pre_mix_norm_w,
             post_mix_norm_w=m_post_mix_norm_w, pre_ffn_norm_w=m_pre_ffn_norm_w, post_ffn_norm_w=m_post_ffn_norm_w,
             w_gate=m_w_gate, w_up=m_w_up, w_down=m_w_down)
    v = dict(w_in=v_w_in, q_norm_w=v_q_norm_w, w_uq=v_w_uq, kv_norm_w=v_kv_norm_w, w_ukv=v_w_ukv, conv_w=v_conv_w,
             conv_b=v_conv_b, dt_bias=v_dt_bias, a_log=v_a_log, d_skip=v_d_skip, ssd_norm_w=v_ssd_norm_w,
             attn_out_norm_w=v_attn_out_norm_w, w_out=v_w_out, pre_mix_norm_w=v_pre_mix_norm_w,
             post_mix_norm_w=v_post_mix_norm_w, pre_ffn_norm_w=v_pre_ffn_norm_w, post_ffn_norm_w=v_post_ffn_norm_w,
             w_gate=v_w_gate, w_up=v_w_up, w_down=v_w_down)
    w, m, v = ({k: t[0] for k, t in d.items()} for d in (w, m, v))
    me = 4 * lax.axis_index("x") + 2 * lax.axis_index("y") + lax.axis_index("c")
    groups = {"qkv_up": ("w_uq", "w_ukv"), "out": ("w_out",), "ffn_in": ("w_gate", "w_up"), "ffn_out": ("w_down",)}
    cshard = CONV_DIM // N_DEV
    for name in _TRANSPOSED:
        w[name], m[name], v[name] = w[name].T, m[name].T, v[name].T

    shards = [w["w_in"].astype(MXU_DTYPE),
              jnp.stack(_split3(w["conv_w"])).reshape(3 * CONV_K, cshard).astype(MXU_DTYPE)]
    w_in_g, cw = _all_gather(shards, name="gather_weights")
    cw = cw.astype(F32).reshape(N_DEV, 3, CONV_K, cshard)
    wg = {"w_in": w_in_g, "conv_w": jnp.transpose(cw[:, 0] + cw[:, 1] + cw[:, 2], (1, 0, 2)).reshape(CONV_K, CONV_DIM)}
    arriving, dep, started = {}, wg["conv_w"], jnp.zeros((), F32)
    small = {name: w[name] for name, _ in _SMALL if name != "conv_w"}
    for group in ("qkv_up", "out", "ffn_in", "ffn_out"):
        token, arriving[group] = _exchange_behind([w[name].astype(MXU_DTYPE) for name in groups[group]], False,
                                                  dep, group + "_weights")
        started = started + token
        dep = jnp.zeros((8, LANE), F32) + started
    small["pre_mix_norm_w"] = small["pre_mix_norm_w"] + started

    leaving = {}

    def on_grads(group, gs):
        token, leaving[group] = _exchange_behind(gs, True, jnp.zeros((8, LANE), F32), group + "_grads")
        return token

    loss_local, dx, g_small = _local_step(x[0], positions[0], loss_target[0], wg, small,
                                          lambda group, after: arriving[group](after), on_grads)
    recv = {}
    for group, names in (("ffn", ("w_gate", "w_up", "w_down")), ("heads", ("w_uq", "w_ukv", "w_out")), ("in", ("w_in",))):
        recv.update(zip(names, zip(*leaving[group](dx, place=False))))
    grads, deltas, new_m, new_v = {}, {}, {}, {}
    me1 = me.astype(jnp.int32).reshape(1)
    for name, (parts, own) in recv.items():
        outs = _adamw(parts, own, me1, w[name], m[name], v[name], name="adamw_" + name)
        if name in _TRANSPOSED:
            outs = [t.T for t in outs]
        grads[name], deltas[name], new_m[name], new_v[name] = outs

    def embed(t):
        return lax.dynamic_update_slice(jnp.zeros((CONV_K, CONV_DIM), F32), t, (0, me * cshard))

    mine_s = _pack_small(g_small).at[_SMALL_ROWS - 1, 0].set(loss_local)
    parts_s = _all_gather([mine_s], name="gather_small_grads")[0]
    packs = [_pack_small({**{n_: d[n_] for n_, _ in _SMALL if n_ != "conv_w"}, "conv_w": embed(d["conv_w"])})
             for d in (w, m, v)]
    summed = _adamw_small(parts_s, *packs)
    loss = summed[0][_SMALL_ROWS - 1, 0]
    outs = [_unpack_small(t) for t in summed]
    for name, n in _SMALL:
        for dst, src in zip((grads, deltas, new_m, new_v), outs):
            if name == "conv_w":
                dst[name] = lax.dynamic_slice(src[name].reshape(CONV_K, CONV_DIM), (0, me * cshard), (CONV_K, cshard))
            else:
                dst[name] = src[name]

    def lead(d):
        return [d[name][None] for name in _WEIGHT_ORDER]

    return (loss, dx[None], *lead(grads), *lead(deltas), *lead(new_m), *lead(new_v))
```

```python
import numpy as np

import jax
import jax.numpy as jnp
from jax import lax
from jax.experimental import pallas as pl
from jax.experimental.pallas import tpu as pltpu

F32 = jnp.float32
BF16 = jnp.bfloat16
MXU_DTYPE = jnp.bfloat16
EPS = 1e-6
VMEM_LIMIT_BYTES = 48 * 1024 * 1024
VMEM_LIMIT_WIDE_BYTES = 56 * 1024 * 1024
K_TILE_MAX = 2048

N_DEV = 8
D_MODEL = 2048
Q_RANK = 512
KV_RANK = 512
ROPE = 64
HALF = ROPE // 2
HEADS = 8
NOPE = 128
VDIM = 128
QK = NOPE + ROPE
SSD_W = 1024
SSD_H = 16
SSD_P = 64
SSD_G = 2
SSD_E = SSD_H // SSD_G
SSD_N = 128
CHUNK = 128
CONV_K = 4
CONV_DIM = SSD_W + 2 * SSD_G * SSD_N
B_OFF = SSD_W
C_OFF = SSD_W + SSD_G * SSD_N
D_FF = 5632
D_IN = Q_RANK + KV_RANK + ROPE + SSD_W + CONV_DIM + SSD_H
ROPE_THETA = 10000.0
LANE = 128
HEAD_LANE = ROPE

ADAM_LR = 0.001
ADAM_B1 = 0.9
ADAM_B2 = 0.999
ADAM_EPS = 1e-08
ADAM_WD = 0.01
ADAM_STEP = 10


def _pick(n, cands):
    for c in cands:
        if n % c == 0:
            return c
    return n


def _params(*sem, vmem=VMEM_LIMIT_BYTES):
    return pltpu.CompilerParams(dimension_semantics=sem, vmem_limit_bytes=vmem)


def _sigmoid(x):
    return 1.0 / (1.0 + jnp.exp(-x))


def _silu(x):
    return x * _sigmoid(x)


def _dsilu(x):
    s = _sigmoid(x)
    return s * (1.0 + x * (1.0 - s))


def _softplus(x):
    e = jnp.exp(-jnp.abs(x))
    small = e * (1.0 - e * (0.5 - e * (1.0 / 3.0)))
    return jnp.maximum(x, 0.0) + jnp.where(e < 0.01, small, jnp.log(1.0 + e))


def _dot(a, b, ca, cb):
    return lax.dot_general(a, b, (((ca,), (cb,)), ((), ())), preferred_element_type=F32)


def _mx(v):
    return v.astype(MXU_DTYPE)


def _split3(a):
    hi = a.astype(BF16)
    r1 = a - hi.astype(F32)
    mid = r1.astype(BF16)
    lo = (r1 - mid.astype(F32)).astype(BF16)
    return hi, mid, lo


def _exact_dot(a, b, ca, cb, split_a):
    if split_a:
        return sum(_dot(p, b, ca, cb) for p in _split3(a))
    return sum(_dot(a, p, ca, cb) for p in _split3(b))


MM_ROW_GROUPS = 4


def _row_slices(tm, align):
    ng = MM_ROW_GROUPS
    while ng > 1 and (tm % ng or (tm // ng) % align):
        ng //= 2
    return [slice(g * (tm // ng), (g + 1) * (tm // ng)) for g in range(ng)]


def _mm(a, b, mode, *, a_blk=False, b_blk=False, out_blk=False, a_cols=None, b_cols=None, add=None, out_dtype=F32,
        fuse=1, wide=False, tm_max=1024, name="mm"):
    a2, b2 = a.shape[-2:], b.shape[-2:]
    a_last = a2[1] if a_cols is None else a_cols[1]
    a_start = 0 if a_cols is None else a_cols[0]
    b_start = 0
    if b_cols is not None:
        assert mode != "nt"
        b_start, b2 = b_cols[0], (b2[0], b_cols[1])
    if mode == "nn":
        m, k, (k2, n) = a2[0], a_last, b2
    elif mode == "nt":
        m, k, (n, k2) = a2[0], a_last, b2
    else:
        k, m, (k2, n) = a2[0], a_last, b2
    assert k == k2, (a.shape, b.shape, mode)
    tm = _pick(m, tuple(c for c in (1024, 704, 512, 256, 128) if c <= tm_max))
    tn = _pick(n, ((2048,) if wide else ()) + (1024, 768, 704, 512, 256, 192, 128))
    k_max = 2 * K_TILE_MAX if mode == "tn" else K_TILE_MAX
    tk = k if k <= k_max else _pick(k, (K_TILE_MAX, 1024, 512))
    nk = k // tk
    jo = N_DEV if out_blk else 1
    reduce_blocks = a_blk and b_blk and not out_blk
    assert fuse == 1 or reduce_blocks
    jr = N_DEV // fuse if reduce_blocks else 1
    ca, cb = {"nn": (1, 0), "nt": (1, 1), "tn": (0, 0)}[mode]
    has_add = add is not None
    single = jr * nk == 1
    if mode == "tn":
        assert a_start % tm == 0
        a_block, a_idx = (tk, tm), (lambda i, kk: (kk, i + a_start // tm))
    else:
        assert a_start % tk == 0
        a_block, a_idx = (tm, tk), (lambda i, kk: (i, kk + a_start // tk))
    assert b_start % tn == 0
    b_block, b_idx = (((tn, tk), (lambda nn_, kk: (nn_, kk))) if mode == "nt"
                      else ((tk, tn), (lambda nn_, kk: (kk, nn_ + b_start // tn))))

    def blk_specs(blocked, block, idx, of_a, t):
        def pos(o, i, nn_, kk):
            return idx(i, kk) if of_a else idx(nn_, kk)
        if blocked:
            return pl.BlockSpec((None,) + block,
                                lambda o, i, nn_, r, kk: ((o if out_blk else r * fuse + t),) + pos(o, i, nn_, kk))
        return pl.BlockSpec(block, lambda o, i, nn_, r, kk: pos(o, i, nn_, kk))

    a_specs = [blk_specs(a_blk, a_block, a_idx, True, t) for t in range(fuse)]
    b_specs = [blk_specs(b_blk, b_block, b_idx, False, t) for t in range(fuse)]
    o_spec = (pl.BlockSpec((None, tm, tn), lambda o, i, nn_, r, kk: (o, i, nn_)) if out_blk
              else pl.BlockSpec((tm, tn), lambda o, i, nn_, r, kk: (i, nn_)))

    groups = _row_slices(tm, LANE if mode == "tn" else 16)

    def body(*refs):
        a_refs, b_refs = refs[:fuse], refs[fuse:2 * fuse]
        add_ref = refs[2 * fuse] if has_add else None
        o_ref = refs[2 * fuse + 1] if has_add else refs[2 * fuse]

        def partial(rs):
            out = None
            for t in range(fuse):
                av = a_refs[t][:, rs] if mode == "tn" else a_refs[t][rs, :]
                d = _dot(_mx(av), _mx(b_refs[t][...]), ca, cb)
                out = d if out is None else out + d
            return out

        if single:
            for rs in groups:
                res = partial(rs)
                if has_add:
                    res = res + add_ref[rs, :]
                o_ref[rs, :] = res.astype(o_ref.dtype)
            return
        acc = refs[-1]
        r, kk = pl.program_id(3), pl.program_id(4)

        @pl.when(jnp.logical_and(r == 0, kk == 0))
        def _():
            acc[...] = jnp.zeros_like(acc)

        for rs in groups:
            acc[rs, :] += partial(rs)

        @pl.when(jnp.logical_and(r == jr - 1, kk == nk - 1))
        def _():
            res = acc[...]
            if has_add:
                res = res + add_ref[...]
            o_ref[...] = res.astype(o_ref.dtype)

    out_shape = ((N_DEV, m, n) if out_blk else (m, n))
    return pl.pallas_call(
        body, name=name, grid=(jo, m // tm, n // tn, jr, nk),
        in_specs=a_specs + b_specs + ([o_spec] if has_add else []), out_specs=o_spec,
        out_shape=jax.ShapeDtypeStruct(out_shape, out_dtype),
        scratch_shapes=[] if single else [pltpu.VMEM((tm, tn), F32)],
        compiler_params=_params("parallel", "parallel", "parallel", "arbitrary", "arbitrary"),
    )(*((a,) * fuse + (b,) * fuse + ((add,) if has_add else ())))


def _mm_sum(a_list, b_list, name="mm_sum"):
    m, n = a_list[0].shape[0], b_list[0].shape[1]
    ns = len(a_list)
    tm = _pick(m, (1024, 512, 256, 128))
    tn = _pick(n, (1024, 512, 256, 128))
    groups = _row_slices(tm, 16)

    def body(*refs):
        a_refs, b_refs, o_ref = refs[:ns], refs[ns:2 * ns], refs[2 * ns]
        for rs in groups:
            acc = _dot(_mx(a_refs[0][rs, :]), _mx(b_refs[0][...]), 1, 0)
            for s in range(1, ns):
                acc = acc + _dot(_mx(a_refs[s][rs, :]), _mx(b_refs[s][...]), 1, 0)
            o_ref[rs, :] = acc

    return pl.pallas_call(
        body, name=name, grid=(m // tm, n // tn),
        in_specs=([pl.BlockSpec((tm, a.shape[1]), lambda i, j: (i, 0)) for a in a_list]
                  + [pl.BlockSpec((b.shape[0], tn), lambda i, j: (0, j)) for b in b_list]),
        out_specs=pl.BlockSpec((tm, tn), lambda i, j: (i, j)),
        out_shape=jax.ShapeDtypeStruct((m, n), F32), compiler_params=_params("parallel", "parallel"),
    )(*a_list, *b_list)


def _row_tile(r_, streams=4):
    return _pick(r_, ((512,) if streams <= 4 else ()) + (256, 128, 64, 32, 16, 8))


def _rms_fwd(t, w, groups=1, res=None, out_dtype=F32, name="rms_fwd"):
    r_, f = t.shape
    fg = f // groups
    tr = _row_tile(r_)
    has_res = res is not None

    def body(*refs):
        t_ref, w_ref = refs[0], refs[1]
        res_ref = refs[2] if has_res else None
        o_ref = refs[-1]
        for g in range(groups):
            sl = slice(g * fg, (g + 1) * fg)
            tv = t_ref[:, sl].astype(F32)
            r = lax.rsqrt(jnp.mean(tv * tv, axis=-1, keepdims=True) + EPS)
            y = tv * r * w_ref[:, sl]
            if has_res:
                y = y + res_ref[:, sl]
            o_ref[:, sl] = y.astype(o_ref.dtype)

    row = pl.BlockSpec((tr, f), lambda i: (i, 0))
    wsp = pl.BlockSpec((1, f), lambda i: (0, 0))
    return pl.pallas_call(
        body, name=name, grid=(r_ // tr,),
        in_specs=[row, wsp] + ([row] if has_res else []), out_specs=row,
        out_shape=jax.ShapeDtypeStruct((r_, f), out_dtype),
        compiler_params=_params("parallel"),
    )(*((t, w.reshape(1, f)) + ((res,) if has_res else ())))


def _rms_bwd(t, w, dys, res=None, out_dtype=F32, name="rms_bwd"):
    r_, f = t.shape
    groups = len(dys)
    fg = f // groups
    tr = _row_tile(r_)
    has_res = res is not None

    def body(*refs):
        t_ref, w_ref = refs[0], refs[1]
        dy_refs = refs[2:2 + groups]
        res_ref = refs[2 + groups] if has_res else None
        dt_ref, dw_ref = refs[-2], refs[-1]

        @pl.when(pl.program_id(0) == 0)
        def _():
            dw_ref[...] = jnp.zeros_like(dw_ref)

        for g in range(groups):
            sl = slice(g * fg, (g + 1) * fg)
            tv = t_ref[:, sl].astype(F32)
            dyv = dy_refs[g][...].astype(F32)
            r = lax.rsqrt(jnp.mean(tv * tv, axis=-1, keepdims=True) + EPS)
            gw = dyv * w_ref[:, sl]
            c = jnp.mean(gw * tv, axis=-1, keepdims=True)
            dt = r * gw - tv * (r * r * r * c)
            if has_res:
                dt = dt + res_ref[:, sl]
            dt_ref[:, sl] = dt.astype(dt_ref.dtype)
            dw_ref[:, sl] += jnp.sum(dyv * tv * r, axis=0, keepdims=True)

    row = pl.BlockSpec((tr, f), lambda i: (i, 0))
    grow = pl.BlockSpec((tr, fg), lambda i: (i, 0))
    wsp = pl.BlockSpec((1, f), lambda i: (0, 0))
    return pl.pallas_call(
        body, name=name, grid=(r_ // tr,),
        in_specs=[row, wsp] + [grow] * groups + ([row] if has_res else []), out_specs=[row, wsp],
        out_shape=[jax.ShapeDtypeStruct((r_, f), out_dtype), jax.ShapeDtypeStruct((1, f), F32)],
        compiler_params=_params("arbitrary"),
    )(*((t, w.reshape(1, f)) + tuple(dys) + ((res,) if has_res else ())))


def _norm_res_norm(t, res, w1, w2, name="post_mix_pre_ffn_norm"):
    r_, f = t.shape
    tr = _row_tile(r_)

    def body(t_ref, res_ref, w1_ref, w2_ref, h_ref, v_ref):
        tv = t_ref[...]
        h = res_ref[...] + tv * lax.rsqrt(jnp.mean(tv * tv, axis=-1, keepdims=True) + EPS) * w1_ref[...]
        h_ref[...] = h
        v_ref[...] = (h * lax.rsqrt(jnp.mean(h * h, axis=-1, keepdims=True) + EPS) * w2_ref[...]).astype(v_ref.dtype)

    row = pl.BlockSpec((tr, f), lambda i: (i, 0))
    wsp = pl.BlockSpec((1, f), lambda i: (0, 0))
    return pl.pallas_call(
        body, name=name, grid=(r_ // tr,), in_specs=[row, row, wsp, wsp], out_specs=[row, row],
        out_shape=[jax.ShapeDtypeStruct((r_, f), F32), jax.ShapeDtypeStruct((r_, f), MXU_DTYPE)],
        compiler_params=_params("parallel"),
    )(t, res, w1.reshape(1, f), w2.reshape(1, f))


INPUT_RING = 3


def _norm_res_norm_bwd(h, w2, dv, dres, t, w1, name="pre_ffn_post_mix_norm_bwd"):
    r_, f = h.shape
    tr = _row_tile(r_, streams=6)
    n = r_ // tr
    ahead = min(INPUT_RING - 1, n)

    def body(h_hbm, w2_ref, dv_hbm, dres_hbm, t_hbm, w1_ref, dh_ref, dt_ref, dw2_ref, dw1_ref,
             h_buf, dv_buf, dres_buf, t_buf, sems):
        i = pl.program_id(0)
        streams = ((h_hbm, h_buf), (dv_hbm, dv_buf), (dres_hbm, dres_buf), (t_hbm, t_buf))

        def fetch(s, step, slot):
            src, buf = streams[s]
            return pltpu.make_async_copy(src.at[pl.ds(pl.multiple_of(step * tr, tr), tr), :], buf.at[slot], sems.at[s, slot])

        @pl.when(i == 0)
        def _():
            dw2_ref[...] = jnp.zeros_like(dw2_ref)
            dw1_ref[...] = jnp.zeros_like(dw1_ref)
            for step in range(ahead):
                for s in range(len(streams)):
                    fetch(s, step, step).start()

        @pl.when(i + ahead < n)
        def _():
            for s in range(len(streams)):
                fetch(s, i + ahead, (i + ahead) % INPUT_RING).start()

        slot = i % INPUT_RING
        for s in range(len(streams)):
            fetch(s, i, slot).wait()

        def rms_bwd(tv, wv, dyv):
            r = lax.rsqrt(jnp.mean(tv * tv, axis=-1, keepdims=True) + EPS)
            gw = dyv * wv
            c = jnp.mean(gw * tv, axis=-1, keepdims=True)
            return r * gw - tv * (r * r * r * c), jnp.sum(dyv * tv * r, axis=0, keepdims=True)

        d1, g2 = rms_bwd(h_buf[slot], w2_ref[...], dv_buf[slot])
        dh = d1 + dres_buf[slot]
        dh_ref[...] = dh
        dw2_ref[...] += g2
        d2, g1 = rms_bwd(t_buf[slot], w1_ref[...], dh)
        dt_ref[...] = d2.astype(dt_ref.dtype)
        dw1_ref[...] += g1

    row = pl.BlockSpec((tr, f), lambda i: (i, 0))
    wsp = pl.BlockSpec((1, f), lambda i: (0, 0))
    hbm = pl.BlockSpec(memory_space=pl.ANY)
    ring = pltpu.VMEM((INPUT_RING, tr, f), F32)
    return pl.pallas_call(
        body, name=name, grid=(n,), in_specs=[hbm, wsp, hbm, hbm, hbm, wsp], out_specs=[row, row, wsp, wsp],
        out_shape=[jax.ShapeDtypeStruct((r_, f), F32), jax.ShapeDtypeStruct((r_, f), MXU_DTYPE),
                   jax.ShapeDtypeStruct((1, f), F32), jax.ShapeDtypeStruct((1, f), F32)],
        scratch_shapes=[ring, ring, ring, ring, pltpu.SemaphoreType.DMA((4, INPUT_RING))],
        compiler_params=_params("arbitrary"),
    )(h, w2.reshape(1, f), dv, dres, t, w1.reshape(1, f))


def _hnorm_fwd(o, w, width, name="attn_out_norm"):
    h, s_, v = o.shape
    tr = _row_tile(s_)

    def body(o_ref, w_ref, y_ref):
        ss = jnp.sum(o_ref[0] * o_ref[0], axis=-1, keepdims=True)
        for i in range(1, h):
            ss = ss + jnp.sum(o_ref[i] * o_ref[i], axis=-1, keepdims=True)
        r = lax.rsqrt(ss * (1.0 / (h * v)) + EPS)
        for i in range(h):
            sl = slice(i * v, (i + 1) * v)
            y_ref[:, sl] = (o_ref[i] * r * w_ref[:, sl]).astype(y_ref.dtype)

    return pl.pallas_call(
        body, name=name, grid=(s_ // tr,),
        in_specs=[pl.BlockSpec((h, tr, v), lambda i: (0, i, 0)), pl.BlockSpec((1, h * v), lambda i: (0, 0))],
        out_specs=pl.BlockSpec((tr, h * v), lambda i: (i, 0)),
        out_shape=jax.ShapeDtypeStruct((s_, width), MXU_DTYPE), compiler_params=_params("parallel"),
    )(o, w)


def _hnorm_bwd(o, w, dy, name="attn_out_norm_bwd"):
    h, s_, v = o.shape
    tr = _row_tile(s_)

    def body(o_ref, w_ref, dy_ref, do_ref, delta_ref, dw_ref):
        @pl.when(pl.program_id(0) == 0)
        def _():
            dw_ref[...] = jnp.zeros_like(dw_ref)

        ss = jnp.zeros((tr, 1), F32)
        cc = jnp.zeros((tr, 1), F32)
        for i in range(h):
            sl = slice(i * v, (i + 1) * v)
            ov = o_ref[i]
            ss = ss + jnp.sum(ov * ov, axis=-1, keepdims=True)
            cc = cc + jnp.sum(dy_ref[:, sl] * w_ref[:, sl] * ov, axis=-1, keepdims=True)
        r = lax.rsqrt(ss * (1.0 / (h * v)) + EPS)
        c = cc * (1.0 / (h * v))
        for i in range(h):
            sl = slice(i * v, (i + 1) * v)
            ov = o_ref[i]
            dyv = dy_ref[:, sl]
            dov = r * dyv * w_ref[:, sl] - ov * (r * r * r * c)
            do_ref[i] = dov.astype(do_ref.dtype)
            delta_ref[i] = jnp.sum(dov * ov, axis=-1, keepdims=True)
            dw_ref[:, sl] += jnp.sum(dyv * ov * r, axis=0, keepdims=True)

    blk = pl.BlockSpec((h, tr, v), lambda i: (0, i, 0))
    wsp = pl.BlockSpec((1, h * v), lambda i: (0, 0))
    return pl.pallas_call(
        body, name=name, grid=(s_ // tr,),
        in_specs=[blk, wsp, pl.BlockSpec((tr, h * v), lambda i: (i, 0))],
        out_specs=[blk, pl.BlockSpec((h, tr, 1), lambda i: (0, i, 0)), wsp],
        out_shape=[jax.ShapeDtypeStruct(o.shape, MXU_DTYPE), jax.ShapeDtypeStruct((h, s_, 1), F32),
                   jax.ShapeDtypeStruct((1, h * v), F32)],
        compiler_params=_params("arbitrary"),
    )(o, w, dy)


def _loss_head(ffn, h1, target, w, name="loss_head"):
    r_, f = ffn.shape
    tr = _row_tile(r_)

    def body(ffn_ref, h1_ref, tg_ref, w_ref, loss_ref, dy_ref, dffn_ref, dw_ref):
        @pl.when(pl.program_id(0) == 0)
        def _():
            dw_ref[...] = jnp.zeros_like(dw_ref)
            loss_ref[...] = jnp.zeros_like(loss_ref)

        tv = ffn_ref[...]
        wv = w_ref[...]
        r = lax.rsqrt(jnp.mean(tv * tv, axis=-1, keepdims=True) + EPS)
        tn = tv * r
        e = h1_ref[...] + tn * wv - tg_ref[...]
        tot = jnp.sum(jnp.sum(e * e, axis=1, keepdims=True), axis=0, keepdims=True) * (0.5 / f)
        loss_ref[...] += tot + jnp.zeros_like(loss_ref)
        dyv = e * (1.0 / f)
        dy_ref[...] = dyv
        gw = dyv * wv
        c = jnp.mean(gw * tv, axis=-1, keepdims=True)
        dffn_ref[...] = (r * gw - tv * (r * r * r * c)).astype(dffn_ref.dtype)
        dw_ref[...] += jnp.sum(dyv * tn, axis=0, keepdims=True)

    row = pl.BlockSpec((tr, f), lambda i: (i, 0))
    wsp = pl.BlockSpec((1, f), lambda i: (0, 0))
    lsp = pl.BlockSpec((1, LANE), lambda i: (0, 0))
    return pl.pallas_call(
        body, name=name, grid=(r_ // tr,),
        in_specs=[row, row, row, wsp], out_specs=[lsp, row, row, wsp],
        out_shape=[jax.ShapeDtypeStruct((1, LANE), F32), jax.ShapeDtypeStruct((r_, f), F32),
                   jax.ShapeDtypeStruct((r_, f), MXU_DTYPE), jax.ShapeDtypeStruct((1, f), F32)],
        compiler_params=_params("arbitrary"),
    )(ffn, h1, target, w.reshape(1, f))


def _rot_matrix():
    p = np.zeros((ROPE, ROPE), np.float32)
    for i in range(HALF):
        p[i + HALF, i] = -1.0
        p[i, i + HALF] = 1.0
    return jnp.asarray(p, BF16)


def _rope_val(r, c2, s2, rot):
    hi, mid, _ = _split3(r)
    return r * c2 + (_dot(hi, rot, 1, 0) + _dot(mid, rot, 1, 0)) * s2


def _q_prep(q, cos2, sin2, scale, name):
    h, s_, _ = q.shape
    tr = _pick(s_, (4096, 2048, 1024, 512, 256, 128, 64, 32, 16))

    def body(q_ref, c_ref, s_ref, rot_ref, o_ref):
        for rs in _row_slices(tr, 16):
            x = q_ref[rs, :]
            o_ref[rs, :NOPE] = (x[:, :NOPE] * scale).astype(o_ref.dtype)
            o_ref[rs, NOPE:] = (_rope_val(x[:, NOPE:], c_ref[rs, :], s_ref[rs, :], rot_ref[...]) * scale).astype(o_ref.dtype)

    blk = pl.BlockSpec((None, tr, QK), lambda hh, i: (hh, i, 0))
    csp = pl.BlockSpec((tr, ROPE), lambda hh, i: (i, 0))
    return pl.pallas_call(
        body, name=name, grid=(h, s_ // tr),
        in_specs=[blk, csp, csp, pl.BlockSpec((ROPE, ROPE), lambda hh, i: (0, 0))], out_specs=blk,
        out_shape=jax.ShapeDtypeStruct(q.shape, MXU_DTYPE), compiler_params=_params("parallel", "parallel"),
    )(q, cos2, sin2, _rot_matrix())


def _q_up(qkvn, w_uq_t, cos2, sin2, scale, name="q_up"):
    s_ = qkvn.shape[0]
    h = w_uq_t.shape[0]
    tm = _pick(s_, (4096, 2048, 1024, 512, 256, 128))

    def body(a_ref, w_ref, c_ref, s_ref, rot_ref, o_ref):
        for rs in _row_slices(tm, 16):
            x = _dot(_mx(a_ref[rs, :]), _mx(w_ref[...]), 1, 1)
            o_ref[rs, :NOPE] = (x[:, :NOPE] * scale).astype(o_ref.dtype)
            o_ref[rs, NOPE:] = (_rope_val(x[:, NOPE:], c_ref[rs, :], s_ref[rs, :], rot_ref[...]) * scale).astype(o_ref.dtype)

    csp = pl.BlockSpec((tm, ROPE), lambda j, i: (i, 0))
    return pl.pallas_call(
        body, name=name, grid=(h, s_ // tm),
        in_specs=[pl.BlockSpec((tm, Q_RANK), lambda j, i: (i, 0)), pl.BlockSpec((None, QK, Q_RANK), lambda j, i: (j, 0, 0)),
                  csp, csp, pl.BlockSpec((ROPE, ROPE), lambda j, i: (0, 0))],
        out_specs=pl.BlockSpec((None, tm, QK), lambda j, i: (j, i, 0)),
        out_shape=jax.ShapeDtypeStruct((h, s_, QK), MXU_DTYPE), compiler_params=_params("parallel", "parallel"),
    )(qkvn, w_uq_t, cos2, sin2, _rot_matrix())


def _kv_up(qkvn, w_ukv, small, cos2, sin2, name="kv_up"):
    s_ = qkvn.shape[0]
    h = w_ukv.shape[0]
    tm = _pick(s_, (4096, 2048, 1024, 512, 256, 128))

    def body(a_ref, w_ref, sm_ref, c_ref, s_ref, rot_ref, k_ref, v_ref):
        for rs in _row_slices(tm, 16):
            x = _dot(_mx(a_ref[rs, :]), _mx(w_ref[...]), 1, 0)
            k_ref[rs, :NOPE] = x[:, :NOPE].astype(k_ref.dtype)
            k_ref[rs, NOPE:] = _rope_val(sm_ref[rs, :ROPE], c_ref[rs, :], s_ref[rs, :], rot_ref[...]).astype(k_ref.dtype)
            v_ref[rs, :] = x[:, NOPE:].astype(v_ref.dtype)

    csp = pl.BlockSpec((tm, ROPE), lambda j, i: (i, 0))
    return pl.pallas_call(
        body, name=name, grid=(h, s_ // tm),
        in_specs=[pl.BlockSpec((tm, KV_RANK), lambda j, i: (i, Q_RANK // KV_RANK)),
                  pl.BlockSpec((None, KV_RANK, NOPE + VDIM), lambda j, i: (j, 0, 0)),
                  pl.BlockSpec((tm, LANE), lambda j, i: (i, 0)), csp, csp, pl.BlockSpec((ROPE, ROPE), lambda j, i: (0, 0))],
        out_specs=[pl.BlockSpec((None, tm, QK), lambda j, i: (j, i, 0)), pl.BlockSpec((None, tm, VDIM), lambda j, i: (j, i, 0))],
        out_shape=[jax.ShapeDtypeStruct((h, s_, QK), MXU_DTYPE), jax.ShapeDtypeStruct((h, s_, VDIM), MXU_DTYPE)],
        compiler_params=_params("parallel", "parallel"),
    )(qkvn, w_ukv, small, cos2, sin2, _rot_matrix())


def _dkv_post(dk, dv, ddt, cos2, nsin2, name="dkv_post"):
    h, s_, _ = dk.shape
    tr = _row_tile(s_)

    def body(dk_ref, dv_ref, ddt_ref, c_ref, s_ref, rot_ref, dkv_ref, dsm_ref):
        acc = dk_ref[0, :, NOPE:]
        for i in range(1, h):
            acc = acc + dk_ref[i, :, NOPE:]
        dsm_ref[:, :ROPE] = _rope_val(acc, c_ref[...], s_ref[...], rot_ref[...]).astype(dsm_ref.dtype)
        dsm_ref[:, ROPE:] = ddt_ref[:, ROPE:].astype(dsm_ref.dtype)
        for i in range(h):
            dkv_ref[i, :, :NOPE] = dk_ref[i, :, :NOPE].astype(dkv_ref.dtype)
            dkv_ref[i, :, NOPE:] = dv_ref[i].astype(dkv_ref.dtype)

    csp = pl.BlockSpec((tr, ROPE), lambda i: (i, 0))
    return pl.pallas_call(
        body, name=name, grid=(s_ // tr,),
        in_specs=[pl.BlockSpec((h, tr, QK), lambda i: (0, i, 0)), pl.BlockSpec((h, tr, VDIM), lambda i: (0, i, 0)),
                  pl.BlockSpec((tr, LANE), lambda i: (i, 0)), csp, csp, pl.BlockSpec((ROPE, ROPE), lambda i: (0, 0))],
        out_specs=[pl.BlockSpec((h, tr, NOPE + VDIM), lambda i: (0, i, 0)), pl.BlockSpec((tr, LANE), lambda i: (i, 0))],
        out_shape=[jax.ShapeDtypeStruct((h, s_, NOPE + VDIM), MXU_DTYPE), jax.ShapeDtypeStruct((s_, LANE), MXU_DTYPE)],
        compiler_params=_params("parallel"),
    )(dk, dv, ddt, cos2, nsin2, _rot_matrix())


def _attn_tile(s):
    return 2048 if s % 4096 == 0 else s // 2


def _pairs(n, by_key):
    if by_key:
        pr = [(i, j) for j in range(n) for i in range(j, n)]
    else:
        pr = [(i, j) for i in range(n) for j in range(i + 1)]
    return (jnp.asarray([p[0] for p in pr], jnp.int32), jnp.asarray([p[1] for p in pr], jnp.int32))


ATTN_ROW_GROUPS = 8


def _row_groups(t, diag):
    tg = t // ATTN_ROW_GROUPS
    out = []
    for r in range(ATTN_ROW_GROUPS):
        nc = (r + 1) * tg if diag else t
        mask = None
        if diag:
            mask = (lax.broadcasted_iota(jnp.int32, (tg, nc), 1)
                    <= lax.broadcasted_iota(jnp.int32, (tg, nc), 0) + r * tg)
        out.append((slice(r * tg, (r + 1) * tg), nc, mask))
    return out


def _flash_specs(t, dk, dv):
    qsp = pl.BlockSpec((None, t, dk), lambda hh, p, qi, kj: (hh, qi[p], 0))
    ksp = pl.BlockSpec((None, t, dk), lambda hh, p, qi, kj: (hh, kj[p], 0))
    vsp = pl.BlockSpec((None, t, dv), lambda hh, p, qi, kj: (hh, kj[p], 0))
    osp = pl.BlockSpec((None, t, dv), lambda hh, p, qi, kj: (hh, qi[p], 0))
    lsp = pl.BlockSpec((None, t, 1), lambda hh, p, qi, kj: (hh, qi[p], 0))
    return qsp, ksp, vsp, osp, lsp


def _flash_fwd(q, k, v, name="flash_fwd"):
    h, s_, dk = q.shape
    dv = v.shape[-1]
    t = _attn_tile(s_)
    n = s_ // t
    qi, kj = _pairs(n, False)

    def body(qi_ref, kj_ref, q_ref, k_ref, v_ref, o_ref, lse_ref, m_s, l_s, acc):
        p_ = pl.program_id(1)
        i, j = qi_ref[p_], kj_ref[p_]

        @pl.when(j == 0)
        def _():
            m_s[...] = jnp.full_like(m_s, -jnp.inf)
            l_s[...] = jnp.zeros_like(l_s)
            acc[...] = jnp.zeros_like(acc)

        def update(diag):
            for rs, nc, mask in _row_groups(t, diag):
                sc = _dot(q_ref[rs, :], k_ref[0:nc, :], 1, 1)
                if mask is not None:
                    sc = jnp.where(mask, sc, -jnp.inf)
                m_old = m_s[rs, :]
                m_new = jnp.maximum(m_old, jnp.max(sc, axis=1, keepdims=True))
                alpha = jnp.exp(m_old - m_new)
                p = jnp.exp(sc - m_new)
                l_s[rs, :] = alpha * l_s[rs, :] + jnp.sum(p, axis=1, keepdims=True)
                acc[rs, :] = alpha * acc[rs, :] + _dot(_mx(p), v_ref[0:nc, :], 1, 0)
                m_s[rs, :] = m_new

        @pl.when(j < i)
        def _():
            update(False)

        @pl.when(j == i)
        def _():
            update(True)
            o_ref[...] = acc[...] / l_s[...]
            lse_ref[...] = m_s[...] + jnp.log(l_s[...])

    qsp, ksp, vsp, osp, lsp = _flash_specs(t, dk, dv)
    gs = pltpu.PrefetchScalarGridSpec(
        num_scalar_prefetch=2, grid=(h, qi.shape[0]), in_specs=[qsp, ksp, vsp], out_specs=[osp, lsp],
        scratch_shapes=[pltpu.VMEM((t, 1), F32), pltpu.VMEM((t, 1), F32), pltpu.VMEM((t, dv), F32)])
    return pl.pallas_call(
        body, name=name, grid_spec=gs,
        out_shape=[jax.ShapeDtypeStruct((h, s_, dv), F32), jax.ShapeDtypeStruct((h, s_, 1), F32)],
        compiler_params=_params("parallel", "arbitrary"),
    )(qi, kj, q, k, v)


def _flash_bwd(q, k, v, do, lse, delta, name="flash_bwd"):
    h, s_, dk = q.shape
    dv = v.shape[-1]
    t = _attn_tile(s_)
    tg = t // ATTN_ROW_GROUPS
    n = s_ // t
    qi, kj = _pairs(n, True)

    def body(qi_ref, kj_ref, q_ref, k_ref, v_ref, do_ref, lse_ref, delta_ref, dq_ref, dk_ref, dv_ref, dk_acc, dv_acc):
        p_ = pl.program_id(1)
        i, j = qi_ref[p_], kj_ref[p_]

        @pl.when(p_ == 0)
        def _():
            dq_ref[...] = jnp.zeros_like(dq_ref)

        def update(diag):
            for g, (rs, nc, mask) in enumerate(_row_groups(t, diag)):
                sc = _dot(q_ref[rs, :], k_ref[0:nc, :], 1, 1)
                if mask is not None:
                    sc = jnp.where(mask, sc, -jnp.inf)
                p = jnp.exp(sc - lse_ref[rs, :])
                dob = _mx(do_ref[rs, :])
                dv_acc[0:nc, :] += _dot(_mx(p), dob, 0, 0)
                dp = _dot(dob, v_ref[0:nc, :], 1, 1)
                dsb = _mx(p * (dp - delta_ref[rs, :]))
                dk_acc[0:nc, :] += _dot(dsb, q_ref[rs, :], 0, 0)
                rows = pl.ds(pl.multiple_of(i * t + g * tg, tg), tg)
                dq_ref[rows, :] += _dot(dsb, k_ref[0:nc, :], 1, 0)

        @pl.when(i == j)
        def _():
            dk_acc[...] = jnp.zeros_like(dk_acc)
            dv_acc[...] = jnp.zeros_like(dv_acc)
            update(True)

        @pl.when(i > j)
        def _():
            update(False)

        @pl.when(i == n - 1)
        def _():
            dk_ref[...] = dk_acc[...]
            dv_ref[...] = dv_acc[...]

    qsp, ksp, vsp, osp, lsp = _flash_specs(t, dk, dv)
    dqsp = pl.BlockSpec((None, s_, dk), lambda hh, p, qi, kj: (hh, 0, 0))
    gs = pltpu.PrefetchScalarGridSpec(
        num_scalar_prefetch=2, grid=(h, qi.shape[0]), in_specs=[qsp, ksp, vsp, osp, lsp, lsp],
        out_specs=[dqsp, ksp, vsp],
        scratch_shapes=[pltpu.VMEM((t, dk), F32), pltpu.VMEM((t, dv), F32)])
    return pl.pallas_call(
        body, name=name, grid_spec=gs,
        out_shape=[jax.ShapeDtypeStruct((h, s_, dk), F32), jax.ShapeDtypeStruct((h, s_, dk), F32),
                   jax.ShapeDtypeStruct((h, s_, dv), F32)],
        compiler_params=_params("parallel", "arbitrary"),
    )(qi, kj, q, k, v, do, lse, delta)


HALO = 8


def _conv_specs(s_, c, tr, after):
    main = pl.BlockSpec((tr, c), lambda i: (i, 0))
    per = tr // HALO
    if after:
        halo = pl.BlockSpec((HALO, c), lambda i: (jnp.minimum((i + 1) * per, s_ // HALO - 1), 0))
    else:
        halo = pl.BlockSpec((HALO, c), lambda i: (jnp.maximum(i * per - 1, 0), 0))
    return main, halo


def _fill_before(ext, t_ref, h_ref, tr):
    ext[0:HALO, :] = jnp.where(pl.program_id(0) > 0, h_ref[...], 0.0)
    ext[HALO:HALO + tr, :] = t_ref[...]


def _taps(ext, w_ref, tr):
    base = HALO - (CONV_K - 1)
    acc = ext[base:base + tr, :] * w_ref[0:1, :]
    for k in range(1, CONV_K):
        acc = acc + ext[base + k:base + k + tr, :] * w_ref[k:k + 1, :]
    return acc


def _conv_fwd(t, w, b, name="conv_fwd"):
    s_, c = t.shape
    tr = _row_tile(s_)

    def body(t_ref, h_ref, w_ref, b_ref, o_ref, ext):
        _fill_before(ext, t_ref, h_ref, tr)
        o_ref[...] = _silu(_taps(ext, w_ref, tr) + b_ref[...])

    main, halo = _conv_specs(s_, c, tr, False)
    return pl.pallas_call(
        body, name=name, grid=(s_ // tr,),
        in_specs=[main, halo, pl.BlockSpec((CONV_K, c), lambda i: (0, 0)), pl.BlockSpec((1, c), lambda i: (0, 0))],
        out_specs=main, out_shape=jax.ShapeDtypeStruct((s_, c), F32),
        scratch_shapes=[pltpu.VMEM((tr + HALO, c), F32)], compiler_params=_params("parallel"),
    )(t, t, w, b)


def _conv_bwd_pre(t, w, b, dact, name="conv_bwd_pre"):
    s_, c = t.shape
    tr = _row_tile(s_)

    def body(t_ref, h_ref, w_ref, b_ref, da_ref, dpre_ref, dwb_ref, ext):
        @pl.when(pl.program_id(0) == 0)
        def _():
            dwb_ref[...] = jnp.zeros_like(dwb_ref)

        _fill_before(ext, t_ref, h_ref, tr)
        dpre = da_ref[...] * _dsilu(_taps(ext, w_ref, tr) + b_ref[...])
        dpre_ref[...] = dpre
        base = HALO - (CONV_K - 1)
        for k in range(CONV_K):
            dwb_ref[k:k + 1, :] += jnp.sum(dpre * ext[base + k:base + k + tr, :], axis=0, keepdims=True)
        dwb_ref[CONV_K:CONV_K + 1, :] += jnp.sum(dpre, axis=0, keepdims=True)

    main, halo = _conv_specs(s_, c, tr, False)
    return pl.pallas_call(
        body, name=name, grid=(s_ // tr,),
        in_specs=[main, halo, pl.BlockSpec((CONV_K, c), lambda i: (0, 0)), pl.BlockSpec((1, c), lambda i: (0, 0)), main],
        out_specs=[main, pl.BlockSpec((8, c), lambda i: (0, 0))],
        out_shape=[jax.ShapeDtypeStruct((s_, c), F32), jax.ShapeDtypeStruct((8, c), F32)],
        scratch_shapes=[pltpu.VMEM((tr + HALO, c), F32)], compiler_params=_params("arbitrary"),
    )(t, t, w, b, dact)


def _conv_bwd_in(dpre, w, name="conv_bwd_in"):
    s_, c = dpre.shape
    tr = _row_tile(s_)
    nt = s_ // tr

    def body(d_ref, h_ref, w_ref, o_ref, ext):
        ext[0:tr, :] = d_ref[...]
        ext[tr:tr + HALO, :] = jnp.where(pl.program_id(0) < nt - 1, h_ref[...], 0.0)
        acc = ext[CONV_K - 1:CONV_K - 1 + tr, :] * w_ref[0:1, :]
        for k in range(1, CONV_K):
            acc = acc + ext[CONV_K - 1 - k:CONV_K - 1 - k + tr, :] * w_ref[k:k + 1, :]
        o_ref[...] = acc.astype(o_ref.dtype)

    main, halo = _conv_specs(s_, c, tr, True)
    return pl.pallas_call(
        body, name=name, grid=(nt,),
        in_specs=[main, halo, pl.BlockSpec((CONV_K, c), lambda i: (0, 0))],
        out_specs=main, out_shape=jax.ShapeDtypeStruct((s_, c), MXU_DTYPE),
        scratch_shapes=[pltpu.VMEM((tr + HALO, c), F32)], compiler_params=_params("parallel"),
    )(dpre, dpre, w)


def _ssd_chunk_common(dt_ref, dtt_ref, br_ref, bc_ref, ar_ref, ac_ref):
    li = lax.broadcasted_iota(jnp.int32, (CHUNK, CHUNK), 0)
    si = lax.broadcasted_iota(jnp.int32, (CHUNK, CHUNK), 1)
    lower = li >= si
    lower_b = lower.astype(BF16)
    upper_b = (li <= si).astype(BF16)
    zr = dt_ref[...] + br_ref[...]
    dtc = _softplus(zr)
    a_row = -jnp.exp(ar_ref[...])
    acum = _exact_dot(lower_b, dtc * a_row, 1, 0, False)
    dtt = _softplus(dtt_ref[...] + bc_ref[...])
    acum_t = _exact_dot(dtt * (-jnp.exp(ac_ref[...])), upper_b, 1, 0, True)
    return lower, upper_b, zr, dtc, a_row, acum, acum_t


def _head_terms(h, lower, dtc, acum, acum_t):
    lane = lax.broadcasted_iota(jnp.int32, (1, LANE), 1)
    sub = lax.broadcasted_iota(jnp.int32, (SSD_H, 1), 0)
    rowid = lax.broadcasted_iota(jnp.int32, (CHUNK, 1), 0)
    oh = (lane == HEAD_LANE + h).astype(F32)
    acol = jnp.sum(acum * oh, axis=1, keepdims=True)
    dcol = jnp.sum(dtc * oh, axis=1, keepdims=True)
    arow = jnp.sum(acum_t * (sub == h).astype(F32), axis=0, keepdims=True)
    alast = jnp.sum(jnp.where(rowid == CHUNK - 1, acol, 0.0), axis=0, keepdims=True)
    decay = jnp.exp(jnp.where(lower, acol - arow, -jnp.inf))
    return oh, acol, dcol, alast, decay


SSD_PAIRS = SSD_H // 2
PAIRS_PER_GROUP = SSD_E // 2


def _ps(q):
    return slice(q * LANE, (q + 1) * LANE)


def _gs(off, g):
    return slice(off + g * SSD_N, off + (g + 1) * SSD_N)


def _lanes(c0, c1):
    return jnp.where(lax.broadcasted_iota(jnp.int32, (1, LANE), 1) < SSD_P, c0, c1)


def _rows(c0, c1):
    return jnp.where(lax.broadcasted_iota(jnp.int32, (LANE, 1), 0) < SSD_P, c0, c1)


def _lane_halves(t):
    first = lax.broadcasted_iota(jnp.int32, (1, LANE), 1) < SSD_P
    return (jnp.sum(jnp.where(first, t, 0.0), axis=1, keepdims=True),
            jnp.sum(jnp.where(first, 0.0, t), axis=1, keepdims=True))


def _ssd_in_specs(rev):
    def ci(c):
        return c if rev is None else rev - c
    return [pl.BlockSpec((CHUNK, CONV_DIM), lambda c: (ci(c), 0)),
            pl.BlockSpec((CHUNK, LANE), lambda c: (ci(c), 0)),
            pl.BlockSpec((SSD_H, CHUNK), lambda c: (0, ci(c))),
            pl.BlockSpec((1, LANE), lambda c: (0, 0)), pl.BlockSpec((SSD_H, 1), lambda c: (0, 0)),
            pl.BlockSpec((1, LANE), lambda c: (0, 0)), pl.BlockSpec((SSD_H, 1), lambda c: (0, 0)),
            pl.BlockSpec((SSD_PAIRS, 1, LANE), lambda c: (0, 0, 0))]


def _ssd_fwd(xbc, small, dtt, bias_r, bias_c, alog_r, alog_c, dsk, name="ssd_fwd"):
    s_ = xbc.shape[0]
    nc = s_ // CHUNK

    def body(x_ref, dt_ref, dtt_ref, br_ref, bc_ref, ar_ref, ac_ref, dsk_ref, y_ref, prev_ref, state):
        @pl.when(pl.program_id(0) == 0)
        def _():
            state[...] = jnp.zeros_like(state)

        lower, _, _, dtc, _, acum, acum_t = _ssd_chunk_common(dt_ref, dtt_ref, br_ref, bc_ref, ar_ref, ac_ref)
        for g in range(SSD_G):
            bb = _mx(x_ref[:, _gs(B_OFF, g)])
            cb_ = _mx(x_ref[:, _gs(C_OFF, g)])
            cbm = _dot(cb_, bb, 1, 1)
            for e in range(PAIRS_PER_GROUP):
                q = g * PAIRS_PER_GROUP + e
                _, acol0, dcol0, alast0, decay0 = _head_terms(2 * q, lower, dtc, acum, acum_t)
                _, acol1, dcol1, alast1, decay1 = _head_terms(2 * q + 1, lower, dtc, acum, acum_t)
                x = x_ref[:, _ps(q)]
                xdt = x * _lanes(dcol0, dcol1)
                xb = _mx(xdt)
                yd = _lanes(_dot(_mx(cbm * decay0), xb, 1, 0), _dot(_mx(cbm * decay1), xb, 1, 0))
                prev = state[q]
                prev_ref[0, q] = prev
                yo = _dot(cb_, _mx(prev), 1, 1) * _lanes(jnp.exp(acol0), jnp.exp(acol1))
                ds = _lanes(jnp.exp(alast0 - acol0), jnp.exp(alast1 - acol1))
                st = _dot(_mx(xdt * ds), bb, 0, 0)
                state[q] = prev * _rows(jnp.exp(alast0), jnp.exp(alast1)) + st
                y_ref[:, _ps(q)] = yd + yo + x * dsk_ref[q]

    psp = pl.BlockSpec((1, SSD_PAIRS, LANE, SSD_N), lambda c: (c, 0, 0, 0))
    return pl.pallas_call(
        body, name=name, grid=(nc,),
        in_specs=_ssd_in_specs(None), out_specs=[pl.BlockSpec((CHUNK, SSD_W), lambda c: (c, 0)), psp],
        out_shape=[jax.ShapeDtypeStruct((s_, SSD_W), F32),
                   jax.ShapeDtypeStruct((nc, SSD_PAIRS, LANE, SSD_N), F32)],
        scratch_shapes=[pltpu.VMEM((SSD_PAIRS, LANE, SSD_N), F32)],
        compiler_params=_params("arbitrary"),
    )(xbc, small, dtt, bias_r, bias_c, alog_r, alog_c, dsk)


def _ssd_bwd(xbc, small, dtt, bias_r, bias_c, alog_r, alog_c, dsk, prev, dy, name="ssd_bwd"):
    s_ = xbc.shape[0]
    nc = s_ // CHUNK

    def body(x_ref, dt_ref, dtt_ref, br_ref, bc_ref, ar_ref, ac_ref, dsk_ref, prev_ref, dy_ref,
             dx_ref, ddt_ref, dpar_ref, dstate):
        @pl.when(pl.program_id(0) == 0)
        def _():
            dstate[...] = jnp.zeros_like(dstate)
            dpar_ref[...] = jnp.zeros_like(dpar_ref)

        lower, upper_b, zr, dtc, a_row, acum, acum_t = _ssd_chunk_common(
            dt_ref, dtt_ref, br_ref, bc_ref, ar_ref, ac_ref)
        strict = (lax.broadcasted_iota(jnp.int32, (CHUNK, CHUNK), 1)
                  < lax.broadcasted_iota(jnp.int32, (CHUNK, CHUNK), 0))
        strict_b = strict.astype(BF16)
        col2 = lax.broadcasted_iota(jnp.int32, (CHUNK, 2 * CHUNK), 1)
        strict2 = (jnp.where(col2 >= CHUNK, col2 - CHUNK, col2)
                   < lax.broadcasted_iota(jnp.int32, (CHUNK, 2 * CHUNK), 0))
        da_in = jnp.zeros((CHUNK, LANE), F32)
        r_off = jnp.zeros((CHUNK, LANE), F32)
        c_int = jnp.zeros((CHUNK, LANE), F32)
        c_row = jnp.zeros((1, LANE), F32)
        ddt = jnp.zeros((CHUNK, LANE), F32)
        dskip = jnp.zeros((1, LANE), F32)
        for g in range(SSD_G):
            bb = _mx(x_ref[:, _gs(B_OFF, g)])
            cb_ = _mx(x_ref[:, _gs(C_OFF, g)])
            cbm = _dot(cb_, bb, 1, 1)
            dcb = jnp.zeros((CHUNK, CHUNK), F32)
            dc_acc = jnp.zeros((CHUNK, SSD_N), F32)
            db_acc = jnp.zeros((CHUNK, SSD_N), F32)
            for e in range(PAIRS_PER_GROUP):
                q = g * PAIRS_PER_GROUP + e
                oh0, acol0, dcol0, alast0, decay0 = _head_terms(2 * q, lower, dtc, acum, acum_t)
                oh1, acol1, dcol1, alast1, decay1 = _head_terms(2 * q + 1, lower, dtc, acum, acum_t)
                x = x_ref[:, _ps(q)]
                dy = dy_ref[:, _ps(q)]
                dcol = _lanes(dcol0, dcol1)
                xdt = x * dcol
                xb = _mx(xdt)
                eacol = _lanes(jnp.exp(acol0), jnp.exp(acol1))
                ds = _lanes(jnp.exp(alast0 - acol0), jnp.exp(alast1 - acol1))
                ealast = _rows(jnp.exp(alast0), jnp.exp(alast1))
                dyb = _mx(dy)
                dyb0, dyb1 = _mx(_lanes(dy, 0.0)), _mx(_lanes(0.0, dy))
                dsh = dstate[q]
                dshb = _mx(dsh)
                prev = prev_ref[0, q]
                prevb = _mx(prev)
                dxdt_inter = ds * _dot(bb, dshb, 1, 1)
                dxdt = _lanes(_dot(_mx(cbm * decay0), dyb, 0, 0), _dot(_mx(cbm * decay1), dyb, 0, 0)) + dxdt_inter
                dwl0 = _dot(dyb0, xb, 1, 1) * decay0
                dwl1 = _dot(dyb1, xb, 1, 1) * decay1
                dcb = dcb + dwl0 + dwl1
                dyeb = _mx(dy * eacol)
                dc_acc = dc_acc + _dot(dyeb, prevb, 1, 0)
                db_acc = db_acc + _dot(_mx(xdt * ds), dshb, 1, 0)
                dstate[q] = _dot(dyeb, cb_, 0, 0) + ealast * dsh
                above = _exact_dot(upper_b, jnp.concatenate([dwl0 * cbm, dwl1 * cbm], axis=1), 1, 0, False)
                above = jnp.where(strict2, above, 0.0)
                da_in = (da_in + jnp.sum(above[:, :CHUNK], axis=1, keepdims=True) * oh0
                         + jnp.sum(above[:, CHUNK:], axis=1, keepdims=True) * oh1)
                y_off = _dot(cb_, prevb, 1, 1) * eacol
                r0, r1 = _lane_halves(dy * y_off)
                r_off = r_off + r0 * oh0 + r1 * oh1
                c0, c1 = _lane_halves(xdt * dxdt_inter)
                c_int = c_int + c0 * oh0 + c1 * oh1
                both = jnp.sum(dsh * prev, axis=1, keepdims=True) * ealast
                c_row = (c_row + jnp.sum(_rows(both, 0.0), axis=0, keepdims=True) * oh0
                         + jnp.sum(_rows(0.0, both), axis=0, keepdims=True) * oh1)
                t0, t1 = _lane_halves(dxdt * x)
                ddt = ddt + t0 * oh0 + t1 * oh1
                dx_ref[:, _ps(q)] = dxdt * dcol + dy * dsk_ref[q]
                k0, k1 = _lane_halves(dy * x)
                dskip = (dskip + jnp.sum(k0, axis=0, keepdims=True) * oh0 + jnp.sum(k1, axis=0, keepdims=True) * oh1)
            dcbb = _mx(dcb)
            dx_ref[:, _gs(C_OFF, g)] = dc_acc + _dot(dcbb, bb, 1, 0)
            dx_ref[:, _gs(B_OFF, g)] = db_acc + _dot(dcbb, cb_, 0, 0)
        da = (da_in + _exact_dot(upper_b, r_off, 1, 0, False) + _exact_dot(strict_b, c_int, 1, 0, False) + c_row)
        draw = (ddt + da * a_row) * _sigmoid(zr)
        ddt_ref[...] = draw
        dpar_ref[0:1, :] += jnp.sum(draw, axis=0, keepdims=True)
        dpar_ref[1:2, :] += jnp.sum(da * dtc, axis=0, keepdims=True) * a_row
        dpar_ref[2:3, :] += dskip

    rev = nc - 1
    psp = pl.BlockSpec((1, SSD_PAIRS, LANE, SSD_N), lambda c: (rev - c, 0, 0, 0))
    return pl.pallas_call(
        body, name=name, grid=(nc,),
        in_specs=_ssd_in_specs(rev) + [psp, pl.BlockSpec((CHUNK, SSD_W), lambda c: (rev - c, 0))],
        out_specs=[pl.BlockSpec((CHUNK, CONV_DIM), lambda c: (rev - c, 0)),
                   pl.BlockSpec((CHUNK, LANE), lambda c: (rev - c, 0)), pl.BlockSpec((8, LANE), lambda c: (0, 0))],
        out_shape=[jax.ShapeDtypeStruct((s_, CONV_DIM), F32), jax.ShapeDtypeStruct((s_, LANE), F32),
                   jax.ShapeDtypeStruct((8, LANE), F32)],
        scratch_shapes=[pltpu.VMEM((SSD_PAIRS, LANE, SSD_N), F32)],
        compiler_params=_params("arbitrary"),
    )(xbc, small, dtt, bias_r, bias_c, alog_r, alog_c, dsk, prev, dy)


GN = SSD_W // SSD_G


def _gated_norm_fwd(y, z, w, cat, name="gated_norm_fwd"):
    s_, f = y.shape
    tr = _row_tile(s_)

    def body(y_ref, z_ref, w_ref, cat_ref, o_ref):
        for g in range(SSD_G):
            sl = slice(g * GN, (g + 1) * GN)
            gg = y_ref[:, sl] * _silu(z_ref[:, sl])
            r = lax.rsqrt(jnp.mean(gg * gg, axis=-1, keepdims=True) + EPS)
            o_ref[:, sl] = (gg * r * w_ref[:, sl]).astype(o_ref.dtype)

    row = pl.BlockSpec((tr, f), lambda i: (i, 0))
    wsp = pl.BlockSpec((1, f), lambda i: (0, 0))
    return pl.pallas_call(
        body, name=name, grid=(s_ // tr,),
        in_specs=[row, row, wsp, pl.BlockSpec(memory_space=pl.ANY)], out_specs=pl.BlockSpec((tr, f), lambda i: (i, 1)),
        out_shape=jax.ShapeDtypeStruct(cat.shape, cat.dtype), input_output_aliases={3: 0},
        compiler_params=_params("parallel"),
    )(y, z, w.reshape(1, f), cat)


def _gated_norm_bwd(y, z, w, dout, name="gated_norm_bwd"):
    s_, f = y.shape
    tr = _row_tile(s_)

    def body(y_ref, z_ref, w_ref, do_ref, dy_ref, dz_ref, dw_ref):
        @pl.when(pl.program_id(0) == 0)
        def _():
            dw_ref[...] = jnp.zeros_like(dw_ref)

        for g in range(SSD_G):
            sl = slice(g * GN, (g + 1) * GN)
            yv = y_ref[:, sl]
            zv = z_ref[:, sl]
            dov = do_ref[:, sl].astype(F32)
            sz = _silu(zv)
            gg = yv * sz
            r = lax.rsqrt(jnp.mean(gg * gg, axis=-1, keepdims=True) + EPS)
            gw = dov * w_ref[:, sl]
            c = jnp.mean(gw * gg, axis=-1, keepdims=True)
            dgg = r * gw - gg * (r * r * r * c)
            dy_ref[:, sl] = dgg * sz
            dz_ref[:, sl] = (dgg * yv * _dsilu(zv)).astype(dz_ref.dtype)
            dw_ref[:, sl] += jnp.sum(dov * gg * r, axis=0, keepdims=True)

    row = pl.BlockSpec((tr, f), lambda i: (i, 0))
    wsp = pl.BlockSpec((1, f), lambda i: (0, 0))
    return pl.pallas_call(
        body, name=name, grid=(s_ // tr,),
        in_specs=[row, row, wsp, pl.BlockSpec((tr, f), lambda i: (i, 1))], out_specs=[row, row, wsp],
        out_shape=[jax.ShapeDtypeStruct((s_, f), F32), jax.ShapeDtypeStruct((s_, f), MXU_DTYPE),
                   jax.ShapeDtypeStruct((1, f), F32)],
        compiler_params=_params("arbitrary"),
    )(y, z, w.reshape(1, f), dout)


def _ffn_fwd(vv, w_gate, w_up, name="ffn_gate_up"):
    s_, d = vv.shape
    nb, f8, _ = w_gate.shape
    tm = _pick(s_, (2048, 1024, 512, 256, 128))

    def body(v_ref, wg_ref, wu_ref, g_ref, u_ref, a_ref):
        for rs in _row_slices(tm, 16):
            a = _mx(v_ref[rs, :])
            g = _dot(a, _mx(wg_ref[...]), 1, 1)
            u = _dot(a, _mx(wu_ref[...]), 1, 1)
            s = _sigmoid(g)
            gs = g * s
            g_ref[rs, :] = (u * (s * (1.0 + g * (1.0 - s)))).astype(g_ref.dtype)
            u_ref[rs, :] = gs.astype(u_ref.dtype)
            a_ref[rs, :] = (gs * u).astype(a_ref.dtype)

    wsp = pl.BlockSpec((None, f8, d), lambda j, i: (j, 0, 0))
    osp = pl.BlockSpec((None, tm, f8), lambda j, i: (j, i, 0))
    return pl.pallas_call(
        body, name=name, grid=(nb, s_ // tm),
        in_specs=[pl.BlockSpec((tm, d), lambda j, i: (i, 0)), wsp, wsp], out_specs=[osp] * 3,
        out_shape=[jax.ShapeDtypeStruct((nb, s_, f8), MXU_DTYPE)] * 3,
        compiler_params=_params("parallel", "parallel", vmem=VMEM_LIMIT_WIDE_BYTES),
    )(vv, w_gate, w_up)


def _ffn_bwd_act(dffn, w_down, gate, up, name="ffn_d_act"):
    s_, d = dffn.shape
    nb, f8, _ = w_down.shape
    tm = _pick(s_, (2048, 1024, 512, 256, 128))

    def body(d_ref, w_ref, g_ref, u_ref, dg_ref, du_ref):
        for rs in _row_slices(tm, 16):
            dact = _dot(_mx(d_ref[rs, :]), _mx(w_ref[...]), 1, 1)
            dg_ref[rs, :] = (dact * g_ref[rs, :].astype(F32)).astype(dg_ref.dtype)
            du_ref[rs, :] = (dact * u_ref[rs, :].astype(F32)).astype(du_ref.dtype)

    osp = pl.BlockSpec((None, tm, f8), lambda i, j: (j, i, 0))
    return pl.pallas_call(
        body, name=name, grid=(s_ // tm, nb),
        in_specs=[pl.BlockSpec((tm, d), lambda i, j: (i, 0)), pl.BlockSpec((None, f8, d), lambda i, j: (j, 0, 0)),
                  osp, osp],
        out_specs=[osp, osp], out_shape=[jax.ShapeDtypeStruct((nb, s_, f8), MXU_DTYPE)] * 2,
        compiler_params=_params("parallel", "parallel", vmem=VMEM_LIMIT_WIDE_BYTES),
    )(dffn, w_down, gate, up)


def _ffn_bwd_in(dgate, w_gate, dup, w_up, name="ffn_d_in"):
    nb, s_, f8 = dgate.shape
    d = w_gate.shape[2]
    tm = _pick(s_, (1024, 512, 256, 128))
    tn = _pick(d, (1024, 512, 256, 128))
    per = 2
    steps = nb // per

    def body(*refs):
        ins, o_ref, acc = refs[:4 * per], refs[4 * per], refs[4 * per + 1]
        j = pl.program_id(2)

        @pl.when(j == 0)
        def _():
            acc[...] = jnp.zeros_like(acc)

        for rs in _row_slices(tm, 16):
            part = None
            for t in range(per):
                dg_ref, wg_ref, du_ref, wu_ref = ins[4 * t:4 * t + 4]
                d_ = (_dot(_mx(dg_ref[rs, :]), _mx(wg_ref[...]), 1, 0)
                      + _dot(_mx(du_ref[rs, :]), _mx(wu_ref[...]), 1, 0))
                part = d_ if part is None else part + d_
            acc[rs, :] += part

        @pl.when(j == steps - 1)
        def _():
            o_ref[...] = acc[...]

    def specs(t):
        asp = pl.BlockSpec((None, tm, f8), lambda i, n, j: (j * per + t, i, 0))
        wsp = pl.BlockSpec((None, f8, tn), lambda i, n, j: (j * per + t, 0, n))
        return [asp, wsp, asp, wsp]

    return pl.pallas_call(
        body, name=name, grid=(s_ // tm, d // tn, steps),
        in_specs=[sp for t in range(per) for sp in specs(t)],
        out_specs=pl.BlockSpec((tm, tn), lambda i, n, j: (i, n)),
        out_shape=jax.ShapeDtypeStruct((s_, d), F32), scratch_shapes=[pltpu.VMEM((tm, tn), F32)],
        compiler_params=_params("parallel", "parallel", "arbitrary"),
    )(*((dgate, w_gate, dup, w_up) * per))


def _adam_math(g, w, m, v):
    m2 = ADAM_B1 * m + (1.0 - ADAM_B1) * g
    v2 = ADAM_B2 * v + (1.0 - ADAM_B2) * (g * g)
    m_hat = m2 / (1.0 - ADAM_B1 ** ADAM_STEP)
    v_hat = v2 / (1.0 - ADAM_B2 ** ADAM_STEP)
    delta = -ADAM_LR * (m_hat / (jnp.sqrt(v_hat) + ADAM_EPS) + ADAM_WD * w)
    return delta, m2, v2


def _adamw(parts, own, me, w, m, v, name="adamw"):
    nd, r_, c = parts.shape
    tr = _pick(r_, (128, 64, 32, 16))
    tc = c
    if tr == r_ and r_ > 128:
        tc = _pick(c, (256, 128))

    def body(me_ref, p_ref, own_ref, w_ref, m_ref, v_ref, g_ref, d_ref, m2_ref, v2_ref):
        mine = me_ref[0]
        g = jnp.zeros((tr, tc), F32)
        for i in range(nd):
            g = g + jnp.where(mine == i, own_ref[...], p_ref[i]).astype(F32)
        delta, m2, v2 = _adam_math(g, w_ref[...], m_ref[...], v_ref[...])
        g_ref[...] = g
        d_ref[...] = delta
        m2_ref[...] = m2
        v2_ref[...] = v2

    row = pl.BlockSpec((tr, tc), lambda i, j, me_: (i, j))
    gs = pltpu.PrefetchScalarGridSpec(
        num_scalar_prefetch=1, grid=(r_ // tr, c // tc),
        in_specs=[pl.BlockSpec((nd, tr, tc), lambda i, j, me_: (0, i, j)),
                  pl.BlockSpec((None, tr, tc), lambda i, j, me_: (me_[0], i, j)), row, row, row],
        out_specs=[row] * 4)
    return pl.pallas_call(
        body, name=name, grid_spec=gs, out_shape=[jax.ShapeDtypeStruct((r_, c), F32)] * 4,
        compiler_params=_params("parallel", "parallel"),
    )(me, parts, own, w, m, v)


def _adamw_small(parts, w, m, v, name="adamw_small"):
    nd = parts.shape[0]

    def body(p_ref, w_ref, m_ref, v_ref, g_ref, d_ref, m2_ref, v2_ref):
        g = p_ref[0]
        for i in range(1, nd):
            g = g + p_ref[i]
        delta, m2, v2 = _adam_math(g, w_ref[...], m_ref[...], v_ref[...])
        g_ref[...] = g
        d_ref[...] = delta
        m2_ref[...] = m2
        v2_ref[...] = v2

    return pl.pallas_call(
        body, name=name, out_shape=[jax.ShapeDtypeStruct(w.shape, F32)] * 4,
        compiler_params=pltpu.CompilerParams(vmem_limit_bytes=VMEM_LIMIT_BYTES),
    )(parts, w, m, v)


_HBM = pl.BlockSpec(memory_space=pltpu.HBM)
_MESH = pl.DeviceIdType.MESH


def _all_gather(xs, name):
    na = len(xs)

    def body(*refs):
        x_refs, out_refs = refs[:na], refs[na:2 * na]
        send_sems, recv_sems, local_sems = refs[2 * na:]
        x, y, c = lax.axis_index("x"), lax.axis_index("y"), lax.axis_index("c")
        me, sibling = (x, y, c), (x, y, 1 - c)
        near = [(1 - x, y), (x, 1 - y)]
        chips = near + [(1 - x, 1 - y)]
        relay_from = (x + c * (1 - 2 * x), y + (1 - c) * (1 - 2 * y))
        relay_to = (x + (1 - c) * (1 - 2 * x), y + c * (1 - 2 * y))

        def slot(a, px, py, pc):
            return out_refs[a].at[4 * px + 2 * py + pc]

        def copy(a, k, block, to, src=None):
            return pltpu.make_async_remote_copy(
                src_ref=slot(a, *block) if src is None else src, dst_ref=slot(a, *block),
                send_sem=send_sems.at[a, k], recv_sem=recv_sems.at[a, k], device_id=to, device_id_type=_MESH)

        mine = [pltpu.make_async_copy(x_refs[a], slot(a, *me), local_sems.at[a]) for a in range(na)]
        started = []
        for a in range(na):
            mine[a].start()
            first = [copy(a, 0, me, sibling, src=x_refs[a])]
            first += [copy(a, 1 + j, me, (*chip, c), src=x_refs[a]) for j, chip in enumerate(near)]
            for cp in first:
                cp.start()
            started += first
        for a in range(na):
            for j, chip in enumerate(chips):
                copy(a, 1 + j, (*chip, c), me).wait_recv()
                fwd = copy(a, 4 + j, (*chip, c), sibling)
                fwd.start()
                started.append(fwd)
                if j == len(near) - 1:
                    relay = copy(a, 1 + len(near), (*relay_from, c), (*relay_to, c))
                    relay.start()
                    started.append(relay)
        for a in range(na):
            copy(a, 0, sibling, me).wait_recv()
            for j, chip in enumerate(chips):
                copy(a, 4 + j, (*chip, 1 - c), me).wait_recv()
        for cp in started:
            cp.wait_send()
        for cp in mine:
            cp.wait()

    return pl.pallas_call(
        body, name=name, out_shape=[jax.ShapeDtypeStruct((N_DEV,) + t.shape, t.dtype) for t in xs],
        in_specs=[_HBM] * na, out_specs=[_HBM] * na,
        scratch_shapes=[pltpu.SemaphoreType.DMA((na, 7)), pltpu.SemaphoreType.DMA((na, 7)),
                        pltpu.SemaphoreType.DMA((na,))],
    )(*xs)


_SEM = pl.BlockSpec(memory_space=pltpu.SEMAPHORE)
_EFFECT = pltpu.SideEffectType.DATAFLOW_SIDE_EFFECTING


def _peers(x, y, c):
    out = []
    for k in range(1, N_DEV):
        px = 1 - x if k & 4 else x
        py = 1 - y if k & 2 else y
        pc = 1 - c if k & 1 else c
        out.append(((px, py, pc), 4 * px + 2 * py + pc))
    return out


def _push_copies(scatter, src_refs, land_refs, send_sems, recv_sems):
    x, y, c = lax.axis_index("x"), lax.axis_index("y"), lax.axis_index("c")
    me = 4 * x + 2 * y + c
    pairs = []
    for a, (src, land) in enumerate(zip(src_refs, land_refs)):
        for k, (peer, slot) in enumerate(_peers(x, y, c)):
            out_src = src.at[slot] if scatter else src
            si = a * (N_DEV - 1) + k
            send = pltpu.make_async_remote_copy(src_ref=out_src, dst_ref=land.at[me], send_sem=send_sems.at[si],
                                                recv_sem=recv_sems.at[si], device_id=peer, device_id_type=_MESH)
            recv = pltpu.make_async_remote_copy(src_ref=out_src, dst_ref=land.at[slot], send_sem=send_sems.at[si],
                                                recv_sem=recv_sems.at[si], device_id=peer, device_id_type=_MESH)
            pairs.append((send, recv))
    return pairs


def _push_start(srcs, scatter, dep, name):
    na = len(srcs)
    shapes = [t.shape[1:] if scatter else t.shape for t in srcs]
    lands = [pltpu.with_memory_space_constraint(lax.empty((N_DEV,) + s, t.dtype), pltpu.HBM) for s, t in zip(shapes, srcs)]

    def body(*refs):
        src_refs, land_refs = refs[:na], refs[na:2 * na]
        send_sems, recv_sems = refs[2 * na + 1], refs[2 * na + 2]
        token = refs[-1]
        for send, _ in _push_copies(scatter, src_refs, land_refs, send_sems, recv_sems):
            send.start()
        token[...] = jnp.zeros_like(token)

    sem = pltpu.SemaphoreType.DMA((na * (N_DEV - 1),))
    outs = pl.pallas_call(
        body, name=name,
        out_shape=(sem, sem) + tuple(pltpu.HBM(t.shape, t.dtype) for t in srcs)
        + tuple(pltpu.HBM(t.shape, t.dtype) for t in lands) + (jax.ShapeDtypeStruct((8, LANE), F32),),
        in_specs=[_HBM] * (2 * na) + [pl.BlockSpec(memory_space=pl.ANY)],
        out_specs=(_SEM, _SEM) + (_HBM,) * (2 * na) + (pl.BlockSpec(memory_space=pltpu.VMEM),),
        input_output_aliases={i: 2 + i for i in range(2 * na)},
        compiler_params=pltpu.CompilerParams(has_side_effects=_EFFECT),
    )(*[pltpu.with_memory_space_constraint(t, pltpu.HBM) for t in srcs], *lands, dep)
    return outs[0], outs[1], outs[2:2 + na], outs[2 + na:2 + 2 * na], outs[-1]


def _push_wait(send_sems, recv_sems, src_thru, land_thru, scatter, after, name):
    na = len(src_thru)

    def body(*refs):
        src_refs, land_refs = refs[:na], refs[na:2 * na]
        ssem, rsem = refs[2 * na], refs[2 * na + 1]
        for send, recv in _push_copies(scatter, src_refs, land_refs, ssem, rsem):
            send.wait_send()
            recv.wait_recv()

    outs = pl.pallas_call(
        body, name=name,
        out_shape=tuple(pltpu.HBM(t.shape, t.dtype) for t in src_thru) + tuple(pltpu.HBM(t.shape, t.dtype) for t in land_thru),
        in_specs=[_HBM] * (2 * na) + [_SEM, _SEM, pl.BlockSpec(memory_space=pl.ANY)],
        out_specs=(_HBM,) * (2 * na),
        input_output_aliases={i: i for i in range(2 * na)},
        compiler_params=pltpu.CompilerParams(has_side_effects=_EFFECT),
    )(*src_thru, *land_thru, send_sems, recv_sems, after)
    return outs[:na], outs[na:]


def _exchange_behind(srcs, scatter, dep, name):
    send_sems, recv_sems, thru, lands, token = _push_start(srcs, scatter, dep, name + "_start")

    def finish(after, place=True):
        src_done, land_done = _push_wait(send_sems, recv_sems, thru, lands, scatter, after, name + "_wait")
        if not place:
            return land_done, src_done
        return _place_own(land_done, src_done, scatter, name + "_own")

    return token[0, 0], finish


def _place_own(lands, srcs, scatter, name):
    me = (4 * lax.axis_index("x") + 2 * lax.axis_index("y") + lax.axis_index("c")).astype(jnp.int32).reshape(1)
    outs = []
    for a, (land, src) in enumerate(zip(lands, srcs)):
        r_, c_ = land.shape[1:]
        tr = _pick(r_, (512, 256, 128, 64, 32, 16))

        def body(me_ref, land_ref, src_ref, out_ref):
            out_ref[...] = src_ref[...]

        src_spec = (pl.BlockSpec((None, tr, c_), lambda i, me_: (me_[0], i, 0)) if scatter
                    else pl.BlockSpec((tr, c_), lambda i, me_: (i, 0)))
        gs = pltpu.PrefetchScalarGridSpec(
            num_scalar_prefetch=1, grid=(r_ // tr,),
            in_specs=[pl.BlockSpec(memory_space=pl.ANY), src_spec],
            out_specs=pl.BlockSpec((None, tr, c_), lambda i, me_: (me_[0], i, 0)))
        outs.append(pl.pallas_call(
            body, name=f"{name}_{a}", grid_spec=gs, out_shape=jax.ShapeDtypeStruct(land.shape, land.dtype),
            input_output_aliases={1: 0}, compiler_params=_params("arbitrary"),
        )(me, land, src))
    return outs


_TRANSPOSED = ("w_in", "w_uq", "w_gate", "w_up")
_CQKV = (0, Q_RANK + KV_RANK)
_KR = (_CQKV[1], _CQKV[1] + ROPE)
_Z = (_KR[1], _KR[1] + SSD_W)
_XBC = (_Z[1], _Z[1] + CONV_DIM)
_DT = (_XBC[1], _XBC[1] + SSD_H)


def _win_segments(w_in_t):
    w = w_in_t.reshape(D_IN, D_MODEL)
    small = jnp.concatenate([w[_KR[0]:_KR[1]], w[_DT[0]:_DT[1]],
                             jnp.zeros((LANE - ROPE - SSD_H, D_MODEL), w.dtype)], axis=0)
    return w[_CQKV[0]:_CQKV[1]], w[_Z[0]:_Z[1]], w[_XBC[0]:_XBC[1]], small


def _win_from_segments(g_cqkv, g_z, g_xbc, g_small):
    w = jnp.concatenate([g_cqkv, g_small[:ROPE], g_z, g_xbc, g_small[ROPE:ROPE + SSD_H]], axis=0)
    return w.reshape(N_DEV, D_IN // N_DEV, D_MODEL)


_SMALL = (("q_norm_w", 512), ("kv_norm_w", 512), ("conv_b", CONV_DIM), ("dt_bias", SSD_H), ("a_log", SSD_H),
          ("d_skip", SSD_H), ("ssd_norm_w", SSD_W), ("attn_out_norm_w", 1024), ("pre_mix_norm_w", D_MODEL),
          ("post_mix_norm_w", D_MODEL), ("pre_ffn_norm_w", D_MODEL), ("post_ffn_norm_w", D_MODEL),
          ("conv_w", CONV_K * CONV_DIM))
_SMALL_ROWS = -(-(sum(-(-n // LANE) for _, n in _SMALL) + 1) // 8) * 8


def _pack_small(vals):
    rows = []
    for name, n in _SMALL:
        v = vals[name].reshape(-1).astype(F32)
        pad = -(-n // LANE) * LANE
        rows.append(jnp.pad(v, (0, pad - n)).reshape(-1, LANE))
    m = jnp.concatenate(rows, axis=0)
    return jnp.pad(m, ((0, _SMALL_ROWS - m.shape[0]), (0, 0)))


def _unpack_small(m):
    out, r = {}, 0
    for name, n in _SMALL:
        nr = -(-n // LANE)
        out[name] = m[r:r + nr].reshape(-1)[:n]
        r += nr
    return out


def _head_row(v):
    return jnp.pad(v.reshape(1, -1).astype(F32), ((0, 0), (HEAD_LANE, LANE - HEAD_LANE - v.shape[-1])))


def _local_step(x, positions, target, wg, small, weights, on_grads):
    w_cqkv, w_z, w_xbc, w_small = _win_segments(wg["w_in"])
    conv_w = wg["conv_w"]
    conv_b = small["conv_b"].reshape(1, CONV_DIM)
    qkv_norm_w = jnp.concatenate([small["q_norm_w"], small["kv_norm_w"]])
    attn_norm_w = small["attn_out_norm_w"].reshape(1, HEADS * VDIM)
    scale = QK ** -0.5

    inv_freq = ROPE_THETA ** (-jnp.arange(0, ROPE, 2, dtype=F32) / ROPE)
    ang = positions.astype(F32)[:, None] * inv_freq
    cos2 = jnp.tile(jnp.cos(ang), (1, 2))
    sin2 = jnp.tile(jnp.sin(ang), (1, 2))

    u = _rms_fwd(x, small["pre_mix_norm_w"], out_dtype=MXU_DTYPE, name="pre_mix_norm")
    cqkv = _mm(u, w_cqkv, "nt", name="in_proj_qkv")
    z = _mm(u, w_z, "nt", name="in_proj_z")
    xbc = _mm(u, w_xbc, "nt", name="in_proj_xbc")
    sm = _mm(u, w_small, "nt", name="in_proj_small")

    w_uq, w_ukv = weights("qkv_up", cqkv)
    qkvn = _rms_fwd(cqkv, qkv_norm_w, groups=2, out_dtype=MXU_DTYPE, name="qkv_norm")
    q_h = _q_up(qkvn, w_uq, cos2, sin2, scale)
    k_h, v_h = _kv_up(qkvn, w_ukv, sm, cos2, sin2)
    o_h, lse = _flash_fwd(q_h, k_h, v_h)
    cat = _hnorm_fwd(o_h, attn_norm_w, D_MODEL)
    w_out = weights("out", o_h)[0].reshape(D_MODEL, D_MODEL)

    xbc_act = _conv_fwd(xbc, conv_w, conv_b)
    dtt = jnp.transpose(sm[:, HEAD_LANE:HEAD_LANE + SSD_H])
    ssd_args = (xbc_act, sm, dtt, _head_row(small["dt_bias"]), small["dt_bias"].reshape(SSD_H, 1),
                _head_row(small["a_log"]), small["a_log"].reshape(SSD_H, 1),
                jnp.broadcast_to(small["d_skip"].reshape(SSD_H, 1), (SSD_H, SSD_P)).reshape(SSD_PAIRS, 1, LANE))
    y_ssd, prev = _ssd_fwd(*ssd_args)
    cat = _gated_norm_fwd(y_ssd, z, small["ssd_norm_w"], cat)

    mix = _mm(cat, w_out, "nn", name="out_proj")
    h1, vv = _norm_res_norm(mix, x, small["post_mix_norm_w"], small["pre_ffn_norm_w"])

    w_gate, w_up = weights("ffn_in", mix)
    gate, up, act = _ffn_fwd(vv, w_gate, w_up)
    w_down, = weights("ffn_out", act)
    ffn = _mm(act, w_down, "nn", a_blk=True, b_blk=True, fuse=N_DEV, tm_max=512, name="ffn_down")
    loss_blk, dy, dffn, g_post_ffn = _loss_head(ffn, h1, target, small["post_ffn_norm_w"])

    g_down = _mm(act, dffn, "tn", a_blk=True, out_blk=True, out_dtype=MXU_DTYPE, name="g_down")
    dgate, dup = _ffn_bwd_act(dffn, w_down, gate, up)
    dvv = _ffn_bwd_in(dgate, w_gate, dup, w_up)
    g_gate = _mm(dgate, vv, "tn", a_blk=True, out_blk=True, out_dtype=MXU_DTYPE, name="g_gate")
    g_up = _mm(dup, vv, "tn", a_blk=True, out_blk=True, out_dtype=MXU_DTYPE, name="g_up")
    pre_ffn_w = small["pre_ffn_norm_w"] + on_grads("ffn", [g_gate, g_up, g_down])
    dh1, dmix, g_pre_ffn, g_post_mix = _norm_res_norm_bwd(h1, pre_ffn_w, dvv, dy, mix, small["post_mix_norm_w"])

    dcat = _mm(dmix, w_out, "nt", name="d_cat")
    g_out = _mm(cat, dmix, "tn", out_dtype=MXU_DTYPE, name="g_out")

    do_h, delta, g_attn_norm = _hnorm_bwd(o_h, attn_norm_w, dcat)
    dq_h, dk_h, dv_h = _flash_bwd(q_h, k_h, v_h, do_h, lse, delta)
    dq = _q_prep(dq_h, cos2, -sin2, scale, name="dq_post")

    dy_ssd, dz, g_ssd_norm = _gated_norm_bwd(y_ssd, z, small["ssd_norm_w"], dcat)
    dxbc_act, ddt, dpar = _ssd_bwd(*ssd_args, prev, dy_ssd)
    dkv, dsm = _dkv_post(dk_h, dv_h, ddt, cos2, -sin2)
    dpre, dwb = _conv_bwd_pre(xbc, conv_w, conv_b, dxbc_act)
    dxbc = _conv_bwd_in(dpre, conv_w)

    dqn = _mm(dq, w_uq, "nn", a_blk=True, b_blk=True, fuse=HEADS, name="d_qn")
    dkvn = _mm(dkv, w_ukv, "nt", a_blk=True, b_blk=True, fuse=HEADS, name="d_kvn")
    g_uq = _mm(dq, qkvn, "tn", a_blk=True, out_blk=True, b_cols=(0, Q_RANK), out_dtype=MXU_DTYPE, name="g_uq")
    g_ukv = _mm(qkvn, dkv, "tn", b_blk=True, out_blk=True, a_cols=(Q_RANK, KV_RANK), out_dtype=MXU_DTYPE, name="g_ukv")
    heads_token = on_grads("heads", [g_uq, g_ukv, g_out.reshape(N_DEV, D_MODEL // N_DEV, D_MODEL)])
    dcqkv, g_qkv_norm = _rms_bwd(cqkv, qkv_norm_w + heads_token, [dqn, dkvn], out_dtype=MXU_DTYPE, name="qkv_norm_bwd")

    g_in = _win_from_segments(_mm(dcqkv, u, "tn", out_dtype=MXU_DTYPE, name="g_in_qkv"),
                              _mm(dz, u, "tn", out_dtype=MXU_DTYPE, name="g_in_z"),
                              _mm(dxbc, u, "tn", out_dtype=MXU_DTYPE, name="g_in_xbc"),
                              _mm(dsm, u, "tn", out_dtype=MXU_DTYPE, name="g_in_small"))
    in_token = on_grads("in", [g_in])
    du = _mm_sum([dsm + in_token.astype(dsm.dtype), dcqkv, dz, dxbc], [w_small, w_cqkv, w_z, w_xbc], name="d_u")
    dx, g_pre_mix = _rms_bwd(x, small["pre_mix_norm_w"], [du], res=dh1, name="pre_mix_norm_bwd")

    hl = slice(HEAD_LANE, HEAD_LANE + SSD_H)
    g_small = {"q_norm_w": g_qkv_norm[0, :Q_RANK], "kv_norm_w": g_qkv_norm[0, Q_RANK:], "conv_b": dwb[CONV_K],
               "dt_bias": dpar[0, hl], "a_log": dpar[1, hl], "d_skip": dpar[2, hl], "ssd_norm_w": g_ssd_norm,
               "attn_out_norm_w": g_attn_norm, "pre_mix_norm_w": g_pre_mix, "post_mix_norm_w": g_post_mix,
               "pre_ffn_norm_w": g_pre_ffn, "post_ffn_norm_w": g_post_ffn, "conv_w": dwb[:CONV_K]}
    return loss_blk[0, 0], dx, g_small


_WEIGHT_ORDER = ("w_in", "q_norm_w", "w_uq", "kv_norm_w", "w_ukv", "conv_w", "conv_b", "dt_bias", "a_log", "d_skip",
                 "ssd_norm_w", "attn_out_norm_w", "w_out", "pre_mix_norm_w", "post_mix_norm_w", "pre_ffn_norm_w",
                 "post_ffn_norm_w", "w_gate", "w_up", "w_down")


def kernel(x, positions, w_in, q_norm_w, w_uq, kv_norm_w, w_ukv, conv_w, conv_b, dt_bias, a_log, d_skip, ssd_norm_w, attn_out_norm_w, w_out, pre_mix_norm_w, post_mix_norm_w, pre_ffn_norm_w, post_ffn_norm_w, w_gate, w_up, w_down, loss_target, m_w_in, m_q_norm_w, m_w_uq, m_kv_norm_w, m_w_ukv, m_conv_w, m_conv_b, m_dt_bias, m_a_log, m_d_skip, m_ssd_norm_w, m_attn_out_norm_w, m_w_out, m_pre_mix_norm_w, m_post_mix_norm_w, m_pre_ffn_norm_w, m_post_ffn_norm_w, m_w_gate, m_w_up, m_w_down, v_w_in, v_q_norm_w, v_w_uq, v_kv_norm_w, v_w_ukv, v_conv_w, v_conv_b, v_dt_bias, v_a_log, v_d_skip, v_ssd_norm_w, v_attn_out_norm_w, v_w_out, v_pre_mix_norm_w, v_post_mix_norm_w, v_pre_ffn_norm_w, v_post_ffn_norm_w, v_w_gate, v_w_up, v_w_down):
    w = dict(w_in=w_in, q_norm_w=q_norm_w, w_uq=w_uq, kv_norm_w=kv_norm_w, w_ukv=w_ukv, conv_w=conv_w, conv_b=conv_b,
             dt_bias=dt_bias, a_log=a_log, d_skip=d_skip, ssd_norm_w=ssd_norm_w, attn_out_norm_w=attn_out_norm_w,
             w_out=w_out, pre_mix_norm_w=pre_mix_norm_w, post_mix_norm_w=post_mix_norm_w,
             pre_ffn_norm_w=pre_ffn_norm_w, post_ffn_norm_w=post_ffn_norm_w, w_gate=w_gate, w_up=w_up, w_down=w_down)
    m = dict(w_in=m_w_in, q_norm_w=m_q_norm_w, w_uq=m_w_uq, kv_norm_w=m_kv_norm_w, w_ukv=m_w_ukv, conv_w=m_conv_w,
             conv_b=m_conv_b, dt_bias=m_dt_bias, a_log=m_a_log, d_skip=m_d_skip, ssd_norm_w=m_ssd_norm_w,
             attn_out_norm_w=m_attn_out_norm_w, w_out=m_w_out, pre_mix_norm_w=m_pre_mix_norm_w,
             post_mix_norm_w=m_post_mix_norm_w, pre_ffn_norm_w=m_pre_ffn_norm_w, post_ffn_norm_w=m_post_ffn_norm_w,
             w_gate=m_w_gate, w_up=m_w_up, w_down=m_w_down)
    v = dict(w_in=v_w_in, q_norm_w=v_q_norm_w, w_uq=v_w_uq, kv_norm_w=v_kv_norm_w, w_ukv=v_w_ukv, conv_w=v_conv_w,
             conv_b=v_conv_b, dt_bias=v_dt_bias, a_log=v_a_log, d_skip=v_d_skip, ssd_norm_w=v_ssd_norm_w,
             attn_out_norm_w=v_attn_out_norm_w, w_out=v_w_out, pre_mix_norm_w=v_pre_mix_norm_w,
             post_mix_norm_w=v_post_mix_norm_w, pre_ffn_norm_w=v_pre_ffn_norm_w, post_ffn_norm_w=v_post_ffn_norm_w,
             w_gate=v_w_gate, w_up=v_w_up, w_down=v_w_down)
    w, m, v = ({k: t[0] for k, t in d.items()} for d in (w, m, v))
    me = 4 * lax.axis_index("x") + 2 * lax.axis_index("y") + lax.axis_index("c")
    groups = {"qkv_up": ("w_uq", "w_ukv"), "out": ("w_out",), "ffn_in": ("w_gate", "w_up"), "ffn_out": ("w_down",)}
    cshard = CONV_DIM // N_DEV
    for name in _TRANSPOSED:
        w[name], m[name], v[name] = w[name].T, m[name].T, v[name].T

    shards = [w["w_in"].astype(MXU_DTYPE),
              jnp.stack(_split3(w["conv_w"])).reshape(3 * CONV_K, cshard).astype(MXU_DTYPE)]
    w_in_g, cw = _all_gather(shards, name="gather_weights")
    cw = cw.astype(F32).reshape(N_DEV, 3, CONV_K, cshard)
    wg = {"w_in": w_in_g, "conv_w": jnp.transpose(cw[:, 0] + cw[:, 1] + cw[:, 2], (1, 0, 2)).reshape(CONV_K, CONV_DIM)}
    arriving, dep, started = {}, wg["conv_w"], jnp.zeros((), F32)
    small = {name: w[name] for name, _ in _SMALL if name != "conv_w"}
    for group in ("qkv_up", "out", "ffn_in", "ffn_out"):
        token, arriving[group] = _exchange_behind([w[name].astype(MXU_DTYPE) for name in groups[group]], False,
                                                  dep, group + "_weights")
        started = started + token
        dep = jnp.zeros((8, LANE), F32) + started
    small["pre_mix_norm_w"] = small["pre_mix_norm_w"] + started

    leaving = {}

    def on_grads(group, gs):
        token, leaving[group] = _exchange_behind(gs, True, jnp.zeros((8, LANE), F32), group + "_grads")
        return token

    loss_local, dx, g_small = _local_step(x[0], positions[0], loss_target[0], wg, small,
                                          lambda group, after: arriving[group](after), on_grads)
    recv = {}
    for group, names in (("ffn", ("w_gate", "w_up", "w_down")), ("heads", ("w_uq", "w_ukv", "w_out")), ("in", ("w_in",))):
        recv.update(zip(names, zip(*leaving[group](dx, place=False))))
    grads, deltas, new_m, new_v = {}, {}, {}, {}
    me1 = me.astype(jnp.int32).reshape(1)
    for name, (parts, own) in recv.items():
        outs = _adamw(parts, own, me1, w[name], m[name], v[name], name="adamw_" + name)
        if name in _TRANSPOSED:
            outs = [t.T for t in outs]
        grads[name], deltas[name], new_m[name], new_v[name] = outs

    def embed(t):
        return lax.dynamic_update_slice(jnp.zeros((CONV_K, CONV_DIM), F32), t, (0, me * cshard))

    mine_s = _pack_small(g_small).at[_SMALL_ROWS - 1, 0].set(loss_local)
    parts_s = _all_gather([mine_s], name="gather_small_grads")[0]
    packs = [_pack_small({**{n_: d[n_] for n_, _ in _SMALL if n_ != "conv_w"}, "conv_w": embed(d["conv_w"])})
             for d in (w, m, v)]
    summed = _adamw_small(parts_s, *packs)
    loss = summed[0][_SMALL_ROWS - 1, 0]
    outs = [_unpack_small(t) for t in summed]
    for name, n in _SMALL:
        for dst, src in zip((grads, deltas, new_m, new_v), outs):
            if name == "conv_w":
                dst[name] = lax.dynamic_slice(src[name].reshape(CONV_K, CONV_DIM), (0, me * cshard), (CONV_K, cshard))
            else:
                dst[name] = src[name]

    def lead(d):
        return [d[name][None] for name in _WEIGHT_ORDER]

    return (loss, dx[None], *lead(grads), *lead(deltas), *lead(new_m), *lead(new_v))
```
